```python
import jax, jax.numpy as jnp
from jax import lax
import numpy as np

D_MODEL = 2048
BATCH = 8
SEQ = 4096
DEPTH = 1

HEAD_DIM = 128
N_ATT_HEADS = D_MODEL // (2 * HEAD_DIM)
N_RET_HEADS = D_MODEL // (2 * HEAD_DIM)
D_ATT = N_ATT_HEADS * HEAD_DIM
D_RET = N_RET_HEADS * HEAD_DIM
D_MIX = D_ATT + D_RET
D_IN_PROJ = 3 * D_ATT + 4 * D_RET
DILATED_PATTERNS = ((128, 1), (512, 4), (2048, 16))
RET_CHUNK = 128
D_FF = -(-(8 * D_MODEL) // (3 * 256)) * 256
EPS = 1e-6

kernel_name = 'hybrid_dilated_attn_retention_block'


def rmsnorm(x, w):
    xf = x.astype(jnp.float32)
    xf = xf * lax.rsqrt(jnp.mean(xf * xf, axis=-1, keepdims=True) + EPS)
    return (xf * w.astype(jnp.float32)).astype(x.dtype)


def alibi_slopes(n_heads):
    return jnp.exp2(-8.0 * jnp.arange(1, n_heads + 1, dtype=jnp.float32) / n_heads)


def dilated_window_partial(q, k, v, slopes, window, dilation):
    B, H, S, Dh = q.shape
    half = window // (2 * dilation)
    blk = half
    L = S // dilation
    nb = -(-L // blk)
    Lp = nb * blk

    def to_sub(t):
        return t.reshape(B, H, L, dilation, Dh).transpose(0, 1, 3, 2, 4)

    qb = jnp.pad(to_sub(q), ((0, 0), (0, 0), (0, 0), (0, Lp - L), (0, 0)))
    qb = qb.reshape(B, H, dilation, nb, blk, Dh)

    def key_windows(t):
        tp = jnp.pad(to_sub(t), ((0, 0), (0, 0), (0, 0), (blk, Lp - L + blk), (0, 0)))
        tb = tp.reshape(B, H, dilation, nb + 2, blk, Dh)
        return jnp.concatenate([tb[:, :, :, :-2], tb[:, :, :, 1:-1], tb[:, :, :, 2:]], axis=4)

    kw = key_windows(k)
    vw = key_windows(v)
    s = jnp.einsum('bhrnqd,bhrnkd->bhrnqk', qb, kw) * (Dh ** -0.5)
    lq = jnp.arange(nb)[:, None] * blk + jnp.arange(blk)[None, :]
    lk = jnp.arange(nb)[:, None] * blk - blk + jnp.arange(3 * blk)[None, :]
    dist = jnp.abs(lq[:, :, None] - lk[:, None, :])
    valid = (dist <= half) & (lk[:, None, :] >= 0) & (lk[:, None, :] < L)
    bias = -slopes[:, None, None, None, None] * (dilation * dist).astype(jnp.float32)
    s = jnp.where(valid, s + bias, -jnp.inf)
    m = jnp.max(s, axis=-1)
    p = jnp.exp(s - m[..., None])
    den = jnp.sum(p, axis=-1)
    num = jnp.einsum('bhrnqk,bhrnkd->bhrnqd', p, vw)
    num = num.reshape(B, H, dilation, Lp, Dh)[:, :, :, :L].transpose(0, 1, 3, 2, 4).reshape(B, H, S, Dh)
    m = m.reshape(B, H, dilation, Lp)[..., :L].transpose(0, 1, 3, 2).reshape(B, H, S)
    den = den.reshape(B, H, dilation, Lp)[..., :L].transpose(0, 1, 3, 2).reshape(B, H, S)
    return num, m, den


def dilated_mixture_attention(q, k, v, slopes):
    q, k, v = (t.astype(jnp.float32) for t in (q, k, v))
    parts = [dilated_window_partial(q, k, v, slopes, w, d) for (w, d) in DILATED_PATTERNS]
    m_all = jnp.max(jnp.stack([p[1] for p in parts], axis=0), axis=0)
    num = 0.0
    den = 0.0
    for (n_i, m_i, d_i) in parts:
        w_i = jnp.exp(m_i - m_all)
        num = num + w_i[..., None] * n_i
        den = den + w_i * d_i
    return num / den[..., None]


def retention_direction(q, k, v, log_gamma, strict):
    B, H, S, Dh = q.shape
    C = RET_CHUNK
    nc = S // C
    idx = jnp.arange(C, dtype=jnp.float32)
    rel = idx[:, None] - idx[None, :]
    inside = (rel > 0) if strict else (rel >= 0)
    decay_mask = jnp.where(inside, jnp.exp(log_gamma[:, None, None] * jnp.maximum(rel, 0.0)), 0.0)
    q_dec = jnp.exp(log_gamma[:, None] * (idx + 1.0))[..., None]
    k_dec = jnp.exp(log_gamma[:, None] * (C - 1.0 - idx))[..., None]
    chunk_dec = jnp.exp(log_gamma * C)[:, None, None]

    def to_chunks(t):
        return t.reshape(B, H, nc, C, Dh).transpose(2, 0, 1, 3, 4)

    def step(state, qkv):
        qc, kc, vc = qkv
        inner = jnp.einsum('bhid,bhjd->bhij', qc, kc) * decay_mask
        o = jnp.einsum('bhij,bhjd->bhid', inner, vc) + jnp.einsum('bhid,bhde->bhie', qc * q_dec, state)
        state = state * chunk_dec + jnp.einsum('bhjd,bhje->bhde', kc * k_dec, vc)
        return state, o

    state0 = jnp.zeros((B, H, Dh, Dh), jnp.float32)
    _, o = lax.scan(step, state0, (to_chunks(q), to_chunks(k), to_chunks(v)))
    return o.transpose(1, 2, 0, 3, 4).reshape(B, H, S, Dh)


def bidirectional_retention(q, k, v, decay_fwd, decay_bwd):
    q, k, v = (t.astype(jnp.float32) for t in (q, k, v))
    q = q * (q.shape[-1] ** -0.5)
    lg_f = -jnp.exp(decay_fwd.astype(jnp.float32))
    lg_b = -jnp.exp(decay_bwd.astype(jnp.float32))
    o_f = retention_direction(q, k, v, lg_f, strict=False)
    flip = lambda t: jnp.flip(t, axis=2)
    o_b = flip(retention_direction(flip(q), flip(k), flip(v), lg_b, strict=True))
    return o_f + o_b


def _fwd_setup_inputs(seed: int = 0) -> dict:
    key = jax.random.key(seed)
    ks = jax.random.split(key, 16)
    f32 = jnp.float32
    nrm = lambda k, shape, scale: jax.random.normal(k, shape, f32) * scale
    gain = lambda k, shape: 1.0 + 0.01 * jax.random.normal(k, shape, f32)
    base = np.log(-np.log(1.0 - 2.0 ** (-5.0 - np.arange(N_RET_HEADS)))).astype(np.float32)
    base = jnp.asarray(base)[None, :]
    return {
        'x': jax.random.normal(ks[0], (BATCH, SEQ, D_MODEL), f32),
        'norm_mix_w': gain(ks[1], (DEPTH, D_MODEL)),
        'w_in': nrm(ks[2], (DEPTH, D_MODEL, D_IN_PROJ), D_MODEL ** -0.5),
        'ret_decay_fwd': base + 0.05 * jax.random.normal(ks[3], (DEPTH, N_RET_HEADS), f32),
        'ret_decay_bwd': base + 0.05 * jax.random.normal(ks[4], (DEPTH, N_RET_HEADS), f32),
        'ret_norm_w': gain(ks[5], (DEPTH, D_RET)),
        'w_out': nrm(ks[6], (DEPTH, D_MIX, D_MODEL), D_MIX ** -0.5),
        'norm_ffn_w': gain(ks[7], (DEPTH, D_MODEL)),
        'w_gate': nrm(ks[8], (DEPTH, D_MODEL, D_FF), D_MODEL ** -0.5),
        'w_up': nrm(ks[9], (DEPTH, D_MODEL, D_FF), D_MODEL ** -0.5),
        'w_down': nrm(ks[10], (DEPTH, D_FF, D_MODEL), D_FF ** -0.5),
        'norm_final_w': gain(ks[11], (D_MODEL,)),
    }


def _fwd_reference(x, norm_mix_w, w_in, ret_decay_fwd, ret_decay_bwd, ret_norm_w, w_out,
              norm_ffn_w, w_gate, w_up, w_down, norm_final_w):
    B, S, _ = x.shape
    slopes = alibi_slopes(N_ATT_HEADS)
    splits = [D_ATT, 2 * D_ATT, 3 * D_ATT, 3 * D_ATT + D_RET, 3 * D_ATT + 2 * D_RET, 3 * D_ATT + 3 * D_RET]

    def heads(t):
        return t.reshape(B, S, -1, HEAD_DIM).transpose(0, 2, 1, 3)

    def merge(t):
        return t.transpose(0, 2, 1, 3).reshape(B, S, -1)

    h = x
    for layer in range(DEPTH):
        n = rmsnorm(h, norm_mix_w[layer])
        proj = n @ w_in[layer]
        q_a, k_a, v_a, q_r, k_r, v_r, g_r = jnp.split(proj, splits, axis=-1)
        attn = dilated_mixture_attention(heads(q_a), heads(k_a), heads(v_a), slopes)
        ret = bidirectional_retention(heads(q_r), heads(k_r), heads(v_r),
                                      ret_decay_fwd[layer], ret_decay_bwd[layer])
        ret = ret * lax.rsqrt(jnp.mean(ret * ret, axis=-1, keepdims=True) + EPS)
        ret = merge(ret) * ret_norm_w[layer].astype(jnp.float32)
        ret = ret * jax.nn.silu(g_r.astype(jnp.float32))
        mixed = jnp.concatenate([merge(attn), ret], axis=-1).astype(x.dtype)
        h = h + mixed @ w_out[layer]
        n2 = rmsnorm(h, norm_ffn_w[layer])
        h = h + (jax.nn.silu(n2 @ w_gate[layer]) * (n2 @ w_up[layer])) @ w_down[layer]
    return rmsnorm(h, norm_final_w)


import jax as _jax
import jax.numpy as _jnp

TWIN_FORMAT = 'train_step'
FWD_PARAMS = ['x', 'norm_mix_w', 'w_in', 'ret_decay_fwd', 'ret_decay_bwd', 'ret_norm_w', 'w_out', 'norm_ffn_w', 'w_gate', 'w_up', 'w_down', 'norm_final_w']
TWIN_WEIGHTS = ['norm_mix_w', 'w_in', 'ret_decay_fwd', 'ret_decay_bwd', 'ret_norm_w', 'w_out', 'norm_ffn_w', 'w_gate', 'w_up', 'w_down', 'norm_final_w']
TWIN_DIFF_INPUT = 'x'
TWIN_INPUTS = ['x', 'norm_mix_w', 'w_in', 'ret_decay_fwd', 'ret_decay_bwd', 'ret_norm_w', 'w_out', 'norm_ffn_w', 'w_gate', 'w_up', 'w_down', 'norm_final_w', 'loss_target', 'm_norm_mix_w', 'm_w_in', 'm_ret_decay_fwd', 'm_ret_decay_bwd', 'm_ret_norm_w', 'm_w_out', 'm_norm_ffn_w', 'm_w_gate', 'm_w_up', 'm_w_down', 'm_norm_final_w', 'v_norm_mix_w', 'v_w_in', 'v_ret_decay_fwd', 'v_ret_decay_bwd', 'v_ret_norm_w', 'v_w_out', 'v_norm_ffn_w', 'v_w_gate', 'v_w_up', 'v_w_down', 'v_norm_final_w']
TWIN_OUTPUTS = ['loss', 'grad_x', 'grad_norm_mix_w', 'grad_w_in', 'grad_ret_decay_fwd', 'grad_ret_decay_bwd', 'grad_ret_norm_w', 'grad_w_out', 'grad_norm_ffn_w', 'grad_w_gate', 'grad_w_up', 'grad_w_down', 'grad_norm_final_w', 'delta_norm_mix_w', 'delta_w_in', 'delta_ret_decay_fwd', 'delta_ret_decay_bwd', 'delta_ret_norm_w', 'delta_w_out', 'delta_norm_ffn_w', 'delta_w_gate', 'delta_w_up', 'delta_w_down', 'delta_norm_final_w', 'new_m_norm_mix_w', 'new_m_w_in', 'new_m_ret_decay_fwd', 'new_m_ret_decay_bwd', 'new_m_ret_norm_w', 'new_m_w_out', 'new_m_norm_ffn_w', 'new_m_w_gate', 'new_m_w_up', 'new_m_w_down', 'new_m_norm_final_w', 'new_v_norm_mix_w', 'new_v_w_in', 'new_v_ret_decay_fwd', 'new_v_ret_decay_bwd', 'new_v_ret_norm_w', 'new_v_w_out', 'new_v_norm_ffn_w', 'new_v_w_gate', 'new_v_w_up', 'new_v_w_down', 'new_v_norm_final_w']
TWIN_LEAF_KINDS = {'loss': 'loss', 'grad_x': 'grad_x', 'grad_norm_mix_w': 'grad_w', 'grad_w_in': 'grad_w', 'grad_ret_decay_fwd': 'grad_w', 'grad_ret_decay_bwd': 'grad_w', 'grad_ret_norm_w': 'grad_w', 'grad_w_out': 'grad_w', 'grad_norm_ffn_w': 'grad_w', 'grad_w_gate': 'grad_w', 'grad_w_up': 'grad_w', 'grad_w_down': 'grad_w', 'grad_norm_final_w': 'grad_w', 'delta_norm_mix_w': 'delta_w', 'delta_w_in': 'delta_w', 'delta_ret_decay_fwd': 'delta_w', 'delta_ret_decay_bwd': 'delta_w', 'delta_ret_norm_w': 'delta_w', 'delta_w_out': 'delta_w', 'delta_norm_ffn_w': 'delta_w', 'delta_w_gate': 'delta_w', 'delta_w_up': 'delta_w', 'delta_w_down': 'delta_w', 'delta_norm_final_w': 'delta_w', 'new_m_norm_mix_w': 'new_m', 'new_m_w_in': 'new_m', 'new_m_ret_decay_fwd': 'new_m', 'new_m_ret_decay_bwd': 'new_m', 'new_m_ret_norm_w': 'new_m', 'new_m_w_out': 'new_m', 'new_m_norm_ffn_w': 'new_m', 'new_m_w_gate': 'new_m', 'new_m_w_up': 'new_m', 'new_m_w_down': 'new_m', 'new_m_norm_final_w': 'new_m', 'new_v_norm_mix_w': 'new_v', 'new_v_w_in': 'new_v', 'new_v_ret_decay_fwd': 'new_v', 'new_v_ret_decay_bwd': 'new_v', 'new_v_ret_norm_w': 'new_v', 'new_v_w_out': 'new_v', 'new_v_norm_ffn_w': 'new_v', 'new_v_w_gate': 'new_v', 'new_v_w_up': 'new_v', 'new_v_w_down': 'new_v', 'new_v_norm_final_w': 'new_v'}


def _forward(args):
    return _fwd_reference(*[args[k] for k in FWD_PARAMS])


def _output_shape():
    def fwd():
        inp = _fwd_setup_inputs(0)
        return _fwd_reference(*[inp[k] for k in FWD_PARAMS])
    out = _jax.eval_shape(fwd)
    return out.shape, out.dtype

N_MICROBATCH = 1
ADAM_LR = 0.001
ADAM_B1 = 0.9
ADAM_B2 = 0.999
ADAM_EPS = 1e-08
ADAM_WD = 0.01
ADAM_STEP = 10
PER_EXAMPLE_BATCH_AXIS = {'x': 0, 'loss_target': 0}
SHARED_INPUTS = []
_WEIGHT_DTYPES = {'norm_mix_w': _jnp.float32, 'w_in': _jnp.float32, 'ret_decay_fwd': _jnp.float32, 'ret_decay_bwd': _jnp.float32, 'ret_norm_w': _jnp.float32, 'w_out': _jnp.float32, 'norm_ffn_w': _jnp.float32, 'w_gate': _jnp.float32, 'w_up': _jnp.float32, 'w_down': _jnp.float32, 'norm_final_w': _jnp.float32}
MOMENT_SCALE = {'norm_mix_w': 8.843295e-02, 'w_in': 4.326888e-02, 'ret_decay_fwd': 1.941150e-01, 'ret_decay_bwd': 2.245673e-01, 'ret_norm_w': 5.308604e-02, 'w_out': 4.181819e-02, 'norm_ffn_w': 6.288986e-02, 'w_gate': 2.620183e-02, 'w_up': 2.534453e-02, 'w_down': 4.207430e-02, 'norm_final_w': 1.598679e+01}


def _to_microbatches(a, axis):
    t = _jnp.moveaxis(a, axis, 0)
    t = t.reshape((N_MICROBATCH, t.shape[0] // N_MICROBATCH) + t.shape[1:])
    return _jnp.moveaxis(t, 1, axis + 1)


def setup_inputs(seed: int = 0) -> dict:
    inp = _fwd_setup_inputs(seed)
    key = _jax.random.fold_in(_jax.random.key(seed), 7919)
    shape, _ = _output_shape()
    out = dict(inp)
    out["loss_target"] = _jax.random.normal(_jax.random.fold_in(key, 0), shape, _jnp.float32)
    for i, name in enumerate(TWIN_WEIGHTS):
        w = inp[name].astype(_jnp.float32)
        if MOMENT_SCALE is None:
            s = _jnp.sqrt(_jnp.mean(_jnp.square(w)) + 1e-30)
        else:
            s = MOMENT_SCALE[name]
        km, kv = _jax.random.split(_jax.random.fold_in(key, i + 1))
        out[name] = w
        out["m_" + name] = s * _jax.random.normal(km, w.shape, _jnp.float32)
        out["v_" + name] = (s * s) * _jax.random.uniform(kv, w.shape, _jnp.float32, 0.5, 1.5)
    if N_MICROBATCH > 1:
        for name, axis in PER_EXAMPLE_BATCH_AXIS.items():
            out[name] = _to_microbatches(out[name], axis)
    return {'x': out['x'], 'norm_mix_w': out['norm_mix_w'], 'w_in': out['w_in'], 'ret_decay_fwd': out['ret_decay_fwd'], 'ret_decay_bwd': out['ret_decay_bwd'], 'ret_norm_w': out['ret_norm_w'], 'w_out': out['w_out'], 'norm_ffn_w': out['norm_ffn_w'], 'w_gate': out['w_gate'], 'w_up': out['w_up'], 'w_down': out['w_down'], 'norm_final_w': out['norm_final_w'], 'loss_target': out['loss_target'], 'm_norm_mix_w': out['m_norm_mix_w'], 'm_w_in': out['m_w_in'], 'm_ret_decay_fwd': out['m_ret_decay_fwd'], 'm_ret_decay_bwd': out['m_ret_decay_bwd'], 'm_ret_norm_w': out['m_ret_norm_w'], 'm_w_out': out['m_w_out'], 'm_norm_ffn_w': out['m_norm_ffn_w'], 'm_w_gate': out['m_w_gate'], 'm_w_up': out['m_w_up'], 'm_w_down': out['m_w_down'], 'm_norm_final_w': out['m_norm_final_w'], 'v_norm_mix_w': out['v_norm_mix_w'], 'v_w_in': out['v_w_in'], 'v_ret_decay_fwd': out['v_ret_decay_fwd'], 'v_ret_decay_bwd': out['v_ret_decay_bwd'], 'v_ret_norm_w': out['v_ret_norm_w'], 'v_w_out': out['v_w_out'], 'v_norm_ffn_w': out['v_norm_ffn_w'], 'v_w_gate': out['v_w_gate'], 'v_w_up': out['v_w_up'], 'v_w_down': out['v_w_down'], 'v_norm_final_w': out['v_norm_final_w']}


def _loss(weights, diff, rest, loss_target):
    with _jax.named_scope("forward"):
        args = {**rest, TWIN_DIFF_INPUT: diff, **{k: w.astype(_WEIGHT_DTYPES[k]) for k, w in weights.items()}}
        y = _forward(args)
    with _jax.named_scope("loss_head"):
        err = _jnp.square(y.astype(_jnp.float32) - loss_target)
        return 0.5 * _jnp.sum(_jnp.mean(err, axis=-1)) if err.ndim else 0.5 * err


def _adamw(w, g, m, v):
    m = ADAM_B1 * m + (1.0 - ADAM_B1) * g
    v = ADAM_B2 * v + (1.0 - ADAM_B2) * _jnp.square(g)
    m_hat = m / (1.0 - ADAM_B1 ** ADAM_STEP)
    v_hat = v / (1.0 - ADAM_B2 ** ADAM_STEP)
    delta = -ADAM_LR * (m_hat / (_jnp.sqrt(v_hat) + ADAM_EPS) + ADAM_WD * w)
    return delta, m, v


def reference(x, norm_mix_w, w_in, ret_decay_fwd, ret_decay_bwd, ret_norm_w, w_out, norm_ffn_w, w_gate, w_up, w_down, norm_final_w, loss_target, m_norm_mix_w, m_w_in, m_ret_decay_fwd, m_ret_decay_bwd, m_ret_norm_w, m_w_out, m_norm_ffn_w, m_w_gate, m_w_up, m_w_down, m_norm_final_w, v_norm_mix_w, v_w_in, v_ret_decay_fwd, v_ret_decay_bwd, v_ret_norm_w, v_w_out, v_norm_ffn_w, v_w_gate, v_w_up, v_w_down, v_norm_final_w):
    given = dict(x=x, norm_mix_w=norm_mix_w, w_in=w_in, ret_decay_fwd=ret_decay_fwd, ret_decay_bwd=ret_decay_bwd, ret_norm_w=ret_norm_w, w_out=w_out, norm_ffn_w=norm_ffn_w, w_gate=w_gate, w_up=w_up, w_down=w_down, norm_final_w=norm_final_w, loss_target=loss_target, m_norm_mix_w=m_norm_mix_w, m_w_in=m_w_in, m_ret_decay_fwd=m_ret_decay_fwd, m_ret_decay_bwd=m_ret_decay_bwd, m_ret_norm_w=m_ret_norm_w, m_w_out=m_w_out, m_norm_ffn_w=m_norm_ffn_w, m_w_gate=m_w_gate, m_w_up=m_w_up, m_w_down=m_w_down, m_norm_final_w=m_norm_final_w, v_norm_mix_w=v_norm_mix_w, v_w_in=v_w_in, v_ret_decay_fwd=v_ret_decay_fwd, v_ret_decay_bwd=v_ret_decay_bwd, v_ret_norm_w=v_ret_norm_w, v_w_out=v_w_out, v_norm_ffn_w=v_norm_ffn_w, v_w_gate=v_w_gate, v_w_up=v_w_up, v_w_down=v_w_down, v_norm_final_w=v_norm_final_w)
    weights = {n: given[n] for n in TWIN_WEIGHTS}
    shared = {n: given[n] for n in SHARED_INPUTS}
    per_example = {n: given[n] for n in ['x']}
    grad_fn = _jax.value_and_grad(_loss, argnums=(0, 1))

    def one_microbatch(ex, loss_target):
        ex = dict(ex)
        diff = ex.pop(TWIN_DIFF_INPUT)
        return grad_fn(weights, diff, {**shared, **ex}, loss_target)

    if N_MICROBATCH == 1:
        loss, (grad_w, grad_x) = one_microbatch(per_example, given["loss_target"])
    else:
        def body(carry, xs):
            loss_sum, grad_sum = carry
            l_k, (gw_k, gx_k) = one_microbatch(xs[0], xs[1])
            with _jax.named_scope("update"):
                return (loss_sum + l_k, _jax.tree.map(_jnp.add, grad_sum, gw_k)), gx_k

        init = (_jnp.zeros((), _jnp.float32), _jax.tree.map(_jnp.zeros_like, weights))
        (loss, grad_w), grad_x = _jax.lax.scan(body, init, (per_example, given["loss_target"]))
    with _jax.named_scope("update"):
        delta_w, new_m, new_v = {}, {}, {}
        for n in TWIN_WEIGHTS:
            delta_w[n], new_m[n], new_v[n] = _adamw(weights[n], grad_w[n], given["m_" + n], given["v_" + n])
    return (loss, grad_x, *[grad_w[n] for n in TWIN_WEIGHTS], *[delta_w[n] for n in TWIN_WEIGHTS],
            *[new_m[n] for n in TWIN_WEIGHTS], *[new_v[n] for n in TWIN_WEIGHTS])
```

```python
import functools
import math

import numpy as np
import jax
import jax.numpy as jnp
from jax import lax
from jax.experimental import pallas as pl
from jax.experimental.pallas import tpu as pltpu

F32 = jnp.float32
BF16 = jnp.bfloat16
MESH = pl.DeviceIdType.MESH

HEAD_DIM = 128
RET_CHUNK = 128
EPS = 1e-6
DILATED_PATTERNS = ((128, 1), (512, 4), (2048, 16))
ATT_BLOCK = 256
ATT_REACH = max(w // 2 for w, _ in DILATED_PATTERNS)
ATT_KB = -(-ATT_REACH // ATT_BLOCK)
MASKED = -1e30
ROW_MAX_INIT = -1e29
N_CHIPS = 4
VMEM_LIMIT_BYTES = 56 * 1024 * 1024
ELEM_BLOCK_BYTES = 2 * 1024 * 1024

ADAM_LR = 0.001
ADAM_B1 = 0.9
ADAM_B2 = 0.999
ADAM_EPS = 1e-08
ADAM_WD = 0.01
ADAM_STEP = 10


def _params(sem=None):
    return pltpu.CompilerParams(dimension_semantics=sem, vmem_limit_bytes=VMEM_LIMIT_BYTES)


def _sigmoid(x):
    return 1.0 / (1.0 + jnp.exp(-x))


def _dot(a, b, ta=False, tb=False):
    return lax.dot_general(a, b, (((0 if ta else 1,), (1 if tb else 0,)), ((), ())),
                           preferred_element_type=F32)


def _tile(n, want):
    t = min(n, want) // 128 * 128
    while n % t:
        t -= 128
    return t


def _a_spec(ta, tm, tk):
    return pl.BlockSpec((tk, tm), lambda i, j, k: (k, i)) if ta else pl.BlockSpec((tm, tk), lambda i, j, k: (i, k))


def _b_spec(tb, tk, tn):
    return pl.BlockSpec((tn, tk), lambda i, j, k: (j, k)) if tb else pl.BlockSpec((tk, tn), lambda i, j, k: (k, j))


def _matmul(a, b, *, name, ta=False, tb=False, out_dtype=F32, residual=None, tm=1024, tn=1024, tk=512):
    m, kdim = (a.shape[1], a.shape[0]) if ta else a.shape
    n = b.shape[0] if tb else b.shape[1]
    tm, tn, tk = _tile(m, tm), _tile(n, tn), _tile(kdim, tk)
    nk = kdim // tk

    def body(*refs):
        if residual is None:
            a_ref, b_ref, o_ref, acc = refs
        else:
            a_ref, b_ref, r_ref, o_ref, acc = refs
        k = pl.program_id(2)

        @pl.when(k == 0)
        def _():
            acc[...] = jnp.zeros_like(acc)

        acc[...] += _dot(a_ref[...], b_ref[...], ta, tb)

        @pl.when(k == nk - 1)
        def _():
            r = acc[...]
            if residual is not None:
                r = r + r_ref[...]
            o_ref[...] = r.astype(out_dtype)

    o_spec = pl.BlockSpec((tm, tn), lambda i, j, k: (i, j))
    in_specs = [_a_spec(ta, tm, tk), _b_spec(tb, tk, tn)]
    operands = [a, b]
    if residual is not None:
        in_specs.append(o_spec)
        operands.append(residual)
    return pl.pallas_call(
        body, name=name, grid=(m // tm, n // tn, nk), in_specs=in_specs, out_specs=o_spec,
        out_shape=jax.ShapeDtypeStruct((m, n), out_dtype),
        scratch_shapes=[pltpu.VMEM((tm, tn), F32)],
        compiler_params=_params(("parallel", "parallel", "arbitrary")),
    )(*operands)


def _swiglu_fwd(n2, w_gate, w_up, *, tm=1024, tn=512, tk=512):
    m, kdim = n2.shape
    n = w_gate.shape[1]
    tm, tn, tk = _tile(m, tm), _tile(n, tn), _tile(kdim, tk)
    nk = kdim // tk

    def body(a_ref, g_ref, u_ref, gate_ref, up_ref, act_ref, acc_g, acc_u):
        k = pl.program_id(2)

        @pl.when(k == 0)
        def _():
            acc_g[...] = jnp.zeros_like(acc_g)
            acc_u[...] = jnp.zeros_like(acc_u)

        a = a_ref[...]
        acc_g[...] += _dot(a, g_ref[...])
        acc_u[...] += _dot(a, u_ref[...])

        @pl.when(k == nk - 1)
        def _():
            g = acc_g[...]
            u = acc_u[...]
            gate_ref[...] = g.astype(BF16)
            up_ref[...] = u.astype(BF16)
            act_ref[...] = (g * _sigmoid(g) * u).astype(BF16)

    o_spec = pl.BlockSpec((tm, tn), lambda i, j, k: (i, j))
    o_shape = jax.ShapeDtypeStruct((m, n), BF16)
    return pl.pallas_call(
        body, name="swiglu_fwd", grid=(m // tm, n // tn, nk),
        in_specs=[_a_spec(False, tm, tk), _b_spec(False, tk, tn), _b_spec(False, tk, tn)],
        out_specs=[o_spec] * 3, out_shape=[o_shape] * 3,
        scratch_shapes=[pltpu.VMEM((tm, tn), F32)] * 2,
        compiler_params=_params(("parallel", "parallel", "arbitrary")),
    )(n2, w_gate, w_up)


def _swiglu_bwd_act(dh2, w_down, gate, up, *, tm=1024, tn=512, tk=512):
    m, kdim = dh2.shape
    n = w_down.shape[0]
    tm, tn, tk = _tile(m, tm), _tile(n, tn), _tile(kdim, tk)
    nk = kdim // tk

    def body(a_ref, b_ref, gate_ref, up_ref, dgate_ref, dup_ref, acc):
        k = pl.program_id(2)

        @pl.when(k == 0)
        def _():
            acc[...] = jnp.zeros_like(acc)

        acc[...] += _dot(a_ref[...], b_ref[...], tb=True)

        @pl.when(k == nk - 1)
        def _():
            dact = acc[...]
            g = gate_ref[...].astype(F32)
            u = up_ref[...].astype(F32)
            sg = _sigmoid(g)
            dup_ref[...] = (dact * g * sg).astype(BF16)
            dgate_ref[...] = (dact * u * sg * (1.0 + g * (1.0 - sg))).astype(BF16)

    o_spec = pl.BlockSpec((tm, tn), lambda i, j, k: (i, j))
    o_shape = jax.ShapeDtypeStruct((m, n), BF16)
    return pl.pallas_call(
        body, name="swiglu_bwd_act", grid=(m // tm, n // tn, nk),
        in_specs=[_a_spec(False, tm, tk), _b_spec(True, tk, tn), o_spec, o_spec],
        out_specs=[o_spec] * 2, out_shape=[o_shape] * 2,
        scratch_shapes=[pltpu.VMEM((tm, tn), F32)],
        compiler_params=_params(("parallel", "parallel", "arbitrary")),
    )(dh2, w_down, gate, up)


def _swiglu_bwd_in(dgate, dup, w_gate, w_up, *, tm=1024, tn=1024, tk=512):
    m, kdim = dgate.shape
    n = w_gate.shape[0]
    tm, tn, tk = _tile(m, tm), _tile(n, tn), _tile(kdim, tk)
    nk = kdim // tk

    def body(a1_ref, a2_ref, b1_ref, b2_ref, o_ref, acc):
        k = pl.program_id(2)

        @pl.when(k == 0)
        def _():
            acc[...] = jnp.zeros_like(acc)

        acc[...] += _dot(a1_ref[...], b1_ref[...], tb=True) + _dot(a2_ref[...], b2_ref[...], tb=True)

        @pl.when(k == nk - 1)
        def _():
            o_ref[...] = acc[...]

    a_spec, b_spec = _a_spec(False, tm, tk), _b_spec(True, tk, tn)
    return pl.pallas_call(
        body, name="swiglu_bwd_in", grid=(m // tm, n // tn, nk),
        in_specs=[a_spec, a_spec, b_spec, b_spec],
        out_specs=pl.BlockSpec((tm, tn), lambda i, j, k: (i, j)),
        out_shape=jax.ShapeDtypeStruct((m, n), F32),
        scratch_shapes=[pltpu.VMEM((tm, tn), F32)],
        compiler_params=_params(("parallel", "parallel", "arbitrary")),
    )(dgate, dup, w_gate, w_up)


def _row_block(rows, cols):
    tr = min(rows, max(16, ELEM_BLOCK_BYTES // (4 * cols) // 16 * 16))
    while rows % tr:
        tr -= 16
    return tr


def _rmsnorm_fwd(x, g, *, name):
    s, d = x.shape
    tr = _row_block(s, d)

    def body(x_ref, g_ref, n_ref):
        xv = x_ref[...]
        r = lax.rsqrt(jnp.mean(xv * xv, axis=-1, keepdims=True) + EPS)
        n_ref[...] = (xv * r * g_ref[...]).astype(BF16)

    row = pl.BlockSpec((tr, d), lambda i: (i, 0))
    return pl.pallas_call(
        body, name=name, grid=(s // tr,), in_specs=[row, pl.BlockSpec((1, d), lambda i: (0, 0))],
        out_specs=row, out_shape=jax.ShapeDtypeStruct((s, d), BF16),
        compiler_params=_params(("parallel",)),
    )(x, g)


def _rmsnorm_bwd_rows(xv, gv, dy):
    r = lax.rsqrt(jnp.mean(xv * xv, axis=-1, keepdims=True) + EPS)
    xhat = xv * r
    dxh = dy * gv
    dx = r * (dxh - xhat * jnp.mean(dxh * xhat, axis=-1, keepdims=True))
    return dx, dy * xhat


def _rmsnorm_bwd(dn, x, g, skip, *, name):
    s, d = x.shape
    tr = _row_block(s, d)

    def body(dn_ref, x_ref, g_ref, skip_ref, dx_ref, dxb_ref, dg_ref):
        dx, dgr = _rmsnorm_bwd_rows(x_ref[...], g_ref[...], dn_ref[...])
        dx = dx + skip_ref[...]
        dx_ref[...] = dx
        dxb_ref[...] = dx.astype(BF16)

        @pl.when(pl.program_id(0) == 0)
        def _():
            dg_ref[...] = jnp.zeros_like(dg_ref)

        dg_ref[...] += jnp.sum(dgr, axis=0, keepdims=True)

    row = pl.BlockSpec((tr, d), lambda i: (i, 0))
    vec = pl.BlockSpec((1, d), lambda i: (0, 0))
    return pl.pallas_call(
        body, name=name, grid=(s // tr,), in_specs=[row, row, vec, row],
        out_specs=[row, row, vec],
        out_shape=[jax.ShapeDtypeStruct((s, d), F32), jax.ShapeDtypeStruct((s, d), BF16),
                   jax.ShapeDtypeStruct((1, d), F32)],
        compiler_params=_params(("arbitrary",)),
    )(dn, x, g, skip)


def _loss_head(h2, g, target):
    s, d = h2.shape
    tr = _row_block(s, d)

    def body(h_ref, g_ref, t_ref, dh_ref, dhb_ref, dg_ref, loss_ref):
        hv = h_ref[...]
        gv = g_ref[...]
        r = lax.rsqrt(jnp.mean(hv * hv, axis=-1, keepdims=True) + EPS)
        err = hv * r * gv - t_ref[...]
        dx, dgr = _rmsnorm_bwd_rows(hv, gv, err * (1.0 / d))
        dh_ref[...] = dx
        dhb_ref[...] = dx.astype(BF16)

        @pl.when(pl.program_id(0) == 0)
        def _():
            dg_ref[...] = jnp.zeros_like(dg_ref)
            loss_ref[...] = jnp.zeros_like(loss_ref)

        dg_ref[...] += jnp.sum(dgr, axis=0, keepdims=True)
        row_loss = jnp.mean(err * err, axis=-1, keepdims=True)
        loss_ref[...] += 0.5 * jnp.sum(row_loss, axis=0, keepdims=True)

    row = pl.BlockSpec((tr, d), lambda i: (i, 0))
    vec = pl.BlockSpec((1, d), lambda i: (0, 0))
    one = pl.BlockSpec((1, 1), lambda i: (0, 0))
    return pl.pallas_call(
        body, name="loss_head", grid=(s // tr,), in_specs=[row, vec, row],
        out_specs=[row, row, vec, one],
        out_shape=[jax.ShapeDtypeStruct((s, d), F32), jax.ShapeDtypeStruct((s, d), BF16),
                   jax.ShapeDtypeStruct((1, d), F32), jax.ShapeDtypeStruct((1, 1), F32)],
        compiler_params=_params(("arbitrary",)),
    )(h2, g, target)


def _attention_bias_tables():
    k = np.arange(-ATT_KB, ATT_KB + 1)[:, None, None]
    delta = k * ATT_BLOCK + np.arange(ATT_BLOCK)[None, None, :] - np.arange(ATT_BLOCK)[None, :, None]
    dist = np.abs(delta)
    count = np.zeros(delta.shape, np.int32)
    for window, dilation in DILATED_PATTERNS:
        count += (delta % dilation == 0) & (dist <= window // 2)
    logc = np.where(count > 0, np.log(np.maximum(count, 1)), MASKED)
    return dist.astype(np.float32), logc.astype(np.float32)


def _attention_fwd(proj, slopes, n_heads):
    s = proj.shape[0]
    nq = s // ATT_BLOCK
    scale = HEAD_DIM ** -0.5
    dist, logc = _attention_bias_tables()

    def body(slope_ref, q_ref, k_ref, v_ref, dist_ref, logc_ref, o_ref, lse_ref):
        h, i = pl.program_id(0), pl.program_id(1)
        slope = slope_ref[h]
        q = q_ref[...]

        def step(j, carry):
            m, l, acc = carry
            rows = pl.ds(pl.multiple_of(j * ATT_BLOCK, ATT_BLOCK), ATT_BLOCK)
            kk = j - i + ATT_KB
            sc = _dot(q, k_ref[rows, :], tb=True) * scale + (logc_ref[kk] - slope * dist_ref[kk])
            m_new = jnp.maximum(m, jnp.max(sc, axis=-1, keepdims=True))
            p = jnp.exp(sc - m_new)
            alpha = jnp.exp(m - m_new)
            l = alpha * l + jnp.sum(p, axis=-1, keepdims=True)
            acc = alpha * acc + _dot(p.astype(BF16), v_ref[rows, :])
            return m_new, l, acc

        init = (jnp.full((ATT_BLOCK, 1), ROW_MAX_INIT, F32), jnp.zeros((ATT_BLOCK, 1), F32),
                jnp.zeros((ATT_BLOCK, HEAD_DIM), F32))
        m, l, acc = lax.fori_loop(jnp.maximum(i - ATT_KB, 0), jnp.minimum(i + ATT_KB, nq - 1) + 1, step, init)
        o_ref[...] = (acc / l).astype(BF16)
        lse_ref[...] = jnp.broadcast_to(m + jnp.log(l), (ATT_BLOCK, HEAD_DIM))

    hh = n_heads
    blk = pl.BlockSpec((ATT_BLOCK, HEAD_DIM), lambda h, i: (i, h))
    table = pl.BlockSpec(dist.shape, lambda h, i: (0, 0, 0))
    return pl.pallas_call(
        body, name="attention_fwd", grid=(hh, nq),
        in_specs=[pl.BlockSpec(memory_space=pltpu.SMEM), blk,
                  pl.BlockSpec((s, HEAD_DIM), lambda h, i: (0, hh + h)),
                  pl.BlockSpec((s, HEAD_DIM), lambda h, i: (0, 2 * hh + h)), table, table],
        out_specs=[blk, blk],
        out_shape=[jax.ShapeDtypeStruct((s, hh * HEAD_DIM), BF16), jax.ShapeDtypeStruct((s, hh * HEAD_DIM), F32)],
        compiler_params=_params(("parallel", "arbitrary")),
    )(slopes, proj, proj, proj, jnp.asarray(dist), jnp.asarray(logc))


def _attention_bwd(proj, slopes, out, lse, dmixed, n_heads):
    s = proj.shape[0]
    nq = s // ATT_BLOCK
    scale = HEAD_DIM ** -0.5
    dist, logc = _attention_bias_tables()

    def body(slope_ref, q_ref, k_ref, v_ref, o_ref, do_ref, lse_ref, dist_ref, logc_ref,
             dq_ref, dk_ref, dv_ref, dk_acc, dv_acc):
        h, i = pl.program_id(0), pl.program_id(1)
        slope = slope_ref[h]

        @pl.when(i == 0)
        def _():
            dk_acc[...] = jnp.zeros_like(dk_acc)
            dv_acc[...] = jnp.zeros_like(dv_acc)

        q = q_ref[...]
        do = do_ref[...]
        lse_col = lse_ref[:, :1]
        delta = jnp.sum(do.astype(F32) * o_ref[...].astype(F32), axis=-1, keepdims=True)

        def step(j, dq):
            rows = pl.ds(pl.multiple_of(j * ATT_BLOCK, ATT_BLOCK), ATT_BLOCK)
            kk = j - i + ATT_KB
            kj = k_ref[rows, :]
            vj = v_ref[rows, :]
            sc = _dot(q, kj, tb=True) * scale + (logc_ref[kk] - slope * dist_ref[kk])
            p = jnp.exp(sc - lse_col)
            dv_acc[rows, :] += _dot(p.astype(BF16), do, ta=True)
            dp = _dot(do, vj, tb=True)
            ds = (p * (dp - delta) * scale).astype(BF16)
            dk_acc[rows, :] += _dot(ds, q, ta=True)
            return dq + _dot(ds, kj)

        dq = lax.fori_loop(jnp.maximum(i - ATT_KB, 0), jnp.minimum(i + ATT_KB, nq - 1) + 1, step,
                           jnp.zeros((ATT_BLOCK, HEAD_DIM), F32))
        dq_ref[...] = dq.astype(BF16)

        @pl.when(i == nq - 1)
        def _():
            dk_ref[...] = dk_acc[...].astype(BF16)
            dv_ref[...] = dv_acc[...].astype(BF16)

    hh = n_heads
    blk = pl.BlockSpec((ATT_BLOCK, HEAD_DIM), lambda h, i: (i, h))
    col = pl.BlockSpec((s, HEAD_DIM), lambda h, i: (0, h))
    table = pl.BlockSpec(dist.shape, lambda h, i: (0, 0, 0))
    o_shape = jax.ShapeDtypeStruct((s, hh * HEAD_DIM), BF16)
    return pl.pallas_call(
        body, name="attention_bwd", grid=(hh, nq),
        in_specs=[pl.BlockSpec(memory_space=pltpu.SMEM), blk,
                  pl.BlockSpec((s, HEAD_DIM), lambda h, i: (0, hh + h)),
                  pl.BlockSpec((s, HEAD_DIM), lambda h, i: (0, 2 * hh + h)),
                  blk, blk, blk, table, table],
        out_specs=[blk, col, col], out_shape=[o_shape] * 3,
        scratch_shapes=[pltpu.VMEM((s, HEAD_DIM), F32)] * 2,
        compiler_params=_params(("parallel", "arbitrary")),
    )(slopes, proj, proj, proj, out, dmixed, lse, jnp.asarray(dist), jnp.asarray(logc))


def _ret_decays(lgc, lga, strict_c, strict_a):
    c = RET_CHUNK
    rel = (lax.broadcasted_iota(jnp.int32, (c, c), 0) - lax.broadcasted_iota(jnp.int32, (c, c), 1)).astype(F32)
    in_c = (rel > 0) if strict_c else (rel >= 0)
    in_a = (rel < 0) if strict_a else (rel <= 0)
    mask = (jnp.where(in_c, jnp.exp(lgc * jnp.maximum(rel, 0.0)), 0.0)
            + jnp.where(in_a, jnp.exp(lga * jnp.maximum(-rel, 0.0)), 0.0))
    idx = lax.broadcasted_iota(jnp.int32, (c, 1), 0).astype(F32)
    ones = jnp.ones((1, HEAD_DIM), F32)
    dec = dict(
        rel=rel, mask=mask, idx=idx,
        a_c=jnp.exp(lgc * (idx + 1.0)), b_c=jnp.exp(lgc * (c - 1.0 - idx)), chunk_c=jnp.exp(ones * (lgc * c)),
        a_a=jnp.exp(lga * (c - idx)), b_a=jnp.exp(lga * idx), chunk_a=jnp.exp(ones * (lga * c)),
    )
    return dec


def _scaled(x, col):
    return (x.astype(F32) * col).astype(BF16)


def _chunk_rows(i):
    return pl.ds(pl.multiple_of(i * RET_CHUNK, RET_CHUNK), RET_CHUNK)


def _retention(a, b, c, lg_c, lg_a, *, strict_c, strict_a, scale, n_heads, name, gate=None, norm_w=None):
    s = a[0].shape[0]
    nc = s // RET_CHUNK
    epilogue = gate is not None

    def body(*refs):
        lgc_ref, lga_ref, a_ref, b_ref, c_ref = refs[:5]
        if epilogue:
            g_ref, w_ref, o_ref, mix_ref, sa_ref = refs[5:]
        else:
            o_ref, sa_ref = refs[5:]
        h = pl.program_id(0)
        dec = _ret_decays(lgc_ref[h], lga_ref[h], strict_c, strict_a)

        def reverse(t, state):
            i = nc - 1 - t
            sa_ref[i] = state.astype(BF16)
            rows = _chunk_rows(i)
            return state * dec["chunk_a"] + _dot(_scaled(b_ref[rows, :], dec["b_a"]), c_ref[rows, :], ta=True)

        lax.fori_loop(0, nc, reverse, jnp.zeros((HEAD_DIM, HEAD_DIM), F32))

        def forward(i, state):
            rows = _chunk_rows(i)
            ai, bi, ci = a_ref[rows, :], b_ref[rows, :], c_ref[rows, :]
            inner = (_dot(ai, bi, tb=True) * dec["mask"]).astype(BF16)
            out = (_dot(inner, ci) + _dot(_scaled(ai, dec["a_c"]), state.astype(BF16))
                   + _dot(_scaled(ai, dec["a_a"]), sa_ref[i])) * scale
            o_ref[rows, :] = out.astype(BF16)
            if epilogue:
                r = lax.rsqrt(jnp.mean(out * out, axis=-1, keepdims=True) + EPS)
                g = g_ref[rows, :].astype(F32)
                mix_ref[rows, :] = (out * r * w_ref[...] * (g * _sigmoid(g))).astype(BF16)
            return state * dec["chunk_c"] + _dot(_scaled(bi, dec["b_c"]), ci, ta=True)

        lax.fori_loop(0, nc, forward, jnp.zeros((HEAD_DIM, HEAD_DIM), F32))

    def col(first):
        return pl.BlockSpec((s, HEAD_DIM), lambda h: (0, first + h))

    smem = pl.BlockSpec(memory_space=pltpu.SMEM)
    in_specs = [smem, smem, col(a[1]), col(b[1]), col(c[1])]
    operands = [lg_c, lg_a, a[0], b[0], c[0]]
    o_shape = jax.ShapeDtypeStruct((s, n_heads * HEAD_DIM), BF16)
    out_specs, out_shape = [col(0)], [o_shape]
    if epilogue:
        in_specs += [col(gate[1]), pl.BlockSpec((1, HEAD_DIM), lambda h: (0, h))]
        operands += [gate[0], norm_w]
        out_specs, out_shape = [col(0)] * 2, [o_shape] * 2
    res = pl.pallas_call(
        body, name=name, grid=(n_heads,), in_specs=in_specs, out_specs=out_specs, out_shape=out_shape,
        scratch_shapes=[pltpu.VMEM((nc, HEAD_DIM, HEAD_DIM), BF16)],
        compiler_params=_params(("parallel",)),
    )(*operands)
    return res if epilogue else res[0]


def _retention_decay_grads(a, b, c, e, lg_c, lg_a, *, scale, n_heads):
    s = a[0].shape[0]
    nc = s // RET_CHUNK
    cf = float(RET_CHUNK)

    def body(lgc_ref, lga_ref, a_ref, b_ref, c_ref, e_ref, gc_ref, ga_ref, sa_ref, ta_ref):
        h = pl.program_id(0)
        lgc, lga = lgc_ref[h], lga_ref[h]
        dec = _ret_decays(lgc, lga, True, True)
        rel, idx = dec["rel"], dec["idx"]
        w_c = jnp.where(rel > 0, rel * jnp.exp(lgc * jnp.maximum(rel, 0.0)), 0.0)
        w_a = jnp.where(rel < 0, -rel * jnp.exp(lga * jnp.maximum(-rel, 0.0)), 0.0)
        zero = jnp.zeros((HEAD_DIM, HEAD_DIM), F32)

        def reverse(t, carry):
            st, dst = carry
            i = nc - 1 - t
            sa_ref[i] = st.astype(BF16)
            ta_ref[i] = dst.astype(BF16)
            rows = _chunk_rows(i)
            bi, ci = b_ref[rows, :], c_ref[rows, :]
            st_new = st * dec["chunk_a"] + _dot(_scaled(bi, dec["b_a"]), ci, ta=True)
            dst_new = (cf * st + dst) * dec["chunk_a"] + _dot(_scaled(bi, idx * dec["b_a"]), ci, ta=True)
            return st_new, dst_new

        lax.fori_loop(0, nc, reverse, (zero, zero))

        def forward(i, carry):
            st, dst, acc_c, acc_a = carry
            rows = _chunk_rows(i)
            ai, bi, ci = a_ref[rows, :], b_ref[rows, :], c_ref[rows, :]
            ev = e_ref[rows, :].astype(F32)
            pg = _dot(ai, bi, tb=True) * _dot(e_ref[rows, :], ci, tb=True)
            a_c, a_a = _scaled(ai, dec["a_c"]), _scaled(ai, dec["a_a"])
            inter_c = _dot(a_c, st.astype(BF16)) * (idx + 1.0) + _dot(a_c, dst.astype(BF16))
            inter_a = _dot(a_a, sa_ref[i]) * (cf - idx) + _dot(a_a, ta_ref[i])
            acc_c = acc_c + jnp.sum(pg * w_c, axis=0, keepdims=True) + jnp.sum(inter_c * ev, axis=0, keepdims=True)
            acc_a = acc_a + jnp.sum(pg * w_a, axis=0, keepdims=True) + jnp.sum(inter_a * ev, axis=0, keepdims=True)
            st_new = st * dec["chunk_c"] + _dot(_scaled(bi, dec["b_c"]), ci, ta=True)
            dst_new = ((cf * st + dst) * dec["chunk_c"]
                       + _dot(_scaled(bi, (cf - 1.0 - idx) * dec["b_c"]), ci, ta=True))
            return st_new, dst_new, acc_c, acc_a

        row = jnp.zeros((1, HEAD_DIM), F32)
        _, _, acc_c, acc_a = lax.fori_loop(0, nc, forward, (zero, zero, row, row))
        gc_ref[...] = jnp.broadcast_to(jnp.sum(acc_c, axis=-1, keepdims=True) * scale, gc_ref.shape)
        ga_ref[...] = jnp.broadcast_to(jnp.sum(acc_a, axis=-1, keepdims=True) * scale, ga_ref.shape)

    def col(first):
        return pl.BlockSpec((s, HEAD_DIM), lambda h: (0, first + h))

    smem = pl.BlockSpec(memory_space=pltpu.SMEM)
    o_spec = pl.BlockSpec((1, 8, HEAD_DIM), lambda h: (h, 0, 0))
    o_shape = jax.ShapeDtypeStruct((n_heads, 8, HEAD_DIM), F32)
    gc, ga = pl.pallas_call(
        body, name="retention_decay_grads", grid=(n_heads,),
        in_specs=[smem, smem, col(a[1]), col(b[1]), col(c[1]), col(e[1])],
        out_specs=[o_spec] * 2, out_shape=[o_shape] * 2,
        scratch_shapes=[pltpu.VMEM((nc, HEAD_DIM, HEAD_DIM), BF16)] * 2,
        compiler_params=_params(("parallel",)),
    )(lg_c, lg_a, a[0], b[0], c[0], e[0])
    return gc[:, 0, 0], ga[:, 0, 0]


def _ret_gate_bwd(dmixed, first_col, out, proj, gate_col, norm_w, n_heads):
    s = out.shape[0]
    tr = _row_block(s, 8 * HEAD_DIM)

    def body(dm_ref, o_ref, g_ref, w_ref, do_ref, dg_ref, dw_ref):
        dm = dm_ref[...].astype(F32)
        ov = o_ref[...].astype(F32)
        g = g_ref[...].astype(F32)
        w = w_ref[...]
        r = lax.rsqrt(jnp.mean(ov * ov, axis=-1, keepdims=True) + EPS)
        ohat = ov * r
        sg = _sigmoid(g)
        silu = g * sg
        dg_ref[...] = (dm * ohat * w * sg * (1.0 + g * (1.0 - sg))).astype(BF16)
        dohat = dm * w * silu
        do_ref[...] = (r * (dohat - ohat * jnp.mean(dohat * ohat, axis=-1, keepdims=True))).astype(BF16)

        @pl.when(pl.program_id(1) == 0)
        def _():
            dw_ref[...] = jnp.zeros_like(dw_ref)

        dw_ref[...] += jnp.sum(dm * ohat * silu, axis=0, keepdims=True)

    def blk(first):
        return pl.BlockSpec((tr, HEAD_DIM), lambda h, i: (i, first + h))

    vec = pl.BlockSpec((1, HEAD_DIM), lambda h, i: (0, h))
    o_shape = jax.ShapeDtypeStruct((s, n_heads * HEAD_DIM), BF16)
    return pl.pallas_call(
        body, name="ret_gate_bwd", grid=(n_heads, s // tr),
        in_specs=[blk(first_col), blk(0), blk(gate_col), vec],
        out_specs=[blk(0), blk(0), vec],
        out_shape=[o_shape, o_shape, jax.ShapeDtypeStruct((1, n_heads * HEAD_DIM), F32)],
        compiler_params=_params(("parallel", "arbitrary")),
    )(dmixed, out, proj, norm_w)


def _local_step(x, target, norm_mix_w, ret_decay_fwd, ret_decay_bwd, ret_norm_w, norm_ffn_w, norm_final_w,
                w_in, w_out, w_gate, w_up, w_down):
    d = x.shape[1]
    nh = d // (2 * HEAD_DIM)
    scale = HEAD_DIM ** -0.5
    slopes = jnp.exp2(-8.0 * jnp.arange(1, nh + 1, dtype=F32) / nh)
    lg_f = -jnp.exp(ret_decay_fwd)
    lg_b = -jnp.exp(ret_decay_bwd)
    q_r, k_r, v_r, g_r = 3 * nh, 4 * nh, 5 * nh, 6 * nh

    n1 = _rmsnorm_fwd(x, norm_mix_w, name="norm_mix_fwd")
    proj = _matmul(n1, w_in, name="in_proj", out_dtype=BF16)
    attn, lse = _attention_fwd(proj, slopes, nh)
    ret, ret_mixed = _retention((proj, q_r), (proj, k_r), (proj, v_r), lg_f, lg_b, strict_c=False, strict_a=True,
                                scale=scale, n_heads=nh, name="retention_fwd", gate=(proj, g_r), norm_w=ret_norm_w)
    mixed = jnp.concatenate([attn, ret_mixed], axis=1)
    h1 = _matmul(mixed, w_out, name="out_proj", residual=x)
    n2 = _rmsnorm_fwd(h1, norm_ffn_w, name="norm_ffn_fwd")
    gate, up, act = _swiglu_fwd(n2, w_gate, w_up)
    h2 = _matmul(act, w_down, name="down_proj", residual=h1)
    dh2, dh2_b, d_norm_final, loss = _loss_head(h2, norm_final_w, target)

    dgate, dup = _swiglu_bwd_act(dh2_b, w_down, gate, up)
    g_down = _matmul(act, dh2_b, name="grad_w_down", ta=True)
    g_gate = _matmul(n2, dgate, name="grad_w_gate", ta=True)
    g_up = _matmul(n2, dup, name="grad_w_up", ta=True)
    dn2 = _swiglu_bwd_in(dgate, dup, w_gate, w_up)
    dh1, dh1_b, d_norm_ffn = _rmsnorm_bwd(dn2, h1, norm_ffn_w, dh2, name="norm_ffn_bwd")

    dmixed = _matmul(dh1_b, w_out, name="out_proj_bwd", tb=True, out_dtype=BF16)
    g_out = _matmul(mixed, dh1_b, name="grad_w_out", ta=True)
    d_ret, dg_r, d_ret_norm = _ret_gate_bwd(dmixed, nh, ret, proj, g_r, ret_norm_w, nh)
    dq_r = _retention((d_ret, 0), (proj, v_r), (proj, k_r), lg_f, lg_b, strict_c=False, strict_a=True,
                      scale=scale, n_heads=nh, name="retention_dq")
    dv_r = _retention((proj, k_r), (proj, q_r), (d_ret, 0), lg_b, lg_f, strict_c=True, strict_a=False,
                      scale=scale, n_heads=nh, name="retention_dv")
    dk_r = _retention((proj, v_r), (d_ret, 0), (proj, q_r), lg_b, lg_f, strict_c=True, strict_a=False,
                      scale=scale, n_heads=nh, name="retention_dk")
    dlg_f, dlg_b = _retention_decay_grads((proj, q_r), (proj, k_r), (proj, v_r), (d_ret, 0), lg_f, lg_b,
                                          scale=scale, n_heads=nh)
    dq_a, dk_a, dv_a = _attention_bwd(proj, slopes, attn, lse, dmixed, nh)
    dproj = jnp.concatenate([dq_a, dk_a, dv_a, dq_r, dk_r, dv_r, dg_r], axis=1)
    dn1 = _matmul(dproj, w_in, name="in_proj_bwd", tb=True)
    g_in = _matmul(n1, dproj, name="grad_w_in", ta=True)
    dx, _, d_norm_mix = _rmsnorm_bwd(dn1, x, norm_mix_w, dh1, name="norm_mix_bwd")

    small = dict(loss=loss[0, 0], norm_mix_w=d_norm_mix, ret_decay_fwd=dlg_f * lg_f, ret_decay_bwd=dlg_b * lg_b,
                 ret_norm_w=d_ret_norm, norm_ffn_w=d_norm_ffn, norm_final_w=d_norm_final)
    return dx, dict(w_in=g_in, w_out=g_out, w_gate=g_gate, w_up=g_up, w_down=g_down), small


def _mesh_position():
    x, y, c = lax.axis_index("x"), lax.axis_index("y"), lax.axis_index("c")
    chips = [(1 - x, y), (x, 1 - y), (1 - x, 1 - y)]
    return x, y, c, chips


def _ds(start, size):
    if isinstance(start, int):
        return pl.ds(start, size)
    return pl.ds(pl.multiple_of(start * size, size), size)


def _region(ref, axis, shard, half, shard_size, half_size):
    along = slice(None) if shard is None else _ds(shard, shard_size)
    other = slice(None) if half is None else _ds(half, half_size)
    return ref.at[other, along] if axis == 1 else ref.at[along, other]


def _all_gather_weights(shards, axes):
    n = len(shards)
    full_shapes = [tuple(d * (N_CHIPS if a == ax else 1) for a, d in enumerate(w.shape)) for w, ax in zip(shards, axes)]

    def body(*refs):
        ins, outs = refs[:n], refs[n:2 * n]
        local_sem, ici_send, ici_recv, d2d_send, d2d_recv = refs[2 * n:]
        x, y, c, chips = _mesh_position()
        me = 2 * x + y

        def half_of_shard(w, ref, half):
            return _region(ref, axes[w], None, half, 0, shards[w].shape[1 - axes[w]] // 2)

        def landing(w, shard, half):
            return _region(outs[w], axes[w], shard, half, shards[w].shape[axes[w]], shards[w].shape[1 - axes[w]] // 2)

        local = [pltpu.make_async_copy(ins[w], landing(w, me, None), local_sem.at[w]) for w in range(n)]
        for cp in local:
            cp.start()

        def over_ici(w, k):
            return pltpu.make_async_remote_copy(
                src_ref=half_of_shard(w, ins[w], c), dst_ref=landing(w, me, c),
                send_sem=ici_send.at[w, k], recv_sem=ici_recv.at[w, k],
                device_id=(chips[k][0], chips[k][1], c), device_id_type=MESH)

        def arrived_over_ici(w, k):
            shard = 2 * chips[k][0] + chips[k][1]
            return pltpu.make_async_remote_copy(
                src_ref=landing(w, shard, c), dst_ref=landing(w, shard, c),
                send_sem=d2d_send.at[w, k], recv_sem=ici_recv.at[w, k],
                device_id=(x, y, 1 - c), device_id_type=MESH)

        def to_sibling(w, k):
            shard = 2 * chips[k][0] + chips[k][1]
            return pltpu.make_async_remote_copy(
                src_ref=landing(w, shard, c), dst_ref=landing(w, shard, c),
                send_sem=d2d_send.at[w, k], recv_sem=d2d_recv.at[w, k],
                device_id=(x, y, 1 - c), device_id_type=MESH)

        def arrived_from_sibling(w, k):
            shard = 2 * chips[k][0] + chips[k][1]
            return pltpu.make_async_remote_copy(
                src_ref=landing(w, shard, 1 - c), dst_ref=landing(w, shard, 1 - c),
                send_sem=d2d_send.at[w, k], recv_sem=d2d_recv.at[w, k],
                device_id=(x, y, 1 - c), device_id_type=MESH)

        for w in range(n):
            for k in range(3):
                over_ici(w, k).start()
        for w in range(n):
            for k in range(3):
                arrived_over_ici(w, k).wait_recv()
                to_sibling(w, k).start()
        for w in range(n):
            for k in range(3):
                arrived_from_sibling(w, k).wait_recv()
        for w in range(n):
            for k in range(3):
                over_ici(w, k).wait_send()
                to_sibling(w, k).wait_send()
        for cp in local:
            cp.wait()

    hbm = pl.BlockSpec(memory_space=pl.ANY)
    return pl.pallas_call(
        body, name="all_gather_weights", in_specs=[hbm] * n, out_specs=[hbm] * n,
        out_shape=[jax.ShapeDtypeStruct(s, BF16) for s in full_shapes],
        scratch_shapes=[pltpu.SemaphoreType.DMA((n,))] + [pltpu.SemaphoreType.DMA((n, 3))] * 4,
    )(*shards)


def _exchange_halves(grads, axes):
    n = len(grads)

    def half_shape(w):
        return tuple(d // 2 if a != axes[w] else d for a, d in enumerate(grads[w].shape))

    def body(*refs):
        ins, outs = refs[:n], refs[n:2 * n]
        send_sem, recv_sem = refs[2 * n:]
        x, y, c, _ = _mesh_position()

        def copy(w):
            return pltpu.make_async_remote_copy(
                src_ref=_region(ins[w], axes[w], None, 1 - c, 0, half_shape(w)[1 - axes[w]]), dst_ref=outs[w],
                send_sem=send_sem.at[w], recv_sem=recv_sem.at[w], device_id=(x, y, 1 - c), device_id_type=MESH)

        for w in range(n):
            copy(w).start()
        for w in range(n):
            copy(w).wait()

    hbm = pl.BlockSpec(memory_space=pl.ANY)
    return pl.pallas_call(
        body, name="grad_exchange_halves", in_specs=[hbm] * n, out_specs=[hbm] * n,
        out_shape=[jax.ShapeDtypeStruct(half_shape(w), F32) for w in range(n)],
        scratch_shapes=[pltpu.SemaphoreType.DMA((n,))] * 2,
    )(*grads)


def _half_block_spec(axis, block, half_blocks, use_half):
    if axis == 1:
        if use_half:
            return pl.BlockSpec(block, lambda i, pos: (pos[0] * half_blocks + i, 0))
        return pl.BlockSpec(block, lambda i, pos: (i, 0))
    if use_half:
        return pl.BlockSpec(block, lambda i, pos: (i, pos[0]))
    return pl.BlockSpec(block, lambda i, pos: (i, 0))


def _add_halves(grad, received, axis, pos, *, name):
    rows, cols = received.shape
    tr = _row_block(rows, cols)
    nb = rows // tr

    def body(pos_ref, g_ref, r_ref, o_ref):
        o_ref[...] = (g_ref[...] + r_ref[...]).astype(BF16)

    blk = (tr, cols)
    return pl.pallas_call(
        body, name=name, out_shape=jax.ShapeDtypeStruct((rows, cols), BF16),
        grid_spec=pltpu.PrefetchScalarGridSpec(
            num_scalar_prefetch=1, grid=(nb,),
            in_specs=[_half_block_spec(axis, blk, nb, True), _half_block_spec(axis, blk, nb, False)],
            out_specs=_half_block_spec(axis, blk, nb, False)),
        compiler_params=_params(("parallel",)),
    )(pos, grad, received)


def _send_chip_sums(sums, axes):
    n = len(sums)

    def part_shape(w):
        return tuple(d // N_CHIPS if a == axes[w] else d for a, d in enumerate(sums[w].shape))

    def body(*refs):
        ins, outs = refs[:n], refs[n:2 * n]
        send_sem, recv_sem = refs[2 * n:]
        x, y, c, chips = _mesh_position()

        def copy(w, k):
            shard = 2 * chips[k][0] + chips[k][1]
            return pltpu.make_async_remote_copy(
                src_ref=_region(ins[w], axes[w], shard, None, part_shape(w)[axes[w]], 0), dst_ref=outs[w].at[k],
                send_sem=send_sem.at[w, k], recv_sem=recv_sem.at[w, k],
                device_id=(chips[k][0], chips[k][1], c), device_id_type=MESH)

        for w in range(n):
            for k in range(3):
                copy(w, k).start()
        for w in range(n):
            for k in range(3):
                copy(w, k).wait()

    hbm = pl.BlockSpec(memory_space=pl.ANY)
    return pl.pallas_call(
        body, name="grad_send_chip_sums", in_specs=[hbm] * n, out_specs=[hbm] * n,
        out_shape=[jax.ShapeDtypeStruct((3,) + part_shape(w), BF16) for w in range(n)],
        scratch_shapes=[pltpu.SemaphoreType.DMA((n, 3))] * 2,
    )(*sums)


def _sum_chip_parts(grad, received, parts, axis, pos, *, name):
    _, rows, cols = parts.shape
    tr = _row_block(rows, cols)
    nb = rows // tr
    blk = (tr, cols)

    def body(pos_ref, g_ref, r_ref, p_ref, o_ref):
        total = g_ref[...] + r_ref[...]
        for k in range(3):
            total = total + p_ref[k].astype(F32)
        o_ref[...] = total

    if axis == 1:
        g_spec = pl.BlockSpec(blk, lambda i, pos: (pos[0] * nb + i, pos[1]))
        r_spec = pl.BlockSpec(blk, lambda i, pos: (i, pos[1]))
    else:
        g_spec = pl.BlockSpec(blk, lambda i, pos: (pos[1] * nb + i, pos[0]))
        r_spec = pl.BlockSpec(blk, lambda i, pos: (pos[1] * nb + i, 0))
    return pl.pallas_call(
        body, name=name, out_shape=jax.ShapeDtypeStruct((rows, cols), F32),
        grid_spec=pltpu.PrefetchScalarGridSpec(
            num_scalar_prefetch=1, grid=(nb,),
            in_specs=[g_spec, r_spec, pl.BlockSpec((3,) + blk, lambda i, pos: (0, i, 0))],
            out_specs=pl.BlockSpec(blk, lambda i, pos: (i, 0))),
        compiler_params=_params(("parallel",)),
    )(pos, grad, received, parts)


def _join_halves(halves, axes):
    n = len(halves)

    def shard_shape(w):
        return tuple(d * 2 if a != axes[w] else d for a, d in enumerate(halves[w].shape))

    def body(*refs):
        ins, outs = refs[:n], refs[n:2 * n]
        local_sem, send_sem, recv_sem = refs[2 * n:]
        x, y, c, _ = _mesh_position()

        def place(w, half):
            return _region(outs[w], axes[w], None, half, 0, halves[w].shape[1 - axes[w]])

        local = [pltpu.make_async_copy(ins[w], place(w, c), local_sem.at[w]) for w in range(n)]

        def to_sibling(w):
            return pltpu.make_async_remote_copy(
                src_ref=ins[w], dst_ref=place(w, c), send_sem=send_sem.at[w], recv_sem=recv_sem.at[w],
                device_id=(x, y, 1 - c), device_id_type=MESH)

        def from_sibling(w):
            return pltpu.make_async_remote_copy(
                src_ref=ins[w], dst_ref=place(w, 1 - c), send_sem=send_sem.at[w], recv_sem=recv_sem.at[w],
                device_id=(x, y, 1 - c), device_id_type=MESH)

        for w in range(n):
            local[w].start()
            to_sibling(w).start()
        for w in range(n):
            from_sibling(w).wait_recv()
            to_sibling(w).wait_send()
            local[w].wait()

    hbm = pl.BlockSpec(memory_space=pl.ANY)
    return pl.pallas_call(
        body, name="grad_join_halves", in_specs=[hbm] * n, out_specs=[hbm] * n,
        out_shape=[jax.ShapeDtypeStruct(shard_shape(w), F32) for w in range(n)],
        scratch_shapes=[pltpu.SemaphoreType.DMA((n,))] * 3,
    )(*halves)


def _all_reduce_small(vec):
    rows, cols = vec.shape

    def body(v_ref, o_ref, land_ref, send_sem, recv_sem):
        x, y, c, _ = _mesh_position()
        me = 4 * x + 2 * y + c
        land_ref[me] = v_ref[...]
        copies = []
        for k in range(1, 8):
            px, py, pc = x ^ (k >> 2), y ^ ((k >> 1) & 1), c ^ (k & 1)
            copies.append(pltpu.make_async_remote_copy(
                src_ref=v_ref, dst_ref=land_ref.at[me], send_sem=send_sem.at[k], recv_sem=recv_sem.at[k],
                device_id=(px, py, pc), device_id_type=MESH))
        for cp in copies:
            cp.start()
        for k in range(1, 8):
            peer = me ^ k
            pltpu.make_async_remote_copy(
                src_ref=v_ref, dst_ref=land_ref.at[peer], send_sem=send_sem.at[k], recv_sem=recv_sem.at[k],
                device_id=(x, y, c), device_id_type=MESH).wait_recv()
        for cp in copies:
            cp.wait_send()
        total = land_ref[0]
        for k in range(1, 8):
            total = total + land_ref[k]
        o_ref[...] = total

    vmem = pl.BlockSpec(memory_space=pltpu.VMEM)
    return pl.pallas_call(
        body, name="all_reduce_small", in_specs=[vmem], out_specs=vmem,
        out_shape=jax.ShapeDtypeStruct((rows, cols), F32),
        scratch_shapes=[pltpu.VMEM((8, rows, cols), F32), pltpu.SemaphoreType.DMA((8,)), pltpu.SemaphoreType.DMA((8,))],
    )(vec)


def _adamw(w, g, m, v, *, name):
    rows, cols = w.shape
    tr = _row_block(rows, cols) if rows % 8 == 0 else rows
    bc1 = 1.0 - ADAM_B1 ** ADAM_STEP
    bc2 = 1.0 - ADAM_B2 ** ADAM_STEP

    def body(w_ref, g_ref, m_ref, v_ref, d_ref, mo_ref, vo_ref):
        gv = g_ref[...]
        mn = ADAM_B1 * m_ref[...] + (1.0 - ADAM_B1) * gv
        vn = ADAM_B2 * v_ref[...] + (1.0 - ADAM_B2) * (gv * gv)
        mo_ref[...] = mn
        vo_ref[...] = vn
        d_ref[...] = -ADAM_LR * ((mn / bc1) / (jnp.sqrt(vn / bc2) + ADAM_EPS) + ADAM_WD * w_ref[...])

    blk = pl.BlockSpec((tr, cols), lambda i: (i, 0))
    shape = jax.ShapeDtypeStruct((rows, cols), F32)
    return pl.pallas_call(
        body, name=name, grid=(rows // tr,), in_specs=[blk] * 4, out_specs=[blk] * 3, out_shape=[shape] * 3,
        compiler_params=_params(("parallel",)),
    )(w, g, m, v)


def _to_bf16(w, *, name):
    rows, cols = w.shape
    tr = _row_block(rows, cols)

    def body(w_ref, o_ref):
        o_ref[...] = w_ref[...].astype(BF16)

    blk = pl.BlockSpec((tr, cols), lambda i: (i, 0))
    return pl.pallas_call(
        body, name=name, grid=(rows // tr,), in_specs=[blk], out_specs=blk,
        out_shape=jax.ShapeDtypeStruct((rows, cols), BF16), compiler_params=_params(("parallel",)),
    )(w)


BIG = ("w_in", "w_out", "w_gate", "w_up", "w_down")
BIG_AXIS = dict(w_in=1, w_out=0, w_gate=1, w_up=1, w_down=0)
SMALL = ("norm_mix_w", "ret_decay_fwd", "ret_decay_bwd", "ret_norm_w", "norm_ffn_w", "norm_final_w")
ALL_WEIGHTS = ("norm_mix_w", "w_in", "ret_decay_fwd", "ret_decay_bwd", "ret_norm_w", "w_out", "norm_ffn_w",
               "w_gate", "w_up", "w_down", "norm_final_w")
SMALL_ROW = 128 * 8


def _pack_small(small):
    pieces = [jnp.reshape(small["loss"], (1,))] + [jnp.reshape(small[k], (-1,)) for k in SMALL]
    rows = []
    for p in pieces:
        pad = -p.shape[0] % (8 * SMALL_ROW)
        rows.append(jnp.reshape(jnp.pad(p, (0, pad)), (-1, SMALL_ROW)))
    return jnp.concatenate(rows, axis=0)


def _unpack_small(block, like):
    out, row = {}, 0
    for k in ("loss",) + SMALL:
        size = 1 if k == "loss" else like[k].size
        nrows = -(-size // (8 * SMALL_ROW)) * 8
        out[k] = jnp.reshape(block[row:row + nrows], (-1,))[:size]
        row += nrows
    return out


def kernel(x, norm_mix_w, w_in, ret_decay_fwd, ret_decay_bwd, ret_norm_w, w_out, norm_ffn_w, w_gate, w_up, w_down, norm_final_w, loss_target, m_norm_mix_w, m_w_in, m_ret_decay_fwd, m_ret_decay_bwd, m_ret_norm_w, m_w_out, m_norm_ffn_w, m_w_gate, m_w_up, m_w_down, m_norm_final_w, v_norm_mix_w, v_w_in, v_ret_decay_fwd, v_ret_decay_bwd, v_ret_norm_w, v_w_out, v_norm_ffn_w, v_w_gate, v_w_up, v_w_down, v_norm_final_w):
    weights = dict(norm_mix_w=norm_mix_w, w_in=w_in, ret_decay_fwd=ret_decay_fwd, ret_decay_bwd=ret_decay_bwd,
                   ret_norm_w=ret_norm_w, w_out=w_out, norm_ffn_w=norm_ffn_w, w_gate=w_gate, w_up=w_up,
                   w_down=w_down, norm_final_w=norm_final_w)
    m_in = dict(norm_mix_w=m_norm_mix_w, w_in=m_w_in, ret_decay_fwd=m_ret_decay_fwd, ret_decay_bwd=m_ret_decay_bwd,
                ret_norm_w=m_ret_norm_w, w_out=m_w_out, norm_ffn_w=m_norm_ffn_w, w_gate=m_w_gate, w_up=m_w_up,
                w_down=m_w_down, norm_final_w=m_norm_final_w)
    v_in = dict(norm_mix_w=v_norm_mix_w, w_in=v_w_in, ret_decay_fwd=v_ret_decay_fwd, ret_decay_bwd=v_ret_decay_bwd,
                ret_norm_w=v_ret_norm_w, w_out=v_w_out, norm_ffn_w=v_norm_ffn_w, w_gate=v_w_gate, w_up=v_w_up,
                w_down=v_w_down, norm_final_w=v_norm_final_w)
    axes = [BIG_AXIS[k] for k in BIG]
    pos = jnp.stack([lax.axis_index("c"), 2 * lax.axis_index("x") + lax.axis_index("y")]).astype(jnp.int32)

    shards = [_to_bf16(weights[k][0], name="cast_" + k) for k in BIG]
    full = dict(zip(BIG, _all_gather_weights(shards, axes)))

    dx, grads, small = _local_step(
        x[0], loss_target[0], norm_mix_w, ret_decay_fwd[0], ret_decay_bwd[0], ret_norm_w, norm_ffn_w,
        norm_final_w[None, :], full["w_in"], full["w_out"], full["w_gate"], full["w_up"], full["w_down"])

    g_full = [grads[k] for k in BIG]
    received = _exchange_halves(g_full, axes)
    sums = [_add_halves(g, r, ax, pos, name="grad_add_halves_" + k) for k, g, r, ax in zip(BIG, g_full, received, axes)]
    parts = _send_chip_sums(sums, axes)
    halves = [_sum_chip_parts(g, r, p, ax, pos, name="grad_sum_parts_" + k)
              for k, g, r, p, ax in zip(BIG, g_full, received, parts, axes)]
    grad_w = dict(zip(BIG, _join_halves(halves, axes)))

    like = {k: weights[k] for k in SMALL}
    reduced = _unpack_small(_all_reduce_small(_pack_small(small)), like)
    loss = reduced["loss"][0]
    for k in SMALL:
        grad_w[k] = jnp.reshape(reduced[k], (1, -1))

    delta, new_m, new_v = {}, {}, {}
    for k in ALL_WEIGHTS:
        shape = weights[k].shape
        as2d = (lambda t: jnp.reshape(t, (-1, shape[-1])))
        delta[k], new_m[k], new_v[k] = (jnp.reshape(t, shape) for t in _adamw(
            as2d(weights[k]), as2d(grad_w[k]), as2d(m_in[k]), as2d(v_in[k]), name="adamw_" + k))
        grad_w[k] = jnp.reshape(grad_w[k], shape)

    return (loss, dx[None], *[grad_w[k] for k in ALL_WEIGHTS], *[delta[k] for k in ALL_WEIGHTS],
            *[new_m[k] for k in ALL_WEIGHTS], *[new_v[k] for k in ALL_WEIGHTS])
```

```python
import functools
import math

import numpy as np
import jax
import jax.numpy as jnp
from jax import lax
from jax.experimental import pallas as pl
from jax.experimental.pallas import tpu as pltpu

F32 = jnp.float32
BF16 = jnp.bfloat16
MESH = pl.DeviceIdType.MESH

HEAD_DIM = 128
RET_CHUNK = 128
EPS = 1e-6
DILATED_PATTERNS = ((128, 1), (512, 4), (2048, 16))
ATT_BLOCK = 256
ATT_REACH = max(w // 2 for w, _ in DILATED_PATTERNS)
ATT_KB = -(-ATT_REACH // ATT_BLOCK)
ATT_WINDOW = 2 * ATT_KB + 1
MASKED = -1e30
ROW_MAX_INIT = -1e29
N_CHIPS = 4
VMEM_LIMIT_BYTES = 56 * 1024 * 1024
ELEM_BLOCK_BYTES = 2 * 1024 * 1024

ADAM_LR = 0.001
ADAM_B1 = 0.9
ADAM_B2 = 0.999
ADAM_EPS = 1e-08
ADAM_WD = 0.01
ADAM_STEP = 10


def _params(sem=None):
    return pltpu.CompilerParams(dimension_semantics=sem, vmem_limit_bytes=VMEM_LIMIT_BYTES)


def _sigmoid(x):
    return 1.0 / (1.0 + jnp.exp(-x))


def _dot(a, b, ta=False, tb=False):
    return lax.dot_general(a, b, (((0 if ta else 1,), (1 if tb else 0,)), ((), ())),
                           preferred_element_type=F32)


def _tile(n, want):
    t = min(n, want) // 128 * 128
    while n % t:
        t -= 128
    return t


def _a_spec(ta, tm, tk):
    return pl.BlockSpec((tk, tm), lambda i, j, k: (k, i)) if ta else pl.BlockSpec((tm, tk), lambda i, j, k: (i, k))


def _b_spec(tb, tk, tn):
    return pl.BlockSpec((tn, tk), lambda i, j, k: (j, k)) if tb else pl.BlockSpec((tk, tn), lambda i, j, k: (k, j))


def _accumulate(accs, nk, products, finish):
    if nk == 1:
        finish(*products())
        return
    k = pl.program_id(2)

    @pl.when(k == 0)
    def _():
        for acc, p in zip(accs, products()):
            acc[...] = p

    if nk > 2:
        @pl.when(jnp.logical_and(k > 0, k < nk - 1))
        def _():
            for acc, p in zip(accs, products()):
                acc[...] += p

    @pl.when(k == nk - 1)
    def _():
        finish(*[acc[...] + p for acc, p in zip(accs, products())])


def _matmul(a, b, *, name, ta=False, tb=False, out_dtype=F32, residual=None, tm=1024, tn=1024, tk=2048):
    m, kdim = (a.shape[1], a.shape[0]) if ta else a.shape
    n = b.shape[0] if tb else b.shape[1]
    tm, tn, tk = _tile(m, tm), _tile(n, tn), _tile(kdim, tk)
    nk = kdim // tk

    def body(*refs):
        a_ref, b_ref = refs[:2]
        r_ref = refs[2] if residual is not None else None
        o_ref = refs[-1] if nk == 1 else refs[-2]

        def finish(total):
            if residual is not None:
                total = total + r_ref[...]
            o_ref[...] = total.astype(out_dtype)

        _accumulate(refs[-1:] if nk > 1 else (), nk, lambda: (_dot(a_ref[...], b_ref[...], ta, tb),), finish)

    o_spec = pl.BlockSpec((tm, tn), lambda i, j, k: (i, j))
    in_specs = [_a_spec(ta, tm, tk), _b_spec(tb, tk, tn)]
    operands = [a, b]
    if residual is not None:
        in_specs.append(o_spec)
        operands.append(residual)
    return pl.pallas_call(
        body, name=name, grid=(m // tm, n // tn, nk), in_specs=in_specs, out_specs=o_spec,
        out_shape=jax.ShapeDtypeStruct((m, n), out_dtype),
        scratch_shapes=[pltpu.VMEM((tm, tn), F32)] * (nk > 1),
        compiler_params=_params(("parallel", "parallel", "arbitrary")),
    )(*operands)


def _swiglu_fwd(n2, w_gate, w_up, *, tm=1024, tn=512, tk=2048):
    m, kdim = n2.shape
    n = w_gate.shape[1]
    tm, tn, tk = _tile(m, tm), _tile(n, tn), _tile(kdim, tk)
    nk = kdim // tk

    def body(a_ref, g_ref, u_ref, gate_ref, up_ref, act_ref, *acc):
        def products():
            a = a_ref[...]
            return _dot(a, g_ref[...]), _dot(a, u_ref[...])

        def finish(g, u):
            gate_ref[...] = g.astype(BF16)
            up_ref[...] = u.astype(BF16)
            act_ref[...] = (g * _sigmoid(g) * u).astype(BF16)

        _accumulate(acc, nk, products, finish)

    o_spec = pl.BlockSpec((tm, tn), lambda i, j, k: (i, j))
    o_shape = jax.ShapeDtypeStruct((m, n), BF16)
    return pl.pallas_call(
        body, name="swiglu_fwd", grid=(m // tm, n // tn, nk),
        in_specs=[_a_spec(False, tm, tk), _b_spec(False, tk, tn), _b_spec(False, tk, tn)],
        out_specs=[o_spec] * 3, out_shape=[o_shape] * 3,
        scratch_shapes=[pltpu.VMEM((tm, tn), F32)] * (2 * (nk > 1)),
        compiler_params=_params(("parallel", "parallel", "arbitrary")),
    )(n2, w_gate, w_up)


def _swiglu_bwd_act(dh2, w_down, gate, up, *, tm=1024, tn=512, tk=2048):
    m, kdim = dh2.shape
    n = w_down.shape[0]
    tm, tn, tk = _tile(m, tm), _tile(n, tn), _tile(kdim, tk)
    nk = kdim // tk

    def body(a_ref, b_ref, gate_ref, up_ref, dgate_ref, dup_ref, *acc):
        def finish(dact):
            g = gate_ref[...].astype(F32)
            u = up_ref[...].astype(F32)
            sg = _sigmoid(g)
            dup_ref[...] = (dact * g * sg).astype(BF16)
            dgate_ref[...] = (dact * u * sg * (1.0 + g * (1.0 - sg))).astype(BF16)

        _accumulate(acc, nk, lambda: (_dot(a_ref[...], b_ref[...], tb=True),), finish)

    o_spec = pl.BlockSpec((tm, tn), lambda i, j, k: (i, j))
    o_shape = jax.ShapeDtypeStruct((m, n), BF16)
    return pl.pallas_call(
        body, name="swiglu_bwd_act", grid=(m // tm, n // tn, nk),
        in_specs=[_a_spec(False, tm, tk), _b_spec(True, tk, tn), o_spec, o_spec],
        out_specs=[o_spec] * 2, out_shape=[o_shape] * 2,
        scratch_shapes=[pltpu.VMEM((tm, tn), F32)] * (nk > 1),
        compiler_params=_params(("parallel", "parallel", "arbitrary")),
    )(dh2, w_down, gate, up)


def _swiglu_bwd_in(dgate, dup, w_gate, w_up, *, tm=1024, tn=1024, tk=1408):
    m, kdim = dgate.shape
    n = w_gate.shape[0]
    tm, tn, tk = _tile(m, tm), _tile(n, tn), _tile(kdim, tk)
    nk = kdim // tk

    def body(a1_ref, a2_ref, b1_ref, b2_ref, o_ref, *acc):
        def product():
            return (_dot(a1_ref[...], b1_ref[...], tb=True) + _dot(a2_ref[...], b2_ref[...], tb=True),)

        def finish(total):
            o_ref[...] = total

        _accumulate(acc, nk, product, finish)

    a_spec, b_spec = _a_spec(False, tm, tk), _b_spec(True, tk, tn)
    return pl.pallas_call(
        body, name="swiglu_bwd_in", grid=(m // tm, n // tn, nk),
        in_specs=[a_spec, a_spec, b_spec, b_spec],
        out_specs=pl.BlockSpec((tm, tn), lambda i, j, k: (i, j)),
        out_shape=jax.ShapeDtypeStruct((m, n), F32),
        scratch_shapes=[pltpu.VMEM((tm, tn), F32)] * (nk > 1),
        compiler_params=_params(("parallel", "parallel", "arbitrary")),
    )(dgate, dup, w_gate, w_up)


def _row_block(rows, cols):
    tr = min(rows, max(16, ELEM_BLOCK_BYTES // (4 * cols) // 16 * 16))
    while rows % tr:
        tr -= 16
    return tr


def _rmsnorm_fwd(x, g, *, name):
    s, d = x.shape
    tr = _row_block(s, d)

    def body(x_ref, g_ref, n_ref):
        xv = x_ref[...]
        r = lax.rsqrt(jnp.mean(xv * xv, axis=-1, keepdims=True) + EPS)
        n_ref[...] = (xv * r * g_ref[...]).astype(BF16)

    row = pl.BlockSpec((tr, d), lambda i: (i, 0))
    return pl.pallas_call(
        body, name=name, grid=(s // tr,), in_specs=[row, pl.BlockSpec((1, d), lambda i: (0, 0))],
        out_specs=row, out_shape=jax.ShapeDtypeStruct((s, d), BF16),
        compiler_params=_params(("parallel",)),
    )(x, g)


def _rmsnorm_bwd_rows(xv, gv, dy):
    r = lax.rsqrt(jnp.mean(xv * xv, axis=-1, keepdims=True) + EPS)
    xhat = xv * r
    dxh = dy * gv
    dx = r * (dxh - xhat * jnp.mean(dxh * xhat, axis=-1, keepdims=True))
    return dx, dy * xhat


def _rmsnorm_bwd(dn, x, g, skip, *, name):
    s, d = x.shape
    tr = _row_block(s, d)

    def body(dn_ref, x_ref, g_ref, skip_ref, dx_ref, dxb_ref, dg_ref):
        dx, dgr = _rmsnorm_bwd_rows(x_ref[...], g_ref[...], dn_ref[...])
        dx = dx + skip_ref[...]
        dx_ref[...] = dx
        dxb_ref[...] = dx.astype(BF16)

        @pl.when(pl.program_id(0) == 0)
        def _():
            dg_ref[...] = jnp.zeros_like(dg_ref)

        dg_ref[...] += jnp.sum(dgr, axis=0, keepdims=True)

    row = pl.BlockSpec((tr, d), lambda i: (i, 0))
    vec = pl.BlockSpec((1, d), lambda i: (0, 0))
    return pl.pallas_call(
        body, name=name, grid=(s // tr,), in_specs=[row, row, vec, row],
        out_specs=[row, row, vec],
        out_shape=[jax.ShapeDtypeStruct((s, d), F32), jax.ShapeDtypeStruct((s, d), BF16),
                   jax.ShapeDtypeStruct((1, d), F32)],
        compiler_params=_params(("arbitrary",)),
    )(dn, x, g, skip)


def _loss_head(h2, g, target):
    s, d = h2.shape
    tr = _row_block(s, d)

    def body(h_ref, g_ref, t_ref, dh_ref, dhb_ref, dg_ref, loss_ref):
        hv = h_ref[...]
        gv = g_ref[...]
        r = lax.rsqrt(jnp.mean(hv * hv, axis=-1, keepdims=True) + EPS)
        err = hv * r * gv - t_ref[...]
        dx, dgr = _rmsnorm_bwd_rows(hv, gv, err * (1.0 / d))
        dh_ref[...] = dx
        dhb_ref[...] = dx.astype(BF16)

        @pl.when(pl.program_id(0) == 0)
        def _():
            dg_ref[...] = jnp.zeros_like(dg_ref)
            loss_ref[...] = jnp.zeros_like(loss_ref)

        dg_ref[...] += jnp.sum(dgr, axis=0, keepdims=True)
        row_loss = jnp.mean(err * err, axis=-1, keepdims=True)
        loss_ref[...] += 0.5 * jnp.sum(row_loss, axis=0, keepdims=True)

    row = pl.BlockSpec((tr, d), lambda i: (i, 0))
    vec = pl.BlockSpec((1, d), lambda i: (0, 0))
    one = pl.BlockSpec((1, 1), lambda i: (0, 0))
    return pl.pallas_call(
        body, name="loss_head", grid=(s // tr,), in_specs=[row, vec, row],
        out_specs=[row, row, vec, one],
        out_shape=[jax.ShapeDtypeStruct((s, d), F32), jax.ShapeDtypeStruct((s, d), BF16),
                   jax.ShapeDtypeStruct((1, d), F32), jax.ShapeDtypeStruct((1, 1), F32)],
        compiler_params=_params(("arbitrary",)),
    )(h2, g, target)


def _attention_bias_tables():
    k = np.arange(-ATT_KB, ATT_KB + 1)[:, None, None]
    delta = k * ATT_BLOCK + np.arange(ATT_BLOCK)[None, None, :] - np.arange(ATT_BLOCK)[None, :, None]
    dist = np.abs(delta)
    count = np.zeros(delta.shape, np.int32)
    for window, dilation in DILATED_PATTERNS:
        count += (delta % dilation == 0) & (dist <= window // 2)
    logc = np.where(count > 0, np.log(np.maximum(count, 1)), MASKED)
    return dist.astype(np.float32), logc.astype(np.float32)


def _head_bias(bias_ref, slope, dist_ref, logc_ref):
    for kk in range(ATT_WINDOW):
        bias_ref[kk] = logc_ref[kk] - slope * dist_ref[kk]
    bias_ref[ATT_WINDOW] = jnp.full((ATT_BLOCK, ATT_BLOCK), MASKED, F32)


def _window_start(i, nq, nwin):
    return jnp.clip(i - ATT_KB, 0, nq - nwin)


def _window_block(j, i):
    rows = pl.ds(pl.multiple_of(j * ATT_BLOCK, ATT_BLOCK), ATT_BLOCK)
    kk = j - i + ATT_KB
    return rows, jnp.where(jnp.logical_and(kk >= 0, kk < ATT_WINDOW), kk, ATT_WINDOW)


def _attention_fwd(proj, slopes, n_heads):
    s = proj.shape[0]
    nq = s // ATT_BLOCK
    scale = HEAD_DIM ** -0.5
    dist, logc = _attention_bias_tables()

    nwin = min(ATT_WINDOW, nq)

    def body(slope_ref, q_ref, k_ref, v_ref, dist_ref, logc_ref, o_ref, lse_ref, bias_ref, s_ref):
        h, i = pl.program_id(0), pl.program_id(1)

        @pl.when(i == 0)
        def _():
            _head_bias(bias_ref, slope_ref[h], dist_ref, logc_ref)

        q = q_ref[...]
        first = _window_start(i, nq, nwin)
        m = jnp.full((ATT_BLOCK, 1), ROW_MAX_INIT, F32)
        for b in range(nwin):
            rows, kk = _window_block(first + b, i)
            sc = _dot(q, k_ref[rows, :], tb=True) * scale + bias_ref[kk]
            s_ref[b] = sc
            m = jnp.maximum(m, jnp.max(sc, axis=-1, keepdims=True))
        l = jnp.zeros((ATT_BLOCK, 1), F32)
        acc = jnp.zeros((ATT_BLOCK, HEAD_DIM), F32)
        for b in range(nwin):
            rows, _ = _window_block(first + b, i)
            p = jnp.exp(s_ref[b] - m)
            l = l + jnp.sum(p, axis=-1, keepdims=True)
            acc = acc + _dot(p.astype(BF16), v_ref[rows, :])
        o_ref[...] = (acc / l).astype(BF16)
        lse_ref[...] = jnp.broadcast_to(m + jnp.log(l), (ATT_BLOCK, HEAD_DIM))

    hh = n_heads
    blk = pl.BlockSpec((ATT_BLOCK, HEAD_DIM), lambda h, i: (i, h))
    table = pl.BlockSpec(dist.shape, lambda h, i: (0, 0, 0))
    return pl.pallas_call(
        body, name="attention_fwd", grid=(hh, nq),
        in_specs=[pl.BlockSpec(memory_space=pltpu.SMEM), blk,
                  pl.BlockSpec((s, HEAD_DIM), lambda h, i: (0, hh + h)),
                  pl.BlockSpec((s, HEAD_DIM), lambda h, i: (0, 2 * hh + h)), table, table],
        out_specs=[blk, blk],
        out_shape=[jax.ShapeDtypeStruct((s, hh * HEAD_DIM), BF16), jax.ShapeDtypeStruct((s, hh * HEAD_DIM), F32)],
        scratch_shapes=[pltpu.VMEM((ATT_WINDOW + 1, ATT_BLOCK, ATT_BLOCK), F32),
                        pltpu.VMEM((nwin, ATT_BLOCK, ATT_BLOCK), F32)],
        compiler_params=_params(("parallel", "arbitrary")),
    )(slopes, proj, proj, proj, jnp.asarray(dist), jnp.asarray(logc))


def _attention_bwd(proj, slopes, out, lse, dmixed, n_heads):
    s = proj.shape[0]
    nq = s // ATT_BLOCK
    scale = HEAD_DIM ** -0.5
    dist, logc = _attention_bias_tables()

    nwin = min(ATT_WINDOW, nq)

    def body(slope_ref, q_ref, k_ref, v_ref, o_ref, do_ref, lse_ref, dist_ref, logc_ref,
             dq_ref, dk_ref, dv_ref, dk_acc, dv_acc, bias_ref):
        h, i = pl.program_id(0), pl.program_id(1)

        @pl.when(i == 0)
        def _():
            dk_acc[...] = jnp.zeros_like(dk_acc)
            dv_acc[...] = jnp.zeros_like(dv_acc)
            _head_bias(bias_ref, slope_ref[h], dist_ref, logc_ref)

        q = q_ref[...]
        do = do_ref[...]
        lse_col = lse_ref[:, :1]
        delta = jnp.sum(do.astype(F32) * o_ref[...].astype(F32), axis=-1, keepdims=True)
        first = _window_start(i, nq, nwin)
        dq = jnp.zeros((ATT_BLOCK, HEAD_DIM), F32)
        for b in range(nwin):
            rows, kk = _window_block(first + b, i)
            kj = k_ref[rows, :]
            vj = v_ref[rows, :]
            p = jnp.exp(_dot(q, kj, tb=True) * scale + bias_ref[kk] - lse_col)
            dv_acc[rows, :] += _dot(p.astype(BF16), do, ta=True)
            dp = _dot(do, vj, tb=True)
            ds = (p * (dp - delta) * scale).astype(BF16)
            dk_acc[rows, :] += _dot(ds, q, ta=True)
            dq = dq + _dot(ds, kj)
        dq_ref[...] = dq.astype(BF16)

        @pl.when(i == nq - 1)
        def _():
            dk_ref[...] = dk_acc[...].astype(BF16)
            dv_ref[...] = dv_acc[...].astype(BF16)

    hh = n_heads
    blk = pl.BlockSpec((ATT_BLOCK, HEAD_DIM), lambda h, i: (i, h))
    col = pl.BlockSpec((s, HEAD_DIM), lambda h, i: (0, h))
    table = pl.BlockSpec(dist.shape, lambda h, i: (0, 0, 0))
    o_shape = jax.ShapeDtypeStruct((s, hh * HEAD_DIM), BF16)
    return pl.pallas_call(
        body, name="attention_bwd", grid=(hh, nq),
        in_specs=[pl.BlockSpec(memory_space=pltpu.SMEM), blk,
                  pl.BlockSpec((s, HEAD_DIM), lambda h, i: (0, hh + h)),
                  pl.BlockSpec((s, HEAD_DIM), lambda h, i: (0, 2 * hh + h)),
                  blk, blk, blk, table, table],
        out_specs=[blk, col, col], out_shape=[o_shape] * 3,
        scratch_shapes=[pltpu.VMEM((s, HEAD_DIM), F32)] * 2
        + [pltpu.VMEM((ATT_WINDOW + 1, ATT_BLOCK, ATT_BLOCK), F32)],
        compiler_params=_params(("parallel", "arbitrary")),
    )(slopes, proj, proj, proj, out, dmixed, lse, jnp.asarray(dist), jnp.asarray(logc))


def _ret_decays(lgc, lga, strict_c, strict_a):
    c = RET_CHUNK
    rel = (lax.broadcasted_iota(jnp.int32, (c, c), 0) - lax.broadcasted_iota(jnp.int32, (c, c), 1)).astype(F32)
    in_c = (rel > 0) if strict_c else (rel >= 0)
    in_a = (rel < 0) if strict_a else (rel <= 0)
    mask = (jnp.where(in_c, jnp.exp(lgc * jnp.maximum(rel, 0.0)), 0.0)
            + jnp.where(in_a, jnp.exp(lga * jnp.maximum(-rel, 0.0)), 0.0))
    idx = lax.broadcasted_iota(jnp.int32, (c, 1), 0).astype(F32)
    ones = jnp.ones((1, HEAD_DIM), F32)
    dec = dict(
        rel=rel, mask=mask, idx=idx,
        a_c=jnp.exp(lgc * (idx + 1.0)), b_c=jnp.exp(lgc * (c - 1.0 - idx)), chunk_c=jnp.exp(ones * (lgc * c)),
        a_a=jnp.exp(lga * (c - idx)), b_a=jnp.exp(lga * idx), chunk_a=jnp.exp(ones * (lga * c)),
    )
    return dec


def _scaled(x, col):
    return (x.astype(F32) * col).astype(BF16)


def _chunk_rows(i):
    return pl.ds(pl.multiple_of(i * RET_CHUNK, RET_CHUNK), RET_CHUNK)


def _retention(a, b, c, lg_c, lg_a, *, strict_c, strict_a, scale, n_heads, name, gate=None, norm_w=None):
    s = a[0].shape[0]
    nc = s // RET_CHUNK
    epilogue = gate is not None

    def body(*refs):
        lgc_ref, lga_ref, a_ref, b_ref, c_ref = refs[:5]
        if epilogue:
            g_ref, w_ref, o_ref, mix_ref, sa_ref = refs[5:]
        else:
            o_ref, sa_ref = refs[5:]
        h = pl.program_id(0)
        dec = _ret_decays(lgc_ref[h], lga_ref[h], strict_c, strict_a)

        def reverse(t, state):
            i = nc - 1 - t
            sa_ref[i] = state.astype(BF16)
            rows = _chunk_rows(i)
            return state * dec["chunk_a"] + _dot(_scaled(b_ref[rows, :], dec["b_a"]), c_ref[rows, :], ta=True)

        lax.fori_loop(0, nc, reverse, jnp.zeros((HEAD_DIM, HEAD_DIM), F32))

        def forward(i, state):
            rows = _chunk_rows(i)
            ai, bi, ci = a_ref[rows, :], b_ref[rows, :], c_ref[rows, :]
            inner = (_dot(ai, bi, tb=True) * dec["mask"]).astype(BF16)
            out = (_dot(inner, ci) + _dot(_scaled(ai, dec["a_c"]), state.astype(BF16))
                   + _dot(_scaled(ai, dec["a_a"]), sa_ref[i])) * scale
            o_ref[rows, :] = out.astype(BF16)
            if epilogue:
                r = lax.rsqrt(jnp.mean(out * out, axis=-1, keepdims=True) + EPS)
                g = g_ref[rows, :].astype(F32)
                mix_ref[rows, :] = (out * r * w_ref[...] * (g * _sigmoid(g))).astype(BF16)
            return state * dec["chunk_c"] + _dot(_scaled(bi, dec["b_c"]), ci, ta=True)

        lax.fori_loop(0, nc, forward, jnp.zeros((HEAD_DIM, HEAD_DIM), F32))

    def col(first):
        return pl.BlockSpec((s, HEAD_DIM), lambda h: (0, first + h))

    smem = pl.BlockSpec(memory_space=pltpu.SMEM)
    in_specs = [smem, smem, col(a[1]), col(b[1]), col(c[1])]
    operands = [lg_c, lg_a, a[0], b[0], c[0]]
    o_shape = jax.ShapeDtypeStruct((s, n_heads * HEAD_DIM), BF16)
    out_specs, out_shape = [col(0)], [o_shape]
    if epilogue:
        in_specs += [col(gate[1]), pl.BlockSpec((1, HEAD_DIM), lambda h: (0, h))]
        operands += [gate[0], norm_w]
        out_specs, out_shape = [col(0)] * 2, [o_shape] * 2
    res = pl.pallas_call(
        body, name=name, grid=(n_heads,), in_specs=in_specs, out_specs=out_specs, out_shape=out_shape,
        scratch_shapes=[pltpu.VMEM((nc, HEAD_DIM, HEAD_DIM), BF16)],
        compiler_params=_params(("parallel",)),
    )(*operands)
    return res if epilogue else res[0]


def _retention_decay_grads(a, b, c, e, lg_c, lg_a, *, scale, n_heads):
    s = a[0].shape[0]
    nc = s // RET_CHUNK
    cf = float(RET_CHUNK)

    def body(lgc_ref, lga_ref, a_ref, b_ref, c_ref, e_ref, gc_ref, ga_ref, sa_ref, ta_ref):
        h = pl.program_id(0)
        lgc, lga = lgc_ref[h], lga_ref[h]
        dec = _ret_decays(lgc, lga, True, True)
        rel, idx = dec["rel"], dec["idx"]
        w_c = jnp.where(rel > 0, rel * jnp.exp(lgc * jnp.maximum(rel, 0.0)), 0.0)
        w_a = jnp.where(rel < 0, -rel * jnp.exp(lga * jnp.maximum(-rel, 0.0)), 0.0)
        zero = jnp.zeros((HEAD_DIM, HEAD_DIM), F32)

        def reverse(t, carry):
            st, dst = carry
            i = nc - 1 - t
            sa_ref[i] = st.astype(BF16)
            ta_ref[i] = dst.astype(BF16)
            rows = _chunk_rows(i)
            bi, ci = b_ref[rows, :], c_ref[rows, :]
            st_new = st * dec["chunk_a"] + _dot(_scaled(bi, dec["b_a"]), ci, ta=True)
            dst_new = (cf * st + dst) * dec["chunk_a"] + _dot(_scaled(bi, idx * dec["b_a"]), ci, ta=True)
            return st_new, dst_new

        lax.fori_loop(0, nc, reverse, (zero, zero))

        def forward(i, carry):
            st, dst, acc_c, acc_a = carry
            rows = _chunk_rows(i)
            ai, bi, ci = a_ref[rows, :], b_ref[rows, :], c_ref[rows, :]
            ev = e_ref[rows, :].astype(F32)
            pg = _dot(ai, bi, tb=True) * _dot(e_ref[rows, :], ci, tb=True)
            a_c, a_a = _scaled(ai, dec["a_c"]), _scaled(ai, dec["a_a"])
            inter_c = _dot(a_c, st.astype(BF16)) * (idx + 1.0) + _dot(a_c, dst.astype(BF16))
            inter_a = _dot(a_a, sa_ref[i]) * (cf - idx) + _dot(a_a, ta_ref[i])
            acc_c = acc_c + jnp.sum(pg * w_c, axis=0, keepdims=True) + jnp.sum(inter_c * ev, axis=0, keepdims=True)
            acc_a = acc_a + jnp.sum(pg * w_a, axis=0, keepdims=True) + jnp.sum(inter_a * ev, axis=0, keepdims=True)
            st_new = st * dec["chunk_c"] + _dot(_scaled(bi, dec["b_c"]), ci, ta=True)
            dst_new = ((cf * st + dst) * dec["chunk_c"]
                       + _dot(_scaled(bi, (cf - 1.0 - idx) * dec["b_c"]), ci, ta=True))
            return st_new, dst_new, acc_c, acc_a

        row = jnp.zeros((1, HEAD_DIM), F32)
        _, _, acc_c, acc_a = lax.fori_loop(0, nc, forward, (zero, zero, row, row))
        gc_ref[...] = jnp.broadcast_to(jnp.sum(acc_c, axis=-1, keepdims=True) * scale, gc_ref.shape)
        ga_ref[...] = jnp.broadcast_to(jnp.sum(acc_a, axis=-1, keepdims=True) * scale, ga_ref.shape)

    def col(first):
        return pl.BlockSpec((s, HEAD_DIM), lambda h: (0, first + h))

    smem = pl.BlockSpec(memory_space=pltpu.SMEM)
    o_spec = pl.BlockSpec((1, 8, HEAD_DIM), lambda h: (h, 0, 0))
    o_shape = jax.ShapeDtypeStruct((n_heads, 8, HEAD_DIM), F32)
    gc, ga = pl.pallas_call(
        body, name="retention_decay_grads", grid=(n_heads,),
        in_specs=[smem, smem, col(a[1]), col(b[1]), col(c[1]), col(e[1])],
        out_specs=[o_spec] * 2, out_shape=[o_shape] * 2,
        scratch_shapes=[pltpu.VMEM((nc, HEAD_DIM, HEAD_DIM), BF16)] * 2,
        compiler_params=_params(("parallel",)),
    )(lg_c, lg_a, a[0], b[0], c[0], e[0])
    return gc[:, 0, 0], ga[:, 0, 0]


def _ret_gate_bwd(dmixed, first_col, out, proj, gate_col, norm_w, n_heads):
    s = out.shape[0]
    tr = _row_block(s, 8 * HEAD_DIM)

    def body(dm_ref, o_ref, g_ref, w_ref, do_ref, dg_ref, dw_ref):
        dm = dm_ref[...].astype(F32)
        ov = o_ref[...].astype(F32)
        g = g_ref[...].astype(F32)
        w = w_ref[...]
        r = lax.rsqrt(jnp.mean(ov * ov, axis=-1, keepdims=True) + EPS)
        ohat = ov * r
        sg = _sigmoid(g)
        silu = g * sg
        dg_ref[...] = (dm * ohat * w * sg * (1.0 + g * (1.0 - sg))).astype(BF16)
        dohat = dm * w * silu
        do_ref[...] = (r * (dohat - ohat * jnp.mean(dohat * ohat, axis=-1, keepdims=True))).astype(BF16)

        @pl.when(pl.program_id(1) == 0)
        def _():
            dw_ref[...] = jnp.zeros_like(dw_ref)

        dw_ref[...] += jnp.sum(dm * ohat * silu, axis=0, keepdims=True)

    def blk(first):
        return pl.BlockSpec((tr, HEAD_DIM), lambda h, i: (i, first + h))

    vec = pl.BlockSpec((1, HEAD_DIM), lambda h, i: (0, h))
    o_shape = jax.ShapeDtypeStruct((s, n_heads * HEAD_DIM), BF16)
    return pl.pallas_call(
        body, name="ret_gate_bwd", grid=(n_heads, s // tr),
        in_specs=[blk(first_col), blk(0), blk(gate_col), vec],
        out_specs=[blk(0), blk(0), vec],
        out_shape=[o_shape, o_shape, jax.ShapeDtypeStruct((1, n_heads * HEAD_DIM), F32)],
        compiler_params=_params(("parallel", "arbitrary")),
    )(dmixed, out, proj, norm_w)


def _local_step(x, target, norm_mix_w, ret_decay_fwd, ret_decay_bwd, ret_norm_w, norm_ffn_w, norm_final_w,
                w_in, w_out, w_gate, w_up, w_down):
    d = x.shape[1]
    nh = d // (2 * HEAD_DIM)
    scale = HEAD_DIM ** -0.5
    slopes = jnp.exp2(-8.0 * jnp.arange(1, nh + 1, dtype=F32) / nh)
    lg_f = -jnp.exp(ret_decay_fwd)
    lg_b = -jnp.exp(ret_decay_bwd)
    q_r, k_r, v_r, g_r = 3 * nh, 4 * nh, 5 * nh, 6 * nh

    n1 = _rmsnorm_fwd(x, norm_mix_w, name="norm_mix_fwd")
    proj = _matmul(n1, w_in, name="in_proj", out_dtype=BF16)
    attn, lse = _attention_fwd(proj, slopes, nh)
    ret, ret_mixed = _retention((proj, q_r), (proj, k_r), (proj, v_r), lg_f, lg_b, strict_c=False, strict_a=True,
                                scale=scale, n_heads=nh, name="retention_fwd", gate=(proj, g_r), norm_w=ret_norm_w)
    mixed = jnp.concatenate([attn, ret_mixed], axis=1)
    h1 = _matmul(mixed, w_out, name="out_proj", residual=x)
    n2 = _rmsnorm_fwd(h1, norm_ffn_w, name="norm_ffn_fwd")
    gate, up, act = _swiglu_fwd(n2, w_gate, w_up)
    h2 = _matmul(act, w_down, name="down_proj", residual=h1)
    dh2, dh2_b, d_norm_final, loss = _loss_head(h2, norm_final_w, target)

    dgate, dup = _swiglu_bwd_act(dh2_b, w_down, gate, up)
    g_down = _matmul(act, dh2_b, name="grad_w_down", ta=True)
    g_gate = _matmul(n2, dgate, name="grad_w_gate", ta=True)
    g_up = _matmul(n2, dup, name="grad_w_up", ta=True)
    dn2 = _swiglu_bwd_in(dgate, dup, w_gate, w_up)
    dh1, dh1_b, d_norm_ffn = _rmsnorm_bwd(dn2, h1, norm_ffn_w, dh2, name="norm_ffn_bwd")

    dmixed = _matmul(dh1_b, w_out, name="out_proj_bwd", tb=True, out_dtype=BF16)
    g_out = _matmul(mixed, dh1_b, name="grad_w_out", ta=True)
    d_ret, dg_r, d_ret_norm = _ret_gate_bwd(dmixed, nh, ret, proj, g_r, ret_norm_w, nh)
    dq_r = _retention((d_ret, 0), (proj, v_r), (proj, k_r), lg_f, lg_b, strict_c=False, strict_a=True,
                      scale=scale, n_heads=nh, name="retention_dq")
    dv_r = _retention((proj, k_r), (proj, q_r), (d_ret, 0), lg_b, lg_f, strict_c=True, strict_a=False,
                      scale=scale, n_heads=nh, name="retention_dv")
    dk_r = _retention((proj, v_r), (d_ret, 0), (proj, q_r), lg_b, lg_f, strict_c=True, strict_a=False,
                      scale=scale, n_heads=nh, name="retention_dk")
    dlg_f, dlg_b = _retention_decay_grads((proj, q_r), (proj, k_r), (proj, v_r), (d_ret, 0), lg_f, lg_b,
                                          scale=scale, n_heads=nh)
    dq_a, dk_a, dv_a = _attention_bwd(proj, slopes, attn, lse, dmixed, nh)
    dproj = jnp.concatenate([dq_a, dk_a, dv_a, dq_r, dk_r, dv_r, dg_r], axis=1)
    dn1 = _matmul(dproj, w_in, name="in_proj_bwd", tb=True)
    g_in = _matmul(n1, dproj, name="grad_w_in", ta=True)
    dx, _, d_norm_mix = _rmsnorm_bwd(dn1, x, norm_mix_w, dh1, name="norm_mix_bwd")

    small = dict(loss=loss[0, 0], norm_mix_w=d_norm_mix, ret_decay_fwd=dlg_f * lg_f, ret_decay_bwd=dlg_b * lg_b,
                 ret_norm_w=d_ret_norm, norm_ffn_w=d_norm_ffn, norm_final_w=d_norm_final)
    return dx, dict(w_in=g_in, w_out=g_out, w_gate=g_gate, w_up=g_up, w_down=g_down), small


def _mesh_position():
    x, y, c = lax.axis_index("x"), lax.axis_index("y"), lax.axis_index("c")
    chips = [(1 - x, y), (x, 1 - y), (1 - x, 1 - y)]
    return x, y, c, chips


def _ds(start, size):
    if isinstance(start, int):
        return pl.ds(start, size)
    return pl.ds(pl.multiple_of(start * size, size), size)


def _region(ref, axis, shard, half, shard_size, half_size):
    along = slice(None) if shard is None else _ds(shard, shard_size)
    other = slice(None) if half is None else _ds(half, half_size)
    return ref.at[other, along] if axis == 1 else ref.at[along, other]


def _all_gather_weights(full, axes):
    n = len(full)

    def body(*refs):
        outs = refs[n:2 * n]
        ici_send, ici_recv, d2d_send, d2d_recv = refs[2 * n:]
        x, y, c, chips = _mesh_position()
        me = 2 * x + y

        def landing(w, shard, half):
            rows_cols = full[w].shape
            return _region(outs[w], axes[w], shard, half, rows_cols[axes[w]] // N_CHIPS, rows_cols[1 - axes[w]] // 2)

        def over_ici(w, k):
            return pltpu.make_async_remote_copy(
                src_ref=landing(w, me, c), dst_ref=landing(w, me, c),
                send_sem=ici_send.at[w, k], recv_sem=ici_recv.at[w, k],
                device_id=(chips[k][0], chips[k][1], c), device_id_type=MESH)

        def arrived_over_ici(w, k):
            shard = 2 * chips[k][0] + chips[k][1]
            return pltpu.make_async_remote_copy(
                src_ref=landing(w, shard, c), dst_ref=landing(w, shard, c),
                send_sem=d2d_send.at[w, k], recv_sem=ici_recv.at[w, k],
                device_id=(x, y, 1 - c), device_id_type=MESH)

        def to_sibling(w, k):
            shard = 2 * chips[k][0] + chips[k][1]
            return pltpu.make_async_remote_copy(
                src_ref=landing(w, shard, c), dst_ref=landing(w, shard, c),
                send_sem=d2d_send.at[w, k], recv_sem=d2d_recv.at[w, k],
                device_id=(x, y, 1 - c), device_id_type=MESH)

        def arrived_from_sibling(w, k):
            shard = 2 * chips[k][0] + chips[k][1]
            return pltpu.make_async_remote_copy(
                src_ref=landing(w, shard, 1 - c), dst_ref=landing(w, shard, 1 - c),
                send_sem=d2d_send.at[w, k], recv_sem=d2d_recv.at[w, k],
                device_id=(x, y, 1 - c), device_id_type=MESH)

        for w in range(n):
            for k in range(3):
                over_ici(w, k).start()
        for w in range(n):
            for k in range(3):
                arrived_over_ici(w, k).wait_recv()
                to_sibling(w, k).start()
        for w in range(n):
            for k in range(3):
                arrived_from_sibling(w, k).wait_recv()
        for w in range(n):
            for k in range(3):
                over_ici(w, k).wait_send()
                to_sibling(w, k).wait_send()

    hbm = pl.BlockSpec(memory_space=pl.ANY)
    return pl.pallas_call(
        body, name="all_gather_weights", in_specs=[hbm] * n, out_specs=[hbm] * n,
        out_shape=[jax.ShapeDtypeStruct(w.shape, w.dtype) for w in full],
        input_output_aliases={w: w for w in range(n)},
        scratch_shapes=[pltpu.SemaphoreType.DMA((n, 3))] * 4,
    )(*full)


def _exchange_halves(grads, axes):
    n = len(grads)

    def half_shape(w):
        return tuple(d // 2 if a != axes[w] else d for a, d in enumerate(grads[w].shape))

    def body(*refs):
        ins, outs = refs[:n], refs[n:2 * n]
        send_sem, recv_sem = refs[2 * n:]
        x, y, c, _ = _mesh_position()

        def copy(w):
            return pltpu.make_async_remote_copy(
                src_ref=_region(ins[w], axes[w], None, 1 - c, 0, half_shape(w)[1 - axes[w]]), dst_ref=outs[w],
                send_sem=send_sem.at[w], recv_sem=recv_sem.at[w], device_id=(x, y, 1 - c), device_id_type=MESH)

        for w in range(n):
            copy(w).start()
        for w in range(n):
            copy(w).wait()

    hbm = pl.BlockSpec(memory_space=pl.ANY)
    return pl.pallas_call(
        body, name="grad_exchange_halves", in_specs=[hbm] * n, out_specs=[hbm] * n,
        out_shape=[jax.ShapeDtypeStruct(half_shape(w), F32) for w in range(n)],
        scratch_shapes=[pltpu.SemaphoreType.DMA((n,))] * 2,
    )(*grads)


def _half_block_spec(axis, block, half_blocks, use_half):
    if axis == 1:
        if use_half:
            return pl.BlockSpec(block, lambda i, pos: (pos[0] * half_blocks + i, 0))
        return pl.BlockSpec(block, lambda i, pos: (i, 0))
    if use_half:
        return pl.BlockSpec(block, lambda i, pos: (i, pos[0]))
    return pl.BlockSpec(block, lambda i, pos: (i, 0))


def _add_halves(grad, received, axis, pos, *, name):
    rows, cols = received.shape
    tr = _row_block(rows, cols)
    nb = rows // tr

    def body(pos_ref, g_ref, r_ref, o_ref):
        o_ref[...] = (g_ref[...] + r_ref[...]).astype(BF16)

    blk = (tr, cols)
    return pl.pallas_call(
        body, name=name, out_shape=jax.ShapeDtypeStruct((rows, cols), BF16),
        grid_spec=pltpu.PrefetchScalarGridSpec(
            num_scalar_prefetch=1, grid=(nb,),
            in_specs=[_half_block_spec(axis, blk, nb, True), _half_block_spec(axis, blk, nb, False)],
            out_specs=_half_block_spec(axis, blk, nb, False)),
        compiler_params=_params(("parallel",)),
    )(pos, grad, received)


def _send_chip_sums(sums, axes):
    n = len(sums)

    def part_shape(w):
        return tuple(d // N_CHIPS if a == axes[w] else d for a, d in enumerate(sums[w].shape))

    def body(*refs):
        ins, outs = refs[:n], refs[n:2 * n]
        send_sem, recv_sem = refs[2 * n:]
        x, y, c, chips = _mesh_position()

        def copy(w, k):
            shard = 2 * chips[k][0] + chips[k][1]
            return pltpu.make_async_remote_copy(
                src_ref=_region(ins[w], axes[w], shard, None, part_shape(w)[axes[w]], 0), dst_ref=outs[w].at[k],
                send_sem=send_sem.at[w, k], recv_sem=recv_sem.at[w, k],
                device_id=(chips[k][0], chips[k][1], c), device_id_type=MESH)

        for w in range(n):
            for k in range(3):
                copy(w, k).start()
        for w in range(n):
            for k in range(3):
                copy(w, k).wait()

    hbm = pl.BlockSpec(memory_space=pl.ANY)
    return pl.pallas_call(
        body, name="grad_send_chip_sums", in_specs=[hbm] * n, out_specs=[hbm] * n,
        out_shape=[jax.ShapeDtypeStruct((3,) + part_shape(w), BF16) for w in range(n)],
        scratch_shapes=[pltpu.SemaphoreType.DMA((n, 3))] * 2,
    )(*sums)


def _sum_chip_parts(grad, received, parts, axis, pos, *, name):
    _, rows, cols = parts.shape
    tr = _row_block(rows, cols)
    nb = rows // tr
    blk = (tr, cols)

    def body(pos_ref, g_ref, r_ref, p_ref, o_ref):
        total = g_ref[...] + r_ref[...]
        for k in range(3):
            total = total + p_ref[k].astype(F32)
        o_ref[...] = total

    if axis == 1:
        g_spec = pl.BlockSpec(blk, lambda i, pos: (pos[0] * nb + i, pos[1]))
        r_spec = pl.BlockSpec(blk, lambda i, pos: (i, pos[1]))
        o_spec = pl.BlockSpec(blk, lambda i, pos: (pos[0] * nb + i, 0))
        shard_shape = (2 * rows, cols)
    else:
        g_spec = pl.BlockSpec(blk, lambda i, pos: (pos[1] * nb + i, pos[0]))
        r_spec = pl.BlockSpec(blk, lambda i, pos: (pos[1] * nb + i, 0))
        o_spec = pl.BlockSpec(blk, lambda i, pos: (i, pos[0]))
        shard_shape = (rows, 2 * cols)
    return pl.pallas_call(
        body, name=name, out_shape=jax.ShapeDtypeStruct(shard_shape, F32),
        grid_spec=pltpu.PrefetchScalarGridSpec(
            num_scalar_prefetch=1, grid=(nb,),
            in_specs=[g_spec, r_spec, pl.BlockSpec((3,) + blk, lambda i, pos: (0, i, 0))],
            out_specs=o_spec),
        compiler_params=_params(("parallel",)),
    )(pos, grad, received, parts)


def _join_halves(shards, axes):
    n = len(shards)

    def body(*refs):
        outs = refs[n:2 * n]
        send_sem, recv_sem = refs[2 * n:]
        x, y, c, _ = _mesh_position()

        def copy(w, half):
            place = _region(outs[w], axes[w], None, half, 0, shards[w].shape[1 - axes[w]] // 2)
            return pltpu.make_async_remote_copy(
                src_ref=place, dst_ref=place, send_sem=send_sem.at[w], recv_sem=recv_sem.at[w],
                device_id=(x, y, 1 - c), device_id_type=MESH)

        for w in range(n):
            copy(w, c).start()
        for w in range(n):
            copy(w, 1 - c).wait_recv()
            copy(w, c).wait_send()

    hbm = pl.BlockSpec(memory_space=pl.ANY)
    return pl.pallas_call(
        body, name="grad_join_halves", in_specs=[hbm] * n, out_specs=[hbm] * n,
        out_shape=[jax.ShapeDtypeStruct(w.shape, w.dtype) for w in shards],
        input_output_aliases={w: w for w in range(n)},
        scratch_shapes=[pltpu.SemaphoreType.DMA((n,))] * 2,
    )(*shards)


def _all_reduce_small(vec):
    rows, cols = vec.shape

    def body(v_ref, o_ref, land_ref, send_sem, recv_sem):
        x, y, c, _ = _mesh_position()
        me = 4 * x + 2 * y + c
        land_ref[me] = v_ref[...]
        copies = []
        for k in range(1, 8):
            px, py, pc = x ^ (k >> 2), y ^ ((k >> 1) & 1), c ^ (k & 1)
            copies.append(pltpu.make_async_remote_copy(
                src_ref=v_ref, dst_ref=land_ref.at[me], send_sem=send_sem.at[k], recv_sem=recv_sem.at[k],
                device_id=(px, py, pc), device_id_type=MESH))
        for cp in copies:
            cp.start()
        for k in range(1, 8):
            peer = me ^ k
            pltpu.make_async_remote_copy(
                src_ref=v_ref, dst_ref=land_ref.at[peer], send_sem=send_sem.at[k], recv_sem=recv_sem.at[k],
                device_id=(x, y, c), device_id_type=MESH).wait_recv()
        for cp in copies:
            cp.wait_send()
        total = land_ref[0]
        for k in range(1, 8):
            total = total + land_ref[k]
        o_ref[...] = total

    vmem = pl.BlockSpec(memory_space=pltpu.VMEM)
    return pl.pallas_call(
        body, name="all_reduce_small", in_specs=[vmem], out_specs=vmem,
        out_shape=jax.ShapeDtypeStruct((rows, cols), F32),
        scratch_shapes=[pltpu.VMEM((8, rows, cols), F32), pltpu.SemaphoreType.DMA((8,)), pltpu.SemaphoreType.DMA((8,))],
    )(vec)


def _adamw(w, g, m, v, *, name):
    rows, cols = w.shape
    tr = _row_block(rows, cols) if rows % 8 == 0 else rows
    bc1 = 1.0 - ADAM_B1 ** ADAM_STEP
    bc2 = 1.0 - ADAM_B2 ** ADAM_STEP

    def body(w_ref, g_ref, m_ref, v_ref, d_ref, mo_ref, vo_ref):
        gv = g_ref[...]
        mn = ADAM_B1 * m_ref[...] + (1.0 - ADAM_B1) * gv
        vn = ADAM_B2 * v_ref[...] + (1.0 - ADAM_B2) * (gv * gv)
        mo_ref[...] = mn
        vo_ref[...] = vn
        d_ref[...] = -ADAM_LR * ((mn / bc1) / (jnp.sqrt(vn / bc2) + ADAM_EPS) + ADAM_WD * w_ref[...])

    blk = pl.BlockSpec((tr, cols), lambda i: (i, 0))
    shape = jax.ShapeDtypeStruct((rows, cols), F32)
    return pl.pallas_call(
        body, name=name, grid=(rows // tr,), in_specs=[blk] * 4, out_specs=[blk] * 3, out_shape=[shape] * 3,
        compiler_params=_params(("parallel",)),
    )(w, g, m, v)


def _to_bf16_in_place(w, axis, pos, *, name):
    rows, cols = w.shape
    tr = _row_block(rows, cols)
    nb = rows // tr

    def body(pos_ref, w_ref, o_ref):
        o_ref[...] = w_ref[...].astype(BF16)

    if axis == 1:
        o_spec = pl.BlockSpec((tr, cols), lambda i, pos: (i, pos[1]))
        full_shape = (rows, N_CHIPS * cols)
    else:
        o_spec = pl.BlockSpec((tr, cols), lambda i, pos: (pos[1] * nb + i, 0))
        full_shape = (N_CHIPS * rows, cols)
    return pl.pallas_call(
        body, name=name, out_shape=jax.ShapeDtypeStruct(full_shape, BF16),
        grid_spec=pltpu.PrefetchScalarGridSpec(
            num_scalar_prefetch=1, grid=(nb,),
            in_specs=[pl.BlockSpec((tr, cols), lambda i, pos: (i, 0))], out_specs=o_spec),
        compiler_params=_params(("parallel",)),
    )(pos, w)


BIG = ("w_in", "w_out", "w_gate", "w_up", "w_down")
BIG_AXIS = dict(w_in=1, w_out=0, w_gate=1, w_up=1, w_down=0)
SMALL = ("norm_mix_w", "ret_decay_fwd", "ret_decay_bwd", "ret_norm_w", "norm_ffn_w", "norm_final_w")
ALL_WEIGHTS = ("norm_mix_w", "w_in", "ret_decay_fwd", "ret_decay_bwd", "ret_norm_w", "w_out", "norm_ffn_w",
               "w_gate", "w_up", "w_down", "norm_final_w")
SMALL_ROW = 128 * 8


def _pack_small(small):
    pieces = [jnp.reshape(small["loss"], (1,))] + [jnp.reshape(small[k], (-1,)) for k in SMALL]
    rows = []
    for p in pieces:
        pad = -p.shape[0] % (8 * SMALL_ROW)
        rows.append(jnp.reshape(jnp.pad(p, (0, pad)), (-1, SMALL_ROW)))
    return jnp.concatenate(rows, axis=0)


def _unpack_small(block, like):
    out, row = {}, 0
    for k in ("loss",) + SMALL:
        size = 1 if k == "loss" else like[k].size
        nrows = -(-size // (8 * SMALL_ROW)) * 8
        out[k] = jnp.reshape(block[row:row + nrows], (-1,))[:size]
        row += nrows
    return out


def kernel(x, norm_mix_w, w_in, ret_decay_fwd, ret_decay_bwd, ret_norm_w, w_out, norm_ffn_w, w_gate, w_up, w_down, norm_final_w, loss_target, m_norm_mix_w, m_w_in, m_ret_decay_fwd, m_ret_decay_bwd, m_ret_norm_w, m_w_out, m_norm_ffn_w, m_w_gate, m_w_up, m_w_down, m_norm_final_w, v_norm_mix_w, v_w_in, v_ret_decay_fwd, v_ret_decay_bwd, v_ret_norm_w, v_w_out, v_norm_ffn_w, v_w_gate, v_w_up, v_w_down, v_norm_final_w):
    weights = dict(norm_mix_w=norm_mix_w, w_in=w_in, ret_decay_fwd=ret_decay_fwd, ret_decay_bwd=ret_decay_bwd,
                   ret_norm_w=ret_norm_w, w_out=w_out, norm_ffn_w=norm_ffn_w, w_gate=w_gate, w_up=w_up,
                   w_down=w_down, norm_final_w=norm_final_w)
    m_in = dict(norm_mix_w=m_norm_mix_w, w_in=m_w_in, ret_decay_fwd=m_ret_decay_fwd, ret_decay_bwd=m_ret_decay_bwd,
                ret_norm_w=m_ret_norm_w, w_out=m_w_out, norm_ffn_w=m_norm_ffn_w, w_gate=m_w_gate, w_up=m_w_up,
                w_down=m_w_down, norm_final_w=m_norm_final_w)
    v_in = dict(norm_mix_w=v_norm_mix_w, w_in=v_w_in, ret_decay_fwd=v_ret_decay_fwd, ret_decay_bwd=v_ret_decay_bwd,
                ret_norm_w=v_ret_norm_w, w_out=v_w_out, norm_ffn_w=v_norm_ffn_w, w_gate=v_w_gate, w_up=v_w_up,
                w_down=v_w_down, norm_final_w=v_norm_final_w)
    axes = [BIG_AXIS[k] for k in BIG]
    pos = jnp.stack([lax.axis_index("c"), 2 * lax.axis_index("x") + lax.axis_index("y")]).astype(jnp.int32)

    own = [_to_bf16_in_place(weights[k][0], BIG_AXIS[k], pos, name="cast_" + k) for k in BIG]
    full = dict(zip(BIG, _all_gather_weights(own, axes)))

    dx, grads, small = _local_step(
        x[0], loss_target[0], norm_mix_w, ret_decay_fwd[0], ret_decay_bwd[0], ret_norm_w, norm_ffn_w,
        norm_final_w[None, :], full["w_in"], full["w_out"], full["w_gate"], full["w_up"], full["w_down"])

    g_full = [grads[k] for k in BIG]
    received = _exchange_halves(g_full, axes)
    sums = [_add_halves(g, r, ax, pos, name="grad_add_halves_" + k) for k, g, r, ax in zip(BIG, g_full, received, axes)]
    parts = _send_chip_sums(sums, axes)
    halves = [_sum_chip_parts(g, r, p, ax, pos, name="grad_sum_parts_" + k)
              for k, g, r, p, ax in zip(BIG, g_full, received, parts, axes)]
    grad_w = dict(zip(BIG, _join_halves(halves, axes)))

    like = {k: weights[k] for k in SMALL}
    reduced = _unpack_small(_all_reduce_small(_pack_small(small)), like)
    loss = reduced["loss"][0]
    for k in SMALL:
        grad_w[k] = jnp.reshape(reduced[k], (1, -1))

    delta, new_m, new_v = {}, {}, {}
    for k in ALL_WEIGHTS:
        shape = weights[k].shape
        as2d = (lambda t: jnp.reshape(t, (-1, shape[-1])))
        delta[k], new_m[k], new_v[k] = (jnp.reshape(t, shape) for t in _adamw(
            as2d(weights[k]), as2d(grad_w[k]), as2d(m_in[k]), as2d(v_in[k]), name="adamw_" + k))
        grad_w[k] = jnp.reshape(grad_w[k], shape)

    return (loss, dx[None], *[grad_w[k] for k in ALL_WEIGHTS], *[delta[k] for k in ALL_WEIGHTS],
            *[new_m[k] for k in ALL_WEIGHTS], *[new_v[k] for k in ALL_WEIGHTS])
```

```python
import functools
import math

import numpy as np
import jax
import jax.numpy as jnp
from jax import lax
from jax.experimental import pallas as pl
from jax.experimental.pallas import tpu as pltpu

F32 = jnp.float32
BF16 = jnp.bfloat16
MESH = pl.DeviceIdType.MESH

HEAD_DIM = 128
RET_CHUNK = 128
EPS = 1e-6
DILATED_PATTERNS = ((128, 1), (512, 4), (2048, 16))
ATT_BLOCK = 256
ATT_REACH = max(w // 2 for w, _ in DILATED_PATTERNS)
ATT_KB = -(-ATT_REACH // ATT_BLOCK)
ATT_WINDOW = 2 * ATT_KB + 1
MASKED = -1e30
ROW_MAX_INIT = -1e29
N_CHIPS = 4
VMEM_LIMIT_BYTES = 56 * 1024 * 1024
ELEM_BLOCK_BYTES = 2 * 1024 * 1024

ADAM_LR = 0.001
ADAM_B1 = 0.9
ADAM_B2 = 0.999
ADAM_EPS = 1e-08
ADAM_WD = 0.01
ADAM_STEP = 10


def _params(sem=None):
    return pltpu.CompilerParams(dimension_semantics=sem, vmem_limit_bytes=VMEM_LIMIT_BYTES)


def _sigmoid(x):
    return 1.0 / (1.0 + jnp.exp(-x))


class _Job:
    def __init__(self, *, ins=(), ios=(), outs=(), sems=(), start, finish):
        self.ins, self.ios, self.outs, self.sems = list(ins), list(ios), list(outs), list(sems)
        self.start, self.finish = start, finish

    def results(self):
        return [jax.ShapeDtypeStruct(a.shape, a.dtype) for a in self.ios] + self.outs


def _call(body, *, name, grid, in_specs, out_specs, out_shape, operands, scratch_shapes=(), semantics=None, jobs=()):
    in_specs, out_specs, out_shape = list(in_specs), list(out_specs), list(out_shape)
    scratch_shapes = list(scratch_shapes)
    if not jobs:
        outs = pl.pallas_call(body, name=name, grid=grid, in_specs=in_specs, out_specs=out_specs, out_shape=out_shape,
                              scratch_shapes=scratch_shapes, compiler_params=_params(semantics))(*operands)
        return outs, []
    n_in, n_out, n_scratch = len(in_specs), len(out_specs), len(scratch_shapes)
    extra_in, extra_out, sems, aliases = [], [], [], {}
    for job in jobs:
        extra_in += job.ins
        for t in range(len(job.ios)):
            aliases[n_in + len(extra_in) + t] = n_out + len(extra_out) + t
        extra_in += job.ios
        extra_out += job.results()
        sems += job.sems

    def carried(*refs):
        x_in = refs[n_in:n_in + len(extra_in)]
        x_out = refs[n_in + len(extra_in) + n_out:n_in + len(extra_in) + n_out + len(extra_out)]
        x_sem = refs[len(refs) - len(sems):]
        views, i_in, i_out, i_sem = [], 0, 0, 0
        for job in jobs:
            data = list(x_in[i_in:i_in + len(job.ins)]) + list(x_out[i_out:i_out + len(job.results())])
            views.append((data, x_sem[i_sem:i_sem + len(job.sems)]))
            i_in += len(job.ins) + len(job.ios)
            i_out += len(job.results())
            i_sem += len(job.sems)
        steps = [pl.program_id(d) for d in range(len(grid))]

        @pl.when(functools.reduce(jnp.logical_and, [s == 0 for s in steps]))
        def _():
            for job, (data, sem) in zip(jobs, views):
                job.start(data, sem)

        body(*refs[:n_in], *refs[n_in + len(extra_in):n_in + len(extra_in) + n_out],
             *refs[len(refs) - len(sems) - n_scratch:len(refs) - len(sems)])

        @pl.when(functools.reduce(jnp.logical_and, [s == g - 1 for s, g in zip(steps, grid)]))
        def _():
            for job, (data, sem) in zip(jobs, views):
                job.finish(data, sem)

    hbm = pl.BlockSpec(memory_space=pl.ANY)
    res = pl.pallas_call(
        carried, name=name, grid=grid, in_specs=in_specs + [hbm] * len(extra_in),
        out_specs=out_specs + [hbm] * len(extra_out), out_shape=out_shape + extra_out,
        input_output_aliases=aliases, scratch_shapes=scratch_shapes + sems,
        compiler_params=_params(("arbitrary",) * len(grid)),
    )(*operands, *extra_in)
    carried_results, at = [], n_out
    for job in jobs:
        carried_results.append(list(res[at:at + len(job.results())]))
        at += len(job.results())
    return list(res[:n_out]), carried_results


def _run_jobs(jobs, *, name):
    first = jobs[0]
    n_in, n_io = len(first.ins), len(first.ios)
    out_shape = first.results()
    n_sems = [len(job.sems) for job in jobs]

    def body(*refs):
        data = list(refs[:n_in]) + list(refs[n_in + n_io:n_in + n_io + len(out_shape)])
        at = n_in + n_io + len(out_shape)
        for job, ns in zip(jobs, n_sems):
            job.start(data, refs[at:at + ns])
            job.finish(data, refs[at:at + ns])
            at += ns

    hbm = pl.BlockSpec(memory_space=pl.ANY)
    return pl.pallas_call(
        body, name=name, in_specs=[hbm] * (n_in + n_io), out_specs=[hbm] * len(out_shape), out_shape=out_shape,
        input_output_aliases={n_in + t: t for t in range(n_io)},
        scratch_shapes=[s for job in jobs for s in job.sems],
    )(*first.ins, *first.ios)


def _dot(a, b, ta=False, tb=False):
    return lax.dot_general(a, b, (((0 if ta else 1,), (1 if tb else 0,)), ((), ())),
                           preferred_element_type=F32)


def _tile(n, want):
    t = min(n, want) // 128 * 128
    while n % t:
        t -= 128
    return t


def _a_spec(ta, tm, tk):
    return pl.BlockSpec((tk, tm), lambda i, j, k: (k, i)) if ta else pl.BlockSpec((tm, tk), lambda i, j, k: (i, k))


def _b_spec(tb, tk, tn):
    return pl.BlockSpec((tn, tk), lambda i, j, k: (j, k)) if tb else pl.BlockSpec((tk, tn), lambda i, j, k: (k, j))


def _accumulate(accs, nk, products, finish):
    if nk == 1:
        finish(*products())
        return
    k = pl.program_id(2)

    @pl.when(k == 0)
    def _():
        for acc, p in zip(accs, products()):
            acc[...] = p

    if nk > 2:
        @pl.when(jnp.logical_and(k > 0, k < nk - 1))
        def _():
            for acc, p in zip(accs, products()):
                acc[...] += p

    @pl.when(k == nk - 1)
    def _():
        finish(*[acc[...] + p for acc, p in zip(accs, products())])


def _matmul(a, b, *, name, ta=False, tb=False, out_dtype=F32, residual=None, tm=1024, tn=1024, tk=2048, jobs=()):
    m, kdim = (a.shape[1], a.shape[0]) if ta else a.shape
    n = b.shape[0] if tb else b.shape[1]
    tm, tn, tk = _tile(m, tm), _tile(n, tn), _tile(kdim, tk)
    nk = kdim // tk

    def body(*refs):
        a_ref, b_ref = refs[:2]
        r_ref = refs[2] if residual is not None else None
        o_ref = refs[-1] if nk == 1 else refs[-2]

        def finish(total):
            if residual is not None:
                total = total + r_ref[...]
            o_ref[...] = total.astype(out_dtype)

        _accumulate(refs[-1:] if nk > 1 else (), nk, lambda: (_dot(a_ref[...], b_ref[...], ta, tb),), finish)

    o_spec = pl.BlockSpec((tm, tn), lambda i, j, k: (i, j))
    in_specs = [_a_spec(ta, tm, tk), _b_spec(tb, tk, tn)]
    operands = [a, b]
    if residual is not None:
        in_specs.append(o_spec)
        operands.append(residual)
    (out,), carried = _call(
        body, name=name, grid=(m // tm, n // tn, nk), in_specs=in_specs, out_specs=[o_spec],
        out_shape=[jax.ShapeDtypeStruct((m, n), out_dtype)], operands=operands,
        scratch_shapes=[pltpu.VMEM((tm, tn), F32)] * (nk > 1),
        semantics=("parallel", "parallel", "arbitrary"), jobs=jobs)
    return (out, carried) if jobs else out


def _swiglu_fwd(n2, w_gate, w_up, *, tm=1024, tn=512, tk=2048):
    m, kdim = n2.shape
    n = w_gate.shape[1]
    tm, tn, tk = _tile(m, tm), _tile(n, tn), _tile(kdim, tk)
    nk = kdim // tk

    def body(a_ref, g_ref, u_ref, gate_ref, up_ref, act_ref, *acc):
        def products():
            a = a_ref[...]
            return _dot(a, g_ref[...]), _dot(a, u_ref[...])

        def finish(g, u):
            gate_ref[...] = g.astype(BF16)
            up_ref[...] = u.astype(BF16)
            act_ref[...] = (g * _sigmoid(g) * u).astype(BF16)

        _accumulate(acc, nk, products, finish)

    o_spec = pl.BlockSpec((tm, tn), lambda i, j, k: (i, j))
    o_shape = jax.ShapeDtypeStruct((m, n), BF16)
    return pl.pallas_call(
        body, name="swiglu_fwd", grid=(m // tm, n // tn, nk),
        in_specs=[_a_spec(False, tm, tk), _b_spec(False, tk, tn), _b_spec(False, tk, tn)],
        out_specs=[o_spec] * 3, out_shape=[o_shape] * 3,
        scratch_shapes=[pltpu.VMEM((tm, tn), F32)] * (2 * (nk > 1)),
        compiler_params=_params(("parallel", "parallel", "arbitrary")),
    )(n2, w_gate, w_up)


def _swiglu_bwd_act(dh2, w_down, gate, up, *, tm=1024, tn=512, tk=2048):
    m, kdim = dh2.shape
    n = w_down.shape[0]
    tm, tn, tk = _tile(m, tm), _tile(n, tn), _tile(kdim, tk)
    nk = kdim // tk

    def body(a_ref, b_ref, gate_ref, up_ref, dgate_ref, dup_ref, *acc):
        def finish(dact):
            g = gate_ref[...].astype(F32)
            u = up_ref[...].astype(F32)
            sg = _sigmoid(g)
            dup_ref[...] = (dact * g * sg).astype(BF16)
            dgate_ref[...] = (dact * u * sg * (1.0 + g * (1.0 - sg))).astype(BF16)

        _accumulate(acc, nk, lambda: (_dot(a_ref[...], b_ref[...], tb=True),), finish)

    o_spec = pl.BlockSpec((tm, tn), lambda i, j, k: (i, j))
    o_shape = jax.ShapeDtypeStruct((m, n), BF16)
    return pl.pallas_call(
        body, name="swiglu_bwd_act", grid=(m // tm, n // tn, nk),
        in_specs=[_a_spec(False, tm, tk), _b_spec(True, tk, tn), o_spec, o_spec],
        out_specs=[o_spec] * 2, out_shape=[o_shape] * 2,
        scratch_shapes=[pltpu.VMEM((tm, tn), F32)] * (nk > 1),
        compiler_params=_params(("parallel", "parallel", "arbitrary")),
    )(dh2, w_down, gate, up)


def _swiglu_bwd_in(dgate, dup, w_gate, w_up, *, tm=1024, tn=1024, tk=1408, jobs=()):
    m, kdim = dgate.shape
    n = w_gate.shape[0]
    tm, tn, tk = _tile(m, tm), _tile(n, tn), _tile(kdim, tk)
    nk = kdim // tk

    def body(a1_ref, a2_ref, b1_ref, b2_ref, o_ref, *acc):
        def product():
            return (_dot(a1_ref[...], b1_ref[...], tb=True) + _dot(a2_ref[...], b2_ref[...], tb=True),)

        def finish(total):
            o_ref[...] = total

        _accumulate(acc, nk, product, finish)

    a_spec, b_spec = _a_spec(False, tm, tk), _b_spec(True, tk, tn)
    (out,), carried = _call(
        body, name="swiglu_bwd_in", grid=(m // tm, n // tn, nk),
        in_specs=[a_spec, a_spec, b_spec, b_spec],
        out_specs=[pl.BlockSpec((tm, tn), lambda i, j, k: (i, j))],
        out_shape=[jax.ShapeDtypeStruct((m, n), F32)], operands=[dgate, dup, w_gate, w_up],
        scratch_shapes=[pltpu.VMEM((tm, tn), F32)] * (nk > 1),
        semantics=("parallel", "parallel", "arbitrary"), jobs=jobs)
    return out, carried


def _row_block(rows, cols):
    tr = min(rows, max(16, ELEM_BLOCK_BYTES // (4 * cols) // 16 * 16))
    while rows % tr:
        tr -= 16
    return tr


def _rmsnorm_fwd(x, g, *, name):
    s, d = x.shape
    tr = _row_block(s, d)

    def body(x_ref, g_ref, n_ref):
        xv = x_ref[...]
        r = lax.rsqrt(jnp.mean(xv * xv, axis=-1, keepdims=True) + EPS)
        n_ref[...] = (xv * r * g_ref[...]).astype(BF16)

    row = pl.BlockSpec((tr, d), lambda i: (i, 0))
    return pl.pallas_call(
        body, name=name, grid=(s // tr,), in_specs=[row, pl.BlockSpec((1, d), lambda i: (0, 0))],
        out_specs=row, out_shape=jax.ShapeDtypeStruct((s, d), BF16),
        compiler_params=_params(("parallel",)),
    )(x, g)


def _rmsnorm_bwd_rows(xv, gv, dy):
    r = lax.rsqrt(jnp.mean(xv * xv, axis=-1, keepdims=True) + EPS)
    xhat = xv * r
    dxh = dy * gv
    dx = r * (dxh - xhat * jnp.mean(dxh * xhat, axis=-1, keepdims=True))
    return dx, dy * xhat


def _rmsnorm_bwd(dn, x, g, skip, *, name):
    s, d = x.shape
    tr = _row_block(s, d)

    def body(dn_ref, x_ref, g_ref, skip_ref, dx_ref, dxb_ref, dg_ref):
        dx, dgr = _rmsnorm_bwd_rows(x_ref[...], g_ref[...], dn_ref[...])
        dx = dx + skip_ref[...]
        dx_ref[...] = dx
        dxb_ref[...] = dx.astype(BF16)

        @pl.when(pl.program_id(0) == 0)
        def _():
            dg_ref[...] = jnp.zeros_like(dg_ref)

        dg_ref[...] += jnp.sum(dgr, axis=0, keepdims=True)

    row = pl.BlockSpec((tr, d), lambda i: (i, 0))
    vec = pl.BlockSpec((1, d), lambda i: (0, 0))
    return pl.pallas_call(
        body, name=name, grid=(s // tr,), in_specs=[row, row, vec, row],
        out_specs=[row, row, vec],
        out_shape=[jax.ShapeDtypeStruct((s, d), F32), jax.ShapeDtypeStruct((s, d), BF16),
                   jax.ShapeDtypeStruct((1, d), F32)],
        compiler_params=_params(("arbitrary",)),
    )(dn, x, g, skip)


def _loss_head(h2, g, target):
    s, d = h2.shape
    tr = _row_block(s, d)

    def body(h_ref, g_ref, t_ref, dh_ref, dhb_ref, dg_ref, loss_ref):
        hv = h_ref[...]
        gv = g_ref[...]
        r = lax.rsqrt(jnp.mean(hv * hv, axis=-1, keepdims=True) + EPS)
        err = hv * r * gv - t_ref[...]
        dx, dgr = _rmsnorm_bwd_rows(hv, gv, err * (1.0 / d))
        dh_ref[...] = dx
        dhb_ref[...] = dx.astype(BF16)

        @pl.when(pl.program_id(0) == 0)
        def _():
            dg_ref[...] = jnp.zeros_like(dg_ref)
            loss_ref[...] = jnp.zeros_like(loss_ref)

        dg_ref[...] += jnp.sum(dgr, axis=0, keepdims=True)
        row_loss = jnp.mean(err * err, axis=-1, keepdims=True)
        loss_ref[...] += 0.5 * jnp.sum(row_loss, axis=0, keepdims=True)

    row = pl.BlockSpec((tr, d), lambda i: (i, 0))
    vec = pl.BlockSpec((1, d), lambda i: (0, 0))
    one = pl.BlockSpec((1, 1), lambda i: (0, 0))
    return pl.pallas_call(
        body, name="loss_head", grid=(s // tr,), in_specs=[row, vec, row],
        out_specs=[row, row, vec, one],
        out_shape=[jax.ShapeDtypeStruct((s, d), F32), jax.ShapeDtypeStruct((s, d), BF16),
                   jax.ShapeDtypeStruct((1, d), F32), jax.ShapeDtypeStruct((1, 1), F32)],
        compiler_params=_params(("arbitrary",)),
    )(h2, g, target)


def _attention_bias_tables():
    k = np.arange(-ATT_KB, ATT_KB + 1)[:, None, None]
    delta = k * ATT_BLOCK + np.arange(ATT_BLOCK)[None, None, :] - np.arange(ATT_BLOCK)[None, :, None]
    dist = np.abs(delta)
    count = np.zeros(delta.shape, np.int32)
    for window, dilation in DILATED_PATTERNS:
        count += (delta % dilation == 0) & (dist <= window // 2)
    logc = np.where(count > 0, np.log(np.maximum(count, 1)), MASKED)
    return dist.astype(np.float32), logc.astype(np.float32)


def _head_bias(bias_ref, slope, dist_ref, logc_ref):
    for kk in range(ATT_WINDOW):
        bias_ref[kk] = logc_ref[kk] - slope * dist_ref[kk]
    bias_ref[ATT_WINDOW] = jnp.full((ATT_BLOCK, ATT_BLOCK), MASKED, F32)


def _window_start(i, nq, nwin):
    return jnp.clip(i - ATT_KB, 0, nq - nwin)


def _window_block(j, i):
    rows = pl.ds(pl.multiple_of(j * ATT_BLOCK, ATT_BLOCK), ATT_BLOCK)
    kk = j - i + ATT_KB
    return rows, jnp.where(jnp.logical_and(kk >= 0, kk < ATT_WINDOW), kk, ATT_WINDOW)


def _attention_fwd(proj, slopes, n_heads, jobs=()):
    s = proj.shape[0]
    nq = s // ATT_BLOCK
    scale = HEAD_DIM ** -0.5
    dist, logc = _attention_bias_tables()

    nwin = min(ATT_WINDOW, nq)

    def body(slope_ref, q_ref, k_ref, v_ref, dist_ref, logc_ref, o_ref, lse_ref, bias_ref, s_ref):
        h, i = pl.program_id(0), pl.program_id(1)

        @pl.when(i == 0)
        def _():
            _head_bias(bias_ref, slope_ref[h], dist_ref, logc_ref)

        q = q_ref[...]
        first = _window_start(i, nq, nwin)
        m = jnp.full((ATT_BLOCK, 1), ROW_MAX_INIT, F32)
        for b in range(nwin):
            rows, kk = _window_block(first + b, i)
            sc = _dot(q, k_ref[rows, :], tb=True) * scale + bias_ref[kk]
            s_ref[b] = sc
            m = jnp.maximum(m, jnp.max(sc, axis=-1, keepdims=True))
        l = jnp.zeros((ATT_BLOCK, 1), F32)
        acc = jnp.zeros((ATT_BLOCK, HEAD_DIM), F32)
        for b in range(nwin):
            rows, _ = _window_block(first + b, i)
            p = jnp.exp(s_ref[b] - m)
            l = l + jnp.sum(p, axis=-1, keepdims=True)
            acc = acc + _dot(p.astype(BF16), v_ref[rows, :])
        o_ref[...] = (acc / l).astype(BF16)
        lse_ref[...] = jnp.broadcast_to(m + jnp.log(l), (ATT_BLOCK, HEAD_DIM))

    hh = n_heads
    blk = pl.BlockSpec((ATT_BLOCK, HEAD_DIM), lambda h, i: (i, h))
    table = pl.BlockSpec(dist.shape, lambda h, i: (0, 0, 0))
    return _call(
        body, name="attention_fwd", grid=(hh, nq),
        in_specs=[pl.BlockSpec(memory_space=pltpu.SMEM), blk,
                  pl.BlockSpec((s, HEAD_DIM), lambda h, i: (0, hh + h)),
                  pl.BlockSpec((s, HEAD_DIM), lambda h, i: (0, 2 * hh + h)), table, table],
        out_specs=[blk, blk],
        out_shape=[jax.ShapeDtypeStruct((s, hh * HEAD_DIM), BF16), jax.ShapeDtypeStruct((s, hh * HEAD_DIM), F32)],
        operands=[slopes, proj, proj, proj, jnp.asarray(dist), jnp.asarray(logc)],
        scratch_shapes=[pltpu.VMEM((ATT_WINDOW + 1, ATT_BLOCK, ATT_BLOCK), F32),
                        pltpu.VMEM((nwin, ATT_BLOCK, ATT_BLOCK), F32)],
        semantics=("parallel", "arbitrary"), jobs=jobs)


def _attention_bwd(proj, slopes, out, lse, dmixed, n_heads, jobs=()):
    s = proj.shape[0]
    nq = s // ATT_BLOCK
    scale = HEAD_DIM ** -0.5
    dist, logc = _attention_bias_tables()

    nwin = min(ATT_WINDOW, nq)

    def body(slope_ref, q_ref, k_ref, v_ref, o_ref, do_ref, lse_ref, dist_ref, logc_ref,
             dq_ref, dk_ref, dv_ref, dk_acc, dv_acc, bias_ref):
        h, i = pl.program_id(0), pl.program_id(1)

        @pl.when(i == 0)
        def _():
            dk_acc[...] = jnp.zeros_like(dk_acc)
            dv_acc[...] = jnp.zeros_like(dv_acc)
            _head_bias(bias_ref, slope_ref[h], dist_ref, logc_ref)

        q = q_ref[...]
        do = do_ref[...]
        lse_col = lse_ref[:, :1]
        delta = jnp.sum(do.astype(F32) * o_ref[...].astype(F32), axis=-1, keepdims=True)
        first = _window_start(i, nq, nwin)
        dq = jnp.zeros((ATT_BLOCK, HEAD_DIM), F32)
        for b in range(nwin):
            rows, kk = _window_block(first + b, i)
            kj = k_ref[rows, :]
            vj = v_ref[rows, :]
            p = jnp.exp(_dot(q, kj, tb=True) * scale + bias_ref[kk] - lse_col)
            dv_acc[rows, :] += _dot(p.astype(BF16), do, ta=True)
            dp = _dot(do, vj, tb=True)
            ds = (p * (dp - delta) * scale).astype(BF16)
            dk_acc[rows, :] += _dot(ds, q, ta=True)
            dq = dq + _dot(ds, kj)
        dq_ref[...] = dq.astype(BF16)

        @pl.when(i == nq - 1)
        def _():
            dk_ref[...] = dk_acc[...].astype(BF16)
            dv_ref[...] = dv_acc[...].astype(BF16)

    hh = n_heads
    blk = pl.BlockSpec((ATT_BLOCK, HEAD_DIM), lambda h, i: (i, h))
    col = pl.BlockSpec((s, HEAD_DIM), lambda h, i: (0, h))
    table = pl.BlockSpec(dist.shape, lambda h, i: (0, 0, 0))
    o_shape = jax.ShapeDtypeStruct((s, hh * HEAD_DIM), BF16)
    return _call(
        body, name="attention_bwd", grid=(hh, nq),
        in_specs=[pl.BlockSpec(memory_space=pltpu.SMEM), blk,
                  pl.BlockSpec((s, HEAD_DIM), lambda h, i: (0, hh + h)),
                  pl.BlockSpec((s, HEAD_DIM), lambda h, i: (0, 2 * hh + h)),
                  blk, blk, blk, table, table],
        out_specs=[blk, col, col], out_shape=[o_shape] * 3,
        operands=[slopes, proj, proj, proj, out, dmixed, lse, jnp.asarray(dist), jnp.asarray(logc)],
        scratch_shapes=[pltpu.VMEM((s, HEAD_DIM), F32)] * 2
        + [pltpu.VMEM((ATT_WINDOW + 1, ATT_BLOCK, ATT_BLOCK), F32)],
        semantics=("parallel", "arbitrary"), jobs=jobs)


def _ret_decays(lgc, lga, strict_c, strict_a):
    c = RET_CHUNK
    rel = (lax.broadcasted_iota(jnp.int32, (c, c), 0) - lax.broadcasted_iota(jnp.int32, (c, c), 1)).astype(F32)
    in_c = (rel > 0) if strict_c else (rel >= 0)
    in_a = (rel < 0) if strict_a else (rel <= 0)
    mask = (jnp.where(in_c, jnp.exp(lgc * jnp.maximum(rel, 0.0)), 0.0)
            + jnp.where(in_a, jnp.exp(lga * jnp.maximum(-rel, 0.0)), 0.0))
    idx = lax.broadcasted_iota(jnp.int32, (c, 1), 0).astype(F32)
    ones = jnp.ones((1, HEAD_DIM), F32)
    dec = dict(
        rel=rel, mask=mask, idx=idx,
        a_c=jnp.exp(lgc * (idx + 1.0)), b_c=jnp.exp(lgc * (c - 1.0 - idx)), chunk_c=jnp.exp(ones * (lgc * c)),
        a_a=jnp.exp(lga * (c - idx)), b_a=jnp.exp(lga * idx), chunk_a=jnp.exp(ones * (lga * c)),
    )
    return dec


def _scaled(x, col):
    return (x.astype(F32) * col).astype(BF16)


def _chunk_rows(i):
    return pl.ds(pl.multiple_of(i * RET_CHUNK, RET_CHUNK), RET_CHUNK)


def _retention(a, b, c, lg_c, lg_a, *, strict_c, strict_a, scale, n_heads, name, gate=None, norm_w=None, jobs=()):
    s = a[0].shape[0]
    nc = s // RET_CHUNK
    epilogue = gate is not None

    def body(*refs):
        lgc_ref, lga_ref, a_ref, b_ref, c_ref = refs[:5]
        if epilogue:
            g_ref, w_ref, o_ref, mix_ref, sa_ref = refs[5:]
        else:
            o_ref, sa_ref = refs[5:]
        h = pl.program_id(0)
        dec = _ret_decays(lgc_ref[h], lga_ref[h], strict_c, strict_a)

        def reverse(t, state):
            i = nc - 1 - t
            sa_ref[i] = state.astype(BF16)
            rows = _chunk_rows(i)
            return state * dec["chunk_a"] + _dot(_scaled(b_ref[rows, :], dec["b_a"]), c_ref[rows, :], ta=True)

        lax.fori_loop(0, nc, reverse, jnp.zeros((HEAD_DIM, HEAD_DIM), F32))

        def forward(i, state):
            rows = _chunk_rows(i)
            ai, bi, ci = a_ref[rows, :], b_ref[rows, :], c_ref[rows, :]
            inner = (_dot(ai, bi, tb=True) * dec["mask"]).astype(BF16)
            out = (_dot(inner, ci) + _dot(_scaled(ai, dec["a_c"]), state.astype(BF16))
                   + _dot(_scaled(ai, dec["a_a"]), sa_ref[i])) * scale
            o_ref[rows, :] = out.astype(BF16)
            if epilogue:
                r = lax.rsqrt(jnp.mean(out * out, axis=-1, keepdims=True) + EPS)
                g = g_ref[rows, :].astype(F32)
                mix_ref[rows, :] = (out * r * w_ref[...] * (g * _sigmoid(g))).astype(BF16)
            return state * dec["chunk_c"] + _dot(_scaled(bi, dec["b_c"]), ci, ta=True)

        lax.fori_loop(0, nc, forward, jnp.zeros((HEAD_DIM, HEAD_DIM), F32))

    def col(first):
        return pl.BlockSpec((s, HEAD_DIM), lambda h: (0, first + h))

    smem = pl.BlockSpec(memory_space=pltpu.SMEM)
    in_specs = [smem, smem, col(a[1]), col(b[1]), col(c[1])]
    operands = [lg_c, lg_a, a[0], b[0], c[0]]
    o_shape = jax.ShapeDtypeStruct((s, n_heads * HEAD_DIM), BF16)
    out_specs, out_shape = [col(0)], [o_shape]
    if epilogue:
        in_specs += [col(gate[1]), pl.BlockSpec((1, HEAD_DIM), lambda h: (0, h))]
        operands += [gate[0], norm_w]
        out_specs, out_shape = [col(0)] * 2, [o_shape] * 2
    res, carried = _call(
        body, name=name, grid=(n_heads,), in_specs=in_specs, out_specs=out_specs, out_shape=out_shape,
        operands=operands, scratch_shapes=[pltpu.VMEM((nc, HEAD_DIM, HEAD_DIM), BF16)],
        semantics=("parallel",), jobs=jobs)
    res = res if epilogue else res[0]
    return (res, carried) if jobs else res


def _retention_decay_grads(a, b, c, e, lg_c, lg_a, *, scale, n_heads):
    s = a[0].shape[0]
    nc = s // RET_CHUNK
    cf = float(RET_CHUNK)

    def body(lgc_ref, lga_ref, a_ref, b_ref, c_ref, e_ref, gc_ref, ga_ref, sa_ref, ta_ref):
        h = pl.program_id(0)
        lgc, lga = lgc_ref[h], lga_ref[h]
        dec = _ret_decays(lgc, lga, True, True)
        rel, idx = dec["rel"], dec["idx"]
        w_c = jnp.where(rel > 0, rel * jnp.exp(lgc * jnp.maximum(rel, 0.0)), 0.0)
        w_a = jnp.where(rel < 0, -rel * jnp.exp(lga * jnp.maximum(-rel, 0.0)), 0.0)
        zero = jnp.zeros((HEAD_DIM, HEAD_DIM), F32)

        def reverse(t, carry):
            st, dst = carry
            i = nc - 1 - t
            sa_ref[i] = st.astype(BF16)
            ta_ref[i] = dst.astype(BF16)
            rows = _chunk_rows(i)
            bi, ci = b_ref[rows, :], c_ref[rows, :]
            st_new = st * dec["chunk_a"] + _dot(_scaled(bi, dec["b_a"]), ci, ta=True)
            dst_new = (cf * st + dst) * dec["chunk_a"] + _dot(_scaled(bi, idx * dec["b_a"]), ci, ta=True)
            return st_new, dst_new

        lax.fori_loop(0, nc, reverse, (zero, zero))

        def forward(i, carry):
            st, dst, acc_c, acc_a = carry
            rows = _chunk_rows(i)
            ai, bi, ci = a_ref[rows, :], b_ref[rows, :], c_ref[rows, :]
            ev = e_ref[rows, :].astype(F32)
            pg = _dot(ai, bi, tb=True) * _dot(e_ref[rows, :], ci, tb=True)
            a_c, a_a = _scaled(ai, dec["a_c"]), _scaled(ai, dec["a_a"])
            inter_c = _dot(a_c, st.astype(BF16)) * (idx + 1.0) + _dot(a_c, dst.astype(BF16))
            inter_a = _dot(a_a, sa_ref[i]) * (cf - idx) + _dot(a_a, ta_ref[i])
            acc_c = acc_c + jnp.sum(pg * w_c, axis=0, keepdims=True) + jnp.sum(inter_c * ev, axis=0, keepdims=True)
            acc_a = acc_a + jnp.sum(pg * w_a, axis=0, keepdims=True) + jnp.sum(inter_a * ev, axis=0, keepdims=True)
            st_new = st * dec["chunk_c"] + _dot(_scaled(bi, dec["b_c"]), ci, ta=True)
            dst_new = ((cf * st + dst) * dec["chunk_c"]
                       + _dot(_scaled(bi, (cf - 1.0 - idx) * dec["b_c"]), ci, ta=True))
            return st_new, dst_new, acc_c, acc_a

        row = jnp.zeros((1, HEAD_DIM), F32)
        _, _, acc_c, acc_a = lax.fori_loop(0, nc, forward, (zero, zero, row, row))
        gc_ref[...] = jnp.broadcast_to(jnp.sum(acc_c, axis=-1, keepdims=True) * scale, gc_ref.shape)
        ga_ref[...] = jnp.broadcast_to(jnp.sum(acc_a, axis=-1, keepdims=True) * scale, ga_ref.shape)

    def col(first):
        return pl.BlockSpec((s, HEAD_DIM), lambda h: (0, first + h))

    smem = pl.BlockSpec(memory_space=pltpu.SMEM)
    o_spec = pl.BlockSpec((1, 8, HEAD_DIM), lambda h: (h, 0, 0))
    o_shape = jax.ShapeDtypeStruct((n_heads, 8, HEAD_DIM), F32)
    gc, ga = pl.pallas_call(
        body, name="retention_decay_grads", grid=(n_heads,),
        in_specs=[smem, smem, col(a[1]), col(b[1]), col(c[1]), col(e[1])],
        out_specs=[o_spec] * 2, out_shape=[o_shape] * 2,
        scratch_shapes=[pltpu.VMEM((nc, HEAD_DIM, HEAD_DIM), BF16)] * 2,
        compiler_params=_params(("parallel",)),
    )(lg_c, lg_a, a[0], b[0], c[0], e[0])
    return gc[:, 0, 0], ga[:, 0, 0]


def _ret_gate_bwd(dmixed, first_col, out, proj, gate_col, norm_w, n_heads):
    s = out.shape[0]
    tr = _row_block(s, 8 * HEAD_DIM)

    def body(dm_ref, o_ref, g_ref, w_ref, do_ref, dg_ref, dw_ref):
        dm = dm_ref[...].astype(F32)
        ov = o_ref[...].astype(F32)
        g = g_ref[...].astype(F32)
        w = w_ref[...]
        r = lax.rsqrt(jnp.mean(ov * ov, axis=-1, keepdims=True) + EPS)
        ohat = ov * r
        sg = _sigmoid(g)
        silu = g * sg
        dg_ref[...] = (dm * ohat * w * sg * (1.0 + g * (1.0 - sg))).astype(BF16)
        dohat = dm * w * silu
        do_ref[...] = (r * (dohat - ohat * jnp.mean(dohat * ohat, axis=-1, keepdims=True))).astype(BF16)

        @pl.when(pl.program_id(1) == 0)
        def _():
            dw_ref[...] = jnp.zeros_like(dw_ref)

        dw_ref[...] += jnp.sum(dm * ohat * silu, axis=0, keepdims=True)

    def blk(first):
        return pl.BlockSpec((tr, HEAD_DIM), lambda h, i: (i, first + h))

    vec = pl.BlockSpec((1, HEAD_DIM), lambda h, i: (0, h))
    o_shape = jax.ShapeDtypeStruct((s, n_heads * HEAD_DIM), BF16)
    return pl.pallas_call(
        body, name="ret_gate_bwd", grid=(n_heads, s // tr),
        in_specs=[blk(first_col), blk(0), blk(gate_col), vec],
        out_specs=[blk(0), blk(0), vec],
        out_shape=[o_shape, o_shape, jax.ShapeDtypeStruct((1, n_heads * HEAD_DIM), F32)],
        compiler_params=_params(("parallel", "arbitrary")),
    )(dmixed, out, proj, norm_w)


def _step(x, target, norm_mix_w, ret_decay_fwd, ret_decay_bwd, ret_norm_w, norm_ffn_w, norm_final_w, own, pos):
    d = x.shape[1]
    nh = d // (2 * HEAD_DIM)
    scale = HEAD_DIM ** -0.5
    slopes = jnp.exp2(-8.0 * jnp.arange(1, nh + 1, dtype=F32) / nh)
    lg_f = -jnp.exp(ret_decay_fwd)
    lg_b = -jnp.exp(ret_decay_bwd)
    q_r, k_r, v_r, g_r = 3 * nh, 4 * nh, 5 * nh, 6 * nh
    ax = BIG_AXIS

    def gather(names, arrays, stage):
        return _gather_job(arrays, [ax[k] for k in names], stage)

    def add_halves(k, g, received):
        return _add_halves(g, received, ax[k], pos, name="grad_add_halves_" + k)

    def sum_parts(k, g, received, parts):
        return _sum_chip_parts(g, received, parts, ax[k], pos, name="grad_sum_parts_" + k)

    (w_in,) = _run_jobs([gather(["w_in"], [own["w_in"]], "ici"), gather(["w_in"], [own["w_in"]], "d2d")],
                        name="all_gather_w_in")
    n1 = _rmsnorm_fwd(x, norm_mix_w, name="norm_mix_fwd")
    proj, [[w_out, w_gate]] = _matmul(
        n1, w_in, name="in_proj", out_dtype=BF16,
        jobs=[gather(["w_out", "w_gate"], [own["w_out"], own["w_gate"]], "ici")])
    (attn, lse), [[w_out, w_gate], [w_up]] = _attention_fwd(
        proj, slopes, nh,
        jobs=[gather(["w_out", "w_gate"], [w_out, w_gate], "d2d"), gather(["w_up"], [own["w_up"]], "ici")])
    (ret, ret_mixed), [[w_up], [w_down]] = _retention(
        (proj, q_r), (proj, k_r), (proj, v_r), lg_f, lg_b, strict_c=False, strict_a=True, scale=scale, n_heads=nh,
        name="retention_fwd", gate=(proj, g_r), norm_w=ret_norm_w,
        jobs=[gather(["w_up"], [w_up], "d2d"), gather(["w_down"], [own["w_down"]], "ici")])
    mixed = jnp.concatenate([attn, ret_mixed], axis=1)
    h1, [[w_down]] = _matmul(mixed, w_out, name="out_proj", residual=x, jobs=[gather(["w_down"], [w_down], "d2d")])
    n2 = _rmsnorm_fwd(h1, norm_ffn_w, name="norm_ffn_fwd")
    gate, up, act = _swiglu_fwd(n2, w_gate, w_up)
    h2 = _matmul(act, w_down, name="down_proj", residual=h1)
    dh2, dh2_b, d_norm_final, loss = _loss_head(h2, norm_final_w, target)

    dgate, dup = _swiglu_bwd_act(dh2_b, w_down, gate, up)
    g_down = _matmul(act, dh2_b, name="grad_w_down", ta=True)
    g_gate, [[r_down]] = _matmul(n2, dgate, name="grad_w_gate", ta=True, jobs=[_exchange_job([g_down], [ax["w_down"]])])
    s_down = add_halves("w_down", g_down, r_down)
    g_up, [[r_gate], [p_down]] = _matmul(
        n2, dup, name="grad_w_up", ta=True,
        jobs=[_exchange_job([g_gate], [ax["w_gate"]]), _send_sums_job([s_down], [ax["w_down"]])])
    s_gate = add_halves("w_gate", g_gate, r_gate)
    h_down = sum_parts("w_down", g_down, r_down, p_down)
    dn2, [[r_up], [p_gate]] = _swiglu_bwd_in(
        dgate, dup, w_gate, w_up,
        jobs=[_exchange_job([g_up], [ax["w_up"]]), _send_sums_job([s_gate], [ax["w_gate"]])])
    s_up = add_halves("w_up", g_up, r_up)
    h_gate = sum_parts("w_gate", g_gate, r_gate, p_gate)
    dh1, dh1_b, d_norm_ffn = _rmsnorm_bwd(dn2, h1, norm_ffn_w, dh2, name="norm_ffn_bwd")

    dmixed, [[gr_down]] = _matmul(dh1_b, w_out, name="out_proj_bwd", tb=True, out_dtype=BF16,
                                  jobs=[_join_job([h_down], [ax["w_down"]])])
    g_out = _matmul(mixed, dh1_b, name="grad_w_out", ta=True)
    d_ret, dg_r, d_ret_norm = _ret_gate_bwd(dmixed, nh, ret, proj, g_r, ret_norm_w, nh)
    (dq_a, dk_a, dv_a), [[p_up], [r_out]] = _attention_bwd(
        proj, slopes, attn, lse, dmixed, nh,
        jobs=[_send_sums_job([s_up], [ax["w_up"]]), _exchange_job([g_out], [ax["w_out"]])])
    s_out = add_halves("w_out", g_out, r_out)
    h_up = sum_parts("w_up", g_up, r_up, p_up)
    dq_r, [[p_out], [gr_gate, gr_up]] = _retention(
        (d_ret, 0), (proj, v_r), (proj, k_r), lg_f, lg_b, strict_c=False, strict_a=True, scale=scale, n_heads=nh,
        name="retention_dq",
        jobs=[_send_sums_job([s_out], [ax["w_out"]]), _join_job([h_gate, h_up], [ax["w_gate"], ax["w_up"]])])
    h_out = sum_parts("w_out", g_out, r_out, p_out)
    dv_r, [[gr_out]] = _retention(
        (proj, k_r), (proj, q_r), (d_ret, 0), lg_b, lg_f, strict_c=True, strict_a=False, scale=scale, n_heads=nh,
        name="retention_dv", jobs=[_join_job([h_out], [ax["w_out"]])])
    dk_r = _retention((proj, v_r), (d_ret, 0), (proj, q_r), lg_b, lg_f, strict_c=True, strict_a=False,
                      scale=scale, n_heads=nh, name="retention_dk")
    dlg_f, dlg_b = _retention_decay_grads((proj, q_r), (proj, k_r), (proj, v_r), (d_ret, 0), lg_f, lg_b,
                                          scale=scale, n_heads=nh)
    dproj = jnp.concatenate([dq_a, dk_a, dv_a, dq_r, dk_r, dv_r, dg_r], axis=1)
    g_in = _matmul(n1, dproj, name="grad_w_in", ta=True)
    (r_in,) = _run_jobs([_exchange_job([g_in], [ax["w_in"]])], name="grad_exchange_w_in")
    s_in = add_halves("w_in", g_in, r_in)
    dn1, [[p_in]] = _matmul(dproj, w_in, name="in_proj_bwd", tb=True, jobs=[_send_sums_job([s_in], [ax["w_in"]])])
    dx, _, d_norm_mix = _rmsnorm_bwd(dn1, x, norm_mix_w, dh1, name="norm_mix_bwd")
    h_in = sum_parts("w_in", g_in, r_in, p_in)
    (gr_in,) = _run_jobs([_join_job([h_in], [ax["w_in"]])], name="grad_join_w_in")

    small = dict(loss=loss[0, 0], norm_mix_w=d_norm_mix, ret_decay_fwd=dlg_f * lg_f, ret_decay_bwd=dlg_b * lg_b,
                 ret_norm_w=d_ret_norm, norm_ffn_w=d_norm_ffn, norm_final_w=d_norm_final)
    return dx, dict(w_in=gr_in, w_out=gr_out, w_gate=gr_gate, w_up=gr_up, w_down=gr_down), small


def _mesh_position():
    x, y, c = lax.axis_index("x"), lax.axis_index("y"), lax.axis_index("c")
    chips = [(1 - x, y), (x, 1 - y), (1 - x, 1 - y)]
    return x, y, c, chips


def _ds(start, size):
    if isinstance(start, int):
        return pl.ds(start, size)
    return pl.ds(pl.multiple_of(start * size, size), size)


def _region(ref, axis, shard, half, shard_size, half_size):
    along = slice(None) if shard is None else _ds(shard, shard_size)
    other = slice(None) if half is None else _ds(half, half_size)
    return ref.at[other, along] if axis == 1 else ref.at[along, other]


def _gather_job(full, axes, stage):
    n = len(full)

    def copies(refs, sems):
        send_sem, recv_sem = sems
        x, y, c, chips = _mesh_position()
        me = 2 * x + y

        def copy(w, k, shard, half, target):
            rows_cols = full[w].shape
            place = _region(refs[w], axes[w], shard, half, rows_cols[axes[w]] // N_CHIPS, rows_cols[1 - axes[w]] // 2)
            return pltpu.make_async_remote_copy(
                src_ref=place, dst_ref=place, send_sem=send_sem.at[w, k], recv_sem=recv_sem.at[w, k],
                device_id=target, device_id_type=MESH)

        def sent(w, k):
            if stage == "ici":
                return copy(w, k, me, c, (chips[k][0], chips[k][1], c))
            return copy(w, k, 2 * chips[k][0] + chips[k][1], c, (x, y, 1 - c))

        def landed(w, k):
            return copy(w, k, 2 * chips[k][0] + chips[k][1], c if stage == "ici" else 1 - c, (x, y, 1 - c))

        return sent, landed

    def start(refs, sems):
        sent, _ = copies(refs, sems)
        for w in range(n):
            for k in range(3):
                sent(w, k).start()

    def finish(refs, sems):
        sent, landed = copies(refs, sems)
        for w in range(n):
            for k in range(3):
                landed(w, k).wait_recv()
                sent(w, k).wait_send()

    return _Job(ios=full, sems=[pltpu.SemaphoreType.DMA((n, 3))] * 2, start=start, finish=finish)


def _exchange_job(grads, axes):
    n = len(grads)

    def half_shape(w):
        return tuple(d // 2 if a != axes[w] else d for a, d in enumerate(grads[w].shape))

    def copy(refs, sems, w):
        x, y, c, _ = _mesh_position()
        return pltpu.make_async_remote_copy(
            src_ref=_region(refs[w], axes[w], None, 1 - c, 0, half_shape(w)[1 - axes[w]]), dst_ref=refs[n + w],
            send_sem=sems[0].at[w], recv_sem=sems[1].at[w], device_id=(x, y, 1 - c), device_id_type=MESH)

    def start(refs, sems):
        for w in range(n):
            copy(refs, sems, w).start()

    def finish(refs, sems):
        for w in range(n):
            copy(refs, sems, w).wait()

    return _Job(ins=grads, outs=[jax.ShapeDtypeStruct(half_shape(w), F32) for w in range(n)],
                sems=[pltpu.SemaphoreType.DMA((n,))] * 2, start=start, finish=finish)


def _half_block_spec(axis, block, half_blocks, use_half):
    if axis == 1:
        if use_half:
            return pl.BlockSpec(block, lambda i, pos: (pos[0] * half_blocks + i, 0))
        return pl.BlockSpec(block, lambda i, pos: (i, 0))
    if use_half:
        return pl.BlockSpec(block, lambda i, pos: (i, pos[0]))
    return pl.BlockSpec(block, lambda i, pos: (i, 0))


def _add_halves(grad, received, axis, pos, *, name):
    rows, cols = received.shape
    tr = _row_block(rows, cols)
    nb = rows // tr

    def body(pos_ref, g_ref, r_ref, o_ref):
        o_ref[...] = (g_ref[...] + r_ref[...]).astype(BF16)

    blk = (tr, cols)
    return pl.pallas_call(
        body, name=name, out_shape=jax.ShapeDtypeStruct((rows, cols), BF16),
        grid_spec=pltpu.PrefetchScalarGridSpec(
            num_scalar_prefetch=1, grid=(nb,),
            in_specs=[_half_block_spec(axis, blk, nb, True), _half_block_spec(axis, blk, nb, False)],
            out_specs=_half_block_spec(axis, blk, nb, False)),
        compiler_params=_params(("parallel",)),
    )(pos, grad, received)


def _send_sums_job(sums, axes):
    n = len(sums)

    def part_shape(w):
        return tuple(d // N_CHIPS if a == axes[w] else d for a, d in enumerate(sums[w].shape))

    def copy(refs, sems, w, k):
        x, y, c, chips = _mesh_position()
        shard = 2 * chips[k][0] + chips[k][1]
        return pltpu.make_async_remote_copy(
            src_ref=_region(refs[w], axes[w], shard, None, part_shape(w)[axes[w]], 0), dst_ref=refs[n + w].at[k],
            send_sem=sems[0].at[w, k], recv_sem=sems[1].at[w, k],
            device_id=(chips[k][0], chips[k][1], c), device_id_type=MESH)

    def start(refs, sems):
        for w in range(n):
            for k in range(3):
                copy(refs, sems, w, k).start()

    def finish(refs, sems):
        for w in range(n):
            for k in range(3):
                copy(refs, sems, w, k).wait()

    return _Job(ins=sums, outs=[jax.ShapeDtypeStruct((3,) + part_shape(w), BF16) for w in range(n)],
                sems=[pltpu.SemaphoreType.DMA((n, 3))] * 2, start=start, finish=finish)


def _sum_chip_parts(grad, received, parts, axis, pos, *, name):
    _, rows, cols = parts.shape
    tr = _row_block(rows, cols)
    nb = rows // tr
    blk = (tr, cols)

    def body(pos_ref, g_ref, r_ref, p_ref, o_ref):
        total = g_ref[...] + r_ref[...]
        for k in range(3):
            total = total + p_ref[k].astype(F32)
        o_ref[...] = total

    if axis == 1:
        g_spec = pl.BlockSpec(blk, lambda i, pos: (pos[0] * nb + i, pos[1]))
        r_spec = pl.BlockSpec(blk, lambda i, pos: (i, pos[1]))
        o_spec = pl.BlockSpec(blk, lambda i, pos: (pos[0] * nb + i, 0))
        shard_shape = (2 * rows, cols)
    else:
        g_spec = pl.BlockSpec(blk, lambda i, pos: (pos[1] * nb + i, pos[0]))
        r_spec = pl.BlockSpec(blk, lambda i, pos: (pos[1] * nb + i, 0))
        o_spec = pl.BlockSpec(blk, lambda i, pos: (i, pos[0]))
        shard_shape = (rows, 2 * cols)
    return pl.pallas_call(
        body, name=name, out_shape=jax.ShapeDtypeStruct(shard_shape, F32),
        grid_spec=pltpu.PrefetchScalarGridSpec(
            num_scalar_prefetch=1, grid=(nb,),
            in_specs=[g_spec, r_spec, pl.BlockSpec((3,) + blk, lambda i, pos: (0, i, 0))],
            out_specs=o_spec),
        compiler_params=_params(("parallel",)),
    )(pos, grad, received, parts)


def _join_job(shards, axes):
    n = len(shards)

    def copy(refs, sems, w, other):
        x, y, c, _ = _mesh_position()
        place = _region(refs[w], axes[w], None, 1 - c if other else c, 0, shards[w].shape[1 - axes[w]] // 2)
        return pltpu.make_async_remote_copy(
            src_ref=place, dst_ref=place, send_sem=sems[0].at[w], recv_sem=sems[1].at[w],
            device_id=(x, y, 1 - c), device_id_type=MESH)

    def start(refs, sems):
        for w in range(n):
            copy(refs, sems, w, False).start()

    def finish(refs, sems):
        for w in range(n):
            copy(refs, sems, w, True).wait_recv()
            copy(refs, sems, w, False).wait_send()

    return _Job(ios=shards, sems=[pltpu.SemaphoreType.DMA((n,))] * 2, start=start, finish=finish)


def _all_reduce_small(vec):
    rows, cols = vec.shape

    def body(v_ref, o_ref, land_ref, send_sem, recv_sem):
        x, y, c, _ = _mesh_position()
        me = 4 * x + 2 * y + c
        land_ref[me] = v_ref[...]
        copies = []
        for k in range(1, 8):
            px, py, pc = x ^ (k >> 2), y ^ ((k >> 1) & 1), c ^ (k & 1)
            copies.append(pltpu.make_async_remote_copy(
                src_ref=v_ref, dst_ref=land_ref.at[me], send_sem=send_sem.at[k], recv_sem=recv_sem.at[k],
                device_id=(px, py, pc), device_id_type=MESH))
        for cp in copies:
            cp.start()
        for k in range(1, 8):
            peer = me ^ k
            pltpu.make_async_remote_copy(
                src_ref=v_ref, dst_ref=land_ref.at[peer], send_sem=send_sem.at[k], recv_sem=recv_sem.at[k],
                device_id=(x, y, c), device_id_type=MESH).wait_recv()
        for cp in copies:
            cp.wait_send()
        total = land_ref[0]
        for k in range(1, 8):
            total = total + land_ref[k]
        o_ref[...] = total

    vmem = pl.BlockSpec(memory_space=pltpu.VMEM)
    return pl.pallas_call(
        body, name="all_reduce_small", in_specs=[vmem], out_specs=vmem,
        out_shape=jax.ShapeDtypeStruct((rows, cols), F32),
        scratch_shapes=[pltpu.VMEM((8, rows, cols), F32), pltpu.SemaphoreType.DMA((8,)), pltpu.SemaphoreType.DMA((8,))],
    )(vec)


def _adamw(w, g, m, v, *, name):
    rows, cols = w.shape
    tr = _row_block(rows, cols) if rows % 8 == 0 else rows
    bc1 = 1.0 - ADAM_B1 ** ADAM_STEP
    bc2 = 1.0 - ADAM_B2 ** ADAM_STEP

    def body(w_ref, g_ref, m_ref, v_ref, d_ref, mo_ref, vo_ref):
        gv = g_ref[...]
        mn = ADAM_B1 * m_ref[...] + (1.0 - ADAM_B1) * gv
        vn = ADAM_B2 * v_ref[...] + (1.0 - ADAM_B2) * (gv * gv)
        mo_ref[...] = mn
        vo_ref[...] = vn
        d_ref[...] = -ADAM_LR * ((mn / bc1) / (jnp.sqrt(vn / bc2) + ADAM_EPS) + ADAM_WD * w_ref[...])

    blk = pl.BlockSpec((tr, cols), lambda i: (i, 0))
    shape = jax.ShapeDtypeStruct((rows, cols), F32)
    return pl.pallas_call(
        body, name=name, grid=(rows // tr,), in_specs=[blk] * 4, out_specs=[blk] * 3, out_shape=[shape] * 3,
        compiler_params=_params(("parallel",)),
    )(w, g, m, v)


def _to_bf16_in_place(w, axis, pos, *, name):
    rows, cols = w.shape
    tr = _row_block(rows, cols)
    nb = rows // tr

    def body(pos_ref, w_ref, o_ref):
        o_ref[...] = w_ref[...].astype(BF16)

    if axis == 1:
        o_spec = pl.BlockSpec((tr, cols), lambda i, pos: (i, pos[1]))
        full_shape = (rows, N_CHIPS * cols)
    else:
        o_spec = pl.BlockSpec((tr, cols), lambda i, pos: (pos[1] * nb + i, 0))
        full_shape = (N_CHIPS * rows, cols)
    return pl.pallas_call(
        body, name=name, out_shape=jax.ShapeDtypeStruct(full_shape, BF16),
        grid_spec=pltpu.PrefetchScalarGridSpec(
            num_scalar_prefetch=1, grid=(nb,),
            in_specs=[pl.BlockSpec((tr, cols), lambda i, pos: (i, 0))], out_specs=o_spec),
        compiler_params=_params(("parallel",)),
    )(pos, w)


BIG = ("w_in", "w_out", "w_gate", "w_up", "w_down")
BIG_AXIS = dict(w_in=1, w_out=0, w_gate=1, w_up=1, w_down=0)
SMALL = ("norm_mix_w", "ret_decay_fwd", "ret_decay_bwd", "ret_norm_w", "norm_ffn_w", "norm_final_w")
ALL_WEIGHTS = ("norm_mix_w", "w_in", "ret_decay_fwd", "ret_decay_bwd", "ret_norm_w", "w_out", "norm_ffn_w",
               "w_gate", "w_up", "w_down", "norm_final_w")
SMALL_ROW = 128 * 8


def _pack_small(small):
    pieces = [jnp.reshape(small["loss"], (1,))] + [jnp.reshape(small[k], (-1,)) for k in SMALL]
    rows = []
    for p in pieces:
        pad = -p.shape[0] % (8 * SMALL_ROW)
        rows.append(jnp.reshape(jnp.pad(p, (0, pad)), (-1, SMALL_ROW)))
    return jnp.concatenate(rows, axis=0)


def _unpack_small(block, like):
    out, row = {}, 0
    for k in ("loss",) + SMALL:
        size = 1 if k == "loss" else like[k].size
        nrows = -(-size // (8 * SMALL_ROW)) * 8
        out[k] = jnp.reshape(block[row:row + nrows], (-1,))[:size]
        row += nrows
    return out


def kernel(x, norm_mix_w, w_in, ret_decay_fwd, ret_decay_bwd, ret_norm_w, w_out, norm_ffn_w, w_gate, w_up, w_down, norm_final_w, loss_target, m_norm_mix_w, m_w_in, m_ret_decay_fwd, m_ret_decay_bwd, m_ret_norm_w, m_w_out, m_norm_ffn_w, m_w_gate, m_w_up, m_w_down, m_norm_final_w, v_norm_mix_w, v_w_in, v_ret_decay_fwd, v_ret_decay_bwd, v_ret_norm_w, v_w_out, v_norm_ffn_w, v_w_gate, v_w_up, v_w_down, v_norm_final_w):
    weights = dict(norm_mix_w=norm_mix_w, w_in=w_in, ret_decay_fwd=ret_decay_fwd, ret_decay_bwd=ret_decay_bwd,
                   ret_norm_w=ret_norm_w, w_out=w_out, norm_ffn_w=norm_ffn_w, w_gate=w_gate, w_up=w_up,
                   w_down=w_down, norm_final_w=norm_final_w)
    m_in = dict(norm_mix_w=m_norm_mix_w, w_in=m_w_in, ret_decay_fwd=m_ret_decay_fwd, ret_decay_bwd=m_ret_decay_bwd,
                ret_norm_w=m_ret_norm_w, w_out=m_w_out, norm_ffn_w=m_norm_ffn_w, w_gate=m_w_gate, w_up=m_w_up,
                w_down=m_w_down, norm_final_w=m_norm_final_w)
    v_in = dict(norm_mix_w=v_norm_mix_w, w_in=v_w_in, ret_decay_fwd=v_ret_decay_fwd, ret_decay_bwd=v_ret_decay_bwd,
                ret_norm_w=v_ret_norm_w, w_out=v_w_out, norm_ffn_w=v_norm_ffn_w, w_gate=v_w_gate, w_up=v_w_up,
                w_down=v_w_down, norm_final_w=v_norm_final_w)
    pos = jnp.stack([lax.axis_index("c"), 2 * lax.axis_index("x") + lax.axis_index("y")]).astype(jnp.int32)

    own = {k: _to_bf16_in_place(weights[k][0], BIG_AXIS[k], pos, name="cast_" + k) for k in BIG}

    dx, grad_w, small = _step(
        x[0], loss_target[0], norm_mix_w, ret_decay_fwd[0], ret_decay_bwd[0], ret_norm_w, norm_ffn_w,
        norm_final_w[None, :], own, pos)

    like = {k: weights[k] for k in SMALL}
    reduced = _unpack_small(_all_reduce_small(_pack_small(small)), like)
    loss = reduced["loss"][0]
    for k in SMALL:
        grad_w[k] = jnp.reshape(reduced[k], (1, -1))

    delta, new_m, new_v = {}, {}, {}
    for k in ALL_WEIGHTS:
        shape = weights[k].shape
        as2d = (lambda t: jnp.reshape(t, (-1, shape[-1])))
        delta[k], new_m[k], new_v[k] = (jnp.reshape(t, shape) for t in _adamw(
            as2d(weights[k]), as2d(grad_w[k]), as2d(m_in[k]), as2d(v_in[k]), name="adamw_" + k))
        grad_w[k] = jnp.reshape(grad_w[k], shape)

    return (loss, dx[None], *[grad_w[k] for k in ALL_WEIGHTS], *[delta[k] for k in ALL_WEIGHTS],
            *[new_m[k] for k in ALL_WEIGHTS], *[new_v[k] for k in ALL_WEIGHTS])
```

```python
import functools
import math

import numpy as np
import jax
import jax.numpy as jnp
from jax import lax
from jax.experimental import pallas as pl
from jax.experimental.pallas import tpu as pltpu

F32 = jnp.float32
BF16 = jnp.bfloat16
MESH = pl.DeviceIdType.MESH

HEAD_DIM = 128
RET_CHUNK = 128
RET_UNROLL = 8
EPS = 1e-6
DILATED_PATTERNS = ((128, 1), (512, 4), (2048, 16))
ATT_BLOCK = 256
ATT_REACH = max(w // 2 for w, _ in DILATED_PATTERNS)
ATT_KB = -(-ATT_REACH // ATT_BLOCK)
ATT_WINDOW = 2 * ATT_KB + 1
MASKED = -1e30
ROW_MAX_INIT = -1e29
N_CHIPS = 4
VMEM_LIMIT_BYTES = 56 * 1024 * 1024
ELEM_BLOCK_BYTES = 2 * 1024 * 1024

ADAM_LR = 0.001
ADAM_B1 = 0.9
ADAM_B2 = 0.999
ADAM_EPS = 1e-08
ADAM_WD = 0.01
ADAM_STEP = 10


def _params(sem=None):
    return pltpu.CompilerParams(dimension_semantics=sem, vmem_limit_bytes=VMEM_LIMIT_BYTES)


def _sigmoid(x):
    return 1.0 / (1.0 + jnp.exp(-x))


class _Job:
    def __init__(self, *, ins=(), ios=(), outs=(), sems=(), start, finish):
        self.ins, self.ios, self.outs, self.sems = list(ins), list(ios), list(outs), list(sems)
        self.start, self.finish = start, finish

    def results(self):
        return [jax.ShapeDtypeStruct(a.shape, a.dtype) for a in self.ios] + self.outs


def _call(body, *, name, grid, in_specs, out_specs, out_shape, operands, scratch_shapes=(), semantics=None, jobs=()):
    in_specs, out_specs, out_shape = list(in_specs), list(out_specs), list(out_shape)
    scratch_shapes = list(scratch_shapes)
    if not jobs:
        outs = pl.pallas_call(body, name=name, grid=grid, in_specs=in_specs, out_specs=out_specs, out_shape=out_shape,
                              scratch_shapes=scratch_shapes, compiler_params=_params(semantics))(*operands)
        return outs, []
    n_in, n_out, n_scratch = len(in_specs), len(out_specs), len(scratch_shapes)
    extra_in, extra_out, sems, aliases = [], [], [], {}
    for job in jobs:
        extra_in += job.ins
        for t in range(len(job.ios)):
            aliases[n_in + len(extra_in) + t] = n_out + len(extra_out) + t
        extra_in += job.ios
        extra_out += job.results()
        sems += job.sems

    def carried(*refs):
        x_in = refs[n_in:n_in + len(extra_in)]
        x_out = refs[n_in + len(extra_in) + n_out:n_in + len(extra_in) + n_out + len(extra_out)]
        x_sem = refs[len(refs) - len(sems):]
        views, i_in, i_out, i_sem = [], 0, 0, 0
        for job in jobs:
            data = list(x_in[i_in:i_in + len(job.ins)]) + list(x_out[i_out:i_out + len(job.results())])
            views.append((data, x_sem[i_sem:i_sem + len(job.sems)]))
            i_in += len(job.ins) + len(job.ios)
            i_out += len(job.results())
            i_sem += len(job.sems)
        steps = [pl.program_id(d) for d in range(len(grid))]

        @pl.when(functools.reduce(jnp.logical_and, [s == 0 for s in steps]))
        def _():
            for job, (data, sem) in zip(jobs, views):
                job.start(data, sem)

        body(*refs[:n_in], *refs[n_in + len(extra_in):n_in + len(extra_in) + n_out],
             *refs[len(refs) - len(sems) - n_scratch:len(refs) - len(sems)])

        @pl.when(functools.reduce(jnp.logical_and, [s == g - 1 for s, g in zip(steps, grid)]))
        def _():
            for job, (data, sem) in zip(jobs, views):
                job.finish(data, sem)

    hbm = pl.BlockSpec(memory_space=pl.ANY)
    res = pl.pallas_call(
        carried, name=name, grid=grid, in_specs=in_specs + [hbm] * len(extra_in),
        out_specs=out_specs + [hbm] * len(extra_out), out_shape=out_shape + extra_out,
        input_output_aliases=aliases, scratch_shapes=scratch_shapes + sems,
        compiler_params=_params(("arbitrary",) * len(grid)),
    )(*operands, *extra_in)
    carried_results, at = [], n_out
    for job in jobs:
        carried_results.append(list(res[at:at + len(job.results())]))
        at += len(job.results())
    return list(res[:n_out]), carried_results


def _run_jobs(jobs, *, name):
    first = jobs[0]
    n_in, n_io = len(first.ins), len(first.ios)
    out_shape = first.results()
    n_sems = [len(job.sems) for job in jobs]

    def body(*refs):
        data = list(refs[:n_in]) + list(refs[n_in + n_io:n_in + n_io + len(out_shape)])
        at = n_in + n_io + len(out_shape)
        for job, ns in zip(jobs, n_sems):
            job.start(data, refs[at:at + ns])
            job.finish(data, refs[at:at + ns])
            at += ns

    hbm = pl.BlockSpec(memory_space=pl.ANY)
    return pl.pallas_call(
        body, name=name, in_specs=[hbm] * (n_in + n_io), out_specs=[hbm] * len(out_shape), out_shape=out_shape,
        input_output_aliases={n_in + t: t for t in range(n_io)},
        scratch_shapes=[s for job in jobs for s in job.sems],
    )(*first.ins, *first.ios)


def _dot(a, b, ta=False, tb=False):
    return lax.dot_general(a, b, (((0 if ta else 1,), (1 if tb else 0,)), ((), ())),
                           preferred_element_type=F32)


def _tile(n, want):
    t = min(n, want) // 128 * 128
    while n % t:
        t -= 128
    return t


def _a_spec(ta, tm, tk):
    return pl.BlockSpec((tk, tm), lambda i, j, k: (k, i)) if ta else pl.BlockSpec((tm, tk), lambda i, j, k: (i, k))


def _b_spec(tb, tk, tn):
    return pl.BlockSpec((tn, tk), lambda i, j, k: (j, k)) if tb else pl.BlockSpec((tk, tn), lambda i, j, k: (k, j))


def _accumulate(accs, nk, products, finish):
    if nk == 1:
        finish(*products())
        return
    k = pl.program_id(2)

    @pl.when(k == 0)
    def _():
        for acc, p in zip(accs, products()):
            acc[...] = p

    if nk > 2:
        @pl.when(jnp.logical_and(k > 0, k < nk - 1))
        def _():
            for acc, p in zip(accs, products()):
                acc[...] += p

    @pl.when(k == nk - 1)
    def _():
        finish(*[acc[...] + p for acc, p in zip(accs, products())])


def _matmul(a, b, *, name, ta=False, tb=False, out_dtype=F32, residual=None, tm=1024, tn=1024, tk=2048, jobs=()):
    m, kdim = (a.shape[1], a.shape[0]) if ta else a.shape
    n = b.shape[0] if tb else b.shape[1]
    tm, tn, tk = _tile(m, tm), _tile(n, tn), _tile(kdim, tk)
    nk = kdim // tk

    def body(*refs):
        a_ref, b_ref = refs[:2]
        r_ref = refs[2] if residual is not None else None
        o_ref = refs[-1] if nk == 1 else refs[-2]

        def finish(total):
            if residual is not None:
                total = total + r_ref[...]
            o_ref[...] = total.astype(out_dtype)

        _accumulate(refs[-1:] if nk > 1 else (), nk, lambda: (_dot(a_ref[...], b_ref[...], ta, tb),), finish)

    o_spec = pl.BlockSpec((tm, tn), lambda i, j, k: (i, j))
    in_specs = [_a_spec(ta, tm, tk), _b_spec(tb, tk, tn)]
    operands = [a, b]
    if residual is not None:
        in_specs.append(o_spec)
        operands.append(residual)
    (out,), carried = _call(
        body, name=name, grid=(m // tm, n // tn, nk), in_specs=in_specs, out_specs=[o_spec],
        out_shape=[jax.ShapeDtypeStruct((m, n), out_dtype)], operands=operands,
        scratch_shapes=[pltpu.VMEM((tm, tn), F32)] * (nk > 1),
        semantics=("parallel", "parallel", "arbitrary"), jobs=jobs)
    return (out, carried) if jobs else out


def _swiglu_fwd(n2, w_gate, w_up, *, tm=1024, tn=512, tk=2048, jobs=()):
    m, kdim = n2.shape
    n = w_gate.shape[1]
    tm, tn, tk = _tile(m, tm), _tile(n, tn), _tile(kdim, tk)
    nk = kdim // tk

    def body(a_ref, g_ref, u_ref, gate_ref, up_ref, act_ref, *acc):
        def products():
            a = a_ref[...]
            return _dot(a, g_ref[...]), _dot(a, u_ref[...])

        def finish(g, u):
            gate_ref[...] = g.astype(BF16)
            up_ref[...] = u.astype(BF16)
            act_ref[...] = (g * _sigmoid(g) * u).astype(BF16)

        _accumulate(acc, nk, products, finish)

    o_spec = pl.BlockSpec((tm, tn), lambda i, j, k: (i, j))
    o_shape = jax.ShapeDtypeStruct((m, n), BF16)
    return _call(
        body, name="swiglu_fwd", grid=(m // tm, n // tn, nk),
        in_specs=[_a_spec(False, tm, tk), _b_spec(False, tk, tn), _b_spec(False, tk, tn)],
        out_specs=[o_spec] * 3, out_shape=[o_shape] * 3, operands=[n2, w_gate, w_up],
        scratch_shapes=[pltpu.VMEM((tm, tn), F32)] * (2 * (nk > 1)),
        semantics=("parallel", "parallel", "arbitrary"), jobs=jobs)


def _swiglu_bwd_act(dh2, w_down, gate, up, *, tm=1024, tn=512, tk=2048):
    m, kdim = dh2.shape
    n = w_down.shape[0]
    tm, tn, tk = _tile(m, tm), _tile(n, tn), _tile(kdim, tk)
    nk = kdim // tk

    def body(a_ref, b_ref, gate_ref, up_ref, dgate_ref, dup_ref, *acc):
        def finish(dact):
            g = gate_ref[...].astype(F32)
            u = up_ref[...].astype(F32)
            sg = _sigmoid(g)
            dup_ref[...] = (dact * g * sg).astype(BF16)
            dgate_ref[...] = (dact * u * sg * (1.0 + g * (1.0 - sg))).astype(BF16)

        _accumulate(acc, nk, lambda: (_dot(a_ref[...], b_ref[...], tb=True),), finish)

    o_spec = pl.BlockSpec((tm, tn), lambda i, j, k: (i, j))
    o_shape = jax.ShapeDtypeStruct((m, n), BF16)
    return pl.pallas_call(
        body, name="swiglu_bwd_act", grid=(m // tm, n // tn, nk),
        in_specs=[_a_spec(False, tm, tk), _b_spec(True, tk, tn), o_spec, o_spec],
        out_specs=[o_spec] * 2, out_shape=[o_shape] * 2,
        scratch_shapes=[pltpu.VMEM((tm, tn), F32)] * (nk > 1),
        compiler_params=_params(("parallel", "parallel", "arbitrary")),
    )(dh2, w_down, gate, up)


def _swiglu_bwd_in(dgate, dup, w_gate, w_up, *, tm=1024, tn=1024, tk=1408, jobs=()):
    m, kdim = dgate.shape
    n = w_gate.shape[0]
    tm, tn, tk = _tile(m, tm), _tile(n, tn), _tile(kdim, tk)
    nk = kdim // tk

    def body(a1_ref, a2_ref, b1_ref, b2_ref, o_ref, *acc):
        def product():
            return (_dot(a1_ref[...], b1_ref[...], tb=True) + _dot(a2_ref[...], b2_ref[...], tb=True),)

        def finish(total):
            o_ref[...] = total

        _accumulate(acc, nk, product, finish)

    a_spec, b_spec = _a_spec(False, tm, tk), _b_spec(True, tk, tn)
    (out,), carried = _call(
        body, name="swiglu_bwd_in", grid=(m // tm, n // tn, nk),
        in_specs=[a_spec, a_spec, b_spec, b_spec],
        out_specs=[pl.BlockSpec((tm, tn), lambda i, j, k: (i, j))],
        out_shape=[jax.ShapeDtypeStruct((m, n), F32)], operands=[dgate, dup, w_gate, w_up],
        scratch_shapes=[pltpu.VMEM((tm, tn), F32)] * (nk > 1),
        semantics=("parallel", "parallel", "arbitrary"), jobs=jobs)
    return out, carried


def _row_block(rows, cols):
    tr = min(rows, max(16, ELEM_BLOCK_BYTES // (4 * cols) // 16 * 16))
    while rows % tr:
        tr -= 16
    return tr


def _rmsnorm_fwd(x, g, *, name):
    s, d = x.shape
    tr = _row_block(s, d)

    def body(x_ref, g_ref, n_ref):
        xv = x_ref[...]
        r = lax.rsqrt(jnp.mean(xv * xv, axis=-1, keepdims=True) + EPS)
        n_ref[...] = (xv * r * g_ref[...]).astype(BF16)

    row = pl.BlockSpec((tr, d), lambda i: (i, 0))
    return pl.pallas_call(
        body, name=name, grid=(s // tr,), in_specs=[row, pl.BlockSpec((1, d), lambda i: (0, 0))],
        out_specs=row, out_shape=jax.ShapeDtypeStruct((s, d), BF16),
        compiler_params=_params(("parallel",)),
    )(x, g)


def _rmsnorm_bwd_rows(xv, gv, dy):
    r = lax.rsqrt(jnp.mean(xv * xv, axis=-1, keepdims=True) + EPS)
    xhat = xv * r
    dxh = dy * gv
    dx = r * (dxh - xhat * jnp.mean(dxh * xhat, axis=-1, keepdims=True))
    return dx, dy * xhat


def _rmsnorm_bwd(dn, x, g, skip, *, name):
    s, d = x.shape
    tr = _row_block(s, d)

    def body(dn_ref, x_ref, g_ref, skip_ref, dx_ref, dxb_ref, dg_ref):
        dx, dgr = _rmsnorm_bwd_rows(x_ref[...], g_ref[...], dn_ref[...])
        dx = dx + skip_ref[...]
        dx_ref[...] = dx
        dxb_ref[...] = dx.astype(BF16)

        @pl.when(pl.program_id(0) == 0)
        def _():
            dg_ref[...] = jnp.zeros_like(dg_ref)

        dg_ref[...] += jnp.sum(dgr, axis=0, keepdims=True)

    row = pl.BlockSpec((tr, d), lambda i: (i, 0))
    vec = pl.BlockSpec((1, d), lambda i: (0, 0))
    return pl.pallas_call(
        body, name=name, grid=(s // tr,), in_specs=[row, row, vec, row],
        out_specs=[row, row, vec],
        out_shape=[jax.ShapeDtypeStruct((s, d), F32), jax.ShapeDtypeStruct((s, d), BF16),
                   jax.ShapeDtypeStruct((1, d), F32)],
        compiler_params=_params(("arbitrary",)),
    )(dn, x, g, skip)


def _loss_head(h2, g, target):
    s, d = h2.shape
    tr = _row_block(s, d)

    def body(h_ref, g_ref, t_ref, dh_ref, dhb_ref, dg_ref, loss_ref):
        hv = h_ref[...]
        gv = g_ref[...]
        r = lax.rsqrt(jnp.mean(hv * hv, axis=-1, keepdims=True) + EPS)
        err = hv * r * gv - t_ref[...]
        dx, dgr = _rmsnorm_bwd_rows(hv, gv, err * (1.0 / d))
        dh_ref[...] = dx
        dhb_ref[...] = dx.astype(BF16)

        @pl.when(pl.program_id(0) == 0)
        def _():
            dg_ref[...] = jnp.zeros_like(dg_ref)
            loss_ref[...] = jnp.zeros_like(loss_ref)

        dg_ref[...] += jnp.sum(dgr, axis=0, keepdims=True)
        row_loss = jnp.mean(err * err, axis=-1, keepdims=True)
        loss_ref[...] += 0.5 * jnp.sum(row_loss, axis=0, keepdims=True)

    row = pl.BlockSpec((tr, d), lambda i: (i, 0))
    vec = pl.BlockSpec((1, d), lambda i: (0, 0))
    one = pl.BlockSpec((1, 1), lambda i: (0, 0))
    return pl.pallas_call(
        body, name="loss_head", grid=(s // tr,), in_specs=[row, vec, row],
        out_specs=[row, row, vec, one],
        out_shape=[jax.ShapeDtypeStruct((s, d), F32), jax.ShapeDtypeStruct((s, d), BF16),
                   jax.ShapeDtypeStruct((1, d), F32), jax.ShapeDtypeStruct((1, 1), F32)],
        compiler_params=_params(("arbitrary",)),
    )(h2, g, target)


def _attention_bias_tables():
    k = np.arange(-ATT_KB, ATT_KB + 1)[:, None, None]
    delta = k * ATT_BLOCK + np.arange(ATT_BLOCK)[None, None, :] - np.arange(ATT_BLOCK)[None, :, None]
    dist = np.abs(delta)
    count = np.zeros(delta.shape, np.int32)
    for window, dilation in DILATED_PATTERNS:
        count += (delta % dilation == 0) & (dist <= window // 2)
    logc = np.where(count > 0, np.log(np.maximum(count, 1)), MASKED)
    return dist.astype(np.float32), logc.astype(np.float32)


def _head_bias(bias_ref, slope, dist_ref, logc_ref):
    for kk in range(ATT_WINDOW):
        bias_ref[kk] = logc_ref[kk] - slope * dist_ref[kk]
    bias_ref[ATT_WINDOW] = jnp.full((ATT_BLOCK, ATT_BLOCK), MASKED, F32)


def _window_start(i, nq, nwin):
    return jnp.clip(i - ATT_KB, 0, nq - nwin)


def _window_block(j, i):
    rows = pl.ds(pl.multiple_of(j * ATT_BLOCK, ATT_BLOCK), ATT_BLOCK)
    kk = j - i + ATT_KB
    return rows, jnp.where(jnp.logical_and(kk >= 0, kk < ATT_WINDOW), kk, ATT_WINDOW)


def _attention_fwd(proj, slopes, n_heads, jobs=()):
    s = proj.shape[0]
    nq = s // ATT_BLOCK
    scale = HEAD_DIM ** -0.5
    dist, logc = _attention_bias_tables()

    nwin = min(ATT_WINDOW, nq)

    def body(slope_ref, q_ref, k_ref, v_ref, dist_ref, logc_ref, o_ref, lse_ref, bias_ref, s_ref):
        h, i = pl.program_id(0), pl.program_id(1)

        @pl.when(i == 0)
        def _():
            _head_bias(bias_ref, slope_ref[h], dist_ref, logc_ref)

        q = q_ref[...]
        first = _window_start(i, nq, nwin)
        m = jnp.full((ATT_BLOCK, 1), ROW_MAX_INIT, F32)
        for b in range(nwin):
            rows, kk = _window_block(first + b, i)
            sc = _dot(q, k_ref[rows, :], tb=True) * scale + bias_ref[kk]
            s_ref[b] = sc
            m = jnp.maximum(m, jnp.max(sc, axis=-1, keepdims=True))
        l = jnp.zeros((ATT_BLOCK, 1), F32)
        acc = jnp.zeros((ATT_BLOCK, HEAD_DIM), F32)
        for b in range(nwin):
            rows, _ = _window_block(first + b, i)
            p = jnp.exp(s_ref[b] - m)
            l = l + jnp.sum(p, axis=-1, keepdims=True)
            acc = acc + _dot(p.astype(BF16), v_ref[rows, :])
        o_ref[...] = (acc / l).astype(BF16)
        lse_ref[...] = jnp.broadcast_to(m + jnp.log(l), (ATT_BLOCK, HEAD_DIM))

    hh = n_heads
    blk = pl.BlockSpec((ATT_BLOCK, HEAD_DIM), lambda h, i: (i, h))
    table = pl.BlockSpec(dist.shape, lambda h, i: (0, 0, 0))
    return _call(
        body, name="attention_fwd", grid=(hh, nq),
        in_specs=[pl.BlockSpec(memory_space=pltpu.SMEM), blk,
                  pl.BlockSpec((s, HEAD_DIM), lambda h, i: (0, hh + h)),
                  pl.BlockSpec((s, HEAD_DIM), lambda h, i: (0, 2 * hh + h)), table, table],
        out_specs=[blk, blk],
        out_shape=[jax.ShapeDtypeStruct((s, hh * HEAD_DIM), BF16), jax.ShapeDtypeStruct((s, hh * HEAD_DIM), F32)],
        operands=[slopes, proj, proj, proj, jnp.asarray(dist), jnp.asarray(logc)],
        scratch_shapes=[pltpu.VMEM((ATT_WINDOW + 1, ATT_BLOCK, ATT_BLOCK), F32),
                        pltpu.VMEM((nwin, ATT_BLOCK, ATT_BLOCK), F32)],
        semantics=("parallel", "arbitrary"), jobs=jobs)


def _attention_bwd(proj, slopes, out, lse, dmixed, n_heads, jobs=()):
    s = proj.shape[0]
    nq = s // ATT_BLOCK
    scale = HEAD_DIM ** -0.5
    dist, logc = _attention_bias_tables()

    nwin = min(ATT_WINDOW, nq)

    def body(slope_ref, q_ref, k_ref, v_ref, o_ref, do_ref, lse_ref, dist_ref, logc_ref,
             dq_ref, dk_ref, dv_ref, dk_acc, dv_acc, bias_ref):
        h, i = pl.program_id(0), pl.program_id(1)

        @pl.when(i == 0)
        def _():
            dk_acc[...] = jnp.zeros_like(dk_acc)
            dv_acc[...] = jnp.zeros_like(dv_acc)
            _head_bias(bias_ref, slope_ref[h], dist_ref, logc_ref)

        q = q_ref[...]
        do = do_ref[...]
        lse_col = lse_ref[:, :1]
        delta = jnp.sum(do.astype(F32) * o_ref[...].astype(F32), axis=-1, keepdims=True)
        first = _window_start(i, nq, nwin)
        dq = jnp.zeros((ATT_BLOCK, HEAD_DIM), F32)
        for b in range(nwin):
            rows, kk = _window_block(first + b, i)
            kj = k_ref[rows, :]
            vj = v_ref[rows, :]
            p = jnp.exp(_dot(q, kj, tb=True) * scale + bias_ref[kk] - lse_col)
            dv_acc[rows, :] += _dot(p.astype(BF16), do, ta=True)
            dp = _dot(do, vj, tb=True)
            ds = (p * (dp - delta) * scale).astype(BF16)
            dk_acc[rows, :] += _dot(ds, q, ta=True)
            dq = dq + _dot(ds, kj)
        dq_ref[...] = dq.astype(BF16)

        @pl.when(i == nq - 1)
        def _():
            dk_ref[...] = dk_acc[...].astype(BF16)
            dv_ref[...] = dv_acc[...].astype(BF16)

    hh = n_heads
    blk = pl.BlockSpec((ATT_BLOCK, HEAD_DIM), lambda h, i: (i, h))
    col = pl.BlockSpec((s, HEAD_DIM), lambda h, i: (0, h))
    table = pl.BlockSpec(dist.shape, lambda h, i: (0, 0, 0))
    o_shape = jax.ShapeDtypeStruct((s, hh * HEAD_DIM), BF16)
    return _call(
        body, name="attention_bwd", grid=(hh, nq),
        in_specs=[pl.BlockSpec(memory_space=pltpu.SMEM), blk,
                  pl.BlockSpec((s, HEAD_DIM), lambda h, i: (0, hh + h)),
                  pl.BlockSpec((s, HEAD_DIM), lambda h, i: (0, 2 * hh + h)),
                  blk, blk, blk, table, table],
        out_specs=[blk, col, col], out_shape=[o_shape] * 3,
        operands=[slopes, proj, proj, proj, out, dmixed, lse, jnp.asarray(dist), jnp.asarray(logc)],
        scratch_shapes=[pltpu.VMEM((s, HEAD_DIM), F32)] * 2
        + [pltpu.VMEM((ATT_WINDOW + 1, ATT_BLOCK, ATT_BLOCK), F32)],
        semantics=("parallel", "arbitrary"), jobs=jobs)


def _ret_decays(lgc, lga, strict_c, strict_a):
    c = RET_CHUNK
    rel = (lax.broadcasted_iota(jnp.int32, (c, c), 0) - lax.broadcasted_iota(jnp.int32, (c, c), 1)).astype(F32)
    in_c = (rel > 0) if strict_c else (rel >= 0)
    in_a = (rel < 0) if strict_a else (rel <= 0)
    mask = (jnp.where(in_c, jnp.exp(lgc * jnp.maximum(rel, 0.0)), 0.0)
            + jnp.where(in_a, jnp.exp(lga * jnp.maximum(-rel, 0.0)), 0.0))
    idx = lax.broadcasted_iota(jnp.int32, (c, 1), 0).astype(F32)
    ones = jnp.ones((1, HEAD_DIM), F32)
    dec = dict(
        rel=rel, mask=mask, idx=idx,
        a_c=jnp.exp(lgc * (idx + 1.0)), b_c=jnp.exp(lgc * (c - 1.0 - idx)), chunk_c=jnp.exp(ones * (lgc * c)),
        a_a=jnp.exp(lga * (c - idx)), b_a=jnp.exp(lga * idx), chunk_a=jnp.exp(ones * (lga * c)),
    )
    return dec


def _scaled(x, col):
    return (x.astype(F32) * col).astype(BF16)


def _chunk_rows(i):
    return pl.ds(pl.multiple_of(i * RET_CHUNK, RET_CHUNK), RET_CHUNK)


def _chunk_loop(nc, step, init, unroll=RET_UNROLL):
    group = math.gcd(nc, unroll)

    def trip(t, carry):
        for u in range(group):
            carry = step(t * group + u, carry)
        return carry

    return lax.fori_loop(0, nc // group, trip, init)


def _retention(a, b, c, lg_c, lg_a, *, strict_c, strict_a, scale, n_heads, name, gate=None, norm_w=None, jobs=()):
    s = a[0].shape[0]
    nc = s // RET_CHUNK
    epilogue = gate is not None

    def body(*refs):
        lgc_ref, lga_ref, a_ref, b_ref, c_ref = refs[:5]
        if epilogue:
            g_ref, w_ref, o_ref, mix_ref, sa_ref = refs[5:]
        else:
            o_ref, sa_ref = refs[5:]
        h = pl.program_id(0)
        dec = _ret_decays(lgc_ref[h], lga_ref[h], strict_c, strict_a)

        def reverse(t, state):
            i = nc - 1 - t
            sa_ref[i] = state.astype(BF16)
            rows = _chunk_rows(i)
            return state * dec["chunk_a"] + _dot(_scaled(b_ref[rows, :], dec["b_a"]), c_ref[rows, :], ta=True)

        _chunk_loop(nc, reverse, jnp.zeros((HEAD_DIM, HEAD_DIM), F32))

        def forward(i, state):
            rows = _chunk_rows(i)
            ai, bi, ci = a_ref[rows, :], b_ref[rows, :], c_ref[rows, :]
            inner = (_dot(ai, bi, tb=True) * dec["mask"]).astype(BF16)
            out = (_dot(inner, ci) + _dot(_scaled(ai, dec["a_c"]), state.astype(BF16))
                   + _dot(_scaled(ai, dec["a_a"]), sa_ref[i])) * scale
            o_ref[rows, :] = out.astype(BF16)
            if epilogue:
                r = lax.rsqrt(jnp.mean(out * out, axis=-1, keepdims=True) + EPS)
                g = g_ref[rows, :].astype(F32)
                mix_ref[rows, :] = (out * r * w_ref[...] * (g * _sigmoid(g))).astype(BF16)
            return state * dec["chunk_c"] + _dot(_scaled(bi, dec["b_c"]), ci, ta=True)

        _chunk_loop(nc, forward, jnp.zeros((HEAD_DIM, HEAD_DIM), F32))

    def col(first):
        return pl.BlockSpec((s, HEAD_DIM), lambda h: (0, first + h))

    smem = pl.BlockSpec(memory_space=pltpu.SMEM)
    in_specs = [smem, smem, col(a[1]), col(b[1]), col(c[1])]
    operands = [lg_c, lg_a, a[0], b[0], c[0]]
    o_shape = jax.ShapeDtypeStruct((s, n_heads * HEAD_DIM), BF16)
    out_specs, out_shape = [col(0)], [o_shape]
    if epilogue:
        in_specs += [col(gate[1]), pl.BlockSpec((1, HEAD_DIM), lambda h: (0, h))]
        operands += [gate[0], norm_w]
        out_specs, out_shape = [col(0)] * 2, [o_shape] * 2
    res, carried = _call(
        body, name=name, grid=(n_heads,), in_specs=in_specs, out_specs=out_specs, out_shape=out_shape,
        operands=operands, scratch_shapes=[pltpu.VMEM((nc, HEAD_DIM, HEAD_DIM), BF16)],
        semantics=("parallel",), jobs=jobs)
    res = res if epilogue else res[0]
    return (res, carried) if jobs else res


def _retention_decay_grads(a, b, c, e, lg_c, lg_a, *, scale, n_heads):
    s = a[0].shape[0]
    nc = s // RET_CHUNK
    cf = float(RET_CHUNK)

    def body(lgc_ref, lga_ref, a_ref, b_ref, c_ref, e_ref, gc_ref, ga_ref, sa_ref, ta_ref):
        h = pl.program_id(0)
        lgc, lga = lgc_ref[h], lga_ref[h]
        dec = _ret_decays(lgc, lga, True, True)
        rel, idx = dec["rel"], dec["idx"]
        w_c = jnp.where(rel > 0, rel * jnp.exp(lgc * jnp.maximum(rel, 0.0)), 0.0)
        w_a = jnp.where(rel < 0, -rel * jnp.exp(lga * jnp.maximum(-rel, 0.0)), 0.0)
        zero = jnp.zeros((HEAD_DIM, HEAD_DIM), F32)

        def reverse(t, carry):
            st, dst = carry
            i = nc - 1 - t
            sa_ref[i] = st.astype(BF16)
            ta_ref[i] = dst.astype(BF16)
            rows = _chunk_rows(i)
            bi, ci = b_ref[rows, :], c_ref[rows, :]
            st_new = st * dec["chunk_a"] + _dot(_scaled(bi, dec["b_a"]), ci, ta=True)
            dst_new = (cf * st + dst) * dec["chunk_a"] + _dot(_scaled(bi, idx * dec["b_a"]), ci, ta=True)
            return st_new, dst_new

        _chunk_loop(nc, reverse, (zero, zero))

        def forward(i, carry):
            st, dst, acc_c, acc_a = carry
            rows = _chunk_rows(i)
            ai, bi, ci = a_ref[rows, :], b_ref[rows, :], c_ref[rows, :]
            ev = e_ref[rows, :].astype(F32)
            pg = _dot(ai, bi, tb=True) * _dot(e_ref[rows, :], ci, tb=True)
            a_c, a_a = _scaled(ai, dec["a_c"]), _scaled(ai, dec["a_a"])
            inter_c = _dot(a_c, st.astype(BF16)) * (idx + 1.0) + _dot(a_c, dst.astype(BF16))
            inter_a = _dot(a_a, sa_ref[i]) * (cf - idx) + _dot(a_a, ta_ref[i])
            acc_c = acc_c + jnp.sum(pg * w_c, axis=0, keepdims=True) + jnp.sum(inter_c * ev, axis=0, keepdims=True)
            acc_a = acc_a + jnp.sum(pg * w_a, axis=0, keepdims=True) + jnp.sum(inter_a * ev, axis=0, keepdims=True)
            st_new = st * dec["chunk_c"] + _dot(_scaled(bi, dec["b_c"]), ci, ta=True)
            dst_new = ((cf * st + dst) * dec["chunk_c"]
                       + _dot(_scaled(bi, (cf - 1.0 - idx) * dec["b_c"]), ci, ta=True))
            return st_new, dst_new, acc_c, acc_a

        row = jnp.zeros((1, HEAD_DIM), F32)
        _, _, acc_c, acc_a = _chunk_loop(nc, forward, (zero, zero, row, row))
        gc_ref[...] = jnp.broadcast_to(jnp.sum(acc_c, axis=-1, keepdims=True) * scale, gc_ref.shape)
        ga_ref[...] = jnp.broadcast_to(jnp.sum(acc_a, axis=-1, keepdims=True) * scale, ga_ref.shape)

    def col(first):
        return pl.BlockSpec((s, HEAD_DIM), lambda h: (0, first + h))

    smem = pl.BlockSpec(memory_space=pltpu.SMEM)
    o_spec = pl.BlockSpec((1, 8, HEAD_DIM), lambda h: (h, 0, 0))
    o_shape = jax.ShapeDtypeStruct((n_heads, 8, HEAD_DIM), F32)
    gc, ga = pl.pallas_call(
        body, name="retention_decay_grads", grid=(n_heads,),
        in_specs=[smem, smem, col(a[1]), col(b[1]), col(c[1]), col(e[1])],
        out_specs=[o_spec] * 2, out_shape=[o_shape] * 2,
        scratch_shapes=[pltpu.VMEM((nc, HEAD_DIM, HEAD_DIM), BF16)] * 2,
        compiler_params=_params(("parallel",)),
    )(lg_c, lg_a, a[0], b[0], c[0], e[0])
    return gc[:, 0, 0], ga[:, 0, 0]


def _ret_gate_bwd(dmixed, first_col, out, proj, gate_col, norm_w, n_heads):
    s = out.shape[0]
    tr = _row_block(s, 8 * HEAD_DIM)

    def body(dm_ref, o_ref, g_ref, w_ref, do_ref, dg_ref, dw_ref):
        dm = dm_ref[...].astype(F32)
        ov = o_ref[...].astype(F32)
        g = g_ref[...].astype(F32)
        w = w_ref[...]
        r = lax.rsqrt(jnp.mean(ov * ov, axis=-1, keepdims=True) + EPS)
        ohat = ov * r
        sg = _sigmoid(g)
        silu = g * sg
        dg_ref[...] = (dm * ohat * w * sg * (1.0 + g * (1.0 - sg))).astype(BF16)
        dohat = dm * w * silu
        do_ref[...] = (r * (dohat - ohat * jnp.mean(dohat * ohat, axis=-1, keepdims=True))).astype(BF16)

        @pl.when(pl.program_id(1) == 0)
        def _():
            dw_ref[...] = jnp.zeros_like(dw_ref)

        dw_ref[...] += jnp.sum(dm * ohat * silu, axis=0, keepdims=True)

    def blk(first):
        return pl.BlockSpec((tr, HEAD_DIM), lambda h, i: (i, first + h))

    vec = pl.BlockSpec((1, HEAD_DIM), lambda h, i: (0, h))
    o_shape = jax.ShapeDtypeStruct((s, n_heads * HEAD_DIM), BF16)
    return pl.pallas_call(
        body, name="ret_gate_bwd", grid=(n_heads, s // tr),
        in_specs=[blk(first_col), blk(0), blk(gate_col), vec],
        out_specs=[blk(0), blk(0), vec],
        out_shape=[o_shape, o_shape, jax.ShapeDtypeStruct((1, n_heads * HEAD_DIM), F32)],
        compiler_params=_params(("parallel", "arbitrary")),
    )(dmixed, out, proj, norm_w)


def _step(x, target, norm_mix_w, ret_decay_fwd, ret_decay_bwd, ret_norm_w, norm_ffn_w, norm_final_w, own, pos):
    d = x.shape[1]
    nh = d // (2 * HEAD_DIM)
    scale = HEAD_DIM ** -0.5
    slopes = jnp.exp2(-8.0 * jnp.arange(1, nh + 1, dtype=F32) / nh)
    lg_f = -jnp.exp(ret_decay_fwd)
    lg_b = -jnp.exp(ret_decay_bwd)
    q_r, k_r, v_r, g_r = 3 * nh, 4 * nh, 5 * nh, 6 * nh
    ax = BIG_AXIS

    def gather(names, arrays, stage):
        return _gather_job(arrays, [ax[k] for k in names], stage)

    def add_halves(k, g, received):
        return _add_halves(g, received, ax[k], pos, name="grad_add_halves_" + k)

    def sum_parts(k, g, received, parts):
        return _sum_chip_parts(g, received, parts, ax[k], pos, name="grad_sum_parts_" + k)

    (w_in,) = _run_jobs([gather(["w_in"], [own["w_in"]], "ici"), gather(["w_in"], [own["w_in"]], "d2d")],
                        name="all_gather_w_in")
    n1 = _rmsnorm_fwd(x, norm_mix_w, name="norm_mix_fwd")
    proj, [[w_gate]] = _matmul(n1, w_in, name="in_proj", out_dtype=BF16,
                               jobs=[gather(["w_gate"], [own["w_gate"]], "ici")])
    (attn, lse), [[w_gate], [w_out, w_up]] = _attention_fwd(
        proj, slopes, nh,
        jobs=[gather(["w_gate"], [w_gate], "d2d"), gather(["w_out", "w_up"], [own["w_out"], own["w_up"]], "ici")])
    (ret, ret_mixed), [[w_out, w_up]] = _retention(
        (proj, q_r), (proj, k_r), (proj, v_r), lg_f, lg_b, strict_c=False, strict_a=True, scale=scale, n_heads=nh,
        name="retention_fwd", gate=(proj, g_r), norm_w=ret_norm_w,
        jobs=[gather(["w_out", "w_up"], [w_out, w_up], "d2d")])
    mixed = jnp.concatenate([attn, ret_mixed], axis=1)
    h1 = _matmul(mixed, w_out, name="out_proj", residual=x)
    n2 = _rmsnorm_fwd(h1, norm_ffn_w, name="norm_ffn_fwd")
    (gate, up, act), [[w_down]] = _swiglu_fwd(n2, w_gate, w_up, jobs=[gather(["w_down"], [own["w_down"]], "ici")])
    (w_down,) = _run_jobs([gather(["w_down"], [w_down], "d2d")], name="all_gather_w_down_sibling")
    h2 = _matmul(act, w_down, name="down_proj", residual=h1)
    dh2, dh2_b, d_norm_final, loss = _loss_head(h2, norm_final_w, target)

    dgate, dup = _swiglu_bwd_act(dh2_b, w_down, gate, up)
    n2_t = jnp.transpose(n2)
    g_down = _matmul(jnp.transpose(act), dh2_b, name="grad_w_down")
    g_gate, [[r_down]] = _matmul(n2_t, dgate, name="grad_w_gate", jobs=[_exchange_job([g_down], [ax["w_down"]])])
    s_down = add_halves("w_down", g_down, r_down)
    g_up, [[r_gate], [p_down]] = _matmul(
        n2_t, dup, name="grad_w_up",
        jobs=[_exchange_job([g_gate], [ax["w_gate"]]), _send_sums_job([s_down], [ax["w_down"]])])
    s_gate = add_halves("w_gate", g_gate, r_gate)
    h_down = sum_parts("w_down", g_down, r_down, p_down)
    dn2, [[r_up], [p_gate]] = _swiglu_bwd_in(
        dgate, dup, w_gate, w_up,
        jobs=[_exchange_job([g_up], [ax["w_up"]]), _send_sums_job([s_gate], [ax["w_gate"]])])
    s_up = add_halves("w_up", g_up, r_up)
    h_gate = sum_parts("w_gate", g_gate, r_gate, p_gate)
    dh1, dh1_b, d_norm_ffn = _rmsnorm_bwd(dn2, h1, norm_ffn_w, dh2, name="norm_ffn_bwd")

    dmixed, [[gr_down]] = _matmul(dh1_b, w_out, name="out_proj_bwd", tb=True, out_dtype=BF16,
                                  jobs=[_join_job([h_down], [ax["w_down"]])])
    g_out = _matmul(jnp.transpose(mixed), dh1_b, name="grad_w_out")
    d_ret, dg_r, d_ret_norm = _ret_gate_bwd(dmixed, nh, ret, proj, g_r, ret_norm_w, nh)
    (dq_a, dk_a, dv_a), [[p_up], [r_out]] = _attention_bwd(
        proj, slopes, attn, lse, dmixed, nh,
        jobs=[_send_sums_job([s_up], [ax["w_up"]]), _exchange_job([g_out], [ax["w_out"]])])
    s_out = add_halves("w_out", g_out, r_out)
    h_up = sum_parts("w_up", g_up, r_up, p_up)
    dq_r, [[p_out], [gr_gate, gr_up]] = _retention(
        (d_ret, 0), (proj, v_r), (proj, k_r), lg_f, lg_b, strict_c=False, strict_a=True, scale=scale, n_heads=nh,
        name="retention_dq",
        jobs=[_send_sums_job([s_out], [ax["w_out"]]), _join_job([h_gate, h_up], [ax["w_gate"], ax["w_up"]])])
    h_out = sum_parts("w_out", g_out, r_out, p_out)
    dv_r, [[gr_out]] = _retention(
        (proj, k_r), (proj, q_r), (d_ret, 0), lg_b, lg_f, strict_c=True, strict_a=False, scale=scale, n_heads=nh,
        name="retention_dv", jobs=[_join_job([h_out], [ax["w_out"]])])
    dk_r = _retention((proj, v_r), (d_ret, 0), (proj, q_r), lg_b, lg_f, strict_c=True, strict_a=False,
                      scale=scale, n_heads=nh, name="retention_dk")
    dlg_f, dlg_b = _retention_decay_grads((proj, q_r), (proj, k_r), (proj, v_r), (d_ret, 0), lg_f, lg_b,
                                          scale=scale, n_heads=nh)
    dproj = jnp.concatenate([dq_a, dk_a, dv_a, dq_r, dk_r, dv_r, dg_r], axis=1)
    g_in = _matmul(jnp.transpose(n1), dproj, name="grad_w_in")
    (r_in,) = _run_jobs([_exchange_job([g_in], [ax["w_in"]])], name="grad_exchange_w_in")
    s_in = add_halves("w_in", g_in, r_in)
    dn1, [[p_in]] = _matmul(dproj, w_in, name="in_proj_bwd", tb=True, jobs=[_send_sums_job([s_in], [ax["w_in"]])])
    dx, _, d_norm_mix = _rmsnorm_bwd(dn1, x, norm_mix_w, dh1, name="norm_mix_bwd")
    h_in = sum_parts("w_in", g_in, r_in, p_in)
    (gr_in,) = _run_jobs([_join_job([h_in], [ax["w_in"]])], name="grad_join_w_in")

    small = dict(loss=loss[0, 0], norm_mix_w=d_norm_mix, ret_decay_fwd=dlg_f * lg_f, ret_decay_bwd=dlg_b * lg_b,
                 ret_norm_w=d_ret_norm, norm_ffn_w=d_norm_ffn, norm_final_w=d_norm_final)
    return dx, dict(w_in=gr_in, w_out=gr_out, w_gate=gr_gate, w_up=gr_up, w_down=gr_down), small


def _mesh_position():
    x, y, c = lax.axis_index("x"), lax.axis_index("y"), lax.axis_index("c")
    chips = [(1 - x, y), (x, 1 - y), (1 - x, 1 - y)]
    return x, y, c, chips


def _ds(start, size):
    if isinstance(start, int):
        return pl.ds(start, size)
    return pl.ds(pl.multiple_of(start * size, size), size)


def _region(ref, axis, shard, half, shard_size, half_size):
    along = slice(None) if shard is None else _ds(shard, shard_size)
    other = slice(None) if half is None else _ds(half, half_size)
    return ref.at[other, along] if axis == 1 else ref.at[along, other]


def _gather_job(full, axes, stage):
    n = len(full)

    def copies(refs, sems):
        send_sem, recv_sem = sems
        x, y, c, chips = _mesh_position()
        me = 2 * x + y

        def copy(w, k, shard, half, target):
            rows_cols = full[w].shape
            place = _region(refs[w], axes[w], shard, half, rows_cols[axes[w]] // N_CHIPS, rows_cols[1 - axes[w]] // 2)
            return pltpu.make_async_remote_copy(
                src_ref=place, dst_ref=place, send_sem=send_sem.at[w, k], recv_sem=recv_sem.at[w, k],
                device_id=target, device_id_type=MESH)

        def sent(w, k):
            if stage == "ici":
                return copy(w, k, me, c, (chips[k][0], chips[k][1], c))
            return copy(w, k, 2 * chips[k][0] + chips[k][1], c, (x, y, 1 - c))

        def landed(w, k):
            return copy(w, k, 2 * chips[k][0] + chips[k][1], c if stage == "ici" else 1 - c, (x, y, 1 - c))

        return sent, landed

    def start(refs, sems):
        sent, _ = copies(refs, sems)
        for w in range(n):
            for k in range(3):
                sent(w, k).start()

    def finish(refs, sems):
        sent, landed = copies(refs, sems)
        for w in range(n):
            for k in range(3):
                landed(w, k).wait_recv()
                sent(w, k).wait_send()

    return _Job(ios=full, sems=[pltpu.SemaphoreType.DMA((n, 3))] * 2, start=start, finish=finish)


def _exchange_job(grads, axes):
    n = len(grads)

    def half_shape(w):
        return tuple(d // 2 if a != axes[w] else d for a, d in enumerate(grads[w].shape))

    def copy(refs, sems, w):
        x, y, c, _ = _mesh_position()
        return pltpu.make_async_remote_copy(
            src_ref=_region(refs[w], axes[w], None, 1 - c, 0, half_shape(w)[1 - axes[w]]), dst_ref=refs[n + w],
            send_sem=sems[0].at[w], recv_sem=sems[1].at[w], device_id=(x, y, 1 - c), device_id_type=MESH)

    def start(refs, sems):
        for w in range(n):
            copy(refs, sems, w).start()

    def finish(refs, sems):
        for w in range(n):
            copy(refs, sems, w).wait()

    return _Job(ins=grads, outs=[jax.ShapeDtypeStruct(half_shape(w), F32) for w in range(n)],
                sems=[pltpu.SemaphoreType.DMA((n,))] * 2, start=start, finish=finish)


def _half_block_spec(axis, block, half_blocks, use_half):
    if axis == 1:
        if use_half:
            return pl.BlockSpec(block, lambda i, pos: (pos[0] * half_blocks + i, 0))
        return pl.BlockSpec(block, lambda i, pos: (i, 0))
    if use_half:
        return pl.BlockSpec(block, lambda i, pos: (i, pos[0]))
    return pl.BlockSpec(block, lambda i, pos: (i, 0))


def _add_halves(grad, received, axis, pos, *, name):
    rows, cols = received.shape
    tr = _row_block(rows, cols)
    nb = rows // tr

    def body(pos_ref, g_ref, r_ref, o_ref):
        o_ref[...] = (g_ref[...] + r_ref[...]).astype(BF16)

    blk = (tr, cols)
    return pl.pallas_call(
        body, name=name, out_shape=jax.ShapeDtypeStruct((rows, cols), BF16),
        grid_spec=pltpu.PrefetchScalarGridSpec(
            num_scalar_prefetch=1, grid=(nb,),
            in_specs=[_half_block_spec(axis, blk, nb, True), _half_block_spec(axis, blk, nb, False)],
            out_specs=_half_block_spec(axis, blk, nb, False)),
        compiler_params=_params(("parallel",)),
    )(pos, grad, received)


def _send_sums_job(sums, axes):
    n = len(sums)

    def part_shape(w):
        return tuple(d // N_CHIPS if a == axes[w] else d for a, d in enumerate(sums[w].shape))

    def copy(refs, sems, w, k):
        x, y, c, chips = _mesh_position()
        shard = 2 * chips[k][0] + chips[k][1]
        return pltpu.make_async_remote_copy(
            src_ref=_region(refs[w], axes[w], shard, None, part_shape(w)[axes[w]], 0), dst_ref=refs[n + w].at[k],
            send_sem=sems[0].at[w, k], recv_sem=sems[1].at[w, k],
            device_id=(chips[k][0], chips[k][1], c), device_id_type=MESH)

    def start(refs, sems):
        for w in range(n):
            for k in range(3):
                copy(refs, sems, w, k).start()

    def finish(refs, sems):
        for w in range(n):
            for k in range(3):
                copy(refs, sems, w, k).wait()

    return _Job(ins=sums, outs=[jax.ShapeDtypeStruct((3,) + part_shape(w), BF16) for w in range(n)],
                sems=[pltpu.SemaphoreType.DMA((n, 3))] * 2, start=start, finish=finish)


def _sum_chip_parts(grad, received, parts, axis, pos, *, name):
    _, rows, cols = parts.shape
    tr = _row_block(rows, cols)
    nb = rows // tr
    blk = (tr, cols)

    def body(pos_ref, g_ref, r_ref, p_ref, o_ref):
        total = g_ref[...] + r_ref[...]
        for k in range(3):
            total = total + p_ref[k].astype(F32)
        o_ref[...] = total

    if axis == 1:
        g_spec = pl.BlockSpec(blk, lambda i, pos: (pos[0] * nb + i, pos[1]))
        r_spec = pl.BlockSpec(blk, lambda i, pos: (i, pos[1]))
        o_spec = pl.BlockSpec(blk, lambda i, pos: (pos[0] * nb + i, 0))
        shard_shape = (2 * rows, cols)
    else:
        g_spec = pl.BlockSpec(blk, lambda i, pos: (pos[1] * nb + i, pos[0]))
        r_spec = pl.BlockSpec(blk, lambda i, pos: (pos[1] * nb + i, 0))
        o_spec = pl.BlockSpec(blk, lambda i, pos: (i, pos[0]))
        shard_shape = (rows, 2 * cols)
    return pl.pallas_call(
        body, name=name, out_shape=jax.ShapeDtypeStruct(shard_shape, F32),
        grid_spec=pltpu.PrefetchScalarGridSpec(
            num_scalar_prefetch=1, grid=(nb,),
            in_specs=[g_spec, r_spec, pl.BlockSpec((3,) + blk, lambda i, pos: (0, i, 0))],
            out_specs=o_spec),
        compiler_params=_params(("parallel",)),
    )(pos, grad, received, parts)


def _join_job(shards, axes):
    n = len(shards)

    def copy(refs, sems, w, other):
        x, y, c, _ = _mesh_position()
        place = _region(refs[w], axes[w], None, 1 - c if other else c, 0, shards[w].shape[1 - axes[w]] // 2)
        return pltpu.make_async_remote_copy(
            src_ref=place, dst_ref=place, send_sem=sems[0].at[w], recv_sem=sems[1].at[w],
            device_id=(x, y, 1 - c), device_id_type=MESH)

    def start(refs, sems):
        for w in range(n):
            copy(refs, sems, w, False).start()

    def finish(refs, sems):
        for w in range(n):
            copy(refs, sems, w, True).wait_recv()
            copy(refs, sems, w, False).wait_send()

    return _Job(ios=shards, sems=[pltpu.SemaphoreType.DMA((n,))] * 2, start=start, finish=finish)


def _all_reduce_small(vec):
    rows, cols = vec.shape

    def body(v_ref, o_ref, land_ref, send_sem, recv_sem):
        x, y, c, _ = _mesh_position()
        me = 4 * x + 2 * y + c
        land_ref[me] = v_ref[...]
        copies = []
        for k in range(1, 8):
            px, py, pc = x ^ (k >> 2), y ^ ((k >> 1) & 1), c ^ (k & 1)
            copies.append(pltpu.make_async_remote_copy(
                src_ref=v_ref, dst_ref=land_ref.at[me], send_sem=send_sem.at[k], recv_sem=recv_sem.at[k],
                device_id=(px, py, pc), device_id_type=MESH))
        for cp in copies:
            cp.start()
        for k in range(1, 8):
            peer = me ^ k
            pltpu.make_async_remote_copy(
                src_ref=v_ref, dst_ref=land_ref.at[peer], send_sem=send_sem.at[k], recv_sem=recv_sem.at[k],
                device_id=(x, y, c), device_id_type=MESH).wait_recv()
        for cp in copies:
            cp.wait_send()
        total = land_ref[0]
        for k in range(1, 8):
            total = total + land_ref[k]
        o_ref[...] = total

    vmem = pl.BlockSpec(memory_space=pltpu.VMEM)
    return pl.pallas_call(
        body, name="all_reduce_small", in_specs=[vmem], out_specs=vmem,
        out_shape=jax.ShapeDtypeStruct((rows, cols), F32),
        scratch_shapes=[pltpu.VMEM((8, rows, cols), F32), pltpu.SemaphoreType.DMA((8,)), pltpu.SemaphoreType.DMA((8,))],
    )(vec)


def _adamw(w, g, m, v, *, name):
    rows, cols = w.shape
    tr = _row_block(rows, cols) if rows % 8 == 0 else rows
    bc1 = 1.0 - ADAM_B1 ** ADAM_STEP
    bc2 = 1.0 - ADAM_B2 ** ADAM_STEP

    def body(w_ref, g_ref, m_ref, v_ref, d_ref, mo_ref, vo_ref):
        gv = g_ref[...]
        mn = ADAM_B1 * m_ref[...] + (1.0 - ADAM_B1) * gv
        vn = ADAM_B2 * v_ref[...] + (1.0 - ADAM_B2) * (gv * gv)
        mo_ref[...] = mn
        vo_ref[...] = vn
        d_ref[...] = -ADAM_LR * ((mn / bc1) / (jnp.sqrt(vn / bc2) + ADAM_EPS) + ADAM_WD * w_ref[...])

    blk = pl.BlockSpec((tr, cols), lambda i: (i, 0))
    shape = jax.ShapeDtypeStruct((rows, cols), F32)
    return pl.pallas_call(
        body, name=name, grid=(rows // tr,), in_specs=[blk] * 4, out_specs=[blk] * 3, out_shape=[shape] * 3,
        compiler_params=_params(("parallel",)),
    )(w, g, m, v)


def _to_bf16_in_place(w, axis, pos, *, name):
    rows, cols = w.shape
    tr = _row_block(rows, cols)
    nb = rows // tr

    def body(pos_ref, w_ref, o_ref):
        o_ref[...] = w_ref[...].astype(BF16)

    if axis == 1:
        o_spec = pl.BlockSpec((tr, cols), lambda i, pos: (i, pos[1]))
        full_shape = (rows, N_CHIPS * cols)
    else:
        o_spec = pl.BlockSpec((tr, cols), lambda i, pos: (pos[1] * nb + i, 0))
        full_shape = (N_CHIPS * rows, cols)
    return pl.pallas_call(
        body, name=name, out_shape=jax.ShapeDtypeStruct(full_shape, BF16),
        grid_spec=pltpu.PrefetchScalarGridSpec(
            num_scalar_prefetch=1, grid=(nb,),
            in_specs=[pl.BlockSpec((tr, cols), lambda i, pos: (i, 0))], out_specs=o_spec),
        compiler_params=_params(("parallel",)),
    )(pos, w)


BIG = ("w_in", "w_out", "w_gate", "w_up", "w_down")
BIG_AXIS = dict(w_in=1, w_out=0, w_gate=1, w_up=1, w_down=0)
SMALL = ("norm_mix_w", "ret_decay_fwd", "ret_decay_bwd", "ret_norm_w", "norm_ffn_w", "norm_final_w")
ALL_WEIGHTS = ("norm_mix_w", "w_in", "ret_decay_fwd", "ret_decay_bwd", "ret_norm_w", "w_out", "norm_ffn_w",
               "w_gate", "w_up", "w_down", "norm_final_w")
SMALL_ROW = 128 * 8


def _pack_small(small):
    pieces = [jnp.reshape(small["loss"], (1,))] + [jnp.reshape(small[k], (-1,)) for k in SMALL]
    rows = []
    for p in pieces:
        pad = -p.shape[0] % (8 * SMALL_ROW)
        rows.append(jnp.reshape(jnp.pad(p, (0, pad)), (-1, SMALL_ROW)))
    return jnp.concatenate(rows, axis=0)


def _unpack_small(block, like):
    out, row = {}, 0
    for k in ("loss",) + SMALL:
        size = 1 if k == "loss" else like[k].size
        nrows = -(-size // (8 * SMALL_ROW)) * 8
        out[k] = jnp.reshape(block[row:row + nrows], (-1,))[:size]
        row += nrows
    return out


def kernel(x, norm_mix_w, w_in, ret_decay_fwd, ret_decay_bwd, ret_norm_w, w_out, norm_ffn_w, w_gate, w_up, w_down, norm_final_w, loss_target, m_norm_mix_w, m_w_in, m_ret_decay_fwd, m_ret_decay_bwd, m_ret_norm_w, m_w_out, m_norm_ffn_w, m_w_gate, m_w_up, m_w_down, m_norm_final_w, v_norm_mix_w, v_w_in, v_ret_decay_fwd, v_ret_decay_bwd, v_ret_norm_w, v_w_out, v_norm_ffn_w, v_w_gate, v_w_up, v_w_down, v_norm_final_w):
    weights = dict(norm_mix_w=norm_mix_w, w_in=w_in, ret_decay_fwd=ret_decay_fwd, ret_decay_bwd=ret_decay_bwd,
                   ret_norm_w=ret_norm_w, w_out=w_out, norm_ffn_w=norm_ffn_w, w_gate=w_gate, w_up=w_up,
                   w_down=w_down, norm_final_w=norm_final_w)
    m_in = dict(norm_mix_w=m_norm_mix_w, w_in=m_w_in, ret_decay_fwd=m_ret_decay_fwd, ret_decay_bwd=m_ret_decay_bwd,
                ret_norm_w=m_ret_norm_w, w_out=m_w_out, norm_ffn_w=m_norm_ffn_w, w_gate=m_w_gate, w_up=m_w_up,
                w_down=m_w_down, norm_final_w=m_norm_final_w)
    v_in = dict(norm_mix_w=v_norm_mix_w, w_in=v_w_in, ret_decay_fwd=v_ret_decay_fwd, ret_decay_bwd=v_ret_decay_bwd,
                ret_norm_w=v_ret_norm_w, w_out=v_w_out, norm_ffn_w=v_norm_ffn_w, w_gate=v_w_gate, w_up=v_w_up,
                w_down=v_w_down, norm_final_w=v_norm_final_w)
    pos = jnp.stack([lax.axis_index("c"), 2 * lax.axis_index("x") + lax.axis_index("y")]).astype(jnp.int32)

    own = {k: _to_bf16_in_place(weights[k][0], BIG_AXIS[k], pos, name="cast_" + k) for k in BIG}

    dx, grad_w, small = _step(
        x[0], loss_target[0], norm_mix_w, ret_decay_fwd[0], ret_decay_bwd[0], ret_norm_w, norm_ffn_w,
        norm_final_w[None, :], own, pos)

    like = {k: weights[k] for k in SMALL}
    reduced = _unpack_small(_all_reduce_small(_pack_small(small)), like)
    loss = reduced["loss"][0]
    for k in SMALL:
        grad_w[k] = jnp.reshape(reduced[k], (1, -1))

    delta, new_m, new_v = {}, {}, {}
    for k in ALL_WEIGHTS:
        shape = weights[k].shape
        as2d = (lambda t: jnp.reshape(t, (-1, shape[-1])))
        delta[k], new_m[k], new_v[k] = (jnp.reshape(t, shape) for t in _adamw(
            as2d(weights[k]), as2d(grad_w[k]), as2d(m_in[k]), as2d(v_in[k]), name="adamw_" + k))
        grad_w[k] = jnp.reshape(grad_w[k], shape)

    return (loss, dx[None], *[grad_w[k] for k in ALL_WEIGHTS], *[delta[k] for k in ALL_WEIGHTS],
            *[new_m[k] for k in ALL_WEIGHTS], *[new_v[k] for k in ALL_WEIGHTS])
```

```python
import functools
import math

import numpy as np
import jax
import jax.numpy as jnp
from jax import lax
from jax.experimental import pallas as pl
from jax.experimental.pallas import tpu as pltpu

F32 = jnp.float32
BF16 = jnp.bfloat16
MESH = pl.DeviceIdType.MESH

HEAD_DIM = 128
RET_CHUNK = 128
RET_UNROLL = 8
EPS = 1e-6
DILATED_PATTERNS = ((128, 1), (512, 4), (2048, 16))
ATT_BLOCK = 256
ATT_REACH = max(w // 2 for w, _ in DILATED_PATTERNS)
ATT_NEAR = ATT_BLOCK
ATT_CLASSES = DILATED_PATTERNS[-1][1]
assert all(w // 2 <= ATT_NEAR for w, _ in DILATED_PATTERNS[:-1])
ATT_KB = -(-ATT_NEAR // ATT_BLOCK)
ATT_WINDOW = 2 * ATT_KB + 1
MASKED = -1e30
ROW_MAX_INIT = -1e29
N_CHIPS = 4
VMEM_LIMIT_BYTES = 56 * 1024 * 1024
ELEM_BLOCK_BYTES = 2 * 1024 * 1024

ADAM_LR = 0.001
ADAM_B1 = 0.9
ADAM_B2 = 0.999
ADAM_EPS = 1e-08
ADAM_WD = 0.01
ADAM_STEP = 10


def _params(sem=None):
    return pltpu.CompilerParams(dimension_semantics=sem, vmem_limit_bytes=VMEM_LIMIT_BYTES)


def _sigmoid(x):
    return 1.0 / (1.0 + jnp.exp(-x))


class _Job:
    def __init__(self, *, ins=(), ios=(), outs=(), sems=(), start, finish):
        self.ins, self.ios, self.outs, self.sems = list(ins), list(ios), list(outs), list(sems)
        self.start, self.finish = start, finish

    def results(self):
        return [jax.ShapeDtypeStruct(a.shape, a.dtype) for a in self.ios] + self.outs


def _call(body, *, name, grid, in_specs, out_specs, out_shape, operands, scratch_shapes=(), semantics=None, jobs=()):
    in_specs, out_specs, out_shape = list(in_specs), list(out_specs), list(out_shape)
    scratch_shapes = list(scratch_shapes)
    if not jobs:
        outs = pl.pallas_call(body, name=name, grid=grid, in_specs=in_specs, out_specs=out_specs, out_shape=out_shape,
                              scratch_shapes=scratch_shapes, compiler_params=_params(semantics))(*operands)
        return outs, []
    n_in, n_out, n_scratch = len(in_specs), len(out_specs), len(scratch_shapes)
    extra_in, extra_out, sems, aliases = [], [], [], {}
    for job in jobs:
        extra_in += job.ins
        for t in range(len(job.ios)):
            aliases[n_in + len(extra_in) + t] = n_out + len(extra_out) + t
        extra_in += job.ios
        extra_out += job.results()
        sems += job.sems

    def carried(*refs):
        x_in = refs[n_in:n_in + len(extra_in)]
        x_out = refs[n_in + len(extra_in) + n_out:n_in + len(extra_in) + n_out + len(extra_out)]
        x_sem = refs[len(refs) - len(sems):]
        views, i_in, i_out, i_sem = [], 0, 0, 0
        for job in jobs:
            data = list(x_in[i_in:i_in + len(job.ins)]) + list(x_out[i_out:i_out + len(job.results())])
            views.append((data, x_sem[i_sem:i_sem + len(job.sems)]))
            i_in += len(job.ins) + len(job.ios)
            i_out += len(job.results())
            i_sem += len(job.sems)
        steps = [pl.program_id(d) for d in range(len(grid))]

        @pl.when(functools.reduce(jnp.logical_and, [s == 0 for s in steps]))
        def _():
            for job, (data, sem) in zip(jobs, views):
                job.start(data, sem)

        body(*refs[:n_in], *refs[n_in + len(extra_in):n_in + len(extra_in) + n_out],
             *refs[len(refs) - len(sems) - n_scratch:len(refs) - len(sems)])

        @pl.when(functools.reduce(jnp.logical_and, [s == g - 1 for s, g in zip(steps, grid)]))
        def _():
            for job, (data, sem) in zip(jobs, views):
                job.finish(data, sem)

    hbm = pl.BlockSpec(memory_space=pl.ANY)
    res = pl.pallas_call(
        carried, name=name, grid=grid, in_specs=in_specs + [hbm] * len(extra_in),
        out_specs=out_specs + [hbm] * len(extra_out), out_shape=out_shape + extra_out,
        input_output_aliases=aliases, scratch_shapes=scratch_shapes + sems,
        compiler_params=_params(("arbitrary",) * len(grid)),
    )(*operands, *extra_in)
    carried_results, at = [], n_out
    for job in jobs:
        carried_results.append(list(res[at:at + len(job.results())]))
        at += len(job.results())
    return list(res[:n_out]), carried_results


def _run_jobs(jobs, *, name):
    first = jobs[0]
    n_in, n_io = len(first.ins), len(first.ios)
    out_shape = first.results()
    n_sems = [len(job.sems) for job in jobs]

    def body(*refs):
        data = list(refs[:n_in]) + list(refs[n_in + n_io:n_in + n_io + len(out_shape)])
        at = n_in + n_io + len(out_shape)
        for job, ns in zip(jobs, n_sems):
            job.start(data, refs[at:at + ns])
            job.finish(data, refs[at:at + ns])
            at += ns

    hbm = pl.BlockSpec(memory_space=pl.ANY)
    return pl.pallas_call(
        body, name=name, in_specs=[hbm] * (n_in + n_io), out_specs=[hbm] * len(out_shape), out_shape=out_shape,
        input_output_aliases={n_in + t: t for t in range(n_io)},
        scratch_shapes=[s for job in jobs for s in job.sems],
    )(*first.ins, *first.ios)


def _dot(a, b, ta=False, tb=False):
    return lax.dot_general(a, b, (((0 if ta else 1,), (1 if tb else 0,)), ((), ())),
                           preferred_element_type=F32)


def _tile(n, want):
    t = min(n, want) // 128 * 128
    while n % t:
        t -= 128
    return t


def _a_spec(ta, tm, tk):
    return pl.BlockSpec((tk, tm), lambda i, j, k: (k, i)) if ta else pl.BlockSpec((tm, tk), lambda i, j, k: (i, k))


def _b_spec(tb, tk, tn):
    return pl.BlockSpec((tn, tk), lambda i, j, k: (j, k)) if tb else pl.BlockSpec((tk, tn), lambda i, j, k: (k, j))


def _accumulate(accs, nk, products, finish):
    if nk == 1:
        finish(*products())
        return
    k = pl.program_id(2)

    @pl.when(k == 0)
    def _():
        for acc, p in zip(accs, products()):
            acc[...] = p

    if nk > 2:
        @pl.when(jnp.logical_and(k > 0, k < nk - 1))
        def _():
            for acc, p in zip(accs, products()):
                acc[...] += p

    @pl.when(k == nk - 1)
    def _():
        finish(*[acc[...] + p for acc, p in zip(accs, products())])


def _matmul(a, b, *, name, ta=False, tb=False, out_dtype=F32, residual=None, tm=1024, tn=1024, tk=2048, jobs=()):
    m, kdim = (a.shape[1], a.shape[0]) if ta else a.shape
    n = b.shape[0] if tb else b.shape[1]
    tm, tn, tk = _tile(m, tm), _tile(n, tn), _tile(kdim, tk)
    nk = kdim // tk

    def body(*refs):
        a_ref, b_ref = refs[:2]
        r_ref = refs[2] if residual is not None else None
        o_ref = refs[-1] if nk == 1 else refs[-2]

        def finish(total):
            if residual is not None:
                total = total + r_ref[...]
            o_ref[...] = total.astype(out_dtype)

        _accumulate(refs[-1:] if nk > 1 else (), nk, lambda: (_dot(a_ref[...], b_ref[...], ta, tb),), finish)

    o_spec = pl.BlockSpec((tm, tn), lambda i, j, k: (i, j))
    in_specs = [_a_spec(ta, tm, tk), _b_spec(tb, tk, tn)]
    operands = [a, b]
    if residual is not None:
        in_specs.append(o_spec)
        operands.append(residual)
    (out,), carried = _call(
        body, name=name, grid=(m // tm, n // tn, nk), in_specs=in_specs, out_specs=[o_spec],
        out_shape=[jax.ShapeDtypeStruct((m, n), out_dtype)], operands=operands,
        scratch_shapes=[pltpu.VMEM((tm, tn), F32)] * (nk > 1),
        semantics=("parallel", "parallel", "arbitrary"), jobs=jobs)
    return (out, carried) if jobs else out


def _weight_grad(a, g, *, name, jobs=()):
    tokens, m = a.shape
    tm = 1024 if m % 1024 == 0 else _tile(m, 1408)
    return _matmul(a, g, name=name, ta=True, tm=tm, tn=512, tk=tokens, jobs=jobs)


def _swiglu_fwd(n2, w_gate, w_up, *, tm=1024, tn=512, tk=2048, jobs=()):
    m, kdim = n2.shape
    n = w_gate.shape[1]
    tm, tn, tk = _tile(m, tm), _tile(n, tn), _tile(kdim, tk)
    nk = kdim // tk

    def body(a_ref, g_ref, u_ref, gate_ref, up_ref, act_ref, *acc):
        def products():
            a = a_ref[...]
            return _dot(a, g_ref[...]), _dot(a, u_ref[...])

        def finish(g, u):
            gate_ref[...] = g.astype(BF16)
            up_ref[...] = u.astype(BF16)
            act_ref[...] = (g * _sigmoid(g) * u).astype(BF16)

        _accumulate(acc, nk, products, finish)

    o_spec = pl.BlockSpec((tm, tn), lambda i, j, k: (i, j))
    o_shape = jax.ShapeDtypeStruct((m, n), BF16)
    return _call(
        body, name="swiglu_fwd", grid=(m // tm, n // tn, nk),
        in_specs=[_a_spec(False, tm, tk), _b_spec(False, tk, tn), _b_spec(False, tk, tn)],
        out_specs=[o_spec] * 3, out_shape=[o_shape] * 3, operands=[n2, w_gate, w_up],
        scratch_shapes=[pltpu.VMEM((tm, tn), F32)] * (2 * (nk > 1)),
        semantics=("parallel", "parallel", "arbitrary"), jobs=jobs)


def _swiglu_bwd_act(dh2, w_down, gate, up, *, tm=1024, tn=512, tk=2048):
    m, kdim = dh2.shape
    n = w_down.shape[0]
    tm, tn, tk = _tile(m, tm), _tile(n, tn), _tile(kdim, tk)
    nk = kdim // tk

    def body(a_ref, b_ref, gate_ref, up_ref, dgate_ref, dup_ref, *acc):
        def finish(dact):
            g = gate_ref[...].astype(F32)
            u = up_ref[...].astype(F32)
            sg = _sigmoid(g)
            dup_ref[...] = (dact * g * sg).astype(BF16)
            dgate_ref[...] = (dact * u * sg * (1.0 + g * (1.0 - sg))).astype(BF16)

        _accumulate(acc, nk, lambda: (_dot(a_ref[...], b_ref[...], tb=True),), finish)

    o_spec = pl.BlockSpec((tm, tn), lambda i, j, k: (i, j))
    o_shape = jax.ShapeDtypeStruct((m, n), BF16)
    return pl.pallas_call(
        body, name="swiglu_bwd_act", grid=(m // tm, n // tn, nk),
        in_specs=[_a_spec(False, tm, tk), _b_spec(True, tk, tn), o_spec, o_spec],
        out_specs=[o_spec] * 2, out_shape=[o_shape] * 2,
        scratch_shapes=[pltpu.VMEM((tm, tn), F32)] * (nk > 1),
        compiler_params=_params(("parallel", "parallel", "arbitrary")),
    )(dh2, w_down, gate, up)


def _swiglu_bwd_in(dgate, dup, w_gate, w_up, *, tm=1024, tn=1024, tk=1408, jobs=()):
    m, kdim = dgate.shape
    n = w_gate.shape[0]
    tm, tn, tk = _tile(m, tm), _tile(n, tn), _tile(kdim, tk)
    nk = kdim // tk

    def body(a1_ref, a2_ref, b1_ref, b2_ref, o_ref, *acc):
        def product():
            return (_dot(a1_ref[...], b1_ref[...], tb=True) + _dot(a2_ref[...], b2_ref[...], tb=True),)

        def finish(total):
            o_ref[...] = total

        _accumulate(acc, nk, product, finish)

    a_spec, b_spec = _a_spec(False, tm, tk), _b_spec(True, tk, tn)
    (out,), carried = _call(
        body, name="swiglu_bwd_in", grid=(m // tm, n // tn, nk),
        in_specs=[a_spec, a_spec, b_spec, b_spec],
        out_specs=[pl.BlockSpec((tm, tn), lambda i, j, k: (i, j))],
        out_shape=[jax.ShapeDtypeStruct((m, n), F32)], operands=[dgate, dup, w_gate, w_up],
        scratch_shapes=[pltpu.VMEM((tm, tn), F32)] * (nk > 1),
        semantics=("parallel", "parallel", "arbitrary"), jobs=jobs)
    return out, carried


def _row_block(rows, cols):
    tr = min(rows, max(16, ELEM_BLOCK_BYTES // (4 * cols) // 16 * 16))
    while rows % tr:
        tr -= 16
    return tr


def _rmsnorm_fwd(x, g, *, name):
    s, d = x.shape
    tr = _row_block(s, d)

    def body(x_ref, g_ref, n_ref):
        xv = x_ref[...]
        r = lax.rsqrt(jnp.mean(xv * xv, axis=-1, keepdims=True) + EPS)
        n_ref[...] = (xv * r * g_ref[...]).astype(BF16)

    row = pl.BlockSpec((tr, d), lambda i: (i, 0))
    return pl.pallas_call(
        body, name=name, grid=(s // tr,), in_specs=[row, pl.BlockSpec((1, d), lambda i: (0, 0))],
        out_specs=row, out_shape=jax.ShapeDtypeStruct((s, d), BF16),
        compiler_params=_params(("parallel",)),
    )(x, g)


def _rmsnorm_bwd_rows(xv, gv, dy):
    r = lax.rsqrt(jnp.mean(xv * xv, axis=-1, keepdims=True) + EPS)
    xhat = xv * r
    dxh = dy * gv
    dx = r * (dxh - xhat * jnp.mean(dxh * xhat, axis=-1, keepdims=True))
    return dx, dy * xhat


def _rmsnorm_bwd(dn, x, g, skip, *, name):
    s, d = x.shape
    tr = _row_block(s, d)

    def body(dn_ref, x_ref, g_ref, skip_ref, dx_ref, dxb_ref, dg_ref):
        dx, dgr = _rmsnorm_bwd_rows(x_ref[...], g_ref[...], dn_ref[...])
        dx = dx + skip_ref[...]
        dx_ref[...] = dx
        dxb_ref[...] = dx.astype(BF16)

        @pl.when(pl.program_id(0) == 0)
        def _():
            dg_ref[...] = jnp.zeros_like(dg_ref)

        dg_ref[...] += jnp.sum(dgr, axis=0, keepdims=True)

    row = pl.BlockSpec((tr, d), lambda i: (i, 0))
    vec = pl.BlockSpec((1, d), lambda i: (0, 0))
    return pl.pallas_call(
        body, name=name, grid=(s // tr,), in_specs=[row, row, vec, row],
        out_specs=[row, row, vec],
        out_shape=[jax.ShapeDtypeStruct((s, d), F32), jax.ShapeDtypeStruct((s, d), BF16),
                   jax.ShapeDtypeStruct((1, d), F32)],
        compiler_params=_params(("arbitrary",)),
    )(dn, x, g, skip)


def _loss_head(h2, g, target):
    s, d = h2.shape
    tr = _row_block(s, d)

    def body(h_ref, g_ref, t_ref, dh_ref, dhb_ref, dg_ref, loss_ref):
        hv = h_ref[...]
        gv = g_ref[...]
        r = lax.rsqrt(jnp.mean(hv * hv, axis=-1, keepdims=True) + EPS)
        err = hv * r * gv - t_ref[...]
        dx, dgr = _rmsnorm_bwd_rows(hv, gv, err * (1.0 / d))
        dh_ref[...] = dx
        dhb_ref[...] = dx.astype(BF16)

        @pl.when(pl.program_id(0) == 0)
        def _():
            dg_ref[...] = jnp.zeros_like(dg_ref)
            loss_ref[...] = jnp.zeros_like(loss_ref)

        dg_ref[...] += jnp.sum(dgr, axis=0, keepdims=True)
        row_loss = jnp.mean(err * err, axis=-1, keepdims=True)
        loss_ref[...] += 0.5 * jnp.sum(row_loss, axis=0, keepdims=True)

    row = pl.BlockSpec((tr, d), lambda i: (i, 0))
    vec = pl.BlockSpec((1, d), lambda i: (0, 0))
    one = pl.BlockSpec((1, 1), lambda i: (0, 0))
    return pl.pallas_call(
        body, name="loss_head", grid=(s // tr,), in_specs=[row, vec, row],
        out_specs=[row, row, vec, one],
        out_shape=[jax.ShapeDtypeStruct((s, d), F32), jax.ShapeDtypeStruct((s, d), BF16),
                   jax.ShapeDtypeStruct((1, d), F32), jax.ShapeDtypeStruct((1, 1), F32)],
        compiler_params=_params(("arbitrary",)),
    )(h2, g, target)


def _attention_bias_tables():
    k = np.arange(-ATT_KB, ATT_KB + 1)[:, None, None]
    delta = k * ATT_BLOCK + np.arange(ATT_BLOCK)[None, None, :] - np.arange(ATT_BLOCK)[None, :, None]
    dist = np.abs(delta)
    count = np.zeros(delta.shape, np.int32)
    for window, dilation in DILATED_PATTERNS:
        count += (delta % dilation == 0) & (dist <= min(window // 2, ATT_NEAR))
    logc = np.where(count > 0, np.log(np.maximum(count, 1)), MASKED)
    return dist.astype(np.float32), logc.astype(np.float32)


def _far_bias_tables(per_class):
    steps = np.abs(np.arange(per_class)[:, None] - np.arange(per_class)[None, :]) * ATT_CLASSES
    valid = (steps > ATT_NEAR) & (steps <= ATT_REACH)
    return steps.astype(np.float32), np.where(valid, 0.0, MASKED).astype(np.float32)


def _to_classes(x):
    s, cols = x.shape
    return jnp.reshape(jnp.transpose(jnp.reshape(x, (s // ATT_CLASSES, ATT_CLASSES, cols)), (1, 0, 2)), (s, cols))


def _from_classes(x):
    s, cols = x.shape
    return jnp.reshape(jnp.transpose(jnp.reshape(x, (ATT_CLASSES, s // ATT_CLASSES, cols)), (1, 0, 2)), (s, cols))


def _head_bias(bias_ref, slope, dist_ref, logc_ref):
    for kk in range(ATT_WINDOW):
        bias_ref[kk] = logc_ref[kk] - slope * dist_ref[kk]
    bias_ref[ATT_WINDOW] = jnp.full((ATT_BLOCK, ATT_BLOCK), MASKED, F32)


def _window_start(i, nq, nwin):
    return jnp.clip(i - ATT_KB, 0, nq - nwin)


def _window_block(j, i):
    rows = pl.ds(pl.multiple_of(j * ATT_BLOCK, ATT_BLOCK), ATT_BLOCK)
    kk = j - i + ATT_KB
    return rows, jnp.where(jnp.logical_and(kk >= 0, kk < ATT_WINDOW), kk, ATT_WINDOW)


def _attention_far_fwd(qkv, slopes, n_heads):
    s = qkv.shape[0]
    per_class = s // ATT_CLASSES
    scale = HEAD_DIM ** -0.5
    dist, logc = _far_bias_tables(per_class)

    def body(slope_ref, q_ref, k_ref, v_ref, dist_ref, logc_ref, o_ref, lse_ref):
        slope = slope_ref[pl.program_id(0)]
        sc = _dot(q_ref[...], k_ref[...], tb=True) * scale + (logc_ref[...] - slope * dist_ref[...])
        m = jnp.maximum(jnp.max(sc, axis=-1, keepdims=True), ROW_MAX_INIT)
        p = jnp.exp(sc - m)
        l = jnp.maximum(jnp.sum(p, axis=-1, keepdims=True), 1e-30)
        o_ref[...] = (_dot(p.astype(BF16), v_ref[...]) / l).astype(BF16)
        lse_ref[...] = jnp.broadcast_to(m + jnp.log(l), (per_class, HEAD_DIM))

    hh = n_heads
    blk = pl.BlockSpec((per_class, HEAD_DIM), lambda h, r: (r, h))
    table = pl.BlockSpec(dist.shape, lambda h, r: (0, 0))
    return pl.pallas_call(
        body, name="attention_far_fwd", grid=(hh, ATT_CLASSES),
        in_specs=[pl.BlockSpec(memory_space=pltpu.SMEM), blk,
                  pl.BlockSpec((per_class, HEAD_DIM), lambda h, r: (r, hh + h)),
                  pl.BlockSpec((per_class, HEAD_DIM), lambda h, r: (r, 2 * hh + h)), table, table],
        out_specs=[blk, blk],
        out_shape=[jax.ShapeDtypeStruct((s, hh * HEAD_DIM), BF16), jax.ShapeDtypeStruct((s, hh * HEAD_DIM), F32)],
        compiler_params=_params(("parallel", "parallel")),
    )(slopes, qkv, qkv, qkv, jnp.asarray(dist), jnp.asarray(logc))


def _attention_fwd(proj, slopes, far_out, far_lse, n_heads, jobs=()):
    s = proj.shape[0]
    nq = s // ATT_BLOCK
    scale = HEAD_DIM ** -0.5
    dist, logc = _attention_bias_tables()

    nwin = min(ATT_WINDOW, nq)

    def body(slope_ref, q_ref, k_ref, v_ref, fo_ref, fl_ref, dist_ref, logc_ref, o_ref, lse_ref, bias_ref, s_ref):
        h, i = pl.program_id(0), pl.program_id(1)

        @pl.when(i == 0)
        def _():
            _head_bias(bias_ref, slope_ref[h], dist_ref, logc_ref)

        q = q_ref[...]
        first = _window_start(i, nq, nwin)
        m = jnp.full((ATT_BLOCK, 1), ROW_MAX_INIT, F32)
        for b in range(nwin):
            rows, kk = _window_block(first + b, i)
            sc = _dot(q, k_ref[rows, :], tb=True) * scale + bias_ref[kk]
            s_ref[b] = sc
            m = jnp.maximum(m, jnp.max(sc, axis=-1, keepdims=True))
        l = jnp.zeros((ATT_BLOCK, 1), F32)
        acc = jnp.zeros((ATT_BLOCK, HEAD_DIM), F32)
        for b in range(nwin):
            rows, _ = _window_block(first + b, i)
            p = jnp.exp(s_ref[b] - m)
            l = l + jnp.sum(p, axis=-1, keepdims=True)
            acc = acc + _dot(p.astype(BF16), v_ref[rows, :])
        near_lse = m + jnp.log(l)
        far_lse_col = fl_ref[:, :1]
        lse = jnp.maximum(near_lse, far_lse_col)
        lse = lse + jnp.log(jnp.exp(near_lse - lse) + jnp.exp(far_lse_col - lse))
        o_ref[...] = (acc * (jnp.exp(near_lse - lse) / l)
                      + fo_ref[...].astype(F32) * jnp.exp(far_lse_col - lse)).astype(BF16)
        lse_ref[...] = jnp.broadcast_to(lse, (ATT_BLOCK, HEAD_DIM))

    hh = n_heads
    blk = pl.BlockSpec((ATT_BLOCK, HEAD_DIM), lambda h, i: (i, h))
    table = pl.BlockSpec(dist.shape, lambda h, i: (0, 0, 0))
    return _call(
        body, name="attention_fwd", grid=(hh, nq),
        in_specs=[pl.BlockSpec(memory_space=pltpu.SMEM), blk,
                  pl.BlockSpec((s, HEAD_DIM), lambda h, i: (0, hh + h)),
                  pl.BlockSpec((s, HEAD_DIM), lambda h, i: (0, 2 * hh + h)), blk, blk, table, table],
        out_specs=[blk, blk],
        out_shape=[jax.ShapeDtypeStruct((s, hh * HEAD_DIM), BF16), jax.ShapeDtypeStruct((s, hh * HEAD_DIM), F32)],
        operands=[slopes, proj, proj, proj, far_out, far_lse, jnp.asarray(dist), jnp.asarray(logc)],
        scratch_shapes=[pltpu.VMEM((ATT_WINDOW + 1, ATT_BLOCK, ATT_BLOCK), F32),
                        pltpu.VMEM((nwin, ATT_BLOCK, ATT_BLOCK), F32)],
        semantics=("parallel", "arbitrary"), jobs=jobs)


def _attention_far_bwd(qkv, slopes, out, dout, lse, n_heads):
    s = qkv.shape[0]
    per_class = s // ATT_CLASSES
    scale = HEAD_DIM ** -0.5
    dist, logc = _far_bias_tables(per_class)

    def body(slope_ref, q_ref, k_ref, v_ref, o_ref, do_ref, lse_ref, dist_ref, logc_ref, dq_ref, dk_ref, dv_ref):
        slope = slope_ref[pl.program_id(0)]
        q, k, do = q_ref[...], k_ref[...], do_ref[...]
        delta = jnp.sum(do.astype(F32) * o_ref[...].astype(F32), axis=-1, keepdims=True)
        sc = _dot(q, k, tb=True) * scale + (logc_ref[...] - slope * dist_ref[...])
        p = jnp.exp(sc - lse_ref[:, :1])
        dv_ref[...] = _dot(p.astype(BF16), do, ta=True).astype(BF16)
        ds = (p * (_dot(do, v_ref[...], tb=True) - delta) * scale).astype(BF16)
        dk_ref[...] = _dot(ds, q, ta=True).astype(BF16)
        dq_ref[...] = _dot(ds, k).astype(BF16)

    hh = n_heads
    blk = pl.BlockSpec((per_class, HEAD_DIM), lambda h, r: (r, h))
    table = pl.BlockSpec(dist.shape, lambda h, r: (0, 0))
    o_shape = jax.ShapeDtypeStruct((s, hh * HEAD_DIM), BF16)
    return pl.pallas_call(
        body, name="attention_far_bwd", grid=(hh, ATT_CLASSES),
        in_specs=[pl.BlockSpec(memory_space=pltpu.SMEM), blk,
                  pl.BlockSpec((per_class, HEAD_DIM), lambda h, r: (r, hh + h)),
                  pl.BlockSpec((per_class, HEAD_DIM), lambda h, r: (r, 2 * hh + h)),
                  blk, blk, blk, table, table],
        out_specs=[blk] * 3, out_shape=[o_shape] * 3,
        compiler_params=_params(("parallel", "parallel")),
    )(slopes, qkv, qkv, qkv, out, dout, lse, jnp.asarray(dist), jnp.asarray(logc))


def _attention_bwd(proj, slopes, out, lse, dmixed, far_grads, n_heads, jobs=()):
    s = proj.shape[0]
    nq = s // ATT_BLOCK
    scale = HEAD_DIM ** -0.5
    dist, logc = _attention_bias_tables()

    nwin = min(ATT_WINDOW, nq)

    def body(slope_ref, q_ref, k_ref, v_ref, o_ref, do_ref, lse_ref, fdq_ref, fdk_ref, fdv_ref, dist_ref, logc_ref,
             dq_ref, dk_ref, dv_ref, dk_acc, dv_acc, bias_ref):
        h, i = pl.program_id(0), pl.program_id(1)

        @pl.when(i == 0)
        def _():
            dk_acc[...] = jnp.zeros_like(dk_acc)
            dv_acc[...] = jnp.zeros_like(dv_acc)
            _head_bias(bias_ref, slope_ref[h], dist_ref, logc_ref)

        q = q_ref[...]
        do = do_ref[...]
        lse_col = lse_ref[:, :1]
        delta = jnp.sum(do.astype(F32) * o_ref[...].astype(F32), axis=-1, keepdims=True)
        first = _window_start(i, nq, nwin)
        dq = jnp.zeros((ATT_BLOCK, HEAD_DIM), F32)
        for b in range(nwin):
            rows, kk = _window_block(first + b, i)
            kj = k_ref[rows, :]
            vj = v_ref[rows, :]
            p = jnp.exp(_dot(q, kj, tb=True) * scale + bias_ref[kk] - lse_col)
            dv_acc[rows, :] += _dot(p.astype(BF16), do, ta=True)
            dp = _dot(do, vj, tb=True)
            ds = (p * (dp - delta) * scale).astype(BF16)
            dk_acc[rows, :] += _dot(ds, q, ta=True)
            dq = dq + _dot(ds, kj)
        dq_ref[...] = (dq + fdq_ref[...].astype(F32)).astype(BF16)

        @pl.when(i == nq - 1)
        def _():
            dk_ref[...] = (dk_acc[...] + fdk_ref[...].astype(F32)).astype(BF16)
            dv_ref[...] = (dv_acc[...] + fdv_ref[...].astype(F32)).astype(BF16)

    hh = n_heads
    blk = pl.BlockSpec((ATT_BLOCK, HEAD_DIM), lambda h, i: (i, h))
    col = pl.BlockSpec((s, HEAD_DIM), lambda h, i: (0, h))
    table = pl.BlockSpec(dist.shape, lambda h, i: (0, 0, 0))
    o_shape = jax.ShapeDtypeStruct((s, hh * HEAD_DIM), BF16)
    return _call(
        body, name="attention_bwd", grid=(hh, nq),
        in_specs=[pl.BlockSpec(memory_space=pltpu.SMEM), blk,
                  pl.BlockSpec((s, HEAD_DIM), lambda h, i: (0, hh + h)),
                  pl.BlockSpec((s, HEAD_DIM), lambda h, i: (0, 2 * hh + h)),
                  blk, blk, blk, blk, col, col, table, table],
        out_specs=[blk, col, col], out_shape=[o_shape] * 3,
        operands=[slopes, proj, proj, proj, out, dmixed, lse, *far_grads, jnp.asarray(dist), jnp.asarray(logc)],
        scratch_shapes=[pltpu.VMEM((s, HEAD_DIM), F32)] * 2
        + [pltpu.VMEM((ATT_WINDOW + 1, ATT_BLOCK, ATT_BLOCK), F32)],
        semantics=("parallel", "arbitrary"), jobs=jobs)


def _ret_decays(lgc, lga, strict_c, strict_a):
    c = RET_CHUNK
    rel = (lax.broadcasted_iota(jnp.int32, (c, c), 0) - lax.broadcasted_iota(jnp.int32, (c, c), 1)).astype(F32)
    in_c = (rel > 0) if strict_c else (rel >= 0)
    in_a = (rel < 0) if strict_a else (rel <= 0)
    mask = (jnp.where(in_c, jnp.exp(lgc * jnp.maximum(rel, 0.0)), 0.0)
            + jnp.where(in_a, jnp.exp(lga * jnp.maximum(-rel, 0.0)), 0.0))
    idx = lax.broadcasted_iota(jnp.int32, (c, 1), 0).astype(F32)
    ones = jnp.ones((1, HEAD_DIM), F32)
    dec = dict(
        rel=rel, mask=mask, idx=idx,
        a_c=jnp.exp(lgc * (idx + 1.0)), b_c=jnp.exp(lgc * (c - 1.0 - idx)), chunk_c=jnp.exp(ones * (lgc * c)),
        a_a=jnp.exp(lga * (c - idx)), b_a=jnp.exp(lga * idx), chunk_a=jnp.exp(ones * (lga * c)),
    )
    return dec


def _scaled(x, col):
    return (x.astype(F32) * col).astype(BF16)


def _chunk_rows(i):
    return pl.ds(pl.multiple_of(i * RET_CHUNK, RET_CHUNK), RET_CHUNK)


def _chunk_loop(nc, step, init, unroll=RET_UNROLL):
    group = math.gcd(nc, unroll)

    def trip(t, carry):
        for u in range(group):
            carry = step(t * group + u, carry)
        return carry

    return lax.fori_loop(0, nc // group, trip, init)


def _retention(a, b, c, lg_c, lg_a, *, strict_c, strict_a, scale, n_heads, name, gate=None, norm_w=None, jobs=()):
    s = a[0].shape[0]
    nc = s // RET_CHUNK
    epilogue = gate is not None

    def body(*refs):
        lgc_ref, lga_ref, a_ref, b_ref, c_ref = refs[:5]
        if epilogue:
            g_ref, w_ref, o_ref, mix_ref, sa_ref = refs[5:]
        else:
            o_ref, sa_ref = refs[5:]
        h = pl.program_id(0)
        dec = _ret_decays(lgc_ref[h], lga_ref[h], strict_c, strict_a)

        def reverse(t, state):
            i = nc - 1 - t
            sa_ref[i] = state.astype(BF16)
            rows = _chunk_rows(i)
            return state * dec["chunk_a"] + _dot(_scaled(b_ref[rows, :], dec["b_a"]), c_ref[rows, :], ta=True)

        _chunk_loop(nc, reverse, jnp.zeros((HEAD_DIM, HEAD_DIM), F32))

        def forward(i, state):
            rows = _chunk_rows(i)
            ai, bi, ci = a_ref[rows, :], b_ref[rows, :], c_ref[rows, :]
            inner = (_dot(ai, bi, tb=True) * dec["mask"]).astype(BF16)
            out = (_dot(inner, ci) + _dot(_scaled(ai, dec["a_c"]), state.astype(BF16))
                   + _dot(_scaled(ai, dec["a_a"]), sa_ref[i])) * scale
            o_ref[rows, :] = out.astype(BF16)
            if epilogue:
                r = lax.rsqrt(jnp.mean(out * out, axis=-1, keepdims=True) + EPS)
                g = g_ref[rows, :].astype(F32)
                mix_ref[rows, :] = (out * r * w_ref[...] * (g * _sigmoid(g))).astype(BF16)
            return state * dec["chunk_c"] + _dot(_scaled(bi, dec["b_c"]), ci, ta=True)

        _chunk_loop(nc, forward, jnp.zeros((HEAD_DIM, HEAD_DIM), F32))

    def col(first):
        return pl.BlockSpec((s, HEAD_DIM), lambda h: (0, first + h))

    smem = pl.BlockSpec(memory_space=pltpu.SMEM)
    in_specs = [smem, smem, col(a[1]), col(b[1]), col(c[1])]
    operands = [lg_c, lg_a, a[0], b[0], c[0]]
    o_shape = jax.ShapeDtypeStruct((s, n_heads * HEAD_DIM), BF16)
    out_specs, out_shape = [col(0)], [o_shape]
    if epilogue:
        in_specs += [col(gate[1]), pl.BlockSpec((1, HEAD_DIM), lambda h: (0, h))]
        operands += [gate[0], norm_w]
        out_specs, out_shape = [col(0)] * 2, [o_shape] * 2
    res, carried = _call(
        body, name=name, grid=(n_heads,), in_specs=in_specs, out_specs=out_specs, out_shape=out_shape,
        operands=operands, scratch_shapes=[pltpu.VMEM((nc, HEAD_DIM, HEAD_DIM), BF16)],
        semantics=("parallel",), jobs=jobs)
    res = res if epilogue else res[0]
    return (res, carried) if jobs else res


def _retention_decay_grads(a, b, c, e, lg_c, lg_a, *, scale, n_heads):
    s = a[0].shape[0]
    nc = s // RET_CHUNK
    cf = float(RET_CHUNK)

    def body(lgc_ref, lga_ref, a_ref, b_ref, c_ref, e_ref, gc_ref, ga_ref, sa_ref, ta_ref):
        h = pl.program_id(0)
        lgc, lga = lgc_ref[h], lga_ref[h]
        dec = _ret_decays(lgc, lga, True, True)
        rel, idx = dec["rel"], dec["idx"]
        w_c = jnp.where(rel > 0, rel * jnp.exp(lgc * jnp.maximum(rel, 0.0)), 0.0)
        w_a = jnp.where(rel < 0, -rel * jnp.exp(lga * jnp.maximum(-rel, 0.0)), 0.0)
        zero = jnp.zeros((HEAD_DIM, HEAD_DIM), F32)

        def reverse(t, carry):
            st, dst = carry
            i = nc - 1 - t
            sa_ref[i] = st.astype(BF16)
            ta_ref[i] = dst.astype(BF16)
            rows = _chunk_rows(i)
            bi, ci = b_ref[rows, :], c_ref[rows, :]
            st_new = st * dec["chunk_a"] + _dot(_scaled(bi, dec["b_a"]), ci, ta=True)
            dst_new = (cf * st + dst) * dec["chunk_a"] + _dot(_scaled(bi, idx * dec["b_a"]), ci, ta=True)
            return st_new, dst_new

        _chunk_loop(nc, reverse, (zero, zero))

        def forward(i, carry):
            st, dst, acc_c, acc_a = carry
            rows = _chunk_rows(i)
            ai, bi, ci = a_ref[rows, :], b_ref[rows, :], c_ref[rows, :]
            ev = e_ref[rows, :].astype(F32)
            pg = _dot(ai, bi, tb=True) * _dot(e_ref[rows, :], ci, tb=True)
            a_c, a_a = _scaled(ai, dec["a_c"]), _scaled(ai, dec["a_a"])
            inter_c = _dot(a_c, st.astype(BF16)) * (idx + 1.0) + _dot(a_c, dst.astype(BF16))
            inter_a = _dot(a_a, sa_ref[i]) * (cf - idx) + _dot(a_a, ta_ref[i])
            acc_c = acc_c + jnp.sum(pg * w_c, axis=0, keepdims=True) + jnp.sum(inter_c * ev, axis=0, keepdims=True)
            acc_a = acc_a + jnp.sum(pg * w_a, axis=0, keepdims=True) + jnp.sum(inter_a * ev, axis=0, keepdims=True)
            st_new = st * dec["chunk_c"] + _dot(_scaled(bi, dec["b_c"]), ci, ta=True)
            dst_new = ((cf * st + dst) * dec["chunk_c"]
                       + _dot(_scaled(bi, (cf - 1.0 - idx) * dec["b_c"]), ci, ta=True))
            return st_new, dst_new, acc_c, acc_a

        row = jnp.zeros((1, HEAD_DIM), F32)
        _, _, acc_c, acc_a = _chunk_loop(nc, forward, (zero, zero, row, row))
        gc_ref[...] = jnp.broadcast_to(jnp.sum(acc_c, axis=-1, keepdims=True) * scale, gc_ref.shape)
        ga_ref[...] = jnp.broadcast_to(jnp.sum(acc_a, axis=-1, keepdims=True) * scale, ga_ref.shape)

    def col(first):
        return pl.BlockSpec((s, HEAD_DIM), lambda h: (0, first + h))

    smem = pl.BlockSpec(memory_space=pltpu.SMEM)
    o_spec = pl.BlockSpec((1, 8, HEAD_DIM), lambda h: (h, 0, 0))
    o_shape = jax.ShapeDtypeStruct((n_heads, 8, HEAD_DIM), F32)
    gc, ga = pl.pallas_call(
        body, name="retention_decay_grads", grid=(n_heads,),
        in_specs=[smem, smem, col(a[1]), col(b[1]), col(c[1]), col(e[1])],
        out_specs=[o_spec] * 2, out_shape=[o_shape] * 2,
        scratch_shapes=[pltpu.VMEM((nc, HEAD_DIM, HEAD_DIM), BF16)] * 2,
        compiler_params=_params(("parallel",)),
    )(lg_c, lg_a, a[0], b[0], c[0], e[0])
    return gc[:, 0, 0], ga[:, 0, 0]


def _ret_gate_bwd(dmixed, first_col, out, proj, gate_col, norm_w, n_heads):
    s = out.shape[0]
    tr = _row_block(s, 8 * HEAD_DIM)

    def body(dm_ref, o_ref, g_ref, w_ref, do_ref, dg_ref, dw_ref):
        dm = dm_ref[...].astype(F32)
        ov = o_ref[...].astype(F32)
        g = g_ref[...].astype(F32)
        w = w_ref[...]
        r = lax.rsqrt(jnp.mean(ov * ov, axis=-1, keepdims=True) + EPS)
        ohat = ov * r
        sg = _sigmoid(g)
        silu = g * sg
        dg_ref[...] = (dm * ohat * w * sg * (1.0 + g * (1.0 - sg))).astype(BF16)
        dohat = dm * w * silu
        do_ref[...] = (r * (dohat - ohat * jnp.mean(dohat * ohat, axis=-1, keepdims=True))).astype(BF16)

        @pl.when(pl.program_id(1) == 0)
        def _():
            dw_ref[...] = jnp.zeros_like(dw_ref)

        dw_ref[...] += jnp.sum(dm * ohat * silu, axis=0, keepdims=True)

    def blk(first):
        return pl.BlockSpec((tr, HEAD_DIM), lambda h, i: (i, first + h))

    vec = pl.BlockSpec((1, HEAD_DIM), lambda h, i: (0, h))
    o_shape = jax.ShapeDtypeStruct((s, n_heads * HEAD_DIM), BF16)
    return pl.pallas_call(
        body, name="ret_gate_bwd", grid=(n_heads, s // tr),
        in_specs=[blk(first_col), blk(0), blk(gate_col), vec],
        out_specs=[blk(0), blk(0), vec],
        out_shape=[o_shape, o_shape, jax.ShapeDtypeStruct((1, n_heads * HEAD_DIM), F32)],
        compiler_params=_params(("parallel", "arbitrary")),
    )(dmixed, out, proj, norm_w)


def _step(x, target, norm_mix_w, ret_decay_fwd, ret_decay_bwd, ret_norm_w, norm_ffn_w, norm_final_w, own, pos):
    d = x.shape[1]
    nh = d // (2 * HEAD_DIM)
    scale = HEAD_DIM ** -0.5
    slopes = jnp.exp2(-8.0 * jnp.arange(1, nh + 1, dtype=F32) / nh)
    lg_f = -jnp.exp(ret_decay_fwd)
    lg_b = -jnp.exp(ret_decay_bwd)
    q_r, k_r, v_r, g_r = 3 * nh, 4 * nh, 5 * nh, 6 * nh
    ax = BIG_AXIS

    def gather(names, arrays, stage):
        return _gather_job(arrays, [ax[k] for k in names], stage)

    def add_halves(k, g, received):
        return _add_halves(g, received, ax[k], pos, name="grad_add_halves_" + k)

    def sum_parts(k, g, received, parts):
        return _sum_chip_parts(g, received, parts, ax[k], pos, name="grad_sum_parts_" + k)

    (w_in,) = _run_jobs([gather(["w_in"], [own["w_in"]], "ici"), gather(["w_in"], [own["w_in"]], "d2d")],
                        name="all_gather_w_in")
    n1 = _rmsnorm_fwd(x, norm_mix_w, name="norm_mix_fwd")
    proj, [[w_gate]] = _matmul(n1, w_in, name="in_proj", out_dtype=BF16,
                               jobs=[gather(["w_gate"], [own["w_gate"]], "ici")])
    qkv_classes = _to_classes(proj[:, :3 * nh * HEAD_DIM])
    far_out, far_lse = _attention_far_fwd(qkv_classes, slopes, nh)
    (attn, lse), [[w_gate], [w_out, w_up]] = _attention_fwd(
        proj, slopes, _from_classes(far_out), _from_classes(far_lse), nh,
        jobs=[gather(["w_gate"], [w_gate], "d2d"), gather(["w_out", "w_up"], [own["w_out"], own["w_up"]], "ici")])
    (ret, ret_mixed), [[w_out, w_up]] = _retention(
        (proj, q_r), (proj, k_r), (proj, v_r), lg_f, lg_b, strict_c=False, strict_a=True, scale=scale, n_heads=nh,
        name="retention_fwd", gate=(proj, g_r), norm_w=ret_norm_w,
        jobs=[gather(["w_out", "w_up"], [w_out, w_up], "d2d")])
    mixed = jnp.concatenate([attn, ret_mixed], axis=1)
    h1 = _matmul(mixed, w_out, name="out_proj", residual=x)
    n2 = _rmsnorm_fwd(h1, norm_ffn_w, name="norm_ffn_fwd")
    (gate, up, act), [[w_down]] = _swiglu_fwd(n2, w_gate, w_up, jobs=[gather(["w_down"], [own["w_down"]], "ici")])
    (w_down,) = _run_jobs([gather(["w_down"], [w_down], "d2d")], name="all_gather_w_down_sibling")
    h2 = _matmul(act, w_down, name="down_proj", residual=h1, tk=2816)
    dh2, dh2_b, d_norm_final, loss = _loss_head(h2, norm_final_w, target)

    dgate, dup = _swiglu_bwd_act(dh2_b, w_down, gate, up)
    g_down = _weight_grad(act, dh2_b, name="grad_w_down")
    g_gate, [[r_down]] = _weight_grad(n2, dgate, name="grad_w_gate", jobs=[_exchange_job([g_down], [ax["w_down"]])])
    s_down = add_halves("w_down", g_down, r_down)
    g_up, [[r_gate], [p_down]] = _weight_grad(
        n2, dup, name="grad_w_up",
        jobs=[_exchange_job([g_gate], [ax["w_gate"]]), _send_sums_job([s_down], [ax["w_down"]])])
    s_gate = add_halves("w_gate", g_gate, r_gate)
    h_down = sum_parts("w_down", g_down, r_down, p_down)
    dn2, [[r_up], [p_gate]] = _swiglu_bwd_in(
        dgate, dup, w_gate, w_up,
        jobs=[_exchange_job([g_up], [ax["w_up"]]), _send_sums_job([s_gate], [ax["w_gate"]])])
    s_up = add_halves("w_up", g_up, r_up)
    h_gate = sum_parts("w_gate", g_gate, r_gate, p_gate)
    dh1, dh1_b, d_norm_ffn = _rmsnorm_bwd(dn2, h1, norm_ffn_w, dh2, name="norm_ffn_bwd")

    dmixed, [[gr_down]] = _matmul(dh1_b, w_out, name="out_proj_bwd", tb=True, out_dtype=BF16,
                                  jobs=[_join_job([h_down], [ax["w_down"]])])
    g_out = _weight_grad(mixed, dh1_b, name="grad_w_out")
    d_ret, dg_r, d_ret_norm = _ret_gate_bwd(dmixed, nh, ret, proj, g_r, ret_norm_w, nh)
    far_grads = _attention_far_bwd(qkv_classes, slopes, _to_classes(attn), _to_classes(dmixed[:, :nh * HEAD_DIM]),
                                   _to_classes(lse), nh)
    (dq_a, dk_a, dv_a), [[p_up], [r_out]] = _attention_bwd(
        proj, slopes, attn, lse, dmixed, [_from_classes(t) for t in far_grads], nh,
        jobs=[_send_sums_job([s_up], [ax["w_up"]]), _exchange_job([g_out], [ax["w_out"]])])
    s_out = add_halves("w_out", g_out, r_out)
    h_up = sum_parts("w_up", g_up, r_up, p_up)
    dq_r, [[p_out], [gr_gate, gr_up]] = _retention(
        (d_ret, 0), (proj, v_r), (proj, k_r), lg_f, lg_b, strict_c=False, strict_a=True, scale=scale, n_heads=nh,
        name="retention_dq",
        jobs=[_send_sums_job([s_out], [ax["w_out"]]), _join_job([h_gate, h_up], [ax["w_gate"], ax["w_up"]])])
    h_out = sum_parts("w_out", g_out, r_out, p_out)
    dv_r, [[gr_out]] = _retention(
        (proj, k_r), (proj, q_r), (d_ret, 0), lg_b, lg_f, strict_c=True, strict_a=False, scale=scale, n_heads=nh,
        name="retention_dv", jobs=[_join_job([h_out], [ax["w_out"]])])
    dk_r = _retention((proj, v_r), (d_ret, 0), (proj, q_r), lg_b, lg_f, strict_c=True, strict_a=False,
                      scale=scale, n_heads=nh, name="retention_dk")
    dlg_f, dlg_b = _retention_decay_grads((proj, q_r), (proj, k_r), (proj, v_r), (d_ret, 0), lg_f, lg_b,
                                          scale=scale, n_heads=nh)
    dproj = jnp.concatenate([dq_a, dk_a, dv_a, dq_r, dk_r, dv_r, dg_r], axis=1)
    g_in = _weight_grad(n1, dproj, name="grad_w_in")
    (r_in,) = _run_jobs([_exchange_job([g_in], [ax["w_in"]])], name="grad_exchange_w_in")
    s_in = add_halves("w_in", g_in, r_in)
    dn1, [[p_in]] = _matmul(dproj, w_in, name="in_proj_bwd", tb=True, tk=3584,
                            jobs=[_send_sums_job([s_in], [ax["w_in"]])])
    dx, _, d_norm_mix = _rmsnorm_bwd(dn1, x, norm_mix_w, dh1, name="norm_mix_bwd")
    h_in = sum_parts("w_in", g_in, r_in, p_in)
    (gr_in,) = _run_jobs([_join_job([h_in], [ax["w_in"]])], name="grad_join_w_in")

    small = dict(loss=loss[0, 0], norm_mix_w=d_norm_mix, ret_decay_fwd=dlg_f * lg_f, ret_decay_bwd=dlg_b * lg_b,
                 ret_norm_w=d_ret_norm, norm_ffn_w=d_norm_ffn, norm_final_w=d_norm_final)
    return dx, dict(w_in=gr_in, w_out=gr_out, w_gate=gr_gate, w_up=gr_up, w_down=gr_down), small


def _mesh_position():
    x, y, c = lax.axis_index("x"), lax.axis_index("y"), lax.axis_index("c")
    chips = [(1 - x, y), (x, 1 - y), (1 - x, 1 - y)]
    return x, y, c, chips


def _ds(start, size):
    if isinstance(start, int):
        return pl.ds(start, size)
    return pl.ds(pl.multiple_of(start * size, size), size)


def _region(ref, axis, shard, half, shard_size, half_size):
    along = slice(None) if shard is None else _ds(shard, shard_size)
    other = slice(None) if half is None else _ds(half, half_size)
    return ref.at[other, along] if axis == 1 else ref.at[along, other]


def _gather_job(full, axes, stage):
    n = len(full)

    def copies(refs, sems):
        send_sem, recv_sem = sems
        x, y, c, chips = _mesh_position()
        me = 2 * x + y

        def copy(w, k, shard, half, target):
            rows_cols = full[w].shape
            place = _region(refs[w], axes[w], shard, half, rows_cols[axes[w]] // N_CHIPS, rows_cols[1 - axes[w]] // 2)
            return pltpu.make_async_remote_copy(
                src_ref=place, dst_ref=place, send_sem=send_sem.at[w, k], recv_sem=recv_sem.at[w, k],
                device_id=target, device_id_type=MESH)

        def sent(w, k):
            if stage == "ici":
                return copy(w, k, me, c, (chips[k][0], chips[k][1], c))
            return copy(w, k, 2 * chips[k][0] + chips[k][1], c, (x, y, 1 - c))

        def landed(w, k):
            return copy(w, k, 2 * chips[k][0] + chips[k][1], c if stage == "ici" else 1 - c, (x, y, 1 - c))

        return sent, landed

    def start(refs, sems):
        sent, _ = copies(refs, sems)
        for w in range(n):
            for k in range(3):
                sent(w, k).start()

    def finish(refs, sems):
        sent, landed = copies(refs, sems)
        for w in range(n):
            for k in range(3):
                landed(w, k).wait_recv()
                sent(w, k).wait_send()

    return _Job(ios=full, sems=[pltpu.SemaphoreType.DMA((n, 3))] * 2, start=start, finish=finish)


def _exchange_job(grads, axes):
    n = len(grads)

    def half_shape(w):
        return tuple(d // 2 if a != axes[w] else d for a, d in enumerate(grads[w].shape))

    def copy(refs, sems, w):
        x, y, c, _ = _mesh_position()
        return pltpu.make_async_remote_copy(
            src_ref=_region(refs[w], axes[w], None, 1 - c, 0, half_shape(w)[1 - axes[w]]), dst_ref=refs[n + w],
            send_sem=sems[0].at[w], recv_sem=sems[1].at[w], device_id=(x, y, 1 - c), device_id_type=MESH)

    def start(refs, sems):
        for w in range(n):
            copy(refs, sems, w).start()

    def finish(refs, sems):
        for w in range(n):
            copy(refs, sems, w).wait()

    return _Job(ins=grads, outs=[jax.ShapeDtypeStruct(half_shape(w), F32) for w in range(n)],
                sems=[pltpu.SemaphoreType.DMA((n,))] * 2, start=start, finish=finish)


def _half_block_spec(axis, block, half_blocks, use_half):
    if axis == 1:
        if use_half:
            return pl.BlockSpec(block, lambda i, pos: (pos[0] * half_blocks + i, 0))
        return pl.BlockSpec(block, lambda i, pos: (i, 0))
    if use_half:
        return pl.BlockSpec(block, lambda i, pos: (i, pos[0]))
    return pl.BlockSpec(block, lambda i, pos: (i, 0))


def _add_halves(grad, received, axis, pos, *, name):
    rows, cols = received.shape
    tr = _row_block(rows, cols)
    nb = rows // tr

    def body(pos_ref, g_ref, r_ref, o_ref):
        o_ref[...] = (g_ref[...] + r_ref[...]).astype(BF16)

    blk = (tr, cols)
    return pl.pallas_call(
        body, name=name, out_shape=jax.ShapeDtypeStruct((rows, cols), BF16),
        grid_spec=pltpu.PrefetchScalarGridSpec(
            num_scalar_prefetch=1, grid=(nb,),
            in_specs=[_half_block_spec(axis, blk, nb, True), _half_block_spec(axis, blk, nb, False)],
            out_specs=_half_block_spec(axis, blk, nb, False)),
        compiler_params=_params(("parallel",)),
    )(pos, grad, received)


def _send_sums_job(sums, axes):
    n = len(sums)

    def part_shape(w):
        return tuple(d // N_CHIPS if a == axes[w] else d for a, d in enumerate(sums[w].shape))

    def copy(refs, sems, w, k):
        x, y, c, chips = _mesh_position()
        shard = 2 * chips[k][0] + chips[k][1]
        return pltpu.make_async_remote_copy(
            src_ref=_region(refs[w], axes[w], shard, None, part_shape(w)[axes[w]], 0), dst_ref=refs[n + w].at[k],
            send_sem=sems[0].at[w, k], recv_sem=sems[1].at[w, k],
            device_id=(chips[k][0], chips[k][1], c), device_id_type=MESH)

    def start(refs, sems):
        for w in range(n):
            for k in range(3):
                copy(refs, sems, w, k).start()

    def finish(refs, sems):
        for w in range(n):
            for k in range(3):
                copy(refs, sems, w, k).wait()

    return _Job(ins=sums, outs=[jax.ShapeDtypeStruct((3,) + part_shape(w), BF16) for w in range(n)],
                sems=[pltpu.SemaphoreType.DMA((n, 3))] * 2, start=start, finish=finish)


def _sum_chip_parts(grad, received, parts, axis, pos, *, name):
    _, rows, cols = parts.shape
    tr = _row_block(rows, cols)
    nb = rows // tr
    blk = (tr, cols)

    def body(pos_ref, g_ref, r_ref, p_ref, o_ref):
        total = g_ref[...] + r_ref[...]
        for k in range(3):
            total = total + p_ref[k].astype(F32)
        o_ref[...] = total

    if axis == 1:
        g_spec = pl.BlockSpec(blk, lambda i, pos: (pos[0] * nb + i, pos[1]))
        r_spec = pl.BlockSpec(blk, lambda i, pos: (i, pos[1]))
        o_spec = pl.BlockSpec(blk, lambda i, pos: (pos[0] * nb + i, 0))
        shard_shape = (2 * rows, cols)
    else:
        g_spec = pl.BlockSpec(blk, lambda i, pos: (pos[1] * nb + i, pos[0]))
        r_spec = pl.BlockSpec(blk, lambda i, pos: (pos[1] * nb + i, 0))
        o_spec = pl.BlockSpec(blk, lambda i, pos: (i, pos[0]))
        shard_shape = (rows, 2 * cols)
    return pl.pallas_call(
        body, name=name, out_shape=jax.ShapeDtypeStruct(shard_shape, F32),
        grid_spec=pltpu.PrefetchScalarGridSpec(
            num_scalar_prefetch=1, grid=(nb,),
            in_specs=[g_spec, r_spec, pl.BlockSpec((3,) + blk, lambda i, pos: (0, i, 0))],
            out_specs=o_spec),
        compiler_params=_params(("parallel",)),
    )(pos, grad, received, parts)


def _join_job(shards, axes):
    n = len(shards)

    def copy(refs, sems, w, other):
        x, y, c, _ = _mesh_position()
        place = _region(refs[w], axes[w], None, 1 - c if other else c, 0, shards[w].shape[1 - axes[w]] // 2)
        return pltpu.make_async_remote_copy(
            src_ref=place, dst_ref=place, send_sem=sems[0].at[w], recv_sem=sems[1].at[w],
            device_id=(x, y, 1 - c), device_id_type=MESH)

    def start(refs, sems):
        for w in range(n):
            copy(refs, sems, w, False).start()

    def finish(refs, sems):
        for w in range(n):
            copy(refs, sems, w, True).wait_recv()
            copy(refs, sems, w, False).wait_send()

    return _Job(ios=shards, sems=[pltpu.SemaphoreType.DMA((n,))] * 2, start=start, finish=finish)


def _all_reduce_small(vec):
    rows, cols = vec.shape

    def body(v_ref, o_ref, land_ref, send_sem, recv_sem):
        x, y, c, _ = _mesh_position()
        me = 4 * x + 2 * y + c
        land_ref[me] = v_ref[...]
        copies = []
        for k in range(1, 8):
            px, py, pc = x ^ (k >> 2), y ^ ((k >> 1) & 1), c ^ (k & 1)
            copies.append(pltpu.make_async_remote_copy(
                src_ref=v_ref, dst_ref=land_ref.at[me], send_sem=send_sem.at[k], recv_sem=recv_sem.at[k],
                device_id=(px, py, pc), device_id_type=MESH))
        for cp in copies:
            cp.start()
        for k in range(1, 8):
            peer = me ^ k
            pltpu.make_async_remote_copy(
                src_ref=v_ref, dst_ref=land_ref.at[peer], send_sem=send_sem.at[k], recv_sem=recv_sem.at[k],
                device_id=(x, y, c), device_id_type=MESH).wait_recv()
        for cp in copies:
            cp.wait_send()
        total = land_ref[0]
        for k in range(1, 8):
            total = total + land_ref[k]
        o_ref[...] = total

    vmem = pl.BlockSpec(memory_space=pltpu.VMEM)
    return pl.pallas_call(
        body, name="all_reduce_small", in_specs=[vmem], out_specs=vmem,
        out_shape=jax.ShapeDtypeStruct((rows, cols), F32),
        scratch_shapes=[pltpu.VMEM((8, rows, cols), F32), pltpu.SemaphoreType.DMA((8,)), pltpu.SemaphoreType.DMA((8,))],
    )(vec)


def _adamw(w, g, m, v, *, name):
    rows, cols = w.shape
    tr = _row_block(rows, cols) if rows % 8 == 0 else rows
    bc1 = 1.0 - ADAM_B1 ** ADAM_STEP
    bc2 = 1.0 - ADAM_B2 ** ADAM_STEP

    def body(w_ref, g_ref, m_ref, v_ref, go_ref, d_ref, mo_ref, vo_ref):
        gv = g_ref[...]
        go_ref[...] = gv
        mn = ADAM_B1 * m_ref[...] + (1.0 - ADAM_B1) * gv
        vn = ADAM_B2 * v_ref[...] + (1.0 - ADAM_B2) * (gv * gv)
        mo_ref[...] = mn
        vo_ref[...] = vn
        d_ref[...] = -ADAM_LR * ((mn / bc1) / (jnp.sqrt(vn / bc2) + ADAM_EPS) + ADAM_WD * w_ref[...])

    blk = pl.BlockSpec((tr, cols), lambda i: (i, 0))
    shape = jax.ShapeDtypeStruct((rows, cols), F32)
    return pl.pallas_call(
        body, name=name, grid=(rows // tr,), in_specs=[blk] * 4, out_specs=[blk] * 4, out_shape=[shape] * 4,
        compiler_params=_params(("parallel",)),
    )(w, g, m, v)


def _to_bf16_in_place(w, axis, pos, *, name):
    rows, cols = w.shape
    tr = _row_block(rows, cols)
    nb = rows // tr

    def body(pos_ref, w_ref, o_ref):
        o_ref[...] = w_ref[...].astype(BF16)

    if axis == 1:
        o_spec = pl.BlockSpec((tr, cols), lambda i, pos: (i, pos[1]))
        full_shape = (rows, N_CHIPS * cols)
    else:
        o_spec = pl.BlockSpec((tr, cols), lambda i, pos: (pos[1] * nb + i, 0))
        full_shape = (N_CHIPS * rows, cols)
    return pl.pallas_call(
        body, name=name, out_shape=jax.ShapeDtypeStruct(full_shape, BF16),
        grid_spec=pltpu.PrefetchScalarGridSpec(
            num_scalar_prefetch=1, grid=(nb,),
            in_specs=[pl.BlockSpec((tr, cols), lambda i, pos: (i, 0))], out_specs=o_spec),
        compiler_params=_params(("parallel",)),
    )(pos, w)


BIG = ("w_in", "w_out", "w_gate", "w_up", "w_down")
BIG_AXIS = dict(w_in=1, w_out=0, w_gate=1, w_up=1, w_down=0)
SMALL = ("norm_mix_w", "ret_decay_fwd", "ret_decay_bwd", "ret_norm_w", "norm_ffn_w", "norm_final_w")
ALL_WEIGHTS = ("norm_mix_w", "w_in", "ret_decay_fwd", "ret_decay_bwd", "ret_norm_w", "w_out", "norm_ffn_w",
               "w_gate", "w_up", "w_down", "norm_final_w")
SMALL_ROW = 128 * 8


def _pack_small(small):
    pieces = [jnp.reshape(small["loss"], (1,))] + [jnp.reshape(small[k], (-1,)) for k in SMALL]
    rows = []
    for p in pieces:
        pad = -p.shape[0] % (8 * SMALL_ROW)
        rows.append(jnp.reshape(jnp.pad(p, (0, pad)), (-1, SMALL_ROW)))
    return jnp.concatenate(rows, axis=0)


def _unpack_small(block, like):
    out, row = {}, 0
    for k in ("loss",) + SMALL:
        size = 1 if k == "loss" else like[k].size
        nrows = -(-size // (8 * SMALL_ROW)) * 8
        out[k] = jnp.reshape(block[row:row + nrows], (-1,))[:size]
        row += nrows
    return out


def kernel(x, norm_mix_w, w_in, ret_decay_fwd, ret_decay_bwd, ret_norm_w, w_out, norm_ffn_w, w_gate, w_up, w_down, norm_final_w, loss_target, m_norm_mix_w, m_w_in, m_ret_decay_fwd, m_ret_decay_bwd, m_ret_norm_w, m_w_out, m_norm_ffn_w, m_w_gate, m_w_up, m_w_down, m_norm_final_w, v_norm_mix_w, v_w_in, v_ret_decay_fwd, v_ret_decay_bwd, v_ret_norm_w, v_w_out, v_norm_ffn_w, v_w_gate, v_w_up, v_w_down, v_norm_final_w):
    weights = dict(norm_mix_w=norm_mix_w, w_in=w_in, ret_decay_fwd=ret_decay_fwd, ret_decay_bwd=ret_decay_bwd,
                   ret_norm_w=ret_norm_w, w_out=w_out, norm_ffn_w=norm_ffn_w, w_gate=w_gate, w_up=w_up,
                   w_down=w_down, norm_final_w=norm_final_w)
    m_in = dict(norm_mix_w=m_norm_mix_w, w_in=m_w_in, ret_decay_fwd=m_ret_decay_fwd, ret_decay_bwd=m_ret_decay_bwd,
                ret_norm_w=m_ret_norm_w, w_out=m_w_out, norm_ffn_w=m_norm_ffn_w, w_gate=m_w_gate, w_up=m_w_up,
                w_down=m_w_down, norm_final_w=m_norm_final_w)
    v_in = dict(norm_mix_w=v_norm_mix_w, w_in=v_w_in, ret_decay_fwd=v_ret_decay_fwd, ret_decay_bwd=v_ret_decay_bwd,
                ret_norm_w=v_ret_norm_w, w_out=v_w_out, norm_ffn_w=v_norm_ffn_w, w_gate=v_w_gate, w_up=v_w_up,
                w_down=v_w_down, norm_final_w=v_norm_final_w)
    pos = jnp.stack([lax.axis_index("c"), 2 * lax.axis_index("x") + lax.axis_index("y")]).astype(jnp.int32)

    own = {k: _to_bf16_in_place(weights[k][0], BIG_AXIS[k], pos, name="cast_" + k) for k in BIG}

    dx, grad_w, small = _step(
        x[0], loss_target[0], norm_mix_w, ret_decay_fwd[0], ret_decay_bwd[0], ret_norm_w, norm_ffn_w,
        norm_final_w[None, :], own, pos)

    like = {k: weights[k] for k in SMALL}
    reduced = _unpack_small(_all_reduce_small(_pack_small(small)), like)
    loss = reduced["loss"][0]
    for k in SMALL:
        grad_w[k] = jnp.reshape(reduced[k], (1, -1))

    delta, new_m, new_v = {}, {}, {}
    for k in ALL_WEIGHTS:
        shape = weights[k].shape
        as2d = (lambda t: jnp.reshape(t, (-1, shape[-1])))
        grad_w[k], delta[k], new_m[k], new_v[k] = (jnp.reshape(t, shape) for t in _adamw(
            as2d(weights[k]), as2d(grad_w[k]), as2d(m_in[k]), as2d(v_in[k]), name="adamw_" + k))

    return (loss, dx[None], *[grad_w[k] for k in ALL_WEIGHTS], *[delta[k] for k in ALL_WEIGHTS],
            *[new_m[k] for k in ALL_WEIGHTS], *[new_v[k] for k in ALL_WEIGHTS])
```

```python
import functools
import math

import numpy as np
import jax
import jax.numpy as jnp
from jax import lax
from jax.experimental import pallas as pl
from jax.experimental.pallas import tpu as pltpu

F32 = jnp.float32
BF16 = jnp.bfloat16
MESH = pl.DeviceIdType.MESH

HEAD_DIM = 128
RET_CHUNK = 128
RET_UNROLL = 8
EPS = 1e-6
DILATED_PATTERNS = ((128, 1), (512, 4), (2048, 16))
ATT_BLOCK = 256
ATT_REACH = max(w // 2 for w, _ in DILATED_PATTERNS)
ATT_NEAR = ATT_BLOCK
ATT_CLASSES = DILATED_PATTERNS[-1][1]
assert all(w // 2 <= ATT_NEAR for w, _ in DILATED_PATTERNS[:-1])
ATT_KB = -(-ATT_NEAR // ATT_BLOCK)
ATT_WINDOW = 2 * ATT_KB + 1
ATT_FAR_GROUP = 4
ATT_NEAR_GROUP = 2
MASKED = -1e30
ROW_MAX_INIT = -1e29
N_CHIPS = 4
VMEM_LIMIT_BYTES = 56 * 1024 * 1024
ELEM_BLOCK_BYTES = 2 * 1024 * 1024

ADAM_LR = 0.001
ADAM_B1 = 0.9
ADAM_B2 = 0.999
ADAM_EPS = 1e-08
ADAM_WD = 0.01
ADAM_STEP = 10


def _params(sem=None):
    return pltpu.CompilerParams(dimension_semantics=sem, vmem_limit_bytes=VMEM_LIMIT_BYTES)


def _sigmoid(x):
    return 1.0 / (1.0 + jnp.exp(-x))


class _Job:
    def __init__(self, *, ins=(), ios=(), outs=(), sems=(), start, finish):
        self.ins, self.ios, self.outs, self.sems = list(ins), list(ios), list(outs), list(sems)
        self.start, self.finish = start, finish

    def results(self):
        return [jax.ShapeDtypeStruct(a.shape, a.dtype) for a in self.ios] + self.outs


def _call(body, *, name, grid, in_specs, out_specs, out_shape, operands, scratch_shapes=(), semantics=None, jobs=()):
    in_specs, out_specs, out_shape = list(in_specs), list(out_specs), list(out_shape)
    scratch_shapes = list(scratch_shapes)
    if not jobs:
        outs = pl.pallas_call(body, name=name, grid=grid, in_specs=in_specs, out_specs=out_specs, out_shape=out_shape,
                              scratch_shapes=scratch_shapes, compiler_params=_params(semantics))(*operands)
        return outs, []
    n_in, n_out, n_scratch = len(in_specs), len(out_specs), len(scratch_shapes)
    extra_in, extra_out, sems, aliases = [], [], [], {}
    for job in jobs:
        extra_in += job.ins
        for t in range(len(job.ios)):
            aliases[n_in + len(extra_in) + t] = n_out + len(extra_out) + t
        extra_in += job.ios
        extra_out += job.results()
        sems += job.sems

    def carried(*refs):
        x_in = refs[n_in:n_in + len(extra_in)]
        x_out = refs[n_in + len(extra_in) + n_out:n_in + len(extra_in) + n_out + len(extra_out)]
        x_sem = refs[len(refs) - len(sems):]
        views, i_in, i_out, i_sem = [], 0, 0, 0
        for job in jobs:
            data = list(x_in[i_in:i_in + len(job.ins)]) + list(x_out[i_out:i_out + len(job.results())])
            views.append((data, x_sem[i_sem:i_sem + len(job.sems)]))
            i_in += len(job.ins) + len(job.ios)
            i_out += len(job.results())
            i_sem += len(job.sems)
        steps = [pl.program_id(d) for d in range(len(grid))]

        @pl.when(functools.reduce(jnp.logical_and, [s == 0 for s in steps]))
        def _():
            for job, (data, sem) in zip(jobs, views):
                job.start(data, sem)

        body(*refs[:n_in], *refs[n_in + len(extra_in):n_in + len(extra_in) + n_out],
             *refs[len(refs) - len(sems) - n_scratch:len(refs) - len(sems)])

        @pl.when(functools.reduce(jnp.logical_and, [s == g - 1 for s, g in zip(steps, grid)]))
        def _():
            for job, (data, sem) in zip(jobs, views):
                job.finish(data, sem)

    hbm = pl.BlockSpec(memory_space=pl.ANY)
    res = pl.pallas_call(
        carried, name=name, grid=grid, in_specs=in_specs + [hbm] * len(extra_in),
        out_specs=out_specs + [hbm] * len(extra_out), out_shape=out_shape + extra_out,
        input_output_aliases=aliases, scratch_shapes=scratch_shapes + sems,
        compiler_params=_params(("arbitrary",) * len(grid)),
    )(*operands, *extra_in)
    carried_results, at = [], n_out
    for job in jobs:
        carried_results.append(list(res[at:at + len(job.results())]))
        at += len(job.results())
    return list(res[:n_out]), carried_results


def _run_jobs(jobs, *, name):
    first = jobs[0]
    n_in, n_io = len(first.ins), len(first.ios)
    out_shape = first.results()
    n_sems = [len(job.sems) for job in jobs]

    def body(*refs):
        data = list(refs[:n_in]) + list(refs[n_in + n_io:n_in + n_io + len(out_shape)])
        at = n_in + n_io + len(out_shape)
        for job, ns in zip(jobs, n_sems):
            job.start(data, refs[at:at + ns])
            job.finish(data, refs[at:at + ns])
            at += ns

    hbm = pl.BlockSpec(memory_space=pl.ANY)
    return pl.pallas_call(
        body, name=name, in_specs=[hbm] * (n_in + n_io), out_specs=[hbm] * len(out_shape), out_shape=out_shape,
        input_output_aliases={n_in + t: t for t in range(n_io)},
        scratch_shapes=[s for job in jobs for s in job.sems],
    )(*first.ins, *first.ios)


def _dot(a, b, ta=False, tb=False):
    return lax.dot_general(a, b, (((0 if ta else 1,), (1 if tb else 0,)), ((), ())),
                           preferred_element_type=F32)


def _tile(n, want):
    t = min(n, want) // 128 * 128
    while n % t:
        t -= 128
    return t


def _a_spec(ta, tm, tk):
    return pl.BlockSpec((tk, tm), lambda i, j, k: (k, i)) if ta else pl.BlockSpec((tm, tk), lambda i, j, k: (i, k))


def _b_spec(tb, tk, tn):
    return pl.BlockSpec((tn, tk), lambda i, j, k: (j, k)) if tb else pl.BlockSpec((tk, tn), lambda i, j, k: (k, j))


def _accumulate(accs, nk, products, finish):
    if nk == 1:
        finish(*products())
        return
    k = pl.program_id(2)

    @pl.when(k == 0)
    def _():
        for acc, p in zip(accs, products()):
            acc[...] = p

    if nk > 2:
        @pl.when(jnp.logical_and(k > 0, k < nk - 1))
        def _():
            for acc, p in zip(accs, products()):
                acc[...] += p

    @pl.when(k == nk - 1)
    def _():
        finish(*[acc[...] + p for acc, p in zip(accs, products())])


def _matmul(a, b, *, name, ta=False, tb=False, out_dtype=F32, residual=None, tm=1024, tn=1024, tk=2048, jobs=()):
    m, kdim = (a.shape[1], a.shape[0]) if ta else a.shape
    n = b.shape[0] if tb else b.shape[1]
    tm, tn, tk = _tile(m, tm), _tile(n, tn), _tile(kdim, tk)
    nk = kdim // tk

    def body(*refs):
        a_ref, b_ref = refs[:2]
        r_ref = refs[2] if residual is not None else None
        o_ref = refs[-1] if nk == 1 else refs[-2]

        def finish(total):
            if residual is not None:
                total = total + r_ref[...]
            o_ref[...] = total.astype(out_dtype)

        _accumulate(refs[-1:] if nk > 1 else (), nk, lambda: (_dot(a_ref[...], b_ref[...], ta, tb),), finish)

    o_spec = pl.BlockSpec((tm, tn), lambda i, j, k: (i, j))
    in_specs = [_a_spec(ta, tm, tk), _b_spec(tb, tk, tn)]
    operands = [a, b]
    if residual is not None:
        in_specs.append(o_spec)
        operands.append(residual)
    (out,), carried = _call(
        body, name=name, grid=(m // tm, n // tn, nk), in_specs=in_specs, out_specs=[o_spec],
        out_shape=[jax.ShapeDtypeStruct((m, n), out_dtype)], operands=operands,
        scratch_shapes=[pltpu.VMEM((tm, tn), F32)] * (nk > 1),
        semantics=("parallel", "parallel", "arbitrary"), jobs=jobs)
    return (out, carried) if jobs else out


def _weight_grad(a, g, *, name, jobs=()):
    tokens, m = a.shape
    tm = 1024 if m % 1024 == 0 else _tile(m, 1408)
    return _matmul(a, g, name=name, ta=True, tm=tm, tn=512, tk=tokens, jobs=jobs)


def _swiglu_fwd(n2, w_gate, w_up, *, tm=1024, tn=512, tk=2048, jobs=()):
    m, kdim = n2.shape
    n = w_gate.shape[1]
    tm, tn, tk = _tile(m, tm), _tile(n, tn), _tile(kdim, tk)
    nk = kdim // tk

    def body(a_ref, g_ref, u_ref, gate_ref, up_ref, act_ref, *acc):
        def products():
            a = a_ref[...]
            return _dot(a, g_ref[...]), _dot(a, u_ref[...])

        def finish(g, u):
            gate_ref[...] = g.astype(BF16)
            up_ref[...] = u.astype(BF16)
            act_ref[...] = (g * _sigmoid(g) * u).astype(BF16)

        _accumulate(acc, nk, products, finish)

    o_spec = pl.BlockSpec((tm, tn), lambda i, j, k: (i, j))
    o_shape = jax.ShapeDtypeStruct((m, n), BF16)
    return _call(
        body, name="swiglu_fwd", grid=(m // tm, n // tn, nk),
        in_specs=[_a_spec(False, tm, tk), _b_spec(False, tk, tn), _b_spec(False, tk, tn)],
        out_specs=[o_spec] * 3, out_shape=[o_shape] * 3, operands=[n2, w_gate, w_up],
        scratch_shapes=[pltpu.VMEM((tm, tn), F32)] * (2 * (nk > 1)),
        semantics=("parallel", "parallel", "arbitrary"), jobs=jobs)


def _swiglu_bwd_act(dh2, w_down, gate, up, *, tm=1024, tn=512, tk=2048):
    m, kdim = dh2.shape
    n = w_down.shape[0]
    tm, tn, tk = _tile(m, tm), _tile(n, tn), _tile(kdim, tk)
    nk = kdim // tk

    def body(a_ref, b_ref, gate_ref, up_ref, dgate_ref, dup_ref, *acc):
        def finish(dact):
            g = gate_ref[...].astype(F32)
            u = up_ref[...].astype(F32)
            sg = _sigmoid(g)
            dup_ref[...] = (dact * g * sg).astype(BF16)
            dgate_ref[...] = (dact * u * sg * (1.0 + g * (1.0 - sg))).astype(BF16)

        _accumulate(acc, nk, lambda: (_dot(a_ref[...], b_ref[...], tb=True),), finish)

    o_spec = pl.BlockSpec((tm, tn), lambda i, j, k: (i, j))
    o_shape = jax.ShapeDtypeStruct((m, n), BF16)
    return pl.pallas_call(
        body, name="swiglu_bwd_act", grid=(m // tm, n // tn, nk),
        in_specs=[_a_spec(False, tm, tk), _b_spec(True, tk, tn), o_spec, o_spec],
        out_specs=[o_spec] * 2, out_shape=[o_shape] * 2,
        scratch_shapes=[pltpu.VMEM((tm, tn), F32)] * (nk > 1),
        compiler_params=_params(("parallel", "parallel", "arbitrary")),
    )(dh2, w_down, gate, up)


def _swiglu_bwd_in(dgate, dup, w_gate, w_up, *, tm=1024, tn=1024, tk=1408, jobs=()):
    m, kdim = dgate.shape
    n = w_gate.shape[0]
    tm, tn, tk = _tile(m, tm), _tile(n, tn), _tile(kdim, tk)
    nk = kdim // tk

    def body(a1_ref, a2_ref, b1_ref, b2_ref, o_ref, *acc):
        def product():
            return (_dot(a1_ref[...], b1_ref[...], tb=True) + _dot(a2_ref[...], b2_ref[...], tb=True),)

        def finish(total):
            o_ref[...] = total

        _accumulate(acc, nk, product, finish)

    a_spec, b_spec = _a_spec(False, tm, tk), _b_spec(True, tk, tn)
    (out,), carried = _call(
        body, name="swiglu_bwd_in", grid=(m // tm, n // tn, nk),
        in_specs=[a_spec, a_spec, b_spec, b_spec],
        out_specs=[pl.BlockSpec((tm, tn), lambda i, j, k: (i, j))],
        out_shape=[jax.ShapeDtypeStruct((m, n), F32)], operands=[dgate, dup, w_gate, w_up],
        scratch_shapes=[pltpu.VMEM((tm, tn), F32)] * (nk > 1),
        semantics=("parallel", "parallel", "arbitrary"), jobs=jobs)
    return out, carried


def _row_block(rows, cols):
    tr = min(rows, max(16, ELEM_BLOCK_BYTES // (4 * cols) // 16 * 16))
    while rows % tr:
        tr -= 16
    return tr


def _rmsnorm_fwd(x, g, *, name):
    s, d = x.shape
    tr = _row_block(s, d)

    def body(x_ref, g_ref, n_ref):
        xv = x_ref[...]
        r = lax.rsqrt(jnp.mean(xv * xv, axis=-1, keepdims=True) + EPS)
        n_ref[...] = (xv * r * g_ref[...]).astype(BF16)

    row = pl.BlockSpec((tr, d), lambda i: (i, 0))
    return pl.pallas_call(
        body, name=name, grid=(s // tr,), in_specs=[row, pl.BlockSpec((1, d), lambda i: (0, 0))],
        out_specs=row, out_shape=jax.ShapeDtypeStruct((s, d), BF16),
        compiler_params=_params(("parallel",)),
    )(x, g)


def _rmsnorm_bwd_rows(xv, gv, dy):
    r = lax.rsqrt(jnp.mean(xv * xv, axis=-1, keepdims=True) + EPS)
    xhat = xv * r
    dxh = dy * gv
    dx = r * (dxh - xhat * jnp.mean(dxh * xhat, axis=-1, keepdims=True))
    return dx, dy * xhat


def _rmsnorm_bwd(dn, x, g, skip, *, name):
    s, d = x.shape
    tr = _row_block(s, d)

    def body(dn_ref, x_ref, g_ref, skip_ref, dx_ref, dxb_ref, dg_ref):
        dx, dgr = _rmsnorm_bwd_rows(x_ref[...], g_ref[...], dn_ref[...])
        dx = dx + skip_ref[...]
        dx_ref[...] = dx
        dxb_ref[...] = dx.astype(BF16)

        @pl.when(pl.program_id(0) == 0)
        def _():
            dg_ref[...] = jnp.zeros_like(dg_ref)

        dg_ref[...] += jnp.sum(dgr, axis=0, keepdims=True)

    row = pl.BlockSpec((tr, d), lambda i: (i, 0))
    vec = pl.BlockSpec((1, d), lambda i: (0, 0))
    return pl.pallas_call(
        body, name=name, grid=(s // tr,), in_specs=[row, row, vec, row],
        out_specs=[row, row, vec],
        out_shape=[jax.ShapeDtypeStruct((s, d), F32), jax.ShapeDtypeStruct((s, d), BF16),
                   jax.ShapeDtypeStruct((1, d), F32)],
        compiler_params=_params(("arbitrary",)),
    )(dn, x, g, skip)


def _loss_head(h2, g, target):
    s, d = h2.shape
    tr = _row_block(s, d)

    def body(h_ref, g_ref, t_ref, dh_ref, dhb_ref, dg_ref, loss_ref):
        hv = h_ref[...]
        gv = g_ref[...]
        r = lax.rsqrt(jnp.mean(hv * hv, axis=-1, keepdims=True) + EPS)
        err = hv * r * gv - t_ref[...]
        dx, dgr = _rmsnorm_bwd_rows(hv, gv, err * (1.0 / d))
        dh_ref[...] = dx
        dhb_ref[...] = dx.astype(BF16)

        @pl.when(pl.program_id(0) == 0)
        def _():
            dg_ref[...] = jnp.zeros_like(dg_ref)
            loss_ref[...] = jnp.zeros_like(loss_ref)

        dg_ref[...] += jnp.sum(dgr, axis=0, keepdims=True)
        row_loss = jnp.mean(err * err, axis=-1, keepdims=True)
        loss_ref[...] += 0.5 * jnp.sum(row_loss, axis=0, keepdims=True)

    row = pl.BlockSpec((tr, d), lambda i: (i, 0))
    vec = pl.BlockSpec((1, d), lambda i: (0, 0))
    one = pl.BlockSpec((1, 1), lambda i: (0, 0))
    return pl.pallas_call(
        body, name="loss_head", grid=(s // tr,), in_specs=[row, vec, row],
        out_specs=[row, row, vec, one],
        out_shape=[jax.ShapeDtypeStruct((s, d), F32), jax.ShapeDtypeStruct((s, d), BF16),
                   jax.ShapeDtypeStruct((1, d), F32), jax.ShapeDtypeStruct((1, 1), F32)],
        compiler_params=_params(("arbitrary",)),
    )(h2, g, target)


def _attention_bias_tables():
    k = np.arange(-ATT_KB, ATT_KB + 1)[:, None, None]
    delta = k * ATT_BLOCK + np.arange(ATT_BLOCK)[None, None, :] - np.arange(ATT_BLOCK)[None, :, None]
    dist = np.abs(delta)
    count = np.zeros(delta.shape, np.int32)
    for window, dilation in DILATED_PATTERNS:
        count += (delta % dilation == 0) & (dist <= min(window // 2, ATT_NEAR))
    logc = np.where(count > 0, np.log(np.maximum(count, 1)), MASKED)
    return dist.astype(np.float32), logc.astype(np.float32)


def _far_bias_tables(per_class):
    steps = np.abs(np.arange(per_class)[:, None] - np.arange(per_class)[None, :]) * ATT_CLASSES
    valid = (steps > ATT_NEAR) & (steps <= ATT_REACH)
    return steps.astype(np.float32), np.where(valid, 0.0, MASKED).astype(np.float32)


def _to_classes(x):
    s, cols = x.shape
    return jnp.reshape(jnp.transpose(jnp.reshape(x, (s // ATT_CLASSES, ATT_CLASSES, cols)), (1, 0, 2)), (s, cols))


def _from_classes(x):
    s, cols = x.shape
    return jnp.reshape(jnp.transpose(jnp.reshape(x, (ATT_CLASSES, s // ATT_CLASSES, cols)), (1, 0, 2)), (s, cols))


def _head_bias(bias_ref, slope, dist_ref, logc_ref):
    for kk in range(ATT_WINDOW):
        bias_ref[kk] = logc_ref[kk] - slope * dist_ref[kk]
    bias_ref[ATT_WINDOW] = jnp.full((ATT_BLOCK, ATT_BLOCK), MASKED, F32)


def _window_start(i, nq, nwin):
    return jnp.clip(i - ATT_KB, 0, nq - nwin)


def _window_block(j, i):
    rows = pl.ds(pl.multiple_of(j * ATT_BLOCK, ATT_BLOCK), ATT_BLOCK)
    kk = j - i + ATT_KB
    return rows, jnp.where(jnp.logical_and(kk >= 0, kk < ATT_WINDOW), kk, ATT_WINDOW)


def _attention_far_fwd(qkv, slopes, n_heads):
    s = qkv.shape[0]
    per_class = s // ATT_CLASSES
    scale = HEAD_DIM ** -0.5
    dist, logc = _far_bias_tables(per_class)

    def body(slope_ref, q_ref, k_ref, v_ref, dist_ref, logc_ref, o_ref, lse_ref):
        bias = logc_ref[...] - slope_ref[pl.program_id(0)] * dist_ref[...]
        for a in range(ATT_FAR_GROUP):
            rows = pl.ds(a * per_class, per_class)
            sc = _dot(q_ref[rows, :], k_ref[rows, :], tb=True) * scale + bias
            m = jnp.maximum(jnp.max(sc, axis=-1, keepdims=True), ROW_MAX_INIT)
            p = jnp.exp(sc - m)
            l = jnp.maximum(jnp.sum(p, axis=-1, keepdims=True), 1e-30)
            o_ref[rows, :] = (_dot(p.astype(BF16), v_ref[rows, :]) / l).astype(BF16)
            lse_ref[rows, :] = jnp.broadcast_to(m + jnp.log(l), (per_class, HEAD_DIM))

    hh = n_heads
    blk = pl.BlockSpec((ATT_FAR_GROUP * per_class, HEAD_DIM), lambda h, r: (r, h))
    table = pl.BlockSpec(dist.shape, lambda h, r: (0, 0))
    return pl.pallas_call(
        body, name="attention_far_fwd", grid=(hh, ATT_CLASSES // ATT_FAR_GROUP),
        in_specs=[pl.BlockSpec(memory_space=pltpu.SMEM), blk,
                  pl.BlockSpec((ATT_FAR_GROUP * per_class, HEAD_DIM), lambda h, r: (r, hh + h)),
                  pl.BlockSpec((ATT_FAR_GROUP * per_class, HEAD_DIM), lambda h, r: (r, 2 * hh + h)), table, table],
        out_specs=[blk, blk],
        out_shape=[jax.ShapeDtypeStruct((s, hh * HEAD_DIM), BF16), jax.ShapeDtypeStruct((s, hh * HEAD_DIM), F32)],
        compiler_params=_params(("parallel", "parallel")),
    )(slopes, qkv, qkv, qkv, jnp.asarray(dist), jnp.asarray(logc))


def _attention_fwd(proj, slopes, far_out, far_lse, n_heads, jobs=()):
    s = proj.shape[0]
    nq = s // ATT_BLOCK
    scale = HEAD_DIM ** -0.5
    dist, logc = _attention_bias_tables()

    nwin = min(ATT_WINDOW, nq)

    group = math.gcd(ATT_NEAR_GROUP, nq)

    def body(slope_ref, q_ref, k_ref, v_ref, fo_ref, fl_ref, dist_ref, logc_ref, o_ref, lse_ref, bias_ref, s_ref):
        h, step = pl.program_id(0), pl.program_id(1)

        @pl.when(step == 0)
        def _():
            _head_bias(bias_ref, slope_ref[h], dist_ref, logc_ref)

        for a in range(group):
            i = step * group + a
            mine = pl.ds(a * ATT_BLOCK, ATT_BLOCK)
            q = q_ref[mine, :]
            first = _window_start(i, nq, nwin)
            m = jnp.full((ATT_BLOCK, 1), ROW_MAX_INIT, F32)
            for b in range(nwin):
                rows, kk = _window_block(first + b, i)
                sc = _dot(q, k_ref[rows, :], tb=True) * scale + bias_ref[kk]
                s_ref[a * nwin + b] = sc
                m = jnp.maximum(m, jnp.max(sc, axis=-1, keepdims=True))
            l = jnp.zeros((ATT_BLOCK, 1), F32)
            acc = jnp.zeros((ATT_BLOCK, HEAD_DIM), F32)
            for b in range(nwin):
                rows, _ = _window_block(first + b, i)
                p = jnp.exp(s_ref[a * nwin + b] - m)
                l = l + jnp.sum(p, axis=-1, keepdims=True)
                acc = acc + _dot(p.astype(BF16), v_ref[rows, :])
            near_lse = m + jnp.log(l)
            far_lse_col = fl_ref[mine, :1]
            lse = jnp.maximum(near_lse, far_lse_col)
            lse = lse + jnp.log(jnp.exp(near_lse - lse) + jnp.exp(far_lse_col - lse))
            o_ref[mine, :] = (acc * (jnp.exp(near_lse - lse) / l)
                              + fo_ref[mine, :].astype(F32) * jnp.exp(far_lse_col - lse)).astype(BF16)
            lse_ref[mine, :] = jnp.broadcast_to(lse, (ATT_BLOCK, HEAD_DIM))

    hh = n_heads
    blk = pl.BlockSpec((group * ATT_BLOCK, HEAD_DIM), lambda h, i: (i, h))
    table = pl.BlockSpec(dist.shape, lambda h, i: (0, 0, 0))
    return _call(
        body, name="attention_fwd", grid=(hh, nq // group),
        in_specs=[pl.BlockSpec(memory_space=pltpu.SMEM), blk,
                  pl.BlockSpec((s, HEAD_DIM), lambda h, i: (0, hh + h)),
                  pl.BlockSpec((s, HEAD_DIM), lambda h, i: (0, 2 * hh + h)), blk, blk, table, table],
        out_specs=[blk, blk],
        out_shape=[jax.ShapeDtypeStruct((s, hh * HEAD_DIM), BF16), jax.ShapeDtypeStruct((s, hh * HEAD_DIM), F32)],
        operands=[slopes, proj, proj, proj, far_out, far_lse, jnp.asarray(dist), jnp.asarray(logc)],
        scratch_shapes=[pltpu.VMEM((ATT_WINDOW + 1, ATT_BLOCK, ATT_BLOCK), F32),
                        pltpu.VMEM((group * nwin, ATT_BLOCK, ATT_BLOCK), F32)],
        semantics=("parallel", "arbitrary"), jobs=jobs)


def _attention_far_bwd(qkv, slopes, out, dout, lse, n_heads):
    s = qkv.shape[0]
    per_class = s // ATT_CLASSES
    scale = HEAD_DIM ** -0.5
    dist, logc = _far_bias_tables(per_class)

    def body(slope_ref, q_ref, k_ref, v_ref, o_ref, do_ref, lse_ref, dist_ref, logc_ref, dq_ref, dk_ref, dv_ref):
        bias = logc_ref[...] - slope_ref[pl.program_id(0)] * dist_ref[...]
        for a in range(ATT_FAR_GROUP):
            rows = pl.ds(a * per_class, per_class)
            q, k, do = q_ref[rows, :], k_ref[rows, :], do_ref[rows, :]
            delta = jnp.sum(do.astype(F32) * o_ref[rows, :].astype(F32), axis=-1, keepdims=True)
            p = jnp.exp(_dot(q, k, tb=True) * scale + bias - lse_ref[rows, :1])
            dv_ref[rows, :] = _dot(p.astype(BF16), do, ta=True).astype(BF16)
            ds = (p * (_dot(do, v_ref[rows, :], tb=True) - delta) * scale).astype(BF16)
            dk_ref[rows, :] = _dot(ds, q, ta=True).astype(BF16)
            dq_ref[rows, :] = _dot(ds, k).astype(BF16)

    hh = n_heads
    blk = pl.BlockSpec((ATT_FAR_GROUP * per_class, HEAD_DIM), lambda h, r: (r, h))
    table = pl.BlockSpec(dist.shape, lambda h, r: (0, 0))
    o_shape = jax.ShapeDtypeStruct((s, hh * HEAD_DIM), BF16)
    return pl.pallas_call(
        body, name="attention_far_bwd", grid=(hh, ATT_CLASSES // ATT_FAR_GROUP),
        in_specs=[pl.BlockSpec(memory_space=pltpu.SMEM), blk,
                  pl.BlockSpec((ATT_FAR_GROUP * per_class, HEAD_DIM), lambda h, r: (r, hh + h)),
                  pl.BlockSpec((ATT_FAR_GROUP * per_class, HEAD_DIM), lambda h, r: (r, 2 * hh + h)),
                  blk, blk, blk, table, table],
        out_specs=[blk] * 3, out_shape=[o_shape] * 3,
        compiler_params=_params(("parallel", "parallel")),
    )(slopes, qkv, qkv, qkv, out, dout, lse, jnp.asarray(dist), jnp.asarray(logc))


def _attention_bwd(proj, slopes, out, lse, dmixed, far_grads, n_heads, jobs=()):
    s = proj.shape[0]
    nq = s // ATT_BLOCK
    scale = HEAD_DIM ** -0.5
    dist, logc = _attention_bias_tables()

    nwin = min(ATT_WINDOW, nq)
    group = math.gcd(ATT_NEAR_GROUP, nq)

    def body(slope_ref, q_ref, k_ref, v_ref, o_ref, do_ref, lse_ref, fdq_ref, fdk_ref, fdv_ref, dist_ref, logc_ref,
             dq_ref, dk_ref, dv_ref, dk_acc, dv_acc, bias_ref):
        h, step = pl.program_id(0), pl.program_id(1)

        @pl.when(step == 0)
        def _():
            dk_acc[...] = jnp.zeros_like(dk_acc)
            dv_acc[...] = jnp.zeros_like(dv_acc)
            _head_bias(bias_ref, slope_ref[h], dist_ref, logc_ref)

        for a in range(group):
            i = step * group + a
            mine = pl.ds(a * ATT_BLOCK, ATT_BLOCK)
            q = q_ref[mine, :]
            do = do_ref[mine, :]
            lse_col = lse_ref[mine, :1]
            delta = jnp.sum(do.astype(F32) * o_ref[mine, :].astype(F32), axis=-1, keepdims=True)
            first = _window_start(i, nq, nwin)
            dq = jnp.zeros((ATT_BLOCK, HEAD_DIM), F32)
            for b in range(nwin):
                rows, kk = _window_block(first + b, i)
                kj = k_ref[rows, :]
                vj = v_ref[rows, :]
                p = jnp.exp(_dot(q, kj, tb=True) * scale + bias_ref[kk] - lse_col)
                dv_acc[rows, :] += _dot(p.astype(BF16), do, ta=True)
                dp = _dot(do, vj, tb=True)
                ds = (p * (dp - delta) * scale).astype(BF16)
                dk_acc[rows, :] += _dot(ds, q, ta=True)
                dq = dq + _dot(ds, kj)
            dq_ref[mine, :] = (dq + fdq_ref[mine, :].astype(F32)).astype(BF16)

        @pl.when(step == nq // group - 1)
        def _():
            dk_ref[...] = (dk_acc[...] + fdk_ref[...].astype(F32)).astype(BF16)
            dv_ref[...] = (dv_acc[...] + fdv_ref[...].astype(F32)).astype(BF16)

    hh = n_heads
    blk = pl.BlockSpec((group * ATT_BLOCK, HEAD_DIM), lambda h, i: (i, h))
    col = pl.BlockSpec((s, HEAD_DIM), lambda h, i: (0, h))
    table = pl.BlockSpec(dist.shape, lambda h, i: (0, 0, 0))
    o_shape = jax.ShapeDtypeStruct((s, hh * HEAD_DIM), BF16)
    return _call(
        body, name="attention_bwd", grid=(hh, nq // group),
        in_specs=[pl.BlockSpec(memory_space=pltpu.SMEM), blk,
                  pl.BlockSpec((s, HEAD_DIM), lambda h, i: (0, hh + h)),
                  pl.BlockSpec((s, HEAD_DIM), lambda h, i: (0, 2 * hh + h)),
                  blk, blk, blk, blk, col, col, table, table],
        out_specs=[blk, col, col], out_shape=[o_shape] * 3,
        operands=[slopes, proj, proj, proj, out, dmixed, lse, *far_grads, jnp.asarray(dist), jnp.asarray(logc)],
        scratch_shapes=[pltpu.VMEM((s, HEAD_DIM), F32)] * 2
        + [pltpu.VMEM((ATT_WINDOW + 1, ATT_BLOCK, ATT_BLOCK), F32)],
        semantics=("parallel", "arbitrary"), jobs=jobs)


def _ret_decays(lgc, lga, strict_c, strict_a):
    c = RET_CHUNK
    rel = (lax.broadcasted_iota(jnp.int32, (c, c), 0) - lax.broadcasted_iota(jnp.int32, (c, c), 1)).astype(F32)
    in_c = (rel > 0) if strict_c else (rel >= 0)
    in_a = (rel < 0) if strict_a else (rel <= 0)
    mask = (jnp.where(in_c, jnp.exp(lgc * jnp.maximum(rel, 0.0)), 0.0)
            + jnp.where(in_a, jnp.exp(lga * jnp.maximum(-rel, 0.0)), 0.0))
    idx = lax.broadcasted_iota(jnp.int32, (c, 1), 0).astype(F32)
    ones = jnp.ones((1, HEAD_DIM), F32)
    dec = dict(
        rel=rel, mask=mask, idx=idx,
        a_c=jnp.exp(lgc * (idx + 1.0)), b_c=jnp.exp(lgc * (c - 1.0 - idx)), chunk_c=jnp.exp(ones * (lgc * c)),
        a_a=jnp.exp(lga * (c - idx)), b_a=jnp.exp(lga * idx), chunk_a=jnp.exp(ones * (lga * c)),
    )
    return dec


def _scaled(x, col):
    return (x.astype(F32) * col).astype(BF16)


def _chunk_rows(i):
    return pl.ds(pl.multiple_of(i * RET_CHUNK, RET_CHUNK), RET_CHUNK)


def _chunk_loop(nc, step, init, unroll=RET_UNROLL):
    group = math.gcd(nc, unroll)

    def trip(t, carry):
        for u in range(group):
            carry = step(t * group + u, carry)
        return carry

    return lax.fori_loop(0, nc // group, trip, init)


def _retention(a, b, c, lg_c, lg_a, *, strict_c, strict_a, scale, n_heads, name, gate=None, norm_w=None, jobs=()):
    s = a[0].shape[0]
    nc = s // RET_CHUNK
    epilogue = gate is not None

    def body(*refs):
        lgc_ref, lga_ref, a_ref, b_ref, c_ref = refs[:5]
        if epilogue:
            g_ref, w_ref, o_ref, mix_ref, sa_ref = refs[5:]
        else:
            o_ref, sa_ref = refs[5:]
        h = pl.program_id(0)
        dec = _ret_decays(lgc_ref[h], lga_ref[h], strict_c, strict_a)

        def reverse(t, state):
            i = nc - 1 - t
            sa_ref[i] = state.astype(BF16)
            rows = _chunk_rows(i)
            return state * dec["chunk_a"] + _dot(_scaled(b_ref[rows, :], dec["b_a"]), c_ref[rows, :], ta=True)

        _chunk_loop(nc, reverse, jnp.zeros((HEAD_DIM, HEAD_DIM), F32))

        def forward(i, state):
            rows = _chunk_rows(i)
            ai, bi, ci = a_ref[rows, :], b_ref[rows, :], c_ref[rows, :]
            inner = (_dot(ai, bi, tb=True) * dec["mask"]).astype(BF16)
            out = (_dot(inner, ci) + _dot(_scaled(ai, dec["a_c"]), state.astype(BF16))
                   + _dot(_scaled(ai, dec["a_a"]), sa_ref[i])) * scale
            o_ref[rows, :] = out.astype(BF16)
            if epilogue:
                r = lax.rsqrt(jnp.mean(out * out, axis=-1, keepdims=True) + EPS)
                g = g_ref[rows, :].astype(F32)
                mix_ref[rows, :] = (out * r * w_ref[...] * (g * _sigmoid(g))).astype(BF16)
            return state * dec["chunk_c"] + _dot(_scaled(bi, dec["b_c"]), ci, ta=True)

        _chunk_loop(nc, forward, jnp.zeros((HEAD_DIM, HEAD_DIM), F32))

    def col(first):
        return pl.BlockSpec((s, HEAD_DIM), lambda h: (0, first + h))

    smem = pl.BlockSpec(memory_space=pltpu.SMEM)
    in_specs = [smem, smem, col(a[1]), col(b[1]), col(c[1])]
    operands = [lg_c, lg_a, a[0], b[0], c[0]]
    o_shape = jax.ShapeDtypeStruct((s, n_heads * HEAD_DIM), BF16)
    out_specs, out_shape = [col(0)], [o_shape]
    if epilogue:
        in_specs += [col(gate[1]), pl.BlockSpec((1, HEAD_DIM), lambda h: (0, h))]
        operands += [gate[0], norm_w]
        out_specs, out_shape = [col(0)] * 2, [o_shape] * 2
    res, carried = _call(
        body, name=name, grid=(n_heads,), in_specs=in_specs, out_specs=out_specs, out_shape=out_shape,
        operands=operands, scratch_shapes=[pltpu.VMEM((nc, HEAD_DIM, HEAD_DIM), BF16)],
        semantics=("parallel",), jobs=jobs)
    res = res if epilogue else res[0]
    return (res, carried) if jobs else res


def _retention_decay_grads(a, b, c, e, lg_c, lg_a, *, scale, n_heads):
    s = a[0].shape[0]
    nc = s // RET_CHUNK
    cf = float(RET_CHUNK)

    def body(lgc_ref, lga_ref, a_ref, b_ref, c_ref, e_ref, gc_ref, ga_ref, sa_ref, ta_ref):
        h = pl.program_id(0)
        lgc, lga = lgc_ref[h], lga_ref[h]
        dec = _ret_decays(lgc, lga, True, True)
        rel, idx = dec["rel"], dec["idx"]
        w_c = jnp.where(rel > 0, rel * jnp.exp(lgc * jnp.maximum(rel, 0.0)), 0.0)
        w_a = jnp.where(rel < 0, -rel * jnp.exp(lga * jnp.maximum(-rel, 0.0)), 0.0)
        zero = jnp.zeros((HEAD_DIM, HEAD_DIM), F32)

        def reverse(t, carry):
            st, dst = carry
            i = nc - 1 - t
            sa_ref[i] = st.astype(BF16)
            ta_ref[i] = dst.astype(BF16)
            rows = _chunk_rows(i)
            bi, ci = b_ref[rows, :], c_ref[rows, :]
            st_new = st * dec["chunk_a"] + _dot(_scaled(bi, dec["b_a"]), ci, ta=True)
            dst_new = (cf * st + dst) * dec["chunk_a"] + _dot(_scaled(bi, idx * dec["b_a"]), ci, ta=True)
            return st_new, dst_new

        _chunk_loop(nc, reverse, (zero, zero))

        def forward(i, carry):
            st, dst, acc_c, acc_a = carry
            rows = _chunk_rows(i)
            ai, bi, ci = a_ref[rows, :], b_ref[rows, :], c_ref[rows, :]
            ev = e_ref[rows, :].astype(F32)
            pg = _dot(ai, bi, tb=True) * _dot(e_ref[rows, :], ci, tb=True)
            a_c, a_a = _scaled(ai, dec["a_c"]), _scaled(ai, dec["a_a"])
            inter_c = _dot(a_c, st.astype(BF16)) * (idx + 1.0) + _dot(a_c, dst.astype(BF16))
            inter_a = _dot(a_a, sa_ref[i]) * (cf - idx) + _dot(a_a, ta_ref[i])
            acc_c = acc_c + jnp.sum(pg * w_c, axis=0, keepdims=True) + jnp.sum(inter_c * ev, axis=0, keepdims=True)
            acc_a = acc_a + jnp.sum(pg * w_a, axis=0, keepdims=True) + jnp.sum(inter_a * ev, axis=0, keepdims=True)
            st_new = st * dec["chunk_c"] + _dot(_scaled(bi, dec["b_c"]), ci, ta=True)
            dst_new = ((cf * st + dst) * dec["chunk_c"]
                       + _dot(_scaled(bi, (cf - 1.0 - idx) * dec["b_c"]), ci, ta=True))
            return st_new, dst_new, acc_c, acc_a

        row = jnp.zeros((1, HEAD_DIM), F32)
        _, _, acc_c, acc_a = _chunk_loop(nc, forward, (zero, zero, row, row))
        gc_ref[...] = jnp.broadcast_to(jnp.sum(acc_c, axis=-1, keepdims=True) * scale, gc_ref.shape)
        ga_ref[...] = jnp.broadcast_to(jnp.sum(acc_a, axis=-1, keepdims=True) * scale, ga_ref.shape)

    def col(first):
        return pl.BlockSpec((s, HEAD_DIM), lambda h: (0, first + h))

    smem = pl.BlockSpec(memory_space=pltpu.SMEM)
    o_spec = pl.BlockSpec((1, 8, HEAD_DIM), lambda h: (h, 0, 0))
    o_shape = jax.ShapeDtypeStruct((n_heads, 8, HEAD_DIM), F32)
    gc, ga = pl.pallas_call(
        body, name="retention_decay_grads", grid=(n_heads,),
        in_specs=[smem, smem, col(a[1]), col(b[1]), col(c[1]), col(e[1])],
        out_specs=[o_spec] * 2, out_shape=[o_shape] * 2,
        scratch_shapes=[pltpu.VMEM((nc, HEAD_DIM, HEAD_DIM), BF16)] * 2,
        compiler_params=_params(("parallel",)),
    )(lg_c, lg_a, a[0], b[0], c[0], e[0])
    return gc[:, 0, 0], ga[:, 0, 0]


def _ret_gate_bwd(dmixed, first_col, out, proj, gate_col, norm_w, n_heads):
    s = out.shape[0]
    tr = _row_block(s, 8 * HEAD_DIM)

    def body(dm_ref, o_ref, g_ref, w_ref, do_ref, dg_ref, dw_ref):
        dm = dm_ref[...].astype(F32)
        ov = o_ref[...].astype(F32)
        g = g_ref[...].astype(F32)
        w = w_ref[...]
        r = lax.rsqrt(jnp.mean(ov * ov, axis=-1, keepdims=True) + EPS)
        ohat = ov * r
        sg = _sigmoid(g)
        silu = g * sg
        dg_ref[...] = (dm * ohat * w * sg * (1.0 + g * (1.0 - sg))).astype(BF16)
        dohat = dm * w * silu
        do_ref[...] = (r * (dohat - ohat * jnp.mean(dohat * ohat, axis=-1, keepdims=True))).astype(BF16)

        @pl.when(pl.program_id(1) == 0)
        def _():
            dw_ref[...] = jnp.zeros_like(dw_ref)

        dw_ref[...] += jnp.sum(dm * ohat * silu, axis=0, keepdims=True)

    def blk(first):
        return pl.BlockSpec((tr, HEAD_DIM), lambda h, i: (i, first + h))

    vec = pl.BlockSpec((1, HEAD_DIM), lambda h, i: (0, h))
    o_shape = jax.ShapeDtypeStruct((s, n_heads * HEAD_DIM), BF16)
    return pl.pallas_call(
        body, name="ret_gate_bwd", grid=(n_heads, s // tr),
        in_specs=[blk(first_col), blk(0), blk(gate_col), vec],
        out_specs=[blk(0), blk(0), vec],
        out_shape=[o_shape, o_shape, jax.ShapeDtypeStruct((1, n_heads * HEAD_DIM), F32)],
        compiler_params=_params(("parallel", "arbitrary")),
    )(dmixed, out, proj, norm_w)


def _step(x, target, norm_mix_w, ret_decay_fwd, ret_decay_bwd, ret_norm_w, norm_ffn_w, norm_final_w, own, pos):
    d = x.shape[1]
    nh = d // (2 * HEAD_DIM)
    scale = HEAD_DIM ** -0.5
    slopes = jnp.exp2(-8.0 * jnp.arange(1, nh + 1, dtype=F32) / nh)
    lg_f = -jnp.exp(ret_decay_fwd)
    lg_b = -jnp.exp(ret_decay_bwd)
    q_r, k_r, v_r, g_r = 3 * nh, 4 * nh, 5 * nh, 6 * nh
    ax = BIG_AXIS

    def gather(names, arrays, stage):
        return _gather_job(arrays, [ax[k] for k in names], stage)

    def add_halves(k, g, received):
        return _add_halves(g, received, ax[k], pos, name="grad_add_halves_" + k)

    def sum_parts(k, g, received, parts):
        return _sum_chip_parts(g, received, parts, ax[k], pos, name="grad_sum_parts_" + k)

    (w_in,) = _run_jobs([gather(["w_in"], [own["w_in"]], "ici"), gather(["w_in"], [own["w_in"]], "d2d")],
                        name="all_gather_w_in")
    n1 = _rmsnorm_fwd(x, norm_mix_w, name="norm_mix_fwd")
    proj, [[w_gate]] = _matmul(n1, w_in, name="in_proj", out_dtype=BF16,
                               jobs=[gather(["w_gate"], [own["w_gate"]], "ici")])
    qkv_classes = _to_classes(proj[:, :3 * nh * HEAD_DIM])
    far_out, far_lse = _attention_far_fwd(qkv_classes, slopes, nh)
    (attn, lse), [[w_gate], [w_out, w_up]] = _attention_fwd(
        proj, slopes, _from_classes(far_out), _from_classes(far_lse), nh,
        jobs=[gather(["w_gate"], [w_gate], "d2d"), gather(["w_out", "w_up"], [own["w_out"], own["w_up"]], "ici")])
    (ret, ret_mixed), [[w_out, w_up]] = _retention(
        (proj, q_r), (proj, k_r), (proj, v_r), lg_f, lg_b, strict_c=False, strict_a=True, scale=scale, n_heads=nh,
        name="retention_fwd", gate=(proj, g_r), norm_w=ret_norm_w,
        jobs=[gather(["w_out", "w_up"], [w_out, w_up], "d2d")])
    mixed = jnp.concatenate([attn, ret_mixed], axis=1)
    h1 = _matmul(mixed, w_out, name="out_proj", residual=x)
    n2 = _rmsnorm_fwd(h1, norm_ffn_w, name="norm_ffn_fwd")
    (gate, up, act), [[w_down]] = _swiglu_fwd(n2, w_gate, w_up, jobs=[gather(["w_down"], [own["w_down"]], "ici")])
    (w_down,) = _run_jobs([gather(["w_down"], [w_down], "d2d")], name="all_gather_w_down_sibling")
    h2 = _matmul(act, w_down, name="down_proj", residual=h1, tk=2816)
    dh2, dh2_b, d_norm_final, loss = _loss_head(h2, norm_final_w, target)

    dgate, dup = _swiglu_bwd_act(dh2_b, w_down, gate, up)
    g_down = _weight_grad(act, dh2_b, name="grad_w_down")
    g_gate, [[r_down]] = _weight_grad(n2, dgate, name="grad_w_gate", jobs=[_exchange_job([g_down], [ax["w_down"]])])
    s_down = add_halves("w_down", g_down, r_down)
    g_up, [[r_gate], [p_down]] = _weight_grad(
        n2, dup, name="grad_w_up",
        jobs=[_exchange_job([g_gate], [ax["w_gate"]]), _send_sums_job([s_down], [ax["w_down"]])])
    s_gate = add_halves("w_gate", g_gate, r_gate)
    h_down = sum_parts("w_down", g_down, r_down, p_down)
    dn2, [[r_up], [p_gate]] = _swiglu_bwd_in(
        dgate, dup, w_gate, w_up,
        jobs=[_exchange_job([g_up], [ax["w_up"]]), _send_sums_job([s_gate], [ax["w_gate"]])])
    s_up = add_halves("w_up", g_up, r_up)
    h_gate = sum_parts("w_gate", g_gate, r_gate, p_gate)
    dh1, dh1_b, d_norm_ffn = _rmsnorm_bwd(dn2, h1, norm_ffn_w, dh2, name="norm_ffn_bwd")

    dmixed, [[gr_down]] = _matmul(dh1_b, w_out, name="out_proj_bwd", tb=True, out_dtype=BF16,
                                  jobs=[_join_job([h_down], [ax["w_down"]])])
    g_out = _weight_grad(mixed, dh1_b, name="grad_w_out")
    d_ret, dg_r, d_ret_norm = _ret_gate_bwd(dmixed, nh, ret, proj, g_r, ret_norm_w, nh)
    far_grads = _attention_far_bwd(qkv_classes, slopes, _to_classes(attn), _to_classes(dmixed[:, :nh * HEAD_DIM]),
                                   _to_classes(lse), nh)
    (dq_a, dk_a, dv_a), [[p_up], [r_out]] = _attention_bwd(
        proj, slopes, attn, lse, dmixed, [_from_classes(t) for t in far_grads], nh,
        jobs=[_send_sums_job([s_up], [ax["w_up"]]), _exchange_job([g_out], [ax["w_out"]])])
    s_out = add_halves("w_out", g_out, r_out)
    h_up = sum_parts("w_up", g_up, r_up, p_up)
    dq_r, [[p_out], [gr_gate, gr_up]] = _retention(
        (d_ret, 0), (proj, v_r), (proj, k_r), lg_f, lg_b, strict_c=False, strict_a=True, scale=scale, n_heads=nh,
        name="retention_dq",
        jobs=[_send_sums_job([s_out], [ax["w_out"]]), _join_job([h_gate, h_up], [ax["w_gate"], ax["w_up"]])])
    h_out = sum_parts("w_out", g_out, r_out, p_out)
    dv_r, [[gr_out]] = _retention(
        (proj, k_r), (proj, q_r), (d_ret, 0), lg_b, lg_f, strict_c=True, strict_a=False, scale=scale, n_heads=nh,
        name="retention_dv", jobs=[_join_job([h_out], [ax["w_out"]])])
    dk_r = _retention((proj, v_r), (d_ret, 0), (proj, q_r), lg_b, lg_f, strict_c=True, strict_a=False,
                      scale=scale, n_heads=nh, name="retention_dk")
    dlg_f, dlg_b = _retention_decay_grads((proj, q_r), (proj, k_r), (proj, v_r), (d_ret, 0), lg_f, lg_b,
                                          scale=scale, n_heads=nh)
    dproj = jnp.concatenate([dq_a, dk_a, dv_a, dq_r, dk_r, dv_r, dg_r], axis=1)
    g_in = _weight_grad(n1, dproj, name="grad_w_in")
    (r_in,) = _run_jobs([_exchange_job([g_in], [ax["w_in"]])], name="grad_exchange_w_in")
    s_in = add_halves("w_in", g_in, r_in)
    dn1, [[p_in]] = _matmul(dproj, w_in, name="in_proj_bwd", tb=True, tk=3584,
                            jobs=[_send_sums_job([s_in], [ax["w_in"]])])
    dx, _, d_norm_mix = _rmsnorm_bwd(dn1, x, norm_mix_w, dh1, name="norm_mix_bwd")
    h_in = sum_parts("w_in", g_in, r_in, p_in)
    (gr_in,) = _run_jobs([_join_job([h_in], [ax["w_in"]])], name="grad_join_w_in")

    small = dict(loss=loss[0, 0], norm_mix_w=d_norm_mix, ret_decay_fwd=dlg_f * lg_f, ret_decay_bwd=dlg_b * lg_b,
                 ret_norm_w=d_ret_norm, norm_ffn_w=d_norm_ffn, norm_final_w=d_norm_final)
    return dx, dict(w_in=gr_in, w_out=gr_out, w_gate=gr_gate, w_up=gr_up, w_down=gr_down), small


def _mesh_position():
    x, y, c = lax.axis_index("x"), lax.axis_index("y"), lax.axis_index("c")
    chips = [(1 - x, y), (x, 1 - y), (1 - x, 1 - y)]
    return x, y, c, chips


def _ds(start, size):
    if isinstance(start, int):
        return pl.ds(start, size)
    return pl.ds(pl.multiple_of(start * size, size), size)


def _region(ref, axis, shard, half, shard_size, half_size):
    along = slice(None) if shard is None else _ds(shard, shard_size)
    other = slice(None) if half is None else _ds(half, half_size)
    return ref.at[other, along] if axis == 1 else ref.at[along, other]


def _gather_job(full, axes, stage):
    n = len(full)

    def copies(refs, sems):
        send_sem, recv_sem = sems
        x, y, c, chips = _mesh_position()
        me = 2 * x + y

        def copy(w, k, shard, half, target):
            rows_cols = full[w].shape
            place = _region(refs[w], axes[w], shard, half, rows_cols[axes[w]] // N_CHIPS, rows_cols[1 - axes[w]] // 2)
            return pltpu.make_async_remote_copy(
                src_ref=place, dst_ref=place, send_sem=send_sem.at[w, k], recv_sem=recv_sem.at[w, k],
                device_id=target, device_id_type=MESH)

        def sent(w, k):
            if stage == "ici":
                return copy(w, k, me, c, (chips[k][0], chips[k][1], c))
            return copy(w, k, 2 * chips[k][0] + chips[k][1], c, (x, y, 1 - c))

        def landed(w, k):
            return copy(w, k, 2 * chips[k][0] + chips[k][1], c if stage == "ici" else 1 - c, (x, y, 1 - c))

        return sent, landed

    def start(refs, sems):
        sent, _ = copies(refs, sems)
        for w in range(n):
            for k in range(3):
                sent(w, k).start()

    def finish(refs, sems):
        sent, landed = copies(refs, sems)
        for w in range(n):
            for k in range(3):
                landed(w, k).wait_recv()
                sent(w, k).wait_send()

    return _Job(ios=full, sems=[pltpu.SemaphoreType.DMA((n, 3))] * 2, start=start, finish=finish)


def _exchange_job(grads, axes):
    n = len(grads)

    def half_shape(w):
        return tuple(d // 2 if a != axes[w] else d for a, d in enumerate(grads[w].shape))

    def copy(refs, sems, w):
        x, y, c, _ = _mesh_position()
        return pltpu.make_async_remote_copy(
            src_ref=_region(refs[w], axes[w], None, 1 - c, 0, half_shape(w)[1 - axes[w]]), dst_ref=refs[n + w],
            send_sem=sems[0].at[w], recv_sem=sems[1].at[w], device_id=(x, y, 1 - c), device_id_type=MESH)

    def start(refs, sems):
        for w in range(n):
            copy(refs, sems, w).start()

    def finish(refs, sems):
        for w in range(n):
            copy(refs, sems, w).wait()

    return _Job(ins=grads, outs=[jax.ShapeDtypeStruct(half_shape(w), F32) for w in range(n)],
                sems=[pltpu.SemaphoreType.DMA((n,))] * 2, start=start, finish=finish)


def _half_block_spec(axis, block, half_blocks, use_half):
    if axis == 1:
        if use_half:
            return pl.BlockSpec(block, lambda i, pos: (pos[0] * half_blocks + i, 0))
        return pl.BlockSpec(block, lambda i, pos: (i, 0))
    if use_half:
        return pl.BlockSpec(block, lambda i, pos: (i, pos[0]))
    return pl.BlockSpec(block, lambda i, pos: (i, 0))


def _add_halves(grad, received, axis, pos, *, name):
    rows, cols = received.shape
    tr = _row_block(rows, cols)
    nb = rows // tr

    def body(pos_ref, g_ref, r_ref, o_ref):
        o_ref[...] = (g_ref[...] + r_ref[...]).astype(BF16)

    blk = (tr, cols)
    return pl.pallas_call(
        body, name=name, out_shape=jax.ShapeDtypeStruct((rows, cols), BF16),
        grid_spec=pltpu.PrefetchScalarGridSpec(
            num_scalar_prefetch=1, grid=(nb,),
            in_specs=[_half_block_spec(axis, blk, nb, True), _half_block_spec(axis, blk, nb, False)],
            out_specs=_half_block_spec(axis, blk, nb, False)),
        compiler_params=_params(("parallel",)),
    )(pos, grad, received)


def _send_sums_job(sums, axes):
    n = len(sums)

    def part_shape(w):
        return tuple(d // N_CHIPS if a == axes[w] else d for a, d in enumerate(sums[w].shape))

    def copy(refs, sems, w, k):
        x, y, c, chips = _mesh_position()
        shard = 2 * chips[k][0] + chips[k][1]
        return pltpu.make_async_remote_copy(
            src_ref=_region(refs[w], axes[w], shard, None, part_shape(w)[axes[w]], 0), dst_ref=refs[n + w].at[k],
            send_sem=sems[0].at[w, k], recv_sem=sems[1].at[w, k],
            device_id=(chips[k][0], chips[k][1], c), device_id_type=MESH)

    def start(refs, sems):
        for w in range(n):
            for k in range(3):
                copy(refs, sems, w, k).start()

    def finish(refs, sems):
        for w in range(n):
            for k in range(3):
                copy(refs, sems, w, k).wait()

    return _Job(ins=sums, outs=[jax.ShapeDtypeStruct((3,) + part_shape(w), BF16) for w in range(n)],
                sems=[pltpu.SemaphoreType.DMA((n, 3))] * 2, start=start, finish=finish)


def _sum_chip_parts(grad, received, parts, axis, pos, *, name):
    _, rows, cols = parts.shape
    tr = _row_block(rows, cols)
    nb = rows // tr
    blk = (tr, cols)

    def body(pos_ref, g_ref, r_ref, p_ref, o_ref):
        total = g_ref[...] + r_ref[...]
        for k in range(3):
            total = total + p_ref[k].astype(F32)
        o_ref[...] = total

    if axis == 1:
        g_spec = pl.BlockSpec(blk, lambda i, pos: (pos[0] * nb + i, pos[1]))
        r_spec = pl.BlockSpec(blk, lambda i, pos: (i, pos[1]))
        o_spec = pl.BlockSpec(blk, lambda i, pos: (pos[0] * nb + i, 0))
        shard_shape = (2 * rows, cols)
    else:
        g_spec = pl.BlockSpec(blk, lambda i, pos: (pos[1] * nb + i, pos[0]))
        r_spec = pl.BlockSpec(blk, lambda i, pos: (pos[1] * nb + i, 0))
        o_spec = pl.BlockSpec(blk, lambda i, pos: (i, pos[0]))
        shard_shape = (rows, 2 * cols)
    return pl.pallas_call(
        body, name=name, out_shape=jax.ShapeDtypeStruct(shard_shape, F32),
        grid_spec=pltpu.PrefetchScalarGridSpec(
            num_scalar_prefetch=1, grid=(nb,),
            in_specs=[g_spec, r_spec, pl.BlockSpec((3,) + blk, lambda i, pos: (0, i, 0))],
            out_specs=o_spec),
        compiler_params=_params(("parallel",)),
    )(pos, grad, received, parts)


def _join_job(shards, axes):
    n = len(shards)

    def copy(refs, sems, w, other):
        x, y, c, _ = _mesh_position()
        place = _region(refs[w], axes[w], None, 1 - c if other else c, 0, shards[w].shape[1 - axes[w]] // 2)
        return pltpu.make_async_remote_copy(
            src_ref=place, dst_ref=place, send_sem=sems[0].at[w], recv_sem=sems[1].at[w],
            device_id=(x, y, 1 - c), device_id_type=MESH)

    def start(refs, sems):
        for w in range(n):
            copy(refs, sems, w, False).start()

    def finish(refs, sems):
        for w in range(n):
            copy(refs, sems, w, True).wait_recv()
            copy(refs, sems, w, False).wait_send()

    return _Job(ios=shards, sems=[pltpu.SemaphoreType.DMA((n,))] * 2, start=start, finish=finish)


def _all_reduce_small(vec):
    rows, cols = vec.shape

    def body(v_ref, o_ref, land_ref, send_sem, recv_sem):
        x, y, c, _ = _mesh_position()
        me = 4 * x + 2 * y + c
        land_ref[me] = v_ref[...]
        copies = []
        for k in range(1, 8):
            px, py, pc = x ^ (k >> 2), y ^ ((k >> 1) & 1), c ^ (k & 1)
            copies.append(pltpu.make_async_remote_copy(
                src_ref=v_ref, dst_ref=land_ref.at[me], send_sem=send_sem.at[k], recv_sem=recv_sem.at[k],
                device_id=(px, py, pc), device_id_type=MESH))
        for cp in copies:
            cp.start()
        for k in range(1, 8):
            peer = me ^ k
            pltpu.make_async_remote_copy(
                src_ref=v_ref, dst_ref=land_ref.at[peer], send_sem=send_sem.at[k], recv_sem=recv_sem.at[k],
                device_id=(x, y, c), device_id_type=MESH).wait_recv()
        for cp in copies:
            cp.wait_send()
        total = land_ref[0]
        for k in range(1, 8):
            total = total + land_ref[k]
        o_ref[...] = total

    vmem = pl.BlockSpec(memory_space=pltpu.VMEM)
    return pl.pallas_call(
        body, name="all_reduce_small", in_specs=[vmem], out_specs=vmem,
        out_shape=jax.ShapeDtypeStruct((rows, cols), F32),
        scratch_shapes=[pltpu.VMEM((8, rows, cols), F32), pltpu.SemaphoreType.DMA((8,)), pltpu.SemaphoreType.DMA((8,))],
    )(vec)


def _adamw(w, g, m, v, *, name):
    rows, cols = w.shape
    tr = _row_block(rows, cols) if rows % 8 == 0 else rows
    bc1 = 1.0 - ADAM_B1 ** ADAM_STEP
    bc2 = 1.0 - ADAM_B2 ** ADAM_STEP

    def body(w_ref, g_ref, m_ref, v_ref, go_ref, d_ref, mo_ref, vo_ref):
        gv = g_ref[...]
        go_ref[...] = gv
        mn = ADAM_B1 * m_ref[...] + (1.0 - ADAM_B1) * gv
        vn = ADAM_B2 * v_ref[...] + (1.0 - ADAM_B2) * (gv * gv)
        mo_ref[...] = mn
        vo_ref[...] = vn
        d_ref[...] = -ADAM_LR * ((mn / bc1) / (jnp.sqrt(vn / bc2) + ADAM_EPS) + ADAM_WD * w_ref[...])

    blk = pl.BlockSpec((tr, cols), lambda i: (i, 0))
    shape = jax.ShapeDtypeStruct((rows, cols), F32)
    return pl.pallas_call(
        body, name=name, grid=(rows // tr,), in_specs=[blk] * 4, out_specs=[blk] * 4, out_shape=[shape] * 4,
        compiler_params=_params(("parallel",)),
    )(w, g, m, v)


def _to_bf16_in_place(w, axis, pos, *, name):
    rows, cols = w.shape
    tr = _row_block(rows, cols)
    nb = rows // tr

    def body(pos_ref, w_ref, o_ref):
        o_ref[...] = w_ref[...].astype(BF16)

    if axis == 1:
        o_spec = pl.BlockSpec((tr, cols), lambda i, pos: (i, pos[1]))
        full_shape = (rows, N_CHIPS * cols)
    else:
        o_spec = pl.BlockSpec((tr, cols), lambda i, pos: (pos[1] * nb + i, 0))
        full_shape = (N_CHIPS * rows, cols)
    return pl.pallas_call(
        body, name=name, out_shape=jax.ShapeDtypeStruct(full_shape, BF16),
        grid_spec=pltpu.PrefetchScalarGridSpec(
            num_scalar_prefetch=1, grid=(nb,),
            in_specs=[pl.BlockSpec((tr, cols), lambda i, pos: (i, 0))], out_specs=o_spec),
        compiler_params=_params(("parallel",)),
    )(pos, w)


BIG = ("w_in", "w_out", "w_gate", "w_up", "w_down")
BIG_AXIS = dict(w_in=1, w_out=0, w_gate=1, w_up=1, w_down=0)
SMALL = ("norm_mix_w", "ret_decay_fwd", "ret_decay_bwd", "ret_norm_w", "norm_ffn_w", "norm_final_w")
ALL_WEIGHTS = ("norm_mix_w", "w_in", "ret_decay_fwd", "ret_decay_bwd", "ret_norm_w", "w_out", "norm_ffn_w",
               "w_gate", "w_up", "w_down", "norm_final_w")
SMALL_ROW = 128 * 8


def _pack_small(small):
    pieces = [jnp.reshape(small["loss"], (1,))] + [jnp.reshape(small[k], (-1,)) for k in SMALL]
    rows = []
    for p in pieces:
        pad = -p.shape[0] % (8 * SMALL_ROW)
        rows.append(jnp.reshape(jnp.pad(p, (0, pad)), (-1, SMALL_ROW)))
    return jnp.concatenate(rows, axis=0)


def _unpack_small(block, like):
    out, row = {}, 0
    for k in ("loss",) + SMALL:
        size = 1 if k == "loss" else like[k].size
        nrows = -(-size // (8 * SMALL_ROW)) * 8
        out[k] = jnp.reshape(block[row:row + nrows], (-1,))[:size]
        row += nrows
    return out


def kernel(x, norm_mix_w, w_in, ret_decay_fwd, ret_decay_bwd, ret_norm_w, w_out, norm_ffn_w, w_gate, w_up, w_down, norm_final_w, loss_target, m_norm_mix_w, m_w_in, m_ret_decay_fwd, m_ret_decay_bwd, m_ret_norm_w, m_w_out, m_norm_ffn_w, m_w_gate, m_w_up, m_w_down, m_norm_final_w, v_norm_mix_w, v_w_in, v_ret_decay_fwd, v_ret_decay_bwd, v_ret_norm_w, v_w_out, v_norm_ffn_w, v_w_gate, v_w_up, v_w_down, v_norm_final_w):
    weights = dict(norm_mix_w=norm_mix_w, w_in=w_in, ret_decay_fwd=ret_decay_fwd, ret_decay_bwd=ret_decay_bwd,
                   ret_norm_w=ret_norm_w, w_out=w_out, norm_ffn_w=norm_ffn_w, w_gate=w_gate, w_up=w_up,
                   w_down=w_down, norm_final_w=norm_final_w)
    m_in = dict(norm_mix_w=m_norm_mix_w, w_in=m_w_in, ret_decay_fwd=m_ret_decay_fwd, ret_decay_bwd=m_ret_decay_bwd,
                ret_norm_w=m_ret_norm_w, w_out=m_w_out, norm_ffn_w=m_norm_ffn_w, w_gate=m_w_gate, w_up=m_w_up,
                w_down=m_w_down, norm_final_w=m_norm_final_w)
    v_in = dict(norm_mix_w=v_norm_mix_w, w_in=v_w_in, ret_decay_fwd=v_ret_decay_fwd, ret_decay_bwd=v_ret_decay_bwd,
                ret_norm_w=v_ret_norm_w, w_out=v_w_out, norm_ffn_w=v_norm_ffn_w, w_gate=v_w_gate, w_up=v_w_up,
                w_down=v_w_down, norm_final_w=v_norm_final_w)
    pos = jnp.stack([lax.axis_index("c"), 2 * lax.axis_index("x") + lax.axis_index("y")]).astype(jnp.int32)

    own = {k: _to_bf16_in_place(weights[k][0], BIG_AXIS[k], pos, name="cast_" + k) for k in BIG}

    dx, grad_w, small = _step(
        x[0], loss_target[0], norm_mix_w, ret_decay_fwd[0], ret_decay_bwd[0], ret_norm_w, norm_ffn_w,
        norm_final_w[None, :], own, pos)

    like = {k: weights[k] for k in SMALL}
    reduced = _unpack_small(_all_reduce_small(_pack_small(small)), like)
    loss = reduced["loss"][0]
    for k in SMALL:
        grad_w[k] = jnp.reshape(reduced[k], (1, -1))

    delta, new_m, new_v = {}, {}, {}
    for k in ALL_WEIGHTS:
        shape = weights[k].shape
        as2d = (lambda t: jnp.reshape(t, (-1, shape[-1])))
        grad_w[k], delta[k], new_m[k], new_v[k] = (jnp.reshape(t, shape) for t in _adamw(
            as2d(weights[k]), as2d(grad_w[k]), as2d(m_in[k]), as2d(v_in[k]), name="adamw_" + k))

    return (loss, dx[None], *[grad_w[k] for k in ALL_WEIGHTS], *[delta[k] for k in ALL_WEIGHTS],
            *[new_m[k] for k in ALL_WEIGHTS], *[new_v[k] for k in ALL_WEIGHTS])
```

```python
import functools
import math

import numpy as np
import jax
import jax.numpy as jnp
from jax import lax
from jax.experimental import pallas as pl
from jax.experimental.pallas import tpu as pltpu

F32 = jnp.float32
BF16 = jnp.bfloat16
MESH = pl.DeviceIdType.MESH

HEAD_DIM = 128
RET_CHUNK = 128
RET_UNROLL = 8
EPS = 1e-6
DILATED_PATTERNS = ((128, 1), (512, 4), (2048, 16))
ATT_BLOCK = 256
ATT_REACH = max(w // 2 for w, _ in DILATED_PATTERNS)
ATT_NEAR = ATT_BLOCK
ATT_CLASSES = DILATED_PATTERNS[-1][1]
assert all(w // 2 <= ATT_NEAR for w, _ in DILATED_PATTERNS[:-1])
ATT_KB = -(-ATT_NEAR // ATT_BLOCK)
ATT_WINDOW = 2 * ATT_KB + 1
ATT_FAR_GROUP = 4
ATT_NEAR_GROUP = 2
MASKED = -1e30
ROW_MAX_INIT = -1e29
N_CHIPS = 4
VMEM_LIMIT_BYTES = 56 * 1024 * 1024
ELEM_BLOCK_BYTES = 2 * 1024 * 1024

ADAM_LR = 0.001
ADAM_B1 = 0.9
ADAM_B2 = 0.999
ADAM_EPS = 1e-08
ADAM_WD = 0.01
ADAM_STEP = 10


def _params(sem=None):
    return pltpu.CompilerParams(dimension_semantics=sem, vmem_limit_bytes=VMEM_LIMIT_BYTES)


def _sigmoid(x):
    return 1.0 / (1.0 + jnp.exp(-x))


class _Job:
    def __init__(self, *, ins=(), ios=(), outs=(), sems=(), start, finish):
        self.ins, self.ios, self.outs, self.sems = list(ins), list(ios), list(outs), list(sems)
        self.start, self.finish = start, finish

    def results(self):
        return [jax.ShapeDtypeStruct(a.shape, a.dtype) for a in self.ios] + self.outs


def _call(body, *, name, grid, in_specs, out_specs, out_shape, operands, scratch_shapes=(), semantics=None, jobs=()):
    in_specs, out_specs, out_shape = list(in_specs), list(out_specs), list(out_shape)
    scratch_shapes = list(scratch_shapes)
    if not jobs:
        outs = pl.pallas_call(body, name=name, grid=grid, in_specs=in_specs, out_specs=out_specs, out_shape=out_shape,
                              scratch_shapes=scratch_shapes, compiler_params=_params(semantics))(*operands)
        return outs, []
    n_in, n_out, n_scratch = len(in_specs), len(out_specs), len(scratch_shapes)
    extra_in, extra_out, sems, aliases = [], [], [], {}
    for job in jobs:
        extra_in += job.ins
        for t in range(len(job.ios)):
            aliases[n_in + len(extra_in) + t] = n_out + len(extra_out) + t
        extra_in += job.ios
        extra_out += job.results()
        sems += job.sems

    def carried(*refs):
        x_in = refs[n_in:n_in + len(extra_in)]
        x_out = refs[n_in + len(extra_in) + n_out:n_in + len(extra_in) + n_out + len(extra_out)]
        x_sem = refs[len(refs) - len(sems):]
        views, i_in, i_out, i_sem = [], 0, 0, 0
        for job in jobs:
            data = list(x_in[i_in:i_in + len(job.ins)]) + list(x_out[i_out:i_out + len(job.results())])
            views.append((data, x_sem[i_sem:i_sem + len(job.sems)]))
            i_in += len(job.ins) + len(job.ios)
            i_out += len(job.results())
            i_sem += len(job.sems)
        steps = [pl.program_id(d) for d in range(len(grid))]

        @pl.when(functools.reduce(jnp.logical_and, [s == 0 for s in steps]))
        def _():
            for job, (data, sem) in zip(jobs, views):
                job.start(data, sem)

        body(*refs[:n_in], *refs[n_in + len(extra_in):n_in + len(extra_in) + n_out],
             *refs[len(refs) - len(sems) - n_scratch:len(refs) - len(sems)])

        @pl.when(functools.reduce(jnp.logical_and, [s == g - 1 for s, g in zip(steps, grid)]))
        def _():
            for job, (data, sem) in zip(jobs, views):
                job.finish(data, sem)

    hbm = pl.BlockSpec(memory_space=pl.ANY)
    res = pl.pallas_call(
        carried, name=name, grid=grid, in_specs=in_specs + [hbm] * len(extra_in),
        out_specs=out_specs + [hbm] * len(extra_out), out_shape=out_shape + extra_out,
        input_output_aliases=aliases, scratch_shapes=scratch_shapes + sems,
        compiler_params=_params(("arbitrary",) * len(grid)),
    )(*operands, *extra_in)
    carried_results, at = [], n_out
    for job in jobs:
        carried_results.append(list(res[at:at + len(job.results())]))
        at += len(job.results())
    return list(res[:n_out]), carried_results


def _run_jobs(jobs, *, name):
    first = jobs[0]
    n_in, n_io = len(first.ins), len(first.ios)
    out_shape = first.results()
    n_sems = [len(job.sems) for job in jobs]

    def body(*refs):
        data = list(refs[:n_in]) + list(refs[n_in + n_io:n_in + n_io + len(out_shape)])
        at = n_in + n_io + len(out_shape)
        for job, ns in zip(jobs, n_sems):
            job.start(data, refs[at:at + ns])
            job.finish(data, refs[at:at + ns])
            at += ns

    hbm = pl.BlockSpec(memory_space=pl.ANY)
    return pl.pallas_call(
        body, name=name, in_specs=[hbm] * (n_in + n_io), out_specs=[hbm] * len(out_shape), out_shape=out_shape,
        input_output_aliases={n_in + t: t for t in range(n_io)},
        scratch_shapes=[s for job in jobs for s in job.sems],
    )(*first.ins, *first.ios)


def _dot(a, b, ta=False, tb=False):
    return lax.dot_general(a, b, (((0 if ta else 1,), (1 if tb else 0,)), ((), ())),
                           preferred_element_type=F32)


def _tile(n, want):
    t = min(n, want) // 128 * 128
    while n % t:
        t -= 128
    return t


def _a_spec(ta, tm, tk):
    return pl.BlockSpec((tk, tm), lambda i, j, k: (k, i)) if ta else pl.BlockSpec((tm, tk), lambda i, j, k: (i, k))


def _b_spec(tb, tk, tn):
    return pl.BlockSpec((tn, tk), lambda i, j, k: (j, k)) if tb else pl.BlockSpec((tk, tn), lambda i, j, k: (k, j))


def _accumulate(accs, nk, products, finish):
    if nk == 1:
        finish(*products())
        return
    k = pl.program_id(2)

    @pl.when(k == 0)
    def _():
        for acc, p in zip(accs, products()):
            acc[...] = p

    if nk > 2:
        @pl.when(jnp.logical_and(k > 0, k < nk - 1))
        def _():
            for acc, p in zip(accs, products()):
                acc[...] += p

    @pl.when(k == nk - 1)
    def _():
        finish(*[acc[...] + p for acc, p in zip(accs, products())])


def _matmul(a, b, *, name, ta=False, tb=False, out_dtype=F32, residual=None, tm=1024, tn=1024, tk=2048, jobs=()):
    m, kdim = (a.shape[1], a.shape[0]) if ta else a.shape
    n = b.shape[0] if tb else b.shape[1]
    tm, tn, tk = _tile(m, tm), _tile(n, tn), _tile(kdim, tk)
    nk = kdim // tk

    def body(*refs):
        a_ref, b_ref = refs[:2]
        r_ref = refs[2] if residual is not None else None
        o_ref = refs[-1] if nk == 1 else refs[-2]

        def finish(total):
            if residual is not None:
                total = total + r_ref[...]
            o_ref[...] = total.astype(out_dtype)

        _accumulate(refs[-1:] if nk > 1 else (), nk, lambda: (_dot(a_ref[...], b_ref[...], ta, tb),), finish)

    o_spec = pl.BlockSpec((tm, tn), lambda i, j, k: (i, j))
    in_specs = [_a_spec(ta, tm, tk), _b_spec(tb, tk, tn)]
    operands = [a, b]
    if residual is not None:
        in_specs.append(o_spec)
        operands.append(residual)
    (out,), carried = _call(
        body, name=name, grid=(m // tm, n // tn, nk), in_specs=in_specs, out_specs=[o_spec],
        out_shape=[jax.ShapeDtypeStruct((m, n), out_dtype)], operands=operands,
        scratch_shapes=[pltpu.VMEM((tm, tn), F32)] * (nk > 1),
        semantics=("parallel", "parallel", "arbitrary"), jobs=jobs)
    return (out, carried) if jobs else out


def _matmul_pieces_nt(pieces, b, *, name, tm=512, tn=1024, jobs=()):
    m, kp = pieces[0].shape
    n = b.shape[0]
    tm, tn = _tile(m, tm), _tile(n, tn)
    count = len(pieces)

    def body(*refs):
        b_ref, o_ref = refs[count], refs[count + 1]
        total = _dot(refs[0][...], b_ref[:, pl.ds(0, kp)], tb=True)
        for p in range(1, count):
            total = total + _dot(refs[p][...], b_ref[:, pl.ds(p * kp, kp)], tb=True)
        o_ref[...] = total

    piece = pl.BlockSpec((tm, kp), lambda j, i: (i, 0))
    (out,), carried = _call(
        body, name=name, grid=(n // tn, m // tm),
        in_specs=[piece] * count + [pl.BlockSpec((tn, count * kp), lambda j, i: (j, 0))],
        out_specs=[pl.BlockSpec((tm, tn), lambda j, i: (i, j))],
        out_shape=[jax.ShapeDtypeStruct((m, n), F32)], operands=[*pieces, b],
        semantics=("parallel", "parallel"), jobs=jobs)
    return (out, carried) if jobs else out


def _weight_grad_pieces(a, pieces, *, name, tm=1024, tk=1024):
    tokens, m = a.shape
    np_ = pieces[0].shape[1]
    tm, tk = _tile(m, tm), _tile(tokens, tk)
    nk = tokens // tk
    count = len(pieces)

    def body(*refs):
        a_ref, o_ref = refs[0], refs[1 + count]
        acc = refs[2 + count:]
        j = pl.program_id(1)

        def finish(total):
            o_ref[...] = total

        for p in range(count):
            @pl.when(j == p)
            def _(p=p):
                _accumulate(acc, nk, lambda: (_dot(a_ref[...], refs[1 + p][...], ta=True),), finish)

    def piece(p):
        return pl.BlockSpec((tk, np_), lambda i, j, k: (jnp.where(j == p, k, 0), 0))

    return pl.pallas_call(
        body, name=name, grid=(m // tm, count, nk),
        in_specs=[pl.BlockSpec((tk, tm), lambda i, j, k: (k, i))] + [piece(p) for p in range(count)],
        out_specs=pl.BlockSpec((tm, np_), lambda i, j, k: (i, j)),
        out_shape=jax.ShapeDtypeStruct((m, count * np_), F32),
        scratch_shapes=[pltpu.VMEM((tm, np_), F32)] * (nk > 1),
        compiler_params=_params(("parallel", "parallel", "arbitrary")),
    )(a, *pieces)


def _weight_grad(a, g, *, name, jobs=()):
    tokens, m = a.shape
    tm = 1024 if m % 1024 == 0 else _tile(m, 1408)
    return _matmul(a, g, name=name, ta=True, tm=tm, tn=512, tk=tokens, jobs=jobs)


def _swiglu_fwd(n2, w_gate, w_up, *, tm=1024, tn=512, tk=2048, jobs=()):
    m, kdim = n2.shape
    n = w_gate.shape[1]
    tm, tn, tk = _tile(m, tm), _tile(n, tn), _tile(kdim, tk)
    nk = kdim // tk

    def body(a_ref, g_ref, u_ref, gate_ref, up_ref, act_ref, *acc):
        def products():
            a = a_ref[...]
            return _dot(a, g_ref[...]), _dot(a, u_ref[...])

        def finish(g, u):
            gate_ref[...] = g.astype(BF16)
            up_ref[...] = u.astype(BF16)
            act_ref[...] = (g * _sigmoid(g) * u).astype(BF16)

        _accumulate(acc, nk, products, finish)

    o_spec = pl.BlockSpec((tm, tn), lambda i, j, k: (i, j))
    o_shape = jax.ShapeDtypeStruct((m, n), BF16)
    return _call(
        body, name="swiglu_fwd", grid=(m // tm, n // tn, nk),
        in_specs=[_a_spec(False, tm, tk), _b_spec(False, tk, tn), _b_spec(False, tk, tn)],
        out_specs=[o_spec] * 3, out_shape=[o_shape] * 3, operands=[n2, w_gate, w_up],
        scratch_shapes=[pltpu.VMEM((tm, tn), F32)] * (2 * (nk > 1)),
        semantics=("parallel", "parallel", "arbitrary"), jobs=jobs)


def _swiglu_bwd_act(dh2, w_down, gate, up, *, tm=1024, tn=512, tk=2048):
    m, kdim = dh2.shape
    n = w_down.shape[0]
    tm, tn, tk = _tile(m, tm), _tile(n, tn), _tile(kdim, tk)
    nk = kdim // tk

    sub = _tile(tn, 256)

    def body(a_ref, b_ref, gate_ref, up_ref, dgate_ref, dup_ref, *acc):
        def finish(dact, cols=slice(None)):
            g = gate_ref[:, cols].astype(F32)
            u = up_ref[:, cols].astype(F32)
            sg = _sigmoid(g)
            dup_ref[:, cols] = (dact * g * sg).astype(BF16)
            dgate_ref[:, cols] = (dact * u * sg * (1.0 + g * (1.0 - sg))).astype(BF16)

        if nk == 1:
            a = a_ref[...]
            for c in range(tn // sub):
                cols = pl.ds(c * sub, sub)
                finish(_dot(a, b_ref[cols, :], tb=True), cols)
        else:
            _accumulate(acc, nk, lambda: (_dot(a_ref[...], b_ref[...], tb=True),), finish)

    o_spec = pl.BlockSpec((tm, tn), lambda i, j, k: (i, j))
    o_shape = jax.ShapeDtypeStruct((m, n), BF16)
    return pl.pallas_call(
        body, name="swiglu_bwd_act", grid=(m // tm, n // tn, nk),
        in_specs=[_a_spec(False, tm, tk), _b_spec(True, tk, tn), o_spec, o_spec],
        out_specs=[o_spec] * 2, out_shape=[o_shape] * 2,
        scratch_shapes=[pltpu.VMEM((tm, tn), F32)] * (nk > 1),
        compiler_params=_params(("parallel", "parallel", "arbitrary")),
    )(dh2, w_down, gate, up)


def _swiglu_bwd_in(dgate, dup, w_gate, w_up, *, tm=1024, tn=1024, tk=1408, jobs=()):
    m, kdim = dgate.shape
    n = w_gate.shape[0]
    tm, tn, tk = _tile(m, tm), _tile(n, tn), _tile(kdim, tk)
    nk = kdim // tk

    def body(a1_ref, a2_ref, b1_ref, b2_ref, o_ref, *acc):
        def product():
            return (_dot(a1_ref[...], b1_ref[...], tb=True) + _dot(a2_ref[...], b2_ref[...], tb=True),)

        def finish(total):
            o_ref[...] = total

        _accumulate(acc, nk, product, finish)

    a_spec, b_spec = _a_spec(False, tm, tk), _b_spec(True, tk, tn)
    (out,), carried = _call(
        body, name="swiglu_bwd_in", grid=(m // tm, n // tn, nk),
        in_specs=[a_spec, a_spec, b_spec, b_spec],
        out_specs=[pl.BlockSpec((tm, tn), lambda i, j, k: (i, j))],
        out_shape=[jax.ShapeDtypeStruct((m, n), F32)], operands=[dgate, dup, w_gate, w_up],
        scratch_shapes=[pltpu.VMEM((tm, tn), F32)] * (nk > 1),
        semantics=("parallel", "parallel", "arbitrary"), jobs=jobs)
    return out, carried


def _row_block(rows, cols):
    tr = min(rows, max(16, ELEM_BLOCK_BYTES // (4 * cols) // 16 * 16))
    while rows % tr:
        tr -= 16
    return tr


def _rmsnorm_fwd(x, g, *, name):
    s, d = x.shape
    tr = _row_block(s, d)

    def body(x_ref, g_ref, n_ref):
        xv = x_ref[...]
        r = lax.rsqrt(jnp.mean(xv * xv, axis=-1, keepdims=True) + EPS)
        n_ref[...] = (xv * r * g_ref[...]).astype(BF16)

    row = pl.BlockSpec((tr, d), lambda i: (i, 0))
    return pl.pallas_call(
        body, name=name, grid=(s // tr,), in_specs=[row, pl.BlockSpec((1, d), lambda i: (0, 0))],
        out_specs=row, out_shape=jax.ShapeDtypeStruct((s, d), BF16),
        compiler_params=_params(("parallel",)),
    )(x, g)


def _rmsnorm_bwd_rows(xv, gv, dy):
    r = lax.rsqrt(jnp.mean(xv * xv, axis=-1, keepdims=True) + EPS)
    xhat = xv * r
    dxh = dy * gv
    dx = r * (dxh - xhat * jnp.mean(dxh * xhat, axis=-1, keepdims=True))
    return dx, dy * xhat


def _rmsnorm_bwd(dn, x, g, skip, *, name):
    s, d = x.shape
    tr = _row_block(s, d)

    def body(dn_ref, x_ref, g_ref, skip_ref, dx_ref, dxb_ref, dg_ref):
        dx, dgr = _rmsnorm_bwd_rows(x_ref[...], g_ref[...], dn_ref[...])
        dx = dx + skip_ref[...]
        dx_ref[...] = dx
        dxb_ref[...] = dx.astype(BF16)

        @pl.when(pl.program_id(0) == 0)
        def _():
            dg_ref[...] = jnp.zeros_like(dg_ref)

        dg_ref[...] += jnp.sum(dgr, axis=0, keepdims=True)

    row = pl.BlockSpec((tr, d), lambda i: (i, 0))
    vec = pl.BlockSpec((1, d), lambda i: (0, 0))
    return pl.pallas_call(
        body, name=name, grid=(s // tr,), in_specs=[row, row, vec, row],
        out_specs=[row, row, vec],
        out_shape=[jax.ShapeDtypeStruct((s, d), F32), jax.ShapeDtypeStruct((s, d), BF16),
                   jax.ShapeDtypeStruct((1, d), F32)],
        compiler_params=_params(("arbitrary",)),
    )(dn, x, g, skip)


def _loss_head(h2, g, target):
    s, d = h2.shape
    tr = _row_block(s, d)

    def body(h_ref, g_ref, t_ref, dh_ref, dhb_ref, dg_ref, loss_ref):
        hv = h_ref[...]
        gv = g_ref[...]
        r = lax.rsqrt(jnp.mean(hv * hv, axis=-1, keepdims=True) + EPS)
        err = hv * r * gv - t_ref[...]
        dx, dgr = _rmsnorm_bwd_rows(hv, gv, err * (1.0 / d))
        dh_ref[...] = dx
        dhb_ref[...] = dx.astype(BF16)

        @pl.when(pl.program_id(0) == 0)
        def _():
            dg_ref[...] = jnp.zeros_like(dg_ref)
            loss_ref[...] = jnp.zeros_like(loss_ref)

        dg_ref[...] += jnp.sum(dgr, axis=0, keepdims=True)
        row_loss = jnp.mean(err * err, axis=-1, keepdims=True)
        loss_ref[...] += 0.5 * jnp.sum(row_loss, axis=0, keepdims=True)

    row = pl.BlockSpec((tr, d), lambda i: (i, 0))
    vec = pl.BlockSpec((1, d), lambda i: (0, 0))
    one = pl.BlockSpec((1, 1), lambda i: (0, 0))
    return pl.pallas_call(
        body, name="loss_head", grid=(s // tr,), in_specs=[row, vec, row],
        out_specs=[row, row, vec, one],
        out_shape=[jax.ShapeDtypeStruct((s, d), F32), jax.ShapeDtypeStruct((s, d), BF16),
                   jax.ShapeDtypeStruct((1, d), F32), jax.ShapeDtypeStruct((1, 1), F32)],
        compiler_params=_params(("arbitrary",)),
    )(h2, g, target)


def _attention_bias_tables():
    k = np.arange(-ATT_KB, ATT_KB + 1)[:, None, None]
    delta = k * ATT_BLOCK + np.arange(ATT_BLOCK)[None, None, :] - np.arange(ATT_BLOCK)[None, :, None]
    dist = np.abs(delta)
    count = np.zeros(delta.shape, np.int32)
    for window, dilation in DILATED_PATTERNS:
        count += (delta % dilation == 0) & (dist <= min(window // 2, ATT_NEAR))
    logc = np.where(count > 0, np.log(np.maximum(count, 1)), MASKED)
    return dist.astype(np.float32), logc.astype(np.float32)


def _far_bias_tables(per_class):
    steps = np.abs(np.arange(per_class)[:, None] - np.arange(per_class)[None, :]) * ATT_CLASSES
    valid = (steps > ATT_NEAR) & (steps <= ATT_REACH)
    return steps.astype(np.float32), np.where(valid, 0.0, MASKED).astype(np.float32)


def _to_classes(x):
    s, cols = x.shape
    return jnp.reshape(jnp.transpose(jnp.reshape(x, (s // ATT_CLASSES, ATT_CLASSES, cols)), (1, 0, 2)), (s, cols))


def _from_classes(x):
    s, cols = x.shape
    return jnp.reshape(jnp.transpose(jnp.reshape(x, (ATT_CLASSES, s // ATT_CLASSES, cols)), (1, 0, 2)), (s, cols))


def _head_bias(bias_ref, slope, dist_ref, logc_ref):
    for kk in range(ATT_WINDOW):
        bias_ref[kk] = logc_ref[kk] - slope * dist_ref[kk]
    bias_ref[ATT_WINDOW] = jnp.full((ATT_BLOCK, ATT_BLOCK), MASKED, F32)


def _window_start(i, nq, nwin):
    return jnp.clip(i - ATT_KB, 0, nq - nwin)


def _window_block(j, i):
    rows = pl.ds(pl.multiple_of(j * ATT_BLOCK, ATT_BLOCK), ATT_BLOCK)
    kk = j - i + ATT_KB
    return rows, jnp.where(jnp.logical_and(kk >= 0, kk < ATT_WINDOW), kk, ATT_WINDOW)


def _attention_far_fwd(qkv, slopes, n_heads):
    s = qkv.shape[0]
    per_class = s // ATT_CLASSES
    scale = HEAD_DIM ** -0.5
    dist, logc = _far_bias_tables(per_class)

    def body(slope_ref, q_ref, k_ref, v_ref, dist_ref, logc_ref, o_ref, lse_ref):
        bias = logc_ref[...] - slope_ref[pl.program_id(0)] * dist_ref[...]
        for a in range(ATT_FAR_GROUP):
            rows = pl.ds(a * per_class, per_class)
            sc = _dot(q_ref[rows, :], k_ref[rows, :], tb=True) * scale + bias
            m = jnp.maximum(jnp.max(sc, axis=-1, keepdims=True), ROW_MAX_INIT)
            p = jnp.exp(sc - m)
            l = jnp.maximum(jnp.sum(p, axis=-1, keepdims=True), 1e-30)
            o_ref[rows, :] = (_dot(p.astype(BF16), v_ref[rows, :]) / l).astype(BF16)
            lse_ref[rows, :] = jnp.broadcast_to(m + jnp.log(l), (per_class, HEAD_DIM))

    hh = n_heads
    blk = pl.BlockSpec((ATT_FAR_GROUP * per_class, HEAD_DIM), lambda h, r: (r, h))
    table = pl.BlockSpec(dist.shape, lambda h, r: (0, 0))
    return pl.pallas_call(
        body, name="attention_far_fwd", grid=(hh, ATT_CLASSES // ATT_FAR_GROUP),
        in_specs=[pl.BlockSpec(memory_space=pltpu.SMEM), blk,
                  pl.BlockSpec((ATT_FAR_GROUP * per_class, HEAD_DIM), lambda h, r: (r, hh + h)),
                  pl.BlockSpec((ATT_FAR_GROUP * per_class, HEAD_DIM), lambda h, r: (r, 2 * hh + h)), table, table],
        out_specs=[blk, blk],
        out_shape=[jax.ShapeDtypeStruct((s, hh * HEAD_DIM), BF16), jax.ShapeDtypeStruct((s, hh * HEAD_DIM), F32)],
        compiler_params=_params(("parallel", "parallel")),
    )(slopes, qkv, qkv, qkv, jnp.asarray(dist), jnp.asarray(logc))


def _attention_fwd(proj, slopes, far_out, far_lse, n_heads, jobs=()):
    s = proj.shape[0]
    nq = s // ATT_BLOCK
    scale = HEAD_DIM ** -0.5
    dist, logc = _attention_bias_tables()

    nwin = min(ATT_WINDOW, nq)

    group = math.gcd(ATT_NEAR_GROUP, nq)

    def body(slope_ref, q_ref, k_ref, v_ref, fo_ref, fl_ref, dist_ref, logc_ref, o_ref, lse_ref, bias_ref, s_ref):
        h, step = pl.program_id(0), pl.program_id(1)

        @pl.when(step == 0)
        def _():
            _head_bias(bias_ref, slope_ref[h], dist_ref, logc_ref)

        for a in range(group):
            i = step * group + a
            mine = pl.ds(a * ATT_BLOCK, ATT_BLOCK)
            q = q_ref[mine, :]
            first = _window_start(i, nq, nwin)
            m = jnp.full((ATT_BLOCK, 1), ROW_MAX_INIT, F32)
            for b in range(nwin):
                rows, kk = _window_block(first + b, i)
                sc = _dot(q, k_ref[rows, :], tb=True) * scale + bias_ref[kk]
                s_ref[a * nwin + b] = sc
                m = jnp.maximum(m, jnp.max(sc, axis=-1, keepdims=True))
            l = jnp.zeros((ATT_BLOCK, 1), F32)
            acc = jnp.zeros((ATT_BLOCK, HEAD_DIM), F32)
            for b in range(nwin):
                rows, _ = _window_block(first + b, i)
                p = jnp.exp(s_ref[a * nwin + b] - m)
                l = l + jnp.sum(p, axis=-1, keepdims=True)
                acc = acc + _dot(p.astype(BF16), v_ref[rows, :])
            near_lse = m + jnp.log(l)
            far_lse_col = fl_ref[mine, :1]
            lse = jnp.maximum(near_lse, far_lse_col)
            lse = lse + jnp.log(jnp.exp(near_lse - lse) + jnp.exp(far_lse_col - lse))
            o_ref[mine, :] = (acc * (jnp.exp(near_lse - lse) / l)
                              + fo_ref[mine, :].astype(F32) * jnp.exp(far_lse_col - lse)).astype(BF16)
            lse_ref[mine, :] = jnp.broadcast_to(lse, (ATT_BLOCK, HEAD_DIM))

    hh = n_heads
    blk = pl.BlockSpec((group * ATT_BLOCK, HEAD_DIM), lambda h, i: (i, h))
    table = pl.BlockSpec(dist.shape, lambda h, i: (0, 0, 0))
    return _call(
        body, name="attention_fwd", grid=(hh, nq // group),
        in_specs=[pl.BlockSpec(memory_space=pltpu.SMEM), blk,
                  pl.BlockSpec((s, HEAD_DIM), lambda h, i: (0, hh + h)),
                  pl.BlockSpec((s, HEAD_DIM), lambda h, i: (0, 2 * hh + h)), blk, blk, table, table],
        out_specs=[blk, blk],
        out_shape=[jax.ShapeDtypeStruct((s, hh * HEAD_DIM), BF16), jax.ShapeDtypeStruct((s, hh * HEAD_DIM), F32)],
        operands=[slopes, proj, proj, proj, far_out, far_lse, jnp.asarray(dist), jnp.asarray(logc)],
        scratch_shapes=[pltpu.VMEM((ATT_WINDOW + 1, ATT_BLOCK, ATT_BLOCK), F32),
                        pltpu.VMEM((group * nwin, ATT_BLOCK, ATT_BLOCK), F32)],
        semantics=("parallel", "arbitrary"), jobs=jobs)


def _attention_far_bwd(qkv, slopes, out, dout, lse, n_heads):
    s = qkv.shape[0]
    per_class = s // ATT_CLASSES
    scale = HEAD_DIM ** -0.5
    dist, logc = _far_bias_tables(per_class)

    def body(slope_ref, q_ref, k_ref, v_ref, o_ref, do_ref, lse_ref, dist_ref, logc_ref, dq_ref, dk_ref, dv_ref):
        bias = logc_ref[...] - slope_ref[pl.program_id(0)] * dist_ref[...]
        for a in range(ATT_FAR_GROUP):
            rows = pl.ds(a * per_class, per_class)
            q, k, do = q_ref[rows, :], k_ref[rows, :], do_ref[rows, :]
            delta = jnp.sum(do.astype(F32) * o_ref[rows, :].astype(F32), axis=-1, keepdims=True)
            p = jnp.exp(_dot(q, k, tb=True) * scale + bias - lse_ref[rows, :1])
            dv_ref[rows, :] = _dot(p.astype(BF16), do, ta=True).astype(BF16)
            ds = (p * (_dot(do, v_ref[rows, :], tb=True) - delta) * scale).astype(BF16)
            dk_ref[rows, :] = _dot(ds, q, ta=True).astype(BF16)
            dq_ref[rows, :] = _dot(ds, k).astype(BF16)

    hh = n_heads
    blk = pl.BlockSpec((ATT_FAR_GROUP * per_class, HEAD_DIM), lambda h, r: (r, h))
    table = pl.BlockSpec(dist.shape, lambda h, r: (0, 0))
    o_shape = jax.ShapeDtypeStruct((s, hh * HEAD_DIM), BF16)
    return pl.pallas_call(
        body, name="attention_far_bwd", grid=(hh, ATT_CLASSES // ATT_FAR_GROUP),
        in_specs=[pl.BlockSpec(memory_space=pltpu.SMEM), blk,
                  pl.BlockSpec((ATT_FAR_GROUP * per_class, HEAD_DIM), lambda h, r: (r, hh + h)),
                  pl.BlockSpec((ATT_FAR_GROUP * per_class, HEAD_DIM), lambda h, r: (r, 2 * hh + h)),
                  blk, blk, blk, table, table],
        out_specs=[blk] * 3, out_shape=[o_shape] * 3,
        compiler_params=_params(("parallel", "parallel")),
    )(slopes, qkv, qkv, qkv, out, dout, lse, jnp.asarray(dist), jnp.asarray(logc))


def _attention_bwd(proj, slopes, out, lse, dmixed, far_grads, n_heads, jobs=()):
    s = proj.shape[0]
    nq = s // ATT_BLOCK
    scale = HEAD_DIM ** -0.5
    dist, logc = _attention_bias_tables()

    nwin = min(ATT_WINDOW, nq)
    group = math.gcd(ATT_NEAR_GROUP, nq)

    def body(slope_ref, q_ref, k_ref, v_ref, o_ref, do_ref, lse_ref, fdq_ref, fdk_ref, fdv_ref, dist_ref, logc_ref,
             dq_ref, dk_ref, dv_ref, dk_acc, dv_acc, bias_ref):
        h, step = pl.program_id(0), pl.program_id(1)

        @pl.when(step == 0)
        def _():
            dk_acc[...] = jnp.zeros_like(dk_acc)
            dv_acc[...] = jnp.zeros_like(dv_acc)
            _head_bias(bias_ref, slope_ref[h], dist_ref, logc_ref)

        for a in range(group):
            i = step * group + a
            mine = pl.ds(a * ATT_BLOCK, ATT_BLOCK)
            q = q_ref[mine, :]
            do = do_ref[mine, :]
            lse_col = lse_ref[mine, :1]
            delta = jnp.sum(do.astype(F32) * o_ref[mine, :].astype(F32), axis=-1, keepdims=True)
            first = _window_start(i, nq, nwin)
            dq = jnp.zeros((ATT_BLOCK, HEAD_DIM), F32)
            for b in range(nwin):
                rows, kk = _window_block(first + b, i)
                kj = k_ref[rows, :]
                vj = v_ref[rows, :]
                p = jnp.exp(_dot(q, kj, tb=True) * scale + bias_ref[kk] - lse_col)
                dv_acc[rows, :] += _dot(p.astype(BF16), do, ta=True)
                dp = _dot(do, vj, tb=True)
                ds = (p * (dp - delta) * scale).astype(BF16)
                dk_acc[rows, :] += _dot(ds, q, ta=True)
                dq = dq + _dot(ds, kj)
            dq_ref[mine, :] = (dq + fdq_ref[mine, :].astype(F32)).astype(BF16)

        @pl.when(step == nq // group - 1)
        def _():
            dk_ref[...] = (dk_acc[...] + fdk_ref[...].astype(F32)).astype(BF16)
            dv_ref[...] = (dv_acc[...] + fdv_ref[...].astype(F32)).astype(BF16)

    hh = n_heads
    blk = pl.BlockSpec((group * ATT_BLOCK, HEAD_DIM), lambda h, i: (i, h))
    col = pl.BlockSpec((s, HEAD_DIM), lambda h, i: (0, h))
    table = pl.BlockSpec(dist.shape, lambda h, i: (0, 0, 0))
    o_shape = jax.ShapeDtypeStruct((s, hh * HEAD_DIM), BF16)
    return _call(
        body, name="attention_bwd", grid=(hh, nq // group),
        in_specs=[pl.BlockSpec(memory_space=pltpu.SMEM), blk,
                  pl.BlockSpec((s, HEAD_DIM), lambda h, i: (0, hh + h)),
                  pl.BlockSpec((s, HEAD_DIM), lambda h, i: (0, 2 * hh + h)),
                  blk, blk, blk, blk, col, col, table, table],
        out_specs=[blk, col, col], out_shape=[o_shape] * 3,
        operands=[slopes, proj, proj, proj, out, dmixed, lse, *far_grads, jnp.asarray(dist), jnp.asarray(logc)],
        scratch_shapes=[pltpu.VMEM((s, HEAD_DIM), F32)] * 2
        + [pltpu.VMEM((ATT_WINDOW + 1, ATT_BLOCK, ATT_BLOCK), F32)],
        semantics=("parallel", "arbitrary"), jobs=jobs)


def _ret_decays(lgc, lga, strict_c, strict_a):
    c = RET_CHUNK
    rel = (lax.broadcasted_iota(jnp.int32, (c, c), 0) - lax.broadcasted_iota(jnp.int32, (c, c), 1)).astype(F32)
    in_c = (rel > 0) if strict_c else (rel >= 0)
    in_a = (rel < 0) if strict_a else (rel <= 0)
    mask = (jnp.where(in_c, jnp.exp(lgc * jnp.maximum(rel, 0.0)), 0.0)
            + jnp.where(in_a, jnp.exp(lga * jnp.maximum(-rel, 0.0)), 0.0))
    idx = lax.broadcasted_iota(jnp.int32, (c, 1), 0).astype(F32)
    ones = jnp.ones((1, HEAD_DIM), F32)
    dec = dict(
        rel=rel, mask=mask, idx=idx,
        a_c=jnp.exp(lgc * (idx + 1.0)), b_c=jnp.exp(lgc * (c - 1.0 - idx)), chunk_c=jnp.exp(ones * (lgc * c)),
        a_a=jnp.exp(lga * (c - idx)), b_a=jnp.exp(lga * idx), chunk_a=jnp.exp(ones * (lga * c)),
    )
    return dec


def _scaled(x, col):
    return (x.astype(F32) * col).astype(BF16)


def _chunk_rows(i):
    return pl.ds(pl.multiple_of(i * RET_CHUNK, RET_CHUNK), RET_CHUNK)


def _chunk_loop(nc, step, init, unroll=RET_UNROLL):
    group = math.gcd(nc, unroll)

    def trip(t, carry):
        for u in range(group):
            carry = step(t * group + u, carry)
        return carry

    return lax.fori_loop(0, nc // group, trip, init)


def _retention(a, b, c, lg_c, lg_a, *, strict_c, strict_a, scale, n_heads, name, gate=None, norm_w=None, jobs=()):
    s = a[0].shape[0]
    nc = s // RET_CHUNK
    epilogue = gate is not None

    def body(*refs):
        lgc_ref, lga_ref, a_ref, b_ref, c_ref = refs[:5]
        if epilogue:
            g_ref, w_ref, o_ref, mix_ref, sa_ref = refs[5:]
        else:
            o_ref, sa_ref = refs[5:]
        h = pl.program_id(0)
        dec = _ret_decays(lgc_ref[h], lga_ref[h], strict_c, strict_a)

        def reverse(t, state):
            i = nc - 1 - t
            sa_ref[i] = state.astype(BF16)
            rows = _chunk_rows(i)
            return state * dec["chunk_a"] + _dot(_scaled(b_ref[rows, :], dec["b_a"]), c_ref[rows, :], ta=True)

        _chunk_loop(nc, reverse, jnp.zeros((HEAD_DIM, HEAD_DIM), F32))

        def forward(i, state):
            rows = _chunk_rows(i)
            ai, bi, ci = a_ref[rows, :], b_ref[rows, :], c_ref[rows, :]
            inner = (_dot(ai, bi, tb=True) * dec["mask"]).astype(BF16)
            out = (_dot(inner, ci) + _dot(_scaled(ai, dec["a_c"]), state.astype(BF16))
                   + _dot(_scaled(ai, dec["a_a"]), sa_ref[i])) * scale
            o_ref[rows, :] = out.astype(BF16)
            if epilogue:
                r = lax.rsqrt(jnp.mean(out * out, axis=-1, keepdims=True) + EPS)
                g = g_ref[rows, :].astype(F32)
                mix_ref[rows, :] = (out * r * w_ref[...] * (g * _sigmoid(g))).astype(BF16)
            return state * dec["chunk_c"] + _dot(_scaled(bi, dec["b_c"]), ci, ta=True)

        _chunk_loop(nc, forward, jnp.zeros((HEAD_DIM, HEAD_DIM), F32))

    def col(first):
        return pl.BlockSpec((s, HEAD_DIM), lambda h: (0, first + h))

    smem = pl.BlockSpec(memory_space=pltpu.SMEM)
    in_specs = [smem, smem, col(a[1]), col(b[1]), col(c[1])]
    operands = [lg_c, lg_a, a[0], b[0], c[0]]
    o_shape = jax.ShapeDtypeStruct((s, n_heads * HEAD_DIM), BF16)
    out_specs, out_shape = [col(0)], [o_shape]
    if epilogue:
        in_specs += [col(gate[1]), pl.BlockSpec((1, HEAD_DIM), lambda h: (0, h))]
        operands += [gate[0], norm_w]
        out_specs, out_shape = [col(0)] * 2, [o_shape] * 2
    res, carried = _call(
        body, name=name, grid=(n_heads,), in_specs=in_specs, out_specs=out_specs, out_shape=out_shape,
        operands=operands, scratch_shapes=[pltpu.VMEM((nc, HEAD_DIM, HEAD_DIM), BF16)],
        semantics=("parallel",), jobs=jobs)
    res = res if epilogue else res[0]
    return (res, carried) if jobs else res


def _retention_decay_grads(a, b, c, e, lg_c, lg_a, *, scale, n_heads):
    s = a[0].shape[0]
    nc = s // RET_CHUNK
    cf = float(RET_CHUNK)

    def body(lgc_ref, lga_ref, a_ref, b_ref, c_ref, e_ref, gc_ref, ga_ref, sa_ref, ta_ref):
        h = pl.program_id(0)
        lgc, lga = lgc_ref[h], lga_ref[h]
        dec = _ret_decays(lgc, lga, True, True)
        rel, idx = dec["rel"], dec["idx"]
        w_c = jnp.where(rel > 0, rel * jnp.exp(lgc * jnp.maximum(rel, 0.0)), 0.0)
        w_a = jnp.where(rel < 0, -rel * jnp.exp(lga * jnp.maximum(-rel, 0.0)), 0.0)
        zero = jnp.zeros((HEAD_DIM, HEAD_DIM), F32)

        def reverse(t, carry):
            st, dst = carry
            i = nc - 1 - t
            sa_ref[i] = st.astype(BF16)
            ta_ref[i] = dst.astype(BF16)
            rows = _chunk_rows(i)
            bi, ci = b_ref[rows, :], c_ref[rows, :]
            st_new = st * dec["chunk_a"] + _dot(_scaled(bi, dec["b_a"]), ci, ta=True)
            dst_new = (cf * st + dst) * dec["chunk_a"] + _dot(_scaled(bi, idx * dec["b_a"]), ci, ta=True)
            return st_new, dst_new

        _chunk_loop(nc, reverse, (zero, zero))

        def forward(i, carry):
            st, dst, acc_c, acc_a = carry
            rows = _chunk_rows(i)
            ai, bi, ci = a_ref[rows, :], b_ref[rows, :], c_ref[rows, :]
            ev = e_ref[rows, :].astype(F32)
            pg = _dot(ai, bi, tb=True) * _dot(e_ref[rows, :], ci, tb=True)
            a_c, a_a = _scaled(ai, dec["a_c"]), _scaled(ai, dec["a_a"])
            inter_c = _dot(a_c, st.astype(BF16)) * (idx + 1.0) + _dot(a_c, dst.astype(BF16))
            inter_a = _dot(a_a, sa_ref[i]) * (cf - idx) + _dot(a_a, ta_ref[i])
            acc_c = acc_c + jnp.sum(pg * w_c, axis=0, keepdims=True) + jnp.sum(inter_c * ev, axis=0, keepdims=True)
            acc_a = acc_a + jnp.sum(pg * w_a, axis=0, keepdims=True) + jnp.sum(inter_a * ev, axis=0, keepdims=True)
            st_new = st * dec["chunk_c"] + _dot(_scaled(bi, dec["b_c"]), ci, ta=True)
            dst_new = ((cf * st + dst) * dec["chunk_c"]
                       + _dot(_scaled(bi, (cf - 1.0 - idx) * dec["b_c"]), ci, ta=True))
            return st_new, dst_new, acc_c, acc_a

        row = jnp.zeros((1, HEAD_DIM), F32)
        _, _, acc_c, acc_a = _chunk_loop(nc, forward, (zero, zero, row, row))
        gc_ref[...] = jnp.broadcast_to(jnp.sum(acc_c, axis=-1, keepdims=True) * scale, gc_ref.shape)
        ga_ref[...] = jnp.broadcast_to(jnp.sum(acc_a, axis=-1, keepdims=True) * scale, ga_ref.shape)

    def col(first):
        return pl.BlockSpec((s, HEAD_DIM), lambda h: (0, first + h))

    smem = pl.BlockSpec(memory_space=pltpu.SMEM)
    o_spec = pl.BlockSpec((1, 8, HEAD_DIM), lambda h: (h, 0, 0))
    o_shape = jax.ShapeDtypeStruct((n_heads, 8, HEAD_DIM), F32)
    gc, ga = pl.pallas_call(
        body, name="retention_decay_grads", grid=(n_heads,),
        in_specs=[smem, smem, col(a[1]), col(b[1]), col(c[1]), col(e[1])],
        out_specs=[o_spec] * 2, out_shape=[o_shape] * 2,
        scratch_shapes=[pltpu.VMEM((nc, HEAD_DIM, HEAD_DIM), BF16)] * 2,
        compiler_params=_params(("parallel",)),
    )(lg_c, lg_a, a[0], b[0], c[0], e[0])
    return gc[:, 0, 0], ga[:, 0, 0]


def _ret_gate_bwd(dmixed, first_col, out, proj, gate_col, norm_w, n_heads):
    s = out.shape[0]
    tr = _row_block(s, 8 * HEAD_DIM)

    def body(dm_ref, o_ref, g_ref, w_ref, do_ref, dg_ref, dw_ref):
        dm = dm_ref[...].astype(F32)
        ov = o_ref[...].astype(F32)
        g = g_ref[...].astype(F32)
        w = w_ref[...]
        r = lax.rsqrt(jnp.mean(ov * ov, axis=-1, keepdims=True) + EPS)
        ohat = ov * r
        sg = _sigmoid(g)
        silu = g * sg
        dg_ref[...] = (dm * ohat * w * sg * (1.0 + g * (1.0 - sg))).astype(BF16)
        dohat = dm * w * silu
        do_ref[...] = (r * (dohat - ohat * jnp.mean(dohat * ohat, axis=-1, keepdims=True))).astype(BF16)

        @pl.when(pl.program_id(1) == 0)
        def _():
            dw_ref[...] = jnp.zeros_like(dw_ref)

        dw_ref[...] += jnp.sum(dm * ohat * silu, axis=0, keepdims=True)

    def blk(first):
        return pl.BlockSpec((tr, HEAD_DIM), lambda h, i: (i, first + h))

    vec = pl.BlockSpec((1, HEAD_DIM), lambda h, i: (0, h))
    o_shape = jax.ShapeDtypeStruct((s, n_heads * HEAD_DIM), BF16)
    return pl.pallas_call(
        body, name="ret_gate_bwd", grid=(n_heads, s // tr),
        in_specs=[blk(first_col), blk(0), blk(gate_col), vec],
        out_specs=[blk(0), blk(0), vec],
        out_shape=[o_shape, o_shape, jax.ShapeDtypeStruct((1, n_heads * HEAD_DIM), F32)],
        compiler_params=_params(("parallel", "arbitrary")),
    )(dmixed, out, proj, norm_w)


def _step(x, target, norm_mix_w, ret_decay_fwd, ret_decay_bwd, ret_norm_w, norm_ffn_w, norm_final_w, own, pos):
    d = x.shape[1]
    nh = d // (2 * HEAD_DIM)
    scale = HEAD_DIM ** -0.5
    slopes = jnp.exp2(-8.0 * jnp.arange(1, nh + 1, dtype=F32) / nh)
    lg_f = -jnp.exp(ret_decay_fwd)
    lg_b = -jnp.exp(ret_decay_bwd)
    q_r, k_r, v_r, g_r = 3 * nh, 4 * nh, 5 * nh, 6 * nh
    ax = BIG_AXIS

    def gather(names, arrays, stage):
        return _gather_job(arrays, [ax[k] for k in names], stage)

    def add_halves(k, g, received):
        return _add_halves(g, received, ax[k], pos, name="grad_add_halves_" + k)

    def sum_parts(k, g, received, parts):
        return _sum_chip_parts(g, received, parts, ax[k], pos, name="grad_sum_parts_" + k)

    (w_in,) = _run_jobs([gather(["w_in"], [own["w_in"]], "ici"), gather(["w_in"], [own["w_in"]], "d2d")],
                        name="all_gather_w_in")
    n1 = _rmsnorm_fwd(x, norm_mix_w, name="norm_mix_fwd")
    proj, [[w_gate]] = _matmul(n1, w_in, name="in_proj", out_dtype=BF16,
                               jobs=[gather(["w_gate"], [own["w_gate"]], "ici")])
    qkv_classes = _to_classes(proj[:, :3 * nh * HEAD_DIM])
    (ret, ret_mixed), [[w_gate], [w_out]] = _retention(
        (proj, q_r), (proj, k_r), (proj, v_r), lg_f, lg_b, strict_c=False, strict_a=True, scale=scale, n_heads=nh,
        name="retention_fwd", gate=(proj, g_r), norm_w=ret_norm_w,
        jobs=[gather(["w_gate"], [w_gate], "d2d"), gather(["w_out"], [own["w_out"]], "ici")])
    far_out, far_lse = _attention_far_fwd(qkv_classes, slopes, nh)
    (attn, lse), [[w_out], [w_up]] = _attention_fwd(
        proj, slopes, _from_classes(far_out), _from_classes(far_lse), nh,
        jobs=[gather(["w_out"], [w_out], "d2d"), gather(["w_up"], [own["w_up"]], "ici")])
    mixed = jnp.concatenate([attn, ret_mixed], axis=1)
    h1, [[w_up]] = _matmul(mixed, w_out, name="out_proj", residual=x, jobs=[gather(["w_up"], [w_up], "d2d")])
    n2 = _rmsnorm_fwd(h1, norm_ffn_w, name="norm_ffn_fwd")
    (gate, up, act), [[w_down]] = _swiglu_fwd(n2, w_gate, w_up, jobs=[gather(["w_down"], [own["w_down"]], "ici")])
    (w_down,) = _run_jobs([gather(["w_down"], [w_down], "d2d")], name="all_gather_w_down_sibling")
    h2 = _matmul(act, w_down, name="down_proj", residual=h1, tk=2816)
    dh2, dh2_b, d_norm_final, loss = _loss_head(h2, norm_final_w, target)

    dgate, dup = _swiglu_bwd_act(dh2_b, w_down, gate, up)
    g_down = _weight_grad(act, dh2_b, name="grad_w_down")
    g_gate, [[r_down]] = _weight_grad(n2, dgate, name="grad_w_gate", jobs=[_exchange_job([g_down], [ax["w_down"]])])
    s_down = add_halves("w_down", g_down, r_down)
    g_up, [[r_gate], [p_down]] = _weight_grad(
        n2, dup, name="grad_w_up",
        jobs=[_exchange_job([g_gate], [ax["w_gate"]]), _send_sums_job([s_down], [ax["w_down"]])])
    s_gate = add_halves("w_gate", g_gate, r_gate)
    h_down = sum_parts("w_down", g_down, r_down, p_down)
    dn2, [[r_up], [p_gate]] = _swiglu_bwd_in(
        dgate, dup, w_gate, w_up,
        jobs=[_exchange_job([g_up], [ax["w_up"]]), _send_sums_job([s_gate], [ax["w_gate"]])])
    s_up = add_halves("w_up", g_up, r_up)
    h_gate = sum_parts("w_gate", g_gate, r_gate, p_gate)
    dh1, dh1_b, d_norm_ffn = _rmsnorm_bwd(dn2, h1, norm_ffn_w, dh2, name="norm_ffn_bwd")

    dmixed, [[gr_down]] = _matmul(dh1_b, w_out, name="out_proj_bwd", tb=True, out_dtype=BF16,
                                  jobs=[_join_job([h_down], [ax["w_down"]])])
    far_in = [_to_classes(t) for t in (attn, dmixed[:, :nh * HEAD_DIM], lse)]
    g_out = _weight_grad(mixed, dh1_b, name="grad_w_out")
    d_ret, dg_r, d_ret_norm = _ret_gate_bwd(dmixed, nh, ret, proj, g_r, ret_norm_w, nh)
    far_grads = _attention_far_bwd(qkv_classes, slopes, *far_in, nh)
    far_grads = [_from_classes(t) for t in far_grads]
    dq_r, [[gr_gate]] = _retention(
        (d_ret, 0), (proj, v_r), (proj, k_r), lg_f, lg_b, strict_c=False, strict_a=True, scale=scale, n_heads=nh,
        name="retention_dq", jobs=[_join_job([h_gate], [ax["w_gate"]])])
    (dq_a, dk_a, dv_a), [[p_up], [r_out]] = _attention_bwd(
        proj, slopes, attn, lse, dmixed, far_grads, nh,
        jobs=[_send_sums_job([s_up], [ax["w_up"]]), _exchange_job([g_out], [ax["w_out"]])])
    s_out = add_halves("w_out", g_out, r_out)
    h_up = sum_parts("w_up", g_up, r_up, p_up)
    dv_r, [[p_out], [gr_up]] = _retention(
        (proj, k_r), (proj, q_r), (d_ret, 0), lg_b, lg_f, strict_c=True, strict_a=False, scale=scale, n_heads=nh,
        name="retention_dv", jobs=[_send_sums_job([s_out], [ax["w_out"]]), _join_job([h_up], [ax["w_up"]])])
    h_out = sum_parts("w_out", g_out, r_out, p_out)
    dk_r, [[gr_out]] = _retention(
        (proj, v_r), (d_ret, 0), (proj, q_r), lg_b, lg_f, strict_c=True, strict_a=False, scale=scale, n_heads=nh,
        name="retention_dk", jobs=[_join_job([h_out], [ax["w_out"]])])
    dlg_f, dlg_b = _retention_decay_grads((proj, q_r), (proj, k_r), (proj, v_r), (d_ret, 0), lg_f, lg_b,
                                          scale=scale, n_heads=nh)
    dproj = [dq_a, dk_a, dv_a, dq_r, dk_r, dv_r, dg_r]
    g_in = _weight_grad_pieces(n1, dproj, name="grad_w_in")
    (r_in,) = _run_jobs([_exchange_job([g_in], [ax["w_in"]])], name="grad_exchange_w_in")
    s_in = add_halves("w_in", g_in, r_in)
    dn1, [[p_in]] = _matmul_pieces_nt(dproj, w_in, name="in_proj_bwd", jobs=[_send_sums_job([s_in], [ax["w_in"]])])
    dx, _, d_norm_mix = _rmsnorm_bwd(dn1, x, norm_mix_w, dh1, name="norm_mix_bwd")
    h_in = sum_parts("w_in", g_in, r_in, p_in)
    (gr_in,) = _run_jobs([_join_job([h_in], [ax["w_in"]])], name="grad_join_w_in")

    small = dict(loss=loss[0, 0], norm_mix_w=d_norm_mix, ret_decay_fwd=dlg_f * lg_f, ret_decay_bwd=dlg_b * lg_b,
                 ret_norm_w=d_ret_norm, norm_ffn_w=d_norm_ffn, norm_final_w=d_norm_final)
    return dx, dict(w_in=gr_in, w_out=gr_out, w_gate=gr_gate, w_up=gr_up, w_down=gr_down), small


def _mesh_position():
    x, y, c = lax.axis_index("x"), lax.axis_index("y"), lax.axis_index("c")
    chips = [(1 - x, y), (x, 1 - y), (1 - x, 1 - y)]
    return x, y, c, chips


def _ds(start, size):
    if isinstance(start, int):
        return pl.ds(start, size)
    return pl.ds(pl.multiple_of(start * size, size), size)


def _region(ref, axis, shard, half, shard_size, half_size):
    along = slice(None) if shard is None else _ds(shard, shard_size)
    other = slice(None) if half is None else _ds(half, half_size)
    return ref.at[other, along] if axis == 1 else ref.at[along, other]


def _gather_job(full, axes, stage):
    n = len(full)

    def copies(refs, sems):
        send_sem, recv_sem = sems
        x, y, c, chips = _mesh_position()
        me = 2 * x + y

        def copy(w, k, shard, half, target):
            rows_cols = full[w].shape
            place = _region(refs[w], axes[w], shard, half, rows_cols[axes[w]] // N_CHIPS, rows_cols[1 - axes[w]] // 2)
            return pltpu.make_async_remote_copy(
                src_ref=place, dst_ref=place, send_sem=send_sem.at[w, k], recv_sem=recv_sem.at[w, k],
                device_id=target, device_id_type=MESH)

        def sent(w, k):
            if stage == "ici":
                return copy(w, k, me, c, (chips[k][0], chips[k][1], c))
            return copy(w, k, 2 * chips[k][0] + chips[k][1], c, (x, y, 1 - c))

        def landed(w, k):
            return copy(w, k, 2 * chips[k][0] + chips[k][1], c if stage == "ici" else 1 - c, (x, y, 1 - c))

        return sent, landed

    def start(refs, sems):
        sent, _ = copies(refs, sems)
        for w in range(n):
            for k in range(3):
                sent(w, k).start()

    def finish(refs, sems):
        sent, landed = copies(refs, sems)
        for w in range(n):
            for k in range(3):
                landed(w, k).wait_recv()
                sent(w, k).wait_send()

    return _Job(ios=full, sems=[pltpu.SemaphoreType.DMA((n, 3))] * 2, start=start, finish=finish)


def _exchange_job(grads, axes):
    n = len(grads)

    def half_shape(w):
        return tuple(d // 2 if a != axes[w] else d for a, d in enumerate(grads[w].shape))

    def copy(refs, sems, w):
        x, y, c, _ = _mesh_position()
        return pltpu.make_async_remote_copy(
            src_ref=_region(refs[w], axes[w], None, 1 - c, 0, half_shape(w)[1 - axes[w]]), dst_ref=refs[n + w],
            send_sem=sems[0].at[w], recv_sem=sems[1].at[w], device_id=(x, y, 1 - c), device_id_type=MESH)

    def start(refs, sems):
        for w in range(n):
            copy(refs, sems, w).start()

    def finish(refs, sems):
        for w in range(n):
            copy(refs, sems, w).wait()

    return _Job(ins=grads, outs=[jax.ShapeDtypeStruct(half_shape(w), F32) for w in range(n)],
                sems=[pltpu.SemaphoreType.DMA((n,))] * 2, start=start, finish=finish)


def _half_block_spec(axis, block, half_blocks, use_half):
    if axis == 1:
        if use_half:
            return pl.BlockSpec(block, lambda i, pos: (pos[0] * half_blocks + i, 0))
        return pl.BlockSpec(block, lambda i, pos: (i, 0))
    if use_half:
        return pl.BlockSpec(block, lambda i, pos: (i, pos[0]))
    return pl.BlockSpec(block, lambda i, pos: (i, 0))


def _add_halves(grad, received, axis, pos, *, name):
    rows, cols = received.shape
    tr = _row_block(rows, cols)
    nb = rows // tr

    def body(pos_ref, g_ref, r_ref, o_ref):
        o_ref[...] = (g_ref[...] + r_ref[...]).astype(BF16)

    blk = (tr, cols)
    return pl.pallas_call(
        body, name=name, out_shape=jax.ShapeDtypeStruct((rows, cols), BF16),
        grid_spec=pltpu.PrefetchScalarGridSpec(
            num_scalar_prefetch=1, grid=(nb,),
            in_specs=[_half_block_spec(axis, blk, nb, True), _half_block_spec(axis, blk, nb, False)],
            out_specs=_half_block_spec(axis, blk, nb, False)),
        compiler_params=_params(("parallel",)),
    )(pos, grad, received)


def _send_sums_job(sums, axes):
    n = len(sums)

    def part_shape(w):
        return tuple(d // N_CHIPS if a == axes[w] else d for a, d in enumerate(sums[w].shape))

    def copy(refs, sems, w, k):
        x, y, c, chips = _mesh_position()
        shard = 2 * chips[k][0] + chips[k][1]
        return pltpu.make_async_remote_copy(
            src_ref=_region(refs[w], axes[w], shard, None, part_shape(w)[axes[w]], 0), dst_ref=refs[n + w].at[k],
            send_sem=sems[0].at[w, k], recv_sem=sems[1].at[w, k],
            device_id=(chips[k][0], chips[k][1], c), device_id_type=MESH)

    def start(refs, sems):
        for w in range(n):
            for k in range(3):
                copy(refs, sems, w, k).start()

    def finish(refs, sems):
        for w in range(n):
            for k in range(3):
                copy(refs, sems, w, k).wait()

    return _Job(ins=sums, outs=[jax.ShapeDtypeStruct((3,) + part_shape(w), BF16) for w in range(n)],
                sems=[pltpu.SemaphoreType.DMA((n, 3))] * 2, start=start, finish=finish)


def _sum_chip_parts(grad, received, parts, axis, pos, *, name):
    _, rows, cols = parts.shape
    tr = _row_block(rows, cols)
    nb = rows // tr
    blk = (tr, cols)

    def body(pos_ref, g_ref, r_ref, p_ref, o_ref):
        total = g_ref[...] + r_ref[...]
        for k in range(3):
            total = total + p_ref[k].astype(F32)
        o_ref[...] = total

    if axis == 1:
        g_spec = pl.BlockSpec(blk, lambda i, pos: (pos[0] * nb + i, pos[1]))
        r_spec = pl.BlockSpec(blk, lambda i, pos: (i, pos[1]))
        o_spec = pl.BlockSpec(blk, lambda i, pos: (pos[0] * nb + i, 0))
        shard_shape = (2 * rows, cols)
    else:
        g_spec = pl.BlockSpec(blk, lambda i, pos: (pos[1] * nb + i, pos[0]))
        r_spec = pl.BlockSpec(blk, lambda i, pos: (pos[1] * nb + i, 0))
        o_spec = pl.BlockSpec(blk, lambda i, pos: (i, pos[0]))
        shard_shape = (rows, 2 * cols)
    return pl.pallas_call(
        body, name=name, out_shape=jax.ShapeDtypeStruct(shard_shape, F32),
        grid_spec=pltpu.PrefetchScalarGridSpec(
            num_scalar_prefetch=1, grid=(nb,),
            in_specs=[g_spec, r_spec, pl.BlockSpec((3,) + blk, lambda i, pos: (0, i, 0))],
            out_specs=o_spec),
        compiler_params=_params(("parallel",)),
    )(pos, grad, received, parts)


def _join_job(shards, axes):
    n = len(shards)

    def copy(refs, sems, w, other):
        x, y, c, _ = _mesh_position()
        place = _region(refs[w], axes[w], None, 1 - c if other else c, 0, shards[w].shape[1 - axes[w]] // 2)
        return pltpu.make_async_remote_copy(
            src_ref=place, dst_ref=place, send_sem=sems[0].at[w], recv_sem=sems[1].at[w],
            device_id=(x, y, 1 - c), device_id_type=MESH)

    def start(refs, sems):
        for w in range(n):
            copy(refs, sems, w, False).start()

    def finish(refs, sems):
        for w in range(n):
            copy(refs, sems, w, True).wait_recv()
            copy(refs, sems, w, False).wait_send()

    return _Job(ios=shards, sems=[pltpu.SemaphoreType.DMA((n,))] * 2, start=start, finish=finish)


def _all_reduce_small(vec):
    rows, cols = vec.shape

    def body(v_ref, o_ref, land_ref, send_sem, recv_sem):
        x, y, c, _ = _mesh_position()
        me = 4 * x + 2 * y + c
        land_ref[me] = v_ref[...]
        copies = []
        for k in range(1, 8):
            px, py, pc = x ^ (k >> 2), y ^ ((k >> 1) & 1), c ^ (k & 1)
            copies.append(pltpu.make_async_remote_copy(
                src_ref=v_ref, dst_ref=land_ref.at[me], send_sem=send_sem.at[k], recv_sem=recv_sem.at[k],
                device_id=(px, py, pc), device_id_type=MESH))
        for cp in copies:
            cp.start()
        for k in range(1, 8):
            peer = me ^ k
            pltpu.make_async_remote_copy(
                src_ref=v_ref, dst_ref=land_ref.at[peer], send_sem=send_sem.at[k], recv_sem=recv_sem.at[k],
                device_id=(x, y, c), device_id_type=MESH).wait_recv()
        for cp in copies:
            cp.wait_send()
        total = land_ref[0]
        for k in range(1, 8):
            total = total + land_ref[k]
        o_ref[...] = total

    vmem = pl.BlockSpec(memory_space=pltpu.VMEM)
    return pl.pallas_call(
        body, name="all_reduce_small", in_specs=[vmem], out_specs=vmem,
        out_shape=jax.ShapeDtypeStruct((rows, cols), F32),
        scratch_shapes=[pltpu.VMEM((8, rows, cols), F32), pltpu.SemaphoreType.DMA((8,)), pltpu.SemaphoreType.DMA((8,))],
    )(vec)


def _adamw(w, g, m, v, *, name):
    rows, cols = w.shape
    tr = _row_block(rows, cols) if rows % 8 == 0 else rows
    bc1 = 1.0 - ADAM_B1 ** ADAM_STEP
    bc2 = 1.0 - ADAM_B2 ** ADAM_STEP

    def body(w_ref, g_ref, m_ref, v_ref, go_ref, d_ref, mo_ref, vo_ref):
        gv = g_ref[...]
        go_ref[...] = gv
        mn = ADAM_B1 * m_ref[...] + (1.0 - ADAM_B1) * gv
        vn = ADAM_B2 * v_ref[...] + (1.0 - ADAM_B2) * (gv * gv)
        mo_ref[...] = mn
        vo_ref[...] = vn
        d_ref[...] = -ADAM_LR * ((mn / bc1) / (jnp.sqrt(vn / bc2) + ADAM_EPS) + ADAM_WD * w_ref[...])

    blk = pl.BlockSpec((tr, cols), lambda i: (i, 0))
    shape = jax.ShapeDtypeStruct((rows, cols), F32)
    return pl.pallas_call(
        body, name=name, grid=(rows // tr,), in_specs=[blk] * 4, out_specs=[blk] * 4, out_shape=[shape] * 4,
        compiler_params=_params(("parallel",)),
    )(w, g, m, v)


def _to_bf16_in_place(w, axis, pos, *, name):
    rows, cols = w.shape
    tr = _row_block(rows, cols)
    nb = rows // tr

    def body(pos_ref, w_ref, o_ref):
        o_ref[...] = w_ref[...].astype(BF16)

    if axis == 1:
        o_spec = pl.BlockSpec((tr, cols), lambda i, pos: (i, pos[1]))
        full_shape = (rows, N_CHIPS * cols)
    else:
        o_spec = pl.BlockSpec((tr, cols), lambda i, pos: (pos[1] * nb + i, 0))
        full_shape = (N_CHIPS * rows, cols)
    return pl.pallas_call(
        body, name=name, out_shape=jax.ShapeDtypeStruct(full_shape, BF16),
        grid_spec=pltpu.PrefetchScalarGridSpec(
            num_scalar_prefetch=1, grid=(nb,),
            in_specs=[pl.BlockSpec((tr, cols), lambda i, pos: (i, 0))], out_specs=o_spec),
        compiler_params=_params(("parallel",)),
    )(pos, w)


BIG = ("w_in", "w_out", "w_gate", "w_up", "w_down")
BIG_AXIS = dict(w_in=1, w_out=0, w_gate=1, w_up=1, w_down=0)
SMALL = ("norm_mix_w", "ret_decay_fwd", "ret_decay_bwd", "ret_norm_w", "norm_ffn_w", "norm_final_w")
ALL_WEIGHTS = ("norm_mix_w", "w_in", "ret_decay_fwd", "ret_decay_bwd", "ret_norm_w", "w_out", "norm_ffn_w",
               "w_gate", "w_up", "w_down", "norm_final_w")
SMALL_ROW = 128 * 8


def _pack_small(small):
    pieces = [jnp.reshape(small["loss"], (1,))] + [jnp.reshape(small[k], (-1,)) for k in SMALL]
    rows = []
    for p in pieces:
        pad = -p.shape[0] % (8 * SMALL_ROW)
        rows.append(jnp.reshape(jnp.pad(p, (0, pad)), (-1, SMALL_ROW)))
    return jnp.concatenate(rows, axis=0)


def _unpack_small(block, like):
    out, row = {}, 0
    for k in ("loss",) + SMALL:
        size = 1 if k == "loss" else like[k].size
        nrows = -(-size // (8 * SMALL_ROW)) * 8
        out[k] = jnp.reshape(block[row:row + nrows], (-1,))[:size]
        row += nrows
    return out


def kernel(x, norm_mix_w, w_in, ret_decay_fwd, ret_decay_bwd, ret_norm_w, w_out, norm_ffn_w, w_gate, w_up, w_down, norm_final_w, loss_target, m_norm_mix_w, m_w_in, m_ret_decay_fwd, m_ret_decay_bwd, m_ret_norm_w, m_w_out, m_norm_ffn_w, m_w_gate, m_w_up, m_w_down, m_norm_final_w, v_norm_mix_w, v_w_in, v_ret_decay_fwd, v_ret_decay_bwd, v_ret_norm_w, v_w_out, v_norm_ffn_w, v_w_gate, v_w_up, v_w_down, v_norm_final_w):
    weights = dict(norm_mix_w=norm_mix_w, w_in=w_in, ret_decay_fwd=ret_decay_fwd, ret_decay_bwd=ret_decay_bwd,
                   ret_norm_w=ret_norm_w, w_out=w_out, norm_ffn_w=norm_ffn_w, w_gate=w_gate, w_up=w_up,
                   w_down=w_down, norm_final_w=norm_final_w)
    m_in = dict(norm_mix_w=m_norm_mix_w, w_in=m_w_in, ret_decay_fwd=m_ret_decay_fwd, ret_decay_bwd=m_ret_decay_bwd,
                ret_norm_w=m_ret_norm_w, w_out=m_w_out, norm_ffn_w=m_norm_ffn_w, w_gate=m_w_gate, w_up=m_w_up,
                w_down=m_w_down, norm_final_w=m_norm_final_w)
    v_in = dict(norm_mix_w=v_norm_mix_w, w_in=v_w_in, ret_decay_fwd=v_ret_decay_fwd, ret_decay_bwd=v_ret_decay_bwd,
                ret_norm_w=v_ret_norm_w, w_out=v_w_out, norm_ffn_w=v_norm_ffn_w, w_gate=v_w_gate, w_up=v_w_up,
                w_down=v_w_down, norm_final_w=v_norm_final_w)
    pos = jnp.stack([lax.axis_index("c"), 2 * lax.axis_index("x") + lax.axis_index("y")]).astype(jnp.int32)

    own = {k: _to_bf16_in_place(weights[k][0], BIG_AXIS[k], pos, name="cast_" + k) for k in BIG}

    dx, grad_w, small = _step(
        x[0], loss_target[0], norm_mix_w, ret_decay_fwd[0], ret_decay_bwd[0], ret_norm_w, norm_ffn_w,
        norm_final_w[None, :], own, pos)

    like = {k: weights[k] for k in SMALL}
    reduced = _unpack_small(_all_reduce_small(_pack_small(small)), like)
    loss = reduced["loss"][0]
    for k in SMALL:
        grad_w[k] = jnp.reshape(reduced[k], (1, -1))

    delta, new_m, new_v = {}, {}, {}
    for k in ALL_WEIGHTS:
        shape = weights[k].shape
        as2d = (lambda t: jnp.reshape(t, (-1, shape[-1])))
        grad_w[k], delta[k], new_m[k], new_v[k] = (jnp.reshape(t, shape) for t in _adamw(
            as2d(weights[k]), as2d(grad_w[k]), as2d(m_in[k]), as2d(v_in[k]), name="adamw_" + k))

    return (loss, dx[None], *[grad_w[k] for k in ALL_WEIGHTS], *[delta[k] for k in ALL_WEIGHTS],
            *[new_m[k] for k in ALL_WEIGHTS], *[new_v[k] for k in ALL_WEIGHTS])
```

```python
import functools
import math

import numpy as np
import jax
import jax.numpy as jnp
from jax import lax
from jax.experimental import pallas as pl
from jax.experimental.pallas import tpu as pltpu

F32 = jnp.float32
BF16 = jnp.bfloat16
MESH = pl.DeviceIdType.MESH

HEAD_DIM = 128
RET_CHUNK = 128
RET_UNROLL = 8
EPS = 1e-6
DILATED_PATTERNS = ((128, 1), (512, 4), (2048, 16))
ATT_BLOCK = 256
ATT_REACH = max(w // 2 for w, _ in DILATED_PATTERNS)
ATT_NEAR = ATT_BLOCK
ATT_CLASSES = DILATED_PATTERNS[-1][1]
assert all(w // 2 <= ATT_NEAR for w, _ in DILATED_PATTERNS[:-1])
ATT_KB = -(-ATT_NEAR // ATT_BLOCK)
ATT_WINDOW = 2 * ATT_KB + 1
ATT_FAR_GROUP = 4
ATT_NEAR_GROUP = 2
MASKED = -1e30
ROW_MAX_INIT = -1e29
N_CHIPS = 4
VMEM_LIMIT_BYTES = 56 * 1024 * 1024
ELEM_BLOCK_BYTES = 2 * 1024 * 1024

ADAM_LR = 0.001
ADAM_B1 = 0.9
ADAM_B2 = 0.999
ADAM_EPS = 1e-08
ADAM_WD = 0.01
ADAM_STEP = 10


def _params(sem=None):
    return pltpu.CompilerParams(dimension_semantics=sem, vmem_limit_bytes=VMEM_LIMIT_BYTES)


def _sigmoid(x):
    return 0.5 * jnp.tanh(0.5 * x) + 0.5


class _Job:
    def __init__(self, *, ins=(), ios=(), outs=(), sems=(), start, finish):
        self.ins, self.ios, self.outs, self.sems = list(ins), list(ios), list(outs), list(sems)
        self.start, self.finish = start, finish

    def results(self):
        return [jax.ShapeDtypeStruct(a.shape, a.dtype) for a in self.ios] + self.outs


def _call(body, *, name, grid, in_specs, out_specs, out_shape, operands, scratch_shapes=(), semantics=None, jobs=()):
    in_specs, out_specs, out_shape = list(in_specs), list(out_specs), list(out_shape)
    scratch_shapes = list(scratch_shapes)
    if not jobs:
        outs = pl.pallas_call(body, name=name, grid=grid, in_specs=in_specs, out_specs=out_specs, out_shape=out_shape,
                              scratch_shapes=scratch_shapes, compiler_params=_params(semantics))(*operands)
        return outs, []
    n_in, n_out, n_scratch = len(in_specs), len(out_specs), len(scratch_shapes)
    extra_in, extra_out, sems, aliases = [], [], [], {}
    for job in jobs:
        extra_in += job.ins
        for t in range(len(job.ios)):
            aliases[n_in + len(extra_in) + t] = n_out + len(extra_out) + t
        extra_in += job.ios
        extra_out += job.results()
        sems += job.sems

    def carried(*refs):
        x_in = refs[n_in:n_in + len(extra_in)]
        x_out = refs[n_in + len(extra_in) + n_out:n_in + len(extra_in) + n_out + len(extra_out)]
        x_sem = refs[len(refs) - len(sems):]
        views, i_in, i_out, i_sem = [], 0, 0, 0
        for job in jobs:
            data = list(x_in[i_in:i_in + len(job.ins)]) + list(x_out[i_out:i_out + len(job.results())])
            views.append((data, x_sem[i_sem:i_sem + len(job.sems)]))
            i_in += len(job.ins) + len(job.ios)
            i_out += len(job.results())
            i_sem += len(job.sems)
        steps = [pl.program_id(d) for d in range(len(grid))]

        @pl.when(functools.reduce(jnp.logical_and, [s == 0 for s in steps]))
        def _():
            for job, (data, sem) in zip(jobs, views):
                job.start(data, sem)

        body(*refs[:n_in], *refs[n_in + len(extra_in):n_in + len(extra_in) + n_out],
             *refs[len(refs) - len(sems) - n_scratch:len(refs) - len(sems)])

        @pl.when(functools.reduce(jnp.logical_and, [s == g - 1 for s, g in zip(steps, grid)]))
        def _():
            for job, (data, sem) in zip(jobs, views):
                job.finish(data, sem)

    hbm = pl.BlockSpec(memory_space=pl.ANY)
    res = pl.pallas_call(
        carried, name=name, grid=grid, in_specs=in_specs + [hbm] * len(extra_in),
        out_specs=out_specs + [hbm] * len(extra_out), out_shape=out_shape + extra_out,
        input_output_aliases=aliases, scratch_shapes=scratch_shapes + sems,
        compiler_params=_params(("arbitrary",) * len(grid)),
    )(*operands, *extra_in)
    carried_results, at = [], n_out
    for job in jobs:
        carried_results.append(list(res[at:at + len(job.results())]))
        at += len(job.results())
    return list(res[:n_out]), carried_results


def _run_jobs(jobs, *, name):
    first = jobs[0]
    n_in, n_io = len(first.ins), len(first.ios)
    out_shape = first.results()
    n_sems = [len(job.sems) for job in jobs]

    def body(*refs):
        data = list(refs[:n_in]) + list(refs[n_in + n_io:n_in + n_io + len(out_shape)])
        at = n_in + n_io + len(out_shape)
        for job, ns in zip(jobs, n_sems):
            job.start(data, refs[at:at + ns])
            job.finish(data, refs[at:at + ns])
            at += ns

    hbm = pl.BlockSpec(memory_space=pl.ANY)
    return pl.pallas_call(
        body, name=name, in_specs=[hbm] * (n_in + n_io), out_specs=[hbm] * len(out_shape), out_shape=out_shape,
        input_output_aliases={n_in + t: t for t in range(n_io)},
        scratch_shapes=[s for job in jobs for s in job.sems],
    )(*first.ins, *first.ios)


def _dot(a, b, ta=False, tb=False):
    return lax.dot_general(a, b, (((0 if ta else 1,), (1 if tb else 0,)), ((), ())),
                           preferred_element_type=F32)


def _tile(n, want):
    t = min(n, want) // 128 * 128
    while n % t:
        t -= 128
    return t


def _a_spec(ta, tm, tk):
    return pl.BlockSpec((tk, tm), lambda i, j, k: (k, i)) if ta else pl.BlockSpec((tm, tk), lambda i, j, k: (i, k))


def _b_spec(tb, tk, tn):
    return pl.BlockSpec((tn, tk), lambda i, j, k: (j, k)) if tb else pl.BlockSpec((tk, tn), lambda i, j, k: (k, j))


def _accumulate(accs, nk, products, finish):
    if nk == 1:
        finish(*products())
        return
    k = pl.program_id(2)

    @pl.when(k == 0)
    def _():
        for acc, p in zip(accs, products()):
            acc[...] = p

    if nk > 2:
        @pl.when(jnp.logical_and(k > 0, k < nk - 1))
        def _():
            for acc, p in zip(accs, products()):
                acc[...] += p

    @pl.when(k == nk - 1)
    def _():
        finish(*[acc[...] + p for acc, p in zip(accs, products())])


def _matmul(a, b, *, name, ta=False, tb=False, out_dtype=F32, residual=None, tm=1024, tn=1024, tk=2048, jobs=()):
    m, kdim = (a.shape[1], a.shape[0]) if ta else a.shape
    n = b.shape[0] if tb else b.shape[1]
    tm, tn, tk = _tile(m, tm), _tile(n, tn), _tile(kdim, tk)
    nk = kdim // tk

    def body(*refs):
        a_ref, b_ref = refs[:2]
        r_ref = refs[2] if residual is not None else None
        o_ref = refs[-1] if nk == 1 else refs[-2]

        def finish(total):
            if residual is not None:
                total = total + r_ref[...]
            o_ref[...] = total.astype(out_dtype)

        _accumulate(refs[-1:] if nk > 1 else (), nk, lambda: (_dot(a_ref[...], b_ref[...], ta, tb),), finish)

    o_spec = pl.BlockSpec((tm, tn), lambda i, j, k: (i, j))
    in_specs = [_a_spec(ta, tm, tk), _b_spec(tb, tk, tn)]
    operands = [a, b]
    if residual is not None:
        in_specs.append(o_spec)
        operands.append(residual)
    (out,), carried = _call(
        body, name=name, grid=(m // tm, n // tn, nk), in_specs=in_specs, out_specs=[o_spec],
        out_shape=[jax.ShapeDtypeStruct((m, n), out_dtype)], operands=operands,
        scratch_shapes=[pltpu.VMEM((tm, tn), F32)] * (nk > 1),
        semantics=("parallel", "parallel", "arbitrary"), jobs=jobs)
    return (out, carried) if jobs else out


def _matmul_pieces_nt(pieces, b, *, name, tm=512, tn=1024, jobs=()):
    m, kp = pieces[0].shape
    n = b.shape[0]
    tm, tn = _tile(m, tm), _tile(n, tn)
    count = len(pieces)

    def body(*refs):
        b_ref, o_ref = refs[count], refs[count + 1]
        total = _dot(refs[0][...], b_ref[:, pl.ds(0, kp)], tb=True)
        for p in range(1, count):
            total = total + _dot(refs[p][...], b_ref[:, pl.ds(p * kp, kp)], tb=True)
        o_ref[...] = total

    piece = pl.BlockSpec((tm, kp), lambda j, i: (i, 0))
    (out,), carried = _call(
        body, name=name, grid=(n // tn, m // tm),
        in_specs=[piece] * count + [pl.BlockSpec((tn, count * kp), lambda j, i: (j, 0))],
        out_specs=[pl.BlockSpec((tm, tn), lambda j, i: (i, j))],
        out_shape=[jax.ShapeDtypeStruct((m, n), F32)], operands=[*pieces, b],
        semantics=("parallel", "parallel"), jobs=jobs)
    return (out, carried) if jobs else out


def _weight_grad_pieces(a, pieces, *, name):
    tokens, m = a.shape
    np_ = pieces[0].shape[1]
    tm = 1024 if m % 1024 == 0 else _tile(m, 1408)
    tn = _tile(np_, 512)
    nb = np_ // tn
    out = None
    for p, piece in enumerate(pieces):
        def body(*refs):
            refs[-1][...] = _dot(refs[0][...], refs[1][...], ta=True)

        in_specs = [pl.BlockSpec((tokens, tm), lambda i, j: (0, i)), pl.BlockSpec((tokens, tn), lambda i, j: (0, j))]
        operands = [a, piece]
        if out is not None:
            in_specs.append(pl.BlockSpec(memory_space=pl.ANY))
            operands.append(out)
        out = pl.pallas_call(
            body, name="%s_%d" % (name, p), grid=(m // tm, nb), in_specs=in_specs,
            out_specs=pl.BlockSpec((tm, tn), lambda i, j, p=p: (i, p * nb + j)),
            out_shape=jax.ShapeDtypeStruct((m, len(pieces) * np_), F32),
            input_output_aliases={2: 0} if len(operands) == 3 else {},
            compiler_params=_params(("parallel", "parallel")),
        )(*operands)
    return out


def _weight_grad(a, g, *, name, jobs=()):
    tokens, m = a.shape
    tm = 1024 if m % 1024 == 0 else _tile(m, 1408)
    return _matmul(a, g, name=name, ta=True, tm=tm, tn=512, tk=tokens, jobs=jobs)


def _swiglu_fwd(n2, w_gate, w_up, *, tm=1024, tn=512, tk=2048, jobs=()):
    m, kdim = n2.shape
    n = w_gate.shape[1]
    tm, tn, tk = _tile(m, tm), _tile(n, tn), _tile(kdim, tk)
    nk = kdim // tk

    def body(a_ref, g_ref, u_ref, gate_ref, up_ref, act_ref, *acc):
        def products():
            a = a_ref[...]
            return _dot(a, g_ref[...]), _dot(a, u_ref[...])

        def finish(g, u):
            gate_ref[...] = g.astype(BF16)
            up_ref[...] = u.astype(BF16)
            act_ref[...] = (g * _sigmoid(g) * u).astype(BF16)

        _accumulate(acc, nk, products, finish)

    o_spec = pl.BlockSpec((tm, tn), lambda i, j, k: (i, j))
    o_shape = jax.ShapeDtypeStruct((m, n), BF16)
    return _call(
        body, name="swiglu_fwd", grid=(m // tm, n // tn, nk),
        in_specs=[_a_spec(False, tm, tk), _b_spec(False, tk, tn), _b_spec(False, tk, tn)],
        out_specs=[o_spec] * 3, out_shape=[o_shape] * 3, operands=[n2, w_gate, w_up],
        scratch_shapes=[pltpu.VMEM((tm, tn), F32)] * (2 * (nk > 1)),
        semantics=("parallel", "parallel", "arbitrary"), jobs=jobs)


def _swiglu_bwd_act(dh2, w_down, gate, up, *, tm=1024, tn=512, tk=2048):
    m, kdim = dh2.shape
    n = w_down.shape[0]
    tm, tn, tk = _tile(m, tm), _tile(n, tn), _tile(kdim, tk)
    nk = kdim // tk

    sub = _tile(tn, 256)

    def body(a_ref, b_ref, gate_ref, up_ref, dgate_ref, dup_ref, *acc):
        def finish(dact, cols=slice(None)):
            g = gate_ref[:, cols].astype(F32)
            u = up_ref[:, cols].astype(F32)
            sg = _sigmoid(g)
            dup_ref[:, cols] = (dact * g * sg).astype(BF16)
            dgate_ref[:, cols] = (dact * u * sg * (1.0 + g * (1.0 - sg))).astype(BF16)

        if nk == 1:
            a = a_ref[...]
            for c in range(tn // sub):
                cols = pl.ds(c * sub, sub)
                finish(_dot(a, b_ref[cols, :], tb=True), cols)
        else:
            _accumulate(acc, nk, lambda: (_dot(a_ref[...], b_ref[...], tb=True),), finish)

    o_spec = pl.BlockSpec((tm, tn), lambda i, j, k: (i, j))
    o_shape = jax.ShapeDtypeStruct((m, n), BF16)
    return pl.pallas_call(
        body, name="swiglu_bwd_act", grid=(m // tm, n // tn, nk),
        in_specs=[_a_spec(False, tm, tk), _b_spec(True, tk, tn), o_spec, o_spec],
        out_specs=[o_spec] * 2, out_shape=[o_shape] * 2,
        scratch_shapes=[pltpu.VMEM((tm, tn), F32)] * (nk > 1),
        compiler_params=_params(("parallel", "parallel", "arbitrary")),
    )(dh2, w_down, gate, up)


def _swiglu_bwd_in(dgate, dup, w_gate, w_up, *, tm=1024, tn=1024, tk=1408, jobs=()):
    m, kdim = dgate.shape
    n = w_gate.shape[0]
    tm, tn, tk = _tile(m, tm), _tile(n, tn), _tile(kdim, tk)
    nk = kdim // tk

    def body(a1_ref, a2_ref, b1_ref, b2_ref, o_ref, *acc):
        def product():
            return (_dot(a1_ref[...], b1_ref[...], tb=True) + _dot(a2_ref[...], b2_ref[...], tb=True),)

        def finish(total):
            o_ref[...] = total

        _accumulate(acc, nk, product, finish)

    a_spec, b_spec = _a_spec(False, tm, tk), _b_spec(True, tk, tn)
    (out,), carried = _call(
        body, name="swiglu_bwd_in", grid=(m // tm, n // tn, nk),
        in_specs=[a_spec, a_spec, b_spec, b_spec],
        out_specs=[pl.BlockSpec((tm, tn), lambda i, j, k: (i, j))],
        out_shape=[jax.ShapeDtypeStruct((m, n), F32)], operands=[dgate, dup, w_gate, w_up],
        scratch_shapes=[pltpu.VMEM((tm, tn), F32)] * (nk > 1),
        semantics=("parallel", "parallel", "arbitrary"), jobs=jobs)
    return out, carried


def _row_block(rows, cols):
    tr = min(rows, max(16, ELEM_BLOCK_BYTES // (4 * cols) // 16 * 16))
    while rows % tr:
        tr -= 16
    return tr


def _rmsnorm_fwd(x, g, *, name):
    s, d = x.shape
    tr = _row_block(s, d)

    def body(x_ref, g_ref, n_ref):
        xv = x_ref[...]
        r = lax.rsqrt(jnp.mean(xv * xv, axis=-1, keepdims=True) + EPS)
        n_ref[...] = (xv * r * g_ref[...]).astype(BF16)

    row = pl.BlockSpec((tr, d), lambda i: (i, 0))
    return pl.pallas_call(
        body, name=name, grid=(s // tr,), in_specs=[row, pl.BlockSpec((1, d), lambda i: (0, 0))],
        out_specs=row, out_shape=jax.ShapeDtypeStruct((s, d), BF16),
        compiler_params=_params(("parallel",)),
    )(x, g)


def _rmsnorm_bwd_rows(xv, gv, dy):
    r = lax.rsqrt(jnp.mean(xv * xv, axis=-1, keepdims=True) + EPS)
    xhat = xv * r
    dxh = dy * gv
    dx = r * (dxh - xhat * jnp.mean(dxh * xhat, axis=-1, keepdims=True))
    return dx, dy * xhat


def _rmsnorm_bwd(dn, x, g, skip, *, name):
    s, d = x.shape
    tr = _row_block(s, d)

    def body(dn_ref, x_ref, g_ref, skip_ref, dx_ref, dxb_ref, dg_ref):
        dx, dgr = _rmsnorm_bwd_rows(x_ref[...], g_ref[...], dn_ref[...])
        dx = dx + skip_ref[...]
        dx_ref[...] = dx
        dxb_ref[...] = dx.astype(BF16)

        @pl.when(pl.program_id(0) == 0)
        def _():
            dg_ref[...] = jnp.zeros_like(dg_ref)

        dg_ref[...] += jnp.sum(dgr, axis=0, keepdims=True)

    row = pl.BlockSpec((tr, d), lambda i: (i, 0))
    vec = pl.BlockSpec((1, d), lambda i: (0, 0))
    return pl.pallas_call(
        body, name=name, grid=(s // tr,), in_specs=[row, row, vec, row],
        out_specs=[row, row, vec],
        out_shape=[jax.ShapeDtypeStruct((s, d), F32), jax.ShapeDtypeStruct((s, d), BF16),
                   jax.ShapeDtypeStruct((1, d), F32)],
        compiler_params=_params(("arbitrary",)),
    )(dn, x, g, skip)


def _loss_head(h2, g, target):
    s, d = h2.shape
    tr = _row_block(s, d)

    def body(h_ref, g_ref, t_ref, dh_ref, dhb_ref, dg_ref, loss_ref):
        hv = h_ref[...]
        gv = g_ref[...]
        r = lax.rsqrt(jnp.mean(hv * hv, axis=-1, keepdims=True) + EPS)
        err = hv * r * gv - t_ref[...]
        dx, dgr = _rmsnorm_bwd_rows(hv, gv, err * (1.0 / d))
        dh_ref[...] = dx
        dhb_ref[...] = dx.astype(BF16)

        @pl.when(pl.program_id(0) == 0)
        def _():
            dg_ref[...] = jnp.zeros_like(dg_ref)
            loss_ref[...] = jnp.zeros_like(loss_ref)

        dg_ref[...] += jnp.sum(dgr, axis=0, keepdims=True)
        row_loss = jnp.mean(err * err, axis=-1, keepdims=True)
        loss_ref[...] += 0.5 * jnp.sum(row_loss, axis=0, keepdims=True)

    row = pl.BlockSpec((tr, d), lambda i: (i, 0))
    vec = pl.BlockSpec((1, d), lambda i: (0, 0))
    one = pl.BlockSpec((1, 1), lambda i: (0, 0))
    return pl.pallas_call(
        body, name="loss_head", grid=(s // tr,), in_specs=[row, vec, row],
        out_specs=[row, row, vec, one],
        out_shape=[jax.ShapeDtypeStruct((s, d), F32), jax.ShapeDtypeStruct((s, d), BF16),
                   jax.ShapeDtypeStruct((1, d), F32), jax.ShapeDtypeStruct((1, 1), F32)],
        compiler_params=_params(("arbitrary",)),
    )(h2, g, target)


def _attention_bias_tables():
    k = np.arange(-ATT_KB, ATT_KB + 1)[:, None, None]
    delta = k * ATT_BLOCK + np.arange(ATT_BLOCK)[None, None, :] - np.arange(ATT_BLOCK)[None, :, None]
    dist = np.abs(delta)
    count = np.zeros(delta.shape, np.int32)
    for window, dilation in DILATED_PATTERNS:
        count += (delta % dilation == 0) & (dist <= min(window // 2, ATT_NEAR))
    logc = np.where(count > 0, np.log(np.maximum(count, 1)), MASKED)
    return dist.astype(np.float32), logc.astype(np.float32)


def _far_bias_tables(per_class):
    steps = np.abs(np.arange(per_class)[:, None] - np.arange(per_class)[None, :]) * ATT_CLASSES
    valid = (steps > ATT_NEAR) & (steps <= ATT_REACH)
    return steps.astype(np.float32), np.where(valid, 0.0, MASKED).astype(np.float32)


def _to_classes(x):
    s, cols = x.shape
    return jnp.reshape(jnp.transpose(jnp.reshape(x, (s // ATT_CLASSES, ATT_CLASSES, cols)), (1, 0, 2)), (s, cols))


def _from_classes(x):
    s, cols = x.shape
    return jnp.reshape(jnp.transpose(jnp.reshape(x, (ATT_CLASSES, s // ATT_CLASSES, cols)), (1, 0, 2)), (s, cols))


def _head_bias(bias_ref, slope, dist_ref, logc_ref):
    for kk in range(ATT_WINDOW):
        bias_ref[kk] = logc_ref[kk] - slope * dist_ref[kk]
    bias_ref[ATT_WINDOW] = jnp.full((ATT_BLOCK, ATT_BLOCK), MASKED, F32)


def _window_start(i, nq, nwin):
    return jnp.clip(i - ATT_KB, 0, nq - nwin)


def _window_block(j, i):
    rows = pl.ds(pl.multiple_of(j * ATT_BLOCK, ATT_BLOCK), ATT_BLOCK)
    kk = j - i + ATT_KB
    return rows, jnp.where(jnp.logical_and(kk >= 0, kk < ATT_WINDOW), kk, ATT_WINDOW)


def _attention_far_fwd(qkv, slopes, n_heads):
    s = qkv.shape[0]
    per_class = s // ATT_CLASSES
    scale = HEAD_DIM ** -0.5
    dist, logc = _far_bias_tables(per_class)

    def body(slope_ref, q_ref, k_ref, v_ref, dist_ref, logc_ref, o_ref, lse_ref):
        bias = logc_ref[...] - slope_ref[pl.program_id(0)] * dist_ref[...]
        for a in range(ATT_FAR_GROUP):
            rows = pl.ds(a * per_class, per_class)
            sc = _dot(q_ref[rows, :], k_ref[rows, :], tb=True) * scale + bias
            m = jnp.maximum(jnp.max(sc, axis=-1, keepdims=True), ROW_MAX_INIT)
            p = jnp.exp(sc - m)
            l = jnp.maximum(jnp.sum(p, axis=-1, keepdims=True), 1e-30)
            o_ref[rows, :] = (_dot(p.astype(BF16), v_ref[rows, :]) / l).astype(BF16)
            lse_ref[rows, :] = jnp.broadcast_to(m + jnp.log(l), (per_class, HEAD_DIM))

    hh = n_heads
    blk = pl.BlockSpec((ATT_FAR_GROUP * per_class, HEAD_DIM), lambda h, r: (r, h))
    table = pl.BlockSpec(dist.shape, lambda h, r: (0, 0))
    return pl.pallas_call(
        body, name="attention_far_fwd", grid=(hh, ATT_CLASSES // ATT_FAR_GROUP),
        in_specs=[pl.BlockSpec(memory_space=pltpu.SMEM), blk,
                  pl.BlockSpec((ATT_FAR_GROUP * per_class, HEAD_DIM), lambda h, r: (r, hh + h)),
                  pl.BlockSpec((ATT_FAR_GROUP * per_class, HEAD_DIM), lambda h, r: (r, 2 * hh + h)), table, table],
        out_specs=[blk, blk],
        out_shape=[jax.ShapeDtypeStruct((s, hh * HEAD_DIM), BF16), jax.ShapeDtypeStruct((s, hh * HEAD_DIM), F32)],
        compiler_params=_params(("parallel", "parallel")),
    )(slopes, qkv, qkv, qkv, jnp.asarray(dist), jnp.asarray(logc))


def _attention_fwd(proj, slopes, far_out, far_lse, n_heads, jobs=()):
    s = proj.shape[0]
    nq = s // ATT_BLOCK
    scale = HEAD_DIM ** -0.5
    dist, logc = _attention_bias_tables()

    nwin = min(ATT_WINDOW, nq)

    group = math.gcd(ATT_NEAR_GROUP, nq)

    def body(slope_ref, q_ref, k_ref, v_ref, fo_ref, fl_ref, dist_ref, logc_ref, o_ref, lse_ref, bias_ref, s_ref):
        h, step = pl.program_id(0), pl.program_id(1)

        @pl.when(step == 0)
        def _():
            _head_bias(bias_ref, slope_ref[h], dist_ref, logc_ref)

        for a in range(group):
            i = step * group + a
            mine = pl.ds(a * ATT_BLOCK, ATT_BLOCK)
            q = q_ref[mine, :]
            first = _window_start(i, nq, nwin)
            m = jnp.full((ATT_BLOCK, 1), ROW_MAX_INIT, F32)
            for b in range(nwin):
                rows, kk = _window_block(first + b, i)
                sc = _dot(q, k_ref[rows, :], tb=True) * scale + bias_ref[kk]
                s_ref[a * nwin + b] = sc
                m = jnp.maximum(m, jnp.max(sc, axis=-1, keepdims=True))
            l = jnp.zeros((ATT_BLOCK, 1), F32)
            acc = jnp.zeros((ATT_BLOCK, HEAD_DIM), F32)
            for b in range(nwin):
                rows, _ = _window_block(first + b, i)
                p = jnp.exp(s_ref[a * nwin + b] - m)
                l = l + jnp.sum(p, axis=-1, keepdims=True)
                acc = acc + _dot(p.astype(BF16), v_ref[rows, :])
            near_lse = m + jnp.log(l)
            far_lse_col = fl_ref[mine, :1]
            lse = jnp.maximum(near_lse, far_lse_col)
            lse = lse + jnp.log(jnp.exp(near_lse - lse) + jnp.exp(far_lse_col - lse))
            o_ref[mine, :] = (acc * (jnp.exp(near_lse - lse) / l)
                              + fo_ref[mine, :].astype(F32) * jnp.exp(far_lse_col - lse)).astype(BF16)
            lse_ref[mine, :] = jnp.broadcast_to(lse, (ATT_BLOCK, HEAD_DIM))

    hh = n_heads
    blk = pl.BlockSpec((group * ATT_BLOCK, HEAD_DIM), lambda h, i: (i, h))
    table = pl.BlockSpec(dist.shape, lambda h, i: (0, 0, 0))
    return _call(
        body, name="attention_fwd", grid=(hh, nq // group),
        in_specs=[pl.BlockSpec(memory_space=pltpu.SMEM), blk,
                  pl.BlockSpec((s, HEAD_DIM), lambda h, i: (0, hh + h)),
                  pl.BlockSpec((s, HEAD_DIM), lambda h, i: (0, 2 * hh + h)), blk, blk, table, table],
        out_specs=[blk, blk],
        out_shape=[jax.ShapeDtypeStruct((s, hh * HEAD_DIM), BF16), jax.ShapeDtypeStruct((s, hh * HEAD_DIM), F32)],
        operands=[slopes, proj, proj, proj, far_out, far_lse, jnp.asarray(dist), jnp.asarray(logc)],
        scratch_shapes=[pltpu.VMEM((ATT_WINDOW + 1, ATT_BLOCK, ATT_BLOCK), F32),
                        pltpu.VMEM((group * nwin, ATT_BLOCK, ATT_BLOCK), F32)],
        semantics=("parallel", "arbitrary"), jobs=jobs)


def _attention_far_bwd(qkv, slopes, out, dout, lse, n_heads):
    s = qkv.shape[0]
    per_class = s // ATT_CLASSES
    scale = HEAD_DIM ** -0.5
    dist, logc = _far_bias_tables(per_class)

    def body(slope_ref, q_ref, k_ref, v_ref, o_ref, do_ref, lse_ref, dist_ref, logc_ref, dq_ref, dk_ref, dv_ref):
        bias = logc_ref[...] - slope_ref[pl.program_id(0)] * dist_ref[...]
        for a in range(ATT_FAR_GROUP):
            rows = pl.ds(a * per_class, per_class)
            q, k, do = q_ref[rows, :], k_ref[rows, :], do_ref[rows, :]
            delta = jnp.sum(do.astype(F32) * o_ref[rows, :].astype(F32), axis=-1, keepdims=True)
            p = jnp.exp(_dot(q, k, tb=True) * scale + bias - lse_ref[rows, :1])
            dv_ref[rows, :] = _dot(p.astype(BF16), do, ta=True).astype(BF16)
            ds = (p * (_dot(do, v_ref[rows, :], tb=True) - delta) * scale).astype(BF16)
            dk_ref[rows, :] = _dot(ds, q, ta=True).astype(BF16)
            dq_ref[rows, :] = _dot(ds, k).astype(BF16)

    hh = n_heads
    blk = pl.BlockSpec((ATT_FAR_GROUP * per_class, HEAD_DIM), lambda h, r: (r, h))
    table = pl.BlockSpec(dist.shape, lambda h, r: (0, 0))
    o_shape = jax.ShapeDtypeStruct((s, hh * HEAD_DIM), BF16)
    return pl.pallas_call(
        body, name="attention_far_bwd", grid=(hh, ATT_CLASSES // ATT_FAR_GROUP),
        in_specs=[pl.BlockSpec(memory_space=pltpu.SMEM), blk,
                  pl.BlockSpec((ATT_FAR_GROUP * per_class, HEAD_DIM), lambda h, r: (r, hh + h)),
                  pl.BlockSpec((ATT_FAR_GROUP * per_class, HEAD_DIM), lambda h, r: (r, 2 * hh + h)),
                  blk, blk, blk, table, table],
        out_specs=[blk] * 3, out_shape=[o_shape] * 3,
        compiler_params=_params(("parallel", "parallel")),
    )(slopes, qkv, qkv, qkv, out, dout, lse, jnp.asarray(dist), jnp.asarray(logc))


def _attention_bwd(proj, slopes, out, lse, dmixed, far_grads, n_heads, jobs=()):
    s = proj.shape[0]
    nq = s // ATT_BLOCK
    scale = HEAD_DIM ** -0.5
    dist, logc = _attention_bias_tables()

    nwin = min(ATT_WINDOW, nq)
    group = math.gcd(ATT_NEAR_GROUP, nq)

    def body(slope_ref, q_ref, k_ref, v_ref, o_ref, do_ref, lse_ref, fdq_ref, fdk_ref, fdv_ref, dist_ref, logc_ref,
             dq_ref, dk_ref, dv_ref, dk_acc, dv_acc, bias_ref):
        h, step = pl.program_id(0), pl.program_id(1)

        @pl.when(step == 0)
        def _():
            dk_acc[...] = jnp.zeros_like(dk_acc)
            dv_acc[...] = jnp.zeros_like(dv_acc)
            _head_bias(bias_ref, slope_ref[h], dist_ref, logc_ref)

        for a in range(group):
            i = step * group + a
            mine = pl.ds(a * ATT_BLOCK, ATT_BLOCK)
            q = q_ref[mine, :]
            do = do_ref[mine, :]
            lse_col = lse_ref[mine, :1]
            delta = jnp.sum(do.astype(F32) * o_ref[mine, :].astype(F32), axis=-1, keepdims=True)
            first = _window_start(i, nq, nwin)
            dq = jnp.zeros((ATT_BLOCK, HEAD_DIM), F32)
            for b in range(nwin):
                rows, kk = _window_block(first + b, i)
                kj = k_ref[rows, :]
                vj = v_ref[rows, :]
                p = jnp.exp(_dot(q, kj, tb=True) * scale + bias_ref[kk] - lse_col)
                dv_acc[rows, :] += _dot(p.astype(BF16), do, ta=True)
                dp = _dot(do, vj, tb=True)
                ds = (p * (dp - delta) * scale).astype(BF16)
                dk_acc[rows, :] += _dot(ds, q, ta=True)
                dq = dq + _dot(ds, kj)
            dq_ref[mine, :] = (dq + fdq_ref[mine, :].astype(F32)).astype(BF16)

        @pl.when(step == nq // group - 1)
        def _():
            dk_ref[...] = (dk_acc[...] + fdk_ref[...].astype(F32)).astype(BF16)
            dv_ref[...] = (dv_acc[...] + fdv_ref[...].astype(F32)).astype(BF16)

    hh = n_heads
    blk = pl.BlockSpec((group * ATT_BLOCK, HEAD_DIM), lambda h, i: (i, h))
    col = pl.BlockSpec((s, HEAD_DIM), lambda h, i: (0, h))
    table = pl.BlockSpec(dist.shape, lambda h, i: (0, 0, 0))
    o_shape = jax.ShapeDtypeStruct((s, hh * HEAD_DIM), BF16)
    return _call(
        body, name="attention_bwd", grid=(hh, nq // group),
        in_specs=[pl.BlockSpec(memory_space=pltpu.SMEM), blk,
                  pl.BlockSpec((s, HEAD_DIM), lambda h, i: (0, hh + h)),
                  pl.BlockSpec((s, HEAD_DIM), lambda h, i: (0, 2 * hh + h)),
                  blk, blk, blk, blk, col, col, table, table],
        out_specs=[blk, col, col], out_shape=[o_shape] * 3,
        operands=[slopes, proj, proj, proj, out, dmixed, lse, *far_grads, jnp.asarray(dist), jnp.asarray(logc)],
        scratch_shapes=[pltpu.VMEM((s, HEAD_DIM), F32)] * 2
        + [pltpu.VMEM((ATT_WINDOW + 1, ATT_BLOCK, ATT_BLOCK), F32)],
        semantics=("parallel", "arbitrary"), jobs=jobs)


def _ret_decays(lgc, lga, strict_c, strict_a):
    c = RET_CHUNK
    rel = (lax.broadcasted_iota(jnp.int32, (c, c), 0) - lax.broadcasted_iota(jnp.int32, (c, c), 1)).astype(F32)
    in_c = (rel > 0) if strict_c else (rel >= 0)
    in_a = (rel < 0) if strict_a else (rel <= 0)
    mask = (jnp.where(in_c, jnp.exp(lgc * jnp.maximum(rel, 0.0)), 0.0)
            + jnp.where(in_a, jnp.exp(lga * jnp.maximum(-rel, 0.0)), 0.0))
    idx = lax.broadcasted_iota(jnp.int32, (c, 1), 0).astype(F32)
    ones = jnp.ones((1, HEAD_DIM), F32)
    dec = dict(
        rel=rel, mask=mask, idx=idx,
        a_c=jnp.exp(lgc * (idx + 1.0)), b_c=jnp.exp(lgc * (c - 1.0 - idx)), chunk_c=jnp.exp(ones * (lgc * c)),
        a_a=jnp.exp(lga * (c - idx)), b_a=jnp.exp(lga * idx), chunk_a=jnp.exp(ones * (lga * c)),
    )
    return dec


def _scaled(x, col):
    return (x.astype(F32) * col).astype(BF16)


def _chunk_rows(i):
    return pl.ds(pl.multiple_of(i * RET_CHUNK, RET_CHUNK), RET_CHUNK)


def _chunk_loop(nc, step, init, unroll=RET_UNROLL):
    group = math.gcd(nc, unroll)

    def trip(t, carry):
        for u in range(group):
            carry = step(t * group + u, carry)
        return carry

    return lax.fori_loop(0, nc // group, trip, init)


def _retention(a, b, c, lg_c, lg_a, *, strict_c, strict_a, scale, n_heads, name, gate=None, norm_w=None, jobs=()):
    s = a[0].shape[0]
    nc = s // RET_CHUNK
    epilogue = gate is not None

    def body(*refs):
        lgc_ref, lga_ref, a_ref, b_ref, c_ref = refs[:5]
        if epilogue:
            g_ref, w_ref, o_ref, mix_ref, sa_ref = refs[5:]
        else:
            o_ref, sa_ref = refs[5:]
        h = pl.program_id(0)
        dec = _ret_decays(lgc_ref[h], lga_ref[h], strict_c, strict_a)

        def reverse(t, state):
            i = nc - 1 - t
            sa_ref[i] = state.astype(BF16)
            rows = _chunk_rows(i)
            return state * dec["chunk_a"] + _dot(_scaled(b_ref[rows, :], dec["b_a"]), c_ref[rows, :], ta=True)

        _chunk_loop(nc, reverse, jnp.zeros((HEAD_DIM, HEAD_DIM), F32))

        def forward(i, state):
            rows = _chunk_rows(i)
            ai, bi, ci = a_ref[rows, :], b_ref[rows, :], c_ref[rows, :]
            inner = (_dot(ai, bi, tb=True) * dec["mask"]).astype(BF16)
            out = (_dot(inner, ci) + _dot(_scaled(ai, dec["a_c"]), state.astype(BF16))
                   + _dot(_scaled(ai, dec["a_a"]), sa_ref[i])) * scale
            o_ref[rows, :] = out.astype(BF16)
            if epilogue:
                r = lax.rsqrt(jnp.mean(out * out, axis=-1, keepdims=True) + EPS)
                g = g_ref[rows, :].astype(F32)
                mix_ref[rows, :] = (out * r * w_ref[...] * (g * _sigmoid(g))).astype(BF16)
            return state * dec["chunk_c"] + _dot(_scaled(bi, dec["b_c"]), ci, ta=True)

        _chunk_loop(nc, forward, jnp.zeros((HEAD_DIM, HEAD_DIM), F32))

    def col(first):
        return pl.BlockSpec((s, HEAD_DIM), lambda h: (0, first + h))

    smem = pl.BlockSpec(memory_space=pltpu.SMEM)
    in_specs = [smem, smem, col(a[1]), col(b[1]), col(c[1])]
    operands = [lg_c, lg_a, a[0], b[0], c[0]]
    o_shape = jax.ShapeDtypeStruct((s, n_heads * HEAD_DIM), BF16)
    out_specs, out_shape = [col(0)], [o_shape]
    if epilogue:
        in_specs += [col(gate[1]), pl.BlockSpec((1, HEAD_DIM), lambda h: (0, h))]
        operands += [gate[0], norm_w]
        out_specs, out_shape = [col(0)] * 2, [o_shape] * 2
    res, carried = _call(
        body, name=name, grid=(n_heads,), in_specs=in_specs, out_specs=out_specs, out_shape=out_shape,
        operands=operands, scratch_shapes=[pltpu.VMEM((nc, HEAD_DIM, HEAD_DIM), BF16)],
        semantics=("parallel",), jobs=jobs)
    res = res if epilogue else res[0]
    return (res, carried) if jobs else res


def _retention_decay_grads(a, b, c, e, lg_c, lg_a, *, scale, n_heads):
    s = a[0].shape[0]
    nc = s // RET_CHUNK
    cf = float(RET_CHUNK)

    def body(lgc_ref, lga_ref, a_ref, b_ref, c_ref, e_ref, gc_ref, ga_ref, sa_ref, ta_ref):
        h = pl.program_id(0)
        lgc, lga = lgc_ref[h], lga_ref[h]
        dec = _ret_decays(lgc, lga, True, True)
        rel, idx = dec["rel"], dec["idx"]
        w_c = jnp.where(rel > 0, rel * jnp.exp(lgc * jnp.maximum(rel, 0.0)), 0.0)
        w_a = jnp.where(rel < 0, -rel * jnp.exp(lga * jnp.maximum(-rel, 0.0)), 0.0)
        zero = jnp.zeros((HEAD_DIM, HEAD_DIM), F32)

        def reverse(t, carry):
            st, dst = carry
            i = nc - 1 - t
            sa_ref[i] = st.astype(BF16)
            ta_ref[i] = dst.astype(BF16)
            rows = _chunk_rows(i)
            bi, ci = b_ref[rows, :], c_ref[rows, :]
            st_new = st * dec["chunk_a"] + _dot(_scaled(bi, dec["b_a"]), ci, ta=True)
            dst_new = (cf * st + dst) * dec["chunk_a"] + _dot(_scaled(bi, idx * dec["b_a"]), ci, ta=True)
            return st_new, dst_new

        _chunk_loop(nc, reverse, (zero, zero))

        def forward(i, carry):
            st, dst, acc_c, acc_a = carry
            rows = _chunk_rows(i)
            ai, bi, ci = a_ref[rows, :], b_ref[rows, :], c_ref[rows, :]
            ev = e_ref[rows, :].astype(F32)
            pg = _dot(ai, bi, tb=True) * _dot(e_ref[rows, :], ci, tb=True)
            a_c, a_a = _scaled(ai, dec["a_c"]), _scaled(ai, dec["a_a"])
            inter_c = _dot(a_c, st.astype(BF16)) * (idx + 1.0) + _dot(a_c, dst.astype(BF16))
            inter_a = _dot(a_a, sa_ref[i]) * (cf - idx) + _dot(a_a, ta_ref[i])
            acc_c = acc_c + jnp.sum(pg * w_c, axis=0, keepdims=True) + jnp.sum(inter_c * ev, axis=0, keepdims=True)
            acc_a = acc_a + jnp.sum(pg * w_a, axis=0, keepdims=True) + jnp.sum(inter_a * ev, axis=0, keepdims=True)
            st_new = st * dec["chunk_c"] + _dot(_scaled(bi, dec["b_c"]), ci, ta=True)
            dst_new = ((cf * st + dst) * dec["chunk_c"]
                       + _dot(_scaled(bi, (cf - 1.0 - idx) * dec["b_c"]), ci, ta=True))
            return st_new, dst_new, acc_c, acc_a

        row = jnp.zeros((1, HEAD_DIM), F32)
        _, _, acc_c, acc_a = _chunk_loop(nc, forward, (zero, zero, row, row))
        gc_ref[...] = jnp.broadcast_to(jnp.sum(acc_c, axis=-1, keepdims=True) * scale, gc_ref.shape)
        ga_ref[...] = jnp.broadcast_to(jnp.sum(acc_a, axis=-1, keepdims=True) * scale, ga_ref.shape)

    def col(first):
        return pl.BlockSpec((s, HEAD_DIM), lambda h: (0, first + h))

    smem = pl.BlockSpec(memory_space=pltpu.SMEM)
    o_spec = pl.BlockSpec((1, 8, HEAD_DIM), lambda h: (h, 0, 0))
    o_shape = jax.ShapeDtypeStruct((n_heads, 8, HEAD_DIM), F32)
    gc, ga = pl.pallas_call(
        body, name="retention_decay_grads", grid=(n_heads,),
        in_specs=[smem, smem, col(a[1]), col(b[1]), col(c[1]), col(e[1])],
        out_specs=[o_spec] * 2, out_shape=[o_shape] * 2,
        scratch_shapes=[pltpu.VMEM((nc, HEAD_DIM, HEAD_DIM), BF16)] * 2,
        compiler_params=_params(("parallel",)),
    )(lg_c, lg_a, a[0], b[0], c[0], e[0])
    return gc[:, 0, 0], ga[:, 0, 0]


def _ret_gate_bwd(dmixed, first_col, out, proj, gate_col, norm_w, n_heads):
    s = out.shape[0]
    tr = _row_block(s, 8 * HEAD_DIM)

    def body(dm_ref, o_ref, g_ref, w_ref, do_ref, dg_ref, dw_ref):
        dm = dm_ref[...].astype(F32)
        ov = o_ref[...].astype(F32)
        g = g_ref[...].astype(F32)
        w = w_ref[...]
        r = lax.rsqrt(jnp.mean(ov * ov, axis=-1, keepdims=True) + EPS)
        ohat = ov * r
        sg = _sigmoid(g)
        silu = g * sg
        dg_ref[...] = (dm * ohat * w * sg * (1.0 + g * (1.0 - sg))).astype(BF16)
        dohat = dm * w * silu
        do_ref[...] = (r * (dohat - ohat * jnp.mean(dohat * ohat, axis=-1, keepdims=True))).astype(BF16)

        @pl.when(pl.program_id(1) == 0)
        def _():
            dw_ref[...] = jnp.zeros_like(dw_ref)

        dw_ref[...] += jnp.sum(dm * ohat * silu, axis=0, keepdims=True)

    def blk(first):
        return pl.BlockSpec((tr, HEAD_DIM), lambda h, i: (i, first + h))

    vec = pl.BlockSpec((1, HEAD_DIM), lambda h, i: (0, h))
    o_shape = jax.ShapeDtypeStruct((s, n_heads * HEAD_DIM), BF16)
    return pl.pallas_call(
        body, name="ret_gate_bwd", grid=(n_heads, s // tr),
        in_specs=[blk(first_col), blk(0), blk(gate_col), vec],
        out_specs=[blk(0), blk(0), vec],
        out_shape=[o_shape, o_shape, jax.ShapeDtypeStruct((1, n_heads * HEAD_DIM), F32)],
        compiler_params=_params(("parallel", "arbitrary")),
    )(dmixed, out, proj, norm_w)


def _step(x, target, norm_mix_w, ret_decay_fwd, ret_decay_bwd, ret_norm_w, norm_ffn_w, norm_final_w, own, pos):
    d = x.shape[1]
    nh = d // (2 * HEAD_DIM)
    scale = HEAD_DIM ** -0.5
    slopes = jnp.exp2(-8.0 * jnp.arange(1, nh + 1, dtype=F32) / nh)
    lg_f = -jnp.exp(ret_decay_fwd)
    lg_b = -jnp.exp(ret_decay_bwd)
    q_r, k_r, v_r, g_r = 3 * nh, 4 * nh, 5 * nh, 6 * nh
    ax = BIG_AXIS

    def gather(names, arrays, stage):
        return _gather_job(arrays, [ax[k] for k in names], stage)

    def add_halves(k, g, received):
        return _add_halves(g, received, ax[k], pos, name="grad_add_halves_" + k)

    def sum_parts(k, g, received, parts):
        return _sum_chip_parts(g, received, parts, ax[k], pos, name="grad_sum_parts_" + k)

    (w_in,) = _run_jobs([gather(["w_in"], [own["w_in"]], "ici"), gather(["w_in"], [own["w_in"]], "d2d")],
                        name="all_gather_w_in")
    n1 = _rmsnorm_fwd(x, norm_mix_w, name="norm_mix_fwd")
    proj, [[w_gate]] = _matmul(n1, w_in, name="in_proj", out_dtype=BF16,
                               jobs=[gather(["w_gate"], [own["w_gate"]], "ici")])
    qkv_classes = _to_classes(proj[:, :3 * nh * HEAD_DIM])
    (ret, ret_mixed), [[w_gate], [w_out]] = _retention(
        (proj, q_r), (proj, k_r), (proj, v_r), lg_f, lg_b, strict_c=False, strict_a=True, scale=scale, n_heads=nh,
        name="retention_fwd", gate=(proj, g_r), norm_w=ret_norm_w,
        jobs=[gather(["w_gate"], [w_gate], "d2d"), gather(["w_out"], [own["w_out"]], "ici")])
    far_out, far_lse = _attention_far_fwd(qkv_classes, slopes, nh)
    (attn, lse), [[w_out], [w_up]] = _attention_fwd(
        proj, slopes, _from_classes(far_out), _from_classes(far_lse), nh,
        jobs=[gather(["w_out"], [w_out], "d2d"), gather(["w_up"], [own["w_up"]], "ici")])
    mixed = jnp.concatenate([attn, ret_mixed], axis=1)
    h1, [[w_up]] = _matmul(mixed, w_out, name="out_proj", residual=x, jobs=[gather(["w_up"], [w_up], "d2d")])
    n2 = _rmsnorm_fwd(h1, norm_ffn_w, name="norm_ffn_fwd")
    (gate, up, act), [[w_down]] = _swiglu_fwd(n2, w_gate, w_up, jobs=[gather(["w_down"], [own["w_down"]], "ici")])
    (w_down,) = _run_jobs([gather(["w_down"], [w_down], "d2d")], name="all_gather_w_down_sibling")
    h2 = _matmul(act, w_down, name="down_proj", residual=h1, tk=2816)
    dh2, dh2_b, d_norm_final, loss = _loss_head(h2, norm_final_w, target)

    dgate, dup = _swiglu_bwd_act(dh2_b, w_down, gate, up)
    g_down = _weight_grad(act, dh2_b, name="grad_w_down")
    g_gate, [[r_down]] = _weight_grad(n2, dgate, name="grad_w_gate", jobs=[_exchange_job([g_down], [ax["w_down"]])])
    s_down = add_halves("w_down", g_down, r_down)
    g_up, [[r_gate], [p_down]] = _weight_grad(
        n2, dup, name="grad_w_up",
        jobs=[_exchange_job([g_gate], [ax["w_gate"]]), _send_sums_job([s_down], [ax["w_down"]])])
    s_gate = add_halves("w_gate", g_gate, r_gate)
    h_down = sum_parts("w_down", g_down, r_down, p_down)
    dn2, [[r_up], [p_gate]] = _swiglu_bwd_in(
        dgate, dup, w_gate, w_up,
        jobs=[_exchange_job([g_up], [ax["w_up"]]), _send_sums_job([s_gate], [ax["w_gate"]])])
    s_up = add_halves("w_up", g_up, r_up)
    h_gate = sum_parts("w_gate", g_gate, r_gate, p_gate)
    dh1, dh1_b, d_norm_ffn = _rmsnorm_bwd(dn2, h1, norm_ffn_w, dh2, name="norm_ffn_bwd")

    dmixed, [[gr_down]] = _matmul(dh1_b, w_out, name="out_proj_bwd", tb=True, out_dtype=BF16,
                                  jobs=[_join_job([h_down], [ax["w_down"]])])
    far_in = [_to_classes(t) for t in (attn, dmixed[:, :nh * HEAD_DIM], lse)]
    g_out = _weight_grad(mixed, dh1_b, name="grad_w_out")
    d_ret, dg_r, d_ret_norm = _ret_gate_bwd(dmixed, nh, ret, proj, g_r, ret_norm_w, nh)
    far_grads = _attention_far_bwd(qkv_classes, slopes, *far_in, nh)
    far_grads = [_from_classes(t) for t in far_grads]
    dq_r, [[gr_gate]] = _retention(
        (d_ret, 0), (proj, v_r), (proj, k_r), lg_f, lg_b, strict_c=False, strict_a=True, scale=scale, n_heads=nh,
        name="retention_dq", jobs=[_join_job([h_gate], [ax["w_gate"]])])
    (dq_a, dk_a, dv_a), [[p_up], [r_out]] = _attention_bwd(
        proj, slopes, attn, lse, dmixed, far_grads, nh,
        jobs=[_send_sums_job([s_up], [ax["w_up"]]), _exchange_job([g_out], [ax["w_out"]])])
    s_out = add_halves("w_out", g_out, r_out)
    h_up = sum_parts("w_up", g_up, r_up, p_up)
    dv_r, [[p_out], [gr_up]] = _retention(
        (proj, k_r), (proj, q_r), (d_ret, 0), lg_b, lg_f, strict_c=True, strict_a=False, scale=scale, n_heads=nh,
        name="retention_dv", jobs=[_send_sums_job([s_out], [ax["w_out"]]), _join_job([h_up], [ax["w_up"]])])
    h_out = sum_parts("w_out", g_out, r_out, p_out)
    dk_r, [[gr_out]] = _retention(
        (proj, v_r), (d_ret, 0), (proj, q_r), lg_b, lg_f, strict_c=True, strict_a=False, scale=scale, n_heads=nh,
        name="retention_dk", jobs=[_join_job([h_out], [ax["w_out"]])])
    dlg_f, dlg_b = _retention_decay_grads((proj, q_r), (proj, k_r), (proj, v_r), (d_ret, 0), lg_f, lg_b,
                                          scale=scale, n_heads=nh)
    dproj = [dq_a, dk_a, dv_a, dq_r, dk_r, dv_r, dg_r]
    g_in = _weight_grad_pieces(n1, dproj, name="grad_w_in")
    (r_in,) = _run_jobs([_exchange_job([g_in], [ax["w_in"]])], name="grad_exchange_w_in")
    s_in = add_halves("w_in", g_in, r_in)
    dn1, [[p_in]] = _matmul_pieces_nt(dproj, w_in, name="in_proj_bwd", jobs=[_send_sums_job([s_in], [ax["w_in"]])])
    dx, _, d_norm_mix = _rmsnorm_bwd(dn1, x, norm_mix_w, dh1, name="norm_mix_bwd")
    h_in = sum_parts("w_in", g_in, r_in, p_in)
    (gr_in,) = _run_jobs([_join_job([h_in], [ax["w_in"]])], name="grad_join_w_in")

    small = dict(loss=loss[0, 0], norm_mix_w=d_norm_mix, ret_decay_fwd=dlg_f * lg_f, ret_decay_bwd=dlg_b * lg_b,
                 ret_norm_w=d_ret_norm, norm_ffn_w=d_norm_ffn, norm_final_w=d_norm_final)
    return dx, dict(w_in=gr_in, w_out=gr_out, w_gate=gr_gate, w_up=gr_up, w_down=gr_down), small


def _mesh_position():
    x, y, c = lax.axis_index("x"), lax.axis_index("y"), lax.axis_index("c")
    chips = [(1 - x, y), (x, 1 - y), (1 - x, 1 - y)]
    return x, y, c, chips


def _ds(start, size):
    if isinstance(start, int):
        return pl.ds(start, size)
    return pl.ds(pl.multiple_of(start * size, size), size)


def _region(ref, axis, shard, half, shard_size, half_size):
    along = slice(None) if shard is None else _ds(shard, shard_size)
    other = slice(None) if half is None else _ds(half, half_size)
    return ref.at[other, along] if axis == 1 else ref.at[along, other]


def _gather_job(full, axes, stage):
    n = len(full)

    def copies(refs, sems):
        send_sem, recv_sem = sems
        x, y, c, chips = _mesh_position()
        me = 2 * x + y

        def copy(w, k, shard, half, target):
            rows_cols = full[w].shape
            place = _region(refs[w], axes[w], shard, half, rows_cols[axes[w]] // N_CHIPS, rows_cols[1 - axes[w]] // 2)
            return pltpu.make_async_remote_copy(
                src_ref=place, dst_ref=place, send_sem=send_sem.at[w, k], recv_sem=recv_sem.at[w, k],
                device_id=target, device_id_type=MESH)

        def sent(w, k):
            if stage == "ici":
                return copy(w, k, me, c, (chips[k][0], chips[k][1], c))
            return copy(w, k, 2 * chips[k][0] + chips[k][1], c, (x, y, 1 - c))

        def landed(w, k):
            return copy(w, k, 2 * chips[k][0] + chips[k][1], c if stage == "ici" else 1 - c, (x, y, 1 - c))

        return sent, landed

    def start(refs, sems):
        sent, _ = copies(refs, sems)
        for w in range(n):
            for k in range(3):
                sent(w, k).start()

    def finish(refs, sems):
        sent, landed = copies(refs, sems)
        for w in range(n):
            for k in range(3):
                landed(w, k).wait_recv()
                sent(w, k).wait_send()

    return _Job(ios=full, sems=[pltpu.SemaphoreType.DMA((n, 3))] * 2, start=start, finish=finish)


def _exchange_job(grads, axes):
    n = len(grads)

    def half_shape(w):
        return tuple(d // 2 if a != axes[w] else d for a, d in enumerate(grads[w].shape))

    def copy(refs, sems, w):
        x, y, c, _ = _mesh_position()
        return pltpu.make_async_remote_copy(
            src_ref=_region(refs[w], axes[w], None, 1 - c, 0, half_shape(w)[1 - axes[w]]), dst_ref=refs[n + w],
            send_sem=sems[0].at[w], recv_sem=sems[1].at[w], device_id=(x, y, 1 - c), device_id_type=MESH)

    def start(refs, sems):
        for w in range(n):
            copy(refs, sems, w).start()

    def finish(refs, sems):
        for w in range(n):
            copy(refs, sems, w).wait()

    return _Job(ins=grads, outs=[jax.ShapeDtypeStruct(half_shape(w), F32) for w in range(n)],
                sems=[pltpu.SemaphoreType.DMA((n,))] * 2, start=start, finish=finish)


def _half_block_spec(axis, block, half_blocks, use_half):
    if axis == 1:
        if use_half:
            return pl.BlockSpec(block, lambda i, pos: (pos[0] * half_blocks + i, 0))
        return pl.BlockSpec(block, lambda i, pos: (i, 0))
    if use_half:
        return pl.BlockSpec(block, lambda i, pos: (i, pos[0]))
    return pl.BlockSpec(block, lambda i, pos: (i, 0))


def _add_halves(grad, received, axis, pos, *, name):
    rows, cols = received.shape
    tr = _row_block(rows, cols)
    nb = rows // tr

    def body(pos_ref, g_ref, r_ref, o_ref):
        o_ref[...] = (g_ref[...] + r_ref[...]).astype(BF16)

    blk = (tr, cols)
    return pl.pallas_call(
        body, name=name, out_shape=jax.ShapeDtypeStruct((rows, cols), BF16),
        grid_spec=pltpu.PrefetchScalarGridSpec(
            num_scalar_prefetch=1, grid=(nb,),
            in_specs=[_half_block_spec(axis, blk, nb, True), _half_block_spec(axis, blk, nb, False)],
            out_specs=_half_block_spec(axis, blk, nb, False)),
        compiler_params=_params(("parallel",)),
    )(pos, grad, received)


def _send_sums_job(sums, axes):
    n = len(sums)

    def part_shape(w):
        return tuple(d // N_CHIPS if a == axes[w] else d for a, d in enumerate(sums[w].shape))

    def copy(refs, sems, w, k):
        x, y, c, chips = _mesh_position()
        shard = 2 * chips[k][0] + chips[k][1]
        return pltpu.make_async_remote_copy(
            src_ref=_region(refs[w], axes[w], shard, None, part_shape(w)[axes[w]], 0), dst_ref=refs[n + w].at[k],
            send_sem=sems[0].at[w, k], recv_sem=sems[1].at[w, k],
            device_id=(chips[k][0], chips[k][1], c), device_id_type=MESH)

    def start(refs, sems):
        for w in range(n):
            for k in range(3):
                copy(refs, sems, w, k).start()

    def finish(refs, sems):
        for w in range(n):
            for k in range(3):
                copy(refs, sems, w, k).wait()

    return _Job(ins=sums, outs=[jax.ShapeDtypeStruct((3,) + part_shape(w), BF16) for w in range(n)],
                sems=[pltpu.SemaphoreType.DMA((n, 3))] * 2, start=start, finish=finish)


def _sum_chip_parts(grad, received, parts, axis, pos, *, name):
    _, rows, cols = parts.shape
    tr = _row_block(rows, cols)
    nb = rows // tr
    blk = (tr, cols)

    def body(pos_ref, g_ref, r_ref, p_ref, o_ref):
        total = g_ref[...] + r_ref[...]
        for k in range(3):
            total = total + p_ref[k].astype(F32)
        o_ref[...] = total

    if axis == 1:
        g_spec = pl.BlockSpec(blk, lambda i, pos: (pos[0] * nb + i, pos[1]))
        r_spec = pl.BlockSpec(blk, lambda i, pos: (i, pos[1]))
        o_spec = pl.BlockSpec(blk, lambda i, pos: (pos[0] * nb + i, 0))
        shard_shape = (2 * rows, cols)
    else:
        g_spec = pl.BlockSpec(blk, lambda i, pos: (pos[1] * nb + i, pos[0]))
        r_spec = pl.BlockSpec(blk, lambda i, pos: (pos[1] * nb + i, 0))
        o_spec = pl.BlockSpec(blk, lambda i, pos: (i, pos[0]))
        shard_shape = (rows, 2 * cols)
    return pl.pallas_call(
        body, name=name, out_shape=jax.ShapeDtypeStruct(shard_shape, F32),
        grid_spec=pltpu.PrefetchScalarGridSpec(
            num_scalar_prefetch=1, grid=(nb,),
            in_specs=[g_spec, r_spec, pl.BlockSpec((3,) + blk, lambda i, pos: (0, i, 0))],
            out_specs=o_spec),
        compiler_params=_params(("parallel",)),
    )(pos, grad, received, parts)


def _join_job(shards, axes):
    n = len(shards)

    def copy(refs, sems, w, other):
        x, y, c, _ = _mesh_position()
        place = _region(refs[w], axes[w], None, 1 - c if other else c, 0, shards[w].shape[1 - axes[w]] // 2)
        return pltpu.make_async_remote_copy(
            src_ref=place, dst_ref=place, send_sem=sems[0].at[w], recv_sem=sems[1].at[w],
            device_id=(x, y, 1 - c), device_id_type=MESH)

    def start(refs, sems):
        for w in range(n):
            copy(refs, sems, w, False).start()

    def finish(refs, sems):
        for w in range(n):
            copy(refs, sems, w, True).wait_recv()
            copy(refs, sems, w, False).wait_send()

    return _Job(ios=shards, sems=[pltpu.SemaphoreType.DMA((n,))] * 2, start=start, finish=finish)


def _all_reduce_small(vec):
    rows, cols = vec.shape

    def body(v_ref, o_ref, land_ref, send_sem, recv_sem):
        x, y, c, _ = _mesh_position()
        me = 4 * x + 2 * y + c
        land_ref[me] = v_ref[...]
        copies = []
        for k in range(1, 8):
            px, py, pc = x ^ (k >> 2), y ^ ((k >> 1) & 1), c ^ (k & 1)
            copies.append(pltpu.make_async_remote_copy(
                src_ref=v_ref, dst_ref=land_ref.at[me], send_sem=send_sem.at[k], recv_sem=recv_sem.at[k],
                device_id=(px, py, pc), device_id_type=MESH))
        for cp in copies:
            cp.start()
        for k in range(1, 8):
            peer = me ^ k
            pltpu.make_async_remote_copy(
                src_ref=v_ref, dst_ref=land_ref.at[peer], send_sem=send_sem.at[k], recv_sem=recv_sem.at[k],
                device_id=(x, y, c), device_id_type=MESH).wait_recv()
        for cp in copies:
            cp.wait_send()
        total = land_ref[0]
        for k in range(1, 8):
            total = total + land_ref[k]
        o_ref[...] = total

    vmem = pl.BlockSpec(memory_space=pltpu.VMEM)
    return pl.pallas_call(
        body, name="all_reduce_small", in_specs=[vmem], out_specs=vmem,
        out_shape=jax.ShapeDtypeStruct((rows, cols), F32),
        scratch_shapes=[pltpu.VMEM((8, rows, cols), F32), pltpu.SemaphoreType.DMA((8,)), pltpu.SemaphoreType.DMA((8,))],
    )(vec)


def _adamw(w, g, m, v, *, name):
    rows, cols = w.shape
    tr = _row_block(rows, cols) if rows % 8 == 0 else rows
    bc1 = 1.0 - ADAM_B1 ** ADAM_STEP
    bc2 = 1.0 - ADAM_B2 ** ADAM_STEP

    def body(w_ref, g_ref, m_ref, v_ref, go_ref, d_ref, mo_ref, vo_ref):
        gv = g_ref[...]
        go_ref[...] = gv
        mn = ADAM_B1 * m_ref[...] + (1.0 - ADAM_B1) * gv
        vn = ADAM_B2 * v_ref[...] + (1.0 - ADAM_B2) * (gv * gv)
        mo_ref[...] = mn
        vo_ref[...] = vn
        d_ref[...] = -ADAM_LR * ((mn / bc1) / (jnp.sqrt(vn / bc2) + ADAM_EPS) + ADAM_WD * w_ref[...])

    blk = pl.BlockSpec((tr, cols), lambda i: (i, 0))
    shape = jax.ShapeDtypeStruct((rows, cols), F32)
    return pl.pallas_call(
        body, name=name, grid=(rows // tr,), in_specs=[blk] * 4, out_specs=[blk] * 4, out_shape=[shape] * 4,
        compiler_params=_params(("parallel",)),
    )(w, g, m, v)


def _to_bf16_in_place(w, axis, pos, *, name):
    rows, cols = w.shape
    tr = _row_block(rows, cols)
    nb = rows // tr

    def body(pos_ref, w_ref, o_ref):
        o_ref[...] = w_ref[...].astype(BF16)

    if axis == 1:
        o_spec = pl.BlockSpec((tr, cols), lambda i, pos: (i, pos[1]))
        full_shape = (rows, N_CHIPS * cols)
    else:
        o_spec = pl.BlockSpec((tr, cols), lambda i, pos: (pos[1] * nb + i, 0))
        full_shape = (N_CHIPS * rows, cols)
    return pl.pallas_call(
        body, name=name, out_shape=jax.ShapeDtypeStruct(full_shape, BF16),
        grid_spec=pltpu.PrefetchScalarGridSpec(
            num_scalar_prefetch=1, grid=(nb,),
            in_specs=[pl.BlockSpec((tr, cols), lambda i, pos: (i, 0))], out_specs=o_spec),
        compiler_params=_params(("parallel",)),
    )(pos, w)


BIG = ("w_in", "w_out", "w_gate", "w_up", "w_down")
BIG_AXIS = dict(w_in=1, w_out=0, w_gate=1, w_up=1, w_down=0)
SMALL = ("norm_mix_w", "ret_decay_fwd", "ret_decay_bwd", "ret_norm_w", "norm_ffn_w", "norm_final_w")
ALL_WEIGHTS = ("norm_mix_w", "w_in", "ret_decay_fwd", "ret_decay_bwd", "ret_norm_w", "w_out", "norm_ffn_w",
               "w_gate", "w_up", "w_down", "norm_final_w")
SMALL_ROW = 128 * 8


def _pack_small(small):
    pieces = [jnp.reshape(small["loss"], (1,))] + [jnp.reshape(small[k], (-1,)) for k in SMALL]
    rows = []
    for p in pieces:
        pad = -p.shape[0] % (8 * SMALL_ROW)
        rows.append(jnp.reshape(jnp.pad(p, (0, pad)), (-1, SMALL_ROW)))
    return jnp.concatenate(rows, axis=0)


def _unpack_small(block, like):
    out, row = {}, 0
    for k in ("loss",) + SMALL:
        size = 1 if k == "loss" else like[k].size
        nrows = -(-size // (8 * SMALL_ROW)) * 8
        out[k] = jnp.reshape(block[row:row + nrows], (-1,))[:size]
        row += nrows
    return out


def kernel(x, norm_mix_w, w_in, ret_decay_fwd, ret_decay_bwd, ret_norm_w, w_out, norm_ffn_w, w_gate, w_up, w_down, norm_final_w, loss_target, m_norm_mix_w, m_w_in, m_ret_decay_fwd, m_ret_decay_bwd, m_ret_norm_w, m_w_out, m_norm_ffn_w, m_w_gate, m_w_up, m_w_down, m_norm_final_w, v_norm_mix_w, v_w_in, v_ret_decay_fwd, v_ret_decay_bwd, v_ret_norm_w, v_w_out, v_norm_ffn_w, v_w_gate, v_w_up, v_w_down, v_norm_final_w):
    weights = dict(norm_mix_w=norm_mix_w, w_in=w_in, ret_decay_fwd=ret_decay_fwd, ret_decay_bwd=ret_decay_bwd,
                   ret_norm_w=ret_norm_w, w_out=w_out, norm_ffn_w=norm_ffn_w, w_gate=w_gate, w_up=w_up,
                   w_down=w_down, norm_final_w=norm_final_w)
    m_in = dict(norm_mix_w=m_norm_mix_w, w_in=m_w_in, ret_decay_fwd=m_ret_decay_fwd, ret_decay_bwd=m_ret_decay_bwd,
                ret_norm_w=m_ret_norm_w, w_out=m_w_out, norm_ffn_w=m_norm_ffn_w, w_gate=m_w_gate, w_up=m_w_up,
                w_down=m_w_down, norm_final_w=m_norm_final_w)
    v_in = dict(norm_mix_w=v_norm_mix_w, w_in=v_w_in, ret_decay_fwd=v_ret_decay_fwd, ret_decay_bwd=v_ret_decay_bwd,
                ret_norm_w=v_ret_norm_w, w_out=v_w_out, norm_ffn_w=v_norm_ffn_w, w_gate=v_w_gate, w_up=v_w_up,
                w_down=v_w_down, norm_final_w=v_norm_final_w)
    pos = jnp.stack([lax.axis_index("c"), 2 * lax.axis_index("x") + lax.axis_index("y")]).astype(jnp.int32)

    own = {k: _to_bf16_in_place(weights[k][0], BIG_AXIS[k], pos, name="cast_" + k) for k in BIG}

    dx, grad_w, small = _step(
        x[0], loss_target[0], norm_mix_w, ret_decay_fwd[0], ret_decay_bwd[0], ret_norm_w, norm_ffn_w,
        norm_final_w[None, :], own, pos)

    like = {k: weights[k] for k in SMALL}
    reduced = _unpack_small(_all_reduce_small(_pack_small(small)), like)
    loss = reduced["loss"][0]
    for k in SMALL:
        grad_w[k] = jnp.reshape(reduced[k], (1, -1))

    delta, new_m, new_v = {}, {}, {}
    for k in ALL_WEIGHTS:
        shape = weights[k].shape
        as2d = (lambda t: jnp.reshape(t, (-1, shape[-1])))
        grad_w[k], delta[k], new_m[k], new_v[k] = (jnp.reshape(t, shape) for t in _adamw(
            as2d(weights[k]), as2d(grad_w[k]), as2d(m_in[k]), as2d(v_in[k]), name="adamw_" + k))

    return (loss, dx[None], *[grad_w[k] for k in ALL_WEIGHTS], *[delta[k] for k in ALL_WEIGHTS],
            *[new_m[k] for k in ALL_WEIGHTS], *[new_v[k] for k in ALL_WEIGHTS])
```

```python
import functools
import math

import numpy as np
import jax
import jax.numpy as jnp
from jax import lax
from jax.experimental import pallas as pl
from jax.experimental.pallas import tpu as pltpu

F32 = jnp.float32
BF16 = jnp.bfloat16
MESH = pl.DeviceIdType.MESH

HEAD_DIM = 128
RET_CHUNK = 128
RET_UNROLL = 8
EPS = 1e-6
DILATED_PATTERNS = ((128, 1), (512, 4), (2048, 16))
ATT_BLOCK = 256
ATT_REACH = max(w // 2 for w, _ in DILATED_PATTERNS)
ATT_NEAR = ATT_BLOCK
ATT_CLASSES = DILATED_PATTERNS[-1][1]
assert all(w // 2 <= ATT_NEAR for w, _ in DILATED_PATTERNS[:-1])
ATT_KB = -(-ATT_NEAR // ATT_BLOCK)
ATT_WINDOW = 2 * ATT_KB + 1
ATT_FAR_GROUP = 8
ATT_NEAR_GROUP = 4
MASKED = -1e30
ROW_MAX_INIT = -1e29
N_CHIPS = 4
VMEM_LIMIT_BYTES = 56 * 1024 * 1024
ELEM_BLOCK_BYTES = 2 * 1024 * 1024

ADAM_LR = 0.001
ADAM_B1 = 0.9
ADAM_B2 = 0.999
ADAM_EPS = 1e-08
ADAM_WD = 0.01
ADAM_STEP = 10


def _params(sem=None):
    return pltpu.CompilerParams(dimension_semantics=sem, vmem_limit_bytes=VMEM_LIMIT_BYTES)


def _sigmoid(x):
    return 0.5 * jnp.tanh(0.5 * x) + 0.5


class _Job:
    def __init__(self, *, ins=(), ios=(), outs=(), sems=(), start, finish):
        self.ins, self.ios, self.outs, self.sems = list(ins), list(ios), list(outs), list(sems)
        self.start, self.finish = start, finish

    def results(self):
        return [jax.ShapeDtypeStruct(a.shape, a.dtype) for a in self.ios] + self.outs


def _call(body, *, name, grid, in_specs, out_specs, out_shape, operands, scratch_shapes=(), semantics=None, jobs=()):
    in_specs, out_specs, out_shape = list(in_specs), list(out_specs), list(out_shape)
    scratch_shapes = list(scratch_shapes)
    if not jobs:
        outs = pl.pallas_call(body, name=name, grid=grid, in_specs=in_specs, out_specs=out_specs, out_shape=out_shape,
                              scratch_shapes=scratch_shapes, compiler_params=_params(semantics))(*operands)
        return outs, []
    n_in, n_out, n_scratch = len(in_specs), len(out_specs), len(scratch_shapes)
    extra_in, extra_out, sems, aliases = [], [], [], {}
    for job in jobs:
        extra_in += job.ins
        for t in range(len(job.ios)):
            aliases[n_in + len(extra_in) + t] = n_out + len(extra_out) + t
        extra_in += job.ios
        extra_out += job.results()
        sems += job.sems

    def carried(*refs):
        x_in = refs[n_in:n_in + len(extra_in)]
        x_out = refs[n_in + len(extra_in) + n_out:n_in + len(extra_in) + n_out + len(extra_out)]
        x_sem = refs[len(refs) - len(sems):]
        views, i_in, i_out, i_sem = [], 0, 0, 0
        for job in jobs:
            data = list(x_in[i_in:i_in + len(job.ins)]) + list(x_out[i_out:i_out + len(job.results())])
            views.append((data, x_sem[i_sem:i_sem + len(job.sems)]))
            i_in += len(job.ins) + len(job.ios)
            i_out += len(job.results())
            i_sem += len(job.sems)
        steps = [pl.program_id(d) for d in range(len(grid))]

        @pl.when(functools.reduce(jnp.logical_and, [s == 0 for s in steps]))
        def _():
            for job, (data, sem) in zip(jobs, views):
                job.start(data, sem)

        body(*refs[:n_in], *refs[n_in + len(extra_in):n_in + len(extra_in) + n_out],
             *refs[len(refs) - len(sems) - n_scratch:len(refs) - len(sems)])

        @pl.when(functools.reduce(jnp.logical_and, [s == g - 1 for s, g in zip(steps, grid)]))
        def _():
            for job, (data, sem) in zip(jobs, views):
                job.finish(data, sem)

    hbm = pl.BlockSpec(memory_space=pl.ANY)
    res = pl.pallas_call(
        carried, name=name, grid=grid, in_specs=in_specs + [hbm] * len(extra_in),
        out_specs=out_specs + [hbm] * len(extra_out), out_shape=out_shape + extra_out,
        input_output_aliases=aliases, scratch_shapes=scratch_shapes + sems,
        compiler_params=_params(("arbitrary",) * len(grid)),
    )(*operands, *extra_in)
    carried_results, at = [], n_out
    for job in jobs:
        carried_results.append(list(res[at:at + len(job.results())]))
        at += len(job.results())
    return list(res[:n_out]), carried_results


def _run_jobs(jobs, *, name):
    first = jobs[0]
    n_in, n_io = len(first.ins), len(first.ios)
    out_shape = first.results()
    n_sems = [len(job.sems) for job in jobs]

    def body(*refs):
        data = list(refs[:n_in]) + list(refs[n_in + n_io:n_in + n_io + len(out_shape)])
        at = n_in + n_io + len(out_shape)
        for job, ns in zip(jobs, n_sems):
            job.start(data, refs[at:at + ns])
            job.finish(data, refs[at:at + ns])
            at += ns

    hbm = pl.BlockSpec(memory_space=pl.ANY)
    return pl.pallas_call(
        body, name=name, in_specs=[hbm] * (n_in + n_io), out_specs=[hbm] * len(out_shape), out_shape=out_shape,
        input_output_aliases={n_in + t: t for t in range(n_io)},
        scratch_shapes=[s for job in jobs for s in job.sems],
    )(*first.ins, *first.ios)


def _dot(a, b, ta=False, tb=False):
    return lax.dot_general(a, b, (((0 if ta else 1,), (1 if tb else 0,)), ((), ())),
                           preferred_element_type=F32)


def _tile(n, want):
    t = min(n, want) // 128 * 128
    while n % t:
        t -= 128
    return t


def _a_spec(ta, tm, tk):
    return pl.BlockSpec((tk, tm), lambda i, j, k: (k, i)) if ta else pl.BlockSpec((tm, tk), lambda i, j, k: (i, k))


def _b_spec(tb, tk, tn):
    return pl.BlockSpec((tn, tk), lambda i, j, k: (j, k)) if tb else pl.BlockSpec((tk, tn), lambda i, j, k: (k, j))


def _accumulate(accs, nk, products, finish):
    if nk == 1:
        finish(*products())
        return
    k = pl.program_id(2)

    @pl.when(k == 0)
    def _():
        for acc, p in zip(accs, products()):
            acc[...] = p

    if nk > 2:
        @pl.when(jnp.logical_and(k > 0, k < nk - 1))
        def _():
            for acc, p in zip(accs, products()):
                acc[...] += p

    @pl.when(k == nk - 1)
    def _():
        finish(*[acc[...] + p for acc, p in zip(accs, products())])


def _matmul(a, b, *, name, ta=False, tb=False, out_dtype=F32, residual=None, tm=1024, tn=1024, tk=2048, jobs=()):
    m, kdim = (a.shape[1], a.shape[0]) if ta else a.shape
    n = b.shape[0] if tb else b.shape[1]
    tm, tn, tk = _tile(m, tm), _tile(n, tn), _tile(kdim, tk)
    nk = kdim // tk

    def body(*refs):
        a_ref, b_ref = refs[:2]
        r_ref = refs[2] if residual is not None else None
        o_ref = refs[-1] if nk == 1 else refs[-2]

        def finish(total):
            if residual is not None:
                total = total + r_ref[...]
            o_ref[...] = total.astype(out_dtype)

        _accumulate(refs[-1:] if nk > 1 else (), nk, lambda: (_dot(a_ref[...], b_ref[...], ta, tb),), finish)

    o_spec = pl.BlockSpec((tm, tn), lambda i, j, k: (i, j))
    in_specs = [_a_spec(ta, tm, tk), _b_spec(tb, tk, tn)]
    operands = [a, b]
    if residual is not None:
        in_specs.append(o_spec)
        operands.append(residual)
    (out,), carried = _call(
        body, name=name, grid=(m // tm, n // tn, nk), in_specs=in_specs, out_specs=[o_spec],
        out_shape=[jax.ShapeDtypeStruct((m, n), out_dtype)], operands=operands,
        scratch_shapes=[pltpu.VMEM((tm, tn), F32)] * (nk > 1),
        semantics=("parallel", "parallel", "arbitrary"), jobs=jobs)
    return (out, carried) if jobs else out


def _matmul_pieces_nt(pieces, b, *, name, tm=512, tn=1024, jobs=()):
    m, kp = pieces[0].shape
    n = b.shape[0]
    tm, tn = _tile(m, tm), _tile(n, tn)
    count = len(pieces)

    def body(*refs):
        b_ref, o_ref = refs[count], refs[count + 1]
        total = _dot(refs[0][...], b_ref[:, pl.ds(0, kp)], tb=True)
        for p in range(1, count):
            total = total + _dot(refs[p][...], b_ref[:, pl.ds(p * kp, kp)], tb=True)
        o_ref[...] = total

    piece = pl.BlockSpec((tm, kp), lambda j, i: (i, 0))
    (out,), carried = _call(
        body, name=name, grid=(n // tn, m // tm),
        in_specs=[piece] * count + [pl.BlockSpec((tn, count * kp), lambda j, i: (j, 0))],
        out_specs=[pl.BlockSpec((tm, tn), lambda j, i: (i, j))],
        out_shape=[jax.ShapeDtypeStruct((m, n), F32)], operands=[*pieces, b],
        semantics=("parallel", "parallel"), jobs=jobs)
    return (out, carried) if jobs else out


def _weight_grad_pieces(a, pieces, *, name):
    tokens, m = a.shape
    np_ = pieces[0].shape[1]
    tm = 1024 if m % 1024 == 0 else _tile(m, 1408)
    tn = _tile(np_, 512)
    nb = np_ // tn
    out = None
    for p, piece in enumerate(pieces):
        def body(*refs):
            refs[-1][...] = _dot(refs[0][...], refs[1][...], ta=True)

        in_specs = [pl.BlockSpec((tokens, tm), lambda i, j: (0, i)), pl.BlockSpec((tokens, tn), lambda i, j: (0, j))]
        operands = [a, piece]
        if out is not None:
            in_specs.append(pl.BlockSpec(memory_space=pl.ANY))
            operands.append(out)
        out = pl.pallas_call(
            body, name="%s_%d" % (name, p), grid=(m // tm, nb), in_specs=in_specs,
            out_specs=pl.BlockSpec((tm, tn), lambda i, j, p=p: (i, p * nb + j)),
            out_shape=jax.ShapeDtypeStruct((m, len(pieces) * np_), F32),
            input_output_aliases={2: 0} if len(operands) == 3 else {},
            compiler_params=_params(("parallel", "parallel")),
        )(*operands)
    return out


def _weight_grad(a, g, *, name, jobs=()):
    tokens, m = a.shape
    tm = 1024 if m % 1024 == 0 else _tile(m, 1408)
    return _matmul(a, g, name=name, ta=True, tm=tm, tn=512, tk=tokens, jobs=jobs)


def _swiglu_fwd(n2, w_gate, w_up, *, tm=1024, tn=512, tk=2048, jobs=()):
    m, kdim = n2.shape
    n = w_gate.shape[1]
    tm, tn, tk = _tile(m, tm), _tile(n, tn), _tile(kdim, tk)
    nk = kdim // tk

    def body(a_ref, g_ref, u_ref, gate_ref, up_ref, act_ref, *acc):
        def products():
            a = a_ref[...]
            return _dot(a, g_ref[...]), _dot(a, u_ref[...])

        def finish(g, u):
            gate_ref[...] = g.astype(BF16)
            up_ref[...] = u.astype(BF16)
            act_ref[...] = (g * _sigmoid(g) * u).astype(BF16)

        _accumulate(acc, nk, products, finish)

    o_spec = pl.BlockSpec((tm, tn), lambda i, j, k: (i, j))
    o_shape = jax.ShapeDtypeStruct((m, n), BF16)
    return _call(
        body, name="swiglu_fwd", grid=(m // tm, n // tn, nk),
        in_specs=[_a_spec(False, tm, tk), _b_spec(False, tk, tn), _b_spec(False, tk, tn)],
        out_specs=[o_spec] * 3, out_shape=[o_shape] * 3, operands=[n2, w_gate, w_up],
        scratch_shapes=[pltpu.VMEM((tm, tn), F32)] * (2 * (nk > 1)),
        semantics=("parallel", "parallel", "arbitrary"), jobs=jobs)


def _swiglu_bwd_act(dh2, w_down, gate, up, *, tm=1024, tn=512, tk=2048):
    m, kdim = dh2.shape
    n = w_down.shape[0]
    tm, tn, tk = _tile(m, tm), _tile(n, tn), _tile(kdim, tk)
    nk = kdim // tk

    sub = _tile(tn, 256)

    def body(a_ref, b_ref, gate_ref, up_ref, dgate_ref, dup_ref, *acc):
        def finish(dact, cols=slice(None)):
            g = gate_ref[:, cols].astype(F32)
            u = up_ref[:, cols].astype(F32)
            sg = _sigmoid(g)
            dup_ref[:, cols] = (dact * g * sg).astype(BF16)
            dgate_ref[:, cols] = (dact * u * sg * (1.0 + g * (1.0 - sg))).astype(BF16)

        if nk == 1:
            a = a_ref[...]
            for c in range(tn // sub):
                cols = pl.ds(c * sub, sub)
                finish(_dot(a, b_ref[cols, :], tb=True), cols)
        else:
            _accumulate(acc, nk, lambda: (_dot(a_ref[...], b_ref[...], tb=True),), finish)

    o_spec = pl.BlockSpec((tm, tn), lambda i, j, k: (i, j))
    o_shape = jax.ShapeDtypeStruct((m, n), BF16)
    return pl.pallas_call(
        body, name="swiglu_bwd_act", grid=(m // tm, n // tn, nk),
        in_specs=[_a_spec(False, tm, tk), _b_spec(True, tk, tn), o_spec, o_spec],
        out_specs=[o_spec] * 2, out_shape=[o_shape] * 2,
        scratch_shapes=[pltpu.VMEM((tm, tn), F32)] * (nk > 1),
        compiler_params=_params(("parallel", "parallel", "arbitrary")),
    )(dh2, w_down, gate, up)


def _swiglu_bwd_in(dgate, dup, w_gate, w_up, *, tm=1024, tn=1024, tk=1408, jobs=()):
    m, kdim = dgate.shape
    n = w_gate.shape[0]
    tm, tn, tk = _tile(m, tm), _tile(n, tn), _tile(kdim, tk)
    nk = kdim // tk

    def body(a1_ref, a2_ref, b1_ref, b2_ref, o_ref, *acc):
        def product():
            return (_dot(a1_ref[...], b1_ref[...], tb=True) + _dot(a2_ref[...], b2_ref[...], tb=True),)

        def finish(total):
            o_ref[...] = total

        _accumulate(acc, nk, product, finish)

    a_spec, b_spec = _a_spec(False, tm, tk), _b_spec(True, tk, tn)
    (out,), carried = _call(
        body, name="swiglu_bwd_in", grid=(m // tm, n // tn, nk),
        in_specs=[a_spec, a_spec, b_spec, b_spec],
        out_specs=[pl.BlockSpec((tm, tn), lambda i, j, k: (i, j))],
        out_shape=[jax.ShapeDtypeStruct((m, n), F32)], operands=[dgate, dup, w_gate, w_up],
        scratch_shapes=[pltpu.VMEM((tm, tn), F32)] * (nk > 1),
        semantics=("parallel", "parallel", "arbitrary"), jobs=jobs)
    return out, carried


def _row_block(rows, cols):
    tr = min(rows, max(16, ELEM_BLOCK_BYTES // (4 * cols) // 16 * 16))
    while rows % tr:
        tr -= 16
    return tr


def _rmsnorm_fwd(x, g, *, name):
    s, d = x.shape
    tr = _row_block(s, d)

    def body(x_ref, g_ref, n_ref):
        xv = x_ref[...]
        r = lax.rsqrt(jnp.mean(xv * xv, axis=-1, keepdims=True) + EPS)
        n_ref[...] = (xv * r * g_ref[...]).astype(BF16)

    row = pl.BlockSpec((tr, d), lambda i: (i, 0))
    return pl.pallas_call(
        body, name=name, grid=(s // tr,), in_specs=[row, pl.BlockSpec((1, d), lambda i: (0, 0))],
        out_specs=row, out_shape=jax.ShapeDtypeStruct((s, d), BF16),
        compiler_params=_params(("parallel",)),
    )(x, g)


def _rmsnorm_bwd_rows(xv, gv, dy):
    r = lax.rsqrt(jnp.mean(xv * xv, axis=-1, keepdims=True) + EPS)
    xhat = xv * r
    dxh = dy * gv
    dx = r * (dxh - xhat * jnp.mean(dxh * xhat, axis=-1, keepdims=True))
    return dx, dy * xhat


def _rmsnorm_bwd(dn, x, g, skip, *, name):
    s, d = x.shape
    tr = _row_block(s, d)

    def body(dn_ref, x_ref, g_ref, skip_ref, dx_ref, dxb_ref, dg_ref):
        dx, dgr = _rmsnorm_bwd_rows(x_ref[...], g_ref[...], dn_ref[...])
        dx = dx + skip_ref[...]
        dx_ref[...] = dx
        dxb_ref[...] = dx.astype(BF16)

        @pl.when(pl.program_id(0) == 0)
        def _():
            dg_ref[...] = jnp.zeros_like(dg_ref)

        dg_ref[...] += jnp.sum(dgr, axis=0, keepdims=True)

    row = pl.BlockSpec((tr, d), lambda i: (i, 0))
    vec = pl.BlockSpec((1, d), lambda i: (0, 0))
    return pl.pallas_call(
        body, name=name, grid=(s // tr,), in_specs=[row, row, vec, row],
        out_specs=[row, row, vec],
        out_shape=[jax.ShapeDtypeStruct((s, d), F32), jax.ShapeDtypeStruct((s, d), BF16),
                   jax.ShapeDtypeStruct((1, d), F32)],
        compiler_params=_params(("arbitrary",)),
    )(dn, x, g, skip)


def _loss_head(h2, g, target):
    s, d = h2.shape
    tr = _row_block(s, d)

    def body(h_ref, g_ref, t_ref, dh_ref, dhb_ref, dg_ref, loss_ref):
        hv = h_ref[...]
        gv = g_ref[...]
        r = lax.rsqrt(jnp.mean(hv * hv, axis=-1, keepdims=True) + EPS)
        err = hv * r * gv - t_ref[...]
        dx, dgr = _rmsnorm_bwd_rows(hv, gv, err * (1.0 / d))
        dh_ref[...] = dx
        dhb_ref[...] = dx.astype(BF16)

        @pl.when(pl.program_id(0) == 0)
        def _():
            dg_ref[...] = jnp.zeros_like(dg_ref)
            loss_ref[...] = jnp.zeros_like(loss_ref)

        dg_ref[...] += jnp.sum(dgr, axis=0, keepdims=True)
        row_loss = jnp.mean(err * err, axis=-1, keepdims=True)
        loss_ref[...] += 0.5 * jnp.sum(row_loss, axis=0, keepdims=True)

    row = pl.BlockSpec((tr, d), lambda i: (i, 0))
    vec = pl.BlockSpec((1, d), lambda i: (0, 0))
    one = pl.BlockSpec((1, 1), lambda i: (0, 0))
    return pl.pallas_call(
        body, name="loss_head", grid=(s // tr,), in_specs=[row, vec, row],
        out_specs=[row, row, vec, one],
        out_shape=[jax.ShapeDtypeStruct((s, d), F32), jax.ShapeDtypeStruct((s, d), BF16),
                   jax.ShapeDtypeStruct((1, d), F32), jax.ShapeDtypeStruct((1, 1), F32)],
        compiler_params=_params(("arbitrary",)),
    )(h2, g, target)


def _attention_bias_tables():
    k = np.arange(-ATT_KB, ATT_KB + 1)[:, None, None]
    delta = k * ATT_BLOCK + np.arange(ATT_BLOCK)[None, None, :] - np.arange(ATT_BLOCK)[None, :, None]
    dist = np.abs(delta)
    count = np.zeros(delta.shape, np.int32)
    for window, dilation in DILATED_PATTERNS:
        count += (delta % dilation == 0) & (dist <= min(window // 2, ATT_NEAR))
    logc = np.where(count > 0, np.log(np.maximum(count, 1)), MASKED)
    return dist.astype(np.float32), logc.astype(np.float32)


def _far_bias_tables(per_class):
    steps = np.abs(np.arange(per_class)[:, None] - np.arange(per_class)[None, :]) * ATT_CLASSES
    valid = (steps > ATT_NEAR) & (steps <= ATT_REACH)
    return steps.astype(np.float32), np.where(valid, 0.0, MASKED).astype(np.float32)


def _to_classes(x):
    s, cols = x.shape
    return jnp.reshape(jnp.transpose(jnp.reshape(x, (s // ATT_CLASSES, ATT_CLASSES, cols)), (1, 0, 2)), (s, cols))


def _from_classes(x):
    s, cols = x.shape
    return jnp.reshape(jnp.transpose(jnp.reshape(x, (ATT_CLASSES, s // ATT_CLASSES, cols)), (1, 0, 2)), (s, cols))


def _head_bias(bias_ref, slope, dist_ref, logc_ref):
    for kk in range(ATT_WINDOW):
        bias_ref[kk] = logc_ref[kk] - slope * dist_ref[kk]
    bias_ref[ATT_WINDOW] = jnp.full((ATT_BLOCK, ATT_BLOCK), MASKED, F32)


def _window_start(i, nq, nwin):
    return jnp.clip(i - ATT_KB, 0, nq - nwin)


def _window_block(j, i):
    rows = pl.ds(pl.multiple_of(j * ATT_BLOCK, ATT_BLOCK), ATT_BLOCK)
    kk = j - i + ATT_KB
    return rows, jnp.where(jnp.logical_and(kk >= 0, kk < ATT_WINDOW), kk, ATT_WINDOW)


def _attention_far_fwd(qkv, slopes, n_heads):
    s = qkv.shape[0]
    per_class = s // ATT_CLASSES
    scale = HEAD_DIM ** -0.5
    dist, logc = _far_bias_tables(per_class)

    def body(slope_ref, q_ref, k_ref, v_ref, dist_ref, logc_ref, o_ref, lse_ref):
        bias = logc_ref[...] - slope_ref[pl.program_id(0)] * dist_ref[...]
        for a in range(ATT_FAR_GROUP):
            rows = pl.ds(a * per_class, per_class)
            sc = _dot(q_ref[rows, :], k_ref[rows, :], tb=True) * scale + bias
            m = jnp.maximum(jnp.max(sc, axis=-1, keepdims=True), ROW_MAX_INIT)
            p = jnp.exp(sc - m)
            l = jnp.maximum(jnp.sum(p, axis=-1, keepdims=True), 1e-30)
            o_ref[rows, :] = (_dot(p.astype(BF16), v_ref[rows, :]) / l).astype(BF16)
            lse_ref[rows, :] = jnp.broadcast_to(m + jnp.log(l), (per_class, HEAD_DIM))

    hh = n_heads
    blk = pl.BlockSpec((ATT_FAR_GROUP * per_class, HEAD_DIM), lambda h, r: (r, h))
    table = pl.BlockSpec(dist.shape, lambda h, r: (0, 0))
    return pl.pallas_call(
        body, name="attention_far_fwd", grid=(hh, ATT_CLASSES // ATT_FAR_GROUP),
        in_specs=[pl.BlockSpec(memory_space=pltpu.SMEM), blk,
                  pl.BlockSpec((ATT_FAR_GROUP * per_class, HEAD_DIM), lambda h, r: (r, hh + h)),
                  pl.BlockSpec((ATT_FAR_GROUP * per_class, HEAD_DIM), lambda h, r: (r, 2 * hh + h)), table, table],
        out_specs=[blk, blk],
        out_shape=[jax.ShapeDtypeStruct((s, hh * HEAD_DIM), BF16), jax.ShapeDtypeStruct((s, hh * HEAD_DIM), F32)],
        compiler_params=_params(("parallel", "parallel")),
    )(slopes, qkv, qkv, qkv, jnp.asarray(dist), jnp.asarray(logc))


def _attention_fwd(proj, slopes, far_out, far_lse, n_heads, jobs=()):
    s = proj.shape[0]
    nq = s // ATT_BLOCK
    scale = HEAD_DIM ** -0.5
    dist, logc = _attention_bias_tables()

    nwin = min(ATT_WINDOW, nq)

    group = math.gcd(ATT_NEAR_GROUP, nq)

    def body(slope_ref, q_ref, k_ref, v_ref, fo_ref, fl_ref, dist_ref, logc_ref, o_ref, lse_ref, bias_ref, s_ref):
        h, step = pl.program_id(0), pl.program_id(1)

        @pl.when(step == 0)
        def _():
            _head_bias(bias_ref, slope_ref[h], dist_ref, logc_ref)

        for a in range(group):
            i = step * group + a
            mine = pl.ds(a * ATT_BLOCK, ATT_BLOCK)
            q = q_ref[mine, :]
            first = _window_start(i, nq, nwin)
            m = jnp.full((ATT_BLOCK, 1), ROW_MAX_INIT, F32)
            for b in range(nwin):
                rows, kk = _window_block(first + b, i)
                sc = _dot(q, k_ref[rows, :], tb=True) * scale + bias_ref[kk]
                s_ref[a * nwin + b] = sc
                m = jnp.maximum(m, jnp.max(sc, axis=-1, keepdims=True))
            l = jnp.zeros((ATT_BLOCK, 1), F32)
            acc = jnp.zeros((ATT_BLOCK, HEAD_DIM), F32)
            for b in range(nwin):
                rows, _ = _window_block(first + b, i)
                p = jnp.exp(s_ref[a * nwin + b] - m)
                l = l + jnp.sum(p, axis=-1, keepdims=True)
                acc = acc + _dot(p.astype(BF16), v_ref[rows, :])
            near_lse = m + jnp.log(l)
            far_lse_col = fl_ref[mine, :1]
            lse = jnp.maximum(near_lse, far_lse_col)
            lse = lse + jnp.log(jnp.exp(near_lse - lse) + jnp.exp(far_lse_col - lse))
            o_ref[mine, :] = (acc * (jnp.exp(near_lse - lse) / l)
                              + fo_ref[mine, :].astype(F32) * jnp.exp(far_lse_col - lse)).astype(BF16)
            lse_ref[mine, :] = jnp.broadcast_to(lse, (ATT_BLOCK, HEAD_DIM))

    hh = n_heads
    blk = pl.BlockSpec((group * ATT_BLOCK, HEAD_DIM), lambda h, i: (i, h))
    table = pl.BlockSpec(dist.shape, lambda h, i: (0, 0, 0))
    return _call(
        body, name="attention_fwd", grid=(hh, nq // group),
        in_specs=[pl.BlockSpec(memory_space=pltpu.SMEM), blk,
                  pl.BlockSpec((s, HEAD_DIM), lambda h, i: (0, hh + h)),
                  pl.BlockSpec((s, HEAD_DIM), lambda h, i: (0, 2 * hh + h)), blk, blk, table, table],
        out_specs=[blk, blk],
        out_shape=[jax.ShapeDtypeStruct((s, hh * HEAD_DIM), BF16), jax.ShapeDtypeStruct((s, hh * HEAD_DIM), F32)],
        operands=[slopes, proj, proj, proj, far_out, far_lse, jnp.asarray(dist), jnp.asarray(logc)],
        scratch_shapes=[pltpu.VMEM((ATT_WINDOW + 1, ATT_BLOCK, ATT_BLOCK), F32),
                        pltpu.VMEM((group * nwin, ATT_BLOCK, ATT_BLOCK), F32)],
        semantics=("parallel", "arbitrary"), jobs=jobs)


def _attention_far_bwd(qkv, slopes, out, dout, lse, n_heads):
    s = qkv.shape[0]
    per_class = s // ATT_CLASSES
    scale = HEAD_DIM ** -0.5
    dist, logc = _far_bias_tables(per_class)

    def body(slope_ref, q_ref, k_ref, v_ref, o_ref, do_ref, lse_ref, dist_ref, logc_ref, dq_ref, dk_ref, dv_ref):
        bias = logc_ref[...] - slope_ref[pl.program_id(0)] * dist_ref[...]
        for a in range(ATT_FAR_GROUP):
            rows = pl.ds(a * per_class, per_class)
            q, k, do = q_ref[rows, :], k_ref[rows, :], do_ref[rows, :]
            delta = jnp.sum(do.astype(F32) * o_ref[rows, :].astype(F32), axis=-1, keepdims=True)
            p = jnp.exp(_dot(q, k, tb=True) * scale + bias - lse_ref[rows, :1])
            dv_ref[rows, :] = _dot(p.astype(BF16), do, ta=True).astype(BF16)
            ds = (p * (_dot(do, v_ref[rows, :], tb=True) - delta) * scale).astype(BF16)
            dk_ref[rows, :] = _dot(ds, q, ta=True).astype(BF16)
            dq_ref[rows, :] = _dot(ds, k).astype(BF16)

    hh = n_heads
    blk = pl.BlockSpec((ATT_FAR_GROUP * per_class, HEAD_DIM), lambda h, r: (r, h))
    table = pl.BlockSpec(dist.shape, lambda h, r: (0, 0))
    o_shape = jax.ShapeDtypeStruct((s, hh * HEAD_DIM), BF16)
    return pl.pallas_call(
        body, name="attention_far_bwd", grid=(hh, ATT_CLASSES // ATT_FAR_GROUP),
        in_specs=[pl.BlockSpec(memory_space=pltpu.SMEM), blk,
                  pl.BlockSpec((ATT_FAR_GROUP * per_class, HEAD_DIM), lambda h, r: (r, hh + h)),
                  pl.BlockSpec((ATT_FAR_GROUP * per_class, HEAD_DIM), lambda h, r: (r, 2 * hh + h)),
                  blk, blk, blk, table, table],
        out_specs=[blk] * 3, out_shape=[o_shape] * 3,
        compiler_params=_params(("parallel", "parallel")),
    )(slopes, qkv, qkv, qkv, out, dout, lse, jnp.asarray(dist), jnp.asarray(logc))


def _attention_bwd(proj, slopes, out, lse, dmixed, far_grads, n_heads, jobs=()):
    s = proj.shape[0]
    nq = s // ATT_BLOCK
    scale = HEAD_DIM ** -0.5
    dist, logc = _attention_bias_tables()

    nwin = min(ATT_WINDOW, nq)
    group = math.gcd(ATT_NEAR_GROUP, nq)

    def body(slope_ref, q_ref, k_ref, v_ref, o_ref, do_ref, lse_ref, fdq_ref, fdk_ref, fdv_ref, dist_ref, logc_ref,
             dq_ref, dk_ref, dv_ref, dk_acc, dv_acc, bias_ref):
        h, step = pl.program_id(0), pl.program_id(1)

        @pl.when(step == 0)
        def _():
            dk_acc[...] = jnp.zeros_like(dk_acc)
            dv_acc[...] = jnp.zeros_like(dv_acc)
            _head_bias(bias_ref, slope_ref[h], dist_ref, logc_ref)

        for a in range(group):
            i = step * group + a
            mine = pl.ds(a * ATT_BLOCK, ATT_BLOCK)
            q = q_ref[mine, :]
            do = do_ref[mine, :]
            lse_col = lse_ref[mine, :1]
            delta = jnp.sum(do.astype(F32) * o_ref[mine, :].astype(F32), axis=-1, keepdims=True)
            first = _window_start(i, nq, nwin)
            dq = jnp.zeros((ATT_BLOCK, HEAD_DIM), F32)
            for b in range(nwin):
                rows, kk = _window_block(first + b, i)
                kj = k_ref[rows, :]
                vj = v_ref[rows, :]
                p = jnp.exp(_dot(q, kj, tb=True) * scale + bias_ref[kk] - lse_col)
                dv_acc[rows, :] += _dot(p.astype(BF16), do, ta=True)
                dp = _dot(do, vj, tb=True)
                ds = (p * (dp - delta) * scale).astype(BF16)
                dk_acc[rows, :] += _dot(ds, q, ta=True)
                dq = dq + _dot(ds, kj)
            dq_ref[mine, :] = (dq + fdq_ref[mine, :].astype(F32)).astype(BF16)

        @pl.when(step == nq // group - 1)
        def _():
            dk_ref[...] = (dk_acc[...] + fdk_ref[...].astype(F32)).astype(BF16)
            dv_ref[...] = (dv_acc[...] + fdv_ref[...].astype(F32)).astype(BF16)

    hh = n_heads
    blk = pl.BlockSpec((group * ATT_BLOCK, HEAD_DIM), lambda h, i: (i, h))
    col = pl.BlockSpec((s, HEAD_DIM), lambda h, i: (0, h))
    table = pl.BlockSpec(dist.shape, lambda h, i: (0, 0, 0))
    o_shape = jax.ShapeDtypeStruct((s, hh * HEAD_DIM), BF16)
    return _call(
        body, name="attention_bwd", grid=(hh, nq // group),
        in_specs=[pl.BlockSpec(memory_space=pltpu.SMEM), blk,
                  pl.BlockSpec((s, HEAD_DIM), lambda h, i: (0, hh + h)),
                  pl.BlockSpec((s, HEAD_DIM), lambda h, i: (0, 2 * hh + h)),
                  blk, blk, blk, blk, col, col, table, table],
        out_specs=[blk, col, col], out_shape=[o_shape] * 3,
        operands=[slopes, proj, proj, proj, out, dmixed, lse, *far_grads, jnp.asarray(dist), jnp.asarray(logc)],
        scratch_shapes=[pltpu.VMEM((s, HEAD_DIM), F32)] * 2
        + [pltpu.VMEM((ATT_WINDOW + 1, ATT_BLOCK, ATT_BLOCK), F32)],
        semantics=("parallel", "arbitrary"), jobs=jobs)


def _ret_decays(lgc, lga, strict_c, strict_a):
    c = RET_CHUNK
    rel = (lax.broadcasted_iota(jnp.int32, (c, c), 0) - lax.broadcasted_iota(jnp.int32, (c, c), 1)).astype(F32)
    in_c = (rel > 0) if strict_c else (rel >= 0)
    in_a = (rel < 0) if strict_a else (rel <= 0)
    mask = (jnp.where(in_c, jnp.exp(lgc * jnp.maximum(rel, 0.0)), 0.0)
            + jnp.where(in_a, jnp.exp(lga * jnp.maximum(-rel, 0.0)), 0.0))
    idx = lax.broadcasted_iota(jnp.int32, (c, 1), 0).astype(F32)
    ones = jnp.ones((1, HEAD_DIM), F32)
    dec = dict(
        rel=rel, mask=mask, idx=idx,
        a_c=jnp.exp(lgc * (idx + 1.0)), b_c=jnp.exp(lgc * (c - 1.0 - idx)), chunk_c=jnp.exp(ones * (lgc * c)),
        a_a=jnp.exp(lga * (c - idx)), b_a=jnp.exp(lga * idx), chunk_a=jnp.exp(ones * (lga * c)),
    )
    return dec


def _scaled(x, col):
    return (x.astype(F32) * col).astype(BF16)


def _chunk_rows(i):
    return pl.ds(pl.multiple_of(i * RET_CHUNK, RET_CHUNK), RET_CHUNK)


def _chunk_loop(nc, step, init, unroll=RET_UNROLL):
    group = math.gcd(nc, unroll)

    def trip(t, carry):
        for u in range(group):
            carry = step(t * group + u, carry)
        return carry

    return lax.fori_loop(0, nc // group, trip, init)


def _retention(a, b, c, lg_c, lg_a, *, strict_c, strict_a, scale, n_heads, name, gate=None, norm_w=None, jobs=()):
    s = a[0].shape[0]
    nc = s // RET_CHUNK
    epilogue = gate is not None

    def body(*refs):
        lgc_ref, lga_ref, a_ref, b_ref, c_ref = refs[:5]
        if epilogue:
            g_ref, w_ref, o_ref, mix_ref, sa_ref = refs[5:]
        else:
            o_ref, sa_ref = refs[5:]
        h = pl.program_id(0)
        dec = _ret_decays(lgc_ref[h], lga_ref[h], strict_c, strict_a)

        def reverse(t, state):
            i = nc - 1 - t
            sa_ref[i] = state.astype(BF16)
            rows = _chunk_rows(i)
            return state * dec["chunk_a"] + _dot(_scaled(b_ref[rows, :], dec["b_a"]), c_ref[rows, :], ta=True)

        _chunk_loop(nc, reverse, jnp.zeros((HEAD_DIM, HEAD_DIM), F32))

        def forward(i, state):
            rows = _chunk_rows(i)
            ai, bi, ci = a_ref[rows, :], b_ref[rows, :], c_ref[rows, :]
            inner = (_dot(ai, bi, tb=True) * dec["mask"]).astype(BF16)
            out = (_dot(inner, ci) + _dot(_scaled(ai, dec["a_c"]), state.astype(BF16))
                   + _dot(_scaled(ai, dec["a_a"]), sa_ref[i])) * scale
            o_ref[rows, :] = out.astype(BF16)
            if epilogue:
                r = lax.rsqrt(jnp.mean(out * out, axis=-1, keepdims=True) + EPS)
                g = g_ref[rows, :].astype(F32)
                mix_ref[rows, :] = (out * r * w_ref[...] * (g * _sigmoid(g))).astype(BF16)
            return state * dec["chunk_c"] + _dot(_scaled(bi, dec["b_c"]), ci, ta=True)

        _chunk_loop(nc, forward, jnp.zeros((HEAD_DIM, HEAD_DIM), F32))

    def col(first):
        return pl.BlockSpec((s, HEAD_DIM), lambda h: (0, first + h))

    smem = pl.BlockSpec(memory_space=pltpu.SMEM)
    in_specs = [smem, smem, col(a[1]), col(b[1]), col(c[1])]
    operands = [lg_c, lg_a, a[0], b[0], c[0]]
    o_shape = jax.ShapeDtypeStruct((s, n_heads * HEAD_DIM), BF16)
    out_specs, out_shape = [col(0)], [o_shape]
    if epilogue:
        in_specs += [col(gate[1]), pl.BlockSpec((1, HEAD_DIM), lambda h: (0, h))]
        operands += [gate[0], norm_w]
        out_specs, out_shape = [col(0)] * 2, [o_shape] * 2
    res, carried = _call(
        body, name=name, grid=(n_heads,), in_specs=in_specs, out_specs=out_specs, out_shape=out_shape,
        operands=operands, scratch_shapes=[pltpu.VMEM((nc, HEAD_DIM, HEAD_DIM), BF16)],
        semantics=("parallel",), jobs=jobs)
    res = res if epilogue else res[0]
    return (res, carried) if jobs else res


def _retention_decay_grads(a, b, c, e, lg_c, lg_a, *, scale, n_heads):
    s = a[0].shape[0]
    nc = s // RET_CHUNK
    cf = float(RET_CHUNK)

    def body(lgc_ref, lga_ref, a_ref, b_ref, c_ref, e_ref, gc_ref, ga_ref, sa_ref, ta_ref):
        h = pl.program_id(0)
        lgc, lga = lgc_ref[h], lga_ref[h]
        dec = _ret_decays(lgc, lga, True, True)
        rel, idx = dec["rel"], dec["idx"]
        w_c = jnp.where(rel > 0, rel * jnp.exp(lgc * jnp.maximum(rel, 0.0)), 0.0)
        w_a = jnp.where(rel < 0, -rel * jnp.exp(lga * jnp.maximum(-rel, 0.0)), 0.0)
        zero = jnp.zeros((HEAD_DIM, HEAD_DIM), F32)

        def reverse(t, carry):
            st, dst = carry
            i = nc - 1 - t
            sa_ref[i] = st.astype(BF16)
            ta_ref[i] = dst.astype(BF16)
            rows = _chunk_rows(i)
            bi, ci = b_ref[rows, :], c_ref[rows, :]
            st_new = st * dec["chunk_a"] + _dot(_scaled(bi, dec["b_a"]), ci, ta=True)
            dst_new = (cf * st + dst) * dec["chunk_a"] + _dot(_scaled(bi, idx * dec["b_a"]), ci, ta=True)
            return st_new, dst_new

        _chunk_loop(nc, reverse, (zero, zero))

        def forward(i, carry):
            st, dst, acc_c, acc_a = carry
            rows = _chunk_rows(i)
            ai, bi, ci = a_ref[rows, :], b_ref[rows, :], c_ref[rows, :]
            ev = e_ref[rows, :].astype(F32)
            pg = _dot(ai, bi, tb=True) * _dot(e_ref[rows, :], ci, tb=True)
            a_c, a_a = _scaled(ai, dec["a_c"]), _scaled(ai, dec["a_a"])
            inter_c = _dot(a_c, st.astype(BF16)) * (idx + 1.0) + _dot(a_c, dst.astype(BF16))
            inter_a = _dot(a_a, sa_ref[i]) * (cf - idx) + _dot(a_a, ta_ref[i])
            acc_c = acc_c + jnp.sum(pg * w_c, axis=0, keepdims=True) + jnp.sum(inter_c * ev, axis=0, keepdims=True)
            acc_a = acc_a + jnp.sum(pg * w_a, axis=0, keepdims=True) + jnp.sum(inter_a * ev, axis=0, keepdims=True)
            st_new = st * dec["chunk_c"] + _dot(_scaled(bi, dec["b_c"]), ci, ta=True)
            dst_new = ((cf * st + dst) * dec["chunk_c"]
                       + _dot(_scaled(bi, (cf - 1.0 - idx) * dec["b_c"]), ci, ta=True))
            return st_new, dst_new, acc_c, acc_a

        row = jnp.zeros((1, HEAD_DIM), F32)
        _, _, acc_c, acc_a = _chunk_loop(nc, forward, (zero, zero, row, row))
        gc_ref[...] = jnp.broadcast_to(jnp.sum(acc_c, axis=-1, keepdims=True) * scale, gc_ref.shape)
        ga_ref[...] = jnp.broadcast_to(jnp.sum(acc_a, axis=-1, keepdims=True) * scale, ga_ref.shape)

    def col(first):
        return pl.BlockSpec((s, HEAD_DIM), lambda h: (0, first + h))

    smem = pl.BlockSpec(memory_space=pltpu.SMEM)
    o_spec = pl.BlockSpec((1, 8, HEAD_DIM), lambda h: (h, 0, 0))
    o_shape = jax.ShapeDtypeStruct((n_heads, 8, HEAD_DIM), F32)
    gc, ga = pl.pallas_call(
        body, name="retention_decay_grads", grid=(n_heads,),
        in_specs=[smem, smem, col(a[1]), col(b[1]), col(c[1]), col(e[1])],
        out_specs=[o_spec] * 2, out_shape=[o_shape] * 2,
        scratch_shapes=[pltpu.VMEM((nc, HEAD_DIM, HEAD_DIM), BF16)] * 2,
        compiler_params=_params(("parallel",)),
    )(lg_c, lg_a, a[0], b[0], c[0], e[0])
    return gc[:, 0, 0], ga[:, 0, 0]


def _ret_gate_bwd(dmixed, first_col, out, proj, gate_col, norm_w, n_heads):
    s = out.shape[0]
    tr = _row_block(s, 8 * HEAD_DIM)

    def body(dm_ref, o_ref, g_ref, w_ref, do_ref, dg_ref, dw_ref):
        dm = dm_ref[...].astype(F32)
        ov = o_ref[...].astype(F32)
        g = g_ref[...].astype(F32)
        w = w_ref[...]
        r = lax.rsqrt(jnp.mean(ov * ov, axis=-1, keepdims=True) + EPS)
        ohat = ov * r
        sg = _sigmoid(g)
        silu = g * sg
        dg_ref[...] = (dm * ohat * w * sg * (1.0 + g * (1.0 - sg))).astype(BF16)
        dohat = dm * w * silu
        do_ref[...] = (r * (dohat - ohat * jnp.mean(dohat * ohat, axis=-1, keepdims=True))).astype(BF16)

        @pl.when(pl.program_id(1) == 0)
        def _():
            dw_ref[...] = jnp.zeros_like(dw_ref)

        dw_ref[...] += jnp.sum(dm * ohat * silu, axis=0, keepdims=True)

    def blk(first):
        return pl.BlockSpec((tr, HEAD_DIM), lambda h, i: (i, first + h))

    vec = pl.BlockSpec((1, HEAD_DIM), lambda h, i: (0, h))
    o_shape = jax.ShapeDtypeStruct((s, n_heads * HEAD_DIM), BF16)
    return pl.pallas_call(
        body, name="ret_gate_bwd", grid=(n_heads, s // tr),
        in_specs=[blk(first_col), blk(0), blk(gate_col), vec],
        out_specs=[blk(0), blk(0), vec],
        out_shape=[o_shape, o_shape, jax.ShapeDtypeStruct((1, n_heads * HEAD_DIM), F32)],
        compiler_params=_params(("parallel", "arbitrary")),
    )(dmixed, out, proj, norm_w)


def _step(x, target, norm_mix_w, ret_decay_fwd, ret_decay_bwd, ret_norm_w, norm_ffn_w, norm_final_w, own, pos):
    d = x.shape[1]
    nh = d // (2 * HEAD_DIM)
    scale = HEAD_DIM ** -0.5
    slopes = jnp.exp2(-8.0 * jnp.arange(1, nh + 1, dtype=F32) / nh)
    lg_f = -jnp.exp(ret_decay_fwd)
    lg_b = -jnp.exp(ret_decay_bwd)
    q_r, k_r, v_r, g_r = 3 * nh, 4 * nh, 5 * nh, 6 * nh
    ax = BIG_AXIS

    def gather(names, arrays, stage):
        return _gather_job(arrays, [ax[k] for k in names], stage)

    def add_halves(k, g, received):
        return _add_halves(g, received, ax[k], pos, name="grad_add_halves_" + k)

    def sum_parts(k, g, received, parts):
        return _sum_chip_parts(g, received, parts, ax[k], pos, name="grad_sum_parts_" + k)

    (w_in,) = _run_jobs([gather(["w_in"], [own["w_in"]], "ici"), gather(["w_in"], [own["w_in"]], "d2d")],
                        name="all_gather_w_in")
    n1 = _rmsnorm_fwd(x, norm_mix_w, name="norm_mix_fwd")
    proj, [[w_gate]] = _matmul(n1, w_in, name="in_proj", out_dtype=BF16, tm=2048,
                               jobs=[gather(["w_gate"], [own["w_gate"]], "ici")])
    qkv_classes = _to_classes(proj[:, :3 * nh * HEAD_DIM])
    (ret, ret_mixed), [[w_gate], [w_out]] = _retention(
        (proj, q_r), (proj, k_r), (proj, v_r), lg_f, lg_b, strict_c=False, strict_a=True, scale=scale, n_heads=nh,
        name="retention_fwd", gate=(proj, g_r), norm_w=ret_norm_w,
        jobs=[gather(["w_gate"], [w_gate], "d2d"), gather(["w_out"], [own["w_out"]], "ici")])
    far_out, far_lse = _attention_far_fwd(qkv_classes, slopes, nh)
    (attn, lse), [[w_out], [w_up]] = _attention_fwd(
        proj, slopes, _from_classes(far_out), _from_classes(far_lse), nh,
        jobs=[gather(["w_out"], [w_out], "d2d"), gather(["w_up"], [own["w_up"]], "ici")])
    mixed = jnp.concatenate([attn, ret_mixed], axis=1)
    h1, [[w_up]] = _matmul(mixed, w_out, name="out_proj", residual=x, jobs=[gather(["w_up"], [w_up], "d2d")])
    n2 = _rmsnorm_fwd(h1, norm_ffn_w, name="norm_ffn_fwd")
    (gate, up, act), [[w_down]] = _swiglu_fwd(n2, w_gate, w_up, jobs=[gather(["w_down"], [own["w_down"]], "ici")])
    (w_down,) = _run_jobs([gather(["w_down"], [w_down], "d2d")], name="all_gather_w_down_sibling")
    h2 = _matmul(act, w_down, name="down_proj", residual=h1, tk=2816)
    dh2, dh2_b, d_norm_final, loss = _loss_head(h2, norm_final_w, target)

    dgate, dup = _swiglu_bwd_act(dh2_b, w_down, gate, up)
    g_down = _weight_grad(act, dh2_b, name="grad_w_down")
    g_gate, [[r_down]] = _weight_grad(n2, dgate, name="grad_w_gate", jobs=[_exchange_job([g_down], [ax["w_down"]])])
    s_down = add_halves("w_down", g_down, r_down)
    g_up, [[r_gate], [p_down]] = _weight_grad(
        n2, dup, name="grad_w_up",
        jobs=[_exchange_job([g_gate], [ax["w_gate"]]), _send_sums_job([s_down], [ax["w_down"]])])
    s_gate = add_halves("w_gate", g_gate, r_gate)
    h_down = sum_parts("w_down", g_down, r_down, p_down)
    dn2, [[r_up], [p_gate]] = _swiglu_bwd_in(
        dgate, dup, w_gate, w_up,
        jobs=[_exchange_job([g_up], [ax["w_up"]]), _send_sums_job([s_gate], [ax["w_gate"]])])
    s_up = add_halves("w_up", g_up, r_up)
    h_gate = sum_parts("w_gate", g_gate, r_gate, p_gate)
    dh1, dh1_b, d_norm_ffn = _rmsnorm_bwd(dn2, h1, norm_ffn_w, dh2, name="norm_ffn_bwd")

    dmixed, [[gr_down]] = _matmul(dh1_b, w_out, name="out_proj_bwd", tb=True, out_dtype=BF16,
                                  jobs=[_join_job([h_down], [ax["w_down"]])])
    far_in = [_to_classes(t) for t in (attn, dmixed[:, :nh * HEAD_DIM], lse)]
    g_out = _weight_grad(mixed, dh1_b, name="grad_w_out")
    d_ret, dg_r, d_ret_norm = _ret_gate_bwd(dmixed, nh, ret, proj, g_r, ret_norm_w, nh)
    far_grads = _attention_far_bwd(qkv_classes, slopes, *far_in, nh)
    far_grads = [_from_classes(t) for t in far_grads]
    dq_r, [[gr_gate]] = _retention(
        (d_ret, 0), (proj, v_r), (proj, k_r), lg_f, lg_b, strict_c=False, strict_a=True, scale=scale, n_heads=nh,
        name="retention_dq", jobs=[_join_job([h_gate], [ax["w_gate"]])])
    (dq_a, dk_a, dv_a), [[p_up], [r_out]] = _attention_bwd(
        proj, slopes, attn, lse, dmixed, far_grads, nh,
        jobs=[_send_sums_job([s_up], [ax["w_up"]]), _exchange_job([g_out], [ax["w_out"]])])
    s_out = add_halves("w_out", g_out, r_out)
    h_up = sum_parts("w_up", g_up, r_up, p_up)
    dv_r, [[p_out], [gr_up]] = _retention(
        (proj, k_r), (proj, q_r), (d_ret, 0), lg_b, lg_f, strict_c=True, strict_a=False, scale=scale, n_heads=nh,
        name="retention_dv", jobs=[_send_sums_job([s_out], [ax["w_out"]]), _join_job([h_up], [ax["w_up"]])])
    h_out = sum_parts("w_out", g_out, r_out, p_out)
    dk_r, [[gr_out]] = _retention(
        (proj, v_r), (d_ret, 0), (proj, q_r), lg_b, lg_f, strict_c=True, strict_a=False, scale=scale, n_heads=nh,
        name="retention_dk", jobs=[_join_job([h_out], [ax["w_out"]])])
    dlg_f, dlg_b = _retention_decay_grads((proj, q_r), (proj, k_r), (proj, v_r), (d_ret, 0), lg_f, lg_b,
                                          scale=scale, n_heads=nh)
    dproj = [dq_a, dk_a, dv_a, dq_r, dk_r, dv_r, dg_r]
    g_in = _weight_grad_pieces(n1, dproj, name="grad_w_in")
    (r_in,) = _run_jobs([_exchange_job([g_in], [ax["w_in"]])], name="grad_exchange_w_in")
    s_in = add_halves("w_in", g_in, r_in)
    dn1, [[p_in]] = _matmul_pieces_nt(dproj, w_in, name="in_proj_bwd", jobs=[_send_sums_job([s_in], [ax["w_in"]])])
    dx, _, d_norm_mix = _rmsnorm_bwd(dn1, x, norm_mix_w, dh1, name="norm_mix_bwd")
    h_in = sum_parts("w_in", g_in, r_in, p_in)
    (gr_in,) = _run_jobs([_join_job([h_in], [ax["w_in"]])], name="grad_join_w_in")

    small = dict(loss=loss[0, 0], norm_mix_w=d_norm_mix, ret_decay_fwd=dlg_f * lg_f, ret_decay_bwd=dlg_b * lg_b,
                 ret_norm_w=d_ret_norm, norm_ffn_w=d_norm_ffn, norm_final_w=d_norm_final)
    return dx, dict(w_in=gr_in, w_out=gr_out, w_gate=gr_gate, w_up=gr_up, w_down=gr_down), small


def _mesh_position():
    x, y, c = lax.axis_index("x"), lax.axis_index("y"), lax.axis_index("c")
    chips = [(1 - x, y), (x, 1 - y), (1 - x, 1 - y)]
    return x, y, c, chips


def _ds(start, size):
    if isinstance(start, int):
        return pl.ds(start, size)
    return pl.ds(pl.multiple_of(start * size, size), size)


def _region(ref, axis, shard, half, shard_size, half_size):
    along = slice(None) if shard is None else _ds(shard, shard_size)
    other = slice(None) if half is None else _ds(half, half_size)
    return ref.at[other, along] if axis == 1 else ref.at[along, other]


def _gather_job(full, axes, stage):
    n = len(full)

    def copies(refs, sems):
        send_sem, recv_sem = sems
        x, y, c, chips = _mesh_position()
        me = 2 * x + y

        def copy(w, k, shard, half, target):
            rows_cols = full[w].shape
            place = _region(refs[w], axes[w], shard, half, rows_cols[axes[w]] // N_CHIPS, rows_cols[1 - axes[w]] // 2)
            return pltpu.make_async_remote_copy(
                src_ref=place, dst_ref=place, send_sem=send_sem.at[w, k], recv_sem=recv_sem.at[w, k],
                device_id=target, device_id_type=MESH)

        def sent(w, k):
            if stage == "ici":
                return copy(w, k, me, c, (chips[k][0], chips[k][1], c))
            return copy(w, k, 2 * chips[k][0] + chips[k][1], c, (x, y, 1 - c))

        def landed(w, k):
            return copy(w, k, 2 * chips[k][0] + chips[k][1], c if stage == "ici" else 1 - c, (x, y, 1 - c))

        return sent, landed

    def start(refs, sems):
        sent, _ = copies(refs, sems)
        for w in range(n):
            for k in range(3):
                sent(w, k).start()

    def finish(refs, sems):
        sent, landed = copies(refs, sems)
        for w in range(n):
            for k in range(3):
                landed(w, k).wait_recv()
                sent(w, k).wait_send()

    return _Job(ios=full, sems=[pltpu.SemaphoreType.DMA((n, 3))] * 2, start=start, finish=finish)


def _exchange_job(grads, axes):
    n = len(grads)

    def half_shape(w):
        return tuple(d // 2 if a != axes[w] else d for a, d in enumerate(grads[w].shape))

    def copy(refs, sems, w):
        x, y, c, _ = _mesh_position()
        return pltpu.make_async_remote_copy(
            src_ref=_region(refs[w], axes[w], None, 1 - c, 0, half_shape(w)[1 - axes[w]]), dst_ref=refs[n + w],
            send_sem=sems[0].at[w], recv_sem=sems[1].at[w], device_id=(x, y, 1 - c), device_id_type=MESH)

    def start(refs, sems):
        for w in range(n):
            copy(refs, sems, w).start()

    def finish(refs, sems):
        for w in range(n):
            copy(refs, sems, w).wait()

    return _Job(ins=grads, outs=[jax.ShapeDtypeStruct(half_shape(w), F32) for w in range(n)],
                sems=[pltpu.SemaphoreType.DMA((n,))] * 2, start=start, finish=finish)


def _half_block_spec(axis, block, half_blocks, use_half):
    if axis == 1:
        if use_half:
            return pl.BlockSpec(block, lambda i, pos: (pos[0] * half_blocks + i, 0))
        return pl.BlockSpec(block, lambda i, pos: (i, 0))
    if use_half:
        return pl.BlockSpec(block, lambda i, pos: (i, pos[0]))
    return pl.BlockSpec(block, lambda i, pos: (i, 0))


def _add_halves(grad, received, axis, pos, *, name):
    rows, cols = received.shape
    tr = _row_block(rows, cols)
    nb = rows // tr

    def body(pos_ref, g_ref, r_ref, o_ref):
        o_ref[...] = (g_ref[...] + r_ref[...]).astype(BF16)

    blk = (tr, cols)
    return pl.pallas_call(
        body, name=name, out_shape=jax.ShapeDtypeStruct((rows, cols), BF16),
        grid_spec=pltpu.PrefetchScalarGridSpec(
            num_scalar_prefetch=1, grid=(nb,),
            in_specs=[_half_block_spec(axis, blk, nb, True), _half_block_spec(axis, blk, nb, False)],
            out_specs=_half_block_spec(axis, blk, nb, False)),
        compiler_params=_params(("parallel",)),
    )(pos, grad, received)


def _send_sums_job(sums, axes):
    n = len(sums)

    def part_shape(w):
        return tuple(d // N_CHIPS if a == axes[w] else d for a, d in enumerate(sums[w].shape))

    def copy(refs, sems, w, k):
        x, y, c, chips = _mesh_position()
        shard = 2 * chips[k][0] + chips[k][1]
        return pltpu.make_async_remote_copy(
            src_ref=_region(refs[w], axes[w], shard, None, part_shape(w)[axes[w]], 0), dst_ref=refs[n + w].at[k],
            send_sem=sems[0].at[w, k], recv_sem=sems[1].at[w, k],
            device_id=(chips[k][0], chips[k][1], c), device_id_type=MESH)

    def start(refs, sems):
        for w in range(n):
            for k in range(3):
                copy(refs, sems, w, k).start()

    def finish(refs, sems):
        for w in range(n):
            for k in range(3):
                copy(refs, sems, w, k).wait()

    return _Job(ins=sums, outs=[jax.ShapeDtypeStruct((3,) + part_shape(w), BF16) for w in range(n)],
                sems=[pltpu.SemaphoreType.DMA((n, 3))] * 2, start=start, finish=finish)


def _sum_chip_parts(grad, received, parts, axis, pos, *, name):
    _, rows, cols = parts.shape
    tr = _row_block(rows, cols)
    nb = rows // tr
    blk = (tr, cols)

    def body(pos_ref, g_ref, r_ref, p_ref, o_ref):
        total = g_ref[...] + r_ref[...]
        for k in range(3):
            total = total + p_ref[k].astype(F32)
        o_ref[...] = total

    if axis == 1:
        g_spec = pl.BlockSpec(blk, lambda i, pos: (pos[0] * nb + i, pos[1]))
        r_spec = pl.BlockSpec(blk, lambda i, pos: (i, pos[1]))
        o_spec = pl.BlockSpec(blk, lambda i, pos: (pos[0] * nb + i, 0))
        shard_shape = (2 * rows, cols)
    else:
        g_spec = pl.BlockSpec(blk, lambda i, pos: (pos[1] * nb + i, pos[0]))
        r_spec = pl.BlockSpec(blk, lambda i, pos: (pos[1] * nb + i, 0))
        o_spec = pl.BlockSpec(blk, lambda i, pos: (i, pos[0]))
        shard_shape = (rows, 2 * cols)
    return pl.pallas_call(
        body, name=name, out_shape=jax.ShapeDtypeStruct(shard_shape, F32),
        grid_spec=pltpu.PrefetchScalarGridSpec(
            num_scalar_prefetch=1, grid=(nb,),
            in_specs=[g_spec, r_spec, pl.BlockSpec((3,) + blk, lambda i, pos: (0, i, 0))],
            out_specs=o_spec),
        compiler_params=_params(("parallel",)),
    )(pos, grad, received, parts)


def _join_job(shards, axes):
    n = len(shards)

    def copy(refs, sems, w, other):
        x, y, c, _ = _mesh_position()
        place = _region(refs[w], axes[w], None, 1 - c if other else c, 0, shards[w].shape[1 - axes[w]] // 2)
        return pltpu.make_async_remote_copy(
            src_ref=place, dst_ref=place, send_sem=sems[0].at[w], recv_sem=sems[1].at[w],
            device_id=(x, y, 1 - c), device_id_type=MESH)

    def start(refs, sems):
        for w in range(n):
            copy(refs, sems, w, False).start()

    def finish(refs, sems):
        for w in range(n):
            copy(refs, sems, w, True).wait_recv()
            copy(refs, sems, w, False).wait_send()

    return _Job(ios=shards, sems=[pltpu.SemaphoreType.DMA((n,))] * 2, start=start, finish=finish)


def _all_reduce_small(vec):
    rows, cols = vec.shape

    def body(v_ref, o_ref, land_ref, send_sem, recv_sem):
        x, y, c, _ = _mesh_position()
        me = 4 * x + 2 * y + c
        land_ref[me] = v_ref[...]
        copies = []
        for k in range(1, 8):
            px, py, pc = x ^ (k >> 2), y ^ ((k >> 1) & 1), c ^ (k & 1)
            copies.append(pltpu.make_async_remote_copy(
                src_ref=v_ref, dst_ref=land_ref.at[me], send_sem=send_sem.at[k], recv_sem=recv_sem.at[k],
                device_id=(px, py, pc), device_id_type=MESH))
        for cp in copies:
            cp.start()
        for k in range(1, 8):
            peer = me ^ k
            pltpu.make_async_remote_copy(
                src_ref=v_ref, dst_ref=land_ref.at[peer], send_sem=send_sem.at[k], recv_sem=recv_sem.at[k],
                device_id=(x, y, c), device_id_type=MESH).wait_recv()
        for cp in copies:
            cp.wait_send()
        total = land_ref[0]
        for k in range(1, 8):
            total = total + land_ref[k]
        o_ref[...] = total

    vmem = pl.BlockSpec(memory_space=pltpu.VMEM)
    return pl.pallas_call(
        body, name="all_reduce_small", in_specs=[vmem], out_specs=vmem,
        out_shape=jax.ShapeDtypeStruct((rows, cols), F32),
        scratch_shapes=[pltpu.VMEM((8, rows, cols), F32), pltpu.SemaphoreType.DMA((8,)), pltpu.SemaphoreType.DMA((8,))],
    )(vec)


def _adamw(w, g, m, v, *, name):
    rows, cols = w.shape
    tr = _row_block(rows, cols) if rows % 8 == 0 else rows
    bc1 = 1.0 - ADAM_B1 ** ADAM_STEP
    bc2 = 1.0 - ADAM_B2 ** ADAM_STEP

    def body(w_ref, g_ref, m_ref, v_ref, go_ref, d_ref, mo_ref, vo_ref):
        gv = g_ref[...]
        go_ref[...] = gv
        mn = ADAM_B1 * m_ref[...] + (1.0 - ADAM_B1) * gv
        vn = ADAM_B2 * v_ref[...] + (1.0 - ADAM_B2) * (gv * gv)
        mo_ref[...] = mn
        vo_ref[...] = vn
        d_ref[...] = -ADAM_LR * ((mn / bc1) / (jnp.sqrt(vn / bc2) + ADAM_EPS) + ADAM_WD * w_ref[...])

    blk = pl.BlockSpec((tr, cols), lambda i: (i, 0))
    shape = jax.ShapeDtypeStruct((rows, cols), F32)
    return pl.pallas_call(
        body, name=name, grid=(rows // tr,), in_specs=[blk] * 4, out_specs=[blk] * 4, out_shape=[shape] * 4,
        compiler_params=_params(("parallel",)),
    )(w, g, m, v)


def _to_bf16_in_place(w, axis, pos, *, name):
    rows, cols = w.shape
    tr = _row_block(rows, cols)
    nb = rows // tr

    def body(pos_ref, w_ref, o_ref):
        o_ref[...] = w_ref[...].astype(BF16)

    if axis == 1:
        o_spec = pl.BlockSpec((tr, cols), lambda i, pos: (i, pos[1]))
        full_shape = (rows, N_CHIPS * cols)
    else:
        o_spec = pl.BlockSpec((tr, cols), lambda i, pos: (pos[1] * nb + i, 0))
        full_shape = (N_CHIPS * rows, cols)
    return pl.pallas_call(
        body, name=name, out_shape=jax.ShapeDtypeStruct(full_shape, BF16),
        grid_spec=pltpu.PrefetchScalarGridSpec(
            num_scalar_prefetch=1, grid=(nb,),
            in_specs=[pl.BlockSpec((tr, cols), lambda i, pos: (i, 0))], out_specs=o_spec),
        compiler_params=_params(("parallel",)),
    )(pos, w)


BIG = ("w_in", "w_out", "w_gate", "w_up", "w_down")
BIG_AXIS = dict(w_in=1, w_out=0, w_gate=1, w_up=1, w_down=0)
SMALL = ("norm_mix_w", "ret_decay_fwd", "ret_decay_bwd", "ret_norm_w", "norm_ffn_w", "norm_final_w")
ALL_WEIGHTS = ("norm_mix_w", "w_in", "ret_decay_fwd", "ret_decay_bwd", "ret_norm_w", "w_out", "norm_ffn_w",
               "w_gate", "w_up", "w_down", "norm_final_w")
SMALL_ROW = 128 * 8


def _pack_small(small):
    pieces = [jnp.reshape(small["loss"], (1,))] + [jnp.reshape(small[k], (-1,)) for k in SMALL]
    rows = []
    for p in pieces:
        pad = -p.shape[0] % (8 * SMALL_ROW)
        rows.append(jnp.reshape(jnp.pad(p, (0, pad)), (-1, SMALL_ROW)))
    return jnp.concatenate(rows, axis=0)


def _unpack_small(block, like):
    out, row = {}, 0
    for k in ("loss",) + SMALL:
        size = 1 if k == "loss" else like[k].size
        nrows = -(-size // (8 * SMALL_ROW)) * 8
        out[k] = jnp.reshape(block[row:row + nrows], (-1,))[:size]
        row += nrows
    return out


def kernel(x, norm_mix_w, w_in, ret_decay_fwd, ret_decay_bwd, ret_norm_w, w_out, norm_ffn_w, w_gate, w_up, w_down, norm_final_w, loss_target, m_norm_mix_w, m_w_in, m_ret_decay_fwd, m_ret_decay_bwd, m_ret_norm_w, m_w_out, m_norm_ffn_w, m_w_gate, m_w_up, m_w_down, m_norm_final_w, v_norm_mix_w, v_w_in, v_ret_decay_fwd, v_ret_decay_bwd, v_ret_norm_w, v_w_out, v_norm_ffn_w, v_w_gate, v_w_up, v_w_down, v_norm_final_w):
    weights = dict(norm_mix_w=norm_mix_w, w_in=w_in, ret_decay_fwd=ret_decay_fwd, ret_decay_bwd=ret_decay_bwd,
                   ret_norm_w=ret_norm_w, w_out=w_out, norm_ffn_w=norm_ffn_w, w_gate=w_gate, w_up=w_up,
                   w_down=w_down, norm_final_w=norm_final_w)
    m_in = dict(norm_mix_w=m_norm_mix_w, w_in=m_w_in, ret_decay_fwd=m_ret_decay_fwd, ret_decay_bwd=m_ret_decay_bwd,
                ret_norm_w=m_ret_norm_w, w_out=m_w_out, norm_ffn_w=m_norm_ffn_w, w_gate=m_w_gate, w_up=m_w_up,
                w_down=m_w_down, norm_final_w=m_norm_final_w)
    v_in = dict(norm_mix_w=v_norm_mix_w, w_in=v_w_in, ret_decay_fwd=v_ret_decay_fwd, ret_decay_bwd=v_ret_decay_bwd,
                ret_norm_w=v_ret_norm_w, w_out=v_w_out, norm_ffn_w=v_norm_ffn_w, w_gate=v_w_gate, w_up=v_w_up,
                w_down=v_w_down, norm_final_w=v_norm_final_w)
    pos = jnp.stack([lax.axis_index("c"), 2 * lax.axis_index("x") + lax.axis_index("y")]).astype(jnp.int32)

    own = {k: _to_bf16_in_place(weights[k][0], BIG_AXIS[k], pos, name="cast_" + k) for k in BIG}

    dx, grad_w, small = _step(
        x[0], loss_target[0], norm_mix_w, ret_decay_fwd[0], ret_decay_bwd[0], ret_norm_w, norm_ffn_w,
        norm_final_w[None, :], own, pos)

    like = {k: weights[k] for k in SMALL}
    reduced = _unpack_small(_all_reduce_small(_pack_small(small)), like)
    loss = reduced["loss"][0]
    for k in SMALL:
        grad_w[k] = jnp.reshape(reduced[k], (1, -1))

    delta, new_m, new_v = {}, {}, {}
    for k in ALL_WEIGHTS:
        shape = weights[k].shape
        as2d = (lambda t: jnp.reshape(t, (-1, shape[-1])))
        grad_w[k], delta[k], new_m[k], new_v[k] = (jnp.reshape(t, shape) for t in _adamw(
            as2d(weights[k]), as2d(grad_w[k]), as2d(m_in[k]), as2d(v_in[k]), name="adamw_" + k))

    return (loss, dx[None], *[grad_w[k] for k in ALL_WEIGHTS], *[delta[k] for k in ALL_WEIGHTS],
            *[new_m[k] for k in ALL_WEIGHTS], *[new_v[k] for k in ALL_WEIGHTS])
```

```python
import functools
import math

import numpy as np
import jax
import jax.numpy as jnp
from jax import lax
from jax.experimental import pallas as pl
from jax.experimental.pallas import tpu as pltpu

F32 = jnp.float32
BF16 = jnp.bfloat16
MESH = pl.DeviceIdType.MESH

HEAD_DIM = 128
RET_CHUNK = 128
RET_UNROLL = 8
EPS = 1e-6
DILATED_PATTERNS = ((128, 1), (512, 4), (2048, 16))
ATT_BLOCK = 256
ATT_REACH = max(w // 2 for w, _ in DILATED_PATTERNS)
ATT_NEAR = ATT_BLOCK
ATT_CLASSES = DILATED_PATTERNS[-1][1]
assert all(w // 2 <= ATT_NEAR for w, _ in DILATED_PATTERNS[:-1])
ATT_KB = -(-ATT_NEAR // ATT_BLOCK)
ATT_WINDOW = 2 * ATT_KB + 1
ATT_FAR_GROUP = 8
ATT_NEAR_GROUP = 4
MASKED = -1e30
ROW_MAX_INIT = -1e29
N_CHIPS = 4
VMEM_LIMIT_BYTES = 56 * 1024 * 1024
ELEM_BLOCK_BYTES = 2 * 1024 * 1024

ADAM_LR = 0.001
ADAM_B1 = 0.9
ADAM_B2 = 0.999
ADAM_EPS = 1e-08
ADAM_WD = 0.01
ADAM_STEP = 10


def _params(sem=None):
    return pltpu.CompilerParams(dimension_semantics=sem, vmem_limit_bytes=VMEM_LIMIT_BYTES)


def _sigmoid(x):
    return 0.5 * jnp.tanh(0.5 * x) + 0.5


class _Job:
    def __init__(self, *, ins=(), ios=(), outs=(), sems=(), start, finish):
        self.ins, self.ios, self.outs, self.sems = list(ins), list(ios), list(outs), list(sems)
        self.start, self.finish = start, finish

    def results(self):
        return [jax.ShapeDtypeStruct(a.shape, a.dtype) for a in self.ios] + self.outs


def _call(body, *, name, grid, in_specs, out_specs, out_shape, operands, scratch_shapes=(), semantics=None, jobs=()):
    in_specs, out_specs, out_shape = list(in_specs), list(out_specs), list(out_shape)
    scratch_shapes = list(scratch_shapes)
    if not jobs:
        outs = pl.pallas_call(body, name=name, grid=grid, in_specs=in_specs, out_specs=out_specs, out_shape=out_shape,
                              scratch_shapes=scratch_shapes, compiler_params=_params(semantics))(*operands)
        return outs, []
    n_in, n_out, n_scratch = len(in_specs), len(out_specs), len(scratch_shapes)
    extra_in, extra_out, sems, aliases = [], [], [], {}
    for job in jobs:
        extra_in += job.ins
        for t in range(len(job.ios)):
            aliases[n_in + len(extra_in) + t] = n_out + len(extra_out) + t
        extra_in += job.ios
        extra_out += job.results()
        sems += job.sems

    def carried(*refs):
        x_in = refs[n_in:n_in + len(extra_in)]
        x_out = refs[n_in + len(extra_in) + n_out:n_in + len(extra_in) + n_out + len(extra_out)]
        x_sem = refs[len(refs) - len(sems):]
        views, i_in, i_out, i_sem = [], 0, 0, 0
        for job in jobs:
            data = list(x_in[i_in:i_in + len(job.ins)]) + list(x_out[i_out:i_out + len(job.results())])
            views.append((data, x_sem[i_sem:i_sem + len(job.sems)]))
            i_in += len(job.ins) + len(job.ios)
            i_out += len(job.results())
            i_sem += len(job.sems)
        steps = [pl.program_id(d) for d in range(len(grid))]

        @pl.when(functools.reduce(jnp.logical_and, [s == 0 for s in steps]))
        def _():
            for job, (data, sem) in zip(jobs, views):
                job.start(data, sem)

        body(*refs[:n_in], *refs[n_in + len(extra_in):n_in + len(extra_in) + n_out],
             *refs[len(refs) - len(sems) - n_scratch:len(refs) - len(sems)])

        @pl.when(functools.reduce(jnp.logical_and, [s == g - 1 for s, g in zip(steps, grid)]))
        def _():
            for job, (data, sem) in zip(jobs, views):
                job.finish(data, sem)

    hbm = pl.BlockSpec(memory_space=pl.ANY)
    res = pl.pallas_call(
        carried, name=name, grid=grid, in_specs=in_specs + [hbm] * len(extra_in),
        out_specs=out_specs + [hbm] * len(extra_out), out_shape=out_shape + extra_out,
        input_output_aliases=aliases, scratch_shapes=scratch_shapes + sems,
        compiler_params=_params(("arbitrary",) * len(grid)),
    )(*operands, *extra_in)
    carried_results, at = [], n_out
    for job in jobs:
        carried_results.append(list(res[at:at + len(job.results())]))
        at += len(job.results())
    return list(res[:n_out]), carried_results


def _run_jobs(jobs, *, name):
    first = jobs[0]
    n_in, n_io = len(first.ins), len(first.ios)
    out_shape = first.results()
    n_sems = [len(job.sems) for job in jobs]

    def body(*refs):
        data = list(refs[:n_in]) + list(refs[n_in + n_io:n_in + n_io + len(out_shape)])
        at = n_in + n_io + len(out_shape)
        for job, ns in zip(jobs, n_sems):
            job.start(data, refs[at:at + ns])
            job.finish(data, refs[at:at + ns])
            at += ns

    hbm = pl.BlockSpec(memory_space=pl.ANY)
    return pl.pallas_call(
        body, name=name, in_specs=[hbm] * (n_in + n_io), out_specs=[hbm] * len(out_shape), out_shape=out_shape,
        input_output_aliases={n_in + t: t for t in range(n_io)},
        scratch_shapes=[s for job in jobs for s in job.sems],
    )(*first.ins, *first.ios)


def _dot(a, b, ta=False, tb=False):
    return lax.dot_general(a, b, (((0 if ta else 1,), (1 if tb else 0,)), ((), ())),
                           preferred_element_type=F32)


def _tile(n, want):
    t = min(n, want) // 128 * 128
    while n % t:
        t -= 128
    return t


def _a_spec(ta, tm, tk):
    return pl.BlockSpec((tk, tm), lambda i, j, k: (k, i)) if ta else pl.BlockSpec((tm, tk), lambda i, j, k: (i, k))


def _b_spec(tb, tk, tn):
    return pl.BlockSpec((tn, tk), lambda i, j, k: (j, k)) if tb else pl.BlockSpec((tk, tn), lambda i, j, k: (k, j))


def _accumulate(accs, nk, products, finish):
    if nk == 1:
        finish(*products())
        return
    k = pl.program_id(2)

    @pl.when(k == 0)
    def _():
        for acc, p in zip(accs, products()):
            acc[...] = p

    if nk > 2:
        @pl.when(jnp.logical_and(k > 0, k < nk - 1))
        def _():
            for acc, p in zip(accs, products()):
                acc[...] += p

    @pl.when(k == nk - 1)
    def _():
        finish(*[acc[...] + p for acc, p in zip(accs, products())])


def _matmul(a, b, *, name, ta=False, tb=False, out_dtype=F32, residual=None, tm=1024, tn=1024, tk=2048, jobs=()):
    m, kdim = (a.shape[1], a.shape[0]) if ta else a.shape
    n = b.shape[0] if tb else b.shape[1]
    tm, tn, tk = _tile(m, tm), _tile(n, tn), _tile(kdim, tk)
    nk = kdim // tk

    def body(*refs):
        a_ref, b_ref = refs[:2]
        r_ref = refs[2] if residual is not None else None
        o_ref = refs[-1] if nk == 1 else refs[-2]

        def finish(total):
            if residual is not None:
                total = total + r_ref[...]
            o_ref[...] = total.astype(out_dtype)

        _accumulate(refs[-1:] if nk > 1 else (), nk, lambda: (_dot(a_ref[...], b_ref[...], ta, tb),), finish)

    o_spec = pl.BlockSpec((tm, tn), lambda i, j, k: (i, j))
    in_specs = [_a_spec(ta, tm, tk), _b_spec(tb, tk, tn)]
    operands = [a, b]
    if residual is not None:
        in_specs.append(o_spec)
        operands.append(residual)
    (out,), carried = _call(
        body, name=name, grid=(m // tm, n // tn, nk), in_specs=in_specs, out_specs=[o_spec],
        out_shape=[jax.ShapeDtypeStruct((m, n), out_dtype)], operands=operands,
        scratch_shapes=[pltpu.VMEM((tm, tn), F32)] * (nk > 1),
        semantics=("parallel", "parallel", "arbitrary"), jobs=jobs)
    return (out, carried) if jobs else out


def _matmul_pieces_nt(pieces, b, *, name, tm=512, tn=1024, jobs=()):
    m, kp = pieces[0].shape
    n = b.shape[0]
    tm, tn = _tile(m, tm), _tile(n, tn)
    count = len(pieces)

    def body(*refs):
        b_ref, o_ref = refs[count], refs[count + 1]
        total = _dot(refs[0][...], b_ref[:, pl.ds(0, kp)], tb=True)
        for p in range(1, count):
            total = total + _dot(refs[p][...], b_ref[:, pl.ds(p * kp, kp)], tb=True)
        o_ref[...] = total

    piece = pl.BlockSpec((tm, kp), lambda j, i: (i, 0))
    (out,), carried = _call(
        body, name=name, grid=(n // tn, m // tm),
        in_specs=[piece] * count + [pl.BlockSpec((tn, count * kp), lambda j, i: (j, 0))],
        out_specs=[pl.BlockSpec((tm, tn), lambda j, i: (i, j))],
        out_shape=[jax.ShapeDtypeStruct((m, n), F32)], operands=[*pieces, b],
        semantics=("parallel", "parallel"), jobs=jobs)
    return (out, carried) if jobs else out


def _weight_grad_pieces(a, pieces, *, name):
    tokens, m = a.shape
    np_ = pieces[0].shape[1]
    tm = 1024 if m % 1024 == 0 else _tile(m, 1408)
    tn = _tile(np_, 512)
    nb = np_ // tn
    out = None
    for p, piece in enumerate(pieces):
        def body(*refs):
            refs[-1][...] = _dot(refs[0][...], refs[1][...], ta=True)

        in_specs = [pl.BlockSpec((tokens, tm), lambda i, j: (0, i)), pl.BlockSpec((tokens, tn), lambda i, j: (0, j))]
        operands = [a, piece]
        if out is not None:
            in_specs.append(pl.BlockSpec(memory_space=pl.ANY))
            operands.append(out)
        out = pl.pallas_call(
            body, name="%s_%d" % (name, p), grid=(m // tm, nb), in_specs=in_specs,
            out_specs=pl.BlockSpec((tm, tn), lambda i, j, p=p: (i, p * nb + j)),
            out_shape=jax.ShapeDtypeStruct((m, len(pieces) * np_), F32),
            input_output_aliases={2: 0} if len(operands) == 3 else {},
            compiler_params=_params(("parallel", "parallel")),
        )(*operands)
    return out


def _weight_grad(a, g, *, name, jobs=()):
    tokens, m = a.shape
    tm = 1024 if m % 1024 == 0 else _tile(m, 1408)
    return _matmul(a, g, name=name, ta=True, tm=tm, tn=512, tk=tokens, jobs=jobs)


def _swiglu_fwd(n2, w_gate, w_up, *, tm=1024, tn=512, tk=2048, jobs=()):
    m, kdim = n2.shape
    n = w_gate.shape[1]
    tm, tn, tk = _tile(m, tm), _tile(n, tn), _tile(kdim, tk)
    nk = kdim // tk

    def body(a_ref, g_ref, u_ref, gate_ref, up_ref, act_ref, *acc):
        def products():
            a = a_ref[...]
            return _dot(a, g_ref[...]), _dot(a, u_ref[...])

        def finish(g, u):
            gate_ref[...] = g.astype(BF16)
            up_ref[...] = u.astype(BF16)
            act_ref[...] = (g * _sigmoid(g) * u).astype(BF16)

        _accumulate(acc, nk, products, finish)

    o_spec = pl.BlockSpec((tm, tn), lambda i, j, k: (i, j))
    o_shape = jax.ShapeDtypeStruct((m, n), BF16)
    return _call(
        body, name="swiglu_fwd", grid=(m // tm, n // tn, nk),
        in_specs=[_a_spec(False, tm, tk), _b_spec(False, tk, tn), _b_spec(False, tk, tn)],
        out_specs=[o_spec] * 3, out_shape=[o_shape] * 3, operands=[n2, w_gate, w_up],
        scratch_shapes=[pltpu.VMEM((tm, tn), F32)] * (2 * (nk > 1)),
        semantics=("parallel", "parallel", "arbitrary"), jobs=jobs)


def _swiglu_bwd_act(dh2, w_down, gate, up, *, tm=1024, tn=512, tk=2048):
    m, kdim = dh2.shape
    n = w_down.shape[0]
    tm, tn, tk = _tile(m, tm), _tile(n, tn), _tile(kdim, tk)
    nk = kdim // tk

    sub = _tile(tn, 256)

    def body(a_ref, b_ref, gate_ref, up_ref, dgate_ref, dup_ref, *acc):
        def finish(dact, cols=slice(None)):
            g = gate_ref[:, cols].astype(F32)
            u = up_ref[:, cols].astype(F32)
            sg = _sigmoid(g)
            dup_ref[:, cols] = (dact * g * sg).astype(BF16)
            dgate_ref[:, cols] = (dact * u * sg * (1.0 + g * (1.0 - sg))).astype(BF16)

        if nk == 1:
            a = a_ref[...]
            for c in range(tn // sub):
                cols = pl.ds(c * sub, sub)
                finish(_dot(a, b_ref[cols, :], tb=True), cols)
        else:
            _accumulate(acc, nk, lambda: (_dot(a_ref[...], b_ref[...], tb=True),), finish)

    o_spec = pl.BlockSpec((tm, tn), lambda i, j, k: (i, j))
    o_shape = jax.ShapeDtypeStruct((m, n), BF16)
    return pl.pallas_call(
        body, name="swiglu_bwd_act", grid=(m // tm, n // tn, nk),
        in_specs=[_a_spec(False, tm, tk), _b_spec(True, tk, tn), o_spec, o_spec],
        out_specs=[o_spec] * 2, out_shape=[o_shape] * 2,
        scratch_shapes=[pltpu.VMEM((tm, tn), F32)] * (nk > 1),
        compiler_params=_params(("parallel", "parallel", "arbitrary")),
    )(dh2, w_down, gate, up)


def _swiglu_bwd_in(dgate, dup, w_gate, w_up, *, tm=1024, tn=1024, tk=1408, jobs=()):
    m, kdim = dgate.shape
    n = w_gate.shape[0]
    tm, tn, tk = _tile(m, tm), _tile(n, tn), _tile(kdim, tk)
    nk = kdim // tk

    def body(a1_ref, a2_ref, b1_ref, b2_ref, o_ref, *acc):
        def product():
            return (_dot(a1_ref[...], b1_ref[...], tb=True) + _dot(a2_ref[...], b2_ref[...], tb=True),)

        def finish(total):
            o_ref[...] = total

        _accumulate(acc, nk, product, finish)

    a_spec, b_spec = _a_spec(False, tm, tk), _b_spec(True, tk, tn)
    (out,), carried = _call(
        body, name="swiglu_bwd_in", grid=(m // tm, n // tn, nk),
        in_specs=[a_spec, a_spec, b_spec, b_spec],
        out_specs=[pl.BlockSpec((tm, tn), lambda i, j, k: (i, j))],
        out_shape=[jax.ShapeDtypeStruct((m, n), F32)], operands=[dgate, dup, w_gate, w_up],
        scratch_shapes=[pltpu.VMEM((tm, tn), F32)] * (nk > 1),
        semantics=("parallel", "parallel", "arbitrary"), jobs=jobs)
    return out, carried


def _row_block(rows, cols):
    tr = min(rows, max(16, ELEM_BLOCK_BYTES // (4 * cols) // 16 * 16))
    while rows % tr:
        tr -= 16
    return tr


def _rmsnorm_fwd(x, g, *, name):
    s, d = x.shape
    tr = _row_block(s, d)

    def body(x_ref, g_ref, n_ref):
        xv = x_ref[...]
        r = lax.rsqrt(jnp.mean(xv * xv, axis=-1, keepdims=True) + EPS)
        n_ref[...] = (xv * r * g_ref[...]).astype(BF16)

    row = pl.BlockSpec((tr, d), lambda i: (i, 0))
    return pl.pallas_call(
        body, name=name, grid=(s // tr,), in_specs=[row, pl.BlockSpec((1, d), lambda i: (0, 0))],
        out_specs=row, out_shape=jax.ShapeDtypeStruct((s, d), BF16),
        compiler_params=_params(("parallel",)),
    )(x, g)


def _rmsnorm_bwd_rows(xv, gv, dy):
    r = lax.rsqrt(jnp.mean(xv * xv, axis=-1, keepdims=True) + EPS)
    xhat = xv * r
    dxh = dy * gv
    dx = r * (dxh - xhat * jnp.mean(dxh * xhat, axis=-1, keepdims=True))
    return dx, dy * xhat


def _rmsnorm_bwd(dn, x, g, skip, *, name):
    s, d = x.shape
    tr = _row_block(s, d)

    def body(dn_ref, x_ref, g_ref, skip_ref, dx_ref, dxb_ref, dg_ref):
        dx, dgr = _rmsnorm_bwd_rows(x_ref[...], g_ref[...], dn_ref[...])
        dx = dx + skip_ref[...]
        dx_ref[...] = dx
        dxb_ref[...] = dx.astype(BF16)

        @pl.when(pl.program_id(0) == 0)
        def _():
            dg_ref[...] = jnp.zeros_like(dg_ref)

        dg_ref[...] += jnp.sum(dgr, axis=0, keepdims=True)

    row = pl.BlockSpec((tr, d), lambda i: (i, 0))
    vec = pl.BlockSpec((1, d), lambda i: (0, 0))
    return pl.pallas_call(
        body, name=name, grid=(s // tr,), in_specs=[row, row, vec, row],
        out_specs=[row, row, vec],
        out_shape=[jax.ShapeDtypeStruct((s, d), F32), jax.ShapeDtypeStruct((s, d), BF16),
                   jax.ShapeDtypeStruct((1, d), F32)],
        compiler_params=_params(("arbitrary",)),
    )(dn, x, g, skip)


def _loss_head(h2, g, target):
    s, d = h2.shape
    tr = _row_block(s, d)

    def body(h_ref, g_ref, t_ref, dh_ref, dhb_ref, dg_ref, loss_ref):
        hv = h_ref[...]
        gv = g_ref[...]
        r = lax.rsqrt(jnp.mean(hv * hv, axis=-1, keepdims=True) + EPS)
        err = hv * r * gv - t_ref[...]
        dx, dgr = _rmsnorm_bwd_rows(hv, gv, err * (1.0 / d))
        dh_ref[...] = dx
        dhb_ref[...] = dx.astype(BF16)

        @pl.when(pl.program_id(0) == 0)
        def _():
            dg_ref[...] = jnp.zeros_like(dg_ref)
            loss_ref[...] = jnp.zeros_like(loss_ref)

        dg_ref[...] += jnp.sum(dgr, axis=0, keepdims=True)
        row_loss = jnp.mean(err * err, axis=-1, keepdims=True)
        loss_ref[...] += 0.5 * jnp.sum(row_loss, axis=0, keepdims=True)

    row = pl.BlockSpec((tr, d), lambda i: (i, 0))
    vec = pl.BlockSpec((1, d), lambda i: (0, 0))
    one = pl.BlockSpec((1, 1), lambda i: (0, 0))
    return pl.pallas_call(
        body, name="loss_head", grid=(s // tr,), in_specs=[row, vec, row],
        out_specs=[row, row, vec, one],
        out_shape=[jax.ShapeDtypeStruct((s, d), F32), jax.ShapeDtypeStruct((s, d), BF16),
                   jax.ShapeDtypeStruct((1, d), F32), jax.ShapeDtypeStruct((1, 1), F32)],
        compiler_params=_params(("arbitrary",)),
    )(h2, g, target)


def _attention_bias_tables():
    k = np.arange(-ATT_KB, ATT_KB + 1)[:, None, None]
    delta = k * ATT_BLOCK + np.arange(ATT_BLOCK)[None, None, :] - np.arange(ATT_BLOCK)[None, :, None]
    dist = np.abs(delta)
    count = np.zeros(delta.shape, np.int32)
    for window, dilation in DILATED_PATTERNS:
        count += (delta % dilation == 0) & (dist <= min(window // 2, ATT_NEAR))
    logc = np.where(count > 0, np.log(np.maximum(count, 1)), MASKED)
    return dist.astype(np.float32), logc.astype(np.float32)


def _far_bias_tables(per_class):
    steps = np.abs(np.arange(per_class)[:, None] - np.arange(per_class)[None, :]) * ATT_CLASSES
    valid = (steps > ATT_NEAR) & (steps <= ATT_REACH)
    return steps.astype(np.float32), np.where(valid, 0.0, MASKED).astype(np.float32)


def _to_classes(x):
    s, cols = x.shape
    return jnp.reshape(jnp.transpose(jnp.reshape(x, (s // ATT_CLASSES, ATT_CLASSES, cols)), (1, 0, 2)), (s, cols))


def _from_classes(x):
    s, cols = x.shape
    return jnp.reshape(jnp.transpose(jnp.reshape(x, (ATT_CLASSES, s // ATT_CLASSES, cols)), (1, 0, 2)), (s, cols))


def _head_bias(bias_ref, slope, dist_ref, logc_ref):
    for kk in range(ATT_WINDOW):
        bias_ref[kk] = logc_ref[kk] - slope * dist_ref[kk]
    bias_ref[ATT_WINDOW] = jnp.full((ATT_BLOCK, ATT_BLOCK), MASKED, F32)


def _window_start(i, nq, nwin):
    return jnp.clip(i - ATT_KB, 0, nq - nwin)


def _window_block(j, i):
    rows = pl.ds(pl.multiple_of(j * ATT_BLOCK, ATT_BLOCK), ATT_BLOCK)
    kk = j - i + ATT_KB
    return rows, jnp.where(jnp.logical_and(kk >= 0, kk < ATT_WINDOW), kk, ATT_WINDOW)


def _attention_far_fwd(qkv, slopes, n_heads, jobs=()):
    s = qkv.shape[0]
    per_class = s // ATT_CLASSES
    scale = HEAD_DIM ** -0.5
    dist, logc = _far_bias_tables(per_class)

    def body(slope_ref, q_ref, k_ref, v_ref, dist_ref, logc_ref, o_ref, lse_ref):
        bias = logc_ref[...] - slope_ref[pl.program_id(0)] * dist_ref[...]
        for a in range(ATT_FAR_GROUP):
            rows = pl.ds(a * per_class, per_class)
            sc = _dot(q_ref[rows, :], k_ref[rows, :], tb=True) * scale + bias
            m = jnp.maximum(jnp.max(sc, axis=-1, keepdims=True), ROW_MAX_INIT)
            p = jnp.exp(sc - m)
            l = jnp.maximum(jnp.sum(p, axis=-1, keepdims=True), 1e-30)
            o_ref[rows, :] = (_dot(p.astype(BF16), v_ref[rows, :]) / l).astype(BF16)
            lse_ref[rows, :] = jnp.broadcast_to(m + jnp.log(l), (per_class, HEAD_DIM))

    hh = n_heads
    blk = pl.BlockSpec((ATT_FAR_GROUP * per_class, HEAD_DIM), lambda h, r: (r, h))
    table = pl.BlockSpec(dist.shape, lambda h, r: (0, 0))
    return _call(
        body, name="attention_far_fwd", grid=(hh, ATT_CLASSES // ATT_FAR_GROUP),
        in_specs=[pl.BlockSpec(memory_space=pltpu.SMEM), blk,
                  pl.BlockSpec((ATT_FAR_GROUP * per_class, HEAD_DIM), lambda h, r: (r, hh + h)),
                  pl.BlockSpec((ATT_FAR_GROUP * per_class, HEAD_DIM), lambda h, r: (r, 2 * hh + h)), table, table],
        out_specs=[blk, blk],
        out_shape=[jax.ShapeDtypeStruct((s, hh * HEAD_DIM), BF16), jax.ShapeDtypeStruct((s, hh * HEAD_DIM), F32)],
        operands=[slopes, qkv, qkv, qkv, jnp.asarray(dist), jnp.asarray(logc)],
        semantics=("parallel", "parallel"), jobs=jobs)


def _attention_fwd(proj, slopes, far_out, far_lse, n_heads, jobs=()):
    s = proj.shape[0]
    nq = s // ATT_BLOCK
    scale = HEAD_DIM ** -0.5
    dist, logc = _attention_bias_tables()

    nwin = min(ATT_WINDOW, nq)

    group = math.gcd(ATT_NEAR_GROUP, nq)

    def body(slope_ref, q_ref, k_ref, v_ref, fo_ref, fl_ref, dist_ref, logc_ref, o_ref, lse_ref, bias_ref, s_ref):
        h, step = pl.program_id(0), pl.program_id(1)

        @pl.when(step == 0)
        def _():
            _head_bias(bias_ref, slope_ref[h], dist_ref, logc_ref)

        for a in range(group):
            i = step * group + a
            mine = pl.ds(a * ATT_BLOCK, ATT_BLOCK)
            q = q_ref[mine, :]
            first = _window_start(i, nq, nwin)
            m = jnp.full((ATT_BLOCK, 1), ROW_MAX_INIT, F32)
            for b in range(nwin):
                rows, kk = _window_block(first + b, i)
                sc = _dot(q, k_ref[rows, :], tb=True) * scale + bias_ref[kk]
                s_ref[a * nwin + b] = sc
                m = jnp.maximum(m, jnp.max(sc, axis=-1, keepdims=True))
            l = jnp.zeros((ATT_BLOCK, 1), F32)
            acc = jnp.zeros((ATT_BLOCK, HEAD_DIM), F32)
            for b in range(nwin):
                rows, _ = _window_block(first + b, i)
                p = jnp.exp(s_ref[a * nwin + b] - m)
                l = l + jnp.sum(p, axis=-1, keepdims=True)
                acc = acc + _dot(p.astype(BF16), v_ref[rows, :])
            near_lse = m + jnp.log(l)
            far_lse_col = fl_ref[mine, :1]
            lse = jnp.maximum(near_lse, far_lse_col)
            lse = lse + jnp.log(jnp.exp(near_lse - lse) + jnp.exp(far_lse_col - lse))
            o_ref[mine, :] = (acc * (jnp.exp(near_lse - lse) / l)
                              + fo_ref[mine, :].astype(F32) * jnp.exp(far_lse_col - lse)).astype(BF16)
            lse_ref[mine, :] = jnp.broadcast_to(lse, (ATT_BLOCK, HEAD_DIM))

    hh = n_heads
    blk = pl.BlockSpec((group * ATT_BLOCK, HEAD_DIM), lambda h, i: (i, h))
    table = pl.BlockSpec(dist.shape, lambda h, i: (0, 0, 0))
    return _call(
        body, name="attention_fwd", grid=(hh, nq // group),
        in_specs=[pl.BlockSpec(memory_space=pltpu.SMEM), blk,
                  pl.BlockSpec((s, HEAD_DIM), lambda h, i: (0, hh + h)),
                  pl.BlockSpec((s, HEAD_DIM), lambda h, i: (0, 2 * hh + h)), blk, blk, table, table],
        out_specs=[blk, blk],
        out_shape=[jax.ShapeDtypeStruct((s, hh * HEAD_DIM), BF16), jax.ShapeDtypeStruct((s, hh * HEAD_DIM), F32)],
        operands=[slopes, proj, proj, proj, far_out, far_lse, jnp.asarray(dist), jnp.asarray(logc)],
        scratch_shapes=[pltpu.VMEM((ATT_WINDOW + 1, ATT_BLOCK, ATT_BLOCK), F32),
                        pltpu.VMEM((group * nwin, ATT_BLOCK, ATT_BLOCK), F32)],
        semantics=("parallel", "arbitrary"), jobs=jobs)


def _attention_far_bwd(qkv, slopes, out, dout, lse, n_heads):
    s = qkv.shape[0]
    per_class = s // ATT_CLASSES
    scale = HEAD_DIM ** -0.5
    dist, logc = _far_bias_tables(per_class)

    def body(slope_ref, q_ref, k_ref, v_ref, o_ref, do_ref, lse_ref, dist_ref, logc_ref, dq_ref, dk_ref, dv_ref):
        bias = logc_ref[...] - slope_ref[pl.program_id(0)] * dist_ref[...]
        for a in range(ATT_FAR_GROUP):
            rows = pl.ds(a * per_class, per_class)
            q, k, do = q_ref[rows, :], k_ref[rows, :], do_ref[rows, :]
            delta = jnp.sum(do.astype(F32) * o_ref[rows, :].astype(F32), axis=-1, keepdims=True)
            p = jnp.exp(_dot(q, k, tb=True) * scale + bias - lse_ref[rows, :1])
            dv_ref[rows, :] = _dot(p.astype(BF16), do, ta=True).astype(BF16)
            ds = (p * (_dot(do, v_ref[rows, :], tb=True) - delta) * scale).astype(BF16)
            dk_ref[rows, :] = _dot(ds, q, ta=True).astype(BF16)
            dq_ref[rows, :] = _dot(ds, k).astype(BF16)

    hh = n_heads
    blk = pl.BlockSpec((ATT_FAR_GROUP * per_class, HEAD_DIM), lambda h, r: (r, h))
    table = pl.BlockSpec(dist.shape, lambda h, r: (0, 0))
    o_shape = jax.ShapeDtypeStruct((s, hh * HEAD_DIM), BF16)
    return pl.pallas_call(
        body, name="attention_far_bwd", grid=(hh, ATT_CLASSES // ATT_FAR_GROUP),
        in_specs=[pl.BlockSpec(memory_space=pltpu.SMEM), blk,
                  pl.BlockSpec((ATT_FAR_GROUP * per_class, HEAD_DIM), lambda h, r: (r, hh + h)),
                  pl.BlockSpec((ATT_FAR_GROUP * per_class, HEAD_DIM), lambda h, r: (r, 2 * hh + h)),
                  blk, blk, blk, table, table],
        out_specs=[blk] * 3, out_shape=[o_shape] * 3,
        compiler_params=_params(("parallel", "parallel")),
    )(slopes, qkv, qkv, qkv, out, dout, lse, jnp.asarray(dist), jnp.asarray(logc))


def _attention_bwd(proj, slopes, out, lse, dmixed, far_grads, n_heads, jobs=()):
    s = proj.shape[0]
    nq = s // ATT_BLOCK
    scale = HEAD_DIM ** -0.5
    dist, logc = _attention_bias_tables()

    nwin = min(ATT_WINDOW, nq)
    group = math.gcd(ATT_NEAR_GROUP, nq)

    def body(slope_ref, q_ref, k_ref, v_ref, o_ref, do_ref, lse_ref, fdq_ref, fdk_ref, fdv_ref, dist_ref, logc_ref,
             dq_ref, dk_ref, dv_ref, dk_acc, dv_acc, bias_ref):
        h, step = pl.program_id(0), pl.program_id(1)

        @pl.when(step == 0)
        def _():
            dk_acc[...] = jnp.zeros_like(dk_acc)
            dv_acc[...] = jnp.zeros_like(dv_acc)
            _head_bias(bias_ref, slope_ref[h], dist_ref, logc_ref)

        for a in range(group):
            i = step * group + a
            mine = pl.ds(a * ATT_BLOCK, ATT_BLOCK)
            q = q_ref[mine, :]
            do = do_ref[mine, :]
            lse_col = lse_ref[mine, :1]
            delta = jnp.sum(do.astype(F32) * o_ref[mine, :].astype(F32), axis=-1, keepdims=True)
            first = _window_start(i, nq, nwin)
            dq = jnp.zeros((ATT_BLOCK, HEAD_DIM), F32)
            for b in range(nwin):
                rows, kk = _window_block(first + b, i)
                kj = k_ref[rows, :]
                vj = v_ref[rows, :]
                p = jnp.exp(_dot(q, kj, tb=True) * scale + bias_ref[kk] - lse_col)
                dv_acc[rows, :] += _dot(p.astype(BF16), do, ta=True)
                dp = _dot(do, vj, tb=True)
                ds = (p * (dp - delta) * scale).astype(BF16)
                dk_acc[rows, :] += _dot(ds, q, ta=True)
                dq = dq + _dot(ds, kj)
            dq_ref[mine, :] = (dq + fdq_ref[mine, :].astype(F32)).astype(BF16)

        @pl.when(step == nq // group - 1)
        def _():
            dk_ref[...] = (dk_acc[...] + fdk_ref[...].astype(F32)).astype(BF16)
            dv_ref[...] = (dv_acc[...] + fdv_ref[...].astype(F32)).astype(BF16)

    hh = n_heads
    blk = pl.BlockSpec((group * ATT_BLOCK, HEAD_DIM), lambda h, i: (i, h))
    col = pl.BlockSpec((s, HEAD_DIM), lambda h, i: (0, h))
    table = pl.BlockSpec(dist.shape, lambda h, i: (0, 0, 0))
    o_shape = jax.ShapeDtypeStruct((s, hh * HEAD_DIM), BF16)
    return _call(
        body, name="attention_bwd", grid=(hh, nq // group),
        in_specs=[pl.BlockSpec(memory_space=pltpu.SMEM), blk,
                  pl.BlockSpec((s, HEAD_DIM), lambda h, i: (0, hh + h)),
                  pl.BlockSpec((s, HEAD_DIM), lambda h, i: (0, 2 * hh + h)),
                  blk, blk, blk, blk, col, col, table, table],
        out_specs=[blk, col, col], out_shape=[o_shape] * 3,
        operands=[slopes, proj, proj, proj, out, dmixed, lse, *far_grads, jnp.asarray(dist), jnp.asarray(logc)],
        scratch_shapes=[pltpu.VMEM((s, HEAD_DIM), F32)] * 2
        + [pltpu.VMEM((ATT_WINDOW + 1, ATT_BLOCK, ATT_BLOCK), F32)],
        semantics=("parallel", "arbitrary"), jobs=jobs)


def _ret_decays(lgc, lga, strict_c, strict_a):
    c = RET_CHUNK
    rel = (lax.broadcasted_iota(jnp.int32, (c, c), 0) - lax.broadcasted_iota(jnp.int32, (c, c), 1)).astype(F32)
    in_c = (rel > 0) if strict_c else (rel >= 0)
    in_a = (rel < 0) if strict_a else (rel <= 0)
    mask = (jnp.where(in_c, jnp.exp(lgc * jnp.maximum(rel, 0.0)), 0.0)
            + jnp.where(in_a, jnp.exp(lga * jnp.maximum(-rel, 0.0)), 0.0))
    idx = lax.broadcasted_iota(jnp.int32, (c, 1), 0).astype(F32)
    ones = jnp.ones((1, HEAD_DIM), F32)
    dec = dict(
        rel=rel, mask=mask, idx=idx,
        a_c=jnp.exp(lgc * (idx + 1.0)), b_c=jnp.exp(lgc * (c - 1.0 - idx)), chunk_c=jnp.exp(ones * (lgc * c)),
        a_a=jnp.exp(lga * (c - idx)), b_a=jnp.exp(lga * idx), chunk_a=jnp.exp(ones * (lga * c)),
    )
    return dec


def _scaled(x, col):
    return (x.astype(F32) * col).astype(BF16)


def _chunk_rows(i):
    return pl.ds(pl.multiple_of(i * RET_CHUNK, RET_CHUNK), RET_CHUNK)


def _chunk_loop(nc, step, init, unroll=RET_UNROLL):
    group = math.gcd(nc, unroll)

    def trip(t, carry):
        for u in range(group):
            carry = step(t * group + u, carry)
        return carry

    return lax.fori_loop(0, nc // group, trip, init)


def _retention(a, b, c, lg_c, lg_a, *, strict_c, strict_a, scale, n_heads, name, gate=None, norm_w=None, jobs=()):
    s = a[0].shape[0]
    nc = s // RET_CHUNK
    epilogue = gate is not None

    def body(*refs):
        lgc_ref, lga_ref, a_ref, b_ref, c_ref = refs[:5]
        if epilogue:
            g_ref, w_ref, o_ref, mix_ref, sa_ref = refs[5:]
        else:
            o_ref, sa_ref = refs[5:]
        h = pl.program_id(0)
        dec = _ret_decays(lgc_ref[h], lga_ref[h], strict_c, strict_a)

        def reverse(t, state):
            i = nc - 1 - t
            sa_ref[i] = state.astype(BF16)
            rows = _chunk_rows(i)
            return state * dec["chunk_a"] + _dot(_scaled(b_ref[rows, :], dec["b_a"]), c_ref[rows, :], ta=True)

        _chunk_loop(nc, reverse, jnp.zeros((HEAD_DIM, HEAD_DIM), F32))

        def forward(i, state):
            rows = _chunk_rows(i)
            ai, bi, ci = a_ref[rows, :], b_ref[rows, :], c_ref[rows, :]
            inner = (_dot(ai, bi, tb=True) * dec["mask"]).astype(BF16)
            out = (_dot(inner, ci) + _dot(_scaled(ai, dec["a_c"]), state.astype(BF16))
                   + _dot(_scaled(ai, dec["a_a"]), sa_ref[i])) * scale
            o_ref[rows, :] = out.astype(BF16)
            if epilogue:
                r = lax.rsqrt(jnp.mean(out * out, axis=-1, keepdims=True) + EPS)
                g = g_ref[rows, :].astype(F32)
                mix_ref[rows, :] = (out * r * w_ref[...] * (g * _sigmoid(g))).astype(BF16)
            return state * dec["chunk_c"] + _dot(_scaled(bi, dec["b_c"]), ci, ta=True)

        _chunk_loop(nc, forward, jnp.zeros((HEAD_DIM, HEAD_DIM), F32))

    def col(first):
        return pl.BlockSpec((s, HEAD_DIM), lambda h: (0, first + h))

    smem = pl.BlockSpec(memory_space=pltpu.SMEM)
    in_specs = [smem, smem, col(a[1]), col(b[1]), col(c[1])]
    operands = [lg_c, lg_a, a[0], b[0], c[0]]
    o_shape = jax.ShapeDtypeStruct((s, n_heads * HEAD_DIM), BF16)
    out_specs, out_shape = [col(0)], [o_shape]
    if epilogue:
        in_specs += [col(gate[1]), pl.BlockSpec((1, HEAD_DIM), lambda h: (0, h))]
        operands += [gate[0], norm_w]
        out_specs, out_shape = [col(0)] * 2, [o_shape] * 2
    res, carried = _call(
        body, name=name, grid=(n_heads,), in_specs=in_specs, out_specs=out_specs, out_shape=out_shape,
        operands=operands, scratch_shapes=[pltpu.VMEM((nc, HEAD_DIM, HEAD_DIM), BF16)],
        semantics=("parallel",), jobs=jobs)
    res = res if epilogue else res[0]
    return (res, carried) if jobs else res


def _retention_decay_grads(a, b, c, e, lg_c, lg_a, *, scale, n_heads):
    s = a[0].shape[0]
    nc = s // RET_CHUNK
    cf = float(RET_CHUNK)

    def body(lgc_ref, lga_ref, a_ref, b_ref, c_ref, e_ref, gc_ref, ga_ref, sa_ref, ta_ref):
        h = pl.program_id(0)
        lgc, lga = lgc_ref[h], lga_ref[h]
        dec = _ret_decays(lgc, lga, True, True)
        rel, idx = dec["rel"], dec["idx"]
        w_c = jnp.where(rel > 0, rel * jnp.exp(lgc * jnp.maximum(rel, 0.0)), 0.0)
        w_a = jnp.where(rel < 0, -rel * jnp.exp(lga * jnp.maximum(-rel, 0.0)), 0.0)
        zero = jnp.zeros((HEAD_DIM, HEAD_DIM), F32)

        def reverse(t, carry):
            st, dst = carry
            i = nc - 1 - t
            sa_ref[i] = st.astype(BF16)
            ta_ref[i] = dst.astype(BF16)
            rows = _chunk_rows(i)
            bi, ci = b_ref[rows, :], c_ref[rows, :]
            st_new = st * dec["chunk_a"] + _dot(_scaled(bi, dec["b_a"]), ci, ta=True)
            dst_new = (cf * st + dst) * dec["chunk_a"] + _dot(_scaled(bi, idx * dec["b_a"]), ci, ta=True)
            return st_new, dst_new

        _chunk_loop(nc, reverse, (zero, zero))

        def forward(i, carry):
            st, dst, acc_c, acc_a = carry
            rows = _chunk_rows(i)
            ai, bi, ci = a_ref[rows, :], b_ref[rows, :], c_ref[rows, :]
            ev = e_ref[rows, :].astype(F32)
            pg = _dot(ai, bi, tb=True) * _dot(e_ref[rows, :], ci, tb=True)
            a_c, a_a = _scaled(ai, dec["a_c"]), _scaled(ai, dec["a_a"])
            inter_c = _dot(a_c, st.astype(BF16)) * (idx + 1.0) + _dot(a_c, dst.astype(BF16))
            inter_a = _dot(a_a, sa_ref[i]) * (cf - idx) + _dot(a_a, ta_ref[i])
            acc_c = acc_c + jnp.sum(pg * w_c, axis=0, keepdims=True) + jnp.sum(inter_c * ev, axis=0, keepdims=True)
            acc_a = acc_a + jnp.sum(pg * w_a, axis=0, keepdims=True) + jnp.sum(inter_a * ev, axis=0, keepdims=True)
            st_new = st * dec["chunk_c"] + _dot(_scaled(bi, dec["b_c"]), ci, ta=True)
            dst_new = ((cf * st + dst) * dec["chunk_c"]
                       + _dot(_scaled(bi, (cf - 1.0 - idx) * dec["b_c"]), ci, ta=True))
            return st_new, dst_new, acc_c, acc_a

        row = jnp.zeros((1, HEAD_DIM), F32)
        _, _, acc_c, acc_a = _chunk_loop(nc, forward, (zero, zero, row, row))
        gc_ref[...] = jnp.broadcast_to(jnp.sum(acc_c, axis=-1, keepdims=True) * scale, gc_ref.shape)
        ga_ref[...] = jnp.broadcast_to(jnp.sum(acc_a, axis=-1, keepdims=True) * scale, ga_ref.shape)

    def col(first):
        return pl.BlockSpec((s, HEAD_DIM), lambda h: (0, first + h))

    smem = pl.BlockSpec(memory_space=pltpu.SMEM)
    o_spec = pl.BlockSpec((1, 8, HEAD_DIM), lambda h: (h, 0, 0))
    o_shape = jax.ShapeDtypeStruct((n_heads, 8, HEAD_DIM), F32)
    gc, ga = pl.pallas_call(
        body, name="retention_decay_grads", grid=(n_heads,),
        in_specs=[smem, smem, col(a[1]), col(b[1]), col(c[1]), col(e[1])],
        out_specs=[o_spec] * 2, out_shape=[o_shape] * 2,
        scratch_shapes=[pltpu.VMEM((nc, HEAD_DIM, HEAD_DIM), BF16)] * 2,
        compiler_params=_params(("parallel",)),
    )(lg_c, lg_a, a[0], b[0], c[0], e[0])
    return gc[:, 0, 0], ga[:, 0, 0]


def _ret_gate_bwd(dmixed, first_col, out, proj, gate_col, norm_w, n_heads):
    s = out.shape[0]
    tr = _row_block(s, 8 * HEAD_DIM)

    def body(dm_ref, o_ref, g_ref, w_ref, do_ref, dg_ref, dw_ref):
        dm = dm_ref[...].astype(F32)
        ov = o_ref[...].astype(F32)
        g = g_ref[...].astype(F32)
        w = w_ref[...]
        r = lax.rsqrt(jnp.mean(ov * ov, axis=-1, keepdims=True) + EPS)
        ohat = ov * r
        sg = _sigmoid(g)
        silu = g * sg
        dg_ref[...] = (dm * ohat * w * sg * (1.0 + g * (1.0 - sg))).astype(BF16)
        dohat = dm * w * silu
        do_ref[...] = (r * (dohat - ohat * jnp.mean(dohat * ohat, axis=-1, keepdims=True))).astype(BF16)

        @pl.when(pl.program_id(1) == 0)
        def _():
            dw_ref[...] = jnp.zeros_like(dw_ref)

        dw_ref[...] += jnp.sum(dm * ohat * silu, axis=0, keepdims=True)

    def blk(first):
        return pl.BlockSpec((tr, HEAD_DIM), lambda h, i: (i, first + h))

    vec = pl.BlockSpec((1, HEAD_DIM), lambda h, i: (0, h))
    o_shape = jax.ShapeDtypeStruct((s, n_heads * HEAD_DIM), BF16)
    return pl.pallas_call(
        body, name="ret_gate_bwd", grid=(n_heads, s // tr),
        in_specs=[blk(first_col), blk(0), blk(gate_col), vec],
        out_specs=[blk(0), blk(0), vec],
        out_shape=[o_shape, o_shape, jax.ShapeDtypeStruct((1, n_heads * HEAD_DIM), F32)],
        compiler_params=_params(("parallel", "arbitrary")),
    )(dmixed, out, proj, norm_w)


def _step(x, target, norm_mix_w, ret_decay_fwd, ret_decay_bwd, ret_norm_w, norm_ffn_w, norm_final_w, own, pos):
    d = x.shape[1]
    nh = d // (2 * HEAD_DIM)
    scale = HEAD_DIM ** -0.5
    slopes = jnp.exp2(-8.0 * jnp.arange(1, nh + 1, dtype=F32) / nh)
    lg_f = -jnp.exp(ret_decay_fwd)
    lg_b = -jnp.exp(ret_decay_bwd)
    q_r, k_r, v_r, g_r = 3 * nh, 4 * nh, 5 * nh, 6 * nh
    ax = BIG_AXIS

    def gather(names, arrays, stage, part=None):
        return _gather_job(arrays, [ax[k] for k in names], stage, part)

    def add_halves(k, g, received):
        return _add_halves(g, received, ax[k], pos, name="grad_add_halves_" + k)

    def sum_parts(k, g, received, parts):
        return _sum_chip_parts(g, received, parts, ax[k], pos, name="grad_sum_parts_" + k)

    (w_in,) = _run_jobs([gather(["w_in"], [own["w_in"]], "ici"), gather(["w_in"], [own["w_in"]], "d2d")],
                        name="all_gather_w_in")
    n1 = _rmsnorm_fwd(x, norm_mix_w, name="norm_mix_fwd")
    proj, [[w_gate]] = _matmul(n1, w_in, name="in_proj", out_dtype=BF16, tm=2048,
                               jobs=[gather(["w_gate"], [own["w_gate"]], "ici")])
    qkv_classes = _to_classes(proj[:, :3 * nh * HEAD_DIM])
    (ret, ret_mixed), [[w_gate], [w_out]] = _retention(
        (proj, q_r), (proj, k_r), (proj, v_r), lg_f, lg_b, strict_c=False, strict_a=True, scale=scale, n_heads=nh,
        name="retention_fwd", gate=(proj, g_r), norm_w=ret_norm_w,
        jobs=[gather(["w_gate"], [w_gate], "d2d"), gather(["w_out"], [own["w_out"]], "ici")])
    (far_out, far_lse), [[w_up]] = _attention_far_fwd(
        qkv_classes, slopes, nh, jobs=[gather(["w_up"], [own["w_up"]], "ici", (0, 1, 4))])
    (attn, lse), [[w_out], [w_up]] = _attention_fwd(
        proj, slopes, _from_classes(far_out), _from_classes(far_lse), nh,
        jobs=[gather(["w_out"], [w_out], "d2d"),
              _fuse(gather(["w_up"], [w_up], "d2d", (0, 1, 4)), gather(["w_up"], [w_up], "ici", (1, 2, 4)))])
    mixed = jnp.concatenate([attn, ret_mixed], axis=1)
    h1, [[w_up]] = _matmul(
        mixed, w_out, name="out_proj", residual=x,
        jobs=[_fuse(gather(["w_up"], [w_up], "d2d", (1, 2, 4)), gather(["w_up"], [w_up], "ici", (3, 1, 4)))])
    (w_up,) = _run_jobs([gather(["w_up"], [w_up], "d2d", (3, 1, 4))], name="all_gather_w_up_sibling")
    n2 = _rmsnorm_fwd(h1, norm_ffn_w, name="norm_ffn_fwd")
    (gate, up, act), [[w_down]] = _swiglu_fwd(n2, w_gate, w_up, jobs=[gather(["w_down"], [own["w_down"]], "ici")])
    (w_down,) = _run_jobs([gather(["w_down"], [w_down], "d2d")], name="all_gather_w_down_sibling")
    h2 = _matmul(act, w_down, name="down_proj", residual=h1, tk=2816)
    dh2, dh2_b, d_norm_final, loss = _loss_head(h2, norm_final_w, target)

    dgate, dup = _swiglu_bwd_act(dh2_b, w_down, gate, up)
    g_down = _weight_grad(act, dh2_b, name="grad_w_down")
    g_gate, [[r_down]] = _weight_grad(n2, dgate, name="grad_w_gate", jobs=[_exchange_job([g_down], [ax["w_down"]])])
    s_down = add_halves("w_down", g_down, r_down)
    g_up, [[r_gate], [p_down]] = _weight_grad(
        n2, dup, name="grad_w_up",
        jobs=[_exchange_job([g_gate], [ax["w_gate"]]), _send_sums_job([s_down], [ax["w_down"]], (0, 1, 2))])
    s_gate = add_halves("w_gate", g_gate, r_gate)
    dn2, [[r_up], [p_gate], [p_down]] = _swiglu_bwd_in(
        dgate, dup, w_gate, w_up,
        jobs=[_exchange_job([g_up], [ax["w_up"]]), _send_sums_job([s_gate], [ax["w_gate"]]),
              _send_sums_job([s_down], [ax["w_down"]], (1, 1, 2), landing=[p_down])])
    h_down = sum_parts("w_down", g_down, r_down, p_down)
    s_up = add_halves("w_up", g_up, r_up)
    h_gate = sum_parts("w_gate", g_gate, r_gate, p_gate)
    dh1, dh1_b, d_norm_ffn = _rmsnorm_bwd(dn2, h1, norm_ffn_w, dh2, name="norm_ffn_bwd")

    dmixed, [[gr_down], [p_up]] = _matmul(
        dh1_b, w_out, name="out_proj_bwd", tb=True, out_dtype=BF16,
        jobs=[_join_job([h_down], [ax["w_down"]]), _send_sums_job([s_up], [ax["w_up"]], (0, 1, 4))])
    far_in = [_to_classes(t) for t in (attn, dmixed[:, :nh * HEAD_DIM], lse)]
    g_out, [[p_up]] = _weight_grad(mixed, dh1_b, name="grad_w_out",
                                   jobs=[_send_sums_job([s_up], [ax["w_up"]], (1, 1, 4), landing=[p_up])])
    d_ret, dg_r, d_ret_norm = _ret_gate_bwd(dmixed, nh, ret, proj, g_r, ret_norm_w, nh)
    far_grads = _attention_far_bwd(qkv_classes, slopes, *far_in, nh)
    far_grads = [_from_classes(t) for t in far_grads]
    dq_r, [[gr_gate], [p_up]] = _retention(
        (d_ret, 0), (proj, v_r), (proj, k_r), lg_f, lg_b, strict_c=False, strict_a=True, scale=scale, n_heads=nh,
        name="retention_dq",
        jobs=[_join_job([h_gate], [ax["w_gate"]]), _send_sums_job([s_up], [ax["w_up"]], (2, 1, 4), landing=[p_up])])
    (dq_a, dk_a, dv_a), [[p_up], [r_out]] = _attention_bwd(
        proj, slopes, attn, lse, dmixed, far_grads, nh,
        jobs=[_send_sums_job([s_up], [ax["w_up"]], (3, 1, 4), landing=[p_up]),
              _exchange_job([g_out], [ax["w_out"]])])
    s_out = add_halves("w_out", g_out, r_out)
    h_up = sum_parts("w_up", g_up, r_up, p_up)
    dv_r, [[p_out], [gr_up]] = _retention(
        (proj, k_r), (proj, q_r), (d_ret, 0), lg_b, lg_f, strict_c=True, strict_a=False, scale=scale, n_heads=nh,
        name="retention_dv", jobs=[_send_sums_job([s_out], [ax["w_out"]]), _join_job([h_up], [ax["w_up"]])])
    h_out = sum_parts("w_out", g_out, r_out, p_out)
    dk_r, [[gr_out]] = _retention(
        (proj, v_r), (d_ret, 0), (proj, q_r), lg_b, lg_f, strict_c=True, strict_a=False, scale=scale, n_heads=nh,
        name="retention_dk", jobs=[_join_job([h_out], [ax["w_out"]])])
    dlg_f, dlg_b = _retention_decay_grads((proj, q_r), (proj, k_r), (proj, v_r), (d_ret, 0), lg_f, lg_b,
                                          scale=scale, n_heads=nh)
    dproj = [dq_a, dk_a, dv_a, dq_r, dk_r, dv_r, dg_r]
    g_in = _weight_grad_pieces(n1, dproj, name="grad_w_in")
    (r_in,) = _run_jobs([_exchange_job([g_in], [ax["w_in"]])], name="grad_exchange_w_in")
    s_in = add_halves("w_in", g_in, r_in)
    dn1, [[p_in]] = _matmul_pieces_nt(dproj, w_in, name="in_proj_bwd", jobs=[_send_sums_job([s_in], [ax["w_in"]])])
    dx, _, d_norm_mix = _rmsnorm_bwd(dn1, x, norm_mix_w, dh1, name="norm_mix_bwd")
    h_in = sum_parts("w_in", g_in, r_in, p_in)
    (gr_in,) = _run_jobs([_join_job([h_in], [ax["w_in"]])], name="grad_join_w_in")

    small = dict(loss=loss[0, 0], norm_mix_w=d_norm_mix, ret_decay_fwd=dlg_f * lg_f, ret_decay_bwd=dlg_b * lg_b,
                 ret_norm_w=d_ret_norm, norm_ffn_w=d_norm_ffn, norm_final_w=d_norm_final)
    return dx, dict(w_in=gr_in, w_out=gr_out, w_gate=gr_gate, w_up=gr_up, w_down=gr_down), small


def _mesh_position():
    x, y, c = lax.axis_index("x"), lax.axis_index("y"), lax.axis_index("c")
    chips = [(1 - x, y), (x, 1 - y), (1 - x, 1 - y)]
    return x, y, c, chips


def _span(span):
    if span is None:
        return slice(None)
    start, size, step = span
    return pl.ds(start if isinstance(start, int) else pl.multiple_of(start, step), size)


def _part_rows(part, rows):
    first, count, of = part
    return first * (rows // of), count * (rows // of), rows // of


def _region(ref, axis, shard, half, shard_size, half_size, part=None, total_rows=None):
    along = None if shard is None else (shard * shard_size, shard_size, shard_size)
    other = None if half is None else (half * half_size, half_size, half_size)
    rows, cols = (other, along) if axis == 1 else (along, other)
    if part is not None:
        start, size, _ = rows if rows is not None else (0, total_rows, None)
        offset, size, step = _part_rows(part, size)
        rows = (start + offset, size, step)
    return ref.at[_span(rows), _span(cols)]


def _fuse(first, second):
    assert not (first.ins or first.outs or second.ins or second.outs)
    assert len(first.ios) == len(second.ios) and all(a is b for a, b in zip(first.ios, second.ios))
    cut = len(first.sems)

    def start(refs, sems):
        first.start(refs, sems[:cut])
        second.start(refs, sems[cut:])

    def finish(refs, sems):
        first.finish(refs, sems[:cut])
        second.finish(refs, sems[cut:])

    return _Job(ios=first.ios, sems=first.sems + second.sems, start=start, finish=finish)


def _gather_job(full, axes, stage, part=None):
    n = len(full)

    def copies(refs, sems):
        send_sem, recv_sem = sems
        x, y, c, chips = _mesh_position()
        me = 2 * x + y

        def copy(w, k, shard, half, target):
            rows_cols = full[w].shape
            place = _region(refs[w], axes[w], shard, half, rows_cols[axes[w]] // N_CHIPS, rows_cols[1 - axes[w]] // 2,
                            part)
            return pltpu.make_async_remote_copy(
                src_ref=place, dst_ref=place, send_sem=send_sem.at[w, k], recv_sem=recv_sem.at[w, k],
                device_id=target, device_id_type=MESH)

        def sent(w, k):
            if stage == "ici":
                return copy(w, k, me, c, (chips[k][0], chips[k][1], c))
            return copy(w, k, 2 * chips[k][0] + chips[k][1], c, (x, y, 1 - c))

        def landed(w, k):
            return copy(w, k, 2 * chips[k][0] + chips[k][1], c if stage == "ici" else 1 - c, (x, y, 1 - c))

        return sent, landed

    def start(refs, sems):
        sent, _ = copies(refs, sems)
        for w in range(n):
            for k in range(3):
                sent(w, k).start()

    def finish(refs, sems):
        sent, landed = copies(refs, sems)
        for w in range(n):
            for k in range(3):
                landed(w, k).wait_recv()
                sent(w, k).wait_send()

    return _Job(ios=full, sems=[pltpu.SemaphoreType.DMA((n, 3))] * 2, start=start, finish=finish)


def _exchange_job(grads, axes):
    n = len(grads)

    def half_shape(w):
        return tuple(d // 2 if a != axes[w] else d for a, d in enumerate(grads[w].shape))

    def copy(refs, sems, w):
        x, y, c, _ = _mesh_position()
        return pltpu.make_async_remote_copy(
            src_ref=_region(refs[w], axes[w], None, 1 - c, 0, half_shape(w)[1 - axes[w]]), dst_ref=refs[n + w],
            send_sem=sems[0].at[w], recv_sem=sems[1].at[w], device_id=(x, y, 1 - c), device_id_type=MESH)

    def start(refs, sems):
        for w in range(n):
            copy(refs, sems, w).start()

    def finish(refs, sems):
        for w in range(n):
            copy(refs, sems, w).wait()

    return _Job(ins=grads, outs=[jax.ShapeDtypeStruct(half_shape(w), F32) for w in range(n)],
                sems=[pltpu.SemaphoreType.DMA((n,))] * 2, start=start, finish=finish)


def _half_block_spec(axis, block, half_blocks, use_half):
    if axis == 1:
        if use_half:
            return pl.BlockSpec(block, lambda i, pos: (pos[0] * half_blocks + i, 0))
        return pl.BlockSpec(block, lambda i, pos: (i, 0))
    if use_half:
        return pl.BlockSpec(block, lambda i, pos: (i, pos[0]))
    return pl.BlockSpec(block, lambda i, pos: (i, 0))


def _add_halves(grad, received, axis, pos, *, name):
    rows, cols = received.shape
    tr = _row_block(rows, cols)
    nb = rows // tr

    def body(pos_ref, g_ref, r_ref, o_ref):
        o_ref[...] = (g_ref[...] + r_ref[...]).astype(BF16)

    blk = (tr, cols)
    return pl.pallas_call(
        body, name=name, out_shape=jax.ShapeDtypeStruct((rows, cols), BF16),
        grid_spec=pltpu.PrefetchScalarGridSpec(
            num_scalar_prefetch=1, grid=(nb,),
            in_specs=[_half_block_spec(axis, blk, nb, True), _half_block_spec(axis, blk, nb, False)],
            out_specs=_half_block_spec(axis, blk, nb, False)),
        compiler_params=_params(("parallel",)),
    )(pos, grad, received)


def _send_sums_job(sums, axes, part=None, landing=None):
    n = len(sums)

    def part_shape(w):
        return tuple(d // N_CHIPS if a == axes[w] else d for a, d in enumerate(sums[w].shape))

    def copy(refs, sems, w, k):
        x, y, c, chips = _mesh_position()
        shard = 2 * chips[k][0] + chips[k][1]
        rows = part_shape(w)[0]
        dst = refs[n + w].at[k]
        if part is not None:
            offset, size, _ = _part_rows(part, rows)
            dst = refs[n + w].at[k, pl.ds(offset, size), :]
        return pltpu.make_async_remote_copy(
            src_ref=_region(refs[w], axes[w], shard, None, part_shape(w)[axes[w]], 0, part, rows), dst_ref=dst,
            send_sem=sems[0].at[w, k], recv_sem=sems[1].at[w, k],
            device_id=(chips[k][0], chips[k][1], c), device_id_type=MESH)

    def start(refs, sems):
        for w in range(n):
            for k in range(3):
                copy(refs, sems, w, k).start()

    def finish(refs, sems):
        for w in range(n):
            for k in range(3):
                copy(refs, sems, w, k).wait()

    sems = [pltpu.SemaphoreType.DMA((n, 3))] * 2
    if landing is not None:
        return _Job(ins=sums, ios=landing, sems=sems, start=start, finish=finish)
    return _Job(ins=sums, outs=[jax.ShapeDtypeStruct((3,) + part_shape(w), BF16) for w in range(n)],
                sems=sems, start=start, finish=finish)


def _sum_chip_parts(grad, received, parts, axis, pos, *, name):
    _, rows, cols = parts.shape
    tr = _row_block(rows, cols)
    nb = rows // tr
    blk = (tr, cols)

    def body(pos_ref, g_ref, r_ref, p_ref, o_ref):
        total = g_ref[...] + r_ref[...]
        for k in range(3):
            total = total + p_ref[k].astype(F32)
        o_ref[...] = total

    if axis == 1:
        g_spec = pl.BlockSpec(blk, lambda i, pos: (pos[0] * nb + i, pos[1]))
        r_spec = pl.BlockSpec(blk, lambda i, pos: (i, pos[1]))
        o_spec = pl.BlockSpec(blk, lambda i, pos: (pos[0] * nb + i, 0))
        shard_shape = (2 * rows, cols)
    else:
        g_spec = pl.BlockSpec(blk, lambda i, pos: (pos[1] * nb + i, pos[0]))
        r_spec = pl.BlockSpec(blk, lambda i, pos: (pos[1] * nb + i, 0))
        o_spec = pl.BlockSpec(blk, lambda i, pos: (i, pos[0]))
        shard_shape = (rows, 2 * cols)
    return pl.pallas_call(
        body, name=name, out_shape=jax.ShapeDtypeStruct(shard_shape, F32),
        grid_spec=pltpu.PrefetchScalarGridSpec(
            num_scalar_prefetch=1, grid=(nb,),
            in_specs=[g_spec, r_spec, pl.BlockSpec((3,) + blk, lambda i, pos: (0, i, 0))],
            out_specs=o_spec),
        compiler_params=_params(("parallel",)),
    )(pos, grad, received, parts)


def _join_job(shards, axes):
    n = len(shards)

    def copy(refs, sems, w, other):
        x, y, c, _ = _mesh_position()
        place = _region(refs[w], axes[w], None, 1 - c if other else c, 0, shards[w].shape[1 - axes[w]] // 2)
        return pltpu.make_async_remote_copy(
            src_ref=place, dst_ref=place, send_sem=sems[0].at[w], recv_sem=sems[1].at[w],
            device_id=(x, y, 1 - c), device_id_type=MESH)

    def start(refs, sems):
        for w in range(n):
            copy(refs, sems, w, False).start()

    def finish(refs, sems):
        for w in range(n):
            copy(refs, sems, w, True).wait_recv()
            copy(refs, sems, w, False).wait_send()

    return _Job(ios=shards, sems=[pltpu.SemaphoreType.DMA((n,))] * 2, start=start, finish=finish)


def _all_reduce_small(vec):
    rows, cols = vec.shape

    def body(v_ref, o_ref, land_ref, send_sem, recv_sem):
        x, y, c, _ = _mesh_position()
        me = 4 * x + 2 * y + c
        land_ref[me] = v_ref[...]
        copies = []
        for k in range(1, 8):
            px, py, pc = x ^ (k >> 2), y ^ ((k >> 1) & 1), c ^ (k & 1)
            copies.append(pltpu.make_async_remote_copy(
                src_ref=v_ref, dst_ref=land_ref.at[me], send_sem=send_sem.at[k], recv_sem=recv_sem.at[k],
                device_id=(px, py, pc), device_id_type=MESH))
        for cp in copies:
            cp.start()
        for k in range(1, 8):
            peer = me ^ k
            pltpu.make_async_remote_copy(
                src_ref=v_ref, dst_ref=land_ref.at[peer], send_sem=send_sem.at[k], recv_sem=recv_sem.at[k],
                device_id=(x, y, c), device_id_type=MESH).wait_recv()
        for cp in copies:
            cp.wait_send()
        total = land_ref[0]
        for k in range(1, 8):
            total = total + land_ref[k]
        o_ref[...] = total

    vmem = pl.BlockSpec(memory_space=pltpu.VMEM)
    return pl.pallas_call(
        body, name="all_reduce_small", in_specs=[vmem], out_specs=vmem,
        out_shape=jax.ShapeDtypeStruct((rows, cols), F32),
        scratch_shapes=[pltpu.VMEM((8, rows, cols), F32), pltpu.SemaphoreType.DMA((8,)), pltpu.SemaphoreType.DMA((8,))],
    )(vec)


def _adamw(w, g, m, v, *, name):
    rows, cols = w.shape
    tr = _row_block(rows, cols) if rows % 8 == 0 else rows
    bc1 = 1.0 - ADAM_B1 ** ADAM_STEP
    bc2 = 1.0 - ADAM_B2 ** ADAM_STEP

    def body(w_ref, g_ref, m_ref, v_ref, go_ref, d_ref, mo_ref, vo_ref):
        gv = g_ref[...]
        go_ref[...] = gv
        mn = ADAM_B1 * m_ref[...] + (1.0 - ADAM_B1) * gv
        vn = ADAM_B2 * v_ref[...] + (1.0 - ADAM_B2) * (gv * gv)
        mo_ref[...] = mn
        vo_ref[...] = vn
        d_ref[...] = -ADAM_LR * ((mn / bc1) / (jnp.sqrt(vn / bc2) + ADAM_EPS) + ADAM_WD * w_ref[...])

    blk = pl.BlockSpec((tr, cols), lambda i: (i, 0))
    shape = jax.ShapeDtypeStruct((rows, cols), F32)
    return pl.pallas_call(
        body, name=name, grid=(rows // tr,), in_specs=[blk] * 4, out_specs=[blk] * 4, out_shape=[shape] * 4,
        compiler_params=_params(("parallel",)),
    )(w, g, m, v)


def _to_bf16_in_place(w, axis, pos, *, name):
    rows, cols = w.shape
    tr = _row_block(rows, cols)
    nb = rows // tr

    def body(pos_ref, w_ref, o_ref):
        o_ref[...] = w_ref[...].astype(BF16)

    if axis == 1:
        o_spec = pl.BlockSpec((tr, cols), lambda i, pos: (i, pos[1]))
        full_shape = (rows, N_CHIPS * cols)
    else:
        o_spec = pl.BlockSpec((tr, cols), lambda i, pos: (pos[1] * nb + i, 0))
        full_shape = (N_CHIPS * rows, cols)
    return pl.pallas_call(
        body, name=name, out_shape=jax.ShapeDtypeStruct(full_shape, BF16),
        grid_spec=pltpu.PrefetchScalarGridSpec(
            num_scalar_prefetch=1, grid=(nb,),
            in_specs=[pl.BlockSpec((tr, cols), lambda i, pos: (i, 0))], out_specs=o_spec),
        compiler_params=_params(("parallel",)),
    )(pos, w)


BIG = ("w_in", "w_out", "w_gate", "w_up", "w_down")
BIG_AXIS = dict(w_in=1, w_out=0, w_gate=1, w_up=1, w_down=0)
SMALL = ("norm_mix_w", "ret_decay_fwd", "ret_decay_bwd", "ret_norm_w", "norm_ffn_w", "norm_final_w")
ALL_WEIGHTS = ("norm_mix_w", "w_in", "ret_decay_fwd", "ret_decay_bwd", "ret_norm_w", "w_out", "norm_ffn_w",
               "w_gate", "w_up", "w_down", "norm_final_w")
SMALL_ROW = 128 * 8


def _pack_small(small):
    pieces = [jnp.reshape(small["loss"], (1,))] + [jnp.reshape(small[k], (-1,)) for k in SMALL]
    rows = []
    for p in pieces:
        pad = -p.shape[0] % (8 * SMALL_ROW)
        rows.append(jnp.reshape(jnp.pad(p, (0, pad)), (-1, SMALL_ROW)))
    return jnp.concatenate(rows, axis=0)


def _unpack_small(block, like):
    out, row = {}, 0
    for k in ("loss",) + SMALL:
        size = 1 if k == "loss" else like[k].size
        nrows = -(-size // (8 * SMALL_ROW)) * 8
        out[k] = jnp.reshape(block[row:row + nrows], (-1,))[:size]
        row += nrows
    return out


def kernel(x, norm_mix_w, w_in, ret_decay_fwd, ret_decay_bwd, ret_norm_w, w_out, norm_ffn_w, w_gate, w_up, w_down, norm_final_w, loss_target, m_norm_mix_w, m_w_in, m_ret_decay_fwd, m_ret_decay_bwd, m_ret_norm_w, m_w_out, m_norm_ffn_w, m_w_gate, m_w_up, m_w_down, m_norm_final_w, v_norm_mix_w, v_w_in, v_ret_decay_fwd, v_ret_decay_bwd, v_ret_norm_w, v_w_out, v_norm_ffn_w, v_w_gate, v_w_up, v_w_down, v_norm_final_w):
    weights = dict(norm_mix_w=norm_mix_w, w_in=w_in, ret_decay_fwd=ret_decay_fwd, ret_decay_bwd=ret_decay_bwd,
                   ret_norm_w=ret_norm_w, w_out=w_out, norm_ffn_w=norm_ffn_w, w_gate=w_gate, w_up=w_up,
                   w_down=w_down, norm_final_w=norm_final_w)
    m_in = dict(norm_mix_w=m_norm_mix_w, w_in=m_w_in, ret_decay_fwd=m_ret_decay_fwd, ret_decay_bwd=m_ret_decay_bwd,
                ret_norm_w=m_ret_norm_w, w_out=m_w_out, norm_ffn_w=m_norm_ffn_w, w_gate=m_w_gate, w_up=m_w_up,
                w_down=m_w_down, norm_final_w=m_norm_final_w)
    v_in = dict(norm_mix_w=v_norm_mix_w, w_in=v_w_in, ret_decay_fwd=v_ret_decay_fwd, ret_decay_bwd=v_ret_decay_bwd,
                ret_norm_w=v_ret_norm_w, w_out=v_w_out, norm_ffn_w=v_norm_ffn_w, w_gate=v_w_gate, w_up=v_w_up,
                w_down=v_w_down, norm_final_w=v_norm_final_w)
    pos = jnp.stack([lax.axis_index("c"), 2 * lax.axis_index("x") + lax.axis_index("y")]).astype(jnp.int32)

    own = {k: _to_bf16_in_place(weights[k][0], BIG_AXIS[k], pos, name="cast_" + k) for k in BIG}

    dx, grad_w, small = _step(
        x[0], loss_target[0], norm_mix_w, ret_decay_fwd[0], ret_decay_bwd[0], ret_norm_w, norm_ffn_w,
        norm_final_w[None, :], own, pos)

    like = {k: weights[k] for k in SMALL}
    reduced = _unpack_small(_all_reduce_small(_pack_small(small)), like)
    loss = reduced["loss"][0]
    for k in SMALL:
        grad_w[k] = jnp.reshape(reduced[k], (1, -1))

    delta, new_m, new_v = {}, {}, {}
    for k in ALL_WEIGHTS:
        shape = weights[k].shape
        as2d = (lambda t: jnp.reshape(t, (-1, shape[-1])))
        grad_w[k], delta[k], new_m[k], new_v[k] = (jnp.reshape(t, shape) for t in _adamw(
            as2d(weights[k]), as2d(grad_w[k]), as2d(m_in[k]), as2d(v_in[k]), name="adamw_" + k))

    return (loss, dx[None], *[grad_w[k] for k in ALL_WEIGHTS], *[delta[k] for k in ALL_WEIGHTS],
            *[new_m[k] for k in ALL_WEIGHTS], *[new_v[k] for k in ALL_WEIGHTS])
```

```python
import functools
import math

import numpy as np
import jax
import jax.numpy as jnp
from jax import lax
from jax.experimental import pallas as pl
from jax.experimental.pallas import tpu as pltpu

F32 = jnp.float32
BF16 = jnp.bfloat16
MESH = pl.DeviceIdType.MESH

HEAD_DIM = 128
RET_CHUNK = 128
RET_UNROLL = 8
EPS = 1e-6
DILATED_PATTERNS = ((128, 1), (512, 4), (2048, 16))
ATT_BLOCK = 256
ATT_REACH = max(w // 2 for w, _ in DILATED_PATTERNS)
ATT_NEAR = ATT_BLOCK
ATT_CLASSES = DILATED_PATTERNS[-1][1]
assert all(w // 2 <= ATT_NEAR for w, _ in DILATED_PATTERNS[:-1])
ATT_KB = -(-ATT_NEAR // ATT_BLOCK)
ATT_WINDOW = 2 * ATT_KB + 1
ATT_FAR_GROUP = 8
ATT_NEAR_GROUP = 4
MASKED = -1e30
ROW_MAX_INIT = -1e29
N_CHIPS = 4
VMEM_LIMIT_BYTES = 56 * 1024 * 1024
ELEM_BLOCK_BYTES = 2 * 1024 * 1024

ADAM_LR = 0.001
ADAM_B1 = 0.9
ADAM_B2 = 0.999
ADAM_EPS = 1e-08
ADAM_WD = 0.01
ADAM_STEP = 10


def _params(sem=None):
    return pltpu.CompilerParams(dimension_semantics=sem, vmem_limit_bytes=VMEM_LIMIT_BYTES)


def _sigmoid(x):
    return 0.5 * jnp.tanh(0.5 * x) + 0.5


class _Job:
    def __init__(self, *, ins=(), ios=(), outs=(), sems=(), start, finish):
        self.ins, self.ios, self.outs, self.sems = list(ins), list(ios), list(outs), list(sems)
        self.start, self.finish = start, finish

    def results(self):
        return [jax.ShapeDtypeStruct(a.shape, a.dtype) for a in self.ios] + self.outs


def _call(body, *, name, grid, in_specs, out_specs, out_shape, operands, scratch_shapes=(), semantics=None, jobs=()):
    in_specs, out_specs, out_shape = list(in_specs), list(out_specs), list(out_shape)
    scratch_shapes = list(scratch_shapes)
    if not jobs:
        outs = pl.pallas_call(body, name=name, grid=grid, in_specs=in_specs, out_specs=out_specs, out_shape=out_shape,
                              scratch_shapes=scratch_shapes, compiler_params=_params(semantics))(*operands)
        return outs, []
    n_in, n_out, n_scratch = len(in_specs), len(out_specs), len(scratch_shapes)
    extra_in, extra_out, sems, aliases = [], [], [], {}
    for job in jobs:
        extra_in += job.ins
        for t in range(len(job.ios)):
            aliases[n_in + len(extra_in) + t] = n_out + len(extra_out) + t
        extra_in += job.ios
        extra_out += job.results()
        sems += job.sems

    def carried(*refs):
        x_in = refs[n_in:n_in + len(extra_in)]
        x_out = refs[n_in + len(extra_in) + n_out:n_in + len(extra_in) + n_out + len(extra_out)]
        x_sem = refs[len(refs) - len(sems):]
        views, i_in, i_out, i_sem = [], 0, 0, 0
        for job in jobs:
            data = list(x_in[i_in:i_in + len(job.ins)]) + list(x_out[i_out:i_out + len(job.results())])
            views.append((data, x_sem[i_sem:i_sem + len(job.sems)]))
            i_in += len(job.ins) + len(job.ios)
            i_out += len(job.results())
            i_sem += len(job.sems)
        steps = [pl.program_id(d) for d in range(len(grid))]

        @pl.when(functools.reduce(jnp.logical_and, [s == 0 for s in steps]))
        def _():
            for job, (data, sem) in zip(jobs, views):
                job.start(data, sem)

        body(*refs[:n_in], *refs[n_in + len(extra_in):n_in + len(extra_in) + n_out],
             *refs[len(refs) - len(sems) - n_scratch:len(refs) - len(sems)])

        @pl.when(functools.reduce(jnp.logical_and, [s == g - 1 for s, g in zip(steps, grid)]))
        def _():
            for job, (data, sem) in zip(jobs, views):
                job.finish(data, sem)

    hbm = pl.BlockSpec(memory_space=pl.ANY)
    res = pl.pallas_call(
        carried, name=name, grid=grid, in_specs=in_specs + [hbm] * len(extra_in),
        out_specs=out_specs + [hbm] * len(extra_out), out_shape=out_shape + extra_out,
        input_output_aliases=aliases, scratch_shapes=scratch_shapes + sems,
        compiler_params=_params(("arbitrary",) * len(grid)),
    )(*operands, *extra_in)
    carried_results, at = [], n_out
    for job in jobs:
        carried_results.append(list(res[at:at + len(job.results())]))
        at += len(job.results())
    return list(res[:n_out]), carried_results


def _run_jobs(jobs, *, name):
    first = jobs[0]
    n_in, n_io = len(first.ins), len(first.ios)
    out_shape = first.results()
    n_sems = [len(job.sems) for job in jobs]

    def body(*refs):
        data = list(refs[:n_in]) + list(refs[n_in + n_io:n_in + n_io + len(out_shape)])
        at = n_in + n_io + len(out_shape)
        for job, ns in zip(jobs, n_sems):
            job.start(data, refs[at:at + ns])
            job.finish(data, refs[at:at + ns])
            at += ns

    hbm = pl.BlockSpec(memory_space=pl.ANY)
    return pl.pallas_call(
        body, name=name, in_specs=[hbm] * (n_in + n_io), out_specs=[hbm] * len(out_shape), out_shape=out_shape,
        input_output_aliases={n_in + t: t for t in range(n_io)},
        scratch_shapes=[s for job in jobs for s in job.sems],
    )(*first.ins, *first.ios)


def _dot(a, b, ta=False, tb=False):
    return lax.dot_general(a, b, (((0 if ta else 1,), (1 if tb else 0,)), ((), ())),
                           preferred_element_type=F32)


def _tile(n, want):
    t = min(n, want) // 128 * 128
    while n % t:
        t -= 128
    return t


def _a_spec(ta, tm, tk):
    return pl.BlockSpec((tk, tm), lambda i, j, k: (k, i)) if ta else pl.BlockSpec((tm, tk), lambda i, j, k: (i, k))


def _b_spec(tb, tk, tn):
    return pl.BlockSpec((tn, tk), lambda i, j, k: (j, k)) if tb else pl.BlockSpec((tk, tn), lambda i, j, k: (k, j))


def _accumulate(accs, nk, products, finish):
    if nk == 1:
        finish(*products())
        return
    k = pl.program_id(2)

    @pl.when(k == 0)
    def _():
        for acc, p in zip(accs, products()):
            acc[...] = p

    if nk > 2:
        @pl.when(jnp.logical_and(k > 0, k < nk - 1))
        def _():
            for acc, p in zip(accs, products()):
                acc[...] += p

    @pl.when(k == nk - 1)
    def _():
        finish(*[acc[...] + p for acc, p in zip(accs, products())])


def _matmul(a, b, *, name, ta=False, tb=False, out_dtype=F32, residual=None, tm=1024, tn=1024, tk=2048, jobs=()):
    m, kdim = (a.shape[1], a.shape[0]) if ta else a.shape
    n = b.shape[0] if tb else b.shape[1]
    tm, tn, tk = _tile(m, tm), _tile(n, tn), _tile(kdim, tk)
    nk = kdim // tk

    def body(*refs):
        a_ref, b_ref = refs[:2]
        r_ref = refs[2] if residual is not None else None
        o_ref = refs[-1] if nk == 1 else refs[-2]

        def finish(total):
            if residual is not None:
                total = total + r_ref[...]
            o_ref[...] = total.astype(out_dtype)

        _accumulate(refs[-1:] if nk > 1 else (), nk, lambda: (_dot(a_ref[...], b_ref[...], ta, tb),), finish)

    o_spec = pl.BlockSpec((tm, tn), lambda i, j, k: (i, j))
    in_specs = [_a_spec(ta, tm, tk), _b_spec(tb, tk, tn)]
    operands = [a, b]
    if residual is not None:
        in_specs.append(o_spec)
        operands.append(residual)
    (out,), carried = _call(
        body, name=name, grid=(m // tm, n // tn, nk), in_specs=in_specs, out_specs=[o_spec],
        out_shape=[jax.ShapeDtypeStruct((m, n), out_dtype)], operands=operands,
        scratch_shapes=[pltpu.VMEM((tm, tn), F32)] * (nk > 1),
        semantics=("parallel", "parallel", "arbitrary"), jobs=jobs)
    return (out, carried) if jobs else out


def _matmul_pieces_nt(pieces, b, *, name, tm=512, tn=1024, jobs=()):
    m, kp = pieces[0].shape
    n = b.shape[0]
    tm, tn = _tile(m, tm), _tile(n, tn)
    count = len(pieces)

    def body(*refs):
        b_ref, o_ref = refs[count], refs[count + 1]
        total = _dot(refs[0][...], b_ref[:, pl.ds(0, kp)], tb=True)
        for p in range(1, count):
            total = total + _dot(refs[p][...], b_ref[:, pl.ds(p * kp, kp)], tb=True)
        o_ref[...] = total

    piece = pl.BlockSpec((tm, kp), lambda j, i: (i, 0))
    (out,), carried = _call(
        body, name=name, grid=(n // tn, m // tm),
        in_specs=[piece] * count + [pl.BlockSpec((tn, count * kp), lambda j, i: (j, 0))],
        out_specs=[pl.BlockSpec((tm, tn), lambda j, i: (i, j))],
        out_shape=[jax.ShapeDtypeStruct((m, n), F32)], operands=[*pieces, b],
        semantics=("parallel", "parallel"), jobs=jobs)
    return (out, carried) if jobs else out


def _weight_grad_pieces(a, pieces, *, name):
    tokens, m = a.shape
    np_ = pieces[0].shape[1]
    tm = 1024 if m % 1024 == 0 else _tile(m, 1408)
    tn = _tile(np_, 512)
    nb = np_ // tn
    out = None
    for p, piece in enumerate(pieces):
        def body(*refs):
            refs[-1][...] = _dot(refs[0][...], refs[1][...], ta=True)

        in_specs = [pl.BlockSpec((tokens, tm), lambda i, j: (0, i)), pl.BlockSpec((tokens, tn), lambda i, j: (0, j))]
        operands = [a, piece]
        if out is not None:
            in_specs.append(pl.BlockSpec(memory_space=pl.ANY))
            operands.append(out)
        out = pl.pallas_call(
            body, name="%s_%d" % (name, p), grid=(m // tm, nb), in_specs=in_specs,
            out_specs=pl.BlockSpec((tm, tn), lambda i, j, p=p: (i, p * nb + j)),
            out_shape=jax.ShapeDtypeStruct((m, len(pieces) * np_), F32),
            input_output_aliases={2: 0} if len(operands) == 3 else {},
            compiler_params=_params(("parallel", "parallel")),
        )(*operands)
    return out


def _weight_grad(a, g, *, name, jobs=()):
    tokens, m = a.shape
    tm = 1024 if m % 1024 == 0 else _tile(m, 1408)
    return _matmul(a, g, name=name, ta=True, tm=tm, tn=512, tk=tokens, jobs=jobs)


def _swiglu_fwd(n2, w_gate, w_up, *, tm=1024, tn=512, tk=2048, jobs=()):
    m, kdim = n2.shape
    n = w_gate.shape[1]
    tm, tn, tk = _tile(m, tm), _tile(n, tn), _tile(kdim, tk)
    nk = kdim // tk

    def body(a_ref, g_ref, u_ref, gate_ref, up_ref, act_ref, *acc):
        def products():
            a = a_ref[...]
            return _dot(a, g_ref[...]), _dot(a, u_ref[...])

        def finish(g, u):
            gate_ref[...] = g.astype(BF16)
            up_ref[...] = u.astype(BF16)
            act_ref[...] = (g * _sigmoid(g) * u).astype(BF16)

        _accumulate(acc, nk, products, finish)

    o_spec = pl.BlockSpec((tm, tn), lambda i, j, k: (i, j))
    o_shape = jax.ShapeDtypeStruct((m, n), BF16)
    return _call(
        body, name="swiglu_fwd", grid=(m // tm, n // tn, nk),
        in_specs=[_a_spec(False, tm, tk), _b_spec(False, tk, tn), _b_spec(False, tk, tn)],
        out_specs=[o_spec] * 3, out_shape=[o_shape] * 3, operands=[n2, w_gate, w_up],
        scratch_shapes=[pltpu.VMEM((tm, tn), F32)] * (2 * (nk > 1)),
        semantics=("parallel", "parallel", "arbitrary"), jobs=jobs)


def _swiglu_bwd_act(dh2, w_down, gate, up, *, tm=1024, tn=512, tk=2048):
    m, kdim = dh2.shape
    n = w_down.shape[0]
    tm, tn, tk = _tile(m, tm), _tile(n, tn), _tile(kdim, tk)
    nk = kdim // tk

    sub = _tile(tn, 256)

    def body(a_ref, b_ref, gate_ref, up_ref, dgate_ref, dup_ref, *acc):
        def finish(dact, cols=slice(None)):
            g = gate_ref[:, cols].astype(F32)
            u = up_ref[:, cols].astype(F32)
            sg = _sigmoid(g)
            dup_ref[:, cols] = (dact * g * sg).astype(BF16)
            dgate_ref[:, cols] = (dact * u * sg * (1.0 + g * (1.0 - sg))).astype(BF16)

        if nk == 1:
            a = a_ref[...]
            for c in range(tn // sub):
                cols = pl.ds(c * sub, sub)
                finish(_dot(a, b_ref[cols, :], tb=True), cols)
        else:
            _accumulate(acc, nk, lambda: (_dot(a_ref[...], b_ref[...], tb=True),), finish)

    o_spec = pl.BlockSpec((tm, tn), lambda i, j, k: (i, j))
    o_shape = jax.ShapeDtypeStruct((m, n), BF16)
    return pl.pallas_call(
        body, name="swiglu_bwd_act", grid=(m // tm, n // tn, nk),
        in_specs=[_a_spec(False, tm, tk), _b_spec(True, tk, tn), o_spec, o_spec],
        out_specs=[o_spec] * 2, out_shape=[o_shape] * 2,
        scratch_shapes=[pltpu.VMEM((tm, tn), F32)] * (nk > 1),
        compiler_params=_params(("parallel", "parallel", "arbitrary")),
    )(dh2, w_down, gate, up)


def _swiglu_bwd_in(dgate, dup, w_gate, w_up, *, tm=1024, tn=1024, tk=1408, jobs=()):
    m, kdim = dgate.shape
    n = w_gate.shape[0]
    tm, tn, tk = _tile(m, tm), _tile(n, tn), _tile(kdim, tk)
    nk = kdim // tk

    def body(a1_ref, a2_ref, b1_ref, b2_ref, o_ref, *acc):
        def product():
            return (_dot(a1_ref[...], b1_ref[...], tb=True) + _dot(a2_ref[...], b2_ref[...], tb=True),)

        def finish(total):
            o_ref[...] = total

        _accumulate(acc, nk, product, finish)

    a_spec, b_spec = _a_spec(False, tm, tk), _b_spec(True, tk, tn)
    (out,), carried = _call(
        body, name="swiglu_bwd_in", grid=(m // tm, n // tn, nk),
        in_specs=[a_spec, a_spec, b_spec, b_spec],
        out_specs=[pl.BlockSpec((tm, tn), lambda i, j, k: (i, j))],
        out_shape=[jax.ShapeDtypeStruct((m, n), F32)], operands=[dgate, dup, w_gate, w_up],
        scratch_shapes=[pltpu.VMEM((tm, tn), F32)] * (nk > 1),
        semantics=("parallel", "parallel", "arbitrary"), jobs=jobs)
    return out, carried


def _row_block(rows, cols):
    tr = min(rows, max(16, ELEM_BLOCK_BYTES // (4 * cols) // 16 * 16))
    while rows % tr:
        tr -= 16
    return tr


def _rmsnorm_fwd(x, g, *, name, after=None):
    s, d = x.shape
    tr = _row_block(s, d)

    def body(x_ref, g_ref, *rest):
        xv = x_ref[...]
        r = lax.rsqrt(jnp.mean(xv * xv, axis=-1, keepdims=True) + EPS)
        rest[-1][...] = (xv * r * g_ref[...]).astype(BF16)

    row = pl.BlockSpec((tr, d), lambda i: (i, 0))
    in_specs = [row, pl.BlockSpec((1, d), lambda i: (0, 0))]
    operands = [x, g]
    if after is not None:
        in_specs.append(pl.BlockSpec(after.shape, lambda i: (0, 0)))
        operands.append(after)
    return pl.pallas_call(
        body, name=name, grid=(s // tr,), in_specs=in_specs,
        out_specs=row, out_shape=jax.ShapeDtypeStruct((s, d), BF16),
        compiler_params=_params(("parallel",)),
    )(*operands)


def _rmsnorm_bwd_rows(xv, gv, dy):
    r = lax.rsqrt(jnp.mean(xv * xv, axis=-1, keepdims=True) + EPS)
    xhat = xv * r
    dxh = dy * gv
    dx = r * (dxh - xhat * jnp.mean(dxh * xhat, axis=-1, keepdims=True))
    return dx, dy * xhat


def _rmsnorm_bwd(dn, x, g, skip, *, name):
    s, d = x.shape
    tr = _row_block(s, d)

    def body(dn_ref, x_ref, g_ref, skip_ref, dx_ref, dxb_ref, dg_ref):
        dx, dgr = _rmsnorm_bwd_rows(x_ref[...], g_ref[...], dn_ref[...])
        dx = dx + skip_ref[...]
        dx_ref[...] = dx
        dxb_ref[...] = dx.astype(BF16)

        @pl.when(pl.program_id(0) == 0)
        def _():
            dg_ref[...] = jnp.zeros_like(dg_ref)

        dg_ref[...] += jnp.sum(dgr, axis=0, keepdims=True)

    row = pl.BlockSpec((tr, d), lambda i: (i, 0))
    vec = pl.BlockSpec((1, d), lambda i: (0, 0))
    return pl.pallas_call(
        body, name=name, grid=(s // tr,), in_specs=[row, row, vec, row],
        out_specs=[row, row, vec],
        out_shape=[jax.ShapeDtypeStruct((s, d), F32), jax.ShapeDtypeStruct((s, d), BF16),
                   jax.ShapeDtypeStruct((1, d), F32)],
        compiler_params=_params(("arbitrary",)),
    )(dn, x, g, skip)


def _loss_head(h2, g, target):
    s, d = h2.shape
    tr = _row_block(s, d)

    def body(h_ref, g_ref, t_ref, dh_ref, dhb_ref, dg_ref, loss_ref):
        hv = h_ref[...]
        gv = g_ref[...]
        r = lax.rsqrt(jnp.mean(hv * hv, axis=-1, keepdims=True) + EPS)
        err = hv * r * gv - t_ref[...]
        dx, dgr = _rmsnorm_bwd_rows(hv, gv, err * (1.0 / d))
        dh_ref[...] = dx
        dhb_ref[...] = dx.astype(BF16)

        @pl.when(pl.program_id(0) == 0)
        def _():
            dg_ref[...] = jnp.zeros_like(dg_ref)
            loss_ref[...] = jnp.zeros_like(loss_ref)

        dg_ref[...] += jnp.sum(dgr, axis=0, keepdims=True)
        row_loss = jnp.mean(err * err, axis=-1, keepdims=True)
        loss_ref[...] += 0.5 * jnp.sum(row_loss, axis=0, keepdims=True)

    row = pl.BlockSpec((tr, d), lambda i: (i, 0))
    vec = pl.BlockSpec((1, d), lambda i: (0, 0))
    one = pl.BlockSpec((1, 1), lambda i: (0, 0))
    return pl.pallas_call(
        body, name="loss_head", grid=(s // tr,), in_specs=[row, vec, row],
        out_specs=[row, row, vec, one],
        out_shape=[jax.ShapeDtypeStruct((s, d), F32), jax.ShapeDtypeStruct((s, d), BF16),
                   jax.ShapeDtypeStruct((1, d), F32), jax.ShapeDtypeStruct((1, 1), F32)],
        compiler_params=_params(("arbitrary",)),
    )(h2, g, target)


def _attention_bias_tables():
    k = np.arange(-ATT_KB, ATT_KB + 1)[:, None, None]
    delta = k * ATT_BLOCK + np.arange(ATT_BLOCK)[None, None, :] - np.arange(ATT_BLOCK)[None, :, None]
    dist = np.abs(delta)
    count = np.zeros(delta.shape, np.int32)
    for window, dilation in DILATED_PATTERNS:
        count += (delta % dilation == 0) & (dist <= min(window // 2, ATT_NEAR))
    logc = np.where(count > 0, np.log(np.maximum(count, 1)), MASKED)
    return dist.astype(np.float32), logc.astype(np.float32)


def _far_bias_tables(per_class):
    steps = np.abs(np.arange(per_class)[:, None] - np.arange(per_class)[None, :]) * ATT_CLASSES
    valid = (steps > ATT_NEAR) & (steps <= ATT_REACH)
    return steps.astype(np.float32), np.where(valid, 0.0, MASKED).astype(np.float32)


def _to_classes(x):
    s, cols = x.shape
    return jnp.reshape(jnp.transpose(jnp.reshape(x, (s // ATT_CLASSES, ATT_CLASSES, cols)), (1, 0, 2)), (s, cols))


def _from_classes(x):
    s, cols = x.shape
    return jnp.reshape(jnp.transpose(jnp.reshape(x, (ATT_CLASSES, s // ATT_CLASSES, cols)), (1, 0, 2)), (s, cols))


def _head_bias(bias_ref, slope, dist_ref, logc_ref):
    for kk in range(ATT_WINDOW):
        bias_ref[kk] = logc_ref[kk] - slope * dist_ref[kk]
    bias_ref[ATT_WINDOW] = jnp.full((ATT_BLOCK, ATT_BLOCK), MASKED, F32)


def _window_start(i, nq, nwin):
    return jnp.clip(i - ATT_KB, 0, nq - nwin)


def _window_block(j, i):
    rows = pl.ds(pl.multiple_of(j * ATT_BLOCK, ATT_BLOCK), ATT_BLOCK)
    kk = j - i + ATT_KB
    return rows, jnp.where(jnp.logical_and(kk >= 0, kk < ATT_WINDOW), kk, ATT_WINDOW)


def _attention_far_fwd(qkv, slopes, n_heads, jobs=()):
    s = qkv.shape[0]
    per_class = s // ATT_CLASSES
    scale = HEAD_DIM ** -0.5
    dist, logc = _far_bias_tables(per_class)

    def body(slope_ref, q_ref, k_ref, v_ref, dist_ref, logc_ref, o_ref, lse_ref):
        bias = logc_ref[...] - slope_ref[pl.program_id(0)] * dist_ref[...]
        for a in range(ATT_FAR_GROUP):
            rows = pl.ds(a * per_class, per_class)
            sc = _dot(q_ref[rows, :], k_ref[rows, :], tb=True) * scale + bias
            m = jnp.maximum(jnp.max(sc, axis=-1, keepdims=True), ROW_MAX_INIT)
            p = jnp.exp(sc - m)
            l = jnp.maximum(jnp.sum(p, axis=-1, keepdims=True), 1e-30)
            o_ref[rows, :] = (_dot(p.astype(BF16), v_ref[rows, :]) / l).astype(BF16)
            lse_ref[rows, :] = jnp.broadcast_to(m + jnp.log(l), (per_class, HEAD_DIM))

    hh = n_heads
    blk = pl.BlockSpec((ATT_FAR_GROUP * per_class, HEAD_DIM), lambda h, r: (r, h))
    table = pl.BlockSpec(dist.shape, lambda h, r: (0, 0))
    return _call(
        body, name="attention_far_fwd", grid=(hh, ATT_CLASSES // ATT_FAR_GROUP),
        in_specs=[pl.BlockSpec(memory_space=pltpu.SMEM), blk,
                  pl.BlockSpec((ATT_FAR_GROUP * per_class, HEAD_DIM), lambda h, r: (r, hh + h)),
                  pl.BlockSpec((ATT_FAR_GROUP * per_class, HEAD_DIM), lambda h, r: (r, 2 * hh + h)), table, table],
        out_specs=[blk, blk],
        out_shape=[jax.ShapeDtypeStruct((s, hh * HEAD_DIM), BF16), jax.ShapeDtypeStruct((s, hh * HEAD_DIM), F32)],
        operands=[slopes, qkv, qkv, qkv, jnp.asarray(dist), jnp.asarray(logc)],
        semantics=("parallel", "parallel"), jobs=jobs)


def _attention_fwd(proj, slopes, far_out, far_lse, n_heads, jobs=()):
    s = proj.shape[0]
    nq = s // ATT_BLOCK
    scale = HEAD_DIM ** -0.5
    dist, logc = _attention_bias_tables()

    nwin = min(ATT_WINDOW, nq)

    group = math.gcd(ATT_NEAR_GROUP, nq)

    def body(slope_ref, q_ref, k_ref, v_ref, fo_ref, fl_ref, dist_ref, logc_ref, o_ref, lse_ref, bias_ref, s_ref):
        h, step = pl.program_id(0), pl.program_id(1)

        @pl.when(step == 0)
        def _():
            _head_bias(bias_ref, slope_ref[h], dist_ref, logc_ref)

        for a in range(group):
            i = step * group + a
            mine = pl.ds(a * ATT_BLOCK, ATT_BLOCK)
            q = q_ref[mine, :]
            first = _window_start(i, nq, nwin)
            m = jnp.full((ATT_BLOCK, 1), ROW_MAX_INIT, F32)
            for b in range(nwin):
                rows, kk = _window_block(first + b, i)
                sc = _dot(q, k_ref[rows, :], tb=True) * scale + bias_ref[kk]
                s_ref[a * nwin + b] = sc
                m = jnp.maximum(m, jnp.max(sc, axis=-1, keepdims=True))
            l = jnp.zeros((ATT_BLOCK, 1), F32)
            acc = jnp.zeros((ATT_BLOCK, HEAD_DIM), F32)
            for b in range(nwin):
                rows, _ = _window_block(first + b, i)
                p = jnp.exp(s_ref[a * nwin + b] - m)
                l = l + jnp.sum(p, axis=-1, keepdims=True)
                acc = acc + _dot(p.astype(BF16), v_ref[rows, :])
            near_lse = m + jnp.log(l)
            far_lse_col = fl_ref[mine, :1]
            lse = jnp.maximum(near_lse, far_lse_col)
            lse = lse + jnp.log(jnp.exp(near_lse - lse) + jnp.exp(far_lse_col - lse))
            o_ref[mine, :] = (acc * (jnp.exp(near_lse - lse) / l)
                              + fo_ref[mine, :].astype(F32) * jnp.exp(far_lse_col - lse)).astype(BF16)
            lse_ref[mine, :] = jnp.broadcast_to(lse, (ATT_BLOCK, HEAD_DIM))

    hh = n_heads
    blk = pl.BlockSpec((group * ATT_BLOCK, HEAD_DIM), lambda h, i: (i, h))
    table = pl.BlockSpec(dist.shape, lambda h, i: (0, 0, 0))
    return _call(
        body, name="attention_fwd", grid=(hh, nq // group),
        in_specs=[pl.BlockSpec(memory_space=pltpu.SMEM), blk,
                  pl.BlockSpec((s, HEAD_DIM), lambda h, i: (0, hh + h)),
                  pl.BlockSpec((s, HEAD_DIM), lambda h, i: (0, 2 * hh + h)), blk, blk, table, table],
        out_specs=[blk, blk],
        out_shape=[jax.ShapeDtypeStruct((s, hh * HEAD_DIM), BF16), jax.ShapeDtypeStruct((s, hh * HEAD_DIM), F32)],
        operands=[slopes, proj, proj, proj, far_out, far_lse, jnp.asarray(dist), jnp.asarray(logc)],
        scratch_shapes=[pltpu.VMEM((ATT_WINDOW + 1, ATT_BLOCK, ATT_BLOCK), F32),
                        pltpu.VMEM((group * nwin, ATT_BLOCK, ATT_BLOCK), F32)],
        semantics=("parallel", "arbitrary"), jobs=jobs)


def _attention_far_bwd(qkv, slopes, out, dout, lse, n_heads):
    s = qkv.shape[0]
    per_class = s // ATT_CLASSES
    scale = HEAD_DIM ** -0.5
    dist, logc = _far_bias_tables(per_class)

    def body(slope_ref, q_ref, k_ref, v_ref, o_ref, do_ref, lse_ref, dist_ref, logc_ref, dq_ref, dk_ref, dv_ref):
        bias = logc_ref[...] - slope_ref[pl.program_id(0)] * dist_ref[...]
        for a in range(ATT_FAR_GROUP):
            rows = pl.ds(a * per_class, per_class)
            q, k, do = q_ref[rows, :], k_ref[rows, :], do_ref[rows, :]
            delta = jnp.sum(do.astype(F32) * o_ref[rows, :].astype(F32), axis=-1, keepdims=True)
            p = jnp.exp(_dot(q, k, tb=True) * scale + bias - lse_ref[rows, :1])
            dv_ref[rows, :] = _dot(p.astype(BF16), do, ta=True).astype(BF16)
            ds = (p * (_dot(do, v_ref[rows, :], tb=True) - delta) * scale).astype(BF16)
            dk_ref[rows, :] = _dot(ds, q, ta=True).astype(BF16)
            dq_ref[rows, :] = _dot(ds, k).astype(BF16)

    hh = n_heads
    blk = pl.BlockSpec((ATT_FAR_GROUP * per_class, HEAD_DIM), lambda h, r: (r, h))
    table = pl.BlockSpec(dist.shape, lambda h, r: (0, 0))
    o_shape = jax.ShapeDtypeStruct((s, hh * HEAD_DIM), BF16)
    return pl.pallas_call(
        body, name="attention_far_bwd", grid=(hh, ATT_CLASSES // ATT_FAR_GROUP),
        in_specs=[pl.BlockSpec(memory_space=pltpu.SMEM), blk,
                  pl.BlockSpec((ATT_FAR_GROUP * per_class, HEAD_DIM), lambda h, r: (r, hh + h)),
                  pl.BlockSpec((ATT_FAR_GROUP * per_class, HEAD_DIM), lambda h, r: (r, 2 * hh + h)),
                  blk, blk, blk, table, table],
        out_specs=[blk] * 3, out_shape=[o_shape] * 3,
        compiler_params=_params(("parallel", "parallel")),
    )(slopes, qkv, qkv, qkv, out, dout, lse, jnp.asarray(dist), jnp.asarray(logc))


def _attention_bwd(proj, slopes, out, lse, dmixed, far_grads, n_heads, jobs=()):
    s = proj.shape[0]
    nq = s // ATT_BLOCK
    scale = HEAD_DIM ** -0.5
    dist, logc = _attention_bias_tables()

    nwin = min(ATT_WINDOW, nq)
    group = math.gcd(ATT_NEAR_GROUP, nq)

    def body(slope_ref, q_ref, k_ref, v_ref, o_ref, do_ref, lse_ref, fdq_ref, fdk_ref, fdv_ref, dist_ref, logc_ref,
             dq_ref, dk_ref, dv_ref, dk_acc, dv_acc, bias_ref):
        h, step = pl.program_id(0), pl.program_id(1)

        @pl.when(step == 0)
        def _():
            dk_acc[...] = jnp.zeros_like(dk_acc)
            dv_acc[...] = jnp.zeros_like(dv_acc)
            _head_bias(bias_ref, slope_ref[h], dist_ref, logc_ref)

        for a in range(group):
            i = step * group + a
            mine = pl.ds(a * ATT_BLOCK, ATT_BLOCK)
            q = q_ref[mine, :]
            do = do_ref[mine, :]
            lse_col = lse_ref[mine, :1]
            delta = jnp.sum(do.astype(F32) * o_ref[mine, :].astype(F32), axis=-1, keepdims=True)
            first = _window_start(i, nq, nwin)
            dq = jnp.zeros((ATT_BLOCK, HEAD_DIM), F32)
            for b in range(nwin):
                rows, kk = _window_block(first + b, i)
                kj = k_ref[rows, :]
                vj = v_ref[rows, :]
                p = jnp.exp(_dot(q, kj, tb=True) * scale + bias_ref[kk] - lse_col)
                dv_acc[rows, :] += _dot(p.astype(BF16), do, ta=True)
                dp = _dot(do, vj, tb=True)
                ds = (p * (dp - delta) * scale).astype(BF16)
                dk_acc[rows, :] += _dot(ds, q, ta=True)
                dq = dq + _dot(ds, kj)
            dq_ref[mine, :] = (dq + fdq_ref[mine, :].astype(F32)).astype(BF16)

        @pl.when(step == nq // group - 1)
        def _():
            dk_ref[...] = (dk_acc[...] + fdk_ref[...].astype(F32)).astype(BF16)
            dv_ref[...] = (dv_acc[...] + fdv_ref[...].astype(F32)).astype(BF16)

    hh = n_heads
    blk = pl.BlockSpec((group * ATT_BLOCK, HEAD_DIM), lambda h, i: (i, h))
    col = pl.BlockSpec((s, HEAD_DIM), lambda h, i: (0, h))
    table = pl.BlockSpec(dist.shape, lambda h, i: (0, 0, 0))
    o_shape = jax.ShapeDtypeStruct((s, hh * HEAD_DIM), BF16)
    return _call(
        body, name="attention_bwd", grid=(hh, nq // group),
        in_specs=[pl.BlockSpec(memory_space=pltpu.SMEM), blk,
                  pl.BlockSpec((s, HEAD_DIM), lambda h, i: (0, hh + h)),
                  pl.BlockSpec((s, HEAD_DIM), lambda h, i: (0, 2 * hh + h)),
                  blk, blk, blk, blk, col, col, table, table],
        out_specs=[blk, col, col], out_shape=[o_shape] * 3,
        operands=[slopes, proj, proj, proj, out, dmixed, lse, *far_grads, jnp.asarray(dist), jnp.asarray(logc)],
        scratch_shapes=[pltpu.VMEM((s, HEAD_DIM), F32)] * 2
        + [pltpu.VMEM((ATT_WINDOW + 1, ATT_BLOCK, ATT_BLOCK), F32)],
        semantics=("parallel", "arbitrary"), jobs=jobs)


def _ret_decays(lgc, lga, strict_c, strict_a):
    c = RET_CHUNK
    rel = (lax.broadcasted_iota(jnp.int32, (c, c), 0) - lax.broadcasted_iota(jnp.int32, (c, c), 1)).astype(F32)
    in_c = (rel > 0) if strict_c else (rel >= 0)
    in_a = (rel < 0) if strict_a else (rel <= 0)
    mask = (jnp.where(in_c, jnp.exp(lgc * jnp.maximum(rel, 0.0)), 0.0)
            + jnp.where(in_a, jnp.exp(lga * jnp.maximum(-rel, 0.0)), 0.0))
    idx = lax.broadcasted_iota(jnp.int32, (c, 1), 0).astype(F32)
    ones = jnp.ones((1, HEAD_DIM), F32)
    dec = dict(
        rel=rel, mask=mask, idx=idx,
        a_c=jnp.exp(lgc * (idx + 1.0)), b_c=jnp.exp(lgc * (c - 1.0 - idx)), chunk_c=jnp.exp(ones * (lgc * c)),
        a_a=jnp.exp(lga * (c - idx)), b_a=jnp.exp(lga * idx), chunk_a=jnp.exp(ones * (lga * c)),
    )
    return dec


def _scaled(x, col):
    return (x.astype(F32) * col).astype(BF16)


def _chunk_rows(i):
    return pl.ds(pl.multiple_of(i * RET_CHUNK, RET_CHUNK), RET_CHUNK)


def _chunk_loop(nc, step, init, unroll=RET_UNROLL):
    group = math.gcd(nc, unroll)

    def trip(t, carry):
        for u in range(group):
            carry = step(t * group + u, carry)
        return carry

    return lax.fori_loop(0, nc // group, trip, init)


def _retention(a, b, c, lg_c, lg_a, *, strict_c, strict_a, scale, n_heads, name, gate=None, norm_w=None, jobs=()):
    s = a[0].shape[0]
    nc = s // RET_CHUNK
    epilogue = gate is not None

    def body(*refs):
        lgc_ref, lga_ref, a_ref, b_ref, c_ref = refs[:5]
        if epilogue:
            g_ref, w_ref, o_ref, mix_ref, sa_ref = refs[5:]
        else:
            o_ref, sa_ref = refs[5:]
        h = pl.program_id(0)
        dec = _ret_decays(lgc_ref[h], lga_ref[h], strict_c, strict_a)

        def reverse(t, state):
            i = nc - 1 - t
            sa_ref[i] = state.astype(BF16)
            rows = _chunk_rows(i)
            return state * dec["chunk_a"] + _dot(_scaled(b_ref[rows, :], dec["b_a"]), c_ref[rows, :], ta=True)

        _chunk_loop(nc, reverse, jnp.zeros((HEAD_DIM, HEAD_DIM), F32))

        def forward(i, state):
            rows = _chunk_rows(i)
            ai, bi, ci = a_ref[rows, :], b_ref[rows, :], c_ref[rows, :]
            inner = (_dot(ai, bi, tb=True) * dec["mask"]).astype(BF16)
            out = (_dot(inner, ci) + _dot(_scaled(ai, dec["a_c"]), state.astype(BF16))
                   + _dot(_scaled(ai, dec["a_a"]), sa_ref[i])) * scale
            o_ref[rows, :] = out.astype(BF16)
            if epilogue:
                r = lax.rsqrt(jnp.mean(out * out, axis=-1, keepdims=True) + EPS)
                g = g_ref[rows, :].astype(F32)
                mix_ref[rows, :] = (out * r * w_ref[...] * (g * _sigmoid(g))).astype(BF16)
            return state * dec["chunk_c"] + _dot(_scaled(bi, dec["b_c"]), ci, ta=True)

        _chunk_loop(nc, forward, jnp.zeros((HEAD_DIM, HEAD_DIM), F32))

    def col(first):
        return pl.BlockSpec((s, HEAD_DIM), lambda h: (0, first + h))

    smem = pl.BlockSpec(memory_space=pltpu.SMEM)
    in_specs = [smem, smem, col(a[1]), col(b[1]), col(c[1])]
    operands = [lg_c, lg_a, a[0], b[0], c[0]]
    o_shape = jax.ShapeDtypeStruct((s, n_heads * HEAD_DIM), BF16)
    out_specs, out_shape = [col(0)], [o_shape]
    if epilogue:
        in_specs += [col(gate[1]), pl.BlockSpec((1, HEAD_DIM), lambda h: (0, h))]
        operands += [gate[0], norm_w]
        out_specs, out_shape = [col(0)] * 2, [o_shape] * 2
    res, carried = _call(
        body, name=name, grid=(n_heads,), in_specs=in_specs, out_specs=out_specs, out_shape=out_shape,
        operands=operands, scratch_shapes=[pltpu.VMEM((nc, HEAD_DIM, HEAD_DIM), BF16)],
        semantics=("parallel",), jobs=jobs)
    res = res if epilogue else res[0]
    return (res, carried) if jobs else res


def _retention_decay_grads(a, b, c, e, lg_c, lg_a, *, scale, n_heads):
    s = a[0].shape[0]
    nc = s // RET_CHUNK
    cf = float(RET_CHUNK)

    def body(lgc_ref, lga_ref, a_ref, b_ref, c_ref, e_ref, gc_ref, ga_ref, sa_ref, ta_ref):
        h = pl.program_id(0)
        lgc, lga = lgc_ref[h], lga_ref[h]
        dec = _ret_decays(lgc, lga, True, True)
        rel, idx = dec["rel"], dec["idx"]
        w_c = jnp.where(rel > 0, rel * jnp.exp(lgc * jnp.maximum(rel, 0.0)), 0.0)
        w_a = jnp.where(rel < 0, -rel * jnp.exp(lga * jnp.maximum(-rel, 0.0)), 0.0)
        zero = jnp.zeros((HEAD_DIM, HEAD_DIM), F32)

        def reverse(t, carry):
            st, dst = carry
            i = nc - 1 - t
            sa_ref[i] = st.astype(BF16)
            ta_ref[i] = dst.astype(BF16)
            rows = _chunk_rows(i)
            bi, ci = b_ref[rows, :], c_ref[rows, :]
            st_new = st * dec["chunk_a"] + _dot(_scaled(bi, dec["b_a"]), ci, ta=True)
            dst_new = (cf * st + dst) * dec["chunk_a"] + _dot(_scaled(bi, idx * dec["b_a"]), ci, ta=True)
            return st_new, dst_new

        _chunk_loop(nc, reverse, (zero, zero))

        def forward(i, carry):
            st, dst, acc_c, acc_a = carry
            rows = _chunk_rows(i)
            ai, bi, ci = a_ref[rows, :], b_ref[rows, :], c_ref[rows, :]
            ev = e_ref[rows, :].astype(F32)
            pg = _dot(ai, bi, tb=True) * _dot(e_ref[rows, :], ci, tb=True)
            a_c, a_a = _scaled(ai, dec["a_c"]), _scaled(ai, dec["a_a"])
            inter_c = _dot(a_c, st.astype(BF16)) * (idx + 1.0) + _dot(a_c, dst.astype(BF16))
            inter_a = _dot(a_a, sa_ref[i]) * (cf - idx) + _dot(a_a, ta_ref[i])
            acc_c = acc_c + jnp.sum(pg * w_c, axis=0, keepdims=True) + jnp.sum(inter_c * ev, axis=0, keepdims=True)
            acc_a = acc_a + jnp.sum(pg * w_a, axis=0, keepdims=True) + jnp.sum(inter_a * ev, axis=0, keepdims=True)
            st_new = st * dec["chunk_c"] + _dot(_scaled(bi, dec["b_c"]), ci, ta=True)
            dst_new = ((cf * st + dst) * dec["chunk_c"]
                       + _dot(_scaled(bi, (cf - 1.0 - idx) * dec["b_c"]), ci, ta=True))
            return st_new, dst_new, acc_c, acc_a

        row = jnp.zeros((1, HEAD_DIM), F32)
        _, _, acc_c, acc_a = _chunk_loop(nc, forward, (zero, zero, row, row))
        gc_ref[...] = jnp.broadcast_to(jnp.sum(acc_c, axis=-1, keepdims=True) * scale, gc_ref.shape)
        ga_ref[...] = jnp.broadcast_to(jnp.sum(acc_a, axis=-1, keepdims=True) * scale, ga_ref.shape)

    def col(first):
        return pl.BlockSpec((s, HEAD_DIM), lambda h: (0, first + h))

    smem = pl.BlockSpec(memory_space=pltpu.SMEM)
    o_spec = pl.BlockSpec((1, 8, HEAD_DIM), lambda h: (h, 0, 0))
    o_shape = jax.ShapeDtypeStruct((n_heads, 8, HEAD_DIM), F32)
    gc, ga = pl.pallas_call(
        body, name="retention_decay_grads", grid=(n_heads,),
        in_specs=[smem, smem, col(a[1]), col(b[1]), col(c[1]), col(e[1])],
        out_specs=[o_spec] * 2, out_shape=[o_shape] * 2,
        scratch_shapes=[pltpu.VMEM((nc, HEAD_DIM, HEAD_DIM), BF16)] * 2,
        compiler_params=_params(("parallel",)),
    )(lg_c, lg_a, a[0], b[0], c[0], e[0])
    return gc[:, 0, 0], ga[:, 0, 0]


def _ret_gate_bwd(dmixed, first_col, out, proj, gate_col, norm_w, n_heads):
    s = out.shape[0]
    tr = _row_block(s, 8 * HEAD_DIM)

    def body(dm_ref, o_ref, g_ref, w_ref, do_ref, dg_ref, dw_ref):
        dm = dm_ref[...].astype(F32)
        ov = o_ref[...].astype(F32)
        g = g_ref[...].astype(F32)
        w = w_ref[...]
        r = lax.rsqrt(jnp.mean(ov * ov, axis=-1, keepdims=True) + EPS)
        ohat = ov * r
        sg = _sigmoid(g)
        silu = g * sg
        dg_ref[...] = (dm * ohat * w * sg * (1.0 + g * (1.0 - sg))).astype(BF16)
        dohat = dm * w * silu
        do_ref[...] = (r * (dohat - ohat * jnp.mean(dohat * ohat, axis=-1, keepdims=True))).astype(BF16)

        @pl.when(pl.program_id(1) == 0)
        def _():
            dw_ref[...] = jnp.zeros_like(dw_ref)

        dw_ref[...] += jnp.sum(dm * ohat * silu, axis=0, keepdims=True)

    def blk(first):
        return pl.BlockSpec((tr, HEAD_DIM), lambda h, i: (i, first + h))

    vec = pl.BlockSpec((1, HEAD_DIM), lambda h, i: (0, h))
    o_shape = jax.ShapeDtypeStruct((s, n_heads * HEAD_DIM), BF16)
    return pl.pallas_call(
        body, name="ret_gate_bwd", grid=(n_heads, s // tr),
        in_specs=[blk(first_col), blk(0), blk(gate_col), vec],
        out_specs=[blk(0), blk(0), vec],
        out_shape=[o_shape, o_shape, jax.ShapeDtypeStruct((1, n_heads * HEAD_DIM), F32)],
        compiler_params=_params(("parallel", "arbitrary")),
    )(dmixed, out, proj, norm_w)


def _step(x, target, norm_mix_w, ret_decay_fwd, ret_decay_bwd, ret_norm_w, norm_ffn_w, norm_final_w, own,
          w_in_started, pos):
    d = x.shape[1]
    nh = d // (2 * HEAD_DIM)
    scale = HEAD_DIM ** -0.5
    slopes = jnp.exp2(-8.0 * jnp.arange(1, nh + 1, dtype=F32) / nh)
    lg_f = -jnp.exp(ret_decay_fwd)
    lg_b = -jnp.exp(ret_decay_bwd)
    q_r, k_r, v_r, g_r = 3 * nh, 4 * nh, 5 * nh, 6 * nh
    ax = BIG_AXIS

    def gather(names, arrays, stage, part=None):
        return _gather_job(arrays, [ax[k] for k in names], stage, part)

    def add_halves(k, g, received):
        return _add_halves(g, received, ax[k], pos, name="grad_add_halves_" + k)

    def sum_parts(k, g, received, parts):
        return _sum_chip_parts(g, received, parts, ax[k], pos, name="grad_sum_parts_" + k)

    sems, w_in, token = w_in_started
    n1 = _rmsnorm_fwd(x, norm_mix_w, name="norm_mix_fwd", after=token)
    w_in = _split_gather_wait(sems, w_in, ax["w_in"], [n1] + [own[k] for k in BIG if k != "w_in"])
    (w_in,) = _run_jobs([gather(["w_in"], [w_in], "d2d")], name="all_gather_w_in_sibling")
    proj, [[w_gate]] = _matmul(n1, w_in, name="in_proj", out_dtype=BF16, tm=2048,
                               jobs=[gather(["w_gate"], [own["w_gate"]], "ici")])
    qkv_classes = _to_classes(proj[:, :3 * nh * HEAD_DIM])
    (ret, ret_mixed), [[w_gate], [w_out]] = _retention(
        (proj, q_r), (proj, k_r), (proj, v_r), lg_f, lg_b, strict_c=False, strict_a=True, scale=scale, n_heads=nh,
        name="retention_fwd", gate=(proj, g_r), norm_w=ret_norm_w,
        jobs=[gather(["w_gate"], [w_gate], "d2d"), gather(["w_out"], [own["w_out"]], "ici")])
    (far_out, far_lse), [[w_up]] = _attention_far_fwd(
        qkv_classes, slopes, nh, jobs=[gather(["w_up"], [own["w_up"]], "ici", (0, 1, 4))])
    (attn, lse), [[w_out], [w_up]] = _attention_fwd(
        proj, slopes, _from_classes(far_out), _from_classes(far_lse), nh,
        jobs=[gather(["w_out"], [w_out], "d2d"),
              _fuse(gather(["w_up"], [w_up], "d2d", (0, 1, 4)), gather(["w_up"], [w_up], "ici", (1, 2, 4)))])
    mixed = jnp.concatenate([attn, ret_mixed], axis=1)
    h1, [[w_up]] = _matmul(
        mixed, w_out, name="out_proj", residual=x,
        jobs=[_fuse(gather(["w_up"], [w_up], "d2d", (1, 2, 4)), gather(["w_up"], [w_up], "ici", (3, 1, 4)))])
    (w_up,) = _run_jobs([gather(["w_up"], [w_up], "d2d", (3, 1, 4))], name="all_gather_w_up_sibling")
    n2 = _rmsnorm_fwd(h1, norm_ffn_w, name="norm_ffn_fwd")
    (gate, up, act), [[w_down]] = _swiglu_fwd(n2, w_gate, w_up, jobs=[gather(["w_down"], [own["w_down"]], "ici")])
    (w_down,) = _run_jobs([gather(["w_down"], [w_down], "d2d")], name="all_gather_w_down_sibling")
    h2 = _matmul(act, w_down, name="down_proj", residual=h1, tk=2816)
    dh2, dh2_b, d_norm_final, loss = _loss_head(h2, norm_final_w, target)

    dgate, dup = _swiglu_bwd_act(dh2_b, w_down, gate, up)
    g_down = _weight_grad(act, dh2_b, name="grad_w_down")
    g_gate, [[r_down]] = _weight_grad(n2, dgate, name="grad_w_gate", jobs=[_exchange_job([g_down], [ax["w_down"]])])
    s_down = add_halves("w_down", g_down, r_down)
    g_up, [[r_gate], [p_down]] = _weight_grad(
        n2, dup, name="grad_w_up",
        jobs=[_exchange_job([g_gate], [ax["w_gate"]]), _send_sums_job([s_down], [ax["w_down"]], (0, 1, 2))])
    s_gate = add_halves("w_gate", g_gate, r_gate)
    dn2, [[r_up], [p_gate], [p_down]] = _swiglu_bwd_in(
        dgate, dup, w_gate, w_up,
        jobs=[_exchange_job([g_up], [ax["w_up"]]), _send_sums_job([s_gate], [ax["w_gate"]]),
              _send_sums_job([s_down], [ax["w_down"]], (1, 1, 2), landing=[p_down])])
    h_down = sum_parts("w_down", g_down, r_down, p_down)
    s_up = add_halves("w_up", g_up, r_up)
    h_gate = sum_parts("w_gate", g_gate, r_gate, p_gate)
    dh1, dh1_b, d_norm_ffn = _rmsnorm_bwd(dn2, h1, norm_ffn_w, dh2, name="norm_ffn_bwd")

    dmixed, [[gr_down], [p_up]] = _matmul(
        dh1_b, w_out, name="out_proj_bwd", tb=True, out_dtype=BF16,
        jobs=[_join_job([h_down], [ax["w_down"]]), _send_sums_job([s_up], [ax["w_up"]], (0, 1, 4))])
    far_in = [_to_classes(t) for t in (attn, dmixed[:, :nh * HEAD_DIM], lse)]
    g_out, [[p_up]] = _weight_grad(mixed, dh1_b, name="grad_w_out",
                                   jobs=[_send_sums_job([s_up], [ax["w_up"]], (1, 1, 4), landing=[p_up])])
    d_ret, dg_r, d_ret_norm = _ret_gate_bwd(dmixed, nh, ret, proj, g_r, ret_norm_w, nh)
    far_grads = _attention_far_bwd(qkv_classes, slopes, *far_in, nh)
    far_grads = [_from_classes(t) for t in far_grads]
    dq_r, [[gr_gate], [p_up]] = _retention(
        (d_ret, 0), (proj, v_r), (proj, k_r), lg_f, lg_b, strict_c=False, strict_a=True, scale=scale, n_heads=nh,
        name="retention_dq",
        jobs=[_join_job([h_gate], [ax["w_gate"]]), _send_sums_job([s_up], [ax["w_up"]], (2, 1, 4), landing=[p_up])])
    (dq_a, dk_a, dv_a), [[p_up], [r_out]] = _attention_bwd(
        proj, slopes, attn, lse, dmixed, far_grads, nh,
        jobs=[_send_sums_job([s_up], [ax["w_up"]], (3, 1, 4), landing=[p_up]),
              _exchange_job([g_out], [ax["w_out"]])])
    s_out = add_halves("w_out", g_out, r_out)
    h_up = sum_parts("w_up", g_up, r_up, p_up)
    dv_r, [[p_out], [gr_up]] = _retention(
        (proj, k_r), (proj, q_r), (d_ret, 0), lg_b, lg_f, strict_c=True, strict_a=False, scale=scale, n_heads=nh,
        name="retention_dv", jobs=[_send_sums_job([s_out], [ax["w_out"]]), _join_job([h_up], [ax["w_up"]])])
    h_out = sum_parts("w_out", g_out, r_out, p_out)
    dk_r, [[gr_out]] = _retention(
        (proj, v_r), (d_ret, 0), (proj, q_r), lg_b, lg_f, strict_c=True, strict_a=False, scale=scale, n_heads=nh,
        name="retention_dk", jobs=[_join_job([h_out], [ax["w_out"]])])
    dlg_f, dlg_b = _retention_decay_grads((proj, q_r), (proj, k_r), (proj, v_r), (d_ret, 0), lg_f, lg_b,
                                          scale=scale, n_heads=nh)
    dproj = [dq_a, dk_a, dv_a, dq_r, dk_r, dv_r, dg_r]
    g_in = _weight_grad_pieces(n1, dproj, name="grad_w_in")
    (r_in,) = _run_jobs([_exchange_job([g_in], [ax["w_in"]])], name="grad_exchange_w_in")
    s_in = add_halves("w_in", g_in, r_in)
    dn1, [[p_in]] = _matmul_pieces_nt(dproj, w_in, name="in_proj_bwd", jobs=[_send_sums_job([s_in], [ax["w_in"]])])
    dx, _, d_norm_mix = _rmsnorm_bwd(dn1, x, norm_mix_w, dh1, name="norm_mix_bwd")
    h_in = sum_parts("w_in", g_in, r_in, p_in)
    (gr_in,) = _run_jobs([_join_job([h_in], [ax["w_in"]])], name="grad_join_w_in")

    small = dict(loss=loss[0, 0], norm_mix_w=d_norm_mix, ret_decay_fwd=dlg_f * lg_f, ret_decay_bwd=dlg_b * lg_b,
                 ret_norm_w=d_ret_norm, norm_ffn_w=d_norm_ffn, norm_final_w=d_norm_final)
    return dx, dict(w_in=gr_in, w_out=gr_out, w_gate=gr_gate, w_up=gr_up, w_down=gr_down), small


def _mesh_position():
    x, y, c = lax.axis_index("x"), lax.axis_index("y"), lax.axis_index("c")
    chips = [(1 - x, y), (x, 1 - y), (1 - x, 1 - y)]
    return x, y, c, chips


def _span(span):
    if span is None:
        return slice(None)
    start, size, step = span
    return pl.ds(start if isinstance(start, int) else pl.multiple_of(start, step), size)


def _part_rows(part, rows):
    first, count, of = part
    return first * (rows // of), count * (rows // of), rows // of


def _region(ref, axis, shard, half, shard_size, half_size, part=None, total_rows=None):
    along = None if shard is None else (shard * shard_size, shard_size, shard_size)
    other = None if half is None else (half * half_size, half_size, half_size)
    rows, cols = (other, along) if axis == 1 else (along, other)
    if part is not None:
        start, size, _ = rows if rows is not None else (0, total_rows, None)
        offset, size, step = _part_rows(part, size)
        rows = (start + offset, size, step)
    return ref.at[_span(rows), _span(cols)]


def _fuse(first, second):
    assert not (first.ins or first.outs or second.ins or second.outs)
    assert len(first.ios) == len(second.ios) and all(a is b for a, b in zip(first.ios, second.ios))
    cut = len(first.sems)

    def start(refs, sems):
        first.start(refs, sems[:cut])
        second.start(refs, sems[cut:])

    def finish(refs, sems):
        first.finish(refs, sems[:cut])
        second.finish(refs, sems[cut:])

    return _Job(ios=first.ios, sems=first.sems + second.sems, start=start, finish=finish)


def _gather_job(full, axes, stage, part=None):
    n = len(full)

    def copies(refs, sems):
        send_sem, recv_sem = sems
        x, y, c, chips = _mesh_position()
        me = 2 * x + y

        def copy(w, k, shard, half, target):
            rows_cols = full[w].shape
            place = _region(refs[w], axes[w], shard, half, rows_cols[axes[w]] // N_CHIPS, rows_cols[1 - axes[w]] // 2,
                            part)
            return pltpu.make_async_remote_copy(
                src_ref=place, dst_ref=place, send_sem=send_sem.at[w, k], recv_sem=recv_sem.at[w, k],
                device_id=target, device_id_type=MESH)

        def sent(w, k):
            if stage == "ici":
                return copy(w, k, me, c, (chips[k][0], chips[k][1], c))
            return copy(w, k, 2 * chips[k][0] + chips[k][1], c, (x, y, 1 - c))

        def landed(w, k):
            return copy(w, k, 2 * chips[k][0] + chips[k][1], c if stage == "ici" else 1 - c, (x, y, 1 - c))

        return sent, landed

    def start(refs, sems):
        sent, _ = copies(refs, sems)
        for w in range(n):
            for k in range(3):
                sent(w, k).start()

    def finish(refs, sems):
        sent, landed = copies(refs, sems)
        for w in range(n):
            for k in range(3):
                landed(w, k).wait_recv()
                sent(w, k).wait_send()

    return _Job(ios=full, sems=[pltpu.SemaphoreType.DMA((n, 3))] * 2, start=start, finish=finish)


def _exchange_job(grads, axes):
    n = len(grads)

    def half_shape(w):
        return tuple(d // 2 if a != axes[w] else d for a, d in enumerate(grads[w].shape))

    def copy(refs, sems, w):
        x, y, c, _ = _mesh_position()
        return pltpu.make_async_remote_copy(
            src_ref=_region(refs[w], axes[w], None, 1 - c, 0, half_shape(w)[1 - axes[w]]), dst_ref=refs[n + w],
            send_sem=sems[0].at[w], recv_sem=sems[1].at[w], device_id=(x, y, 1 - c), device_id_type=MESH)

    def start(refs, sems):
        for w in range(n):
            copy(refs, sems, w).start()

    def finish(refs, sems):
        for w in range(n):
            copy(refs, sems, w).wait()

    return _Job(ins=grads, outs=[jax.ShapeDtypeStruct(half_shape(w), F32) for w in range(n)],
                sems=[pltpu.SemaphoreType.DMA((n,))] * 2, start=start, finish=finish)


def _half_block_spec(axis, block, half_blocks, use_half):
    if axis == 1:
        if use_half:
            return pl.BlockSpec(block, lambda i, pos: (pos[0] * half_blocks + i, 0))
        return pl.BlockSpec(block, lambda i, pos: (i, 0))
    if use_half:
        return pl.BlockSpec(block, lambda i, pos: (i, pos[0]))
    return pl.BlockSpec(block, lambda i, pos: (i, 0))


def _add_halves(grad, received, axis, pos, *, name):
    rows, cols = received.shape
    tr = _row_block(rows, cols)
    nb = rows // tr

    def body(pos_ref, g_ref, r_ref, o_ref):
        o_ref[...] = (g_ref[...] + r_ref[...]).astype(BF16)

    blk = (tr, cols)
    return pl.pallas_call(
        body, name=name, out_shape=jax.ShapeDtypeStruct((rows, cols), BF16),
        grid_spec=pltpu.PrefetchScalarGridSpec(
            num_scalar_prefetch=1, grid=(nb,),
            in_specs=[_half_block_spec(axis, blk, nb, True), _half_block_spec(axis, blk, nb, False)],
            out_specs=_half_block_spec(axis, blk, nb, False)),
        compiler_params=_params(("parallel",)),
    )(pos, grad, received)


def _send_sums_job(sums, axes, part=None, landing=None):
    n = len(sums)

    def part_shape(w):
        return tuple(d // N_CHIPS if a == axes[w] else d for a, d in enumerate(sums[w].shape))

    def copy(refs, sems, w, k):
        x, y, c, chips = _mesh_position()
        shard = 2 * chips[k][0] + chips[k][1]
        rows = part_shape(w)[0]
        dst = refs[n + w].at[k]
        if part is not None:
            offset, size, _ = _part_rows(part, rows)
            dst = refs[n + w].at[k, pl.ds(offset, size), :]
        return pltpu.make_async_remote_copy(
            src_ref=_region(refs[w], axes[w], shard, None, part_shape(w)[axes[w]], 0, part, rows), dst_ref=dst,
            send_sem=sems[0].at[w, k], recv_sem=sems[1].at[w, k],
            device_id=(chips[k][0], chips[k][1], c), device_id_type=MESH)

    def start(refs, sems):
        for w in range(n):
            for k in range(3):
                copy(refs, sems, w, k).start()

    def finish(refs, sems):
        for w in range(n):
            for k in range(3):
                copy(refs, sems, w, k).wait()

    sems = [pltpu.SemaphoreType.DMA((n, 3))] * 2
    if landing is not None:
        return _Job(ins=sums, ios=landing, sems=sems, start=start, finish=finish)
    return _Job(ins=sums, outs=[jax.ShapeDtypeStruct((3,) + part_shape(w), BF16) for w in range(n)],
                sems=sems, start=start, finish=finish)


def _sum_chip_parts(grad, received, parts, axis, pos, *, name):
    _, rows, cols = parts.shape
    tr = _row_block(rows, cols)
    nb = rows // tr
    blk = (tr, cols)

    def body(pos_ref, g_ref, r_ref, p_ref, o_ref):
        total = g_ref[...] + r_ref[...]
        for k in range(3):
            total = total + p_ref[k].astype(F32)
        o_ref[...] = total

    if axis == 1:
        g_spec = pl.BlockSpec(blk, lambda i, pos: (pos[0] * nb + i, pos[1]))
        r_spec = pl.BlockSpec(blk, lambda i, pos: (i, pos[1]))
        o_spec = pl.BlockSpec(blk, lambda i, pos: (pos[0] * nb + i, 0))
        shard_shape = (2 * rows, cols)
    else:
        g_spec = pl.BlockSpec(blk, lambda i, pos: (pos[1] * nb + i, pos[0]))
        r_spec = pl.BlockSpec(blk, lambda i, pos: (pos[1] * nb + i, 0))
        o_spec = pl.BlockSpec(blk, lambda i, pos: (i, pos[0]))
        shard_shape = (rows, 2 * cols)
    return pl.pallas_call(
        body, name=name, out_shape=jax.ShapeDtypeStruct(shard_shape, F32),
        grid_spec=pltpu.PrefetchScalarGridSpec(
            num_scalar_prefetch=1, grid=(nb,),
            in_specs=[g_spec, r_spec, pl.BlockSpec((3,) + blk, lambda i, pos: (0, i, 0))],
            out_specs=o_spec),
        compiler_params=_params(("parallel",)),
    )(pos, grad, received, parts)


def _join_job(shards, axes):
    n = len(shards)

    def copy(refs, sems, w, other):
        x, y, c, _ = _mesh_position()
        place = _region(refs[w], axes[w], None, 1 - c if other else c, 0, shards[w].shape[1 - axes[w]] // 2)
        return pltpu.make_async_remote_copy(
            src_ref=place, dst_ref=place, send_sem=sems[0].at[w], recv_sem=sems[1].at[w],
            device_id=(x, y, 1 - c), device_id_type=MESH)

    def start(refs, sems):
        for w in range(n):
            copy(refs, sems, w, False).start()

    def finish(refs, sems):
        for w in range(n):
            copy(refs, sems, w, True).wait_recv()
            copy(refs, sems, w, False).wait_send()

    return _Job(ios=shards, sems=[pltpu.SemaphoreType.DMA((n,))] * 2, start=start, finish=finish)


def _all_reduce_small(vec):
    rows, cols = vec.shape

    def body(v_ref, o_ref, land_ref, send_sem, recv_sem):
        x, y, c, _ = _mesh_position()
        me = 4 * x + 2 * y + c
        land_ref[me] = v_ref[...]
        copies = []
        for k in range(1, 8):
            px, py, pc = x ^ (k >> 2), y ^ ((k >> 1) & 1), c ^ (k & 1)
            copies.append(pltpu.make_async_remote_copy(
                src_ref=v_ref, dst_ref=land_ref.at[me], send_sem=send_sem.at[k], recv_sem=recv_sem.at[k],
                device_id=(px, py, pc), device_id_type=MESH))
        for cp in copies:
            cp.start()
        for k in range(1, 8):
            peer = me ^ k
            pltpu.make_async_remote_copy(
                src_ref=v_ref, dst_ref=land_ref.at[peer], send_sem=send_sem.at[k], recv_sem=recv_sem.at[k],
                device_id=(x, y, c), device_id_type=MESH).wait_recv()
        for cp in copies:
            cp.wait_send()
        total = land_ref[0]
        for k in range(1, 8):
            total = total + land_ref[k]
        o_ref[...] = total

    vmem = pl.BlockSpec(memory_space=pltpu.VMEM)
    return pl.pallas_call(
        body, name="all_reduce_small", in_specs=[vmem], out_specs=vmem,
        out_shape=jax.ShapeDtypeStruct((rows, cols), F32),
        scratch_shapes=[pltpu.VMEM((8, rows, cols), F32), pltpu.SemaphoreType.DMA((8,)), pltpu.SemaphoreType.DMA((8,))],
    )(vec)


def _adamw(w, g, m, v, *, name):
    rows, cols = w.shape
    tr = _row_block(rows, cols) if rows % 8 == 0 else rows
    bc1 = 1.0 - ADAM_B1 ** ADAM_STEP
    bc2 = 1.0 - ADAM_B2 ** ADAM_STEP

    def body(w_ref, g_ref, m_ref, v_ref, go_ref, d_ref, mo_ref, vo_ref):
        gv = g_ref[...]
        go_ref[...] = gv
        mn = ADAM_B1 * m_ref[...] + (1.0 - ADAM_B1) * gv
        vn = ADAM_B2 * v_ref[...] + (1.0 - ADAM_B2) * (gv * gv)
        mo_ref[...] = mn
        vo_ref[...] = vn
        d_ref[...] = -ADAM_LR * ((mn / bc1) / (jnp.sqrt(vn / bc2) + ADAM_EPS) + ADAM_WD * w_ref[...])

    blk = pl.BlockSpec((tr, cols), lambda i: (i, 0))
    shape = jax.ShapeDtypeStruct((rows, cols), F32)
    return pl.pallas_call(
        body, name=name, grid=(rows // tr,), in_specs=[blk] * 4, out_specs=[blk] * 4, out_shape=[shape] * 4,
        compiler_params=_params(("parallel",)),
    )(w, g, m, v)


def _to_bf16_in_place(w, axis, pos, *, name, after=None):
    rows, cols = w.shape
    tr = _row_block(rows, cols)
    nb = rows // tr

    def body(pos_ref, w_ref, *rest):
        rest[-1][...] = w_ref[...].astype(BF16)

    if axis == 1:
        o_spec = pl.BlockSpec((tr, cols), lambda i, pos: (i, pos[1]))
        full_shape = (rows, N_CHIPS * cols)
    else:
        o_spec = pl.BlockSpec((tr, cols), lambda i, pos: (pos[1] * nb + i, 0))
        full_shape = (N_CHIPS * rows, cols)
    in_specs = [pl.BlockSpec((tr, cols), lambda i, pos: (i, 0))]
    operands = [pos, w]
    if after is not None:
        in_specs.append(pl.BlockSpec(after.shape, lambda i, pos: (0, 0)))
        operands.append(after)
    return pl.pallas_call(
        body, name=name, out_shape=jax.ShapeDtypeStruct(full_shape, BF16),
        grid_spec=pltpu.PrefetchScalarGridSpec(num_scalar_prefetch=1, grid=(nb,), in_specs=in_specs, out_specs=o_spec),
        compiler_params=_params(("parallel",)),
    )(*operands)


def _split_gather_start(full, axis):
    rows_cols = full.shape

    def body(buf_ref, *rest):
        sems = rest[:6]
        token_ref = rest[7]
        x, y, c, chips = _mesh_position()
        place = _region(buf_ref, axis, 2 * x + y, c, rows_cols[axis] // N_CHIPS, rows_cols[1 - axis] // 2)
        for k in range(3):
            pltpu.make_async_remote_copy(
                src_ref=place, dst_ref=place, send_sem=sems[k], recv_sem=sems[3 + k],
                device_id=(chips[k][0], chips[k][1], c), device_id_type=MESH).start()
        token_ref[...] = jnp.zeros_like(token_ref)

    hbm = pl.BlockSpec(memory_space=pltpu.HBM)
    sem = pl.BlockSpec(memory_space=pltpu.SEMAPHORE)
    res = pl.pallas_call(
        body, name="all_gather_w_in_start",
        out_shape=(*[pltpu.SemaphoreType.DMA(())] * 6, pltpu.HBM(full.shape, full.dtype),
                   jax.ShapeDtypeStruct((8, HEAD_DIM), F32)),
        in_specs=(hbm,), out_specs=(*[sem] * 6, hbm, pl.BlockSpec(memory_space=pltpu.VMEM)),
        input_output_aliases={0: 6},
        compiler_params=pltpu.CompilerParams(has_side_effects=pltpu.SideEffectType.DATAFLOW_SIDE_EFFECTING),
    )(pltpu.with_memory_space_constraint(full, pltpu.HBM))
    return list(res[:6]), res[6], res[7]


def _split_gather_wait(sems, full, axis, after):
    rows_cols = full.shape

    def body(buf_ref, *rest):
        sem_refs = rest[:6]
        x, y, c, chips = _mesh_position()

        def copy(k, shard):
            place = _region(buf_ref, axis, shard, c, rows_cols[axis] // N_CHIPS, rows_cols[1 - axis] // 2)
            return pltpu.make_async_remote_copy(
                src_ref=place, dst_ref=place, send_sem=sem_refs[k], recv_sem=sem_refs[3 + k],
                device_id=(chips[k][0], chips[k][1], c), device_id_type=MESH)

        for k in range(3):
            copy(k, 2 * x + y).wait_send()
            copy(k, 2 * chips[k][0] + chips[k][1]).wait_recv()

    hbm = pl.BlockSpec(memory_space=pltpu.HBM)
    sem = pl.BlockSpec(memory_space=pltpu.SEMAPHORE)
    return pl.pallas_call(
        body, name="all_gather_w_in_wait", out_shape=pltpu.HBM(full.shape, full.dtype),
        in_specs=(hbm, *[sem] * 6, *[pl.BlockSpec(memory_space=pl.ANY)] * len(after)), out_specs=hbm,
        input_output_aliases={0: 0},
        compiler_params=pltpu.CompilerParams(has_side_effects=pltpu.SideEffectType.DATAFLOW_SIDE_EFFECTING),
    )(full, *sems, *after)


BIG = ("w_in", "w_out", "w_gate", "w_up", "w_down")
BIG_AXIS = dict(w_in=1, w_out=0, w_gate=1, w_up=1, w_down=0)
SMALL = ("norm_mix_w", "ret_decay_fwd", "ret_decay_bwd", "ret_norm_w", "norm_ffn_w", "norm_final_w")
ALL_WEIGHTS = ("norm_mix_w", "w_in", "ret_decay_fwd", "ret_decay_bwd", "ret_norm_w", "w_out", "norm_ffn_w",
               "w_gate", "w_up", "w_down", "norm_final_w")
SMALL_ROW = 128 * 8


def _pack_small(small):
    pieces = [jnp.reshape(small["loss"], (1,))] + [jnp.reshape(small[k], (-1,)) for k in SMALL]
    rows = []
    for p in pieces:
        pad = -p.shape[0] % (8 * SMALL_ROW)
        rows.append(jnp.reshape(jnp.pad(p, (0, pad)), (-1, SMALL_ROW)))
    return jnp.concatenate(rows, axis=0)


def _unpack_small(block, like):
    out, row = {}, 0
    for k in ("loss",) + SMALL:
        size = 1 if k == "loss" else like[k].size
        nrows = -(-size // (8 * SMALL_ROW)) * 8
        out[k] = jnp.reshape(block[row:row + nrows], (-1,))[:size]
        row += nrows
    return out


def kernel(x, norm_mix_w, w_in, ret_decay_fwd, ret_decay_bwd, ret_norm_w, w_out, norm_ffn_w, w_gate, w_up, w_down, norm_final_w, loss_target, m_norm_mix_w, m_w_in, m_ret_decay_fwd, m_ret_decay_bwd, m_ret_norm_w, m_w_out, m_norm_ffn_w, m_w_gate, m_w_up, m_w_down, m_norm_final_w, v_norm_mix_w, v_w_in, v_ret_decay_fwd, v_ret_decay_bwd, v_ret_norm_w, v_w_out, v_norm_ffn_w, v_w_gate, v_w_up, v_w_down, v_norm_final_w):
    weights = dict(norm_mix_w=norm_mix_w, w_in=w_in, ret_decay_fwd=ret_decay_fwd, ret_decay_bwd=ret_decay_bwd,
                   ret_norm_w=ret_norm_w, w_out=w_out, norm_ffn_w=norm_ffn_w, w_gate=w_gate, w_up=w_up,
                   w_down=w_down, norm_final_w=norm_final_w)
    m_in = dict(norm_mix_w=m_norm_mix_w, w_in=m_w_in, ret_decay_fwd=m_ret_decay_fwd, ret_decay_bwd=m_ret_decay_bwd,
                ret_norm_w=m_ret_norm_w, w_out=m_w_out, norm_ffn_w=m_norm_ffn_w, w_gate=m_w_gate, w_up=m_w_up,
                w_down=m_w_down, norm_final_w=m_norm_final_w)
    v_in = dict(norm_mix_w=v_norm_mix_w, w_in=v_w_in, ret_decay_fwd=v_ret_decay_fwd, ret_decay_bwd=v_ret_decay_bwd,
                ret_norm_w=v_ret_norm_w, w_out=v_w_out, norm_ffn_w=v_norm_ffn_w, w_gate=v_w_gate, w_up=v_w_up,
                w_down=v_w_down, norm_final_w=v_norm_final_w)
    pos = jnp.stack([lax.axis_index("c"), 2 * lax.axis_index("x") + lax.axis_index("y")]).astype(jnp.int32)

    own = {"w_in": _to_bf16_in_place(weights["w_in"][0], BIG_AXIS["w_in"], pos, name="cast_w_in")}
    w_in_started = _split_gather_start(own["w_in"], BIG_AXIS["w_in"])
    for k in BIG[1:]:
        own[k] = _to_bf16_in_place(weights[k][0], BIG_AXIS[k], pos, name="cast_" + k, after=w_in_started[2])

    dx, grad_w, small = _step(
        x[0], loss_target[0], norm_mix_w, ret_decay_fwd[0], ret_decay_bwd[0], ret_norm_w, norm_ffn_w,
        norm_final_w[None, :], own, w_in_started, pos)

    like = {k: weights[k] for k in SMALL}
    reduced = _unpack_small(_all_reduce_small(_pack_small(small)), like)
    loss = reduced["loss"][0]
    for k in SMALL:
        grad_w[k] = jnp.reshape(reduced[k], (1, -1))

    delta, new_m, new_v = {}, {}, {}
    for k in ALL_WEIGHTS:
        shape = weights[k].shape
        as2d = (lambda t: jnp.reshape(t, (-1, shape[-1])))
        grad_w[k], delta[k], new_m[k], new_v[k] = (jnp.reshape(t, shape) for t in _adamw(
            as2d(weights[k]), as2d(grad_w[k]), as2d(m_in[k]), as2d(v_in[k]), name="adamw_" + k))

    return (loss, dx[None], *[grad_w[k] for k in ALL_WEIGHTS], *[delta[k] for k in ALL_WEIGHTS],
            *[new_m[k] for k in ALL_WEIGHTS], *[new_v[k] for k in ALL_WEIGHTS])
```

```python
import functools
import math

import numpy as np
import jax
import jax.numpy as jnp
from jax import lax
from jax.experimental import pallas as pl
from jax.experimental.pallas import tpu as pltpu

F32 = jnp.float32
BF16 = jnp.bfloat16
MESH = pl.DeviceIdType.MESH

HEAD_DIM = 128
RET_CHUNK = 128
RET_UNROLL = 8
EPS = 1e-6
DILATED_PATTERNS = ((128, 1), (512, 4), (2048, 16))
ATT_BLOCK = 256
ATT_REACH = max(w // 2 for w, _ in DILATED_PATTERNS)
ATT_NEAR = ATT_BLOCK
ATT_CLASSES = DILATED_PATTERNS[-1][1]
assert all(w // 2 <= ATT_NEAR for w, _ in DILATED_PATTERNS[:-1])
ATT_KB = -(-ATT_NEAR // ATT_BLOCK)
ATT_WINDOW = 2 * ATT_KB + 1
ATT_FAR_GROUP = 8
ATT_NEAR_GROUP = 4
MASKED = -1e30
ROW_MAX_INIT = -1e29
N_CHIPS = 4
VMEM_LIMIT_BYTES = 56 * 1024 * 1024
ELEM_BLOCK_BYTES = 2 * 1024 * 1024

ADAM_LR = 0.001
ADAM_B1 = 0.9
ADAM_B2 = 0.999
ADAM_EPS = 1e-08
ADAM_WD = 0.01
ADAM_STEP = 10


def _params(sem=None):
    return pltpu.CompilerParams(dimension_semantics=sem, vmem_limit_bytes=VMEM_LIMIT_BYTES)


def _sigmoid(x):
    return 0.5 * jnp.tanh(0.5 * x) + 0.5


class _Job:
    def __init__(self, *, ins=(), ios=(), outs=(), sems=(), start, finish):
        self.ins, self.ios, self.outs, self.sems = list(ins), list(ios), list(outs), list(sems)
        self.start, self.finish = start, finish

    def results(self):
        return [jax.ShapeDtypeStruct(a.shape, a.dtype) for a in self.ios] + self.outs


def _call(body, *, name, grid, in_specs, out_specs, out_shape, operands, scratch_shapes=(), semantics=None, jobs=(),
          after=()):
    in_specs, out_specs, out_shape = list(in_specs), list(out_specs), list(out_shape)
    scratch_shapes = list(scratch_shapes)
    if not jobs:
        n_real = len(in_specs)

        def ordered(*refs):
            body(*refs[:n_real], *refs[n_real + len(after):])

        outs = pl.pallas_call(
            ordered if after else body, name=name, grid=grid,
            in_specs=in_specs + [pl.BlockSpec(memory_space=pl.ANY)] * len(after), out_specs=out_specs,
            out_shape=out_shape, scratch_shapes=scratch_shapes, compiler_params=_params(semantics))(*operands, *after)
        return outs, []
    n_in, n_out, n_scratch = len(in_specs), len(out_specs), len(scratch_shapes)
    extra_in, extra_out, sems, aliases = [], [], [], {}
    for job in jobs:
        extra_in += job.ins
        for t in range(len(job.ios)):
            aliases[n_in + len(extra_in) + t] = n_out + len(extra_out) + t
        extra_in += job.ios
        extra_out += job.results()
        sems += job.sems

    def carried(*refs):
        x_in = refs[n_in:n_in + len(extra_in)]
        x_out = refs[n_in + len(extra_in) + n_out:n_in + len(extra_in) + n_out + len(extra_out)]
        x_sem = refs[len(refs) - len(sems):]
        views, i_in, i_out, i_sem = [], 0, 0, 0
        for job in jobs:
            data = list(x_in[i_in:i_in + len(job.ins)]) + list(x_out[i_out:i_out + len(job.results())])
            views.append((data, x_sem[i_sem:i_sem + len(job.sems)]))
            i_in += len(job.ins) + len(job.ios)
            i_out += len(job.results())
            i_sem += len(job.sems)
        steps = [pl.program_id(d) for d in range(len(grid))]

        @pl.when(functools.reduce(jnp.logical_and, [s == 0 for s in steps]))
        def _():
            for job, (data, sem) in zip(jobs, views):
                job.start(data, sem)

        body(*refs[:n_in], *refs[n_in + len(extra_in):n_in + len(extra_in) + n_out],
             *refs[len(refs) - len(sems) - n_scratch:len(refs) - len(sems)])

        @pl.when(functools.reduce(jnp.logical_and, [s == g - 1 for s, g in zip(steps, grid)]))
        def _():
            for job, (data, sem) in zip(jobs, views):
                job.finish(data, sem)

    hbm = pl.BlockSpec(memory_space=pl.ANY)
    res = pl.pallas_call(
        carried, name=name, grid=grid, in_specs=in_specs + [hbm] * len(extra_in),
        out_specs=out_specs + [hbm] * len(extra_out), out_shape=out_shape + extra_out,
        input_output_aliases=aliases, scratch_shapes=scratch_shapes + sems,
        compiler_params=_params(("arbitrary",) * len(grid)),
    )(*operands, *extra_in)
    carried_results, at = [], n_out
    for job in jobs:
        carried_results.append(list(res[at:at + len(job.results())]))
        at += len(job.results())
    return list(res[:n_out]), carried_results


def _run_jobs(jobs, *, name):
    first = jobs[0]
    n_in, n_io = len(first.ins), len(first.ios)
    out_shape = first.results()
    n_sems = [len(job.sems) for job in jobs]

    def body(*refs):
        data = list(refs[:n_in]) + list(refs[n_in + n_io:n_in + n_io + len(out_shape)])
        at = n_in + n_io + len(out_shape)
        for job, ns in zip(jobs, n_sems):
            job.start(data, refs[at:at + ns])
            job.finish(data, refs[at:at + ns])
            at += ns

    hbm = pl.BlockSpec(memory_space=pl.ANY)
    return pl.pallas_call(
        body, name=name, in_specs=[hbm] * (n_in + n_io), out_specs=[hbm] * len(out_shape), out_shape=out_shape,
        input_output_aliases={n_in + t: t for t in range(n_io)},
        scratch_shapes=[s for job in jobs for s in job.sems],
    )(*first.ins, *first.ios)


class _SemaphoreGrid:
    def __init__(self, refs, shape):
        self.refs, self.shape = list(refs), tuple(shape)

    @property
    def at(self):
        return self

    def __getitem__(self, index):
        index = index if isinstance(index, tuple) else (index,)
        flat = 0
        for i, extent in zip(index, self.shape):
            flat = flat * extent + i
        return self.refs[flat]


def _semaphore_grids(job, refs):
    grids, at = [], 0
    for sem in job.sems:
        count = math.prod(sem.shape)
        grids.append(_SemaphoreGrid(refs[at:at + count], sem.shape))
        at += count
    return grids


def _split_start(job, *, name):
    arrays = job.ins + job.ios + [lax.empty(s.shape, s.dtype) for s in job.outs]
    n, ns = len(arrays), sum(math.prod(sem.shape) for sem in job.sems)

    def body(*refs):
        job.start(list(refs[:n]), _semaphore_grids(job, refs[n:n + ns]))
        refs[-1][...] = jnp.zeros_like(refs[-1])

    hbm = pl.BlockSpec(memory_space=pltpu.HBM)
    res = pl.pallas_call(
        body, name=name,
        out_shape=(*[pltpu.SemaphoreType.DMA(())] * ns, *[pltpu.HBM(a.shape, a.dtype) for a in arrays],
                   jax.ShapeDtypeStruct((8, HEAD_DIM), F32)),
        in_specs=[hbm] * n,
        out_specs=(*[pl.BlockSpec(memory_space=pltpu.SEMAPHORE)] * ns, *[hbm] * n,
                   pl.BlockSpec(memory_space=pltpu.VMEM)),
        input_output_aliases={t: ns + t for t in range(n)},
        compiler_params=pltpu.CompilerParams(has_side_effects=pltpu.SideEffectType.DATAFLOW_SIDE_EFFECTING),
    )(*[pltpu.with_memory_space_constraint(a, pltpu.HBM) for a in arrays])
    return dict(job=job, sems=list(res[:ns]), arrays=list(res[ns:ns + n]), token=res[-1])


def _split_wait(started, after, *, name):
    job, arrays, sems = started["job"], started["arrays"], started["sems"]
    n, ns = len(arrays), len(sems)

    def body(*refs):
        job.finish(list(refs[:n]), _semaphore_grids(job, refs[n:n + ns]))

    hbm = pl.BlockSpec(memory_space=pltpu.HBM)
    return pl.pallas_call(
        body, name=name, out_shape=[pltpu.HBM(a.shape, a.dtype) for a in arrays],
        in_specs=[hbm] * n + [pl.BlockSpec(memory_space=pltpu.SEMAPHORE)] * ns
        + [pl.BlockSpec(memory_space=pl.ANY)] * len(after),
        out_specs=[hbm] * n, input_output_aliases={t: t for t in range(n)},
        compiler_params=pltpu.CompilerParams(has_side_effects=pltpu.SideEffectType.DATAFLOW_SIDE_EFFECTING),
    )(*arrays, *sems, *after)


def _dot(a, b, ta=False, tb=False):
    return lax.dot_general(a, b, (((0 if ta else 1,), (1 if tb else 0,)), ((), ())),
                           preferred_element_type=F32)


def _tile(n, want):
    t = min(n, want) // 128 * 128
    while n % t:
        t -= 128
    return t


def _a_spec(ta, tm, tk):
    return pl.BlockSpec((tk, tm), lambda i, j, k: (k, i)) if ta else pl.BlockSpec((tm, tk), lambda i, j, k: (i, k))


def _b_spec(tb, tk, tn):
    return pl.BlockSpec((tn, tk), lambda i, j, k: (j, k)) if tb else pl.BlockSpec((tk, tn), lambda i, j, k: (k, j))


def _accumulate(accs, nk, products, finish):
    if nk == 1:
        finish(*products())
        return
    k = pl.program_id(2)

    @pl.when(k == 0)
    def _():
        for acc, p in zip(accs, products()):
            acc[...] = p

    if nk > 2:
        @pl.when(jnp.logical_and(k > 0, k < nk - 1))
        def _():
            for acc, p in zip(accs, products()):
                acc[...] += p

    @pl.when(k == nk - 1)
    def _():
        finish(*[acc[...] + p for acc, p in zip(accs, products())])


def _matmul(a, b, *, name, ta=False, tb=False, out_dtype=F32, residual=None, tm=1024, tn=1024, tk=2048, jobs=()):
    m, kdim = (a.shape[1], a.shape[0]) if ta else a.shape
    n = b.shape[0] if tb else b.shape[1]
    tm, tn, tk = _tile(m, tm), _tile(n, tn), _tile(kdim, tk)
    nk = kdim // tk

    def body(*refs):
        a_ref, b_ref = refs[:2]
        r_ref = refs[2] if residual is not None else None
        o_ref = refs[-1] if nk == 1 else refs[-2]

        def finish(total):
            if residual is not None:
                total = total + r_ref[...]
            o_ref[...] = total.astype(out_dtype)

        _accumulate(refs[-1:] if nk > 1 else (), nk, lambda: (_dot(a_ref[...], b_ref[...], ta, tb),), finish)

    o_spec = pl.BlockSpec((tm, tn), lambda i, j, k: (i, j))
    in_specs = [_a_spec(ta, tm, tk), _b_spec(tb, tk, tn)]
    operands = [a, b]
    if residual is not None:
        in_specs.append(o_spec)
        operands.append(residual)
    (out,), carried = _call(
        body, name=name, grid=(m // tm, n // tn, nk), in_specs=in_specs, out_specs=[o_spec],
        out_shape=[jax.ShapeDtypeStruct((m, n), out_dtype)], operands=operands,
        scratch_shapes=[pltpu.VMEM((tm, tn), F32)] * (nk > 1),
        semantics=("parallel", "parallel", "arbitrary"), jobs=jobs)
    return (out, carried) if jobs else out


def _matmul_pieces_nt(pieces, b, *, name, tm=512, tn=1024, jobs=(), after=()):
    m, kp = pieces[0].shape
    n = b.shape[0]
    tm, tn = _tile(m, tm), _tile(n, tn)
    count = len(pieces)

    def body(*refs):
        b_ref, o_ref = refs[count], refs[count + 1]
        total = _dot(refs[0][...], b_ref[:, pl.ds(0, kp)], tb=True)
        for p in range(1, count):
            total = total + _dot(refs[p][...], b_ref[:, pl.ds(p * kp, kp)], tb=True)
        o_ref[...] = total

    piece = pl.BlockSpec((tm, kp), lambda j, i: (i, 0))
    (out,), carried = _call(
        body, name=name, grid=(n // tn, m // tm),
        in_specs=[piece] * count + [pl.BlockSpec((tn, count * kp), lambda j, i: (j, 0))],
        out_specs=[pl.BlockSpec((tm, tn), lambda j, i: (i, j))],
        out_shape=[jax.ShapeDtypeStruct((m, n), F32)], operands=[*pieces, b],
        semantics=("parallel", "parallel"), jobs=jobs, after=after)
    return (out, carried) if jobs else out


def _weight_grad_pieces(a, pieces, *, name):
    tokens, m = a.shape
    np_ = pieces[0].shape[1]
    tm = 1024 if m % 1024 == 0 else _tile(m, 1408)
    tn = _tile(np_, 512)
    nb = np_ // tn
    out = None
    for p, piece in enumerate(pieces):
        def body(*refs):
            refs[-1][...] = _dot(refs[0][...], refs[1][...], ta=True)

        in_specs = [pl.BlockSpec((tokens, tm), lambda i, j: (0, i)), pl.BlockSpec((tokens, tn), lambda i, j: (0, j))]
        operands = [a, piece]
        if out is not None:
            in_specs.append(pl.BlockSpec(memory_space=pl.ANY))
            operands.append(out)
        out = pl.pallas_call(
            body, name="%s_%d" % (name, p), grid=(m // tm, nb), in_specs=in_specs,
            out_specs=pl.BlockSpec((tm, tn), lambda i, j, p=p: (i, p * nb + j)),
            out_shape=jax.ShapeDtypeStruct((m, len(pieces) * np_), F32),
            input_output_aliases={2: 0} if len(operands) == 3 else {},
            compiler_params=_params(("parallel", "parallel")),
        )(*operands)
    return out


def _weight_grad(a, g, *, name, jobs=()):
    tokens, m = a.shape
    tm = 1024 if m % 1024 == 0 else _tile(m, 1408)
    return _matmul(a, g, name=name, ta=True, tm=tm, tn=512, tk=tokens, jobs=jobs)


def _swiglu_fwd(n2, w_gate, w_up, *, tm=1024, tn=512, tk=2048, jobs=()):
    m, kdim = n2.shape
    n = w_gate.shape[1]
    tm, tn, tk = _tile(m, tm), _tile(n, tn), _tile(kdim, tk)
    nk = kdim // tk

    def body(a_ref, g_ref, u_ref, gate_ref, up_ref, act_ref, *acc):
        def products():
            a = a_ref[...]
            return _dot(a, g_ref[...]), _dot(a, u_ref[...])

        def finish(g, u):
            gate_ref[...] = g.astype(BF16)
            up_ref[...] = u.astype(BF16)
            act_ref[...] = (g * _sigmoid(g) * u).astype(BF16)

        _accumulate(acc, nk, products, finish)

    o_spec = pl.BlockSpec((tm, tn), lambda i, j, k: (i, j))
    o_shape = jax.ShapeDtypeStruct((m, n), BF16)
    return _call(
        body, name="swiglu_fwd", grid=(m // tm, n // tn, nk),
        in_specs=[_a_spec(False, tm, tk), _b_spec(False, tk, tn), _b_spec(False, tk, tn)],
        out_specs=[o_spec] * 3, out_shape=[o_shape] * 3, operands=[n2, w_gate, w_up],
        scratch_shapes=[pltpu.VMEM((tm, tn), F32)] * (2 * (nk > 1)),
        semantics=("parallel", "parallel", "arbitrary"), jobs=jobs)


def _swiglu_bwd_act(dh2, w_down, gate, up, *, tm=1024, tn=512, tk=2048):
    m, kdim = dh2.shape
    n = w_down.shape[0]
    tm, tn, tk = _tile(m, tm), _tile(n, tn), _tile(kdim, tk)
    nk = kdim // tk

    sub = _tile(tn, 256)

    def body(a_ref, b_ref, gate_ref, up_ref, dgate_ref, dup_ref, *acc):
        def finish(dact, cols=slice(None)):
            g = gate_ref[:, cols].astype(F32)
            u = up_ref[:, cols].astype(F32)
            sg = _sigmoid(g)
            dup_ref[:, cols] = (dact * g * sg).astype(BF16)
            dgate_ref[:, cols] = (dact * u * sg * (1.0 + g * (1.0 - sg))).astype(BF16)

        if nk == 1:
            a = a_ref[...]
            for c in range(tn // sub):
                cols = pl.ds(c * sub, sub)
                finish(_dot(a, b_ref[cols, :], tb=True), cols)
        else:
            _accumulate(acc, nk, lambda: (_dot(a_ref[...], b_ref[...], tb=True),), finish)

    o_spec = pl.BlockSpec((tm, tn), lambda i, j, k: (i, j))
    o_shape = jax.ShapeDtypeStruct((m, n), BF16)
    return pl.pallas_call(
        body, name="swiglu_bwd_act", grid=(m // tm, n // tn, nk),
        in_specs=[_a_spec(False, tm, tk), _b_spec(True, tk, tn), o_spec, o_spec],
        out_specs=[o_spec] * 2, out_shape=[o_shape] * 2,
        scratch_shapes=[pltpu.VMEM((tm, tn), F32)] * (nk > 1),
        compiler_params=_params(("parallel", "parallel", "arbitrary")),
    )(dh2, w_down, gate, up)


def _swiglu_bwd_in(dgate, dup, w_gate, w_up, *, tm=1024, tn=1024, tk=1408, jobs=()):
    m, kdim = dgate.shape
    n = w_gate.shape[0]
    tm, tn, tk = _tile(m, tm), _tile(n, tn), _tile(kdim, tk)
    nk = kdim // tk

    def body(a1_ref, a2_ref, b1_ref, b2_ref, o_ref, *acc):
        def product():
            return (_dot(a1_ref[...], b1_ref[...], tb=True) + _dot(a2_ref[...], b2_ref[...], tb=True),)

        def finish(total):
            o_ref[...] = total

        _accumulate(acc, nk, product, finish)

    a_spec, b_spec = _a_spec(False, tm, tk), _b_spec(True, tk, tn)
    (out,), carried = _call(
        body, name="swiglu_bwd_in", grid=(m // tm, n // tn, nk),
        in_specs=[a_spec, a_spec, b_spec, b_spec],
        out_specs=[pl.BlockSpec((tm, tn), lambda i, j, k: (i, j))],
        out_shape=[jax.ShapeDtypeStruct((m, n), F32)], operands=[dgate, dup, w_gate, w_up],
        scratch_shapes=[pltpu.VMEM((tm, tn), F32)] * (nk > 1),
        semantics=("parallel", "parallel", "arbitrary"), jobs=jobs)
    return out, carried


def _row_block(rows, cols):
    tr = min(rows, max(16, ELEM_BLOCK_BYTES // (4 * cols) // 16 * 16))
    while rows % tr:
        tr -= 16
    return tr


def _rmsnorm_fwd(x, g, *, name, after=None):
    s, d = x.shape
    tr = _row_block(s, d)

    def body(x_ref, g_ref, *rest):
        xv = x_ref[...]
        r = lax.rsqrt(jnp.mean(xv * xv, axis=-1, keepdims=True) + EPS)
        rest[-1][...] = (xv * r * g_ref[...]).astype(BF16)

    row = pl.BlockSpec((tr, d), lambda i: (i, 0))
    in_specs = [row, pl.BlockSpec((1, d), lambda i: (0, 0))]
    operands = [x, g]
    if after is not None:
        in_specs.append(pl.BlockSpec(after.shape, lambda i: (0, 0)))
        operands.append(after)
    return pl.pallas_call(
        body, name=name, grid=(s // tr,), in_specs=in_specs,
        out_specs=row, out_shape=jax.ShapeDtypeStruct((s, d), BF16),
        compiler_params=_params(("parallel",)),
    )(*operands)


def _rmsnorm_bwd_rows(xv, gv, dy):
    r = lax.rsqrt(jnp.mean(xv * xv, axis=-1, keepdims=True) + EPS)
    xhat = xv * r
    dxh = dy * gv
    dx = r * (dxh - xhat * jnp.mean(dxh * xhat, axis=-1, keepdims=True))
    return dx, dy * xhat


def _rmsnorm_bwd(dn, x, g, skip, *, name, after=()):
    s, d = x.shape
    tr = _row_block(s, d)

    def body(dn_ref, x_ref, g_ref, skip_ref, *rest):
        dx_ref, dxb_ref, dg_ref = rest[len(after):]
        dx, dgr = _rmsnorm_bwd_rows(x_ref[...], g_ref[...], dn_ref[...])
        dx = dx + skip_ref[...]
        dx_ref[...] = dx
        dxb_ref[...] = dx.astype(BF16)

        @pl.when(pl.program_id(0) == 0)
        def _():
            dg_ref[...] = jnp.zeros_like(dg_ref)

        dg_ref[...] += jnp.sum(dgr, axis=0, keepdims=True)

    row = pl.BlockSpec((tr, d), lambda i: (i, 0))
    vec = pl.BlockSpec((1, d), lambda i: (0, 0))
    return pl.pallas_call(
        body, name=name, grid=(s // tr,),
        in_specs=[row, row, vec, row] + [pl.BlockSpec(memory_space=pl.ANY)] * len(after),
        out_specs=[row, row, vec],
        out_shape=[jax.ShapeDtypeStruct((s, d), F32), jax.ShapeDtypeStruct((s, d), BF16),
                   jax.ShapeDtypeStruct((1, d), F32)],
        compiler_params=_params(("arbitrary",)),
    )(dn, x, g, skip, *after)


def _loss_head(h2, g, target):
    s, d = h2.shape
    tr = _row_block(s, d)

    def body(h_ref, g_ref, t_ref, dh_ref, dhb_ref, dg_ref, loss_ref):
        hv = h_ref[...]
        gv = g_ref[...]
        r = lax.rsqrt(jnp.mean(hv * hv, axis=-1, keepdims=True) + EPS)
        err = hv * r * gv - t_ref[...]
        dx, dgr = _rmsnorm_bwd_rows(hv, gv, err * (1.0 / d))
        dh_ref[...] = dx
        dhb_ref[...] = dx.astype(BF16)

        @pl.when(pl.program_id(0) == 0)
        def _():
            dg_ref[...] = jnp.zeros_like(dg_ref)
            loss_ref[...] = jnp.zeros_like(loss_ref)

        dg_ref[...] += jnp.sum(dgr, axis=0, keepdims=True)
        row_loss = jnp.mean(err * err, axis=-1, keepdims=True)
        loss_ref[...] += 0.5 * jnp.sum(row_loss, axis=0, keepdims=True)

    row = pl.BlockSpec((tr, d), lambda i: (i, 0))
    vec = pl.BlockSpec((1, d), lambda i: (0, 0))
    one = pl.BlockSpec((1, 1), lambda i: (0, 0))
    return pl.pallas_call(
        body, name="loss_head", grid=(s // tr,), in_specs=[row, vec, row],
        out_specs=[row, row, vec, one],
        out_shape=[jax.ShapeDtypeStruct((s, d), F32), jax.ShapeDtypeStruct((s, d), BF16),
                   jax.ShapeDtypeStruct((1, d), F32), jax.ShapeDtypeStruct((1, 1), F32)],
        compiler_params=_params(("arbitrary",)),
    )(h2, g, target)


def _attention_bias_tables():
    k = np.arange(-ATT_KB, ATT_KB + 1)[:, None, None]
    delta = k * ATT_BLOCK + np.arange(ATT_BLOCK)[None, None, :] - np.arange(ATT_BLOCK)[None, :, None]
    dist = np.abs(delta)
    count = np.zeros(delta.shape, np.int32)
    for window, dilation in DILATED_PATTERNS:
        count += (delta % dilation == 0) & (dist <= min(window // 2, ATT_NEAR))
    logc = np.where(count > 0, np.log(np.maximum(count, 1)), MASKED)
    return dist.astype(np.float32), logc.astype(np.float32)


def _far_bias_tables(per_class):
    steps = np.abs(np.arange(per_class)[:, None] - np.arange(per_class)[None, :]) * ATT_CLASSES
    valid = (steps > ATT_NEAR) & (steps <= ATT_REACH)
    return steps.astype(np.float32), np.where(valid, 0.0, MASKED).astype(np.float32)


def _to_classes(x):
    s, cols = x.shape
    return jnp.reshape(jnp.transpose(jnp.reshape(x, (s // ATT_CLASSES, ATT_CLASSES, cols)), (1, 0, 2)), (s, cols))


def _from_classes(x):
    s, cols = x.shape
    return jnp.reshape(jnp.transpose(jnp.reshape(x, (ATT_CLASSES, s // ATT_CLASSES, cols)), (1, 0, 2)), (s, cols))


def _head_bias(bias_ref, slope, dist_ref, logc_ref):
    for kk in range(ATT_WINDOW):
        bias_ref[kk] = logc_ref[kk] - slope * dist_ref[kk]
    bias_ref[ATT_WINDOW] = jnp.full((ATT_BLOCK, ATT_BLOCK), MASKED, F32)


def _window_start(i, nq, nwin):
    return jnp.clip(i - ATT_KB, 0, nq - nwin)


def _window_block(j, i):
    rows = pl.ds(pl.multiple_of(j * ATT_BLOCK, ATT_BLOCK), ATT_BLOCK)
    kk = j - i + ATT_KB
    return rows, jnp.where(jnp.logical_and(kk >= 0, kk < ATT_WINDOW), kk, ATT_WINDOW)


def _attention_far_fwd(qkv, slopes, n_heads, jobs=()):
    s = qkv.shape[0]
    per_class = s // ATT_CLASSES
    scale = HEAD_DIM ** -0.5
    dist, logc = _far_bias_tables(per_class)

    def body(slope_ref, q_ref, k_ref, v_ref, dist_ref, logc_ref, o_ref, lse_ref):
        bias = logc_ref[...] - slope_ref[pl.program_id(0)] * dist_ref[...]
        for a in range(ATT_FAR_GROUP):
            rows = pl.ds(a * per_class, per_class)
            sc = _dot(q_ref[rows, :], k_ref[rows, :], tb=True) * scale + bias
            m = jnp.maximum(jnp.max(sc, axis=-1, keepdims=True), ROW_MAX_INIT)
            p = jnp.exp(sc - m)
            l = jnp.maximum(jnp.sum(p, axis=-1, keepdims=True), 1e-30)
            o_ref[rows, :] = (_dot(p.astype(BF16), v_ref[rows, :]) / l).astype(BF16)
            lse_ref[rows, :] = jnp.broadcast_to(m + jnp.log(l), (per_class, HEAD_DIM))

    hh = n_heads
    blk = pl.BlockSpec((ATT_FAR_GROUP * per_class, HEAD_DIM), lambda h, r: (r, h))
    table = pl.BlockSpec(dist.shape, lambda h, r: (0, 0))
    return _call(
        body, name="attention_far_fwd", grid=(hh, ATT_CLASSES // ATT_FAR_GROUP),
        in_specs=[pl.BlockSpec(memory_space=pltpu.SMEM), blk,
                  pl.BlockSpec((ATT_FAR_GROUP * per_class, HEAD_DIM), lambda h, r: (r, hh + h)),
                  pl.BlockSpec((ATT_FAR_GROUP * per_class, HEAD_DIM), lambda h, r: (r, 2 * hh + h)), table, table],
        out_specs=[blk, blk],
        out_shape=[jax.ShapeDtypeStruct((s, hh * HEAD_DIM), BF16), jax.ShapeDtypeStruct((s, hh * HEAD_DIM), F32)],
        operands=[slopes, qkv, qkv, qkv, jnp.asarray(dist), jnp.asarray(logc)],
        semantics=("parallel", "parallel"), jobs=jobs)


def _attention_fwd(proj, slopes, far_out, far_lse, n_heads, jobs=()):
    s = proj.shape[0]
    nq = s // ATT_BLOCK
    scale = HEAD_DIM ** -0.5
    dist, logc = _attention_bias_tables()

    nwin = min(ATT_WINDOW, nq)

    group = math.gcd(ATT_NEAR_GROUP, nq)

    def body(slope_ref, q_ref, k_ref, v_ref, fo_ref, fl_ref, dist_ref, logc_ref, o_ref, lse_ref, bias_ref, s_ref):
        h, step = pl.program_id(0), pl.program_id(1)

        @pl.when(step == 0)
        def _():
            _head_bias(bias_ref, slope_ref[h], dist_ref, logc_ref)

        for a in range(group):
            i = step * group + a
            mine = pl.ds(a * ATT_BLOCK, ATT_BLOCK)
            q = q_ref[mine, :]
            first = _window_start(i, nq, nwin)
            m = jnp.full((ATT_BLOCK, 1), ROW_MAX_INIT, F32)
            for b in range(nwin):
                rows, kk = _window_block(first + b, i)
                sc = _dot(q, k_ref[rows, :], tb=True) * scale + bias_ref[kk]
                s_ref[a * nwin + b] = sc
                m = jnp.maximum(m, jnp.max(sc, axis=-1, keepdims=True))
            l = jnp.zeros((ATT_BLOCK, 1), F32)
            acc = jnp.zeros((ATT_BLOCK, HEAD_DIM), F32)
            for b in range(nwin):
                rows, _ = _window_block(first + b, i)
                p = jnp.exp(s_ref[a * nwin + b] - m)
                l = l + jnp.sum(p, axis=-1, keepdims=True)
                acc = acc + _dot(p.astype(BF16), v_ref[rows, :])
            near_lse = m + jnp.log(l)
            far_lse_col = fl_ref[mine, :1]
            lse = jnp.maximum(near_lse, far_lse_col)
            lse = lse + jnp.log(jnp.exp(near_lse - lse) + jnp.exp(far_lse_col - lse))
            o_ref[mine, :] = (acc * (jnp.exp(near_lse - lse) / l)
                              + fo_ref[mine, :].astype(F32) * jnp.exp(far_lse_col - lse)).astype(BF16)
            lse_ref[mine, :] = jnp.broadcast_to(lse, (ATT_BLOCK, HEAD_DIM))

    hh = n_heads
    blk = pl.BlockSpec((group * ATT_BLOCK, HEAD_DIM), lambda h, i: (i, h))
    table = pl.BlockSpec(dist.shape, lambda h, i: (0, 0, 0))
    return _call(
        body, name="attention_fwd", grid=(hh, nq // group),
        in_specs=[pl.BlockSpec(memory_space=pltpu.SMEM), blk,
                  pl.BlockSpec((s, HEAD_DIM), lambda h, i: (0, hh + h)),
                  pl.BlockSpec((s, HEAD_DIM), lambda h, i: (0, 2 * hh + h)), blk, blk, table, table],
        out_specs=[blk, blk],
        out_shape=[jax.ShapeDtypeStruct((s, hh * HEAD_DIM), BF16), jax.ShapeDtypeStruct((s, hh * HEAD_DIM), F32)],
        operands=[slopes, proj, proj, proj, far_out, far_lse, jnp.asarray(dist), jnp.asarray(logc)],
        scratch_shapes=[pltpu.VMEM((ATT_WINDOW + 1, ATT_BLOCK, ATT_BLOCK), F32),
                        pltpu.VMEM((group * nwin, ATT_BLOCK, ATT_BLOCK), F32)],
        semantics=("parallel", "arbitrary"), jobs=jobs)


def _attention_far_bwd(qkv, slopes, out, dout, lse, n_heads):
    s = qkv.shape[0]
    per_class = s // ATT_CLASSES
    scale = HEAD_DIM ** -0.5
    dist, logc = _far_bias_tables(per_class)

    def body(slope_ref, q_ref, k_ref, v_ref, o_ref, do_ref, lse_ref, dist_ref, logc_ref, dq_ref, dk_ref, dv_ref):
        bias = logc_ref[...] - slope_ref[pl.program_id(0)] * dist_ref[...]
        for a in range(ATT_FAR_GROUP):
            rows = pl.ds(a * per_class, per_class)
            q, k, do = q_ref[rows, :], k_ref[rows, :], do_ref[rows, :]
            delta = jnp.sum(do.astype(F32) * o_ref[rows, :].astype(F32), axis=-1, keepdims=True)
            p = jnp.exp(_dot(q, k, tb=True) * scale + bias - lse_ref[rows, :1])
            dv_ref[rows, :] = _dot(p.astype(BF16), do, ta=True).astype(BF16)
            ds = (p * (_dot(do, v_ref[rows, :], tb=True) - delta) * scale).astype(BF16)
            dk_ref[rows, :] = _dot(ds, q, ta=True).astype(BF16)
            dq_ref[rows, :] = _dot(ds, k).astype(BF16)

    hh = n_heads
    blk = pl.BlockSpec((ATT_FAR_GROUP * per_class, HEAD_DIM), lambda h, r: (r, h))
    table = pl.BlockSpec(dist.shape, lambda h, r: (0, 0))
    o_shape = jax.ShapeDtypeStruct((s, hh * HEAD_DIM), BF16)
    return pl.pallas_call(
        body, name="attention_far_bwd", grid=(hh, ATT_CLASSES // ATT_FAR_GROUP),
        in_specs=[pl.BlockSpec(memory_space=pltpu.SMEM), blk,
                  pl.BlockSpec((ATT_FAR_GROUP * per_class, HEAD_DIM), lambda h, r: (r, hh + h)),
                  pl.BlockSpec((ATT_FAR_GROUP * per_class, HEAD_DIM), lambda h, r: (r, 2 * hh + h)),
                  blk, blk, blk, table, table],
        out_specs=[blk] * 3, out_shape=[o_shape] * 3,
        compiler_params=_params(("parallel", "parallel")),
    )(slopes, qkv, qkv, qkv, out, dout, lse, jnp.asarray(dist), jnp.asarray(logc))


def _attention_bwd(proj, slopes, out, lse, dmixed, far_grads, n_heads, jobs=()):
    s = proj.shape[0]
    nq = s // ATT_BLOCK
    scale = HEAD_DIM ** -0.5
    dist, logc = _attention_bias_tables()

    nwin = min(ATT_WINDOW, nq)
    group = math.gcd(ATT_NEAR_GROUP, nq)

    def body(slope_ref, q_ref, k_ref, v_ref, o_ref, do_ref, lse_ref, fdq_ref, fdk_ref, fdv_ref, dist_ref, logc_ref,
             dq_ref, dk_ref, dv_ref, dk_acc, dv_acc, bias_ref):
        h, step = pl.program_id(0), pl.program_id(1)

        @pl.when(step == 0)
        def _():
            dk_acc[...] = jnp.zeros_like(dk_acc)
            dv_acc[...] = jnp.zeros_like(dv_acc)
            _head_bias(bias_ref, slope_ref[h], dist_ref, logc_ref)

        for a in range(group):
            i = step * group + a
            mine = pl.ds(a * ATT_BLOCK, ATT_BLOCK)
            q = q_ref[mine, :]
            do = do_ref[mine, :]
            lse_col = lse_ref[mine, :1]
            delta = jnp.sum(do.astype(F32) * o_ref[mine, :].astype(F32), axis=-1, keepdims=True)
            first = _window_start(i, nq, nwin)
            dq = jnp.zeros((ATT_BLOCK, HEAD_DIM), F32)
            for b in range(nwin):
                rows, kk = _window_block(first + b, i)
                kj = k_ref[rows, :]
                vj = v_ref[rows, :]
                p = jnp.exp(_dot(q, kj, tb=True) * scale + bias_ref[kk] - lse_col)
                dv_acc[rows, :] += _dot(p.astype(BF16), do, ta=True)
                dp = _dot(do, vj, tb=True)
                ds = (p * (dp - delta) * scale).astype(BF16)
                dk_acc[rows, :] += _dot(ds, q, ta=True)
                dq = dq + _dot(ds, kj)
            dq_ref[mine, :] = (dq + fdq_ref[mine, :].astype(F32)).astype(BF16)

        @pl.when(step == nq // group - 1)
        def _():
            dk_ref[...] = (dk_acc[...] + fdk_ref[...].astype(F32)).astype(BF16)
            dv_ref[...] = (dv_acc[...] + fdv_ref[...].astype(F32)).astype(BF16)

    hh = n_heads
    blk = pl.BlockSpec((group * ATT_BLOCK, HEAD_DIM), lambda h, i: (i, h))
    col = pl.BlockSpec((s, HEAD_DIM), lambda h, i: (0, h))
    table = pl.BlockSpec(dist.shape, lambda h, i: (0, 0, 0))
    o_shape = jax.ShapeDtypeStruct((s, hh * HEAD_DIM), BF16)
    return _call(
        body, name="attention_bwd", grid=(hh, nq // group),
        in_specs=[pl.BlockSpec(memory_space=pltpu.SMEM), blk,
                  pl.BlockSpec((s, HEAD_DIM), lambda h, i: (0, hh + h)),
                  pl.BlockSpec((s, HEAD_DIM), lambda h, i: (0, 2 * hh + h)),
                  blk, blk, blk, blk, col, col, table, table],
        out_specs=[blk, col, col], out_shape=[o_shape] * 3,
        operands=[slopes, proj, proj, proj, out, dmixed, lse, *far_grads, jnp.asarray(dist), jnp.asarray(logc)],
        scratch_shapes=[pltpu.VMEM((s, HEAD_DIM), F32)] * 2
        + [pltpu.VMEM((ATT_WINDOW + 1, ATT_BLOCK, ATT_BLOCK), F32)],
        semantics=("parallel", "arbitrary"), jobs=jobs)


def _ret_decays(lgc, lga, strict_c, strict_a):
    c = RET_CHUNK
    rel = (lax.broadcasted_iota(jnp.int32, (c, c), 0) - lax.broadcasted_iota(jnp.int32, (c, c), 1)).astype(F32)
    in_c = (rel > 0) if strict_c else (rel >= 0)
    in_a = (rel < 0) if strict_a else (rel <= 0)
    mask = (jnp.where(in_c, jnp.exp(lgc * jnp.maximum(rel, 0.0)), 0.0)
            + jnp.where(in_a, jnp.exp(lga * jnp.maximum(-rel, 0.0)), 0.0))
    idx = lax.broadcasted_iota(jnp.int32, (c, 1), 0).astype(F32)
    ones = jnp.ones((1, HEAD_DIM), F32)
    dec = dict(
        rel=rel, mask=mask, idx=idx,
        a_c=jnp.exp(lgc * (idx + 1.0)), b_c=jnp.exp(lgc * (c - 1.0 - idx)), chunk_c=jnp.exp(ones * (lgc * c)),
        a_a=jnp.exp(lga * (c - idx)), b_a=jnp.exp(lga * idx), chunk_a=jnp.exp(ones * (lga * c)),
    )
    return dec


def _scaled(x, col):
    return (x.astype(F32) * col).astype(BF16)


def _chunk_rows(i):
    return pl.ds(pl.multiple_of(i * RET_CHUNK, RET_CHUNK), RET_CHUNK)


def _chunk_loop(nc, step, init, unroll=RET_UNROLL):
    group = math.gcd(nc, unroll)

    def trip(t, carry):
        for u in range(group):
            carry = step(t * group + u, carry)
        return carry

    return lax.fori_loop(0, nc // group, trip, init)


def _retention(a, b, c, lg_c, lg_a, *, strict_c, strict_a, scale, n_heads, name, gate=None, norm_w=None, jobs=()):
    s = a[0].shape[0]
    nc = s // RET_CHUNK
    epilogue = gate is not None

    def body(*refs):
        lgc_ref, lga_ref, a_ref, b_ref, c_ref = refs[:5]
        if epilogue:
            g_ref, w_ref, o_ref, mix_ref, sa_ref = refs[5:]
        else:
            o_ref, sa_ref = refs[5:]
        h = pl.program_id(0)
        dec = _ret_decays(lgc_ref[h], lga_ref[h], strict_c, strict_a)

        def reverse(t, state):
            i = nc - 1 - t
            sa_ref[i] = state.astype(BF16)
            rows = _chunk_rows(i)
            return state * dec["chunk_a"] + _dot(_scaled(b_ref[rows, :], dec["b_a"]), c_ref[rows, :], ta=True)

        _chunk_loop(nc, reverse, jnp.zeros((HEAD_DIM, HEAD_DIM), F32))

        def forward(i, state):
            rows = _chunk_rows(i)
            ai, bi, ci = a_ref[rows, :], b_ref[rows, :], c_ref[rows, :]
            inner = (_dot(ai, bi, tb=True) * dec["mask"]).astype(BF16)
            out = (_dot(inner, ci) + _dot(_scaled(ai, dec["a_c"]), state.astype(BF16))
                   + _dot(_scaled(ai, dec["a_a"]), sa_ref[i])) * scale
            o_ref[rows, :] = out.astype(BF16)
            if epilogue:
                r = lax.rsqrt(jnp.mean(out * out, axis=-1, keepdims=True) + EPS)
                g = g_ref[rows, :].astype(F32)
                mix_ref[rows, :] = (out * r * w_ref[...] * (g * _sigmoid(g))).astype(BF16)
            return state * dec["chunk_c"] + _dot(_scaled(bi, dec["b_c"]), ci, ta=True)

        _chunk_loop(nc, forward, jnp.zeros((HEAD_DIM, HEAD_DIM), F32))

    def col(first):
        return pl.BlockSpec((s, HEAD_DIM), lambda h: (0, first + h))

    smem = pl.BlockSpec(memory_space=pltpu.SMEM)
    in_specs = [smem, smem, col(a[1]), col(b[1]), col(c[1])]
    operands = [lg_c, lg_a, a[0], b[0], c[0]]
    o_shape = jax.ShapeDtypeStruct((s, n_heads * HEAD_DIM), BF16)
    out_specs, out_shape = [col(0)], [o_shape]
    if epilogue:
        in_specs += [col(gate[1]), pl.BlockSpec((1, HEAD_DIM), lambda h: (0, h))]
        operands += [gate[0], norm_w]
        out_specs, out_shape = [col(0)] * 2, [o_shape] * 2
    res, carried = _call(
        body, name=name, grid=(n_heads,), in_specs=in_specs, out_specs=out_specs, out_shape=out_shape,
        operands=operands, scratch_shapes=[pltpu.VMEM((nc, HEAD_DIM, HEAD_DIM), BF16)],
        semantics=("parallel",), jobs=jobs)
    res = res if epilogue else res[0]
    return (res, carried) if jobs else res


def _retention_decay_grads(a, b, c, e, lg_c, lg_a, *, scale, n_heads):
    s = a[0].shape[0]
    nc = s // RET_CHUNK
    cf = float(RET_CHUNK)

    def body(lgc_ref, lga_ref, a_ref, b_ref, c_ref, e_ref, gc_ref, ga_ref, sa_ref, ta_ref):
        h = pl.program_id(0)
        lgc, lga = lgc_ref[h], lga_ref[h]
        dec = _ret_decays(lgc, lga, True, True)
        rel, idx = dec["rel"], dec["idx"]
        w_c = jnp.where(rel > 0, rel * jnp.exp(lgc * jnp.maximum(rel, 0.0)), 0.0)
        w_a = jnp.where(rel < 0, -rel * jnp.exp(lga * jnp.maximum(-rel, 0.0)), 0.0)
        zero = jnp.zeros((HEAD_DIM, HEAD_DIM), F32)

        def reverse(t, carry):
            st, dst = carry
            i = nc - 1 - t
            sa_ref[i] = st.astype(BF16)
            ta_ref[i] = dst.astype(BF16)
            rows = _chunk_rows(i)
            bi, ci = b_ref[rows, :], c_ref[rows, :]
            st_new = st * dec["chunk_a"] + _dot(_scaled(bi, dec["b_a"]), ci, ta=True)
            dst_new = (cf * st + dst) * dec["chunk_a"] + _dot(_scaled(bi, idx * dec["b_a"]), ci, ta=True)
            return st_new, dst_new

        _chunk_loop(nc, reverse, (zero, zero))

        def forward(i, carry):
            st, dst, acc_c, acc_a = carry
            rows = _chunk_rows(i)
            ai, bi, ci = a_ref[rows, :], b_ref[rows, :], c_ref[rows, :]
            ev = e_ref[rows, :].astype(F32)
            pg = _dot(ai, bi, tb=True) * _dot(e_ref[rows, :], ci, tb=True)
            a_c, a_a = _scaled(ai, dec["a_c"]), _scaled(ai, dec["a_a"])
            inter_c = _dot(a_c, st.astype(BF16)) * (idx + 1.0) + _dot(a_c, dst.astype(BF16))
            inter_a = _dot(a_a, sa_ref[i]) * (cf - idx) + _dot(a_a, ta_ref[i])
            acc_c = acc_c + jnp.sum(pg * w_c, axis=0, keepdims=True) + jnp.sum(inter_c * ev, axis=0, keepdims=True)
            acc_a = acc_a + jnp.sum(pg * w_a, axis=0, keepdims=True) + jnp.sum(inter_a * ev, axis=0, keepdims=True)
            st_new = st * dec["chunk_c"] + _dot(_scaled(bi, dec["b_c"]), ci, ta=True)
            dst_new = ((cf * st + dst) * dec["chunk_c"]
                       + _dot(_scaled(bi, (cf - 1.0 - idx) * dec["b_c"]), ci, ta=True))
            return st_new, dst_new, acc_c, acc_a

        row = jnp.zeros((1, HEAD_DIM), F32)
        _, _, acc_c, acc_a = _chunk_loop(nc, forward, (zero, zero, row, row))
        gc_ref[...] = jnp.broadcast_to(jnp.sum(acc_c, axis=-1, keepdims=True) * scale, gc_ref.shape)
        ga_ref[...] = jnp.broadcast_to(jnp.sum(acc_a, axis=-1, keepdims=True) * scale, ga_ref.shape)

    def col(first):
        return pl.BlockSpec((s, HEAD_DIM), lambda h: (0, first + h))

    smem = pl.BlockSpec(memory_space=pltpu.SMEM)
    o_spec = pl.BlockSpec((1, 8, HEAD_DIM), lambda h: (h, 0, 0))
    o_shape = jax.ShapeDtypeStruct((n_heads, 8, HEAD_DIM), F32)
    gc, ga = pl.pallas_call(
        body, name="retention_decay_grads", grid=(n_heads,),
        in_specs=[smem, smem, col(a[1]), col(b[1]), col(c[1]), col(e[1])],
        out_specs=[o_spec] * 2, out_shape=[o_shape] * 2,
        scratch_shapes=[pltpu.VMEM((nc, HEAD_DIM, HEAD_DIM), BF16)] * 2,
        compiler_params=_params(("parallel",)),
    )(lg_c, lg_a, a[0], b[0], c[0], e[0])
    return gc[:, 0, 0], ga[:, 0, 0]


def _ret_gate_bwd(dmixed, first_col, out, proj, gate_col, norm_w, n_heads):
    s = out.shape[0]
    tr = _row_block(s, 8 * HEAD_DIM)

    def body(dm_ref, o_ref, g_ref, w_ref, do_ref, dg_ref, dw_ref):
        dm = dm_ref[...].astype(F32)
        ov = o_ref[...].astype(F32)
        g = g_ref[...].astype(F32)
        w = w_ref[...]
        r = lax.rsqrt(jnp.mean(ov * ov, axis=-1, keepdims=True) + EPS)
        ohat = ov * r
        sg = _sigmoid(g)
        silu = g * sg
        dg_ref[...] = (dm * ohat * w * sg * (1.0 + g * (1.0 - sg))).astype(BF16)
        dohat = dm * w * silu
        do_ref[...] = (r * (dohat - ohat * jnp.mean(dohat * ohat, axis=-1, keepdims=True))).astype(BF16)

        @pl.when(pl.program_id(1) == 0)
        def _():
            dw_ref[...] = jnp.zeros_like(dw_ref)

        dw_ref[...] += jnp.sum(dm * ohat * silu, axis=0, keepdims=True)

    def blk(first):
        return pl.BlockSpec((tr, HEAD_DIM), lambda h, i: (i, first + h))

    vec = pl.BlockSpec((1, HEAD_DIM), lambda h, i: (0, h))
    o_shape = jax.ShapeDtypeStruct((s, n_heads * HEAD_DIM), BF16)
    return pl.pallas_call(
        body, name="ret_gate_bwd", grid=(n_heads, s // tr),
        in_specs=[blk(first_col), blk(0), blk(gate_col), vec],
        out_specs=[blk(0), blk(0), vec],
        out_shape=[o_shape, o_shape, jax.ShapeDtypeStruct((1, n_heads * HEAD_DIM), F32)],
        compiler_params=_params(("parallel", "arbitrary")),
    )(dmixed, out, proj, norm_w)


def _step(x, target, norm_mix_w, ret_decay_fwd, ret_decay_bwd, ret_norm_w, norm_ffn_w, norm_final_w, own,
          w_in_started, pos):
    d = x.shape[1]
    nh = d // (2 * HEAD_DIM)
    scale = HEAD_DIM ** -0.5
    slopes = jnp.exp2(-8.0 * jnp.arange(1, nh + 1, dtype=F32) / nh)
    lg_f = -jnp.exp(ret_decay_fwd)
    lg_b = -jnp.exp(ret_decay_bwd)
    q_r, k_r, v_r, g_r = 3 * nh, 4 * nh, 5 * nh, 6 * nh
    ax = BIG_AXIS

    def gather(names, arrays, stage, part=None):
        return _gather_job(arrays, [ax[k] for k in names], stage, part)

    def add_halves(k, g, received):
        return _add_halves(g, received, ax[k], pos, name="grad_add_halves_" + k)

    def sum_parts(k, g, received, parts):
        return _sum_chip_parts(g, received, parts, ax[k], pos, name="grad_sum_parts_" + k)

    sems, w_in, token = w_in_started
    n1 = _rmsnorm_fwd(x, norm_mix_w, name="norm_mix_fwd", after=token)
    w_in = _split_gather_wait(sems, w_in, ax["w_in"], [n1] + [own[k] for k in BIG if k != "w_in"])
    (w_in,) = _run_jobs([gather(["w_in"], [w_in], "d2d")], name="all_gather_w_in_sibling")
    proj, [[w_gate]] = _matmul(n1, w_in, name="in_proj", out_dtype=BF16, tm=2048,
                               jobs=[gather(["w_gate"], [own["w_gate"]], "ici")])
    qkv_classes = _to_classes(proj[:, :3 * nh * HEAD_DIM])
    (ret, ret_mixed), [[w_gate], [w_out]] = _retention(
        (proj, q_r), (proj, k_r), (proj, v_r), lg_f, lg_b, strict_c=False, strict_a=True, scale=scale, n_heads=nh,
        name="retention_fwd", gate=(proj, g_r), norm_w=ret_norm_w,
        jobs=[gather(["w_gate"], [w_gate], "d2d"), gather(["w_out"], [own["w_out"]], "ici")])
    (far_out, far_lse), [[w_up]] = _attention_far_fwd(
        qkv_classes, slopes, nh, jobs=[gather(["w_up"], [own["w_up"]], "ici", (0, 1, 4))])
    (attn, lse), [[w_out], [w_up]] = _attention_fwd(
        proj, slopes, _from_classes(far_out), _from_classes(far_lse), nh,
        jobs=[gather(["w_out"], [w_out], "d2d"),
              _fuse(gather(["w_up"], [w_up], "d2d", (0, 1, 4)), gather(["w_up"], [w_up], "ici", (1, 2, 4)))])
    mixed = jnp.concatenate([attn, ret_mixed], axis=1)
    h1, [[w_up]] = _matmul(
        mixed, w_out, name="out_proj", residual=x,
        jobs=[_fuse(gather(["w_up"], [w_up], "d2d", (1, 2, 4)), gather(["w_up"], [w_up], "ici", (3, 1, 4)))])
    (w_up,) = _run_jobs([gather(["w_up"], [w_up], "d2d", (3, 1, 4))], name="all_gather_w_up_sibling")
    n2 = _rmsnorm_fwd(h1, norm_ffn_w, name="norm_ffn_fwd")
    (gate, up, act), [[w_down]] = _swiglu_fwd(n2, w_gate, w_up, jobs=[gather(["w_down"], [own["w_down"]], "ici")])
    (w_down,) = _run_jobs([gather(["w_down"], [w_down], "d2d")], name="all_gather_w_down_sibling")
    h2 = _matmul(act, w_down, name="down_proj", residual=h1, tk=2816)
    dh2, dh2_b, d_norm_final, loss = _loss_head(h2, norm_final_w, target)

    dgate, dup = _swiglu_bwd_act(dh2_b, w_down, gate, up)
    g_down = _weight_grad(act, dh2_b, name="grad_w_down")
    g_gate, [[r_down]] = _weight_grad(n2, dgate, name="grad_w_gate", jobs=[_exchange_job([g_down], [ax["w_down"]])])
    s_down = add_halves("w_down", g_down, r_down)
    g_up, [[r_gate], [p_down]] = _weight_grad(
        n2, dup, name="grad_w_up",
        jobs=[_exchange_job([g_gate], [ax["w_gate"]]), _send_sums_job([s_down], [ax["w_down"]], (0, 1, 2))])
    s_gate = add_halves("w_gate", g_gate, r_gate)
    dn2, [[r_up], [p_gate], [p_down]] = _swiglu_bwd_in(
        dgate, dup, w_gate, w_up,
        jobs=[_exchange_job([g_up], [ax["w_up"]]), _send_sums_job([s_gate], [ax["w_gate"]]),
              _send_sums_job([s_down], [ax["w_down"]], (1, 1, 2), landing=[p_down])])
    h_down = sum_parts("w_down", g_down, r_down, p_down)
    s_up = add_halves("w_up", g_up, r_up)
    h_gate = sum_parts("w_gate", g_gate, r_gate, p_gate)
    dh1, dh1_b, d_norm_ffn = _rmsnorm_bwd(dn2, h1, norm_ffn_w, dh2, name="norm_ffn_bwd")

    dmixed, [[gr_down], [p_up]] = _matmul(
        dh1_b, w_out, name="out_proj_bwd", tb=True, out_dtype=BF16,
        jobs=[_join_job([h_down], [ax["w_down"]]), _send_sums_job([s_up], [ax["w_up"]], (0, 1, 4))])
    far_in = [_to_classes(t) for t in (attn, dmixed[:, :nh * HEAD_DIM], lse)]
    g_out, [[p_up]] = _weight_grad(mixed, dh1_b, name="grad_w_out",
                                   jobs=[_send_sums_job([s_up], [ax["w_up"]], (1, 1, 4), landing=[p_up])])
    d_ret, dg_r, d_ret_norm = _ret_gate_bwd(dmixed, nh, ret, proj, g_r, ret_norm_w, nh)
    far_grads = _attention_far_bwd(qkv_classes, slopes, *far_in, nh)
    far_grads = [_from_classes(t) for t in far_grads]
    dq_r, [[gr_gate], [p_up]] = _retention(
        (d_ret, 0), (proj, v_r), (proj, k_r), lg_f, lg_b, strict_c=False, strict_a=True, scale=scale, n_heads=nh,
        name="retention_dq",
        jobs=[_join_job([h_gate], [ax["w_gate"]]), _send_sums_job([s_up], [ax["w_up"]], (2, 1, 4), landing=[p_up])])
    (dq_a, dk_a, dv_a), [[p_up], [r_out]] = _attention_bwd(
        proj, slopes, attn, lse, dmixed, far_grads, nh,
        jobs=[_send_sums_job([s_up], [ax["w_up"]], (3, 1, 4), landing=[p_up]),
              _exchange_job([g_out], [ax["w_out"]])])
    s_out = add_halves("w_out", g_out, r_out)
    h_up = sum_parts("w_up", g_up, r_up, p_up)
    dv_r, [[p_out], [gr_up]] = _retention(
        (proj, k_r), (proj, q_r), (d_ret, 0), lg_b, lg_f, strict_c=True, strict_a=False, scale=scale, n_heads=nh,
        name="retention_dv", jobs=[_send_sums_job([s_out], [ax["w_out"]]), _join_job([h_up], [ax["w_up"]])])
    h_out = sum_parts("w_out", g_out, r_out, p_out)
    dk_r, [[gr_out]] = _retention(
        (proj, v_r), (d_ret, 0), (proj, q_r), lg_b, lg_f, strict_c=True, strict_a=False, scale=scale, n_heads=nh,
        name="retention_dk", jobs=[_join_job([h_out], [ax["w_out"]])])
    dlg_f, dlg_b = _retention_decay_grads((proj, q_r), (proj, k_r), (proj, v_r), (d_ret, 0), lg_f, lg_b,
                                          scale=scale, n_heads=nh)
    dproj = [dq_a, dk_a, dv_a, dq_r, dk_r, dv_r, dg_r]
    g_in = _weight_grad_pieces(n1, dproj, name="grad_w_in")
    exchange = _split_start(_exchange_job([g_in], [ax["w_in"]]), name="grad_exchange_w_in_start")
    dn1 = _matmul_pieces_nt(dproj, w_in, name="in_proj_bwd", after=[exchange["token"]])
    g_in, r_in = _split_wait(exchange, [dn1], name="grad_exchange_w_in_wait")
    s_in = add_halves("w_in", g_in, r_in)
    sending = _split_start(_send_sums_job([s_in], [ax["w_in"]]), name="grad_send_w_in_start")
    dx, _, d_norm_mix = _rmsnorm_bwd(dn1, x, norm_mix_w, dh1, name="norm_mix_bwd", after=[sending["token"]])

    small = dict(loss=loss[0, 0], norm_mix_w=d_norm_mix, ret_decay_fwd=dlg_f * lg_f, ret_decay_bwd=dlg_b * lg_b,
                 ret_norm_w=d_ret_norm, norm_ffn_w=d_norm_ffn, norm_final_w=d_norm_final)
    return (dx, dict(w_out=gr_out, w_gate=gr_gate, w_up=gr_up, w_down=gr_down), small,
            dict(sending=sending, grad=g_in, received=r_in))


def _mesh_position():
    x, y, c = lax.axis_index("x"), lax.axis_index("y"), lax.axis_index("c")
    chips = [(1 - x, y), (x, 1 - y), (1 - x, 1 - y)]
    return x, y, c, chips


def _span(span):
    if span is None:
        return slice(None)
    start, size, step = span
    return pl.ds(start if isinstance(start, int) else pl.multiple_of(start, step), size)


def _part_rows(part, rows):
    first, count, of = part
    return first * (rows // of), count * (rows // of), rows // of


def _region(ref, axis, shard, half, shard_size, half_size, part=None, total_rows=None):
    along = None if shard is None else (shard * shard_size, shard_size, shard_size)
    other = None if half is None else (half * half_size, half_size, half_size)
    rows, cols = (other, along) if axis == 1 else (along, other)
    if part is not None:
        start, size, _ = rows if rows is not None else (0, total_rows, None)
        offset, size, step = _part_rows(part, size)
        rows = (start + offset, size, step)
    return ref.at[_span(rows), _span(cols)]


def _fuse(first, second):
    assert not (first.ins or first.outs or second.ins or second.outs)
    assert len(first.ios) == len(second.ios) and all(a is b for a, b in zip(first.ios, second.ios))
    cut = len(first.sems)

    def start(refs, sems):
        first.start(refs, sems[:cut])
        second.start(refs, sems[cut:])

    def finish(refs, sems):
        first.finish(refs, sems[:cut])
        second.finish(refs, sems[cut:])

    return _Job(ios=first.ios, sems=first.sems + second.sems, start=start, finish=finish)


def _gather_job(full, axes, stage, part=None):
    n = len(full)

    def copies(refs, sems):
        send_sem, recv_sem = sems
        x, y, c, chips = _mesh_position()
        me = 2 * x + y

        def copy(w, k, shard, half, target):
            rows_cols = full[w].shape
            place = _region(refs[w], axes[w], shard, half, rows_cols[axes[w]] // N_CHIPS, rows_cols[1 - axes[w]] // 2,
                            part)
            return pltpu.make_async_remote_copy(
                src_ref=place, dst_ref=place, send_sem=send_sem.at[w, k], recv_sem=recv_sem.at[w, k],
                device_id=target, device_id_type=MESH)

        def sent(w, k):
            if stage == "ici":
                return copy(w, k, me, c, (chips[k][0], chips[k][1], c))
            return copy(w, k, 2 * chips[k][0] + chips[k][1], c, (x, y, 1 - c))

        def landed(w, k):
            return copy(w, k, 2 * chips[k][0] + chips[k][1], c if stage == "ici" else 1 - c, (x, y, 1 - c))

        return sent, landed

    def start(refs, sems):
        sent, _ = copies(refs, sems)
        for w in range(n):
            for k in range(3):
                sent(w, k).start()

    def finish(refs, sems):
        sent, landed = copies(refs, sems)
        for w in range(n):
            for k in range(3):
                landed(w, k).wait_recv()
                sent(w, k).wait_send()

    return _Job(ios=full, sems=[pltpu.SemaphoreType.DMA((n, 3))] * 2, start=start, finish=finish)


def _exchange_job(grads, axes):
    n = len(grads)

    def half_shape(w):
        return tuple(d // 2 if a != axes[w] else d for a, d in enumerate(grads[w].shape))

    def copy(refs, sems, w):
        x, y, c, _ = _mesh_position()
        return pltpu.make_async_remote_copy(
            src_ref=_region(refs[w], axes[w], None, 1 - c, 0, half_shape(w)[1 - axes[w]]), dst_ref=refs[n + w],
            send_sem=sems[0].at[w], recv_sem=sems[1].at[w], device_id=(x, y, 1 - c), device_id_type=MESH)

    def start(refs, sems):
        for w in range(n):
            copy(refs, sems, w).start()

    def finish(refs, sems):
        for w in range(n):
            copy(refs, sems, w).wait()

    return _Job(ins=grads, outs=[jax.ShapeDtypeStruct(half_shape(w), F32) for w in range(n)],
                sems=[pltpu.SemaphoreType.DMA((n,))] * 2, start=start, finish=finish)


def _half_block_spec(axis, block, half_blocks, use_half):
    if axis == 1:
        if use_half:
            return pl.BlockSpec(block, lambda i, pos: (pos[0] * half_blocks + i, 0))
        return pl.BlockSpec(block, lambda i, pos: (i, 0))
    if use_half:
        return pl.BlockSpec(block, lambda i, pos: (i, pos[0]))
    return pl.BlockSpec(block, lambda i, pos: (i, 0))


def _add_halves(grad, received, axis, pos, *, name):
    rows, cols = received.shape
    tr = _row_block(rows, cols)
    nb = rows // tr

    def body(pos_ref, g_ref, r_ref, o_ref):
        o_ref[...] = (g_ref[...] + r_ref[...]).astype(BF16)

    blk = (tr, cols)
    return pl.pallas_call(
        body, name=name, out_shape=jax.ShapeDtypeStruct((rows, cols), BF16),
        grid_spec=pltpu.PrefetchScalarGridSpec(
            num_scalar_prefetch=1, grid=(nb,),
            in_specs=[_half_block_spec(axis, blk, nb, True), _half_block_spec(axis, blk, nb, False)],
            out_specs=_half_block_spec(axis, blk, nb, False)),
        compiler_params=_params(("parallel",)),
    )(pos, grad, received)


def _send_sums_job(sums, axes, part=None, landing=None):
    n = len(sums)

    def part_shape(w):
        return tuple(d // N_CHIPS if a == axes[w] else d for a, d in enumerate(sums[w].shape))

    def copy(refs, sems, w, k):
        x, y, c, chips = _mesh_position()
        shard = 2 * chips[k][0] + chips[k][1]
        rows = part_shape(w)[0]
        dst = refs[n + w].at[k]
        if part is not None:
            offset, size, _ = _part_rows(part, rows)
            dst = refs[n + w].at[k, pl.ds(offset, size), :]
        return pltpu.make_async_remote_copy(
            src_ref=_region(refs[w], axes[w], shard, None, part_shape(w)[axes[w]], 0, part, rows), dst_ref=dst,
            send_sem=sems[0].at[w, k], recv_sem=sems[1].at[w, k],
            device_id=(chips[k][0], chips[k][1], c), device_id_type=MESH)

    def start(refs, sems):
        for w in range(n):
            for k in range(3):
                copy(refs, sems, w, k).start()

    def finish(refs, sems):
        for w in range(n):
            for k in range(3):
                copy(refs, sems, w, k).wait()

    sems = [pltpu.SemaphoreType.DMA((n, 3))] * 2
    if landing is not None:
        return _Job(ins=sums, ios=landing, sems=sems, start=start, finish=finish)
    return _Job(ins=sums, outs=[jax.ShapeDtypeStruct((3,) + part_shape(w), BF16) for w in range(n)],
                sems=sems, start=start, finish=finish)


def _sum_chip_parts(grad, received, parts, axis, pos, *, name):
    _, rows, cols = parts.shape
    tr = _row_block(rows, cols)
    nb = rows // tr
    blk = (tr, cols)

    def body(pos_ref, g_ref, r_ref, p_ref, o_ref):
        total = g_ref[...] + r_ref[...]
        for k in range(3):
            total = total + p_ref[k].astype(F32)
        o_ref[...] = total

    if axis == 1:
        g_spec = pl.BlockSpec(blk, lambda i, pos: (pos[0] * nb + i, pos[1]))
        r_spec = pl.BlockSpec(blk, lambda i, pos: (i, pos[1]))
        o_spec = pl.BlockSpec(blk, lambda i, pos: (pos[0] * nb + i, 0))
        shard_shape = (2 * rows, cols)
    else:
        g_spec = pl.BlockSpec(blk, lambda i, pos: (pos[1] * nb + i, pos[0]))
        r_spec = pl.BlockSpec(blk, lambda i, pos: (pos[1] * nb + i, 0))
        o_spec = pl.BlockSpec(blk, lambda i, pos: (i, pos[0]))
        shard_shape = (rows, 2 * cols)
    return pl.pallas_call(
        body, name=name, out_shape=jax.ShapeDtypeStruct(shard_shape, F32),
        grid_spec=pltpu.PrefetchScalarGridSpec(
            num_scalar_prefetch=1, grid=(nb,),
            in_specs=[g_spec, r_spec, pl.BlockSpec((3,) + blk, lambda i, pos: (0, i, 0))],
            out_specs=o_spec),
        compiler_params=_params(("parallel",)),
    )(pos, grad, received, parts)


def _join_job(shards, axes):
    n = len(shards)

    def copy(refs, sems, w, other):
        x, y, c, _ = _mesh_position()
        place = _region(refs[w], axes[w], None, 1 - c if other else c, 0, shards[w].shape[1 - axes[w]] // 2)
        return pltpu.make_async_remote_copy(
            src_ref=place, dst_ref=place, send_sem=sems[0].at[w], recv_sem=sems[1].at[w],
            device_id=(x, y, 1 - c), device_id_type=MESH)

    def start(refs, sems):
        for w in range(n):
            copy(refs, sems, w, False).start()

    def finish(refs, sems):
        for w in range(n):
            copy(refs, sems, w, True).wait_recv()
            copy(refs, sems, w, False).wait_send()

    return _Job(ios=shards, sems=[pltpu.SemaphoreType.DMA((n,))] * 2, start=start, finish=finish)


def _all_reduce_small(vec):
    rows, cols = vec.shape

    def body(v_ref, o_ref, land_ref, send_sem, recv_sem):
        x, y, c, _ = _mesh_position()
        me = 4 * x + 2 * y + c
        land_ref[me] = v_ref[...]
        copies = []
        for k in range(1, 8):
            px, py, pc = x ^ (k >> 2), y ^ ((k >> 1) & 1), c ^ (k & 1)
            copies.append(pltpu.make_async_remote_copy(
                src_ref=v_ref, dst_ref=land_ref.at[me], send_sem=send_sem.at[k], recv_sem=recv_sem.at[k],
                device_id=(px, py, pc), device_id_type=MESH))
        for cp in copies:
            cp.start()
        for k in range(1, 8):
            peer = me ^ k
            pltpu.make_async_remote_copy(
                src_ref=v_ref, dst_ref=land_ref.at[peer], send_sem=send_sem.at[k], recv_sem=recv_sem.at[k],
                device_id=(x, y, c), device_id_type=MESH).wait_recv()
        for cp in copies:
            cp.wait_send()
        total = land_ref[0]
        for k in range(1, 8):
            total = total + land_ref[k]
        o_ref[...] = total

    vmem = pl.BlockSpec(memory_space=pltpu.VMEM)
    return pl.pallas_call(
        body, name="all_reduce_small", in_specs=[vmem], out_specs=vmem,
        out_shape=jax.ShapeDtypeStruct((rows, cols), F32),
        scratch_shapes=[pltpu.VMEM((8, rows, cols), F32), pltpu.SemaphoreType.DMA((8,)), pltpu.SemaphoreType.DMA((8,))],
    )(vec)


def _adamw(w, g, m, v, *, name, after=()):
    rows, cols = w.shape
    tr = _row_block(rows, cols) if rows % 8 == 0 else rows
    bc1 = 1.0 - ADAM_B1 ** ADAM_STEP
    bc2 = 1.0 - ADAM_B2 ** ADAM_STEP

    def body(w_ref, g_ref, m_ref, v_ref, *rest):
        go_ref, d_ref, mo_ref, vo_ref = rest[len(after):]
        gv = g_ref[...]
        go_ref[...] = gv
        mn = ADAM_B1 * m_ref[...] + (1.0 - ADAM_B1) * gv
        vn = ADAM_B2 * v_ref[...] + (1.0 - ADAM_B2) * (gv * gv)
        mo_ref[...] = mn
        vo_ref[...] = vn
        d_ref[...] = -ADAM_LR * ((mn / bc1) / (jnp.sqrt(vn / bc2) + ADAM_EPS) + ADAM_WD * w_ref[...])

    blk = pl.BlockSpec((tr, cols), lambda i: (i, 0))
    shape = jax.ShapeDtypeStruct((rows, cols), F32)
    return pl.pallas_call(
        body, name=name, grid=(rows // tr,), in_specs=[blk] * 4 + [pl.BlockSpec(memory_space=pl.ANY)] * len(after),
        out_specs=[blk] * 4, out_shape=[shape] * 4, compiler_params=_params(("parallel",)),
    )(w, g, m, v, *after)


def _to_bf16_in_place(w, axis, pos, *, name, after=None):
    rows, cols = w.shape
    tr = _row_block(rows, cols)
    nb = rows // tr

    def body(pos_ref, w_ref, *rest):
        rest[-1][...] = w_ref[...].astype(BF16)

    if axis == 1:
        o_spec = pl.BlockSpec((tr, cols), lambda i, pos: (i, pos[1]))
        full_shape = (rows, N_CHIPS * cols)
    else:
        o_spec = pl.BlockSpec((tr, cols), lambda i, pos: (pos[1] * nb + i, 0))
        full_shape = (N_CHIPS * rows, cols)
    in_specs = [pl.BlockSpec((tr, cols), lambda i, pos: (i, 0))]
    operands = [pos, w]
    if after is not None:
        in_specs.append(pl.BlockSpec(after.shape, lambda i, pos: (0, 0)))
        operands.append(after)
    return pl.pallas_call(
        body, name=name, out_shape=jax.ShapeDtypeStruct(full_shape, BF16),
        grid_spec=pltpu.PrefetchScalarGridSpec(num_scalar_prefetch=1, grid=(nb,), in_specs=in_specs, out_specs=o_spec),
        compiler_params=_params(("parallel",)),
    )(*operands)


def _split_gather_start(full, axis):
    rows_cols = full.shape

    def body(buf_ref, *rest):
        sems = rest[:6]
        token_ref = rest[7]
        x, y, c, chips = _mesh_position()
        place = _region(buf_ref, axis, 2 * x + y, c, rows_cols[axis] // N_CHIPS, rows_cols[1 - axis] // 2)
        for k in range(3):
            pltpu.make_async_remote_copy(
                src_ref=place, dst_ref=place, send_sem=sems[k], recv_sem=sems[3 + k],
                device_id=(chips[k][0], chips[k][1], c), device_id_type=MESH).start()
        token_ref[...] = jnp.zeros_like(token_ref)

    hbm = pl.BlockSpec(memory_space=pltpu.HBM)
    sem = pl.BlockSpec(memory_space=pltpu.SEMAPHORE)
    res = pl.pallas_call(
        body, name="all_gather_w_in_start",
        out_shape=(*[pltpu.SemaphoreType.DMA(())] * 6, pltpu.HBM(full.shape, full.dtype),
                   jax.ShapeDtypeStruct((8, HEAD_DIM), F32)),
        in_specs=(hbm,), out_specs=(*[sem] * 6, hbm, pl.BlockSpec(memory_space=pltpu.VMEM)),
        input_output_aliases={0: 6},
        compiler_params=pltpu.CompilerParams(has_side_effects=pltpu.SideEffectType.DATAFLOW_SIDE_EFFECTING),
    )(pltpu.with_memory_space_constraint(full, pltpu.HBM))
    return list(res[:6]), res[6], res[7]


def _split_gather_wait(sems, full, axis, after):
    rows_cols = full.shape

    def body(buf_ref, *rest):
        sem_refs = rest[:6]
        x, y, c, chips = _mesh_position()

        def copy(k, shard):
            place = _region(buf_ref, axis, shard, c, rows_cols[axis] // N_CHIPS, rows_cols[1 - axis] // 2)
            return pltpu.make_async_remote_copy(
                src_ref=place, dst_ref=place, send_sem=sem_refs[k], recv_sem=sem_refs[3 + k],
                device_id=(chips[k][0], chips[k][1], c), device_id_type=MESH)

        for k in range(3):
            copy(k, 2 * x + y).wait_send()
            copy(k, 2 * chips[k][0] + chips[k][1]).wait_recv()

    hbm = pl.BlockSpec(memory_space=pltpu.HBM)
    sem = pl.BlockSpec(memory_space=pltpu.SEMAPHORE)
    return pl.pallas_call(
        body, name="all_gather_w_in_wait", out_shape=pltpu.HBM(full.shape, full.dtype),
        in_specs=(hbm, *[sem] * 6, *[pl.BlockSpec(memory_space=pl.ANY)] * len(after)), out_specs=hbm,
        input_output_aliases={0: 0},
        compiler_params=pltpu.CompilerParams(has_side_effects=pltpu.SideEffectType.DATAFLOW_SIDE_EFFECTING),
    )(full, *sems, *after)


BIG = ("w_in", "w_out", "w_gate", "w_up", "w_down")
BIG_AXIS = dict(w_in=1, w_out=0, w_gate=1, w_up=1, w_down=0)
SMALL = ("norm_mix_w", "ret_decay_fwd", "ret_decay_bwd", "ret_norm_w", "norm_ffn_w", "norm_final_w")
ALL_WEIGHTS = ("norm_mix_w", "w_in", "ret_decay_fwd", "ret_decay_bwd", "ret_norm_w", "w_out", "norm_ffn_w",
               "w_gate", "w_up", "w_down", "norm_final_w")
SMALL_ROW = 128 * 8


def _pack_small(small):
    pieces = [jnp.reshape(small["loss"], (1,))] + [jnp.reshape(small[k], (-1,)) for k in SMALL]
    rows = []
    for p in pieces:
        pad = -p.shape[0] % (8 * SMALL_ROW)
        rows.append(jnp.reshape(jnp.pad(p, (0, pad)), (-1, SMALL_ROW)))
    return jnp.concatenate(rows, axis=0)


def _unpack_small(block, like):
    out, row = {}, 0
    for k in ("loss",) + SMALL:
        size = 1 if k == "loss" else like[k].size
        nrows = -(-size // (8 * SMALL_ROW)) * 8
        out[k] = jnp.reshape(block[row:row + nrows], (-1,))[:size]
        row += nrows
    return out


def kernel(x, norm_mix_w, w_in, ret_decay_fwd, ret_decay_bwd, ret_norm_w, w_out, norm_ffn_w, w_gate, w_up, w_down, norm_final_w, loss_target, m_norm_mix_w, m_w_in, m_ret_decay_fwd, m_ret_decay_bwd, m_ret_norm_w, m_w_out, m_norm_ffn_w, m_w_gate, m_w_up, m_w_down, m_norm_final_w, v_norm_mix_w, v_w_in, v_ret_decay_fwd, v_ret_decay_bwd, v_ret_norm_w, v_w_out, v_norm_ffn_w, v_w_gate, v_w_up, v_w_down, v_norm_final_w):
    weights = dict(norm_mix_w=norm_mix_w, w_in=w_in, ret_decay_fwd=ret_decay_fwd, ret_decay_bwd=ret_decay_bwd,
                   ret_norm_w=ret_norm_w, w_out=w_out, norm_ffn_w=norm_ffn_w, w_gate=w_gate, w_up=w_up,
                   w_down=w_down, norm_final_w=norm_final_w)
    m_in = dict(norm_mix_w=m_norm_mix_w, w_in=m_w_in, ret_decay_fwd=m_ret_decay_fwd, ret_decay_bwd=m_ret_decay_bwd,
                ret_norm_w=m_ret_norm_w, w_out=m_w_out, norm_ffn_w=m_norm_ffn_w, w_gate=m_w_gate, w_up=m_w_up,
                w_down=m_w_down, norm_final_w=m_norm_final_w)
    v_in = dict(norm_mix_w=v_norm_mix_w, w_in=v_w_in, ret_decay_fwd=v_ret_decay_fwd, ret_decay_bwd=v_ret_decay_bwd,
                ret_norm_w=v_ret_norm_w, w_out=v_w_out, norm_ffn_w=v_norm_ffn_w, w_gate=v_w_gate, w_up=v_w_up,
                w_down=v_w_down, norm_final_w=v_norm_final_w)
    pos = jnp.stack([lax.axis_index("c"), 2 * lax.axis_index("x") + lax.axis_index("y")]).astype(jnp.int32)

    own = {"w_in": _to_bf16_in_place(weights["w_in"][0], BIG_AXIS["w_in"], pos, name="cast_w_in")}
    w_in_started = _split_gather_start(own["w_in"], BIG_AXIS["w_in"])
    for k in BIG[1:]:
        own[k] = _to_bf16_in_place(weights[k][0], BIG_AXIS[k], pos, name="cast_" + k, after=w_in_started[2])

    dx, grad_w, small, w_in_pending = _step(
        x[0], loss_target[0], norm_mix_w, ret_decay_fwd[0], ret_decay_bwd[0], ret_norm_w, norm_ffn_w,
        norm_final_w[None, :], own, w_in_started, pos)

    like = {k: weights[k] for k in SMALL}
    reduced = _unpack_small(_all_reduce_small(_pack_small(small)), like)
    loss = reduced["loss"][0]
    for k in SMALL:
        grad_w[k] = jnp.reshape(reduced[k], (1, -1))

    delta, new_m, new_v = {}, {}, {}

    def update(k, after):
        shape = weights[k].shape
        as2d = (lambda t: jnp.reshape(t, (-1, shape[-1])))
        grad_w[k], delta[k], new_m[k], new_v[k] = (jnp.reshape(t, shape) for t in _adamw(
            as2d(weights[k]), as2d(grad_w[k]), as2d(m_in[k]), as2d(v_in[k]), name="adamw_" + k, after=after))

    others = [k for k in ALL_WEIGHTS if k != "w_in"]
    for k in others:
        update(k, [w_in_pending["sending"]["token"]])
    _, parts = _split_wait(w_in_pending["sending"], [delta[k] for k in others], name="grad_send_w_in_wait")
    half = _sum_chip_parts(w_in_pending["grad"], w_in_pending["received"], parts, BIG_AXIS["w_in"], pos,
                           name="grad_sum_parts_w_in")
    (grad_w["w_in"],) = _run_jobs([_join_job([half], [BIG_AXIS["w_in"]])], name="grad_join_w_in")
    update("w_in", [])

    return (loss, dx[None], *[grad_w[k] for k in ALL_WEIGHTS], *[delta[k] for k in ALL_WEIGHTS],
            *[new_m[k] for k in ALL_WEIGHTS], *[new_v[k] for k in ALL_WEIGHTS])
```

```python
import functools
import math

import numpy as np
import jax
import jax.numpy as jnp
from jax import lax
from jax.experimental import pallas as pl
from jax.experimental.pallas import tpu as pltpu

F32 = jnp.float32
BF16 = jnp.bfloat16
MESH = pl.DeviceIdType.MESH

HEAD_DIM = 128
RET_CHUNK = 128
RET_UNROLL = 8
EPS = 1e-6
DILATED_PATTERNS = ((128, 1), (512, 4), (2048, 16))
ATT_BLOCK = 256
ATT_REACH = max(w // 2 for w, _ in DILATED_PATTERNS)
ATT_NEAR = ATT_BLOCK
ATT_CLASSES = DILATED_PATTERNS[-1][1]
assert all(w // 2 <= ATT_NEAR for w, _ in DILATED_PATTERNS[:-1])
ATT_KB = -(-ATT_NEAR // ATT_BLOCK)
ATT_WINDOW = 2 * ATT_KB + 1
ATT_FAR_GROUP = 8
ATT_NEAR_GROUP = 4
MASKED = -1e30
ROW_MAX_INIT = -1e29
N_CHIPS = 4
VMEM_LIMIT_BYTES = 56 * 1024 * 1024
ELEM_BLOCK_BYTES = 2 * 1024 * 1024

ADAM_LR = 0.001
ADAM_B1 = 0.9
ADAM_B2 = 0.999
ADAM_EPS = 1e-08
ADAM_WD = 0.01
ADAM_STEP = 10


def _params(sem=None):
    return pltpu.CompilerParams(dimension_semantics=sem, vmem_limit_bytes=VMEM_LIMIT_BYTES)


def _sigmoid(x):
    return 0.5 * jnp.tanh(0.5 * x) + 0.5


class _Job:
    def __init__(self, *, ins=(), ios=(), outs=(), sems=(), start, finish):
        self.ins, self.ios, self.outs, self.sems = list(ins), list(ios), list(outs), list(sems)
        self.start, self.finish = start, finish

    def results(self):
        return [jax.ShapeDtypeStruct(a.shape, a.dtype) for a in self.ios] + self.outs


def _call(body, *, name, grid, in_specs, out_specs, out_shape, operands, scratch_shapes=(), semantics=None, jobs=(),
          after=()):
    in_specs, out_specs, out_shape = list(in_specs), list(out_specs), list(out_shape)
    scratch_shapes = list(scratch_shapes)
    if not jobs:
        n_real = len(in_specs)

        def ordered(*refs):
            body(*refs[:n_real], *refs[n_real + len(after):])

        outs = pl.pallas_call(
            ordered if after else body, name=name, grid=grid,
            in_specs=in_specs + [pl.BlockSpec(memory_space=pl.ANY)] * len(after), out_specs=out_specs,
            out_shape=out_shape, scratch_shapes=scratch_shapes, compiler_params=_params(semantics))(*operands, *after)
        return outs, []
    n_in, n_out, n_scratch = len(in_specs), len(out_specs), len(scratch_shapes)
    extra_in, extra_out, sems, aliases = [], [], [], {}
    for job in jobs:
        extra_in += job.ins
        for t in range(len(job.ios)):
            aliases[n_in + len(extra_in) + t] = n_out + len(extra_out) + t
        extra_in += job.ios
        extra_out += job.results()
        sems += job.sems

    def carried(*refs):
        x_in = refs[n_in:n_in + len(extra_in)]
        x_out = refs[n_in + len(extra_in) + n_out:n_in + len(extra_in) + n_out + len(extra_out)]
        x_sem = refs[len(refs) - len(sems):]
        views, i_in, i_out, i_sem = [], 0, 0, 0
        for job in jobs:
            data = list(x_in[i_in:i_in + len(job.ins)]) + list(x_out[i_out:i_out + len(job.results())])
            views.append((data, x_sem[i_sem:i_sem + len(job.sems)]))
            i_in += len(job.ins) + len(job.ios)
            i_out += len(job.results())
            i_sem += len(job.sems)
        steps = [pl.program_id(d) for d in range(len(grid))]

        @pl.when(functools.reduce(jnp.logical_and, [s == 0 for s in steps]))
        def _():
            for job, (data, sem) in zip(jobs, views):
                job.start(data, sem)

        body(*refs[:n_in], *refs[n_in + len(extra_in):n_in + len(extra_in) + n_out],
             *refs[len(refs) - len(sems) - n_scratch:len(refs) - len(sems)])

        @pl.when(functools.reduce(jnp.logical_and, [s == g - 1 for s, g in zip(steps, grid)]))
        def _():
            for job, (data, sem) in zip(jobs, views):
                job.finish(data, sem)

    hbm = pl.BlockSpec(memory_space=pl.ANY)
    res = pl.pallas_call(
        carried, name=name, grid=grid, in_specs=in_specs + [hbm] * len(extra_in),
        out_specs=out_specs + [hbm] * len(extra_out), out_shape=out_shape + extra_out,
        input_output_aliases=aliases, scratch_shapes=scratch_shapes + sems,
        compiler_params=_params(("arbitrary",) * len(grid)),
    )(*operands, *extra_in)
    carried_results, at = [], n_out
    for job in jobs:
        carried_results.append(list(res[at:at + len(job.results())]))
        at += len(job.results())
    return list(res[:n_out]), carried_results


def _run_jobs(jobs, *, name):
    first = jobs[0]
    n_in, n_io = len(first.ins), len(first.ios)
    out_shape = first.results()
    n_sems = [len(job.sems) for job in jobs]

    def body(*refs):
        data = list(refs[:n_in]) + list(refs[n_in + n_io:n_in + n_io + len(out_shape)])
        at = n_in + n_io + len(out_shape)
        for job, ns in zip(jobs, n_sems):
            job.start(data, refs[at:at + ns])
            job.finish(data, refs[at:at + ns])
            at += ns

    hbm = pl.BlockSpec(memory_space=pl.ANY)
    return pl.pallas_call(
        body, name=name, in_specs=[hbm] * (n_in + n_io), out_specs=[hbm] * len(out_shape), out_shape=out_shape,
        input_output_aliases={n_in + t: t for t in range(n_io)},
        scratch_shapes=[s for job in jobs for s in job.sems],
    )(*first.ins, *first.ios)


class _SemaphoreGrid:
    def __init__(self, refs, shape):
        self.refs, self.shape = list(refs), tuple(shape)

    @property
    def at(self):
        return self

    def __getitem__(self, index):
        index = index if isinstance(index, tuple) else (index,)
        flat = 0
        for i, extent in zip(index, self.shape):
            flat = flat * extent + i
        return self.refs[flat]


def _semaphore_grids(job, refs):
    grids, at = [], 0
    for sem in job.sems:
        count = math.prod(sem.shape)
        grids.append(_SemaphoreGrid(refs[at:at + count], sem.shape))
        at += count
    return grids


def _split_start(job, *, name):
    arrays = job.ins + job.ios + [lax.empty(s.shape, s.dtype) for s in job.outs]
    n, ns = len(arrays), sum(math.prod(sem.shape) for sem in job.sems)

    def body(*refs):
        job.start(list(refs[:n]), _semaphore_grids(job, refs[n:n + ns]))
        refs[-1][...] = jnp.zeros_like(refs[-1])

    hbm = pl.BlockSpec(memory_space=pltpu.HBM)
    res = pl.pallas_call(
        body, name=name,
        out_shape=(*[pltpu.SemaphoreType.DMA(())] * ns, *[pltpu.HBM(a.shape, a.dtype) for a in arrays],
                   jax.ShapeDtypeStruct((8, HEAD_DIM), F32)),
        in_specs=[hbm] * n,
        out_specs=(*[pl.BlockSpec(memory_space=pltpu.SEMAPHORE)] * ns, *[hbm] * n,
                   pl.BlockSpec(memory_space=pltpu.VMEM)),
        input_output_aliases={t: ns + t for t in range(n)},
        compiler_params=pltpu.CompilerParams(has_side_effects=pltpu.SideEffectType.DATAFLOW_SIDE_EFFECTING),
    )(*[pltpu.with_memory_space_constraint(a, pltpu.HBM) for a in arrays])
    return dict(job=job, sems=list(res[:ns]), arrays=list(res[ns:ns + n]), token=res[-1])


def _split_wait(started, after, *, name):
    job, arrays, sems = started["job"], started["arrays"], started["sems"]
    n, ns = len(arrays), len(sems)

    def body(*refs):
        job.finish(list(refs[:n]), _semaphore_grids(job, refs[n:n + ns]))

    hbm = pl.BlockSpec(memory_space=pltpu.HBM)
    return pl.pallas_call(
        body, name=name, out_shape=[pltpu.HBM(a.shape, a.dtype) for a in arrays],
        in_specs=[hbm] * n + [pl.BlockSpec(memory_space=pltpu.SEMAPHORE)] * ns
        + [pl.BlockSpec(memory_space=pl.ANY)] * len(after),
        out_specs=[hbm] * n, input_output_aliases={t: t for t in range(n)},
        compiler_params=pltpu.CompilerParams(has_side_effects=pltpu.SideEffectType.DATAFLOW_SIDE_EFFECTING),
    )(*arrays, *sems, *after)


def _dot(a, b, ta=False, tb=False):
    return lax.dot_general(a, b, (((0 if ta else 1,), (1 if tb else 0,)), ((), ())),
                           preferred_element_type=F32)


def _tile(n, want):
    t = min(n, want) // 128 * 128
    while n % t:
        t -= 128
    return t


def _a_spec(ta, tm, tk):
    return pl.BlockSpec((tk, tm), lambda i, j, k: (k, i)) if ta else pl.BlockSpec((tm, tk), lambda i, j, k: (i, k))


def _b_spec(tb, tk, tn):
    return pl.BlockSpec((tn, tk), lambda i, j, k: (j, k)) if tb else pl.BlockSpec((tk, tn), lambda i, j, k: (k, j))


def _accumulate(accs, nk, products, finish):
    if nk == 1:
        finish(*products())
        return
    k = pl.program_id(2)

    @pl.when(k == 0)
    def _():
        for acc, p in zip(accs, products()):
            acc[...] = p

    if nk > 2:
        @pl.when(jnp.logical_and(k > 0, k < nk - 1))
        def _():
            for acc, p in zip(accs, products()):
                acc[...] += p

    @pl.when(k == nk - 1)
    def _():
        finish(*[acc[...] + p for acc, p in zip(accs, products())])


def _matmul(a, b, *, name, ta=False, tb=False, out_dtype=F32, residual=None, tm=1024, tn=1024, tk=2048, jobs=()):
    m, kdim = (a.shape[1], a.shape[0]) if ta else a.shape
    n = b.shape[0] if tb else b.shape[1]
    tm, tn, tk = _tile(m, tm), _tile(n, tn), _tile(kdim, tk)
    nk = kdim // tk

    def body(*refs):
        a_ref, b_ref = refs[:2]
        r_ref = refs[2] if residual is not None else None
        o_ref = refs[-1] if nk == 1 else refs[-2]

        def finish(total):
            if residual is not None:
                total = total + r_ref[...]
            o_ref[...] = total.astype(out_dtype)

        _accumulate(refs[-1:] if nk > 1 else (), nk, lambda: (_dot(a_ref[...], b_ref[...], ta, tb),), finish)

    o_spec = pl.BlockSpec((tm, tn), lambda i, j, k: (i, j))
    in_specs = [_a_spec(ta, tm, tk), _b_spec(tb, tk, tn)]
    operands = [a, b]
    if residual is not None:
        in_specs.append(o_spec)
        operands.append(residual)
    (out,), carried = _call(
        body, name=name, grid=(m // tm, n // tn, nk), in_specs=in_specs, out_specs=[o_spec],
        out_shape=[jax.ShapeDtypeStruct((m, n), out_dtype)], operands=operands,
        scratch_shapes=[pltpu.VMEM((tm, tn), F32)] * (nk > 1),
        semantics=("parallel", "parallel", "arbitrary"), jobs=jobs)
    return (out, carried) if jobs else out


def _matmul_pieces_nt(pieces, b, *, name, tm=512, tn=1024, jobs=(), after=()):
    m, kp = pieces[0].shape
    n = b.shape[0]
    tm, tn = _tile(m, tm), _tile(n, tn)
    count = len(pieces)

    def body(*refs):
        b_ref, o_ref = refs[count], refs[count + 1]
        total = _dot(refs[0][...], b_ref[:, pl.ds(0, kp)], tb=True)
        for p in range(1, count):
            total = total + _dot(refs[p][...], b_ref[:, pl.ds(p * kp, kp)], tb=True)
        o_ref[...] = total

    piece = pl.BlockSpec((tm, kp), lambda j, i: (i, 0))
    (out,), carried = _call(
        body, name=name, grid=(n // tn, m // tm),
        in_specs=[piece] * count + [pl.BlockSpec((tn, count * kp), lambda j, i: (j, 0))],
        out_specs=[pl.BlockSpec((tm, tn), lambda j, i: (i, j))],
        out_shape=[jax.ShapeDtypeStruct((m, n), F32)], operands=[*pieces, b],
        semantics=("parallel", "parallel"), jobs=jobs, after=after)
    return (out, carried) if jobs else out


def _weight_grad_pieces(a, pieces, *, name):
    tokens, m = a.shape
    np_ = pieces[0].shape[1]
    tm = 1024 if m % 1024 == 0 else _tile(m, 1408)
    tn = _tile(np_, 512)
    nb = np_ // tn
    out = None
    for p, piece in enumerate(pieces):
        def body(*refs):
            refs[-1][...] = _dot(refs[0][...], refs[1][...], ta=True)

        in_specs = [pl.BlockSpec((tokens, tm), lambda i, j: (0, i)), pl.BlockSpec((tokens, tn), lambda i, j: (0, j))]
        operands = [a, piece]
        if out is not None:
            in_specs.append(pl.BlockSpec(memory_space=pl.ANY))
            operands.append(out)
        out = pl.pallas_call(
            body, name="%s_%d" % (name, p), grid=(m // tm, nb), in_specs=in_specs,
            out_specs=pl.BlockSpec((tm, tn), lambda i, j, p=p: (i, p * nb + j)),
            out_shape=jax.ShapeDtypeStruct((m, len(pieces) * np_), F32),
            input_output_aliases={2: 0} if len(operands) == 3 else {},
            compiler_params=_params(("parallel", "parallel")),
        )(*operands)
    return out


def _weight_grad(a, g, *, name, jobs=()):
    tokens, m = a.shape
    tm = 1024 if m % 1024 == 0 else _tile(m, 1408)
    return _matmul(a, g, name=name, ta=True, tm=tm, tn=512, tk=tokens, jobs=jobs)


def _swiglu_fwd(n2, w_gate, w_up, *, tm=1024, tn=512, tk=2048, jobs=()):
    m, kdim = n2.shape
    n = w_gate.shape[1]
    tm, tn, tk = _tile(m, tm), _tile(n, tn), _tile(kdim, tk)
    nk = kdim // tk

    def body(a_ref, g_ref, u_ref, gate_ref, up_ref, act_ref, *acc):
        def products():
            a = a_ref[...]
            return _dot(a, g_ref[...]), _dot(a, u_ref[...])

        def finish(g, u):
            gate_ref[...] = g.astype(BF16)
            up_ref[...] = u.astype(BF16)
            act_ref[...] = (g * _sigmoid(g) * u).astype(BF16)

        _accumulate(acc, nk, products, finish)

    o_spec = pl.BlockSpec((tm, tn), lambda i, j, k: (i, j))
    o_shape = jax.ShapeDtypeStruct((m, n), BF16)
    return _call(
        body, name="swiglu_fwd", grid=(m // tm, n // tn, nk),
        in_specs=[_a_spec(False, tm, tk), _b_spec(False, tk, tn), _b_spec(False, tk, tn)],
        out_specs=[o_spec] * 3, out_shape=[o_shape] * 3, operands=[n2, w_gate, w_up],
        scratch_shapes=[pltpu.VMEM((tm, tn), F32)] * (2 * (nk > 1)),
        semantics=("parallel", "parallel", "arbitrary"), jobs=jobs)


def _swiglu_bwd_act(dh2, w_down, gate, up, *, tm=1024, tn=512, tk=2048):
    m, kdim = dh2.shape
    n = w_down.shape[0]
    tm, tn, tk = _tile(m, tm), _tile(n, tn), _tile(kdim, tk)
    nk = kdim // tk

    sub = _tile(tn, 256)

    def body(a_ref, b_ref, gate_ref, up_ref, dgate_ref, dup_ref, *acc):
        def finish(dact, cols=slice(None)):
            g = gate_ref[:, cols].astype(F32)
            u = up_ref[:, cols].astype(F32)
            sg = _sigmoid(g)
            dup_ref[:, cols] = (dact * g * sg).astype(BF16)
            dgate_ref[:, cols] = (dact * u * sg * (1.0 + g * (1.0 - sg))).astype(BF16)

        if nk == 1:
            a = a_ref[...]
            for c in range(tn // sub):
                cols = pl.ds(c * sub, sub)
                finish(_dot(a, b_ref[cols, :], tb=True), cols)
        else:
            _accumulate(acc, nk, lambda: (_dot(a_ref[...], b_ref[...], tb=True),), finish)

    o_spec = pl.BlockSpec((tm, tn), lambda i, j, k: (i, j))
    o_shape = jax.ShapeDtypeStruct((m, n), BF16)
    return pl.pallas_call(
        body, name="swiglu_bwd_act", grid=(m // tm, n // tn, nk),
        in_specs=[_a_spec(False, tm, tk), _b_spec(True, tk, tn), o_spec, o_spec],
        out_specs=[o_spec] * 2, out_shape=[o_shape] * 2,
        scratch_shapes=[pltpu.VMEM((tm, tn), F32)] * (nk > 1),
        compiler_params=_params(("parallel", "parallel", "arbitrary")),
    )(dh2, w_down, gate, up)


def _swiglu_bwd_in(dgate, dup, w_gate, w_up, *, tm=1024, tn=1024, tk=1408, jobs=()):
    m, kdim = dgate.shape
    n = w_gate.shape[0]
    tm, tn, tk = _tile(m, tm), _tile(n, tn), _tile(kdim, tk)
    nk = kdim // tk

    def body(a1_ref, a2_ref, b1_ref, b2_ref, o_ref, *acc):
        def product():
            return (_dot(a1_ref[...], b1_ref[...], tb=True) + _dot(a2_ref[...], b2_ref[...], tb=True),)

        def finish(total):
            o_ref[...] = total

        _accumulate(acc, nk, product, finish)

    a_spec, b_spec = _a_spec(False, tm, tk), _b_spec(True, tk, tn)
    (out,), carried = _call(
        body, name="swiglu_bwd_in", grid=(m // tm, n // tn, nk),
        in_specs=[a_spec, a_spec, b_spec, b_spec],
        out_specs=[pl.BlockSpec((tm, tn), lambda i, j, k: (i, j))],
        out_shape=[jax.ShapeDtypeStruct((m, n), F32)], operands=[dgate, dup, w_gate, w_up],
        scratch_shapes=[pltpu.VMEM((tm, tn), F32)] * (nk > 1),
        semantics=("parallel", "parallel", "arbitrary"), jobs=jobs)
    return out, carried


def _row_block(rows, cols):
    tr = min(rows, max(16, ELEM_BLOCK_BYTES // (4 * cols) // 16 * 16))
    while rows % tr:
        tr -= 16
    return tr


def _rmsnorm_fwd(x, g, *, name, after=None):
    s, d = x.shape
    tr = _row_block(s, d)

    def body(x_ref, g_ref, *rest):
        xv = x_ref[...]
        r = lax.rsqrt(jnp.mean(xv * xv, axis=-1, keepdims=True) + EPS)
        rest[-1][...] = (xv * r * g_ref[...]).astype(BF16)

    row = pl.BlockSpec((tr, d), lambda i: (i, 0))
    in_specs = [row, pl.BlockSpec((1, d), lambda i: (0, 0))]
    operands = [x, g]
    if after is not None:
        in_specs.append(pl.BlockSpec(after.shape, lambda i: (0, 0)))
        operands.append(after)
    return pl.pallas_call(
        body, name=name, grid=(s // tr,), in_specs=in_specs,
        out_specs=row, out_shape=jax.ShapeDtypeStruct((s, d), BF16),
        compiler_params=_params(("parallel",)),
    )(*operands)


def _rmsnorm_bwd_rows(xv, gv, dy):
    r = lax.rsqrt(jnp.mean(xv * xv, axis=-1, keepdims=True) + EPS)
    xhat = xv * r
    dxh = dy * gv
    dx = r * (dxh - xhat * jnp.mean(dxh * xhat, axis=-1, keepdims=True))
    return dx, dy * xhat


def _rmsnorm_bwd(dn, x, g, skip, *, name, after=()):
    s, d = x.shape
    tr = _row_block(s, d)

    def body(dn_ref, x_ref, g_ref, skip_ref, *rest):
        dx_ref, dxb_ref, dg_ref = rest[len(after):]
        dx, dgr = _rmsnorm_bwd_rows(x_ref[...], g_ref[...], dn_ref[...])
        dx = dx + skip_ref[...]
        dx_ref[...] = dx
        dxb_ref[...] = dx.astype(BF16)

        @pl.when(pl.program_id(0) == 0)
        def _():
            dg_ref[...] = jnp.zeros_like(dg_ref)

        dg_ref[...] += jnp.sum(dgr, axis=0, keepdims=True)

    row = pl.BlockSpec((tr, d), lambda i: (i, 0))
    vec = pl.BlockSpec((1, d), lambda i: (0, 0))
    return pl.pallas_call(
        body, name=name, grid=(s // tr,),
        in_specs=[row, row, vec, row] + [pl.BlockSpec(memory_space=pl.ANY)] * len(after),
        out_specs=[row, row, vec],
        out_shape=[jax.ShapeDtypeStruct((s, d), F32), jax.ShapeDtypeStruct((s, d), BF16),
                   jax.ShapeDtypeStruct((1, d), F32)],
        compiler_params=_params(("arbitrary",)),
    )(dn, x, g, skip, *after)


def _loss_head(h2, g, target):
    s, d = h2.shape
    tr = _row_block(s, d)

    def body(h_ref, g_ref, t_ref, dh_ref, dhb_ref, dg_ref, loss_ref):
        hv = h_ref[...]
        gv = g_ref[...]
        r = lax.rsqrt(jnp.mean(hv * hv, axis=-1, keepdims=True) + EPS)
        err = hv * r * gv - t_ref[...]
        dx, dgr = _rmsnorm_bwd_rows(hv, gv, err * (1.0 / d))
        dh_ref[...] = dx
        dhb_ref[...] = dx.astype(BF16)

        @pl.when(pl.program_id(0) == 0)
        def _():
            dg_ref[...] = jnp.zeros_like(dg_ref)
            loss_ref[...] = jnp.zeros_like(loss_ref)

        dg_ref[...] += jnp.sum(dgr, axis=0, keepdims=True)
        row_loss = jnp.mean(err * err, axis=-1, keepdims=True)
        loss_ref[...] += 0.5 * jnp.sum(row_loss, axis=0, keepdims=True)

    row = pl.BlockSpec((tr, d), lambda i: (i, 0))
    vec = pl.BlockSpec((1, d), lambda i: (0, 0))
    one = pl.BlockSpec((1, 1), lambda i: (0, 0))
    return pl.pallas_call(
        body, name="loss_head", grid=(s // tr,), in_specs=[row, vec, row],
        out_specs=[row, row, vec, one],
        out_shape=[jax.ShapeDtypeStruct((s, d), F32), jax.ShapeDtypeStruct((s, d), BF16),
                   jax.ShapeDtypeStruct((1, d), F32), jax.ShapeDtypeStruct((1, 1), F32)],
        compiler_params=_params(("arbitrary",)),
    )(h2, g, target)


def _attention_bias_tables():
    k = np.arange(-ATT_KB, ATT_KB + 1)[:, None, None]
    delta = k * ATT_BLOCK + np.arange(ATT_BLOCK)[None, None, :] - np.arange(ATT_BLOCK)[None, :, None]
    dist = np.abs(delta)
    count = np.zeros(delta.shape, np.int32)
    for window, dilation in DILATED_PATTERNS:
        count += (delta % dilation == 0) & (dist <= min(window // 2, ATT_NEAR))
    logc = np.where(count > 0, np.log(np.maximum(count, 1)), MASKED)
    return dist.astype(np.float32), logc.astype(np.float32)


def _far_bias_tables(per_class):
    steps = np.abs(np.arange(per_class)[:, None] - np.arange(per_class)[None, :]) * ATT_CLASSES
    valid = (steps > ATT_NEAR) & (steps <= ATT_REACH)
    return steps.astype(np.float32), np.where(valid, 0.0, MASKED).astype(np.float32)


def _to_classes(x):
    s, cols = x.shape
    return jnp.reshape(jnp.transpose(jnp.reshape(x, (s // ATT_CLASSES, ATT_CLASSES, cols)), (1, 0, 2)), (s, cols))


def _from_classes(x):
    s, cols = x.shape
    return jnp.reshape(jnp.transpose(jnp.reshape(x, (ATT_CLASSES, s // ATT_CLASSES, cols)), (1, 0, 2)), (s, cols))


def _head_bias(bias_ref, slope, dist_ref, logc_ref):
    for kk in range(ATT_WINDOW):
        bias_ref[kk] = logc_ref[kk] - slope * dist_ref[kk]
    bias_ref[ATT_WINDOW] = jnp.full((ATT_BLOCK, ATT_BLOCK), MASKED, F32)


def _window_start(i, nq, nwin):
    return jnp.clip(i - ATT_KB, 0, nq - nwin)


def _window_block(j, i):
    rows = pl.ds(pl.multiple_of(j * ATT_BLOCK, ATT_BLOCK), ATT_BLOCK)
    kk = j - i + ATT_KB
    return rows, jnp.where(jnp.logical_and(kk >= 0, kk < ATT_WINDOW), kk, ATT_WINDOW)


def _attention_far_fwd(qkv, slopes, n_heads, jobs=()):
    s = qkv.shape[0]
    per_class = s // ATT_CLASSES
    scale = HEAD_DIM ** -0.5
    dist, logc = _far_bias_tables(per_class)

    def body(slope_ref, q_ref, k_ref, v_ref, dist_ref, logc_ref, o_ref, lse_ref):
        bias = logc_ref[...] - slope_ref[pl.program_id(0)] * dist_ref[...]
        for a in range(ATT_FAR_GROUP):
            rows = pl.ds(a * per_class, per_class)
            sc = _dot(q_ref[rows, :], k_ref[rows, :], tb=True) * scale + bias
            m = jnp.maximum(jnp.max(sc, axis=-1, keepdims=True), ROW_MAX_INIT)
            p = jnp.exp(sc - m)
            l = jnp.maximum(jnp.sum(p, axis=-1, keepdims=True), 1e-30)
            o_ref[rows, :] = (_dot(p.astype(BF16), v_ref[rows, :]) / l).astype(BF16)
            lse_ref[rows, :] = jnp.broadcast_to(m + jnp.log(l), (per_class, HEAD_DIM))

    hh = n_heads
    blk = pl.BlockSpec((ATT_FAR_GROUP * per_class, HEAD_DIM), lambda h, r: (r, h))
    table = pl.BlockSpec(dist.shape, lambda h, r: (0, 0))
    return _call(
        body, name="attention_far_fwd", grid=(hh, ATT_CLASSES // ATT_FAR_GROUP),
        in_specs=[pl.BlockSpec(memory_space=pltpu.SMEM), blk,
                  pl.BlockSpec((ATT_FAR_GROUP * per_class, HEAD_DIM), lambda h, r: (r, hh + h)),
                  pl.BlockSpec((ATT_FAR_GROUP * per_class, HEAD_DIM), lambda h, r: (r, 2 * hh + h)), table, table],
        out_specs=[blk, blk],
        out_shape=[jax.ShapeDtypeStruct((s, hh * HEAD_DIM), BF16), jax.ShapeDtypeStruct((s, hh * HEAD_DIM), F32)],
        operands=[slopes, qkv, qkv, qkv, jnp.asarray(dist), jnp.asarray(logc)],
        semantics=("parallel", "parallel"), jobs=jobs)


def _attention_fwd(proj, slopes, far_out, far_lse, n_heads, jobs=()):
    s = proj.shape[0]
    nq = s // ATT_BLOCK
    scale = HEAD_DIM ** -0.5
    dist, logc = _attention_bias_tables()

    nwin = min(ATT_WINDOW, nq)

    group = math.gcd(ATT_NEAR_GROUP, nq)

    def body(slope_ref, q_ref, k_ref, v_ref, fo_ref, fl_ref, dist_ref, logc_ref, o_ref, lse_ref, bias_ref, s_ref):
        h, step = pl.program_id(0), pl.program_id(1)

        @pl.when(step == 0)
        def _():
            _head_bias(bias_ref, slope_ref[h], dist_ref, logc_ref)

        for a in range(group):
            i = step * group + a
            mine = pl.ds(a * ATT_BLOCK, ATT_BLOCK)
            q = q_ref[mine, :]
            first = _window_start(i, nq, nwin)
            m = jnp.full((ATT_BLOCK, 1), ROW_MAX_INIT, F32)
            for b in range(nwin):
                rows, kk = _window_block(first + b, i)
                sc = _dot(q, k_ref[rows, :], tb=True) * scale + bias_ref[kk]
                s_ref[a * nwin + b] = sc
                m = jnp.maximum(m, jnp.max(sc, axis=-1, keepdims=True))
            l = jnp.zeros((ATT_BLOCK, 1), F32)
            acc = jnp.zeros((ATT_BLOCK, HEAD_DIM), F32)
            for b in range(nwin):
                rows, _ = _window_block(first + b, i)
                p = jnp.exp(s_ref[a * nwin + b] - m)
                l = l + jnp.sum(p, axis=-1, keepdims=True)
                acc = acc + _dot(p.astype(BF16), v_ref[rows, :])
            near_lse = m + jnp.log(l)
            far_lse_col = fl_ref[mine, :1]
            lse = jnp.maximum(near_lse, far_lse_col)
            lse = lse + jnp.log(jnp.exp(near_lse - lse) + jnp.exp(far_lse_col - lse))
            o_ref[mine, :] = (acc * (jnp.exp(near_lse - lse) / l)
                              + fo_ref[mine, :].astype(F32) * jnp.exp(far_lse_col - lse)).astype(BF16)
            lse_ref[mine, :] = jnp.broadcast_to(lse, (ATT_BLOCK, HEAD_DIM))

    hh = n_heads
    blk = pl.BlockSpec((group * ATT_BLOCK, HEAD_DIM), lambda h, i: (i, h))
    table = pl.BlockSpec(dist.shape, lambda h, i: (0, 0, 0))
    return _call(
        body, name="attention_fwd", grid=(hh, nq // group),
        in_specs=[pl.BlockSpec(memory_space=pltpu.SMEM), blk,
                  pl.BlockSpec((s, HEAD_DIM), lambda h, i: (0, hh + h)),
                  pl.BlockSpec((s, HEAD_DIM), lambda h, i: (0, 2 * hh + h)), blk, blk, table, table],
        out_specs=[blk, blk],
        out_shape=[jax.ShapeDtypeStruct((s, hh * HEAD_DIM), BF16), jax.ShapeDtypeStruct((s, hh * HEAD_DIM), F32)],
        operands=[slopes, proj, proj, proj, far_out, far_lse, jnp.asarray(dist), jnp.asarray(logc)],
        scratch_shapes=[pltpu.VMEM((ATT_WINDOW + 1, ATT_BLOCK, ATT_BLOCK), F32),
                        pltpu.VMEM((group * nwin, ATT_BLOCK, ATT_BLOCK), F32)],
        semantics=("parallel", "arbitrary"), jobs=jobs)


def _attention_far_bwd(qkv, slopes, out, dout, lse, n_heads):
    s = qkv.shape[0]
    per_class = s // ATT_CLASSES
    scale = HEAD_DIM ** -0.5
    dist, logc = _far_bias_tables(per_class)

    def body(slope_ref, q_ref, k_ref, v_ref, o_ref, do_ref, lse_ref, dist_ref, logc_ref, dq_ref, dk_ref, dv_ref):
        bias = logc_ref[...] - slope_ref[pl.program_id(0)] * dist_ref[...]
        for a in range(ATT_FAR_GROUP):
            rows = pl.ds(a * per_class, per_class)
            q, k, do = q_ref[rows, :], k_ref[rows, :], do_ref[rows, :]
            delta = jnp.sum(do.astype(F32) * o_ref[rows, :].astype(F32), axis=-1, keepdims=True)
            p = jnp.exp(_dot(q, k, tb=True) * scale + bias - lse_ref[rows, :1])
            dv_ref[rows, :] = _dot(p.astype(BF16), do, ta=True).astype(BF16)
            ds = (p * (_dot(do, v_ref[rows, :], tb=True) - delta) * scale).astype(BF16)
            dk_ref[rows, :] = _dot(ds, q, ta=True).astype(BF16)
            dq_ref[rows, :] = _dot(ds, k).astype(BF16)

    hh = n_heads
    blk = pl.BlockSpec((ATT_FAR_GROUP * per_class, HEAD_DIM), lambda h, r: (r, h))
    table = pl.BlockSpec(dist.shape, lambda h, r: (0, 0))
    o_shape = jax.ShapeDtypeStruct((s, hh * HEAD_DIM), BF16)
    return pl.pallas_call(
        body, name="attention_far_bwd", grid=(hh, ATT_CLASSES // ATT_FAR_GROUP),
        in_specs=[pl.BlockSpec(memory_space=pltpu.SMEM), blk,
                  pl.BlockSpec((ATT_FAR_GROUP * per_class, HEAD_DIM), lambda h, r: (r, hh + h)),
                  pl.BlockSpec((ATT_FAR_GROUP * per_class, HEAD_DIM), lambda h, r: (r, 2 * hh + h)),
                  blk, blk, blk, table, table],
        out_specs=[blk] * 3, out_shape=[o_shape] * 3,
        compiler_params=_params(("parallel", "parallel")),
    )(slopes, qkv, qkv, qkv, out, dout, lse, jnp.asarray(dist), jnp.asarray(logc))


def _attention_bwd(proj, slopes, out, lse, dmixed, far_grads, n_heads, jobs=()):
    s = proj.shape[0]
    nq = s // ATT_BLOCK
    scale = HEAD_DIM ** -0.5
    dist, logc = _attention_bias_tables()

    nwin = min(ATT_WINDOW, nq)
    group = math.gcd(ATT_NEAR_GROUP, nq)

    def body(slope_ref, q_ref, k_ref, v_ref, o_ref, do_ref, lse_ref, fdq_ref, fdk_ref, fdv_ref, dist_ref, logc_ref,
             dq_ref, dk_ref, dv_ref, dk_acc, dv_acc, bias_ref):
        h, step = pl.program_id(0), pl.program_id(1)

        @pl.when(step == 0)
        def _():
            dk_acc[...] = jnp.zeros_like(dk_acc)
            dv_acc[...] = jnp.zeros_like(dv_acc)
            _head_bias(bias_ref, slope_ref[h], dist_ref, logc_ref)

        for a in range(group):
            i = step * group + a
            mine = pl.ds(a * ATT_BLOCK, ATT_BLOCK)
            q = q_ref[mine, :]
            do = do_ref[mine, :]
            lse_col = lse_ref[mine, :1]
            delta = jnp.sum(do.astype(F32) * o_ref[mine, :].astype(F32), axis=-1, keepdims=True)
            first = _window_start(i, nq, nwin)
            dq = jnp.zeros((ATT_BLOCK, HEAD_DIM), F32)
            for b in range(nwin):
                rows, kk = _window_block(first + b, i)
                kj = k_ref[rows, :]
                vj = v_ref[rows, :]
                p = jnp.exp(_dot(q, kj, tb=True) * scale + bias_ref[kk] - lse_col)
                dv_acc[rows, :] += _dot(p.astype(BF16), do, ta=True)
                dp = _dot(do, vj, tb=True)
                ds = (p * (dp - delta) * scale).astype(BF16)
                dk_acc[rows, :] += _dot(ds, q, ta=True)
                dq = dq + _dot(ds, kj)
            dq_ref[mine, :] = (dq + fdq_ref[mine, :].astype(F32)).astype(BF16)

        @pl.when(step == nq // group - 1)
        def _():
            dk_ref[...] = (dk_acc[...] + fdk_ref[...].astype(F32)).astype(BF16)
            dv_ref[...] = (dv_acc[...] + fdv_ref[...].astype(F32)).astype(BF16)

    hh = n_heads
    blk = pl.BlockSpec((group * ATT_BLOCK, HEAD_DIM), lambda h, i: (i, h))
    col = pl.BlockSpec((s, HEAD_DIM), lambda h, i: (0, h))
    table = pl.BlockSpec(dist.shape, lambda h, i: (0, 0, 0))
    o_shape = jax.ShapeDtypeStruct((s, hh * HEAD_DIM), BF16)
    return _call(
        body, name="attention_bwd", grid=(hh, nq // group),
        in_specs=[pl.BlockSpec(memory_space=pltpu.SMEM), blk,
                  pl.BlockSpec((s, HEAD_DIM), lambda h, i: (0, hh + h)),
                  pl.BlockSpec((s, HEAD_DIM), lambda h, i: (0, 2 * hh + h)),
                  blk, blk, blk, blk, col, col, table, table],
        out_specs=[blk, col, col], out_shape=[o_shape] * 3,
        operands=[slopes, proj, proj, proj, out, dmixed, lse, *far_grads, jnp.asarray(dist), jnp.asarray(logc)],
        scratch_shapes=[pltpu.VMEM((s, HEAD_DIM), F32)] * 2
        + [pltpu.VMEM((ATT_WINDOW + 1, ATT_BLOCK, ATT_BLOCK), F32)],
        semantics=("parallel", "arbitrary"), jobs=jobs)


def _ret_decays(lgc, lga, strict_c, strict_a):
    c = RET_CHUNK
    rel = (lax.broadcasted_iota(jnp.int32, (c, c), 0) - lax.broadcasted_iota(jnp.int32, (c, c), 1)).astype(F32)
    in_c = (rel > 0) if strict_c else (rel >= 0)
    in_a = (rel < 0) if strict_a else (rel <= 0)
    mask = (jnp.where(in_c, jnp.exp(lgc * jnp.maximum(rel, 0.0)), 0.0)
            + jnp.where(in_a, jnp.exp(lga * jnp.maximum(-rel, 0.0)), 0.0))
    idx = lax.broadcasted_iota(jnp.int32, (c, 1), 0).astype(F32)
    ones = jnp.ones((1, HEAD_DIM), F32)
    dec = dict(
        rel=rel, mask=mask, idx=idx,
        a_c=jnp.exp(lgc * (idx + 1.0)), b_c=jnp.exp(lgc * (c - 1.0 - idx)), chunk_c=jnp.exp(ones * (lgc * c)),
        a_a=jnp.exp(lga * (c - idx)), b_a=jnp.exp(lga * idx), chunk_a=jnp.exp(ones * (lga * c)),
    )
    return dec


def _scaled(x, col):
    return (x.astype(F32) * col).astype(BF16)


def _chunk_rows(i):
    return pl.ds(pl.multiple_of(i * RET_CHUNK, RET_CHUNK), RET_CHUNK)


def _chunk_loop(nc, step, init, unroll=RET_UNROLL):
    group = math.gcd(nc, unroll)

    def trip(t, carry):
        for u in range(group):
            carry = step(t * group + u, carry)
        return carry

    return lax.fori_loop(0, nc // group, trip, init)


def _retention(a, b, c, lg_c, lg_a, *, strict_c, strict_a, scale, n_heads, name, gate=None, norm_w=None, jobs=()):
    s = a[0].shape[0]
    nc = s // RET_CHUNK
    epilogue = gate is not None

    def body(*refs):
        lgc_ref, lga_ref, a_ref, b_ref, c_ref = refs[:5]
        if epilogue:
            g_ref, w_ref, o_ref, mix_ref, sa_ref = refs[5:]
        else:
            o_ref, sa_ref = refs[5:]
        h = pl.program_id(0)
        dec = _ret_decays(lgc_ref[h], lga_ref[h], strict_c, strict_a)

        def reverse(t, state):
            i = nc - 1 - t
            sa_ref[i] = state.astype(BF16)
            rows = _chunk_rows(i)
            return state * dec["chunk_a"] + _dot(_scaled(b_ref[rows, :], dec["b_a"]), c_ref[rows, :], ta=True)

        _chunk_loop(nc, reverse, jnp.zeros((HEAD_DIM, HEAD_DIM), F32))

        def forward(i, state):
            rows = _chunk_rows(i)
            ai, bi, ci = a_ref[rows, :], b_ref[rows, :], c_ref[rows, :]
            inner = (_dot(ai, bi, tb=True) * dec["mask"]).astype(BF16)
            out = (_dot(inner, ci) + _dot(_scaled(ai, dec["a_c"]), state.astype(BF16))
                   + _dot(_scaled(ai, dec["a_a"]), sa_ref[i])) * scale
            o_ref[rows, :] = out.astype(BF16)
            if epilogue:
                r = lax.rsqrt(jnp.mean(out * out, axis=-1, keepdims=True) + EPS)
                g = g_ref[rows, :].astype(F32)
                mix_ref[rows, :] = (out * r * w_ref[...] * (g * _sigmoid(g))).astype(BF16)
            return state * dec["chunk_c"] + _dot(_scaled(bi, dec["b_c"]), ci, ta=True)

        _chunk_loop(nc, forward, jnp.zeros((HEAD_DIM, HEAD_DIM), F32))

    def col(first):
        return pl.BlockSpec((s, HEAD_DIM), lambda h: (0, first + h))

    smem = pl.BlockSpec(memory_space=pltpu.SMEM)
    in_specs = [smem, smem, col(a[1]), col(b[1]), col(c[1])]
    operands = [lg_c, lg_a, a[0], b[0], c[0]]
    o_shape = jax.ShapeDtypeStruct((s, n_heads * HEAD_DIM), BF16)
    out_specs, out_shape = [col(0)], [o_shape]
    if epilogue:
        in_specs += [col(gate[1]), pl.BlockSpec((1, HEAD_DIM), lambda h: (0, h))]
        operands += [gate[0], norm_w]
        out_specs, out_shape = [col(0)] * 2, [o_shape] * 2
    res, carried = _call(
        body, name=name, grid=(n_heads,), in_specs=in_specs, out_specs=out_specs, out_shape=out_shape,
        operands=operands, scratch_shapes=[pltpu.VMEM((nc, HEAD_DIM, HEAD_DIM), BF16)],
        semantics=("parallel",), jobs=jobs)
    res = res if epilogue else res[0]
    return (res, carried) if jobs else res


def _retention_decay_grads(a, b, c, e, lg_c, lg_a, *, scale, n_heads):
    s = a[0].shape[0]
    nc = s // RET_CHUNK
    cf = float(RET_CHUNK)

    def body(lgc_ref, lga_ref, a_ref, b_ref, c_ref, e_ref, gc_ref, ga_ref, sa_ref, ta_ref):
        h = pl.program_id(0)
        lgc, lga = lgc_ref[h], lga_ref[h]
        dec = _ret_decays(lgc, lga, True, True)
        rel, idx = dec["rel"], dec["idx"]
        w_c = jnp.where(rel > 0, rel * jnp.exp(lgc * jnp.maximum(rel, 0.0)), 0.0)
        w_a = jnp.where(rel < 0, -rel * jnp.exp(lga * jnp.maximum(-rel, 0.0)), 0.0)
        zero = jnp.zeros((HEAD_DIM, HEAD_DIM), F32)

        def reverse(t, carry):
            st, dst = carry
            i = nc - 1 - t
            sa_ref[i] = st.astype(BF16)
            ta_ref[i] = dst.astype(BF16)
            rows = _chunk_rows(i)
            bi, ci = b_ref[rows, :], c_ref[rows, :]
            st_new = st * dec["chunk_a"] + _dot(_scaled(bi, dec["b_a"]), ci, ta=True)
            dst_new = (cf * st + dst) * dec["chunk_a"] + _dot(_scaled(bi, idx * dec["b_a"]), ci, ta=True)
            return st_new, dst_new

        _chunk_loop(nc, reverse, (zero, zero))

        def forward(i, carry):
            st, dst, acc_c, acc_a = carry
            rows = _chunk_rows(i)
            ai, bi, ci = a_ref[rows, :], b_ref[rows, :], c_ref[rows, :]
            ev = e_ref[rows, :].astype(F32)
            pg = _dot(ai, bi, tb=True) * _dot(e_ref[rows, :], ci, tb=True)
            a_c, a_a = _scaled(ai, dec["a_c"]), _scaled(ai, dec["a_a"])
            inter_c = _dot(a_c, st.astype(BF16)) * (idx + 1.0) + _dot(a_c, dst.astype(BF16))
            inter_a = _dot(a_a, sa_ref[i]) * (cf - idx) + _dot(a_a, ta_ref[i])
            acc_c = acc_c + jnp.sum(pg * w_c, axis=0, keepdims=True) + jnp.sum(inter_c * ev, axis=0, keepdims=True)
            acc_a = acc_a + jnp.sum(pg * w_a, axis=0, keepdims=True) + jnp.sum(inter_a * ev, axis=0, keepdims=True)
            st_new = st * dec["chunk_c"] + _dot(_scaled(bi, dec["b_c"]), ci, ta=True)
            dst_new = ((cf * st + dst) * dec["chunk_c"]
                       + _dot(_scaled(bi, (cf - 1.0 - idx) * dec["b_c"]), ci, ta=True))
            return st_new, dst_new, acc_c, acc_a

        row = jnp.zeros((1, HEAD_DIM), F32)
        _, _, acc_c, acc_a = _chunk_loop(nc, forward, (zero, zero, row, row))
        gc_ref[...] = jnp.broadcast_to(jnp.sum(acc_c, axis=-1, keepdims=True) * scale, gc_ref.shape)
        ga_ref[...] = jnp.broadcast_to(jnp.sum(acc_a, axis=-1, keepdims=True) * scale, ga_ref.shape)

    def col(first):
        return pl.BlockSpec((s, HEAD_DIM), lambda h: (0, first + h))

    smem = pl.BlockSpec(memory_space=pltpu.SMEM)
    o_spec = pl.BlockSpec((1, 8, HEAD_DIM), lambda h: (h, 0, 0))
    o_shape = jax.ShapeDtypeStruct((n_heads, 8, HEAD_DIM), F32)
    gc, ga = pl.pallas_call(
        body, name="retention_decay_grads", grid=(n_heads,),
        in_specs=[smem, smem, col(a[1]), col(b[1]), col(c[1]), col(e[1])],
        out_specs=[o_spec] * 2, out_shape=[o_shape] * 2,
        scratch_shapes=[pltpu.VMEM((nc, HEAD_DIM, HEAD_DIM), BF16)] * 2,
        compiler_params=_params(("parallel",)),
    )(lg_c, lg_a, a[0], b[0], c[0], e[0])
    return gc[:, 0, 0], ga[:, 0, 0]


def _ret_gate_bwd(dmixed, first_col, out, proj, gate_col, norm_w, n_heads):
    s = out.shape[0]
    tr = _row_block(s, 8 * HEAD_DIM)

    def body(dm_ref, o_ref, g_ref, w_ref, do_ref, dg_ref, dw_ref):
        dm = dm_ref[...].astype(F32)
        ov = o_ref[...].astype(F32)
        g = g_ref[...].astype(F32)
        w = w_ref[...]
        r = lax.rsqrt(jnp.mean(ov * ov, axis=-1, keepdims=True) + EPS)
        ohat = ov * r
        sg = _sigmoid(g)
        silu = g * sg
        dg_ref[...] = (dm * ohat * w * sg * (1.0 + g * (1.0 - sg))).astype(BF16)
        dohat = dm * w * silu
        do_ref[...] = (r * (dohat - ohat * jnp.mean(dohat * ohat, axis=-1, keepdims=True))).astype(BF16)

        @pl.when(pl.program_id(1) == 0)
        def _():
            dw_ref[...] = jnp.zeros_like(dw_ref)

        dw_ref[...] += jnp.sum(dm * ohat * silu, axis=0, keepdims=True)

    def blk(first):
        return pl.BlockSpec((tr, HEAD_DIM), lambda h, i: (i, first + h))

    vec = pl.BlockSpec((1, HEAD_DIM), lambda h, i: (0, h))
    o_shape = jax.ShapeDtypeStruct((s, n_heads * HEAD_DIM), BF16)
    return pl.pallas_call(
        body, name="ret_gate_bwd", grid=(n_heads, s // tr),
        in_specs=[blk(first_col), blk(0), blk(gate_col), vec],
        out_specs=[blk(0), blk(0), vec],
        out_shape=[o_shape, o_shape, jax.ShapeDtypeStruct((1, n_heads * HEAD_DIM), F32)],
        compiler_params=_params(("parallel", "arbitrary")),
    )(dmixed, out, proj, norm_w)


def _step(x, target, norm_mix_w, ret_decay_fwd, ret_decay_bwd, ret_norm_w, norm_ffn_w, norm_final_w, own,
          w_in_started, pos):
    d = x.shape[1]
    nh = d // (2 * HEAD_DIM)
    scale = HEAD_DIM ** -0.5
    slopes = jnp.exp2(-8.0 * jnp.arange(1, nh + 1, dtype=F32) / nh)
    lg_f = -jnp.exp(ret_decay_fwd)
    lg_b = -jnp.exp(ret_decay_bwd)
    q_r, k_r, v_r, g_r = 3 * nh, 4 * nh, 5 * nh, 6 * nh
    ax = BIG_AXIS

    def gather(names, arrays, stage, part=None):
        return _gather_job(arrays, [ax[k] for k in names], stage, part)

    def add_halves(k, g, received):
        return _add_halves(g, received, ax[k], pos, name="grad_add_halves_" + k)

    def sum_parts(k, g, received, parts):
        return _sum_chip_parts(g, received, parts, ax[k], pos, name="grad_sum_parts_" + k)

    sems, w_in, token = w_in_started
    n1 = _rmsnorm_fwd(x, norm_mix_w, name="norm_mix_fwd", after=token)
    w_in = _split_gather_wait(sems, w_in, ax["w_in"], [n1] + [own[k] for k in BIG if k != "w_in"])
    (w_in,) = _run_jobs([gather(["w_in"], [w_in], "d2d")], name="all_gather_w_in_sibling")
    proj, [[w_gate]] = _matmul(n1, w_in, name="in_proj", out_dtype=BF16, tm=2048,
                               jobs=[gather(["w_gate"], [own["w_gate"]], "ici")])
    qkv_classes = _to_classes(proj[:, :3 * nh * HEAD_DIM])
    (ret, ret_mixed), [[w_gate], [w_out]] = _retention(
        (proj, q_r), (proj, k_r), (proj, v_r), lg_f, lg_b, strict_c=False, strict_a=True, scale=scale, n_heads=nh,
        name="retention_fwd", gate=(proj, g_r), norm_w=ret_norm_w,
        jobs=[gather(["w_gate"], [w_gate], "d2d"), gather(["w_out"], [own["w_out"]], "ici")])
    (far_out, far_lse), [[w_up]] = _attention_far_fwd(
        qkv_classes, slopes, nh, jobs=[gather(["w_up"], [own["w_up"]], "ici", (0, 1, 4))])
    (attn, lse), [[w_out], [w_up]] = _attention_fwd(
        proj, slopes, _from_classes(far_out), _from_classes(far_lse), nh,
        jobs=[gather(["w_out"], [w_out], "d2d"),
              _fuse(gather(["w_up"], [w_up], "d2d", (0, 1, 4)), gather(["w_up"], [w_up], "ici", (1, 2, 4)))])
    mixed = jnp.concatenate([attn, ret_mixed], axis=1)
    h1, [[w_up]] = _matmul(
        mixed, w_out, name="out_proj", residual=x,
        jobs=[_fuse(gather(["w_up"], [w_up], "d2d", (1, 2, 4)), gather(["w_up"], [w_up], "ici", (3, 1, 4)))])
    (w_up,) = _run_jobs([gather(["w_up"], [w_up], "d2d", (3, 1, 4))], name="all_gather_w_up_sibling")
    n2 = _rmsnorm_fwd(h1, norm_ffn_w, name="norm_ffn_fwd")
    (gate, up, act), [[w_down]] = _swiglu_fwd(n2, w_gate, w_up, jobs=[gather(["w_down"], [own["w_down"]], "ici")])
    (w_down,) = _run_jobs([gather(["w_down"], [w_down], "d2d")], name="all_gather_w_down_sibling")
    h2 = _matmul(act, w_down, name="down_proj", residual=h1, tk=2816)
    dh2, dh2_b, d_norm_final, loss = _loss_head(h2, norm_final_w, target)

    dgate, dup = _swiglu_bwd_act(dh2_b, w_down, gate, up)
    g_down = _weight_grad(act, dh2_b, name="grad_w_down")
    g_gate, [[r_down]] = _weight_grad(n2, dgate, name="grad_w_gate", jobs=[_exchange_job([g_down], [ax["w_down"]])])
    s_down = add_halves("w_down", g_down, r_down)
    g_up, [[r_gate], [p_down]] = _weight_grad(
        n2, dup, name="grad_w_up",
        jobs=[_exchange_job([g_gate], [ax["w_gate"]]), _send_sums_job([s_down], [ax["w_down"]], (0, 1, 2))])
    s_gate = add_halves("w_gate", g_gate, r_gate)
    dn2, [[r_up], [p_gate], [p_down]] = _swiglu_bwd_in(
        dgate, dup, w_gate, w_up,
        jobs=[_exchange_job([g_up], [ax["w_up"]]), _send_sums_job([s_gate], [ax["w_gate"]]),
              _send_sums_job([s_down], [ax["w_down"]], (1, 1, 2), landing=[p_down])])
    h_down = sum_parts("w_down", g_down, r_down, p_down)
    s_up = add_halves("w_up", g_up, r_up)
    h_gate = sum_parts("w_gate", g_gate, r_gate, p_gate)
    dh1, dh1_b, d_norm_ffn = _rmsnorm_bwd(dn2, h1, norm_ffn_w, dh2, name="norm_ffn_bwd")

    dmixed, [[gr_down], [p_up]] = _matmul(
        dh1_b, w_out, name="out_proj_bwd", tb=True, out_dtype=BF16,
        jobs=[_join_job([h_down], [ax["w_down"]]), _send_sums_job([s_up], [ax["w_up"]], (0, 1, 4))])
    far_in = [_to_classes(t) for t in (attn, dmixed[:, :nh * HEAD_DIM], lse)]
    g_out, [[p_up]] = _weight_grad(mixed, dh1_b, name="grad_w_out",
                                   jobs=[_send_sums_job([s_up], [ax["w_up"]], (1, 1, 4), landing=[p_up])])
    d_ret, dg_r, d_ret_norm = _ret_gate_bwd(dmixed, nh, ret, proj, g_r, ret_norm_w, nh)
    far_grads = _attention_far_bwd(qkv_classes, slopes, *far_in, nh)
    far_grads = [_from_classes(t) for t in far_grads]
    dq_r, [[gr_gate], [p_up]] = _retention(
        (d_ret, 0), (proj, v_r), (proj, k_r), lg_f, lg_b, strict_c=False, strict_a=True, scale=scale, n_heads=nh,
        name="retention_dq",
        jobs=[_join_job([h_gate], [ax["w_gate"]]), _send_sums_job([s_up], [ax["w_up"]], (2, 1, 4), landing=[p_up])])
    (dq_a, dk_a, dv_a), [[p_up], [r_out]] = _attention_bwd(
        proj, slopes, attn, lse, dmixed, far_grads, nh,
        jobs=[_send_sums_job([s_up], [ax["w_up"]], (3, 1, 4), landing=[p_up]),
              _exchange_job([g_out], [ax["w_out"]])])
    s_out = add_halves("w_out", g_out, r_out)
    h_up = sum_parts("w_up", g_up, r_up, p_up)
    dv_r, [[p_out], [gr_up]] = _retention(
        (proj, k_r), (proj, q_r), (d_ret, 0), lg_b, lg_f, strict_c=True, strict_a=False, scale=scale, n_heads=nh,
        name="retention_dv", jobs=[_send_sums_job([s_out], [ax["w_out"]]), _join_job([h_up], [ax["w_up"]])])
    h_out = sum_parts("w_out", g_out, r_out, p_out)
    dk_r, [[gr_out]] = _retention(
        (proj, v_r), (d_ret, 0), (proj, q_r), lg_b, lg_f, strict_c=True, strict_a=False, scale=scale, n_heads=nh,
        name="retention_dk", jobs=[_join_job([h_out], [ax["w_out"]])])
    dlg_f, dlg_b = _retention_decay_grads((proj, q_r), (proj, k_r), (proj, v_r), (d_ret, 0), lg_f, lg_b,
                                          scale=scale, n_heads=nh)
    dproj = [dq_a, dk_a, dv_a, dq_r, dk_r, dv_r, dg_r]
    g_in = _weight_grad_pieces(n1, dproj, name="grad_w_in")
    exchange = _split_start(_exchange_job([g_in], [ax["w_in"]]), name="grad_exchange_w_in_start")
    dn1 = _matmul_pieces_nt(dproj, w_in, name="in_proj_bwd", after=[exchange["token"]])
    g_in, r_in = _split_wait(exchange, [dn1], name="grad_exchange_w_in_wait")
    s_in = add_halves("w_in", g_in, r_in)
    sending = _split_start(_send_sums_job([s_in], [ax["w_in"]]), name="grad_send_w_in_start")
    dx, _, d_norm_mix = _rmsnorm_bwd(dn1, x, norm_mix_w, dh1, name="norm_mix_bwd", after=[sending["token"]])

    small = dict(loss=loss[0, 0], norm_mix_w=d_norm_mix, ret_decay_fwd=dlg_f * lg_f, ret_decay_bwd=dlg_b * lg_b,
                 ret_norm_w=d_ret_norm, norm_ffn_w=d_norm_ffn, norm_final_w=d_norm_final)
    return (dx, dict(w_out=gr_out, w_gate=gr_gate, w_up=gr_up, w_down=gr_down), small,
            dict(sending=sending, grad=g_in, received=r_in))


def _mesh_position():
    x, y, c = lax.axis_index("x"), lax.axis_index("y"), lax.axis_index("c")
    chips = [(1 - x, y), (x, 1 - y), (1 - x, 1 - y)]
    return x, y, c, chips


def _span(span):
    if span is None:
        return slice(None)
    start, size, step = span
    return pl.ds(start if isinstance(start, int) else pl.multiple_of(start, step), size)


def _part_rows(part, rows):
    first, count, of = part
    return first * (rows // of), count * (rows // of), rows // of


def _region(ref, axis, shard, half, shard_size, half_size, part=None, total_rows=None):
    along = None if shard is None else (shard * shard_size, shard_size, shard_size)
    other = None if half is None else (half * half_size, half_size, half_size)
    rows, cols = (other, along) if axis == 1 else (along, other)
    if part is not None:
        start, size, _ = rows if rows is not None else (0, total_rows, None)
        offset, size, step = _part_rows(part, size)
        rows = (start + offset, size, step)
    return ref.at[_span(rows), _span(cols)]


def _fuse(first, second):
    assert not (first.ins or first.outs or second.ins or second.outs)
    assert len(first.ios) == len(second.ios) and all(a is b for a, b in zip(first.ios, second.ios))
    cut = len(first.sems)

    def start(refs, sems):
        first.start(refs, sems[:cut])
        second.start(refs, sems[cut:])

    def finish(refs, sems):
        first.finish(refs, sems[:cut])
        second.finish(refs, sems[cut:])

    return _Job(ios=first.ios, sems=first.sems + second.sems, start=start, finish=finish)


def _gather_job(full, axes, stage, part=None):
    n = len(full)

    def copies(refs, sems):
        send_sem, recv_sem = sems
        x, y, c, chips = _mesh_position()
        me = 2 * x + y

        def copy(w, k, shard, half, target):
            rows_cols = full[w].shape
            place = _region(refs[w], axes[w], shard, half, rows_cols[axes[w]] // N_CHIPS, rows_cols[1 - axes[w]] // 2,
                            part)
            return pltpu.make_async_remote_copy(
                src_ref=place, dst_ref=place, send_sem=send_sem.at[w, k], recv_sem=recv_sem.at[w, k],
                device_id=target, device_id_type=MESH)

        def sent(w, k):
            if stage == "ici":
                return copy(w, k, me, c, (chips[k][0], chips[k][1], c))
            return copy(w, k, 2 * chips[k][0] + chips[k][1], c, (x, y, 1 - c))

        def landed(w, k):
            return copy(w, k, 2 * chips[k][0] + chips[k][1], c if stage == "ici" else 1 - c, (x, y, 1 - c))

        return sent, landed

    def start(refs, sems):
        sent, _ = copies(refs, sems)
        for w in range(n):
            for k in range(3):
                sent(w, k).start()

    def finish(refs, sems):
        sent, landed = copies(refs, sems)
        for w in range(n):
            for k in range(3):
                landed(w, k).wait_recv()
                sent(w, k).wait_send()

    return _Job(ios=full, sems=[pltpu.SemaphoreType.DMA((n, 3))] * 2, start=start, finish=finish)


def _exchange_job(grads, axes):
    n = len(grads)

    def half_shape(w):
        return tuple(d // 2 if a != axes[w] else d for a, d in enumerate(grads[w].shape))

    def copy(refs, sems, w):
        x, y, c, _ = _mesh_position()
        return pltpu.make_async_remote_copy(
            src_ref=_region(refs[w], axes[w], None, 1 - c, 0, half_shape(w)[1 - axes[w]]), dst_ref=refs[n + w],
            send_sem=sems[0].at[w], recv_sem=sems[1].at[w], device_id=(x, y, 1 - c), device_id_type=MESH)

    def start(refs, sems):
        for w in range(n):
            copy(refs, sems, w).start()

    def finish(refs, sems):
        for w in range(n):
            copy(refs, sems, w).wait()

    return _Job(ins=grads, outs=[jax.ShapeDtypeStruct(half_shape(w), F32) for w in range(n)],
                sems=[pltpu.SemaphoreType.DMA((n,))] * 2, start=start, finish=finish)


def _half_block_spec(axis, block, half_blocks, use_half):
    if axis == 1:
        if use_half:
            return pl.BlockSpec(block, lambda i, pos: (pos[0] * half_blocks + i, 0))
        return pl.BlockSpec(block, lambda i, pos: (i, 0))
    if use_half:
        return pl.BlockSpec(block, lambda i, pos: (i, pos[0]))
    return pl.BlockSpec(block, lambda i, pos: (i, 0))


def _add_halves(grad, received, axis, pos, *, name):
    rows, cols = received.shape
    tr = _row_block(rows, cols)
    nb = rows // tr

    def body(pos_ref, g_ref, r_ref, o_ref):
        o_ref[...] = (g_ref[...] + r_ref[...]).astype(BF16)

    blk = (tr, cols)
    return pl.pallas_call(
        body, name=name, out_shape=jax.ShapeDtypeStruct((rows, cols), BF16),
        grid_spec=pltpu.PrefetchScalarGridSpec(
            num_scalar_prefetch=1, grid=(nb,),
            in_specs=[_half_block_spec(axis, blk, nb, True), _half_block_spec(axis, blk, nb, False)],
            out_specs=_half_block_spec(axis, blk, nb, False)),
        compiler_params=_params(("parallel",)),
    )(pos, grad, received)


def _send_sums_job(sums, axes, part=None, landing=None):
    n = len(sums)

    def part_shape(w):
        return tuple(d // N_CHIPS if a == axes[w] else d for a, d in enumerate(sums[w].shape))

    def copy(refs, sems, w, k):
        x, y, c, chips = _mesh_position()
        shard = 2 * chips[k][0] + chips[k][1]
        rows = part_shape(w)[0]
        dst = refs[n + w].at[k]
        if part is not None:
            offset, size, _ = _part_rows(part, rows)
            dst = refs[n + w].at[k, pl.ds(offset, size), :]
        return pltpu.make_async_remote_copy(
            src_ref=_region(refs[w], axes[w], shard, None, part_shape(w)[axes[w]], 0, part, rows), dst_ref=dst,
            send_sem=sems[0].at[w, k], recv_sem=sems[1].at[w, k],
            device_id=(chips[k][0], chips[k][1], c), device_id_type=MESH)

    def start(refs, sems):
        for w in range(n):
            for k in range(3):
                copy(refs, sems, w, k).start()

    def finish(refs, sems):
        for w in range(n):
            for k in range(3):
                copy(refs, sems, w, k).wait()

    sems = [pltpu.SemaphoreType.DMA((n, 3))] * 2
    if landing is not None:
        return _Job(ins=sums, ios=landing, sems=sems, start=start, finish=finish)
    return _Job(ins=sums, outs=[jax.ShapeDtypeStruct((3,) + part_shape(w), BF16) for w in range(n)],
                sems=sems, start=start, finish=finish)


def _sum_chip_parts(grad, received, parts, axis, pos, *, name):
    _, rows, cols = parts.shape
    tr = _row_block(rows, cols)
    nb = rows // tr
    blk = (tr, cols)

    def body(pos_ref, g_ref, r_ref, p_ref, o_ref):
        total = g_ref[...] + r_ref[...]
        for k in range(3):
            total = total + p_ref[k].astype(F32)
        o_ref[...] = total

    if axis == 1:
        g_spec = pl.BlockSpec(blk, lambda i, pos: (pos[0] * nb + i, pos[1]))
        r_spec = pl.BlockSpec(blk, lambda i, pos: (i, pos[1]))
        o_spec = pl.BlockSpec(blk, lambda i, pos: (pos[0] * nb + i, 0))
        shard_shape = (2 * rows, cols)
    else:
        g_spec = pl.BlockSpec(blk, lambda i, pos: (pos[1] * nb + i, pos[0]))
        r_spec = pl.BlockSpec(blk, lambda i, pos: (pos[1] * nb + i, 0))
        o_spec = pl.BlockSpec(blk, lambda i, pos: (i, pos[0]))
        shard_shape = (rows, 2 * cols)
    return pl.pallas_call(
        body, name=name, out_shape=jax.ShapeDtypeStruct(shard_shape, F32),
        grid_spec=pltpu.PrefetchScalarGridSpec(
            num_scalar_prefetch=1, grid=(nb,),
            in_specs=[g_spec, r_spec, pl.BlockSpec((3,) + blk, lambda i, pos: (0, i, 0))],
            out_specs=o_spec),
        compiler_params=_params(("parallel",)),
    )(pos, grad, received, parts)


def _join_job(shards, axes):
    n = len(shards)

    def copy(refs, sems, w, other):
        x, y, c, _ = _mesh_position()
        place = _region(refs[w], axes[w], None, 1 - c if other else c, 0, shards[w].shape[1 - axes[w]] // 2)
        return pltpu.make_async_remote_copy(
            src_ref=place, dst_ref=place, send_sem=sems[0].at[w], recv_sem=sems[1].at[w],
            device_id=(x, y, 1 - c), device_id_type=MESH)

    def start(refs, sems):
        for w in range(n):
            copy(refs, sems, w, False).start()

    def finish(refs, sems):
        for w in range(n):
            copy(refs, sems, w, True).wait_recv()
            copy(refs, sems, w, False).wait_send()

    return _Job(ios=shards, sems=[pltpu.SemaphoreType.DMA((n,))] * 2, start=start, finish=finish)


def _all_reduce_small(vec, after=()):
    rows, cols = vec.shape

    def body(v_ref, *rest):
        o_ref, land_ref, send_sem, recv_sem = rest[len(after):]
        x, y, c, _ = _mesh_position()
        me = 4 * x + 2 * y + c
        land_ref[me] = v_ref[...]
        copies = []
        for k in range(1, 8):
            px, py, pc = x ^ (k >> 2), y ^ ((k >> 1) & 1), c ^ (k & 1)
            copies.append(pltpu.make_async_remote_copy(
                src_ref=v_ref, dst_ref=land_ref.at[me], send_sem=send_sem.at[k], recv_sem=recv_sem.at[k],
                device_id=(px, py, pc), device_id_type=MESH))
        for cp in copies:
            cp.start()
        for k in range(1, 8):
            peer = me ^ k
            pltpu.make_async_remote_copy(
                src_ref=v_ref, dst_ref=land_ref.at[peer], send_sem=send_sem.at[k], recv_sem=recv_sem.at[k],
                device_id=(x, y, c), device_id_type=MESH).wait_recv()
        for cp in copies:
            cp.wait_send()
        total = land_ref[0]
        for k in range(1, 8):
            total = total + land_ref[k]
        o_ref[...] = total

    vmem = pl.BlockSpec(memory_space=pltpu.VMEM)
    return pl.pallas_call(
        body, name="all_reduce_small", in_specs=[vmem] + [pl.BlockSpec(memory_space=pl.ANY)] * len(after),
        out_specs=vmem, out_shape=jax.ShapeDtypeStruct((rows, cols), F32),
        scratch_shapes=[pltpu.VMEM((8, rows, cols), F32), pltpu.SemaphoreType.DMA((8,)), pltpu.SemaphoreType.DMA((8,))],
    )(vec, *after)


def _adamw(w, g, m, v, *, name, after=()):
    rows, cols = w.shape
    tr = _row_block(rows, cols) if rows % 8 == 0 else rows
    bc1 = 1.0 - ADAM_B1 ** ADAM_STEP
    bc2 = 1.0 - ADAM_B2 ** ADAM_STEP

    def body(w_ref, g_ref, m_ref, v_ref, *rest):
        go_ref, d_ref, mo_ref, vo_ref = rest[len(after):]
        gv = g_ref[...]
        go_ref[...] = gv
        mn = ADAM_B1 * m_ref[...] + (1.0 - ADAM_B1) * gv
        vn = ADAM_B2 * v_ref[...] + (1.0 - ADAM_B2) * (gv * gv)
        mo_ref[...] = mn
        vo_ref[...] = vn
        d_ref[...] = -ADAM_LR * ((mn / bc1) / (jnp.sqrt(vn / bc2) + ADAM_EPS) + ADAM_WD * w_ref[...])

    blk = pl.BlockSpec((tr, cols), lambda i: (i, 0))
    shape = jax.ShapeDtypeStruct((rows, cols), F32)
    return pl.pallas_call(
        body, name=name, grid=(rows // tr,), in_specs=[blk] * 4 + [pl.BlockSpec(memory_space=pl.ANY)] * len(after),
        out_specs=[blk] * 4, out_shape=[shape] * 4, compiler_params=_params(("parallel",)),
    )(w, g, m, v, *after)


def _to_bf16_in_place(w, axis, pos, *, name, after=None):
    rows, cols = w.shape
    tr = _row_block(rows, cols)
    nb = rows // tr

    def body(pos_ref, w_ref, *rest):
        rest[-1][...] = w_ref[...].astype(BF16)

    if axis == 1:
        o_spec = pl.BlockSpec((tr, cols), lambda i, pos: (i, pos[1]))
        full_shape = (rows, N_CHIPS * cols)
    else:
        o_spec = pl.BlockSpec((tr, cols), lambda i, pos: (pos[1] * nb + i, 0))
        full_shape = (N_CHIPS * rows, cols)
    in_specs = [pl.BlockSpec((tr, cols), lambda i, pos: (i, 0))]
    operands = [pos, w]
    if after is not None:
        in_specs.append(pl.BlockSpec(after.shape, lambda i, pos: (0, 0)))
        operands.append(after)
    return pl.pallas_call(
        body, name=name, out_shape=jax.ShapeDtypeStruct(full_shape, BF16),
        grid_spec=pltpu.PrefetchScalarGridSpec(num_scalar_prefetch=1, grid=(nb,), in_specs=in_specs, out_specs=o_spec),
        compiler_params=_params(("parallel",)),
    )(*operands)


def _split_gather_start(full, axis):
    rows_cols = full.shape

    def body(buf_ref, *rest):
        sems = rest[:6]
        token_ref = rest[7]
        x, y, c, chips = _mesh_position()
        place = _region(buf_ref, axis, 2 * x + y, c, rows_cols[axis] // N_CHIPS, rows_cols[1 - axis] // 2)
        for k in range(3):
            pltpu.make_async_remote_copy(
                src_ref=place, dst_ref=place, send_sem=sems[k], recv_sem=sems[3 + k],
                device_id=(chips[k][0], chips[k][1], c), device_id_type=MESH).start()
        token_ref[...] = jnp.zeros_like(token_ref)

    hbm = pl.BlockSpec(memory_space=pltpu.HBM)
    sem = pl.BlockSpec(memory_space=pltpu.SEMAPHORE)
    res = pl.pallas_call(
        body, name="all_gather_w_in_start",
        out_shape=(*[pltpu.SemaphoreType.DMA(())] * 6, pltpu.HBM(full.shape, full.dtype),
                   jax.ShapeDtypeStruct((8, HEAD_DIM), F32)),
        in_specs=(hbm,), out_specs=(*[sem] * 6, hbm, pl.BlockSpec(memory_space=pltpu.VMEM)),
        input_output_aliases={0: 6},
        compiler_params=pltpu.CompilerParams(has_side_effects=pltpu.SideEffectType.DATAFLOW_SIDE_EFFECTING),
    )(pltpu.with_memory_space_constraint(full, pltpu.HBM))
    return list(res[:6]), res[6], res[7]


def _split_gather_wait(sems, full, axis, after):
    rows_cols = full.shape

    def body(buf_ref, *rest):
        sem_refs = rest[:6]
        x, y, c, chips = _mesh_position()

        def copy(k, shard):
            place = _region(buf_ref, axis, shard, c, rows_cols[axis] // N_CHIPS, rows_cols[1 - axis] // 2)
            return pltpu.make_async_remote_copy(
                src_ref=place, dst_ref=place, send_sem=sem_refs[k], recv_sem=sem_refs[3 + k],
                device_id=(chips[k][0], chips[k][1], c), device_id_type=MESH)

        for k in range(3):
            copy(k, 2 * x + y).wait_send()
            copy(k, 2 * chips[k][0] + chips[k][1]).wait_recv()

    hbm = pl.BlockSpec(memory_space=pltpu.HBM)
    sem = pl.BlockSpec(memory_space=pltpu.SEMAPHORE)
    return pl.pallas_call(
        body, name="all_gather_w_in_wait", out_shape=pltpu.HBM(full.shape, full.dtype),
        in_specs=(hbm, *[sem] * 6, *[pl.BlockSpec(memory_space=pl.ANY)] * len(after)), out_specs=hbm,
        input_output_aliases={0: 0},
        compiler_params=pltpu.CompilerParams(has_side_effects=pltpu.SideEffectType.DATAFLOW_SIDE_EFFECTING),
    )(full, *sems, *after)


BIG = ("w_in", "w_out", "w_gate", "w_up", "w_down")
BIG_AXIS = dict(w_in=1, w_out=0, w_gate=1, w_up=1, w_down=0)
SMALL = ("norm_mix_w", "ret_decay_fwd", "ret_decay_bwd", "ret_norm_w", "norm_ffn_w", "norm_final_w")
ALL_WEIGHTS = ("norm_mix_w", "w_in", "ret_decay_fwd", "ret_decay_bwd", "ret_norm_w", "w_out", "norm_ffn_w",
               "w_gate", "w_up", "w_down", "norm_final_w")
SMALL_ROW = 128 * 8


def _pack_small(small):
    pieces = [jnp.reshape(small["loss"], (1,))] + [jnp.reshape(small[k], (-1,)) for k in SMALL]
    rows = []
    for p in pieces:
        pad = -p.shape[0] % (8 * SMALL_ROW)
        rows.append(jnp.reshape(jnp.pad(p, (0, pad)), (-1, SMALL_ROW)))
    return jnp.concatenate(rows, axis=0)


def _unpack_small(block, like):
    out, row = {}, 0
    for k in ("loss",) + SMALL:
        size = 1 if k == "loss" else like[k].size
        nrows = -(-size // (8 * SMALL_ROW)) * 8
        out[k] = jnp.reshape(block[row:row + nrows], (-1,))[:size]
        row += nrows
    return out


def kernel(x, norm_mix_w, w_in, ret_decay_fwd, ret_decay_bwd, ret_norm_w, w_out, norm_ffn_w, w_gate, w_up, w_down, norm_final_w, loss_target, m_norm_mix_w, m_w_in, m_ret_decay_fwd, m_ret_decay_bwd, m_ret_norm_w, m_w_out, m_norm_ffn_w, m_w_gate, m_w_up, m_w_down, m_norm_final_w, v_norm_mix_w, v_w_in, v_ret_decay_fwd, v_ret_decay_bwd, v_ret_norm_w, v_w_out, v_norm_ffn_w, v_w_gate, v_w_up, v_w_down, v_norm_final_w):
    weights = dict(norm_mix_w=norm_mix_w, w_in=w_in, ret_decay_fwd=ret_decay_fwd, ret_decay_bwd=ret_decay_bwd,
                   ret_norm_w=ret_norm_w, w_out=w_out, norm_ffn_w=norm_ffn_w, w_gate=w_gate, w_up=w_up,
                   w_down=w_down, norm_final_w=norm_final_w)
    m_in = dict(norm_mix_w=m_norm_mix_w, w_in=m_w_in, ret_decay_fwd=m_ret_decay_fwd, ret_decay_bwd=m_ret_decay_bwd,
                ret_norm_w=m_ret_norm_w, w_out=m_w_out, norm_ffn_w=m_norm_ffn_w, w_gate=m_w_gate, w_up=m_w_up,
                w_down=m_w_down, norm_final_w=m_norm_final_w)
    v_in = dict(norm_mix_w=v_norm_mix_w, w_in=v_w_in, ret_decay_fwd=v_ret_decay_fwd, ret_decay_bwd=v_ret_decay_bwd,
                ret_norm_w=v_ret_norm_w, w_out=v_w_out, norm_ffn_w=v_norm_ffn_w, w_gate=v_w_gate, w_up=v_w_up,
                w_down=v_w_down, norm_final_w=v_norm_final_w)
    pos = jnp.stack([lax.axis_index("c"), 2 * lax.axis_index("x") + lax.axis_index("y")]).astype(jnp.int32)

    own = {"w_in": _to_bf16_in_place(weights["w_in"][0], BIG_AXIS["w_in"], pos, name="cast_w_in")}
    w_in_started = _split_gather_start(own["w_in"], BIG_AXIS["w_in"])
    for k in BIG[1:]:
        own[k] = _to_bf16_in_place(weights[k][0], BIG_AXIS[k], pos, name="cast_" + k, after=w_in_started[2])

    dx, grad_w, small, w_in_pending = _step(
        x[0], loss_target[0], norm_mix_w, ret_decay_fwd[0], ret_decay_bwd[0], ret_norm_w, norm_ffn_w,
        norm_final_w[None, :], own, w_in_started, pos)

    delta, new_m, new_v = {}, {}, {}

    def update(k, after):
        shape = weights[k].shape
        as2d = (lambda t: jnp.reshape(t, (-1, shape[-1])))
        grad_w[k], delta[k], new_m[k], new_v[k] = (jnp.reshape(t, shape) for t in _adamw(
            as2d(weights[k]), as2d(grad_w[k]), as2d(m_in[k]), as2d(v_in[k]), name="adamw_" + k, after=after))

    others = [k for k in BIG if k != "w_in"]
    for k in others:
        update(k, [w_in_pending["sending"]["token"]])
    _, parts = _split_wait(w_in_pending["sending"], [delta[k] for k in others], name="grad_send_w_in_wait")

    like = {k: weights[k] for k in SMALL}
    reduced = _unpack_small(_all_reduce_small(_pack_small(small), after=[parts]), like)
    loss = reduced["loss"][0]
    for k in SMALL:
        grad_w[k] = jnp.reshape(reduced[k], (1, -1))
        update(k, [])

    half = _sum_chip_parts(w_in_pending["grad"], w_in_pending["received"], parts, BIG_AXIS["w_in"], pos,
                           name="grad_sum_parts_w_in")
    (grad_w["w_in"],) = _run_jobs([_join_job([half], [BIG_AXIS["w_in"]])], name="grad_join_w_in")
    update("w_in", [])

    return (loss, dx[None], *[grad_w[k] for k in ALL_WEIGHTS], *[delta[k] for k in ALL_WEIGHTS],
            *[new_m[k] for k in ALL_WEIGHTS], *[new_v[k] for k in ALL_WEIGHTS])
```

```python
import functools
import math

import numpy as np
import jax
import jax.numpy as jnp
from jax import lax
from jax.experimental import pallas as pl
from jax.experimental.pallas import tpu as pltpu

F32 = jnp.float32
BF16 = jnp.bfloat16
MESH = pl.DeviceIdType.MESH

HEAD_DIM = 128
RET_CHUNK = 128
RET_UNROLL = 8
EPS = 1e-6
DILATED_PATTERNS = ((128, 1), (512, 4), (2048, 16))
ATT_BLOCK = 256
ATT_REACH = max(w // 2 for w, _ in DILATED_PATTERNS)
ATT_NEAR = ATT_BLOCK
ATT_CLASSES = DILATED_PATTERNS[-1][1]
assert all(w // 2 <= ATT_NEAR for w, _ in DILATED_PATTERNS[:-1])
ATT_KB = -(-ATT_NEAR // ATT_BLOCK)
ATT_WINDOW = 2 * ATT_KB + 1
ATT_FAR_GROUP = 8
ATT_NEAR_GROUP = 4
MASKED = -1e30
ROW_MAX_INIT = -1e29
N_CHIPS = 4
VMEM_LIMIT_BYTES = 56 * 1024 * 1024
ELEM_BLOCK_BYTES = 2 * 1024 * 1024

ADAM_LR = 0.001
ADAM_B1 = 0.9
ADAM_B2 = 0.999
ADAM_EPS = 1e-08
ADAM_WD = 0.01
ADAM_STEP = 10


def _params(sem=None):
    return pltpu.CompilerParams(dimension_semantics=sem, vmem_limit_bytes=VMEM_LIMIT_BYTES)


def _sigmoid(x):
    return 0.5 * jnp.tanh(0.5 * x) + 0.5


class _Job:
    def __init__(self, *, ins=(), ios=(), outs=(), sems=(), start, finish):
        self.ins, self.ios, self.outs, self.sems = list(ins), list(ios), list(outs), list(sems)
        self.start, self.finish = start, finish

    def results(self):
        return [jax.ShapeDtypeStruct(a.shape, a.dtype) for a in self.ios] + self.outs


def _call(body, *, name, grid, in_specs, out_specs, out_shape, operands, scratch_shapes=(), semantics=None, jobs=(),
          after=()):
    in_specs, out_specs, out_shape = list(in_specs), list(out_specs), list(out_shape)
    scratch_shapes = list(scratch_shapes)
    if not jobs:
        n_real = len(in_specs)

        def ordered(*refs):
            body(*refs[:n_real], *refs[n_real + len(after):])

        outs = pl.pallas_call(
            ordered if after else body, name=name, grid=grid,
            in_specs=in_specs + [pl.BlockSpec(memory_space=pl.ANY)] * len(after), out_specs=out_specs,
            out_shape=out_shape, scratch_shapes=scratch_shapes, compiler_params=_params(semantics))(*operands, *after)
        return outs, []
    n_in, n_out, n_scratch = len(in_specs), len(out_specs), len(scratch_shapes)
    extra_in, extra_out, sems, aliases = [], [], [], {}
    for job in jobs:
        extra_in += job.ins
        for t in range(len(job.ios)):
            aliases[n_in + len(extra_in) + t] = n_out + len(extra_out) + t
        extra_in += job.ios
        extra_out += job.results()
        sems += job.sems

    def carried(*refs):
        x_in = refs[n_in:n_in + len(extra_in)]
        x_out = refs[n_in + len(extra_in) + n_out:n_in + len(extra_in) + n_out + len(extra_out)]
        x_sem = refs[len(refs) - len(sems):]
        views, i_in, i_out, i_sem = [], 0, 0, 0
        for job in jobs:
            data = list(x_in[i_in:i_in + len(job.ins)]) + list(x_out[i_out:i_out + len(job.results())])
            views.append((data, x_sem[i_sem:i_sem + len(job.sems)]))
            i_in += len(job.ins) + len(job.ios)
            i_out += len(job.results())
            i_sem += len(job.sems)
        steps = [pl.program_id(d) for d in range(len(grid))]

        @pl.when(functools.reduce(jnp.logical_and, [s == 0 for s in steps]))
        def _():
            for job, (data, sem) in zip(jobs, views):
                job.start(data, sem)

        body(*refs[:n_in], *refs[n_in + len(extra_in):n_in + len(extra_in) + n_out],
             *refs[len(refs) - len(sems) - n_scratch:len(refs) - len(sems)])

        @pl.when(functools.reduce(jnp.logical_and, [s == g - 1 for s, g in zip(steps, grid)]))
        def _():
            for job, (data, sem) in zip(jobs, views):
                job.finish(data, sem)

    hbm = pl.BlockSpec(memory_space=pl.ANY)
    res = pl.pallas_call(
        carried, name=name, grid=grid, in_specs=in_specs + [hbm] * len(extra_in),
        out_specs=out_specs + [hbm] * len(extra_out), out_shape=out_shape + extra_out,
        input_output_aliases=aliases, scratch_shapes=scratch_shapes + sems,
        compiler_params=_params(("arbitrary",) * len(grid)),
    )(*operands, *extra_in)
    carried_results, at = [], n_out
    for job in jobs:
        carried_results.append(list(res[at:at + len(job.results())]))
        at += len(job.results())
    return list(res[:n_out]), carried_results


def _run_jobs(jobs, *, name):
    first = jobs[0]
    n_in, n_io = len(first.ins), len(first.ios)
    out_shape = first.results()
    n_sems = [len(job.sems) for job in jobs]

    def body(*refs):
        data = list(refs[:n_in]) + list(refs[n_in + n_io:n_in + n_io + len(out_shape)])
        at = n_in + n_io + len(out_shape)
        for job, ns in zip(jobs, n_sems):
            job.start(data, refs[at:at + ns])
            job.finish(data, refs[at:at + ns])
            at += ns

    hbm = pl.BlockSpec(memory_space=pl.ANY)
    return pl.pallas_call(
        body, name=name, in_specs=[hbm] * (n_in + n_io), out_specs=[hbm] * len(out_shape), out_shape=out_shape,
        input_output_aliases={n_in + t: t for t in range(n_io)},
        scratch_shapes=[s for job in jobs for s in job.sems],
    )(*first.ins, *first.ios)


class _SemaphoreGrid:
    def __init__(self, refs, shape):
        self.refs, self.shape = list(refs), tuple(shape)

    @property
    def at(self):
        return self

    def __getitem__(self, index):
        index = index if isinstance(index, tuple) else (index,)
        flat = 0
        for i, extent in zip(index, self.shape):
            flat = flat * extent + i
        return self.refs[flat]


def _semaphore_grids(job, refs):
    grids, at = [], 0
    for sem in job.sems:
        count = math.prod(sem.shape)
        grids.append(_SemaphoreGrid(refs[at:at + count], sem.shape))
        at += count
    return grids


def _split_start(job, *, name):
    arrays = job.ins + job.ios + [lax.empty(s.shape, s.dtype) for s in job.outs]
    n, ns = len(arrays), sum(math.prod(sem.shape) for sem in job.sems)

    def body(*refs):
        job.start(list(refs[:n]), _semaphore_grids(job, refs[n:n + ns]))
        refs[-1][...] = jnp.zeros_like(refs[-1])

    hbm = pl.BlockSpec(memory_space=pltpu.HBM)
    res = pl.pallas_call(
        body, name=name,
        out_shape=(*[pltpu.SemaphoreType.DMA(())] * ns, *[pltpu.HBM(a.shape, a.dtype) for a in arrays],
                   jax.ShapeDtypeStruct((8, HEAD_DIM), F32)),
        in_specs=[hbm] * n,
        out_specs=(*[pl.BlockSpec(memory_space=pltpu.SEMAPHORE)] * ns, *[hbm] * n,
                   pl.BlockSpec(memory_space=pltpu.VMEM)),
        input_output_aliases={t: ns + t for t in range(n)},
        compiler_params=pltpu.CompilerParams(has_side_effects=pltpu.SideEffectType.DATAFLOW_SIDE_EFFECTING),
    )(*[pltpu.with_memory_space_constraint(a, pltpu.HBM) for a in arrays])
    return dict(job=job, sems=list(res[:ns]), arrays=list(res[ns:ns + n]), token=res[-1])


def _split_wait(started, after, *, name):
    job, arrays, sems = started["job"], started["arrays"], started["sems"]
    n, ns = len(arrays), len(sems)

    def body(*refs):
        job.finish(list(refs[:n]), _semaphore_grids(job, refs[n:n + ns]))

    hbm = pl.BlockSpec(memory_space=pltpu.HBM)
    return pl.pallas_call(
        body, name=name, out_shape=[pltpu.HBM(a.shape, a.dtype) for a in arrays],
        in_specs=[hbm] * n + [pl.BlockSpec(memory_space=pltpu.SEMAPHORE)] * ns
        + [pl.BlockSpec(memory_space=pl.ANY)] * len(after),
        out_specs=[hbm] * n, input_output_aliases={t: t for t in range(n)},
        compiler_params=pltpu.CompilerParams(has_side_effects=pltpu.SideEffectType.DATAFLOW_SIDE_EFFECTING),
    )(*arrays, *sems, *after)


def _dot(a, b, ta=False, tb=False):
    return lax.dot_general(a, b, (((0 if ta else 1,), (1 if tb else 0,)), ((), ())),
                           preferred_element_type=F32)


def _tile(n, want):
    t = min(n, want) // 128 * 128
    while n % t:
        t -= 128
    return t


def _a_spec(ta, tm, tk):
    return pl.BlockSpec((tk, tm), lambda i, j, k: (k, i)) if ta else pl.BlockSpec((tm, tk), lambda i, j, k: (i, k))


def _b_spec(tb, tk, tn):
    return pl.BlockSpec((tn, tk), lambda i, j, k: (j, k)) if tb else pl.BlockSpec((tk, tn), lambda i, j, k: (k, j))


def _accumulate(accs, nk, products, finish):
    if nk == 1:
        finish(*products())
        return
    k = pl.program_id(2)

    @pl.when(k == 0)
    def _():
        for acc, p in zip(accs, products()):
            acc[...] = p

    if nk > 2:
        @pl.when(jnp.logical_and(k > 0, k < nk - 1))
        def _():
            for acc, p in zip(accs, products()):
                acc[...] += p

    @pl.when(k == nk - 1)
    def _():
        finish(*[acc[...] + p for acc, p in zip(accs, products())])


def _matmul(a, b, *, name, ta=False, tb=False, out_dtype=F32, residual=None, tm=1024, tn=1024, tk=2048, jobs=()):
    m, kdim = (a.shape[1], a.shape[0]) if ta else a.shape
    n = b.shape[0] if tb else b.shape[1]
    tm, tn, tk = _tile(m, tm), _tile(n, tn), _tile(kdim, tk)
    nk = kdim // tk

    def body(*refs):
        a_ref, b_ref = refs[:2]
        r_ref = refs[2] if residual is not None else None
        o_ref = refs[-1] if nk == 1 else refs[-2]

        def finish(total):
            if residual is not None:
                total = total + r_ref[...]
            o_ref[...] = total.astype(out_dtype)

        _accumulate(refs[-1:] if nk > 1 else (), nk, lambda: (_dot(a_ref[...], b_ref[...], ta, tb),), finish)

    o_spec = pl.BlockSpec((tm, tn), lambda i, j, k: (i, j))
    in_specs = [_a_spec(ta, tm, tk), _b_spec(tb, tk, tn)]
    operands = [a, b]
    if residual is not None:
        in_specs.append(o_spec)
        operands.append(residual)
    (out,), carried = _call(
        body, name=name, grid=(m // tm, n // tn, nk), in_specs=in_specs, out_specs=[o_spec],
        out_shape=[jax.ShapeDtypeStruct((m, n), out_dtype)], operands=operands,
        scratch_shapes=[pltpu.VMEM((tm, tn), F32)] * (nk > 1),
        semantics=("parallel", "parallel", "arbitrary"), jobs=jobs)
    return (out, carried) if jobs else out


def _matmul_pieces_nt(pieces, b, *, name, tm=512, tn=1024, jobs=(), after=()):
    m, kp = pieces[0].shape
    n = b.shape[0]
    tm, tn = _tile(m, tm), _tile(n, tn)
    count = len(pieces)

    def body(*refs):
        b_ref, o_ref = refs[count], refs[count + 1]
        total = _dot(refs[0][...], b_ref[:, pl.ds(0, kp)], tb=True)
        for p in range(1, count):
            total = total + _dot(refs[p][...], b_ref[:, pl.ds(p * kp, kp)], tb=True)
        o_ref[...] = total

    piece = pl.BlockSpec((tm, kp), lambda j, i: (i, 0))
    (out,), carried = _call(
        body, name=name, grid=(n // tn, m // tm),
        in_specs=[piece] * count + [pl.BlockSpec((tn, count * kp), lambda j, i: (j, 0))],
        out_specs=[pl.BlockSpec((tm, tn), lambda j, i: (i, j))],
        out_shape=[jax.ShapeDtypeStruct((m, n), F32)], operands=[*pieces, b],
        semantics=("parallel", "parallel"), jobs=jobs, after=after)
    return (out, carried) if jobs else out


def _weight_grad_pieces(a, pieces, *, name):
    tokens, m = a.shape
    np_ = pieces[0].shape[1]
    tm = 1024 if m % 1024 == 0 else _tile(m, 1408)
    tn = _tile(np_, 512)
    nb = np_ // tn
    out = None
    for p, piece in enumerate(pieces):
        def body(*refs):
            refs[-1][...] = _dot(refs[0][...], refs[1][...], ta=True)

        in_specs = [pl.BlockSpec((tokens, tm), lambda i, j: (0, i)), pl.BlockSpec((tokens, tn), lambda i, j: (0, j))]
        operands = [a, piece]
        if out is not None:
            in_specs.append(pl.BlockSpec(memory_space=pl.ANY))
            operands.append(out)
        out = pl.pallas_call(
            body, name="%s_%d" % (name, p), grid=(m // tm, nb), in_specs=in_specs,
            out_specs=pl.BlockSpec((tm, tn), lambda i, j, p=p: (i, p * nb + j)),
            out_shape=jax.ShapeDtypeStruct((m, len(pieces) * np_), F32),
            input_output_aliases={2: 0} if len(operands) == 3 else {},
            compiler_params=_params(("parallel", "parallel")),
        )(*operands)
    return out


def _weight_grad(a, g, *, name, jobs=()):
    tokens, m = a.shape
    tm = 1024 if m % 1024 == 0 else _tile(m, 1408)
    return _matmul(a, g, name=name, ta=True, tm=tm, tn=512, tk=tokens, jobs=jobs)


def _swiglu_fwd(n2, w_gate, w_up, *, tm=1024, tn=512, tk=2048, jobs=()):
    m, kdim = n2.shape
    n = w_gate.shape[1]
    tm, tn, tk = _tile(m, tm), _tile(n, tn), _tile(kdim, tk)
    nk = kdim // tk

    def body(a_ref, g_ref, u_ref, gate_ref, up_ref, act_ref, *acc):
        def products():
            a = a_ref[...]
            return _dot(a, g_ref[...]), _dot(a, u_ref[...])

        def finish(g, u):
            gate_ref[...] = g.astype(BF16)
            up_ref[...] = u.astype(BF16)
            act_ref[...] = (g * _sigmoid(g) * u).astype(BF16)

        _accumulate(acc, nk, products, finish)

    o_spec = pl.BlockSpec((tm, tn), lambda i, j, k: (i, j))
    o_shape = jax.ShapeDtypeStruct((m, n), BF16)
    return _call(
        body, name="swiglu_fwd", grid=(m // tm, n // tn, nk),
        in_specs=[_a_spec(False, tm, tk), _b_spec(False, tk, tn), _b_spec(False, tk, tn)],
        out_specs=[o_spec] * 3, out_shape=[o_shape] * 3, operands=[n2, w_gate, w_up],
        scratch_shapes=[pltpu.VMEM((tm, tn), F32)] * (2 * (nk > 1)),
        semantics=("parallel", "parallel", "arbitrary"), jobs=jobs)


def _swiglu_bwd_act(dh2, w_down, gate, up, *, tm=1024, tn=512, tk=2048):
    m, kdim = dh2.shape
    n = w_down.shape[0]
    tm, tn, tk = _tile(m, tm), _tile(n, tn), _tile(kdim, tk)
    nk = kdim // tk

    sub = _tile(tn, 256)

    def body(a_ref, b_ref, gate_ref, up_ref, dgate_ref, dup_ref, *acc):
        def finish(dact, cols=slice(None)):
            g = gate_ref[:, cols].astype(F32)
            u = up_ref[:, cols].astype(F32)
            sg = _sigmoid(g)
            dup_ref[:, cols] = (dact * g * sg).astype(BF16)
            dgate_ref[:, cols] = (dact * u * sg * (1.0 + g * (1.0 - sg))).astype(BF16)

        if nk == 1:
            a = a_ref[...]
            for c in range(tn // sub):
                cols = pl.ds(c * sub, sub)
                finish(_dot(a, b_ref[cols, :], tb=True), cols)
        else:
            _accumulate(acc, nk, lambda: (_dot(a_ref[...], b_ref[...], tb=True),), finish)

    o_spec = pl.BlockSpec((tm, tn), lambda i, j, k: (i, j))
    o_shape = jax.ShapeDtypeStruct((m, n), BF16)
    return pl.pallas_call(
        body, name="swiglu_bwd_act", grid=(m // tm, n // tn, nk),
        in_specs=[_a_spec(False, tm, tk), _b_spec(True, tk, tn), o_spec, o_spec],
        out_specs=[o_spec] * 2, out_shape=[o_shape] * 2,
        scratch_shapes=[pltpu.VMEM((tm, tn), F32)] * (nk > 1),
        compiler_params=_params(("parallel", "parallel", "arbitrary")),
    )(dh2, w_down, gate, up)


def _swiglu_bwd_in(dgate, dup, w_gate, w_up, *, tm=1024, tn=1024, tk=1408, jobs=()):
    m, kdim = dgate.shape
    n = w_gate.shape[0]
    tm, tn, tk = _tile(m, tm), _tile(n, tn), _tile(kdim, tk)
    nk = kdim // tk

    def body(a1_ref, a2_ref, b1_ref, b2_ref, o_ref, *acc):
        def product():
            return (_dot(a1_ref[...], b1_ref[...], tb=True) + _dot(a2_ref[...], b2_ref[...], tb=True),)

        def finish(total):
            o_ref[...] = total

        _accumulate(acc, nk, product, finish)

    a_spec, b_spec = _a_spec(False, tm, tk), _b_spec(True, tk, tn)
    (out,), carried = _call(
        body, name="swiglu_bwd_in", grid=(m // tm, n // tn, nk),
        in_specs=[a_spec, a_spec, b_spec, b_spec],
        out_specs=[pl.BlockSpec((tm, tn), lambda i, j, k: (i, j))],
        out_shape=[jax.ShapeDtypeStruct((m, n), F32)], operands=[dgate, dup, w_gate, w_up],
        scratch_shapes=[pltpu.VMEM((tm, tn), F32)] * (nk > 1),
        semantics=("parallel", "parallel", "arbitrary"), jobs=jobs)
    return out, carried


def _row_block(rows, cols):
    tr = min(rows, max(16, ELEM_BLOCK_BYTES // (4 * cols) // 16 * 16))
    while rows % tr:
        tr -= 16
    return tr


def _rmsnorm_fwd(x, g, *, name, after=None):
    s, d = x.shape
    tr = _row_block(s, d)

    def body(x_ref, g_ref, *rest):
        xv = x_ref[...]
        r = lax.rsqrt(jnp.mean(xv * xv, axis=-1, keepdims=True) + EPS)
        rest[-1][...] = (xv * r * g_ref[...]).astype(BF16)

    row = pl.BlockSpec((tr, d), lambda i: (i, 0))
    in_specs = [row, pl.BlockSpec((1, d), lambda i: (0, 0))]
    operands = [x, g]
    if after is not None:
        in_specs.append(pl.BlockSpec(after.shape, lambda i: (0, 0)))
        operands.append(after)
    return pl.pallas_call(
        body, name=name, grid=(s // tr,), in_specs=in_specs,
        out_specs=row, out_shape=jax.ShapeDtypeStruct((s, d), BF16),
        compiler_params=_params(("parallel",)),
    )(*operands)


def _rmsnorm_bwd_rows(xv, gv, dy):
    r = lax.rsqrt(jnp.mean(xv * xv, axis=-1, keepdims=True) + EPS)
    xhat = xv * r
    dxh = dy * gv
    dx = r * (dxh - xhat * jnp.mean(dxh * xhat, axis=-1, keepdims=True))
    return dx, dy * xhat


def _rmsnorm_bwd(dn, x, g, skip, *, name, after=()):
    s, d = x.shape
    tr = _row_block(s, d)

    def body(dn_ref, x_ref, g_ref, skip_ref, *rest):
        dx_ref, dxb_ref, dg_ref = rest[len(after):]
        dx, dgr = _rmsnorm_bwd_rows(x_ref[...], g_ref[...], dn_ref[...])
        dx = dx + skip_ref[...]
        dx_ref[...] = dx
        dxb_ref[...] = dx.astype(BF16)

        @pl.when(pl.program_id(0) == 0)
        def _():
            dg_ref[...] = jnp.zeros_like(dg_ref)

        dg_ref[...] += jnp.sum(dgr, axis=0, keepdims=True)

    row = pl.BlockSpec((tr, d), lambda i: (i, 0))
    vec = pl.BlockSpec((1, d), lambda i: (0, 0))
    return pl.pallas_call(
        body, name=name, grid=(s // tr,),
        in_specs=[row, row, vec, row] + [pl.BlockSpec(memory_space=pl.ANY)] * len(after),
        out_specs=[row, row, vec],
        out_shape=[jax.ShapeDtypeStruct((s, d), F32), jax.ShapeDtypeStruct((s, d), BF16),
                   jax.ShapeDtypeStruct((1, d), F32)],
        compiler_params=_params(("arbitrary",)),
    )(dn, x, g, skip, *after)


def _loss_head(h2, g, target):
    s, d = h2.shape
    tr = _row_block(s, d)

    def body(h_ref, g_ref, t_ref, dh_ref, dhb_ref, dg_ref, loss_ref):
        hv = h_ref[...]
        gv = g_ref[...]
        r = lax.rsqrt(jnp.mean(hv * hv, axis=-1, keepdims=True) + EPS)
        err = hv * r * gv - t_ref[...]
        dx, dgr = _rmsnorm_bwd_rows(hv, gv, err * (1.0 / d))
        dh_ref[...] = dx
        dhb_ref[...] = dx.astype(BF16)

        @pl.when(pl.program_id(0) == 0)
        def _():
            dg_ref[...] = jnp.zeros_like(dg_ref)
            loss_ref[...] = jnp.zeros_like(loss_ref)

        dg_ref[...] += jnp.sum(dgr, axis=0, keepdims=True)
        row_loss = jnp.mean(err * err, axis=-1, keepdims=True)
        loss_ref[...] += 0.5 * jnp.sum(row_loss, axis=0, keepdims=True)

    row = pl.BlockSpec((tr, d), lambda i: (i, 0))
    vec = pl.BlockSpec((1, d), lambda i: (0, 0))
    one = pl.BlockSpec((1, 1), lambda i: (0, 0))
    return pl.pallas_call(
        body, name="loss_head", grid=(s // tr,), in_specs=[row, vec, row],
        out_specs=[row, row, vec, one],
        out_shape=[jax.ShapeDtypeStruct((s, d), F32), jax.ShapeDtypeStruct((s, d), BF16),
                   jax.ShapeDtypeStruct((1, d), F32), jax.ShapeDtypeStruct((1, 1), F32)],
        compiler_params=_params(("arbitrary",)),
    )(h2, g, target)


def _attention_bias_tables():
    k = np.arange(-ATT_KB, ATT_KB + 1)[:, None, None]
    delta = k * ATT_BLOCK + np.arange(ATT_BLOCK)[None, None, :] - np.arange(ATT_BLOCK)[None, :, None]
    dist = np.abs(delta)
    count = np.zeros(delta.shape, np.int32)
    for window, dilation in DILATED_PATTERNS:
        count += (delta % dilation == 0) & (dist <= min(window // 2, ATT_NEAR))
    logc = np.where(count > 0, np.log(np.maximum(count, 1)), MASKED)
    return dist.astype(np.float32), logc.astype(np.float32)


def _far_bias_tables(per_class):
    steps = np.abs(np.arange(per_class)[:, None] - np.arange(per_class)[None, :]) * ATT_CLASSES
    valid = (steps > ATT_NEAR) & (steps <= ATT_REACH)
    return steps.astype(np.float32), np.where(valid, 0.0, MASKED).astype(np.float32)


def _to_classes(x):
    s, cols = x.shape
    return jnp.reshape(jnp.transpose(jnp.reshape(x, (s // ATT_CLASSES, ATT_CLASSES, cols)), (1, 0, 2)), (s, cols))


def _from_classes(x):
    s, cols = x.shape
    return jnp.reshape(jnp.transpose(jnp.reshape(x, (ATT_CLASSES, s // ATT_CLASSES, cols)), (1, 0, 2)), (s, cols))


def _head_bias(bias_ref, slope, dist_ref, logc_ref):
    for kk in range(ATT_WINDOW):
        bias_ref[kk] = logc_ref[kk] - slope * dist_ref[kk]
    bias_ref[ATT_WINDOW] = jnp.full((ATT_BLOCK, ATT_BLOCK), MASKED, F32)


def _window_start(i, nq, nwin):
    return jnp.clip(i - ATT_KB, 0, nq - nwin)


def _window_block(j, i):
    rows = pl.ds(pl.multiple_of(j * ATT_BLOCK, ATT_BLOCK), ATT_BLOCK)
    kk = j - i + ATT_KB
    return rows, jnp.where(jnp.logical_and(kk >= 0, kk < ATT_WINDOW), kk, ATT_WINDOW)


def _attention_far_fwd(qkv, slopes, n_heads, jobs=()):
    s = qkv.shape[0]
    per_class = s // ATT_CLASSES
    scale = HEAD_DIM ** -0.5
    dist, logc = _far_bias_tables(per_class)

    def body(slope_ref, q_ref, k_ref, v_ref, dist_ref, logc_ref, o_ref, lse_ref):
        bias = logc_ref[...] - slope_ref[pl.program_id(0)] * dist_ref[...]
        for a in range(ATT_FAR_GROUP):
            rows = pl.ds(a * per_class, per_class)
            sc = _dot(q_ref[rows, :], k_ref[rows, :], tb=True) * scale + bias
            m = jnp.maximum(jnp.max(sc, axis=-1, keepdims=True), ROW_MAX_INIT)
            p = jnp.exp(sc - m)
            l = jnp.maximum(jnp.sum(p, axis=-1, keepdims=True), 1e-30)
            o_ref[rows, :] = (_dot(p.astype(BF16), v_ref[rows, :]) / l).astype(BF16)
            lse_ref[rows, :] = jnp.broadcast_to(m + jnp.log(l), (per_class, HEAD_DIM))

    hh = n_heads
    blk = pl.BlockSpec((ATT_FAR_GROUP * per_class, HEAD_DIM), lambda h, r: (r, h))
    table = pl.BlockSpec(dist.shape, lambda h, r: (0, 0))
    return _call(
        body, name="attention_far_fwd", grid=(hh, ATT_CLASSES // ATT_FAR_GROUP),
        in_specs=[pl.BlockSpec(memory_space=pltpu.SMEM), blk,
                  pl.BlockSpec((ATT_FAR_GROUP * per_class, HEAD_DIM), lambda h, r: (r, hh + h)),
                  pl.BlockSpec((ATT_FAR_GROUP * per_class, HEAD_DIM), lambda h, r: (r, 2 * hh + h)), table, table],
        out_specs=[blk, blk],
        out_shape=[jax.ShapeDtypeStruct((s, hh * HEAD_DIM), BF16), jax.ShapeDtypeStruct((s, hh * HEAD_DIM), F32)],
        operands=[slopes, qkv, qkv, qkv, jnp.asarray(dist), jnp.asarray(logc)],
        semantics=("parallel", "parallel"), jobs=jobs)


def _attention_fwd(proj, slopes, far_out, far_lse, n_heads, jobs=()):
    s = proj.shape[0]
    nq = s // ATT_BLOCK
    scale = HEAD_DIM ** -0.5
    dist, logc = _attention_bias_tables()

    nwin = min(ATT_WINDOW, nq)

    group = math.gcd(ATT_NEAR_GROUP, nq)

    def body(slope_ref, q_ref, k_ref, v_ref, fo_ref, fl_ref, dist_ref, logc_ref, o_ref, lse_ref, bias_ref, s_ref):
        h, step = pl.program_id(0), pl.program_id(1)

        @pl.when(step == 0)
        def _():
            _head_bias(bias_ref, slope_ref[h], dist_ref, logc_ref)

        for a in range(group):
            i = step * group + a
            mine = pl.ds(a * ATT_BLOCK, ATT_BLOCK)
            q = q_ref[mine, :]
            first = _window_start(i, nq, nwin)
            m = jnp.full((ATT_BLOCK, 1), ROW_MAX_INIT, F32)
            for b in range(nwin):
                rows, kk = _window_block(first + b, i)
                sc = _dot(q, k_ref[rows, :], tb=True) * scale + bias_ref[kk]
                s_ref[a * nwin + b] = sc
                m = jnp.maximum(m, jnp.max(sc, axis=-1, keepdims=True))
            l = jnp.zeros((ATT_BLOCK, 1), F32)
            acc = jnp.zeros((ATT_BLOCK, HEAD_DIM), F32)
            for b in range(nwin):
                rows, _ = _window_block(first + b, i)
                p = jnp.exp(s_ref[a * nwin + b] - m)
                l = l + jnp.sum(p, axis=-1, keepdims=True)
                acc = acc + _dot(p.astype(BF16), v_ref[rows, :])
            near_lse = m + jnp.log(l)
            far_lse_col = fl_ref[mine, :1]
            lse = jnp.maximum(near_lse, far_lse_col)
            lse = lse + jnp.log(jnp.exp(near_lse - lse) + jnp.exp(far_lse_col - lse))
            o_ref[mine, :] = (acc * (jnp.exp(near_lse - lse) / l)
                              + fo_ref[mine, :].astype(F32) * jnp.exp(far_lse_col - lse)).astype(BF16)
            lse_ref[mine, :] = jnp.broadcast_to(lse, (ATT_BLOCK, HEAD_DIM))

    hh = n_heads
    blk = pl.BlockSpec((group * ATT_BLOCK, HEAD_DIM), lambda h, i: (i, h))
    table = pl.BlockSpec(dist.shape, lambda h, i: (0, 0, 0))
    return _call(
        body, name="attention_fwd", grid=(hh, nq // group),
        in_specs=[pl.BlockSpec(memory_space=pltpu.SMEM), blk,
                  pl.BlockSpec((s, HEAD_DIM), lambda h, i: (0, hh + h)),
                  pl.BlockSpec((s, HEAD_DIM), lambda h, i: (0, 2 * hh + h)), blk, blk, table, table],
        out_specs=[blk, blk],
        out_shape=[jax.ShapeDtypeStruct((s, hh * HEAD_DIM), BF16), jax.ShapeDtypeStruct((s, hh * HEAD_DIM), F32)],
        operands=[slopes, proj, proj, proj, far_out, far_lse, jnp.asarray(dist), jnp.asarray(logc)],
        scratch_shapes=[pltpu.VMEM((ATT_WINDOW + 1, ATT_BLOCK, ATT_BLOCK), F32),
                        pltpu.VMEM((group * nwin, ATT_BLOCK, ATT_BLOCK), F32)],
        semantics=("parallel", "arbitrary"), jobs=jobs)


def _attention_far_bwd(qkv, slopes, out, dout, lse, n_heads):
    s = qkv.shape[0]
    per_class = s // ATT_CLASSES
    scale = HEAD_DIM ** -0.5
    dist, logc = _far_bias_tables(per_class)

    def body(slope_ref, q_ref, k_ref, v_ref, o_ref, do_ref, lse_ref, dist_ref, logc_ref, dq_ref, dk_ref, dv_ref):
        bias = logc_ref[...] - slope_ref[pl.program_id(0)] * dist_ref[...]
        for a in range(ATT_FAR_GROUP):
            rows = pl.ds(a * per_class, per_class)
            q, k, do = q_ref[rows, :], k_ref[rows, :], do_ref[rows, :]
            delta = jnp.sum(do.astype(F32) * o_ref[rows, :].astype(F32), axis=-1, keepdims=True)
            p = jnp.exp(_dot(q, k, tb=True) * scale + bias - lse_ref[rows, :1])
            dv_ref[rows, :] = _dot(p.astype(BF16), do, ta=True).astype(BF16)
            ds = (p * (_dot(do, v_ref[rows, :], tb=True) - delta) * scale).astype(BF16)
            dk_ref[rows, :] = _dot(ds, q, ta=True).astype(BF16)
            dq_ref[rows, :] = _dot(ds, k).astype(BF16)

    hh = n_heads
    blk = pl.BlockSpec((ATT_FAR_GROUP * per_class, HEAD_DIM), lambda h, r: (r, h))
    table = pl.BlockSpec(dist.shape, lambda h, r: (0, 0))
    o_shape = jax.ShapeDtypeStruct((s, hh * HEAD_DIM), BF16)
    return pl.pallas_call(
        body, name="attention_far_bwd", grid=(hh, ATT_CLASSES // ATT_FAR_GROUP),
        in_specs=[pl.BlockSpec(memory_space=pltpu.SMEM), blk,
                  pl.BlockSpec((ATT_FAR_GROUP * per_class, HEAD_DIM), lambda h, r: (r, hh + h)),
                  pl.BlockSpec((ATT_FAR_GROUP * per_class, HEAD_DIM), lambda h, r: (r, 2 * hh + h)),
                  blk, blk, blk, table, table],
        out_specs=[blk] * 3, out_shape=[o_shape] * 3,
        compiler_params=_params(("parallel", "parallel")),
    )(slopes, qkv, qkv, qkv, out, dout, lse, jnp.asarray(dist), jnp.asarray(logc))


def _attention_bwd(proj, slopes, out, lse, dmixed, far_grads, n_heads, jobs=()):
    s = proj.shape[0]
    nq = s // ATT_BLOCK
    scale = HEAD_DIM ** -0.5
    dist, logc = _attention_bias_tables()

    nwin = min(ATT_WINDOW, nq)
    group = math.gcd(ATT_NEAR_GROUP, nq)

    def body(slope_ref, q_ref, k_ref, v_ref, o_ref, do_ref, lse_ref, fdq_ref, fdk_ref, fdv_ref, dist_ref, logc_ref,
             dq_ref, dk_ref, dv_ref, dk_acc, dv_acc, bias_ref):
        h, step = pl.program_id(0), pl.program_id(1)

        @pl.when(step == 0)
        def _():
            dk_acc[...] = jnp.zeros_like(dk_acc)
            dv_acc[...] = jnp.zeros_like(dv_acc)
            _head_bias(bias_ref, slope_ref[h], dist_ref, logc_ref)

        for a in range(group):
            i = step * group + a
            mine = pl.ds(a * ATT_BLOCK, ATT_BLOCK)
            q = q_ref[mine, :]
            do = do_ref[mine, :]
            lse_col = lse_ref[mine, :1]
            delta = jnp.sum(do.astype(F32) * o_ref[mine, :].astype(F32), axis=-1, keepdims=True)
            first = _window_start(i, nq, nwin)
            dq = jnp.zeros((ATT_BLOCK, HEAD_DIM), F32)
            for b in range(nwin):
                rows, kk = _window_block(first + b, i)
                kj = k_ref[rows, :]
                vj = v_ref[rows, :]
                p = jnp.exp(_dot(q, kj, tb=True) * scale + bias_ref[kk] - lse_col)
                dv_acc[rows, :] += _dot(p.astype(BF16), do, ta=True)
                dp = _dot(do, vj, tb=True)
                ds = (p * (dp - delta) * scale).astype(BF16)
                dk_acc[rows, :] += _dot(ds, q, ta=True)
                dq = dq + _dot(ds, kj)
            dq_ref[mine, :] = (dq + fdq_ref[mine, :].astype(F32)).astype(BF16)

        @pl.when(step == nq // group - 1)
        def _():
            dk_ref[...] = (dk_acc[...] + fdk_ref[...].astype(F32)).astype(BF16)
            dv_ref[...] = (dv_acc[...] + fdv_ref[...].astype(F32)).astype(BF16)

    hh = n_heads
    blk = pl.BlockSpec((group * ATT_BLOCK, HEAD_DIM), lambda h, i: (i, h))
    col = pl.BlockSpec((s, HEAD_DIM), lambda h, i: (0, h))
    table = pl.BlockSpec(dist.shape, lambda h, i: (0, 0, 0))
    o_shape = jax.ShapeDtypeStruct((s, hh * HEAD_DIM), BF16)
    return _call(
        body, name="attention_bwd", grid=(hh, nq // group),
        in_specs=[pl.BlockSpec(memory_space=pltpu.SMEM), blk,
                  pl.BlockSpec((s, HEAD_DIM), lambda h, i: (0, hh + h)),
                  pl.BlockSpec((s, HEAD_DIM), lambda h, i: (0, 2 * hh + h)),
                  blk, blk, blk, blk, col, col, table, table],
        out_specs=[blk, col, col], out_shape=[o_shape] * 3,
        operands=[slopes, proj, proj, proj, out, dmixed, lse, *far_grads, jnp.asarray(dist), jnp.asarray(logc)],
        scratch_shapes=[pltpu.VMEM((s, HEAD_DIM), F32)] * 2
        + [pltpu.VMEM((ATT_WINDOW + 1, ATT_BLOCK, ATT_BLOCK), F32)],
        semantics=("parallel", "arbitrary"), jobs=jobs)


def _ret_decays(lgc, lga, strict_c, strict_a):
    c = RET_CHUNK
    rel = (lax.broadcasted_iota(jnp.int32, (c, c), 0) - lax.broadcasted_iota(jnp.int32, (c, c), 1)).astype(F32)
    in_c = (rel > 0) if strict_c else (rel >= 0)
    in_a = (rel < 0) if strict_a else (rel <= 0)
    mask = (jnp.where(in_c, jnp.exp(lgc * jnp.maximum(rel, 0.0)), 0.0)
            + jnp.where(in_a, jnp.exp(lga * jnp.maximum(-rel, 0.0)), 0.0))
    idx = lax.broadcasted_iota(jnp.int32, (c, 1), 0).astype(F32)
    ones = jnp.ones((1, HEAD_DIM), F32)
    dec = dict(
        rel=rel, mask=mask, idx=idx,
        a_c=jnp.exp(lgc * (idx + 1.0)), b_c=jnp.exp(lgc * (c - 1.0 - idx)), chunk_c=jnp.exp(ones * (lgc * c)),
        a_a=jnp.exp(lga * (c - idx)), b_a=jnp.exp(lga * idx), chunk_a=jnp.exp(ones * (lga * c)),
    )
    return dec


def _scaled(x, col):
    return (x.astype(F32) * col).astype(BF16)


def _chunk_rows(i):
    return pl.ds(pl.multiple_of(i * RET_CHUNK, RET_CHUNK), RET_CHUNK)


def _chunk_loop(nc, step, init, unroll=RET_UNROLL):
    group = math.gcd(nc, unroll)

    def trip(t, carry):
        for u in range(group):
            carry = step(t * group + u, carry)
        return carry

    return lax.fori_loop(0, nc // group, trip, init)


def _retention(a, b, c, lg_c, lg_a, *, strict_c, strict_a, scale, n_heads, name, gate=None, norm_w=None, jobs=()):
    s = a[0].shape[0]
    nc = s // RET_CHUNK
    epilogue = gate is not None

    def body(*refs):
        lgc_ref, lga_ref, a_ref, b_ref, c_ref = refs[:5]
        if epilogue:
            g_ref, w_ref, o_ref, mix_ref, sa_ref = refs[5:]
        else:
            o_ref, sa_ref = refs[5:]
        h = pl.program_id(0)
        dec = _ret_decays(lgc_ref[h], lga_ref[h], strict_c, strict_a)

        def reverse(t, state):
            i = nc - 1 - t
            sa_ref[i] = state.astype(BF16)
            rows = _chunk_rows(i)
            return state * dec["chunk_a"] + _dot(_scaled(b_ref[rows, :], dec["b_a"]), c_ref[rows, :], ta=True)

        _chunk_loop(nc, reverse, jnp.zeros((HEAD_DIM, HEAD_DIM), F32))

        def forward(i, state):
            rows = _chunk_rows(i)
            ai, bi, ci = a_ref[rows, :], b_ref[rows, :], c_ref[rows, :]
            inner = (_dot(ai, bi, tb=True) * dec["mask"]).astype(BF16)
            out = (_dot(inner, ci) + _dot(_scaled(ai, dec["a_c"]), state.astype(BF16))
                   + _dot(_scaled(ai, dec["a_a"]), sa_ref[i])) * scale
            o_ref[rows, :] = out.astype(BF16)
            if epilogue:
                r = lax.rsqrt(jnp.mean(out * out, axis=-1, keepdims=True) + EPS)
                g = g_ref[rows, :].astype(F32)
                mix_ref[rows, :] = (out * r * w_ref[...] * (g * _sigmoid(g))).astype(BF16)
            return state * dec["chunk_c"] + _dot(_scaled(bi, dec["b_c"]), ci, ta=True)

        _chunk_loop(nc, forward, jnp.zeros((HEAD_DIM, HEAD_DIM), F32))

    def col(first):
        return pl.BlockSpec((s, HEAD_DIM), lambda h: (0, first + h))

    smem = pl.BlockSpec(memory_space=pltpu.SMEM)
    in_specs = [smem, smem, col(a[1]), col(b[1]), col(c[1])]
    operands = [lg_c, lg_a, a[0], b[0], c[0]]
    o_shape = jax.ShapeDtypeStruct((s, n_heads * HEAD_DIM), BF16)
    out_specs, out_shape = [col(0)], [o_shape]
    if epilogue:
        in_specs += [col(gate[1]), pl.BlockSpec((1, HEAD_DIM), lambda h: (0, h))]
        operands += [gate[0], norm_w]
        out_specs, out_shape = [col(0)] * 2, [o_shape] * 2
    res, carried = _call(
        body, name=name, grid=(n_heads,), in_specs=in_specs, out_specs=out_specs, out_shape=out_shape,
        operands=operands, scratch_shapes=[pltpu.VMEM((nc, HEAD_DIM, HEAD_DIM), BF16)],
        semantics=("parallel",), jobs=jobs)
    res = res if epilogue else res[0]
    return (res, carried) if jobs else res


def _retention_decay_grads(a, b, c, e, lg_c, lg_a, *, scale, n_heads):
    s = a[0].shape[0]
    nc = s // RET_CHUNK
    cf = float(RET_CHUNK)

    def body(lgc_ref, lga_ref, a_ref, b_ref, c_ref, e_ref, gc_ref, ga_ref, sa_ref, ta_ref):
        h = pl.program_id(0)
        lgc, lga = lgc_ref[h], lga_ref[h]
        dec = _ret_decays(lgc, lga, True, True)
        rel, idx = dec["rel"], dec["idx"]
        w_c = jnp.where(rel > 0, rel * jnp.exp(lgc * jnp.maximum(rel, 0.0)), 0.0)
        w_a = jnp.where(rel < 0, -rel * jnp.exp(lga * jnp.maximum(-rel, 0.0)), 0.0)
        zero = jnp.zeros((HEAD_DIM, HEAD_DIM), F32)

        def reverse(t, carry):
            st, dst = carry
            i = nc - 1 - t
            sa_ref[i] = st.astype(BF16)
            ta_ref[i] = dst.astype(BF16)
            rows = _chunk_rows(i)
            bi, ci = b_ref[rows, :], c_ref[rows, :]
            st_new = st * dec["chunk_a"] + _dot(_scaled(bi, dec["b_a"]), ci, ta=True)
            dst_new = (cf * st + dst) * dec["chunk_a"] + _dot(_scaled(bi, idx * dec["b_a"]), ci, ta=True)
            return st_new, dst_new

        _chunk_loop(nc, reverse, (zero, zero))

        def forward(i, carry):
            st, dst, acc_c, acc_a = carry
            rows = _chunk_rows(i)
            ai, bi, ci = a_ref[rows, :], b_ref[rows, :], c_ref[rows, :]
            ev = e_ref[rows, :].astype(F32)
            pg = _dot(ai, bi, tb=True) * _dot(e_ref[rows, :], ci, tb=True)
            a_c, a_a = _scaled(ai, dec["a_c"]), _scaled(ai, dec["a_a"])
            inter_c = _dot(a_c, st.astype(BF16)) * (idx + 1.0) + _dot(a_c, dst.astype(BF16))
            inter_a = _dot(a_a, sa_ref[i]) * (cf - idx) + _dot(a_a, ta_ref[i])
            acc_c = acc_c + jnp.sum(pg * w_c, axis=0, keepdims=True) + jnp.sum(inter_c * ev, axis=0, keepdims=True)
            acc_a = acc_a + jnp.sum(pg * w_a, axis=0, keepdims=True) + jnp.sum(inter_a * ev, axis=0, keepdims=True)
            st_new = st * dec["chunk_c"] + _dot(_scaled(bi, dec["b_c"]), ci, ta=True)
            dst_new = ((cf * st + dst) * dec["chunk_c"]
                       + _dot(_scaled(bi, (cf - 1.0 - idx) * dec["b_c"]), ci, ta=True))
            return st_new, dst_new, acc_c, acc_a

        row = jnp.zeros((1, HEAD_DIM), F32)
        _, _, acc_c, acc_a = _chunk_loop(nc, forward, (zero, zero, row, row))
        gc_ref[...] = jnp.broadcast_to(jnp.sum(acc_c, axis=-1, keepdims=True) * scale, gc_ref.shape)
        ga_ref[...] = jnp.broadcast_to(jnp.sum(acc_a, axis=-1, keepdims=True) * scale, ga_ref.shape)

    def col(first):
        return pl.BlockSpec((s, HEAD_DIM), lambda h: (0, first + h))

    smem = pl.BlockSpec(memory_space=pltpu.SMEM)
    o_spec = pl.BlockSpec((1, 8, HEAD_DIM), lambda h: (h, 0, 0))
    o_shape = jax.ShapeDtypeStruct((n_heads, 8, HEAD_DIM), F32)
    gc, ga = pl.pallas_call(
        body, name="retention_decay_grads", grid=(n_heads,),
        in_specs=[smem, smem, col(a[1]), col(b[1]), col(c[1]), col(e[1])],
        out_specs=[o_spec] * 2, out_shape=[o_shape] * 2,
        scratch_shapes=[pltpu.VMEM((nc, HEAD_DIM, HEAD_DIM), BF16)] * 2,
        compiler_params=_params(("parallel",)),
    )(lg_c, lg_a, a[0], b[0], c[0], e[0])
    return gc[:, 0, 0], ga[:, 0, 0]


def _ret_gate_bwd(dmixed, first_col, out, proj, gate_col, norm_w, n_heads):
    s = out.shape[0]
    tr = _row_block(s, 8 * HEAD_DIM)

    def body(dm_ref, o_ref, g_ref, w_ref, do_ref, dg_ref, dw_ref):
        dm = dm_ref[...].astype(F32)
        ov = o_ref[...].astype(F32)
        g = g_ref[...].astype(F32)
        w = w_ref[...]
        r = lax.rsqrt(jnp.mean(ov * ov, axis=-1, keepdims=True) + EPS)
        ohat = ov * r
        sg = _sigmoid(g)
        silu = g * sg
        dg_ref[...] = (dm * ohat * w * sg * (1.0 + g * (1.0 - sg))).astype(BF16)
        dohat = dm * w * silu
        do_ref[...] = (r * (dohat - ohat * jnp.mean(dohat * ohat, axis=-1, keepdims=True))).astype(BF16)

        @pl.when(pl.program_id(1) == 0)
        def _():
            dw_ref[...] = jnp.zeros_like(dw_ref)

        dw_ref[...] += jnp.sum(dm * ohat * silu, axis=0, keepdims=True)

    def blk(first):
        return pl.BlockSpec((tr, HEAD_DIM), lambda h, i: (i, first + h))

    vec = pl.BlockSpec((1, HEAD_DIM), lambda h, i: (0, h))
    o_shape = jax.ShapeDtypeStruct((s, n_heads * HEAD_DIM), BF16)
    return pl.pallas_call(
        body, name="ret_gate_bwd", grid=(n_heads, s // tr),
        in_specs=[blk(first_col), blk(0), blk(gate_col), vec],
        out_specs=[blk(0), blk(0), vec],
        out_shape=[o_shape, o_shape, jax.ShapeDtypeStruct((1, n_heads * HEAD_DIM), F32)],
        compiler_params=_params(("parallel", "arbitrary")),
    )(dmixed, out, proj, norm_w)


def _step(x, target, norm_mix_w, ret_decay_fwd, ret_decay_bwd, ret_norm_w, norm_ffn_w, norm_final_w, own,
          w_in_started, pos):
    d = x.shape[1]
    nh = d // (2 * HEAD_DIM)
    scale = HEAD_DIM ** -0.5
    slopes = jnp.exp2(-8.0 * jnp.arange(1, nh + 1, dtype=F32) / nh)
    lg_f = -jnp.exp(ret_decay_fwd)
    lg_b = -jnp.exp(ret_decay_bwd)
    q_r, k_r, v_r, g_r = 3 * nh, 4 * nh, 5 * nh, 6 * nh
    ax = BIG_AXIS

    def gather(names, arrays, stage, part=None):
        return _gather_job(arrays, [ax[k] for k in names], stage, part)

    def add_halves(k, g, received):
        return _add_halves(g, received, ax[k], pos, name="grad_add_halves_" + k)

    def sum_parts(k, g, received, parts):
        return _sum_chip_parts(g, received, parts, ax[k], pos, name="grad_sum_parts_" + k)

    sems, w_in, token = w_in_started
    n1 = _rmsnorm_fwd(x, norm_mix_w, name="norm_mix_fwd", after=token)
    w_in = _split_gather_wait(sems, w_in, ax["w_in"], [n1] + [own[k] for k in BIG if k != "w_in"])
    (w_in,) = _run_jobs([gather(["w_in"], [w_in], "d2d")], name="all_gather_w_in_sibling")
    proj, [[w_gate]] = _matmul(n1, w_in, name="in_proj", out_dtype=BF16, tm=2048,
                               jobs=[gather(["w_gate"], [own["w_gate"]], "ici")])
    qkv_classes = _to_classes(proj[:, :3 * nh * HEAD_DIM])
    (ret, ret_mixed), [[w_gate], [w_out]] = _retention(
        (proj, q_r), (proj, k_r), (proj, v_r), lg_f, lg_b, strict_c=False, strict_a=True, scale=scale, n_heads=nh,
        name="retention_fwd", gate=(proj, g_r), norm_w=ret_norm_w,
        jobs=[gather(["w_gate"], [w_gate], "d2d"), gather(["w_out"], [own["w_out"]], "ici")])
    (far_out, far_lse), [[w_up]] = _attention_far_fwd(
        qkv_classes, slopes, nh, jobs=[gather(["w_up"], [own["w_up"]], "ici", (0, 1, 4))])
    (attn, lse), [[w_out], [w_up]] = _attention_fwd(
        proj, slopes, _from_classes(far_out), _from_classes(far_lse), nh,
        jobs=[gather(["w_out"], [w_out], "d2d"),
              _fuse(gather(["w_up"], [w_up], "d2d", (0, 1, 4)), gather(["w_up"], [w_up], "ici", (1, 2, 4)))])
    mixed = jnp.concatenate([attn, ret_mixed], axis=1)
    h1, [[w_up]] = _matmul(
        mixed, w_out, name="out_proj", residual=x,
        jobs=[_fuse(gather(["w_up"], [w_up], "d2d", (1, 2, 4)), gather(["w_up"], [w_up], "ici", (3, 1, 4)))])
    (w_up,) = _run_jobs([gather(["w_up"], [w_up], "d2d", (3, 1, 4))], name="all_gather_w_up_sibling")
    n2 = _rmsnorm_fwd(h1, norm_ffn_w, name="norm_ffn_fwd")
    (gate, up, act), [[w_down]] = _swiglu_fwd(n2, w_gate, w_up, jobs=[gather(["w_down"], [own["w_down"]], "ici")])
    (w_down,) = _run_jobs([gather(["w_down"], [w_down], "d2d")], name="all_gather_w_down_sibling")
    h2 = _matmul(act, w_down, name="down_proj", residual=h1, tk=2816)
    dh2, dh2_b, d_norm_final, loss = _loss_head(h2, norm_final_w, target)

    dgate, dup = _swiglu_bwd_act(dh2_b, w_down, gate, up)
    g_down = _weight_grad(act, dh2_b, name="grad_w_down")
    g_gate, [[r_down]] = _weight_grad(n2, dgate, name="grad_w_gate", jobs=[_exchange_job([g_down], [ax["w_down"]])])
    s_down = add_halves("w_down", g_down, r_down)
    g_up, [[r_gate], [p_down]] = _weight_grad(
        n2, dup, name="grad_w_up",
        jobs=[_exchange_job([g_gate], [ax["w_gate"]]), _send_sums_job([s_down], [ax["w_down"]], (0, 1, 2))])
    s_gate = add_halves("w_gate", g_gate, r_gate)
    dn2, [[r_up], [p_gate], [p_down]] = _swiglu_bwd_in(
        dgate, dup, w_gate, w_up,
        jobs=[_exchange_job([g_up], [ax["w_up"]]), _send_sums_job([s_gate], [ax["w_gate"]]),
              _send_sums_job([s_down], [ax["w_down"]], (1, 1, 2), landing=[p_down])])
    h_down = sum_parts("w_down", g_down, r_down, p_down)
    s_up = add_halves("w_up", g_up, r_up)
    h_gate = sum_parts("w_gate", g_gate, r_gate, p_gate)
    dh1, dh1_b, d_norm_ffn = _rmsnorm_bwd(dn2, h1, norm_ffn_w, dh2, name="norm_ffn_bwd")

    dmixed, [[gr_down], [p_up]] = _matmul(
        dh1_b, w_out, name="out_proj_bwd", tb=True, out_dtype=BF16,
        jobs=[_join_job([h_down], [ax["w_down"]]), _send_sums_job([s_up], [ax["w_up"]], (0, 1, 4))])
    far_in = [_to_classes(t) for t in (attn, dmixed[:, :nh * HEAD_DIM], lse)]
    g_out, [[p_up]] = _weight_grad(mixed, dh1_b, name="grad_w_out",
                                   jobs=[_send_sums_job([s_up], [ax["w_up"]], (1, 1, 4), landing=[p_up])])
    d_ret, dg_r, d_ret_norm = _ret_gate_bwd(dmixed, nh, ret, proj, g_r, ret_norm_w, nh)
    far_grads = _attention_far_bwd(qkv_classes, slopes, *far_in, nh)
    far_grads = [_from_classes(t) for t in far_grads]
    dq_r, [[gr_gate], [p_up]] = _retention(
        (d_ret, 0), (proj, v_r), (proj, k_r), lg_f, lg_b, strict_c=False, strict_a=True, scale=scale, n_heads=nh,
        name="retention_dq",
        jobs=[_join_job([h_gate], [ax["w_gate"]]), _send_sums_job([s_up], [ax["w_up"]], (2, 1, 4), landing=[p_up])])
    (dq_a, dk_a, dv_a), [[p_up], [r_out]] = _attention_bwd(
        proj, slopes, attn, lse, dmixed, far_grads, nh,
        jobs=[_send_sums_job([s_up], [ax["w_up"]], (3, 1, 4), landing=[p_up]),
              _exchange_job([g_out], [ax["w_out"]])])
    s_out = add_halves("w_out", g_out, r_out)
    h_up = sum_parts("w_up", g_up, r_up, p_up)
    dv_r, [[p_out], [gr_up]] = _retention(
        (proj, k_r), (proj, q_r), (d_ret, 0), lg_b, lg_f, strict_c=True, strict_a=False, scale=scale, n_heads=nh,
        name="retention_dv", jobs=[_send_sums_job([s_out], [ax["w_out"]]), _join_job([h_up], [ax["w_up"]])])
    h_out = sum_parts("w_out", g_out, r_out, p_out)
    dk_r, [[gr_out]] = _retention(
        (proj, v_r), (d_ret, 0), (proj, q_r), lg_b, lg_f, strict_c=True, strict_a=False, scale=scale, n_heads=nh,
        name="retention_dk", jobs=[_join_job([h_out], [ax["w_out"]])])
    dlg_f, dlg_b = _retention_decay_grads((proj, q_r), (proj, k_r), (proj, v_r), (d_ret, 0), lg_f, lg_b,
                                          scale=scale, n_heads=nh)
    dproj = [dq_a, dk_a, dv_a, dq_r, dk_r, dv_r, dg_r]
    g_in = _weight_grad_pieces(n1, dproj, name="grad_w_in")
    exchange = _split_start(_exchange_job([g_in], [ax["w_in"]]), name="grad_exchange_w_in_start")
    dn1 = _matmul_pieces_nt(dproj, w_in, name="in_proj_bwd", after=[exchange["token"]])
    g_in, r_in = _split_wait(exchange, [dn1], name="grad_exchange_w_in_wait")
    s_in = add_halves("w_in", g_in, r_in)
    sending = _split_start(_send_sums_job([s_in], [ax["w_in"]]), name="grad_send_w_in_start")
    dx, _, d_norm_mix = _rmsnorm_bwd(dn1, x, norm_mix_w, dh1, name="norm_mix_bwd", after=[sending["token"]])

    small = dict(loss=loss[0, 0], norm_mix_w=d_norm_mix, ret_decay_fwd=dlg_f * lg_f, ret_decay_bwd=dlg_b * lg_b,
                 ret_norm_w=d_ret_norm, norm_ffn_w=d_norm_ffn, norm_final_w=d_norm_final)
    return (dx, dict(w_out=gr_out, w_gate=gr_gate, w_up=gr_up, w_down=gr_down), small,
            dict(sending=sending, grad=g_in, received=r_in))


def _mesh_position():
    x, y, c = lax.axis_index("x"), lax.axis_index("y"), lax.axis_index("c")
    chips = [(1 - x, y), (x, 1 - y), (1 - x, 1 - y)]
    return x, y, c, chips


def _span(span):
    if span is None:
        return slice(None)
    start, size, step = span
    return pl.ds(start if isinstance(start, int) else pl.multiple_of(start, step), size)


def _part_rows(part, rows):
    first, count, of = part
    return first * (rows // of), count * (rows // of), rows // of


def _region(ref, axis, shard, half, shard_size, half_size, part=None, total_rows=None):
    along = None if shard is None else (shard * shard_size, shard_size, shard_size)
    other = None if half is None else (half * half_size, half_size, half_size)
    rows, cols = (other, along) if axis == 1 else (along, other)
    if part is not None:
        start, size, _ = rows if rows is not None else (0, total_rows, None)
        offset, size, step = _part_rows(part, size)
        rows = (start + offset, size, step)
    return ref.at[_span(rows), _span(cols)]


def _fuse(first, second):
    assert not (first.ins or first.outs or second.ins or second.outs)
    assert len(first.ios) == len(second.ios) and all(a is b for a, b in zip(first.ios, second.ios))
    cut = len(first.sems)

    def start(refs, sems):
        first.start(refs, sems[:cut])
        second.start(refs, sems[cut:])

    def finish(refs, sems):
        first.finish(refs, sems[:cut])
        second.finish(refs, sems[cut:])

    return _Job(ios=first.ios, sems=first.sems + second.sems, start=start, finish=finish)


def _gather_job(full, axes, stage, part=None):
    n = len(full)

    def copies(refs, sems):
        send_sem, recv_sem = sems
        x, y, c, chips = _mesh_position()
        me = 2 * x + y

        def copy(w, k, shard, half, target):
            rows_cols = full[w].shape
            place = _region(refs[w], axes[w], shard, half, rows_cols[axes[w]] // N_CHIPS, rows_cols[1 - axes[w]] // 2,
                            part)
            return pltpu.make_async_remote_copy(
                src_ref=place, dst_ref=place, send_sem=send_sem.at[w, k], recv_sem=recv_sem.at[w, k],
                device_id=target, device_id_type=MESH)

        def sent(w, k):
            if stage == "ici":
                return copy(w, k, me, c, (chips[k][0], chips[k][1], c))
            return copy(w, k, 2 * chips[k][0] + chips[k][1], c, (x, y, 1 - c))

        def landed(w, k):
            return copy(w, k, 2 * chips[k][0] + chips[k][1], c if stage == "ici" else 1 - c, (x, y, 1 - c))

        return sent, landed

    def start(refs, sems):
        sent, _ = copies(refs, sems)
        for w in range(n):
            for k in range(3):
                sent(w, k).start()

    def finish(refs, sems):
        sent, landed = copies(refs, sems)
        for w in range(n):
            for k in range(3):
                landed(w, k).wait_recv()
                sent(w, k).wait_send()

    return _Job(ios=full, sems=[pltpu.SemaphoreType.DMA((n, 3))] * 2, start=start, finish=finish)


def _exchange_job(grads, axes):
    n = len(grads)

    def half_shape(w):
        return tuple(d // 2 if a != axes[w] else d for a, d in enumerate(grads[w].shape))

    def copy(refs, sems, w):
        x, y, c, _ = _mesh_position()
        return pltpu.make_async_remote_copy(
            src_ref=_region(refs[w], axes[w], None, 1 - c, 0, half_shape(w)[1 - axes[w]]), dst_ref=refs[n + w],
            send_sem=sems[0].at[w], recv_sem=sems[1].at[w], device_id=(x, y, 1 - c), device_id_type=MESH)

    def start(refs, sems):
        for w in range(n):
            copy(refs, sems, w).start()

    def finish(refs, sems):
        for w in range(n):
            copy(refs, sems, w).wait()

    return _Job(ins=grads, outs=[jax.ShapeDtypeStruct(half_shape(w), F32) for w in range(n)],
                sems=[pltpu.SemaphoreType.DMA((n,))] * 2, start=start, finish=finish)


def _half_block_spec(axis, block, half_blocks, use_half):
    if axis == 1:
        if use_half:
            return pl.BlockSpec(block, lambda i, pos: (pos[0] * half_blocks + i, 0))
        return pl.BlockSpec(block, lambda i, pos: (i, 0))
    if use_half:
        return pl.BlockSpec(block, lambda i, pos: (i, pos[0]))
    return pl.BlockSpec(block, lambda i, pos: (i, 0))


def _add_halves(grad, received, axis, pos, *, name):
    rows, cols = received.shape
    tr = _row_block(rows, cols)
    nb = rows // tr

    def body(pos_ref, g_ref, r_ref, o_ref):
        o_ref[...] = (g_ref[...] + r_ref[...]).astype(BF16)

    blk = (tr, cols)
    return pl.pallas_call(
        body, name=name, out_shape=jax.ShapeDtypeStruct((rows, cols), BF16),
        grid_spec=pltpu.PrefetchScalarGridSpec(
            num_scalar_prefetch=1, grid=(nb,),
            in_specs=[_half_block_spec(axis, blk, nb, True), _half_block_spec(axis, blk, nb, False)],
            out_specs=_half_block_spec(axis, blk, nb, False)),
        compiler_params=_params(("parallel",)),
    )(pos, grad, received)


def _send_sums_job(sums, axes, part=None, landing=None):
    n = len(sums)

    def part_shape(w):
        return tuple(d // N_CHIPS if a == axes[w] else d for a, d in enumerate(sums[w].shape))

    def copy(refs, sems, w, k):
        x, y, c, chips = _mesh_position()
        shard = 2 * chips[k][0] + chips[k][1]
        rows = part_shape(w)[0]
        dst = refs[n + w].at[k]
        if part is not None:
            offset, size, _ = _part_rows(part, rows)
            dst = refs[n + w].at[k, pl.ds(offset, size), :]
        return pltpu.make_async_remote_copy(
            src_ref=_region(refs[w], axes[w], shard, None, part_shape(w)[axes[w]], 0, part, rows), dst_ref=dst,
            send_sem=sems[0].at[w, k], recv_sem=sems[1].at[w, k],
            device_id=(chips[k][0], chips[k][1], c), device_id_type=MESH)

    def start(refs, sems):
        for w in range(n):
            for k in range(3):
                copy(refs, sems, w, k).start()

    def finish(refs, sems):
        for w in range(n):
            for k in range(3):
                copy(refs, sems, w, k).wait()

    sems = [pltpu.SemaphoreType.DMA((n, 3))] * 2
    if landing is not None:
        return _Job(ins=sums, ios=landing, sems=sems, start=start, finish=finish)
    return _Job(ins=sums, outs=[jax.ShapeDtypeStruct((3,) + part_shape(w), BF16) for w in range(n)],
                sems=sems, start=start, finish=finish)


def _sum_chip_parts(grad, received, parts, axis, pos, *, name):
    _, rows, cols = parts.shape
    tr = _row_block(rows, cols)
    nb = rows // tr
    blk = (tr, cols)

    def body(pos_ref, g_ref, r_ref, p_ref, o_ref):
        total = g_ref[...] + r_ref[...]
        for k in range(3):
            total = total + p_ref[k].astype(F32)
        o_ref[...] = total

    if axis == 1:
        g_spec = pl.BlockSpec(blk, lambda i, pos: (pos[0] * nb + i, pos[1]))
        r_spec = pl.BlockSpec(blk, lambda i, pos: (i, pos[1]))
        o_spec = pl.BlockSpec(blk, lambda i, pos: (pos[0] * nb + i, 0))
        shard_shape = (2 * rows, cols)
    else:
        g_spec = pl.BlockSpec(blk, lambda i, pos: (pos[1] * nb + i, pos[0]))
        r_spec = pl.BlockSpec(blk, lambda i, pos: (pos[1] * nb + i, 0))
        o_spec = pl.BlockSpec(blk, lambda i, pos: (i, pos[0]))
        shard_shape = (rows, 2 * cols)
    return pl.pallas_call(
        body, name=name, out_shape=jax.ShapeDtypeStruct(shard_shape, F32),
        grid_spec=pltpu.PrefetchScalarGridSpec(
            num_scalar_prefetch=1, grid=(nb,),
            in_specs=[g_spec, r_spec, pl.BlockSpec((3,) + blk, lambda i, pos: (0, i, 0))],
            out_specs=o_spec),
        compiler_params=_params(("parallel",)),
    )(pos, grad, received, parts)


def _join_job(shards, axes):
    n = len(shards)

    def copy(refs, sems, w, other):
        x, y, c, _ = _mesh_position()
        place = _region(refs[w], axes[w], None, 1 - c if other else c, 0, shards[w].shape[1 - axes[w]] // 2)
        return pltpu.make_async_remote_copy(
            src_ref=place, dst_ref=place, send_sem=sems[0].at[w], recv_sem=sems[1].at[w],
            device_id=(x, y, 1 - c), device_id_type=MESH)

    def start(refs, sems):
        for w in range(n):
            copy(refs, sems, w, False).start()

    def finish(refs, sems):
        for w in range(n):
            copy(refs, sems, w, True).wait_recv()
            copy(refs, sems, w, False).wait_send()

    return _Job(ios=shards, sems=[pltpu.SemaphoreType.DMA((n,))] * 2, start=start, finish=finish)


def _all_reduce_small(vec, after=()):
    rows, cols = vec.shape

    def body(v_ref, *rest):
        o_ref, land_ref, send_sem, recv_sem = rest[len(after):]
        x, y, c, _ = _mesh_position()
        me = 4 * x + 2 * y + c
        land_ref[me] = v_ref[...]
        copies = []
        for k in range(1, 8):
            px, py, pc = x ^ (k >> 2), y ^ ((k >> 1) & 1), c ^ (k & 1)
            copies.append(pltpu.make_async_remote_copy(
                src_ref=v_ref, dst_ref=land_ref.at[me], send_sem=send_sem.at[k], recv_sem=recv_sem.at[k],
                device_id=(px, py, pc), device_id_type=MESH))
        for cp in copies:
            cp.start()
        for k in range(1, 8):
            peer = me ^ k
            pltpu.make_async_remote_copy(
                src_ref=v_ref, dst_ref=land_ref.at[peer], send_sem=send_sem.at[k], recv_sem=recv_sem.at[k],
                device_id=(x, y, c), device_id_type=MESH).wait_recv()
        for cp in copies:
            cp.wait_send()
        total = land_ref[0]
        for k in range(1, 8):
            total = total + land_ref[k]
        o_ref[...] = total

    vmem = pl.BlockSpec(memory_space=pltpu.VMEM)
    return pl.pallas_call(
        body, name="all_reduce_small", in_specs=[vmem] + [pl.BlockSpec(memory_space=pl.ANY)] * len(after),
        out_specs=vmem, out_shape=jax.ShapeDtypeStruct((rows, cols), F32),
        scratch_shapes=[pltpu.VMEM((8, rows, cols), F32), pltpu.SemaphoreType.DMA((8,)), pltpu.SemaphoreType.DMA((8,))],
    )(vec, *after)


def _adamw(w, g, m, v, *, name, after=()):
    rows, cols = w.shape
    tr = _row_block(rows, cols) if rows % 8 == 0 else rows
    bc1 = 1.0 - ADAM_B1 ** ADAM_STEP
    bc2 = 1.0 - ADAM_B2 ** ADAM_STEP

    def body(w_ref, g_ref, m_ref, v_ref, *rest):
        go_ref, d_ref, mo_ref, vo_ref = rest[len(after):]
        gv = g_ref[...]
        go_ref[...] = gv
        mn = ADAM_B1 * m_ref[...] + (1.0 - ADAM_B1) * gv
        vn = ADAM_B2 * v_ref[...] + (1.0 - ADAM_B2) * (gv * gv)
        mo_ref[...] = mn
        vo_ref[...] = vn
        d_ref[...] = -ADAM_LR * ((mn / bc1) / (jnp.sqrt(vn / bc2) + ADAM_EPS) + ADAM_WD * w_ref[...])

    blk = pl.BlockSpec((tr, cols), lambda i: (i, 0))
    shape = jax.ShapeDtypeStruct((rows, cols), F32)
    return pl.pallas_call(
        body, name=name, grid=(rows // tr,), in_specs=[blk] * 4 + [pl.BlockSpec(memory_space=pl.ANY)] * len(after),
        out_specs=[blk] * 4, out_shape=[shape] * 4, compiler_params=_params(("parallel",)),
    )(w, g, m, v, *after)


def _to_bf16_in_place(w, axis, pos, *, name, after=None):
    rows, cols = w.shape
    tr = _row_block(rows, cols)
    nb = rows // tr

    def body(pos_ref, w_ref, *rest):
        rest[-1][...] = w_ref[...].astype(BF16)

    if axis == 1:
        o_spec = pl.BlockSpec((tr, cols), lambda i, pos: (i, pos[1]))
        full_shape = (rows, N_CHIPS * cols)
    else:
        o_spec = pl.BlockSpec((tr, cols), lambda i, pos: (pos[1] * nb + i, 0))
        full_shape = (N_CHIPS * rows, cols)
    in_specs = [pl.BlockSpec((tr, cols), lambda i, pos: (i, 0))]
    operands = [pos, w]
    if after is not None:
        in_specs.append(pl.BlockSpec(after.shape, lambda i, pos: (0, 0)))
        operands.append(after)
    return pl.pallas_call(
        body, name=name, out_shape=jax.ShapeDtypeStruct(full_shape, BF16),
        grid_spec=pltpu.PrefetchScalarGridSpec(num_scalar_prefetch=1, grid=(nb,), in_specs=in_specs, out_specs=o_spec),
        compiler_params=_params(("parallel",)),
    )(*operands)


def _split_gather_start(full, axis):
    rows_cols = full.shape

    def body(buf_ref, *rest):
        sems = rest[:6]
        token_ref = rest[7]
        x, y, c, chips = _mesh_position()
        place = _region(buf_ref, axis, 2 * x + y, c, rows_cols[axis] // N_CHIPS, rows_cols[1 - axis] // 2)
        for k in range(3):
            pltpu.make_async_remote_copy(
                src_ref=place, dst_ref=place, send_sem=sems[k], recv_sem=sems[3 + k],
                device_id=(chips[k][0], chips[k][1], c), device_id_type=MESH).start()
        token_ref[...] = jnp.zeros_like(token_ref)

    hbm = pl.BlockSpec(memory_space=pltpu.HBM)
    sem = pl.BlockSpec(memory_space=pltpu.SEMAPHORE)
    res = pl.pallas_call(
        body, name="all_gather_w_in_start",
        out_shape=(*[pltpu.SemaphoreType.DMA(())] * 6, pltpu.HBM(full.shape, full.dtype),
                   jax.ShapeDtypeStruct((8, HEAD_DIM), F32)),
        in_specs=(hbm,), out_specs=(*[sem] * 6, hbm, pl.BlockSpec(memory_space=pltpu.VMEM)),
        input_output_aliases={0: 6},
        compiler_params=pltpu.CompilerParams(has_side_effects=pltpu.SideEffectType.DATAFLOW_SIDE_EFFECTING),
    )(pltpu.with_memory_space_constraint(full, pltpu.HBM))
    return list(res[:6]), res[6], res[7]


def _split_gather_wait(sems, full, axis, after):
    rows_cols = full.shape

    def body(buf_ref, *rest):
        sem_refs = rest[:6]
        x, y, c, chips = _mesh_position()

        def copy(k, shard):
            place = _region(buf_ref, axis, shard, c, rows_cols[axis] // N_CHIPS, rows_cols[1 - axis] // 2)
            return pltpu.make_async_remote_copy(
                src_ref=place, dst_ref=place, send_sem=sem_refs[k], recv_sem=sem_refs[3 + k],
                device_id=(chips[k][0], chips[k][1], c), device_id_type=MESH)

        for k in range(3):
            copy(k, 2 * x + y).wait_send()
            copy(k, 2 * chips[k][0] + chips[k][1]).wait_recv()

    hbm = pl.BlockSpec(memory_space=pltpu.HBM)
    sem = pl.BlockSpec(memory_space=pltpu.SEMAPHORE)
    return pl.pallas_call(
        body, name="all_gather_w_in_wait", out_shape=pltpu.HBM(full.shape, full.dtype),
        in_specs=(hbm, *[sem] * 6, *[pl.BlockSpec(memory_space=pl.ANY)] * len(after)), out_specs=hbm,
        input_output_aliases={0: 0},
        compiler_params=pltpu.CompilerParams(has_side_effects=pltpu.SideEffectType.DATAFLOW_SIDE_EFFECTING),
    )(full, *sems, *after)


BIG = ("w_in", "w_out", "w_gate", "w_up", "w_down")
BIG_AXIS = dict(w_in=1, w_out=0, w_gate=1, w_up=1, w_down=0)
SMALL = ("norm_mix_w", "ret_decay_fwd", "ret_decay_bwd", "ret_norm_w", "norm_ffn_w", "norm_final_w")
ALL_WEIGHTS = ("norm_mix_w", "w_in", "ret_decay_fwd", "ret_decay_bwd", "ret_norm_w", "w_out", "norm_ffn_w",
               "w_gate", "w_up", "w_down", "norm_final_w")
SMALL_ROW = 128 * 8


def _pack_small(small):
    pieces = [jnp.reshape(small["loss"], (1,))] + [jnp.reshape(small[k], (-1,)) for k in SMALL]
    rows = []
    for p in pieces:
        pad = -p.shape[0] % (8 * SMALL_ROW)
        rows.append(jnp.reshape(jnp.pad(p, (0, pad)), (-1, SMALL_ROW)))
    return jnp.concatenate(rows, axis=0)


def _unpack_small(block, like):
    out, row = {}, 0
    for k in ("loss",) + SMALL:
        size = 1 if k == "loss" else like[k].size
        nrows = -(-size // (8 * SMALL_ROW)) * 8
        out[k] = jnp.reshape(block[row:row + nrows], (-1,))[:size]
        row += nrows
    return out


def kernel(x, norm_mix_w, w_in, ret_decay_fwd, ret_decay_bwd, ret_norm_w, w_out, norm_ffn_w, w_gate, w_up, w_down, norm_final_w, loss_target, m_norm_mix_w, m_w_in, m_ret_decay_fwd, m_ret_decay_bwd, m_ret_norm_w, m_w_out, m_norm_ffn_w, m_w_gate, m_w_up, m_w_down, m_norm_final_w, v_norm_mix_w, v_w_in, v_ret_decay_fwd, v_ret_decay_bwd, v_ret_norm_w, v_w_out, v_norm_ffn_w, v_w_gate, v_w_up, v_w_down, v_norm_final_w):
    weights = dict(norm_mix_w=norm_mix_w, w_in=w_in, ret_decay_fwd=ret_decay_fwd, ret_decay_bwd=ret_decay_bwd,
                   ret_norm_w=ret_norm_w, w_out=w_out, norm_ffn_w=norm_ffn_w, w_gate=w_gate, w_up=w_up,
                   w_down=w_down, norm_final_w=norm_final_w)
    m_in = dict(norm_mix_w=m_norm_mix_w, w_in=m_w_in, ret_decay_fwd=m_ret_decay_fwd, ret_decay_bwd=m_ret_decay_bwd,
                ret_norm_w=m_ret_norm_w, w_out=m_w_out, norm_ffn_w=m_norm_ffn_w, w_gate=m_w_gate, w_up=m_w_up,
                w_down=m_w_down, norm_final_w=m_norm_final_w)
    v_in = dict(norm_mix_w=v_norm_mix_w, w_in=v_w_in, ret_decay_fwd=v_ret_decay_fwd, ret_decay_bwd=v_ret_decay_bwd,
                ret_norm_w=v_ret_norm_w, w_out=v_w_out, norm_ffn_w=v_norm_ffn_w, w_gate=v_w_gate, w_up=v_w_up,
                w_down=v_w_down, norm_final_w=v_norm_final_w)
    pos = jnp.stack([lax.axis_index("c"), 2 * lax.axis_index("x") + lax.axis_index("y")]).astype(jnp.int32)

    own = {"w_in": _to_bf16_in_place(weights["w_in"][0], BIG_AXIS["w_in"], pos, name="cast_w_in")}
    w_in_started = _split_gather_start(own["w_in"], BIG_AXIS["w_in"])
    for k in BIG[1:]:
        own[k] = _to_bf16_in_place(weights[k][0], BIG_AXIS[k], pos, name="cast_" + k, after=w_in_started[2])

    dx, grad_w, small, w_in_pending = _step(
        x[0], loss_target[0], norm_mix_w, ret_decay_fwd[0], ret_decay_bwd[0], ret_norm_w, norm_ffn_w,
        norm_final_w[None, :], own, w_in_started, pos)

    delta, new_m, new_v = {}, {}, {}

    def update(k, after):
        shape = weights[k].shape
        as2d = (lambda t: jnp.reshape(t, (-1, shape[-1])))
        grad_w[k], delta[k], new_m[k], new_v[k] = (jnp.reshape(t, shape) for t in _adamw(
            as2d(weights[k]), as2d(grad_w[k]), as2d(m_in[k]), as2d(v_in[k]), name="adamw_" + k, after=after))

    others = [k for k in BIG if k != "w_in"]
    for k in others:
        update(k, [w_in_pending["sending"]["token"]])
    _, parts = _split_wait(w_in_pending["sending"], [dx] + [delta[k] for k in others], name="grad_send_w_in_wait")

    half = _sum_chip_parts(w_in_pending["grad"], w_in_pending["received"], parts, BIG_AXIS["w_in"], pos,
                           name="grad_sum_parts_w_in")
    joining = _split_start(_join_job([half], [BIG_AXIS["w_in"]]), name="grad_join_w_in_start")

    like = {k: weights[k] for k in SMALL}
    reduced = _unpack_small(_all_reduce_small(_pack_small(small), after=[joining["token"]]), like)
    loss = reduced["loss"][0]
    for k in SMALL:
        grad_w[k] = jnp.reshape(reduced[k], (1, -1))
        update(k, [])
    (grad_w["w_in"],) = _split_wait(joining, [delta[k] for k in SMALL], name="grad_join_w_in_wait")
    update("w_in", [])

    return (loss, dx[None], *[grad_w[k] for k in ALL_WEIGHTS], *[delta[k] for k in ALL_WEIGHTS],
            *[new_m[k] for k in ALL_WEIGHTS], *[new_v[k] for k in ALL_WEIGHTS])
```

```python
import functools
import math

import numpy as np
import jax
import jax.numpy as jnp
from jax import lax
from jax.experimental import pallas as pl
from jax.experimental.pallas import tpu as pltpu

F32 = jnp.float32
BF16 = jnp.bfloat16
MESH = pl.DeviceIdType.MESH

HEAD_DIM = 128
RET_CHUNK = 128
RET_UNROLL = 8
EPS = 1e-6
DILATED_PATTERNS = ((128, 1), (512, 4), (2048, 16))
ATT_BLOCK = 256
ATT_REACH = max(w // 2 for w, _ in DILATED_PATTERNS)
ATT_NEAR = ATT_BLOCK
ATT_CLASSES = DILATED_PATTERNS[-1][1]
assert all(w // 2 <= ATT_NEAR for w, _ in DILATED_PATTERNS[:-1])
ATT_KB = -(-ATT_NEAR // ATT_BLOCK)
ATT_WINDOW = 2 * ATT_KB + 1
ATT_FAR_GROUP = 8
ATT_NEAR_GROUP = 4
MASKED = -1e30
ROW_MAX_INIT = -1e29
N_CHIPS = 4
VMEM_LIMIT_BYTES = 56 * 1024 * 1024
ELEM_BLOCK_BYTES = 2 * 1024 * 1024

ADAM_LR = 0.001
ADAM_B1 = 0.9
ADAM_B2 = 0.999
ADAM_EPS = 1e-08
ADAM_WD = 0.01
ADAM_STEP = 10


def _params(sem=None):
    return pltpu.CompilerParams(dimension_semantics=sem, vmem_limit_bytes=VMEM_LIMIT_BYTES)


def _sigmoid(x):
    return 0.5 * jnp.tanh(0.5 * x) + 0.5


class _Job:
    def __init__(self, *, ins=(), ios=(), outs=(), sems=(), start, finish):
        self.ins, self.ios, self.outs, self.sems = list(ins), list(ios), list(outs), list(sems)
        self.start, self.finish = start, finish

    def results(self):
        return [jax.ShapeDtypeStruct(a.shape, a.dtype) for a in self.ios] + self.outs


def _call(body, *, name, grid, in_specs, out_specs, out_shape, operands, scratch_shapes=(), semantics=None, jobs=(),
          after=()):
    in_specs, out_specs, out_shape = list(in_specs), list(out_specs), list(out_shape)
    scratch_shapes = list(scratch_shapes)
    if not jobs:
        n_real = len(in_specs)

        def ordered(*refs):
            body(*refs[:n_real], *refs[n_real + len(after):])

        outs = pl.pallas_call(
            ordered if after else body, name=name, grid=grid,
            in_specs=in_specs + [pl.BlockSpec(memory_space=pl.ANY)] * len(after), out_specs=out_specs,
            out_shape=out_shape, scratch_shapes=scratch_shapes, compiler_params=_params(semantics))(*operands, *after)
        return outs, []
    n_in, n_out, n_scratch = len(in_specs), len(out_specs), len(scratch_shapes)
    extra_in, extra_out, sems, aliases = [], [], [], {}
    for job in jobs:
        extra_in += job.ins
        for t in range(len(job.ios)):
            aliases[n_in + len(extra_in) + t] = n_out + len(extra_out) + t
        extra_in += job.ios
        extra_out += job.results()
        sems += job.sems

    def carried(*refs):
        x_in = refs[n_in:n_in + len(extra_in)]
        x_out = refs[n_in + len(extra_in) + n_out:n_in + len(extra_in) + n_out + len(extra_out)]
        x_sem = refs[len(refs) - len(sems):]
        views, i_in, i_out, i_sem = [], 0, 0, 0
        for job in jobs:
            data = list(x_in[i_in:i_in + len(job.ins)]) + list(x_out[i_out:i_out + len(job.results())])
            views.append((data, x_sem[i_sem:i_sem + len(job.sems)]))
            i_in += len(job.ins) + len(job.ios)
            i_out += len(job.results())
            i_sem += len(job.sems)
        steps = [pl.program_id(d) for d in range(len(grid))]

        @pl.when(functools.reduce(jnp.logical_and, [s == 0 for s in steps]))
        def _():
            for job, (data, sem) in zip(jobs, views):
                job.start(data, sem)

        body(*refs[:n_in], *refs[n_in + len(extra_in):n_in + len(extra_in) + n_out],
             *refs[len(refs) - len(sems) - n_scratch:len(refs) - len(sems)])

        @pl.when(functools.reduce(jnp.logical_and, [s == g - 1 for s, g in zip(steps, grid)]))
        def _():
            for job, (data, sem) in zip(jobs, views):
                job.finish(data, sem)

    hbm = pl.BlockSpec(memory_space=pl.ANY)
    res = pl.pallas_call(
        carried, name=name, grid=grid, in_specs=in_specs + [hbm] * len(extra_in),
        out_specs=out_specs + [hbm] * len(extra_out), out_shape=out_shape + extra_out,
        input_output_aliases=aliases, scratch_shapes=scratch_shapes + sems,
        compiler_params=_params(("arbitrary",) * len(grid)),
    )(*operands, *extra_in)
    carried_results, at = [], n_out
    for job in jobs:
        carried_results.append(list(res[at:at + len(job.results())]))
        at += len(job.results())
    return list(res[:n_out]), carried_results


def _run_jobs(jobs, *, name):
    first = jobs[0]
    n_in, n_io = len(first.ins), len(first.ios)
    out_shape = first.results()
    n_sems = [len(job.sems) for job in jobs]

    def body(*refs):
        data = list(refs[:n_in]) + list(refs[n_in + n_io:n_in + n_io + len(out_shape)])
        at = n_in + n_io + len(out_shape)
        for job, ns in zip(jobs, n_sems):
            job.start(data, refs[at:at + ns])
            job.finish(data, refs[at:at + ns])
            at += ns

    hbm = pl.BlockSpec(memory_space=pl.ANY)
    return pl.pallas_call(
        body, name=name, in_specs=[hbm] * (n_in + n_io), out_specs=[hbm] * len(out_shape), out_shape=out_shape,
        input_output_aliases={n_in + t: t for t in range(n_io)},
        scratch_shapes=[s for job in jobs for s in job.sems],
    )(*first.ins, *first.ios)


class _SemaphoreGrid:
    def __init__(self, refs, shape):
        self.refs, self.shape = list(refs), tuple(shape)

    @property
    def at(self):
        return self

    def __getitem__(self, index):
        index = index if isinstance(index, tuple) else (index,)
        flat = 0
        for i, extent in zip(index, self.shape):
            flat = flat * extent + i
        return self.refs[flat]


def _semaphore_grids(job, refs):
    grids, at = [], 0
    for sem in job.sems:
        count = math.prod(sem.shape)
        grids.append(_SemaphoreGrid(refs[at:at + count], sem.shape))
        at += count
    return grids


def _split_start(job, *, name):
    arrays = job.ins + job.ios + [lax.empty(s.shape, s.dtype) for s in job.outs]
    n, ns = len(arrays), sum(math.prod(sem.shape) for sem in job.sems)

    def body(*refs):
        job.start(list(refs[:n]), _semaphore_grids(job, refs[n:n + ns]))
        refs[-1][...] = jnp.zeros_like(refs[-1])

    hbm = pl.BlockSpec(memory_space=pltpu.HBM)
    res = pl.pallas_call(
        body, name=name,
        out_shape=(*[pltpu.SemaphoreType.DMA(())] * ns, *[pltpu.HBM(a.shape, a.dtype) for a in arrays],
                   jax.ShapeDtypeStruct((8, HEAD_DIM), F32)),
        in_specs=[hbm] * n,
        out_specs=(*[pl.BlockSpec(memory_space=pltpu.SEMAPHORE)] * ns, *[hbm] * n,
                   pl.BlockSpec(memory_space=pltpu.VMEM)),
        input_output_aliases={t: ns + t for t in range(n)},
        compiler_params=pltpu.CompilerParams(has_side_effects=pltpu.SideEffectType.DATAFLOW_SIDE_EFFECTING),
    )(*[pltpu.with_memory_space_constraint(a, pltpu.HBM) for a in arrays])
    return dict(job=job, sems=list(res[:ns]), arrays=list(res[ns:ns + n]), token=res[-1])


def _split_wait(started, after, *, name):
    job, arrays, sems = started["job"], started["arrays"], started["sems"]
    n, ns = len(arrays), len(sems)

    def body(*refs):
        job.finish(list(refs[:n]), _semaphore_grids(job, refs[n:n + ns]))

    hbm = pl.BlockSpec(memory_space=pltpu.HBM)
    return pl.pallas_call(
        body, name=name, out_shape=[pltpu.HBM(a.shape, a.dtype) for a in arrays],
        in_specs=[hbm] * n + [pl.BlockSpec(memory_space=pltpu.SEMAPHORE)] * ns
        + [pl.BlockSpec(memory_space=pl.ANY)] * len(after),
        out_specs=[hbm] * n, input_output_aliases={t: t for t in range(n)},
        compiler_params=pltpu.CompilerParams(has_side_effects=pltpu.SideEffectType.DATAFLOW_SIDE_EFFECTING),
    )(*arrays, *sems, *after)


def _dot(a, b, ta=False, tb=False):
    return lax.dot_general(a, b, (((0 if ta else 1,), (1 if tb else 0,)), ((), ())),
                           preferred_element_type=F32)


def _tile(n, want):
    t = min(n, want) // 128 * 128
    while n % t:
        t -= 128
    return t


def _a_spec(ta, tm, tk):
    return pl.BlockSpec((tk, tm), lambda i, j, k: (k, i)) if ta else pl.BlockSpec((tm, tk), lambda i, j, k: (i, k))


def _b_spec(tb, tk, tn):
    return pl.BlockSpec((tn, tk), lambda i, j, k: (j, k)) if tb else pl.BlockSpec((tk, tn), lambda i, j, k: (k, j))


def _accumulate(accs, nk, products, finish):
    if nk == 1:
        finish(*products())
        return
    k = pl.program_id(2)

    @pl.when(k == 0)
    def _():
        for acc, p in zip(accs, products()):
            acc[...] = p

    if nk > 2:
        @pl.when(jnp.logical_and(k > 0, k < nk - 1))
        def _():
            for acc, p in zip(accs, products()):
                acc[...] += p

    @pl.when(k == nk - 1)
    def _():
        finish(*[acc[...] + p for acc, p in zip(accs, products())])


def _matmul(a, b, *, name, ta=False, tb=False, out_dtype=F32, residual=None, tm=1024, tn=1024, tk=2048, jobs=()):
    m, kdim = (a.shape[1], a.shape[0]) if ta else a.shape
    n = b.shape[0] if tb else b.shape[1]
    tm, tn, tk = _tile(m, tm), _tile(n, tn), _tile(kdim, tk)
    nk = kdim // tk

    def body(*refs):
        a_ref, b_ref = refs[:2]
        r_ref = refs[2] if residual is not None else None
        o_ref = refs[-1] if nk == 1 else refs[-2]

        def finish(total):
            if residual is not None:
                total = total + r_ref[...]
            o_ref[...] = total.astype(out_dtype)

        _accumulate(refs[-1:] if nk > 1 else (), nk, lambda: (_dot(a_ref[...], b_ref[...], ta, tb),), finish)

    o_spec = pl.BlockSpec((tm, tn), lambda i, j, k: (i, j))
    in_specs = [_a_spec(ta, tm, tk), _b_spec(tb, tk, tn)]
    operands = [a, b]
    if residual is not None:
        in_specs.append(o_spec)
        operands.append(residual)
    (out,), carried = _call(
        body, name=name, grid=(m // tm, n // tn, nk), in_specs=in_specs, out_specs=[o_spec],
        out_shape=[jax.ShapeDtypeStruct((m, n), out_dtype)], operands=operands,
        scratch_shapes=[pltpu.VMEM((tm, tn), F32)] * (nk > 1),
        semantics=("parallel", "parallel", "arbitrary"), jobs=jobs)
    return (out, carried) if jobs else out


def _matmul_pieces_nt(pieces, b, *, name, tm=512, tn=1024, jobs=(), after=()):
    m, kp = pieces[0].shape
    n = b.shape[0]
    tm, tn = _tile(m, tm), _tile(n, tn)
    count = len(pieces)

    def body(*refs):
        b_ref, o_ref = refs[count], refs[count + 1]
        total = _dot(refs[0][...], b_ref[:, pl.ds(0, kp)], tb=True)
        for p in range(1, count):
            total = total + _dot(refs[p][...], b_ref[:, pl.ds(p * kp, kp)], tb=True)
        o_ref[...] = total

    piece = pl.BlockSpec((tm, kp), lambda j, i: (i, 0))
    (out,), carried = _call(
        body, name=name, grid=(n // tn, m // tm),
        in_specs=[piece] * count + [pl.BlockSpec((tn, count * kp), lambda j, i: (j, 0))],
        out_specs=[pl.BlockSpec((tm, tn), lambda j, i: (i, j))],
        out_shape=[jax.ShapeDtypeStruct((m, n), F32)], operands=[*pieces, b],
        semantics=("parallel", "parallel"), jobs=jobs, after=after)
    return (out, carried) if jobs else out


def _weight_grad_pieces(a, pieces, *, name):
    tokens, m = a.shape
    np_ = pieces[0].shape[1]
    tm = _tile(m, 2048)
    tn = _tile(np_, 512)
    nb = np_ // tn
    out = None
    for p, piece in enumerate(pieces):
        def body(*refs):
            refs[-1][...] = _dot(refs[0][...], refs[1][...], ta=True)

        in_specs = [pl.BlockSpec((tokens, tm), lambda i, j: (0, i)), pl.BlockSpec((tokens, tn), lambda i, j: (0, j))]
        operands = [a, piece]
        if out is not None:
            in_specs.append(pl.BlockSpec(memory_space=pl.ANY))
            operands.append(out)
        out = pl.pallas_call(
            body, name="%s_%d" % (name, p), grid=(m // tm, nb), in_specs=in_specs,
            out_specs=pl.BlockSpec((tm, tn), lambda i, j, p=p: (i, p * nb + j)),
            out_shape=jax.ShapeDtypeStruct((m, len(pieces) * np_), F32),
            input_output_aliases={2: 0} if len(operands) == 3 else {},
            compiler_params=_params(("parallel", "parallel")),
        )(*operands)
    return out


def _weight_grad(a, g, *, name, jobs=()):
    tokens, m = a.shape
    tm = 1024 if m % 1024 == 0 else _tile(m, 1408)
    return _matmul(a, g, name=name, ta=True, tm=tm, tn=512, tk=tokens, jobs=jobs)


def _swiglu_fwd(n2, w_gate, w_up, *, tm=1024, tn=512, tk=2048, jobs=()):
    m, kdim = n2.shape
    n = w_gate.shape[1]
    tm, tn, tk = _tile(m, tm), _tile(n, tn), _tile(kdim, tk)
    nk = kdim // tk

    def body(a_ref, g_ref, u_ref, gate_ref, up_ref, act_ref, *acc):
        def products():
            a = a_ref[...]
            return _dot(a, g_ref[...]), _dot(a, u_ref[...])

        def finish(g, u):
            gate_ref[...] = g.astype(BF16)
            up_ref[...] = u.astype(BF16)
            act_ref[...] = (g * _sigmoid(g) * u).astype(BF16)

        _accumulate(acc, nk, products, finish)

    o_spec = pl.BlockSpec((tm, tn), lambda i, j, k: (i, j))
    o_shape = jax.ShapeDtypeStruct((m, n), BF16)
    return _call(
        body, name="swiglu_fwd", grid=(m // tm, n // tn, nk),
        in_specs=[_a_spec(False, tm, tk), _b_spec(False, tk, tn), _b_spec(False, tk, tn)],
        out_specs=[o_spec] * 3, out_shape=[o_shape] * 3, operands=[n2, w_gate, w_up],
        scratch_shapes=[pltpu.VMEM((tm, tn), F32)] * (2 * (nk > 1)),
        semantics=("parallel", "parallel", "arbitrary"), jobs=jobs)


def _swiglu_bwd_act(dh2, w_down, gate, up, *, tm=1024, tn=512, tk=2048):
    m, kdim = dh2.shape
    n = w_down.shape[0]
    tm, tn, tk = _tile(m, tm), _tile(n, tn), _tile(kdim, tk)
    nk = kdim // tk

    sub = _tile(tn, 256)

    def body(a_ref, b_ref, gate_ref, up_ref, dgate_ref, dup_ref, *acc):
        def finish(dact, cols=slice(None)):
            g = gate_ref[:, cols].astype(F32)
            u = up_ref[:, cols].astype(F32)
            sg = _sigmoid(g)
            dup_ref[:, cols] = (dact * g * sg).astype(BF16)
            dgate_ref[:, cols] = (dact * u * sg * (1.0 + g * (1.0 - sg))).astype(BF16)

        if nk == 1:
            a = a_ref[...]
            for c in range(tn // sub):
                cols = pl.ds(c * sub, sub)
                finish(_dot(a, b_ref[cols, :], tb=True), cols)
        else:
            _accumulate(acc, nk, lambda: (_dot(a_ref[...], b_ref[...], tb=True),), finish)

    o_spec = pl.BlockSpec((tm, tn), lambda i, j, k: (i, j))
    o_shape = jax.ShapeDtypeStruct((m, n), BF16)
    return pl.pallas_call(
        body, name="swiglu_bwd_act", grid=(m // tm, n // tn, nk),
        in_specs=[_a_spec(False, tm, tk), _b_spec(True, tk, tn), o_spec, o_spec],
        out_specs=[o_spec] * 2, out_shape=[o_shape] * 2,
        scratch_shapes=[pltpu.VMEM((tm, tn), F32)] * (nk > 1),
        compiler_params=_params(("parallel", "parallel", "arbitrary")),
    )(dh2, w_down, gate, up)


def _swiglu_bwd_in(dgate, dup, w_gate, w_up, *, tm=1024, tn=1024, tk=1408, jobs=()):
    m, kdim = dgate.shape
    n = w_gate.shape[0]
    tm, tn, tk = _tile(m, tm), _tile(n, tn), _tile(kdim, tk)
    nk = kdim // tk

    def body(a1_ref, a2_ref, b1_ref, b2_ref, o_ref, *acc):
        def product():
            return (_dot(a1_ref[...], b1_ref[...], tb=True) + _dot(a2_ref[...], b2_ref[...], tb=True),)

        def finish(total):
            o_ref[...] = total

        _accumulate(acc, nk, product, finish)

    a_spec, b_spec = _a_spec(False, tm, tk), _b_spec(True, tk, tn)
    (out,), carried = _call(
        body, name="swiglu_bwd_in", grid=(m // tm, n // tn, nk),
        in_specs=[a_spec, a_spec, b_spec, b_spec],
        out_specs=[pl.BlockSpec((tm, tn), lambda i, j, k: (i, j))],
        out_shape=[jax.ShapeDtypeStruct((m, n), F32)], operands=[dgate, dup, w_gate, w_up],
        scratch_shapes=[pltpu.VMEM((tm, tn), F32)] * (nk > 1),
        semantics=("parallel", "parallel", "arbitrary"), jobs=jobs)
    return out, carried


def _row_block(rows, cols):
    tr = min(rows, max(16, ELEM_BLOCK_BYTES // (4 * cols) // 16 * 16))
    while rows % tr:
        tr -= 16
    return tr


def _rmsnorm_fwd(x, g, *, name, after=None):
    s, d = x.shape
    tr = _row_block(s, d)

    def body(x_ref, g_ref, *rest):
        xv = x_ref[...]
        r = lax.rsqrt(jnp.mean(xv * xv, axis=-1, keepdims=True) + EPS)
        rest[-1][...] = (xv * r * g_ref[...]).astype(BF16)

    row = pl.BlockSpec((tr, d), lambda i: (i, 0))
    in_specs = [row, pl.BlockSpec((1, d), lambda i: (0, 0))]
    operands = [x, g]
    if after is not None:
        in_specs.append(pl.BlockSpec(after.shape, lambda i: (0, 0)))
        operands.append(after)
    return pl.pallas_call(
        body, name=name, grid=(s // tr,), in_specs=in_specs,
        out_specs=row, out_shape=jax.ShapeDtypeStruct((s, d), BF16),
        compiler_params=_params(("parallel",)),
    )(*operands)


def _rmsnorm_bwd_rows(xv, gv, dy):
    r = lax.rsqrt(jnp.mean(xv * xv, axis=-1, keepdims=True) + EPS)
    xhat = xv * r
    dxh = dy * gv
    dx = r * (dxh - xhat * jnp.mean(dxh * xhat, axis=-1, keepdims=True))
    return dx, dy * xhat


def _rmsnorm_bwd(dn, x, g, skip, *, name, after=()):
    s, d = x.shape
    tr = _row_block(s, d)

    def body(dn_ref, x_ref, g_ref, skip_ref, *rest):
        dx_ref, dxb_ref, dg_ref = rest[len(after):]
        dx, dgr = _rmsnorm_bwd_rows(x_ref[...], g_ref[...], dn_ref[...])
        dx = dx + skip_ref[...]
        dx_ref[...] = dx
        dxb_ref[...] = dx.astype(BF16)

        @pl.when(pl.program_id(0) == 0)
        def _():
            dg_ref[...] = jnp.zeros_like(dg_ref)

        dg_ref[...] += jnp.sum(dgr, axis=0, keepdims=True)

    row = pl.BlockSpec((tr, d), lambda i: (i, 0))
    vec = pl.BlockSpec((1, d), lambda i: (0, 0))
    return pl.pallas_call(
        body, name=name, grid=(s // tr,),
        in_specs=[row, row, vec, row] + [pl.BlockSpec(memory_space=pl.ANY)] * len(after),
        out_specs=[row, row, vec],
        out_shape=[jax.ShapeDtypeStruct((s, d), F32), jax.ShapeDtypeStruct((s, d), BF16),
                   jax.ShapeDtypeStruct((1, d), F32)],
        compiler_params=_params(("arbitrary",)),
    )(dn, x, g, skip, *after)


def _loss_head(h2, g, target):
    s, d = h2.shape
    tr = _row_block(s, d)

    def body(h_ref, g_ref, t_ref, dh_ref, dhb_ref, dg_ref, loss_ref):
        hv = h_ref[...]
        gv = g_ref[...]
        r = lax.rsqrt(jnp.mean(hv * hv, axis=-1, keepdims=True) + EPS)
        err = hv * r * gv - t_ref[...]
        dx, dgr = _rmsnorm_bwd_rows(hv, gv, err * (1.0 / d))
        dh_ref[...] = dx
        dhb_ref[...] = dx.astype(BF16)

        @pl.when(pl.program_id(0) == 0)
        def _():
            dg_ref[...] = jnp.zeros_like(dg_ref)
            loss_ref[...] = jnp.zeros_like(loss_ref)

        dg_ref[...] += jnp.sum(dgr, axis=0, keepdims=True)
        row_loss = jnp.mean(err * err, axis=-1, keepdims=True)
        loss_ref[...] += 0.5 * jnp.sum(row_loss, axis=0, keepdims=True)

    row = pl.BlockSpec((tr, d), lambda i: (i, 0))
    vec = pl.BlockSpec((1, d), lambda i: (0, 0))
    one = pl.BlockSpec((1, 1), lambda i: (0, 0))
    return pl.pallas_call(
        body, name="loss_head", grid=(s // tr,), in_specs=[row, vec, row],
        out_specs=[row, row, vec, one],
        out_shape=[jax.ShapeDtypeStruct((s, d), F32), jax.ShapeDtypeStruct((s, d), BF16),
                   jax.ShapeDtypeStruct((1, d), F32), jax.ShapeDtypeStruct((1, 1), F32)],
        compiler_params=_params(("arbitrary",)),
    )(h2, g, target)


def _attention_bias_tables():
    k = np.arange(-ATT_KB, ATT_KB + 1)[:, None, None]
    delta = k * ATT_BLOCK + np.arange(ATT_BLOCK)[None, None, :] - np.arange(ATT_BLOCK)[None, :, None]
    dist = np.abs(delta)
    count = np.zeros(delta.shape, np.int32)
    for window, dilation in DILATED_PATTERNS:
        count += (delta % dilation == 0) & (dist <= min(window // 2, ATT_NEAR))
    logc = np.where(count > 0, np.log(np.maximum(count, 1)), MASKED)
    return dist.astype(np.float32), logc.astype(np.float32)


def _far_bias_tables(per_class):
    steps = np.abs(np.arange(per_class)[:, None] - np.arange(per_class)[None, :]) * ATT_CLASSES
    valid = (steps > ATT_NEAR) & (steps <= ATT_REACH)
    return steps.astype(np.float32), np.where(valid, 0.0, MASKED).astype(np.float32)


def _to_classes(x):
    s, cols = x.shape
    return jnp.reshape(jnp.transpose(jnp.reshape(x, (s // ATT_CLASSES, ATT_CLASSES, cols)), (1, 0, 2)), (s, cols))


def _from_classes(x):
    s, cols = x.shape
    return jnp.reshape(jnp.transpose(jnp.reshape(x, (ATT_CLASSES, s // ATT_CLASSES, cols)), (1, 0, 2)), (s, cols))


def _head_bias(bias_ref, slope, dist_ref, logc_ref):
    for kk in range(ATT_WINDOW):
        bias_ref[kk] = logc_ref[kk] - slope * dist_ref[kk]
    bias_ref[ATT_WINDOW] = jnp.full((ATT_BLOCK, ATT_BLOCK), MASKED, F32)


def _window_start(i, nq, nwin):
    return jnp.clip(i - ATT_KB, 0, nq - nwin)


def _window_block(j, i):
    rows = pl.ds(pl.multiple_of(j * ATT_BLOCK, ATT_BLOCK), ATT_BLOCK)
    kk = j - i + ATT_KB
    return rows, jnp.where(jnp.logical_and(kk >= 0, kk < ATT_WINDOW), kk, ATT_WINDOW)


def _attention_far_fwd(qkv, slopes, n_heads, jobs=()):
    s = qkv.shape[0]
    per_class = s // ATT_CLASSES
    scale = HEAD_DIM ** -0.5
    dist, logc = _far_bias_tables(per_class)

    def body(slope_ref, q_ref, k_ref, v_ref, dist_ref, logc_ref, o_ref, lse_ref):
        bias = logc_ref[...] - slope_ref[pl.program_id(0)] * dist_ref[...]
        for a in range(ATT_FAR_GROUP):
            rows = pl.ds(a * per_class, per_class)
            sc = _dot(q_ref[rows, :], k_ref[rows, :], tb=True) * scale + bias
            m = jnp.maximum(jnp.max(sc, axis=-1, keepdims=True), ROW_MAX_INIT)
            p = jnp.exp(sc - m)
            l = jnp.maximum(jnp.sum(p, axis=-1, keepdims=True), 1e-30)
            o_ref[rows, :] = (_dot(p.astype(BF16), v_ref[rows, :]) / l).astype(BF16)
            lse_ref[rows, :] = jnp.broadcast_to(m + jnp.log(l), (per_class, HEAD_DIM))

    hh = n_heads
    blk = pl.BlockSpec((ATT_FAR_GROUP * per_class, HEAD_DIM), lambda h, r: (r, h))
    table = pl.BlockSpec(dist.shape, lambda h, r: (0, 0))
    return _call(
        body, name="attention_far_fwd", grid=(hh, ATT_CLASSES // ATT_FAR_GROUP),
        in_specs=[pl.BlockSpec(memory_space=pltpu.SMEM), blk,
                  pl.BlockSpec((ATT_FAR_GROUP * per_class, HEAD_DIM), lambda h, r: (r, hh + h)),
                  pl.BlockSpec((ATT_FAR_GROUP * per_class, HEAD_DIM), lambda h, r: (r, 2 * hh + h)), table, table],
        out_specs=[blk, blk],
        out_shape=[jax.ShapeDtypeStruct((s, hh * HEAD_DIM), BF16), jax.ShapeDtypeStruct((s, hh * HEAD_DIM), F32)],
        operands=[slopes, qkv, qkv, qkv, jnp.asarray(dist), jnp.asarray(logc)],
        semantics=("parallel", "parallel"), jobs=jobs)


def _attention_fwd(proj, slopes, far_out, far_lse, n_heads, jobs=()):
    s = proj.shape[0]
    nq = s // ATT_BLOCK
    scale = HEAD_DIM ** -0.5
    dist, logc = _attention_bias_tables()

    nwin = min(ATT_WINDOW, nq)

    group = math.gcd(ATT_NEAR_GROUP, nq)

    def body(slope_ref, q_ref, k_ref, v_ref, fo_ref, fl_ref, dist_ref, logc_ref, o_ref, lse_ref, bias_ref, s_ref):
        h, step = pl.program_id(0), pl.program_id(1)

        @pl.when(step == 0)
        def _():
            _head_bias(bias_ref, slope_ref[h], dist_ref, logc_ref)

        for a in range(group):
            i = step * group + a
            mine = pl.ds(a * ATT_BLOCK, ATT_BLOCK)
            q = q_ref[mine, :]
            first = _window_start(i, nq, nwin)
            m = jnp.full((ATT_BLOCK, 1), ROW_MAX_INIT, F32)
            for b in range(nwin):
                rows, kk = _window_block(first + b, i)
                sc = _dot(q, k_ref[rows, :], tb=True) * scale + bias_ref[kk]
                s_ref[a * nwin + b] = sc
                m = jnp.maximum(m, jnp.max(sc, axis=-1, keepdims=True))
            l = jnp.zeros((ATT_BLOCK, 1), F32)
            acc = jnp.zeros((ATT_BLOCK, HEAD_DIM), F32)
            for b in range(nwin):
                rows, _ = _window_block(first + b, i)
                p = jnp.exp(s_ref[a * nwin + b] - m)
                l = l + jnp.sum(p, axis=-1, keepdims=True)
                acc = acc + _dot(p.astype(BF16), v_ref[rows, :])
            near_lse = m + jnp.log(l)
            far_lse_col = fl_ref[mine, :1]
            lse = jnp.maximum(near_lse, far_lse_col)
            lse = lse + jnp.log(jnp.exp(near_lse - lse) + jnp.exp(far_lse_col - lse))
            o_ref[mine, :] = (acc * (jnp.exp(near_lse - lse) / l)
                              + fo_ref[mine, :].astype(F32) * jnp.exp(far_lse_col - lse)).astype(BF16)
            lse_ref[mine, :] = jnp.broadcast_to(lse, (ATT_BLOCK, HEAD_DIM))

    hh = n_heads
    blk = pl.BlockSpec((group * ATT_BLOCK, HEAD_DIM), lambda h, i: (i, h))
    table = pl.BlockSpec(dist.shape, lambda h, i: (0, 0, 0))
    return _call(
        body, name="attention_fwd", grid=(hh, nq // group),
        in_specs=[pl.BlockSpec(memory_space=pltpu.SMEM), blk,
                  pl.BlockSpec((s, HEAD_DIM), lambda h, i: (0, hh + h)),
                  pl.BlockSpec((s, HEAD_DIM), lambda h, i: (0, 2 * hh + h)), blk, blk, table, table],
        out_specs=[blk, blk],
        out_shape=[jax.ShapeDtypeStruct((s, hh * HEAD_DIM), BF16), jax.ShapeDtypeStruct((s, hh * HEAD_DIM), F32)],
        operands=[slopes, proj, proj, proj, far_out, far_lse, jnp.asarray(dist), jnp.asarray(logc)],
        scratch_shapes=[pltpu.VMEM((ATT_WINDOW + 1, ATT_BLOCK, ATT_BLOCK), F32),
                        pltpu.VMEM((group * nwin, ATT_BLOCK, ATT_BLOCK), F32)],
        semantics=("parallel", "arbitrary"), jobs=jobs)


def _attention_far_bwd(qkv, slopes, out, dout, lse, n_heads):
    s = qkv.shape[0]
    per_class = s // ATT_CLASSES
    scale = HEAD_DIM ** -0.5
    dist, logc = _far_bias_tables(per_class)

    def body(slope_ref, q_ref, k_ref, v_ref, o_ref, do_ref, lse_ref, dist_ref, logc_ref, dq_ref, dk_ref, dv_ref):
        bias = logc_ref[...] - slope_ref[pl.program_id(0)] * dist_ref[...]
        for a in range(ATT_FAR_GROUP):
            rows = pl.ds(a * per_class, per_class)
            q, k, do = q_ref[rows, :], k_ref[rows, :], do_ref[rows, :]
            delta = jnp.sum(do.astype(F32) * o_ref[rows, :].astype(F32), axis=-1, keepdims=True)
            p = jnp.exp(_dot(q, k, tb=True) * scale + bias - lse_ref[rows, :1])
            dv_ref[rows, :] = _dot(p.astype(BF16), do, ta=True).astype(BF16)
            ds = (p * (_dot(do, v_ref[rows, :], tb=True) - delta) * scale).astype(BF16)
            dk_ref[rows, :] = _dot(ds, q, ta=True).astype(BF16)
            dq_ref[rows, :] = _dot(ds, k).astype(BF16)

    hh = n_heads
    blk = pl.BlockSpec((ATT_FAR_GROUP * per_class, HEAD_DIM), lambda h, r: (r, h))
    table = pl.BlockSpec(dist.shape, lambda h, r: (0, 0))
    o_shape = jax.ShapeDtypeStruct((s, hh * HEAD_DIM), BF16)
    return pl.pallas_call(
        body, name="attention_far_bwd", grid=(hh, ATT_CLASSES // ATT_FAR_GROUP),
        in_specs=[pl.BlockSpec(memory_space=pltpu.SMEM), blk,
                  pl.BlockSpec((ATT_FAR_GROUP * per_class, HEAD_DIM), lambda h, r: (r, hh + h)),
                  pl.BlockSpec((ATT_FAR_GROUP * per_class, HEAD_DIM), lambda h, r: (r, 2 * hh + h)),
                  blk, blk, blk, table, table],
        out_specs=[blk] * 3, out_shape=[o_shape] * 3,
        compiler_params=_params(("parallel", "parallel")),
    )(slopes, qkv, qkv, qkv, out, dout, lse, jnp.asarray(dist), jnp.asarray(logc))


def _attention_bwd(proj, slopes, out, lse, dmixed, far_grads, n_heads, jobs=()):
    s = proj.shape[0]
    nq = s // ATT_BLOCK
    scale = HEAD_DIM ** -0.5
    dist, logc = _attention_bias_tables()

    nwin = min(ATT_WINDOW, nq)
    group = math.gcd(ATT_NEAR_GROUP, nq)

    def body(slope_ref, q_ref, k_ref, v_ref, o_ref, do_ref, lse_ref, fdq_ref, fdk_ref, fdv_ref, dist_ref, logc_ref,
             dq_ref, dk_ref, dv_ref, dk_acc, dv_acc, bias_ref):
        h, step = pl.program_id(0), pl.program_id(1)

        @pl.when(step == 0)
        def _():
            dk_acc[...] = jnp.zeros_like(dk_acc)
            dv_acc[...] = jnp.zeros_like(dv_acc)
            _head_bias(bias_ref, slope_ref[h], dist_ref, logc_ref)

        for a in range(group):
            i = step * group + a
            mine = pl.ds(a * ATT_BLOCK, ATT_BLOCK)
            q = q_ref[mine, :]
            do = do_ref[mine, :]
            lse_col = lse_ref[mine, :1]
            delta = jnp.sum(do.astype(F32) * o_ref[mine, :].astype(F32), axis=-1, keepdims=True)
            first = _window_start(i, nq, nwin)
            dq = jnp.zeros((ATT_BLOCK, HEAD_DIM), F32)
            for b in range(nwin):
                rows, kk = _window_block(first + b, i)
                kj = k_ref[rows, :]
                vj = v_ref[rows, :]
                p = jnp.exp(_dot(q, kj, tb=True) * scale + bias_ref[kk] - lse_col)
                dv_acc[rows, :] += _dot(p.astype(BF16), do, ta=True)
                dp = _dot(do, vj, tb=True)
                ds = (p * (dp - delta) * scale).astype(BF16)
                dk_acc[rows, :] += _dot(ds, q, ta=True)
                dq = dq + _dot(ds, kj)
            dq_ref[mine, :] = (dq + fdq_ref[mine, :].astype(F32)).astype(BF16)

        @pl.when(step == nq // group - 1)
        def _():
            dk_ref[...] = (dk_acc[...] + fdk_ref[...].astype(F32)).astype(BF16)
            dv_ref[...] = (dv_acc[...] + fdv_ref[...].astype(F32)).astype(BF16)

    hh = n_heads
    blk = pl.BlockSpec((group * ATT_BLOCK, HEAD_DIM), lambda h, i: (i, h))
    col = pl.BlockSpec((s, HEAD_DIM), lambda h, i: (0, h))
    table = pl.BlockSpec(dist.shape, lambda h, i: (0, 0, 0))
    o_shape = jax.ShapeDtypeStruct((s, hh * HEAD_DIM), BF16)
    return _call(
        body, name="attention_bwd", grid=(hh, nq // group),
        in_specs=[pl.BlockSpec(memory_space=pltpu.SMEM), blk,
                  pl.BlockSpec((s, HEAD_DIM), lambda h, i: (0, hh + h)),
                  pl.BlockSpec((s, HEAD_DIM), lambda h, i: (0, 2 * hh + h)),
                  blk, blk, blk, blk, col, col, table, table],
        out_specs=[blk, col, col], out_shape=[o_shape] * 3,
        operands=[slopes, proj, proj, proj, out, dmixed, lse, *far_grads, jnp.asarray(dist), jnp.asarray(logc)],
        scratch_shapes=[pltpu.VMEM((s, HEAD_DIM), F32)] * 2
        + [pltpu.VMEM((ATT_WINDOW + 1, ATT_BLOCK, ATT_BLOCK), F32)],
        semantics=("parallel", "arbitrary"), jobs=jobs)


def _ret_decays(lgc, lga, strict_c, strict_a):
    c = RET_CHUNK
    rel = (lax.broadcasted_iota(jnp.int32, (c, c), 0) - lax.broadcasted_iota(jnp.int32, (c, c), 1)).astype(F32)
    in_c = (rel > 0) if strict_c else (rel >= 0)
    in_a = (rel < 0) if strict_a else (rel <= 0)
    mask = (jnp.where(in_c, jnp.exp(lgc * jnp.maximum(rel, 0.0)), 0.0)
            + jnp.where(in_a, jnp.exp(lga * jnp.maximum(-rel, 0.0)), 0.0))
    idx = lax.broadcasted_iota(jnp.int32, (c, 1), 0).astype(F32)
    ones = jnp.ones((1, HEAD_DIM), F32)
    dec = dict(
        rel=rel, mask=mask, idx=idx,
        a_c=jnp.exp(lgc * (idx + 1.0)), b_c=jnp.exp(lgc * (c - 1.0 - idx)), chunk_c=jnp.exp(ones * (lgc * c)),
        a_a=jnp.exp(lga * (c - idx)), b_a=jnp.exp(lga * idx), chunk_a=jnp.exp(ones * (lga * c)),
    )
    return dec


def _scaled(x, col):
    return (x.astype(F32) * col).astype(BF16)


def _chunk_rows(i):
    return pl.ds(pl.multiple_of(i * RET_CHUNK, RET_CHUNK), RET_CHUNK)


def _chunk_loop(nc, step, init, unroll=RET_UNROLL):
    group = math.gcd(nc, unroll)

    def trip(t, carry):
        for u in range(group):
            carry = step(t * group + u, carry)
        return carry

    return lax.fori_loop(0, nc // group, trip, init)


def _retention(a, b, c, lg_c, lg_a, *, strict_c, strict_a, scale, n_heads, name, gate=None, norm_w=None, jobs=()):
    s = a[0].shape[0]
    nc = s // RET_CHUNK
    epilogue = gate is not None

    def body(*refs):
        lgc_ref, lga_ref, a_ref, b_ref, c_ref = refs[:5]
        if epilogue:
            g_ref, w_ref, o_ref, mix_ref, sa_ref = refs[5:]
        else:
            o_ref, sa_ref = refs[5:]
        h = pl.program_id(0)
        dec = _ret_decays(lgc_ref[h], lga_ref[h], strict_c, strict_a)

        def reverse(t, state):
            i = nc - 1 - t
            sa_ref[i] = state.astype(BF16)
            rows = _chunk_rows(i)
            return state * dec["chunk_a"] + _dot(_scaled(b_ref[rows, :], dec["b_a"]), c_ref[rows, :], ta=True)

        _chunk_loop(nc, reverse, jnp.zeros((HEAD_DIM, HEAD_DIM), F32))

        def forward(i, state):
            rows = _chunk_rows(i)
            ai, bi, ci = a_ref[rows, :], b_ref[rows, :], c_ref[rows, :]
            inner = (_dot(ai, bi, tb=True) * dec["mask"]).astype(BF16)
            out = (_dot(inner, ci) + _dot(_scaled(ai, dec["a_c"]), state.astype(BF16))
                   + _dot(_scaled(ai, dec["a_a"]), sa_ref[i])) * scale
            o_ref[rows, :] = out.astype(BF16)
            if epilogue:
                r = lax.rsqrt(jnp.mean(out * out, axis=-1, keepdims=True) + EPS)
                g = g_ref[rows, :].astype(F32)
                mix_ref[rows, :] = (out * r * w_ref[...] * (g * _sigmoid(g))).astype(BF16)
            return state * dec["chunk_c"] + _dot(_scaled(bi, dec["b_c"]), ci, ta=True)

        _chunk_loop(nc, forward, jnp.zeros((HEAD_DIM, HEAD_DIM), F32))

    def col(first):
        return pl.BlockSpec((s, HEAD_DIM), lambda h: (0, first + h))

    smem = pl.BlockSpec(memory_space=pltpu.SMEM)
    in_specs = [smem, smem, col(a[1]), col(b[1]), col(c[1])]
    operands = [lg_c, lg_a, a[0], b[0], c[0]]
    o_shape = jax.ShapeDtypeStruct((s, n_heads * HEAD_DIM), BF16)
    out_specs, out_shape = [col(0)], [o_shape]
    if epilogue:
        in_specs += [col(gate[1]), pl.BlockSpec((1, HEAD_DIM), lambda h: (0, h))]
        operands += [gate[0], norm_w]
        out_specs, out_shape = [col(0)] * 2, [o_shape] * 2
    res, carried = _call(
        body, name=name, grid=(n_heads,), in_specs=in_specs, out_specs=out_specs, out_shape=out_shape,
        operands=operands, scratch_shapes=[pltpu.VMEM((nc, HEAD_DIM, HEAD_DIM), BF16)],
        semantics=("parallel",), jobs=jobs)
    res = res if epilogue else res[0]
    return (res, carried) if jobs else res


def _retention_decay_grads(a, b, c, e, lg_c, lg_a, *, scale, n_heads):
    s = a[0].shape[0]
    nc = s // RET_CHUNK
    cf = float(RET_CHUNK)

    def body(lgc_ref, lga_ref, a_ref, b_ref, c_ref, e_ref, gc_ref, ga_ref, sa_ref, ta_ref):
        h = pl.program_id(0)
        lgc, lga = lgc_ref[h], lga_ref[h]
        dec = _ret_decays(lgc, lga, True, True)
        rel, idx = dec["rel"], dec["idx"]
        w_c = jnp.where(rel > 0, rel * jnp.exp(lgc * jnp.maximum(rel, 0.0)), 0.0)
        w_a = jnp.where(rel < 0, -rel * jnp.exp(lga * jnp.maximum(-rel, 0.0)), 0.0)
        zero = jnp.zeros((HEAD_DIM, HEAD_DIM), F32)

        def reverse(t, carry):
            st, dst = carry
            i = nc - 1 - t
            sa_ref[i] = st.astype(BF16)
            ta_ref[i] = dst.astype(BF16)
            rows = _chunk_rows(i)
            bi, ci = b_ref[rows, :], c_ref[rows, :]
            st_new = st * dec["chunk_a"] + _dot(_scaled(bi, dec["b_a"]), ci, ta=True)
            dst_new = (cf * st + dst) * dec["chunk_a"] + _dot(_scaled(bi, idx * dec["b_a"]), ci, ta=True)
            return st_new, dst_new

        _chunk_loop(nc, reverse, (zero, zero))

        def forward(i, carry):
            st, dst, acc_c, acc_a = carry
            rows = _chunk_rows(i)
            ai, bi, ci = a_ref[rows, :], b_ref[rows, :], c_ref[rows, :]
            ev = e_ref[rows, :].astype(F32)
            pg = _dot(ai, bi, tb=True) * _dot(e_ref[rows, :], ci, tb=True)
            a_c, a_a = _scaled(ai, dec["a_c"]), _scaled(ai, dec["a_a"])
            inter_c = _dot(a_c, st.astype(BF16)) * (idx + 1.0) + _dot(a_c, dst.astype(BF16))
            inter_a = _dot(a_a, sa_ref[i]) * (cf - idx) + _dot(a_a, ta_ref[i])
            acc_c = acc_c + jnp.sum(pg * w_c, axis=0, keepdims=True) + jnp.sum(inter_c * ev, axis=0, keepdims=True)
            acc_a = acc_a + jnp.sum(pg * w_a, axis=0, keepdims=True) + jnp.sum(inter_a * ev, axis=0, keepdims=True)
            st_new = st * dec["chunk_c"] + _dot(_scaled(bi, dec["b_c"]), ci, ta=True)
            dst_new = ((cf * st + dst) * dec["chunk_c"]
                       + _dot(_scaled(bi, (cf - 1.0 - idx) * dec["b_c"]), ci, ta=True))
            return st_new, dst_new, acc_c, acc_a

        row = jnp.zeros((1, HEAD_DIM), F32)
        _, _, acc_c, acc_a = _chunk_loop(nc, forward, (zero, zero, row, row))
        gc_ref[...] = jnp.broadcast_to(jnp.sum(acc_c, axis=-1, keepdims=True) * scale, gc_ref.shape)
        ga_ref[...] = jnp.broadcast_to(jnp.sum(acc_a, axis=-1, keepdims=True) * scale, ga_ref.shape)

    def col(first):
        return pl.BlockSpec((s, HEAD_DIM), lambda h: (0, first + h))

    smem = pl.BlockSpec(memory_space=pltpu.SMEM)
    o_spec = pl.BlockSpec((1, 8, HEAD_DIM), lambda h: (h, 0, 0))
    o_shape = jax.ShapeDtypeStruct((n_heads, 8, HEAD_DIM), F32)
    gc, ga = pl.pallas_call(
        body, name="retention_decay_grads", grid=(n_heads,),
        in_specs=[smem, smem, col(a[1]), col(b[1]), col(c[1]), col(e[1])],
        out_specs=[o_spec] * 2, out_shape=[o_shape] * 2,
        scratch_shapes=[pltpu.VMEM((nc, HEAD_DIM, HEAD_DIM), BF16)] * 2,
        compiler_params=_params(("parallel",)),
    )(lg_c, lg_a, a[0], b[0], c[0], e[0])
    return gc[:, 0, 0], ga[:, 0, 0]


def _ret_gate_bwd(dmixed, first_col, out, proj, gate_col, norm_w, n_heads):
    s = out.shape[0]
    tr = _row_block(s, 8 * HEAD_DIM)

    def body(dm_ref, o_ref, g_ref, w_ref, do_ref, dg_ref, dw_ref):
        dm = dm_ref[...].astype(F32)
        ov = o_ref[...].astype(F32)
        g = g_ref[...].astype(F32)
        w = w_ref[...]
        r = lax.rsqrt(jnp.mean(ov * ov, axis=-1, keepdims=True) + EPS)
        ohat = ov * r
        sg = _sigmoid(g)
        silu = g * sg
        dg_ref[...] = (dm * ohat * w * sg * (1.0 + g * (1.0 - sg))).astype(BF16)
        dohat = dm * w * silu
        do_ref[...] = (r * (dohat - ohat * jnp.mean(dohat * ohat, axis=-1, keepdims=True))).astype(BF16)

        @pl.when(pl.program_id(1) == 0)
        def _():
            dw_ref[...] = jnp.zeros_like(dw_ref)

        dw_ref[...] += jnp.sum(dm * ohat * silu, axis=0, keepdims=True)

    def blk(first):
        return pl.BlockSpec((tr, HEAD_DIM), lambda h, i: (i, first + h))

    vec = pl.BlockSpec((1, HEAD_DIM), lambda h, i: (0, h))
    o_shape = jax.ShapeDtypeStruct((s, n_heads * HEAD_DIM), BF16)
    return pl.pallas_call(
        body, name="ret_gate_bwd", grid=(n_heads, s // tr),
        in_specs=[blk(first_col), blk(0), blk(gate_col), vec],
        out_specs=[blk(0), blk(0), vec],
        out_shape=[o_shape, o_shape, jax.ShapeDtypeStruct((1, n_heads * HEAD_DIM), F32)],
        compiler_params=_params(("parallel", "arbitrary")),
    )(dmixed, out, proj, norm_w)


def _step(x, target, norm_mix_w, ret_decay_fwd, ret_decay_bwd, ret_norm_w, norm_ffn_w, norm_final_w, own,
          w_in_started, pos):
    d = x.shape[1]
    nh = d // (2 * HEAD_DIM)
    scale = HEAD_DIM ** -0.5
    slopes = jnp.exp2(-8.0 * jnp.arange(1, nh + 1, dtype=F32) / nh)
    lg_f = -jnp.exp(ret_decay_fwd)
    lg_b = -jnp.exp(ret_decay_bwd)
    q_r, k_r, v_r, g_r = 3 * nh, 4 * nh, 5 * nh, 6 * nh
    ax = BIG_AXIS

    def gather(names, arrays, stage, part=None):
        return _gather_job(arrays, [ax[k] for k in names], stage, part)

    def add_halves(k, g, received):
        return _add_halves(g, received, ax[k], pos, name="grad_add_halves_" + k)

    def sum_parts(k, g, received, parts):
        return _sum_chip_parts(g, received, parts, ax[k], pos, name="grad_sum_parts_" + k)

    sems, w_in, token = w_in_started
    n1 = _rmsnorm_fwd(x, norm_mix_w, name="norm_mix_fwd", after=token)
    w_in = _split_gather_wait(sems, w_in, ax["w_in"], [n1] + [own[k] for k in BIG if k != "w_in"])
    (w_in,) = _run_jobs([gather(["w_in"], [w_in], "d2d")], name="all_gather_w_in_sibling")
    proj, [[w_gate]] = _matmul(n1, w_in, name="in_proj", out_dtype=BF16, tm=2048,
                               jobs=[gather(["w_gate"], [own["w_gate"]], "ici")])
    qkv_classes = _to_classes(proj[:, :3 * nh * HEAD_DIM])
    (ret, ret_mixed), [[w_gate], [w_out]] = _retention(
        (proj, q_r), (proj, k_r), (proj, v_r), lg_f, lg_b, strict_c=False, strict_a=True, scale=scale, n_heads=nh,
        name="retention_fwd", gate=(proj, g_r), norm_w=ret_norm_w,
        jobs=[gather(["w_gate"], [w_gate], "d2d"), gather(["w_out"], [own["w_out"]], "ici")])
    (far_out, far_lse), [[w_up]] = _attention_far_fwd(
        qkv_classes, slopes, nh, jobs=[gather(["w_up"], [own["w_up"]], "ici", (0, 1, 4))])
    (attn, lse), [[w_out], [w_up]] = _attention_fwd(
        proj, slopes, _from_classes(far_out), _from_classes(far_lse), nh,
        jobs=[gather(["w_out"], [w_out], "d2d"),
              _fuse(gather(["w_up"], [w_up], "d2d", (0, 1, 4)), gather(["w_up"], [w_up], "ici", (1, 2, 4)))])
    mixed = jnp.concatenate([attn, ret_mixed], axis=1)
    h1, [[w_up]] = _matmul(
        mixed, w_out, name="out_proj", residual=x,
        jobs=[_fuse(gather(["w_up"], [w_up], "d2d", (1, 2, 4)), gather(["w_up"], [w_up], "ici", (3, 1, 4)))])
    up_sibling = _split_start(gather(["w_up"], [w_up], "d2d", (3, 1, 4)), name="all_gather_w_up_sibling_start")
    n2 = _rmsnorm_fwd(h1, norm_ffn_w, name="norm_ffn_fwd", after=up_sibling["token"])
    (w_up,) = _split_wait(up_sibling, [n2], name="all_gather_w_up_sibling_wait")
    (gate, up, act), [[w_down]] = _swiglu_fwd(n2, w_gate, w_up, jobs=[gather(["w_down"], [own["w_down"]], "ici")])
    (w_down,) = _run_jobs([gather(["w_down"], [w_down], "d2d")], name="all_gather_w_down_sibling")
    h2 = _matmul(act, w_down, name="down_proj", residual=h1, tk=2816)
    dh2, dh2_b, d_norm_final, loss = _loss_head(h2, norm_final_w, target)

    dgate, dup = _swiglu_bwd_act(dh2_b, w_down, gate, up)
    g_down = _weight_grad(act, dh2_b, name="grad_w_down")
    g_gate, [[r_down]] = _weight_grad(n2, dgate, name="grad_w_gate", jobs=[_exchange_job([g_down], [ax["w_down"]])])
    s_down = add_halves("w_down", g_down, r_down)
    g_up, [[r_gate], [p_down]] = _weight_grad(
        n2, dup, name="grad_w_up",
        jobs=[_exchange_job([g_gate], [ax["w_gate"]]), _send_sums_job([s_down], [ax["w_down"]], (0, 1, 2))])
    s_gate = add_halves("w_gate", g_gate, r_gate)
    dn2, [[r_up], [p_gate], [p_down]] = _swiglu_bwd_in(
        dgate, dup, w_gate, w_up,
        jobs=[_exchange_job([g_up], [ax["w_up"]]), _send_sums_job([s_gate], [ax["w_gate"]]),
              _send_sums_job([s_down], [ax["w_down"]], (1, 1, 2), landing=[p_down])])
    h_down = sum_parts("w_down", g_down, r_down, p_down)
    s_up = add_halves("w_up", g_up, r_up)
    h_gate = sum_parts("w_gate", g_gate, r_gate, p_gate)
    dh1, dh1_b, d_norm_ffn = _rmsnorm_bwd(dn2, h1, norm_ffn_w, dh2, name="norm_ffn_bwd")

    dmixed, [[gr_down], [p_up]] = _matmul(
        dh1_b, w_out, name="out_proj_bwd", tb=True, out_dtype=BF16,
        jobs=[_join_job([h_down], [ax["w_down"]]), _send_sums_job([s_up], [ax["w_up"]], (0, 1, 4))])
    far_in = [_to_classes(t) for t in (attn, dmixed[:, :nh * HEAD_DIM], lse)]
    g_out, [[p_up]] = _weight_grad(mixed, dh1_b, name="grad_w_out",
                                   jobs=[_send_sums_job([s_up], [ax["w_up"]], (1, 1, 4), landing=[p_up])])
    d_ret, dg_r, d_ret_norm = _ret_gate_bwd(dmixed, nh, ret, proj, g_r, ret_norm_w, nh)
    far_grads = _attention_far_bwd(qkv_classes, slopes, *far_in, nh)
    far_grads = [_from_classes(t) for t in far_grads]
    dq_r, [[gr_gate], [p_up]] = _retention(
        (d_ret, 0), (proj, v_r), (proj, k_r), lg_f, lg_b, strict_c=False, strict_a=True, scale=scale, n_heads=nh,
        name="retention_dq",
        jobs=[_join_job([h_gate], [ax["w_gate"]]), _send_sums_job([s_up], [ax["w_up"]], (2, 1, 4), landing=[p_up])])
    (dq_a, dk_a, dv_a), [[p_up], [r_out]] = _attention_bwd(
        proj, slopes, attn, lse, dmixed, far_grads, nh,
        jobs=[_send_sums_job([s_up], [ax["w_up"]], (3, 1, 4), landing=[p_up]),
              _exchange_job([g_out], [ax["w_out"]])])
    s_out = add_halves("w_out", g_out, r_out)
    h_up = sum_parts("w_up", g_up, r_up, p_up)
    dv_r, [[p_out], [gr_up]] = _retention(
        (proj, k_r), (proj, q_r), (d_ret, 0), lg_b, lg_f, strict_c=True, strict_a=False, scale=scale, n_heads=nh,
        name="retention_dv", jobs=[_send_sums_job([s_out], [ax["w_out"]]), _join_job([h_up], [ax["w_up"]])])
    h_out = sum_parts("w_out", g_out, r_out, p_out)
    dk_r, [[gr_out]] = _retention(
        (proj, v_r), (d_ret, 0), (proj, q_r), lg_b, lg_f, strict_c=True, strict_a=False, scale=scale, n_heads=nh,
        name="retention_dk", jobs=[_join_job([h_out], [ax["w_out"]])])
    dlg_f, dlg_b = _retention_decay_grads((proj, q_r), (proj, k_r), (proj, v_r), (d_ret, 0), lg_f, lg_b,
                                          scale=scale, n_heads=nh)
    dproj = [dq_a, dk_a, dv_a, dq_r, dk_r, dv_r, dg_r]
    g_in = _weight_grad_pieces(n1, dproj, name="grad_w_in")
    exchange = _split_start(_exchange_job([g_in], [ax["w_in"]]), name="grad_exchange_w_in_start")
    dn1 = _matmul_pieces_nt(dproj, w_in, name="in_proj_bwd", after=[exchange["token"]])
    g_in, r_in = _split_wait(exchange, [dn1], name="grad_exchange_w_in_wait")
    s_in = add_halves("w_in", g_in, r_in)
    sending = _split_start(_send_sums_job([s_in], [ax["w_in"]]), name="grad_send_w_in_start")
    dx, _, d_norm_mix = _rmsnorm_bwd(dn1, x, norm_mix_w, dh1, name="norm_mix_bwd", after=[sending["token"]])

    small = dict(loss=loss[0, 0], norm_mix_w=d_norm_mix, ret_decay_fwd=dlg_f * lg_f, ret_decay_bwd=dlg_b * lg_b,
                 ret_norm_w=d_ret_norm, norm_ffn_w=d_norm_ffn, norm_final_w=d_norm_final)
    return (dx, dict(w_out=gr_out, w_gate=gr_gate, w_up=gr_up, w_down=gr_down), small,
            dict(sending=sending, grad=g_in, received=r_in))


def _mesh_position():
    x, y, c = lax.axis_index("x"), lax.axis_index("y"), lax.axis_index("c")
    chips = [(1 - x, y), (x, 1 - y), (1 - x, 1 - y)]
    return x, y, c, chips


def _span(span):
    if span is None:
        return slice(None)
    start, size, step = span
    return pl.ds(start if isinstance(start, int) else pl.multiple_of(start, step), size)


def _part_rows(part, rows):
    first, count, of = part
    return first * (rows // of), count * (rows // of), rows // of


def _region(ref, axis, shard, half, shard_size, half_size, part=None, total_rows=None):
    along = None if shard is None else (shard * shard_size, shard_size, shard_size)
    other = None if half is None else (half * half_size, half_size, half_size)
    rows, cols = (other, along) if axis == 1 else (along, other)
    if part is not None:
        start, size, _ = rows if rows is not None else (0, total_rows, None)
        offset, size, step = _part_rows(part, size)
        rows = (start + offset, size, step)
    return ref.at[_span(rows), _span(cols)]


def _fuse(first, second):
    assert not (first.ins or first.outs or second.ins or second.outs)
    assert len(first.ios) == len(second.ios) and all(a is b for a, b in zip(first.ios, second.ios))
    cut = len(first.sems)

    def start(refs, sems):
        first.start(refs, sems[:cut])
        second.start(refs, sems[cut:])

    def finish(refs, sems):
        first.finish(refs, sems[:cut])
        second.finish(refs, sems[cut:])

    return _Job(ios=first.ios, sems=first.sems + second.sems, start=start, finish=finish)


def _gather_job(full, axes, stage, part=None):
    n = len(full)

    def copies(refs, sems):
        send_sem, recv_sem = sems
        x, y, c, chips = _mesh_position()
        me = 2 * x + y

        def copy(w, k, shard, half, target):
            rows_cols = full[w].shape
            place = _region(refs[w], axes[w], shard, half, rows_cols[axes[w]] // N_CHIPS, rows_cols[1 - axes[w]] // 2,
                            part)
            return pltpu.make_async_remote_copy(
                src_ref=place, dst_ref=place, send_sem=send_sem.at[w, k], recv_sem=recv_sem.at[w, k],
                device_id=target, device_id_type=MESH)

        def sent(w, k):
            if stage == "ici":
                return copy(w, k, me, c, (chips[k][0], chips[k][1], c))
            return copy(w, k, 2 * chips[k][0] + chips[k][1], c, (x, y, 1 - c))

        def landed(w, k):
            return copy(w, k, 2 * chips[k][0] + chips[k][1], c if stage == "ici" else 1 - c, (x, y, 1 - c))

        return sent, landed

    def start(refs, sems):
        sent, _ = copies(refs, sems)
        for w in range(n):
            for k in range(3):
                sent(w, k).start()

    def finish(refs, sems):
        sent, landed = copies(refs, sems)
        for w in range(n):
            for k in range(3):
                landed(w, k).wait_recv()
                sent(w, k).wait_send()

    return _Job(ios=full, sems=[pltpu.SemaphoreType.DMA((n, 3))] * 2, start=start, finish=finish)


def _exchange_job(grads, axes):
    n = len(grads)

    def half_shape(w):
        return tuple(d // 2 if a != axes[w] else d for a, d in enumerate(grads[w].shape))

    def copy(refs, sems, w):
        x, y, c, _ = _mesh_position()
        return pltpu.make_async_remote_copy(
            src_ref=_region(refs[w], axes[w], None, 1 - c, 0, half_shape(w)[1 - axes[w]]), dst_ref=refs[n + w],
            send_sem=sems[0].at[w], recv_sem=sems[1].at[w], device_id=(x, y, 1 - c), device_id_type=MESH)

    def start(refs, sems):
        for w in range(n):
            copy(refs, sems, w).start()

    def finish(refs, sems):
        for w in range(n):
            copy(refs, sems, w).wait()

    return _Job(ins=grads, outs=[jax.ShapeDtypeStruct(half_shape(w), F32) for w in range(n)],
                sems=[pltpu.SemaphoreType.DMA((n,))] * 2, start=start, finish=finish)


def _half_block_spec(axis, block, half_blocks, use_half):
    if axis == 1:
        if use_half:
            return pl.BlockSpec(block, lambda i, pos: (pos[0] * half_blocks + i, 0))
        return pl.BlockSpec(block, lambda i, pos: (i, 0))
    if use_half:
        return pl.BlockSpec(block, lambda i, pos: (i, pos[0]))
    return pl.BlockSpec(block, lambda i, pos: (i, 0))


def _add_halves(grad, received, axis, pos, *, name):
    rows, cols = received.shape
    tr = _row_block(rows, cols)
    nb = rows // tr

    def body(pos_ref, g_ref, r_ref, o_ref):
        o_ref[...] = (g_ref[...] + r_ref[...]).astype(BF16)

    blk = (tr, cols)
    return pl.pallas_call(
        body, name=name, out_shape=jax.ShapeDtypeStruct((rows, cols), BF16),
        grid_spec=pltpu.PrefetchScalarGridSpec(
            num_scalar_prefetch=1, grid=(nb,),
            in_specs=[_half_block_spec(axis, blk, nb, True), _half_block_spec(axis, blk, nb, False)],
            out_specs=_half_block_spec(axis, blk, nb, False)),
        compiler_params=_params(("parallel",)),
    )(pos, grad, received)


def _send_sums_job(sums, axes, part=None, landing=None):
    n = len(sums)

    def part_shape(w):
        return tuple(d // N_CHIPS if a == axes[w] else d for a, d in enumerate(sums[w].shape))

    def copy(refs, sems, w, k):
        x, y, c, chips = _mesh_position()
        shard = 2 * chips[k][0] + chips[k][1]
        rows = part_shape(w)[0]
        dst = refs[n + w].at[k]
        if part is not None:
            offset, size, _ = _part_rows(part, rows)
            dst = refs[n + w].at[k, pl.ds(offset, size), :]
        return pltpu.make_async_remote_copy(
            src_ref=_region(refs[w], axes[w], shard, None, part_shape(w)[axes[w]], 0, part, rows), dst_ref=dst,
            send_sem=sems[0].at[w, k], recv_sem=sems[1].at[w, k],
            device_id=(chips[k][0], chips[k][1], c), device_id_type=MESH)

    def start(refs, sems):
        for w in range(n):
            for k in range(3):
                copy(refs, sems, w, k).start()

    def finish(refs, sems):
        for w in range(n):
            for k in range(3):
                copy(refs, sems, w, k).wait()

    sems = [pltpu.SemaphoreType.DMA((n, 3))] * 2
    if landing is not None:
        return _Job(ins=sums, ios=landing, sems=sems, start=start, finish=finish)
    return _Job(ins=sums, outs=[jax.ShapeDtypeStruct((3,) + part_shape(w), BF16) for w in range(n)],
                sems=sems, start=start, finish=finish)


def _sum_chip_parts(grad, received, parts, axis, pos, *, name):
    _, rows, cols = parts.shape
    tr = _row_block(rows, cols)
    nb = rows // tr
    blk = (tr, cols)

    def body(pos_ref, g_ref, r_ref, p_ref, o_ref):
        total = g_ref[...] + r_ref[...]
        for k in range(3):
            total = total + p_ref[k].astype(F32)
        o_ref[...] = total

    if axis == 1:
        g_spec = pl.BlockSpec(blk, lambda i, pos: (pos[0] * nb + i, pos[1]))
        r_spec = pl.BlockSpec(blk, lambda i, pos: (i, pos[1]))
        o_spec = pl.BlockSpec(blk, lambda i, pos: (pos[0] * nb + i, 0))
        shard_shape = (2 * rows, cols)
    else:
        g_spec = pl.BlockSpec(blk, lambda i, pos: (pos[1] * nb + i, pos[0]))
        r_spec = pl.BlockSpec(blk, lambda i, pos: (pos[1] * nb + i, 0))
        o_spec = pl.BlockSpec(blk, lambda i, pos: (i, pos[0]))
        shard_shape = (rows, 2 * cols)
    return pl.pallas_call(
        body, name=name, out_shape=jax.ShapeDtypeStruct(shard_shape, F32),
        grid_spec=pltpu.PrefetchScalarGridSpec(
            num_scalar_prefetch=1, grid=(nb,),
            in_specs=[g_spec, r_spec, pl.BlockSpec((3,) + blk, lambda i, pos: (0, i, 0))],
            out_specs=o_spec),
        compiler_params=_params(("parallel",)),
    )(pos, grad, received, parts)


def _join_job(shards, axes):
    n = len(shards)

    def copy(refs, sems, w, other):
        x, y, c, _ = _mesh_position()
        place = _region(refs[w], axes[w], None, 1 - c if other else c, 0, shards[w].shape[1 - axes[w]] // 2)
        return pltpu.make_async_remote_copy(
            src_ref=place, dst_ref=place, send_sem=sems[0].at[w], recv_sem=sems[1].at[w],
            device_id=(x, y, 1 - c), device_id_type=MESH)

    def start(refs, sems):
        for w in range(n):
            copy(refs, sems, w, False).start()

    def finish(refs, sems):
        for w in range(n):
            copy(refs, sems, w, True).wait_recv()
            copy(refs, sems, w, False).wait_send()

    return _Job(ios=shards, sems=[pltpu.SemaphoreType.DMA((n,))] * 2, start=start, finish=finish)


def _all_reduce_small(vec, after=()):
    rows, cols = vec.shape

    def body(v_ref, *rest):
        o_ref, land_ref, send_sem, recv_sem = rest[len(after):]
        x, y, c, _ = _mesh_position()
        me = 4 * x + 2 * y + c
        land_ref[me] = v_ref[...]
        copies = []
        for k in range(1, 8):
            px, py, pc = x ^ (k >> 2), y ^ ((k >> 1) & 1), c ^ (k & 1)
            copies.append(pltpu.make_async_remote_copy(
                src_ref=v_ref, dst_ref=land_ref.at[me], send_sem=send_sem.at[k], recv_sem=recv_sem.at[k],
                device_id=(px, py, pc), device_id_type=MESH))
        for cp in copies:
            cp.start()
        for k in range(1, 8):
            peer = me ^ k
            pltpu.make_async_remote_copy(
                src_ref=v_ref, dst_ref=land_ref.at[peer], send_sem=send_sem.at[k], recv_sem=recv_sem.at[k],
                device_id=(x, y, c), device_id_type=MESH).wait_recv()
        for cp in copies:
            cp.wait_send()
        total = land_ref[0]
        for k in range(1, 8):
            total = total + land_ref[k]
        o_ref[...] = total

    vmem = pl.BlockSpec(memory_space=pltpu.VMEM)
    return pl.pallas_call(
        body, name="all_reduce_small", in_specs=[vmem] + [pl.BlockSpec(memory_space=pl.ANY)] * len(after),
        out_specs=vmem, out_shape=jax.ShapeDtypeStruct((rows, cols), F32),
        scratch_shapes=[pltpu.VMEM((8, rows, cols), F32), pltpu.SemaphoreType.DMA((8,)), pltpu.SemaphoreType.DMA((8,))],
    )(vec, *after)


def _adamw(w, g, m, v, *, name, after=()):
    rows, cols = w.shape
    tr = _row_block(rows, cols) if rows % 8 == 0 else rows
    bc1 = 1.0 - ADAM_B1 ** ADAM_STEP
    bc2 = 1.0 - ADAM_B2 ** ADAM_STEP

    def body(w_ref, g_ref, m_ref, v_ref, *rest):
        go_ref, d_ref, mo_ref, vo_ref = rest[len(after):]
        gv = g_ref[...]
        go_ref[...] = gv
        mn = ADAM_B1 * m_ref[...] + (1.0 - ADAM_B1) * gv
        vn = ADAM_B2 * v_ref[...] + (1.0 - ADAM_B2) * (gv * gv)
        mo_ref[...] = mn
        vo_ref[...] = vn
        d_ref[...] = -ADAM_LR * ((mn / bc1) / (jnp.sqrt(vn / bc2) + ADAM_EPS) + ADAM_WD * w_ref[...])

    blk = pl.BlockSpec((tr, cols), lambda i: (i, 0))
    shape = jax.ShapeDtypeStruct((rows, cols), F32)
    return pl.pallas_call(
        body, name=name, grid=(rows // tr,), in_specs=[blk] * 4 + [pl.BlockSpec(memory_space=pl.ANY)] * len(after),
        out_specs=[blk] * 4, out_shape=[shape] * 4, compiler_params=_params(("parallel",)),
    )(w, g, m, v, *after)


def _to_bf16_in_place(w, axis, pos, *, name, after=None):
    rows, cols = w.shape
    tr = _row_block(rows, cols)
    nb = rows // tr

    def body(pos_ref, w_ref, *rest):
        rest[-1][...] = w_ref[...].astype(BF16)

    if axis == 1:
        o_spec = pl.BlockSpec((tr, cols), lambda i, pos: (i, pos[1]))
        full_shape = (rows, N_CHIPS * cols)
    else:
        o_spec = pl.BlockSpec((tr, cols), lambda i, pos: (pos[1] * nb + i, 0))
        full_shape = (N_CHIPS * rows, cols)
    in_specs = [pl.BlockSpec((tr, cols), lambda i, pos: (i, 0))]
    operands = [pos, w]
    if after is not None:
        in_specs.append(pl.BlockSpec(after.shape, lambda i, pos: (0, 0)))
        operands.append(after)
    return pl.pallas_call(
        body, name=name, out_shape=jax.ShapeDtypeStruct(full_shape, BF16),
        grid_spec=pltpu.PrefetchScalarGridSpec(num_scalar_prefetch=1, grid=(nb,), in_specs=in_specs, out_specs=o_spec),
        compiler_params=_params(("parallel",)),
    )(*operands)


def _split_gather_start(full, axis):
    rows_cols = full.shape

    def body(buf_ref, *rest):
        sems = rest[:6]
        token_ref = rest[7]
        x, y, c, chips = _mesh_position()
        place = _region(buf_ref, axis, 2 * x + y, c, rows_cols[axis] // N_CHIPS, rows_cols[1 - axis] // 2)
        for k in range(3):
            pltpu.make_async_remote_copy(
                src_ref=place, dst_ref=place, send_sem=sems[k], recv_sem=sems[3 + k],
                device_id=(chips[k][0], chips[k][1], c), device_id_type=MESH).start()
        token_ref[...] = jnp.zeros_like(token_ref)

    hbm = pl.BlockSpec(memory_space=pltpu.HBM)
    sem = pl.BlockSpec(memory_space=pltpu.SEMAPHORE)
    res = pl.pallas_call(
        body, name="all_gather_w_in_start",
        out_shape=(*[pltpu.SemaphoreType.DMA(())] * 6, pltpu.HBM(full.shape, full.dtype),
                   jax.ShapeDtypeStruct((8, HEAD_DIM), F32)),
        in_specs=(hbm,), out_specs=(*[sem] * 6, hbm, pl.BlockSpec(memory_space=pltpu.VMEM)),
        input_output_aliases={0: 6},
        compiler_params=pltpu.CompilerParams(has_side_effects=pltpu.SideEffectType.DATAFLOW_SIDE_EFFECTING),
    )(pltpu.with_memory_space_constraint(full, pltpu.HBM))
    return list(res[:6]), res[6], res[7]


def _split_gather_wait(sems, full, axis, after):
    rows_cols = full.shape

    def body(buf_ref, *rest):
        sem_refs = rest[:6]
        x, y, c, chips = _mesh_position()

        def copy(k, shard):
            place = _region(buf_ref, axis, shard, c, rows_cols[axis] // N_CHIPS, rows_cols[1 - axis] // 2)
            return pltpu.make_async_remote_copy(
                src_ref=place, dst_ref=place, send_sem=sem_refs[k], recv_sem=sem_refs[3 + k],
                device_id=(chips[k][0], chips[k][1], c), device_id_type=MESH)

        for k in range(3):
            copy(k, 2 * x + y).wait_send()
            copy(k, 2 * chips[k][0] + chips[k][1]).wait_recv()

    hbm = pl.BlockSpec(memory_space=pltpu.HBM)
    sem = pl.BlockSpec(memory_space=pltpu.SEMAPHORE)
    return pl.pallas_call(
        body, name="all_gather_w_in_wait", out_shape=pltpu.HBM(full.shape, full.dtype),
        in_specs=(hbm, *[sem] * 6, *[pl.BlockSpec(memory_space=pl.ANY)] * len(after)), out_specs=hbm,
        input_output_aliases={0: 0},
        compiler_params=pltpu.CompilerParams(has_side_effects=pltpu.SideEffectType.DATAFLOW_SIDE_EFFECTING),
    )(full, *sems, *after)


BIG = ("w_in", "w_out", "w_gate", "w_up", "w_down")
BIG_AXIS = dict(w_in=1, w_out=0, w_gate=1, w_up=1, w_down=0)
SMALL = ("norm_mix_w", "ret_decay_fwd", "ret_decay_bwd", "ret_norm_w", "norm_ffn_w", "norm_final_w")
ALL_WEIGHTS = ("norm_mix_w", "w_in", "ret_decay_fwd", "ret_decay_bwd", "ret_norm_w", "w_out", "norm_ffn_w",
               "w_gate", "w_up", "w_down", "norm_final_w")
SMALL_ROW = 128 * 8


def _pack_small(small):
    pieces = [jnp.reshape(small["loss"], (1,))] + [jnp.reshape(small[k], (-1,)) for k in SMALL]
    rows = []
    for p in pieces:
        pad = -p.shape[0] % (8 * SMALL_ROW)
        rows.append(jnp.reshape(jnp.pad(p, (0, pad)), (-1, SMALL_ROW)))
    return jnp.concatenate(rows, axis=0)


def _unpack_small(block, like):
    out, row = {}, 0
    for k in ("loss",) + SMALL:
        size = 1 if k == "loss" else like[k].size
        nrows = -(-size // (8 * SMALL_ROW)) * 8
        out[k] = jnp.reshape(block[row:row + nrows], (-1,))[:size]
        row += nrows
    return out


def kernel(x, norm_mix_w, w_in, ret_decay_fwd, ret_decay_bwd, ret_norm_w, w_out, norm_ffn_w, w_gate, w_up, w_down, norm_final_w, loss_target, m_norm_mix_w, m_w_in, m_ret_decay_fwd, m_ret_decay_bwd, m_ret_norm_w, m_w_out, m_norm_ffn_w, m_w_gate, m_w_up, m_w_down, m_norm_final_w, v_norm_mix_w, v_w_in, v_ret_decay_fwd, v_ret_decay_bwd, v_ret_norm_w, v_w_out, v_norm_ffn_w, v_w_gate, v_w_up, v_w_down, v_norm_final_w):
    weights = dict(norm_mix_w=norm_mix_w, w_in=w_in, ret_decay_fwd=ret_decay_fwd, ret_decay_bwd=ret_decay_bwd,
                   ret_norm_w=ret_norm_w, w_out=w_out, norm_ffn_w=norm_ffn_w, w_gate=w_gate, w_up=w_up,
                   w_down=w_down, norm_final_w=norm_final_w)
    m_in = dict(norm_mix_w=m_norm_mix_w, w_in=m_w_in, ret_decay_fwd=m_ret_decay_fwd, ret_decay_bwd=m_ret_decay_bwd,
                ret_norm_w=m_ret_norm_w, w_out=m_w_out, norm_ffn_w=m_norm_ffn_w, w_gate=m_w_gate, w_up=m_w_up,
                w_down=m_w_down, norm_final_w=m_norm_final_w)
    v_in = dict(norm_mix_w=v_norm_mix_w, w_in=v_w_in, ret_decay_fwd=v_ret_decay_fwd, ret_decay_bwd=v_ret_decay_bwd,
                ret_norm_w=v_ret_norm_w, w_out=v_w_out, norm_ffn_w=v_norm_ffn_w, w_gate=v_w_gate, w_up=v_w_up,
                w_down=v_w_down, norm_final_w=v_norm_final_w)
    pos = jnp.stack([lax.axis_index("c"), 2 * lax.axis_index("x") + lax.axis_index("y")]).astype(jnp.int32)

    own = {"w_in": _to_bf16_in_place(weights["w_in"][0], BIG_AXIS["w_in"], pos, name="cast_w_in")}
    w_in_started = _split_gather_start(own["w_in"], BIG_AXIS["w_in"])
    for k in BIG[1:]:
        own[k] = _to_bf16_in_place(weights[k][0], BIG_AXIS[k], pos, name="cast_" + k, after=w_in_started[2])

    dx, grad_w, small, w_in_pending = _step(
        x[0], loss_target[0], norm_mix_w, ret_decay_fwd[0], ret_decay_bwd[0], ret_norm_w, norm_ffn_w,
        norm_final_w[None, :], own, w_in_started, pos)

    delta, new_m, new_v = {}, {}, {}

    def update(k, after):
        shape = weights[k].shape
        as2d = (lambda t: jnp.reshape(t, (-1, shape[-1])))
        grad_w[k], delta[k], new_m[k], new_v[k] = (jnp.reshape(t, shape) for t in _adamw(
            as2d(weights[k]), as2d(grad_w[k]), as2d(m_in[k]), as2d(v_in[k]), name="adamw_" + k, after=after))

    others = [k for k in BIG if k != "w_in"]
    for k in others:
        update(k, [w_in_pending["sending"]["token"]])
    _, parts = _split_wait(w_in_pending["sending"], [dx] + [delta[k] for k in others], name="grad_send_w_in_wait")

    half = _sum_chip_parts(w_in_pending["grad"], w_in_pending["received"], parts, BIG_AXIS["w_in"], pos,
                           name="grad_sum_parts_w_in")
    joining = _split_start(_join_job([half], [BIG_AXIS["w_in"]]), name="grad_join_w_in_start")

    like = {k: weights[k] for k in SMALL}
    reduced = _unpack_small(_all_reduce_small(_pack_small(small), after=[joining["token"]]), like)
    loss = reduced["loss"][0]
    for k in SMALL:
        grad_w[k] = jnp.reshape(reduced[k], (1, -1))
        update(k, [])
    (grad_w["w_in"],) = _split_wait(joining, [delta[k] for k in SMALL], name="grad_join_w_in_wait")
    update("w_in", [])

    return (loss, dx[None], *[grad_w[k] for k in ALL_WEIGHTS], *[delta[k] for k in ALL_WEIGHTS],
            *[new_m[k] for k in ALL_WEIGHTS], *[new_v[k] for k in ALL_WEIGHTS])
```

```python
import functools
import math

import numpy as np
import jax
import jax.numpy as jnp
from jax import lax
from jax.experimental import pallas as pl
from jax.experimental.pallas import tpu as pltpu

F32 = jnp.float32
BF16 = jnp.bfloat16
MESH = pl.DeviceIdType.MESH

HEAD_DIM = 128
RET_CHUNK = 128
RET_UNROLL = 8
EPS = 1e-6
DILATED_PATTERNS = ((128, 1), (512, 4), (2048, 16))
ATT_BLOCK = 256
ATT_REACH = max(w // 2 for w, _ in DILATED_PATTERNS)
ATT_NEAR = ATT_BLOCK
ATT_CLASSES = DILATED_PATTERNS[-1][1]
assert all(w // 2 <= ATT_NEAR for w, _ in DILATED_PATTERNS[:-1])
ATT_KB = -(-ATT_NEAR // ATT_BLOCK)
ATT_WINDOW = 2 * ATT_KB + 1
ATT_FAR_GROUP = 8
ATT_NEAR_GROUP = 4
MASKED = -1e30
ROW_MAX_INIT = -1e29
N_CHIPS = 4
VMEM_LIMIT_BYTES = 56 * 1024 * 1024
ELEM_BLOCK_BYTES = 2 * 1024 * 1024
WEIGHT_GRAD_GROUP = 4

ADAM_LR = 0.001
ADAM_B1 = 0.9
ADAM_B2 = 0.999
ADAM_EPS = 1e-08
ADAM_WD = 0.01
ADAM_STEP = 10


def _params(sem=None):
    return pltpu.CompilerParams(dimension_semantics=sem, vmem_limit_bytes=VMEM_LIMIT_BYTES)


def _sigmoid(x):
    return 0.5 * jnp.tanh(0.5 * x) + 0.5


class _Job:
    def __init__(self, *, ins=(), ios=(), outs=(), sems=(), start, finish):
        self.ins, self.ios, self.outs, self.sems = list(ins), list(ios), list(outs), list(sems)
        self.start, self.finish = start, finish

    def results(self):
        return [jax.ShapeDtypeStruct(a.shape, a.dtype) for a in self.ios] + self.outs


def _call(body, *, name, grid, in_specs, out_specs, out_shape, operands, scratch_shapes=(), semantics=None, jobs=(),
          after=()):
    in_specs, out_specs, out_shape = list(in_specs), list(out_specs), list(out_shape)
    scratch_shapes = list(scratch_shapes)
    if not jobs:
        n_real = len(in_specs)

        def ordered(*refs):
            body(*refs[:n_real], *refs[n_real + len(after):])

        outs = pl.pallas_call(
            ordered if after else body, name=name, grid=grid,
            in_specs=in_specs + [pl.BlockSpec(memory_space=pl.ANY)] * len(after), out_specs=out_specs,
            out_shape=out_shape, scratch_shapes=scratch_shapes, compiler_params=_params(semantics))(*operands, *after)
        return outs, []
    n_in, n_out, n_scratch = len(in_specs), len(out_specs), len(scratch_shapes)
    extra_in, extra_out, sems, aliases = [], [], [], {}
    for job in jobs:
        extra_in += job.ins
        for t in range(len(job.ios)):
            aliases[n_in + len(extra_in) + t] = n_out + len(extra_out) + t
        extra_in += job.ios
        extra_out += job.results()
        sems += job.sems

    def carried(*refs):
        x_in = refs[n_in:n_in + len(extra_in)]
        x_out = refs[n_in + len(extra_in) + n_out:n_in + len(extra_in) + n_out + len(extra_out)]
        x_sem = refs[len(refs) - len(sems):]
        views, i_in, i_out, i_sem = [], 0, 0, 0
        for job in jobs:
            data = list(x_in[i_in:i_in + len(job.ins)]) + list(x_out[i_out:i_out + len(job.results())])
            views.append((data, x_sem[i_sem:i_sem + len(job.sems)]))
            i_in += len(job.ins) + len(job.ios)
            i_out += len(job.results())
            i_sem += len(job.sems)
        steps = [pl.program_id(d) for d in range(len(grid))]

        @pl.when(functools.reduce(jnp.logical_and, [s == 0 for s in steps]))
        def _():
            for job, (data, sem) in zip(jobs, views):
                job.start(data, sem)

        body(*refs[:n_in], *refs[n_in + len(extra_in):n_in + len(extra_in) + n_out],
             *refs[len(refs) - len(sems) - n_scratch:len(refs) - len(sems)])

        @pl.when(functools.reduce(jnp.logical_and, [s == g - 1 for s, g in zip(steps, grid)]))
        def _():
            for job, (data, sem) in zip(jobs, views):
                job.finish(data, sem)

    hbm = pl.BlockSpec(memory_space=pl.ANY)
    res = pl.pallas_call(
        carried, name=name, grid=grid, in_specs=in_specs + [hbm] * len(extra_in),
        out_specs=out_specs + [hbm] * len(extra_out), out_shape=out_shape + extra_out,
        input_output_aliases=aliases, scratch_shapes=scratch_shapes + sems,
        compiler_params=_params(("arbitrary",) * len(grid)),
    )(*operands, *extra_in)
    carried_results, at = [], n_out
    for job in jobs:
        carried_results.append(list(res[at:at + len(job.results())]))
        at += len(job.results())
    return list(res[:n_out]), carried_results


def _run_jobs(jobs, *, name):
    first = jobs[0]
    n_in, n_io = len(first.ins), len(first.ios)
    out_shape = first.results()
    n_sems = [len(job.sems) for job in jobs]

    def body(*refs):
        data = list(refs[:n_in]) + list(refs[n_in + n_io:n_in + n_io + len(out_shape)])
        at = n_in + n_io + len(out_shape)
        for job, ns in zip(jobs, n_sems):
            job.start(data, refs[at:at + ns])
            job.finish(data, refs[at:at + ns])
            at += ns

    hbm = pl.BlockSpec(memory_space=pl.ANY)
    return pl.pallas_call(
        body, name=name, in_specs=[hbm] * (n_in + n_io), out_specs=[hbm] * len(out_shape), out_shape=out_shape,
        input_output_aliases={n_in + t: t for t in range(n_io)},
        scratch_shapes=[s for job in jobs for s in job.sems],
    )(*first.ins, *first.ios)


class _SemaphoreGrid:
    def __init__(self, refs, shape):
        self.refs, self.shape = list(refs), tuple(shape)

    @property
    def at(self):
        return self

    def __getitem__(self, index):
        index = index if isinstance(index, tuple) else (index,)
        flat = 0
        for i, extent in zip(index, self.shape):
            flat = flat * extent + i
        return self.refs[flat]


def _semaphore_grids(job, refs):
    grids, at = [], 0
    for sem in job.sems:
        count = math.prod(sem.shape)
        grids.append(_SemaphoreGrid(refs[at:at + count], sem.shape))
        at += count
    return grids


def _split_start(job, *, name):
    arrays = job.ins + job.ios + [lax.empty(s.shape, s.dtype) for s in job.outs]
    n, ns = len(arrays), sum(math.prod(sem.shape) for sem in job.sems)

    def body(*refs):
        job.start(list(refs[:n]), _semaphore_grids(job, refs[n:n + ns]))
        refs[-1][...] = jnp.zeros_like(refs[-1])

    hbm = pl.BlockSpec(memory_space=pltpu.HBM)
    res = pl.pallas_call(
        body, name=name,
        out_shape=(*[pltpu.SemaphoreType.DMA(())] * ns, *[pltpu.HBM(a.shape, a.dtype) for a in arrays],
                   jax.ShapeDtypeStruct((8, HEAD_DIM), F32)),
        in_specs=[hbm] * n,
        out_specs=(*[pl.BlockSpec(memory_space=pltpu.SEMAPHORE)] * ns, *[hbm] * n,
                   pl.BlockSpec(memory_space=pltpu.VMEM)),
        input_output_aliases={t: ns + t for t in range(n)},
        compiler_params=pltpu.CompilerParams(has_side_effects=pltpu.SideEffectType.DATAFLOW_SIDE_EFFECTING),
    )(*[pltpu.with_memory_space_constraint(a, pltpu.HBM) for a in arrays])
    return dict(job=job, sems=list(res[:ns]), arrays=list(res[ns:ns + n]), token=res[-1])


def _split_wait(started, after, *, name):
    job, arrays, sems = started["job"], started["arrays"], started["sems"]
    n, ns = len(arrays), len(sems)

    def body(*refs):
        job.finish(list(refs[:n]), _semaphore_grids(job, refs[n:n + ns]))

    hbm = pl.BlockSpec(memory_space=pltpu.HBM)
    return pl.pallas_call(
        body, name=name, out_shape=[pltpu.HBM(a.shape, a.dtype) for a in arrays],
        in_specs=[hbm] * n + [pl.BlockSpec(memory_space=pltpu.SEMAPHORE)] * ns
        + [pl.BlockSpec(memory_space=pl.ANY)] * len(after),
        out_specs=[hbm] * n, input_output_aliases={t: t for t in range(n)},
        compiler_params=pltpu.CompilerParams(has_side_effects=pltpu.SideEffectType.DATAFLOW_SIDE_EFFECTING),
    )(*arrays, *sems, *after)


def _dot(a, b, ta=False, tb=False):
    return lax.dot_general(a, b, (((0 if ta else 1,), (1 if tb else 0,)), ((), ())),
                           preferred_element_type=F32)


def _tile(n, want):
    t = min(n, want) // 128 * 128
    while n % t:
        t -= 128
    return t


def _a_spec(ta, tm, tk):
    return pl.BlockSpec((tk, tm), lambda i, j, k: (k, i)) if ta else pl.BlockSpec((tm, tk), lambda i, j, k: (i, k))


def _b_spec(tb, tk, tn):
    return pl.BlockSpec((tn, tk), lambda i, j, k: (j, k)) if tb else pl.BlockSpec((tk, tn), lambda i, j, k: (k, j))


def _accumulate(accs, nk, products, finish):
    if nk == 1:
        finish(*products())
        return
    k = pl.program_id(2)

    @pl.when(k == 0)
    def _():
        for acc, p in zip(accs, products()):
            acc[...] = p

    if nk > 2:
        @pl.when(jnp.logical_and(k > 0, k < nk - 1))
        def _():
            for acc, p in zip(accs, products()):
                acc[...] += p

    @pl.when(k == nk - 1)
    def _():
        finish(*[acc[...] + p for acc, p in zip(accs, products())])


def _matmul(a, b, *, name, ta=False, tb=False, out_dtype=F32, residual=None, tm=1024, tn=1024, tk=2048, jobs=()):
    m, kdim = (a.shape[1], a.shape[0]) if ta else a.shape
    n = b.shape[0] if tb else b.shape[1]
    tm, tn, tk = _tile(m, tm), _tile(n, tn), _tile(kdim, tk)
    nk = kdim // tk

    def body(*refs):
        a_ref, b_ref = refs[:2]
        r_ref = refs[2] if residual is not None else None
        o_ref = refs[-1] if nk == 1 else refs[-2]

        def finish(total):
            if residual is not None:
                total = total + r_ref[...]
            o_ref[...] = total.astype(out_dtype)

        _accumulate(refs[-1:] if nk > 1 else (), nk, lambda: (_dot(a_ref[...], b_ref[...], ta, tb),), finish)

    o_spec = pl.BlockSpec((tm, tn), lambda i, j, k: (i, j))
    in_specs = [_a_spec(ta, tm, tk), _b_spec(tb, tk, tn)]
    operands = [a, b]
    if residual is not None:
        in_specs.append(o_spec)
        operands.append(residual)
    (out,), carried = _call(
        body, name=name, grid=(m // tm, n // tn, nk), in_specs=in_specs, out_specs=[o_spec],
        out_shape=[jax.ShapeDtypeStruct((m, n), out_dtype)], operands=operands,
        scratch_shapes=[pltpu.VMEM((tm, tn), F32)] * (nk > 1),
        semantics=("parallel", "parallel", "arbitrary"), jobs=jobs)
    return (out, carried) if jobs else out


def _matmul_pieces_nt(pieces, b, *, name, tm=512, tn=1024, jobs=(), after=()):
    m, kp = pieces[0].shape
    n = b.shape[0]
    tm, tn = _tile(m, tm), _tile(n, tn)
    count = len(pieces)

    def body(*refs):
        b_ref, o_ref = refs[count], refs[count + 1]
        total = _dot(refs[0][...], b_ref[:, pl.ds(0, kp)], tb=True)
        for p in range(1, count):
            total = total + _dot(refs[p][...], b_ref[:, pl.ds(p * kp, kp)], tb=True)
        o_ref[...] = total

    piece = pl.BlockSpec((tm, kp), lambda j, i: (i, 0))
    (out,), carried = _call(
        body, name=name, grid=(n // tn, m // tm),
        in_specs=[piece] * count + [pl.BlockSpec((tn, count * kp), lambda j, i: (j, 0))],
        out_specs=[pl.BlockSpec((tm, tn), lambda j, i: (i, j))],
        out_shape=[jax.ShapeDtypeStruct((m, n), F32)], operands=[*pieces, b],
        semantics=("parallel", "parallel"), jobs=jobs, after=after)
    return (out, carried) if jobs else out


def _weight_grad_pieces(a, pieces, *, name):
    tokens, m = a.shape
    np_ = pieces[0].shape[1]
    tm = 1024 if m % 1024 == 0 else _tile(m, 1408)
    tn = _tile(np_, 512)
    nb = np_ // tn
    out = None
    for first in range(0, len(pieces), WEIGHT_GRAD_GROUP):
        group = pieces[first:first + WEIGHT_GRAD_GROUP]

        def body(*refs, count=len(group)):
            t_now = pl.program_id(1) // nb
            for t in range(count):
                @pl.when(t_now == t)
                def _(t=t):
                    refs[-1][...] = _dot(refs[0][...], refs[1 + t][...], ta=True)

        def piece_spec(t):
            return pl.BlockSpec((tokens, tn), lambda i, j: (0, jnp.clip(j - t * nb, 0, nb - 1)))

        in_specs = [pl.BlockSpec((tokens, tm), lambda i, j: (0, i))] + [piece_spec(t) for t in range(len(group))]
        operands = [a, *group]
        if out is not None:
            in_specs.append(pl.BlockSpec(memory_space=pl.ANY))
            operands.append(out)
        out = pl.pallas_call(
            body, name="%s_%d" % (name, first), grid=(m // tm, nb * len(group)), in_specs=in_specs,
            out_specs=pl.BlockSpec((tm, tn), lambda i, j, first=first: (i, first * nb + j)),
            out_shape=jax.ShapeDtypeStruct((m, len(pieces) * np_), F32),
            input_output_aliases={len(operands) - 1: 0} if out is not None else {},
            compiler_params=_params(("parallel", "arbitrary")),
        )(*operands)
    return out


def _weight_grad(a, g, *, name, jobs=()):
    tokens, m = a.shape
    tm = 1024 if m % 1024 == 0 else _tile(m, 1408)
    return _matmul(a, g, name=name, ta=True, tm=tm, tn=512, tk=tokens, jobs=jobs)


def _swiglu_fwd(n2, w_gate, w_up, *, tm=1024, tn=512, tk=2048, jobs=()):
    m, kdim = n2.shape
    n = w_gate.shape[1]
    tm, tn, tk = _tile(m, tm), _tile(n, tn), _tile(kdim, tk)
    nk = kdim // tk

    def body(a_ref, g_ref, u_ref, gate_ref, up_ref, act_ref, *acc):
        def products():
            a = a_ref[...]
            return _dot(a, g_ref[...]), _dot(a, u_ref[...])

        def finish(g, u):
            gate_ref[...] = g.astype(BF16)
            up_ref[...] = u.astype(BF16)
            act_ref[...] = (g * _sigmoid(g) * u).astype(BF16)

        _accumulate(acc, nk, products, finish)

    o_spec = pl.BlockSpec((tm, tn), lambda i, j, k: (i, j))
    o_shape = jax.ShapeDtypeStruct((m, n), BF16)
    return _call(
        body, name="swiglu_fwd", grid=(m // tm, n // tn, nk),
        in_specs=[_a_spec(False, tm, tk), _b_spec(False, tk, tn), _b_spec(False, tk, tn)],
        out_specs=[o_spec] * 3, out_shape=[o_shape] * 3, operands=[n2, w_gate, w_up],
        scratch_shapes=[pltpu.VMEM((tm, tn), F32)] * (2 * (nk > 1)),
        semantics=("parallel", "parallel", "arbitrary"), jobs=jobs)


def _swiglu_bwd_act(dh2, w_down, gate, up, *, tm=1024, tn=512, tk=2048):
    m, kdim = dh2.shape
    n = w_down.shape[0]
    tm, tn, tk = _tile(m, tm), _tile(n, tn), _tile(kdim, tk)
    nk = kdim // tk

    sub = _tile(tn, 256)

    def body(a_ref, b_ref, gate_ref, up_ref, dgate_ref, dup_ref, *acc):
        def finish(dact, cols=slice(None)):
            g = gate_ref[:, cols].astype(F32)
            u = up_ref[:, cols].astype(F32)
            sg = _sigmoid(g)
            dup_ref[:, cols] = (dact * g * sg).astype(BF16)
            dgate_ref[:, cols] = (dact * u * sg * (1.0 + g * (1.0 - sg))).astype(BF16)

        if nk == 1:
            a = a_ref[...]
            for c in range(tn // sub):
                cols = pl.ds(c * sub, sub)
                finish(_dot(a, b_ref[cols, :], tb=True), cols)
        else:
            _accumulate(acc, nk, lambda: (_dot(a_ref[...], b_ref[...], tb=True),), finish)

    o_spec = pl.BlockSpec((tm, tn), lambda i, j, k: (i, j))
    o_shape = jax.ShapeDtypeStruct((m, n), BF16)
    return pl.pallas_call(
        body, name="swiglu_bwd_act", grid=(m // tm, n // tn, nk),
        in_specs=[_a_spec(False, tm, tk), _b_spec(True, tk, tn), o_spec, o_spec],
        out_specs=[o_spec] * 2, out_shape=[o_shape] * 2,
        scratch_shapes=[pltpu.VMEM((tm, tn), F32)] * (nk > 1),
        compiler_params=_params(("parallel", "parallel", "arbitrary")),
    )(dh2, w_down, gate, up)


def _swiglu_bwd_in(dgate, dup, w_gate, w_up, *, tm=1024, tn=1024, tk=1408, jobs=()):
    m, kdim = dgate.shape
    n = w_gate.shape[0]
    tm, tn, tk = _tile(m, tm), _tile(n, tn), _tile(kdim, tk)
    nk = kdim // tk

    def body(a1_ref, a2_ref, b1_ref, b2_ref, o_ref, *acc):
        def product():
            return (_dot(a1_ref[...], b1_ref[...], tb=True) + _dot(a2_ref[...], b2_ref[...], tb=True),)

        def finish(total):
            o_ref[...] = total

        _accumulate(acc, nk, product, finish)

    a_spec, b_spec = _a_spec(False, tm, tk), _b_spec(True, tk, tn)
    (out,), carried = _call(
        body, name="swiglu_bwd_in", grid=(m // tm, n // tn, nk),
        in_specs=[a_spec, a_spec, b_spec, b_spec],
        out_specs=[pl.BlockSpec((tm, tn), lambda i, j, k: (i, j))],
        out_shape=[jax.ShapeDtypeStruct((m, n), F32)], operands=[dgate, dup, w_gate, w_up],
        scratch_shapes=[pltpu.VMEM((tm, tn), F32)] * (nk > 1),
        semantics=("parallel", "parallel", "arbitrary"), jobs=jobs)
    return out, carried


def _row_block(rows, cols):
    tr = min(rows, max(16, ELEM_BLOCK_BYTES // (4 * cols) // 16 * 16))
    while rows % tr:
        tr -= 16
    return tr


def _rmsnorm_fwd(x, g, *, name, after=None):
    s, d = x.shape
    tr = _row_block(s, d)

    def body(x_ref, g_ref, *rest):
        xv = x_ref[...]
        r = lax.rsqrt(jnp.mean(xv * xv, axis=-1, keepdims=True) + EPS)
        rest[-1][...] = (xv * r * g_ref[...]).astype(BF16)

    row = pl.BlockSpec((tr, d), lambda i: (i, 0))
    in_specs = [row, pl.BlockSpec((1, d), lambda i: (0, 0))]
    operands = [x, g]
    if after is not None:
        in_specs.append(pl.BlockSpec(after.shape, lambda i: (0, 0)))
        operands.append(after)
    return pl.pallas_call(
        body, name=name, grid=(s // tr,), in_specs=in_specs,
        out_specs=row, out_shape=jax.ShapeDtypeStruct((s, d), BF16),
        compiler_params=_params(("parallel",)),
    )(*operands)


def _rmsnorm_bwd_rows(xv, gv, dy):
    r = lax.rsqrt(jnp.mean(xv * xv, axis=-1, keepdims=True) + EPS)
    xhat = xv * r
    dxh = dy * gv
    dx = r * (dxh - xhat * jnp.mean(dxh * xhat, axis=-1, keepdims=True))
    return dx, dy * xhat


def _rmsnorm_bwd(dn, x, g, skip, *, name, after=()):
    s, d = x.shape
    tr = _row_block(s, d)

    def body(dn_ref, x_ref, g_ref, skip_ref, *rest):
        dx_ref, dxb_ref, dg_ref = rest[len(after):]
        dx, dgr = _rmsnorm_bwd_rows(x_ref[...], g_ref[...], dn_ref[...])
        dx = dx + skip_ref[...]
        dx_ref[...] = dx
        dxb_ref[...] = dx.astype(BF16)

        @pl.when(pl.program_id(0) == 0)
        def _():
            dg_ref[...] = jnp.zeros_like(dg_ref)

        dg_ref[...] += jnp.sum(dgr, axis=0, keepdims=True)

    row = pl.BlockSpec((tr, d), lambda i: (i, 0))
    vec = pl.BlockSpec((1, d), lambda i: (0, 0))
    return pl.pallas_call(
        body, name=name, grid=(s // tr,),
        in_specs=[row, row, vec, row] + [pl.BlockSpec(memory_space=pl.ANY)] * len(after),
        out_specs=[row, row, vec],
        out_shape=[jax.ShapeDtypeStruct((s, d), F32), jax.ShapeDtypeStruct((s, d), BF16),
                   jax.ShapeDtypeStruct((1, d), F32)],
        compiler_params=_params(("arbitrary",)),
    )(dn, x, g, skip, *after)


def _loss_head(h2, g, target):
    s, d = h2.shape
    tr = _row_block(s, d)

    def body(h_ref, g_ref, t_ref, dh_ref, dhb_ref, dg_ref, loss_ref):
        hv = h_ref[...]
        gv = g_ref[...]
        r = lax.rsqrt(jnp.mean(hv * hv, axis=-1, keepdims=True) + EPS)
        err = hv * r * gv - t_ref[...]
        dx, dgr = _rmsnorm_bwd_rows(hv, gv, err * (1.0 / d))
        dh_ref[...] = dx
        dhb_ref[...] = dx.astype(BF16)

        @pl.when(pl.program_id(0) == 0)
        def _():
            dg_ref[...] = jnp.zeros_like(dg_ref)
            loss_ref[...] = jnp.zeros_like(loss_ref)

        dg_ref[...] += jnp.sum(dgr, axis=0, keepdims=True)
        row_loss = jnp.mean(err * err, axis=-1, keepdims=True)
        loss_ref[...] += 0.5 * jnp.sum(row_loss, axis=0, keepdims=True)

    row = pl.BlockSpec((tr, d), lambda i: (i, 0))
    vec = pl.BlockSpec((1, d), lambda i: (0, 0))
    one = pl.BlockSpec((1, 1), lambda i: (0, 0))
    return pl.pallas_call(
        body, name="loss_head", grid=(s // tr,), in_specs=[row, vec, row],
        out_specs=[row, row, vec, one],
        out_shape=[jax.ShapeDtypeStruct((s, d), F32), jax.ShapeDtypeStruct((s, d), BF16),
                   jax.ShapeDtypeStruct((1, d), F32), jax.ShapeDtypeStruct((1, 1), F32)],
        compiler_params=_params(("arbitrary",)),
    )(h2, g, target)


def _attention_bias_tables():
    k = np.arange(-ATT_KB, ATT_KB + 1)[:, None, None]
    delta = k * ATT_BLOCK + np.arange(ATT_BLOCK)[None, None, :] - np.arange(ATT_BLOCK)[None, :, None]
    dist = np.abs(delta)
    count = np.zeros(delta.shape, np.int32)
    for window, dilation in DILATED_PATTERNS:
        count += (delta % dilation == 0) & (dist <= min(window // 2, ATT_NEAR))
    logc = np.where(count > 0, np.log(np.maximum(count, 1)), MASKED)
    return dist.astype(np.float32), logc.astype(np.float32)


def _far_bias_tables(per_class):
    steps = np.abs(np.arange(per_class)[:, None] - np.arange(per_class)[None, :]) * ATT_CLASSES
    valid = (steps > ATT_NEAR) & (steps <= ATT_REACH)
    return steps.astype(np.float32), np.where(valid, 0.0, MASKED).astype(np.float32)


def _to_classes(x):
    s, cols = x.shape
    return jnp.reshape(jnp.transpose(jnp.reshape(x, (s // ATT_CLASSES, ATT_CLASSES, cols)), (1, 0, 2)), (s, cols))


def _from_classes(x):
    s, cols = x.shape
    return jnp.reshape(jnp.transpose(jnp.reshape(x, (ATT_CLASSES, s // ATT_CLASSES, cols)), (1, 0, 2)), (s, cols))


def _head_bias(bias_ref, slope, dist_ref, logc_ref):
    for kk in range(ATT_WINDOW):
        bias_ref[kk] = logc_ref[kk] - slope * dist_ref[kk]
    bias_ref[ATT_WINDOW] = jnp.full((ATT_BLOCK, ATT_BLOCK), MASKED, F32)


def _window_start(i, nq, nwin):
    return jnp.clip(i - ATT_KB, 0, nq - nwin)


def _window_block(j, i):
    rows = pl.ds(pl.multiple_of(j * ATT_BLOCK, ATT_BLOCK), ATT_BLOCK)
    kk = j - i + ATT_KB
    return rows, jnp.where(jnp.logical_and(kk >= 0, kk < ATT_WINDOW), kk, ATT_WINDOW)


def _attention_far_fwd(qkv, slopes, n_heads, jobs=()):
    s = qkv.shape[0]
    per_class = s // ATT_CLASSES
    scale = HEAD_DIM ** -0.5
    dist, logc = _far_bias_tables(per_class)

    def body(slope_ref, q_ref, k_ref, v_ref, dist_ref, logc_ref, o_ref, lse_ref):
        bias = logc_ref[...] - slope_ref[pl.program_id(0)] * dist_ref[...]
        for a in range(ATT_FAR_GROUP):
            rows = pl.ds(a * per_class, per_class)
            sc = _dot(q_ref[rows, :], k_ref[rows, :], tb=True) * scale + bias
            m = jnp.maximum(jnp.max(sc, axis=-1, keepdims=True), ROW_MAX_INIT)
            p = jnp.exp(sc - m)
            l = jnp.maximum(jnp.sum(p, axis=-1, keepdims=True), 1e-30)
            o_ref[rows, :] = (_dot(p.astype(BF16), v_ref[rows, :]) / l).astype(BF16)
            lse_ref[rows, :] = jnp.broadcast_to(m + jnp.log(l), (per_class, HEAD_DIM))

    hh = n_heads
    blk = pl.BlockSpec((ATT_FAR_GROUP * per_class, HEAD_DIM), lambda h, r: (r, h))
    table = pl.BlockSpec(dist.shape, lambda h, r: (0, 0))
    return _call(
        body, name="attention_far_fwd", grid=(hh, ATT_CLASSES // ATT_FAR_GROUP),
        in_specs=[pl.BlockSpec(memory_space=pltpu.SMEM), blk,
                  pl.BlockSpec((ATT_FAR_GROUP * per_class, HEAD_DIM), lambda h, r: (r, hh + h)),
                  pl.BlockSpec((ATT_FAR_GROUP * per_class, HEAD_DIM), lambda h, r: (r, 2 * hh + h)), table, table],
        out_specs=[blk, blk],
        out_shape=[jax.ShapeDtypeStruct((s, hh * HEAD_DIM), BF16), jax.ShapeDtypeStruct((s, hh * HEAD_DIM), F32)],
        operands=[slopes, qkv, qkv, qkv, jnp.asarray(dist), jnp.asarray(logc)],
        semantics=("parallel", "parallel"), jobs=jobs)


def _attention_fwd(proj, slopes, far_out, far_lse, n_heads, jobs=()):
    s = proj.shape[0]
    nq = s // ATT_BLOCK
    scale = HEAD_DIM ** -0.5
    dist, logc = _attention_bias_tables()

    nwin = min(ATT_WINDOW, nq)

    group = math.gcd(ATT_NEAR_GROUP, nq)

    def body(slope_ref, q_ref, k_ref, v_ref, fo_ref, fl_ref, dist_ref, logc_ref, o_ref, lse_ref, bias_ref, s_ref):
        h, step = pl.program_id(0), pl.program_id(1)

        @pl.when(step == 0)
        def _():
            _head_bias(bias_ref, slope_ref[h], dist_ref, logc_ref)

        for a in range(group):
            i = step * group + a
            mine = pl.ds(a * ATT_BLOCK, ATT_BLOCK)
            q = q_ref[mine, :]
            first = _window_start(i, nq, nwin)
            m = jnp.full((ATT_BLOCK, 1), ROW_MAX_INIT, F32)
            for b in range(nwin):
                rows, kk = _window_block(first + b, i)
                sc = _dot(q, k_ref[rows, :], tb=True) * scale + bias_ref[kk]
                s_ref[a * nwin + b] = sc
                m = jnp.maximum(m, jnp.max(sc, axis=-1, keepdims=True))
            l = jnp.zeros((ATT_BLOCK, 1), F32)
            acc = jnp.zeros((ATT_BLOCK, HEAD_DIM), F32)
            for b in range(nwin):
                rows, _ = _window_block(first + b, i)
                p = jnp.exp(s_ref[a * nwin + b] - m)
                l = l + jnp.sum(p, axis=-1, keepdims=True)
                acc = acc + _dot(p.astype(BF16), v_ref[rows, :])
            near_lse = m + jnp.log(l)
            far_lse_col = fl_ref[mine, :1]
            lse = jnp.maximum(near_lse, far_lse_col)
            lse = lse + jnp.log(jnp.exp(near_lse - lse) + jnp.exp(far_lse_col - lse))
            o_ref[mine, :] = (acc * (jnp.exp(near_lse - lse) / l)
                              + fo_ref[mine, :].astype(F32) * jnp.exp(far_lse_col - lse)).astype(BF16)
            lse_ref[mine, :] = jnp.broadcast_to(lse, (ATT_BLOCK, HEAD_DIM))

    hh = n_heads
    blk = pl.BlockSpec((group * ATT_BLOCK, HEAD_DIM), lambda h, i: (i, h))
    table = pl.BlockSpec(dist.shape, lambda h, i: (0, 0, 0))
    return _call(
        body, name="attention_fwd", grid=(hh, nq // group),
        in_specs=[pl.BlockSpec(memory_space=pltpu.SMEM), blk,
                  pl.BlockSpec((s, HEAD_DIM), lambda h, i: (0, hh + h)),
                  pl.BlockSpec((s, HEAD_DIM), lambda h, i: (0, 2 * hh + h)), blk, blk, table, table],
        out_specs=[blk, blk],
        out_shape=[jax.ShapeDtypeStruct((s, hh * HEAD_DIM), BF16), jax.ShapeDtypeStruct((s, hh * HEAD_DIM), F32)],
        operands=[slopes, proj, proj, proj, far_out, far_lse, jnp.asarray(dist), jnp.asarray(logc)],
        scratch_shapes=[pltpu.VMEM((ATT_WINDOW + 1, ATT_BLOCK, ATT_BLOCK), F32),
                        pltpu.VMEM((group * nwin, ATT_BLOCK, ATT_BLOCK), F32)],
        semantics=("parallel", "arbitrary"), jobs=jobs)


def _attention_far_bwd(qkv, slopes, out, dout, lse, n_heads):
    s = qkv.shape[0]
    per_class = s // ATT_CLASSES
    scale = HEAD_DIM ** -0.5
    dist, logc = _far_bias_tables(per_class)

    def body(slope_ref, q_ref, k_ref, v_ref, o_ref, do_ref, lse_ref, dist_ref, logc_ref, dq_ref, dk_ref, dv_ref):
        bias = logc_ref[...] - slope_ref[pl.program_id(0)] * dist_ref[...]
        for a in range(ATT_FAR_GROUP):
            rows = pl.ds(a * per_class, per_class)
            q, k, do = q_ref[rows, :], k_ref[rows, :], do_ref[rows, :]
            delta = jnp.sum(do.astype(F32) * o_ref[rows, :].astype(F32), axis=-1, keepdims=True)
            p = jnp.exp(_dot(q, k, tb=True) * scale + bias - lse_ref[rows, :1])
            dv_ref[rows, :] = _dot(p.astype(BF16), do, ta=True).astype(BF16)
            ds = (p * (_dot(do, v_ref[rows, :], tb=True) - delta) * scale).astype(BF16)
            dk_ref[rows, :] = _dot(ds, q, ta=True).astype(BF16)
            dq_ref[rows, :] = _dot(ds, k).astype(BF16)

    hh = n_heads
    blk = pl.BlockSpec((ATT_FAR_GROUP * per_class, HEAD_DIM), lambda h, r: (r, h))
    table = pl.BlockSpec(dist.shape, lambda h, r: (0, 0))
    o_shape = jax.ShapeDtypeStruct((s, hh * HEAD_DIM), BF16)
    return pl.pallas_call(
        body, name="attention_far_bwd", grid=(hh, ATT_CLASSES // ATT_FAR_GROUP),
        in_specs=[pl.BlockSpec(memory_space=pltpu.SMEM), blk,
                  pl.BlockSpec((ATT_FAR_GROUP * per_class, HEAD_DIM), lambda h, r: (r, hh + h)),
                  pl.BlockSpec((ATT_FAR_GROUP * per_class, HEAD_DIM), lambda h, r: (r, 2 * hh + h)),
                  blk, blk, blk, table, table],
        out_specs=[blk] * 3, out_shape=[o_shape] * 3,
        compiler_params=_params(("parallel", "parallel")),
    )(slopes, qkv, qkv, qkv, out, dout, lse, jnp.asarray(dist), jnp.asarray(logc))


def _attention_bwd(proj, slopes, out, lse, dmixed, far_grads, n_heads, jobs=()):
    s = proj.shape[0]
    nq = s // ATT_BLOCK
    scale = HEAD_DIM ** -0.5
    dist, logc = _attention_bias_tables()

    nwin = min(ATT_WINDOW, nq)
    group = math.gcd(ATT_NEAR_GROUP, nq)

    def body(slope_ref, q_ref, k_ref, v_ref, o_ref, do_ref, lse_ref, fdq_ref, fdk_ref, fdv_ref, dist_ref, logc_ref,
             dq_ref, dk_ref, dv_ref, dk_acc, dv_acc, bias_ref):
        h, step = pl.program_id(0), pl.program_id(1)

        @pl.when(step == 0)
        def _():
            dk_acc[...] = jnp.zeros_like(dk_acc)
            dv_acc[...] = jnp.zeros_like(dv_acc)
            _head_bias(bias_ref, slope_ref[h], dist_ref, logc_ref)

        for a in range(group):
            i = step * group + a
            mine = pl.ds(a * ATT_BLOCK, ATT_BLOCK)
            q = q_ref[mine, :]
            do = do_ref[mine, :]
            lse_col = lse_ref[mine, :1]
            delta = jnp.sum(do.astype(F32) * o_ref[mine, :].astype(F32), axis=-1, keepdims=True)
            first = _window_start(i, nq, nwin)
            dq = jnp.zeros((ATT_BLOCK, HEAD_DIM), F32)
            for b in range(nwin):
                rows, kk = _window_block(first + b, i)
                kj = k_ref[rows, :]
                vj = v_ref[rows, :]
                p = jnp.exp(_dot(q, kj, tb=True) * scale + bias_ref[kk] - lse_col)
                dv_acc[rows, :] += _dot(p.astype(BF16), do, ta=True)
                dp = _dot(do, vj, tb=True)
                ds = (p * (dp - delta) * scale).astype(BF16)
                dk_acc[rows, :] += _dot(ds, q, ta=True)
                dq = dq + _dot(ds, kj)
            dq_ref[mine, :] = (dq + fdq_ref[mine, :].astype(F32)).astype(BF16)

        @pl.when(step == nq // group - 1)
        def _():
            dk_ref[...] = (dk_acc[...] + fdk_ref[...].astype(F32)).astype(BF16)
            dv_ref[...] = (dv_acc[...] + fdv_ref[...].astype(F32)).astype(BF16)

    hh = n_heads
    blk = pl.BlockSpec((group * ATT_BLOCK, HEAD_DIM), lambda h, i: (i, h))
    col = pl.BlockSpec((s, HEAD_DIM), lambda h, i: (0, h))
    table = pl.BlockSpec(dist.shape, lambda h, i: (0, 0, 0))
    o_shape = jax.ShapeDtypeStruct((s, hh * HEAD_DIM), BF16)
    return _call(
        body, name="attention_bwd", grid=(hh, nq // group),
        in_specs=[pl.BlockSpec(memory_space=pltpu.SMEM), blk,
                  pl.BlockSpec((s, HEAD_DIM), lambda h, i: (0, hh + h)),
                  pl.BlockSpec((s, HEAD_DIM), lambda h, i: (0, 2 * hh + h)),
                  blk, blk, blk, blk, col, col, table, table],
        out_specs=[blk, col, col], out_shape=[o_shape] * 3,
        operands=[slopes, proj, proj, proj, out, dmixed, lse, *far_grads, jnp.asarray(dist), jnp.asarray(logc)],
        scratch_shapes=[pltpu.VMEM((s, HEAD_DIM), F32)] * 2
        + [pltpu.VMEM((ATT_WINDOW + 1, ATT_BLOCK, ATT_BLOCK), F32)],
        semantics=("parallel", "arbitrary"), jobs=jobs)


def _ret_decays(lgc, lga, strict_c, strict_a):
    c = RET_CHUNK
    rel = (lax.broadcasted_iota(jnp.int32, (c, c), 0) - lax.broadcasted_iota(jnp.int32, (c, c), 1)).astype(F32)
    in_c = (rel > 0) if strict_c else (rel >= 0)
    in_a = (rel < 0) if strict_a else (rel <= 0)
    mask = (jnp.where(in_c, jnp.exp(lgc * jnp.maximum(rel, 0.0)), 0.0)
            + jnp.where(in_a, jnp.exp(lga * jnp.maximum(-rel, 0.0)), 0.0))
    idx = lax.broadcasted_iota(jnp.int32, (c, 1), 0).astype(F32)
    ones = jnp.ones((1, HEAD_DIM), F32)
    dec = dict(
        rel=rel, mask=mask, idx=idx,
        a_c=jnp.exp(lgc * (idx + 1.0)), b_c=jnp.exp(lgc * (c - 1.0 - idx)), chunk_c=jnp.exp(ones * (lgc * c)),
        a_a=jnp.exp(lga * (c - idx)), b_a=jnp.exp(lga * idx), chunk_a=jnp.exp(ones * (lga * c)),
    )
    return dec


def _scaled(x, col):
    return (x.astype(F32) * col).astype(BF16)


def _chunk_rows(i):
    return pl.ds(pl.multiple_of(i * RET_CHUNK, RET_CHUNK), RET_CHUNK)


def _chunk_loop(nc, step, init, unroll=RET_UNROLL):
    group = math.gcd(nc, unroll)

    def trip(t, carry):
        for u in range(group):
            carry = step(t * group + u, carry)
        return carry

    return lax.fori_loop(0, nc // group, trip, init)


def _retention(a, b, c, lg_c, lg_a, *, strict_c, strict_a, scale, n_heads, name, gate=None, norm_w=None, jobs=()):
    s = a[0].shape[0]
    nc = s // RET_CHUNK
    epilogue = gate is not None

    def body(*refs):
        lgc_ref, lga_ref, a_ref, b_ref, c_ref = refs[:5]
        if epilogue:
            g_ref, w_ref, o_ref, mix_ref, sa_ref = refs[5:]
        else:
            o_ref, sa_ref = refs[5:]
        h = pl.program_id(0)
        dec = _ret_decays(lgc_ref[h], lga_ref[h], strict_c, strict_a)

        def reverse(t, state):
            i = nc - 1 - t
            sa_ref[i] = state.astype(BF16)
            rows = _chunk_rows(i)
            return state * dec["chunk_a"] + _dot(_scaled(b_ref[rows, :], dec["b_a"]), c_ref[rows, :], ta=True)

        _chunk_loop(nc, reverse, jnp.zeros((HEAD_DIM, HEAD_DIM), F32))

        def forward(i, state):
            rows = _chunk_rows(i)
            ai, bi, ci = a_ref[rows, :], b_ref[rows, :], c_ref[rows, :]
            inner = (_dot(ai, bi, tb=True) * dec["mask"]).astype(BF16)
            out = (_dot(inner, ci) + _dot(_scaled(ai, dec["a_c"]), state.astype(BF16))
                   + _dot(_scaled(ai, dec["a_a"]), sa_ref[i])) * scale
            o_ref[rows, :] = out.astype(BF16)
            if epilogue:
                r = lax.rsqrt(jnp.mean(out * out, axis=-1, keepdims=True) + EPS)
                g = g_ref[rows, :].astype(F32)
                mix_ref[rows, :] = (out * r * w_ref[...] * (g * _sigmoid(g))).astype(BF16)
            return state * dec["chunk_c"] + _dot(_scaled(bi, dec["b_c"]), ci, ta=True)

        _chunk_loop(nc, forward, jnp.zeros((HEAD_DIM, HEAD_DIM), F32))

    def col(first):
        return pl.BlockSpec((s, HEAD_DIM), lambda h: (0, first + h))

    smem = pl.BlockSpec(memory_space=pltpu.SMEM)
    in_specs = [smem, smem, col(a[1]), col(b[1]), col(c[1])]
    operands = [lg_c, lg_a, a[0], b[0], c[0]]
    o_shape = jax.ShapeDtypeStruct((s, n_heads * HEAD_DIM), BF16)
    out_specs, out_shape = [col(0)], [o_shape]
    if epilogue:
        in_specs += [col(gate[1]), pl.BlockSpec((1, HEAD_DIM), lambda h: (0, h))]
        operands += [gate[0], norm_w]
        out_specs, out_shape = [col(0)] * 2, [o_shape] * 2
    res, carried = _call(
        body, name=name, grid=(n_heads,), in_specs=in_specs, out_specs=out_specs, out_shape=out_shape,
        operands=operands, scratch_shapes=[pltpu.VMEM((nc, HEAD_DIM, HEAD_DIM), BF16)],
        semantics=("parallel",), jobs=jobs)
    res = res if epilogue else res[0]
    return (res, carried) if jobs else res


def _retention_decay_grads(a, b, c, e, lg_c, lg_a, *, scale, n_heads):
    s = a[0].shape[0]
    nc = s // RET_CHUNK
    cf = float(RET_CHUNK)

    def body(lgc_ref, lga_ref, a_ref, b_ref, c_ref, e_ref, gc_ref, ga_ref, sa_ref, ta_ref):
        h = pl.program_id(0)
        lgc, lga = lgc_ref[h], lga_ref[h]
        dec = _ret_decays(lgc, lga, True, True)
        rel, idx = dec["rel"], dec["idx"]
        w_c = jnp.where(rel > 0, rel * jnp.exp(lgc * jnp.maximum(rel, 0.0)), 0.0)
        w_a = jnp.where(rel < 0, -rel * jnp.exp(lga * jnp.maximum(-rel, 0.0)), 0.0)
        zero = jnp.zeros((HEAD_DIM, HEAD_DIM), F32)

        def reverse(t, carry):
            st, dst = carry
            i = nc - 1 - t
            sa_ref[i] = st.astype(BF16)
            ta_ref[i] = dst.astype(BF16)
            rows = _chunk_rows(i)
            bi, ci = b_ref[rows, :], c_ref[rows, :]
            st_new = st * dec["chunk_a"] + _dot(_scaled(bi, dec["b_a"]), ci, ta=True)
            dst_new = (cf * st + dst) * dec["chunk_a"] + _dot(_scaled(bi, idx * dec["b_a"]), ci, ta=True)
            return st_new, dst_new

        _chunk_loop(nc, reverse, (zero, zero))

        def forward(i, carry):
            st, dst, acc_c, acc_a = carry
            rows = _chunk_rows(i)
            ai, bi, ci = a_ref[rows, :], b_ref[rows, :], c_ref[rows, :]
            ev = e_ref[rows, :].astype(F32)
            pg = _dot(ai, bi, tb=True) * _dot(e_ref[rows, :], ci, tb=True)
            a_c, a_a = _scaled(ai, dec["a_c"]), _scaled(ai, dec["a_a"])
            inter_c = _dot(a_c, st.astype(BF16)) * (idx + 1.0) + _dot(a_c, dst.astype(BF16))
            inter_a = _dot(a_a, sa_ref[i]) * (cf - idx) + _dot(a_a, ta_ref[i])
            acc_c = acc_c + jnp.sum(pg * w_c, axis=0, keepdims=True) + jnp.sum(inter_c * ev, axis=0, keepdims=True)
            acc_a = acc_a + jnp.sum(pg * w_a, axis=0, keepdims=True) + jnp.sum(inter_a * ev, axis=0, keepdims=True)
            st_new = st * dec["chunk_c"] + _dot(_scaled(bi, dec["b_c"]), ci, ta=True)
            dst_new = ((cf * st + dst) * dec["chunk_c"]
                       + _dot(_scaled(bi, (cf - 1.0 - idx) * dec["b_c"]), ci, ta=True))
            return st_new, dst_new, acc_c, acc_a

        row = jnp.zeros((1, HEAD_DIM), F32)
        _, _, acc_c, acc_a = _chunk_loop(nc, forward, (zero, zero, row, row))
        gc_ref[...] = jnp.broadcast_to(jnp.sum(acc_c, axis=-1, keepdims=True) * scale, gc_ref.shape)
        ga_ref[...] = jnp.broadcast_to(jnp.sum(acc_a, axis=-1, keepdims=True) * scale, ga_ref.shape)

    def col(first):
        return pl.BlockSpec((s, HEAD_DIM), lambda h: (0, first + h))

    smem = pl.BlockSpec(memory_space=pltpu.SMEM)
    o_spec = pl.BlockSpec((1, 8, HEAD_DIM), lambda h: (h, 0, 0))
    o_shape = jax.ShapeDtypeStruct((n_heads, 8, HEAD_DIM), F32)
    gc, ga = pl.pallas_call(
        body, name="retention_decay_grads", grid=(n_heads,),
        in_specs=[smem, smem, col(a[1]), col(b[1]), col(c[1]), col(e[1])],
        out_specs=[o_spec] * 2, out_shape=[o_shape] * 2,
        scratch_shapes=[pltpu.VMEM((nc, HEAD_DIM, HEAD_DIM), BF16)] * 2,
        compiler_params=_params(("parallel",)),
    )(lg_c, lg_a, a[0], b[0], c[0], e[0])
    return gc[:, 0, 0], ga[:, 0, 0]


def _ret_gate_bwd(dmixed, first_col, out, proj, gate_col, norm_w, n_heads):
    s = out.shape[0]
    tr = _row_block(s, 8 * HEAD_DIM)

    def body(dm_ref, o_ref, g_ref, w_ref, do_ref, dg_ref, dw_ref):
        dm = dm_ref[...].astype(F32)
        ov = o_ref[...].astype(F32)
        g = g_ref[...].astype(F32)
        w = w_ref[...]
        r = lax.rsqrt(jnp.mean(ov * ov, axis=-1, keepdims=True) + EPS)
        ohat = ov * r
        sg = _sigmoid(g)
        silu = g * sg
        dg_ref[...] = (dm * ohat * w * sg * (1.0 + g * (1.0 - sg))).astype(BF16)
        dohat = dm * w * silu
        do_ref[...] = (r * (dohat - ohat * jnp.mean(dohat * ohat, axis=-1, keepdims=True))).astype(BF16)

        @pl.when(pl.program_id(1) == 0)
        def _():
            dw_ref[...] = jnp.zeros_like(dw_ref)

        dw_ref[...] += jnp.sum(dm * ohat * silu, axis=0, keepdims=True)

    def blk(first):
        return pl.BlockSpec((tr, HEAD_DIM), lambda h, i: (i, first + h))

    vec = pl.BlockSpec((1, HEAD_DIM), lambda h, i: (0, h))
    o_shape = jax.ShapeDtypeStruct((s, n_heads * HEAD_DIM), BF16)
    return pl.pallas_call(
        body, name="ret_gate_bwd", grid=(n_heads, s // tr),
        in_specs=[blk(first_col), blk(0), blk(gate_col), vec],
        out_specs=[blk(0), blk(0), vec],
        out_shape=[o_shape, o_shape, jax.ShapeDtypeStruct((1, n_heads * HEAD_DIM), F32)],
        compiler_params=_params(("parallel", "arbitrary")),
    )(dmixed, out, proj, norm_w)


def _step(x, target, norm_mix_w, ret_decay_fwd, ret_decay_bwd, ret_norm_w, norm_ffn_w, norm_final_w, own,
          w_in_started, pos):
    d = x.shape[1]
    nh = d // (2 * HEAD_DIM)
    scale = HEAD_DIM ** -0.5
    slopes = jnp.exp2(-8.0 * jnp.arange(1, nh + 1, dtype=F32) / nh)
    lg_f = -jnp.exp(ret_decay_fwd)
    lg_b = -jnp.exp(ret_decay_bwd)
    q_r, k_r, v_r, g_r = 3 * nh, 4 * nh, 5 * nh, 6 * nh
    ax = BIG_AXIS

    def gather(names, arrays, stage, part=None):
        return _gather_job(arrays, [ax[k] for k in names], stage, part)

    def add_halves(k, g, received):
        return _add_halves(g, received, ax[k], pos, name="grad_add_halves_" + k)

    def sum_parts(k, g, received, parts):
        return _sum_chip_parts(g, received, parts, ax[k], pos, name="grad_sum_parts_" + k)

    sems, w_in, token = w_in_started
    n1 = _rmsnorm_fwd(x, norm_mix_w, name="norm_mix_fwd", after=token)
    w_in = _split_gather_wait(sems, w_in, ax["w_in"], [n1] + [own[k] for k in BIG if k != "w_in"])
    (w_in,) = _run_jobs([gather(["w_in"], [w_in], "d2d")], name="all_gather_w_in_sibling")
    proj, [[w_gate]] = _matmul(n1, w_in, name="in_proj", out_dtype=BF16, tm=2048,
                               jobs=[gather(["w_gate"], [own["w_gate"]], "ici")])
    qkv_classes = _to_classes(proj[:, :3 * nh * HEAD_DIM])
    (ret, ret_mixed), [[w_gate], [w_out]] = _retention(
        (proj, q_r), (proj, k_r), (proj, v_r), lg_f, lg_b, strict_c=False, strict_a=True, scale=scale, n_heads=nh,
        name="retention_fwd", gate=(proj, g_r), norm_w=ret_norm_w,
        jobs=[gather(["w_gate"], [w_gate], "d2d"), gather(["w_out"], [own["w_out"]], "ici")])
    (far_out, far_lse), [[w_up]] = _attention_far_fwd(
        qkv_classes, slopes, nh, jobs=[gather(["w_up"], [own["w_up"]], "ici", (0, 1, 4))])
    (attn, lse), [[w_out], [w_up]] = _attention_fwd(
        proj, slopes, _from_classes(far_out), _from_classes(far_lse), nh,
        jobs=[gather(["w_out"], [w_out], "d2d"),
              _fuse(gather(["w_up"], [w_up], "d2d", (0, 1, 4)), gather(["w_up"], [w_up], "ici", (1, 2, 4)))])
    mixed = jnp.concatenate([attn, ret_mixed], axis=1)
    h1, [[w_up]] = _matmul(
        mixed, w_out, name="out_proj", residual=x,
        jobs=[_fuse(gather(["w_up"], [w_up], "d2d", (1, 2, 4)), gather(["w_up"], [w_up], "ici", (3, 1, 4)))])
    up_sibling = _split_start(gather(["w_up"], [w_up], "d2d", (3, 1, 4)), name="all_gather_w_up_sibling_start")
    n2 = _rmsnorm_fwd(h1, norm_ffn_w, name="norm_ffn_fwd", after=up_sibling["token"])
    (w_up,) = _split_wait(up_sibling, [n2], name="all_gather_w_up_sibling_wait")
    (gate, up, act), [[w_down]] = _swiglu_fwd(n2, w_gate, w_up, jobs=[gather(["w_down"], [own["w_down"]], "ici")])
    (w_down,) = _run_jobs([gather(["w_down"], [w_down], "d2d")], name="all_gather_w_down_sibling")
    h2 = _matmul(act, w_down, name="down_proj", residual=h1, tk=2816)
    dh2, dh2_b, d_norm_final, loss = _loss_head(h2, norm_final_w, target)

    dgate, dup = _swiglu_bwd_act(dh2_b, w_down, gate, up)
    g_down = _weight_grad(act, dh2_b, name="grad_w_down")
    g_gate, [[r_down]] = _weight_grad(n2, dgate, name="grad_w_gate", jobs=[_exchange_job([g_down], [ax["w_down"]])])
    s_down = add_halves("w_down", g_down, r_down)
    g_up, [[r_gate], [p_down]] = _weight_grad(
        n2, dup, name="grad_w_up",
        jobs=[_exchange_job([g_gate], [ax["w_gate"]]), _send_sums_job([s_down], [ax["w_down"]], (0, 1, 2))])
    s_gate = add_halves("w_gate", g_gate, r_gate)
    dn2, [[r_up], [p_gate], [p_down]] = _swiglu_bwd_in(
        dgate, dup, w_gate, w_up,
        jobs=[_exchange_job([g_up], [ax["w_up"]]), _send_sums_job([s_gate], [ax["w_gate"]]),
              _send_sums_job([s_down], [ax["w_down"]], (1, 1, 2), landing=[p_down])])
    h_down = sum_parts("w_down", g_down, r_down, p_down)
    s_up = add_halves("w_up", g_up, r_up)
    h_gate = sum_parts("w_gate", g_gate, r_gate, p_gate)
    dh1, dh1_b, d_norm_ffn = _rmsnorm_bwd(dn2, h1, norm_ffn_w, dh2, name="norm_ffn_bwd")

    dmixed, [[gr_down], [p_up]] = _matmul(
        dh1_b, w_out, name="out_proj_bwd", tb=True, out_dtype=BF16,
        jobs=[_join_job([h_down], [ax["w_down"]]), _send_sums_job([s_up], [ax["w_up"]], (0, 1, 4))])
    far_in = [_to_classes(t) for t in (attn, dmixed[:, :nh * HEAD_DIM], lse)]
    g_out, [[p_up]] = _weight_grad(mixed, dh1_b, name="grad_w_out",
                                   jobs=[_send_sums_job([s_up], [ax["w_up"]], (1, 1, 4), landing=[p_up])])
    d_ret, dg_r, d_ret_norm = _ret_gate_bwd(dmixed, nh, ret, proj, g_r, ret_norm_w, nh)
    far_grads = _attention_far_bwd(qkv_classes, slopes, *far_in, nh)
    far_grads = [_from_classes(t) for t in far_grads]
    dq_r, [[gr_gate], [p_up]] = _retention(
        (d_ret, 0), (proj, v_r), (proj, k_r), lg_f, lg_b, strict_c=False, strict_a=True, scale=scale, n_heads=nh,
        name="retention_dq",
        jobs=[_join_job([h_gate], [ax["w_gate"]]), _send_sums_job([s_up], [ax["w_up"]], (2, 1, 4), landing=[p_up])])
    (dq_a, dk_a, dv_a), [[p_up], [r_out]] = _attention_bwd(
        proj, slopes, attn, lse, dmixed, far_grads, nh,
        jobs=[_send_sums_job([s_up], [ax["w_up"]], (3, 1, 4), landing=[p_up]),
              _exchange_job([g_out], [ax["w_out"]])])
    s_out = add_halves("w_out", g_out, r_out)
    h_up = sum_parts("w_up", g_up, r_up, p_up)
    dv_r, [[p_out], [gr_up]] = _retention(
        (proj, k_r), (proj, q_r), (d_ret, 0), lg_b, lg_f, strict_c=True, strict_a=False, scale=scale, n_heads=nh,
        name="retention_dv", jobs=[_send_sums_job([s_out], [ax["w_out"]]), _join_job([h_up], [ax["w_up"]])])
    h_out = sum_parts("w_out", g_out, r_out, p_out)
    dk_r, [[gr_out]] = _retention(
        (proj, v_r), (d_ret, 0), (proj, q_r), lg_b, lg_f, strict_c=True, strict_a=False, scale=scale, n_heads=nh,
        name="retention_dk", jobs=[_join_job([h_out], [ax["w_out"]])])
    dlg_f, dlg_b = _retention_decay_grads((proj, q_r), (proj, k_r), (proj, v_r), (d_ret, 0), lg_f, lg_b,
                                          scale=scale, n_heads=nh)
    dproj = [dq_a, dk_a, dv_a, dq_r, dk_r, dv_r, dg_r]
    g_in = _weight_grad_pieces(n1, dproj, name="grad_w_in")
    exchange = _split_start(_exchange_job([g_in], [ax["w_in"]]), name="grad_exchange_w_in_start")
    dn1 = _matmul_pieces_nt(dproj, w_in, name="in_proj_bwd", after=[exchange["token"]])
    g_in, r_in = _split_wait(exchange, [dn1], name="grad_exchange_w_in_wait")
    s_in = add_halves("w_in", g_in, r_in)
    sending = _split_start(_send_sums_job([s_in], [ax["w_in"]]), name="grad_send_w_in_start")
    dx, _, d_norm_mix = _rmsnorm_bwd(dn1, x, norm_mix_w, dh1, name="norm_mix_bwd", after=[sending["token"]])

    small = dict(loss=loss[0, 0], norm_mix_w=d_norm_mix, ret_decay_fwd=dlg_f * lg_f, ret_decay_bwd=dlg_b * lg_b,
                 ret_norm_w=d_ret_norm, norm_ffn_w=d_norm_ffn, norm_final_w=d_norm_final)
    return (dx, dict(w_out=gr_out, w_gate=gr_gate, w_up=gr_up, w_down=gr_down), small,
            dict(sending=sending, grad=g_in, received=r_in))


def _mesh_position():
    x, y, c = lax.axis_index("x"), lax.axis_index("y"), lax.axis_index("c")
    chips = [(1 - x, y), (x, 1 - y), (1 - x, 1 - y)]
    return x, y, c, chips


def _span(span):
    if span is None:
        return slice(None)
    start, size, step = span
    return pl.ds(start if isinstance(start, int) else pl.multiple_of(start, step), size)


def _part_rows(part, rows):
    first, count, of = part
    return first * (rows // of), count * (rows // of), rows // of


def _region(ref, axis, shard, half, shard_size, half_size, part=None, total_rows=None):
    along = None if shard is None else (shard * shard_size, shard_size, shard_size)
    other = None if half is None else (half * half_size, half_size, half_size)
    rows, cols = (other, along) if axis == 1 else (along, other)
    if part is not None:
        start, size, _ = rows if rows is not None else (0, total_rows, None)
        offset, size, step = _part_rows(part, size)
        rows = (start + offset, size, step)
    return ref.at[_span(rows), _span(cols)]


def _fuse(first, second):
    assert not (first.ins or first.outs or second.ins or second.outs)
    assert len(first.ios) == len(second.ios) and all(a is b for a, b in zip(first.ios, second.ios))
    cut = len(first.sems)

    def start(refs, sems):
        first.start(refs, sems[:cut])
        second.start(refs, sems[cut:])

    def finish(refs, sems):
        first.finish(refs, sems[:cut])
        second.finish(refs, sems[cut:])

    return _Job(ios=first.ios, sems=first.sems + second.sems, start=start, finish=finish)


def _gather_job(full, axes, stage, part=None):
    n = len(full)

    def copies(refs, sems):
        send_sem, recv_sem = sems
        x, y, c, chips = _mesh_position()
        me = 2 * x + y

        def copy(w, k, shard, half, target):
            rows_cols = full[w].shape
            place = _region(refs[w], axes[w], shard, half, rows_cols[axes[w]] // N_CHIPS, rows_cols[1 - axes[w]] // 2,
                            part)
            return pltpu.make_async_remote_copy(
                src_ref=place, dst_ref=place, send_sem=send_sem.at[w, k], recv_sem=recv_sem.at[w, k],
                device_id=target, device_id_type=MESH)

        def sent(w, k):
            if stage == "ici":
                return copy(w, k, me, c, (chips[k][0], chips[k][1], c))
            return copy(w, k, 2 * chips[k][0] + chips[k][1], c, (x, y, 1 - c))

        def landed(w, k):
            return copy(w, k, 2 * chips[k][0] + chips[k][1], c if stage == "ici" else 1 - c, (x, y, 1 - c))

        return sent, landed

    def start(refs, sems):
        sent, _ = copies(refs, sems)
        for w in range(n):
            for k in range(3):
                sent(w, k).start()

    def finish(refs, sems):
        sent, landed = copies(refs, sems)
        for w in range(n):
            for k in range(3):
                landed(w, k).wait_recv()
                sent(w, k).wait_send()

    return _Job(ios=full, sems=[pltpu.SemaphoreType.DMA((n, 3))] * 2, start=start, finish=finish)


def _exchange_job(grads, axes):
    n = len(grads)

    def half_shape(w):
        return tuple(d // 2 if a != axes[w] else d for a, d in enumerate(grads[w].shape))

    def copy(refs, sems, w):
        x, y, c, _ = _mesh_position()
        return pltpu.make_async_remote_copy(
            src_ref=_region(refs[w], axes[w], None, 1 - c, 0, half_shape(w)[1 - axes[w]]), dst_ref=refs[n + w],
            send_sem=sems[0].at[w], recv_sem=sems[1].at[w], device_id=(x, y, 1 - c), device_id_type=MESH)

    def start(refs, sems):
        for w in range(n):
            copy(refs, sems, w).start()

    def finish(refs, sems):
        for w in range(n):
            copy(refs, sems, w).wait()

    return _Job(ins=grads, outs=[jax.ShapeDtypeStruct(half_shape(w), F32) for w in range(n)],
                sems=[pltpu.SemaphoreType.DMA((n,))] * 2, start=start, finish=finish)


def _half_block_spec(axis, block, half_blocks, use_half):
    if axis == 1:
        if use_half:
            return pl.BlockSpec(block, lambda i, pos: (pos[0] * half_blocks + i, 0))
        return pl.BlockSpec(block, lambda i, pos: (i, 0))
    if use_half:
        return pl.BlockSpec(block, lambda i, pos: (i, pos[0]))
    return pl.BlockSpec(block, lambda i, pos: (i, 0))


def _add_halves(grad, received, axis, pos, *, name):
    rows, cols = received.shape
    tr = _row_block(rows, cols)
    nb = rows // tr

    def body(pos_ref, g_ref, r_ref, o_ref):
        o_ref[...] = (g_ref[...] + r_ref[...]).astype(BF16)

    blk = (tr, cols)
    return pl.pallas_call(
        body, name=name, out_shape=jax.ShapeDtypeStruct((rows, cols), BF16),
        grid_spec=pltpu.PrefetchScalarGridSpec(
            num_scalar_prefetch=1, grid=(nb,),
            in_specs=[_half_block_spec(axis, blk, nb, True), _half_block_spec(axis, blk, nb, False)],
            out_specs=_half_block_spec(axis, blk, nb, False)),
        compiler_params=_params(("parallel",)),
    )(pos, grad, received)


def _send_sums_job(sums, axes, part=None, landing=None):
    n = len(sums)

    def part_shape(w):
        return tuple(d // N_CHIPS if a == axes[w] else d for a, d in enumerate(sums[w].shape))

    def copy(refs, sems, w, k):
        x, y, c, chips = _mesh_position()
        shard = 2 * chips[k][0] + chips[k][1]
        rows = part_shape(w)[0]
        dst = refs[n + w].at[k]
        if part is not None:
            offset, size, _ = _part_rows(part, rows)
            dst = refs[n + w].at[k, pl.ds(offset, size), :]
        return pltpu.make_async_remote_copy(
            src_ref=_region(refs[w], axes[w], shard, None, part_shape(w)[axes[w]], 0, part, rows), dst_ref=dst,
            send_sem=sems[0].at[w, k], recv_sem=sems[1].at[w, k],
            device_id=(chips[k][0], chips[k][1], c), device_id_type=MESH)

    def start(refs, sems):
        for w in range(n):
            for k in range(3):
                copy(refs, sems, w, k).start()

    def finish(refs, sems):
        for w in range(n):
            for k in range(3):
                copy(refs, sems, w, k).wait()

    sems = [pltpu.SemaphoreType.DMA((n, 3))] * 2
    if landing is not None:
        return _Job(ins=sums, ios=landing, sems=sems, start=start, finish=finish)
    return _Job(ins=sums, outs=[jax.ShapeDtypeStruct((3,) + part_shape(w), BF16) for w in range(n)],
                sems=sems, start=start, finish=finish)


def _sum_chip_parts(grad, received, parts, axis, pos, *, name):
    _, rows, cols = parts.shape
    tr = _row_block(rows, cols)
    nb = rows // tr
    blk = (tr, cols)

    def body(pos_ref, g_ref, r_ref, p_ref, o_ref):
        total = g_ref[...] + r_ref[...]
        for k in range(3):
            total = total + p_ref[k].astype(F32)
        o_ref[...] = total

    if axis == 1:
        g_spec = pl.BlockSpec(blk, lambda i, pos: (pos[0] * nb + i, pos[1]))
        r_spec = pl.BlockSpec(blk, lambda i, pos: (i, pos[1]))
        o_spec = pl.BlockSpec(blk, lambda i, pos: (pos[0] * nb + i, 0))
        shard_shape = (2 * rows, cols)
    else:
        g_spec = pl.BlockSpec(blk, lambda i, pos: (pos[1] * nb + i, pos[0]))
        r_spec = pl.BlockSpec(blk, lambda i, pos: (pos[1] * nb + i, 0))
        o_spec = pl.BlockSpec(blk, lambda i, pos: (i, pos[0]))
        shard_shape = (rows, 2 * cols)
    return pl.pallas_call(
        body, name=name, out_shape=jax.ShapeDtypeStruct(shard_shape, F32),
        grid_spec=pltpu.PrefetchScalarGridSpec(
            num_scalar_prefetch=1, grid=(nb,),
            in_specs=[g_spec, r_spec, pl.BlockSpec((3,) + blk, lambda i, pos: (0, i, 0))],
            out_specs=o_spec),
        compiler_params=_params(("parallel",)),
    )(pos, grad, received, parts)


def _join_job(shards, axes):
    n = len(shards)

    def copy(refs, sems, w, other):
        x, y, c, _ = _mesh_position()
        place = _region(refs[w], axes[w], None, 1 - c if other else c, 0, shards[w].shape[1 - axes[w]] // 2)
        return pltpu.make_async_remote_copy(
            src_ref=place, dst_ref=place, send_sem=sems[0].at[w], recv_sem=sems[1].at[w],
            device_id=(x, y, 1 - c), device_id_type=MESH)

    def start(refs, sems):
        for w in range(n):
            copy(refs, sems, w, False).start()

    def finish(refs, sems):
        for w in range(n):
            copy(refs, sems, w, True).wait_recv()
            copy(refs, sems, w, False).wait_send()

    return _Job(ios=shards, sems=[pltpu.SemaphoreType.DMA((n,))] * 2, start=start, finish=finish)


def _all_reduce_small(vec, after=()):
    rows, cols = vec.shape

    def body(v_ref, *rest):
        o_ref, land_ref, send_sem, recv_sem = rest[len(after):]
        x, y, c, _ = _mesh_position()
        me = 4 * x + 2 * y + c
        land_ref[me] = v_ref[...]
        copies = []
        for k in range(1, 8):
            px, py, pc = x ^ (k >> 2), y ^ ((k >> 1) & 1), c ^ (k & 1)
            copies.append(pltpu.make_async_remote_copy(
                src_ref=v_ref, dst_ref=land_ref.at[me], send_sem=send_sem.at[k], recv_sem=recv_sem.at[k],
                device_id=(px, py, pc), device_id_type=MESH))
        for cp in copies:
            cp.start()
        for k in range(1, 8):
            peer = me ^ k
            pltpu.make_async_remote_copy(
                src_ref=v_ref, dst_ref=land_ref.at[peer], send_sem=send_sem.at[k], recv_sem=recv_sem.at[k],
                device_id=(x, y, c), device_id_type=MESH).wait_recv()
        for cp in copies:
            cp.wait_send()
        total = land_ref[0]
        for k in range(1, 8):
            total = total + land_ref[k]
        o_ref[...] = total

    vmem = pl.BlockSpec(memory_space=pltpu.VMEM)
    return pl.pallas_call(
        body, name="all_reduce_small", in_specs=[vmem] + [pl.BlockSpec(memory_space=pl.ANY)] * len(after),
        out_specs=vmem, out_shape=jax.ShapeDtypeStruct((rows, cols), F32),
        scratch_shapes=[pltpu.VMEM((8, rows, cols), F32), pltpu.SemaphoreType.DMA((8,)), pltpu.SemaphoreType.DMA((8,))],
    )(vec, *after)


def _adamw(w, g, m, v, *, name, after=()):
    rows, cols = w.shape
    tr = _row_block(rows, cols) if rows % 8 == 0 else rows
    bc1 = 1.0 - ADAM_B1 ** ADAM_STEP
    bc2 = 1.0 - ADAM_B2 ** ADAM_STEP

    def body(w_ref, g_ref, m_ref, v_ref, *rest):
        go_ref, d_ref, mo_ref, vo_ref = rest[len(after):]
        gv = g_ref[...]
        go_ref[...] = gv
        mn = ADAM_B1 * m_ref[...] + (1.0 - ADAM_B1) * gv
        vn = ADAM_B2 * v_ref[...] + (1.0 - ADAM_B2) * (gv * gv)
        mo_ref[...] = mn
        vo_ref[...] = vn
        d_ref[...] = -ADAM_LR * ((mn / bc1) / (jnp.sqrt(vn / bc2) + ADAM_EPS) + ADAM_WD * w_ref[...])

    blk = pl.BlockSpec((tr, cols), lambda i: (i, 0))
    shape = jax.ShapeDtypeStruct((rows, cols), F32)
    return pl.pallas_call(
        body, name=name, grid=(rows // tr,), in_specs=[blk] * 4 + [pl.BlockSpec(memory_space=pl.ANY)] * len(after),
        out_specs=[blk] * 4, out_shape=[shape] * 4, compiler_params=_params(("parallel",)),
    )(w, g, m, v, *after)


def _to_bf16_in_place(w, axis, pos, *, name, after=None):
    rows, cols = w.shape
    tr = _row_block(rows, cols)
    nb = rows // tr

    def body(pos_ref, w_ref, *rest):
        rest[-1][...] = w_ref[...].astype(BF16)

    if axis == 1:
        o_spec = pl.BlockSpec((tr, cols), lambda i, pos: (i, pos[1]))
        full_shape = (rows, N_CHIPS * cols)
    else:
        o_spec = pl.BlockSpec((tr, cols), lambda i, pos: (pos[1] * nb + i, 0))
        full_shape = (N_CHIPS * rows, cols)
    in_specs = [pl.BlockSpec((tr, cols), lambda i, pos: (i, 0))]
    operands = [pos, w]
    if after is not None:
        in_specs.append(pl.BlockSpec(after.shape, lambda i, pos: (0, 0)))
        operands.append(after)
    return pl.pallas_call(
        body, name=name, out_shape=jax.ShapeDtypeStruct(full_shape, BF16),
        grid_spec=pltpu.PrefetchScalarGridSpec(num_scalar_prefetch=1, grid=(nb,), in_specs=in_specs, out_specs=o_spec),
        compiler_params=_params(("parallel",)),
    )(*operands)


def _split_gather_start(full, axis):
    rows_cols = full.shape

    def body(buf_ref, *rest):
        sems = rest[:6]
        token_ref = rest[7]
        x, y, c, chips = _mesh_position()
        place = _region(buf_ref, axis, 2 * x + y, c, rows_cols[axis] // N_CHIPS, rows_cols[1 - axis] // 2)
        for k in range(3):
            pltpu.make_async_remote_copy(
                src_ref=place, dst_ref=place, send_sem=sems[k], recv_sem=sems[3 + k],
                device_id=(chips[k][0], chips[k][1], c), device_id_type=MESH).start()
        token_ref[...] = jnp.zeros_like(token_ref)

    hbm = pl.BlockSpec(memory_space=pltpu.HBM)
    sem = pl.BlockSpec(memory_space=pltpu.SEMAPHORE)
    res = pl.pallas_call(
        body, name="all_gather_w_in_start",
        out_shape=(*[pltpu.SemaphoreType.DMA(())] * 6, pltpu.HBM(full.shape, full.dtype),
                   jax.ShapeDtypeStruct((8, HEAD_DIM), F32)),
        in_specs=(hbm,), out_specs=(*[sem] * 6, hbm, pl.BlockSpec(memory_space=pltpu.VMEM)),
        input_output_aliases={0: 6},
        compiler_params=pltpu.CompilerParams(has_side_effects=pltpu.SideEffectType.DATAFLOW_SIDE_EFFECTING),
    )(pltpu.with_memory_space_constraint(full, pltpu.HBM))
    return list(res[:6]), res[6], res[7]


def _split_gather_wait(sems, full, axis, after):
    rows_cols = full.shape

    def body(buf_ref, *rest):
        sem_refs = rest[:6]
        x, y, c, chips = _mesh_position()

        def copy(k, shard):
            place = _region(buf_ref, axis, shard, c, rows_cols[axis] // N_CHIPS, rows_cols[1 - axis] // 2)
            return pltpu.make_async_remote_copy(
                src_ref=place, dst_ref=place, send_sem=sem_refs[k], recv_sem=sem_refs[3 + k],
                device_id=(chips[k][0], chips[k][1], c), device_id_type=MESH)

        for k in range(3):
            copy(k, 2 * x + y).wait_send()
            copy(k, 2 * chips[k][0] + chips[k][1]).wait_recv()

    hbm = pl.BlockSpec(memory_space=pltpu.HBM)
    sem = pl.BlockSpec(memory_space=pltpu.SEMAPHORE)
    return pl.pallas_call(
        body, name="all_gather_w_in_wait", out_shape=pltpu.HBM(full.shape, full.dtype),
        in_specs=(hbm, *[sem] * 6, *[pl.BlockSpec(memory_space=pl.ANY)] * len(after)), out_specs=hbm,
        input_output_aliases={0: 0},
        compiler_params=pltpu.CompilerParams(has_side_effects=pltpu.SideEffectType.DATAFLOW_SIDE_EFFECTING),
    )(full, *sems, *after)


BIG = ("w_in", "w_out", "w_gate", "w_up", "w_down")
BIG_AXIS = dict(w_in=1, w_out=0, w_gate=1, w_up=1, w_down=0)
SMALL = ("norm_mix_w", "ret_decay_fwd", "ret_decay_bwd", "ret_norm_w", "norm_ffn_w", "norm_final_w")
ALL_WEIGHTS = ("norm_mix_w", "w_in", "ret_decay_fwd", "ret_decay_bwd", "ret_norm_w", "w_out", "norm_ffn_w",
               "w_gate", "w_up", "w_down", "norm_final_w")
SMALL_ROW = 128 * 8


def _pack_small(small):
    pieces = [jnp.reshape(small["loss"], (1,))] + [jnp.reshape(small[k], (-1,)) for k in SMALL]
    rows = []
    for p in pieces:
        pad = -p.shape[0] % (8 * SMALL_ROW)
        rows.append(jnp.reshape(jnp.pad(p, (0, pad)), (-1, SMALL_ROW)))
    return jnp.concatenate(rows, axis=0)


def _unpack_small(block, like):
    out, row = {}, 0
    for k in ("loss",) + SMALL:
        size = 1 if k == "loss" else like[k].size
        nrows = -(-size // (8 * SMALL_ROW)) * 8
        out[k] = jnp.reshape(block[row:row + nrows], (-1,))[:size]
        row += nrows
    return out


def kernel(x, norm_mix_w, w_in, ret_decay_fwd, ret_decay_bwd, ret_norm_w, w_out, norm_ffn_w, w_gate, w_up, w_down, norm_final_w, loss_target, m_norm_mix_w, m_w_in, m_ret_decay_fwd, m_ret_decay_bwd, m_ret_norm_w, m_w_out, m_norm_ffn_w, m_w_gate, m_w_up, m_w_down, m_norm_final_w, v_norm_mix_w, v_w_in, v_ret_decay_fwd, v_ret_decay_bwd, v_ret_norm_w, v_w_out, v_norm_ffn_w, v_w_gate, v_w_up, v_w_down, v_norm_final_w):
    weights = dict(norm_mix_w=norm_mix_w, w_in=w_in, ret_decay_fwd=ret_decay_fwd, ret_decay_bwd=ret_decay_bwd,
                   ret_norm_w=ret_norm_w, w_out=w_out, norm_ffn_w=norm_ffn_w, w_gate=w_gate, w_up=w_up,
                   w_down=w_down, norm_final_w=norm_final_w)
    m_in = dict(norm_mix_w=m_norm_mix_w, w_in=m_w_in, ret_decay_fwd=m_ret_decay_fwd, ret_decay_bwd=m_ret_decay_bwd,
                ret_norm_w=m_ret_norm_w, w_out=m_w_out, norm_ffn_w=m_norm_ffn_w, w_gate=m_w_gate, w_up=m_w_up,
                w_down=m_w_down, norm_final_w=m_norm_final_w)
    v_in = dict(norm_mix_w=v_norm_mix_w, w_in=v_w_in, ret_decay_fwd=v_ret_decay_fwd, ret_decay_bwd=v_ret_decay_bwd,
                ret_norm_w=v_ret_norm_w, w_out=v_w_out, norm_ffn_w=v_norm_ffn_w, w_gate=v_w_gate, w_up=v_w_up,
                w_down=v_w_down, norm_final_w=v_norm_final_w)
    pos = jnp.stack([lax.axis_index("c"), 2 * lax.axis_index("x") + lax.axis_index("y")]).astype(jnp.int32)

    own = {"w_in": _to_bf16_in_place(weights["w_in"][0], BIG_AXIS["w_in"], pos, name="cast_w_in")}
    w_in_started = _split_gather_start(own["w_in"], BIG_AXIS["w_in"])
    for k in BIG[1:]:
        own[k] = _to_bf16_in_place(weights[k][0], BIG_AXIS[k], pos, name="cast_" + k, after=w_in_started[2])

    dx, grad_w, small, w_in_pending = _step(
        x[0], loss_target[0], norm_mix_w, ret_decay_fwd[0], ret_decay_bwd[0], ret_norm_w, norm_ffn_w,
        norm_final_w[None, :], own, w_in_started, pos)

    delta, new_m, new_v = {}, {}, {}

    def update(k, after):
        shape = weights[k].shape
        as2d = (lambda t: jnp.reshape(t, (-1, shape[-1])))
        grad_w[k], delta[k], new_m[k], new_v[k] = (jnp.reshape(t, shape) for t in _adamw(
            as2d(weights[k]), as2d(grad_w[k]), as2d(m_in[k]), as2d(v_in[k]), name="adamw_" + k, after=after))

    others = [k for k in BIG if k != "w_in"]
    for k in others:
        update(k, [w_in_pending["sending"]["token"]])
    _, parts = _split_wait(w_in_pending["sending"], [dx] + [delta[k] for k in others], name="grad_send_w_in_wait")

    half = _sum_chip_parts(w_in_pending["grad"], w_in_pending["received"], parts, BIG_AXIS["w_in"], pos,
                           name="grad_sum_parts_w_in")
    joining = _split_start(_join_job([half], [BIG_AXIS["w_in"]]), name="grad_join_w_in_start")

    like = {k: weights[k] for k in SMALL}
    reduced = _unpack_small(_all_reduce_small(_pack_small(small), after=[joining["token"]]), like)
    loss = reduced["loss"][0]
    for k in SMALL:
        grad_w[k] = jnp.reshape(reduced[k], (1, -1))
        update(k, [])
    (grad_w["w_in"],) = _split_wait(joining, [delta[k] for k in SMALL], name="grad_join_w_in_wait")
    update("w_in", [])

    return (loss, dx[None], *[grad_w[k] for k in ALL_WEIGHTS], *[delta[k] for k in ALL_WEIGHTS],
            *[new_m[k] for k in ALL_WEIGHTS], *[new_v[k] for k in ALL_WEIGHTS])
```

```python
import functools
import math

import numpy as np
import jax
import jax.numpy as jnp
from jax import lax
from jax.experimental import pallas as pl
from jax.experimental.pallas import tpu as pltpu

F32 = jnp.float32
BF16 = jnp.bfloat16
MESH = pl.DeviceIdType.MESH

HEAD_DIM = 128
RET_CHUNK = 128
RET_UNROLL = 8
EPS = 1e-6
DILATED_PATTERNS = ((128, 1), (512, 4), (2048, 16))
ATT_BLOCK = 256
ATT_REACH = max(w // 2 for w, _ in DILATED_PATTERNS)
ATT_NEAR = ATT_BLOCK
ATT_CLASSES = DILATED_PATTERNS[-1][1]
assert all(w // 2 <= ATT_NEAR for w, _ in DILATED_PATTERNS[:-1])
ATT_KB = -(-ATT_NEAR // ATT_BLOCK)
ATT_WINDOW = 2 * ATT_KB + 1
ATT_FAR_GROUP = 8
ATT_NEAR_GROUP = 4
MASKED = -1e30
ROW_MAX_INIT = -1e29
N_CHIPS = 4
VMEM_LIMIT_BYTES = 56 * 1024 * 1024
ELEM_BLOCK_BYTES = 2 * 1024 * 1024
WEIGHT_GRAD_GROUP = 4

ADAM_LR = 0.001
ADAM_B1 = 0.9
ADAM_B2 = 0.999
ADAM_EPS = 1e-08
ADAM_WD = 0.01
ADAM_STEP = 10


def _params(sem=None):
    return pltpu.CompilerParams(dimension_semantics=sem, vmem_limit_bytes=VMEM_LIMIT_BYTES)


def _sigmoid(x):
    return 0.5 * jnp.tanh(0.5 * x) + 0.5


class _Job:
    def __init__(self, *, ins=(), ios=(), outs=(), sems=(), start, finish):
        self.ins, self.ios, self.outs, self.sems = list(ins), list(ios), list(outs), list(sems)
        self.start, self.finish = start, finish

    def results(self):
        return [jax.ShapeDtypeStruct(a.shape, a.dtype) for a in self.ios] + self.outs


def _call(body, *, name, grid, in_specs, out_specs, out_shape, operands, scratch_shapes=(), semantics=None, jobs=(),
          after=()):
    in_specs, out_specs, out_shape = list(in_specs), list(out_specs), list(out_shape)
    scratch_shapes = list(scratch_shapes)
    if not jobs:
        n_real = len(in_specs)

        def ordered(*refs):
            body(*refs[:n_real], *refs[n_real + len(after):])

        outs = pl.pallas_call(
            ordered if after else body, name=name, grid=grid,
            in_specs=in_specs + [pl.BlockSpec(memory_space=pl.ANY)] * len(after), out_specs=out_specs,
            out_shape=out_shape, scratch_shapes=scratch_shapes, compiler_params=_params(semantics))(*operands, *after)
        return outs, []
    n_in, n_out, n_scratch = len(in_specs), len(out_specs), len(scratch_shapes)
    extra_in, extra_out, sems, aliases = [], [], [], {}
    for job in jobs:
        extra_in += job.ins
        for t in range(len(job.ios)):
            aliases[n_in + len(extra_in) + t] = n_out + len(extra_out) + t
        extra_in += job.ios
        extra_out += job.results()
        sems += job.sems

    def carried(*refs):
        x_in = refs[n_in:n_in + len(extra_in)]
        x_out = refs[n_in + len(extra_in) + n_out:n_in + len(extra_in) + n_out + len(extra_out)]
        x_sem = refs[len(refs) - len(sems):]
        views, i_in, i_out, i_sem = [], 0, 0, 0
        for job in jobs:
            data = list(x_in[i_in:i_in + len(job.ins)]) + list(x_out[i_out:i_out + len(job.results())])
            views.append((data, x_sem[i_sem:i_sem + len(job.sems)]))
            i_in += len(job.ins) + len(job.ios)
            i_out += len(job.results())
            i_sem += len(job.sems)
        steps = [pl.program_id(d) for d in range(len(grid))]

        @pl.when(functools.reduce(jnp.logical_and, [s == 0 for s in steps]))
        def _():
            for job, (data, sem) in zip(jobs, views):
                job.start(data, sem)

        body(*refs[:n_in], *refs[n_in + len(extra_in):n_in + len(extra_in) + n_out],
             *refs[len(refs) - len(sems) - n_scratch:len(refs) - len(sems)])

        @pl.when(functools.reduce(jnp.logical_and, [s == g - 1 for s, g in zip(steps, grid)]))
        def _():
            for job, (data, sem) in zip(jobs, views):
                job.finish(data, sem)

    hbm = pl.BlockSpec(memory_space=pl.ANY)
    res = pl.pallas_call(
        carried, name=name, grid=grid, in_specs=in_specs + [hbm] * len(extra_in),
        out_specs=out_specs + [hbm] * len(extra_out), out_shape=out_shape + extra_out,
        input_output_aliases=aliases, scratch_shapes=scratch_shapes + sems,
        compiler_params=_params(("arbitrary",) * len(grid)),
    )(*operands, *extra_in)
    carried_results, at = [], n_out
    for job in jobs:
        carried_results.append(list(res[at:at + len(job.results())]))
        at += len(job.results())
    return list(res[:n_out]), carried_results


def _run_jobs(jobs, *, name):
    first = jobs[0]
    n_in, n_io = len(first.ins), len(first.ios)
    out_shape = first.results()
    n_sems = [len(job.sems) for job in jobs]

    def body(*refs):
        data = list(refs[:n_in]) + list(refs[n_in + n_io:n_in + n_io + len(out_shape)])
        at = n_in + n_io + len(out_shape)
        for job, ns in zip(jobs, n_sems):
            job.start(data, refs[at:at + ns])
            job.finish(data, refs[at:at + ns])
            at += ns

    hbm = pl.BlockSpec(memory_space=pl.ANY)
    return pl.pallas_call(
        body, name=name, in_specs=[hbm] * (n_in + n_io), out_specs=[hbm] * len(out_shape), out_shape=out_shape,
        input_output_aliases={n_in + t: t for t in range(n_io)},
        scratch_shapes=[s for job in jobs for s in job.sems],
    )(*first.ins, *first.ios)


class _SemaphoreGrid:
    def __init__(self, refs, shape):
        self.refs, self.shape = list(refs), tuple(shape)

    @property
    def at(self):
        return self

    def __getitem__(self, index):
        index = index if isinstance(index, tuple) else (index,)
        flat = 0
        for i, extent in zip(index, self.shape):
            flat = flat * extent + i
        return self.refs[flat]


def _semaphore_grids(job, refs):
    grids, at = [], 0
    for sem in job.sems:
        count = math.prod(sem.shape)
        grids.append(_SemaphoreGrid(refs[at:at + count], sem.shape))
        at += count
    return grids


def _split_start(job, *, name):
    arrays = job.ins + job.ios + [lax.empty(s.shape, s.dtype) for s in job.outs]
    n, ns = len(arrays), sum(math.prod(sem.shape) for sem in job.sems)

    def body(*refs):
        job.start(list(refs[:n]), _semaphore_grids(job, refs[n:n + ns]))
        refs[-1][...] = jnp.zeros_like(refs[-1])

    hbm = pl.BlockSpec(memory_space=pltpu.HBM)
    res = pl.pallas_call(
        body, name=name,
        out_shape=(*[pltpu.SemaphoreType.DMA(())] * ns, *[pltpu.HBM(a.shape, a.dtype) for a in arrays],
                   jax.ShapeDtypeStruct((8, HEAD_DIM), F32)),
        in_specs=[hbm] * n,
        out_specs=(*[pl.BlockSpec(memory_space=pltpu.SEMAPHORE)] * ns, *[hbm] * n,
                   pl.BlockSpec(memory_space=pltpu.VMEM)),
        input_output_aliases={t: ns + t for t in range(n)},
        compiler_params=pltpu.CompilerParams(has_side_effects=pltpu.SideEffectType.DATAFLOW_SIDE_EFFECTING),
    )(*[pltpu.with_memory_space_constraint(a, pltpu.HBM) for a in arrays])
    return dict(job=job, sems=list(res[:ns]), arrays=list(res[ns:ns + n]), token=res[-1])


def _split_wait(started, after, *, name):
    job, arrays, sems = started["job"], started["arrays"], started["sems"]
    n, ns = len(arrays), len(sems)

    def body(*refs):
        job.finish(list(refs[:n]), _semaphore_grids(job, refs[n:n + ns]))

    hbm = pl.BlockSpec(memory_space=pltpu.HBM)
    return pl.pallas_call(
        body, name=name, out_shape=[pltpu.HBM(a.shape, a.dtype) for a in arrays],
        in_specs=[hbm] * n + [pl.BlockSpec(memory_space=pltpu.SEMAPHORE)] * ns
        + [pl.BlockSpec(memory_space=pl.ANY)] * len(after),
        out_specs=[hbm] * n, input_output_aliases={t: t for t in range(n)},
        compiler_params=pltpu.CompilerParams(has_side_effects=pltpu.SideEffectType.DATAFLOW_SIDE_EFFECTING),
    )(*arrays, *sems, *after)


def _dot(a, b, ta=False, tb=False):
    return lax.dot_general(a, b, (((0 if ta else 1,), (1 if tb else 0,)), ((), ())),
                           preferred_element_type=F32)


def _tile(n, want):
    t = min(n, want) // 128 * 128
    while n % t:
        t -= 128
    return t


def _a_spec(ta, tm, tk):
    return pl.BlockSpec((tk, tm), lambda i, j, k: (k, i)) if ta else pl.BlockSpec((tm, tk), lambda i, j, k: (i, k))


def _b_spec(tb, tk, tn):
    return pl.BlockSpec((tn, tk), lambda i, j, k: (j, k)) if tb else pl.BlockSpec((tk, tn), lambda i, j, k: (k, j))


def _accumulate(accs, nk, products, finish):
    if nk == 1:
        finish(*products())
        return
    k = pl.program_id(2)

    @pl.when(k == 0)
    def _():
        for acc, p in zip(accs, products()):
            acc[...] = p

    if nk > 2:
        @pl.when(jnp.logical_and(k > 0, k < nk - 1))
        def _():
            for acc, p in zip(accs, products()):
                acc[...] += p

    @pl.when(k == nk - 1)
    def _():
        finish(*[acc[...] + p for acc, p in zip(accs, products())])


def _matmul(a, b, *, name, ta=False, tb=False, out_dtype=F32, residual=None, tm=1024, tn=1024, tk=2048, jobs=()):
    m, kdim = (a.shape[1], a.shape[0]) if ta else a.shape
    n = b.shape[0] if tb else b.shape[1]
    tm, tn, tk = _tile(m, tm), _tile(n, tn), _tile(kdim, tk)
    nk = kdim // tk

    def body(*refs):
        a_ref, b_ref = refs[:2]
        r_ref = refs[2] if residual is not None else None
        o_ref = refs[-1] if nk == 1 else refs[-2]

        def finish(total):
            if residual is not None:
                total = total + r_ref[...]
            o_ref[...] = total.astype(out_dtype)

        _accumulate(refs[-1:] if nk > 1 else (), nk, lambda: (_dot(a_ref[...], b_ref[...], ta, tb),), finish)

    o_spec = pl.BlockSpec((tm, tn), lambda i, j, k: (i, j))
    in_specs = [_a_spec(ta, tm, tk), _b_spec(tb, tk, tn)]
    operands = [a, b]
    if residual is not None:
        in_specs.append(o_spec)
        operands.append(residual)
    (out,), carried = _call(
        body, name=name, grid=(m // tm, n // tn, nk), in_specs=in_specs, out_specs=[o_spec],
        out_shape=[jax.ShapeDtypeStruct((m, n), out_dtype)], operands=operands,
        scratch_shapes=[pltpu.VMEM((tm, tn), F32)] * (nk > 1),
        semantics=("parallel", "parallel", "arbitrary"), jobs=jobs)
    return (out, carried) if jobs else out


def _matmul_pieces_nt(pieces, b, *, name, tm=512, tn=1024, jobs=(), after=()):
    m, kp = pieces[0].shape
    n = b.shape[0]
    tm, tn = _tile(m, tm), _tile(n, tn)
    count = len(pieces)

    def body(*refs):
        b_ref, o_ref = refs[count], refs[count + 1]
        total = _dot(refs[0][...], b_ref[:, pl.ds(0, kp)], tb=True)
        for p in range(1, count):
            total = total + _dot(refs[p][...], b_ref[:, pl.ds(p * kp, kp)], tb=True)
        o_ref[...] = total

    piece = pl.BlockSpec((tm, kp), lambda j, i: (i, 0))
    (out,), carried = _call(
        body, name=name, grid=(n // tn, m // tm),
        in_specs=[piece] * count + [pl.BlockSpec((tn, count * kp), lambda j, i: (j, 0))],
        out_specs=[pl.BlockSpec((tm, tn), lambda j, i: (i, j))],
        out_shape=[jax.ShapeDtypeStruct((m, n), F32)], operands=[*pieces, b],
        semantics=("parallel", "parallel"), jobs=jobs, after=after)
    return (out, carried) if jobs else out


def _weight_grad_pieces(a, pieces, *, name):
    tokens, m = a.shape
    np_ = pieces[0].shape[1]
    tm = 1024 if m % 1024 == 0 else _tile(m, 1408)
    tn = _tile(np_, 512)
    nb = np_ // tn
    out = None
    for first in range(0, len(pieces), WEIGHT_GRAD_GROUP):
        group = pieces[first:first + WEIGHT_GRAD_GROUP]

        def body(*refs, count=len(group)):
            t_now = pl.program_id(1) // nb
            for t in range(count):
                @pl.when(t_now == t)
                def _(t=t):
                    refs[-1][...] = _dot(refs[0][...], refs[1 + t][...], ta=True)

        def piece_spec(t):
            return pl.BlockSpec((tokens, tn), lambda i, j: (0, jnp.clip(j - t * nb, 0, nb - 1)))

        in_specs = [pl.BlockSpec((tokens, tm), lambda i, j: (0, i))] + [piece_spec(t) for t in range(len(group))]
        operands = [a, *group]
        if out is not None:
            in_specs.append(pl.BlockSpec(memory_space=pl.ANY))
            operands.append(out)
        out = pl.pallas_call(
            body, name="%s_%d" % (name, first), grid=(m // tm, nb * len(group)), in_specs=in_specs,
            out_specs=pl.BlockSpec((tm, tn), lambda i, j, first=first: (i, first * nb + j)),
            out_shape=jax.ShapeDtypeStruct((m, len(pieces) * np_), F32),
            input_output_aliases={len(operands) - 1: 0} if out is not None else {},
            compiler_params=_params(("parallel", "arbitrary")),
        )(*operands)
    return out


def _weight_grad(a, g, *, name, jobs=()):
    tokens, m = a.shape
    tm = 1024 if m % 1024 == 0 else _tile(m, 1408)
    return _matmul(a, g, name=name, ta=True, tm=tm, tn=512, tk=tokens, jobs=jobs)


def _swiglu_fwd(n2, w_gate, w_up, *, tm=1024, tn=512, tk=2048, jobs=()):
    m, kdim = n2.shape
    n = w_gate.shape[1]
    tm, tn, tk = _tile(m, tm), _tile(n, tn), _tile(kdim, tk)
    nk = kdim // tk

    def body(a_ref, g_ref, u_ref, gate_ref, up_ref, act_ref, *acc):
        def products():
            a = a_ref[...]
            return _dot(a, g_ref[...]), _dot(a, u_ref[...])

        def finish(g, u):
            gate_ref[...] = g.astype(BF16)
            up_ref[...] = u.astype(BF16)
            act_ref[...] = (g * _sigmoid(g) * u).astype(BF16)

        _accumulate(acc, nk, products, finish)

    o_spec = pl.BlockSpec((tm, tn), lambda i, j, k: (i, j))
    o_shape = jax.ShapeDtypeStruct((m, n), BF16)
    return _call(
        body, name="swiglu_fwd", grid=(m // tm, n // tn, nk),
        in_specs=[_a_spec(False, tm, tk), _b_spec(False, tk, tn), _b_spec(False, tk, tn)],
        out_specs=[o_spec] * 3, out_shape=[o_shape] * 3, operands=[n2, w_gate, w_up],
        scratch_shapes=[pltpu.VMEM((tm, tn), F32)] * (2 * (nk > 1)),
        semantics=("parallel", "parallel", "arbitrary"), jobs=jobs)


def _swiglu_bwd_act(dh2, w_down, gate, up, *, tm=1024, tn=512, tk=2048):
    m, kdim = dh2.shape
    n = w_down.shape[0]
    tm, tn, tk = _tile(m, tm), _tile(n, tn), _tile(kdim, tk)
    nk = kdim // tk

    sub = _tile(tn, 256)

    def body(a_ref, b_ref, gate_ref, up_ref, dgate_ref, dup_ref, *acc):
        def finish(dact, cols=slice(None)):
            g = gate_ref[:, cols].astype(F32)
            u = up_ref[:, cols].astype(F32)
            sg = _sigmoid(g)
            dup_ref[:, cols] = (dact * g * sg).astype(BF16)
            dgate_ref[:, cols] = (dact * u * sg * (1.0 + g * (1.0 - sg))).astype(BF16)

        if nk == 1:
            a = a_ref[...]
            for c in range(tn // sub):
                cols = pl.ds(c * sub, sub)
                finish(_dot(a, b_ref[cols, :], tb=True), cols)
        else:
            _accumulate(acc, nk, lambda: (_dot(a_ref[...], b_ref[...], tb=True),), finish)

    o_spec = pl.BlockSpec((tm, tn), lambda i, j, k: (i, j))
    o_shape = jax.ShapeDtypeStruct((m, n), BF16)
    return pl.pallas_call(
        body, name="swiglu_bwd_act", grid=(m // tm, n // tn, nk),
        in_specs=[_a_spec(False, tm, tk), _b_spec(True, tk, tn), o_spec, o_spec],
        out_specs=[o_spec] * 2, out_shape=[o_shape] * 2,
        scratch_shapes=[pltpu.VMEM((tm, tn), F32)] * (nk > 1),
        compiler_params=_params(("parallel", "parallel", "arbitrary")),
    )(dh2, w_down, gate, up)


def _swiglu_bwd_in(dgate, dup, w_gate, w_up, *, tm=1024, tn=1024, tk=1408, jobs=()):
    m, kdim = dgate.shape
    n = w_gate.shape[0]
    tm, tn, tk = _tile(m, tm), _tile(n, tn), _tile(kdim, tk)
    nk = kdim // tk

    def body(a1_ref, a2_ref, b1_ref, b2_ref, o_ref, *acc):
        def product():
            return (_dot(a1_ref[...], b1_ref[...], tb=True) + _dot(a2_ref[...], b2_ref[...], tb=True),)

        def finish(total):
            o_ref[...] = total

        _accumulate(acc, nk, product, finish)

    a_spec, b_spec = _a_spec(False, tm, tk), _b_spec(True, tk, tn)
    (out,), carried = _call(
        body, name="swiglu_bwd_in", grid=(m // tm, n // tn, nk),
        in_specs=[a_spec, a_spec, b_spec, b_spec],
        out_specs=[pl.BlockSpec((tm, tn), lambda i, j, k: (i, j))],
        out_shape=[jax.ShapeDtypeStruct((m, n), F32)], operands=[dgate, dup, w_gate, w_up],
        scratch_shapes=[pltpu.VMEM((tm, tn), F32)] * (nk > 1),
        semantics=("parallel", "parallel", "arbitrary"), jobs=jobs)
    return out, carried


def _row_block(rows, cols):
    tr = min(rows, max(16, ELEM_BLOCK_BYTES // (4 * cols) // 16 * 16))
    while rows % tr:
        tr -= 16
    return tr


def _rmsnorm_fwd(x, g, *, name, after=None):
    s, d = x.shape
    tr = _row_block(s, d)

    def body(x_ref, g_ref, *rest):
        xv = x_ref[...]
        r = lax.rsqrt(jnp.mean(xv * xv, axis=-1, keepdims=True) + EPS)
        rest[-1][...] = (xv * r * g_ref[...]).astype(BF16)

    row = pl.BlockSpec((tr, d), lambda i: (i, 0))
    in_specs = [row, pl.BlockSpec((1, d), lambda i: (0, 0))]
    operands = [x, g]
    if after is not None:
        in_specs.append(pl.BlockSpec(after.shape, lambda i: (0, 0)))
        operands.append(after)
    return pl.pallas_call(
        body, name=name, grid=(s // tr,), in_specs=in_specs,
        out_specs=row, out_shape=jax.ShapeDtypeStruct((s, d), BF16),
        compiler_params=_params(("parallel",)),
    )(*operands)


def _rmsnorm_bwd_rows(xv, gv, dy):
    r = lax.rsqrt(jnp.mean(xv * xv, axis=-1, keepdims=True) + EPS)
    xhat = xv * r
    dxh = dy * gv
    dx = r * (dxh - xhat * jnp.mean(dxh * xhat, axis=-1, keepdims=True))
    return dx, dy * xhat


def _rmsnorm_bwd(dn, x, g, skip, *, name, after=()):
    s, d = x.shape
    tr = _row_block(s, d)

    def body(dn_ref, x_ref, g_ref, skip_ref, *rest):
        dx_ref, dxb_ref, dg_ref = rest[len(after):]
        dx, dgr = _rmsnorm_bwd_rows(x_ref[...], g_ref[...], dn_ref[...])
        dx = dx + skip_ref[...]
        dx_ref[...] = dx
        dxb_ref[...] = dx.astype(BF16)

        @pl.when(pl.program_id(0) == 0)
        def _():
            dg_ref[...] = jnp.zeros_like(dg_ref)

        dg_ref[...] += jnp.sum(dgr, axis=0, keepdims=True)

    row = pl.BlockSpec((tr, d), lambda i: (i, 0))
    vec = pl.BlockSpec((1, d), lambda i: (0, 0))
    return pl.pallas_call(
        body, name=name, grid=(s // tr,),
        in_specs=[row, row, vec, row] + [pl.BlockSpec(memory_space=pl.ANY)] * len(after),
        out_specs=[row, row, vec],
        out_shape=[jax.ShapeDtypeStruct((s, d), F32), jax.ShapeDtypeStruct((s, d), BF16),
                   jax.ShapeDtypeStruct((1, d), F32)],
        compiler_params=_params(("arbitrary",)),
    )(dn, x, g, skip, *after)


def _loss_head(h2, g, target):
    s, d = h2.shape
    tr = _row_block(s, d)

    def body(h_ref, g_ref, t_ref, dh_ref, dhb_ref, dg_ref, loss_ref):
        hv = h_ref[...]
        gv = g_ref[...]
        r = lax.rsqrt(jnp.mean(hv * hv, axis=-1, keepdims=True) + EPS)
        err = hv * r * gv - t_ref[...]
        dx, dgr = _rmsnorm_bwd_rows(hv, gv, err * (1.0 / d))
        dh_ref[...] = dx
        dhb_ref[...] = dx.astype(BF16)

        @pl.when(pl.program_id(0) == 0)
        def _():
            dg_ref[...] = jnp.zeros_like(dg_ref)
            loss_ref[...] = jnp.zeros_like(loss_ref)

        dg_ref[...] += jnp.sum(dgr, axis=0, keepdims=True)
        row_loss = jnp.mean(err * err, axis=-1, keepdims=True)
        loss_ref[...] += 0.5 * jnp.sum(row_loss, axis=0, keepdims=True)

    row = pl.BlockSpec((tr, d), lambda i: (i, 0))
    vec = pl.BlockSpec((1, d), lambda i: (0, 0))
    one = pl.BlockSpec((1, 1), lambda i: (0, 0))
    return pl.pallas_call(
        body, name="loss_head", grid=(s // tr,), in_specs=[row, vec, row],
        out_specs=[row, row, vec, one],
        out_shape=[jax.ShapeDtypeStruct((s, d), F32), jax.ShapeDtypeStruct((s, d), BF16),
                   jax.ShapeDtypeStruct((1, d), F32), jax.ShapeDtypeStruct((1, 1), F32)],
        compiler_params=_params(("arbitrary",)),
    )(h2, g, target)


def _attention_bias_tables():
    k = np.arange(-ATT_KB, ATT_KB + 1)[:, None, None]
    delta = k * ATT_BLOCK + np.arange(ATT_BLOCK)[None, None, :] - np.arange(ATT_BLOCK)[None, :, None]
    dist = np.abs(delta)
    count = np.zeros(delta.shape, np.int32)
    for window, dilation in DILATED_PATTERNS:
        count += (delta % dilation == 0) & (dist <= min(window // 2, ATT_NEAR))
    logc = np.where(count > 0, np.log(np.maximum(count, 1)), MASKED)
    return dist.astype(np.float32), logc.astype(np.float32)


def _far_bias_tables(per_class):
    steps = np.abs(np.arange(per_class)[:, None] - np.arange(per_class)[None, :]) * ATT_CLASSES
    valid = (steps > ATT_NEAR) & (steps <= ATT_REACH)
    return steps.astype(np.float32), np.where(valid, 0.0, MASKED).astype(np.float32)


def _to_classes(x):
    s, cols = x.shape
    return jnp.reshape(jnp.transpose(jnp.reshape(x, (s // ATT_CLASSES, ATT_CLASSES, cols)), (1, 0, 2)), (s, cols))


def _from_classes(x):
    s, cols = x.shape
    return jnp.reshape(jnp.transpose(jnp.reshape(x, (ATT_CLASSES, s // ATT_CLASSES, cols)), (1, 0, 2)), (s, cols))


def _head_bias(bias_ref, slope, dist_ref, logc_ref):
    for kk in range(ATT_WINDOW):
        bias_ref[kk] = logc_ref[kk] - slope * dist_ref[kk]
    bias_ref[ATT_WINDOW] = jnp.full((ATT_BLOCK, ATT_BLOCK), MASKED, F32)


def _window_start(i, nq, nwin):
    return jnp.clip(i - ATT_KB, 0, nq - nwin)


def _window_block(j, i):
    rows = pl.ds(pl.multiple_of(j * ATT_BLOCK, ATT_BLOCK), ATT_BLOCK)
    kk = j - i + ATT_KB
    return rows, jnp.where(jnp.logical_and(kk >= 0, kk < ATT_WINDOW), kk, ATT_WINDOW)


def _attention_far_fwd(qkv, slopes, n_heads, jobs=()):
    s = qkv.shape[0]
    per_class = s // ATT_CLASSES
    scale = HEAD_DIM ** -0.5
    dist, logc = _far_bias_tables(per_class)

    def body(slope_ref, q_ref, k_ref, v_ref, dist_ref, logc_ref, o_ref, lse_ref):
        bias = logc_ref[...] - slope_ref[pl.program_id(0)] * dist_ref[...]
        for a in range(ATT_FAR_GROUP):
            rows = pl.ds(a * per_class, per_class)
            sc = _dot(q_ref[rows, :], k_ref[rows, :], tb=True) * scale + bias
            m = jnp.maximum(jnp.max(sc, axis=-1, keepdims=True), ROW_MAX_INIT)
            p = jnp.exp(sc - m)
            l = jnp.maximum(jnp.sum(p, axis=-1, keepdims=True), 1e-30)
            o_ref[rows, :] = (_dot(p.astype(BF16), v_ref[rows, :]) / l).astype(BF16)
            lse_ref[rows, :] = jnp.broadcast_to(m + jnp.log(l), (per_class, HEAD_DIM))

    hh = n_heads
    blk = pl.BlockSpec((ATT_FAR_GROUP * per_class, HEAD_DIM), lambda h, r: (r, h))
    table = pl.BlockSpec(dist.shape, lambda h, r: (0, 0))
    return _call(
        body, name="attention_far_fwd", grid=(hh, ATT_CLASSES // ATT_FAR_GROUP),
        in_specs=[pl.BlockSpec(memory_space=pltpu.SMEM), blk,
                  pl.BlockSpec((ATT_FAR_GROUP * per_class, HEAD_DIM), lambda h, r: (r, hh + h)),
                  pl.BlockSpec((ATT_FAR_GROUP * per_class, HEAD_DIM), lambda h, r: (r, 2 * hh + h)), table, table],
        out_specs=[blk, blk],
        out_shape=[jax.ShapeDtypeStruct((s, hh * HEAD_DIM), BF16), jax.ShapeDtypeStruct((s, hh * HEAD_DIM), F32)],
        operands=[slopes, qkv, qkv, qkv, jnp.asarray(dist), jnp.asarray(logc)],
        semantics=("parallel", "parallel"), jobs=jobs)


def _attention_fwd(proj, slopes, far_out, far_lse, n_heads, jobs=()):
    s = proj.shape[0]
    nq = s // ATT_BLOCK
    scale = HEAD_DIM ** -0.5
    dist, logc = _attention_bias_tables()

    nwin = min(ATT_WINDOW, nq)

    group = math.gcd(ATT_NEAR_GROUP, nq)

    def body(slope_ref, q_ref, k_ref, v_ref, fo_ref, fl_ref, dist_ref, logc_ref, o_ref, lse_ref, bias_ref, s_ref):
        h, step = pl.program_id(0), pl.program_id(1)

        @pl.when(step == 0)
        def _():
            _head_bias(bias_ref, slope_ref[h], dist_ref, logc_ref)

        for a in range(group):
            i = step * group + a
            mine = pl.ds(a * ATT_BLOCK, ATT_BLOCK)
            q = q_ref[mine, :]
            first = _window_start(i, nq, nwin)
            m = jnp.full((ATT_BLOCK, 1), ROW_MAX_INIT, F32)
            for b in range(nwin):
                rows, kk = _window_block(first + b, i)
                sc = _dot(q, k_ref[rows, :], tb=True) * scale + bias_ref[kk]
                s_ref[a * nwin + b] = sc
                m = jnp.maximum(m, jnp.max(sc, axis=-1, keepdims=True))
            l = jnp.zeros((ATT_BLOCK, 1), F32)
            acc = jnp.zeros((ATT_BLOCK, HEAD_DIM), F32)
            for b in range(nwin):
                rows, _ = _window_block(first + b, i)
                p = jnp.exp(s_ref[a * nwin + b] - m)
                l = l + jnp.sum(p, axis=-1, keepdims=True)
                acc = acc + _dot(p.astype(BF16), v_ref[rows, :])
            near_lse = m + jnp.log(l)
            far_lse_col = fl_ref[mine, :1]
            lse = jnp.maximum(near_lse, far_lse_col)
            lse = lse + jnp.log(jnp.exp(near_lse - lse) + jnp.exp(far_lse_col - lse))
            o_ref[mine, :] = (acc * (jnp.exp(near_lse - lse) / l)
                              + fo_ref[mine, :].astype(F32) * jnp.exp(far_lse_col - lse)).astype(BF16)
            lse_ref[mine, :] = jnp.broadcast_to(lse, (ATT_BLOCK, HEAD_DIM))

    hh = n_heads
    blk = pl.BlockSpec((group * ATT_BLOCK, HEAD_DIM), lambda h, i: (i, h))
    table = pl.BlockSpec(dist.shape, lambda h, i: (0, 0, 0))
    return _call(
        body, name="attention_fwd", grid=(hh, nq // group),
        in_specs=[pl.BlockSpec(memory_space=pltpu.SMEM), blk,
                  pl.BlockSpec((s, HEAD_DIM), lambda h, i: (0, hh + h)),
                  pl.BlockSpec((s, HEAD_DIM), lambda h, i: (0, 2 * hh + h)), blk, blk, table, table],
        out_specs=[blk, blk],
        out_shape=[jax.ShapeDtypeStruct((s, hh * HEAD_DIM), BF16), jax.ShapeDtypeStruct((s, hh * HEAD_DIM), F32)],
        operands=[slopes, proj, proj, proj, far_out, far_lse, jnp.asarray(dist), jnp.asarray(logc)],
        scratch_shapes=[pltpu.VMEM((ATT_WINDOW + 1, ATT_BLOCK, ATT_BLOCK), F32),
                        pltpu.VMEM((group * nwin, ATT_BLOCK, ATT_BLOCK), F32)],
        semantics=("parallel", "arbitrary"), jobs=jobs)


def _attention_far_bwd(qkv, slopes, out, dout, lse, n_heads):
    s = qkv.shape[0]
    per_class = s // ATT_CLASSES
    scale = HEAD_DIM ** -0.5
    dist, logc = _far_bias_tables(per_class)

    def body(slope_ref, q_ref, k_ref, v_ref, o_ref, do_ref, lse_ref, dist_ref, logc_ref, dq_ref, dk_ref, dv_ref):
        bias = logc_ref[...] - slope_ref[pl.program_id(0)] * dist_ref[...]
        for a in range(ATT_FAR_GROUP):
            rows = pl.ds(a * per_class, per_class)
            q, k, do = q_ref[rows, :], k_ref[rows, :], do_ref[rows, :]
            delta = jnp.sum(do.astype(F32) * o_ref[rows, :].astype(F32), axis=-1, keepdims=True)
            p = jnp.exp(_dot(q, k, tb=True) * scale + bias - lse_ref[rows, :1])
            dv_ref[rows, :] = _dot(p.astype(BF16), do, ta=True).astype(BF16)
            ds = (p * (_dot(do, v_ref[rows, :], tb=True) - delta) * scale).astype(BF16)
            dk_ref[rows, :] = _dot(ds, q, ta=True).astype(BF16)
            dq_ref[rows, :] = _dot(ds, k).astype(BF16)

    hh = n_heads
    blk = pl.BlockSpec((ATT_FAR_GROUP * per_class, HEAD_DIM), lambda h, r: (r, h))
    table = pl.BlockSpec(dist.shape, lambda h, r: (0, 0))
    o_shape = jax.ShapeDtypeStruct((s, hh * HEAD_DIM), BF16)
    return pl.pallas_call(
        body, name="attention_far_bwd", grid=(hh, ATT_CLASSES // ATT_FAR_GROUP),
        in_specs=[pl.BlockSpec(memory_space=pltpu.SMEM), blk,
                  pl.BlockSpec((ATT_FAR_GROUP * per_class, HEAD_DIM), lambda h, r: (r, hh + h)),
                  pl.BlockSpec((ATT_FAR_GROUP * per_class, HEAD_DIM), lambda h, r: (r, 2 * hh + h)),
                  blk, blk, blk, table, table],
        out_specs=[blk] * 3, out_shape=[o_shape] * 3,
        compiler_params=_params(("parallel", "parallel")),
    )(slopes, qkv, qkv, qkv, out, dout, lse, jnp.asarray(dist), jnp.asarray(logc))


def _attention_bwd(proj, slopes, out, lse, dmixed, far_grads, n_heads, jobs=()):
    s = proj.shape[0]
    nq = s // ATT_BLOCK
    scale = HEAD_DIM ** -0.5
    dist, logc = _attention_bias_tables()

    nwin = min(ATT_WINDOW, nq)
    group = math.gcd(ATT_NEAR_GROUP, nq)

    def body(slope_ref, q_ref, k_ref, v_ref, o_ref, do_ref, lse_ref, fdq_ref, fdk_ref, fdv_ref, dist_ref, logc_ref,
             dq_ref, dk_ref, dv_ref, dk_acc, dv_acc, bias_ref):
        h, step = pl.program_id(0), pl.program_id(1)

        @pl.when(step == 0)
        def _():
            dk_acc[...] = jnp.zeros_like(dk_acc)
            dv_acc[...] = jnp.zeros_like(dv_acc)
            _head_bias(bias_ref, slope_ref[h], dist_ref, logc_ref)

        for a in range(group):
            i = step * group + a
            mine = pl.ds(a * ATT_BLOCK, ATT_BLOCK)
            q = q_ref[mine, :]
            do = do_ref[mine, :]
            lse_col = lse_ref[mine, :1]
            delta = jnp.sum(do.astype(F32) * o_ref[mine, :].astype(F32), axis=-1, keepdims=True)
            first = _window_start(i, nq, nwin)
            dq = jnp.zeros((ATT_BLOCK, HEAD_DIM), F32)
            for b in range(nwin):
                rows, kk = _window_block(first + b, i)
                kj = k_ref[rows, :]
                vj = v_ref[rows, :]
                p = jnp.exp(_dot(q, kj, tb=True) * scale + bias_ref[kk] - lse_col)
                dv_acc[rows, :] += _dot(p.astype(BF16), do, ta=True)
                dp = _dot(do, vj, tb=True)
                ds = (p * (dp - delta) * scale).astype(BF16)
                dk_acc[rows, :] += _dot(ds, q, ta=True)
                dq = dq + _dot(ds, kj)
            dq_ref[mine, :] = (dq + fdq_ref[mine, :].astype(F32)).astype(BF16)

        @pl.when(step == nq // group - 1)
        def _():
            dk_ref[...] = (dk_acc[...] + fdk_ref[...].astype(F32)).astype(BF16)
            dv_ref[...] = (dv_acc[...] + fdv_ref[...].astype(F32)).astype(BF16)

    hh = n_heads
    blk = pl.BlockSpec((group * ATT_BLOCK, HEAD_DIM), lambda h, i: (i, h))
    col = pl.BlockSpec((s, HEAD_DIM), lambda h, i: (0, h))
    table = pl.BlockSpec(dist.shape, lambda h, i: (0, 0, 0))
    o_shape = jax.ShapeDtypeStruct((s, hh * HEAD_DIM), BF16)
    return _call(
        body, name="attention_bwd", grid=(hh, nq // group),
        in_specs=[pl.BlockSpec(memory_space=pltpu.SMEM), blk,
                  pl.BlockSpec((s, HEAD_DIM), lambda h, i: (0, hh + h)),
                  pl.BlockSpec((s, HEAD_DIM), lambda h, i: (0, 2 * hh + h)),
                  blk, blk, blk, blk, col, col, table, table],
        out_specs=[blk, col, col], out_shape=[o_shape] * 3,
        operands=[slopes, proj, proj, proj, out, dmixed, lse, *far_grads, jnp.asarray(dist), jnp.asarray(logc)],
        scratch_shapes=[pltpu.VMEM((s, HEAD_DIM), F32)] * 2
        + [pltpu.VMEM((ATT_WINDOW + 1, ATT_BLOCK, ATT_BLOCK), F32)],
        semantics=("parallel", "arbitrary"), jobs=jobs)


def _ret_decays(lgc, lga, strict_c, strict_a):
    c = RET_CHUNK
    rel = (lax.broadcasted_iota(jnp.int32, (c, c), 0) - lax.broadcasted_iota(jnp.int32, (c, c), 1)).astype(F32)
    in_c = (rel > 0) if strict_c else (rel >= 0)
    in_a = (rel < 0) if strict_a else (rel <= 0)
    mask = (jnp.where(in_c, jnp.exp(lgc * jnp.maximum(rel, 0.0)), 0.0)
            + jnp.where(in_a, jnp.exp(lga * jnp.maximum(-rel, 0.0)), 0.0))
    idx = lax.broadcasted_iota(jnp.int32, (c, 1), 0).astype(F32)
    ones = jnp.ones((1, HEAD_DIM), F32)
    dec = dict(
        rel=rel, mask=mask, idx=idx,
        a_c=jnp.exp(lgc * (idx + 1.0)), b_c=jnp.exp(lgc * (c - 1.0 - idx)), chunk_c=jnp.exp(ones * (lgc * c)),
        a_a=jnp.exp(lga * (c - idx)), b_a=jnp.exp(lga * idx), chunk_a=jnp.exp(ones * (lga * c)),
    )
    return dec


def _scaled(x, col):
    return (x.astype(F32) * col).astype(BF16)


def _chunk_rows(i):
    return pl.ds(pl.multiple_of(i * RET_CHUNK, RET_CHUNK), RET_CHUNK)


def _chunk_loop(nc, step, init, unroll=RET_UNROLL):
    group = math.gcd(nc, unroll)

    def trip(t, carry):
        for u in range(group):
            carry = step(t * group + u, carry)
        return carry

    return lax.fori_loop(0, nc // group, trip, init)


def _retention(a, b, c, lg_c, lg_a, *, strict_c, strict_a, scale, n_heads, name, gate=None, norm_w=None, jobs=()):
    s = a[0].shape[0]
    nc = s // RET_CHUNK
    epilogue = gate is not None

    def body(*refs):
        lgc_ref, lga_ref, a_ref, b_ref, c_ref = refs[:5]
        if epilogue:
            g_ref, w_ref, o_ref, mix_ref, sa_ref = refs[5:]
        else:
            o_ref, sa_ref = refs[5:]
        h = pl.program_id(0)
        dec = _ret_decays(lgc_ref[h], lga_ref[h], strict_c, strict_a)

        def reverse(t, state):
            i = nc - 1 - t
            sa_ref[i] = state.astype(BF16)
            rows = _chunk_rows(i)
            return state * dec["chunk_a"] + _dot(_scaled(b_ref[rows, :], dec["b_a"]), c_ref[rows, :], ta=True)

        _chunk_loop(nc, reverse, jnp.zeros((HEAD_DIM, HEAD_DIM), F32))

        def forward(i, state):
            rows = _chunk_rows(i)
            ai, bi, ci = a_ref[rows, :], b_ref[rows, :], c_ref[rows, :]
            inner = (_dot(ai, bi, tb=True) * dec["mask"]).astype(BF16)
            out = (_dot(inner, ci) + _dot(_scaled(ai, dec["a_c"]), state.astype(BF16))
                   + _dot(_scaled(ai, dec["a_a"]), sa_ref[i])) * scale
            o_ref[rows, :] = out.astype(BF16)
            if epilogue:
                r = lax.rsqrt(jnp.mean(out * out, axis=-1, keepdims=True) + EPS)
                g = g_ref[rows, :].astype(F32)
                mix_ref[rows, :] = (out * r * w_ref[...] * (g * _sigmoid(g))).astype(BF16)
            return state * dec["chunk_c"] + _dot(_scaled(bi, dec["b_c"]), ci, ta=True)

        _chunk_loop(nc, forward, jnp.zeros((HEAD_DIM, HEAD_DIM), F32))

    def col(first):
        return pl.BlockSpec((s, HEAD_DIM), lambda h: (0, first + h))

    smem = pl.BlockSpec(memory_space=pltpu.SMEM)
    in_specs = [smem, smem, col(a[1]), col(b[1]), col(c[1])]
    operands = [lg_c, lg_a, a[0], b[0], c[0]]
    o_shape = jax.ShapeDtypeStruct((s, n_heads * HEAD_DIM), BF16)
    out_specs, out_shape = [col(0)], [o_shape]
    if epilogue:
        in_specs += [col(gate[1]), pl.BlockSpec((1, HEAD_DIM), lambda h: (0, h))]
        operands += [gate[0], norm_w]
        out_specs, out_shape = [col(0)] * 2, [o_shape] * 2
    res, carried = _call(
        body, name=name, grid=(n_heads,), in_specs=in_specs, out_specs=out_specs, out_shape=out_shape,
        operands=operands, scratch_shapes=[pltpu.VMEM((nc, HEAD_DIM, HEAD_DIM), BF16)],
        semantics=("parallel",), jobs=jobs)
    res = res if epilogue else res[0]
    return (res, carried) if jobs else res


def _retention_decay_grads(a, b, c, e, lg_c, lg_a, *, scale, n_heads):
    s = a[0].shape[0]
    nc = s // RET_CHUNK
    cf = float(RET_CHUNK)

    def body(lgc_ref, lga_ref, a_ref, b_ref, c_ref, e_ref, gc_ref, ga_ref, sa_ref, ta_ref):
        h = pl.program_id(0)
        lgc, lga = lgc_ref[h], lga_ref[h]
        dec = _ret_decays(lgc, lga, True, True)
        rel, idx = dec["rel"], dec["idx"]
        w_c = jnp.where(rel > 0, rel * jnp.exp(lgc * jnp.maximum(rel, 0.0)), 0.0)
        w_a = jnp.where(rel < 0, -rel * jnp.exp(lga * jnp.maximum(-rel, 0.0)), 0.0)
        zero = jnp.zeros((HEAD_DIM, HEAD_DIM), F32)

        def reverse(t, carry):
            st, dst = carry
            i = nc - 1 - t
            sa_ref[i] = st.astype(BF16)
            ta_ref[i] = dst.astype(BF16)
            rows = _chunk_rows(i)
            bi, ci = b_ref[rows, :], c_ref[rows, :]
            st_new = st * dec["chunk_a"] + _dot(_scaled(bi, dec["b_a"]), ci, ta=True)
            dst_new = (cf * st + dst) * dec["chunk_a"] + _dot(_scaled(bi, idx * dec["b_a"]), ci, ta=True)
            return st_new, dst_new

        _chunk_loop(nc, reverse, (zero, zero))

        def forward(i, carry):
            st, dst, acc_c, acc_a = carry
            rows = _chunk_rows(i)
            ai, bi, ci = a_ref[rows, :], b_ref[rows, :], c_ref[rows, :]
            ev = e_ref[rows, :].astype(F32)
            pg = _dot(ai, bi, tb=True) * _dot(e_ref[rows, :], ci, tb=True)
            a_c, a_a = _scaled(ai, dec["a_c"]), _scaled(ai, dec["a_a"])
            inter_c = _dot(a_c, st.astype(BF16)) * (idx + 1.0) + _dot(a_c, dst.astype(BF16))
            inter_a = _dot(a_a, sa_ref[i]) * (cf - idx) + _dot(a_a, ta_ref[i])
            acc_c = acc_c + jnp.sum(pg * w_c, axis=0, keepdims=True) + jnp.sum(inter_c * ev, axis=0, keepdims=True)
            acc_a = acc_a + jnp.sum(pg * w_a, axis=0, keepdims=True) + jnp.sum(inter_a * ev, axis=0, keepdims=True)
            st_new = st * dec["chunk_c"] + _dot(_scaled(bi, dec["b_c"]), ci, ta=True)
            dst_new = ((cf * st + dst) * dec["chunk_c"]
                       + _dot(_scaled(bi, (cf - 1.0 - idx) * dec["b_c"]), ci, ta=True))
            return st_new, dst_new, acc_c, acc_a

        row = jnp.zeros((1, HEAD_DIM), F32)
        _, _, acc_c, acc_a = _chunk_loop(nc, forward, (zero, zero, row, row))
        gc_ref[...] = jnp.broadcast_to(jnp.sum(acc_c, axis=-1, keepdims=True) * scale, gc_ref.shape)
        ga_ref[...] = jnp.broadcast_to(jnp.sum(acc_a, axis=-1, keepdims=True) * scale, ga_ref.shape)

    def col(first):
        return pl.BlockSpec((s, HEAD_DIM), lambda h: (0, first + h))

    smem = pl.BlockSpec(memory_space=pltpu.SMEM)
    o_spec = pl.BlockSpec((1, 8, HEAD_DIM), lambda h: (h, 0, 0))
    o_shape = jax.ShapeDtypeStruct((n_heads, 8, HEAD_DIM), F32)
    gc, ga = pl.pallas_call(
        body, name="retention_decay_grads", grid=(n_heads,),
        in_specs=[smem, smem, col(a[1]), col(b[1]), col(c[1]), col(e[1])],
        out_specs=[o_spec] * 2, out_shape=[o_shape] * 2,
        scratch_shapes=[pltpu.VMEM((nc, HEAD_DIM, HEAD_DIM), BF16)] * 2,
        compiler_params=_params(("parallel",)),
    )(lg_c, lg_a, a[0], b[0], c[0], e[0])
    return gc[:, 0, 0], ga[:, 0, 0]


def _ret_gate_bwd(dmixed, first_col, out, proj, gate_col, norm_w, n_heads):
    s = out.shape[0]
    tr = _row_block(s, 8 * HEAD_DIM)

    def body(dm_ref, o_ref, g_ref, w_ref, do_ref, dg_ref, dw_ref):
        dm = dm_ref[...].astype(F32)
        ov = o_ref[...].astype(F32)
        g = g_ref[...].astype(F32)
        w = w_ref[...]
        r = lax.rsqrt(jnp.mean(ov * ov, axis=-1, keepdims=True) + EPS)
        ohat = ov * r
        sg = _sigmoid(g)
        silu = g * sg
        dg_ref[...] = (dm * ohat * w * sg * (1.0 + g * (1.0 - sg))).astype(BF16)
        dohat = dm * w * silu
        do_ref[...] = (r * (dohat - ohat * jnp.mean(dohat * ohat, axis=-1, keepdims=True))).astype(BF16)

        @pl.when(pl.program_id(1) == 0)
        def _():
            dw_ref[...] = jnp.zeros_like(dw_ref)

        dw_ref[...] += jnp.sum(dm * ohat * silu, axis=0, keepdims=True)

    def blk(first):
        return pl.BlockSpec((tr, HEAD_DIM), lambda h, i: (i, first + h))

    vec = pl.BlockSpec((1, HEAD_DIM), lambda h, i: (0, h))
    o_shape = jax.ShapeDtypeStruct((s, n_heads * HEAD_DIM), BF16)
    return pl.pallas_call(
        body, name="ret_gate_bwd", grid=(n_heads, s // tr),
        in_specs=[blk(first_col), blk(0), blk(gate_col), vec],
        out_specs=[blk(0), blk(0), vec],
        out_shape=[o_shape, o_shape, jax.ShapeDtypeStruct((1, n_heads * HEAD_DIM), F32)],
        compiler_params=_params(("parallel", "arbitrary")),
    )(dmixed, out, proj, norm_w)


def _step(x, target, norm_mix_w, ret_decay_fwd, ret_decay_bwd, ret_norm_w, norm_ffn_w, norm_final_w, own,
          w_in_started, w_gate_started, shard_ids, pos):
    d = x.shape[1]
    nh = d // (2 * HEAD_DIM)
    scale = HEAD_DIM ** -0.5
    slopes = jnp.exp2(-8.0 * jnp.arange(1, nh + 1, dtype=F32) / nh)
    lg_f = -jnp.exp(ret_decay_fwd)
    lg_b = -jnp.exp(ret_decay_bwd)
    q_r, k_r, v_r, g_r = 3 * nh, 4 * nh, 5 * nh, 6 * nh
    ax = BIG_AXIS

    def gather(names, arrays, stage, part=None, peers=(0, 1, 2)):
        return _gather_job(arrays, [ax[k] for k in names], stage, part, peers)

    def add_halves(k, g, received):
        return _add_halves(g, received, ax[k], pos, name="grad_add_halves_" + k)

    def sum_parts(k, g, received, parts):
        return _sum_chip_parts(g, received, parts, ax[k], pos, name="grad_sum_parts_" + k)

    sems, w_in, token = w_in_started
    n1 = _rmsnorm_fwd(x, norm_mix_w, name="norm_mix_fwd", after=token)
    proj = _in_proj_part(n1, w_in, None, shard_ids, 0, out_cols=w_in.shape[1])
    for peer in range(3):
        behind = [proj] + ([own[k] for k in BIG if k != "w_in"] if peer == 0 else [])
        w_in = _split_gather_wait(sems, w_in, ax["w_in"], peer, behind)
        (w_in,) = _run_jobs([gather(["w_in"], [w_in], "d2d", peers=(peer,))], name="all_gather_w_in_sibling_%d" % peer)
        proj = _in_proj_part(n1, w_in, proj, shard_ids, 1 + peer, out_cols=w_in.shape[1])
    (w_gate,) = _split_wait(w_gate_started, [proj], name="all_gather_w_gate_wait")
    qkv_classes = _to_classes(proj[:, :3 * nh * HEAD_DIM])
    (ret, ret_mixed), [[w_gate], [w_out]] = _retention(
        (proj, q_r), (proj, k_r), (proj, v_r), lg_f, lg_b, strict_c=False, strict_a=True, scale=scale, n_heads=nh,
        name="retention_fwd", gate=(proj, g_r), norm_w=ret_norm_w,
        jobs=[gather(["w_gate"], [w_gate], "d2d"), gather(["w_out"], [own["w_out"]], "ici")])
    (far_out, far_lse), [[w_up]] = _attention_far_fwd(
        qkv_classes, slopes, nh, jobs=[gather(["w_up"], [own["w_up"]], "ici", (0, 1, 4))])
    (attn, lse), [[w_out], [w_up]] = _attention_fwd(
        proj, slopes, _from_classes(far_out), _from_classes(far_lse), nh,
        jobs=[gather(["w_out"], [w_out], "d2d"),
              _fuse(gather(["w_up"], [w_up], "d2d", (0, 1, 4)), gather(["w_up"], [w_up], "ici", (1, 2, 4)))])
    mixed = jnp.concatenate([attn, ret_mixed], axis=1)
    h1, [[w_up]] = _matmul(
        mixed, w_out, name="out_proj", residual=x,
        jobs=[_fuse(gather(["w_up"], [w_up], "d2d", (1, 2, 4)), gather(["w_up"], [w_up], "ici", (3, 1, 4)))])
    up_sibling = _split_start(gather(["w_up"], [w_up], "d2d", (3, 1, 4)), name="all_gather_w_up_sibling_start")
    n2 = _rmsnorm_fwd(h1, norm_ffn_w, name="norm_ffn_fwd", after=up_sibling["token"])
    (w_up,) = _split_wait(up_sibling, [n2], name="all_gather_w_up_sibling_wait")
    (gate, up, act), [[w_down]] = _swiglu_fwd(n2, w_gate, w_up, jobs=[gather(["w_down"], [own["w_down"]], "ici")])
    (w_down,) = _run_jobs([gather(["w_down"], [w_down], "d2d")], name="all_gather_w_down_sibling")
    h2 = _matmul(act, w_down, name="down_proj", residual=h1, tk=2816)
    dh2, dh2_b, d_norm_final, loss = _loss_head(h2, norm_final_w, target)

    dgate, dup = _swiglu_bwd_act(dh2_b, w_down, gate, up)
    g_down = _weight_grad(act, dh2_b, name="grad_w_down")
    g_gate, [[r_down]] = _weight_grad(n2, dgate, name="grad_w_gate", jobs=[_exchange_job([g_down], [ax["w_down"]])])
    s_down = add_halves("w_down", g_down, r_down)
    g_up, [[r_gate], [p_down]] = _weight_grad(
        n2, dup, name="grad_w_up",
        jobs=[_exchange_job([g_gate], [ax["w_gate"]]), _send_sums_job([s_down], [ax["w_down"]], (0, 1, 2))])
    s_gate = add_halves("w_gate", g_gate, r_gate)
    dn2, [[r_up], [p_gate], [p_down]] = _swiglu_bwd_in(
        dgate, dup, w_gate, w_up,
        jobs=[_exchange_job([g_up], [ax["w_up"]]), _send_sums_job([s_gate], [ax["w_gate"]]),
              _send_sums_job([s_down], [ax["w_down"]], (1, 1, 2), landing=[p_down])])
    h_down = sum_parts("w_down", g_down, r_down, p_down)
    s_up = add_halves("w_up", g_up, r_up)
    h_gate = sum_parts("w_gate", g_gate, r_gate, p_gate)
    dh1, dh1_b, d_norm_ffn = _rmsnorm_bwd(dn2, h1, norm_ffn_w, dh2, name="norm_ffn_bwd")

    dmixed, [[gr_down], [p_up]] = _matmul(
        dh1_b, w_out, name="out_proj_bwd", tb=True, out_dtype=BF16,
        jobs=[_join_job([h_down], [ax["w_down"]]), _send_sums_job([s_up], [ax["w_up"]], (0, 1, 4))])
    far_in = [_to_classes(t) for t in (attn, dmixed[:, :nh * HEAD_DIM], lse)]
    g_out, [[p_up]] = _weight_grad(mixed, dh1_b, name="grad_w_out",
                                   jobs=[_send_sums_job([s_up], [ax["w_up"]], (1, 1, 4), landing=[p_up])])
    d_ret, dg_r, d_ret_norm = _ret_gate_bwd(dmixed, nh, ret, proj, g_r, ret_norm_w, nh)
    far_grads = _attention_far_bwd(qkv_classes, slopes, *far_in, nh)
    far_grads = [_from_classes(t) for t in far_grads]
    dq_r, [[gr_gate], [p_up]] = _retention(
        (d_ret, 0), (proj, v_r), (proj, k_r), lg_f, lg_b, strict_c=False, strict_a=True, scale=scale, n_heads=nh,
        name="retention_dq",
        jobs=[_join_job([h_gate], [ax["w_gate"]]), _send_sums_job([s_up], [ax["w_up"]], (2, 1, 4), landing=[p_up])])
    (dq_a, dk_a, dv_a), [[p_up], [r_out]] = _attention_bwd(
        proj, slopes, attn, lse, dmixed, far_grads, nh,
        jobs=[_send_sums_job([s_up], [ax["w_up"]], (3, 1, 4), landing=[p_up]),
              _exchange_job([g_out], [ax["w_out"]])])
    s_out = add_halves("w_out", g_out, r_out)
    h_up = sum_parts("w_up", g_up, r_up, p_up)
    dv_r, [[p_out], [gr_up]] = _retention(
        (proj, k_r), (proj, q_r), (d_ret, 0), lg_b, lg_f, strict_c=True, strict_a=False, scale=scale, n_heads=nh,
        name="retention_dv", jobs=[_send_sums_job([s_out], [ax["w_out"]]), _join_job([h_up], [ax["w_up"]])])
    h_out = sum_parts("w_out", g_out, r_out, p_out)
    dk_r, [[gr_out]] = _retention(
        (proj, v_r), (d_ret, 0), (proj, q_r), lg_b, lg_f, strict_c=True, strict_a=False, scale=scale, n_heads=nh,
        name="retention_dk", jobs=[_join_job([h_out], [ax["w_out"]])])
    dlg_f, dlg_b = _retention_decay_grads((proj, q_r), (proj, k_r), (proj, v_r), (d_ret, 0), lg_f, lg_b,
                                          scale=scale, n_heads=nh)
    dproj = [dq_a, dk_a, dv_a, dq_r, dk_r, dv_r, dg_r]
    g_in = _weight_grad_pieces(n1, dproj, name="grad_w_in")
    exchange = _split_start(_exchange_job([g_in], [ax["w_in"]]), name="grad_exchange_w_in_start")
    dn1 = _matmul_pieces_nt(dproj, w_in, name="in_proj_bwd", after=[exchange["token"]])
    g_in, r_in = _split_wait(exchange, [dn1], name="grad_exchange_w_in_wait")
    s_in = add_halves("w_in", g_in, r_in)
    sending = _split_start(_send_sums_job([s_in], [ax["w_in"]]), name="grad_send_w_in_start")
    dx, _, d_norm_mix = _rmsnorm_bwd(dn1, x, norm_mix_w, dh1, name="norm_mix_bwd", after=[sending["token"]])

    small = dict(loss=loss[0, 0], norm_mix_w=d_norm_mix, ret_decay_fwd=dlg_f * lg_f, ret_decay_bwd=dlg_b * lg_b,
                 ret_norm_w=d_ret_norm, norm_ffn_w=d_norm_ffn, norm_final_w=d_norm_final)
    return (dx, dict(w_out=gr_out, w_gate=gr_gate, w_up=gr_up, w_down=gr_down), small,
            dict(sending=sending, grad=g_in, received=r_in))


def _mesh_position():
    x, y, c = lax.axis_index("x"), lax.axis_index("y"), lax.axis_index("c")
    chips = [(1 - x, y), (x, 1 - y), (1 - x, 1 - y)]
    return x, y, c, chips


def _span(span):
    if span is None:
        return slice(None)
    start, size, step = span
    return pl.ds(start if isinstance(start, int) else pl.multiple_of(start, step), size)


def _part_rows(part, rows):
    first, count, of = part
    return first * (rows // of), count * (rows // of), rows // of


def _region(ref, axis, shard, half, shard_size, half_size, part=None, total_rows=None):
    along = None if shard is None else (shard * shard_size, shard_size, shard_size)
    other = None if half is None else (half * half_size, half_size, half_size)
    rows, cols = (other, along) if axis == 1 else (along, other)
    if part is not None:
        start, size, _ = rows if rows is not None else (0, total_rows, None)
        offset, size, step = _part_rows(part, size)
        rows = (start + offset, size, step)
    return ref.at[_span(rows), _span(cols)]


def _fuse(first, second):
    assert not (first.ins or first.outs or second.ins or second.outs)
    assert len(first.ios) == len(second.ios) and all(a is b for a, b in zip(first.ios, second.ios))
    cut = len(first.sems)

    def start(refs, sems):
        first.start(refs, sems[:cut])
        second.start(refs, sems[cut:])

    def finish(refs, sems):
        first.finish(refs, sems[:cut])
        second.finish(refs, sems[cut:])

    return _Job(ios=first.ios, sems=first.sems + second.sems, start=start, finish=finish)


def _gather_job(full, axes, stage, part=None, peers=(0, 1, 2)):
    n = len(full)

    def copies(refs, sems):
        send_sem, recv_sem = sems
        x, y, c, chips = _mesh_position()
        me = 2 * x + y

        def copy(w, k, shard, half, target):
            rows_cols = full[w].shape
            place = _region(refs[w], axes[w], shard, half, rows_cols[axes[w]] // N_CHIPS, rows_cols[1 - axes[w]] // 2,
                            part)
            return pltpu.make_async_remote_copy(
                src_ref=place, dst_ref=place, send_sem=send_sem.at[w, k], recv_sem=recv_sem.at[w, k],
                device_id=target, device_id_type=MESH)

        def sent(w, k):
            if stage == "ici":
                return copy(w, k, me, c, (chips[k][0], chips[k][1], c))
            return copy(w, k, 2 * chips[k][0] + chips[k][1], c, (x, y, 1 - c))

        def landed(w, k):
            return copy(w, k, 2 * chips[k][0] + chips[k][1], c if stage == "ici" else 1 - c, (x, y, 1 - c))

        return sent, landed

    def start(refs, sems):
        sent, _ = copies(refs, sems)
        for w in range(n):
            for k in peers:
                sent(w, k).start()

    def finish(refs, sems):
        sent, landed = copies(refs, sems)
        for w in range(n):
            for k in peers:
                landed(w, k).wait_recv()
                sent(w, k).wait_send()

    return _Job(ios=full, sems=[pltpu.SemaphoreType.DMA((n, 3))] * 2, start=start, finish=finish)


def _exchange_job(grads, axes):
    n = len(grads)

    def half_shape(w):
        return tuple(d // 2 if a != axes[w] else d for a, d in enumerate(grads[w].shape))

    def copy(refs, sems, w):
        x, y, c, _ = _mesh_position()
        return pltpu.make_async_remote_copy(
            src_ref=_region(refs[w], axes[w], None, 1 - c, 0, half_shape(w)[1 - axes[w]]), dst_ref=refs[n + w],
            send_sem=sems[0].at[w], recv_sem=sems[1].at[w], device_id=(x, y, 1 - c), device_id_type=MESH)

    def start(refs, sems):
        for w in range(n):
            copy(refs, sems, w).start()

    def finish(refs, sems):
        for w in range(n):
            copy(refs, sems, w).wait()

    return _Job(ins=grads, outs=[jax.ShapeDtypeStruct(half_shape(w), F32) for w in range(n)],
                sems=[pltpu.SemaphoreType.DMA((n,))] * 2, start=start, finish=finish)


def _half_block_spec(axis, block, half_blocks, use_half):
    if axis == 1:
        if use_half:
            return pl.BlockSpec(block, lambda i, pos: (pos[0] * half_blocks + i, 0))
        return pl.BlockSpec(block, lambda i, pos: (i, 0))
    if use_half:
        return pl.BlockSpec(block, lambda i, pos: (i, pos[0]))
    return pl.BlockSpec(block, lambda i, pos: (i, 0))


def _add_halves(grad, received, axis, pos, *, name):
    rows, cols = received.shape
    tr = _row_block(rows, cols)
    nb = rows // tr

    def body(pos_ref, g_ref, r_ref, o_ref):
        o_ref[...] = (g_ref[...] + r_ref[...]).astype(BF16)

    blk = (tr, cols)
    return pl.pallas_call(
        body, name=name, out_shape=jax.ShapeDtypeStruct((rows, cols), BF16),
        grid_spec=pltpu.PrefetchScalarGridSpec(
            num_scalar_prefetch=1, grid=(nb,),
            in_specs=[_half_block_spec(axis, blk, nb, True), _half_block_spec(axis, blk, nb, False)],
            out_specs=_half_block_spec(axis, blk, nb, False)),
        compiler_params=_params(("parallel",)),
    )(pos, grad, received)


def _send_sums_job(sums, axes, part=None, landing=None):
    n = len(sums)

    def part_shape(w):
        return tuple(d // N_CHIPS if a == axes[w] else d for a, d in enumerate(sums[w].shape))

    def copy(refs, sems, w, k):
        x, y, c, chips = _mesh_position()
        shard = 2 * chips[k][0] + chips[k][1]
        rows = part_shape(w)[0]
        dst = refs[n + w].at[k]
        if part is not None:
            offset, size, _ = _part_rows(part, rows)
            dst = refs[n + w].at[k, pl.ds(offset, size), :]
        return pltpu.make_async_remote_copy(
            src_ref=_region(refs[w], axes[w], shard, None, part_shape(w)[axes[w]], 0, part, rows), dst_ref=dst,
            send_sem=sems[0].at[w, k], recv_sem=sems[1].at[w, k],
            device_id=(chips[k][0], chips[k][1], c), device_id_type=MESH)

    def start(refs, sems):
        for w in range(n):
            for k in range(3):
                copy(refs, sems, w, k).start()

    def finish(refs, sems):
        for w in range(n):
            for k in range(3):
                copy(refs, sems, w, k).wait()

    sems = [pltpu.SemaphoreType.DMA((n, 3))] * 2
    if landing is not None:
        return _Job(ins=sums, ios=landing, sems=sems, start=start, finish=finish)
    return _Job(ins=sums, outs=[jax.ShapeDtypeStruct((3,) + part_shape(w), BF16) for w in range(n)],
                sems=sems, start=start, finish=finish)


def _sum_chip_parts(grad, received, parts, axis, pos, *, name):
    _, rows, cols = parts.shape
    tr = _row_block(rows, cols)
    nb = rows // tr
    blk = (tr, cols)

    def body(pos_ref, g_ref, r_ref, p_ref, o_ref):
        total = g_ref[...] + r_ref[...]
        for k in range(3):
            total = total + p_ref[k].astype(F32)
        o_ref[...] = total

    if axis == 1:
        g_spec = pl.BlockSpec(blk, lambda i, pos: (pos[0] * nb + i, pos[1]))
        r_spec = pl.BlockSpec(blk, lambda i, pos: (i, pos[1]))
        o_spec = pl.BlockSpec(blk, lambda i, pos: (pos[0] * nb + i, 0))
        shard_shape = (2 * rows, cols)
    else:
        g_spec = pl.BlockSpec(blk, lambda i, pos: (pos[1] * nb + i, pos[0]))
        r_spec = pl.BlockSpec(blk, lambda i, pos: (pos[1] * nb + i, 0))
        o_spec = pl.BlockSpec(blk, lambda i, pos: (i, pos[0]))
        shard_shape = (rows, 2 * cols)
    return pl.pallas_call(
        body, name=name, out_shape=jax.ShapeDtypeStruct(shard_shape, F32),
        grid_spec=pltpu.PrefetchScalarGridSpec(
            num_scalar_prefetch=1, grid=(nb,),
            in_specs=[g_spec, r_spec, pl.BlockSpec((3,) + blk, lambda i, pos: (0, i, 0))],
            out_specs=o_spec),
        compiler_params=_params(("parallel",)),
    )(pos, grad, received, parts)


def _join_job(shards, axes):
    n = len(shards)

    def copy(refs, sems, w, other):
        x, y, c, _ = _mesh_position()
        place = _region(refs[w], axes[w], None, 1 - c if other else c, 0, shards[w].shape[1 - axes[w]] // 2)
        return pltpu.make_async_remote_copy(
            src_ref=place, dst_ref=place, send_sem=sems[0].at[w], recv_sem=sems[1].at[w],
            device_id=(x, y, 1 - c), device_id_type=MESH)

    def start(refs, sems):
        for w in range(n):
            copy(refs, sems, w, False).start()

    def finish(refs, sems):
        for w in range(n):
            copy(refs, sems, w, True).wait_recv()
            copy(refs, sems, w, False).wait_send()

    return _Job(ios=shards, sems=[pltpu.SemaphoreType.DMA((n,))] * 2, start=start, finish=finish)


def _all_reduce_small(vec, after=()):
    rows, cols = vec.shape

    def body(v_ref, *rest):
        o_ref, land_ref, send_sem, recv_sem = rest[len(after):]
        x, y, c, _ = _mesh_position()
        me = 4 * x + 2 * y + c
        land_ref[me] = v_ref[...]
        copies = []
        for k in range(1, 8):
            px, py, pc = x ^ (k >> 2), y ^ ((k >> 1) & 1), c ^ (k & 1)
            copies.append(pltpu.make_async_remote_copy(
                src_ref=v_ref, dst_ref=land_ref.at[me], send_sem=send_sem.at[k], recv_sem=recv_sem.at[k],
                device_id=(px, py, pc), device_id_type=MESH))
        for cp in copies:
            cp.start()
        for k in range(1, 8):
            peer = me ^ k
            pltpu.make_async_remote_copy(
                src_ref=v_ref, dst_ref=land_ref.at[peer], send_sem=send_sem.at[k], recv_sem=recv_sem.at[k],
                device_id=(x, y, c), device_id_type=MESH).wait_recv()
        for cp in copies:
            cp.wait_send()
        total = land_ref[0]
        for k in range(1, 8):
            total = total + land_ref[k]
        o_ref[...] = total

    vmem = pl.BlockSpec(memory_space=pltpu.VMEM)
    return pl.pallas_call(
        body, name="all_reduce_small", in_specs=[vmem] + [pl.BlockSpec(memory_space=pl.ANY)] * len(after),
        out_specs=vmem, out_shape=jax.ShapeDtypeStruct((rows, cols), F32),
        scratch_shapes=[pltpu.VMEM((8, rows, cols), F32), pltpu.SemaphoreType.DMA((8,)), pltpu.SemaphoreType.DMA((8,))],
    )(vec, *after)


def _adamw(w, g, m, v, *, name, after=()):
    rows, cols = w.shape
    tr = _row_block(rows, cols) if rows % 8 == 0 else rows
    bc1 = 1.0 - ADAM_B1 ** ADAM_STEP
    bc2 = 1.0 - ADAM_B2 ** ADAM_STEP

    def body(w_ref, g_ref, m_ref, v_ref, *rest):
        go_ref, d_ref, mo_ref, vo_ref = rest[len(after):]
        gv = g_ref[...]
        go_ref[...] = gv
        mn = ADAM_B1 * m_ref[...] + (1.0 - ADAM_B1) * gv
        vn = ADAM_B2 * v_ref[...] + (1.0 - ADAM_B2) * (gv * gv)
        mo_ref[...] = mn
        vo_ref[...] = vn
        d_ref[...] = -ADAM_LR * ((mn / bc1) / (jnp.sqrt(vn / bc2) + ADAM_EPS) + ADAM_WD * w_ref[...])

    blk = pl.BlockSpec((tr, cols), lambda i: (i, 0))
    shape = jax.ShapeDtypeStruct((rows, cols), F32)
    return pl.pallas_call(
        body, name=name, grid=(rows // tr,), in_specs=[blk] * 4 + [pl.BlockSpec(memory_space=pl.ANY)] * len(after),
        out_specs=[blk] * 4, out_shape=[shape] * 4, compiler_params=_params(("parallel",)),
    )(w, g, m, v, *after)


def _to_bf16_in_place(w, axis, pos, *, name, after=None):
    rows, cols = w.shape
    tr = _row_block(rows, cols)
    nb = rows // tr

    def body(pos_ref, w_ref, *rest):
        rest[-1][...] = w_ref[...].astype(BF16)

    if axis == 1:
        o_spec = pl.BlockSpec((tr, cols), lambda i, pos: (i, pos[1]))
        full_shape = (rows, N_CHIPS * cols)
    else:
        o_spec = pl.BlockSpec((tr, cols), lambda i, pos: (pos[1] * nb + i, 0))
        full_shape = (N_CHIPS * rows, cols)
    in_specs = [pl.BlockSpec((tr, cols), lambda i, pos: (i, 0))]
    operands = [pos, w]
    if after is not None:
        in_specs.append(pl.BlockSpec(after.shape, lambda i, pos: (0, 0)))
        operands.append(after)
    return pl.pallas_call(
        body, name=name, out_shape=jax.ShapeDtypeStruct(full_shape, BF16),
        grid_spec=pltpu.PrefetchScalarGridSpec(num_scalar_prefetch=1, grid=(nb,), in_specs=in_specs, out_specs=o_spec),
        compiler_params=_params(("parallel",)),
    )(*operands)


def _split_gather_start(full, axis):
    rows_cols = full.shape

    def body(buf_ref, *rest):
        sems = rest[:6]
        token_ref = rest[7]
        x, y, c, chips = _mesh_position()
        place = _region(buf_ref, axis, 2 * x + y, c, rows_cols[axis] // N_CHIPS, rows_cols[1 - axis] // 2)
        for k in range(3):
            pltpu.make_async_remote_copy(
                src_ref=place, dst_ref=place, send_sem=sems[k], recv_sem=sems[3 + k],
                device_id=(chips[k][0], chips[k][1], c), device_id_type=MESH).start()
        token_ref[...] = jnp.zeros_like(token_ref)

    hbm = pl.BlockSpec(memory_space=pltpu.HBM)
    sem = pl.BlockSpec(memory_space=pltpu.SEMAPHORE)
    res = pl.pallas_call(
        body, name="all_gather_w_in_start",
        out_shape=(*[pltpu.SemaphoreType.DMA(())] * 6, pltpu.HBM(full.shape, full.dtype),
                   jax.ShapeDtypeStruct((8, HEAD_DIM), F32)),
        in_specs=(hbm,), out_specs=(*[sem] * 6, hbm, pl.BlockSpec(memory_space=pltpu.VMEM)),
        input_output_aliases={0: 6},
        compiler_params=pltpu.CompilerParams(has_side_effects=pltpu.SideEffectType.DATAFLOW_SIDE_EFFECTING),
    )(pltpu.with_memory_space_constraint(full, pltpu.HBM))
    return list(res[:6]), res[6], res[7]


def _split_gather_wait(sems, full, axis, peer, after):
    rows_cols = full.shape

    def body(buf_ref, send_sem, recv_sem, *rest):
        x, y, c, chips = _mesh_position()

        def copy(shard):
            place = _region(buf_ref, axis, shard, c, rows_cols[axis] // N_CHIPS, rows_cols[1 - axis] // 2)
            return pltpu.make_async_remote_copy(
                src_ref=place, dst_ref=place, send_sem=send_sem, recv_sem=recv_sem,
                device_id=(chips[peer][0], chips[peer][1], c), device_id_type=MESH)

        copy(2 * x + y).wait_send()
        copy(2 * chips[peer][0] + chips[peer][1]).wait_recv()

    hbm = pl.BlockSpec(memory_space=pltpu.HBM)
    sem = pl.BlockSpec(memory_space=pltpu.SEMAPHORE)
    return pl.pallas_call(
        body, name="all_gather_w_in_wait_%d" % peer, out_shape=pltpu.HBM(full.shape, full.dtype),
        in_specs=(hbm, sem, sem, *[pl.BlockSpec(memory_space=pl.ANY)] * len(after)), out_specs=hbm,
        input_output_aliases={0: 0},
        compiler_params=pltpu.CompilerParams(has_side_effects=pltpu.SideEffectType.DATAFLOW_SIDE_EFFECTING),
    )(full, sems[peer], sems[3 + peer], *after)


def _in_proj_part(n1, w_in, proj, shard_ids, which, *, out_cols):
    m, kdim = n1.shape
    cols = out_cols // N_CHIPS
    tm = _tile(m, 1024)

    def body(ids_ref, a_ref, b_ref, *rest):
        rest[-1][...] = _dot(a_ref[...], b_ref[...]).astype(BF16)

    in_specs = [pl.BlockSpec((tm, kdim), lambda i, ids: (i, 0)),
                pl.BlockSpec((kdim, cols), lambda i, ids: (0, ids[which]))]
    operands = [shard_ids, n1, w_in]
    if proj is not None:
        in_specs.append(pl.BlockSpec(memory_space=pl.ANY))
        operands.append(proj)
    return pl.pallas_call(
        body, name="in_proj_%d" % which, out_shape=jax.ShapeDtypeStruct((m, out_cols), BF16),
        grid_spec=pltpu.PrefetchScalarGridSpec(
            num_scalar_prefetch=1, grid=(m // tm,), in_specs=in_specs,
            out_specs=pl.BlockSpec((tm, cols), lambda i, ids: (i, ids[which]))),
        input_output_aliases={3: 0} if proj is not None else {},
        compiler_params=_params(("parallel",)),
    )(*operands)


BIG = ("w_in", "w_out", "w_gate", "w_up", "w_down")
BIG_AXIS = dict(w_in=1, w_out=0, w_gate=1, w_up=1, w_down=0)
SMALL = ("norm_mix_w", "ret_decay_fwd", "ret_decay_bwd", "ret_norm_w", "norm_ffn_w", "norm_final_w")
ALL_WEIGHTS = ("norm_mix_w", "w_in", "ret_decay_fwd", "ret_decay_bwd", "ret_norm_w", "w_out", "norm_ffn_w",
               "w_gate", "w_up", "w_down", "norm_final_w")
SMALL_ROW = 128 * 8


def _pack_small(small):
    pieces = [jnp.reshape(small["loss"], (1,))] + [jnp.reshape(small[k], (-1,)) for k in SMALL]
    rows = []
    for p in pieces:
        pad = -p.shape[0] % (8 * SMALL_ROW)
        rows.append(jnp.reshape(jnp.pad(p, (0, pad)), (-1, SMALL_ROW)))
    return jnp.concatenate(rows, axis=0)


def _unpack_small(block, like):
    out, row = {}, 0
    for k in ("loss",) + SMALL:
        size = 1 if k == "loss" else like[k].size
        nrows = -(-size // (8 * SMALL_ROW)) * 8
        out[k] = jnp.reshape(block[row:row + nrows], (-1,))[:size]
        row += nrows
    return out


def kernel(x, norm_mix_w, w_in, ret_decay_fwd, ret_decay_bwd, ret_norm_w, w_out, norm_ffn_w, w_gate, w_up, w_down, norm_final_w, loss_target, m_norm_mix_w, m_w_in, m_ret_decay_fwd, m_ret_decay_bwd, m_ret_norm_w, m_w_out, m_norm_ffn_w, m_w_gate, m_w_up, m_w_down, m_norm_final_w, v_norm_mix_w, v_w_in, v_ret_decay_fwd, v_ret_decay_bwd, v_ret_norm_w, v_w_out, v_norm_ffn_w, v_w_gate, v_w_up, v_w_down, v_norm_final_w):
    weights = dict(norm_mix_w=norm_mix_w, w_in=w_in, ret_decay_fwd=ret_decay_fwd, ret_decay_bwd=ret_decay_bwd,
                   ret_norm_w=ret_norm_w, w_out=w_out, norm_ffn_w=norm_ffn_w, w_gate=w_gate, w_up=w_up,
                   w_down=w_down, norm_final_w=norm_final_w)
    m_in = dict(norm_mix_w=m_norm_mix_w, w_in=m_w_in, ret_decay_fwd=m_ret_decay_fwd, ret_decay_bwd=m_ret_decay_bwd,
                ret_norm_w=m_ret_norm_w, w_out=m_w_out, norm_ffn_w=m_norm_ffn_w, w_gate=m_w_gate, w_up=m_w_up,
                w_down=m_w_down, norm_final_w=m_norm_final_w)
    v_in = dict(norm_mix_w=v_norm_mix_w, w_in=v_w_in, ret_decay_fwd=v_ret_decay_fwd, ret_decay_bwd=v_ret_decay_bwd,
                ret_norm_w=v_ret_norm_w, w_out=v_w_out, norm_ffn_w=v_norm_ffn_w, w_gate=v_w_gate, w_up=v_w_up,
                w_down=v_w_down, norm_final_w=v_norm_final_w)
    pos = jnp.stack([lax.axis_index("c"), 2 * lax.axis_index("x") + lax.axis_index("y")]).astype(jnp.int32)

    own = {"w_in": _to_bf16_in_place(weights["w_in"][0], BIG_AXIS["w_in"], pos, name="cast_w_in")}
    w_in_started = _split_gather_start(own["w_in"], BIG_AXIS["w_in"])
    own["w_gate"] = _to_bf16_in_place(weights["w_gate"][0], BIG_AXIS["w_gate"], pos, name="cast_w_gate",
                                      after=w_in_started[2])
    w_gate_started = _split_start(_gather_job([own["w_gate"]], [BIG_AXIS["w_gate"]], "ici"),
                                  name="all_gather_w_gate_start")
    for k in BIG:
        if k not in own:
            own[k] = _to_bf16_in_place(weights[k][0], BIG_AXIS[k], pos, name="cast_" + k,
                                       after=w_gate_started["token"])
    cx, cy = lax.axis_index("x"), lax.axis_index("y")
    shard_ids = jnp.stack([2 * cx + cy, 2 * (1 - cx) + cy, 2 * cx + 1 - cy, 2 * (1 - cx) + 1 - cy]).astype(jnp.int32)

    dx, grad_w, small, w_in_pending = _step(
        x[0], loss_target[0], norm_mix_w, ret_decay_fwd[0], ret_decay_bwd[0], ret_norm_w, norm_ffn_w,
        norm_final_w[None, :], own, w_in_started, w_gate_started, shard_ids, pos)

    delta, new_m, new_v = {}, {}, {}

    def update(k, after):
        shape = weights[k].shape
        as2d = (lambda t: jnp.reshape(t, (-1, shape[-1])))
        grad_w[k], delta[k], new_m[k], new_v[k] = (jnp.reshape(t, shape) for t in _adamw(
            as2d(weights[k]), as2d(grad_w[k]), as2d(m_in[k]), as2d(v_in[k]), name="adamw_" + k, after=after))

    others = [k for k in BIG if k != "w_in"]
    for k in others:
        update(k, [w_in_pending["sending"]["token"]])
    _, parts = _split_wait(w_in_pending["sending"], [dx] + [delta[k] for k in others], name="grad_send_w_in_wait")

    half = _sum_chip_parts(w_in_pending["grad"], w_in_pending["received"], parts, BIG_AXIS["w_in"], pos,
                           name="grad_sum_parts_w_in")
    joining = _split_start(_join_job([half], [BIG_AXIS["w_in"]]), name="grad_join_w_in_start")

    like = {k: weights[k] for k in SMALL}
    reduced = _unpack_small(_all_reduce_small(_pack_small(small), after=[joining["token"]]), like)
    loss = reduced["loss"][0]
    for k in SMALL:
        grad_w[k] = jnp.reshape(reduced[k], (1, -1))
        update(k, [])
    (grad_w["w_in"],) = _split_wait(joining, [delta[k] for k in SMALL], name="grad_join_w_in_wait")
    update("w_in", [])

    return (loss, dx[None], *[grad_w[k] for k in ALL_WEIGHTS], *[delta[k] for k in ALL_WEIGHTS],
            *[new_m[k] for k in ALL_WEIGHTS], *[new_v[k] for k in ALL_WEIGHTS])
```

```python
import functools
import math

import numpy as np
import jax
import jax.numpy as jnp
from jax import lax
from jax.experimental import pallas as pl
from jax.experimental.pallas import tpu as pltpu

F32 = jnp.float32
BF16 = jnp.bfloat16
MESH = pl.DeviceIdType.MESH

HEAD_DIM = 128
RET_CHUNK = 128
RET_UNROLL = 8
EPS = 1e-6
DILATED_PATTERNS = ((128, 1), (512, 4), (2048, 16))
ATT_BLOCK = 256
ATT_REACH = max(w // 2 for w, _ in DILATED_PATTERNS)
ATT_NEAR = ATT_BLOCK
ATT_CLASSES = DILATED_PATTERNS[-1][1]
assert all(w // 2 <= ATT_NEAR for w, _ in DILATED_PATTERNS[:-1])
ATT_KB = -(-ATT_NEAR // ATT_BLOCK)
ATT_WINDOW = 2 * ATT_KB + 1
ATT_FAR_GROUP = 8
ATT_NEAR_GROUP = 4
MASKED = -1e30
ROW_MAX_INIT = -1e29
N_CHIPS = 4
VMEM_LIMIT_BYTES = 56 * 1024 * 1024
ELEM_BLOCK_BYTES = 2 * 1024 * 1024
WEIGHT_GRAD_GROUP = 4

ADAM_LR = 0.001
ADAM_B1 = 0.9
ADAM_B2 = 0.999
ADAM_EPS = 1e-08
ADAM_WD = 0.01
ADAM_STEP = 10


def _params(sem=None):
    return pltpu.CompilerParams(dimension_semantics=sem, vmem_limit_bytes=VMEM_LIMIT_BYTES)


def _sigmoid(x):
    return 0.5 * jnp.tanh(0.5 * x) + 0.5


class _Job:
    def __init__(self, *, ins=(), ios=(), outs=(), sems=(), start, finish):
        self.ins, self.ios, self.outs, self.sems = list(ins), list(ios), list(outs), list(sems)
        self.start, self.finish = start, finish

    def results(self):
        return [jax.ShapeDtypeStruct(a.shape, a.dtype) for a in self.ios] + self.outs


def _call(body, *, name, grid, in_specs, out_specs, out_shape, operands, scratch_shapes=(), semantics=None, jobs=(),
          after=()):
    in_specs, out_specs, out_shape = list(in_specs), list(out_specs), list(out_shape)
    scratch_shapes = list(scratch_shapes)
    if not jobs:
        n_real = len(in_specs)

        def ordered(*refs):
            body(*refs[:n_real], *refs[n_real + len(after):])

        outs = pl.pallas_call(
            ordered if after else body, name=name, grid=grid,
            in_specs=in_specs + [pl.BlockSpec(memory_space=pl.ANY)] * len(after), out_specs=out_specs,
            out_shape=out_shape, scratch_shapes=scratch_shapes, compiler_params=_params(semantics))(*operands, *after)
        return outs, []
    n_in, n_out, n_scratch = len(in_specs), len(out_specs), len(scratch_shapes)
    extra_in, extra_out, sems, aliases = [], [], [], {}
    for job in jobs:
        extra_in += job.ins
        for t in range(len(job.ios)):
            aliases[n_in + len(extra_in) + t] = n_out + len(extra_out) + t
        extra_in += job.ios
        extra_out += job.results()
        sems += job.sems

    def carried(*refs):
        x_in = refs[n_in:n_in + len(extra_in)]
        first_out = n_in + len(extra_in) + len(after)
        x_out = refs[first_out + n_out:first_out + n_out + len(extra_out)]
        x_sem = refs[len(refs) - len(sems):]
        views, i_in, i_out, i_sem = [], 0, 0, 0
        for job in jobs:
            data = list(x_in[i_in:i_in + len(job.ins)]) + list(x_out[i_out:i_out + len(job.results())])
            views.append((data, x_sem[i_sem:i_sem + len(job.sems)]))
            i_in += len(job.ins) + len(job.ios)
            i_out += len(job.results())
            i_sem += len(job.sems)
        steps = [pl.program_id(d) for d in range(len(grid))]

        @pl.when(functools.reduce(jnp.logical_and, [s == 0 for s in steps]))
        def _():
            for job, (data, sem) in zip(jobs, views):
                job.start(data, sem)

        body(*refs[:n_in], *refs[first_out:first_out + n_out],
             *refs[len(refs) - len(sems) - n_scratch:len(refs) - len(sems)])

        @pl.when(functools.reduce(jnp.logical_and, [s == g - 1 for s, g in zip(steps, grid)]))
        def _():
            for job, (data, sem) in zip(jobs, views):
                job.finish(data, sem)

    hbm = pl.BlockSpec(memory_space=pl.ANY)
    res = pl.pallas_call(
        carried, name=name, grid=grid, in_specs=in_specs + [hbm] * (len(extra_in) + len(after)),
        out_specs=out_specs + [hbm] * len(extra_out), out_shape=out_shape + extra_out,
        input_output_aliases=aliases, scratch_shapes=scratch_shapes + sems,
        compiler_params=_params(("arbitrary",) * len(grid)),
    )(*operands, *extra_in, *after)
    carried_results, at = [], n_out
    for job in jobs:
        carried_results.append(list(res[at:at + len(job.results())]))
        at += len(job.results())
    return list(res[:n_out]), carried_results


def _run_jobs(jobs, *, name):
    first = jobs[0]
    n_in, n_io = len(first.ins), len(first.ios)
    out_shape = first.results()
    n_sems = [len(job.sems) for job in jobs]

    def body(*refs):
        data = list(refs[:n_in]) + list(refs[n_in + n_io:n_in + n_io + len(out_shape)])
        at = n_in + n_io + len(out_shape)
        for job, ns in zip(jobs, n_sems):
            job.start(data, refs[at:at + ns])
            job.finish(data, refs[at:at + ns])
            at += ns

    hbm = pl.BlockSpec(memory_space=pl.ANY)
    return pl.pallas_call(
        body, name=name, in_specs=[hbm] * (n_in + n_io), out_specs=[hbm] * len(out_shape), out_shape=out_shape,
        input_output_aliases={n_in + t: t for t in range(n_io)},
        scratch_shapes=[s for job in jobs for s in job.sems],
    )(*first.ins, *first.ios)


class _SemaphoreGrid:
    def __init__(self, refs, shape):
        self.refs, self.shape = list(refs), tuple(shape)

    @property
    def at(self):
        return self

    def __getitem__(self, index):
        index = index if isinstance(index, tuple) else (index,)
        flat = 0
        for i, extent in zip(index, self.shape):
            flat = flat * extent + i
        return self.refs[flat]


def _semaphore_grids(job, refs):
    grids, at = [], 0
    for sem in job.sems:
        count = math.prod(sem.shape)
        grids.append(_SemaphoreGrid(refs[at:at + count], sem.shape))
        at += count
    return grids


def _split_start(job, *, name):
    arrays = job.ins + job.ios + [lax.empty(s.shape, s.dtype) for s in job.outs]
    n, ns = len(arrays), sum(math.prod(sem.shape) for sem in job.sems)

    def body(*refs):
        job.start(list(refs[:n]), _semaphore_grids(job, refs[n:n + ns]))
        refs[-1][...] = jnp.zeros_like(refs[-1])

    hbm = pl.BlockSpec(memory_space=pltpu.HBM)
    res = pl.pallas_call(
        body, name=name,
        out_shape=(*[pltpu.SemaphoreType.DMA(())] * ns, *[pltpu.HBM(a.shape, a.dtype) for a in arrays],
                   jax.ShapeDtypeStruct((8, HEAD_DIM), F32)),
        in_specs=[hbm] * n,
        out_specs=(*[pl.BlockSpec(memory_space=pltpu.SEMAPHORE)] * ns, *[hbm] * n,
                   pl.BlockSpec(memory_space=pltpu.VMEM)),
        input_output_aliases={t: ns + t for t in range(n)},
        compiler_params=pltpu.CompilerParams(has_side_effects=pltpu.SideEffectType.DATAFLOW_SIDE_EFFECTING),
    )(*[pltpu.with_memory_space_constraint(a, pltpu.HBM) for a in arrays])
    return dict(job=job, sems=list(res[:ns]), arrays=list(res[ns:ns + n]), token=res[-1])


def _split_wait(started, after, *, name):
    job, arrays, sems = started["job"], started["arrays"], started["sems"]
    n, ns = len(arrays), len(sems)

    def body(*refs):
        job.finish(list(refs[:n]), _semaphore_grids(job, refs[n:n + ns]))

    hbm = pl.BlockSpec(memory_space=pltpu.HBM)
    return pl.pallas_call(
        body, name=name, out_shape=[pltpu.HBM(a.shape, a.dtype) for a in arrays],
        in_specs=[hbm] * n + [pl.BlockSpec(memory_space=pltpu.SEMAPHORE)] * ns
        + [pl.BlockSpec(memory_space=pl.ANY)] * len(after),
        out_specs=[hbm] * n, input_output_aliases={t: t for t in range(n)},
        compiler_params=pltpu.CompilerParams(has_side_effects=pltpu.SideEffectType.DATAFLOW_SIDE_EFFECTING),
    )(*arrays, *sems, *after)


def _dot(a, b, ta=False, tb=False):
    return lax.dot_general(a, b, (((0 if ta else 1,), (1 if tb else 0,)), ((), ())),
                           preferred_element_type=F32)


def _tile(n, want):
    t = min(n, want) // 128 * 128
    while n % t:
        t -= 128
    return t


def _a_spec(ta, tm, tk):
    return pl.BlockSpec((tk, tm), lambda i, j, k: (k, i)) if ta else pl.BlockSpec((tm, tk), lambda i, j, k: (i, k))


def _b_spec(tb, tk, tn):
    return pl.BlockSpec((tn, tk), lambda i, j, k: (j, k)) if tb else pl.BlockSpec((tk, tn), lambda i, j, k: (k, j))


def _accumulate(accs, nk, products, finish):
    if nk == 1:
        finish(*products())
        return
    k = pl.program_id(2)

    @pl.when(k == 0)
    def _():
        for acc, p in zip(accs, products()):
            acc[...] = p

    if nk > 2:
        @pl.when(jnp.logical_and(k > 0, k < nk - 1))
        def _():
            for acc, p in zip(accs, products()):
                acc[...] += p

    @pl.when(k == nk - 1)
    def _():
        finish(*[acc[...] + p for acc, p in zip(accs, products())])


def _matmul(a, b, *, name, ta=False, tb=False, out_dtype=F32, residual=None, tm=1024, tn=1024, tk=2048, jobs=()):
    m, kdim = (a.shape[1], a.shape[0]) if ta else a.shape
    n = b.shape[0] if tb else b.shape[1]
    tm, tn, tk = _tile(m, tm), _tile(n, tn), _tile(kdim, tk)
    nk = kdim // tk

    def body(*refs):
        a_ref, b_ref = refs[:2]
        r_ref = refs[2] if residual is not None else None
        o_ref = refs[-1] if nk == 1 else refs[-2]

        def finish(total):
            if residual is not None:
                total = total + r_ref[...]
            o_ref[...] = total.astype(out_dtype)

        _accumulate(refs[-1:] if nk > 1 else (), nk, lambda: (_dot(a_ref[...], b_ref[...], ta, tb),), finish)

    o_spec = pl.BlockSpec((tm, tn), lambda i, j, k: (i, j))
    in_specs = [_a_spec(ta, tm, tk), _b_spec(tb, tk, tn)]
    operands = [a, b]
    if residual is not None:
        in_specs.append(o_spec)
        operands.append(residual)
    (out,), carried = _call(
        body, name=name, grid=(m // tm, n // tn, nk), in_specs=in_specs, out_specs=[o_spec],
        out_shape=[jax.ShapeDtypeStruct((m, n), out_dtype)], operands=operands,
        scratch_shapes=[pltpu.VMEM((tm, tn), F32)] * (nk > 1),
        semantics=("parallel", "parallel", "arbitrary"), jobs=jobs)
    return (out, carried) if jobs else out


def _matmul_pieces_nt(pieces, b, *, name, tm=512, tn=1024, jobs=(), after=()):
    m, kp = pieces[0].shape
    n = b.shape[0]
    tm, tn = _tile(m, tm), _tile(n, tn)
    count = len(pieces)

    def body(*refs):
        b_ref, o_ref = refs[count], refs[count + 1]
        total = _dot(refs[0][...], b_ref[:, pl.ds(0, kp)], tb=True)
        for p in range(1, count):
            total = total + _dot(refs[p][...], b_ref[:, pl.ds(p * kp, kp)], tb=True)
        o_ref[...] = total

    piece = pl.BlockSpec((tm, kp), lambda j, i: (i, 0))
    (out,), carried = _call(
        body, name=name, grid=(n // tn, m // tm),
        in_specs=[piece] * count + [pl.BlockSpec((tn, count * kp), lambda j, i: (j, 0))],
        out_specs=[pl.BlockSpec((tm, tn), lambda j, i: (i, j))],
        out_shape=[jax.ShapeDtypeStruct((m, n), F32)], operands=[*pieces, b],
        semantics=("parallel", "parallel"), jobs=jobs, after=after)
    return (out, carried) if jobs else out


def _weight_grad_pieces(a, pieces, *, name):
    tokens, m = a.shape
    np_ = pieces[0].shape[1]
    tm = 1024 if m % 1024 == 0 else _tile(m, 1408)
    tn = _tile(np_, 512)
    nb = np_ // tn
    out = None
    for first in range(0, len(pieces), WEIGHT_GRAD_GROUP):
        group = pieces[first:first + WEIGHT_GRAD_GROUP]

        def body(*refs, count=len(group)):
            t_now = pl.program_id(1) // nb
            for t in range(count):
                @pl.when(t_now == t)
                def _(t=t):
                    refs[-1][...] = _dot(refs[0][...], refs[1 + t][...], ta=True)

        def piece_spec(t):
            return pl.BlockSpec((tokens, tn), lambda i, j: (0, jnp.clip(j - t * nb, 0, nb - 1)))

        in_specs = [pl.BlockSpec((tokens, tm), lambda i, j: (0, i))] + [piece_spec(t) for t in range(len(group))]
        operands = [a, *group]
        if out is not None:
            in_specs.append(pl.BlockSpec(memory_space=pl.ANY))
            operands.append(out)
        out = pl.pallas_call(
            body, name="%s_%d" % (name, first), grid=(m // tm, nb * len(group)), in_specs=in_specs,
            out_specs=pl.BlockSpec((tm, tn), lambda i, j, first=first: (i, first * nb + j)),
            out_shape=jax.ShapeDtypeStruct((m, len(pieces) * np_), F32),
            input_output_aliases={len(operands) - 1: 0} if out is not None else {},
            compiler_params=_params(("parallel", "arbitrary")),
        )(*operands)
    return out


def _weight_grad(a, g, *, name, jobs=()):
    tokens, m = a.shape
    tm = 1024 if m % 1024 == 0 else _tile(m, 1408)
    return _matmul(a, g, name=name, ta=True, tm=tm, tn=512, tk=tokens, jobs=jobs)


def _swiglu_fwd(n2, w_gate, w_up, *, tm=1024, tn=512, tk=2048, jobs=()):
    m, kdim = n2.shape
    n = w_gate.shape[1]
    tm, tn, tk = _tile(m, tm), _tile(n, tn), _tile(kdim, tk)
    nk = kdim // tk

    def body(a_ref, g_ref, u_ref, gate_ref, up_ref, act_ref, *acc):
        def products():
            a = a_ref[...]
            return _dot(a, g_ref[...]), _dot(a, u_ref[...])

        def finish(g, u):
            gate_ref[...] = g.astype(BF16)
            up_ref[...] = u.astype(BF16)
            act_ref[...] = (g * _sigmoid(g) * u).astype(BF16)

        _accumulate(acc, nk, products, finish)

    o_spec = pl.BlockSpec((tm, tn), lambda i, j, k: (i, j))
    o_shape = jax.ShapeDtypeStruct((m, n), BF16)
    return _call(
        body, name="swiglu_fwd", grid=(m // tm, n // tn, nk),
        in_specs=[_a_spec(False, tm, tk), _b_spec(False, tk, tn), _b_spec(False, tk, tn)],
        out_specs=[o_spec] * 3, out_shape=[o_shape] * 3, operands=[n2, w_gate, w_up],
        scratch_shapes=[pltpu.VMEM((tm, tn), F32)] * (2 * (nk > 1)),
        semantics=("parallel", "parallel", "arbitrary"), jobs=jobs)


def _swiglu_bwd_act(dh2, w_down, gate, up, *, tm=1024, tn=512, tk=2048):
    m, kdim = dh2.shape
    n = w_down.shape[0]
    tm, tn, tk = _tile(m, tm), _tile(n, tn), _tile(kdim, tk)
    nk = kdim // tk

    sub = _tile(tn, 256)

    def body(a_ref, b_ref, gate_ref, up_ref, dgate_ref, dup_ref, *acc):
        def finish(dact, cols=slice(None)):
            g = gate_ref[:, cols].astype(F32)
            u = up_ref[:, cols].astype(F32)
            sg = _sigmoid(g)
            dup_ref[:, cols] = (dact * g * sg).astype(BF16)
            dgate_ref[:, cols] = (dact * u * sg * (1.0 + g * (1.0 - sg))).astype(BF16)

        if nk == 1:
            a = a_ref[...]
            for c in range(tn // sub):
                cols = pl.ds(c * sub, sub)
                finish(_dot(a, b_ref[cols, :], tb=True), cols)
        else:
            _accumulate(acc, nk, lambda: (_dot(a_ref[...], b_ref[...], tb=True),), finish)

    o_spec = pl.BlockSpec((tm, tn), lambda i, j, k: (i, j))
    o_shape = jax.ShapeDtypeStruct((m, n), BF16)
    return pl.pallas_call(
        body, name="swiglu_bwd_act", grid=(m // tm, n // tn, nk),
        in_specs=[_a_spec(False, tm, tk), _b_spec(True, tk, tn), o_spec, o_spec],
        out_specs=[o_spec] * 2, out_shape=[o_shape] * 2,
        scratch_shapes=[pltpu.VMEM((tm, tn), F32)] * (nk > 1),
        compiler_params=_params(("parallel", "parallel", "arbitrary")),
    )(dh2, w_down, gate, up)


def _swiglu_bwd_in(dgate, dup, w_gate, w_up, *, tm=1024, tn=1024, tk=1408, jobs=()):
    m, kdim = dgate.shape
    n = w_gate.shape[0]
    tm, tn, tk = _tile(m, tm), _tile(n, tn), _tile(kdim, tk)
    nk = kdim // tk

    def body(a1_ref, a2_ref, b1_ref, b2_ref, o_ref, *acc):
        def product():
            return (_dot(a1_ref[...], b1_ref[...], tb=True) + _dot(a2_ref[...], b2_ref[...], tb=True),)

        def finish(total):
            o_ref[...] = total

        _accumulate(acc, nk, product, finish)

    a_spec, b_spec = _a_spec(False, tm, tk), _b_spec(True, tk, tn)
    (out,), carried = _call(
        body, name="swiglu_bwd_in", grid=(m // tm, n // tn, nk),
        in_specs=[a_spec, a_spec, b_spec, b_spec],
        out_specs=[pl.BlockSpec((tm, tn), lambda i, j, k: (i, j))],
        out_shape=[jax.ShapeDtypeStruct((m, n), F32)], operands=[dgate, dup, w_gate, w_up],
        scratch_shapes=[pltpu.VMEM((tm, tn), F32)] * (nk > 1),
        semantics=("parallel", "parallel", "arbitrary"), jobs=jobs)
    return out, carried


def _row_block(rows, cols):
    tr = min(rows, max(16, ELEM_BLOCK_BYTES // (4 * cols) // 16 * 16))
    while rows % tr:
        tr -= 16
    return tr


def _rmsnorm_fwd(x, g, *, name, after=None):
    s, d = x.shape
    tr = _row_block(s, d)

    def body(x_ref, g_ref, *rest):
        xv = x_ref[...]
        r = lax.rsqrt(jnp.mean(xv * xv, axis=-1, keepdims=True) + EPS)
        rest[-1][...] = (xv * r * g_ref[...]).astype(BF16)

    row = pl.BlockSpec((tr, d), lambda i: (i, 0))
    in_specs = [row, pl.BlockSpec((1, d), lambda i: (0, 0))]
    operands = [x, g]
    if after is not None:
        in_specs.append(pl.BlockSpec(after.shape, lambda i: (0, 0)))
        operands.append(after)
    return pl.pallas_call(
        body, name=name, grid=(s // tr,), in_specs=in_specs,
        out_specs=row, out_shape=jax.ShapeDtypeStruct((s, d), BF16),
        compiler_params=_params(("parallel",)),
    )(*operands)


def _rmsnorm_bwd_rows(xv, gv, dy):
    r = lax.rsqrt(jnp.mean(xv * xv, axis=-1, keepdims=True) + EPS)
    xhat = xv * r
    dxh = dy * gv
    dx = r * (dxh - xhat * jnp.mean(dxh * xhat, axis=-1, keepdims=True))
    return dx, dy * xhat


def _rmsnorm_bwd(dn, x, g, skip, *, name, after=()):
    s, d = x.shape
    tr = _row_block(s, d)

    def body(dn_ref, x_ref, g_ref, skip_ref, *rest):
        dx_ref, dxb_ref, dg_ref = rest[len(after):]
        dx, dgr = _rmsnorm_bwd_rows(x_ref[...], g_ref[...], dn_ref[...])
        dx = dx + skip_ref[...]
        dx_ref[...] = dx
        dxb_ref[...] = dx.astype(BF16)

        @pl.when(pl.program_id(0) == 0)
        def _():
            dg_ref[...] = jnp.zeros_like(dg_ref)

        dg_ref[...] += jnp.sum(dgr, axis=0, keepdims=True)

    row = pl.BlockSpec((tr, d), lambda i: (i, 0))
    vec = pl.BlockSpec((1, d), lambda i: (0, 0))
    return pl.pallas_call(
        body, name=name, grid=(s // tr,),
        in_specs=[row, row, vec, row] + [pl.BlockSpec(memory_space=pl.ANY)] * len(after),
        out_specs=[row, row, vec],
        out_shape=[jax.ShapeDtypeStruct((s, d), F32), jax.ShapeDtypeStruct((s, d), BF16),
                   jax.ShapeDtypeStruct((1, d), F32)],
        compiler_params=_params(("arbitrary",)),
    )(dn, x, g, skip, *after)


def _loss_head(h2, g, target):
    s, d = h2.shape
    tr = _row_block(s, d)

    def body(h_ref, g_ref, t_ref, dh_ref, dhb_ref, dg_ref, loss_ref):
        hv = h_ref[...]
        gv = g_ref[...]
        r = lax.rsqrt(jnp.mean(hv * hv, axis=-1, keepdims=True) + EPS)
        err = hv * r * gv - t_ref[...]
        dx, dgr = _rmsnorm_bwd_rows(hv, gv, err * (1.0 / d))
        dh_ref[...] = dx
        dhb_ref[...] = dx.astype(BF16)

        @pl.when(pl.program_id(0) == 0)
        def _():
            dg_ref[...] = jnp.zeros_like(dg_ref)
            loss_ref[...] = jnp.zeros_like(loss_ref)

        dg_ref[...] += jnp.sum(dgr, axis=0, keepdims=True)
        row_loss = jnp.mean(err * err, axis=-1, keepdims=True)
        loss_ref[...] += 0.5 * jnp.sum(row_loss, axis=0, keepdims=True)

    row = pl.BlockSpec((tr, d), lambda i: (i, 0))
    vec = pl.BlockSpec((1, d), lambda i: (0, 0))
    one = pl.BlockSpec((1, 1), lambda i: (0, 0))
    return pl.pallas_call(
        body, name="loss_head", grid=(s // tr,), in_specs=[row, vec, row],
        out_specs=[row, row, vec, one],
        out_shape=[jax.ShapeDtypeStruct((s, d), F32), jax.ShapeDtypeStruct((s, d), BF16),
                   jax.ShapeDtypeStruct((1, d), F32), jax.ShapeDtypeStruct((1, 1), F32)],
        compiler_params=_params(("arbitrary",)),
    )(h2, g, target)


def _attention_bias_tables():
    k = np.arange(-ATT_KB, ATT_KB + 1)[:, None, None]
    delta = k * ATT_BLOCK + np.arange(ATT_BLOCK)[None, None, :] - np.arange(ATT_BLOCK)[None, :, None]
    dist = np.abs(delta)
    count = np.zeros(delta.shape, np.int32)
    for window, dilation in DILATED_PATTERNS:
        count += (delta % dilation == 0) & (dist <= min(window // 2, ATT_NEAR))
    logc = np.where(count > 0, np.log(np.maximum(count, 1)), MASKED)
    return dist.astype(np.float32), logc.astype(np.float32)


def _far_bias_tables(per_class):
    steps = np.abs(np.arange(per_class)[:, None] - np.arange(per_class)[None, :]) * ATT_CLASSES
    valid = (steps > ATT_NEAR) & (steps <= ATT_REACH)
    return steps.astype(np.float32), np.where(valid, 0.0, MASKED).astype(np.float32)


def _to_classes(x):
    s, cols = x.shape
    return jnp.reshape(jnp.transpose(jnp.reshape(x, (s // ATT_CLASSES, ATT_CLASSES, cols)), (1, 0, 2)), (s, cols))


def _from_classes(x):
    s, cols = x.shape
    return jnp.reshape(jnp.transpose(jnp.reshape(x, (ATT_CLASSES, s // ATT_CLASSES, cols)), (1, 0, 2)), (s, cols))


def _head_bias(bias_ref, slope, dist_ref, logc_ref):
    for kk in range(ATT_WINDOW):
        bias_ref[kk] = logc_ref[kk] - slope * dist_ref[kk]
    bias_ref[ATT_WINDOW] = jnp.full((ATT_BLOCK, ATT_BLOCK), MASKED, F32)


def _window_start(i, nq, nwin):
    return jnp.clip(i - ATT_KB, 0, nq - nwin)


def _window_block(j, i):
    rows = pl.ds(pl.multiple_of(j * ATT_BLOCK, ATT_BLOCK), ATT_BLOCK)
    kk = j - i + ATT_KB
    return rows, jnp.where(jnp.logical_and(kk >= 0, kk < ATT_WINDOW), kk, ATT_WINDOW)


def _attention_far_fwd(qkv, slopes, n_heads, jobs=(), after=()):
    s = qkv.shape[0]
    per_class = s // ATT_CLASSES
    scale = HEAD_DIM ** -0.5
    dist, logc = _far_bias_tables(per_class)

    def body(slope_ref, q_ref, k_ref, v_ref, dist_ref, logc_ref, o_ref, lse_ref):
        bias = logc_ref[...] - slope_ref[pl.program_id(0)] * dist_ref[...]
        for a in range(ATT_FAR_GROUP):
            rows = pl.ds(a * per_class, per_class)
            sc = _dot(q_ref[rows, :], k_ref[rows, :], tb=True) * scale + bias
            m = jnp.maximum(jnp.max(sc, axis=-1, keepdims=True), ROW_MAX_INIT)
            p = jnp.exp(sc - m)
            l = jnp.maximum(jnp.sum(p, axis=-1, keepdims=True), 1e-30)
            o_ref[rows, :] = (_dot(p.astype(BF16), v_ref[rows, :]) / l).astype(BF16)
            lse_ref[rows, :] = jnp.broadcast_to(m + jnp.log(l), (per_class, HEAD_DIM))

    hh = n_heads
    blk = pl.BlockSpec((ATT_FAR_GROUP * per_class, HEAD_DIM), lambda h, r: (r, h))
    table = pl.BlockSpec(dist.shape, lambda h, r: (0, 0))
    return _call(
        body, name="attention_far_fwd", grid=(hh, ATT_CLASSES // ATT_FAR_GROUP),
        in_specs=[pl.BlockSpec(memory_space=pltpu.SMEM), blk,
                  pl.BlockSpec((ATT_FAR_GROUP * per_class, HEAD_DIM), lambda h, r: (r, hh + h)),
                  pl.BlockSpec((ATT_FAR_GROUP * per_class, HEAD_DIM), lambda h, r: (r, 2 * hh + h)), table, table],
        out_specs=[blk, blk],
        out_shape=[jax.ShapeDtypeStruct((s, hh * HEAD_DIM), BF16), jax.ShapeDtypeStruct((s, hh * HEAD_DIM), F32)],
        operands=[slopes, qkv, qkv, qkv, jnp.asarray(dist), jnp.asarray(logc)],
        semantics=("parallel", "parallel"), jobs=jobs, after=after)


def _attention_fwd(proj, slopes, far_out, far_lse, n_heads, jobs=()):
    s = proj.shape[0]
    nq = s // ATT_BLOCK
    scale = HEAD_DIM ** -0.5
    dist, logc = _attention_bias_tables()

    nwin = min(ATT_WINDOW, nq)

    group = math.gcd(ATT_NEAR_GROUP, nq)

    def body(slope_ref, q_ref, k_ref, v_ref, fo_ref, fl_ref, dist_ref, logc_ref, o_ref, lse_ref, bias_ref, s_ref):
        h, step = pl.program_id(0), pl.program_id(1)

        @pl.when(step == 0)
        def _():
            _head_bias(bias_ref, slope_ref[h], dist_ref, logc_ref)

        for a in range(group):
            i = step * group + a
            mine = pl.ds(a * ATT_BLOCK, ATT_BLOCK)
            q = q_ref[mine, :]
            first = _window_start(i, nq, nwin)
            m = jnp.full((ATT_BLOCK, 1), ROW_MAX_INIT, F32)
            for b in range(nwin):
                rows, kk = _window_block(first + b, i)
                sc = _dot(q, k_ref[rows, :], tb=True) * scale + bias_ref[kk]
                s_ref[a * nwin + b] = sc
                m = jnp.maximum(m, jnp.max(sc, axis=-1, keepdims=True))
            l = jnp.zeros((ATT_BLOCK, 1), F32)
            acc = jnp.zeros((ATT_BLOCK, HEAD_DIM), F32)
            for b in range(nwin):
                rows, _ = _window_block(first + b, i)
                p = jnp.exp(s_ref[a * nwin + b] - m)
                l = l + jnp.sum(p, axis=-1, keepdims=True)
                acc = acc + _dot(p.astype(BF16), v_ref[rows, :])
            near_lse = m + jnp.log(l)
            far_lse_col = fl_ref[mine, :1]
            lse = jnp.maximum(near_lse, far_lse_col)
            lse = lse + jnp.log(jnp.exp(near_lse - lse) + jnp.exp(far_lse_col - lse))
            o_ref[mine, :] = (acc * (jnp.exp(near_lse - lse) / l)
                              + fo_ref[mine, :].astype(F32) * jnp.exp(far_lse_col - lse)).astype(BF16)
            lse_ref[mine, :] = jnp.broadcast_to(lse, (ATT_BLOCK, HEAD_DIM))

    hh = n_heads
    blk = pl.BlockSpec((group * ATT_BLOCK, HEAD_DIM), lambda h, i: (i, h))
    table = pl.BlockSpec(dist.shape, lambda h, i: (0, 0, 0))
    return _call(
        body, name="attention_fwd", grid=(hh, nq // group),
        in_specs=[pl.BlockSpec(memory_space=pltpu.SMEM), blk,
                  pl.BlockSpec((s, HEAD_DIM), lambda h, i: (0, hh + h)),
                  pl.BlockSpec((s, HEAD_DIM), lambda h, i: (0, 2 * hh + h)), blk, blk, table, table],
        out_specs=[blk, blk],
        out_shape=[jax.ShapeDtypeStruct((s, hh * HEAD_DIM), BF16), jax.ShapeDtypeStruct((s, hh * HEAD_DIM), F32)],
        operands=[slopes, proj, proj, proj, far_out, far_lse, jnp.asarray(dist), jnp.asarray(logc)],
        scratch_shapes=[pltpu.VMEM((ATT_WINDOW + 1, ATT_BLOCK, ATT_BLOCK), F32),
                        pltpu.VMEM((group * nwin, ATT_BLOCK, ATT_BLOCK), F32)],
        semantics=("parallel", "arbitrary"), jobs=jobs)


def _attention_far_bwd(qkv, slopes, out, dout, lse, n_heads):
    s = qkv.shape[0]
    per_class = s // ATT_CLASSES
    scale = HEAD_DIM ** -0.5
    dist, logc = _far_bias_tables(per_class)

    def body(slope_ref, q_ref, k_ref, v_ref, o_ref, do_ref, lse_ref, dist_ref, logc_ref, dq_ref, dk_ref, dv_ref):
        bias = logc_ref[...] - slope_ref[pl.program_id(0)] * dist_ref[...]
        for a in range(ATT_FAR_GROUP):
            rows = pl.ds(a * per_class, per_class)
            q, k, do = q_ref[rows, :], k_ref[rows, :], do_ref[rows, :]
            delta = jnp.sum(do.astype(F32) * o_ref[rows, :].astype(F32), axis=-1, keepdims=True)
            p = jnp.exp(_dot(q, k, tb=True) * scale + bias - lse_ref[rows, :1])
            dv_ref[rows, :] = _dot(p.astype(BF16), do, ta=True).astype(BF16)
            ds = (p * (_dot(do, v_ref[rows, :], tb=True) - delta) * scale).astype(BF16)
            dk_ref[rows, :] = _dot(ds, q, ta=True).astype(BF16)
            dq_ref[rows, :] = _dot(ds, k).astype(BF16)

    hh = n_heads
    blk = pl.BlockSpec((ATT_FAR_GROUP * per_class, HEAD_DIM), lambda h, r: (r, h))
    table = pl.BlockSpec(dist.shape, lambda h, r: (0, 0))
    o_shape = jax.ShapeDtypeStruct((s, hh * HEAD_DIM), BF16)
    return pl.pallas_call(
        body, name="attention_far_bwd", grid=(hh, ATT_CLASSES // ATT_FAR_GROUP),
        in_specs=[pl.BlockSpec(memory_space=pltpu.SMEM), blk,
                  pl.BlockSpec((ATT_FAR_GROUP * per_class, HEAD_DIM), lambda h, r: (r, hh + h)),
                  pl.BlockSpec((ATT_FAR_GROUP * per_class, HEAD_DIM), lambda h, r: (r, 2 * hh + h)),
                  blk, blk, blk, table, table],
        out_specs=[blk] * 3, out_shape=[o_shape] * 3,
        compiler_params=_params(("parallel", "parallel")),
    )(slopes, qkv, qkv, qkv, out, dout, lse, jnp.asarray(dist), jnp.asarray(logc))


def _attention_bwd(proj, slopes, out, lse, dmixed, far_grads, n_heads, jobs=()):
    s = proj.shape[0]
    nq = s // ATT_BLOCK
    scale = HEAD_DIM ** -0.5
    dist, logc = _attention_bias_tables()

    nwin = min(ATT_WINDOW, nq)
    group = math.gcd(ATT_NEAR_GROUP, nq)

    def body(slope_ref, q_ref, k_ref, v_ref, o_ref, do_ref, lse_ref, fdq_ref, fdk_ref, fdv_ref, dist_ref, logc_ref,
             dq_ref, dk_ref, dv_ref, dk_acc, dv_acc, bias_ref):
        h, step = pl.program_id(0), pl.program_id(1)

        @pl.when(step == 0)
        def _():
            dk_acc[...] = jnp.zeros_like(dk_acc)
            dv_acc[...] = jnp.zeros_like(dv_acc)
            _head_bias(bias_ref, slope_ref[h], dist_ref, logc_ref)

        for a in range(group):
            i = step * group + a
            mine = pl.ds(a * ATT_BLOCK, ATT_BLOCK)
            q = q_ref[mine, :]
            do = do_ref[mine, :]
            lse_col = lse_ref[mine, :1]
            delta = jnp.sum(do.astype(F32) * o_ref[mine, :].astype(F32), axis=-1, keepdims=True)
            first = _window_start(i, nq, nwin)
            dq = jnp.zeros((ATT_BLOCK, HEAD_DIM), F32)
            for b in range(nwin):
                rows, kk = _window_block(first + b, i)
                kj = k_ref[rows, :]
                vj = v_ref[rows, :]
                p = jnp.exp(_dot(q, kj, tb=True) * scale + bias_ref[kk] - lse_col)
                dv_acc[rows, :] += _dot(p.astype(BF16), do, ta=True)
                dp = _dot(do, vj, tb=True)
                ds = (p * (dp - delta) * scale).astype(BF16)
                dk_acc[rows, :] += _dot(ds, q, ta=True)
                dq = dq + _dot(ds, kj)
            dq_ref[mine, :] = (dq + fdq_ref[mine, :].astype(F32)).astype(BF16)

        @pl.when(step == nq // group - 1)
        def _():
            dk_ref[...] = (dk_acc[...] + fdk_ref[...].astype(F32)).astype(BF16)
            dv_ref[...] = (dv_acc[...] + fdv_ref[...].astype(F32)).astype(BF16)

    hh = n_heads
    blk = pl.BlockSpec((group * ATT_BLOCK, HEAD_DIM), lambda h, i: (i, h))
    col = pl.BlockSpec((s, HEAD_DIM), lambda h, i: (0, h))
    table = pl.BlockSpec(dist.shape, lambda h, i: (0, 0, 0))
    o_shape = jax.ShapeDtypeStruct((s, hh * HEAD_DIM), BF16)
    return _call(
        body, name="attention_bwd", grid=(hh, nq // group),
        in_specs=[pl.BlockSpec(memory_space=pltpu.SMEM), blk,
                  pl.BlockSpec((s, HEAD_DIM), lambda h, i: (0, hh + h)),
                  pl.BlockSpec((s, HEAD_DIM), lambda h, i: (0, 2 * hh + h)),
                  blk, blk, blk, blk, col, col, table, table],
        out_specs=[blk, col, col], out_shape=[o_shape] * 3,
        operands=[slopes, proj, proj, proj, out, dmixed, lse, *far_grads, jnp.asarray(dist), jnp.asarray(logc)],
        scratch_shapes=[pltpu.VMEM((s, HEAD_DIM), F32)] * 2
        + [pltpu.VMEM((ATT_WINDOW + 1, ATT_BLOCK, ATT_BLOCK), F32)],
        semantics=("parallel", "arbitrary"), jobs=jobs)


def _ret_decays(lgc, lga, strict_c, strict_a):
    c = RET_CHUNK
    rel = (lax.broadcasted_iota(jnp.int32, (c, c), 0) - lax.broadcasted_iota(jnp.int32, (c, c), 1)).astype(F32)
    in_c = (rel > 0) if strict_c else (rel >= 0)
    in_a = (rel < 0) if strict_a else (rel <= 0)
    mask = (jnp.where(in_c, jnp.exp(lgc * jnp.maximum(rel, 0.0)), 0.0)
            + jnp.where(in_a, jnp.exp(lga * jnp.maximum(-rel, 0.0)), 0.0))
    idx = lax.broadcasted_iota(jnp.int32, (c, 1), 0).astype(F32)
    ones = jnp.ones((1, HEAD_DIM), F32)
    dec = dict(
        rel=rel, mask=mask, idx=idx,
        a_c=jnp.exp(lgc * (idx + 1.0)), b_c=jnp.exp(lgc * (c - 1.0 - idx)), chunk_c=jnp.exp(ones * (lgc * c)),
        a_a=jnp.exp(lga * (c - idx)), b_a=jnp.exp(lga * idx), chunk_a=jnp.exp(ones * (lga * c)),
    )
    return dec


def _scaled(x, col):
    return (x.astype(F32) * col).astype(BF16)


def _chunk_rows(i):
    return pl.ds(pl.multiple_of(i * RET_CHUNK, RET_CHUNK), RET_CHUNK)


def _chunk_loop(nc, step, init, unroll=RET_UNROLL):
    group = math.gcd(nc, unroll)

    def trip(t, carry):
        for u in range(group):
            carry = step(t * group + u, carry)
        return carry

    return lax.fori_loop(0, nc // group, trip, init)


def _retention(a, b, c, lg_c, lg_a, *, strict_c, strict_a, scale, n_heads, name, gate=None, norm_w=None, jobs=()):
    s = a[0].shape[0]
    nc = s // RET_CHUNK
    epilogue = gate is not None

    def body(*refs):
        lgc_ref, lga_ref, a_ref, b_ref, c_ref = refs[:5]
        if epilogue:
            g_ref, w_ref, o_ref, mix_ref, sa_ref = refs[5:]
        else:
            o_ref, sa_ref = refs[5:]
        h = pl.program_id(0)
        dec = _ret_decays(lgc_ref[h], lga_ref[h], strict_c, strict_a)

        def reverse(t, state):
            i = nc - 1 - t
            sa_ref[i] = state.astype(BF16)
            rows = _chunk_rows(i)
            return state * dec["chunk_a"] + _dot(_scaled(b_ref[rows, :], dec["b_a"]), c_ref[rows, :], ta=True)

        _chunk_loop(nc, reverse, jnp.zeros((HEAD_DIM, HEAD_DIM), F32))

        def forward(i, state):
            rows = _chunk_rows(i)
            ai, bi, ci = a_ref[rows, :], b_ref[rows, :], c_ref[rows, :]
            inner = (_dot(ai, bi, tb=True) * dec["mask"]).astype(BF16)
            out = (_dot(inner, ci) + _dot(_scaled(ai, dec["a_c"]), state.astype(BF16))
                   + _dot(_scaled(ai, dec["a_a"]), sa_ref[i])) * scale
            o_ref[rows, :] = out.astype(BF16)
            if epilogue:
                r = lax.rsqrt(jnp.mean(out * out, axis=-1, keepdims=True) + EPS)
                g = g_ref[rows, :].astype(F32)
                mix_ref[rows, :] = (out * r * w_ref[...] * (g * _sigmoid(g))).astype(BF16)
            return state * dec["chunk_c"] + _dot(_scaled(bi, dec["b_c"]), ci, ta=True)

        _chunk_loop(nc, forward, jnp.zeros((HEAD_DIM, HEAD_DIM), F32))

    def col(first):
        return pl.BlockSpec((s, HEAD_DIM), lambda h: (0, first + h))

    smem = pl.BlockSpec(memory_space=pltpu.SMEM)
    in_specs = [smem, smem, col(a[1]), col(b[1]), col(c[1])]
    operands = [lg_c, lg_a, a[0], b[0], c[0]]
    o_shape = jax.ShapeDtypeStruct((s, n_heads * HEAD_DIM), BF16)
    out_specs, out_shape = [col(0)], [o_shape]
    if epilogue:
        in_specs += [col(gate[1]), pl.BlockSpec((1, HEAD_DIM), lambda h: (0, h))]
        operands += [gate[0], norm_w]
        out_specs, out_shape = [col(0)] * 2, [o_shape] * 2
    res, carried = _call(
        body, name=name, grid=(n_heads,), in_specs=in_specs, out_specs=out_specs, out_shape=out_shape,
        operands=operands, scratch_shapes=[pltpu.VMEM((nc, HEAD_DIM, HEAD_DIM), BF16)],
        semantics=("parallel",), jobs=jobs)
    res = res if epilogue else res[0]
    return (res, carried) if jobs else res


def _retention_decay_grads(a, b, c, e, lg_c, lg_a, *, scale, n_heads):
    s = a[0].shape[0]
    nc = s // RET_CHUNK
    cf = float(RET_CHUNK)

    def body(lgc_ref, lga_ref, a_ref, b_ref, c_ref, e_ref, gc_ref, ga_ref, sa_ref, ta_ref):
        h = pl.program_id(0)
        lgc, lga = lgc_ref[h], lga_ref[h]
        dec = _ret_decays(lgc, lga, True, True)
        rel, idx = dec["rel"], dec["idx"]
        w_c = jnp.where(rel > 0, rel * jnp.exp(lgc * jnp.maximum(rel, 0.0)), 0.0)
        w_a = jnp.where(rel < 0, -rel * jnp.exp(lga * jnp.maximum(-rel, 0.0)), 0.0)
        zero = jnp.zeros((HEAD_DIM, HEAD_DIM), F32)

        def reverse(t, carry):
            st, dst = carry
            i = nc - 1 - t
            sa_ref[i] = st.astype(BF16)
            ta_ref[i] = dst.astype(BF16)
            rows = _chunk_rows(i)
            bi, ci = b_ref[rows, :], c_ref[rows, :]
            st_new = st * dec["chunk_a"] + _dot(_scaled(bi, dec["b_a"]), ci, ta=True)
            dst_new = (cf * st + dst) * dec["chunk_a"] + _dot(_scaled(bi, idx * dec["b_a"]), ci, ta=True)
            return st_new, dst_new

        _chunk_loop(nc, reverse, (zero, zero))

        def forward(i, carry):
            st, dst, acc_c, acc_a = carry
            rows = _chunk_rows(i)
            ai, bi, ci = a_ref[rows, :], b_ref[rows, :], c_ref[rows, :]
            ev = e_ref[rows, :].astype(F32)
            pg = _dot(ai, bi, tb=True) * _dot(e_ref[rows, :], ci, tb=True)
            a_c, a_a = _scaled(ai, dec["a_c"]), _scaled(ai, dec["a_a"])
            inter_c = _dot(a_c, st.astype(BF16)) * (idx + 1.0) + _dot(a_c, dst.astype(BF16))
            inter_a = _dot(a_a, sa_ref[i]) * (cf - idx) + _dot(a_a, ta_ref[i])
            acc_c = acc_c + jnp.sum(pg * w_c, axis=0, keepdims=True) + jnp.sum(inter_c * ev, axis=0, keepdims=True)
            acc_a = acc_a + jnp.sum(pg * w_a, axis=0, keepdims=True) + jnp.sum(inter_a * ev, axis=0, keepdims=True)
            st_new = st * dec["chunk_c"] + _dot(_scaled(bi, dec["b_c"]), ci, ta=True)
            dst_new = ((cf * st + dst) * dec["chunk_c"]
                       + _dot(_scaled(bi, (cf - 1.0 - idx) * dec["b_c"]), ci, ta=True))
            return st_new, dst_new, acc_c, acc_a

        row = jnp.zeros((1, HEAD_DIM), F32)
        _, _, acc_c, acc_a = _chunk_loop(nc, forward, (zero, zero, row, row))
        gc_ref[...] = jnp.broadcast_to(jnp.sum(acc_c, axis=-1, keepdims=True) * scale, gc_ref.shape)
        ga_ref[...] = jnp.broadcast_to(jnp.sum(acc_a, axis=-1, keepdims=True) * scale, ga_ref.shape)

    def col(first):
        return pl.BlockSpec((s, HEAD_DIM), lambda h: (0, first + h))

    smem = pl.BlockSpec(memory_space=pltpu.SMEM)
    o_spec = pl.BlockSpec((1, 8, HEAD_DIM), lambda h: (h, 0, 0))
    o_shape = jax.ShapeDtypeStruct((n_heads, 8, HEAD_DIM), F32)
    gc, ga = pl.pallas_call(
        body, name="retention_decay_grads", grid=(n_heads,),
        in_specs=[smem, smem, col(a[1]), col(b[1]), col(c[1]), col(e[1])],
        out_specs=[o_spec] * 2, out_shape=[o_shape] * 2,
        scratch_shapes=[pltpu.VMEM((nc, HEAD_DIM, HEAD_DIM), BF16)] * 2,
        compiler_params=_params(("parallel",)),
    )(lg_c, lg_a, a[0], b[0], c[0], e[0])
    return gc[:, 0, 0], ga[:, 0, 0]


def _ret_gate_bwd(dmixed, first_col, out, proj, gate_col, norm_w, n_heads):
    s = out.shape[0]
    tr = _row_block(s, 8 * HEAD_DIM)

    def body(dm_ref, o_ref, g_ref, w_ref, do_ref, dg_ref, dw_ref):
        dm = dm_ref[...].astype(F32)
        ov = o_ref[...].astype(F32)
        g = g_ref[...].astype(F32)
        w = w_ref[...]
        r = lax.rsqrt(jnp.mean(ov * ov, axis=-1, keepdims=True) + EPS)
        ohat = ov * r
        sg = _sigmoid(g)
        silu = g * sg
        dg_ref[...] = (dm * ohat * w * sg * (1.0 + g * (1.0 - sg))).astype(BF16)
        dohat = dm * w * silu
        do_ref[...] = (r * (dohat - ohat * jnp.mean(dohat * ohat, axis=-1, keepdims=True))).astype(BF16)

        @pl.when(pl.program_id(1) == 0)
        def _():
            dw_ref[...] = jnp.zeros_like(dw_ref)

        dw_ref[...] += jnp.sum(dm * ohat * silu, axis=0, keepdims=True)

    def blk(first):
        return pl.BlockSpec((tr, HEAD_DIM), lambda h, i: (i, first + h))

    vec = pl.BlockSpec((1, HEAD_DIM), lambda h, i: (0, h))
    o_shape = jax.ShapeDtypeStruct((s, n_heads * HEAD_DIM), BF16)
    return pl.pallas_call(
        body, name="ret_gate_bwd", grid=(n_heads, s // tr),
        in_specs=[blk(first_col), blk(0), blk(gate_col), vec],
        out_specs=[blk(0), blk(0), vec],
        out_shape=[o_shape, o_shape, jax.ShapeDtypeStruct((1, n_heads * HEAD_DIM), F32)],
        compiler_params=_params(("parallel", "arbitrary")),
    )(dmixed, out, proj, norm_w)


def _step(x, target, norm_mix_w, ret_decay_fwd, ret_decay_bwd, ret_norm_w, norm_ffn_w, norm_final_w, own,
          w_in_started, w_gate_started, shard_ids, pos):
    d = x.shape[1]
    nh = d // (2 * HEAD_DIM)
    scale = HEAD_DIM ** -0.5
    slopes = jnp.exp2(-8.0 * jnp.arange(1, nh + 1, dtype=F32) / nh)
    lg_f = -jnp.exp(ret_decay_fwd)
    lg_b = -jnp.exp(ret_decay_bwd)
    q_r, k_r, v_r, g_r = 3 * nh, 4 * nh, 5 * nh, 6 * nh
    ax = BIG_AXIS

    def gather(names, arrays, stage, part=None, peers=(0, 1, 2)):
        return _gather_job(arrays, [ax[k] for k in names], stage, part, peers)

    def add_halves(k, g, received):
        return _add_halves(g, received, ax[k], pos, name="grad_add_halves_" + k)

    def sum_parts(k, g, received, parts):
        return _sum_chip_parts(g, received, parts, ax[k], pos, name="grad_sum_parts_" + k)

    sems, w_in, token = w_in_started
    n1 = _rmsnorm_fwd(x, norm_mix_w, name="norm_mix_fwd", after=token)
    proj = _in_proj_part(n1, w_in, None, shard_ids, 0, out_cols=w_in.shape[1])
    for peer in range(3):
        behind = [proj] + ([own[k] for k in ("w_out", "w_up", "w_down")] if peer == 0 else [])
        w_in = _split_gather_wait(sems, w_in, ax["w_in"], peer, behind)
        (w_in,) = _run_jobs([gather(["w_in"], [w_in], "d2d", peers=(peer,))], name="all_gather_w_in_sibling_%d" % peer)
        proj = _in_proj_part(n1, w_in, proj, shard_ids, 1 + peer, out_cols=w_in.shape[1])
    (w_gate,) = _split_wait(w_gate_started, [proj], name="all_gather_w_gate_wait")
    qkv_classes = _to_classes(proj[:, :3 * nh * HEAD_DIM])
    (ret, ret_mixed), [[w_gate], [w_out]] = _retention(
        (proj, q_r), (proj, k_r), (proj, v_r), lg_f, lg_b, strict_c=False, strict_a=True, scale=scale, n_heads=nh,
        name="retention_fwd", gate=(proj, g_r), norm_w=ret_norm_w,
        jobs=[gather(["w_gate"], [w_gate], "d2d"), gather(["w_out"], [own["w_out"]], "ici")])
    (far_out, far_lse), [[w_up]] = _attention_far_fwd(
        qkv_classes, slopes, nh, jobs=[gather(["w_up"], [own["w_up"]], "ici", (0, 1, 4))], after=[ret_mixed])
    (attn, lse), [[w_out], [w_up]] = _attention_fwd(
        proj, slopes, _from_classes(far_out), _from_classes(far_lse), nh,
        jobs=[gather(["w_out"], [w_out], "d2d"),
              _fuse(gather(["w_up"], [w_up], "d2d", (0, 1, 4)), gather(["w_up"], [w_up], "ici", (1, 2, 4)))])
    mixed = jnp.concatenate([attn, ret_mixed], axis=1)
    h1, [[w_up]] = _matmul(
        mixed, w_out, name="out_proj", residual=x,
        jobs=[_fuse(gather(["w_up"], [w_up], "d2d", (1, 2, 4)), gather(["w_up"], [w_up], "ici", (3, 1, 4)))])
    up_sibling = _split_start(gather(["w_up"], [w_up], "d2d", (3, 1, 4)), name="all_gather_w_up_sibling_start")
    n2 = _rmsnorm_fwd(h1, norm_ffn_w, name="norm_ffn_fwd", after=up_sibling["token"])
    (w_up,) = _split_wait(up_sibling, [n2], name="all_gather_w_up_sibling_wait")
    (gate, up, act), [[w_down]] = _swiglu_fwd(n2, w_gate, w_up, jobs=[gather(["w_down"], [own["w_down"]], "ici")])
    (w_down,) = _run_jobs([gather(["w_down"], [w_down], "d2d")], name="all_gather_w_down_sibling")
    h2 = _matmul(act, w_down, name="down_proj", residual=h1, tk=2816)
    dh2, dh2_b, d_norm_final, loss = _loss_head(h2, norm_final_w, target)

    dgate, dup = _swiglu_bwd_act(dh2_b, w_down, gate, up)
    g_down = _weight_grad(act, dh2_b, name="grad_w_down")
    g_gate, [[r_down]] = _weight_grad(n2, dgate, name="grad_w_gate", jobs=[_exchange_job([g_down], [ax["w_down"]])])
    s_down = add_halves("w_down", g_down, r_down)
    g_up, [[r_gate], [p_down]] = _weight_grad(
        n2, dup, name="grad_w_up",
        jobs=[_exchange_job([g_gate], [ax["w_gate"]]), _send_sums_job([s_down], [ax["w_down"]], (0, 1, 2))])
    s_gate = add_halves("w_gate", g_gate, r_gate)
    dn2, [[r_up], [p_gate], [p_down]] = _swiglu_bwd_in(
        dgate, dup, w_gate, w_up,
        jobs=[_exchange_job([g_up], [ax["w_up"]]), _send_sums_job([s_gate], [ax["w_gate"]]),
              _send_sums_job([s_down], [ax["w_down"]], (1, 1, 2), landing=[p_down])])
    h_down = sum_parts("w_down", g_down, r_down, p_down)
    s_up = add_halves("w_up", g_up, r_up)
    h_gate = sum_parts("w_gate", g_gate, r_gate, p_gate)
    dh1, dh1_b, d_norm_ffn = _rmsnorm_bwd(dn2, h1, norm_ffn_w, dh2, name="norm_ffn_bwd")

    dmixed, [[gr_down], [p_up]] = _matmul(
        dh1_b, w_out, name="out_proj_bwd", tb=True, out_dtype=BF16,
        jobs=[_join_job([h_down], [ax["w_down"]]), _send_sums_job([s_up], [ax["w_up"]], (0, 1, 4))])
    far_in = [_to_classes(t) for t in (attn, dmixed[:, :nh * HEAD_DIM], lse)]
    g_out, [[p_up]] = _weight_grad(mixed, dh1_b, name="grad_w_out",
                                   jobs=[_send_sums_job([s_up], [ax["w_up"]], (1, 1, 4), landing=[p_up])])
    d_ret, dg_r, d_ret_norm = _ret_gate_bwd(dmixed, nh, ret, proj, g_r, ret_norm_w, nh)
    far_grads = _attention_far_bwd(qkv_classes, slopes, *far_in, nh)
    far_grads = [_from_classes(t) for t in far_grads]
    dq_r, [[gr_gate], [p_up]] = _retention(
        (d_ret, 0), (proj, v_r), (proj, k_r), lg_f, lg_b, strict_c=False, strict_a=True, scale=scale, n_heads=nh,
        name="retention_dq",
        jobs=[_join_job([h_gate], [ax["w_gate"]]), _send_sums_job([s_up], [ax["w_up"]], (2, 1, 4), landing=[p_up])])
    (dq_a, dk_a, dv_a), [[p_up], [r_out]] = _attention_bwd(
        proj, slopes, attn, lse, dmixed, far_grads, nh,
        jobs=[_send_sums_job([s_up], [ax["w_up"]], (3, 1, 4), landing=[p_up]),
              _exchange_job([g_out], [ax["w_out"]])])
    s_out = add_halves("w_out", g_out, r_out)
    h_up = sum_parts("w_up", g_up, r_up, p_up)
    dv_r, [[p_out], [gr_up]] = _retention(
        (proj, k_r), (proj, q_r), (d_ret, 0), lg_b, lg_f, strict_c=True, strict_a=False, scale=scale, n_heads=nh,
        name="retention_dv", jobs=[_send_sums_job([s_out], [ax["w_out"]]), _join_job([h_up], [ax["w_up"]])])
    h_out = sum_parts("w_out", g_out, r_out, p_out)
    dk_r, [[gr_out]] = _retention(
        (proj, v_r), (d_ret, 0), (proj, q_r), lg_b, lg_f, strict_c=True, strict_a=False, scale=scale, n_heads=nh,
        name="retention_dk", jobs=[_join_job([h_out], [ax["w_out"]])])
    dlg_f, dlg_b = _retention_decay_grads((proj, q_r), (proj, k_r), (proj, v_r), (d_ret, 0), lg_f, lg_b,
                                          scale=scale, n_heads=nh)
    dproj = [dq_a, dk_a, dv_a, dq_r, dk_r, dv_r, dg_r]
    g_in = _weight_grad_pieces(n1, dproj, name="grad_w_in")
    exchange = _split_start(_exchange_job([g_in], [ax["w_in"]]), name="grad_exchange_w_in_start")
    dn1 = _matmul_pieces_nt(dproj, w_in, name="in_proj_bwd", after=[exchange["token"]])
    g_in, r_in = _split_wait(exchange, [dn1], name="grad_exchange_w_in_wait")
    s_in = add_halves("w_in", g_in, r_in)
    sending = _split_start(_send_sums_job([s_in], [ax["w_in"]]), name="grad_send_w_in_start")
    dx, _, d_norm_mix = _rmsnorm_bwd(dn1, x, norm_mix_w, dh1, name="norm_mix_bwd", after=[sending["token"]])

    small = dict(loss=loss[0, 0], norm_mix_w=d_norm_mix, ret_decay_fwd=dlg_f * lg_f, ret_decay_bwd=dlg_b * lg_b,
                 ret_norm_w=d_ret_norm, norm_ffn_w=d_norm_ffn, norm_final_w=d_norm_final)
    return (dx, dict(w_out=gr_out, w_gate=gr_gate, w_up=gr_up, w_down=gr_down), small,
            dict(sending=sending, grad=g_in, received=r_in))


def _mesh_position():
    x, y, c = lax.axis_index("x"), lax.axis_index("y"), lax.axis_index("c")
    chips = [(1 - x, y), (x, 1 - y), (1 - x, 1 - y)]
    return x, y, c, chips


def _span(span):
    if span is None:
        return slice(None)
    start, size, step = span
    return pl.ds(start if isinstance(start, int) else pl.multiple_of(start, step), size)


def _part_rows(part, rows):
    first, count, of = part
    return first * (rows // of), count * (rows // of), rows // of


def _region(ref, axis, shard, half, shard_size, half_size, part=None, total_rows=None):
    along = None if shard is None else (shard * shard_size, shard_size, shard_size)
    other = None if half is None else (half * half_size, half_size, half_size)
    rows, cols = (other, along) if axis == 1 else (along, other)
    if part is not None:
        start, size, _ = rows if rows is not None else (0, total_rows, None)
        offset, size, step = _part_rows(part, size)
        rows = (start + offset, size, step)
    return ref.at[_span(rows), _span(cols)]


def _fuse(first, second):
    assert not (first.ins or first.outs or second.ins or second.outs)
    assert len(first.ios) == len(second.ios) and all(a is b for a, b in zip(first.ios, second.ios))
    cut = len(first.sems)

    def start(refs, sems):
        first.start(refs, sems[:cut])
        second.start(refs, sems[cut:])

    def finish(refs, sems):
        first.finish(refs, sems[:cut])
        second.finish(refs, sems[cut:])

    return _Job(ios=first.ios, sems=first.sems + second.sems, start=start, finish=finish)


def _gather_job(full, axes, stage, part=None, peers=(0, 1, 2)):
    n = len(full)

    def copies(refs, sems):
        send_sem, recv_sem = sems
        x, y, c, chips = _mesh_position()
        me = 2 * x + y

        def copy(w, k, shard, half, target):
            rows_cols = full[w].shape
            place = _region(refs[w], axes[w], shard, half, rows_cols[axes[w]] // N_CHIPS, rows_cols[1 - axes[w]] // 2,
                            part)
            return pltpu.make_async_remote_copy(
                src_ref=place, dst_ref=place, send_sem=send_sem.at[w, k], recv_sem=recv_sem.at[w, k],
                device_id=target, device_id_type=MESH)

        def sent(w, k):
            if stage == "ici":
                return copy(w, k, me, c, (chips[k][0], chips[k][1], c))
            return copy(w, k, 2 * chips[k][0] + chips[k][1], c, (x, y, 1 - c))

        def landed(w, k):
            return copy(w, k, 2 * chips[k][0] + chips[k][1], c if stage == "ici" else 1 - c, (x, y, 1 - c))

        return sent, landed

    def start(refs, sems):
        sent, _ = copies(refs, sems)
        for w in range(n):
            for k in peers:
                sent(w, k).start()

    def finish(refs, sems):
        sent, landed = copies(refs, sems)
        for w in range(n):
            for k in peers:
                landed(w, k).wait_recv()
                sent(w, k).wait_send()

    return _Job(ios=full, sems=[pltpu.SemaphoreType.DMA((n, 3))] * 2, start=start, finish=finish)


def _exchange_job(grads, axes):
    n = len(grads)

    def half_shape(w):
        return tuple(d // 2 if a != axes[w] else d for a, d in enumerate(grads[w].shape))

    def copy(refs, sems, w):
        x, y, c, _ = _mesh_position()
        return pltpu.make_async_remote_copy(
            src_ref=_region(refs[w], axes[w], None, 1 - c, 0, half_shape(w)[1 - axes[w]]), dst_ref=refs[n + w],
            send_sem=sems[0].at[w], recv_sem=sems[1].at[w], device_id=(x, y, 1 - c), device_id_type=MESH)

    def start(refs, sems):
        for w in range(n):
            copy(refs, sems, w).start()

    def finish(refs, sems):
        for w in range(n):
            copy(refs, sems, w).wait()

    return _Job(ins=grads, outs=[jax.ShapeDtypeStruct(half_shape(w), F32) for w in range(n)],
                sems=[pltpu.SemaphoreType.DMA((n,))] * 2, start=start, finish=finish)


def _half_block_spec(axis, block, half_blocks, use_half):
    if axis == 1:
        if use_half:
            return pl.BlockSpec(block, lambda i, pos: (pos[0] * half_blocks + i, 0))
        return pl.BlockSpec(block, lambda i, pos: (i, 0))
    if use_half:
        return pl.BlockSpec(block, lambda i, pos: (i, pos[0]))
    return pl.BlockSpec(block, lambda i, pos: (i, 0))


def _add_halves(grad, received, axis, pos, *, name):
    rows, cols = received.shape
    tr = _row_block(rows, cols)
    nb = rows // tr

    def body(pos_ref, g_ref, r_ref, o_ref):
        o_ref[...] = (g_ref[...] + r_ref[...]).astype(BF16)

    blk = (tr, cols)
    return pl.pallas_call(
        body, name=name, out_shape=jax.ShapeDtypeStruct((rows, cols), BF16),
        grid_spec=pltpu.PrefetchScalarGridSpec(
            num_scalar_prefetch=1, grid=(nb,),
            in_specs=[_half_block_spec(axis, blk, nb, True), _half_block_spec(axis, blk, nb, False)],
            out_specs=_half_block_spec(axis, blk, nb, False)),
        compiler_params=_params(("parallel",)),
    )(pos, grad, received)


def _send_sums_job(sums, axes, part=None, landing=None):
    n = len(sums)

    def part_shape(w):
        return tuple(d // N_CHIPS if a == axes[w] else d for a, d in enumerate(sums[w].shape))

    def copy(refs, sems, w, k):
        x, y, c, chips = _mesh_position()
        shard = 2 * chips[k][0] + chips[k][1]
        rows = part_shape(w)[0]
        dst = refs[n + w].at[k]
        if part is not None:
            offset, size, _ = _part_rows(part, rows)
            dst = refs[n + w].at[k, pl.ds(offset, size), :]
        return pltpu.make_async_remote_copy(
            src_ref=_region(refs[w], axes[w], shard, None, part_shape(w)[axes[w]], 0, part, rows), dst_ref=dst,
            send_sem=sems[0].at[w, k], recv_sem=sems[1].at[w, k],
            device_id=(chips[k][0], chips[k][1], c), device_id_type=MESH)

    def start(refs, sems):
        for w in range(n):
            for k in range(3):
                copy(refs, sems, w, k).start()

    def finish(refs, sems):
        for w in range(n):
            for k in range(3):
                copy(refs, sems, w, k).wait()

    sems = [pltpu.SemaphoreType.DMA((n, 3))] * 2
    if landing is not None:
        return _Job(ins=sums, ios=landing, sems=sems, start=start, finish=finish)
    return _Job(ins=sums, outs=[jax.ShapeDtypeStruct((3,) + part_shape(w), BF16) for w in range(n)],
                sems=sems, start=start, finish=finish)


def _sum_chip_parts(grad, received, parts, axis, pos, *, name):
    _, rows, cols = parts.shape
    tr = _row_block(rows, cols)
    nb = rows // tr
    blk = (tr, cols)

    def body(pos_ref, g_ref, r_ref, p_ref, o_ref):
        total = g_ref[...] + r_ref[...]
        for k in range(3):
            total = total + p_ref[k].astype(F32)
        o_ref[...] = total

    if axis == 1:
        g_spec = pl.BlockSpec(blk, lambda i, pos: (pos[0] * nb + i, pos[1]))
        r_spec = pl.BlockSpec(blk, lambda i, pos: (i, pos[1]))
        o_spec = pl.BlockSpec(blk, lambda i, pos: (pos[0] * nb + i, 0))
        shard_shape = (2 * rows, cols)
    else:
        g_spec = pl.BlockSpec(blk, lambda i, pos: (pos[1] * nb + i, pos[0]))
        r_spec = pl.BlockSpec(blk, lambda i, pos: (pos[1] * nb + i, 0))
        o_spec = pl.BlockSpec(blk, lambda i, pos: (i, pos[0]))
        shard_shape = (rows, 2 * cols)
    return pl.pallas_call(
        body, name=name, out_shape=jax.ShapeDtypeStruct(shard_shape, F32),
        grid_spec=pltpu.PrefetchScalarGridSpec(
            num_scalar_prefetch=1, grid=(nb,),
            in_specs=[g_spec, r_spec, pl.BlockSpec((3,) + blk, lambda i, pos: (0, i, 0))],
            out_specs=o_spec),
        compiler_params=_params(("parallel",)),
    )(pos, grad, received, parts)


def _join_job(shards, axes):
    n = len(shards)

    def copy(refs, sems, w, other):
        x, y, c, _ = _mesh_position()
        place = _region(refs[w], axes[w], None, 1 - c if other else c, 0, shards[w].shape[1 - axes[w]] // 2)
        return pltpu.make_async_remote_copy(
            src_ref=place, dst_ref=place, send_sem=sems[0].at[w], recv_sem=sems[1].at[w],
            device_id=(x, y, 1 - c), device_id_type=MESH)

    def start(refs, sems):
        for w in range(n):
            copy(refs, sems, w, False).start()

    def finish(refs, sems):
        for w in range(n):
            copy(refs, sems, w, True).wait_recv()
            copy(refs, sems, w, False).wait_send()

    return _Job(ios=shards, sems=[pltpu.SemaphoreType.DMA((n,))] * 2, start=start, finish=finish)


def _all_reduce_small(vec, after=()):
    rows, cols = vec.shape

    def body(v_ref, *rest):
        o_ref, land_ref, send_sem, recv_sem = rest[len(after):]
        x, y, c, _ = _mesh_position()
        me = 4 * x + 2 * y + c
        land_ref[me] = v_ref[...]
        copies = []
        for k in range(1, 8):
            px, py, pc = x ^ (k >> 2), y ^ ((k >> 1) & 1), c ^ (k & 1)
            copies.append(pltpu.make_async_remote_copy(
                src_ref=v_ref, dst_ref=land_ref.at[me], send_sem=send_sem.at[k], recv_sem=recv_sem.at[k],
                device_id=(px, py, pc), device_id_type=MESH))
        for cp in copies:
            cp.start()
        for k in range(1, 8):
            peer = me ^ k
            pltpu.make_async_remote_copy(
                src_ref=v_ref, dst_ref=land_ref.at[peer], send_sem=send_sem.at[k], recv_sem=recv_sem.at[k],
                device_id=(x, y, c), device_id_type=MESH).wait_recv()
        for cp in copies:
            cp.wait_send()
        total = land_ref[0]
        for k in range(1, 8):
            total = total + land_ref[k]
        o_ref[...] = total

    vmem = pl.BlockSpec(memory_space=pltpu.VMEM)
    return pl.pallas_call(
        body, name="all_reduce_small", in_specs=[vmem] + [pl.BlockSpec(memory_space=pl.ANY)] * len(after),
        out_specs=vmem, out_shape=jax.ShapeDtypeStruct((rows, cols), F32),
        scratch_shapes=[pltpu.VMEM((8, rows, cols), F32), pltpu.SemaphoreType.DMA((8,)), pltpu.SemaphoreType.DMA((8,))],
    )(vec, *after)


def _adamw(w, g, m, v, *, name, after=()):
    rows, cols = w.shape
    tr = _row_block(rows, cols) if rows % 8 == 0 else rows
    bc1 = 1.0 - ADAM_B1 ** ADAM_STEP
    bc2 = 1.0 - ADAM_B2 ** ADAM_STEP

    def body(w_ref, g_ref, m_ref, v_ref, *rest):
        go_ref, d_ref, mo_ref, vo_ref = rest[len(after):]
        gv = g_ref[...]
        go_ref[...] = gv
        mn = ADAM_B1 * m_ref[...] + (1.0 - ADAM_B1) * gv
        vn = ADAM_B2 * v_ref[...] + (1.0 - ADAM_B2) * (gv * gv)
        mo_ref[...] = mn
        vo_ref[...] = vn
        d_ref[...] = -ADAM_LR * ((mn / bc1) / (jnp.sqrt(vn / bc2) + ADAM_EPS) + ADAM_WD * w_ref[...])

    blk = pl.BlockSpec((tr, cols), lambda i: (i, 0))
    shape = jax.ShapeDtypeStruct((rows, cols), F32)
    return pl.pallas_call(
        body, name=name, grid=(rows // tr,), in_specs=[blk] * 4 + [pl.BlockSpec(memory_space=pl.ANY)] * len(after),
        out_specs=[blk] * 4, out_shape=[shape] * 4, compiler_params=_params(("parallel",)),
    )(w, g, m, v, *after)


def _to_bf16_in_place(w, axis, pos, *, name, after=None):
    rows, cols = w.shape
    tr = _row_block(rows, cols)
    nb = rows // tr

    def body(pos_ref, w_ref, *rest):
        rest[-1][...] = w_ref[...].astype(BF16)

    if axis == 1:
        o_spec = pl.BlockSpec((tr, cols), lambda i, pos: (i, pos[1]))
        full_shape = (rows, N_CHIPS * cols)
    else:
        o_spec = pl.BlockSpec((tr, cols), lambda i, pos: (pos[1] * nb + i, 0))
        full_shape = (N_CHIPS * rows, cols)
    in_specs = [pl.BlockSpec((tr, cols), lambda i, pos: (i, 0))]
    operands = [pos, w]
    if after is not None:
        in_specs.append(pl.BlockSpec(after.shape, lambda i, pos: (0, 0)))
        operands.append(after)
    return pl.pallas_call(
        body, name=name, out_shape=jax.ShapeDtypeStruct(full_shape, BF16),
        grid_spec=pltpu.PrefetchScalarGridSpec(num_scalar_prefetch=1, grid=(nb,), in_specs=in_specs, out_specs=o_spec),
        compiler_params=_params(("parallel",)),
    )(*operands)


def _split_gather_start(full, axis):
    rows_cols = full.shape

    def body(buf_ref, *rest):
        sems = rest[:6]
        token_ref = rest[7]
        x, y, c, chips = _mesh_position()
        place = _region(buf_ref, axis, 2 * x + y, c, rows_cols[axis] // N_CHIPS, rows_cols[1 - axis] // 2)
        for k in range(3):
            pltpu.make_async_remote_copy(
                src_ref=place, dst_ref=place, send_sem=sems[k], recv_sem=sems[3 + k],
                device_id=(chips[k][0], chips[k][1], c), device_id_type=MESH).start()
        token_ref[...] = jnp.zeros_like(token_ref)

    hbm = pl.BlockSpec(memory_space=pltpu.HBM)
    sem = pl.BlockSpec(memory_space=pltpu.SEMAPHORE)
    res = pl.pallas_call(
        body, name="all_gather_w_in_start",
        out_shape=(*[pltpu.SemaphoreType.DMA(())] * 6, pltpu.HBM(full.shape, full.dtype),
                   jax.ShapeDtypeStruct((8, HEAD_DIM), F32)),
        in_specs=(hbm,), out_specs=(*[sem] * 6, hbm, pl.BlockSpec(memory_space=pltpu.VMEM)),
        input_output_aliases={0: 6},
        compiler_params=pltpu.CompilerParams(has_side_effects=pltpu.SideEffectType.DATAFLOW_SIDE_EFFECTING),
    )(pltpu.with_memory_space_constraint(full, pltpu.HBM))
    return list(res[:6]), res[6], res[7]


def _split_gather_wait(sems, full, axis, peer, after):
    rows_cols = full.shape

    def body(buf_ref, send_sem, recv_sem, *rest):
        x, y, c, chips = _mesh_position()

        def copy(shard):
            place = _region(buf_ref, axis, shard, c, rows_cols[axis] // N_CHIPS, rows_cols[1 - axis] // 2)
            return pltpu.make_async_remote_copy(
                src_ref=place, dst_ref=place, send_sem=send_sem, recv_sem=recv_sem,
                device_id=(chips[peer][0], chips[peer][1], c), device_id_type=MESH)

        copy(2 * x + y).wait_send()
        copy(2 * chips[peer][0] + chips[peer][1]).wait_recv()

    hbm = pl.BlockSpec(memory_space=pltpu.HBM)
    sem = pl.BlockSpec(memory_space=pltpu.SEMAPHORE)
    return pl.pallas_call(
        body, name="all_gather_w_in_wait_%d" % peer, out_shape=pltpu.HBM(full.shape, full.dtype),
        in_specs=(hbm, sem, sem, *[pl.BlockSpec(memory_space=pl.ANY)] * len(after)), out_specs=hbm,
        input_output_aliases={0: 0},
        compiler_params=pltpu.CompilerParams(has_side_effects=pltpu.SideEffectType.DATAFLOW_SIDE_EFFECTING),
    )(full, sems[peer], sems[3 + peer], *after)


def _in_proj_part(n1, w_in, proj, shard_ids, which, *, out_cols):
    m, kdim = n1.shape
    cols = out_cols // N_CHIPS
    tm = _tile(m, 1024)

    def body(ids_ref, a_ref, b_ref, *rest):
        rest[-1][...] = _dot(a_ref[...], b_ref[...]).astype(BF16)

    in_specs = [pl.BlockSpec((tm, kdim), lambda i, ids: (i, 0)),
                pl.BlockSpec((kdim, cols), lambda i, ids: (0, ids[which]))]
    operands = [shard_ids, n1, w_in]
    if proj is not None:
        in_specs.append(pl.BlockSpec(memory_space=pl.ANY))
        operands.append(proj)
    return pl.pallas_call(
        body, name="in_proj_%d" % which, out_shape=jax.ShapeDtypeStruct((m, out_cols), BF16),
        grid_spec=pltpu.PrefetchScalarGridSpec(
            num_scalar_prefetch=1, grid=(m // tm,), in_specs=in_specs,
            out_specs=pl.BlockSpec((tm, cols), lambda i, ids: (i, ids[which]))),
        input_output_aliases={3: 0} if proj is not None else {},
        compiler_params=_params(("parallel",)),
    )(*operands)


BIG = ("w_in", "w_out", "w_gate", "w_up", "w_down")
BIG_AXIS = dict(w_in=1, w_out=0, w_gate=1, w_up=1, w_down=0)
SMALL = ("norm_mix_w", "ret_decay_fwd", "ret_decay_bwd", "ret_norm_w", "norm_ffn_w", "norm_final_w")
ALL_WEIGHTS = ("norm_mix_w", "w_in", "ret_decay_fwd", "ret_decay_bwd", "ret_norm_w", "w_out", "norm_ffn_w",
               "w_gate", "w_up", "w_down", "norm_final_w")
SMALL_ROW = 128 * 8


def _pack_small(small):
    pieces = [jnp.reshape(small["loss"], (1,))] + [jnp.reshape(small[k], (-1,)) for k in SMALL]
    rows = []
    for p in pieces:
        pad = -p.shape[0] % (8 * SMALL_ROW)
        rows.append(jnp.reshape(jnp.pad(p, (0, pad)), (-1, SMALL_ROW)))
    return jnp.concatenate(rows, axis=0)


def _unpack_small(block, like):
    out, row = {}, 0
    for k in ("loss",) + SMALL:
        size = 1 if k == "loss" else like[k].size
        nrows = -(-size // (8 * SMALL_ROW)) * 8
        out[k] = jnp.reshape(block[row:row + nrows], (-1,))[:size]
        row += nrows
    return out


def kernel(x, norm_mix_w, w_in, ret_decay_fwd, ret_decay_bwd, ret_norm_w, w_out, norm_ffn_w, w_gate, w_up, w_down, norm_final_w, loss_target, m_norm_mix_w, m_w_in, m_ret_decay_fwd, m_ret_decay_bwd, m_ret_norm_w, m_w_out, m_norm_ffn_w, m_w_gate, m_w_up, m_w_down, m_norm_final_w, v_norm_mix_w, v_w_in, v_ret_decay_fwd, v_ret_decay_bwd, v_ret_norm_w, v_w_out, v_norm_ffn_w, v_w_gate, v_w_up, v_w_down, v_norm_final_w):
    weights = dict(norm_mix_w=norm_mix_w, w_in=w_in, ret_decay_fwd=ret_decay_fwd, ret_decay_bwd=ret_decay_bwd,
                   ret_norm_w=ret_norm_w, w_out=w_out, norm_ffn_w=norm_ffn_w, w_gate=w_gate, w_up=w_up,
                   w_down=w_down, norm_final_w=norm_final_w)
    m_in = dict(norm_mix_w=m_norm_mix_w, w_in=m_w_in, ret_decay_fwd=m_ret_decay_fwd, ret_decay_bwd=m_ret_decay_bwd,
                ret_norm_w=m_ret_norm_w, w_out=m_w_out, norm_ffn_w=m_norm_ffn_w, w_gate=m_w_gate, w_up=m_w_up,
                w_down=m_w_down, norm_final_w=m_norm_final_w)
    v_in = dict(norm_mix_w=v_norm_mix_w, w_in=v_w_in, ret_decay_fwd=v_ret_decay_fwd, ret_decay_bwd=v_ret_decay_bwd,
                ret_norm_w=v_ret_norm_w, w_out=v_w_out, norm_ffn_w=v_norm_ffn_w, w_gate=v_w_gate, w_up=v_w_up,
                w_down=v_w_down, norm_final_w=v_norm_final_w)
    pos = jnp.stack([lax.axis_index("c"), 2 * lax.axis_index("x") + lax.axis_index("y")]).astype(jnp.int32)

    own = {"w_in": _to_bf16_in_place(weights["w_in"][0], BIG_AXIS["w_in"], pos, name="cast_w_in")}
    w_in_started = _split_gather_start(own["w_in"], BIG_AXIS["w_in"])
    own["w_gate"] = _to_bf16_in_place(weights["w_gate"][0], BIG_AXIS["w_gate"], pos, name="cast_w_gate",
                                      after=w_in_started[2])
    w_gate_started = _split_start(_gather_job([own["w_gate"]], [BIG_AXIS["w_gate"]], "ici"),
                                  name="all_gather_w_gate_start")
    for k in BIG:
        if k not in own:
            own[k] = _to_bf16_in_place(weights[k][0], BIG_AXIS[k], pos, name="cast_" + k,
                                       after=w_gate_started["token"])
    cx, cy = lax.axis_index("x"), lax.axis_index("y")
    shard_ids = jnp.stack([2 * cx + cy, 2 * (1 - cx) + cy, 2 * cx + 1 - cy, 2 * (1 - cx) + 1 - cy]).astype(jnp.int32)

    dx, grad_w, small, w_in_pending = _step(
        x[0], loss_target[0], norm_mix_w, ret_decay_fwd[0], ret_decay_bwd[0], ret_norm_w, norm_ffn_w,
        norm_final_w[None, :], own, w_in_started, w_gate_started, shard_ids, pos)

    delta, new_m, new_v = {}, {}, {}

    def update(k, after):
        shape = weights[k].shape
        as2d = (lambda t: jnp.reshape(t, (-1, shape[-1])))
        grad_w[k], delta[k], new_m[k], new_v[k] = (jnp.reshape(t, shape) for t in _adamw(
            as2d(weights[k]), as2d(grad_w[k]), as2d(m_in[k]), as2d(v_in[k]), name="adamw_" + k, after=after))

    others = [k for k in BIG if k != "w_in"]
    for k in others:
        update(k, [w_in_pending["sending"]["token"]])
    _, parts = _split_wait(w_in_pending["sending"], [dx] + [delta[k] for k in others], name="grad_send_w_in_wait")

    half = _sum_chip_parts(w_in_pending["grad"], w_in_pending["received"], parts, BIG_AXIS["w_in"], pos,
                           name="grad_sum_parts_w_in")
    joining = _split_start(_join_job([half], [BIG_AXIS["w_in"]]), name="grad_join_w_in_start")

    like = {k: weights[k] for k in SMALL}
    reduced = _unpack_small(_all_reduce_small(_pack_small(small), after=[joining["token"]]), like)
    loss = reduced["loss"][0]
    for k in SMALL:
        grad_w[k] = jnp.reshape(reduced[k], (1, -1))
        update(k, [])
    (grad_w["w_in"],) = _split_wait(joining, [delta[k] for k in SMALL], name="grad_join_w_in_wait")
    update("w_in", [])

    return (loss, dx[None], *[grad_w[k] for k in ALL_WEIGHTS], *[delta[k] for k in ALL_WEIGHTS],
            *[new_m[k] for k in ALL_WEIGHTS], *[new_v[k] for k in ALL_WEIGHTS])
```

```python
import functools
import math

import numpy as np
import jax
import jax.numpy as jnp
from jax import lax
from jax.experimental import pallas as pl
from jax.experimental.pallas import tpu as pltpu

F32 = jnp.float32
BF16 = jnp.bfloat16
MESH = pl.DeviceIdType.MESH

HEAD_DIM = 128
RET_CHUNK = 128
RET_UNROLL = 8
EPS = 1e-6
DILATED_PATTERNS = ((128, 1), (512, 4), (2048, 16))
ATT_BLOCK = 256
ATT_REACH = max(w // 2 for w, _ in DILATED_PATTERNS)
ATT_NEAR = ATT_BLOCK
ATT_CLASSES = DILATED_PATTERNS[-1][1]
assert all(w // 2 <= ATT_NEAR for w, _ in DILATED_PATTERNS[:-1])
ATT_KB = -(-ATT_NEAR // ATT_BLOCK)
ATT_WINDOW = 2 * ATT_KB + 1
ATT_FAR_GROUP = 8
ATT_NEAR_GROUP = 4
MASKED = -1e30
ROW_MAX_INIT = -1e29
N_CHIPS = 4
VMEM_LIMIT_BYTES = 56 * 1024 * 1024
ELEM_BLOCK_BYTES = 2 * 1024 * 1024
WEIGHT_GRAD_GROUP = 4

ADAM_LR = 0.001
ADAM_B1 = 0.9
ADAM_B2 = 0.999
ADAM_EPS = 1e-08
ADAM_WD = 0.01
ADAM_STEP = 10


def _params(sem=None):
    return pltpu.CompilerParams(dimension_semantics=sem, vmem_limit_bytes=VMEM_LIMIT_BYTES)


def _sigmoid(x):
    return 0.5 * jnp.tanh(0.5 * x) + 0.5


class _Job:
    def __init__(self, *, ins=(), ios=(), outs=(), sems=(), start, finish):
        self.ins, self.ios, self.outs, self.sems = list(ins), list(ios), list(outs), list(sems)
        self.start, self.finish = start, finish

    def results(self):
        return [jax.ShapeDtypeStruct(a.shape, a.dtype) for a in self.ios] + self.outs


def _call(body, *, name, grid, in_specs, out_specs, out_shape, operands, scratch_shapes=(), semantics=None, jobs=(),
          after=(), updates=None):
    in_specs, out_specs, out_shape = list(in_specs), list(out_specs), list(out_shape)
    scratch_shapes = list(scratch_shapes)
    if not jobs:
        n_real = len(in_specs)

        def ordered(*refs):
            body(*refs[:n_real], *refs[n_real + len(after):])

        outs = pl.pallas_call(
            ordered if after else body, name=name, grid=grid,
            in_specs=in_specs + [pl.BlockSpec(memory_space=pl.ANY)] * len(after), out_specs=out_specs,
            out_shape=out_shape, scratch_shapes=scratch_shapes, input_output_aliases=dict(updates or {}),
            compiler_params=_params(semantics))(*operands, *after)
        return outs, []
    n_in, n_out, n_scratch = len(in_specs), len(out_specs), len(scratch_shapes)
    extra_in, extra_out, sems, aliases = [], [], [], dict(updates or {})
    for job in jobs:
        extra_in += job.ins
        for t in range(len(job.ios)):
            aliases[n_in + len(extra_in) + t] = n_out + len(extra_out) + t
        extra_in += job.ios
        extra_out += job.results()
        sems += job.sems

    def carried(*refs):
        x_in = refs[n_in:n_in + len(extra_in)]
        first_out = n_in + len(extra_in) + len(after)
        x_out = refs[first_out + n_out:first_out + n_out + len(extra_out)]
        x_sem = refs[len(refs) - len(sems):]
        views, i_in, i_out, i_sem = [], 0, 0, 0
        for job in jobs:
            data = list(x_in[i_in:i_in + len(job.ins)]) + list(x_out[i_out:i_out + len(job.results())])
            views.append((data, x_sem[i_sem:i_sem + len(job.sems)]))
            i_in += len(job.ins) + len(job.ios)
            i_out += len(job.results())
            i_sem += len(job.sems)
        steps = [pl.program_id(d) for d in range(len(grid))]

        @pl.when(functools.reduce(jnp.logical_and, [s == 0 for s in steps]))
        def _():
            for job, (data, sem) in zip(jobs, views):
                job.start(data, sem)

        body(*refs[:n_in], *refs[first_out:first_out + n_out],
             *refs[len(refs) - len(sems) - n_scratch:len(refs) - len(sems)])

        @pl.when(functools.reduce(jnp.logical_and, [s == g - 1 for s, g in zip(steps, grid)]))
        def _():
            for job, (data, sem) in zip(jobs, views):
                job.finish(data, sem)

    hbm = pl.BlockSpec(memory_space=pl.ANY)
    res = pl.pallas_call(
        carried, name=name, grid=grid, in_specs=in_specs + [hbm] * (len(extra_in) + len(after)),
        out_specs=out_specs + [hbm] * len(extra_out), out_shape=out_shape + extra_out,
        input_output_aliases=aliases, scratch_shapes=scratch_shapes + sems,
        compiler_params=_params(("arbitrary",) * len(grid)),
    )(*operands, *extra_in, *after)
    carried_results, at = [], n_out
    for job in jobs:
        carried_results.append(list(res[at:at + len(job.results())]))
        at += len(job.results())
    return list(res[:n_out]), carried_results


def _run_jobs(jobs, *, name):
    first = jobs[0]
    n_in, n_io = len(first.ins), len(first.ios)
    out_shape = first.results()
    n_sems = [len(job.sems) for job in jobs]

    def body(*refs):
        data = list(refs[:n_in]) + list(refs[n_in + n_io:n_in + n_io + len(out_shape)])
        at = n_in + n_io + len(out_shape)
        for job, ns in zip(jobs, n_sems):
            job.start(data, refs[at:at + ns])
            job.finish(data, refs[at:at + ns])
            at += ns

    hbm = pl.BlockSpec(memory_space=pl.ANY)
    return pl.pallas_call(
        body, name=name, in_specs=[hbm] * (n_in + n_io), out_specs=[hbm] * len(out_shape), out_shape=out_shape,
        input_output_aliases={n_in + t: t for t in range(n_io)},
        scratch_shapes=[s for job in jobs for s in job.sems],
    )(*first.ins, *first.ios)


class _SemaphoreGrid:
    def __init__(self, refs, shape):
        self.refs, self.shape = list(refs), tuple(shape)

    @property
    def at(self):
        return self

    def __getitem__(self, index):
        index = index if isinstance(index, tuple) else (index,)
        flat = 0
        for i, extent in zip(index, self.shape):
            flat = flat * extent + i
        return self.refs[flat]


def _semaphore_grids(job, refs):
    grids, at = [], 0
    for sem in job.sems:
        count = math.prod(sem.shape)
        grids.append(_SemaphoreGrid(refs[at:at + count], sem.shape))
        at += count
    return grids


def _split_start(job, *, name):
    arrays = job.ins + job.ios + [lax.empty(s.shape, s.dtype) for s in job.outs]
    n, ns = len(arrays), sum(math.prod(sem.shape) for sem in job.sems)

    def body(*refs):
        job.start(list(refs[:n]), _semaphore_grids(job, refs[n:n + ns]))
        refs[-1][...] = jnp.zeros_like(refs[-1])

    hbm = pl.BlockSpec(memory_space=pltpu.HBM)
    res = pl.pallas_call(
        body, name=name,
        out_shape=(*[pltpu.SemaphoreType.DMA(())] * ns, *[pltpu.HBM(a.shape, a.dtype) for a in arrays],
                   jax.ShapeDtypeStruct((8, HEAD_DIM), F32)),
        in_specs=[hbm] * n,
        out_specs=(*[pl.BlockSpec(memory_space=pltpu.SEMAPHORE)] * ns, *[hbm] * n,
                   pl.BlockSpec(memory_space=pltpu.VMEM)),
        input_output_aliases={t: ns + t for t in range(n)},
        compiler_params=pltpu.CompilerParams(has_side_effects=pltpu.SideEffectType.DATAFLOW_SIDE_EFFECTING),
    )(*[pltpu.with_memory_space_constraint(a, pltpu.HBM) for a in arrays])
    return dict(job=job, sems=list(res[:ns]), arrays=list(res[ns:ns + n]), token=res[-1])


def _split_wait(started, after, *, name):
    job, arrays, sems = started["job"], started["arrays"], started["sems"]
    n, ns = len(arrays), len(sems)

    def body(*refs):
        job.finish(list(refs[:n]), _semaphore_grids(job, refs[n:n + ns]))

    hbm = pl.BlockSpec(memory_space=pltpu.HBM)
    return pl.pallas_call(
        body, name=name, out_shape=[pltpu.HBM(a.shape, a.dtype) for a in arrays],
        in_specs=[hbm] * n + [pl.BlockSpec(memory_space=pltpu.SEMAPHORE)] * ns
        + [pl.BlockSpec(memory_space=pl.ANY)] * len(after),
        out_specs=[hbm] * n, input_output_aliases={t: t for t in range(n)},
        compiler_params=pltpu.CompilerParams(has_side_effects=pltpu.SideEffectType.DATAFLOW_SIDE_EFFECTING),
    )(*arrays, *sems, *after)


def _dot(a, b, ta=False, tb=False):
    return lax.dot_general(a, b, (((0 if ta else 1,), (1 if tb else 0,)), ((), ())),
                           preferred_element_type=F32)


def _tile(n, want):
    t = min(n, want) // 128 * 128
    while n % t:
        t -= 128
    return t


def _a_spec(ta, tm, tk):
    return pl.BlockSpec((tk, tm), lambda i, j, k: (k, i)) if ta else pl.BlockSpec((tm, tk), lambda i, j, k: (i, k))


def _b_spec(tb, tk, tn):
    return pl.BlockSpec((tn, tk), lambda i, j, k: (j, k)) if tb else pl.BlockSpec((tk, tn), lambda i, j, k: (k, j))


def _accumulate(accs, nk, products, finish):
    if nk == 1:
        finish(*products())
        return
    k = pl.program_id(2)

    @pl.when(k == 0)
    def _():
        for acc, p in zip(accs, products()):
            acc[...] = p

    if nk > 2:
        @pl.when(jnp.logical_and(k > 0, k < nk - 1))
        def _():
            for acc, p in zip(accs, products()):
                acc[...] += p

    @pl.when(k == nk - 1)
    def _():
        finish(*[acc[...] + p for acc, p in zip(accs, products())])


def _matmul(a, b, *, name, ta=False, tb=False, out_dtype=F32, residual=None, tm=1024, tn=1024, tk=2048, jobs=()):
    m, kdim = (a.shape[1], a.shape[0]) if ta else a.shape
    n = b.shape[0] if tb else b.shape[1]
    tm, tn, tk = _tile(m, tm), _tile(n, tn), _tile(kdim, tk)
    nk = kdim // tk

    def body(*refs):
        a_ref, b_ref = refs[:2]
        r_ref = refs[2] if residual is not None else None
        o_ref = refs[-1] if nk == 1 else refs[-2]

        def finish(total):
            if residual is not None:
                total = total + r_ref[...]
            o_ref[...] = total.astype(out_dtype)

        _accumulate(refs[-1:] if nk > 1 else (), nk, lambda: (_dot(a_ref[...], b_ref[...], ta, tb),), finish)

    o_spec = pl.BlockSpec((tm, tn), lambda i, j, k: (i, j))
    in_specs = [_a_spec(ta, tm, tk), _b_spec(tb, tk, tn)]
    operands = [a, b]
    if residual is not None:
        in_specs.append(o_spec)
        operands.append(residual)
    (out,), carried = _call(
        body, name=name, grid=(m // tm, n // tn, nk), in_specs=in_specs, out_specs=[o_spec],
        out_shape=[jax.ShapeDtypeStruct((m, n), out_dtype)], operands=operands,
        scratch_shapes=[pltpu.VMEM((tm, tn), F32)] * (nk > 1),
        semantics=("parallel", "parallel", "arbitrary"), jobs=jobs)
    return (out, carried) if jobs else out


def _matmul_pieces_nt(pieces, b, *, name, tm=512, tn=1024, jobs=(), after=()):
    m, kp = pieces[0].shape
    n = b.shape[0]
    tm, tn = _tile(m, tm), _tile(n, tn)
    count = len(pieces)

    def body(*refs):
        b_ref, o_ref = refs[count], refs[count + 1]
        total = _dot(refs[0][...], b_ref[:, pl.ds(0, kp)], tb=True)
        for p in range(1, count):
            total = total + _dot(refs[p][...], b_ref[:, pl.ds(p * kp, kp)], tb=True)
        o_ref[...] = total

    piece = pl.BlockSpec((tm, kp), lambda j, i: (i, 0))
    (out,), carried = _call(
        body, name=name, grid=(n // tn, m // tm),
        in_specs=[piece] * count + [pl.BlockSpec((tn, count * kp), lambda j, i: (j, 0))],
        out_specs=[pl.BlockSpec((tm, tn), lambda j, i: (i, j))],
        out_shape=[jax.ShapeDtypeStruct((m, n), F32)], operands=[*pieces, b],
        semantics=("parallel", "parallel"), jobs=jobs, after=after)
    return (out, carried) if jobs else out


def _weight_grad_pieces(a, pieces, *, name):
    tokens, m = a.shape
    np_ = pieces[0].shape[1]
    tm = 1024 if m % 1024 == 0 else _tile(m, 1408)
    tn = _tile(np_, 512)
    nb = np_ // tn
    out = None
    for first in range(0, len(pieces), WEIGHT_GRAD_GROUP):
        group = pieces[first:first + WEIGHT_GRAD_GROUP]

        def body(*refs, count=len(group)):
            t_now = pl.program_id(1) // nb
            for t in range(count):
                @pl.when(t_now == t)
                def _(t=t):
                    refs[-1][...] = _dot(refs[0][...], refs[1 + t][...], ta=True)

        def piece_spec(t):
            return pl.BlockSpec((tokens, tn), lambda i, j: (0, jnp.clip(j - t * nb, 0, nb - 1)))

        in_specs = [pl.BlockSpec((tokens, tm), lambda i, j: (0, i))] + [piece_spec(t) for t in range(len(group))]
        operands = [a, *group]
        if out is not None:
            in_specs.append(pl.BlockSpec(memory_space=pl.ANY))
            operands.append(out)
        out = pl.pallas_call(
            body, name="%s_%d" % (name, first), grid=(m // tm, nb * len(group)), in_specs=in_specs,
            out_specs=pl.BlockSpec((tm, tn), lambda i, j, first=first: (i, first * nb + j)),
            out_shape=jax.ShapeDtypeStruct((m, len(pieces) * np_), F32),
            input_output_aliases={len(operands) - 1: 0} if out is not None else {},
            compiler_params=_params(("parallel", "arbitrary")),
        )(*operands)
    return out


def _weight_grad(a, g, *, name, jobs=()):
    tokens, m = a.shape
    tm = 1024 if m % 1024 == 0 else _tile(m, 1408)
    return _matmul(a, g, name=name, ta=True, tm=tm, tn=512, tk=tokens, jobs=jobs)


def _swiglu_fwd(n2, w_gate, w_up, *, tm=1024, tn=512, tk=2048, jobs=()):
    m, kdim = n2.shape
    n = w_gate.shape[1]
    tm, tn, tk = _tile(m, tm), _tile(n, tn), _tile(kdim, tk)
    nk = kdim // tk

    def body(a_ref, g_ref, u_ref, gate_ref, up_ref, act_ref, *acc):
        def products():
            a = a_ref[...]
            return _dot(a, g_ref[...]), _dot(a, u_ref[...])

        def finish(g, u):
            gate_ref[...] = g.astype(BF16)
            up_ref[...] = u.astype(BF16)
            act_ref[...] = (g * _sigmoid(g) * u).astype(BF16)

        _accumulate(acc, nk, products, finish)

    o_spec = pl.BlockSpec((tm, tn), lambda i, j, k: (i, j))
    o_shape = jax.ShapeDtypeStruct((m, n), BF16)
    return _call(
        body, name="swiglu_fwd", grid=(m // tm, n // tn, nk),
        in_specs=[_a_spec(False, tm, tk), _b_spec(False, tk, tn), _b_spec(False, tk, tn)],
        out_specs=[o_spec] * 3, out_shape=[o_shape] * 3, operands=[n2, w_gate, w_up],
        scratch_shapes=[pltpu.VMEM((tm, tn), F32)] * (2 * (nk > 1)),
        semantics=("parallel", "parallel", "arbitrary"), jobs=jobs)


def _swiglu_bwd_act(dh2, w_down, gate, up, *, tm=1024, tn=512, tk=2048):
    m, kdim = dh2.shape
    n = w_down.shape[0]
    tm, tn, tk = _tile(m, tm), _tile(n, tn), _tile(kdim, tk)
    nk = kdim // tk

    sub = _tile(tn, 256)

    def body(a_ref, b_ref, gate_ref, up_ref, dgate_ref, dup_ref, *acc):
        def finish(dact, cols=slice(None)):
            g = gate_ref[:, cols].astype(F32)
            u = up_ref[:, cols].astype(F32)
            sg = _sigmoid(g)
            dup_ref[:, cols] = (dact * g * sg).astype(BF16)
            dgate_ref[:, cols] = (dact * u * sg * (1.0 + g * (1.0 - sg))).astype(BF16)

        if nk == 1:
            a = a_ref[...]
            for c in range(tn // sub):
                cols = pl.ds(c * sub, sub)
                finish(_dot(a, b_ref[cols, :], tb=True), cols)
        else:
            _accumulate(acc, nk, lambda: (_dot(a_ref[...], b_ref[...], tb=True),), finish)

    o_spec = pl.BlockSpec((tm, tn), lambda i, j, k: (i, j))
    o_shape = jax.ShapeDtypeStruct((m, n), BF16)
    return pl.pallas_call(
        body, name="swiglu_bwd_act", grid=(m // tm, n // tn, nk),
        in_specs=[_a_spec(False, tm, tk), _b_spec(True, tk, tn), o_spec, o_spec],
        out_specs=[o_spec] * 2, out_shape=[o_shape] * 2,
        scratch_shapes=[pltpu.VMEM((tm, tn), F32)] * (nk > 1),
        compiler_params=_params(("parallel", "parallel", "arbitrary")),
    )(dh2, w_down, gate, up)


def _swiglu_bwd_in(dgate, dup, w_gate, w_up, *, tm=1024, tn=1024, tk=1408, jobs=()):
    m, kdim = dgate.shape
    n = w_gate.shape[0]
    tm, tn, tk = _tile(m, tm), _tile(n, tn), _tile(kdim, tk)
    nk = kdim // tk

    def body(a1_ref, a2_ref, b1_ref, b2_ref, o_ref, *acc):
        def product():
            return (_dot(a1_ref[...], b1_ref[...], tb=True) + _dot(a2_ref[...], b2_ref[...], tb=True),)

        def finish(total):
            o_ref[...] = total

        _accumulate(acc, nk, product, finish)

    a_spec, b_spec = _a_spec(False, tm, tk), _b_spec(True, tk, tn)
    (out,), carried = _call(
        body, name="swiglu_bwd_in", grid=(m // tm, n // tn, nk),
        in_specs=[a_spec, a_spec, b_spec, b_spec],
        out_specs=[pl.BlockSpec((tm, tn), lambda i, j, k: (i, j))],
        out_shape=[jax.ShapeDtypeStruct((m, n), F32)], operands=[dgate, dup, w_gate, w_up],
        scratch_shapes=[pltpu.VMEM((tm, tn), F32)] * (nk > 1),
        semantics=("parallel", "parallel", "arbitrary"), jobs=jobs)
    return out, carried


def _row_block(rows, cols):
    tr = min(rows, max(16, ELEM_BLOCK_BYTES // (4 * cols) // 16 * 16))
    while rows % tr:
        tr -= 16
    return tr


def _rmsnorm_fwd(x, g, *, name, after=None):
    s, d = x.shape
    tr = _row_block(s, d)

    def body(x_ref, g_ref, *rest):
        xv = x_ref[...]
        r = lax.rsqrt(jnp.mean(xv * xv, axis=-1, keepdims=True) + EPS)
        rest[-1][...] = (xv * r * g_ref[...]).astype(BF16)

    row = pl.BlockSpec((tr, d), lambda i: (i, 0))
    in_specs = [row, pl.BlockSpec((1, d), lambda i: (0, 0))]
    operands = [x, g]
    if after is not None:
        in_specs.append(pl.BlockSpec(after.shape, lambda i: (0, 0)))
        operands.append(after)
    return pl.pallas_call(
        body, name=name, grid=(s // tr,), in_specs=in_specs,
        out_specs=row, out_shape=jax.ShapeDtypeStruct((s, d), BF16),
        compiler_params=_params(("parallel",)),
    )(*operands)


def _rmsnorm_bwd_rows(xv, gv, dy):
    r = lax.rsqrt(jnp.mean(xv * xv, axis=-1, keepdims=True) + EPS)
    xhat = xv * r
    dxh = dy * gv
    dx = r * (dxh - xhat * jnp.mean(dxh * xhat, axis=-1, keepdims=True))
    return dx, dy * xhat


def _rmsnorm_bwd(dn, x, g, skip, *, name, after=()):
    s, d = x.shape
    tr = _row_block(s, d)

    def body(dn_ref, x_ref, g_ref, skip_ref, *rest):
        dx_ref, dxb_ref, dg_ref = rest[len(after):]
        dx, dgr = _rmsnorm_bwd_rows(x_ref[...], g_ref[...], dn_ref[...])
        dx = dx + skip_ref[...]
        dx_ref[...] = dx
        dxb_ref[...] = dx.astype(BF16)

        @pl.when(pl.program_id(0) == 0)
        def _():
            dg_ref[...] = jnp.zeros_like(dg_ref)

        dg_ref[...] += jnp.sum(dgr, axis=0, keepdims=True)

    row = pl.BlockSpec((tr, d), lambda i: (i, 0))
    vec = pl.BlockSpec((1, d), lambda i: (0, 0))
    return pl.pallas_call(
        body, name=name, grid=(s // tr,),
        in_specs=[row, row, vec, row] + [pl.BlockSpec(memory_space=pl.ANY)] * len(after),
        out_specs=[row, row, vec],
        out_shape=[jax.ShapeDtypeStruct((s, d), F32), jax.ShapeDtypeStruct((s, d), BF16),
                   jax.ShapeDtypeStruct((1, d), F32)],
        compiler_params=_params(("arbitrary",)),
    )(dn, x, g, skip, *after)


def _loss_head(h2, g, target):
    s, d = h2.shape
    tr = _row_block(s, d)

    def body(h_ref, g_ref, t_ref, dh_ref, dhb_ref, dg_ref, loss_ref):
        hv = h_ref[...]
        gv = g_ref[...]
        r = lax.rsqrt(jnp.mean(hv * hv, axis=-1, keepdims=True) + EPS)
        err = hv * r * gv - t_ref[...]
        dx, dgr = _rmsnorm_bwd_rows(hv, gv, err * (1.0 / d))
        dh_ref[...] = dx
        dhb_ref[...] = dx.astype(BF16)

        @pl.when(pl.program_id(0) == 0)
        def _():
            dg_ref[...] = jnp.zeros_like(dg_ref)
            loss_ref[...] = jnp.zeros_like(loss_ref)

        dg_ref[...] += jnp.sum(dgr, axis=0, keepdims=True)
        row_loss = jnp.mean(err * err, axis=-1, keepdims=True)
        loss_ref[...] += 0.5 * jnp.sum(row_loss, axis=0, keepdims=True)

    row = pl.BlockSpec((tr, d), lambda i: (i, 0))
    vec = pl.BlockSpec((1, d), lambda i: (0, 0))
    one = pl.BlockSpec((1, 1), lambda i: (0, 0))
    return pl.pallas_call(
        body, name="loss_head", grid=(s // tr,), in_specs=[row, vec, row],
        out_specs=[row, row, vec, one],
        out_shape=[jax.ShapeDtypeStruct((s, d), F32), jax.ShapeDtypeStruct((s, d), BF16),
                   jax.ShapeDtypeStruct((1, d), F32), jax.ShapeDtypeStruct((1, 1), F32)],
        compiler_params=_params(("arbitrary",)),
    )(h2, g, target)


def _attention_bias_tables():
    k = np.arange(-ATT_KB, ATT_KB + 1)[:, None, None]
    delta = k * ATT_BLOCK + np.arange(ATT_BLOCK)[None, None, :] - np.arange(ATT_BLOCK)[None, :, None]
    dist = np.abs(delta)
    count = np.zeros(delta.shape, np.int32)
    for window, dilation in DILATED_PATTERNS:
        count += (delta % dilation == 0) & (dist <= min(window // 2, ATT_NEAR))
    logc = np.where(count > 0, np.log(np.maximum(count, 1)), MASKED)
    return dist.astype(np.float32), logc.astype(np.float32)


def _far_bias_tables(per_class):
    steps = np.abs(np.arange(per_class)[:, None] - np.arange(per_class)[None, :]) * ATT_CLASSES
    valid = (steps > ATT_NEAR) & (steps <= ATT_REACH)
    return steps.astype(np.float32), np.where(valid, 0.0, MASKED).astype(np.float32)


def _to_classes(x):
    s, cols = x.shape
    return jnp.reshape(jnp.transpose(jnp.reshape(x, (s // ATT_CLASSES, ATT_CLASSES, cols)), (1, 0, 2)), (s, cols))


def _from_classes(x):
    s, cols = x.shape
    return jnp.reshape(jnp.transpose(jnp.reshape(x, (ATT_CLASSES, s // ATT_CLASSES, cols)), (1, 0, 2)), (s, cols))


def _head_bias(bias_ref, slope, dist_ref, logc_ref):
    for kk in range(ATT_WINDOW):
        bias_ref[kk] = logc_ref[kk] - slope * dist_ref[kk]
    bias_ref[ATT_WINDOW] = jnp.full((ATT_BLOCK, ATT_BLOCK), MASKED, F32)


def _window_start(i, nq, nwin):
    return jnp.clip(i - ATT_KB, 0, nq - nwin)


def _window_block(j, i):
    rows = pl.ds(pl.multiple_of(j * ATT_BLOCK, ATT_BLOCK), ATT_BLOCK)
    kk = j - i + ATT_KB
    return rows, jnp.where(jnp.logical_and(kk >= 0, kk < ATT_WINDOW), kk, ATT_WINDOW)


def _attention_far_fwd(qkv, slopes, n_heads, jobs=(), after=()):
    s = qkv.shape[0]
    per_class = s // ATT_CLASSES
    scale = HEAD_DIM ** -0.5
    dist, logc = _far_bias_tables(per_class)

    def body(slope_ref, q_ref, k_ref, v_ref, dist_ref, logc_ref, o_ref, lse_ref):
        bias = logc_ref[...] - slope_ref[pl.program_id(0)] * dist_ref[...]
        for a in range(ATT_FAR_GROUP):
            rows = pl.ds(a * per_class, per_class)
            sc = _dot(q_ref[rows, :], k_ref[rows, :], tb=True) * scale + bias
            m = jnp.maximum(jnp.max(sc, axis=-1, keepdims=True), ROW_MAX_INIT)
            p = jnp.exp(sc - m)
            l = jnp.maximum(jnp.sum(p, axis=-1, keepdims=True), 1e-30)
            o_ref[rows, :] = (_dot(p.astype(BF16), v_ref[rows, :]) / l).astype(BF16)
            lse_ref[rows, :] = jnp.broadcast_to(m + jnp.log(l), (per_class, HEAD_DIM))

    hh = n_heads
    blk = pl.BlockSpec((ATT_FAR_GROUP * per_class, HEAD_DIM), lambda h, r: (r, h))
    table = pl.BlockSpec(dist.shape, lambda h, r: (0, 0))
    return _call(
        body, name="attention_far_fwd", grid=(hh, ATT_CLASSES // ATT_FAR_GROUP),
        in_specs=[pl.BlockSpec(memory_space=pltpu.SMEM), blk,
                  pl.BlockSpec((ATT_FAR_GROUP * per_class, HEAD_DIM), lambda h, r: (r, hh + h)),
                  pl.BlockSpec((ATT_FAR_GROUP * per_class, HEAD_DIM), lambda h, r: (r, 2 * hh + h)), table, table],
        out_specs=[blk, blk],
        out_shape=[jax.ShapeDtypeStruct((s, hh * HEAD_DIM), BF16), jax.ShapeDtypeStruct((s, hh * HEAD_DIM), F32)],
        operands=[slopes, qkv, qkv, qkv, jnp.asarray(dist), jnp.asarray(logc)],
        semantics=("parallel", "parallel"), jobs=jobs, after=after)


def _attention_fwd(proj, slopes, far_out, far_lse, n_heads, jobs=(), after=()):
    s = proj.shape[0]
    nq = s // ATT_BLOCK
    scale = HEAD_DIM ** -0.5
    dist, logc = _attention_bias_tables()

    nwin = min(ATT_WINDOW, nq)

    group = math.gcd(ATT_NEAR_GROUP, nq)

    def body(slope_ref, q_ref, k_ref, v_ref, fo_ref, fl_ref, dist_ref, logc_ref, o_ref, lse_ref, bias_ref, s_ref):
        h, step = pl.program_id(0), pl.program_id(1)

        @pl.when(step == 0)
        def _():
            _head_bias(bias_ref, slope_ref[h], dist_ref, logc_ref)

        for a in range(group):
            i = step * group + a
            mine = pl.ds(a * ATT_BLOCK, ATT_BLOCK)
            q = q_ref[mine, :]
            first = _window_start(i, nq, nwin)
            m = jnp.full((ATT_BLOCK, 1), ROW_MAX_INIT, F32)
            for b in range(nwin):
                rows, kk = _window_block(first + b, i)
                sc = _dot(q, k_ref[rows, :], tb=True) * scale + bias_ref[kk]
                s_ref[a * nwin + b] = sc
                m = jnp.maximum(m, jnp.max(sc, axis=-1, keepdims=True))
            l = jnp.zeros((ATT_BLOCK, 1), F32)
            acc = jnp.zeros((ATT_BLOCK, HEAD_DIM), F32)
            for b in range(nwin):
                rows, _ = _window_block(first + b, i)
                p = jnp.exp(s_ref[a * nwin + b] - m)
                l = l + jnp.sum(p, axis=-1, keepdims=True)
                acc = acc + _dot(p.astype(BF16), v_ref[rows, :])
            near_lse = m + jnp.log(l)
            far_lse_col = fl_ref[mine, :1]
            lse = jnp.maximum(near_lse, far_lse_col)
            lse = lse + jnp.log(jnp.exp(near_lse - lse) + jnp.exp(far_lse_col - lse))
            o_ref[mine, :] = (acc * (jnp.exp(near_lse - lse) / l)
                              + fo_ref[mine, :].astype(F32) * jnp.exp(far_lse_col - lse)).astype(BF16)
            lse_ref[mine, :] = jnp.broadcast_to(lse, (ATT_BLOCK, HEAD_DIM))

    hh = n_heads
    blk = pl.BlockSpec((group * ATT_BLOCK, HEAD_DIM), lambda h, i: (i, h))
    table = pl.BlockSpec(dist.shape, lambda h, i: (0, 0, 0))
    return _call(
        body, name="attention_fwd", grid=(hh, nq // group),
        in_specs=[pl.BlockSpec(memory_space=pltpu.SMEM), blk,
                  pl.BlockSpec((s, HEAD_DIM), lambda h, i: (0, hh + h)),
                  pl.BlockSpec((s, HEAD_DIM), lambda h, i: (0, 2 * hh + h)), blk, blk, table, table],
        out_specs=[blk, blk],
        out_shape=[jax.ShapeDtypeStruct((s, hh * HEAD_DIM), BF16), jax.ShapeDtypeStruct((s, hh * HEAD_DIM), F32)],
        operands=[slopes, proj, proj, proj, far_out, far_lse, jnp.asarray(dist), jnp.asarray(logc)],
        scratch_shapes=[pltpu.VMEM((ATT_WINDOW + 1, ATT_BLOCK, ATT_BLOCK), F32),
                        pltpu.VMEM((group * nwin, ATT_BLOCK, ATT_BLOCK), F32)],
        semantics=("parallel", "arbitrary"), jobs=jobs, after=after)


def _attention_far_bwd(qkv, slopes, out, dout, lse, n_heads):
    s = qkv.shape[0]
    per_class = s // ATT_CLASSES
    scale = HEAD_DIM ** -0.5
    dist, logc = _far_bias_tables(per_class)

    def body(slope_ref, q_ref, k_ref, v_ref, o_ref, do_ref, lse_ref, dist_ref, logc_ref, dq_ref, dk_ref, dv_ref):
        bias = logc_ref[...] - slope_ref[pl.program_id(0)] * dist_ref[...]
        for a in range(ATT_FAR_GROUP):
            rows = pl.ds(a * per_class, per_class)
            q, k, do = q_ref[rows, :], k_ref[rows, :], do_ref[rows, :]
            delta = jnp.sum(do.astype(F32) * o_ref[rows, :].astype(F32), axis=-1, keepdims=True)
            p = jnp.exp(_dot(q, k, tb=True) * scale + bias - lse_ref[rows, :1])
            dv_ref[rows, :] = _dot(p.astype(BF16), do, ta=True).astype(BF16)
            ds = (p * (_dot(do, v_ref[rows, :], tb=True) - delta) * scale).astype(BF16)
            dk_ref[rows, :] = _dot(ds, q, ta=True).astype(BF16)
            dq_ref[rows, :] = _dot(ds, k).astype(BF16)

    hh = n_heads
    blk = pl.BlockSpec((ATT_FAR_GROUP * per_class, HEAD_DIM), lambda h, r: (r, h))
    table = pl.BlockSpec(dist.shape, lambda h, r: (0, 0))
    o_shape = jax.ShapeDtypeStruct((s, hh * HEAD_DIM), BF16)
    return pl.pallas_call(
        body, name="attention_far_bwd", grid=(hh, ATT_CLASSES // ATT_FAR_GROUP),
        in_specs=[pl.BlockSpec(memory_space=pltpu.SMEM), blk,
                  pl.BlockSpec((ATT_FAR_GROUP * per_class, HEAD_DIM), lambda h, r: (r, hh + h)),
                  pl.BlockSpec((ATT_FAR_GROUP * per_class, HEAD_DIM), lambda h, r: (r, 2 * hh + h)),
                  blk, blk, blk, table, table],
        out_specs=[blk] * 3, out_shape=[o_shape] * 3,
        compiler_params=_params(("parallel", "parallel")),
    )(slopes, qkv, qkv, qkv, out, dout, lse, jnp.asarray(dist), jnp.asarray(logc))


def _attention_bwd(proj, slopes, out, lse, dmixed, far_grads, n_heads, jobs=()):
    s = proj.shape[0]
    nq = s // ATT_BLOCK
    scale = HEAD_DIM ** -0.5
    dist, logc = _attention_bias_tables()

    nwin = min(ATT_WINDOW, nq)
    group = math.gcd(ATT_NEAR_GROUP, nq)

    def body(slope_ref, q_ref, k_ref, v_ref, o_ref, do_ref, lse_ref, fdq_ref, fdk_ref, fdv_ref, dist_ref, logc_ref,
             dq_ref, dk_ref, dv_ref, dk_acc, dv_acc, bias_ref):
        h, step = pl.program_id(0), pl.program_id(1)

        @pl.when(step == 0)
        def _():
            dk_acc[...] = jnp.zeros_like(dk_acc)
            dv_acc[...] = jnp.zeros_like(dv_acc)
            _head_bias(bias_ref, slope_ref[h], dist_ref, logc_ref)

        for a in range(group):
            i = step * group + a
            mine = pl.ds(a * ATT_BLOCK, ATT_BLOCK)
            q = q_ref[mine, :]
            do = do_ref[mine, :]
            lse_col = lse_ref[mine, :1]
            delta = jnp.sum(do.astype(F32) * o_ref[mine, :].astype(F32), axis=-1, keepdims=True)
            first = _window_start(i, nq, nwin)
            dq = jnp.zeros((ATT_BLOCK, HEAD_DIM), F32)
            for b in range(nwin):
                rows, kk = _window_block(first + b, i)
                kj = k_ref[rows, :]
                vj = v_ref[rows, :]
                p = jnp.exp(_dot(q, kj, tb=True) * scale + bias_ref[kk] - lse_col)
                dv_acc[rows, :] += _dot(p.astype(BF16), do, ta=True)
                dp = _dot(do, vj, tb=True)
                ds = (p * (dp - delta) * scale).astype(BF16)
                dk_acc[rows, :] += _dot(ds, q, ta=True)
                dq = dq + _dot(ds, kj)
            dq_ref[mine, :] = (dq + fdq_ref[mine, :].astype(F32)).astype(BF16)

        @pl.when(step == nq // group - 1)
        def _():
            dk_ref[...] = (dk_acc[...] + fdk_ref[...].astype(F32)).astype(BF16)
            dv_ref[...] = (dv_acc[...] + fdv_ref[...].astype(F32)).astype(BF16)

    hh = n_heads
    blk = pl.BlockSpec((group * ATT_BLOCK, HEAD_DIM), lambda h, i: (i, h))
    col = pl.BlockSpec((s, HEAD_DIM), lambda h, i: (0, h))
    table = pl.BlockSpec(dist.shape, lambda h, i: (0, 0, 0))
    o_shape = jax.ShapeDtypeStruct((s, hh * HEAD_DIM), BF16)
    return _call(
        body, name="attention_bwd", grid=(hh, nq // group),
        in_specs=[pl.BlockSpec(memory_space=pltpu.SMEM), blk,
                  pl.BlockSpec((s, HEAD_DIM), lambda h, i: (0, hh + h)),
                  pl.BlockSpec((s, HEAD_DIM), lambda h, i: (0, 2 * hh + h)),
                  blk, blk, blk, blk, col, col, table, table],
        out_specs=[blk, col, col], out_shape=[o_shape] * 3,
        operands=[slopes, proj, proj, proj, out, dmixed, lse, *far_grads, jnp.asarray(dist), jnp.asarray(logc)],
        scratch_shapes=[pltpu.VMEM((s, HEAD_DIM), F32)] * 2
        + [pltpu.VMEM((ATT_WINDOW + 1, ATT_BLOCK, ATT_BLOCK), F32)],
        semantics=("parallel", "arbitrary"), jobs=jobs)


def _ret_decays(lgc, lga, strict_c, strict_a):
    c = RET_CHUNK
    rel = (lax.broadcasted_iota(jnp.int32, (c, c), 0) - lax.broadcasted_iota(jnp.int32, (c, c), 1)).astype(F32)
    in_c = (rel > 0) if strict_c else (rel >= 0)
    in_a = (rel < 0) if strict_a else (rel <= 0)
    mask = (jnp.where(in_c, jnp.exp(lgc * jnp.maximum(rel, 0.0)), 0.0)
            + jnp.where(in_a, jnp.exp(lga * jnp.maximum(-rel, 0.0)), 0.0))
    idx = lax.broadcasted_iota(jnp.int32, (c, 1), 0).astype(F32)
    ones = jnp.ones((1, HEAD_DIM), F32)
    dec = dict(
        rel=rel, mask=mask, idx=idx,
        a_c=jnp.exp(lgc * (idx + 1.0)), b_c=jnp.exp(lgc * (c - 1.0 - idx)), chunk_c=jnp.exp(ones * (lgc * c)),
        a_a=jnp.exp(lga * (c - idx)), b_a=jnp.exp(lga * idx), chunk_a=jnp.exp(ones * (lga * c)),
    )
    return dec


def _scaled(x, col):
    return (x.astype(F32) * col).astype(BF16)


def _chunk_rows(i):
    return pl.ds(pl.multiple_of(i * RET_CHUNK, RET_CHUNK), RET_CHUNK)


def _chunk_loop(nc, step, init, unroll=RET_UNROLL):
    group = math.gcd(nc, unroll)

    def trip(t, carry):
        for u in range(group):
            carry = step(t * group + u, carry)
        return carry

    return lax.fori_loop(0, nc // group, trip, init)


def _retention(a, b, c, lg_c, lg_a, *, strict_c, strict_a, scale, n_heads, name, gate=None, norm_w=None, jobs=(),
               heads=None, so_far=None, after=()):
    s = a[0].shape[0]
    nc = s // RET_CHUNK
    epilogue = gate is not None
    first_head, head_count = heads if heads is not None else (0, n_heads)

    def body(*refs):
        lgc_ref, lga_ref, a_ref, b_ref, c_ref = refs[:5]
        if epilogue:
            g_ref, w_ref = refs[5:7]
            o_ref, mix_ref, sa_ref = refs[-3:]
        else:
            o_ref, sa_ref = refs[-2:]
        h = first_head + pl.program_id(0)
        dec = _ret_decays(lgc_ref[h], lga_ref[h], strict_c, strict_a)

        def reverse(t, state):
            i = nc - 1 - t
            sa_ref[i] = state.astype(BF16)
            rows = _chunk_rows(i)
            return state * dec["chunk_a"] + _dot(_scaled(b_ref[rows, :], dec["b_a"]), c_ref[rows, :], ta=True)

        _chunk_loop(nc, reverse, jnp.zeros((HEAD_DIM, HEAD_DIM), F32))

        def forward(i, state):
            rows = _chunk_rows(i)
            ai, bi, ci = a_ref[rows, :], b_ref[rows, :], c_ref[rows, :]
            inner = (_dot(ai, bi, tb=True) * dec["mask"]).astype(BF16)
            out = (_dot(inner, ci) + _dot(_scaled(ai, dec["a_c"]), state.astype(BF16))
                   + _dot(_scaled(ai, dec["a_a"]), sa_ref[i])) * scale
            o_ref[rows, :] = out.astype(BF16)
            if epilogue:
                r = lax.rsqrt(jnp.mean(out * out, axis=-1, keepdims=True) + EPS)
                g = g_ref[rows, :].astype(F32)
                mix_ref[rows, :] = (out * r * w_ref[...] * (g * _sigmoid(g))).astype(BF16)
            return state * dec["chunk_c"] + _dot(_scaled(bi, dec["b_c"]), ci, ta=True)

        _chunk_loop(nc, forward, jnp.zeros((HEAD_DIM, HEAD_DIM), F32))

    def col(first):
        return pl.BlockSpec((s, HEAD_DIM), lambda h: (0, first + first_head + h))

    smem = pl.BlockSpec(memory_space=pltpu.SMEM)
    in_specs = [smem, smem, col(a[1]), col(b[1]), col(c[1])]
    operands = [lg_c, lg_a, a[0], b[0], c[0]]
    o_shape = jax.ShapeDtypeStruct((s, n_heads * HEAD_DIM), BF16)
    out_specs, out_shape = [col(0)], [o_shape]
    if epilogue:
        in_specs += [col(gate[1]), pl.BlockSpec((1, HEAD_DIM), lambda h: (0, first_head + h))]
        operands += [gate[0], norm_w]
        out_specs, out_shape = [col(0)] * 2, [o_shape] * 2
    updates = None
    if so_far is not None:
        updates = {len(operands) + t: t for t in range(len(so_far))}
        in_specs += [pl.BlockSpec(memory_space=pl.ANY)] * len(so_far)
        operands += list(so_far)
    res, carried = _call(
        body, name=name, grid=(head_count,), in_specs=in_specs, out_specs=out_specs, out_shape=out_shape,
        operands=operands, scratch_shapes=[pltpu.VMEM((nc, HEAD_DIM, HEAD_DIM), BF16)],
        semantics=("parallel",), jobs=jobs, updates=updates, after=after)
    res = res if epilogue else res[0]
    return (res, carried) if jobs else res


def _retention_decay_grads(a, b, c, e, lg_c, lg_a, *, scale, n_heads):
    s = a[0].shape[0]
    nc = s // RET_CHUNK
    cf = float(RET_CHUNK)

    def body(lgc_ref, lga_ref, a_ref, b_ref, c_ref, e_ref, gc_ref, ga_ref, sa_ref, ta_ref):
        h = pl.program_id(0)
        lgc, lga = lgc_ref[h], lga_ref[h]
        dec = _ret_decays(lgc, lga, True, True)
        rel, idx = dec["rel"], dec["idx"]
        w_c = jnp.where(rel > 0, rel * jnp.exp(lgc * jnp.maximum(rel, 0.0)), 0.0)
        w_a = jnp.where(rel < 0, -rel * jnp.exp(lga * jnp.maximum(-rel, 0.0)), 0.0)
        zero = jnp.zeros((HEAD_DIM, HEAD_DIM), F32)

        def reverse(t, carry):
            st, dst = carry
            i = nc - 1 - t
            sa_ref[i] = st.astype(BF16)
            ta_ref[i] = dst.astype(BF16)
            rows = _chunk_rows(i)
            bi, ci = b_ref[rows, :], c_ref[rows, :]
            st_new = st * dec["chunk_a"] + _dot(_scaled(bi, dec["b_a"]), ci, ta=True)
            dst_new = (cf * st + dst) * dec["chunk_a"] + _dot(_scaled(bi, idx * dec["b_a"]), ci, ta=True)
            return st_new, dst_new

        _chunk_loop(nc, reverse, (zero, zero))

        def forward(i, carry):
            st, dst, acc_c, acc_a = carry
            rows = _chunk_rows(i)
            ai, bi, ci = a_ref[rows, :], b_ref[rows, :], c_ref[rows, :]
            ev = e_ref[rows, :].astype(F32)
            pg = _dot(ai, bi, tb=True) * _dot(e_ref[rows, :], ci, tb=True)
            a_c, a_a = _scaled(ai, dec["a_c"]), _scaled(ai, dec["a_a"])
            inter_c = _dot(a_c, st.astype(BF16)) * (idx + 1.0) + _dot(a_c, dst.astype(BF16))
            inter_a = _dot(a_a, sa_ref[i]) * (cf - idx) + _dot(a_a, ta_ref[i])
            acc_c = acc_c + jnp.sum(pg * w_c, axis=0, keepdims=True) + jnp.sum(inter_c * ev, axis=0, keepdims=True)
            acc_a = acc_a + jnp.sum(pg * w_a, axis=0, keepdims=True) + jnp.sum(inter_a * ev, axis=0, keepdims=True)
            st_new = st * dec["chunk_c"] + _dot(_scaled(bi, dec["b_c"]), ci, ta=True)
            dst_new = ((cf * st + dst) * dec["chunk_c"]
                       + _dot(_scaled(bi, (cf - 1.0 - idx) * dec["b_c"]), ci, ta=True))
            return st_new, dst_new, acc_c, acc_a

        row = jnp.zeros((1, HEAD_DIM), F32)
        _, _, acc_c, acc_a = _chunk_loop(nc, forward, (zero, zero, row, row))
        gc_ref[...] = jnp.broadcast_to(jnp.sum(acc_c, axis=-1, keepdims=True) * scale, gc_ref.shape)
        ga_ref[...] = jnp.broadcast_to(jnp.sum(acc_a, axis=-1, keepdims=True) * scale, ga_ref.shape)

    def col(first):
        return pl.BlockSpec((s, HEAD_DIM), lambda h: (0, first + h))

    smem = pl.BlockSpec(memory_space=pltpu.SMEM)
    o_spec = pl.BlockSpec((1, 8, HEAD_DIM), lambda h: (h, 0, 0))
    o_shape = jax.ShapeDtypeStruct((n_heads, 8, HEAD_DIM), F32)
    gc, ga = pl.pallas_call(
        body, name="retention_decay_grads", grid=(n_heads,),
        in_specs=[smem, smem, col(a[1]), col(b[1]), col(c[1]), col(e[1])],
        out_specs=[o_spec] * 2, out_shape=[o_shape] * 2,
        scratch_shapes=[pltpu.VMEM((nc, HEAD_DIM, HEAD_DIM), BF16)] * 2,
        compiler_params=_params(("parallel",)),
    )(lg_c, lg_a, a[0], b[0], c[0], e[0])
    return gc[:, 0, 0], ga[:, 0, 0]


def _ret_gate_bwd(dmixed, first_col, out, proj, gate_col, norm_w, n_heads):
    s = out.shape[0]
    tr = _row_block(s, 8 * HEAD_DIM)

    def body(dm_ref, o_ref, g_ref, w_ref, do_ref, dg_ref, dw_ref):
        dm = dm_ref[...].astype(F32)
        ov = o_ref[...].astype(F32)
        g = g_ref[...].astype(F32)
        w = w_ref[...]
        r = lax.rsqrt(jnp.mean(ov * ov, axis=-1, keepdims=True) + EPS)
        ohat = ov * r
        sg = _sigmoid(g)
        silu = g * sg
        dg_ref[...] = (dm * ohat * w * sg * (1.0 + g * (1.0 - sg))).astype(BF16)
        dohat = dm * w * silu
        do_ref[...] = (r * (dohat - ohat * jnp.mean(dohat * ohat, axis=-1, keepdims=True))).astype(BF16)

        @pl.when(pl.program_id(1) == 0)
        def _():
            dw_ref[...] = jnp.zeros_like(dw_ref)

        dw_ref[...] += jnp.sum(dm * ohat * silu, axis=0, keepdims=True)

    def blk(first):
        return pl.BlockSpec((tr, HEAD_DIM), lambda h, i: (i, first + h))

    vec = pl.BlockSpec((1, HEAD_DIM), lambda h, i: (0, h))
    o_shape = jax.ShapeDtypeStruct((s, n_heads * HEAD_DIM), BF16)
    return pl.pallas_call(
        body, name="ret_gate_bwd", grid=(n_heads, s // tr),
        in_specs=[blk(first_col), blk(0), blk(gate_col), vec],
        out_specs=[blk(0), blk(0), vec],
        out_shape=[o_shape, o_shape, jax.ShapeDtypeStruct((1, n_heads * HEAD_DIM), F32)],
        compiler_params=_params(("parallel", "arbitrary")),
    )(dmixed, out, proj, norm_w)


def _step(x, target, norm_mix_w, ret_decay_fwd, ret_decay_bwd, ret_norm_w, norm_ffn_w, norm_final_w, own,
          w_in_started, w_gate_started, shard_ids, pos):
    d = x.shape[1]
    nh = d // (2 * HEAD_DIM)
    scale = HEAD_DIM ** -0.5
    slopes = jnp.exp2(-8.0 * jnp.arange(1, nh + 1, dtype=F32) / nh)
    lg_f = -jnp.exp(ret_decay_fwd)
    lg_b = -jnp.exp(ret_decay_bwd)
    q_r, k_r, v_r, g_r = 3 * nh, 4 * nh, 5 * nh, 6 * nh
    ax = BIG_AXIS

    def gather(names, arrays, stage, part=None, peers=(0, 1, 2)):
        return _gather_job(arrays, [ax[k] for k in names], stage, part, peers)

    def add_halves(k, g, received):
        return _add_halves(g, received, ax[k], pos, name="grad_add_halves_" + k)

    def sum_parts(k, g, received, parts):
        return _sum_chip_parts(g, received, parts, ax[k], pos, name="grad_sum_parts_" + k)

    sems, w_in, token = w_in_started
    n1 = _rmsnorm_fwd(x, norm_mix_w, name="norm_mix_fwd", after=token)
    proj = _in_proj_part(n1, w_in, None, shard_ids, 0, out_cols=w_in.shape[1])
    for peer in range(3):
        behind = [proj] + ([own[k] for k in ("w_out", "w_up", "w_down")] if peer == 0 else [])
        w_in = _split_gather_wait(sems, w_in, ax["w_in"], peer, behind)
        (w_in,) = _run_jobs([gather(["w_in"], [w_in], "d2d", peers=(peer,))], name="all_gather_w_in_sibling_%d" % peer)
        proj = _in_proj_part(n1, w_in, proj, shard_ids, 1 + peer, out_cols=w_in.shape[1])
    (w_gate,) = _split_wait(w_gate_started, [proj], name="all_gather_w_gate_wait")
    qkv_classes = _to_classes(proj[:, :3 * nh * HEAD_DIM])
    (ret, ret_mixed), [[w_gate], [w_out]] = _retention(
        (proj, q_r), (proj, k_r), (proj, v_r), lg_f, lg_b, strict_c=False, strict_a=True, scale=scale, n_heads=nh,
        name="retention_fwd_first", gate=(proj, g_r), norm_w=ret_norm_w, heads=(0, nh // 2),
        jobs=[gather(["w_gate"], [w_gate], "d2d"), gather(["w_out"], [own["w_out"]], "ici")])
    (far_out, far_lse), [[w_up]] = _attention_far_fwd(
        qkv_classes, slopes, nh, jobs=[gather(["w_up"], [own["w_up"]], "ici", (0, 1, 4))], after=[ret_mixed])
    ret, ret_mixed = _retention(
        (proj, q_r), (proj, k_r), (proj, v_r), lg_f, lg_b, strict_c=False, strict_a=True, scale=scale, n_heads=nh,
        name="retention_fwd_second", gate=(proj, g_r), norm_w=ret_norm_w, heads=(nh // 2, nh - nh // 2),
        so_far=[ret, ret_mixed], after=[far_out])
    (attn, lse), [[w_out], [w_up]] = _attention_fwd(
        proj, slopes, _from_classes(far_out), _from_classes(far_lse), nh, after=[ret_mixed],
        jobs=[gather(["w_out"], [w_out], "d2d"),
              _fuse(gather(["w_up"], [w_up], "d2d", (0, 1, 4)), gather(["w_up"], [w_up], "ici", (1, 2, 4)))])
    mixed = jnp.concatenate([attn, ret_mixed], axis=1)
    h1, [[w_up]] = _matmul(
        mixed, w_out, name="out_proj", residual=x,
        jobs=[_fuse(gather(["w_up"], [w_up], "d2d", (1, 2, 4)), gather(["w_up"], [w_up], "ici", (3, 1, 4)))])
    up_sibling = _split_start(gather(["w_up"], [w_up], "d2d", (3, 1, 4)), name="all_gather_w_up_sibling_start")
    n2 = _rmsnorm_fwd(h1, norm_ffn_w, name="norm_ffn_fwd", after=up_sibling["token"])
    (w_up,) = _split_wait(up_sibling, [n2], name="all_gather_w_up_sibling_wait")
    (gate, up, act), [[w_down]] = _swiglu_fwd(n2, w_gate, w_up, jobs=[gather(["w_down"], [own["w_down"]], "ici")])
    (w_down,) = _run_jobs([gather(["w_down"], [w_down], "d2d")], name="all_gather_w_down_sibling")
    h2 = _matmul(act, w_down, name="down_proj", residual=h1, tk=2816)
    dh2, dh2_b, d_norm_final, loss = _loss_head(h2, norm_final_w, target)

    dgate, dup = _swiglu_bwd_act(dh2_b, w_down, gate, up)
    g_down = _weight_grad(act, dh2_b, name="grad_w_down")
    g_gate, [[r_down]] = _weight_grad(n2, dgate, name="grad_w_gate", jobs=[_exchange_job([g_down], [ax["w_down"]])])
    s_down = add_halves("w_down", g_down, r_down)
    g_up, [[r_gate], [p_down]] = _weight_grad(
        n2, dup, name="grad_w_up",
        jobs=[_exchange_job([g_gate], [ax["w_gate"]]), _send_sums_job([s_down], [ax["w_down"]], (0, 1, 2))])
    s_gate = add_halves("w_gate", g_gate, r_gate)
    dn2, [[r_up], [p_gate], [p_down]] = _swiglu_bwd_in(
        dgate, dup, w_gate, w_up,
        jobs=[_exchange_job([g_up], [ax["w_up"]]), _send_sums_job([s_gate], [ax["w_gate"]]),
              _send_sums_job([s_down], [ax["w_down"]], (1, 1, 2), landing=[p_down])])
    h_down = sum_parts("w_down", g_down, r_down, p_down)
    s_up = add_halves("w_up", g_up, r_up)
    h_gate = sum_parts("w_gate", g_gate, r_gate, p_gate)
    dh1, dh1_b, d_norm_ffn = _rmsnorm_bwd(dn2, h1, norm_ffn_w, dh2, name="norm_ffn_bwd")

    dmixed, [[gr_down], [p_up]] = _matmul(
        dh1_b, w_out, name="out_proj_bwd", tb=True, out_dtype=BF16,
        jobs=[_join_job([h_down], [ax["w_down"]]), _send_sums_job([s_up], [ax["w_up"]], (0, 1, 4))])
    far_in = [_to_classes(t) for t in (attn, dmixed[:, :nh * HEAD_DIM], lse)]
    g_out, [[p_up]] = _weight_grad(mixed, dh1_b, name="grad_w_out",
                                   jobs=[_send_sums_job([s_up], [ax["w_up"]], (1, 1, 4), landing=[p_up])])
    d_ret, dg_r, d_ret_norm = _ret_gate_bwd(dmixed, nh, ret, proj, g_r, ret_norm_w, nh)
    far_grads = _attention_far_bwd(qkv_classes, slopes, *far_in, nh)
    far_grads = [_from_classes(t) for t in far_grads]
    dq_r, [[gr_gate], [p_up]] = _retention(
        (d_ret, 0), (proj, v_r), (proj, k_r), lg_f, lg_b, strict_c=False, strict_a=True, scale=scale, n_heads=nh,
        name="retention_dq",
        jobs=[_join_job([h_gate], [ax["w_gate"]]), _send_sums_job([s_up], [ax["w_up"]], (2, 1, 4), landing=[p_up])])
    (dq_a, dk_a, dv_a), [[p_up], [r_out]] = _attention_bwd(
        proj, slopes, attn, lse, dmixed, far_grads, nh,
        jobs=[_send_sums_job([s_up], [ax["w_up"]], (3, 1, 4), landing=[p_up]),
              _exchange_job([g_out], [ax["w_out"]])])
    s_out = add_halves("w_out", g_out, r_out)
    h_up = sum_parts("w_up", g_up, r_up, p_up)
    dv_r, [[p_out], [gr_up]] = _retention(
        (proj, k_r), (proj, q_r), (d_ret, 0), lg_b, lg_f, strict_c=True, strict_a=False, scale=scale, n_heads=nh,
        name="retention_dv", jobs=[_send_sums_job([s_out], [ax["w_out"]]), _join_job([h_up], [ax["w_up"]])])
    h_out = sum_parts("w_out", g_out, r_out, p_out)
    dk_r, [[gr_out]] = _retention(
        (proj, v_r), (d_ret, 0), (proj, q_r), lg_b, lg_f, strict_c=True, strict_a=False, scale=scale, n_heads=nh,
        name="retention_dk", jobs=[_join_job([h_out], [ax["w_out"]])])
    dlg_f, dlg_b = _retention_decay_grads((proj, q_r), (proj, k_r), (proj, v_r), (d_ret, 0), lg_f, lg_b,
                                          scale=scale, n_heads=nh)
    dproj = [dq_a, dk_a, dv_a, dq_r, dk_r, dv_r, dg_r]
    g_in = _weight_grad_pieces(n1, dproj, name="grad_w_in")
    exchange = _split_start(_exchange_job([g_in], [ax["w_in"]]), name="grad_exchange_w_in_start")
    dn1 = _matmul_pieces_nt(dproj, w_in, name="in_proj_bwd", after=[exchange["token"]])
    g_in, r_in = _split_wait(exchange, [dn1], name="grad_exchange_w_in_wait")
    s_in = add_halves("w_in", g_in, r_in)
    sending = _split_start(_send_sums_job([s_in], [ax["w_in"]]), name="grad_send_w_in_start")
    dx, _, d_norm_mix = _rmsnorm_bwd(dn1, x, norm_mix_w, dh1, name="norm_mix_bwd", after=[sending["token"]])

    small = dict(loss=loss[0, 0], norm_mix_w=d_norm_mix, ret_decay_fwd=dlg_f * lg_f, ret_decay_bwd=dlg_b * lg_b,
                 ret_norm_w=d_ret_norm, norm_ffn_w=d_norm_ffn, norm_final_w=d_norm_final)
    return (dx, dict(w_out=gr_out, w_gate=gr_gate, w_up=gr_up, w_down=gr_down), small,
            dict(sending=sending, grad=g_in, received=r_in))


def _mesh_position():
    x, y, c = lax.axis_index("x"), lax.axis_index("y"), lax.axis_index("c")
    chips = [(1 - x, y), (x, 1 - y), (1 - x, 1 - y)]
    return x, y, c, chips


def _span(span):
    if span is None:
        return slice(None)
    start, size, step = span
    return pl.ds(start if isinstance(start, int) else pl.multiple_of(start, step), size)


def _part_rows(part, rows):
    first, count, of = part
    return first * (rows // of), count * (rows // of), rows // of


def _region(ref, axis, shard, half, shard_size, half_size, part=None, total_rows=None):
    along = None if shard is None else (shard * shard_size, shard_size, shard_size)
    other = None if half is None else (half * half_size, half_size, half_size)
    rows, cols = (other, along) if axis == 1 else (along, other)
    if part is not None:
        start, size, _ = rows if rows is not None else (0, total_rows, None)
        offset, size, step = _part_rows(part, size)
        rows = (start + offset, size, step)
    return ref.at[_span(rows), _span(cols)]


def _fuse(first, second):
    assert not (first.ins or first.outs or second.ins or second.outs)
    assert len(first.ios) == len(second.ios) and all(a is b for a, b in zip(first.ios, second.ios))
    cut = len(first.sems)

    def start(refs, sems):
        first.start(refs, sems[:cut])
        second.start(refs, sems[cut:])

    def finish(refs, sems):
        first.finish(refs, sems[:cut])
        second.finish(refs, sems[cut:])

    return _Job(ios=first.ios, sems=first.sems + second.sems, start=start, finish=finish)


def _gather_job(full, axes, stage, part=None, peers=(0, 1, 2)):
    n = len(full)

    def copies(refs, sems):
        send_sem, recv_sem = sems
        x, y, c, chips = _mesh_position()
        me = 2 * x + y

        def copy(w, k, shard, half, target):
            rows_cols = full[w].shape
            place = _region(refs[w], axes[w], shard, half, rows_cols[axes[w]] // N_CHIPS, rows_cols[1 - axes[w]] // 2,
                            part)
            return pltpu.make_async_remote_copy(
                src_ref=place, dst_ref=place, send_sem=send_sem.at[w, k], recv_sem=recv_sem.at[w, k],
                device_id=target, device_id_type=MESH)

        def sent(w, k):
            if stage == "ici":
                return copy(w, k, me, c, (chips[k][0], chips[k][1], c))
            return copy(w, k, 2 * chips[k][0] + chips[k][1], c, (x, y, 1 - c))

        def landed(w, k):
            return copy(w, k, 2 * chips[k][0] + chips[k][1], c if stage == "ici" else 1 - c, (x, y, 1 - c))

        return sent, landed

    def start(refs, sems):
        sent, _ = copies(refs, sems)
        for w in range(n):
            for k in peers:
                sent(w, k).start()

    def finish(refs, sems):
        sent, landed = copies(refs, sems)
        for w in range(n):
            for k in peers:
                landed(w, k).wait_recv()
                sent(w, k).wait_send()

    return _Job(ios=full, sems=[pltpu.SemaphoreType.DMA((n, 3))] * 2, start=start, finish=finish)


def _exchange_job(grads, axes):
    n = len(grads)

    def half_shape(w):
        return tuple(d // 2 if a != axes[w] else d for a, d in enumerate(grads[w].shape))

    def copy(refs, sems, w):
        x, y, c, _ = _mesh_position()
        return pltpu.make_async_remote_copy(
            src_ref=_region(refs[w], axes[w], None, 1 - c, 0, half_shape(w)[1 - axes[w]]), dst_ref=refs[n + w],
            send_sem=sems[0].at[w], recv_sem=sems[1].at[w], device_id=(x, y, 1 - c), device_id_type=MESH)

    def start(refs, sems):
        for w in range(n):
            copy(refs, sems, w).start()

    def finish(refs, sems):
        for w in range(n):
            copy(refs, sems, w).wait()

    return _Job(ins=grads, outs=[jax.ShapeDtypeStruct(half_shape(w), F32) for w in range(n)],
                sems=[pltpu.SemaphoreType.DMA((n,))] * 2, start=start, finish=finish)


def _half_block_spec(axis, block, half_blocks, use_half):
    if axis == 1:
        if use_half:
            return pl.BlockSpec(block, lambda i, pos: (pos[0] * half_blocks + i, 0))
        return pl.BlockSpec(block, lambda i, pos: (i, 0))
    if use_half:
        return pl.BlockSpec(block, lambda i, pos: (i, pos[0]))
    return pl.BlockSpec(block, lambda i, pos: (i, 0))


def _add_halves(grad, received, axis, pos, *, name):
    rows, cols = received.shape
    tr = _row_block(rows, cols)
    nb = rows // tr

    def body(pos_ref, g_ref, r_ref, o_ref):
        o_ref[...] = (g_ref[...] + r_ref[...]).astype(BF16)

    blk = (tr, cols)
    return pl.pallas_call(
        body, name=name, out_shape=jax.ShapeDtypeStruct((rows, cols), BF16),
        grid_spec=pltpu.PrefetchScalarGridSpec(
            num_scalar_prefetch=1, grid=(nb,),
            in_specs=[_half_block_spec(axis, blk, nb, True), _half_block_spec(axis, blk, nb, False)],
            out_specs=_half_block_spec(axis, blk, nb, False)),
        compiler_params=_params(("parallel",)),
    )(pos, grad, received)


def _send_sums_job(sums, axes, part=None, landing=None):
    n = len(sums)

    def part_shape(w):
        return tuple(d // N_CHIPS if a == axes[w] else d for a, d in enumerate(sums[w].shape))

    def copy(refs, sems, w, k):
        x, y, c, chips = _mesh_position()
        shard = 2 * chips[k][0] + chips[k][1]
        rows = part_shape(w)[0]
        dst = refs[n + w].at[k]
        if part is not None:
            offset, size, _ = _part_rows(part, rows)
            dst = refs[n + w].at[k, pl.ds(offset, size), :]
        return pltpu.make_async_remote_copy(
            src_ref=_region(refs[w], axes[w], shard, None, part_shape(w)[axes[w]], 0, part, rows), dst_ref=dst,
            send_sem=sems[0].at[w, k], recv_sem=sems[1].at[w, k],
            device_id=(chips[k][0], chips[k][1], c), device_id_type=MESH)

    def start(refs, sems):
        for w in range(n):
            for k in range(3):
                copy(refs, sems, w, k).start()

    def finish(refs, sems):
        for w in range(n):
            for k in range(3):
                copy(refs, sems, w, k).wait()

    sems = [pltpu.SemaphoreType.DMA((n, 3))] * 2
    if landing is not None:
        return _Job(ins=sums, ios=landing, sems=sems, start=start, finish=finish)
    return _Job(ins=sums, outs=[jax.ShapeDtypeStruct((3,) + part_shape(w), BF16) for w in range(n)],
                sems=sems, start=start, finish=finish)


def _sum_chip_parts(grad, received, parts, axis, pos, *, name):
    _, rows, cols = parts.shape
    tr = _row_block(rows, cols)
    nb = rows // tr
    blk = (tr, cols)

    def body(pos_ref, g_ref, r_ref, p_ref, o_ref):
        total = g_ref[...] + r_ref[...]
        for k in range(3):
            total = total + p_ref[k].astype(F32)
        o_ref[...] = total

    if axis == 1:
        g_spec = pl.BlockSpec(blk, lambda i, pos: (pos[0] * nb + i, pos[1]))
        r_spec = pl.BlockSpec(blk, lambda i, pos: (i, pos[1]))
        o_spec = pl.BlockSpec(blk, lambda i, pos: (pos[0] * nb + i, 0))
        shard_shape = (2 * rows, cols)
    else:
        g_spec = pl.BlockSpec(blk, lambda i, pos: (pos[1] * nb + i, pos[0]))
        r_spec = pl.BlockSpec(blk, lambda i, pos: (pos[1] * nb + i, 0))
        o_spec = pl.BlockSpec(blk, lambda i, pos: (i, pos[0]))
        shard_shape = (rows, 2 * cols)
    return pl.pallas_call(
        body, name=name, out_shape=jax.ShapeDtypeStruct(shard_shape, F32),
        grid_spec=pltpu.PrefetchScalarGridSpec(
            num_scalar_prefetch=1, grid=(nb,),
            in_specs=[g_spec, r_spec, pl.BlockSpec((3,) + blk, lambda i, pos: (0, i, 0))],
            out_specs=o_spec),
        compiler_params=_params(("parallel",)),
    )(pos, grad, received, parts)


def _join_job(shards, axes):
    n = len(shards)

    def copy(refs, sems, w, other):
        x, y, c, _ = _mesh_position()
        place = _region(refs[w], axes[w], None, 1 - c if other else c, 0, shards[w].shape[1 - axes[w]] // 2)
        return pltpu.make_async_remote_copy(
            src_ref=place, dst_ref=place, send_sem=sems[0].at[w], recv_sem=sems[1].at[w],
            device_id=(x, y, 1 - c), device_id_type=MESH)

    def start(refs, sems):
        for w in range(n):
            copy(refs, sems, w, False).start()

    def finish(refs, sems):
        for w in range(n):
            copy(refs, sems, w, True).wait_recv()
            copy(refs, sems, w, False).wait_send()

    return _Job(ios=shards, sems=[pltpu.SemaphoreType.DMA((n,))] * 2, start=start, finish=finish)


def _all_reduce_small(vec, after=()):
    rows, cols = vec.shape

    def body(v_ref, *rest):
        o_ref, land_ref, send_sem, recv_sem = rest[len(after):]
        x, y, c, _ = _mesh_position()
        me = 4 * x + 2 * y + c
        land_ref[me] = v_ref[...]
        copies = []
        for k in range(1, 8):
            px, py, pc = x ^ (k >> 2), y ^ ((k >> 1) & 1), c ^ (k & 1)
            copies.append(pltpu.make_async_remote_copy(
                src_ref=v_ref, dst_ref=land_ref.at[me], send_sem=send_sem.at[k], recv_sem=recv_sem.at[k],
                device_id=(px, py, pc), device_id_type=MESH))
        for cp in copies:
            cp.start()
        for k in range(1, 8):
            peer = me ^ k
            pltpu.make_async_remote_copy(
                src_ref=v_ref, dst_ref=land_ref.at[peer], send_sem=send_sem.at[k], recv_sem=recv_sem.at[k],
                device_id=(x, y, c), device_id_type=MESH).wait_recv()
        for cp in copies:
            cp.wait_send()
        total = land_ref[0]
        for k in range(1, 8):
            total = total + land_ref[k]
        o_ref[...] = total

    vmem = pl.BlockSpec(memory_space=pltpu.VMEM)
    return pl.pallas_call(
        body, name="all_reduce_small", in_specs=[vmem] + [pl.BlockSpec(memory_space=pl.ANY)] * len(after),
        out_specs=vmem, out_shape=jax.ShapeDtypeStruct((rows, cols), F32),
        scratch_shapes=[pltpu.VMEM((8, rows, cols), F32), pltpu.SemaphoreType.DMA((8,)), pltpu.SemaphoreType.DMA((8,))],
    )(vec, *after)


def _adamw(w, g, m, v, *, name, after=()):
    rows, cols = w.shape
    tr = _row_block(rows, cols) if rows % 8 == 0 else rows
    bc1 = 1.0 - ADAM_B1 ** ADAM_STEP
    bc2 = 1.0 - ADAM_B2 ** ADAM_STEP

    def body(w_ref, g_ref, m_ref, v_ref, *rest):
        go_ref, d_ref, mo_ref, vo_ref = rest[len(after):]
        gv = g_ref[...]
        go_ref[...] = gv
        mn = ADAM_B1 * m_ref[...] + (1.0 - ADAM_B1) * gv
        vn = ADAM_B2 * v_ref[...] + (1.0 - ADAM_B2) * (gv * gv)
        mo_ref[...] = mn
        vo_ref[...] = vn
        d_ref[...] = -ADAM_LR * ((mn / bc1) / (jnp.sqrt(vn / bc2) + ADAM_EPS) + ADAM_WD * w_ref[...])

    blk = pl.BlockSpec((tr, cols), lambda i: (i, 0))
    shape = jax.ShapeDtypeStruct((rows, cols), F32)
    return pl.pallas_call(
        body, name=name, grid=(rows // tr,), in_specs=[blk] * 4 + [pl.BlockSpec(memory_space=pl.ANY)] * len(after),
        out_specs=[blk] * 4, out_shape=[shape] * 4, compiler_params=_params(("parallel",)),
    )(w, g, m, v, *after)


def _to_bf16_in_place(w, axis, pos, *, name, after=None):
    rows, cols = w.shape
    tr = _row_block(rows, cols)
    nb = rows // tr

    def body(pos_ref, w_ref, *rest):
        rest[-1][...] = w_ref[...].astype(BF16)

    if axis == 1:
        o_spec = pl.BlockSpec((tr, cols), lambda i, pos: (i, pos[1]))
        full_shape = (rows, N_CHIPS * cols)
    else:
        o_spec = pl.BlockSpec((tr, cols), lambda i, pos: (pos[1] * nb + i, 0))
        full_shape = (N_CHIPS * rows, cols)
    in_specs = [pl.BlockSpec((tr, cols), lambda i, pos: (i, 0))]
    operands = [pos, w]
    if after is not None:
        in_specs.append(pl.BlockSpec(after.shape, lambda i, pos: (0, 0)))
        operands.append(after)
    return pl.pallas_call(
        body, name=name, out_shape=jax.ShapeDtypeStruct(full_shape, BF16),
        grid_spec=pltpu.PrefetchScalarGridSpec(num_scalar_prefetch=1, grid=(nb,), in_specs=in_specs, out_specs=o_spec),
        compiler_params=_params(("parallel",)),
    )(*operands)


def _split_gather_start(full, axis):
    rows_cols = full.shape

    def body(buf_ref, *rest):
        sems = rest[:6]
        token_ref = rest[7]
        x, y, c, chips = _mesh_position()
        place = _region(buf_ref, axis, 2 * x + y, c, rows_cols[axis] // N_CHIPS, rows_cols[1 - axis] // 2)
        for k in range(3):
            pltpu.make_async_remote_copy(
                src_ref=place, dst_ref=place, send_sem=sems[k], recv_sem=sems[3 + k],
                device_id=(chips[k][0], chips[k][1], c), device_id_type=MESH).start()
        token_ref[...] = jnp.zeros_like(token_ref)

    hbm = pl.BlockSpec(memory_space=pltpu.HBM)
    sem = pl.BlockSpec(memory_space=pltpu.SEMAPHORE)
    res = pl.pallas_call(
        body, name="all_gather_w_in_start",
        out_shape=(*[pltpu.SemaphoreType.DMA(())] * 6, pltpu.HBM(full.shape, full.dtype),
                   jax.ShapeDtypeStruct((8, HEAD_DIM), F32)),
        in_specs=(hbm,), out_specs=(*[sem] * 6, hbm, pl.BlockSpec(memory_space=pltpu.VMEM)),
        input_output_aliases={0: 6},
        compiler_params=pltpu.CompilerParams(has_side_effects=pltpu.SideEffectType.DATAFLOW_SIDE_EFFECTING),
    )(pltpu.with_memory_space_constraint(full, pltpu.HBM))
    return list(res[:6]), res[6], res[7]


def _split_gather_wait(sems, full, axis, peer, after):
    rows_cols = full.shape

    def body(buf_ref, send_sem, recv_sem, *rest):
        x, y, c, chips = _mesh_position()

        def copy(shard):
            place = _region(buf_ref, axis, shard, c, rows_cols[axis] // N_CHIPS, rows_cols[1 - axis] // 2)
            return pltpu.make_async_remote_copy(
                src_ref=place, dst_ref=place, send_sem=send_sem, recv_sem=recv_sem,
                device_id=(chips[peer][0], chips[peer][1], c), device_id_type=MESH)

        copy(2 * x + y).wait_send()
        copy(2 * chips[peer][0] + chips[peer][1]).wait_recv()

    hbm = pl.BlockSpec(memory_space=pltpu.HBM)
    sem = pl.BlockSpec(memory_space=pltpu.SEMAPHORE)
    return pl.pallas_call(
        body, name="all_gather_w_in_wait_%d" % peer, out_shape=pltpu.HBM(full.shape, full.dtype),
        in_specs=(hbm, sem, sem, *[pl.BlockSpec(memory_space=pl.ANY)] * len(after)), out_specs=hbm,
        input_output_aliases={0: 0},
        compiler_params=pltpu.CompilerParams(has_side_effects=pltpu.SideEffectType.DATAFLOW_SIDE_EFFECTING),
    )(full, sems[peer], sems[3 + peer], *after)


def _in_proj_part(n1, w_in, proj, shard_ids, which, *, out_cols):
    m, kdim = n1.shape
    cols = out_cols // N_CHIPS
    tm = _tile(m, 1024)

    def body(ids_ref, a_ref, b_ref, *rest):
        rest[-1][...] = _dot(a_ref[...], b_ref[...]).astype(BF16)

    in_specs = [pl.BlockSpec((tm, kdim), lambda i, ids: (i, 0)),
                pl.BlockSpec((kdim, cols), lambda i, ids: (0, ids[which]))]
    operands = [shard_ids, n1, w_in]
    if proj is not None:
        in_specs.append(pl.BlockSpec(memory_space=pl.ANY))
        operands.append(proj)
    return pl.pallas_call(
        body, name="in_proj_%d" % which, out_shape=jax.ShapeDtypeStruct((m, out_cols), BF16),
        grid_spec=pltpu.PrefetchScalarGridSpec(
            num_scalar_prefetch=1, grid=(m // tm,), in_specs=in_specs,
            out_specs=pl.BlockSpec((tm, cols), lambda i, ids: (i, ids[which]))),
        input_output_aliases={3: 0} if proj is not None else {},
        compiler_params=_params(("parallel",)),
    )(*operands)


BIG = ("w_in", "w_out", "w_gate", "w_up", "w_down")
BIG_AXIS = dict(w_in=1, w_out=0, w_gate=1, w_up=1, w_down=0)
SMALL = ("norm_mix_w", "ret_decay_fwd", "ret_decay_bwd", "ret_norm_w", "norm_ffn_w", "norm_final_w")
ALL_WEIGHTS = ("norm_mix_w", "w_in", "ret_decay_fwd", "ret_decay_bwd", "ret_norm_w", "w_out", "norm_ffn_w",
               "w_gate", "w_up", "w_down", "norm_final_w")
SMALL_ROW = 128 * 8


def _pack_small(small):
    pieces = [jnp.reshape(small["loss"], (1,))] + [jnp.reshape(small[k], (-1,)) for k in SMALL]
    rows = []
    for p in pieces:
        pad = -p.shape[0] % (8 * SMALL_ROW)
        rows.append(jnp.reshape(jnp.pad(p, (0, pad)), (-1, SMALL_ROW)))
    return jnp.concatenate(rows, axis=0)


def _unpack_small(block, like):
    out, row = {}, 0
    for k in ("loss",) + SMALL:
        size = 1 if k == "loss" else like[k].size
        nrows = -(-size // (8 * SMALL_ROW)) * 8
        out[k] = jnp.reshape(block[row:row + nrows], (-1,))[:size]
        row += nrows
    return out


def kernel(x, norm_mix_w, w_in, ret_decay_fwd, ret_decay_bwd, ret_norm_w, w_out, norm_ffn_w, w_gate, w_up, w_down, norm_final_w, loss_target, m_norm_mix_w, m_w_in, m_ret_decay_fwd, m_ret_decay_bwd, m_ret_norm_w, m_w_out, m_norm_ffn_w, m_w_gate, m_w_up, m_w_down, m_norm_final_w, v_norm_mix_w, v_w_in, v_ret_decay_fwd, v_ret_decay_bwd, v_ret_norm_w, v_w_out, v_norm_ffn_w, v_w_gate, v_w_up, v_w_down, v_norm_final_w):
    weights = dict(norm_mix_w=norm_mix_w, w_in=w_in, ret_decay_fwd=ret_decay_fwd, ret_decay_bwd=ret_decay_bwd,
                   ret_norm_w=ret_norm_w, w_out=w_out, norm_ffn_w=norm_ffn_w, w_gate=w_gate, w_up=w_up,
                   w_down=w_down, norm_final_w=norm_final_w)
    m_in = dict(norm_mix_w=m_norm_mix_w, w_in=m_w_in, ret_decay_fwd=m_ret_decay_fwd, ret_decay_bwd=m_ret_decay_bwd,
                ret_norm_w=m_ret_norm_w, w_out=m_w_out, norm_ffn_w=m_norm_ffn_w, w_gate=m_w_gate, w_up=m_w_up,
                w_down=m_w_down, norm_final_w=m_norm_final_w)
    v_in = dict(norm_mix_w=v_norm_mix_w, w_in=v_w_in, ret_decay_fwd=v_ret_decay_fwd, ret_decay_bwd=v_ret_decay_bwd,
                ret_norm_w=v_ret_norm_w, w_out=v_w_out, norm_ffn_w=v_norm_ffn_w, w_gate=v_w_gate, w_up=v_w_up,
                w_down=v_w_down, norm_final_w=v_norm_final_w)
    pos = jnp.stack([lax.axis_index("c"), 2 * lax.axis_index("x") + lax.axis_index("y")]).astype(jnp.int32)

    own = {"w_in": _to_bf16_in_place(weights["w_in"][0], BIG_AXIS["w_in"], pos, name="cast_w_in")}
    w_in_started = _split_gather_start(own["w_in"], BIG_AXIS["w_in"])
    own["w_gate"] = _to_bf16_in_place(weights["w_gate"][0], BIG_AXIS["w_gate"], pos, name="cast_w_gate",
                                      after=w_in_started[2])
    w_gate_started = _split_start(_gather_job([own["w_gate"]], [BIG_AXIS["w_gate"]], "ici"),
                                  name="all_gather_w_gate_start")
    for k in BIG:
        if k not in own:
            own[k] = _to_bf16_in_place(weights[k][0], BIG_AXIS[k], pos, name="cast_" + k,
                                       after=w_gate_started["token"])
    cx, cy = lax.axis_index("x"), lax.axis_index("y")
    shard_ids = jnp.stack([2 * cx + cy, 2 * (1 - cx) + cy, 2 * cx + 1 - cy, 2 * (1 - cx) + 1 - cy]).astype(jnp.int32)

    dx, grad_w, small, w_in_pending = _step(
        x[0], loss_target[0], norm_mix_w, ret_decay_fwd[0], ret_decay_bwd[0], ret_norm_w, norm_ffn_w,
        norm_final_w[None, :], own, w_in_started, w_gate_started, shard_ids, pos)

    delta, new_m, new_v = {}, {}, {}

    def update(k, after):
        shape = weights[k].shape
        as2d = (lambda t: jnp.reshape(t, (-1, shape[-1])))
        grad_w[k], delta[k], new_m[k], new_v[k] = (jnp.reshape(t, shape) for t in _adamw(
            as2d(weights[k]), as2d(grad_w[k]), as2d(m_in[k]), as2d(v_in[k]), name="adamw_" + k, after=after))

    others = [k for k in BIG if k != "w_in"]
    for k in others:
        update(k, [w_in_pending["sending"]["token"]])
    _, parts = _split_wait(w_in_pending["sending"], [dx] + [delta[k] for k in others], name="grad_send_w_in_wait")

    half = _sum_chip_parts(w_in_pending["grad"], w_in_pending["received"], parts, BIG_AXIS["w_in"], pos,
                           name="grad_sum_parts_w_in")
    joining = _split_start(_join_job([half], [BIG_AXIS["w_in"]]), name="grad_join_w_in_start")

    like = {k: weights[k] for k in SMALL}
    reduced = _unpack_small(_all_reduce_small(_pack_small(small), after=[joining["token"]]), like)
    loss = reduced["loss"][0]
    for k in SMALL:
        grad_w[k] = jnp.reshape(reduced[k], (1, -1))
        update(k, [])
    (grad_w["w_in"],) = _split_wait(joining, [delta[k] for k in SMALL], name="grad_join_w_in_wait")
    update("w_in", [])

    return (loss, dx[None], *[grad_w[k] for k in ALL_WEIGHTS], *[delta[k] for k in ALL_WEIGHTS],
            *[new_m[k] for k in ALL_WEIGHTS], *[new_v[k] for k in ALL_WEIGHTS])
```

```python
import functools
import math

import numpy as np
import jax
import jax.numpy as jnp
from jax import lax
from jax.experimental import pallas as pl
from jax.experimental.pallas import tpu as pltpu

F32 = jnp.float32
BF16 = jnp.bfloat16
MESH = pl.DeviceIdType.MESH

HEAD_DIM = 128
RET_CHUNK = 128
RET_UNROLL = 8
EPS = 1e-6
DILATED_PATTERNS = ((128, 1), (512, 4), (2048, 16))
ATT_BLOCK = 256
ATT_REACH = max(w // 2 for w, _ in DILATED_PATTERNS)
ATT_NEAR = ATT_BLOCK
ATT_CLASSES = DILATED_PATTERNS[-1][1]
assert all(w // 2 <= ATT_NEAR for w, _ in DILATED_PATTERNS[:-1])
ATT_KB = -(-ATT_NEAR // ATT_BLOCK)
ATT_WINDOW = 2 * ATT_KB + 1
ATT_FAR_GROUP = 8
ATT_NEAR_GROUP = 4
MASKED = -1e30
ROW_MAX_INIT = -1e29
N_CHIPS = 4
VMEM_LIMIT_BYTES = 56 * 1024 * 1024
ELEM_BLOCK_BYTES = 2 * 1024 * 1024
WEIGHT_GRAD_GROUP = 4

ADAM_LR = 0.001
ADAM_B1 = 0.9
ADAM_B2 = 0.999
ADAM_EPS = 1e-08
ADAM_WD = 0.01
ADAM_STEP = 10


def _params(sem=None):
    return pltpu.CompilerParams(dimension_semantics=sem, vmem_limit_bytes=VMEM_LIMIT_BYTES)


def _sigmoid(x):
    return 0.5 * jnp.tanh(0.5 * x) + 0.5


class _Job:
    def __init__(self, *, ins=(), ios=(), outs=(), sems=(), start, finish):
        self.ins, self.ios, self.outs, self.sems = list(ins), list(ios), list(outs), list(sems)
        self.start, self.finish = start, finish

    def results(self):
        return [jax.ShapeDtypeStruct(a.shape, a.dtype) for a in self.ios] + self.outs


def _call(body, *, name, grid, in_specs, out_specs, out_shape, operands, scratch_shapes=(), semantics=None, jobs=(),
          after=(), updates=None):
    in_specs, out_specs, out_shape = list(in_specs), list(out_specs), list(out_shape)
    scratch_shapes = list(scratch_shapes)
    if not jobs:
        n_real = len(in_specs)

        def ordered(*refs):
            body(*refs[:n_real], *refs[n_real + len(after):])

        outs = pl.pallas_call(
            ordered if after else body, name=name, grid=grid,
            in_specs=in_specs + [pl.BlockSpec(memory_space=pl.ANY)] * len(after), out_specs=out_specs,
            out_shape=out_shape, scratch_shapes=scratch_shapes, input_output_aliases=dict(updates or {}),
            compiler_params=_params(semantics))(*operands, *after)
        return outs, []
    n_in, n_out, n_scratch = len(in_specs), len(out_specs), len(scratch_shapes)
    extra_in, extra_out, sems, aliases = [], [], [], dict(updates or {})
    for job in jobs:
        extra_in += job.ins
        for t in range(len(job.ios)):
            aliases[n_in + len(extra_in) + t] = n_out + len(extra_out) + t
        extra_in += job.ios
        extra_out += job.results()
        sems += job.sems

    def carried(*refs):
        x_in = refs[n_in:n_in + len(extra_in)]
        first_out = n_in + len(extra_in) + len(after)
        x_out = refs[first_out + n_out:first_out + n_out + len(extra_out)]
        x_sem = refs[len(refs) - len(sems):]
        views, i_in, i_out, i_sem = [], 0, 0, 0
        for job in jobs:
            data = list(x_in[i_in:i_in + len(job.ins)]) + list(x_out[i_out:i_out + len(job.results())])
            views.append((data, x_sem[i_sem:i_sem + len(job.sems)]))
            i_in += len(job.ins) + len(job.ios)
            i_out += len(job.results())
            i_sem += len(job.sems)
        steps = [pl.program_id(d) for d in range(len(grid))]

        @pl.when(functools.reduce(jnp.logical_and, [s == 0 for s in steps]))
        def _():
            for job, (data, sem) in zip(jobs, views):
                job.start(data, sem)

        body(*refs[:n_in], *refs[first_out:first_out + n_out],
             *refs[len(refs) - len(sems) - n_scratch:len(refs) - len(sems)])

        @pl.when(functools.reduce(jnp.logical_and, [s == g - 1 for s, g in zip(steps, grid)]))
        def _():
            for job, (data, sem) in zip(jobs, views):
                job.finish(data, sem)

    hbm = pl.BlockSpec(memory_space=pl.ANY)
    res = pl.pallas_call(
        carried, name=name, grid=grid, in_specs=in_specs + [hbm] * (len(extra_in) + len(after)),
        out_specs=out_specs + [hbm] * len(extra_out), out_shape=out_shape + extra_out,
        input_output_aliases=aliases, scratch_shapes=scratch_shapes + sems,
        compiler_params=_params(("arbitrary",) * len(grid)),
    )(*operands, *extra_in, *after)
    carried_results, at = [], n_out
    for job in jobs:
        carried_results.append(list(res[at:at + len(job.results())]))
        at += len(job.results())
    return list(res[:n_out]), carried_results


def _run_jobs(jobs, *, name):
    first = jobs[0]
    n_in, n_io = len(first.ins), len(first.ios)
    out_shape = first.results()
    n_sems = [len(job.sems) for job in jobs]

    def body(*refs):
        data = list(refs[:n_in]) + list(refs[n_in + n_io:n_in + n_io + len(out_shape)])
        at = n_in + n_io + len(out_shape)
        for job, ns in zip(jobs, n_sems):
            job.start(data, refs[at:at + ns])
            job.finish(data, refs[at:at + ns])
            at += ns

    hbm = pl.BlockSpec(memory_space=pl.ANY)
    return pl.pallas_call(
        body, name=name, in_specs=[hbm] * (n_in + n_io), out_specs=[hbm] * len(out_shape), out_shape=out_shape,
        input_output_aliases={n_in + t: t for t in range(n_io)},
        scratch_shapes=[s for job in jobs for s in job.sems],
    )(*first.ins, *first.ios)


class _SemaphoreGrid:
    def __init__(self, refs, shape):
        self.refs, self.shape = list(refs), tuple(shape)

    @property
    def at(self):
        return self

    def __getitem__(self, index):
        index = index if isinstance(index, tuple) else (index,)
        flat = 0
        for i, extent in zip(index, self.shape):
            flat = flat * extent + i
        return self.refs[flat]


def _semaphore_grids(job, refs):
    grids, at = [], 0
    for sem in job.sems:
        count = math.prod(sem.shape)
        grids.append(_SemaphoreGrid(refs[at:at + count], sem.shape))
        at += count
    return grids


def _split_start(job, *, name):
    arrays = job.ins + job.ios + [lax.empty(s.shape, s.dtype) for s in job.outs]
    n, ns = len(arrays), sum(math.prod(sem.shape) for sem in job.sems)

    def body(*refs):
        job.start(list(refs[:n]), _semaphore_grids(job, refs[n:n + ns]))
        refs[-1][...] = jnp.zeros_like(refs[-1])

    hbm = pl.BlockSpec(memory_space=pltpu.HBM)
    res = pl.pallas_call(
        body, name=name,
        out_shape=(*[pltpu.SemaphoreType.DMA(())] * ns, *[pltpu.HBM(a.shape, a.dtype) for a in arrays],
                   jax.ShapeDtypeStruct((8, HEAD_DIM), F32)),
        in_specs=[hbm] * n,
        out_specs=(*[pl.BlockSpec(memory_space=pltpu.SEMAPHORE)] * ns, *[hbm] * n,
                   pl.BlockSpec(memory_space=pltpu.VMEM)),
        input_output_aliases={t: ns + t for t in range(n)},
        compiler_params=pltpu.CompilerParams(has_side_effects=pltpu.SideEffectType.DATAFLOW_SIDE_EFFECTING),
    )(*[pltpu.with_memory_space_constraint(a, pltpu.HBM) for a in arrays])
    return dict(job=job, sems=list(res[:ns]), arrays=list(res[ns:ns + n]), token=res[-1])


def _split_wait(started, after, *, name):
    job, arrays, sems = started["job"], started["arrays"], started["sems"]
    n, ns = len(arrays), len(sems)

    def body(*refs):
        job.finish(list(refs[:n]), _semaphore_grids(job, refs[n:n + ns]))

    hbm = pl.BlockSpec(memory_space=pltpu.HBM)
    return pl.pallas_call(
        body, name=name, out_shape=[pltpu.HBM(a.shape, a.dtype) for a in arrays],
        in_specs=[hbm] * n + [pl.BlockSpec(memory_space=pltpu.SEMAPHORE)] * ns
        + [pl.BlockSpec(memory_space=pl.ANY)] * len(after),
        out_specs=[hbm] * n, input_output_aliases={t: t for t in range(n)},
        compiler_params=pltpu.CompilerParams(has_side_effects=pltpu.SideEffectType.DATAFLOW_SIDE_EFFECTING),
    )(*arrays, *sems, *after)


def _dot(a, b, ta=False, tb=False):
    return lax.dot_general(a, b, (((0 if ta else 1,), (1 if tb else 0,)), ((), ())),
                           preferred_element_type=F32)


def _tile(n, want):
    t = min(n, want) // 128 * 128
    while n % t:
        t -= 128
    return t


def _a_spec(ta, tm, tk):
    return pl.BlockSpec((tk, tm), lambda i, j, k: (k, i)) if ta else pl.BlockSpec((tm, tk), lambda i, j, k: (i, k))


def _b_spec(tb, tk, tn):
    return pl.BlockSpec((tn, tk), lambda i, j, k: (j, k)) if tb else pl.BlockSpec((tk, tn), lambda i, j, k: (k, j))


def _accumulate(accs, nk, products, finish):
    if nk == 1:
        finish(*products())
        return
    k = pl.program_id(2)

    @pl.when(k == 0)
    def _():
        for acc, p in zip(accs, products()):
            acc[...] = p

    if nk > 2:
        @pl.when(jnp.logical_and(k > 0, k < nk - 1))
        def _():
            for acc, p in zip(accs, products()):
                acc[...] += p

    @pl.when(k == nk - 1)
    def _():
        finish(*[acc[...] + p for acc, p in zip(accs, products())])


def _matmul(a, b, *, name, ta=False, tb=False, out_dtype=F32, residual=None, tm=1024, tn=1024, tk=2048, jobs=()):
    m, kdim = (a.shape[1], a.shape[0]) if ta else a.shape
    n = b.shape[0] if tb else b.shape[1]
    tm, tn, tk = _tile(m, tm), _tile(n, tn), _tile(kdim, tk)
    nk = kdim // tk

    def body(*refs):
        a_ref, b_ref = refs[:2]
        r_ref = refs[2] if residual is not None else None
        o_ref = refs[-1] if nk == 1 else refs[-2]

        def finish(total):
            if residual is not None:
                total = total + r_ref[...]
            o_ref[...] = total.astype(out_dtype)

        _accumulate(refs[-1:] if nk > 1 else (), nk, lambda: (_dot(a_ref[...], b_ref[...], ta, tb),), finish)

    o_spec = pl.BlockSpec((tm, tn), lambda i, j, k: (i, j))
    in_specs = [_a_spec(ta, tm, tk), _b_spec(tb, tk, tn)]
    operands = [a, b]
    if residual is not None:
        in_specs.append(o_spec)
        operands.append(residual)
    (out,), carried = _call(
        body, name=name, grid=(m // tm, n // tn, nk), in_specs=in_specs, out_specs=[o_spec],
        out_shape=[jax.ShapeDtypeStruct((m, n), out_dtype)], operands=operands,
        scratch_shapes=[pltpu.VMEM((tm, tn), F32)] * (nk > 1),
        semantics=("parallel", "parallel", "arbitrary"), jobs=jobs)
    return (out, carried) if jobs else out


def _matmul_pieces_nt(pieces, b, *, name, tm=512, tn=1024, jobs=(), after=()):
    m, kp = pieces[0].shape
    n = b.shape[0]
    tm, tn = _tile(m, tm), _tile(n, tn)
    count = len(pieces)

    def body(*refs):
        b_ref, o_ref = refs[count], refs[count + 1]
        total = _dot(refs[0][...], b_ref[:, pl.ds(0, kp)], tb=True)
        for p in range(1, count):
            total = total + _dot(refs[p][...], b_ref[:, pl.ds(p * kp, kp)], tb=True)
        o_ref[...] = total

    piece = pl.BlockSpec((tm, kp), lambda j, i: (i, 0))
    (out,), carried = _call(
        body, name=name, grid=(n // tn, m // tm),
        in_specs=[piece] * count + [pl.BlockSpec((tn, count * kp), lambda j, i: (j, 0))],
        out_specs=[pl.BlockSpec((tm, tn), lambda j, i: (i, j))],
        out_shape=[jax.ShapeDtypeStruct((m, n), F32)], operands=[*pieces, b],
        semantics=("parallel", "parallel"), jobs=jobs, after=after)
    return (out, carried) if jobs else out


def _weight_grad_pieces(a, pieces, *, name):
    tokens, m = a.shape
    np_ = pieces[0].shape[1]
    tm = 1024 if m % 1024 == 0 else _tile(m, 1408)
    tn = _tile(np_, 512)
    nb = np_ // tn
    out = None
    for first in range(0, len(pieces), WEIGHT_GRAD_GROUP):
        group = pieces[first:first + WEIGHT_GRAD_GROUP]

        def body(*refs, count=len(group)):
            t_now = pl.program_id(1) // nb
            for t in range(count):
                @pl.when(t_now == t)
                def _(t=t):
                    refs[-1][...] = _dot(refs[0][...], refs[1 + t][...], ta=True)

        def piece_spec(t):
            return pl.BlockSpec((tokens, tn), lambda i, j: (0, jnp.clip(j - t * nb, 0, nb - 1)))

        in_specs = [pl.BlockSpec((tokens, tm), lambda i, j: (0, i))] + [piece_spec(t) for t in range(len(group))]
        operands = [a, *group]
        if out is not None:
            in_specs.append(pl.BlockSpec(memory_space=pl.ANY))
            operands.append(out)
        out = pl.pallas_call(
            body, name="%s_%d" % (name, first), grid=(m // tm, nb * len(group)), in_specs=in_specs,
            out_specs=pl.BlockSpec((tm, tn), lambda i, j, first=first: (i, first * nb + j)),
            out_shape=jax.ShapeDtypeStruct((m, len(pieces) * np_), F32),
            input_output_aliases={len(operands) - 1: 0} if out is not None else {},
            compiler_params=_params(("parallel", "arbitrary")),
        )(*operands)
    return out


def _weight_grad(a, g, *, name, jobs=()):
    tokens, m = a.shape
    tm = 1024 if m % 1024 == 0 else _tile(m, 1408)
    return _matmul(a, g, name=name, ta=True, tm=tm, tn=512, tk=tokens, jobs=jobs)


def _swiglu_fwd(n2, w_gate, w_up, *, tm=1024, tn=512, tk=2048, jobs=()):
    m, kdim = n2.shape
    n = w_gate.shape[1]
    tm, tn, tk = _tile(m, tm), _tile(n, tn), _tile(kdim, tk)
    nk = kdim // tk

    def body(a_ref, g_ref, u_ref, gate_ref, up_ref, act_ref, *acc):
        def products():
            a = a_ref[...]
            return _dot(a, g_ref[...]), _dot(a, u_ref[...])

        def finish(g, u):
            gate_ref[...] = g.astype(BF16)
            up_ref[...] = u.astype(BF16)
            act_ref[...] = (g * _sigmoid(g) * u).astype(BF16)

        _accumulate(acc, nk, products, finish)

    o_spec = pl.BlockSpec((tm, tn), lambda i, j, k: (i, j))
    o_shape = jax.ShapeDtypeStruct((m, n), BF16)
    return _call(
        body, name="swiglu_fwd", grid=(m // tm, n // tn, nk),
        in_specs=[_a_spec(False, tm, tk), _b_spec(False, tk, tn), _b_spec(False, tk, tn)],
        out_specs=[o_spec] * 3, out_shape=[o_shape] * 3, operands=[n2, w_gate, w_up],
        scratch_shapes=[pltpu.VMEM((tm, tn), F32)] * (2 * (nk > 1)),
        semantics=("parallel", "parallel", "arbitrary"), jobs=jobs)


def _swiglu_bwd_act(dh2, w_down, gate, up, *, tm=1024, tn=512, tk=2048):
    m, kdim = dh2.shape
    n = w_down.shape[0]
    tm, tn, tk = _tile(m, tm), _tile(n, tn), _tile(kdim, tk)
    nk = kdim // tk

    sub = _tile(tn, 256)

    def body(a_ref, b_ref, gate_ref, up_ref, dgate_ref, dup_ref, *acc):
        def finish(dact, cols=slice(None)):
            g = gate_ref[:, cols].astype(F32)
            u = up_ref[:, cols].astype(F32)
            sg = _sigmoid(g)
            dup_ref[:, cols] = (dact * g * sg).astype(BF16)
            dgate_ref[:, cols] = (dact * u * sg * (1.0 + g * (1.0 - sg))).astype(BF16)

        if nk == 1:
            a = a_ref[...]
            for c in range(tn // sub):
                cols = pl.ds(c * sub, sub)
                finish(_dot(a, b_ref[cols, :], tb=True), cols)
        else:
            _accumulate(acc, nk, lambda: (_dot(a_ref[...], b_ref[...], tb=True),), finish)

    o_spec = pl.BlockSpec((tm, tn), lambda i, j, k: (i, j))
    o_shape = jax.ShapeDtypeStruct((m, n), BF16)
    return pl.pallas_call(
        body, name="swiglu_bwd_act", grid=(m // tm, n // tn, nk),
        in_specs=[_a_spec(False, tm, tk), _b_spec(True, tk, tn), o_spec, o_spec],
        out_specs=[o_spec] * 2, out_shape=[o_shape] * 2,
        scratch_shapes=[pltpu.VMEM((tm, tn), F32)] * (nk > 1),
        compiler_params=_params(("parallel", "parallel", "arbitrary")),
    )(dh2, w_down, gate, up)


def _swiglu_bwd_in(dgate, dup, w_gate, w_up, *, tm=1024, tn=1024, tk=1408, jobs=()):
    m, kdim = dgate.shape
    n = w_gate.shape[0]
    tm, tn, tk = _tile(m, tm), _tile(n, tn), _tile(kdim, tk)
    nk = kdim // tk

    def body(a1_ref, a2_ref, b1_ref, b2_ref, o_ref, *acc):
        def product():
            return (_dot(a1_ref[...], b1_ref[...], tb=True) + _dot(a2_ref[...], b2_ref[...], tb=True),)

        def finish(total):
            o_ref[...] = total

        _accumulate(acc, nk, product, finish)

    a_spec, b_spec = _a_spec(False, tm, tk), _b_spec(True, tk, tn)
    (out,), carried = _call(
        body, name="swiglu_bwd_in", grid=(m // tm, n // tn, nk),
        in_specs=[a_spec, a_spec, b_spec, b_spec],
        out_specs=[pl.BlockSpec((tm, tn), lambda i, j, k: (i, j))],
        out_shape=[jax.ShapeDtypeStruct((m, n), F32)], operands=[dgate, dup, w_gate, w_up],
        scratch_shapes=[pltpu.VMEM((tm, tn), F32)] * (nk > 1),
        semantics=("parallel", "parallel", "arbitrary"), jobs=jobs)
    return out, carried


def _row_block(rows, cols):
    tr = min(rows, max(16, ELEM_BLOCK_BYTES // (4 * cols) // 16 * 16))
    while rows % tr:
        tr -= 16
    return tr


def _rmsnorm_fwd(x, g, *, name, after=None):
    s, d = x.shape
    tr = _row_block(s, d)

    def body(x_ref, g_ref, *rest):
        xv = x_ref[...]
        r = lax.rsqrt(jnp.mean(xv * xv, axis=-1, keepdims=True) + EPS)
        rest[-1][...] = (xv * r * g_ref[...]).astype(BF16)

    row = pl.BlockSpec((tr, d), lambda i: (i, 0))
    in_specs = [row, pl.BlockSpec((1, d), lambda i: (0, 0))]
    operands = [x, g]
    if after is not None:
        in_specs.append(pl.BlockSpec(after.shape, lambda i: (0, 0)))
        operands.append(after)
    return pl.pallas_call(
        body, name=name, grid=(s // tr,), in_specs=in_specs,
        out_specs=row, out_shape=jax.ShapeDtypeStruct((s, d), BF16),
        compiler_params=_params(("parallel",)),
    )(*operands)


def _rmsnorm_bwd_rows(xv, gv, dy):
    r = lax.rsqrt(jnp.mean(xv * xv, axis=-1, keepdims=True) + EPS)
    xhat = xv * r
    dxh = dy * gv
    dx = r * (dxh - xhat * jnp.mean(dxh * xhat, axis=-1, keepdims=True))
    return dx, dy * xhat


def _rmsnorm_bwd(dn, x, g, skip, *, name, after=()):
    s, d = x.shape
    tr = _row_block(s, d)

    def body(dn_ref, x_ref, g_ref, skip_ref, *rest):
        dx_ref, dxb_ref, dg_ref = rest[len(after):]
        dx, dgr = _rmsnorm_bwd_rows(x_ref[...], g_ref[...], dn_ref[...])
        dx = dx + skip_ref[...]
        dx_ref[...] = dx
        dxb_ref[...] = dx.astype(BF16)

        @pl.when(pl.program_id(0) == 0)
        def _():
            dg_ref[...] = jnp.zeros_like(dg_ref)

        dg_ref[...] += jnp.sum(dgr, axis=0, keepdims=True)

    row = pl.BlockSpec((tr, d), lambda i: (i, 0))
    vec = pl.BlockSpec((1, d), lambda i: (0, 0))
    return pl.pallas_call(
        body, name=name, grid=(s // tr,),
        in_specs=[row, row, vec, row] + [pl.BlockSpec(memory_space=pl.ANY)] * len(after),
        out_specs=[row, row, vec],
        out_shape=[jax.ShapeDtypeStruct((s, d), F32), jax.ShapeDtypeStruct((s, d), BF16),
                   jax.ShapeDtypeStruct((1, d), F32)],
        compiler_params=_params(("arbitrary",)),
    )(dn, x, g, skip, *after)


def _loss_head(h2, g, target):
    s, d = h2.shape
    tr = _row_block(s, d)

    def body(h_ref, g_ref, t_ref, dh_ref, dhb_ref, dg_ref, loss_ref):
        hv = h_ref[...]
        gv = g_ref[...]
        r = lax.rsqrt(jnp.mean(hv * hv, axis=-1, keepdims=True) + EPS)
        err = hv * r * gv - t_ref[...]
        dx, dgr = _rmsnorm_bwd_rows(hv, gv, err * (1.0 / d))
        dh_ref[...] = dx
        dhb_ref[...] = dx.astype(BF16)

        @pl.when(pl.program_id(0) == 0)
        def _():
            dg_ref[...] = jnp.zeros_like(dg_ref)
            loss_ref[...] = jnp.zeros_like(loss_ref)

        dg_ref[...] += jnp.sum(dgr, axis=0, keepdims=True)
        row_loss = jnp.mean(err * err, axis=-1, keepdims=True)
        loss_ref[...] += 0.5 * jnp.sum(row_loss, axis=0, keepdims=True)

    row = pl.BlockSpec((tr, d), lambda i: (i, 0))
    vec = pl.BlockSpec((1, d), lambda i: (0, 0))
    one = pl.BlockSpec((1, 1), lambda i: (0, 0))
    return pl.pallas_call(
        body, name="loss_head", grid=(s // tr,), in_specs=[row, vec, row],
        out_specs=[row, row, vec, one],
        out_shape=[jax.ShapeDtypeStruct((s, d), F32), jax.ShapeDtypeStruct((s, d), BF16),
                   jax.ShapeDtypeStruct((1, d), F32), jax.ShapeDtypeStruct((1, 1), F32)],
        compiler_params=_params(("arbitrary",)),
    )(h2, g, target)


def _attention_bias_tables():
    k = np.arange(-ATT_KB, ATT_KB + 1)[:, None, None]
    delta = k * ATT_BLOCK + np.arange(ATT_BLOCK)[None, None, :] - np.arange(ATT_BLOCK)[None, :, None]
    dist = np.abs(delta)
    count = np.zeros(delta.shape, np.int32)
    for window, dilation in DILATED_PATTERNS:
        count += (delta % dilation == 0) & (dist <= min(window // 2, ATT_NEAR))
    logc = np.where(count > 0, np.log(np.maximum(count, 1)), MASKED)
    return dist.astype(np.float32), logc.astype(np.float32)


def _far_bias_tables(per_class):
    steps = np.abs(np.arange(per_class)[:, None] - np.arange(per_class)[None, :]) * ATT_CLASSES
    valid = (steps > ATT_NEAR) & (steps <= ATT_REACH)
    return steps.astype(np.float32), np.where(valid, 0.0, MASKED).astype(np.float32)


def _to_classes(x):
    s, cols = x.shape
    return jnp.reshape(jnp.transpose(jnp.reshape(x, (s // ATT_CLASSES, ATT_CLASSES, cols)), (1, 0, 2)), (s, cols))


def _from_classes(x):
    s, cols = x.shape
    return jnp.reshape(jnp.transpose(jnp.reshape(x, (ATT_CLASSES, s // ATT_CLASSES, cols)), (1, 0, 2)), (s, cols))


def _head_bias(bias_ref, slope, dist_ref, logc_ref):
    for kk in range(ATT_WINDOW):
        bias_ref[kk] = logc_ref[kk] - slope * dist_ref[kk]
    bias_ref[ATT_WINDOW] = jnp.full((ATT_BLOCK, ATT_BLOCK), MASKED, F32)


def _window_start(i, nq, nwin):
    return jnp.clip(i - ATT_KB, 0, nq - nwin)


def _window_block(j, i):
    rows = pl.ds(pl.multiple_of(j * ATT_BLOCK, ATT_BLOCK), ATT_BLOCK)
    kk = j - i + ATT_KB
    return rows, jnp.where(jnp.logical_and(kk >= 0, kk < ATT_WINDOW), kk, ATT_WINDOW)


def _attention_far_fwd(qkv, slopes, n_heads, jobs=(), after=()):
    s = qkv.shape[0]
    per_class = s // ATT_CLASSES
    scale = HEAD_DIM ** -0.5
    dist, logc = _far_bias_tables(per_class)

    def body(slope_ref, q_ref, k_ref, v_ref, dist_ref, logc_ref, o_ref, lse_ref):
        bias = logc_ref[...] - slope_ref[pl.program_id(0)] * dist_ref[...]
        for a in range(ATT_FAR_GROUP):
            rows = pl.ds(a * per_class, per_class)
            sc = _dot(q_ref[rows, :], k_ref[rows, :], tb=True) * scale + bias
            m = jnp.maximum(jnp.max(sc, axis=-1, keepdims=True), ROW_MAX_INIT)
            p = jnp.exp(sc - m)
            l = jnp.maximum(jnp.sum(p, axis=-1, keepdims=True), 1e-30)
            o_ref[rows, :] = (_dot(p.astype(BF16), v_ref[rows, :]) / l).astype(BF16)
            lse_ref[rows, :] = jnp.broadcast_to(m + jnp.log(l), (per_class, HEAD_DIM))

    hh = n_heads
    blk = pl.BlockSpec((ATT_FAR_GROUP * per_class, HEAD_DIM), lambda h, r: (r, h))
    table = pl.BlockSpec(dist.shape, lambda h, r: (0, 0))
    return _call(
        body, name="attention_far_fwd", grid=(hh, ATT_CLASSES // ATT_FAR_GROUP),
        in_specs=[pl.BlockSpec(memory_space=pltpu.SMEM), blk,
                  pl.BlockSpec((ATT_FAR_GROUP * per_class, HEAD_DIM), lambda h, r: (r, hh + h)),
                  pl.BlockSpec((ATT_FAR_GROUP * per_class, HEAD_DIM), lambda h, r: (r, 2 * hh + h)), table, table],
        out_specs=[blk, blk],
        out_shape=[jax.ShapeDtypeStruct((s, hh * HEAD_DIM), BF16), jax.ShapeDtypeStruct((s, hh * HEAD_DIM), F32)],
        operands=[slopes, qkv, qkv, qkv, jnp.asarray(dist), jnp.asarray(logc)],
        semantics=("parallel", "parallel"), jobs=jobs, after=after)


def _attention_fwd(proj, slopes, far_out, far_lse, n_heads, jobs=(), after=()):
    s = proj.shape[0]
    nq = s // ATT_BLOCK
    scale = HEAD_DIM ** -0.5
    dist, logc = _attention_bias_tables()

    nwin = min(ATT_WINDOW, nq)

    group = math.gcd(ATT_NEAR_GROUP, nq)

    def body(slope_ref, q_ref, k_ref, v_ref, fo_ref, fl_ref, dist_ref, logc_ref, o_ref, lse_ref, bias_ref, s_ref):
        h, step = pl.program_id(0), pl.program_id(1)

        @pl.when(step == 0)
        def _():
            _head_bias(bias_ref, slope_ref[h], dist_ref, logc_ref)

        for a in range(group):
            i = step * group + a
            mine = pl.ds(a * ATT_BLOCK, ATT_BLOCK)
            q = q_ref[mine, :]
            first = _window_start(i, nq, nwin)
            m = jnp.full((ATT_BLOCK, 1), ROW_MAX_INIT, F32)
            for b in range(nwin):
                rows, kk = _window_block(first + b, i)
                sc = _dot(q, k_ref[rows, :], tb=True) * scale + bias_ref[kk]
                s_ref[a * nwin + b] = sc
                m = jnp.maximum(m, jnp.max(sc, axis=-1, keepdims=True))
            l = jnp.zeros((ATT_BLOCK, 1), F32)
            acc = jnp.zeros((ATT_BLOCK, HEAD_DIM), F32)
            for b in range(nwin):
                rows, _ = _window_block(first + b, i)
                p = jnp.exp(s_ref[a * nwin + b] - m)
                l = l + jnp.sum(p, axis=-1, keepdims=True)
                acc = acc + _dot(p.astype(BF16), v_ref[rows, :])
            near_lse = m + jnp.log(l)
            far_lse_col = fl_ref[mine, :1]
            lse = jnp.maximum(near_lse, far_lse_col)
            lse = lse + jnp.log(jnp.exp(near_lse - lse) + jnp.exp(far_lse_col - lse))
            o_ref[mine, :] = (acc * (jnp.exp(near_lse - lse) / l)
                              + fo_ref[mine, :].astype(F32) * jnp.exp(far_lse_col - lse)).astype(BF16)
            lse_ref[mine, :] = jnp.broadcast_to(lse, (ATT_BLOCK, HEAD_DIM))

    hh = n_heads
    blk = pl.BlockSpec((group * ATT_BLOCK, HEAD_DIM), lambda h, i: (i, h))
    table = pl.BlockSpec(dist.shape, lambda h, i: (0, 0, 0))
    return _call(
        body, name="attention_fwd", grid=(hh, nq // group),
        in_specs=[pl.BlockSpec(memory_space=pltpu.SMEM), blk,
                  pl.BlockSpec((s, HEAD_DIM), lambda h, i: (0, hh + h)),
                  pl.BlockSpec((s, HEAD_DIM), lambda h, i: (0, 2 * hh + h)), blk, blk, table, table],
        out_specs=[blk, blk],
        out_shape=[jax.ShapeDtypeStruct((s, hh * HEAD_DIM), BF16), jax.ShapeDtypeStruct((s, hh * HEAD_DIM), F32)],
        operands=[slopes, proj, proj, proj, far_out, far_lse, jnp.asarray(dist), jnp.asarray(logc)],
        scratch_shapes=[pltpu.VMEM((ATT_WINDOW + 1, ATT_BLOCK, ATT_BLOCK), F32),
                        pltpu.VMEM((group * nwin, ATT_BLOCK, ATT_BLOCK), F32)],
        semantics=("parallel", "arbitrary"), jobs=jobs, after=after)


def _attention_far_bwd(qkv, slopes, out, dout, lse, n_heads):
    s = qkv.shape[0]
    per_class = s // ATT_CLASSES
    scale = HEAD_DIM ** -0.5
    dist, logc = _far_bias_tables(per_class)

    def body(slope_ref, q_ref, k_ref, v_ref, o_ref, do_ref, lse_ref, dist_ref, logc_ref, dq_ref, dk_ref, dv_ref):
        bias = logc_ref[...] - slope_ref[pl.program_id(0)] * dist_ref[...]
        for a in range(ATT_FAR_GROUP):
            rows = pl.ds(a * per_class, per_class)
            q, k, do = q_ref[rows, :], k_ref[rows, :], do_ref[rows, :]
            delta = jnp.sum(do.astype(F32) * o_ref[rows, :].astype(F32), axis=-1, keepdims=True)
            p = jnp.exp(_dot(q, k, tb=True) * scale + bias - lse_ref[rows, :1])
            dv_ref[rows, :] = _dot(p.astype(BF16), do, ta=True).astype(BF16)
            ds = (p * (_dot(do, v_ref[rows, :], tb=True) - delta) * scale).astype(BF16)
            dk_ref[rows, :] = _dot(ds, q, ta=True).astype(BF16)
            dq_ref[rows, :] = _dot(ds, k).astype(BF16)

    hh = n_heads
    blk = pl.BlockSpec((ATT_FAR_GROUP * per_class, HEAD_DIM), lambda h, r: (r, h))
    table = pl.BlockSpec(dist.shape, lambda h, r: (0, 0))
    o_shape = jax.ShapeDtypeStruct((s, hh * HEAD_DIM), BF16)
    return pl.pallas_call(
        body, name="attention_far_bwd", grid=(hh, ATT_CLASSES // ATT_FAR_GROUP),
        in_specs=[pl.BlockSpec(memory_space=pltpu.SMEM), blk,
                  pl.BlockSpec((ATT_FAR_GROUP * per_class, HEAD_DIM), lambda h, r: (r, hh + h)),
                  pl.BlockSpec((ATT_FAR_GROUP * per_class, HEAD_DIM), lambda h, r: (r, 2 * hh + h)),
                  blk, blk, blk, table, table],
        out_specs=[blk] * 3, out_shape=[o_shape] * 3,
        compiler_params=_params(("parallel", "parallel")),
    )(slopes, qkv, qkv, qkv, out, dout, lse, jnp.asarray(dist), jnp.asarray(logc))


def _attention_bwd(proj, slopes, out, lse, dmixed, far_grads, n_heads, jobs=()):
    s = proj.shape[0]
    nq = s // ATT_BLOCK
    scale = HEAD_DIM ** -0.5
    dist, logc = _attention_bias_tables()

    nwin = min(ATT_WINDOW, nq)
    group = math.gcd(ATT_NEAR_GROUP, nq)

    def body(slope_ref, q_ref, k_ref, v_ref, o_ref, do_ref, lse_ref, fdq_ref, fdk_ref, fdv_ref, dist_ref, logc_ref,
             dq_ref, dk_ref, dv_ref, dk_acc, dv_acc, bias_ref):
        h, step = pl.program_id(0), pl.program_id(1)

        @pl.when(step == 0)
        def _():
            dk_acc[...] = jnp.zeros_like(dk_acc)
            dv_acc[...] = jnp.zeros_like(dv_acc)
            _head_bias(bias_ref, slope_ref[h], dist_ref, logc_ref)

        for a in range(group):
            i = step * group + a
            mine = pl.ds(a * ATT_BLOCK, ATT_BLOCK)
            q = q_ref[mine, :]
            do = do_ref[mine, :]
            lse_col = lse_ref[mine, :1]
            delta = jnp.sum(do.astype(F32) * o_ref[mine, :].astype(F32), axis=-1, keepdims=True)
            first = _window_start(i, nq, nwin)
            dq = jnp.zeros((ATT_BLOCK, HEAD_DIM), F32)
            for b in range(nwin):
                rows, kk = _window_block(first + b, i)
                kj = k_ref[rows, :]
                vj = v_ref[rows, :]
                p = jnp.exp(_dot(q, kj, tb=True) * scale + bias_ref[kk] - lse_col)
                dv_acc[rows, :] += _dot(p.astype(BF16), do, ta=True)
                dp = _dot(do, vj, tb=True)
                ds = (p * (dp - delta) * scale).astype(BF16)
                dk_acc[rows, :] += _dot(ds, q, ta=True)
                dq = dq + _dot(ds, kj)
            dq_ref[mine, :] = (dq + fdq_ref[mine, :].astype(F32)).astype(BF16)

        @pl.when(step == nq // group - 1)
        def _():
            dk_ref[...] = (dk_acc[...] + fdk_ref[...].astype(F32)).astype(BF16)
            dv_ref[...] = (dv_acc[...] + fdv_ref[...].astype(F32)).astype(BF16)

    hh = n_heads
    blk = pl.BlockSpec((group * ATT_BLOCK, HEAD_DIM), lambda h, i: (i, h))
    col = pl.BlockSpec((s, HEAD_DIM), lambda h, i: (0, h))
    table = pl.BlockSpec(dist.shape, lambda h, i: (0, 0, 0))
    o_shape = jax.ShapeDtypeStruct((s, hh * HEAD_DIM), BF16)
    return _call(
        body, name="attention_bwd", grid=(hh, nq // group),
        in_specs=[pl.BlockSpec(memory_space=pltpu.SMEM), blk,
                  pl.BlockSpec((s, HEAD_DIM), lambda h, i: (0, hh + h)),
                  pl.BlockSpec((s, HEAD_DIM), lambda h, i: (0, 2 * hh + h)),
                  blk, blk, blk, blk, col, col, table, table],
        out_specs=[blk, col, col], out_shape=[o_shape] * 3,
        operands=[slopes, proj, proj, proj, out, dmixed, lse, *far_grads, jnp.asarray(dist), jnp.asarray(logc)],
        scratch_shapes=[pltpu.VMEM((s, HEAD_DIM), F32)] * 2
        + [pltpu.VMEM((ATT_WINDOW + 1, ATT_BLOCK, ATT_BLOCK), F32)],
        semantics=("parallel", "arbitrary"), jobs=jobs)


def _ret_decays(lgc, lga, strict_c, strict_a):
    c = RET_CHUNK
    rel = (lax.broadcasted_iota(jnp.int32, (c, c), 0) - lax.broadcasted_iota(jnp.int32, (c, c), 1)).astype(F32)
    in_c = (rel > 0) if strict_c else (rel >= 0)
    in_a = (rel < 0) if strict_a else (rel <= 0)
    mask = (jnp.where(in_c, jnp.exp(lgc * jnp.maximum(rel, 0.0)), 0.0)
            + jnp.where(in_a, jnp.exp(lga * jnp.maximum(-rel, 0.0)), 0.0))
    idx = lax.broadcasted_iota(jnp.int32, (c, 1), 0).astype(F32)
    ones = jnp.ones((1, HEAD_DIM), F32)
    dec = dict(
        rel=rel, mask=mask, idx=idx,
        a_c=jnp.exp(lgc * (idx + 1.0)), b_c=jnp.exp(lgc * (c - 1.0 - idx)), chunk_c=jnp.exp(ones * (lgc * c)),
        a_a=jnp.exp(lga * (c - idx)), b_a=jnp.exp(lga * idx), chunk_a=jnp.exp(ones * (lga * c)),
    )
    return dec


def _scaled(x, col):
    return (x.astype(F32) * col).astype(BF16)


def _chunk_rows(i):
    return pl.ds(pl.multiple_of(i * RET_CHUNK, RET_CHUNK), RET_CHUNK)


def _chunk_loop(nc, step, init, unroll=RET_UNROLL):
    group = math.gcd(nc, unroll)

    def trip(t, carry):
        for u in range(group):
            carry = step(t * group + u, carry)
        return carry

    return lax.fori_loop(0, nc // group, trip, init)


def _retention(a, b, c, lg_c, lg_a, *, strict_c, strict_a, scale, n_heads, name, gate=None, norm_w=None, jobs=(),
               heads=None, so_far=None, after=()):
    s = a[0].shape[0]
    nc = s // RET_CHUNK
    epilogue = gate is not None
    first_head, head_count = heads if heads is not None else (0, n_heads)

    def body(*refs):
        lgc_ref, lga_ref, a_ref, b_ref, c_ref = refs[:5]
        if epilogue:
            g_ref, w_ref = refs[5:7]
            o_ref, mix_ref, sa_ref = refs[-3:]
        else:
            o_ref, sa_ref = refs[-2:]
        h = first_head + pl.program_id(0)
        dec = _ret_decays(lgc_ref[h], lga_ref[h], strict_c, strict_a)

        def reverse(t, state):
            i = nc - 1 - t
            sa_ref[i] = state.astype(BF16)
            rows = _chunk_rows(i)
            return state * dec["chunk_a"] + _dot(_scaled(b_ref[rows, :], dec["b_a"]), c_ref[rows, :], ta=True)

        _chunk_loop(nc, reverse, jnp.zeros((HEAD_DIM, HEAD_DIM), F32))

        def forward(i, state):
            rows = _chunk_rows(i)
            ai, bi, ci = a_ref[rows, :], b_ref[rows, :], c_ref[rows, :]
            inner = (_dot(ai, bi, tb=True) * dec["mask"]).astype(BF16)
            out = (_dot(inner, ci) + _dot(_scaled(ai, dec["a_c"]), state.astype(BF16))
                   + _dot(_scaled(ai, dec["a_a"]), sa_ref[i])) * scale
            o_ref[rows, :] = out.astype(BF16)
            if epilogue:
                r = lax.rsqrt(jnp.mean(out * out, axis=-1, keepdims=True) + EPS)
                g = g_ref[rows, :].astype(F32)
                mix_ref[rows, :] = (out * r * w_ref[...] * (g * _sigmoid(g))).astype(BF16)
            return state * dec["chunk_c"] + _dot(_scaled(bi, dec["b_c"]), ci, ta=True)

        _chunk_loop(nc, forward, jnp.zeros((HEAD_DIM, HEAD_DIM), F32))

    def col(first):
        return pl.BlockSpec((s, HEAD_DIM), lambda h: (0, first + first_head + h))

    smem = pl.BlockSpec(memory_space=pltpu.SMEM)
    in_specs = [smem, smem, col(a[1]), col(b[1]), col(c[1])]
    operands = [lg_c, lg_a, a[0], b[0], c[0]]
    o_shape = jax.ShapeDtypeStruct((s, n_heads * HEAD_DIM), BF16)
    out_specs, out_shape = [col(0)], [o_shape]
    if epilogue:
        in_specs += [col(gate[1]), pl.BlockSpec((1, HEAD_DIM), lambda h: (0, first_head + h))]
        operands += [gate[0], norm_w]
        out_specs, out_shape = [col(0)] * 2, [o_shape] * 2
    updates = None
    if so_far is not None:
        updates = {len(operands) + t: t for t in range(len(so_far))}
        in_specs += [pl.BlockSpec(memory_space=pl.ANY)] * len(so_far)
        operands += list(so_far)
    res, carried = _call(
        body, name=name, grid=(head_count,), in_specs=in_specs, out_specs=out_specs, out_shape=out_shape,
        operands=operands, scratch_shapes=[pltpu.VMEM((nc, HEAD_DIM, HEAD_DIM), BF16)],
        semantics=("parallel",), jobs=jobs, updates=updates, after=after)
    res = res if epilogue else res[0]
    return (res, carried) if jobs else res


def _retention_decay_grads(a, b, c, e, lg_c, lg_a, *, scale, n_heads):
    s = a[0].shape[0]
    nc = s // RET_CHUNK
    cf = float(RET_CHUNK)

    def body(lgc_ref, lga_ref, a_ref, b_ref, c_ref, e_ref, gc_ref, ga_ref, sa_ref, ta_ref):
        h = pl.program_id(0)
        lgc, lga = lgc_ref[h], lga_ref[h]
        dec = _ret_decays(lgc, lga, True, True)
        rel, idx = dec["rel"], dec["idx"]
        w_c = jnp.where(rel > 0, rel * jnp.exp(lgc * jnp.maximum(rel, 0.0)), 0.0)
        w_a = jnp.where(rel < 0, -rel * jnp.exp(lga * jnp.maximum(-rel, 0.0)), 0.0)
        zero = jnp.zeros((HEAD_DIM, HEAD_DIM), F32)

        def reverse(t, carry):
            st, dst = carry
            i = nc - 1 - t
            sa_ref[i] = st.astype(BF16)
            ta_ref[i] = dst.astype(BF16)
            rows = _chunk_rows(i)
            bi, ci = b_ref[rows, :], c_ref[rows, :]
            st_new = st * dec["chunk_a"] + _dot(_scaled(bi, dec["b_a"]), ci, ta=True)
            dst_new = (cf * st + dst) * dec["chunk_a"] + _dot(_scaled(bi, idx * dec["b_a"]), ci, ta=True)
            return st_new, dst_new

        _chunk_loop(nc, reverse, (zero, zero))

        def forward(i, carry):
            st, dst, acc_c, acc_a = carry
            rows = _chunk_rows(i)
            ai, bi, ci = a_ref[rows, :], b_ref[rows, :], c_ref[rows, :]
            ev = e_ref[rows, :].astype(F32)
            pg = _dot(ai, bi, tb=True) * _dot(e_ref[rows, :], ci, tb=True)
            a_c, a_a = _scaled(ai, dec["a_c"]), _scaled(ai, dec["a_a"])
            inter_c = _dot(a_c, st.astype(BF16)) * (idx + 1.0) + _dot(a_c, dst.astype(BF16))
            inter_a = _dot(a_a, sa_ref[i]) * (cf - idx) + _dot(a_a, ta_ref[i])
            acc_c = acc_c + jnp.sum(pg * w_c, axis=0, keepdims=True) + jnp.sum(inter_c * ev, axis=0, keepdims=True)
            acc_a = acc_a + jnp.sum(pg * w_a, axis=0, keepdims=True) + jnp.sum(inter_a * ev, axis=0, keepdims=True)
            st_new = st * dec["chunk_c"] + _dot(_scaled(bi, dec["b_c"]), ci, ta=True)
            dst_new = ((cf * st + dst) * dec["chunk_c"]
                       + _dot(_scaled(bi, (cf - 1.0 - idx) * dec["b_c"]), ci, ta=True))
            return st_new, dst_new, acc_c, acc_a

        row = jnp.zeros((1, HEAD_DIM), F32)
        _, _, acc_c, acc_a = _chunk_loop(nc, forward, (zero, zero, row, row))
        gc_ref[...] = jnp.broadcast_to(jnp.sum(acc_c, axis=-1, keepdims=True) * scale, gc_ref.shape)
        ga_ref[...] = jnp.broadcast_to(jnp.sum(acc_a, axis=-1, keepdims=True) * scale, ga_ref.shape)

    def col(first):
        return pl.BlockSpec((s, HEAD_DIM), lambda h: (0, first + h))

    smem = pl.BlockSpec(memory_space=pltpu.SMEM)
    o_spec = pl.BlockSpec((1, 8, HEAD_DIM), lambda h: (h, 0, 0))
    o_shape = jax.ShapeDtypeStruct((n_heads, 8, HEAD_DIM), F32)
    gc, ga = pl.pallas_call(
        body, name="retention_decay_grads", grid=(n_heads,),
        in_specs=[smem, smem, col(a[1]), col(b[1]), col(c[1]), col(e[1])],
        out_specs=[o_spec] * 2, out_shape=[o_shape] * 2,
        scratch_shapes=[pltpu.VMEM((nc, HEAD_DIM, HEAD_DIM), BF16)] * 2,
        compiler_params=_params(("parallel",)),
    )(lg_c, lg_a, a[0], b[0], c[0], e[0])
    return gc[:, 0, 0], ga[:, 0, 0]


def _ret_gate_bwd(dmixed, first_col, out, proj, gate_col, norm_w, n_heads):
    s = out.shape[0]
    tr = _row_block(s, 8 * HEAD_DIM)

    def body(dm_ref, o_ref, g_ref, w_ref, do_ref, dg_ref, dw_ref):
        dm = dm_ref[...].astype(F32)
        ov = o_ref[...].astype(F32)
        g = g_ref[...].astype(F32)
        w = w_ref[...]
        r = lax.rsqrt(jnp.mean(ov * ov, axis=-1, keepdims=True) + EPS)
        ohat = ov * r
        sg = _sigmoid(g)
        silu = g * sg
        dg_ref[...] = (dm * ohat * w * sg * (1.0 + g * (1.0 - sg))).astype(BF16)
        dohat = dm * w * silu
        do_ref[...] = (r * (dohat - ohat * jnp.mean(dohat * ohat, axis=-1, keepdims=True))).astype(BF16)

        @pl.when(pl.program_id(1) == 0)
        def _():
            dw_ref[...] = jnp.zeros_like(dw_ref)

        dw_ref[...] += jnp.sum(dm * ohat * silu, axis=0, keepdims=True)

    def blk(first):
        return pl.BlockSpec((tr, HEAD_DIM), lambda h, i: (i, first + h))

    vec = pl.BlockSpec((1, HEAD_DIM), lambda h, i: (0, h))
    o_shape = jax.ShapeDtypeStruct((s, n_heads * HEAD_DIM), BF16)
    return pl.pallas_call(
        body, name="ret_gate_bwd", grid=(n_heads, s // tr),
        in_specs=[blk(first_col), blk(0), blk(gate_col), vec],
        out_specs=[blk(0), blk(0), vec],
        out_shape=[o_shape, o_shape, jax.ShapeDtypeStruct((1, n_heads * HEAD_DIM), F32)],
        compiler_params=_params(("parallel", "arbitrary")),
    )(dmixed, out, proj, norm_w)


def _step(x, target, norm_mix_w, ret_decay_fwd, ret_decay_bwd, ret_norm_w, norm_ffn_w, norm_final_w, own,
          w_in_started, queued, shard_ids, pos):
    d = x.shape[1]
    nh = d // (2 * HEAD_DIM)
    scale = HEAD_DIM ** -0.5
    slopes = jnp.exp2(-8.0 * jnp.arange(1, nh + 1, dtype=F32) / nh)
    lg_f = -jnp.exp(ret_decay_fwd)
    lg_b = -jnp.exp(ret_decay_bwd)
    q_r, k_r, v_r, g_r = 3 * nh, 4 * nh, 5 * nh, 6 * nh
    ax = BIG_AXIS

    def gather(names, arrays, stage, part=None, peers=(0, 1, 2)):
        return _gather_job(arrays, [ax[k] for k in names], stage, part, peers)

    def add_halves(k, g, received):
        return _add_halves(g, received, ax[k], pos, name="grad_add_halves_" + k)

    def sum_parts(k, g, received, parts):
        return _sum_chip_parts(g, received, parts, ax[k], pos, name="grad_sum_parts_" + k)

    sems, w_in, token = w_in_started
    n1 = _rmsnorm_fwd(x, norm_mix_w, name="norm_mix_fwd", after=token)
    proj = _in_proj_part(n1, w_in, None, shard_ids, 0, out_cols=w_in.shape[1])
    for peer in range(3):
        behind = [proj] + ([own[k] for k in ("w_up", "w_down")] if peer == 0 else [])
        w_in = _split_gather_wait(sems, w_in, ax["w_in"], peer, behind)
        (w_in,) = _run_jobs([gather(["w_in"], [w_in], "d2d", peers=(peer,))], name="all_gather_w_in_sibling_%d" % peer)
        proj = _in_proj_part(n1, w_in, proj, shard_ids, 1 + peer, out_cols=w_in.shape[1])
    (w_gate,) = _split_wait(queued["w_gate"], [proj], name="all_gather_w_gate_wait")
    qkv_classes = _to_classes(proj[:, :3 * nh * HEAD_DIM])
    (ret, ret_mixed), [[w_gate]] = _retention(
        (proj, q_r), (proj, k_r), (proj, v_r), lg_f, lg_b, strict_c=False, strict_a=True, scale=scale, n_heads=nh,
        name="retention_fwd_first", gate=(proj, g_r), norm_w=ret_norm_w, heads=(0, nh // 2),
        jobs=[gather(["w_gate"], [w_gate], "d2d")])
    (far_out, far_lse), [[w_up]] = _attention_far_fwd(
        qkv_classes, slopes, nh, jobs=[gather(["w_up"], [own["w_up"]], "ici", (0, 1, 4))], after=[ret_mixed])
    ret, ret_mixed = _retention(
        (proj, q_r), (proj, k_r), (proj, v_r), lg_f, lg_b, strict_c=False, strict_a=True, scale=scale, n_heads=nh,
        name="retention_fwd_second", gate=(proj, g_r), norm_w=ret_norm_w, heads=(nh // 2, nh - nh // 2),
        so_far=[ret, ret_mixed], after=[far_out])
    (w_out,) = _split_wait(queued["w_out"], [ret_mixed], name="all_gather_w_out_wait")
    (attn, lse), [[w_out], [w_up]] = _attention_fwd(
        proj, slopes, _from_classes(far_out), _from_classes(far_lse), nh, after=[ret_mixed],
        jobs=[gather(["w_out"], [w_out], "d2d"),
              _fuse(gather(["w_up"], [w_up], "d2d", (0, 1, 4)), gather(["w_up"], [w_up], "ici", (1, 2, 4)))])
    mixed = jnp.concatenate([attn, ret_mixed], axis=1)
    h1, [[w_up]] = _matmul(
        mixed, w_out, name="out_proj", residual=x,
        jobs=[_fuse(gather(["w_up"], [w_up], "d2d", (1, 2, 4)), gather(["w_up"], [w_up], "ici", (3, 1, 4)))])
    up_sibling = _split_start(gather(["w_up"], [w_up], "d2d", (3, 1, 4)), name="all_gather_w_up_sibling_start")
    n2 = _rmsnorm_fwd(h1, norm_ffn_w, name="norm_ffn_fwd", after=up_sibling["token"])
    (w_up,) = _split_wait(up_sibling, [n2], name="all_gather_w_up_sibling_wait")
    (gate, up, act), [[w_down]] = _swiglu_fwd(n2, w_gate, w_up, jobs=[gather(["w_down"], [own["w_down"]], "ici")])
    (w_down,) = _run_jobs([gather(["w_down"], [w_down], "d2d")], name="all_gather_w_down_sibling")
    h2 = _matmul(act, w_down, name="down_proj", residual=h1, tk=2816)
    dh2, dh2_b, d_norm_final, loss = _loss_head(h2, norm_final_w, target)

    dgate, dup = _swiglu_bwd_act(dh2_b, w_down, gate, up)
    g_down = _weight_grad(act, dh2_b, name="grad_w_down")
    g_gate, [[r_down]] = _weight_grad(n2, dgate, name="grad_w_gate", jobs=[_exchange_job([g_down], [ax["w_down"]])])
    s_down = add_halves("w_down", g_down, r_down)
    g_up, [[r_gate], [p_down]] = _weight_grad(
        n2, dup, name="grad_w_up",
        jobs=[_exchange_job([g_gate], [ax["w_gate"]]), _send_sums_job([s_down], [ax["w_down"]], (0, 1, 2))])
    s_gate = add_halves("w_gate", g_gate, r_gate)
    dn2, [[r_up], [p_gate], [p_down]] = _swiglu_bwd_in(
        dgate, dup, w_gate, w_up,
        jobs=[_exchange_job([g_up], [ax["w_up"]]), _send_sums_job([s_gate], [ax["w_gate"]]),
              _send_sums_job([s_down], [ax["w_down"]], (1, 1, 2), landing=[p_down])])
    h_down = sum_parts("w_down", g_down, r_down, p_down)
    s_up = add_halves("w_up", g_up, r_up)
    h_gate = sum_parts("w_gate", g_gate, r_gate, p_gate)
    dh1, dh1_b, d_norm_ffn = _rmsnorm_bwd(dn2, h1, norm_ffn_w, dh2, name="norm_ffn_bwd")

    dmixed, [[gr_down], [p_up]] = _matmul(
        dh1_b, w_out, name="out_proj_bwd", tb=True, out_dtype=BF16,
        jobs=[_join_job([h_down], [ax["w_down"]]), _send_sums_job([s_up], [ax["w_up"]], (0, 1, 4))])
    far_in = [_to_classes(t) for t in (attn, dmixed[:, :nh * HEAD_DIM], lse)]
    g_out, [[p_up]] = _weight_grad(mixed, dh1_b, name="grad_w_out",
                                   jobs=[_send_sums_job([s_up], [ax["w_up"]], (1, 1, 4), landing=[p_up])])
    d_ret, dg_r, d_ret_norm = _ret_gate_bwd(dmixed, nh, ret, proj, g_r, ret_norm_w, nh)
    far_grads = _attention_far_bwd(qkv_classes, slopes, *far_in, nh)
    far_grads = [_from_classes(t) for t in far_grads]
    dq_r, [[gr_gate], [p_up]] = _retention(
        (d_ret, 0), (proj, v_r), (proj, k_r), lg_f, lg_b, strict_c=False, strict_a=True, scale=scale, n_heads=nh,
        name="retention_dq",
        jobs=[_join_job([h_gate], [ax["w_gate"]]), _send_sums_job([s_up], [ax["w_up"]], (2, 1, 4), landing=[p_up])])
    (dq_a, dk_a, dv_a), [[p_up], [r_out]] = _attention_bwd(
        proj, slopes, attn, lse, dmixed, far_grads, nh,
        jobs=[_send_sums_job([s_up], [ax["w_up"]], (3, 1, 4), landing=[p_up]),
              _exchange_job([g_out], [ax["w_out"]])])
    s_out = add_halves("w_out", g_out, r_out)
    h_up = sum_parts("w_up", g_up, r_up, p_up)
    dv_r, [[p_out], [gr_up]] = _retention(
        (proj, k_r), (proj, q_r), (d_ret, 0), lg_b, lg_f, strict_c=True, strict_a=False, scale=scale, n_heads=nh,
        name="retention_dv", jobs=[_send_sums_job([s_out], [ax["w_out"]]), _join_job([h_up], [ax["w_up"]])])
    h_out = sum_parts("w_out", g_out, r_out, p_out)
    dk_r, [[gr_out]] = _retention(
        (proj, v_r), (d_ret, 0), (proj, q_r), lg_b, lg_f, strict_c=True, strict_a=False, scale=scale, n_heads=nh,
        name="retention_dk", jobs=[_join_job([h_out], [ax["w_out"]])])
    dlg_f, dlg_b = _retention_decay_grads((proj, q_r), (proj, k_r), (proj, v_r), (d_ret, 0), lg_f, lg_b,
                                          scale=scale, n_heads=nh)
    dproj = [dq_a, dk_a, dv_a, dq_r, dk_r, dv_r, dg_r]
    g_in = _weight_grad_pieces(n1, dproj, name="grad_w_in")
    exchange = _split_start(_exchange_job([g_in], [ax["w_in"]]), name="grad_exchange_w_in_start")
    dn1 = _matmul_pieces_nt(dproj, w_in, name="in_proj_bwd", after=[exchange["token"]])
    g_in, r_in = _split_wait(exchange, [dn1], name="grad_exchange_w_in_wait")
    s_in = add_halves("w_in", g_in, r_in)
    sending = _split_start(_send_sums_job([s_in], [ax["w_in"]]), name="grad_send_w_in_start")
    dx, _, d_norm_mix = _rmsnorm_bwd(dn1, x, norm_mix_w, dh1, name="norm_mix_bwd", after=[sending["token"]])

    small = dict(loss=loss[0, 0], norm_mix_w=d_norm_mix, ret_decay_fwd=dlg_f * lg_f, ret_decay_bwd=dlg_b * lg_b,
                 ret_norm_w=d_ret_norm, norm_ffn_w=d_norm_ffn, norm_final_w=d_norm_final)
    return (dx, dict(w_out=gr_out, w_gate=gr_gate, w_up=gr_up, w_down=gr_down), small,
            dict(sending=sending, grad=g_in, received=r_in))


def _mesh_position():
    x, y, c = lax.axis_index("x"), lax.axis_index("y"), lax.axis_index("c")
    chips = [(1 - x, y), (x, 1 - y), (1 - x, 1 - y)]
    return x, y, c, chips


def _span(span):
    if span is None:
        return slice(None)
    start, size, step = span
    return pl.ds(start if isinstance(start, int) else pl.multiple_of(start, step), size)


def _part_rows(part, rows):
    first, count, of = part
    return first * (rows // of), count * (rows // of), rows // of


def _region(ref, axis, shard, half, shard_size, half_size, part=None, total_rows=None):
    along = None if shard is None else (shard * shard_size, shard_size, shard_size)
    other = None if half is None else (half * half_size, half_size, half_size)
    rows, cols = (other, along) if axis == 1 else (along, other)
    if part is not None:
        start, size, _ = rows if rows is not None else (0, total_rows, None)
        offset, size, step = _part_rows(part, size)
        rows = (start + offset, size, step)
    return ref.at[_span(rows), _span(cols)]


def _fuse(first, second):
    assert not (first.ins or first.outs or second.ins or second.outs)
    assert len(first.ios) == len(second.ios) and all(a is b for a, b in zip(first.ios, second.ios))
    cut = len(first.sems)

    def start(refs, sems):
        first.start(refs, sems[:cut])
        second.start(refs, sems[cut:])

    def finish(refs, sems):
        first.finish(refs, sems[:cut])
        second.finish(refs, sems[cut:])

    return _Job(ios=first.ios, sems=first.sems + second.sems, start=start, finish=finish)


def _gather_job(full, axes, stage, part=None, peers=(0, 1, 2)):
    n = len(full)

    def copies(refs, sems):
        send_sem, recv_sem = sems
        x, y, c, chips = _mesh_position()
        me = 2 * x + y

        def copy(w, k, shard, half, target):
            rows_cols = full[w].shape
            place = _region(refs[w], axes[w], shard, half, rows_cols[axes[w]] // N_CHIPS, rows_cols[1 - axes[w]] // 2,
                            part)
            return pltpu.make_async_remote_copy(
                src_ref=place, dst_ref=place, send_sem=send_sem.at[w, k], recv_sem=recv_sem.at[w, k],
                device_id=target, device_id_type=MESH)

        def sent(w, k):
            if stage == "ici":
                return copy(w, k, me, c, (chips[k][0], chips[k][1], c))
            return copy(w, k, 2 * chips[k][0] + chips[k][1], c, (x, y, 1 - c))

        def landed(w, k):
            return copy(w, k, 2 * chips[k][0] + chips[k][1], c if stage == "ici" else 1 - c, (x, y, 1 - c))

        return sent, landed

    def start(refs, sems):
        sent, _ = copies(refs, sems)
        for w in range(n):
            for k in peers:
                sent(w, k).start()

    def finish(refs, sems):
        sent, landed = copies(refs, sems)
        for w in range(n):
            for k in peers:
                landed(w, k).wait_recv()
                sent(w, k).wait_send()

    return _Job(ios=full, sems=[pltpu.SemaphoreType.DMA((n, 3))] * 2, start=start, finish=finish)


def _exchange_job(grads, axes):
    n = len(grads)

    def half_shape(w):
        return tuple(d // 2 if a != axes[w] else d for a, d in enumerate(grads[w].shape))

    def copy(refs, sems, w):
        x, y, c, _ = _mesh_position()
        return pltpu.make_async_remote_copy(
            src_ref=_region(refs[w], axes[w], None, 1 - c, 0, half_shape(w)[1 - axes[w]]), dst_ref=refs[n + w],
            send_sem=sems[0].at[w], recv_sem=sems[1].at[w], device_id=(x, y, 1 - c), device_id_type=MESH)

    def start(refs, sems):
        for w in range(n):
            copy(refs, sems, w).start()

    def finish(refs, sems):
        for w in range(n):
            copy(refs, sems, w).wait()

    return _Job(ins=grads, outs=[jax.ShapeDtypeStruct(half_shape(w), F32) for w in range(n)],
                sems=[pltpu.SemaphoreType.DMA((n,))] * 2, start=start, finish=finish)


def _half_block_spec(axis, block, half_blocks, use_half):
    if axis == 1:
        if use_half:
            return pl.BlockSpec(block, lambda i, pos: (pos[0] * half_blocks + i, 0))
        return pl.BlockSpec(block, lambda i, pos: (i, 0))
    if use_half:
        return pl.BlockSpec(block, lambda i, pos: (i, pos[0]))
    return pl.BlockSpec(block, lambda i, pos: (i, 0))


def _add_halves(grad, received, axis, pos, *, name):
    rows, cols = received.shape
    tr = _row_block(rows, cols)
    nb = rows // tr

    def body(pos_ref, g_ref, r_ref, o_ref):
        o_ref[...] = (g_ref[...] + r_ref[...]).astype(BF16)

    blk = (tr, cols)
    return pl.pallas_call(
        body, name=name, out_shape=jax.ShapeDtypeStruct((rows, cols), BF16),
        grid_spec=pltpu.PrefetchScalarGridSpec(
            num_scalar_prefetch=1, grid=(nb,),
            in_specs=[_half_block_spec(axis, blk, nb, True), _half_block_spec(axis, blk, nb, False)],
            out_specs=_half_block_spec(axis, blk, nb, False)),
        compiler_params=_params(("parallel",)),
    )(pos, grad, received)


def _send_sums_job(sums, axes, part=None, landing=None):
    n = len(sums)

    def part_shape(w):
        return tuple(d // N_CHIPS if a == axes[w] else d for a, d in enumerate(sums[w].shape))

    def copy(refs, sems, w, k):
        x, y, c, chips = _mesh_position()
        shard = 2 * chips[k][0] + chips[k][1]
        rows = part_shape(w)[0]
        dst = refs[n + w].at[k]
        if part is not None:
            offset, size, _ = _part_rows(part, rows)
            dst = refs[n + w].at[k, pl.ds(offset, size), :]
        return pltpu.make_async_remote_copy(
            src_ref=_region(refs[w], axes[w], shard, None, part_shape(w)[axes[w]], 0, part, rows), dst_ref=dst,
            send_sem=sems[0].at[w, k], recv_sem=sems[1].at[w, k],
            device_id=(chips[k][0], chips[k][1], c), device_id_type=MESH)

    def start(refs, sems):
        for w in range(n):
            for k in range(3):
                copy(refs, sems, w, k).start()

    def finish(refs, sems):
        for w in range(n):
            for k in range(3):
                copy(refs, sems, w, k).wait()

    sems = [pltpu.SemaphoreType.DMA((n, 3))] * 2
    if landing is not None:
        return _Job(ins=sums, ios=landing, sems=sems, start=start, finish=finish)
    return _Job(ins=sums, outs=[jax.ShapeDtypeStruct((3,) + part_shape(w), BF16) for w in range(n)],
                sems=sems, start=start, finish=finish)


def _sum_chip_parts(grad, received, parts, axis, pos, *, name):
    _, rows, cols = parts.shape
    tr = _row_block(rows, cols)
    nb = rows // tr
    blk = (tr, cols)

    def body(pos_ref, g_ref, r_ref, p_ref, o_ref):
        total = g_ref[...] + r_ref[...]
        for k in range(3):
            total = total + p_ref[k].astype(F32)
        o_ref[...] = total

    if axis == 1:
        g_spec = pl.BlockSpec(blk, lambda i, pos: (pos[0] * nb + i, pos[1]))
        r_spec = pl.BlockSpec(blk, lambda i, pos: (i, pos[1]))
        o_spec = pl.BlockSpec(blk, lambda i, pos: (pos[0] * nb + i, 0))
        shard_shape = (2 * rows, cols)
    else:
        g_spec = pl.BlockSpec(blk, lambda i, pos: (pos[1] * nb + i, pos[0]))
        r_spec = pl.BlockSpec(blk, lambda i, pos: (pos[1] * nb + i, 0))
        o_spec = pl.BlockSpec(blk, lambda i, pos: (i, pos[0]))
        shard_shape = (rows, 2 * cols)
    return pl.pallas_call(
        body, name=name, out_shape=jax.ShapeDtypeStruct(shard_shape, F32),
        grid_spec=pltpu.PrefetchScalarGridSpec(
            num_scalar_prefetch=1, grid=(nb,),
            in_specs=[g_spec, r_spec, pl.BlockSpec((3,) + blk, lambda i, pos: (0, i, 0))],
            out_specs=o_spec),
        compiler_params=_params(("parallel",)),
    )(pos, grad, received, parts)


def _join_job(shards, axes):
    n = len(shards)

    def copy(refs, sems, w, other):
        x, y, c, _ = _mesh_position()
        place = _region(refs[w], axes[w], None, 1 - c if other else c, 0, shards[w].shape[1 - axes[w]] // 2)
        return pltpu.make_async_remote_copy(
            src_ref=place, dst_ref=place, send_sem=sems[0].at[w], recv_sem=sems[1].at[w],
            device_id=(x, y, 1 - c), device_id_type=MESH)

    def start(refs, sems):
        for w in range(n):
            copy(refs, sems, w, False).start()

    def finish(refs, sems):
        for w in range(n):
            copy(refs, sems, w, True).wait_recv()
            copy(refs, sems, w, False).wait_send()

    return _Job(ios=shards, sems=[pltpu.SemaphoreType.DMA((n,))] * 2, start=start, finish=finish)


def _all_reduce_small(vec, after=()):
    rows, cols = vec.shape

    def body(v_ref, *rest):
        o_ref, land_ref, send_sem, recv_sem = rest[len(after):]
        x, y, c, _ = _mesh_position()
        me = 4 * x + 2 * y + c
        land_ref[me] = v_ref[...]
        copies = []
        for k in range(1, 8):
            px, py, pc = x ^ (k >> 2), y ^ ((k >> 1) & 1), c ^ (k & 1)
            copies.append(pltpu.make_async_remote_copy(
                src_ref=v_ref, dst_ref=land_ref.at[me], send_sem=send_sem.at[k], recv_sem=recv_sem.at[k],
                device_id=(px, py, pc), device_id_type=MESH))
        for cp in copies:
            cp.start()
        for k in range(1, 8):
            peer = me ^ k
            pltpu.make_async_remote_copy(
                src_ref=v_ref, dst_ref=land_ref.at[peer], send_sem=send_sem.at[k], recv_sem=recv_sem.at[k],
                device_id=(x, y, c), device_id_type=MESH).wait_recv()
        for cp in copies:
            cp.wait_send()
        total = land_ref[0]
        for k in range(1, 8):
            total = total + land_ref[k]
        o_ref[...] = total

    vmem = pl.BlockSpec(memory_space=pltpu.VMEM)
    return pl.pallas_call(
        body, name="all_reduce_small", in_specs=[vmem] + [pl.BlockSpec(memory_space=pl.ANY)] * len(after),
        out_specs=vmem, out_shape=jax.ShapeDtypeStruct((rows, cols), F32),
        scratch_shapes=[pltpu.VMEM((8, rows, cols), F32), pltpu.SemaphoreType.DMA((8,)), pltpu.SemaphoreType.DMA((8,))],
    )(vec, *after)


def _adamw(w, g, m, v, *, name, after=()):
    rows, cols = w.shape
    tr = _row_block(rows, cols) if rows % 8 == 0 else rows
    bc1 = 1.0 - ADAM_B1 ** ADAM_STEP
    bc2 = 1.0 - ADAM_B2 ** ADAM_STEP

    def body(w_ref, g_ref, m_ref, v_ref, *rest):
        go_ref, d_ref, mo_ref, vo_ref = rest[len(after):]
        gv = g_ref[...]
        go_ref[...] = gv
        mn = ADAM_B1 * m_ref[...] + (1.0 - ADAM_B1) * gv
        vn = ADAM_B2 * v_ref[...] + (1.0 - ADAM_B2) * (gv * gv)
        mo_ref[...] = mn
        vo_ref[...] = vn
        d_ref[...] = -ADAM_LR * ((mn / bc1) / (jnp.sqrt(vn / bc2) + ADAM_EPS) + ADAM_WD * w_ref[...])

    blk = pl.BlockSpec((tr, cols), lambda i: (i, 0))
    shape = jax.ShapeDtypeStruct((rows, cols), F32)
    return pl.pallas_call(
        body, name=name, grid=(rows // tr,), in_specs=[blk] * 4 + [pl.BlockSpec(memory_space=pl.ANY)] * len(after),
        out_specs=[blk] * 4, out_shape=[shape] * 4, compiler_params=_params(("parallel",)),
    )(w, g, m, v, *after)


def _to_bf16_in_place(w, axis, pos, *, name, after=None):
    rows, cols = w.shape
    tr = _row_block(rows, cols)
    nb = rows // tr

    def body(pos_ref, w_ref, *rest):
        rest[-1][...] = w_ref[...].astype(BF16)

    if axis == 1:
        o_spec = pl.BlockSpec((tr, cols), lambda i, pos: (i, pos[1]))
        full_shape = (rows, N_CHIPS * cols)
    else:
        o_spec = pl.BlockSpec((tr, cols), lambda i, pos: (pos[1] * nb + i, 0))
        full_shape = (N_CHIPS * rows, cols)
    in_specs = [pl.BlockSpec((tr, cols), lambda i, pos: (i, 0))]
    operands = [pos, w]
    if after is not None:
        in_specs.append(pl.BlockSpec(after.shape, lambda i, pos: (0, 0)))
        operands.append(after)
    return pl.pallas_call(
        body, name=name, out_shape=jax.ShapeDtypeStruct(full_shape, BF16),
        grid_spec=pltpu.PrefetchScalarGridSpec(num_scalar_prefetch=1, grid=(nb,), in_specs=in_specs, out_specs=o_spec),
        compiler_params=_params(("parallel",)),
    )(*operands)


def _split_gather_start(full, axis):
    rows_cols = full.shape

    def body(buf_ref, *rest):
        sems = rest[:6]
        token_ref = rest[7]
        x, y, c, chips = _mesh_position()
        place = _region(buf_ref, axis, 2 * x + y, c, rows_cols[axis] // N_CHIPS, rows_cols[1 - axis] // 2)
        for k in range(3):
            pltpu.make_async_remote_copy(
                src_ref=place, dst_ref=place, send_sem=sems[k], recv_sem=sems[3 + k],
                device_id=(chips[k][0], chips[k][1], c), device_id_type=MESH).start()
        token_ref[...] = jnp.zeros_like(token_ref)

    hbm = pl.BlockSpec(memory_space=pltpu.HBM)
    sem = pl.BlockSpec(memory_space=pltpu.SEMAPHORE)
    res = pl.pallas_call(
        body, name="all_gather_w_in_start",
        out_shape=(*[pltpu.SemaphoreType.DMA(())] * 6, pltpu.HBM(full.shape, full.dtype),
                   jax.ShapeDtypeStruct((8, HEAD_DIM), F32)),
        in_specs=(hbm,), out_specs=(*[sem] * 6, hbm, pl.BlockSpec(memory_space=pltpu.VMEM)),
        input_output_aliases={0: 6},
        compiler_params=pltpu.CompilerParams(has_side_effects=pltpu.SideEffectType.DATAFLOW_SIDE_EFFECTING),
    )(pltpu.with_memory_space_constraint(full, pltpu.HBM))
    return list(res[:6]), res[6], res[7]


def _split_gather_wait(sems, full, axis, peer, after):
    rows_cols = full.shape

    def body(buf_ref, send_sem, recv_sem, *rest):
        x, y, c, chips = _mesh_position()

        def copy(shard):
            place = _region(buf_ref, axis, shard, c, rows_cols[axis] // N_CHIPS, rows_cols[1 - axis] // 2)
            return pltpu.make_async_remote_copy(
                src_ref=place, dst_ref=place, send_sem=send_sem, recv_sem=recv_sem,
                device_id=(chips[peer][0], chips[peer][1], c), device_id_type=MESH)

        copy(2 * x + y).wait_send()
        copy(2 * chips[peer][0] + chips[peer][1]).wait_recv()

    hbm = pl.BlockSpec(memory_space=pltpu.HBM)
    sem = pl.BlockSpec(memory_space=pltpu.SEMAPHORE)
    return pl.pallas_call(
        body, name="all_gather_w_in_wait_%d" % peer, out_shape=pltpu.HBM(full.shape, full.dtype),
        in_specs=(hbm, sem, sem, *[pl.BlockSpec(memory_space=pl.ANY)] * len(after)), out_specs=hbm,
        input_output_aliases={0: 0},
        compiler_params=pltpu.CompilerParams(has_side_effects=pltpu.SideEffectType.DATAFLOW_SIDE_EFFECTING),
    )(full, sems[peer], sems[3 + peer], *after)


def _in_proj_part(n1, w_in, proj, shard_ids, which, *, out_cols):
    m, kdim = n1.shape
    cols = out_cols // N_CHIPS
    tm = _tile(m, 1024)

    def body(ids_ref, a_ref, b_ref, *rest):
        rest[-1][...] = _dot(a_ref[...], b_ref[...]).astype(BF16)

    in_specs = [pl.BlockSpec((tm, kdim), lambda i, ids: (i, 0)),
                pl.BlockSpec((kdim, cols), lambda i, ids: (0, ids[which]))]
    operands = [shard_ids, n1, w_in]
    if proj is not None:
        in_specs.append(pl.BlockSpec(memory_space=pl.ANY))
        operands.append(proj)
    return pl.pallas_call(
        body, name="in_proj_%d" % which, out_shape=jax.ShapeDtypeStruct((m, out_cols), BF16),
        grid_spec=pltpu.PrefetchScalarGridSpec(
            num_scalar_prefetch=1, grid=(m // tm,), in_specs=in_specs,
            out_specs=pl.BlockSpec((tm, cols), lambda i, ids: (i, ids[which]))),
        input_output_aliases={3: 0} if proj is not None else {},
        compiler_params=_params(("parallel",)),
    )(*operands)


BIG = ("w_in", "w_out", "w_gate", "w_up", "w_down")
BIG_AXIS = dict(w_in=1, w_out=0, w_gate=1, w_up=1, w_down=0)
SMALL = ("norm_mix_w", "ret_decay_fwd", "ret_decay_bwd", "ret_norm_w", "norm_ffn_w", "norm_final_w")
ALL_WEIGHTS = ("norm_mix_w", "w_in", "ret_decay_fwd", "ret_decay_bwd", "ret_norm_w", "w_out", "norm_ffn_w",
               "w_gate", "w_up", "w_down", "norm_final_w")
SMALL_ROW = 128 * 8


def _pack_small(small):
    pieces = [jnp.reshape(small["loss"], (1,))] + [jnp.reshape(small[k], (-1,)) for k in SMALL]
    rows = []
    for p in pieces:
        pad = -p.shape[0] % (8 * SMALL_ROW)
        rows.append(jnp.reshape(jnp.pad(p, (0, pad)), (-1, SMALL_ROW)))
    return jnp.concatenate(rows, axis=0)


def _unpack_small(block, like):
    out, row = {}, 0
    for k in ("loss",) + SMALL:
        size = 1 if k == "loss" else like[k].size
        nrows = -(-size // (8 * SMALL_ROW)) * 8
        out[k] = jnp.reshape(block[row:row + nrows], (-1,))[:size]
        row += nrows
    return out


def kernel(x, norm_mix_w, w_in, ret_decay_fwd, ret_decay_bwd, ret_norm_w, w_out, norm_ffn_w, w_gate, w_up, w_down, norm_final_w, loss_target, m_norm_mix_w, m_w_in, m_ret_decay_fwd, m_ret_decay_bwd, m_ret_norm_w, m_w_out, m_norm_ffn_w, m_w_gate, m_w_up, m_w_down, m_norm_final_w, v_norm_mix_w, v_w_in, v_ret_decay_fwd, v_ret_decay_bwd, v_ret_norm_w, v_w_out, v_norm_ffn_w, v_w_gate, v_w_up, v_w_down, v_norm_final_w):
    weights = dict(norm_mix_w=norm_mix_w, w_in=w_in, ret_decay_fwd=ret_decay_fwd, ret_decay_bwd=ret_decay_bwd,
                   ret_norm_w=ret_norm_w, w_out=w_out, norm_ffn_w=norm_ffn_w, w_gate=w_gate, w_up=w_up,
                   w_down=w_down, norm_final_w=norm_final_w)
    m_in = dict(norm_mix_w=m_norm_mix_w, w_in=m_w_in, ret_decay_fwd=m_ret_decay_fwd, ret_decay_bwd=m_ret_decay_bwd,
                ret_norm_w=m_ret_norm_w, w_out=m_w_out, norm_ffn_w=m_norm_ffn_w, w_gate=m_w_gate, w_up=m_w_up,
                w_down=m_w_down, norm_final_w=m_norm_final_w)
    v_in = dict(norm_mix_w=v_norm_mix_w, w_in=v_w_in, ret_decay_fwd=v_ret_decay_fwd, ret_decay_bwd=v_ret_decay_bwd,
                ret_norm_w=v_ret_norm_w, w_out=v_w_out, norm_ffn_w=v_norm_ffn_w, w_gate=v_w_gate, w_up=v_w_up,
                w_down=v_w_down, norm_final_w=v_norm_final_w)
    pos = jnp.stack([lax.axis_index("c"), 2 * lax.axis_index("x") + lax.axis_index("y")]).astype(jnp.int32)

    own = {"w_in": _to_bf16_in_place(weights["w_in"][0], BIG_AXIS["w_in"], pos, name="cast_w_in")}
    w_in_started = _split_gather_start(own["w_in"], BIG_AXIS["w_in"])
    queued, token = {}, w_in_started[2]
    for k in ("w_gate", "w_out"):
        own[k] = _to_bf16_in_place(weights[k][0], BIG_AXIS[k], pos, name="cast_" + k, after=token)
        queued[k] = _split_start(_gather_job([own[k]], [BIG_AXIS[k]], "ici"), name="all_gather_%s_start" % k)
        token = queued[k]["token"]
    for k in BIG:
        if k not in own:
            own[k] = _to_bf16_in_place(weights[k][0], BIG_AXIS[k], pos, name="cast_" + k, after=token)
    cx, cy = lax.axis_index("x"), lax.axis_index("y")
    shard_ids = jnp.stack([2 * cx + cy, 2 * (1 - cx) + cy, 2 * cx + 1 - cy, 2 * (1 - cx) + 1 - cy]).astype(jnp.int32)

    dx, grad_w, small, w_in_pending = _step(
        x[0], loss_target[0], norm_mix_w, ret_decay_fwd[0], ret_decay_bwd[0], ret_norm_w, norm_ffn_w,
        norm_final_w[None, :], own, w_in_started, queued, shard_ids, pos)

    delta, new_m, new_v = {}, {}, {}

    def update(k, after):
        shape = weights[k].shape
        as2d = (lambda t: jnp.reshape(t, (-1, shape[-1])))
        grad_w[k], delta[k], new_m[k], new_v[k] = (jnp.reshape(t, shape) for t in _adamw(
            as2d(weights[k]), as2d(grad_w[k]), as2d(m_in[k]), as2d(v_in[k]), name="adamw_" + k, after=after))

    others = [k for k in BIG if k != "w_in"]
    for k in others:
        update(k, [w_in_pending["sending"]["token"]])
    _, parts = _split_wait(w_in_pending["sending"], [dx] + [delta[k] for k in others], name="grad_send_w_in_wait")

    half = _sum_chip_parts(w_in_pending["grad"], w_in_pending["received"], parts, BIG_AXIS["w_in"], pos,
                           name="grad_sum_parts_w_in")
    joining = _split_start(_join_job([half], [BIG_AXIS["w_in"]]), name="grad_join_w_in_start")

    like = {k: weights[k] for k in SMALL}
    reduced = _unpack_small(_all_reduce_small(_pack_small(small), after=[joining["token"]]), like)
    loss = reduced["loss"][0]
    for k in SMALL:
        grad_w[k] = jnp.reshape(reduced[k], (1, -1))
        update(k, [])
    (grad_w["w_in"],) = _split_wait(joining, [delta[k] for k in SMALL], name="grad_join_w_in_wait")
    update("w_in", [])

    return (loss, dx[None], *[grad_w[k] for k in ALL_WEIGHTS], *[delta[k] for k in ALL_WEIGHTS],
            *[new_m[k] for k in ALL_WEIGHTS], *[new_v[k] for k in ALL_WEIGHTS])
```

```python
import functools
import math

import numpy as np
import jax
import jax.numpy as jnp
from jax import lax
from jax.experimental import pallas as pl
from jax.experimental.pallas import tpu as pltpu

F32 = jnp.float32
BF16 = jnp.bfloat16
MESH = pl.DeviceIdType.MESH

HEAD_DIM = 128
RET_CHUNK = 128
RET_UNROLL = 8
EPS = 1e-6
DILATED_PATTERNS = ((128, 1), (512, 4), (2048, 16))
ATT_BLOCK = 256
ATT_REACH = max(w // 2 for w, _ in DILATED_PATTERNS)
ATT_NEAR = ATT_BLOCK
ATT_CLASSES = DILATED_PATTERNS[-1][1]
assert all(w // 2 <= ATT_NEAR for w, _ in DILATED_PATTERNS[:-1])
ATT_KB = -(-ATT_NEAR // ATT_BLOCK)
ATT_WINDOW = 2 * ATT_KB + 1
ATT_FAR_GROUP = 8
ATT_NEAR_GROUP = 4
MASKED = -1e30
ROW_MAX_INIT = -1e29
N_CHIPS = 4
VMEM_LIMIT_BYTES = 56 * 1024 * 1024
ELEM_BLOCK_BYTES = 2 * 1024 * 1024
WEIGHT_GRAD_GROUP = 4

ADAM_LR = 0.001
ADAM_B1 = 0.9
ADAM_B2 = 0.999
ADAM_EPS = 1e-08
ADAM_WD = 0.01
ADAM_STEP = 10


def _params(sem=None):
    return pltpu.CompilerParams(dimension_semantics=sem, vmem_limit_bytes=VMEM_LIMIT_BYTES)


def _sigmoid(x):
    return 0.5 * jnp.tanh(0.5 * x) + 0.5


class _Job:
    def __init__(self, *, ins=(), ios=(), outs=(), sems=(), start, finish):
        self.ins, self.ios, self.outs, self.sems = list(ins), list(ios), list(outs), list(sems)
        self.start, self.finish = start, finish

    def results(self):
        return [jax.ShapeDtypeStruct(a.shape, a.dtype) for a in self.ios] + self.outs


def _call(body, *, name, grid, in_specs, out_specs, out_shape, operands, scratch_shapes=(), semantics=None, jobs=(),
          after=(), updates=None):
    in_specs, out_specs, out_shape = list(in_specs), list(out_specs), list(out_shape)
    scratch_shapes = list(scratch_shapes)
    if not jobs:
        n_real = len(in_specs)

        def ordered(*refs):
            body(*refs[:n_real], *refs[n_real + len(after):])

        outs = pl.pallas_call(
            ordered if after else body, name=name, grid=grid,
            in_specs=in_specs + [pl.BlockSpec(memory_space=pl.ANY)] * len(after), out_specs=out_specs,
            out_shape=out_shape, scratch_shapes=scratch_shapes, input_output_aliases=dict(updates or {}),
            compiler_params=_params(semantics))(*operands, *after)
        return outs, []
    n_in, n_out, n_scratch = len(in_specs), len(out_specs), len(scratch_shapes)
    extra_in, extra_out, sems, aliases = [], [], [], dict(updates or {})
    for job in jobs:
        extra_in += job.ins
        for t in range(len(job.ios)):
            aliases[n_in + len(extra_in) + t] = n_out + len(extra_out) + t
        extra_in += job.ios
        extra_out += job.results()
        sems += job.sems

    def carried(*refs):
        x_in = refs[n_in:n_in + len(extra_in)]
        first_out = n_in + len(extra_in) + len(after)
        x_out = refs[first_out + n_out:first_out + n_out + len(extra_out)]
        x_sem = refs[len(refs) - len(sems):]
        views, i_in, i_out, i_sem = [], 0, 0, 0
        for job in jobs:
            data = list(x_in[i_in:i_in + len(job.ins)]) + list(x_out[i_out:i_out + len(job.results())])
            views.append((data, x_sem[i_sem:i_sem + len(job.sems)]))
            i_in += len(job.ins) + len(job.ios)
            i_out += len(job.results())
            i_sem += len(job.sems)
        steps = [pl.program_id(d) for d in range(len(grid))]

        @pl.when(functools.reduce(jnp.logical_and, [s == 0 for s in steps]))
        def _():
            for job, (data, sem) in zip(jobs, views):
                job.start(data, sem)

        body(*refs[:n_in], *refs[first_out:first_out + n_out],
             *refs[len(refs) - len(sems) - n_scratch:len(refs) - len(sems)])

        @pl.when(functools.reduce(jnp.logical_and, [s == g - 1 for s, g in zip(steps, grid)]))
        def _():
            for job, (data, sem) in zip(jobs, views):
                job.finish(data, sem)

    hbm = pl.BlockSpec(memory_space=pl.ANY)
    res = pl.pallas_call(
        carried, name=name, grid=grid, in_specs=in_specs + [hbm] * (len(extra_in) + len(after)),
        out_specs=out_specs + [hbm] * len(extra_out), out_shape=out_shape + extra_out,
        input_output_aliases=aliases, scratch_shapes=scratch_shapes + sems,
        compiler_params=_params(("arbitrary",) * len(grid)),
    )(*operands, *extra_in, *after)
    carried_results, at = [], n_out
    for job in jobs:
        carried_results.append(list(res[at:at + len(job.results())]))
        at += len(job.results())
    return list(res[:n_out]), carried_results


def _run_jobs(jobs, *, name):
    first = jobs[0]
    n_in, n_io = len(first.ins), len(first.ios)
    out_shape = first.results()
    n_sems = [len(job.sems) for job in jobs]

    def body(*refs):
        data = list(refs[:n_in]) + list(refs[n_in + n_io:n_in + n_io + len(out_shape)])
        at = n_in + n_io + len(out_shape)
        for job, ns in zip(jobs, n_sems):
            job.start(data, refs[at:at + ns])
            job.finish(data, refs[at:at + ns])
            at += ns

    hbm = pl.BlockSpec(memory_space=pl.ANY)
    return pl.pallas_call(
        body, name=name, in_specs=[hbm] * (n_in + n_io), out_specs=[hbm] * len(out_shape), out_shape=out_shape,
        input_output_aliases={n_in + t: t for t in range(n_io)},
        scratch_shapes=[s for job in jobs for s in job.sems],
    )(*first.ins, *first.ios)


class _SemaphoreGrid:
    def __init__(self, refs, shape):
        self.refs, self.shape = list(refs), tuple(shape)

    @property
    def at(self):
        return self

    def __getitem__(self, index):
        index = index if isinstance(index, tuple) else (index,)
        flat = 0
        for i, extent in zip(index, self.shape):
            flat = flat * extent + i
        return self.refs[flat]


def _semaphore_grids(job, refs):
    grids, at = [], 0
    for sem in job.sems:
        count = math.prod(sem.shape)
        grids.append(_SemaphoreGrid(refs[at:at + count], sem.shape))
        at += count
    return grids


def _split_start(job, *, name):
    arrays = job.ins + job.ios + [lax.empty(s.shape, s.dtype) for s in job.outs]
    n, ns = len(arrays), sum(math.prod(sem.shape) for sem in job.sems)

    def body(*refs):
        job.start(list(refs[:n]), _semaphore_grids(job, refs[n:n + ns]))
        refs[-1][...] = jnp.zeros_like(refs[-1])

    hbm = pl.BlockSpec(memory_space=pltpu.HBM)
    res = pl.pallas_call(
        body, name=name,
        out_shape=(*[pltpu.SemaphoreType.DMA(())] * ns, *[pltpu.HBM(a.shape, a.dtype) for a in arrays],
                   jax.ShapeDtypeStruct((8, HEAD_DIM), F32)),
        in_specs=[hbm] * n,
        out_specs=(*[pl.BlockSpec(memory_space=pltpu.SEMAPHORE)] * ns, *[hbm] * n,
                   pl.BlockSpec(memory_space=pltpu.VMEM)),
        input_output_aliases={t: ns + t for t in range(n)},
        compiler_params=pltpu.CompilerParams(has_side_effects=pltpu.SideEffectType.DATAFLOW_SIDE_EFFECTING),
    )(*[pltpu.with_memory_space_constraint(a, pltpu.HBM) for a in arrays])
    return dict(job=job, sems=list(res[:ns]), arrays=list(res[ns:ns + n]), token=res[-1])


def _split_wait(started, after, *, name):
    job, arrays, sems = started["job"], started["arrays"], started["sems"]
    n, ns = len(arrays), len(sems)

    def body(*refs):
        job.finish(list(refs[:n]), _semaphore_grids(job, refs[n:n + ns]))

    hbm = pl.BlockSpec(memory_space=pltpu.HBM)
    return pl.pallas_call(
        body, name=name, out_shape=[pltpu.HBM(a.shape, a.dtype) for a in arrays],
        in_specs=[hbm] * n + [pl.BlockSpec(memory_space=pltpu.SEMAPHORE)] * ns
        + [pl.BlockSpec(memory_space=pl.ANY)] * len(after),
        out_specs=[hbm] * n, input_output_aliases={t: t for t in range(n)},
        compiler_params=pltpu.CompilerParams(has_side_effects=pltpu.SideEffectType.DATAFLOW_SIDE_EFFECTING),
    )(*arrays, *sems, *after)


def _dot(a, b, ta=False, tb=False):
    return lax.dot_general(a, b, (((0 if ta else 1,), (1 if tb else 0,)), ((), ())),
                           preferred_element_type=F32)


def _tile(n, want):
    t = min(n, want) // 128 * 128
    while n % t:
        t -= 128
    return t


def _a_spec(ta, tm, tk):
    return pl.BlockSpec((tk, tm), lambda i, j, k: (k, i)) if ta else pl.BlockSpec((tm, tk), lambda i, j, k: (i, k))


def _b_spec(tb, tk, tn):
    return pl.BlockSpec((tn, tk), lambda i, j, k: (j, k)) if tb else pl.BlockSpec((tk, tn), lambda i, j, k: (k, j))


def _accumulate(accs, nk, products, finish):
    if nk == 1:
        finish(*products())
        return
    k = pl.program_id(2)

    @pl.when(k == 0)
    def _():
        for acc, p in zip(accs, products()):
            acc[...] = p

    if nk > 2:
        @pl.when(jnp.logical_and(k > 0, k < nk - 1))
        def _():
            for acc, p in zip(accs, products()):
                acc[...] += p

    @pl.when(k == nk - 1)
    def _():
        finish(*[acc[...] + p for acc, p in zip(accs, products())])


def _matmul(a, b, *, name, ta=False, tb=False, out_dtype=F32, residual=None, tm=1024, tn=1024, tk=2048, jobs=()):
    m, kdim = (a.shape[1], a.shape[0]) if ta else a.shape
    n = b.shape[0] if tb else b.shape[1]
    tm, tn, tk = _tile(m, tm), _tile(n, tn), _tile(kdim, tk)
    nk = kdim // tk

    def body(*refs):
        a_ref, b_ref = refs[:2]
        r_ref = refs[2] if residual is not None else None
        o_ref = refs[-1] if nk == 1 else refs[-2]

        def finish(total):
            if residual is not None:
                total = total + r_ref[...]
            o_ref[...] = total.astype(out_dtype)

        _accumulate(refs[-1:] if nk > 1 else (), nk, lambda: (_dot(a_ref[...], b_ref[...], ta, tb),), finish)

    o_spec = pl.BlockSpec((tm, tn), lambda i, j, k: (i, j))
    in_specs = [_a_spec(ta, tm, tk), _b_spec(tb, tk, tn)]
    operands = [a, b]
    if residual is not None:
        in_specs.append(o_spec)
        operands.append(residual)
    (out,), carried = _call(
        body, name=name, grid=(m // tm, n // tn, nk), in_specs=in_specs, out_specs=[o_spec],
        out_shape=[jax.ShapeDtypeStruct((m, n), out_dtype)], operands=operands,
        scratch_shapes=[pltpu.VMEM((tm, tn), F32)] * (nk > 1),
        semantics=("parallel", "parallel", "arbitrary"), jobs=jobs)
    return (out, carried) if jobs else out


def _matmul_pieces_nt(pieces, b, *, name, tm=512, tn=1024, jobs=(), after=()):
    m, kp = pieces[0].shape
    n = b.shape[0]
    tm, tn = _tile(m, tm), _tile(n, tn)
    count = len(pieces)

    def body(*refs):
        b_ref, o_ref = refs[count], refs[count + 1]
        total = _dot(refs[0][...], b_ref[:, pl.ds(0, kp)], tb=True)
        for p in range(1, count):
            total = total + _dot(refs[p][...], b_ref[:, pl.ds(p * kp, kp)], tb=True)
        o_ref[...] = total

    piece = pl.BlockSpec((tm, kp), lambda j, i: (i, 0))
    (out,), carried = _call(
        body, name=name, grid=(n // tn, m // tm),
        in_specs=[piece] * count + [pl.BlockSpec((tn, count * kp), lambda j, i: (j, 0))],
        out_specs=[pl.BlockSpec((tm, tn), lambda j, i: (i, j))],
        out_shape=[jax.ShapeDtypeStruct((m, n), F32)], operands=[*pieces, b],
        semantics=("parallel", "parallel"), jobs=jobs, after=after)
    return (out, carried) if jobs else out


def _weight_grad_pieces(a, pieces, *, name):
    tokens, m = a.shape
    np_ = pieces[0].shape[1]
    tm = 1024 if m % 1024 == 0 else _tile(m, 1408)
    tn = _tile(np_, 512)
    nb = np_ // tn
    out = None
    for first in range(0, len(pieces), WEIGHT_GRAD_GROUP):
        group = pieces[first:first + WEIGHT_GRAD_GROUP]

        def body(*refs, count=len(group)):
            t_now = pl.program_id(1) // nb
            for t in range(count):
                @pl.when(t_now == t)
                def _(t=t):
                    refs[-1][...] = _dot(refs[0][...], refs[1 + t][...], ta=True)

        def piece_spec(t):
            return pl.BlockSpec((tokens, tn), lambda i, j: (0, jnp.clip(j - t * nb, 0, nb - 1)))

        in_specs = [pl.BlockSpec((tokens, tm), lambda i, j: (0, i))] + [piece_spec(t) for t in range(len(group))]
        operands = [a, *group]
        if out is not None:
            in_specs.append(pl.BlockSpec(memory_space=pl.ANY))
            operands.append(out)
        out = pl.pallas_call(
            body, name="%s_%d" % (name, first), grid=(m // tm, nb * len(group)), in_specs=in_specs,
            out_specs=pl.BlockSpec((tm, tn), lambda i, j, first=first: (i, first * nb + j)),
            out_shape=jax.ShapeDtypeStruct((m, len(pieces) * np_), F32),
            input_output_aliases={len(operands) - 1: 0} if out is not None else {},
            compiler_params=_params(("parallel", "arbitrary")),
        )(*operands)
    return out


def _weight_grad(a, g, *, name, jobs=()):
    tokens, m = a.shape
    tm = 1024 if m % 1024 == 0 else _tile(m, 1408)
    return _matmul(a, g, name=name, ta=True, tm=tm, tn=512, tk=tokens, jobs=jobs)


def _swiglu_fwd(n2, w_gate, w_up, *, tm=1024, tn=512, tk=2048, jobs=()):
    m, kdim = n2.shape
    n = w_gate.shape[1]
    tm, tn, tk = _tile(m, tm), _tile(n, tn), _tile(kdim, tk)
    nk = kdim // tk

    def body(a_ref, g_ref, u_ref, gate_ref, up_ref, act_ref, *acc):
        def products():
            a = a_ref[...]
            return _dot(a, g_ref[...]), _dot(a, u_ref[...])

        def finish(g, u):
            gate_ref[...] = g.astype(BF16)
            up_ref[...] = u.astype(BF16)
            act_ref[...] = (g * _sigmoid(g) * u).astype(BF16)

        _accumulate(acc, nk, products, finish)

    o_spec = pl.BlockSpec((tm, tn), lambda i, j, k: (i, j))
    o_shape = jax.ShapeDtypeStruct((m, n), BF16)
    return _call(
        body, name="swiglu_fwd", grid=(m // tm, n // tn, nk),
        in_specs=[_a_spec(False, tm, tk), _b_spec(False, tk, tn), _b_spec(False, tk, tn)],
        out_specs=[o_spec] * 3, out_shape=[o_shape] * 3, operands=[n2, w_gate, w_up],
        scratch_shapes=[pltpu.VMEM((tm, tn), F32)] * (2 * (nk > 1)),
        semantics=("parallel", "parallel", "arbitrary"), jobs=jobs)


def _swiglu_bwd_act(dh2, w_down, gate, up, *, tm=1024, tn=512, tk=2048):
    m, kdim = dh2.shape
    n = w_down.shape[0]
    tm, tn, tk = _tile(m, tm), _tile(n, tn), _tile(kdim, tk)
    nk = kdim // tk

    sub = _tile(tn, 256)

    def body(a_ref, b_ref, gate_ref, up_ref, dgate_ref, dup_ref, *acc):
        def finish(dact, cols=slice(None)):
            g = gate_ref[:, cols].astype(F32)
            u = up_ref[:, cols].astype(F32)
            sg = _sigmoid(g)
            dup_ref[:, cols] = (dact * g * sg).astype(BF16)
            dgate_ref[:, cols] = (dact * u * sg * (1.0 + g * (1.0 - sg))).astype(BF16)

        if nk == 1:
            a = a_ref[...]
            for c in range(tn // sub):
                cols = pl.ds(c * sub, sub)
                finish(_dot(a, b_ref[cols, :], tb=True), cols)
        else:
            _accumulate(acc, nk, lambda: (_dot(a_ref[...], b_ref[...], tb=True),), finish)

    o_spec = pl.BlockSpec((tm, tn), lambda i, j, k: (i, j))
    o_shape = jax.ShapeDtypeStruct((m, n), BF16)
    return pl.pallas_call(
        body, name="swiglu_bwd_act", grid=(m // tm, n // tn, nk),
        in_specs=[_a_spec(False, tm, tk), _b_spec(True, tk, tn), o_spec, o_spec],
        out_specs=[o_spec] * 2, out_shape=[o_shape] * 2,
        scratch_shapes=[pltpu.VMEM((tm, tn), F32)] * (nk > 1),
        compiler_params=_params(("parallel", "parallel", "arbitrary")),
    )(dh2, w_down, gate, up)


def _swiglu_bwd_in(dgate, dup, w_gate, w_up, *, tm=1024, tn=1024, tk=1408, jobs=()):
    m, kdim = dgate.shape
    n = w_gate.shape[0]
    tm, tn, tk = _tile(m, tm), _tile(n, tn), _tile(kdim, tk)
    nk = kdim // tk

    def body(a1_ref, a2_ref, b1_ref, b2_ref, o_ref, *acc):
        def product():
            return (_dot(a1_ref[...], b1_ref[...], tb=True) + _dot(a2_ref[...], b2_ref[...], tb=True),)

        def finish(total):
            o_ref[...] = total

        _accumulate(acc, nk, product, finish)

    a_spec, b_spec = _a_spec(False, tm, tk), _b_spec(True, tk, tn)
    (out,), carried = _call(
        body, name="swiglu_bwd_in", grid=(m // tm, n // tn, nk),
        in_specs=[a_spec, a_spec, b_spec, b_spec],
        out_specs=[pl.BlockSpec((tm, tn), lambda i, j, k: (i, j))],
        out_shape=[jax.ShapeDtypeStruct((m, n), F32)], operands=[dgate, dup, w_gate, w_up],
        scratch_shapes=[pltpu.VMEM((tm, tn), F32)] * (nk > 1),
        semantics=("parallel", "parallel", "arbitrary"), jobs=jobs)
    return out, carried


def _row_block(rows, cols):
    tr = min(rows, max(16, ELEM_BLOCK_BYTES // (4 * cols) // 16 * 16))
    while rows % tr:
        tr -= 16
    return tr


def _rmsnorm_fwd(x, g, *, name, after=None):
    s, d = x.shape
    tr = _row_block(s, d)

    def body(x_ref, g_ref, *rest):
        xv = x_ref[...]
        r = lax.rsqrt(jnp.mean(xv * xv, axis=-1, keepdims=True) + EPS)
        rest[-1][...] = (xv * r * g_ref[...]).astype(BF16)

    row = pl.BlockSpec((tr, d), lambda i: (i, 0))
    in_specs = [row, pl.BlockSpec((1, d), lambda i: (0, 0))]
    operands = [x, g]
    if after is not None:
        in_specs.append(pl.BlockSpec(after.shape, lambda i: (0, 0)))
        operands.append(after)
    return pl.pallas_call(
        body, name=name, grid=(s // tr,), in_specs=in_specs,
        out_specs=row, out_shape=jax.ShapeDtypeStruct((s, d), BF16),
        compiler_params=_params(("parallel",)),
    )(*operands)


def _rmsnorm_bwd_rows(xv, gv, dy):
    r = lax.rsqrt(jnp.mean(xv * xv, axis=-1, keepdims=True) + EPS)
    xhat = xv * r
    dxh = dy * gv
    dx = r * (dxh - xhat * jnp.mean(dxh * xhat, axis=-1, keepdims=True))
    return dx, dy * xhat


def _rmsnorm_bwd(dn, x, g, skip, *, name, after=()):
    s, d = x.shape
    tr = _row_block(s, d)

    def body(dn_ref, x_ref, g_ref, skip_ref, *rest):
        dx_ref, dxb_ref, dg_ref = rest[len(after):]
        dx, dgr = _rmsnorm_bwd_rows(x_ref[...], g_ref[...], dn_ref[...])
        dx = dx + skip_ref[...]
        dx_ref[...] = dx
        dxb_ref[...] = dx.astype(BF16)

        @pl.when(pl.program_id(0) == 0)
        def _():
            dg_ref[...] = jnp.zeros_like(dg_ref)

        dg_ref[...] += jnp.sum(dgr, axis=0, keepdims=True)

    row = pl.BlockSpec((tr, d), lambda i: (i, 0))
    vec = pl.BlockSpec((1, d), lambda i: (0, 0))
    return pl.pallas_call(
        body, name=name, grid=(s // tr,),
        in_specs=[row, row, vec, row] + [pl.BlockSpec(memory_space=pl.ANY)] * len(after),
        out_specs=[row, row, vec],
        out_shape=[jax.ShapeDtypeStruct((s, d), F32), jax.ShapeDtypeStruct((s, d), BF16),
                   jax.ShapeDtypeStruct((1, d), F32)],
        compiler_params=_params(("arbitrary",)),
    )(dn, x, g, skip, *after)


def _loss_head(h2, g, target):
    s, d = h2.shape
    tr = _row_block(s, d)

    def body(h_ref, g_ref, t_ref, dh_ref, dhb_ref, dg_ref, loss_ref):
        hv = h_ref[...]
        gv = g_ref[...]
        r = lax.rsqrt(jnp.mean(hv * hv, axis=-1, keepdims=True) + EPS)
        err = hv * r * gv - t_ref[...]
        dx, dgr = _rmsnorm_bwd_rows(hv, gv, err * (1.0 / d))
        dh_ref[...] = dx
        dhb_ref[...] = dx.astype(BF16)

        @pl.when(pl.program_id(0) == 0)
        def _():
            dg_ref[...] = jnp.zeros_like(dg_ref)
            loss_ref[...] = jnp.zeros_like(loss_ref)

        dg_ref[...] += jnp.sum(dgr, axis=0, keepdims=True)
        row_loss = jnp.mean(err * err, axis=-1, keepdims=True)
        loss_ref[...] += 0.5 * jnp.sum(row_loss, axis=0, keepdims=True)

    row = pl.BlockSpec((tr, d), lambda i: (i, 0))
    vec = pl.BlockSpec((1, d), lambda i: (0, 0))
    one = pl.BlockSpec((1, 1), lambda i: (0, 0))
    return pl.pallas_call(
        body, name="loss_head", grid=(s // tr,), in_specs=[row, vec, row],
        out_specs=[row, row, vec, one],
        out_shape=[jax.ShapeDtypeStruct((s, d), F32), jax.ShapeDtypeStruct((s, d), BF16),
                   jax.ShapeDtypeStruct((1, d), F32), jax.ShapeDtypeStruct((1, 1), F32)],
        compiler_params=_params(("arbitrary",)),
    )(h2, g, target)


def _attention_bias_tables():
    k = np.arange(-ATT_KB, ATT_KB + 1)[:, None, None]
    delta = k * ATT_BLOCK + np.arange(ATT_BLOCK)[None, None, :] - np.arange(ATT_BLOCK)[None, :, None]
    dist = np.abs(delta)
    count = np.zeros(delta.shape, np.int32)
    for window, dilation in DILATED_PATTERNS:
        count += (delta % dilation == 0) & (dist <= min(window // 2, ATT_NEAR))
    logc = np.where(count > 0, np.log(np.maximum(count, 1)), MASKED)
    return dist.astype(np.float32), logc.astype(np.float32)


def _far_bias_tables(per_class):
    steps = np.abs(np.arange(per_class)[:, None] - np.arange(per_class)[None, :]) * ATT_CLASSES
    valid = (steps > ATT_NEAR) & (steps <= ATT_REACH)
    return steps.astype(np.float32), np.where(valid, 0.0, MASKED).astype(np.float32)


def _to_classes(x):
    s, cols = x.shape
    return jnp.reshape(jnp.transpose(jnp.reshape(x, (s // ATT_CLASSES, ATT_CLASSES, cols)), (1, 0, 2)), (s, cols))


def _from_classes(x):
    s, cols = x.shape
    return jnp.reshape(jnp.transpose(jnp.reshape(x, (ATT_CLASSES, s // ATT_CLASSES, cols)), (1, 0, 2)), (s, cols))


def _head_bias(bias_ref, slope, dist_ref, logc_ref):
    for kk in range(ATT_WINDOW):
        bias_ref[kk] = logc_ref[kk] - slope * dist_ref[kk]
    bias_ref[ATT_WINDOW] = jnp.full((ATT_BLOCK, ATT_BLOCK), MASKED, F32)


def _window_start(i, nq, nwin):
    return jnp.clip(i - ATT_KB, 0, nq - nwin)


def _window_block(j, i):
    rows = pl.ds(pl.multiple_of(j * ATT_BLOCK, ATT_BLOCK), ATT_BLOCK)
    kk = j - i + ATT_KB
    return rows, jnp.where(jnp.logical_and(kk >= 0, kk < ATT_WINDOW), kk, ATT_WINDOW)


def _attention_far_fwd(qkv, slopes, n_heads, jobs=(), after=()):
    s = qkv.shape[0]
    per_class = s // ATT_CLASSES
    scale = HEAD_DIM ** -0.5
    dist, logc = _far_bias_tables(per_class)

    def body(slope_ref, q_ref, k_ref, v_ref, dist_ref, logc_ref, o_ref, lse_ref):
        bias = logc_ref[...] - slope_ref[pl.program_id(0)] * dist_ref[...]
        for a in range(ATT_FAR_GROUP):
            rows = pl.ds(a * per_class, per_class)
            sc = _dot(q_ref[rows, :], k_ref[rows, :], tb=True) * scale + bias
            m = jnp.maximum(jnp.max(sc, axis=-1, keepdims=True), ROW_MAX_INIT)
            p = jnp.exp(sc - m)
            l = jnp.maximum(jnp.sum(p, axis=-1, keepdims=True), 1e-30)
            o_ref[rows, :] = (_dot(p.astype(BF16), v_ref[rows, :]) / l).astype(BF16)
            lse_ref[rows, :] = jnp.broadcast_to(m + jnp.log(l), (per_class, HEAD_DIM))

    hh = n_heads
    blk = pl.BlockSpec((ATT_FAR_GROUP * per_class, HEAD_DIM), lambda h, r: (r, h))
    table = pl.BlockSpec(dist.shape, lambda h, r: (0, 0))
    return _call(
        body, name="attention_far_fwd", grid=(hh, ATT_CLASSES // ATT_FAR_GROUP),
        in_specs=[pl.BlockSpec(memory_space=pltpu.SMEM), blk,
                  pl.BlockSpec((ATT_FAR_GROUP * per_class, HEAD_DIM), lambda h, r: (r, hh + h)),
                  pl.BlockSpec((ATT_FAR_GROUP * per_class, HEAD_DIM), lambda h, r: (r, 2 * hh + h)), table, table],
        out_specs=[blk, blk],
        out_shape=[jax.ShapeDtypeStruct((s, hh * HEAD_DIM), BF16), jax.ShapeDtypeStruct((s, hh * HEAD_DIM), F32)],
        operands=[slopes, qkv, qkv, qkv, jnp.asarray(dist), jnp.asarray(logc)],
        semantics=("parallel", "parallel"), jobs=jobs, after=after)


def _attention_fwd(proj, slopes, far_out, far_lse, n_heads, jobs=(), after=()):
    s = proj.shape[0]
    nq = s // ATT_BLOCK
    scale = HEAD_DIM ** -0.5
    dist, logc = _attention_bias_tables()

    nwin = min(ATT_WINDOW, nq)

    group = math.gcd(ATT_NEAR_GROUP, nq)

    def body(slope_ref, q_ref, k_ref, v_ref, fo_ref, fl_ref, dist_ref, logc_ref, o_ref, lse_ref, bias_ref, s_ref):
        h, step = pl.program_id(0), pl.program_id(1)

        @pl.when(step == 0)
        def _():
            _head_bias(bias_ref, slope_ref[h], dist_ref, logc_ref)

        for a in range(group):
            i = step * group + a
            mine = pl.ds(a * ATT_BLOCK, ATT_BLOCK)
            q = q_ref[mine, :]
            first = _window_start(i, nq, nwin)
            m = jnp.full((ATT_BLOCK, 1), ROW_MAX_INIT, F32)
            for b in range(nwin):
                rows, kk = _window_block(first + b, i)
                sc = _dot(q, k_ref[rows, :], tb=True) * scale + bias_ref[kk]
                s_ref[a * nwin + b] = sc
                m = jnp.maximum(m, jnp.max(sc, axis=-1, keepdims=True))
            l = jnp.zeros((ATT_BLOCK, 1), F32)
            acc = jnp.zeros((ATT_BLOCK, HEAD_DIM), F32)
            for b in range(nwin):
                rows, _ = _window_block(first + b, i)
                p = jnp.exp(s_ref[a * nwin + b] - m)
                l = l + jnp.sum(p, axis=-1, keepdims=True)
                acc = acc + _dot(p.astype(BF16), v_ref[rows, :])
            near_lse = m + jnp.log(l)
            far_lse_col = fl_ref[mine, :1]
            lse = jnp.maximum(near_lse, far_lse_col)
            lse = lse + jnp.log(jnp.exp(near_lse - lse) + jnp.exp(far_lse_col - lse))
            o_ref[mine, :] = (acc * (jnp.exp(near_lse - lse) / l)
                              + fo_ref[mine, :].astype(F32) * jnp.exp(far_lse_col - lse)).astype(BF16)
            lse_ref[mine, :] = jnp.broadcast_to(lse, (ATT_BLOCK, HEAD_DIM))

    hh = n_heads
    blk = pl.BlockSpec((group * ATT_BLOCK, HEAD_DIM), lambda h, i: (i, h))
    table = pl.BlockSpec(dist.shape, lambda h, i: (0, 0, 0))
    return _call(
        body, name="attention_fwd", grid=(hh, nq // group),
        in_specs=[pl.BlockSpec(memory_space=pltpu.SMEM), blk,
                  pl.BlockSpec((s, HEAD_DIM), lambda h, i: (0, hh + h)),
                  pl.BlockSpec((s, HEAD_DIM), lambda h, i: (0, 2 * hh + h)), blk, blk, table, table],
        out_specs=[blk, blk],
        out_shape=[jax.ShapeDtypeStruct((s, hh * HEAD_DIM), BF16), jax.ShapeDtypeStruct((s, hh * HEAD_DIM), F32)],
        operands=[slopes, proj, proj, proj, far_out, far_lse, jnp.asarray(dist), jnp.asarray(logc)],
        scratch_shapes=[pltpu.VMEM((ATT_WINDOW + 1, ATT_BLOCK, ATT_BLOCK), F32),
                        pltpu.VMEM((group * nwin, ATT_BLOCK, ATT_BLOCK), F32)],
        semantics=("parallel", "arbitrary"), jobs=jobs, after=after)


def _attention_far_bwd(qkv, slopes, out, dout, lse, n_heads):
    s = qkv.shape[0]
    per_class = s // ATT_CLASSES
    scale = HEAD_DIM ** -0.5
    dist, logc = _far_bias_tables(per_class)

    def body(slope_ref, q_ref, k_ref, v_ref, o_ref, do_ref, lse_ref, dist_ref, logc_ref, dq_ref, dk_ref, dv_ref):
        bias = logc_ref[...] - slope_ref[pl.program_id(0)] * dist_ref[...]
        for a in range(ATT_FAR_GROUP):
            rows = pl.ds(a * per_class, per_class)
            q, k, do = q_ref[rows, :], k_ref[rows, :], do_ref[rows, :]
            delta = jnp.sum(do.astype(F32) * o_ref[rows, :].astype(F32), axis=-1, keepdims=True)
            p = jnp.exp(_dot(q, k, tb=True) * scale + bias - lse_ref[rows, :1])
            dv_ref[rows, :] = _dot(p.astype(BF16), do, ta=True).astype(BF16)
            ds = (p * (_dot(do, v_ref[rows, :], tb=True) - delta) * scale).astype(BF16)
            dk_ref[rows, :] = _dot(ds, q, ta=True).astype(BF16)
            dq_ref[rows, :] = _dot(ds, k).astype(BF16)

    hh = n_heads
    blk = pl.BlockSpec((ATT_FAR_GROUP * per_class, HEAD_DIM), lambda h, r: (r, h))
    table = pl.BlockSpec(dist.shape, lambda h, r: (0, 0))
    o_shape = jax.ShapeDtypeStruct((s, hh * HEAD_DIM), BF16)
    return pl.pallas_call(
        body, name="attention_far_bwd", grid=(hh, ATT_CLASSES // ATT_FAR_GROUP),
        in_specs=[pl.BlockSpec(memory_space=pltpu.SMEM), blk,
                  pl.BlockSpec((ATT_FAR_GROUP * per_class, HEAD_DIM), lambda h, r: (r, hh + h)),
                  pl.BlockSpec((ATT_FAR_GROUP * per_class, HEAD_DIM), lambda h, r: (r, 2 * hh + h)),
                  blk, blk, blk, table, table],
        out_specs=[blk] * 3, out_shape=[o_shape] * 3,
        compiler_params=_params(("parallel", "parallel")),
    )(slopes, qkv, qkv, qkv, out, dout, lse, jnp.asarray(dist), jnp.asarray(logc))


def _attention_bwd(proj, slopes, out, lse, dmixed, far_grads, n_heads, jobs=()):
    s = proj.shape[0]
    nq = s // ATT_BLOCK
    scale = HEAD_DIM ** -0.5
    dist, logc = _attention_bias_tables()

    nwin = min(ATT_WINDOW, nq)
    group = math.gcd(ATT_NEAR_GROUP, nq)

    def body(slope_ref, q_ref, k_ref, v_ref, o_ref, do_ref, lse_ref, fdq_ref, fdk_ref, fdv_ref, dist_ref, logc_ref,
             dq_ref, dk_ref, dv_ref, dk_acc, dv_acc, bias_ref):
        h, step = pl.program_id(0), pl.program_id(1)

        @pl.when(step == 0)
        def _():
            dk_acc[...] = jnp.zeros_like(dk_acc)
            dv_acc[...] = jnp.zeros_like(dv_acc)
            _head_bias(bias_ref, slope_ref[h], dist_ref, logc_ref)

        for a in range(group):
            i = step * group + a
            mine = pl.ds(a * ATT_BLOCK, ATT_BLOCK)
            q = q_ref[mine, :]
            do = do_ref[mine, :]
            lse_col = lse_ref[mine, :1]
            delta = jnp.sum(do.astype(F32) * o_ref[mine, :].astype(F32), axis=-1, keepdims=True)
            first = _window_start(i, nq, nwin)
            dq = jnp.zeros((ATT_BLOCK, HEAD_DIM), F32)
            for b in range(nwin):
                rows, kk = _window_block(first + b, i)
                kj = k_ref[rows, :]
                vj = v_ref[rows, :]
                p = jnp.exp(_dot(q, kj, tb=True) * scale + bias_ref[kk] - lse_col)
                dv_acc[rows, :] += _dot(p.astype(BF16), do, ta=True)
                dp = _dot(do, vj, tb=True)
                ds = (p * (dp - delta) * scale).astype(BF16)
                dk_acc[rows, :] += _dot(ds, q, ta=True)
                dq = dq + _dot(ds, kj)
            dq_ref[mine, :] = (dq + fdq_ref[mine, :].astype(F32)).astype(BF16)

        @pl.when(step == nq // group - 1)
        def _():
            dk_ref[...] = (dk_acc[...] + fdk_ref[...].astype(F32)).astype(BF16)
            dv_ref[...] = (dv_acc[...] + fdv_ref[...].astype(F32)).astype(BF16)

    hh = n_heads
    blk = pl.BlockSpec((group * ATT_BLOCK, HEAD_DIM), lambda h, i: (i, h))
    col = pl.BlockSpec((s, HEAD_DIM), lambda h, i: (0, h))
    table = pl.BlockSpec(dist.shape, lambda h, i: (0, 0, 0))
    o_shape = jax.ShapeDtypeStruct((s, hh * HEAD_DIM), BF16)
    return _call(
        body, name="attention_bwd", grid=(hh, nq // group),
        in_specs=[pl.BlockSpec(memory_space=pltpu.SMEM), blk,
                  pl.BlockSpec((s, HEAD_DIM), lambda h, i: (0, hh + h)),
                  pl.BlockSpec((s, HEAD_DIM), lambda h, i: (0, 2 * hh + h)),
                  blk, blk, blk, blk, col, col, table, table],
        out_specs=[blk, col, col], out_shape=[o_shape] * 3,
        operands=[slopes, proj, proj, proj, out, dmixed, lse, *far_grads, jnp.asarray(dist), jnp.asarray(logc)],
        scratch_shapes=[pltpu.VMEM((s, HEAD_DIM), F32)] * 2
        + [pltpu.VMEM((ATT_WINDOW + 1, ATT_BLOCK, ATT_BLOCK), F32)],
        semantics=("parallel", "arbitrary"), jobs=jobs)


def _ret_decays(lgc, lga, strict_c, strict_a):
    c = RET_CHUNK
    rel = (lax.broadcasted_iota(jnp.int32, (c, c), 0) - lax.broadcasted_iota(jnp.int32, (c, c), 1)).astype(F32)
    in_c = (rel > 0) if strict_c else (rel >= 0)
    in_a = (rel < 0) if strict_a else (rel <= 0)
    mask = (jnp.where(in_c, jnp.exp(lgc * jnp.maximum(rel, 0.0)), 0.0)
            + jnp.where(in_a, jnp.exp(lga * jnp.maximum(-rel, 0.0)), 0.0))
    idx = lax.broadcasted_iota(jnp.int32, (c, 1), 0).astype(F32)
    ones = jnp.ones((1, HEAD_DIM), F32)
    dec = dict(
        rel=rel, mask=mask, idx=idx,
        a_c=jnp.exp(lgc * (idx + 1.0)), b_c=jnp.exp(lgc * (c - 1.0 - idx)), chunk_c=jnp.exp(ones * (lgc * c)),
        a_a=jnp.exp(lga * (c - idx)), b_a=jnp.exp(lga * idx), chunk_a=jnp.exp(ones * (lga * c)),
    )
    return dec


def _scaled(x, col):
    return (x.astype(F32) * col).astype(BF16)


def _chunk_rows(i):
    return pl.ds(pl.multiple_of(i * RET_CHUNK, RET_CHUNK), RET_CHUNK)


def _chunk_loop(nc, step, init, unroll=RET_UNROLL):
    group = math.gcd(nc, unroll)

    def trip(t, carry):
        for u in range(group):
            carry = step(t * group + u, carry)
        return carry

    return lax.fori_loop(0, nc // group, trip, init)


def _retention(a, b, c, lg_c, lg_a, *, strict_c, strict_a, scale, n_heads, name, gate=None, norm_w=None, jobs=(),
               heads=None, so_far=None, after=()):
    s = a[0].shape[0]
    nc = s // RET_CHUNK
    epilogue = gate is not None
    first_head, head_count = heads if heads is not None else (0, n_heads)

    def body(*refs):
        lgc_ref, lga_ref, a_ref, b_ref, c_ref = refs[:5]
        if epilogue:
            g_ref, w_ref = refs[5:7]
            o_ref, mix_ref, sa_ref = refs[-3:]
        else:
            o_ref, sa_ref = refs[-2:]
        h = first_head + pl.program_id(0)
        dec = _ret_decays(lgc_ref[h], lga_ref[h], strict_c, strict_a)

        def reverse(t, state):
            i = nc - 1 - t
            sa_ref[i] = state.astype(BF16)
            rows = _chunk_rows(i)
            return state * dec["chunk_a"] + _dot(_scaled(b_ref[rows, :], dec["b_a"]), c_ref[rows, :], ta=True)

        _chunk_loop(nc, reverse, jnp.zeros((HEAD_DIM, HEAD_DIM), F32))

        def forward(i, state):
            rows = _chunk_rows(i)
            ai, bi, ci = a_ref[rows, :], b_ref[rows, :], c_ref[rows, :]
            inner = (_dot(ai, bi, tb=True) * dec["mask"]).astype(BF16)
            out = (_dot(inner, ci) + _dot(_scaled(ai, dec["a_c"]), state.astype(BF16))
                   + _dot(_scaled(ai, dec["a_a"]), sa_ref[i])) * scale
            o_ref[rows, :] = out.astype(BF16)
            if epilogue:
                r = lax.rsqrt(jnp.mean(out * out, axis=-1, keepdims=True) + EPS)
                g = g_ref[rows, :].astype(F32)
                mix_ref[rows, :] = (out * r * w_ref[...] * (g * _sigmoid(g))).astype(BF16)
            return state * dec["chunk_c"] + _dot(_scaled(bi, dec["b_c"]), ci, ta=True)

        _chunk_loop(nc, forward, jnp.zeros((HEAD_DIM, HEAD_DIM), F32))

    def col(first):
        return pl.BlockSpec((s, HEAD_DIM), lambda h: (0, first + first_head + h))

    smem = pl.BlockSpec(memory_space=pltpu.SMEM)
    in_specs = [smem, smem, col(a[1]), col(b[1]), col(c[1])]
    operands = [lg_c, lg_a, a[0], b[0], c[0]]
    o_shape = jax.ShapeDtypeStruct((s, n_heads * HEAD_DIM), BF16)
    out_specs, out_shape = [col(0)], [o_shape]
    if epilogue:
        in_specs += [col(gate[1]), pl.BlockSpec((1, HEAD_DIM), lambda h: (0, first_head + h))]
        operands += [gate[0], norm_w]
        out_specs, out_shape = [col(0)] * 2, [o_shape] * 2
    updates = None
    if so_far is not None:
        updates = {len(operands) + t: t for t in range(len(so_far))}
        in_specs += [pl.BlockSpec(memory_space=pl.ANY)] * len(so_far)
        operands += list(so_far)
    res, carried = _call(
        body, name=name, grid=(head_count,), in_specs=in_specs, out_specs=out_specs, out_shape=out_shape,
        operands=operands, scratch_shapes=[pltpu.VMEM((nc, HEAD_DIM, HEAD_DIM), BF16)],
        semantics=("parallel",), jobs=jobs, updates=updates, after=after)
    res = res if epilogue else res[0]
    return (res, carried) if jobs else res


def _retention_decay_grads(a, b, c, e, lg_c, lg_a, *, scale, n_heads):
    s = a[0].shape[0]
    nc = s // RET_CHUNK
    cf = float(RET_CHUNK)

    def body(lgc_ref, lga_ref, a_ref, b_ref, c_ref, e_ref, gc_ref, ga_ref, sa_ref, ta_ref):
        h = pl.program_id(0)
        lgc, lga = lgc_ref[h], lga_ref[h]
        dec = _ret_decays(lgc, lga, True, True)
        rel, idx = dec["rel"], dec["idx"]
        w_c = jnp.where(rel > 0, rel * jnp.exp(lgc * jnp.maximum(rel, 0.0)), 0.0)
        w_a = jnp.where(rel < 0, -rel * jnp.exp(lga * jnp.maximum(-rel, 0.0)), 0.0)
        zero = jnp.zeros((HEAD_DIM, HEAD_DIM), F32)

        def reverse(t, carry):
            st, dst = carry
            i = nc - 1 - t
            sa_ref[i] = st.astype(BF16)
            ta_ref[i] = dst.astype(BF16)
            rows = _chunk_rows(i)
            bi, ci = b_ref[rows, :], c_ref[rows, :]
            st_new = st * dec["chunk_a"] + _dot(_scaled(bi, dec["b_a"]), ci, ta=True)
            dst_new = (cf * st + dst) * dec["chunk_a"] + _dot(_scaled(bi, idx * dec["b_a"]), ci, ta=True)
            return st_new, dst_new

        _chunk_loop(nc, reverse, (zero, zero))

        def forward(i, carry):
            st, dst, acc_c, acc_a = carry
            rows = _chunk_rows(i)
            ai, bi, ci = a_ref[rows, :], b_ref[rows, :], c_ref[rows, :]
            ev = e_ref[rows, :].astype(F32)
            pg = _dot(ai, bi, tb=True) * _dot(e_ref[rows, :], ci, tb=True)
            a_c, a_a = _scaled(ai, dec["a_c"]), _scaled(ai, dec["a_a"])
            inter_c = _dot(a_c, st.astype(BF16)) * (idx + 1.0) + _dot(a_c, dst.astype(BF16))
            inter_a = _dot(a_a, sa_ref[i]) * (cf - idx) + _dot(a_a, ta_ref[i])
            acc_c = acc_c + jnp.sum(pg * w_c, axis=0, keepdims=True) + jnp.sum(inter_c * ev, axis=0, keepdims=True)
            acc_a = acc_a + jnp.sum(pg * w_a, axis=0, keepdims=True) + jnp.sum(inter_a * ev, axis=0, keepdims=True)
            st_new = st * dec["chunk_c"] + _dot(_scaled(bi, dec["b_c"]), ci, ta=True)
            dst_new = ((cf * st + dst) * dec["chunk_c"]
                       + _dot(_scaled(bi, (cf - 1.0 - idx) * dec["b_c"]), ci, ta=True))
            return st_new, dst_new, acc_c, acc_a

        row = jnp.zeros((1, HEAD_DIM), F32)
        _, _, acc_c, acc_a = _chunk_loop(nc, forward, (zero, zero, row, row))
        gc_ref[...] = jnp.broadcast_to(jnp.sum(acc_c, axis=-1, keepdims=True) * scale, gc_ref.shape)
        ga_ref[...] = jnp.broadcast_to(jnp.sum(acc_a, axis=-1, keepdims=True) * scale, ga_ref.shape)

    def col(first):
        return pl.BlockSpec((s, HEAD_DIM), lambda h: (0, first + h))

    smem = pl.BlockSpec(memory_space=pltpu.SMEM)
    o_spec = pl.BlockSpec((1, 8, HEAD_DIM), lambda h: (h, 0, 0))
    o_shape = jax.ShapeDtypeStruct((n_heads, 8, HEAD_DIM), F32)
    gc, ga = pl.pallas_call(
        body, name="retention_decay_grads", grid=(n_heads,),
        in_specs=[smem, smem, col(a[1]), col(b[1]), col(c[1]), col(e[1])],
        out_specs=[o_spec] * 2, out_shape=[o_shape] * 2,
        scratch_shapes=[pltpu.VMEM((nc, HEAD_DIM, HEAD_DIM), BF16)] * 2,
        compiler_params=_params(("parallel",)),
    )(lg_c, lg_a, a[0], b[0], c[0], e[0])
    return gc[:, 0, 0], ga[:, 0, 0]


def _ret_gate_bwd(dmixed, first_col, out, proj, gate_col, norm_w, n_heads):
    s = out.shape[0]
    tr = _row_block(s, 8 * HEAD_DIM)

    def body(dm_ref, o_ref, g_ref, w_ref, do_ref, dg_ref, dw_ref):
        dm = dm_ref[...].astype(F32)
        ov = o_ref[...].astype(F32)
        g = g_ref[...].astype(F32)
        w = w_ref[...]
        r = lax.rsqrt(jnp.mean(ov * ov, axis=-1, keepdims=True) + EPS)
        ohat = ov * r
        sg = _sigmoid(g)
        silu = g * sg
        dg_ref[...] = (dm * ohat * w * sg * (1.0 + g * (1.0 - sg))).astype(BF16)
        dohat = dm * w * silu
        do_ref[...] = (r * (dohat - ohat * jnp.mean(dohat * ohat, axis=-1, keepdims=True))).astype(BF16)

        @pl.when(pl.program_id(1) == 0)
        def _():
            dw_ref[...] = jnp.zeros_like(dw_ref)

        dw_ref[...] += jnp.sum(dm * ohat * silu, axis=0, keepdims=True)

    def blk(first):
        return pl.BlockSpec((tr, HEAD_DIM), lambda h, i: (i, first + h))

    vec = pl.BlockSpec((1, HEAD_DIM), lambda h, i: (0, h))
    o_shape = jax.ShapeDtypeStruct((s, n_heads * HEAD_DIM), BF16)
    return pl.pallas_call(
        body, name="ret_gate_bwd", grid=(n_heads, s // tr),
        in_specs=[blk(first_col), blk(0), blk(gate_col), vec],
        out_specs=[blk(0), blk(0), vec],
        out_shape=[o_shape, o_shape, jax.ShapeDtypeStruct((1, n_heads * HEAD_DIM), F32)],
        compiler_params=_params(("parallel", "arbitrary")),
    )(dmixed, out, proj, norm_w)


def _step(x, target, norm_mix_w, ret_decay_fwd, ret_decay_bwd, ret_norm_w, norm_ffn_w, norm_final_w, own,
          w_in_started, queued, shard_ids, pos):
    d = x.shape[1]
    nh = d // (2 * HEAD_DIM)
    scale = HEAD_DIM ** -0.5
    slopes = jnp.exp2(-8.0 * jnp.arange(1, nh + 1, dtype=F32) / nh)
    lg_f = -jnp.exp(ret_decay_fwd)
    lg_b = -jnp.exp(ret_decay_bwd)
    q_r, k_r, v_r, g_r = 3 * nh, 4 * nh, 5 * nh, 6 * nh
    ax = BIG_AXIS

    def gather(names, arrays, stage, part=None, peers=(0, 1, 2)):
        return _gather_job(arrays, [ax[k] for k in names], stage, part, peers)

    def add_halves(k, g, received):
        return _add_halves(g, received, ax[k], pos, name="grad_add_halves_" + k)

    def sum_parts(k, g, received, parts):
        return _sum_chip_parts(g, received, parts, ax[k], pos, name="grad_sum_parts_" + k)

    sems, w_in, token = w_in_started
    n1 = _rmsnorm_fwd(x, norm_mix_w, name="norm_mix_fwd", after=token)
    proj = _in_proj_part(n1, w_in, None, shard_ids, 0, out_cols=w_in.shape[1])
    for peer in range(3):
        behind = [proj] + ([queued["w_down"]["token"]] if peer == 0 else [])
        w_in = _split_gather_wait(sems, w_in, ax["w_in"], peer, behind)
        (w_in,) = _run_jobs([gather(["w_in"], [w_in], "d2d", peers=(peer,))], name="all_gather_w_in_sibling_%d" % peer)
        proj = _in_proj_part(n1, w_in, proj, shard_ids, 1 + peer, out_cols=w_in.shape[1])
    (w_gate,) = _split_wait(queued["w_gate"], [proj], name="all_gather_w_gate_wait")
    qkv_classes = _to_classes(proj[:, :3 * nh * HEAD_DIM])
    (ret, ret_mixed), [[w_gate]] = _retention(
        (proj, q_r), (proj, k_r), (proj, v_r), lg_f, lg_b, strict_c=False, strict_a=True, scale=scale, n_heads=nh,
        name="retention_fwd_first", gate=(proj, g_r), norm_w=ret_norm_w, heads=(0, nh // 2),
        jobs=[gather(["w_gate"], [w_gate], "d2d")])
    (far_out, far_lse), _ = _attention_far_fwd(qkv_classes, slopes, nh, after=[ret_mixed])
    ret, ret_mixed = _retention(
        (proj, q_r), (proj, k_r), (proj, v_r), lg_f, lg_b, strict_c=False, strict_a=True, scale=scale, n_heads=nh,
        name="retention_fwd_second", gate=(proj, g_r), norm_w=ret_norm_w, heads=(nh // 2, nh - nh // 2),
        so_far=[ret, ret_mixed], after=[far_out])
    (w_out,) = _split_wait(queued["w_out"], [ret_mixed], name="all_gather_w_out_wait")
    (attn, lse), [[w_out]] = _attention_fwd(
        proj, slopes, _from_classes(far_out), _from_classes(far_lse), nh, after=[ret_mixed],
        jobs=[gather(["w_out"], [w_out], "d2d")])
    mixed = jnp.concatenate([attn, ret_mixed], axis=1)
    (w_up,) = _split_wait(queued["w_up"], [mixed], name="all_gather_w_up_wait")
    h1, [[w_up]] = _matmul(mixed, w_out, name="out_proj", residual=x, jobs=[gather(["w_up"], [w_up], "d2d")])
    n2 = _rmsnorm_fwd(h1, norm_ffn_w, name="norm_ffn_fwd")
    (w_down,) = _split_wait(queued["w_down"], [n2], name="all_gather_w_down_wait")
    (gate, up, act), [[w_down]] = _swiglu_fwd(n2, w_gate, w_up, jobs=[gather(["w_down"], [w_down], "d2d")])
    h2 = _matmul(act, w_down, name="down_proj", residual=h1, tk=2816)
    dh2, dh2_b, d_norm_final, loss = _loss_head(h2, norm_final_w, target)

    dgate, dup = _swiglu_bwd_act(dh2_b, w_down, gate, up)
    g_down = _weight_grad(act, dh2_b, name="grad_w_down")
    g_gate, [[r_down]] = _weight_grad(n2, dgate, name="grad_w_gate", jobs=[_exchange_job([g_down], [ax["w_down"]])])
    s_down = add_halves("w_down", g_down, r_down)
    g_up, [[r_gate], [p_down]] = _weight_grad(
        n2, dup, name="grad_w_up",
        jobs=[_exchange_job([g_gate], [ax["w_gate"]]), _send_sums_job([s_down], [ax["w_down"]], (0, 1, 2))])
    s_gate = add_halves("w_gate", g_gate, r_gate)
    dn2, [[r_up], [p_gate], [p_down]] = _swiglu_bwd_in(
        dgate, dup, w_gate, w_up,
        jobs=[_exchange_job([g_up], [ax["w_up"]]), _send_sums_job([s_gate], [ax["w_gate"]]),
              _send_sums_job([s_down], [ax["w_down"]], (1, 1, 2), landing=[p_down])])
    h_down = sum_parts("w_down", g_down, r_down, p_down)
    s_up = add_halves("w_up", g_up, r_up)
    h_gate = sum_parts("w_gate", g_gate, r_gate, p_gate)
    dh1, dh1_b, d_norm_ffn = _rmsnorm_bwd(dn2, h1, norm_ffn_w, dh2, name="norm_ffn_bwd")

    dmixed, [[gr_down], [p_up]] = _matmul(
        dh1_b, w_out, name="out_proj_bwd", tb=True, out_dtype=BF16,
        jobs=[_join_job([h_down], [ax["w_down"]]), _send_sums_job([s_up], [ax["w_up"]], (0, 1, 4))])
    far_in = [_to_classes(t) for t in (attn, dmixed[:, :nh * HEAD_DIM], lse)]
    g_out, [[p_up]] = _weight_grad(mixed, dh1_b, name="grad_w_out",
                                   jobs=[_send_sums_job([s_up], [ax["w_up"]], (1, 1, 4), landing=[p_up])])
    d_ret, dg_r, d_ret_norm = _ret_gate_bwd(dmixed, nh, ret, proj, g_r, ret_norm_w, nh)
    far_grads = _attention_far_bwd(qkv_classes, slopes, *far_in, nh)
    far_grads = [_from_classes(t) for t in far_grads]
    dq_r, [[gr_gate], [p_up]] = _retention(
        (d_ret, 0), (proj, v_r), (proj, k_r), lg_f, lg_b, strict_c=False, strict_a=True, scale=scale, n_heads=nh,
        name="retention_dq",
        jobs=[_join_job([h_gate], [ax["w_gate"]]), _send_sums_job([s_up], [ax["w_up"]], (2, 1, 4), landing=[p_up])])
    (dq_a, dk_a, dv_a), [[p_up], [r_out]] = _attention_bwd(
        proj, slopes, attn, lse, dmixed, far_grads, nh,
        jobs=[_send_sums_job([s_up], [ax["w_up"]], (3, 1, 4), landing=[p_up]),
              _exchange_job([g_out], [ax["w_out"]])])
    s_out = add_halves("w_out", g_out, r_out)
    h_up = sum_parts("w_up", g_up, r_up, p_up)
    dv_r, [[p_out], [gr_up]] = _retention(
        (proj, k_r), (proj, q_r), (d_ret, 0), lg_b, lg_f, strict_c=True, strict_a=False, scale=scale, n_heads=nh,
        name="retention_dv", jobs=[_send_sums_job([s_out], [ax["w_out"]]), _join_job([h_up], [ax["w_up"]])])
    h_out = sum_parts("w_out", g_out, r_out, p_out)
    dk_r, [[gr_out]] = _retention(
        (proj, v_r), (d_ret, 0), (proj, q_r), lg_b, lg_f, strict_c=True, strict_a=False, scale=scale, n_heads=nh,
        name="retention_dk", jobs=[_join_job([h_out], [ax["w_out"]])])
    dlg_f, dlg_b = _retention_decay_grads((proj, q_r), (proj, k_r), (proj, v_r), (d_ret, 0), lg_f, lg_b,
                                          scale=scale, n_heads=nh)
    dproj = [dq_a, dk_a, dv_a, dq_r, dk_r, dv_r, dg_r]
    g_in = _weight_grad_pieces(n1, dproj, name="grad_w_in")
    exchange = _split_start(_exchange_job([g_in], [ax["w_in"]]), name="grad_exchange_w_in_start")
    dn1 = _matmul_pieces_nt(dproj, w_in, name="in_proj_bwd", after=[exchange["token"]])
    g_in, r_in = _split_wait(exchange, [dn1], name="grad_exchange_w_in_wait")
    s_in = add_halves("w_in", g_in, r_in)
    sending = _split_start(_send_sums_job([s_in], [ax["w_in"]]), name="grad_send_w_in_start")
    dx, _, d_norm_mix = _rmsnorm_bwd(dn1, x, norm_mix_w, dh1, name="norm_mix_bwd", after=[sending["token"]])

    small = dict(loss=loss[0, 0], norm_mix_w=d_norm_mix, ret_decay_fwd=dlg_f * lg_f, ret_decay_bwd=dlg_b * lg_b,
                 ret_norm_w=d_ret_norm, norm_ffn_w=d_norm_ffn, norm_final_w=d_norm_final)
    return (dx, dict(w_out=gr_out, w_gate=gr_gate, w_up=gr_up, w_down=gr_down), small,
            dict(sending=sending, grad=g_in, received=r_in))


def _mesh_position():
    x, y, c = lax.axis_index("x"), lax.axis_index("y"), lax.axis_index("c")
    chips = [(1 - x, y), (x, 1 - y), (1 - x, 1 - y)]
    return x, y, c, chips


def _span(span):
    if span is None:
        return slice(None)
    start, size, step = span
    return pl.ds(start if isinstance(start, int) else pl.multiple_of(start, step), size)


def _part_rows(part, rows):
    first, count, of = part
    return first * (rows // of), count * (rows // of), rows // of


def _region(ref, axis, shard, half, shard_size, half_size, part=None, total_rows=None):
    along = None if shard is None else (shard * shard_size, shard_size, shard_size)
    other = None if half is None else (half * half_size, half_size, half_size)
    rows, cols = (other, along) if axis == 1 else (along, other)
    if part is not None:
        start, size, _ = rows if rows is not None else (0, total_rows, None)
        offset, size, step = _part_rows(part, size)
        rows = (start + offset, size, step)
    return ref.at[_span(rows), _span(cols)]


def _fuse(first, second):
    assert not (first.ins or first.outs or second.ins or second.outs)
    assert len(first.ios) == len(second.ios) and all(a is b for a, b in zip(first.ios, second.ios))
    cut = len(first.sems)

    def start(refs, sems):
        first.start(refs, sems[:cut])
        second.start(refs, sems[cut:])

    def finish(refs, sems):
        first.finish(refs, sems[:cut])
        second.finish(refs, sems[cut:])

    return _Job(ios=first.ios, sems=first.sems + second.sems, start=start, finish=finish)


def _gather_job(full, axes, stage, part=None, peers=(0, 1, 2)):
    n = len(full)

    def copies(refs, sems):
        send_sem, recv_sem = sems
        x, y, c, chips = _mesh_position()
        me = 2 * x + y

        def copy(w, k, shard, half, target):
            rows_cols = full[w].shape
            place = _region(refs[w], axes[w], shard, half, rows_cols[axes[w]] // N_CHIPS, rows_cols[1 - axes[w]] // 2,
                            part)
            return pltpu.make_async_remote_copy(
                src_ref=place, dst_ref=place, send_sem=send_sem.at[w, k], recv_sem=recv_sem.at[w, k],
                device_id=target, device_id_type=MESH)

        def sent(w, k):
            if stage == "ici":
                return copy(w, k, me, c, (chips[k][0], chips[k][1], c))
            return copy(w, k, 2 * chips[k][0] + chips[k][1], c, (x, y, 1 - c))

        def landed(w, k):
            return copy(w, k, 2 * chips[k][0] + chips[k][1], c if stage == "ici" else 1 - c, (x, y, 1 - c))

        return sent, landed

    def start(refs, sems):
        sent, _ = copies(refs, sems)
        for w in range(n):
            for k in peers:
                sent(w, k).start()

    def finish(refs, sems):
        sent, landed = copies(refs, sems)
        for w in range(n):
            for k in peers:
                landed(w, k).wait_recv()
                sent(w, k).wait_send()

    return _Job(ios=full, sems=[pltpu.SemaphoreType.DMA((n, 3))] * 2, start=start, finish=finish)


def _exchange_job(grads, axes):
    n = len(grads)

    def half_shape(w):
        return tuple(d // 2 if a != axes[w] else d for a, d in enumerate(grads[w].shape))

    def copy(refs, sems, w):
        x, y, c, _ = _mesh_position()
        return pltpu.make_async_remote_copy(
            src_ref=_region(refs[w], axes[w], None, 1 - c, 0, half_shape(w)[1 - axes[w]]), dst_ref=refs[n + w],
            send_sem=sems[0].at[w], recv_sem=sems[1].at[w], device_id=(x, y, 1 - c), device_id_type=MESH)

    def start(refs, sems):
        for w in range(n):
            copy(refs, sems, w).start()

    def finish(refs, sems):
        for w in range(n):
            copy(refs, sems, w).wait()

    return _Job(ins=grads, outs=[jax.ShapeDtypeStruct(half_shape(w), F32) for w in range(n)],
                sems=[pltpu.SemaphoreType.DMA((n,))] * 2, start=start, finish=finish)


def _half_block_spec(axis, block, half_blocks, use_half):
    if axis == 1:
        if use_half:
            return pl.BlockSpec(block, lambda i, pos: (pos[0] * half_blocks + i, 0))
        return pl.BlockSpec(block, lambda i, pos: (i, 0))
    if use_half:
        return pl.BlockSpec(block, lambda i, pos: (i, pos[0]))
    return pl.BlockSpec(block, lambda i, pos: (i, 0))


def _add_halves(grad, received, axis, pos, *, name):
    rows, cols = received.shape
    tr = _row_block(rows, cols)
    nb = rows // tr

    def body(pos_ref, g_ref, r_ref, o_ref):
        o_ref[...] = (g_ref[...] + r_ref[...]).astype(BF16)

    blk = (tr, cols)
    return pl.pallas_call(
        body, name=name, out_shape=jax.ShapeDtypeStruct((rows, cols), BF16),
        grid_spec=pltpu.PrefetchScalarGridSpec(
            num_scalar_prefetch=1, grid=(nb,),
            in_specs=[_half_block_spec(axis, blk, nb, True), _half_block_spec(axis, blk, nb, False)],
            out_specs=_half_block_spec(axis, blk, nb, False)),
        compiler_params=_params(("parallel",)),
    )(pos, grad, received)


def _send_sums_job(sums, axes, part=None, landing=None):
    n = len(sums)

    def part_shape(w):
        return tuple(d // N_CHIPS if a == axes[w] else d for a, d in enumerate(sums[w].shape))

    def copy(refs, sems, w, k):
        x, y, c, chips = _mesh_position()
        shard = 2 * chips[k][0] + chips[k][1]
        rows = part_shape(w)[0]
        dst = refs[n + w].at[k]
        if part is not None:
            offset, size, _ = _part_rows(part, rows)
            dst = refs[n + w].at[k, pl.ds(offset, size), :]
        return pltpu.make_async_remote_copy(
            src_ref=_region(refs[w], axes[w], shard, None, part_shape(w)[axes[w]], 0, part, rows), dst_ref=dst,
            send_sem=sems[0].at[w, k], recv_sem=sems[1].at[w, k],
            device_id=(chips[k][0], chips[k][1], c), device_id_type=MESH)

    def start(refs, sems):
        for w in range(n):
            for k in range(3):
                copy(refs, sems, w, k).start()

    def finish(refs, sems):
        for w in range(n):
            for k in range(3):
                copy(refs, sems, w, k).wait()

    sems = [pltpu.SemaphoreType.DMA((n, 3))] * 2
    if landing is not None:
        return _Job(ins=sums, ios=landing, sems=sems, start=start, finish=finish)
    return _Job(ins=sums, outs=[jax.ShapeDtypeStruct((3,) + part_shape(w), BF16) for w in range(n)],
                sems=sems, start=start, finish=finish)


def _sum_chip_parts(grad, received, parts, axis, pos, *, name):
    _, rows, cols = parts.shape
    tr = _row_block(rows, cols)
    nb = rows // tr
    blk = (tr, cols)

    def body(pos_ref, g_ref, r_ref, p_ref, o_ref):
        total = g_ref[...] + r_ref[...]
        for k in range(3):
            total = total + p_ref[k].astype(F32)
        o_ref[...] = total

    if axis == 1:
        g_spec = pl.BlockSpec(blk, lambda i, pos: (pos[0] * nb + i, pos[1]))
        r_spec = pl.BlockSpec(blk, lambda i, pos: (i, pos[1]))
        o_spec = pl.BlockSpec(blk, lambda i, pos: (pos[0] * nb + i, 0))
        shard_shape = (2 * rows, cols)
    else:
        g_spec = pl.BlockSpec(blk, lambda i, pos: (pos[1] * nb + i, pos[0]))
        r_spec = pl.BlockSpec(blk, lambda i, pos: (pos[1] * nb + i, 0))
        o_spec = pl.BlockSpec(blk, lambda i, pos: (i, pos[0]))
        shard_shape = (rows, 2 * cols)
    return pl.pallas_call(
        body, name=name, out_shape=jax.ShapeDtypeStruct(shard_shape, F32),
        grid_spec=pltpu.PrefetchScalarGridSpec(
            num_scalar_prefetch=1, grid=(nb,),
            in_specs=[g_spec, r_spec, pl.BlockSpec((3,) + blk, lambda i, pos: (0, i, 0))],
            out_specs=o_spec),
        compiler_params=_params(("parallel",)),
    )(pos, grad, received, parts)


def _join_job(shards, axes):
    n = len(shards)

    def copy(refs, sems, w, other):
        x, y, c, _ = _mesh_position()
        place = _region(refs[w], axes[w], None, 1 - c if other else c, 0, shards[w].shape[1 - axes[w]] // 2)
        return pltpu.make_async_remote_copy(
            src_ref=place, dst_ref=place, send_sem=sems[0].at[w], recv_sem=sems[1].at[w],
            device_id=(x, y, 1 - c), device_id_type=MESH)

    def start(refs, sems):
        for w in range(n):
            copy(refs, sems, w, False).start()

    def finish(refs, sems):
        for w in range(n):
            copy(refs, sems, w, True).wait_recv()
            copy(refs, sems, w, False).wait_send()

    return _Job(ios=shards, sems=[pltpu.SemaphoreType.DMA((n,))] * 2, start=start, finish=finish)


def _all_reduce_small(vec, after=()):
    rows, cols = vec.shape

    def body(v_ref, *rest):
        o_ref, land_ref, send_sem, recv_sem = rest[len(after):]
        x, y, c, _ = _mesh_position()
        me = 4 * x + 2 * y + c
        land_ref[me] = v_ref[...]
        copies = []
        for k in range(1, 8):
            px, py, pc = x ^ (k >> 2), y ^ ((k >> 1) & 1), c ^ (k & 1)
            copies.append(pltpu.make_async_remote_copy(
                src_ref=v_ref, dst_ref=land_ref.at[me], send_sem=send_sem.at[k], recv_sem=recv_sem.at[k],
                device_id=(px, py, pc), device_id_type=MESH))
        for cp in copies:
            cp.start()
        for k in range(1, 8):
            peer = me ^ k
            pltpu.make_async_remote_copy(
                src_ref=v_ref, dst_ref=land_ref.at[peer], send_sem=send_sem.at[k], recv_sem=recv_sem.at[k],
                device_id=(x, y, c), device_id_type=MESH).wait_recv()
        for cp in copies:
            cp.wait_send()
        total = land_ref[0]
        for k in range(1, 8):
            total = total + land_ref[k]
        o_ref[...] = total

    vmem = pl.BlockSpec(memory_space=pltpu.VMEM)
    return pl.pallas_call(
        body, name="all_reduce_small", in_specs=[vmem] + [pl.BlockSpec(memory_space=pl.ANY)] * len(after),
        out_specs=vmem, out_shape=jax.ShapeDtypeStruct((rows, cols), F32),
        scratch_shapes=[pltpu.VMEM((8, rows, cols), F32), pltpu.SemaphoreType.DMA((8,)), pltpu.SemaphoreType.DMA((8,))],
    )(vec, *after)


def _adamw(w, g, m, v, *, name, after=()):
    rows, cols = w.shape
    tr = _row_block(rows, cols) if rows % 8 == 0 else rows
    bc1 = 1.0 - ADAM_B1 ** ADAM_STEP
    bc2 = 1.0 - ADAM_B2 ** ADAM_STEP

    def body(w_ref, g_ref, m_ref, v_ref, *rest):
        go_ref, d_ref, mo_ref, vo_ref = rest[len(after):]
        gv = g_ref[...]
        go_ref[...] = gv
        mn = ADAM_B1 * m_ref[...] + (1.0 - ADAM_B1) * gv
        vn = ADAM_B2 * v_ref[...] + (1.0 - ADAM_B2) * (gv * gv)
        mo_ref[...] = mn
        vo_ref[...] = vn
        d_ref[...] = -ADAM_LR * ((mn / bc1) / (jnp.sqrt(vn / bc2) + ADAM_EPS) + ADAM_WD * w_ref[...])

    blk = pl.BlockSpec((tr, cols), lambda i: (i, 0))
    shape = jax.ShapeDtypeStruct((rows, cols), F32)
    return pl.pallas_call(
        body, name=name, grid=(rows // tr,), in_specs=[blk] * 4 + [pl.BlockSpec(memory_space=pl.ANY)] * len(after),
        out_specs=[blk] * 4, out_shape=[shape] * 4, compiler_params=_params(("parallel",)),
    )(w, g, m, v, *after)


def _to_bf16_in_place(w, axis, pos, *, name, after=None):
    rows, cols = w.shape
    tr = _row_block(rows, cols)
    nb = rows // tr

    def body(pos_ref, w_ref, *rest):
        rest[-1][...] = w_ref[...].astype(BF16)

    if axis == 1:
        o_spec = pl.BlockSpec((tr, cols), lambda i, pos: (i, pos[1]))
        full_shape = (rows, N_CHIPS * cols)
    else:
        o_spec = pl.BlockSpec((tr, cols), lambda i, pos: (pos[1] * nb + i, 0))
        full_shape = (N_CHIPS * rows, cols)
    in_specs = [pl.BlockSpec((tr, cols), lambda i, pos: (i, 0))]
    operands = [pos, w]
    if after is not None:
        in_specs.append(pl.BlockSpec(after.shape, lambda i, pos: (0, 0)))
        operands.append(after)
    return pl.pallas_call(
        body, name=name, out_shape=jax.ShapeDtypeStruct(full_shape, BF16),
        grid_spec=pltpu.PrefetchScalarGridSpec(num_scalar_prefetch=1, grid=(nb,), in_specs=in_specs, out_specs=o_spec),
        compiler_params=_params(("parallel",)),
    )(*operands)


def _split_gather_start(full, axis):
    rows_cols = full.shape

    def body(buf_ref, *rest):
        sems = rest[:6]
        token_ref = rest[7]
        x, y, c, chips = _mesh_position()
        place = _region(buf_ref, axis, 2 * x + y, c, rows_cols[axis] // N_CHIPS, rows_cols[1 - axis] // 2)
        for k in range(3):
            pltpu.make_async_remote_copy(
                src_ref=place, dst_ref=place, send_sem=sems[k], recv_sem=sems[3 + k],
                device_id=(chips[k][0], chips[k][1], c), device_id_type=MESH).start()
        token_ref[...] = jnp.zeros_like(token_ref)

    hbm = pl.BlockSpec(memory_space=pltpu.HBM)
    sem = pl.BlockSpec(memory_space=pltpu.SEMAPHORE)
    res = pl.pallas_call(
        body, name="all_gather_w_in_start",
        out_shape=(*[pltpu.SemaphoreType.DMA(())] * 6, pltpu.HBM(full.shape, full.dtype),
                   jax.ShapeDtypeStruct((8, HEAD_DIM), F32)),
        in_specs=(hbm,), out_specs=(*[sem] * 6, hbm, pl.BlockSpec(memory_space=pltpu.VMEM)),
        input_output_aliases={0: 6},
        compiler_params=pltpu.CompilerParams(has_side_effects=pltpu.SideEffectType.DATAFLOW_SIDE_EFFECTING),
    )(pltpu.with_memory_space_constraint(full, pltpu.HBM))
    return list(res[:6]), res[6], res[7]


def _split_gather_wait(sems, full, axis, peer, after):
    rows_cols = full.shape

    def body(buf_ref, send_sem, recv_sem, *rest):
        x, y, c, chips = _mesh_position()

        def copy(shard):
            place = _region(buf_ref, axis, shard, c, rows_cols[axis] // N_CHIPS, rows_cols[1 - axis] // 2)
            return pltpu.make_async_remote_copy(
                src_ref=place, dst_ref=place, send_sem=send_sem, recv_sem=recv_sem,
                device_id=(chips[peer][0], chips[peer][1], c), device_id_type=MESH)

        copy(2 * x + y).wait_send()
        copy(2 * chips[peer][0] + chips[peer][1]).wait_recv()

    hbm = pl.BlockSpec(memory_space=pltpu.HBM)
    sem = pl.BlockSpec(memory_space=pltpu.SEMAPHORE)
    return pl.pallas_call(
        body, name="all_gather_w_in_wait_%d" % peer, out_shape=pltpu.HBM(full.shape, full.dtype),
        in_specs=(hbm, sem, sem, *[pl.BlockSpec(memory_space=pl.ANY)] * len(after)), out_specs=hbm,
        input_output_aliases={0: 0},
        compiler_params=pltpu.CompilerParams(has_side_effects=pltpu.SideEffectType.DATAFLOW_SIDE_EFFECTING),
    )(full, sems[peer], sems[3 + peer], *after)


def _in_proj_part(n1, w_in, proj, shard_ids, which, *, out_cols):
    m, kdim = n1.shape
    cols = out_cols // N_CHIPS
    tm = _tile(m, 1024)

    def body(ids_ref, a_ref, b_ref, *rest):
        rest[-1][...] = _dot(a_ref[...], b_ref[...]).astype(BF16)

    in_specs = [pl.BlockSpec((tm, kdim), lambda i, ids: (i, 0)),
                pl.BlockSpec((kdim, cols), lambda i, ids: (0, ids[which]))]
    operands = [shard_ids, n1, w_in]
    if proj is not None:
        in_specs.append(pl.BlockSpec(memory_space=pl.ANY))
        operands.append(proj)
    return pl.pallas_call(
        body, name="in_proj_%d" % which, out_shape=jax.ShapeDtypeStruct((m, out_cols), BF16),
        grid_spec=pltpu.PrefetchScalarGridSpec(
            num_scalar_prefetch=1, grid=(m // tm,), in_specs=in_specs,
            out_specs=pl.BlockSpec((tm, cols), lambda i, ids: (i, ids[which]))),
        input_output_aliases={3: 0} if proj is not None else {},
        compiler_params=_params(("parallel",)),
    )(*operands)


BIG = ("w_in", "w_out", "w_gate", "w_up", "w_down")
BIG_AXIS = dict(w_in=1, w_out=0, w_gate=1, w_up=1, w_down=0)
SMALL = ("norm_mix_w", "ret_decay_fwd", "ret_decay_bwd", "ret_norm_w", "norm_ffn_w", "norm_final_w")
ALL_WEIGHTS = ("norm_mix_w", "w_in", "ret_decay_fwd", "ret_decay_bwd", "ret_norm_w", "w_out", "norm_ffn_w",
               "w_gate", "w_up", "w_down", "norm_final_w")
SMALL_ROW = 128 * 8


def _pack_small(small):
    pieces = [jnp.reshape(small["loss"], (1,))] + [jnp.reshape(small[k], (-1,)) for k in SMALL]
    rows = []
    for p in pieces:
        pad = -p.shape[0] % (8 * SMALL_ROW)
        rows.append(jnp.reshape(jnp.pad(p, (0, pad)), (-1, SMALL_ROW)))
    return jnp.concatenate(rows, axis=0)


def _unpack_small(block, like):
    out, row = {}, 0
    for k in ("loss",) + SMALL:
        size = 1 if k == "loss" else like[k].size
        nrows = -(-size // (8 * SMALL_ROW)) * 8
        out[k] = jnp.reshape(block[row:row + nrows], (-1,))[:size]
        row += nrows
    return out


def kernel(x, norm_mix_w, w_in, ret_decay_fwd, ret_decay_bwd, ret_norm_w, w_out, norm_ffn_w, w_gate, w_up, w_down, norm_final_w, loss_target, m_norm_mix_w, m_w_in, m_ret_decay_fwd, m_ret_decay_bwd, m_ret_norm_w, m_w_out, m_norm_ffn_w, m_w_gate, m_w_up, m_w_down, m_norm_final_w, v_norm_mix_w, v_w_in, v_ret_decay_fwd, v_ret_decay_bwd, v_ret_norm_w, v_w_out, v_norm_ffn_w, v_w_gate, v_w_up, v_w_down, v_norm_final_w):
    weights = dict(norm_mix_w=norm_mix_w, w_in=w_in, ret_decay_fwd=ret_decay_fwd, ret_decay_bwd=ret_decay_bwd,
                   ret_norm_w=ret_norm_w, w_out=w_out, norm_ffn_w=norm_ffn_w, w_gate=w_gate, w_up=w_up,
                   w_down=w_down, norm_final_w=norm_final_w)
    m_in = dict(norm_mix_w=m_norm_mix_w, w_in=m_w_in, ret_decay_fwd=m_ret_decay_fwd, ret_decay_bwd=m_ret_decay_bwd,
                ret_norm_w=m_ret_norm_w, w_out=m_w_out, norm_ffn_w=m_norm_ffn_w, w_gate=m_w_gate, w_up=m_w_up,
                w_down=m_w_down, norm_final_w=m_norm_final_w)
    v_in = dict(norm_mix_w=v_norm_mix_w, w_in=v_w_in, ret_decay_fwd=v_ret_decay_fwd, ret_decay_bwd=v_ret_decay_bwd,
                ret_norm_w=v_ret_norm_w, w_out=v_w_out, norm_ffn_w=v_norm_ffn_w, w_gate=v_w_gate, w_up=v_w_up,
                w_down=v_w_down, norm_final_w=v_norm_final_w)
    pos = jnp.stack([lax.axis_index("c"), 2 * lax.axis_index("x") + lax.axis_index("y")]).astype(jnp.int32)

    own = {"w_in": _to_bf16_in_place(weights["w_in"][0], BIG_AXIS["w_in"], pos, name="cast_w_in")}
    w_in_started = _split_gather_start(own["w_in"], BIG_AXIS["w_in"])
    queued, token = {}, w_in_started[2]
    for k in ("w_gate", "w_out", "w_up", "w_down"):
        own[k] = _to_bf16_in_place(weights[k][0], BIG_AXIS[k], pos, name="cast_" + k, after=token)
        queued[k] = _split_start(_gather_job([own[k]], [BIG_AXIS[k]], "ici"), name="all_gather_%s_start" % k)
        token = queued[k]["token"]
    cx, cy = lax.axis_index("x"), lax.axis_index("y")
    shard_ids = jnp.stack([2 * cx + cy, 2 * (1 - cx) + cy, 2 * cx + 1 - cy, 2 * (1 - cx) + 1 - cy]).astype(jnp.int32)

    dx, grad_w, small, w_in_pending = _step(
        x[0], loss_target[0], norm_mix_w, ret_decay_fwd[0], ret_decay_bwd[0], ret_norm_w, norm_ffn_w,
        norm_final_w[None, :], own, w_in_started, queued, shard_ids, pos)

    delta, new_m, new_v = {}, {}, {}

    def update(k, after):
        shape = weights[k].shape
        as2d = (lambda t: jnp.reshape(t, (-1, shape[-1])))
        grad_w[k], delta[k], new_m[k], new_v[k] = (jnp.reshape(t, shape) for t in _adamw(
            as2d(weights[k]), as2d(grad_w[k]), as2d(m_in[k]), as2d(v_in[k]), name="adamw_" + k, after=after))

    others = [k for k in BIG if k != "w_in"]
    for k in others:
        update(k, [w_in_pending["sending"]["token"]])
    _, parts = _split_wait(w_in_pending["sending"], [dx] + [delta[k] for k in others], name="grad_send_w_in_wait")

    half = _sum_chip_parts(w_in_pending["grad"], w_in_pending["received"], parts, BIG_AXIS["w_in"], pos,
                           name="grad_sum_parts_w_in")
    joining = _split_start(_join_job([half], [BIG_AXIS["w_in"]]), name="grad_join_w_in_start")

    like = {k: weights[k] for k in SMALL}
    reduced = _unpack_small(_all_reduce_small(_pack_small(small), after=[joining["token"]]), like)
    loss = reduced["loss"][0]
    for k in SMALL:
        grad_w[k] = jnp.reshape(reduced[k], (1, -1))
        update(k, [])
    (grad_w["w_in"],) = _split_wait(joining, [delta[k] for k in SMALL], name="grad_join_w_in_wait")
    update("w_in", [])

    return (loss, dx[None], *[grad_w[k] for k in ALL_WEIGHTS], *[delta[k] for k in ALL_WEIGHTS],
            *[new_m[k] for k in ALL_WEIGHTS], *[new_v[k] for k in ALL_WEIGHTS])
```

```python
import functools
import math

import numpy as np
import jax
import jax.numpy as jnp
from jax import lax
from jax.experimental import pallas as pl
from jax.experimental.pallas import tpu as pltpu

F32 = jnp.float32
BF16 = jnp.bfloat16
MESH = pl.DeviceIdType.MESH

HEAD_DIM = 128
RET_CHUNK = 128
RET_UNROLL = 8
EPS = 1e-6
DILATED_PATTERNS = ((128, 1), (512, 4), (2048, 16))
ATT_BLOCK = 256
ATT_REACH = max(w // 2 for w, _ in DILATED_PATTERNS)
ATT_NEAR = ATT_BLOCK
ATT_CLASSES = DILATED_PATTERNS[-1][1]
assert all(w // 2 <= ATT_NEAR for w, _ in DILATED_PATTERNS[:-1])
ATT_KB = -(-ATT_NEAR // ATT_BLOCK)
ATT_WINDOW = 2 * ATT_KB + 1
ATT_FAR_GROUP = 8
ATT_NEAR_GROUP = 4
MASKED = -1e30
ROW_MAX_INIT = -1e29
N_CHIPS = 4
VMEM_LIMIT_BYTES = 56 * 1024 * 1024
ELEM_BLOCK_BYTES = 2 * 1024 * 1024
WEIGHT_GRAD_GROUP = 4

ADAM_LR = 0.001
ADAM_B1 = 0.9
ADAM_B2 = 0.999
ADAM_EPS = 1e-08
ADAM_WD = 0.01
ADAM_STEP = 10


def _params(sem=None):
    return pltpu.CompilerParams(dimension_semantics=sem, vmem_limit_bytes=VMEM_LIMIT_BYTES)


def _sigmoid(x):
    return 0.5 * jnp.tanh(0.5 * x) + 0.5


class _Job:
    def __init__(self, *, ins=(), ios=(), outs=(), sems=(), start, finish):
        self.ins, self.ios, self.outs, self.sems = list(ins), list(ios), list(outs), list(sems)
        self.start, self.finish = start, finish

    def results(self):
        return [jax.ShapeDtypeStruct(a.shape, a.dtype) for a in self.ios] + self.outs


def _call(body, *, name, grid, in_specs, out_specs, out_shape, operands, scratch_shapes=(), semantics=None, jobs=(),
          after=(), updates=None):
    in_specs, out_specs, out_shape = list(in_specs), list(out_specs), list(out_shape)
    scratch_shapes = list(scratch_shapes)
    if not jobs:
        n_real = len(in_specs)

        def ordered(*refs):
            body(*refs[:n_real], *refs[n_real + len(after):])

        outs = pl.pallas_call(
            ordered if after else body, name=name, grid=grid,
            in_specs=in_specs + [pl.BlockSpec(memory_space=pl.ANY)] * len(after), out_specs=out_specs,
            out_shape=out_shape, scratch_shapes=scratch_shapes, input_output_aliases=dict(updates or {}),
            compiler_params=_params(semantics))(*operands, *after)
        return outs, []
    n_in, n_out, n_scratch = len(in_specs), len(out_specs), len(scratch_shapes)
    extra_in, extra_out, sems, aliases = [], [], [], dict(updates or {})
    for job in jobs:
        extra_in += job.ins
        for t in range(len(job.ios)):
            aliases[n_in + len(extra_in) + t] = n_out + len(extra_out) + t
        extra_in += job.ios
        extra_out += job.results()
        sems += job.sems

    def carried(*refs):
        x_in = refs[n_in:n_in + len(extra_in)]
        first_out = n_in + len(extra_in) + len(after)
        x_out = refs[first_out + n_out:first_out + n_out + len(extra_out)]
        x_sem = refs[len(refs) - len(sems):]
        views, i_in, i_out, i_sem = [], 0, 0, 0
        for job in jobs:
            data = list(x_in[i_in:i_in + len(job.ins)]) + list(x_out[i_out:i_out + len(job.results())])
            views.append((data, x_sem[i_sem:i_sem + len(job.sems)]))
            i_in += len(job.ins) + len(job.ios)
            i_out += len(job.results())
            i_sem += len(job.sems)
        steps = [pl.program_id(d) for d in range(len(grid))]

        @pl.when(functools.reduce(jnp.logical_and, [s == 0 for s in steps]))
        def _():
            for job, (data, sem) in zip(jobs, views):
                job.start(data, sem)

        body(*refs[:n_in], *refs[first_out:first_out + n_out],
             *refs[len(refs) - len(sems) - n_scratch:len(refs) - len(sems)])

        @pl.when(functools.reduce(jnp.logical_and, [s == g - 1 for s, g in zip(steps, grid)]))
        def _():
            for job, (data, sem) in zip(jobs, views):
                job.finish(data, sem)

    hbm = pl.BlockSpec(memory_space=pl.ANY)
    res = pl.pallas_call(
        carried, name=name, grid=grid, in_specs=in_specs + [hbm] * (len(extra_in) + len(after)),
        out_specs=out_specs + [hbm] * len(extra_out), out_shape=out_shape + extra_out,
        input_output_aliases=aliases, scratch_shapes=scratch_shapes + sems,
        compiler_params=_params(("arbitrary",) * len(grid)),
    )(*operands, *extra_in, *after)
    carried_results, at = [], n_out
    for job in jobs:
        carried_results.append(list(res[at:at + len(job.results())]))
        at += len(job.results())
    return list(res[:n_out]), carried_results


def _run_jobs(jobs, *, name):
    first = jobs[0]
    n_in, n_io = len(first.ins), len(first.ios)
    out_shape = first.results()
    n_sems = [len(job.sems) for job in jobs]

    def body(*refs):
        data = list(refs[:n_in]) + list(refs[n_in + n_io:n_in + n_io + len(out_shape)])
        at = n_in + n_io + len(out_shape)
        for job, ns in zip(jobs, n_sems):
            job.start(data, refs[at:at + ns])
            job.finish(data, refs[at:at + ns])
            at += ns

    hbm = pl.BlockSpec(memory_space=pl.ANY)
    return pl.pallas_call(
        body, name=name, in_specs=[hbm] * (n_in + n_io), out_specs=[hbm] * len(out_shape), out_shape=out_shape,
        input_output_aliases={n_in + t: t for t in range(n_io)},
        scratch_shapes=[s for job in jobs for s in job.sems],
    )(*first.ins, *first.ios)


class _SemaphoreGrid:
    def __init__(self, refs, shape):
        self.refs, self.shape = list(refs), tuple(shape)

    @property
    def at(self):
        return self

    def __getitem__(self, index):
        index = index if isinstance(index, tuple) else (index,)
        flat = 0
        for i, extent in zip(index, self.shape):
            flat = flat * extent + i
        return self.refs[flat]


def _semaphore_grids(job, refs):
    grids, at = [], 0
    for sem in job.sems:
        count = math.prod(sem.shape)
        grids.append(_SemaphoreGrid(refs[at:at + count], sem.shape))
        at += count
    return grids


def _split_start(job, *, name):
    arrays = job.ins + job.ios + [lax.empty(s.shape, s.dtype) for s in job.outs]
    n, ns = len(arrays), sum(math.prod(sem.shape) for sem in job.sems)

    def body(*refs):
        job.start(list(refs[:n]), _semaphore_grids(job, refs[n:n + ns]))
        refs[-1][...] = jnp.zeros_like(refs[-1])

    hbm = pl.BlockSpec(memory_space=pltpu.HBM)
    res = pl.pallas_call(
        body, name=name,
        out_shape=(*[pltpu.SemaphoreType.DMA(())] * ns, *[pltpu.HBM(a.shape, a.dtype) for a in arrays],
                   jax.ShapeDtypeStruct((8, HEAD_DIM), F32)),
        in_specs=[hbm] * n,
        out_specs=(*[pl.BlockSpec(memory_space=pltpu.SEMAPHORE)] * ns, *[hbm] * n,
                   pl.BlockSpec(memory_space=pltpu.VMEM)),
        input_output_aliases={t: ns + t for t in range(n)},
        compiler_params=pltpu.CompilerParams(has_side_effects=pltpu.SideEffectType.DATAFLOW_SIDE_EFFECTING),
    )(*[pltpu.with_memory_space_constraint(a, pltpu.HBM) for a in arrays])
    return dict(job=job, sems=list(res[:ns]), arrays=list(res[ns:ns + n]), token=res[-1])


def _split_wait(started, after, *, name):
    job, arrays, sems = started["job"], started["arrays"], started["sems"]
    n, ns = len(arrays), len(sems)

    def body(*refs):
        job.finish(list(refs[:n]), _semaphore_grids(job, refs[n:n + ns]))

    hbm = pl.BlockSpec(memory_space=pltpu.HBM)
    return pl.pallas_call(
        body, name=name, out_shape=[pltpu.HBM(a.shape, a.dtype) for a in arrays],
        in_specs=[hbm] * n + [pl.BlockSpec(memory_space=pltpu.SEMAPHORE)] * ns
        + [pl.BlockSpec(memory_space=pl.ANY)] * len(after),
        out_specs=[hbm] * n, input_output_aliases={t: t for t in range(n)},
        compiler_params=pltpu.CompilerParams(has_side_effects=pltpu.SideEffectType.DATAFLOW_SIDE_EFFECTING),
    )(*arrays, *sems, *after)


def _dot(a, b, ta=False, tb=False):
    return lax.dot_general(a, b, (((0 if ta else 1,), (1 if tb else 0,)), ((), ())),
                           preferred_element_type=F32)


def _tile(n, want):
    t = min(n, want) // 128 * 128
    while n % t:
        t -= 128
    return t


def _a_spec(ta, tm, tk):
    return pl.BlockSpec((tk, tm), lambda i, j, k: (k, i)) if ta else pl.BlockSpec((tm, tk), lambda i, j, k: (i, k))


def _b_spec(tb, tk, tn):
    return pl.BlockSpec((tn, tk), lambda i, j, k: (j, k)) if tb else pl.BlockSpec((tk, tn), lambda i, j, k: (k, j))


def _accumulate(accs, nk, products, finish):
    if nk == 1:
        finish(*products())
        return
    k = pl.program_id(2)

    @pl.when(k == 0)
    def _():
        for acc, p in zip(accs, products()):
            acc[...] = p

    if nk > 2:
        @pl.when(jnp.logical_and(k > 0, k < nk - 1))
        def _():
            for acc, p in zip(accs, products()):
                acc[...] += p

    @pl.when(k == nk - 1)
    def _():
        finish(*[acc[...] + p for acc, p in zip(accs, products())])


def _matmul(a, b, *, name, ta=False, tb=False, out_dtype=F32, residual=None, tm=1024, tn=1024, tk=2048, jobs=(),
            after=()):
    m, kdim = (a.shape[1], a.shape[0]) if ta else a.shape
    n = b.shape[0] if tb else b.shape[1]
    tm, tn, tk = _tile(m, tm), _tile(n, tn), _tile(kdim, tk)
    nk = kdim // tk

    def body(*refs):
        a_ref, b_ref = refs[:2]
        r_ref = refs[2] if residual is not None else None
        o_ref = refs[-1] if nk == 1 else refs[-2]

        def finish(total):
            if residual is not None:
                total = total + r_ref[...]
            o_ref[...] = total.astype(out_dtype)

        _accumulate(refs[-1:] if nk > 1 else (), nk, lambda: (_dot(a_ref[...], b_ref[...], ta, tb),), finish)

    o_spec = pl.BlockSpec((tm, tn), lambda i, j, k: (i, j))
    in_specs = [_a_spec(ta, tm, tk), _b_spec(tb, tk, tn)]
    operands = [a, b]
    if residual is not None:
        in_specs.append(o_spec)
        operands.append(residual)
    (out,), carried = _call(
        body, name=name, grid=(m // tm, n // tn, nk), in_specs=in_specs, out_specs=[o_spec],
        out_shape=[jax.ShapeDtypeStruct((m, n), out_dtype)], operands=operands,
        scratch_shapes=[pltpu.VMEM((tm, tn), F32)] * (nk > 1),
        semantics=("parallel", "parallel", "arbitrary"), jobs=jobs, after=after)
    return (out, carried) if jobs else out


def _matmul_pieces_nt(pieces, b, *, name, tm=512, tn=1024, jobs=(), after=()):
    m, kp = pieces[0].shape
    n = b.shape[0]
    tm, tn = _tile(m, tm), _tile(n, tn)
    count = len(pieces)

    def body(*refs):
        b_ref, o_ref = refs[count], refs[count + 1]
        total = _dot(refs[0][...], b_ref[:, pl.ds(0, kp)], tb=True)
        for p in range(1, count):
            total = total + _dot(refs[p][...], b_ref[:, pl.ds(p * kp, kp)], tb=True)
        o_ref[...] = total

    piece = pl.BlockSpec((tm, kp), lambda j, i: (i, 0))
    (out,), carried = _call(
        body, name=name, grid=(n // tn, m // tm),
        in_specs=[piece] * count + [pl.BlockSpec((tn, count * kp), lambda j, i: (j, 0))],
        out_specs=[pl.BlockSpec((tm, tn), lambda j, i: (i, j))],
        out_shape=[jax.ShapeDtypeStruct((m, n), F32)], operands=[*pieces, b],
        semantics=("parallel", "parallel"), jobs=jobs, after=after)
    return (out, carried) if jobs else out


def _weight_grad_pieces(a, pieces, *, name):
    tokens, m = a.shape
    np_ = pieces[0].shape[1]
    tm = 1024 if m % 1024 == 0 else _tile(m, 1408)
    tn = _tile(np_, 512)
    nb = np_ // tn
    out = None
    for first in range(0, len(pieces), WEIGHT_GRAD_GROUP):
        group = pieces[first:first + WEIGHT_GRAD_GROUP]

        def body(*refs, count=len(group)):
            t_now = pl.program_id(1) // nb
            for t in range(count):
                @pl.when(t_now == t)
                def _(t=t):
                    refs[-1][...] = _dot(refs[0][...], refs[1 + t][...], ta=True)

        def piece_spec(t):
            return pl.BlockSpec((tokens, tn), lambda i, j: (0, jnp.clip(j - t * nb, 0, nb - 1)))

        in_specs = [pl.BlockSpec((tokens, tm), lambda i, j: (0, i))] + [piece_spec(t) for t in range(len(group))]
        operands = [a, *group]
        if out is not None:
            in_specs.append(pl.BlockSpec(memory_space=pl.ANY))
            operands.append(out)
        out = pl.pallas_call(
            body, name="%s_%d" % (name, first), grid=(m // tm, nb * len(group)), in_specs=in_specs,
            out_specs=pl.BlockSpec((tm, tn), lambda i, j, first=first: (i, first * nb + j)),
            out_shape=jax.ShapeDtypeStruct((m, len(pieces) * np_), F32),
            input_output_aliases={len(operands) - 1: 0} if out is not None else {},
            compiler_params=_params(("parallel", "arbitrary")),
        )(*operands)
    return out


def _weight_grad(a, g, *, name, jobs=(), after=()):
    tokens, m = a.shape
    tm = 1024 if m % 1024 == 0 else _tile(m, 1408)
    return _matmul(a, g, name=name, ta=True, tm=tm, tn=512, tk=tokens, jobs=jobs, after=after)


def _swiglu_fwd(n2, w_gate, w_up, *, tm=1024, tn=512, tk=2048, jobs=()):
    m, kdim = n2.shape
    n = w_gate.shape[1]
    tm, tn, tk = _tile(m, tm), _tile(n, tn), _tile(kdim, tk)
    nk = kdim // tk

    def body(a_ref, g_ref, u_ref, gate_ref, up_ref, act_ref, *acc):
        def products():
            a = a_ref[...]
            return _dot(a, g_ref[...]), _dot(a, u_ref[...])

        def finish(g, u):
            gate_ref[...] = g.astype(BF16)
            up_ref[...] = u.astype(BF16)
            act_ref[...] = (g * _sigmoid(g) * u).astype(BF16)

        _accumulate(acc, nk, products, finish)

    o_spec = pl.BlockSpec((tm, tn), lambda i, j, k: (i, j))
    o_shape = jax.ShapeDtypeStruct((m, n), BF16)
    return _call(
        body, name="swiglu_fwd", grid=(m // tm, n // tn, nk),
        in_specs=[_a_spec(False, tm, tk), _b_spec(False, tk, tn), _b_spec(False, tk, tn)],
        out_specs=[o_spec] * 3, out_shape=[o_shape] * 3, operands=[n2, w_gate, w_up],
        scratch_shapes=[pltpu.VMEM((tm, tn), F32)] * (2 * (nk > 1)),
        semantics=("parallel", "parallel", "arbitrary"), jobs=jobs)


def _swiglu_bwd_act(dh2, w_down, gate, up, *, tm=1024, tn=512, tk=2048):
    m, kdim = dh2.shape
    n = w_down.shape[0]
    tm, tn, tk = _tile(m, tm), _tile(n, tn), _tile(kdim, tk)
    nk = kdim // tk

    sub = _tile(tn, 256)

    def body(a_ref, b_ref, gate_ref, up_ref, dgate_ref, dup_ref, *acc):
        def finish(dact, cols=slice(None)):
            g = gate_ref[:, cols].astype(F32)
            u = up_ref[:, cols].astype(F32)
            sg = _sigmoid(g)
            dup_ref[:, cols] = (dact * g * sg).astype(BF16)
            dgate_ref[:, cols] = (dact * u * sg * (1.0 + g * (1.0 - sg))).astype(BF16)

        if nk == 1:
            a = a_ref[...]
            for c in range(tn // sub):
                cols = pl.ds(c * sub, sub)
                finish(_dot(a, b_ref[cols, :], tb=True), cols)
        else:
            _accumulate(acc, nk, lambda: (_dot(a_ref[...], b_ref[...], tb=True),), finish)

    o_spec = pl.BlockSpec((tm, tn), lambda i, j, k: (i, j))
    o_shape = jax.ShapeDtypeStruct((m, n), BF16)
    return pl.pallas_call(
        body, name="swiglu_bwd_act", grid=(m // tm, n // tn, nk),
        in_specs=[_a_spec(False, tm, tk), _b_spec(True, tk, tn), o_spec, o_spec],
        out_specs=[o_spec] * 2, out_shape=[o_shape] * 2,
        scratch_shapes=[pltpu.VMEM((tm, tn), F32)] * (nk > 1),
        compiler_params=_params(("parallel", "parallel", "arbitrary")),
    )(dh2, w_down, gate, up)


def _swiglu_bwd_in(dgate, dup, w_gate, w_up, *, tm=1024, tn=1024, tk=1408, jobs=(), after=()):
    m, kdim = dgate.shape
    n = w_gate.shape[0]
    tm, tn, tk = _tile(m, tm), _tile(n, tn), _tile(kdim, tk)
    nk = kdim // tk

    def body(a1_ref, a2_ref, b1_ref, b2_ref, o_ref, *acc):
        def product():
            return (_dot(a1_ref[...], b1_ref[...], tb=True) + _dot(a2_ref[...], b2_ref[...], tb=True),)

        def finish(total):
            o_ref[...] = total

        _accumulate(acc, nk, product, finish)

    a_spec, b_spec = _a_spec(False, tm, tk), _b_spec(True, tk, tn)
    (out,), carried = _call(
        body, name="swiglu_bwd_in", grid=(m // tm, n // tn, nk),
        in_specs=[a_spec, a_spec, b_spec, b_spec],
        out_specs=[pl.BlockSpec((tm, tn), lambda i, j, k: (i, j))],
        out_shape=[jax.ShapeDtypeStruct((m, n), F32)], operands=[dgate, dup, w_gate, w_up],
        scratch_shapes=[pltpu.VMEM((tm, tn), F32)] * (nk > 1),
        semantics=("parallel", "parallel", "arbitrary"), jobs=jobs, after=after)
    return out, carried


def _row_block(rows, cols):
    tr = min(rows, max(16, ELEM_BLOCK_BYTES // (4 * cols) // 16 * 16))
    while rows % tr:
        tr -= 16
    return tr


def _rmsnorm_fwd(x, g, *, name, after=None):
    s, d = x.shape
    tr = _row_block(s, d)

    def body(x_ref, g_ref, *rest):
        xv = x_ref[...]
        r = lax.rsqrt(jnp.mean(xv * xv, axis=-1, keepdims=True) + EPS)
        rest[-1][...] = (xv * r * g_ref[...]).astype(BF16)

    row = pl.BlockSpec((tr, d), lambda i: (i, 0))
    in_specs = [row, pl.BlockSpec((1, d), lambda i: (0, 0))]
    operands = [x, g]
    if after is not None:
        in_specs.append(pl.BlockSpec(after.shape, lambda i: (0, 0)))
        operands.append(after)
    return pl.pallas_call(
        body, name=name, grid=(s // tr,), in_specs=in_specs,
        out_specs=row, out_shape=jax.ShapeDtypeStruct((s, d), BF16),
        compiler_params=_params(("parallel",)),
    )(*operands)


def _rmsnorm_bwd_rows(xv, gv, dy):
    r = lax.rsqrt(jnp.mean(xv * xv, axis=-1, keepdims=True) + EPS)
    xhat = xv * r
    dxh = dy * gv
    dx = r * (dxh - xhat * jnp.mean(dxh * xhat, axis=-1, keepdims=True))
    return dx, dy * xhat


def _rmsnorm_bwd(dn, x, g, skip, *, name, after=()):
    s, d = x.shape
    tr = _row_block(s, d)

    def body(dn_ref, x_ref, g_ref, skip_ref, *rest):
        dx_ref, dxb_ref, dg_ref = rest[len(after):]
        dx, dgr = _rmsnorm_bwd_rows(x_ref[...], g_ref[...], dn_ref[...])
        dx = dx + skip_ref[...]
        dx_ref[...] = dx
        dxb_ref[...] = dx.astype(BF16)

        @pl.when(pl.program_id(0) == 0)
        def _():
            dg_ref[...] = jnp.zeros_like(dg_ref)

        dg_ref[...] += jnp.sum(dgr, axis=0, keepdims=True)

    row = pl.BlockSpec((tr, d), lambda i: (i, 0))
    vec = pl.BlockSpec((1, d), lambda i: (0, 0))
    return pl.pallas_call(
        body, name=name, grid=(s // tr,),
        in_specs=[row, row, vec, row] + [pl.BlockSpec(memory_space=pl.ANY)] * len(after),
        out_specs=[row, row, vec],
        out_shape=[jax.ShapeDtypeStruct((s, d), F32), jax.ShapeDtypeStruct((s, d), BF16),
                   jax.ShapeDtypeStruct((1, d), F32)],
        compiler_params=_params(("arbitrary",)),
    )(dn, x, g, skip, *after)


def _loss_head(h2, g, target):
    s, d = h2.shape
    tr = _row_block(s, d)

    def body(h_ref, g_ref, t_ref, dh_ref, dhb_ref, dg_ref, loss_ref):
        hv = h_ref[...]
        gv = g_ref[...]
        r = lax.rsqrt(jnp.mean(hv * hv, axis=-1, keepdims=True) + EPS)
        err = hv * r * gv - t_ref[...]
        dx, dgr = _rmsnorm_bwd_rows(hv, gv, err * (1.0 / d))
        dh_ref[...] = dx
        dhb_ref[...] = dx.astype(BF16)

        @pl.when(pl.program_id(0) == 0)
        def _():
            dg_ref[...] = jnp.zeros_like(dg_ref)
            loss_ref[...] = jnp.zeros_like(loss_ref)

        dg_ref[...] += jnp.sum(dgr, axis=0, keepdims=True)
        row_loss = jnp.mean(err * err, axis=-1, keepdims=True)
        loss_ref[...] += 0.5 * jnp.sum(row_loss, axis=0, keepdims=True)

    row = pl.BlockSpec((tr, d), lambda i: (i, 0))
    vec = pl.BlockSpec((1, d), lambda i: (0, 0))
    one = pl.BlockSpec((1, 1), lambda i: (0, 0))
    return pl.pallas_call(
        body, name="loss_head", grid=(s // tr,), in_specs=[row, vec, row],
        out_specs=[row, row, vec, one],
        out_shape=[jax.ShapeDtypeStruct((s, d), F32), jax.ShapeDtypeStruct((s, d), BF16),
                   jax.ShapeDtypeStruct((1, d), F32), jax.ShapeDtypeStruct((1, 1), F32)],
        compiler_params=_params(("arbitrary",)),
    )(h2, g, target)


def _attention_bias_tables():
    k = np.arange(-ATT_KB, ATT_KB + 1)[:, None, None]
    delta = k * ATT_BLOCK + np.arange(ATT_BLOCK)[None, None, :] - np.arange(ATT_BLOCK)[None, :, None]
    dist = np.abs(delta)
    count = np.zeros(delta.shape, np.int32)
    for window, dilation in DILATED_PATTERNS:
        count += (delta % dilation == 0) & (dist <= min(window // 2, ATT_NEAR))
    logc = np.where(count > 0, np.log(np.maximum(count, 1)), MASKED)
    return dist.astype(np.float32), logc.astype(np.float32)


def _far_bias_tables(per_class):
    steps = np.abs(np.arange(per_class)[:, None] - np.arange(per_class)[None, :]) * ATT_CLASSES
    valid = (steps > ATT_NEAR) & (steps <= ATT_REACH)
    return steps.astype(np.float32), np.where(valid, 0.0, MASKED).astype(np.float32)


def _to_classes(x):
    s, cols = x.shape
    return jnp.reshape(jnp.transpose(jnp.reshape(x, (s // ATT_CLASSES, ATT_CLASSES, cols)), (1, 0, 2)), (s, cols))


def _from_classes(x):
    s, cols = x.shape
    return jnp.reshape(jnp.transpose(jnp.reshape(x, (ATT_CLASSES, s // ATT_CLASSES, cols)), (1, 0, 2)), (s, cols))


def _head_bias(bias_ref, slope, dist_ref, logc_ref):
    for kk in range(ATT_WINDOW):
        bias_ref[kk] = logc_ref[kk] - slope * dist_ref[kk]
    bias_ref[ATT_WINDOW] = jnp.full((ATT_BLOCK, ATT_BLOCK), MASKED, F32)


def _window_start(i, nq, nwin):
    return jnp.clip(i - ATT_KB, 0, nq - nwin)


def _window_block(j, i):
    rows = pl.ds(pl.multiple_of(j * ATT_BLOCK, ATT_BLOCK), ATT_BLOCK)
    kk = j - i + ATT_KB
    return rows, jnp.where(jnp.logical_and(kk >= 0, kk < ATT_WINDOW), kk, ATT_WINDOW)


def _attention_far_fwd(qkv, slopes, n_heads, jobs=(), after=()):
    s = qkv.shape[0]
    per_class = s // ATT_CLASSES
    scale = HEAD_DIM ** -0.5
    dist, logc = _far_bias_tables(per_class)

    def body(slope_ref, q_ref, k_ref, v_ref, dist_ref, logc_ref, o_ref, lse_ref):
        bias = logc_ref[...] - slope_ref[pl.program_id(0)] * dist_ref[...]
        for a in range(ATT_FAR_GROUP):
            rows = pl.ds(a * per_class, per_class)
            sc = _dot(q_ref[rows, :], k_ref[rows, :], tb=True) * scale + bias
            m = jnp.maximum(jnp.max(sc, axis=-1, keepdims=True), ROW_MAX_INIT)
            p = jnp.exp(sc - m)
            l = jnp.maximum(jnp.sum(p, axis=-1, keepdims=True), 1e-30)
            o_ref[rows, :] = (_dot(p.astype(BF16), v_ref[rows, :]) / l).astype(BF16)
            lse_ref[rows, :] = jnp.broadcast_to(m + jnp.log(l), (per_class, HEAD_DIM))

    hh = n_heads
    blk = pl.BlockSpec((ATT_FAR_GROUP * per_class, HEAD_DIM), lambda h, r: (r, h))
    table = pl.BlockSpec(dist.shape, lambda h, r: (0, 0))
    return _call(
        body, name="attention_far_fwd", grid=(hh, ATT_CLASSES // ATT_FAR_GROUP),
        in_specs=[pl.BlockSpec(memory_space=pltpu.SMEM), blk,
                  pl.BlockSpec((ATT_FAR_GROUP * per_class, HEAD_DIM), lambda h, r: (r, hh + h)),
                  pl.BlockSpec((ATT_FAR_GROUP * per_class, HEAD_DIM), lambda h, r: (r, 2 * hh + h)), table, table],
        out_specs=[blk, blk],
        out_shape=[jax.ShapeDtypeStruct((s, hh * HEAD_DIM), BF16), jax.ShapeDtypeStruct((s, hh * HEAD_DIM), F32)],
        operands=[slopes, qkv, qkv, qkv, jnp.asarray(dist), jnp.asarray(logc)],
        semantics=("parallel", "parallel"), jobs=jobs, after=after)


def _attention_fwd(proj, slopes, far_out, far_lse, n_heads, jobs=(), after=()):
    s = proj.shape[0]
    nq = s // ATT_BLOCK
    scale = HEAD_DIM ** -0.5
    dist, logc = _attention_bias_tables()

    nwin = min(ATT_WINDOW, nq)

    group = math.gcd(ATT_NEAR_GROUP, nq)

    def body(slope_ref, q_ref, k_ref, v_ref, fo_ref, fl_ref, dist_ref, logc_ref, o_ref, lse_ref, bias_ref, s_ref):
        h, step = pl.program_id(0), pl.program_id(1)

        @pl.when(step == 0)
        def _():
            _head_bias(bias_ref, slope_ref[h], dist_ref, logc_ref)

        for a in range(group):
            i = step * group + a
            mine = pl.ds(a * ATT_BLOCK, ATT_BLOCK)
            q = q_ref[mine, :]
            first = _window_start(i, nq, nwin)
            m = jnp.full((ATT_BLOCK, 1), ROW_MAX_INIT, F32)
            for b in range(nwin):
                rows, kk = _window_block(first + b, i)
                sc = _dot(q, k_ref[rows, :], tb=True) * scale + bias_ref[kk]
                s_ref[a * nwin + b] = sc
                m = jnp.maximum(m, jnp.max(sc, axis=-1, keepdims=True))
            l = jnp.zeros((ATT_BLOCK, 1), F32)
            acc = jnp.zeros((ATT_BLOCK, HEAD_DIM), F32)
            for b in range(nwin):
                rows, _ = _window_block(first + b, i)
                p = jnp.exp(s_ref[a * nwin + b] - m)
                l = l + jnp.sum(p, axis=-1, keepdims=True)
                acc = acc + _dot(p.astype(BF16), v_ref[rows, :])
            near_lse = m + jnp.log(l)
            far_lse_col = fl_ref[mine, :1]
            lse = jnp.maximum(near_lse, far_lse_col)
            lse = lse + jnp.log(jnp.exp(near_lse - lse) + jnp.exp(far_lse_col - lse))
            o_ref[mine, :] = (acc * (jnp.exp(near_lse - lse) / l)
                              + fo_ref[mine, :].astype(F32) * jnp.exp(far_lse_col - lse)).astype(BF16)
            lse_ref[mine, :] = jnp.broadcast_to(lse, (ATT_BLOCK, HEAD_DIM))

    hh = n_heads
    blk = pl.BlockSpec((group * ATT_BLOCK, HEAD_DIM), lambda h, i: (i, h))
    table = pl.BlockSpec(dist.shape, lambda h, i: (0, 0, 0))
    return _call(
        body, name="attention_fwd", grid=(hh, nq // group),
        in_specs=[pl.BlockSpec(memory_space=pltpu.SMEM), blk,
                  pl.BlockSpec((s, HEAD_DIM), lambda h, i: (0, hh + h)),
                  pl.BlockSpec((s, HEAD_DIM), lambda h, i: (0, 2 * hh + h)), blk, blk, table, table],
        out_specs=[blk, blk],
        out_shape=[jax.ShapeDtypeStruct((s, hh * HEAD_DIM), BF16), jax.ShapeDtypeStruct((s, hh * HEAD_DIM), F32)],
        operands=[slopes, proj, proj, proj, far_out, far_lse, jnp.asarray(dist), jnp.asarray(logc)],
        scratch_shapes=[pltpu.VMEM((ATT_WINDOW + 1, ATT_BLOCK, ATT_BLOCK), F32),
                        pltpu.VMEM((group * nwin, ATT_BLOCK, ATT_BLOCK), F32)],
        semantics=("parallel", "arbitrary"), jobs=jobs, after=after)


def _attention_far_bwd(qkv, slopes, out, dout, lse, n_heads):
    s = qkv.shape[0]
    per_class = s // ATT_CLASSES
    scale = HEAD_DIM ** -0.5
    dist, logc = _far_bias_tables(per_class)

    def body(slope_ref, q_ref, k_ref, v_ref, o_ref, do_ref, lse_ref, dist_ref, logc_ref, dq_ref, dk_ref, dv_ref):
        bias = logc_ref[...] - slope_ref[pl.program_id(0)] * dist_ref[...]
        for a in range(ATT_FAR_GROUP):
            rows = pl.ds(a * per_class, per_class)
            q, k, do = q_ref[rows, :], k_ref[rows, :], do_ref[rows, :]
            delta = jnp.sum(do.astype(F32) * o_ref[rows, :].astype(F32), axis=-1, keepdims=True)
            p = jnp.exp(_dot(q, k, tb=True) * scale + bias - lse_ref[rows, :1])
            dv_ref[rows, :] = _dot(p.astype(BF16), do, ta=True).astype(BF16)
            ds = (p * (_dot(do, v_ref[rows, :], tb=True) - delta) * scale).astype(BF16)
            dk_ref[rows, :] = _dot(ds, q, ta=True).astype(BF16)
            dq_ref[rows, :] = _dot(ds, k).astype(BF16)

    hh = n_heads
    blk = pl.BlockSpec((ATT_FAR_GROUP * per_class, HEAD_DIM), lambda h, r: (r, h))
    table = pl.BlockSpec(dist.shape, lambda h, r: (0, 0))
    o_shape = jax.ShapeDtypeStruct((s, hh * HEAD_DIM), BF16)
    return pl.pallas_call(
        body, name="attention_far_bwd", grid=(hh, ATT_CLASSES // ATT_FAR_GROUP),
        in_specs=[pl.BlockSpec(memory_space=pltpu.SMEM), blk,
                  pl.BlockSpec((ATT_FAR_GROUP * per_class, HEAD_DIM), lambda h, r: (r, hh + h)),
                  pl.BlockSpec((ATT_FAR_GROUP * per_class, HEAD_DIM), lambda h, r: (r, 2 * hh + h)),
                  blk, blk, blk, table, table],
        out_specs=[blk] * 3, out_shape=[o_shape] * 3,
        compiler_params=_params(("parallel", "parallel")),
    )(slopes, qkv, qkv, qkv, out, dout, lse, jnp.asarray(dist), jnp.asarray(logc))


def _attention_bwd(proj, slopes, out, lse, dmixed, far_grads, n_heads, jobs=(), after=()):
    s = proj.shape[0]
    nq = s // ATT_BLOCK
    scale = HEAD_DIM ** -0.5
    dist, logc = _attention_bias_tables()

    nwin = min(ATT_WINDOW, nq)
    group = math.gcd(ATT_NEAR_GROUP, nq)

    def body(slope_ref, q_ref, k_ref, v_ref, o_ref, do_ref, lse_ref, fdq_ref, fdk_ref, fdv_ref, dist_ref, logc_ref,
             dq_ref, dk_ref, dv_ref, dk_acc, dv_acc, bias_ref):
        h, step = pl.program_id(0), pl.program_id(1)

        @pl.when(step == 0)
        def _():
            dk_acc[...] = jnp.zeros_like(dk_acc)
            dv_acc[...] = jnp.zeros_like(dv_acc)
            _head_bias(bias_ref, slope_ref[h], dist_ref, logc_ref)

        for a in range(group):
            i = step * group + a
            mine = pl.ds(a * ATT_BLOCK, ATT_BLOCK)
            q = q_ref[mine, :]
            do = do_ref[mine, :]
            lse_col = lse_ref[mine, :1]
            delta = jnp.sum(do.astype(F32) * o_ref[mine, :].astype(F32), axis=-1, keepdims=True)
            first = _window_start(i, nq, nwin)
            dq = jnp.zeros((ATT_BLOCK, HEAD_DIM), F32)
            for b in range(nwin):
                rows, kk = _window_block(first + b, i)
                kj = k_ref[rows, :]
                vj = v_ref[rows, :]
                p = jnp.exp(_dot(q, kj, tb=True) * scale + bias_ref[kk] - lse_col)
                dv_acc[rows, :] += _dot(p.astype(BF16), do, ta=True)
                dp = _dot(do, vj, tb=True)
                ds = (p * (dp - delta) * scale).astype(BF16)
                dk_acc[rows, :] += _dot(ds, q, ta=True)
                dq = dq + _dot(ds, kj)
            dq_ref[mine, :] = (dq + fdq_ref[mine, :].astype(F32)).astype(BF16)

        @pl.when(step == nq // group - 1)
        def _():
            dk_ref[...] = (dk_acc[...] + fdk_ref[...].astype(F32)).astype(BF16)
            dv_ref[...] = (dv_acc[...] + fdv_ref[...].astype(F32)).astype(BF16)

    hh = n_heads
    blk = pl.BlockSpec((group * ATT_BLOCK, HEAD_DIM), lambda h, i: (i, h))
    col = pl.BlockSpec((s, HEAD_DIM), lambda h, i: (0, h))
    table = pl.BlockSpec(dist.shape, lambda h, i: (0, 0, 0))
    o_shape = jax.ShapeDtypeStruct((s, hh * HEAD_DIM), BF16)
    return _call(
        body, name="attention_bwd", grid=(hh, nq // group),
        in_specs=[pl.BlockSpec(memory_space=pltpu.SMEM), blk,
                  pl.BlockSpec((s, HEAD_DIM), lambda h, i: (0, hh + h)),
                  pl.BlockSpec((s, HEAD_DIM), lambda h, i: (0, 2 * hh + h)),
                  blk, blk, blk, blk, col, col, table, table],
        out_specs=[blk, col, col], out_shape=[o_shape] * 3,
        operands=[slopes, proj, proj, proj, out, dmixed, lse, *far_grads, jnp.asarray(dist), jnp.asarray(logc)],
        scratch_shapes=[pltpu.VMEM((s, HEAD_DIM), F32)] * 2
        + [pltpu.VMEM((ATT_WINDOW + 1, ATT_BLOCK, ATT_BLOCK), F32)],
        semantics=("parallel", "arbitrary"), jobs=jobs, after=after)


def _ret_decays(lgc, lga, strict_c, strict_a):
    c = RET_CHUNK
    rel = (lax.broadcasted_iota(jnp.int32, (c, c), 0) - lax.broadcasted_iota(jnp.int32, (c, c), 1)).astype(F32)
    in_c = (rel > 0) if strict_c else (rel >= 0)
    in_a = (rel < 0) if strict_a else (rel <= 0)
    mask = (jnp.where(in_c, jnp.exp(lgc * jnp.maximum(rel, 0.0)), 0.0)
            + jnp.where(in_a, jnp.exp(lga * jnp.maximum(-rel, 0.0)), 0.0))
    idx = lax.broadcasted_iota(jnp.int32, (c, 1), 0).astype(F32)
    ones = jnp.ones((1, HEAD_DIM), F32)
    dec = dict(
        rel=rel, mask=mask, idx=idx,
        a_c=jnp.exp(lgc * (idx + 1.0)), b_c=jnp.exp(lgc * (c - 1.0 - idx)), chunk_c=jnp.exp(ones * (lgc * c)),
        a_a=jnp.exp(lga * (c - idx)), b_a=jnp.exp(lga * idx), chunk_a=jnp.exp(ones * (lga * c)),
    )
    return dec


def _scaled(x, col):
    return (x.astype(F32) * col).astype(BF16)


def _chunk_rows(i):
    return pl.ds(pl.multiple_of(i * RET_CHUNK, RET_CHUNK), RET_CHUNK)


def _chunk_loop(nc, step, init, unroll=RET_UNROLL):
    group = math.gcd(nc, unroll)

    def trip(t, carry):
        for u in range(group):
            carry = step(t * group + u, carry)
        return carry

    return lax.fori_loop(0, nc // group, trip, init)


def _retention(a, b, c, lg_c, lg_a, *, strict_c, strict_a, scale, n_heads, name, gate=None, norm_w=None, jobs=(),
               heads=None, so_far=None, after=()):
    s = a[0].shape[0]
    nc = s // RET_CHUNK
    epilogue = gate is not None
    first_head, head_count = heads if heads is not None else (0, n_heads)

    def body(*refs):
        lgc_ref, lga_ref, a_ref, b_ref, c_ref = refs[:5]
        if epilogue:
            g_ref, w_ref = refs[5:7]
            o_ref, mix_ref, sa_ref = refs[-3:]
        else:
            o_ref, sa_ref = refs[-2:]
        h = first_head + pl.program_id(0)
        dec = _ret_decays(lgc_ref[h], lga_ref[h], strict_c, strict_a)

        def reverse(t, state):
            i = nc - 1 - t
            sa_ref[i] = state.astype(BF16)
            rows = _chunk_rows(i)
            return state * dec["chunk_a"] + _dot(_scaled(b_ref[rows, :], dec["b_a"]), c_ref[rows, :], ta=True)

        _chunk_loop(nc, reverse, jnp.zeros((HEAD_DIM, HEAD_DIM), F32))

        def forward(i, state):
            rows = _chunk_rows(i)
            ai, bi, ci = a_ref[rows, :], b_ref[rows, :], c_ref[rows, :]
            inner = (_dot(ai, bi, tb=True) * dec["mask"]).astype(BF16)
            out = (_dot(inner, ci) + _dot(_scaled(ai, dec["a_c"]), state.astype(BF16))
                   + _dot(_scaled(ai, dec["a_a"]), sa_ref[i])) * scale
            o_ref[rows, :] = out.astype(BF16)
            if epilogue:
                r = lax.rsqrt(jnp.mean(out * out, axis=-1, keepdims=True) + EPS)
                g = g_ref[rows, :].astype(F32)
                mix_ref[rows, :] = (out * r * w_ref[...] * (g * _sigmoid(g))).astype(BF16)
            return state * dec["chunk_c"] + _dot(_scaled(bi, dec["b_c"]), ci, ta=True)

        _chunk_loop(nc, forward, jnp.zeros((HEAD_DIM, HEAD_DIM), F32))

    def col(first):
        return pl.BlockSpec((s, HEAD_DIM), lambda h: (0, first + first_head + h))

    smem = pl.BlockSpec(memory_space=pltpu.SMEM)
    in_specs = [smem, smem, col(a[1]), col(b[1]), col(c[1])]
    operands = [lg_c, lg_a, a[0], b[0], c[0]]
    o_shape = jax.ShapeDtypeStruct((s, n_heads * HEAD_DIM), BF16)
    out_specs, out_shape = [col(0)], [o_shape]
    if epilogue:
        in_specs += [col(gate[1]), pl.BlockSpec((1, HEAD_DIM), lambda h: (0, first_head + h))]
        operands += [gate[0], norm_w]
        out_specs, out_shape = [col(0)] * 2, [o_shape] * 2
    updates = None
    if so_far is not None:
        updates = {len(operands) + t: t for t in range(len(so_far))}
        in_specs += [pl.BlockSpec(memory_space=pl.ANY)] * len(so_far)
        operands += list(so_far)
    res, carried = _call(
        body, name=name, grid=(head_count,), in_specs=in_specs, out_specs=out_specs, out_shape=out_shape,
        operands=operands, scratch_shapes=[pltpu.VMEM((nc, HEAD_DIM, HEAD_DIM), BF16)],
        semantics=("parallel",), jobs=jobs, updates=updates, after=after)
    res = res if epilogue else res[0]
    return (res, carried) if jobs else res


def _retention_decay_grads(a, b, c, e, lg_c, lg_a, *, scale, n_heads):
    s = a[0].shape[0]
    nc = s // RET_CHUNK
    cf = float(RET_CHUNK)

    def body(lgc_ref, lga_ref, a_ref, b_ref, c_ref, e_ref, gc_ref, ga_ref, sa_ref, ta_ref):
        h = pl.program_id(0)
        lgc, lga = lgc_ref[h], lga_ref[h]
        dec = _ret_decays(lgc, lga, True, True)
        rel, idx = dec["rel"], dec["idx"]
        w_c = jnp.where(rel > 0, rel * jnp.exp(lgc * jnp.maximum(rel, 0.0)), 0.0)
        w_a = jnp.where(rel < 0, -rel * jnp.exp(lga * jnp.maximum(-rel, 0.0)), 0.0)
        zero = jnp.zeros((HEAD_DIM, HEAD_DIM), F32)

        def reverse(t, carry):
            st, dst = carry
            i = nc - 1 - t
            sa_ref[i] = st.astype(BF16)
            ta_ref[i] = dst.astype(BF16)
            rows = _chunk_rows(i)
            bi, ci = b_ref[rows, :], c_ref[rows, :]
            st_new = st * dec["chunk_a"] + _dot(_scaled(bi, dec["b_a"]), ci, ta=True)
            dst_new = (cf * st + dst) * dec["chunk_a"] + _dot(_scaled(bi, idx * dec["b_a"]), ci, ta=True)
            return st_new, dst_new

        _chunk_loop(nc, reverse, (zero, zero))

        def forward(i, carry):
            st, dst, acc_c, acc_a = carry
            rows = _chunk_rows(i)
            ai, bi, ci = a_ref[rows, :], b_ref[rows, :], c_ref[rows, :]
            ev = e_ref[rows, :].astype(F32)
            pg = _dot(ai, bi, tb=True) * _dot(e_ref[rows, :], ci, tb=True)
            a_c, a_a = _scaled(ai, dec["a_c"]), _scaled(ai, dec["a_a"])
            inter_c = _dot(a_c, st.astype(BF16)) * (idx + 1.0) + _dot(a_c, dst.astype(BF16))
            inter_a = _dot(a_a, sa_ref[i]) * (cf - idx) + _dot(a_a, ta_ref[i])
            acc_c = acc_c + jnp.sum(pg * w_c, axis=0, keepdims=True) + jnp.sum(inter_c * ev, axis=0, keepdims=True)
            acc_a = acc_a + jnp.sum(pg * w_a, axis=0, keepdims=True) + jnp.sum(inter_a * ev, axis=0, keepdims=True)
            st_new = st * dec["chunk_c"] + _dot(_scaled(bi, dec["b_c"]), ci, ta=True)
            dst_new = ((cf * st + dst) * dec["chunk_c"]
                       + _dot(_scaled(bi, (cf - 1.0 - idx) * dec["b_c"]), ci, ta=True))
            return st_new, dst_new, acc_c, acc_a

        row = jnp.zeros((1, HEAD_DIM), F32)
        _, _, acc_c, acc_a = _chunk_loop(nc, forward, (zero, zero, row, row))
        gc_ref[...] = jnp.broadcast_to(jnp.sum(acc_c, axis=-1, keepdims=True) * scale, gc_ref.shape)
        ga_ref[...] = jnp.broadcast_to(jnp.sum(acc_a, axis=-1, keepdims=True) * scale, ga_ref.shape)

    def col(first):
        return pl.BlockSpec((s, HEAD_DIM), lambda h: (0, first + h))

    smem = pl.BlockSpec(memory_space=pltpu.SMEM)
    o_spec = pl.BlockSpec((1, 8, HEAD_DIM), lambda h: (h, 0, 0))
    o_shape = jax.ShapeDtypeStruct((n_heads, 8, HEAD_DIM), F32)
    gc, ga = pl.pallas_call(
        body, name="retention_decay_grads", grid=(n_heads,),
        in_specs=[smem, smem, col(a[1]), col(b[1]), col(c[1]), col(e[1])],
        out_specs=[o_spec] * 2, out_shape=[o_shape] * 2,
        scratch_shapes=[pltpu.VMEM((nc, HEAD_DIM, HEAD_DIM), BF16)] * 2,
        compiler_params=_params(("parallel",)),
    )(lg_c, lg_a, a[0], b[0], c[0], e[0])
    return gc[:, 0, 0], ga[:, 0, 0]


def _ret_gate_bwd(dmixed, first_col, out, proj, gate_col, norm_w, n_heads):
    s = out.shape[0]
    tr = _row_block(s, 8 * HEAD_DIM)

    def body(dm_ref, o_ref, g_ref, w_ref, do_ref, dg_ref, dw_ref):
        dm = dm_ref[...].astype(F32)
        ov = o_ref[...].astype(F32)
        g = g_ref[...].astype(F32)
        w = w_ref[...]
        r = lax.rsqrt(jnp.mean(ov * ov, axis=-1, keepdims=True) + EPS)
        ohat = ov * r
        sg = _sigmoid(g)
        silu = g * sg
        dg_ref[...] = (dm * ohat * w * sg * (1.0 + g * (1.0 - sg))).astype(BF16)
        dohat = dm * w * silu
        do_ref[...] = (r * (dohat - ohat * jnp.mean(dohat * ohat, axis=-1, keepdims=True))).astype(BF16)

        @pl.when(pl.program_id(1) == 0)
        def _():
            dw_ref[...] = jnp.zeros_like(dw_ref)

        dw_ref[...] += jnp.sum(dm * ohat * silu, axis=0, keepdims=True)

    def blk(first):
        return pl.BlockSpec((tr, HEAD_DIM), lambda h, i: (i, first + h))

    vec = pl.BlockSpec((1, HEAD_DIM), lambda h, i: (0, h))
    o_shape = jax.ShapeDtypeStruct((s, n_heads * HEAD_DIM), BF16)
    return pl.pallas_call(
        body, name="ret_gate_bwd", grid=(n_heads, s // tr),
        in_specs=[blk(first_col), blk(0), blk(gate_col), vec],
        out_specs=[blk(0), blk(0), vec],
        out_shape=[o_shape, o_shape, jax.ShapeDtypeStruct((1, n_heads * HEAD_DIM), F32)],
        compiler_params=_params(("parallel", "arbitrary")),
    )(dmixed, out, proj, norm_w)


def _step(x, target, norm_mix_w, ret_decay_fwd, ret_decay_bwd, ret_norm_w, norm_ffn_w, norm_final_w, own,
          w_in_started, queued, shard_ids, pos):
    d = x.shape[1]
    nh = d // (2 * HEAD_DIM)
    scale = HEAD_DIM ** -0.5
    slopes = jnp.exp2(-8.0 * jnp.arange(1, nh + 1, dtype=F32) / nh)
    lg_f = -jnp.exp(ret_decay_fwd)
    lg_b = -jnp.exp(ret_decay_bwd)
    q_r, k_r, v_r, g_r = 3 * nh, 4 * nh, 5 * nh, 6 * nh
    ax = BIG_AXIS

    def gather(names, arrays, stage, part=None, peers=(0, 1, 2)):
        return _gather_job(arrays, [ax[k] for k in names], stage, part, peers)

    def add_halves(k, g, received):
        return _add_halves(g, received, ax[k], pos, name="grad_add_halves_" + k)

    def sum_parts(k, g, received, parts):
        return _sum_chip_parts(g, received, parts, ax[k], pos, name="grad_sum_parts_" + k)

    sems, w_in, token = w_in_started
    n1 = _rmsnorm_fwd(x, norm_mix_w, name="norm_mix_fwd", after=token)
    proj = _in_proj_part(n1, w_in, None, shard_ids, 0, out_cols=w_in.shape[1])
    for peer in range(3):
        behind = [proj] + ([queued["w_down"]["token"]] if peer == 0 else [])
        w_in = _split_gather_wait(sems, w_in, ax["w_in"], peer, behind)
        (w_in,) = _run_jobs([gather(["w_in"], [w_in], "d2d", peers=(peer,))], name="all_gather_w_in_sibling_%d" % peer)
        proj = _in_proj_part(n1, w_in, proj, shard_ids, 1 + peer, out_cols=w_in.shape[1])
    (w_gate,) = _split_wait(queued["w_gate"], [proj], name="all_gather_w_gate_wait")
    qkv_classes = _to_classes(proj[:, :3 * nh * HEAD_DIM])
    (ret, ret_mixed), [[w_gate]] = _retention(
        (proj, q_r), (proj, k_r), (proj, v_r), lg_f, lg_b, strict_c=False, strict_a=True, scale=scale, n_heads=nh,
        name="retention_fwd_first", gate=(proj, g_r), norm_w=ret_norm_w, heads=(0, nh // 2),
        jobs=[gather(["w_gate"], [w_gate], "d2d")])
    (far_out, far_lse), _ = _attention_far_fwd(qkv_classes, slopes, nh, after=[ret_mixed])
    ret, ret_mixed = _retention(
        (proj, q_r), (proj, k_r), (proj, v_r), lg_f, lg_b, strict_c=False, strict_a=True, scale=scale, n_heads=nh,
        name="retention_fwd_second", gate=(proj, g_r), norm_w=ret_norm_w, heads=(nh // 2, nh - nh // 2),
        so_far=[ret, ret_mixed], after=[far_out])
    (w_out,) = _split_wait(queued["w_out"], [ret_mixed], name="all_gather_w_out_wait")
    (attn, lse), [[w_out]] = _attention_fwd(
        proj, slopes, _from_classes(far_out), _from_classes(far_lse), nh, after=[ret_mixed],
        jobs=[gather(["w_out"], [w_out], "d2d")])
    mixed = jnp.concatenate([attn, ret_mixed], axis=1)
    (w_up,) = _split_wait(queued["w_up"], [mixed], name="all_gather_w_up_wait")
    h1, [[w_up]] = _matmul(mixed, w_out, name="out_proj", residual=x, jobs=[gather(["w_up"], [w_up], "d2d")])
    n2 = _rmsnorm_fwd(h1, norm_ffn_w, name="norm_ffn_fwd")
    (w_down,) = _split_wait(queued["w_down"], [n2], name="all_gather_w_down_wait")
    (gate, up, act), [[w_down]] = _swiglu_fwd(n2, w_gate, w_up, jobs=[gather(["w_down"], [w_down], "d2d")])
    h2 = _matmul(act, w_down, name="down_proj", residual=h1, tk=2816)
    dh2, dh2_b, d_norm_final, loss = _loss_head(h2, norm_final_w, target)

    dgate, dup = _swiglu_bwd_act(dh2_b, w_down, gate, up)
    g_down = _weight_grad(act, dh2_b, name="grad_w_down")
    g_gate, [[r_down]] = _weight_grad(n2, dgate, name="grad_w_gate", jobs=[_exchange_job([g_down], [ax["w_down"]])])
    def send(k, sums):
        return _split_start(_send_sums_job([sums], [ax[k]]), name="grad_send_%s_start" % k)

    def sent(k, started, behind):
        return _split_wait(started, behind, name="grad_send_%s_wait" % k)[1]

    s_down = add_halves("w_down", g_down, r_down)
    sending_down = send("w_down", s_down)
    g_up, [[r_gate]] = _weight_grad(n2, dup, name="grad_w_up", jobs=[_exchange_job([g_gate], [ax["w_gate"]])],
                                    after=[sending_down["token"]])
    s_gate = add_halves("w_gate", g_gate, r_gate)
    sending_gate = send("w_gate", s_gate)
    dn2, [[r_up]] = _swiglu_bwd_in(dgate, dup, w_gate, w_up, jobs=[_exchange_job([g_up], [ax["w_up"]])],
                                   after=[sending_gate["token"]])
    h_down = sum_parts("w_down", g_down, r_down, sent("w_down", sending_down, [dn2]))
    s_up = add_halves("w_up", g_up, r_up)
    sending_up = send("w_up", s_up)
    dh1, dh1_b, d_norm_ffn = _rmsnorm_bwd(dn2, h1, norm_ffn_w, dh2, name="norm_ffn_bwd", after=[sending_up["token"]])
    h_gate = sum_parts("w_gate", g_gate, r_gate, sent("w_gate", sending_gate, [dh1_b]))

    dmixed, [[gr_down]] = _matmul(dh1_b, w_out, name="out_proj_bwd", tb=True, out_dtype=BF16,
                                  jobs=[_join_job([h_down], [ax["w_down"]])])
    far_in = [_to_classes(t) for t in (attn, dmixed[:, :nh * HEAD_DIM], lse)]
    g_out = _weight_grad(mixed, dh1_b, name="grad_w_out")
    d_ret, dg_r, d_ret_norm = _ret_gate_bwd(dmixed, nh, ret, proj, g_r, ret_norm_w, nh)
    far_grads = _attention_far_bwd(qkv_classes, slopes, *far_in, nh)
    far_grads = [_from_classes(t) for t in far_grads]
    dq_r, [[gr_gate]] = _retention(
        (d_ret, 0), (proj, v_r), (proj, k_r), lg_f, lg_b, strict_c=False, strict_a=True, scale=scale, n_heads=nh,
        name="retention_dq", jobs=[_join_job([h_gate], [ax["w_gate"]])])
    (dq_a, dk_a, dv_a), [[r_out]] = _attention_bwd(
        proj, slopes, attn, lse, dmixed, far_grads, nh, jobs=[_exchange_job([g_out], [ax["w_out"]])])
    s_out = add_halves("w_out", g_out, r_out)
    sending_out = send("w_out", s_out)
    h_up = sum_parts("w_up", g_up, r_up, sent("w_up", sending_up, [dq_a]))
    dv_r, [[gr_up]] = _retention(
        (proj, k_r), (proj, q_r), (d_ret, 0), lg_b, lg_f, strict_c=True, strict_a=False, scale=scale, n_heads=nh,
        name="retention_dv", jobs=[_join_job([h_up], [ax["w_up"]])], after=[sending_out["token"]])
    h_out = sum_parts("w_out", g_out, r_out, sent("w_out", sending_out, [dv_r]))
    dk_r, [[gr_out]] = _retention(
        (proj, v_r), (d_ret, 0), (proj, q_r), lg_b, lg_f, strict_c=True, strict_a=False, scale=scale, n_heads=nh,
        name="retention_dk", jobs=[_join_job([h_out], [ax["w_out"]])])
    dlg_f, dlg_b = _retention_decay_grads((proj, q_r), (proj, k_r), (proj, v_r), (d_ret, 0), lg_f, lg_b,
                                          scale=scale, n_heads=nh)
    dproj = [dq_a, dk_a, dv_a, dq_r, dk_r, dv_r, dg_r]
    g_in = _weight_grad_pieces(n1, dproj, name="grad_w_in")
    exchange = _split_start(_exchange_job([g_in], [ax["w_in"]]), name="grad_exchange_w_in_start")
    dn1 = _matmul_pieces_nt(dproj, w_in, name="in_proj_bwd", after=[exchange["token"]])
    g_in, r_in = _split_wait(exchange, [dn1], name="grad_exchange_w_in_wait")
    s_in = add_halves("w_in", g_in, r_in)
    sending = _split_start(_send_sums_job([s_in], [ax["w_in"]]), name="grad_send_w_in_start")
    dx, _, d_norm_mix = _rmsnorm_bwd(dn1, x, norm_mix_w, dh1, name="norm_mix_bwd", after=[sending["token"]])

    small = dict(loss=loss[0, 0], norm_mix_w=d_norm_mix, ret_decay_fwd=dlg_f * lg_f, ret_decay_bwd=dlg_b * lg_b,
                 ret_norm_w=d_ret_norm, norm_ffn_w=d_norm_ffn, norm_final_w=d_norm_final)
    return (dx, dict(w_out=gr_out, w_gate=gr_gate, w_up=gr_up, w_down=gr_down), small,
            dict(sending=sending, grad=g_in, received=r_in))


def _mesh_position():
    x, y, c = lax.axis_index("x"), lax.axis_index("y"), lax.axis_index("c")
    chips = [(1 - x, y), (x, 1 - y), (1 - x, 1 - y)]
    return x, y, c, chips


def _span(span):
    if span is None:
        return slice(None)
    start, size, step = span
    return pl.ds(start if isinstance(start, int) else pl.multiple_of(start, step), size)


def _part_rows(part, rows):
    first, count, of = part
    return first * (rows // of), count * (rows // of), rows // of


def _region(ref, axis, shard, half, shard_size, half_size, part=None, total_rows=None):
    along = None if shard is None else (shard * shard_size, shard_size, shard_size)
    other = None if half is None else (half * half_size, half_size, half_size)
    rows, cols = (other, along) if axis == 1 else (along, other)
    if part is not None:
        start, size, _ = rows if rows is not None else (0, total_rows, None)
        offset, size, step = _part_rows(part, size)
        rows = (start + offset, size, step)
    return ref.at[_span(rows), _span(cols)]


def _fuse(first, second):
    assert not (first.ins or first.outs or second.ins or second.outs)
    assert len(first.ios) == len(second.ios) and all(a is b for a, b in zip(first.ios, second.ios))
    cut = len(first.sems)

    def start(refs, sems):
        first.start(refs, sems[:cut])
        second.start(refs, sems[cut:])

    def finish(refs, sems):
        first.finish(refs, sems[:cut])
        second.finish(refs, sems[cut:])

    return _Job(ios=first.ios, sems=first.sems + second.sems, start=start, finish=finish)


def _gather_job(full, axes, stage, part=None, peers=(0, 1, 2)):
    n = len(full)

    def copies(refs, sems):
        send_sem, recv_sem = sems
        x, y, c, chips = _mesh_position()
        me = 2 * x + y

        def copy(w, k, shard, half, target):
            rows_cols = full[w].shape
            place = _region(refs[w], axes[w], shard, half, rows_cols[axes[w]] // N_CHIPS, rows_cols[1 - axes[w]] // 2,
                            part)
            return pltpu.make_async_remote_copy(
                src_ref=place, dst_ref=place, send_sem=send_sem.at[w, k], recv_sem=recv_sem.at[w, k],
                device_id=target, device_id_type=MESH)

        def sent(w, k):
            if stage == "ici":
                return copy(w, k, me, c, (chips[k][0], chips[k][1], c))
            return copy(w, k, 2 * chips[k][0] + chips[k][1], c, (x, y, 1 - c))

        def landed(w, k):
            return copy(w, k, 2 * chips[k][0] + chips[k][1], c if stage == "ici" else 1 - c, (x, y, 1 - c))

        return sent, landed

    def start(refs, sems):
        sent, _ = copies(refs, sems)
        for w in range(n):
            for k in peers:
                sent(w, k).start()

    def finish(refs, sems):
        sent, landed = copies(refs, sems)
        for w in range(n):
            for k in peers:
                landed(w, k).wait_recv()
                sent(w, k).wait_send()

    return _Job(ios=full, sems=[pltpu.SemaphoreType.DMA((n, 3))] * 2, start=start, finish=finish)


def _exchange_job(grads, axes):
    n = len(grads)

    def half_shape(w):
        return tuple(d // 2 if a != axes[w] else d for a, d in enumerate(grads[w].shape))

    def copy(refs, sems, w):
        x, y, c, _ = _mesh_position()
        return pltpu.make_async_remote_copy(
            src_ref=_region(refs[w], axes[w], None, 1 - c, 0, half_shape(w)[1 - axes[w]]), dst_ref=refs[n + w],
            send_sem=sems[0].at[w], recv_sem=sems[1].at[w], device_id=(x, y, 1 - c), device_id_type=MESH)

    def start(refs, sems):
        for w in range(n):
            copy(refs, sems, w).start()

    def finish(refs, sems):
        for w in range(n):
            copy(refs, sems, w).wait()

    return _Job(ins=grads, outs=[jax.ShapeDtypeStruct(half_shape(w), F32) for w in range(n)],
                sems=[pltpu.SemaphoreType.DMA((n,))] * 2, start=start, finish=finish)


def _half_block_spec(axis, block, half_blocks, use_half):
    if axis == 1:
        if use_half:
            return pl.BlockSpec(block, lambda i, pos: (pos[0] * half_blocks + i, 0))
        return pl.BlockSpec(block, lambda i, pos: (i, 0))
    if use_half:
        return pl.BlockSpec(block, lambda i, pos: (i, pos[0]))
    return pl.BlockSpec(block, lambda i, pos: (i, 0))


def _add_halves(grad, received, axis, pos, *, name):
    rows, cols = received.shape
    tr = _row_block(rows, cols)
    nb = rows // tr

    def body(pos_ref, g_ref, r_ref, o_ref):
        o_ref[...] = (g_ref[...] + r_ref[...]).astype(BF16)

    blk = (tr, cols)
    return pl.pallas_call(
        body, name=name, out_shape=jax.ShapeDtypeStruct((rows, cols), BF16),
        grid_spec=pltpu.PrefetchScalarGridSpec(
            num_scalar_prefetch=1, grid=(nb,),
            in_specs=[_half_block_spec(axis, blk, nb, True), _half_block_spec(axis, blk, nb, False)],
            out_specs=_half_block_spec(axis, blk, nb, False)),
        compiler_params=_params(("parallel",)),
    )(pos, grad, received)


def _send_sums_job(sums, axes, part=None, landing=None):
    n = len(sums)

    def part_shape(w):
        return tuple(d // N_CHIPS if a == axes[w] else d for a, d in enumerate(sums[w].shape))

    def copy(refs, sems, w, k):
        x, y, c, chips = _mesh_position()
        shard = 2 * chips[k][0] + chips[k][1]
        rows = part_shape(w)[0]
        dst = refs[n + w].at[k]
        if part is not None:
            offset, size, _ = _part_rows(part, rows)
            dst = refs[n + w].at[k, pl.ds(offset, size), :]
        return pltpu.make_async_remote_copy(
            src_ref=_region(refs[w], axes[w], shard, None, part_shape(w)[axes[w]], 0, part, rows), dst_ref=dst,
            send_sem=sems[0].at[w, k], recv_sem=sems[1].at[w, k],
            device_id=(chips[k][0], chips[k][1], c), device_id_type=MESH)

    def start(refs, sems):
        for w in range(n):
            for k in range(3):
                copy(refs, sems, w, k).start()

    def finish(refs, sems):
        for w in range(n):
            for k in range(3):
                copy(refs, sems, w, k).wait()

    sems = [pltpu.SemaphoreType.DMA((n, 3))] * 2
    if landing is not None:
        return _Job(ins=sums, ios=landing, sems=sems, start=start, finish=finish)
    return _Job(ins=sums, outs=[jax.ShapeDtypeStruct((3,) + part_shape(w), BF16) for w in range(n)],
                sems=sems, start=start, finish=finish)


def _sum_chip_parts(grad, received, parts, axis, pos, *, name):
    _, rows, cols = parts.shape
    tr = _row_block(rows, cols)
    nb = rows // tr
    blk = (tr, cols)

    def body(pos_ref, g_ref, r_ref, p_ref, o_ref):
        total = g_ref[...] + r_ref[...]
        for k in range(3):
            total = total + p_ref[k].astype(F32)
        o_ref[...] = total

    if axis == 1:
        g_spec = pl.BlockSpec(blk, lambda i, pos: (pos[0] * nb + i, pos[1]))
        r_spec = pl.BlockSpec(blk, lambda i, pos: (i, pos[1]))
        o_spec = pl.BlockSpec(blk, lambda i, pos: (pos[0] * nb + i, 0))
        shard_shape = (2 * rows, cols)
    else:
        g_spec = pl.BlockSpec(blk, lambda i, pos: (pos[1] * nb + i, pos[0]))
        r_spec = pl.BlockSpec(blk, lambda i, pos: (pos[1] * nb + i, 0))
        o_spec = pl.BlockSpec(blk, lambda i, pos: (i, pos[0]))
        shard_shape = (rows, 2 * cols)
    return pl.pallas_call(
        body, name=name, out_shape=jax.ShapeDtypeStruct(shard_shape, F32),
        grid_spec=pltpu.PrefetchScalarGridSpec(
            num_scalar_prefetch=1, grid=(nb,),
            in_specs=[g_spec, r_spec, pl.BlockSpec((3,) + blk, lambda i, pos: (0, i, 0))],
            out_specs=o_spec),
        compiler_params=_params(("parallel",)),
    )(pos, grad, received, parts)


def _join_job(shards, axes):
    n = len(shards)

    def copy(refs, sems, w, other):
        x, y, c, _ = _mesh_position()
        place = _region(refs[w], axes[w], None, 1 - c if other else c, 0, shards[w].shape[1 - axes[w]] // 2)
        return pltpu.make_async_remote_copy(
            src_ref=place, dst_ref=place, send_sem=sems[0].at[w], recv_sem=sems[1].at[w],
            device_id=(x, y, 1 - c), device_id_type=MESH)

    def start(refs, sems):
        for w in range(n):
            copy(refs, sems, w, False).start()

    def finish(refs, sems):
        for w in range(n):
            copy(refs, sems, w, True).wait_recv()
            copy(refs, sems, w, False).wait_send()

    return _Job(ios=shards, sems=[pltpu.SemaphoreType.DMA((n,))] * 2, start=start, finish=finish)


def _all_reduce_small(vec, after=()):
    rows, cols = vec.shape

    def body(v_ref, *rest):
        o_ref, land_ref, send_sem, recv_sem = rest[len(after):]
        x, y, c, _ = _mesh_position()
        me = 4 * x + 2 * y + c
        land_ref[me] = v_ref[...]
        copies = []
        for k in range(1, 8):
            px, py, pc = x ^ (k >> 2), y ^ ((k >> 1) & 1), c ^ (k & 1)
            copies.append(pltpu.make_async_remote_copy(
                src_ref=v_ref, dst_ref=land_ref.at[me], send_sem=send_sem.at[k], recv_sem=recv_sem.at[k],
                device_id=(px, py, pc), device_id_type=MESH))
        for cp in copies:
            cp.start()
        for k in range(1, 8):
            peer = me ^ k
            pltpu.make_async_remote_copy(
                src_ref=v_ref, dst_ref=land_ref.at[peer], send_sem=send_sem.at[k], recv_sem=recv_sem.at[k],
                device_id=(x, y, c), device_id_type=MESH).wait_recv()
        for cp in copies:
            cp.wait_send()
        total = land_ref[0]
        for k in range(1, 8):
            total = total + land_ref[k]
        o_ref[...] = total

    vmem = pl.BlockSpec(memory_space=pltpu.VMEM)
    return pl.pallas_call(
        body, name="all_reduce_small", in_specs=[vmem] + [pl.BlockSpec(memory_space=pl.ANY)] * len(after),
        out_specs=vmem, out_shape=jax.ShapeDtypeStruct((rows, cols), F32),
        scratch_shapes=[pltpu.VMEM((8, rows, cols), F32), pltpu.SemaphoreType.DMA((8,)), pltpu.SemaphoreType.DMA((8,))],
    )(vec, *after)


def _adamw(w, g, m, v, *, name, after=()):
    rows, cols = w.shape
    tr = _row_block(rows, cols) if rows % 8 == 0 else rows
    bc1 = 1.0 - ADAM_B1 ** ADAM_STEP
    bc2 = 1.0 - ADAM_B2 ** ADAM_STEP

    def body(w_ref, g_ref, m_ref, v_ref, *rest):
        go_ref, d_ref, mo_ref, vo_ref = rest[len(after):]
        gv = g_ref[...]
        go_ref[...] = gv
        mn = ADAM_B1 * m_ref[...] + (1.0 - ADAM_B1) * gv
        vn = ADAM_B2 * v_ref[...] + (1.0 - ADAM_B2) * (gv * gv)
        mo_ref[...] = mn
        vo_ref[...] = vn
        d_ref[...] = -ADAM_LR * ((mn / bc1) / (jnp.sqrt(vn / bc2) + ADAM_EPS) + ADAM_WD * w_ref[...])

    blk = pl.BlockSpec((tr, cols), lambda i: (i, 0))
    shape = jax.ShapeDtypeStruct((rows, cols), F32)
    return pl.pallas_call(
        body, name=name, grid=(rows // tr,), in_specs=[blk] * 4 + [pl.BlockSpec(memory_space=pl.ANY)] * len(after),
        out_specs=[blk] * 4, out_shape=[shape] * 4, compiler_params=_params(("parallel",)),
    )(w, g, m, v, *after)


def _to_bf16_in_place(w, axis, pos, *, name, after=None):
    rows, cols = w.shape
    tr = _row_block(rows, cols)
    nb = rows // tr

    def body(pos_ref, w_ref, *rest):
        rest[-1][...] = w_ref[...].astype(BF16)

    if axis == 1:
        o_spec = pl.BlockSpec((tr, cols), lambda i, pos: (i, pos[1]))
        full_shape = (rows, N_CHIPS * cols)
    else:
        o_spec = pl.BlockSpec((tr, cols), lambda i, pos: (pos[1] * nb + i, 0))
        full_shape = (N_CHIPS * rows, cols)
    in_specs = [pl.BlockSpec((tr, cols), lambda i, pos: (i, 0))]
    operands = [pos, w]
    if after is not None:
        in_specs.append(pl.BlockSpec(after.shape, lambda i, pos: (0, 0)))
        operands.append(after)
    return pl.pallas_call(
        body, name=name, out_shape=jax.ShapeDtypeStruct(full_shape, BF16),
        grid_spec=pltpu.PrefetchScalarGridSpec(num_scalar_prefetch=1, grid=(nb,), in_specs=in_specs, out_specs=o_spec),
        compiler_params=_params(("parallel",)),
    )(*operands)


def _split_gather_start(full, axis):
    rows_cols = full.shape

    def body(buf_ref, *rest):
        sems = rest[:6]
        token_ref = rest[7]
        x, y, c, chips = _mesh_position()
        place = _region(buf_ref, axis, 2 * x + y, c, rows_cols[axis] // N_CHIPS, rows_cols[1 - axis] // 2)
        for k in range(3):
            pltpu.make_async_remote_copy(
                src_ref=place, dst_ref=place, send_sem=sems[k], recv_sem=sems[3 + k],
                device_id=(chips[k][0], chips[k][1], c), device_id_type=MESH).start()
        token_ref[...] = jnp.zeros_like(token_ref)

    hbm = pl.BlockSpec(memory_space=pltpu.HBM)
    sem = pl.BlockSpec(memory_space=pltpu.SEMAPHORE)
    res = pl.pallas_call(
        body, name="all_gather_w_in_start",
        out_shape=(*[pltpu.SemaphoreType.DMA(())] * 6, pltpu.HBM(full.shape, full.dtype),
                   jax.ShapeDtypeStruct((8, HEAD_DIM), F32)),
        in_specs=(hbm,), out_specs=(*[sem] * 6, hbm, pl.BlockSpec(memory_space=pltpu.VMEM)),
        input_output_aliases={0: 6},
        compiler_params=pltpu.CompilerParams(has_side_effects=pltpu.SideEffectType.DATAFLOW_SIDE_EFFECTING),
    )(pltpu.with_memory_space_constraint(full, pltpu.HBM))
    return list(res[:6]), res[6], res[7]


def _split_gather_wait(sems, full, axis, peer, after):
    rows_cols = full.shape

    def body(buf_ref, send_sem, recv_sem, *rest):
        x, y, c, chips = _mesh_position()

        def copy(shard):
            place = _region(buf_ref, axis, shard, c, rows_cols[axis] // N_CHIPS, rows_cols[1 - axis] // 2)
            return pltpu.make_async_remote_copy(
                src_ref=place, dst_ref=place, send_sem=send_sem, recv_sem=recv_sem,
                device_id=(chips[peer][0], chips[peer][1], c), device_id_type=MESH)

        copy(2 * x + y).wait_send()
        copy(2 * chips[peer][0] + chips[peer][1]).wait_recv()

    hbm = pl.BlockSpec(memory_space=pltpu.HBM)
    sem = pl.BlockSpec(memory_space=pltpu.SEMAPHORE)
    return pl.pallas_call(
        body, name="all_gather_w_in_wait_%d" % peer, out_shape=pltpu.HBM(full.shape, full.dtype),
        in_specs=(hbm, sem, sem, *[pl.BlockSpec(memory_space=pl.ANY)] * len(after)), out_specs=hbm,
        input_output_aliases={0: 0},
        compiler_params=pltpu.CompilerParams(has_side_effects=pltpu.SideEffectType.DATAFLOW_SIDE_EFFECTING),
    )(full, sems[peer], sems[3 + peer], *after)


def _in_proj_part(n1, w_in, proj, shard_ids, which, *, out_cols):
    m, kdim = n1.shape
    cols = out_cols // N_CHIPS
    tm = _tile(m, 1024)

    def body(ids_ref, a_ref, b_ref, *rest):
        rest[-1][...] = _dot(a_ref[...], b_ref[...]).astype(BF16)

    in_specs = [pl.BlockSpec((tm, kdim), lambda i, ids: (i, 0)),
                pl.BlockSpec((kdim, cols), lambda i, ids: (0, ids[which]))]
    operands = [shard_ids, n1, w_in]
    if proj is not None:
        in_specs.append(pl.BlockSpec(memory_space=pl.ANY))
        operands.append(proj)
    return pl.pallas_call(
        body, name="in_proj_%d" % which, out_shape=jax.ShapeDtypeStruct((m, out_cols), BF16),
        grid_spec=pltpu.PrefetchScalarGridSpec(
            num_scalar_prefetch=1, grid=(m // tm,), in_specs=in_specs,
            out_specs=pl.BlockSpec((tm, cols), lambda i, ids: (i, ids[which]))),
        input_output_aliases={3: 0} if proj is not None else {},
        compiler_params=_params(("parallel",)),
    )(*operands)


BIG = ("w_in", "w_out", "w_gate", "w_up", "w_down")
BIG_AXIS = dict(w_in=1, w_out=0, w_gate=1, w_up=1, w_down=0)
SMALL = ("norm_mix_w", "ret_decay_fwd", "ret_decay_bwd", "ret_norm_w", "norm_ffn_w", "norm_final_w")
ALL_WEIGHTS = ("norm_mix_w", "w_in", "ret_decay_fwd", "ret_decay_bwd", "ret_norm_w", "w_out", "norm_ffn_w",
               "w_gate", "w_up", "w_down", "norm_final_w")
SMALL_ROW = 128 * 8


def _pack_small(small):
    pieces = [jnp.reshape(small["loss"], (1,))] + [jnp.reshape(small[k], (-1,)) for k in SMALL]
    rows = []
    for p in pieces:
        pad = -p.shape[0] % (8 * SMALL_ROW)
        rows.append(jnp.reshape(jnp.pad(p, (0, pad)), (-1, SMALL_ROW)))
    return jnp.concatenate(rows, axis=0)


def _unpack_small(block, like):
    out, row = {}, 0
    for k in ("loss",) + SMALL:
        size = 1 if k == "loss" else like[k].size
        nrows = -(-size // (8 * SMALL_ROW)) * 8
        out[k] = jnp.reshape(block[row:row + nrows], (-1,))[:size]
        row += nrows
    return out


def kernel(x, norm_mix_w, w_in, ret_decay_fwd, ret_decay_bwd, ret_norm_w, w_out, norm_ffn_w, w_gate, w_up, w_down, norm_final_w, loss_target, m_norm_mix_w, m_w_in, m_ret_decay_fwd, m_ret_decay_bwd, m_ret_norm_w, m_w_out, m_norm_ffn_w, m_w_gate, m_w_up, m_w_down, m_norm_final_w, v_norm_mix_w, v_w_in, v_ret_decay_fwd, v_ret_decay_bwd, v_ret_norm_w, v_w_out, v_norm_ffn_w, v_w_gate, v_w_up, v_w_down, v_norm_final_w):
    weights = dict(norm_mix_w=norm_mix_w, w_in=w_in, ret_decay_fwd=ret_decay_fwd, ret_decay_bwd=ret_decay_bwd,
                   ret_norm_w=ret_norm_w, w_out=w_out, norm_ffn_w=norm_ffn_w, w_gate=w_gate, w_up=w_up,
                   w_down=w_down, norm_final_w=norm_final_w)
    m_in = dict(norm_mix_w=m_norm_mix_w, w_in=m_w_in, ret_decay_fwd=m_ret_decay_fwd, ret_decay_bwd=m_ret_decay_bwd,
                ret_norm_w=m_ret_norm_w, w_out=m_w_out, norm_ffn_w=m_norm_ffn_w, w_gate=m_w_gate, w_up=m_w_up,
                w_down=m_w_down, norm_final_w=m_norm_final_w)
    v_in = dict(norm_mix_w=v_norm_mix_w, w_in=v_w_in, ret_decay_fwd=v_ret_decay_fwd, ret_decay_bwd=v_ret_decay_bwd,
                ret_norm_w=v_ret_norm_w, w_out=v_w_out, norm_ffn_w=v_norm_ffn_w, w_gate=v_w_gate, w_up=v_w_up,
                w_down=v_w_down, norm_final_w=v_norm_final_w)
    pos = jnp.stack([lax.axis_index("c"), 2 * lax.axis_index("x") + lax.axis_index("y")]).astype(jnp.int32)

    own = {"w_in": _to_bf16_in_place(weights["w_in"][0], BIG_AXIS["w_in"], pos, name="cast_w_in")}
    w_in_started = _split_gather_start(own["w_in"], BIG_AXIS["w_in"])
    queued, token = {}, w_in_started[2]
    for k in ("w_gate", "w_out", "w_up", "w_down"):
        own[k] = _to_bf16_in_place(weights[k][0], BIG_AXIS[k], pos, name="cast_" + k, after=token)
        queued[k] = _split_start(_gather_job([own[k]], [BIG_AXIS[k]], "ici"), name="all_gather_%s_start" % k)
        token = queued[k]["token"]
    cx, cy = lax.axis_index("x"), lax.axis_index("y")
    shard_ids = jnp.stack([2 * cx + cy, 2 * (1 - cx) + cy, 2 * cx + 1 - cy, 2 * (1 - cx) + 1 - cy]).astype(jnp.int32)

    dx, grad_w, small, w_in_pending = _step(
        x[0], loss_target[0], norm_mix_w, ret_decay_fwd[0], ret_decay_bwd[0], ret_norm_w, norm_ffn_w,
        norm_final_w[None, :], own, w_in_started, queued, shard_ids, pos)

    delta, new_m, new_v = {}, {}, {}

    def update(k, after):
        shape = weights[k].shape
        as2d = (lambda t: jnp.reshape(t, (-1, shape[-1])))
        grad_w[k], delta[k], new_m[k], new_v[k] = (jnp.reshape(t, shape) for t in _adamw(
            as2d(weights[k]), as2d(grad_w[k]), as2d(m_in[k]), as2d(v_in[k]), name="adamw_" + k, after=after))

    others = [k for k in BIG if k != "w_in"]
    for k in others:
        update(k, [w_in_pending["sending"]["token"]])
    _, parts = _split_wait(w_in_pending["sending"], [dx] + [delta[k] for k in others], name="grad_send_w_in_wait")

    half = _sum_chip_parts(w_in_pending["grad"], w_in_pending["received"], parts, BIG_AXIS["w_in"], pos,
                           name="grad_sum_parts_w_in")
    joining = _split_start(_join_job([half], [BIG_AXIS["w_in"]]), name="grad_join_w_in_start")

    like = {k: weights[k] for k in SMALL}
    reduced = _unpack_small(_all_reduce_small(_pack_small(small), after=[joining["token"]]), like)
    loss = reduced["loss"][0]
    for k in SMALL:
        grad_w[k] = jnp.reshape(reduced[k], (1, -1))
        update(k, [])
    (grad_w["w_in"],) = _split_wait(joining, [delta[k] for k in SMALL], name="grad_join_w_in_wait")
    update("w_in", [])

    return (loss, dx[None], *[grad_w[k] for k in ALL_WEIGHTS], *[delta[k] for k in ALL_WEIGHTS],
            *[new_m[k] for k in ALL_WEIGHTS], *[new_v[k] for k in ALL_WEIGHTS])
```

```python
import functools
import math

import numpy as np
import jax
import jax.numpy as jnp
from jax import lax
from jax.experimental import pallas as pl
from jax.experimental.pallas import tpu as pltpu

F32 = jnp.float32
BF16 = jnp.bfloat16
MESH = pl.DeviceIdType.MESH

HEAD_DIM = 128
RET_CHUNK = 128
RET_UNROLL = 8
EPS = 1e-6
DILATED_PATTERNS = ((128, 1), (512, 4), (2048, 16))
ATT_BLOCK = 256
ATT_REACH = max(w // 2 for w, _ in DILATED_PATTERNS)
ATT_NEAR = ATT_BLOCK
ATT_CLASSES = DILATED_PATTERNS[-1][1]
assert all(w // 2 <= ATT_NEAR for w, _ in DILATED_PATTERNS[:-1])
ATT_KB = -(-ATT_NEAR // ATT_BLOCK)
ATT_WINDOW = 2 * ATT_KB + 1
ATT_FAR_GROUP = 8
ATT_NEAR_GROUP = 8
MASKED = -1e30
ROW_MAX_INIT = -1e29
N_CHIPS = 4
VMEM_LIMIT_BYTES = 56 * 1024 * 1024
ELEM_BLOCK_BYTES = 2 * 1024 * 1024
WEIGHT_GRAD_GROUP = 4

ADAM_LR = 0.001
ADAM_B1 = 0.9
ADAM_B2 = 0.999
ADAM_EPS = 1e-08
ADAM_WD = 0.01
ADAM_STEP = 10


def _params(sem=None):
    return pltpu.CompilerParams(dimension_semantics=sem, vmem_limit_bytes=VMEM_LIMIT_BYTES)


def _sigmoid(x):
    return 0.5 * jnp.tanh(0.5 * x) + 0.5


class _Job:
    def __init__(self, *, ins=(), ios=(), outs=(), sems=(), start, finish):
        self.ins, self.ios, self.outs, self.sems = list(ins), list(ios), list(outs), list(sems)
        self.start, self.finish = start, finish

    def results(self):
        return [jax.ShapeDtypeStruct(a.shape, a.dtype) for a in self.ios] + self.outs


def _call(body, *, name, grid, in_specs, out_specs, out_shape, operands, scratch_shapes=(), semantics=None, jobs=(),
          after=(), updates=None):
    in_specs, out_specs, out_shape = list(in_specs), list(out_specs), list(out_shape)
    scratch_shapes = list(scratch_shapes)
    if not jobs:
        n_real = len(in_specs)

        def ordered(*refs):
            body(*refs[:n_real], *refs[n_real + len(after):])

        outs = pl.pallas_call(
            ordered if after else body, name=name, grid=grid,
            in_specs=in_specs + [pl.BlockSpec(memory_space=pl.ANY)] * len(after), out_specs=out_specs,
            out_shape=out_shape, scratch_shapes=scratch_shapes, input_output_aliases=dict(updates or {}),
            compiler_params=_params(semantics))(*operands, *after)
        return outs, []
    n_in, n_out, n_scratch = len(in_specs), len(out_specs), len(scratch_shapes)
    extra_in, extra_out, sems, aliases = [], [], [], dict(updates or {})
    for job in jobs:
        extra_in += job.ins
        for t in range(len(job.ios)):
            aliases[n_in + len(extra_in) + t] = n_out + len(extra_out) + t
        extra_in += job.ios
        extra_out += job.results()
        sems += job.sems

    def carried(*refs):
        x_in = refs[n_in:n_in + len(extra_in)]
        first_out = n_in + len(extra_in) + len(after)
        x_out = refs[first_out + n_out:first_out + n_out + len(extra_out)]
        x_sem = refs[len(refs) - len(sems):]
        views, i_in, i_out, i_sem = [], 0, 0, 0
        for job in jobs:
            data = list(x_in[i_in:i_in + len(job.ins)]) + list(x_out[i_out:i_out + len(job.results())])
            views.append((data, x_sem[i_sem:i_sem + len(job.sems)]))
            i_in += len(job.ins) + len(job.ios)
            i_out += len(job.results())
            i_sem += len(job.sems)
        steps = [pl.program_id(d) for d in range(len(grid))]

        @pl.when(functools.reduce(jnp.logical_and, [s == 0 for s in steps]))
        def _():
            for job, (data, sem) in zip(jobs, views):
                job.start(data, sem)

        body(*refs[:n_in], *refs[first_out:first_out + n_out],
             *refs[len(refs) - len(sems) - n_scratch:len(refs) - len(sems)])

        @pl.when(functools.reduce(jnp.logical_and, [s == g - 1 for s, g in zip(steps, grid)]))
        def _():
            for job, (data, sem) in zip(jobs, views):
                job.finish(data, sem)

    hbm = pl.BlockSpec(memory_space=pl.ANY)
    res = pl.pallas_call(
        carried, name=name, grid=grid, in_specs=in_specs + [hbm] * (len(extra_in) + len(after)),
        out_specs=out_specs + [hbm] * len(extra_out), out_shape=out_shape + extra_out,
        input_output_aliases=aliases, scratch_shapes=scratch_shapes + sems,
        compiler_params=_params(("arbitrary",) * len(grid)),
    )(*operands, *extra_in, *after)
    carried_results, at = [], n_out
    for job in jobs:
        carried_results.append(list(res[at:at + len(job.results())]))
        at += len(job.results())
    return list(res[:n_out]), carried_results


def _run_jobs(jobs, *, name):
    first = jobs[0]
    n_in, n_io = len(first.ins), len(first.ios)
    out_shape = first.results()
    n_sems = [len(job.sems) for job in jobs]

    def body(*refs):
        data = list(refs[:n_in]) + list(refs[n_in + n_io:n_in + n_io + len(out_shape)])
        at = n_in + n_io + len(out_shape)
        for job, ns in zip(jobs, n_sems):
            job.start(data, refs[at:at + ns])
            job.finish(data, refs[at:at + ns])
            at += ns

    hbm = pl.BlockSpec(memory_space=pl.ANY)
    return pl.pallas_call(
        body, name=name, in_specs=[hbm] * (n_in + n_io), out_specs=[hbm] * len(out_shape), out_shape=out_shape,
        input_output_aliases={n_in + t: t for t in range(n_io)},
        scratch_shapes=[s for job in jobs for s in job.sems],
    )(*first.ins, *first.ios)


class _SemaphoreGrid:
    def __init__(self, refs, shape):
        self.refs, self.shape = list(refs), tuple(shape)

    @property
    def at(self):
        return self

    def __getitem__(self, index):
        index = index if isinstance(index, tuple) else (index,)
        flat = 0
        for i, extent in zip(index, self.shape):
            flat = flat * extent + i
        return self.refs[flat]


def _semaphore_grids(job, refs):
    grids, at = [], 0
    for sem in job.sems:
        count = math.prod(sem.shape)
        grids.append(_SemaphoreGrid(refs[at:at + count], sem.shape))
        at += count
    return grids


def _split_start(job, *, name):
    arrays = job.ins + job.ios + [lax.empty(s.shape, s.dtype) for s in job.outs]
    n, ns = len(arrays), sum(math.prod(sem.shape) for sem in job.sems)

    def body(*refs):
        job.start(list(refs[:n]), _semaphore_grids(job, refs[n:n + ns]))
        refs[-1][...] = jnp.zeros_like(refs[-1])

    hbm = pl.BlockSpec(memory_space=pltpu.HBM)
    res = pl.pallas_call(
        body, name=name,
        out_shape=(*[pltpu.SemaphoreType.DMA(())] * ns, *[pltpu.HBM(a.shape, a.dtype) for a in arrays],
                   jax.ShapeDtypeStruct((8, HEAD_DIM), F32)),
        in_specs=[hbm] * n,
        out_specs=(*[pl.BlockSpec(memory_space=pltpu.SEMAPHORE)] * ns, *[hbm] * n,
                   pl.BlockSpec(memory_space=pltpu.VMEM)),
        input_output_aliases={t: ns + t for t in range(n)},
        compiler_params=pltpu.CompilerParams(has_side_effects=pltpu.SideEffectType.DATAFLOW_SIDE_EFFECTING),
    )(*[pltpu.with_memory_space_constraint(a, pltpu.HBM) for a in arrays])
    return dict(job=job, sems=list(res[:ns]), arrays=list(res[ns:ns + n]), token=res[-1])


def _split_wait(started, after, *, name):
    job, arrays, sems = started["job"], started["arrays"], started["sems"]
    n, ns = len(arrays), len(sems)

    def body(*refs):
        job.finish(list(refs[:n]), _semaphore_grids(job, refs[n:n + ns]))

    hbm = pl.BlockSpec(memory_space=pltpu.HBM)
    return pl.pallas_call(
        body, name=name, out_shape=[pltpu.HBM(a.shape, a.dtype) for a in arrays],
        in_specs=[hbm] * n + [pl.BlockSpec(memory_space=pltpu.SEMAPHORE)] * ns
        + [pl.BlockSpec(memory_space=pl.ANY)] * len(after),
        out_specs=[hbm] * n, input_output_aliases={t: t for t in range(n)},
        compiler_params=pltpu.CompilerParams(has_side_effects=pltpu.SideEffectType.DATAFLOW_SIDE_EFFECTING),
    )(*arrays, *sems, *after)


def _dot(a, b, ta=False, tb=False):
    return lax.dot_general(a, b, (((0 if ta else 1,), (1 if tb else 0,)), ((), ())),
                           preferred_element_type=F32)


def _tile(n, want):
    t = min(n, want) // 128 * 128
    while n % t:
        t -= 128
    return t


def _a_spec(ta, tm, tk):
    return pl.BlockSpec((tk, tm), lambda i, j, k: (k, i)) if ta else pl.BlockSpec((tm, tk), lambda i, j, k: (i, k))


def _b_spec(tb, tk, tn):
    return pl.BlockSpec((tn, tk), lambda i, j, k: (j, k)) if tb else pl.BlockSpec((tk, tn), lambda i, j, k: (k, j))


def _accumulate(accs, nk, products, finish):
    if nk == 1:
        finish(*products())
        return
    k = pl.program_id(2)

    @pl.when(k == 0)
    def _():
        for acc, p in zip(accs, products()):
            acc[...] = p

    if nk > 2:
        @pl.when(jnp.logical_and(k > 0, k < nk - 1))
        def _():
            for acc, p in zip(accs, products()):
                acc[...] += p

    @pl.when(k == nk - 1)
    def _():
        finish(*[acc[...] + p for acc, p in zip(accs, products())])


def _matmul(a, b, *, name, ta=False, tb=False, out_dtype=F32, residual=None, tm=1024, tn=1024, tk=2048, jobs=()):
    m, kdim = (a.shape[1], a.shape[0]) if ta else a.shape
    n = b.shape[0] if tb else b.shape[1]
    tm, tn, tk = _tile(m, tm), _tile(n, tn), _tile(kdim, tk)
    nk = kdim // tk

    def body(*refs):
        a_ref, b_ref = refs[:2]
        r_ref = refs[2] if residual is not None else None
        o_ref = refs[-1] if nk == 1 else refs[-2]

        def finish(total):
            if residual is not None:
                total = total + r_ref[...]
            o_ref[...] = total.astype(out_dtype)

        _accumulate(refs[-1:] if nk > 1 else (), nk, lambda: (_dot(a_ref[...], b_ref[...], ta, tb),), finish)

    o_spec = pl.BlockSpec((tm, tn), lambda i, j, k: (i, j))
    in_specs = [_a_spec(ta, tm, tk), _b_spec(tb, tk, tn)]
    operands = [a, b]
    if residual is not None:
        in_specs.append(o_spec)
        operands.append(residual)
    (out,), carried = _call(
        body, name=name, grid=(m // tm, n // tn, nk), in_specs=in_specs, out_specs=[o_spec],
        out_shape=[jax.ShapeDtypeStruct((m, n), out_dtype)], operands=operands,
        scratch_shapes=[pltpu.VMEM((tm, tn), F32)] * (nk > 1),
        semantics=("parallel", "parallel", "arbitrary"), jobs=jobs)
    return (out, carried) if jobs else out


def _matmul_pieces_nt(pieces, b, *, name, tm=512, tn=1024, jobs=(), after=()):
    m, kp = pieces[0].shape
    n = b.shape[0]
    tm, tn = _tile(m, tm), _tile(n, tn)
    count = len(pieces)

    def body(*refs):
        b_ref, o_ref = refs[count], refs[count + 1]
        total = _dot(refs[0][...], b_ref[:, pl.ds(0, kp)], tb=True)
        for p in range(1, count):
            total = total + _dot(refs[p][...], b_ref[:, pl.ds(p * kp, kp)], tb=True)
        o_ref[...] = total

    piece = pl.BlockSpec((tm, kp), lambda j, i: (i, 0))
    (out,), carried = _call(
        body, name=name, grid=(n // tn, m // tm),
        in_specs=[piece] * count + [pl.BlockSpec((tn, count * kp), lambda j, i: (j, 0))],
        out_specs=[pl.BlockSpec((tm, tn), lambda j, i: (i, j))],
        out_shape=[jax.ShapeDtypeStruct((m, n), F32)], operands=[*pieces, b],
        semantics=("parallel", "parallel"), jobs=jobs, after=after)
    return (out, carried) if jobs else out


def _weight_grad_pieces(a, pieces, *, name):
    tokens, m = a.shape
    np_ = pieces[0].shape[1]
    tm = 1024 if m % 1024 == 0 else _tile(m, 1408)
    tn = _tile(np_, 512)
    nb = np_ // tn
    out = None
    for first in range(0, len(pieces), WEIGHT_GRAD_GROUP):
        group = pieces[first:first + WEIGHT_GRAD_GROUP]

        def body(*refs, count=len(group)):
            t_now = pl.program_id(1) // nb
            for t in range(count):
                @pl.when(t_now == t)
                def _(t=t):
                    refs[-1][...] = _dot(refs[0][...], refs[1 + t][...], ta=True)

        def piece_spec(t):
            return pl.BlockSpec((tokens, tn), lambda i, j: (0, jnp.clip(j - t * nb, 0, nb - 1)))

        in_specs = [pl.BlockSpec((tokens, tm), lambda i, j: (0, i))] + [piece_spec(t) for t in range(len(group))]
        operands = [a, *group]
        if out is not None:
            in_specs.append(pl.BlockSpec(memory_space=pl.ANY))
            operands.append(out)
        out = pl.pallas_call(
            body, name="%s_%d" % (name, first), grid=(m // tm, nb * len(group)), in_specs=in_specs,
            out_specs=pl.BlockSpec((tm, tn), lambda i, j, first=first: (i, first * nb + j)),
            out_shape=jax.ShapeDtypeStruct((m, len(pieces) * np_), F32),
            input_output_aliases={len(operands) - 1: 0} if out is not None else {},
            compiler_params=_params(("parallel", "arbitrary")),
        )(*operands)
    return out


def _weight_grad(a, g, *, name, jobs=()):
    tokens, m = a.shape
    tm = 1024 if m % 1024 == 0 else _tile(m, 1408)
    return _matmul(a, g, name=name, ta=True, tm=tm, tn=512, tk=tokens, jobs=jobs)


def _swiglu_fwd(n2, w_gate, w_up, *, tm=1024, tn=512, tk=2048, jobs=()):
    m, kdim = n2.shape
    n = w_gate.shape[1]
    tm, tn, tk = _tile(m, tm), _tile(n, tn), _tile(kdim, tk)
    nk = kdim // tk

    def body(a_ref, g_ref, u_ref, gate_ref, up_ref, act_ref, *acc):
        def products():
            a = a_ref[...]
            return _dot(a, g_ref[...]), _dot(a, u_ref[...])

        def finish(g, u):
            gate_ref[...] = g.astype(BF16)
            up_ref[...] = u.astype(BF16)
            act_ref[...] = (g * _sigmoid(g) * u).astype(BF16)

        _accumulate(acc, nk, products, finish)

    o_spec = pl.BlockSpec((tm, tn), lambda i, j, k: (i, j))
    o_shape = jax.ShapeDtypeStruct((m, n), BF16)
    return _call(
        body, name="swiglu_fwd", grid=(m // tm, n // tn, nk),
        in_specs=[_a_spec(False, tm, tk), _b_spec(False, tk, tn), _b_spec(False, tk, tn)],
        out_specs=[o_spec] * 3, out_shape=[o_shape] * 3, operands=[n2, w_gate, w_up],
        scratch_shapes=[pltpu.VMEM((tm, tn), F32)] * (2 * (nk > 1)),
        semantics=("parallel", "parallel", "arbitrary"), jobs=jobs)


def _swiglu_bwd_act(dh2, w_down, gate, up, *, tm=1024, tn=512, tk=2048):
    m, kdim = dh2.shape
    n = w_down.shape[0]
    tm, tn, tk = _tile(m, tm), _tile(n, tn), _tile(kdim, tk)
    nk = kdim // tk

    sub = _tile(tn, 256)

    def body(a_ref, b_ref, gate_ref, up_ref, dgate_ref, dup_ref, *acc):
        def finish(dact, cols=slice(None)):
            g = gate_ref[:, cols].astype(F32)
            u = up_ref[:, cols].astype(F32)
            sg = _sigmoid(g)
            dup_ref[:, cols] = (dact * g * sg).astype(BF16)
            dgate_ref[:, cols] = (dact * u * sg * (1.0 + g * (1.0 - sg))).astype(BF16)

        if nk == 1:
            a = a_ref[...]
            for c in range(tn // sub):
                cols = pl.ds(c * sub, sub)
                finish(_dot(a, b_ref[cols, :], tb=True), cols)
        else:
            _accumulate(acc, nk, lambda: (_dot(a_ref[...], b_ref[...], tb=True),), finish)

    o_spec = pl.BlockSpec((tm, tn), lambda i, j, k: (i, j))
    o_shape = jax.ShapeDtypeStruct((m, n), BF16)
    return pl.pallas_call(
        body, name="swiglu_bwd_act", grid=(m // tm, n // tn, nk),
        in_specs=[_a_spec(False, tm, tk), _b_spec(True, tk, tn), o_spec, o_spec],
        out_specs=[o_spec] * 2, out_shape=[o_shape] * 2,
        scratch_shapes=[pltpu.VMEM((tm, tn), F32)] * (nk > 1),
        compiler_params=_params(("parallel", "parallel", "arbitrary")),
    )(dh2, w_down, gate, up)


def _swiglu_bwd_in(dgate, dup, w_gate, w_up, *, tm=1024, tn=1024, tk=1408, jobs=()):
    m, kdim = dgate.shape
    n = w_gate.shape[0]
    tm, tn, tk = _tile(m, tm), _tile(n, tn), _tile(kdim, tk)
    nk = kdim // tk

    def body(a1_ref, a2_ref, b1_ref, b2_ref, o_ref, *acc):
        def product():
            return (_dot(a1_ref[...], b1_ref[...], tb=True) + _dot(a2_ref[...], b2_ref[...], tb=True),)

        def finish(total):
            o_ref[...] = total

        _accumulate(acc, nk, product, finish)

    a_spec, b_spec = _a_spec(False, tm, tk), _b_spec(True, tk, tn)
    (out,), carried = _call(
        body, name="swiglu_bwd_in", grid=(m // tm, n // tn, nk),
        in_specs=[a_spec, a_spec, b_spec, b_spec],
        out_specs=[pl.BlockSpec((tm, tn), lambda i, j, k: (i, j))],
        out_shape=[jax.ShapeDtypeStruct((m, n), F32)], operands=[dgate, dup, w_gate, w_up],
        scratch_shapes=[pltpu.VMEM((tm, tn), F32)] * (nk > 1),
        semantics=("parallel", "parallel", "arbitrary"), jobs=jobs)
    return out, carried


def _row_block(rows, cols):
    tr = min(rows, max(16, ELEM_BLOCK_BYTES // (4 * cols) // 16 * 16))
    while rows % tr:
        tr -= 16
    return tr


def _rmsnorm_fwd(x, g, *, name, after=None):
    s, d = x.shape
    tr = _row_block(s, d)

    def body(x_ref, g_ref, *rest):
        xv = x_ref[...]
        r = lax.rsqrt(jnp.mean(xv * xv, axis=-1, keepdims=True) + EPS)
        rest[-1][...] = (xv * r * g_ref[...]).astype(BF16)

    row = pl.BlockSpec((tr, d), lambda i: (i, 0))
    in_specs = [row, pl.BlockSpec((1, d), lambda i: (0, 0))]
    operands = [x, g]
    if after is not None:
        in_specs.append(pl.BlockSpec(after.shape, lambda i: (0, 0)))
        operands.append(after)
    return pl.pallas_call(
        body, name=name, grid=(s // tr,), in_specs=in_specs,
        out_specs=row, out_shape=jax.ShapeDtypeStruct((s, d), BF16),
        compiler_params=_params(("parallel",)),
    )(*operands)


def _rmsnorm_bwd_rows(xv, gv, dy):
    r = lax.rsqrt(jnp.mean(xv * xv, axis=-1, keepdims=True) + EPS)
    xhat = xv * r
    dxh = dy * gv
    dx = r * (dxh - xhat * jnp.mean(dxh * xhat, axis=-1, keepdims=True))
    return dx, dy * xhat


def _rmsnorm_bwd(dn, x, g, skip, *, name, after=()):
    s, d = x.shape
    tr = _row_block(s, d)

    def body(dn_ref, x_ref, g_ref, skip_ref, *rest):
        dx_ref, dxb_ref, dg_ref = rest[len(after):]
        dx, dgr = _rmsnorm_bwd_rows(x_ref[...], g_ref[...], dn_ref[...])
        dx = dx + skip_ref[...]
        dx_ref[...] = dx
        dxb_ref[...] = dx.astype(BF16)

        @pl.when(pl.program_id(0) == 0)
        def _():
            dg_ref[...] = jnp.zeros_like(dg_ref)

        dg_ref[...] += jnp.sum(dgr, axis=0, keepdims=True)

    row = pl.BlockSpec((tr, d), lambda i: (i, 0))
    vec = pl.BlockSpec((1, d), lambda i: (0, 0))
    return pl.pallas_call(
        body, name=name, grid=(s // tr,),
        in_specs=[row, row, vec, row] + [pl.BlockSpec(memory_space=pl.ANY)] * len(after),
        out_specs=[row, row, vec],
        out_shape=[jax.ShapeDtypeStruct((s, d), F32), jax.ShapeDtypeStruct((s, d), BF16),
                   jax.ShapeDtypeStruct((1, d), F32)],
        compiler_params=_params(("arbitrary",)),
    )(dn, x, g, skip, *after)


def _loss_head(h2, g, target):
    s, d = h2.shape
    tr = _row_block(s, d)

    def body(h_ref, g_ref, t_ref, dh_ref, dhb_ref, dg_ref, loss_ref):
        hv = h_ref[...]
        gv = g_ref[...]
        r = lax.rsqrt(jnp.mean(hv * hv, axis=-1, keepdims=True) + EPS)
        err = hv * r * gv - t_ref[...]
        dx, dgr = _rmsnorm_bwd_rows(hv, gv, err * (1.0 / d))
        dh_ref[...] = dx
        dhb_ref[...] = dx.astype(BF16)

        @pl.when(pl.program_id(0) == 0)
        def _():
            dg_ref[...] = jnp.zeros_like(dg_ref)
            loss_ref[...] = jnp.zeros_like(loss_ref)

        dg_ref[...] += jnp.sum(dgr, axis=0, keepdims=True)
        row_loss = jnp.mean(err * err, axis=-1, keepdims=True)
        loss_ref[...] += 0.5 * jnp.sum(row_loss, axis=0, keepdims=True)

    row = pl.BlockSpec((tr, d), lambda i: (i, 0))
    vec = pl.BlockSpec((1, d), lambda i: (0, 0))
    one = pl.BlockSpec((1, 1), lambda i: (0, 0))
    return pl.pallas_call(
        body, name="loss_head", grid=(s // tr,), in_specs=[row, vec, row],
        out_specs=[row, row, vec, one],
        out_shape=[jax.ShapeDtypeStruct((s, d), F32), jax.ShapeDtypeStruct((s, d), BF16),
                   jax.ShapeDtypeStruct((1, d), F32), jax.ShapeDtypeStruct((1, 1), F32)],
        compiler_params=_params(("arbitrary",)),
    )(h2, g, target)


def _attention_bias_tables():
    k = np.arange(-ATT_KB, ATT_KB + 1)[:, None, None]
    delta = k * ATT_BLOCK + np.arange(ATT_BLOCK)[None, None, :] - np.arange(ATT_BLOCK)[None, :, None]
    dist = np.abs(delta)
    count = np.zeros(delta.shape, np.int32)
    for window, dilation in DILATED_PATTERNS:
        count += (delta % dilation == 0) & (dist <= min(window // 2, ATT_NEAR))
    logc = np.where(count > 0, np.log(np.maximum(count, 1)), MASKED)
    return dist.astype(np.float32), logc.astype(np.float32)


def _far_bias_tables(per_class):
    steps = np.abs(np.arange(per_class)[:, None] - np.arange(per_class)[None, :]) * ATT_CLASSES
    valid = (steps > ATT_NEAR) & (steps <= ATT_REACH)
    return steps.astype(np.float32), np.where(valid, 0.0, MASKED).astype(np.float32)


def _to_classes(x):
    s, cols = x.shape
    return jnp.reshape(jnp.transpose(jnp.reshape(x, (s // ATT_CLASSES, ATT_CLASSES, cols)), (1, 0, 2)), (s, cols))


def _from_classes(x):
    s, cols = x.shape
    return jnp.reshape(jnp.transpose(jnp.reshape(x, (ATT_CLASSES, s // ATT_CLASSES, cols)), (1, 0, 2)), (s, cols))


def _head_bias(bias_ref, slope, dist_ref, logc_ref):
    for kk in range(ATT_WINDOW):
        bias_ref[kk] = logc_ref[kk] - slope * dist_ref[kk]
    bias_ref[ATT_WINDOW] = jnp.full((ATT_BLOCK, ATT_BLOCK), MASKED, F32)


def _window_start(i, nq, nwin):
    return jnp.clip(i - ATT_KB, 0, nq - nwin)


def _window_block(j, i):
    rows = pl.ds(pl.multiple_of(j * ATT_BLOCK, ATT_BLOCK), ATT_BLOCK)
    kk = j - i + ATT_KB
    return rows, jnp.where(jnp.logical_and(kk >= 0, kk < ATT_WINDOW), kk, ATT_WINDOW)


def _attention_far_fwd(qkv, slopes, n_heads, jobs=(), after=()):
    s = qkv.shape[0]
    per_class = s // ATT_CLASSES
    scale = HEAD_DIM ** -0.5
    dist, logc = _far_bias_tables(per_class)

    def body(slope_ref, q_ref, k_ref, v_ref, dist_ref, logc_ref, o_ref, lse_ref):
        bias = logc_ref[...] - slope_ref[pl.program_id(0)] * dist_ref[...]
        for a in range(ATT_FAR_GROUP):
            rows = pl.ds(a * per_class, per_class)
            sc = _dot(q_ref[rows, :], k_ref[rows, :], tb=True) * scale + bias
            m = jnp.maximum(jnp.max(sc, axis=-1, keepdims=True), ROW_MAX_INIT)
            p = jnp.exp(sc - m)
            l = jnp.maximum(jnp.sum(p, axis=-1, keepdims=True), 1e-30)
            o_ref[rows, :] = (_dot(p.astype(BF16), v_ref[rows, :]) / l).astype(BF16)
            lse_ref[rows, :] = jnp.broadcast_to(m + jnp.log(l), (per_class, HEAD_DIM))

    hh = n_heads
    blk = pl.BlockSpec((ATT_FAR_GROUP * per_class, HEAD_DIM), lambda h, r: (r, h))
    table = pl.BlockSpec(dist.shape, lambda h, r: (0, 0))
    return _call(
        body, name="attention_far_fwd", grid=(hh, ATT_CLASSES // ATT_FAR_GROUP),
        in_specs=[pl.BlockSpec(memory_space=pltpu.SMEM), blk,
                  pl.BlockSpec((ATT_FAR_GROUP * per_class, HEAD_DIM), lambda h, r: (r, hh + h)),
                  pl.BlockSpec((ATT_FAR_GROUP * per_class, HEAD_DIM), lambda h, r: (r, 2 * hh + h)), table, table],
        out_specs=[blk, blk],
        out_shape=[jax.ShapeDtypeStruct((s, hh * HEAD_DIM), BF16), jax.ShapeDtypeStruct((s, hh * HEAD_DIM), F32)],
        operands=[slopes, qkv, qkv, qkv, jnp.asarray(dist), jnp.asarray(logc)],
        semantics=("parallel", "parallel"), jobs=jobs, after=after)


def _attention_fwd(proj, slopes, far_out, far_lse, n_heads, jobs=(), after=()):
    s = proj.shape[0]
    nq = s // ATT_BLOCK
    scale = HEAD_DIM ** -0.5
    dist, logc = _attention_bias_tables()

    nwin = min(ATT_WINDOW, nq)

    group = math.gcd(ATT_NEAR_GROUP, nq)

    def body(slope_ref, q_ref, k_ref, v_ref, fo_ref, fl_ref, dist_ref, logc_ref, o_ref, lse_ref, bias_ref, s_ref):
        h, step = pl.program_id(0), pl.program_id(1)

        @pl.when(step == 0)
        def _():
            _head_bias(bias_ref, slope_ref[h], dist_ref, logc_ref)

        for a in range(group):
            i = step * group + a
            mine = pl.ds(a * ATT_BLOCK, ATT_BLOCK)
            q = q_ref[mine, :]
            first = _window_start(i, nq, nwin)
            m = jnp.full((ATT_BLOCK, 1), ROW_MAX_INIT, F32)
            for b in range(nwin):
                rows, kk = _window_block(first + b, i)
                sc = _dot(q, k_ref[rows, :], tb=True) * scale + bias_ref[kk]
                s_ref[a * nwin + b] = sc
                m = jnp.maximum(m, jnp.max(sc, axis=-1, keepdims=True))
            l = jnp.zeros((ATT_BLOCK, 1), F32)
            acc = jnp.zeros((ATT_BLOCK, HEAD_DIM), F32)
            for b in range(nwin):
                rows, _ = _window_block(first + b, i)
                p = jnp.exp(s_ref[a * nwin + b] - m)
                l = l + jnp.sum(p, axis=-1, keepdims=True)
                acc = acc + _dot(p.astype(BF16), v_ref[rows, :])
            near_lse = m + jnp.log(l)
            far_lse_col = fl_ref[mine, :1]
            lse = jnp.maximum(near_lse, far_lse_col)
            lse = lse + jnp.log(jnp.exp(near_lse - lse) + jnp.exp(far_lse_col - lse))
            o_ref[mine, :] = (acc * (jnp.exp(near_lse - lse) / l)
                              + fo_ref[mine, :].astype(F32) * jnp.exp(far_lse_col - lse)).astype(BF16)
            lse_ref[mine, :] = jnp.broadcast_to(lse, (ATT_BLOCK, HEAD_DIM))

    hh = n_heads
    blk = pl.BlockSpec((group * ATT_BLOCK, HEAD_DIM), lambda h, i: (i, h))
    table = pl.BlockSpec(dist.shape, lambda h, i: (0, 0, 0))
    return _call(
        body, name="attention_fwd", grid=(hh, nq // group),
        in_specs=[pl.BlockSpec(memory_space=pltpu.SMEM), blk,
                  pl.BlockSpec((s, HEAD_DIM), lambda h, i: (0, hh + h)),
                  pl.BlockSpec((s, HEAD_DIM), lambda h, i: (0, 2 * hh + h)), blk, blk, table, table],
        out_specs=[blk, blk],
        out_shape=[jax.ShapeDtypeStruct((s, hh * HEAD_DIM), BF16), jax.ShapeDtypeStruct((s, hh * HEAD_DIM), F32)],
        operands=[slopes, proj, proj, proj, far_out, far_lse, jnp.asarray(dist), jnp.asarray(logc)],
        scratch_shapes=[pltpu.VMEM((ATT_WINDOW + 1, ATT_BLOCK, ATT_BLOCK), F32),
                        pltpu.VMEM((group * nwin, ATT_BLOCK, ATT_BLOCK), F32)],
        semantics=("parallel", "arbitrary"), jobs=jobs, after=after)


def _attention_far_bwd(qkv, slopes, out, dout, lse, n_heads):
    s = qkv.shape[0]
    per_class = s // ATT_CLASSES
    scale = HEAD_DIM ** -0.5
    dist, logc = _far_bias_tables(per_class)

    def body(slope_ref, q_ref, k_ref, v_ref, o_ref, do_ref, lse_ref, dist_ref, logc_ref, dq_ref, dk_ref, dv_ref):
        bias = logc_ref[...] - slope_ref[pl.program_id(0)] * dist_ref[...]
        for a in range(ATT_FAR_GROUP):
            rows = pl.ds(a * per_class, per_class)
            q, k, do = q_ref[rows, :], k_ref[rows, :], do_ref[rows, :]
            delta = jnp.sum(do.astype(F32) * o_ref[rows, :].astype(F32), axis=-1, keepdims=True)
            p = jnp.exp(_dot(q, k, tb=True) * scale + bias - lse_ref[rows, :1])
            dv_ref[rows, :] = _dot(p.astype(BF16), do, ta=True).astype(BF16)
            ds = (p * (_dot(do, v_ref[rows, :], tb=True) - delta) * scale).astype(BF16)
            dk_ref[rows, :] = _dot(ds, q, ta=True).astype(BF16)
            dq_ref[rows, :] = _dot(ds, k).astype(BF16)

    hh = n_heads
    blk = pl.BlockSpec((ATT_FAR_GROUP * per_class, HEAD_DIM), lambda h, r: (r, h))
    table = pl.BlockSpec(dist.shape, lambda h, r: (0, 0))
    o_shape = jax.ShapeDtypeStruct((s, hh * HEAD_DIM), BF16)
    return pl.pallas_call(
        body, name="attention_far_bwd", grid=(hh, ATT_CLASSES // ATT_FAR_GROUP),
        in_specs=[pl.BlockSpec(memory_space=pltpu.SMEM), blk,
                  pl.BlockSpec((ATT_FAR_GROUP * per_class, HEAD_DIM), lambda h, r: (r, hh + h)),
                  pl.BlockSpec((ATT_FAR_GROUP * per_class, HEAD_DIM), lambda h, r: (r, 2 * hh + h)),
                  blk, blk, blk, table, table],
        out_specs=[blk] * 3, out_shape=[o_shape] * 3,
        compiler_params=_params(("parallel", "parallel")),
    )(slopes, qkv, qkv, qkv, out, dout, lse, jnp.asarray(dist), jnp.asarray(logc))


def _attention_bwd(proj, slopes, out, lse, dmixed, far_grads, n_heads, jobs=()):
    s = proj.shape[0]
    nq = s // ATT_BLOCK
    scale = HEAD_DIM ** -0.5
    dist, logc = _attention_bias_tables()

    nwin = min(ATT_WINDOW, nq)
    group = math.gcd(ATT_NEAR_GROUP, nq)

    def body(slope_ref, q_ref, k_ref, v_ref, o_ref, do_ref, lse_ref, fdq_ref, fdk_ref, fdv_ref, dist_ref, logc_ref,
             dq_ref, dk_ref, dv_ref, dk_acc, dv_acc, bias_ref):
        h, step = pl.program_id(0), pl.program_id(1)

        @pl.when(step == 0)
        def _():
            dk_acc[...] = jnp.zeros_like(dk_acc)
            dv_acc[...] = jnp.zeros_like(dv_acc)
            _head_bias(bias_ref, slope_ref[h], dist_ref, logc_ref)

        for a in range(group):
            i = step * group + a
            mine = pl.ds(a * ATT_BLOCK, ATT_BLOCK)
            q = q_ref[mine, :]
            do = do_ref[mine, :]
            lse_col = lse_ref[mine, :1]
            delta = jnp.sum(do.astype(F32) * o_ref[mine, :].astype(F32), axis=-1, keepdims=True)
            first = _window_start(i, nq, nwin)
            dq = jnp.zeros((ATT_BLOCK, HEAD_DIM), F32)
            for b in range(nwin):
                rows, kk = _window_block(first + b, i)
                kj = k_ref[rows, :]
                vj = v_ref[rows, :]
                p = jnp.exp(_dot(q, kj, tb=True) * scale + bias_ref[kk] - lse_col)
                dv_acc[rows, :] += _dot(p.astype(BF16), do, ta=True)
                dp = _dot(do, vj, tb=True)
                ds = (p * (dp - delta) * scale).astype(BF16)
                dk_acc[rows, :] += _dot(ds, q, ta=True)
                dq = dq + _dot(ds, kj)
            dq_ref[mine, :] = (dq + fdq_ref[mine, :].astype(F32)).astype(BF16)

        @pl.when(step == nq // group - 1)
        def _():
            dk_ref[...] = (dk_acc[...] + fdk_ref[...].astype(F32)).astype(BF16)
            dv_ref[...] = (dv_acc[...] + fdv_ref[...].astype(F32)).astype(BF16)

    hh = n_heads
    blk = pl.BlockSpec((group * ATT_BLOCK, HEAD_DIM), lambda h, i: (i, h))
    col = pl.BlockSpec((s, HEAD_DIM), lambda h, i: (0, h))
    table = pl.BlockSpec(dist.shape, lambda h, i: (0, 0, 0))
    o_shape = jax.ShapeDtypeStruct((s, hh * HEAD_DIM), BF16)
    return _call(
        body, name="attention_bwd", grid=(hh, nq // group),
        in_specs=[pl.BlockSpec(memory_space=pltpu.SMEM), blk,
                  pl.BlockSpec((s, HEAD_DIM), lambda h, i: (0, hh + h)),
                  pl.BlockSpec((s, HEAD_DIM), lambda h, i: (0, 2 * hh + h)),
                  blk, blk, blk, blk, col, col, table, table],
        out_specs=[blk, col, col], out_shape=[o_shape] * 3,
        operands=[slopes, proj, proj, proj, out, dmixed, lse, *far_grads, jnp.asarray(dist), jnp.asarray(logc)],
        scratch_shapes=[pltpu.VMEM((s, HEAD_DIM), F32)] * 2
        + [pltpu.VMEM((ATT_WINDOW + 1, ATT_BLOCK, ATT_BLOCK), F32)],
        semantics=("parallel", "arbitrary"), jobs=jobs)


def _ret_decays(lgc, lga, strict_c, strict_a):
    c = RET_CHUNK
    rel = (lax.broadcasted_iota(jnp.int32, (c, c), 0) - lax.broadcasted_iota(jnp.int32, (c, c), 1)).astype(F32)
    in_c = (rel > 0) if strict_c else (rel >= 0)
    in_a = (rel < 0) if strict_a else (rel <= 0)
    mask = (jnp.where(in_c, jnp.exp(lgc * jnp.maximum(rel, 0.0)), 0.0)
            + jnp.where(in_a, jnp.exp(lga * jnp.maximum(-rel, 0.0)), 0.0))
    idx = lax.broadcasted_iota(jnp.int32, (c, 1), 0).astype(F32)
    ones = jnp.ones((1, HEAD_DIM), F32)
    dec = dict(
        rel=rel, mask=mask, idx=idx,
        a_c=jnp.exp(lgc * (idx + 1.0)), b_c=jnp.exp(lgc * (c - 1.0 - idx)), chunk_c=jnp.exp(ones * (lgc * c)),
        a_a=jnp.exp(lga * (c - idx)), b_a=jnp.exp(lga * idx), chunk_a=jnp.exp(ones * (lga * c)),
    )
    return dec


def _scaled(x, col):
    return (x.astype(F32) * col).astype(BF16)


def _chunk_rows(i):
    return pl.ds(pl.multiple_of(i * RET_CHUNK, RET_CHUNK), RET_CHUNK)


def _chunk_loop(nc, step, init, unroll=RET_UNROLL):
    group = math.gcd(nc, unroll)

    def trip(t, carry):
        for u in range(group):
            carry = step(t * group + u, carry)
        return carry

    return lax.fori_loop(0, nc // group, trip, init)


def _retention(a, b, c, lg_c, lg_a, *, strict_c, strict_a, scale, n_heads, name, gate=None, norm_w=None, jobs=(),
               heads=None, so_far=None, after=()):
    s = a[0].shape[0]
    nc = s // RET_CHUNK
    epilogue = gate is not None
    first_head, head_count = heads if heads is not None else (0, n_heads)

    def body(*refs):
        lgc_ref, lga_ref, a_ref, b_ref, c_ref = refs[:5]
        if epilogue:
            g_ref, w_ref = refs[5:7]
            o_ref, mix_ref, sa_ref = refs[-3:]
        else:
            o_ref, sa_ref = refs[-2:]
        h = first_head + pl.program_id(0)
        dec = _ret_decays(lgc_ref[h], lga_ref[h], strict_c, strict_a)

        def reverse(t, state):
            i = nc - 1 - t
            sa_ref[i] = state.astype(BF16)
            rows = _chunk_rows(i)
            return state * dec["chunk_a"] + _dot(_scaled(b_ref[rows, :], dec["b_a"]), c_ref[rows, :], ta=True)

        _chunk_loop(nc, reverse, jnp.zeros((HEAD_DIM, HEAD_DIM), F32))

        def forward(i, state):
            rows = _chunk_rows(i)
            ai, bi, ci = a_ref[rows, :], b_ref[rows, :], c_ref[rows, :]
            inner = (_dot(ai, bi, tb=True) * dec["mask"]).astype(BF16)
            out = (_dot(inner, ci) + _dot(_scaled(ai, dec["a_c"]), state.astype(BF16))
                   + _dot(_scaled(ai, dec["a_a"]), sa_ref[i])) * scale
            o_ref[rows, :] = out.astype(BF16)
            if epilogue:
                r = lax.rsqrt(jnp.mean(out * out, axis=-1, keepdims=True) + EPS)
                g = g_ref[rows, :].astype(F32)
                mix_ref[rows, :] = (out * r * w_ref[...] * (g * _sigmoid(g))).astype(BF16)
            return state * dec["chunk_c"] + _dot(_scaled(bi, dec["b_c"]), ci, ta=True)

        _chunk_loop(nc, forward, jnp.zeros((HEAD_DIM, HEAD_DIM), F32))

    def col(first):
        return pl.BlockSpec((s, HEAD_DIM), lambda h: (0, first + first_head + h))

    smem = pl.BlockSpec(memory_space=pltpu.SMEM)
    in_specs = [smem, smem, col(a[1]), col(b[1]), col(c[1])]
    operands = [lg_c, lg_a, a[0], b[0], c[0]]
    o_shape = jax.ShapeDtypeStruct((s, n_heads * HEAD_DIM), BF16)
    out_specs, out_shape = [col(0)], [o_shape]
    if epilogue:
        in_specs += [col(gate[1]), pl.BlockSpec((1, HEAD_DIM), lambda h: (0, first_head + h))]
        operands += [gate[0], norm_w]
        out_specs, out_shape = [col(0)] * 2, [o_shape] * 2
    updates = None
    if so_far is not None:
        updates = {len(operands) + t: t for t in range(len(so_far))}
        in_specs += [pl.BlockSpec(memory_space=pl.ANY)] * len(so_far)
        operands += list(so_far)
    res, carried = _call(
        body, name=name, grid=(head_count,), in_specs=in_specs, out_specs=out_specs, out_shape=out_shape,
        operands=operands, scratch_shapes=[pltpu.VMEM((nc, HEAD_DIM, HEAD_DIM), BF16)],
        semantics=("parallel",), jobs=jobs, updates=updates, after=after)
    res = res if epilogue else res[0]
    return (res, carried) if jobs else res


def _retention_decay_grads(a, b, c, e, lg_c, lg_a, *, scale, n_heads):
    s = a[0].shape[0]
    nc = s // RET_CHUNK
    cf = float(RET_CHUNK)

    def body(lgc_ref, lga_ref, a_ref, b_ref, c_ref, e_ref, gc_ref, ga_ref, sa_ref, ta_ref):
        h = pl.program_id(0)
        lgc, lga = lgc_ref[h], lga_ref[h]
        dec = _ret_decays(lgc, lga, True, True)
        rel, idx = dec["rel"], dec["idx"]
        w_c = jnp.where(rel > 0, rel * jnp.exp(lgc * jnp.maximum(rel, 0.0)), 0.0)
        w_a = jnp.where(rel < 0, -rel * jnp.exp(lga * jnp.maximum(-rel, 0.0)), 0.0)
        zero = jnp.zeros((HEAD_DIM, HEAD_DIM), F32)

        def reverse(t, carry):
            st, dst = carry
            i = nc - 1 - t
            sa_ref[i] = st.astype(BF16)
            ta_ref[i] = dst.astype(BF16)
            rows = _chunk_rows(i)
            bi, ci = b_ref[rows, :], c_ref[rows, :]
            st_new = st * dec["chunk_a"] + _dot(_scaled(bi, dec["b_a"]), ci, ta=True)
            dst_new = (cf * st + dst) * dec["chunk_a"] + _dot(_scaled(bi, idx * dec["b_a"]), ci, ta=True)
            return st_new, dst_new

        _chunk_loop(nc, reverse, (zero, zero))

        def forward(i, carry):
            st, dst, acc_c, acc_a = carry
            rows = _chunk_rows(i)
            ai, bi, ci = a_ref[rows, :], b_ref[rows, :], c_ref[rows, :]
            ev = e_ref[rows, :].astype(F32)
            pg = _dot(ai, bi, tb=True) * _dot(e_ref[rows, :], ci, tb=True)
            a_c, a_a = _scaled(ai, dec["a_c"]), _scaled(ai, dec["a_a"])
            inter_c = _dot(a_c, st.astype(BF16)) * (idx + 1.0) + _dot(a_c, dst.astype(BF16))
            inter_a = _dot(a_a, sa_ref[i]) * (cf - idx) + _dot(a_a, ta_ref[i])
            acc_c = acc_c + jnp.sum(pg * w_c, axis=0, keepdims=True) + jnp.sum(inter_c * ev, axis=0, keepdims=True)
            acc_a = acc_a + jnp.sum(pg * w_a, axis=0, keepdims=True) + jnp.sum(inter_a * ev, axis=0, keepdims=True)
            st_new = st * dec["chunk_c"] + _dot(_scaled(bi, dec["b_c"]), ci, ta=True)
            dst_new = ((cf * st + dst) * dec["chunk_c"]
                       + _dot(_scaled(bi, (cf - 1.0 - idx) * dec["b_c"]), ci, ta=True))
            return st_new, dst_new, acc_c, acc_a

        row = jnp.zeros((1, HEAD_DIM), F32)
        _, _, acc_c, acc_a = _chunk_loop(nc, forward, (zero, zero, row, row))
        gc_ref[...] = jnp.broadcast_to(jnp.sum(acc_c, axis=-1, keepdims=True) * scale, gc_ref.shape)
        ga_ref[...] = jnp.broadcast_to(jnp.sum(acc_a, axis=-1, keepdims=True) * scale, ga_ref.shape)

    def col(first):
        return pl.BlockSpec((s, HEAD_DIM), lambda h: (0, first + h))

    smem = pl.BlockSpec(memory_space=pltpu.SMEM)
    o_spec = pl.BlockSpec((1, 8, HEAD_DIM), lambda h: (h, 0, 0))
    o_shape = jax.ShapeDtypeStruct((n_heads, 8, HEAD_DIM), F32)
    gc, ga = pl.pallas_call(
        body, name="retention_decay_grads", grid=(n_heads,),
        in_specs=[smem, smem, col(a[1]), col(b[1]), col(c[1]), col(e[1])],
        out_specs=[o_spec] * 2, out_shape=[o_shape] * 2,
        scratch_shapes=[pltpu.VMEM((nc, HEAD_DIM, HEAD_DIM), BF16)] * 2,
        compiler_params=_params(("parallel",)),
    )(lg_c, lg_a, a[0], b[0], c[0], e[0])
    return gc[:, 0, 0], ga[:, 0, 0]


def _ret_gate_bwd(dmixed, first_col, out, proj, gate_col, norm_w, n_heads):
    s = out.shape[0]
    tr = _row_block(s, 8 * HEAD_DIM)

    def body(dm_ref, o_ref, g_ref, w_ref, do_ref, dg_ref, dw_ref):
        dm = dm_ref[...].astype(F32)
        ov = o_ref[...].astype(F32)
        g = g_ref[...].astype(F32)
        w = w_ref[...]
        r = lax.rsqrt(jnp.mean(ov * ov, axis=-1, keepdims=True) + EPS)
        ohat = ov * r
        sg = _sigmoid(g)
        silu = g * sg
        dg_ref[...] = (dm * ohat * w * sg * (1.0 + g * (1.0 - sg))).astype(BF16)
        dohat = dm * w * silu
        do_ref[...] = (r * (dohat - ohat * jnp.mean(dohat * ohat, axis=-1, keepdims=True))).astype(BF16)

        @pl.when(pl.program_id(1) == 0)
        def _():
            dw_ref[...] = jnp.zeros_like(dw_ref)

        dw_ref[...] += jnp.sum(dm * ohat * silu, axis=0, keepdims=True)

    def blk(first):
        return pl.BlockSpec((tr, HEAD_DIM), lambda h, i: (i, first + h))

    vec = pl.BlockSpec((1, HEAD_DIM), lambda h, i: (0, h))
    o_shape = jax.ShapeDtypeStruct((s, n_heads * HEAD_DIM), BF16)
    return pl.pallas_call(
        body, name="ret_gate_bwd", grid=(n_heads, s // tr),
        in_specs=[blk(first_col), blk(0), blk(gate_col), vec],
        out_specs=[blk(0), blk(0), vec],
        out_shape=[o_shape, o_shape, jax.ShapeDtypeStruct((1, n_heads * HEAD_DIM), F32)],
        compiler_params=_params(("parallel", "arbitrary")),
    )(dmixed, out, proj, norm_w)


def _step(x, target, norm_mix_w, ret_decay_fwd, ret_decay_bwd, ret_norm_w, norm_ffn_w, norm_final_w, own,
          w_in_started, queued, shard_ids, pos):
    d = x.shape[1]
    nh = d // (2 * HEAD_DIM)
    scale = HEAD_DIM ** -0.5
    slopes = jnp.exp2(-8.0 * jnp.arange(1, nh + 1, dtype=F32) / nh)
    lg_f = -jnp.exp(ret_decay_fwd)
    lg_b = -jnp.exp(ret_decay_bwd)
    q_r, k_r, v_r, g_r = 3 * nh, 4 * nh, 5 * nh, 6 * nh
    ax = BIG_AXIS

    def gather(names, arrays, stage, part=None, peers=(0, 1, 2)):
        return _gather_job(arrays, [ax[k] for k in names], stage, part, peers)

    def add_halves(k, g, received):
        return _add_halves(g, received, ax[k], pos, name="grad_add_halves_" + k)

    def sum_parts(k, g, received, parts):
        return _sum_chip_parts(g, received, parts, ax[k], pos, name="grad_sum_parts_" + k)

    sems, w_in, token = w_in_started
    n1 = _rmsnorm_fwd(x, norm_mix_w, name="norm_mix_fwd", after=token)
    proj = _in_proj_part(n1, w_in, None, shard_ids, 0, out_cols=w_in.shape[1])
    for peer in range(3):
        behind = [proj] + ([queued["w_down"]["token"]] if peer == 0 else [])
        w_in = _split_gather_wait(sems, w_in, ax["w_in"], peer, behind)
        (w_in,) = _run_jobs([gather(["w_in"], [w_in], "d2d", peers=(peer,))], name="all_gather_w_in_sibling_%d" % peer)
        proj = _in_proj_part(n1, w_in, proj, shard_ids, 1 + peer, out_cols=w_in.shape[1])
    (w_gate,) = _split_wait(queued["w_gate"], [proj], name="all_gather_w_gate_wait")
    qkv_classes = _to_classes(proj[:, :3 * nh * HEAD_DIM])
    (ret, ret_mixed), [[w_gate]] = _retention(
        (proj, q_r), (proj, k_r), (proj, v_r), lg_f, lg_b, strict_c=False, strict_a=True, scale=scale, n_heads=nh,
        name="retention_fwd_first", gate=(proj, g_r), norm_w=ret_norm_w, heads=(0, nh // 2),
        jobs=[gather(["w_gate"], [w_gate], "d2d")])
    (far_out, far_lse), _ = _attention_far_fwd(qkv_classes, slopes, nh, after=[ret_mixed])
    ret, ret_mixed = _retention(
        (proj, q_r), (proj, k_r), (proj, v_r), lg_f, lg_b, strict_c=False, strict_a=True, scale=scale, n_heads=nh,
        name="retention_fwd_second", gate=(proj, g_r), norm_w=ret_norm_w, heads=(nh // 2, nh - nh // 2),
        so_far=[ret, ret_mixed], after=[far_out])
    (w_out,) = _split_wait(queued["w_out"], [ret_mixed], name="all_gather_w_out_wait")
    (attn, lse), [[w_out]] = _attention_fwd(
        proj, slopes, _from_classes(far_out), _from_classes(far_lse), nh, after=[ret_mixed],
        jobs=[gather(["w_out"], [w_out], "d2d")])
    mixed = jnp.concatenate([attn, ret_mixed], axis=1)
    (w_up,) = _split_wait(queued["w_up"], [mixed], name="all_gather_w_up_wait")
    h1, [[w_up]] = _matmul(mixed, w_out, name="out_proj", residual=x, jobs=[gather(["w_up"], [w_up], "d2d")])
    n2 = _rmsnorm_fwd(h1, norm_ffn_w, name="norm_ffn_fwd")
    (w_down,) = _split_wait(queued["w_down"], [n2], name="all_gather_w_down_wait")
    (gate, up, act), [[w_down]] = _swiglu_fwd(n2, w_gate, w_up, jobs=[gather(["w_down"], [w_down], "d2d")])
    h2 = _matmul(act, w_down, name="down_proj", residual=h1, tk=2816)
    dh2, dh2_b, d_norm_final, loss = _loss_head(h2, norm_final_w, target)

    dgate, dup = _swiglu_bwd_act(dh2_b, w_down, gate, up)
    g_down = _weight_grad(act, dh2_b, name="grad_w_down")
    g_gate, [[r_down]] = _weight_grad(n2, dgate, name="grad_w_gate", jobs=[_exchange_job([g_down], [ax["w_down"]])])
    s_down = add_halves("w_down", g_down, r_down)
    g_up, [[r_gate], [p_down]] = _weight_grad(
        n2, dup, name="grad_w_up",
        jobs=[_exchange_job([g_gate], [ax["w_gate"]]), _send_sums_job([s_down], [ax["w_down"]], (0, 1, 2))])
    s_gate = add_halves("w_gate", g_gate, r_gate)
    dn2, [[r_up], [p_gate], [p_down]] = _swiglu_bwd_in(
        dgate, dup, w_gate, w_up,
        jobs=[_exchange_job([g_up], [ax["w_up"]]), _send_sums_job([s_gate], [ax["w_gate"]]),
              _send_sums_job([s_down], [ax["w_down"]], (1, 1, 2), landing=[p_down])])
    h_down = sum_parts("w_down", g_down, r_down, p_down)
    s_up = add_halves("w_up", g_up, r_up)
    h_gate = sum_parts("w_gate", g_gate, r_gate, p_gate)
    dh1, dh1_b, d_norm_ffn = _rmsnorm_bwd(dn2, h1, norm_ffn_w, dh2, name="norm_ffn_bwd")

    dmixed, [[gr_down], [p_up]] = _matmul(
        dh1_b, w_out, name="out_proj_bwd", tb=True, out_dtype=BF16,
        jobs=[_join_job([h_down], [ax["w_down"]]), _send_sums_job([s_up], [ax["w_up"]], (0, 1, 4))])
    far_in = [_to_classes(t) for t in (attn, dmixed[:, :nh * HEAD_DIM], lse)]
    g_out, [[p_up]] = _weight_grad(mixed, dh1_b, name="grad_w_out",
                                   jobs=[_send_sums_job([s_up], [ax["w_up"]], (1, 1, 4), landing=[p_up])])
    d_ret, dg_r, d_ret_norm = _ret_gate_bwd(dmixed, nh, ret, proj, g_r, ret_norm_w, nh)
    far_grads = _attention_far_bwd(qkv_classes, slopes, *far_in, nh)
    far_grads = [_from_classes(t) for t in far_grads]
    dq_r, [[gr_gate], [p_up]] = _retention(
        (d_ret, 0), (proj, v_r), (proj, k_r), lg_f, lg_b, strict_c=False, strict_a=True, scale=scale, n_heads=nh,
        name="retention_dq",
        jobs=[_join_job([h_gate], [ax["w_gate"]]), _send_sums_job([s_up], [ax["w_up"]], (2, 1, 4), landing=[p_up])])
    (dq_a, dk_a, dv_a), [[p_up], [r_out]] = _attention_bwd(
        proj, slopes, attn, lse, dmixed, far_grads, nh,
        jobs=[_send_sums_job([s_up], [ax["w_up"]], (3, 1, 4), landing=[p_up]),
              _exchange_job([g_out], [ax["w_out"]])])
    s_out = add_halves("w_out", g_out, r_out)
    h_up = sum_parts("w_up", g_up, r_up, p_up)
    dv_r, [[p_out], [gr_up]] = _retention(
        (proj, k_r), (proj, q_r), (d_ret, 0), lg_b, lg_f, strict_c=True, strict_a=False, scale=scale, n_heads=nh,
        name="retention_dv", jobs=[_send_sums_job([s_out], [ax["w_out"]]), _join_job([h_up], [ax["w_up"]])])
    h_out = sum_parts("w_out", g_out, r_out, p_out)
    dk_r, [[gr_out]] = _retention(
        (proj, v_r), (d_ret, 0), (proj, q_r), lg_b, lg_f, strict_c=True, strict_a=False, scale=scale, n_heads=nh,
        name="retention_dk", jobs=[_join_job([h_out], [ax["w_out"]])])
    dlg_f, dlg_b = _retention_decay_grads((proj, q_r), (proj, k_r), (proj, v_r), (d_ret, 0), lg_f, lg_b,
                                          scale=scale, n_heads=nh)
    dproj = [dq_a, dk_a, dv_a, dq_r, dk_r, dv_r, dg_r]
    g_in = _weight_grad_pieces(n1, dproj, name="grad_w_in")
    exchange = _split_start(_exchange_job([g_in], [ax["w_in"]]), name="grad_exchange_w_in_start")
    dn1 = _matmul_pieces_nt(dproj, w_in, name="in_proj_bwd", after=[exchange["token"]])
    g_in, r_in = _split_wait(exchange, [dn1], name="grad_exchange_w_in_wait")
    s_in = add_halves("w_in", g_in, r_in)
    sending = _split_start(_send_sums_job([s_in], [ax["w_in"]]), name="grad_send_w_in_start")
    dx, _, d_norm_mix = _rmsnorm_bwd(dn1, x, norm_mix_w, dh1, name="norm_mix_bwd", after=[sending["token"]])

    small = dict(loss=loss[0, 0], norm_mix_w=d_norm_mix, ret_decay_fwd=dlg_f * lg_f, ret_decay_bwd=dlg_b * lg_b,
                 ret_norm_w=d_ret_norm, norm_ffn_w=d_norm_ffn, norm_final_w=d_norm_final)
    return (dx, dict(w_out=gr_out, w_gate=gr_gate, w_up=gr_up, w_down=gr_down), small,
            dict(sending=sending, grad=g_in, received=r_in))


def _mesh_position():
    x, y, c = lax.axis_index("x"), lax.axis_index("y"), lax.axis_index("c")
    chips = [(1 - x, y), (x, 1 - y), (1 - x, 1 - y)]
    return x, y, c, chips


def _span(span):
    if span is None:
        return slice(None)
    start, size, step = span
    return pl.ds(start if isinstance(start, int) else pl.multiple_of(start, step), size)


def _part_rows(part, rows):
    first, count, of = part
    return first * (rows // of), count * (rows // of), rows // of


def _region(ref, axis, shard, half, shard_size, half_size, part=None, total_rows=None):
    along = None if shard is None else (shard * shard_size, shard_size, shard_size)
    other = None if half is None else (half * half_size, half_size, half_size)
    rows, cols = (other, along) if axis == 1 else (along, other)
    if part is not None:
        start, size, _ = rows if rows is not None else (0, total_rows, None)
        offset, size, step = _part_rows(part, size)
        rows = (start + offset, size, step)
    return ref.at[_span(rows), _span(cols)]


def _fuse(first, second):
    assert not (first.ins or first.outs or second.ins or second.outs)
    assert len(first.ios) == len(second.ios) and all(a is b for a, b in zip(first.ios, second.ios))
    cut = len(first.sems)

    def start(refs, sems):
        first.start(refs, sems[:cut])
        second.start(refs, sems[cut:])

    def finish(refs, sems):
        first.finish(refs, sems[:cut])
        second.finish(refs, sems[cut:])

    return _Job(ios=first.ios, sems=first.sems + second.sems, start=start, finish=finish)


def _gather_job(full, axes, stage, part=None, peers=(0, 1, 2)):
    n = len(full)

    def copies(refs, sems):
        send_sem, recv_sem = sems
        x, y, c, chips = _mesh_position()
        me = 2 * x + y

        def copy(w, k, shard, half, target):
            rows_cols = full[w].shape
            place = _region(refs[w], axes[w], shard, half, rows_cols[axes[w]] // N_CHIPS, rows_cols[1 - axes[w]] // 2,
                            part)
            return pltpu.make_async_remote_copy(
                src_ref=place, dst_ref=place, send_sem=send_sem.at[w, k], recv_sem=recv_sem.at[w, k],
                device_id=target, device_id_type=MESH)

        def sent(w, k):
            if stage == "ici":
                return copy(w, k, me, c, (chips[k][0], chips[k][1], c))
            return copy(w, k, 2 * chips[k][0] + chips[k][1], c, (x, y, 1 - c))

        def landed(w, k):
            return copy(w, k, 2 * chips[k][0] + chips[k][1], c if stage == "ici" else 1 - c, (x, y, 1 - c))

        return sent, landed

    def start(refs, sems):
        sent, _ = copies(refs, sems)
        for w in range(n):
            for k in peers:
                sent(w, k).start()

    def finish(refs, sems):
        sent, landed = copies(refs, sems)
        for w in range(n):
            for k in peers:
                landed(w, k).wait_recv()
                sent(w, k).wait_send()

    return _Job(ios=full, sems=[pltpu.SemaphoreType.DMA((n, 3))] * 2, start=start, finish=finish)


def _exchange_job(grads, axes):
    n = len(grads)

    def half_shape(w):
        return tuple(d // 2 if a != axes[w] else d for a, d in enumerate(grads[w].shape))

    def copy(refs, sems, w):
        x, y, c, _ = _mesh_position()
        return pltpu.make_async_remote_copy(
            src_ref=_region(refs[w], axes[w], None, 1 - c, 0, half_shape(w)[1 - axes[w]]), dst_ref=refs[n + w],
            send_sem=sems[0].at[w], recv_sem=sems[1].at[w], device_id=(x, y, 1 - c), device_id_type=MESH)

    def start(refs, sems):
        for w in range(n):
            copy(refs, sems, w).start()

    def finish(refs, sems):
        for w in range(n):
            copy(refs, sems, w).wait()

    return _Job(ins=grads, outs=[jax.ShapeDtypeStruct(half_shape(w), F32) for w in range(n)],
                sems=[pltpu.SemaphoreType.DMA((n,))] * 2, start=start, finish=finish)


def _half_block_spec(axis, block, half_blocks, use_half):
    if axis == 1:
        if use_half:
            return pl.BlockSpec(block, lambda i, pos: (pos[0] * half_blocks + i, 0))
        return pl.BlockSpec(block, lambda i, pos: (i, 0))
    if use_half:
        return pl.BlockSpec(block, lambda i, pos: (i, pos[0]))
    return pl.BlockSpec(block, lambda i, pos: (i, 0))


def _add_halves(grad, received, axis, pos, *, name):
    rows, cols = received.shape
    tr = _row_block(rows, cols)
    nb = rows // tr

    def body(pos_ref, g_ref, r_ref, o_ref):
        o_ref[...] = (g_ref[...] + r_ref[...]).astype(BF16)

    blk = (tr, cols)
    return pl.pallas_call(
        body, name=name, out_shape=jax.ShapeDtypeStruct((rows, cols), BF16),
        grid_spec=pltpu.PrefetchScalarGridSpec(
            num_scalar_prefetch=1, grid=(nb,),
            in_specs=[_half_block_spec(axis, blk, nb, True), _half_block_spec(axis, blk, nb, False)],
            out_specs=_half_block_spec(axis, blk, nb, False)),
        compiler_params=_params(("parallel",)),
    )(pos, grad, received)


def _send_sums_job(sums, axes, part=None, landing=None):
    n = len(sums)

    def part_shape(w):
        return tuple(d // N_CHIPS if a == axes[w] else d for a, d in enumerate(sums[w].shape))

    def copy(refs, sems, w, k):
        x, y, c, chips = _mesh_position()
        shard = 2 * chips[k][0] + chips[k][1]
        rows = part_shape(w)[0]
        dst = refs[n + w].at[k]
        if part is not None:
            offset, size, _ = _part_rows(part, rows)
            dst = refs[n + w].at[k, pl.ds(offset, size), :]
        return pltpu.make_async_remote_copy(
            src_ref=_region(refs[w], axes[w], shard, None, part_shape(w)[axes[w]], 0, part, rows), dst_ref=dst,
            send_sem=sems[0].at[w, k], recv_sem=sems[1].at[w, k],
            device_id=(chips[k][0], chips[k][1], c), device_id_type=MESH)

    def start(refs, sems):
        for w in range(n):
            for k in range(3):
                copy(refs, sems, w, k).start()

    def finish(refs, sems):
        for w in range(n):
            for k in range(3):
                copy(refs, sems, w, k).wait()

    sems = [pltpu.SemaphoreType.DMA((n, 3))] * 2
    if landing is not None:
        return _Job(ins=sums, ios=landing, sems=sems, start=start, finish=finish)
    return _Job(ins=sums, outs=[jax.ShapeDtypeStruct((3,) + part_shape(w), BF16) for w in range(n)],
                sems=sems, start=start, finish=finish)


def _sum_chip_parts(grad, received, parts, axis, pos, *, name):
    _, rows, cols = parts.shape
    tr = _row_block(rows, cols)
    nb = rows // tr
    blk = (tr, cols)

    def body(pos_ref, g_ref, r_ref, p_ref, o_ref):
        total = g_ref[...] + r_ref[...]
        for k in range(3):
            total = total + p_ref[k].astype(F32)
        o_ref[...] = total

    if axis == 1:
        g_spec = pl.BlockSpec(blk, lambda i, pos: (pos[0] * nb + i, pos[1]))
        r_spec = pl.BlockSpec(blk, lambda i, pos: (i, pos[1]))
        o_spec = pl.BlockSpec(blk, lambda i, pos: (pos[0] * nb + i, 0))
        shard_shape = (2 * rows, cols)
    else:
        g_spec = pl.BlockSpec(blk, lambda i, pos: (pos[1] * nb + i, pos[0]))
        r_spec = pl.BlockSpec(blk, lambda i, pos: (pos[1] * nb + i, 0))
        o_spec = pl.BlockSpec(blk, lambda i, pos: (i, pos[0]))
        shard_shape = (rows, 2 * cols)
    return pl.pallas_call(
        body, name=name, out_shape=jax.ShapeDtypeStruct(shard_shape, F32),
        grid_spec=pltpu.PrefetchScalarGridSpec(
            num_scalar_prefetch=1, grid=(nb,),
            in_specs=[g_spec, r_spec, pl.BlockSpec((3,) + blk, lambda i, pos: (0, i, 0))],
            out_specs=o_spec),
        compiler_params=_params(("parallel",)),
    )(pos, grad, received, parts)


def _join_job(shards, axes):
    n = len(shards)

    def copy(refs, sems, w, other):
        x, y, c, _ = _mesh_position()
        place = _region(refs[w], axes[w], None, 1 - c if other else c, 0, shards[w].shape[1 - axes[w]] // 2)
        return pltpu.make_async_remote_copy(
            src_ref=place, dst_ref=place, send_sem=sems[0].at[w], recv_sem=sems[1].at[w],
            device_id=(x, y, 1 - c), device_id_type=MESH)

    def start(refs, sems):
        for w in range(n):
            copy(refs, sems, w, False).start()

    def finish(refs, sems):
        for w in range(n):
            copy(refs, sems, w, True).wait_recv()
            copy(refs, sems, w, False).wait_send()

    return _Job(ios=shards, sems=[pltpu.SemaphoreType.DMA((n,))] * 2, start=start, finish=finish)


def _all_reduce_small(vec, after=()):
    rows, cols = vec.shape

    def body(v_ref, *rest):
        o_ref, land_ref, send_sem, recv_sem = rest[len(after):]
        x, y, c, _ = _mesh_position()
        me = 4 * x + 2 * y + c
        land_ref[me] = v_ref[...]
        copies = []
        for k in range(1, 8):
            px, py, pc = x ^ (k >> 2), y ^ ((k >> 1) & 1), c ^ (k & 1)
            copies.append(pltpu.make_async_remote_copy(
                src_ref=v_ref, dst_ref=land_ref.at[me], send_sem=send_sem.at[k], recv_sem=recv_sem.at[k],
                device_id=(px, py, pc), device_id_type=MESH))
        for cp in copies:
            cp.start()
        for k in range(1, 8):
            peer = me ^ k
            pltpu.make_async_remote_copy(
                src_ref=v_ref, dst_ref=land_ref.at[peer], send_sem=send_sem.at[k], recv_sem=recv_sem.at[k],
                device_id=(x, y, c), device_id_type=MESH).wait_recv()
        for cp in copies:
            cp.wait_send()
        total = land_ref[0]
        for k in range(1, 8):
            total = total + land_ref[k]
        o_ref[...] = total

    vmem = pl.BlockSpec(memory_space=pltpu.VMEM)
    return pl.pallas_call(
        body, name="all_reduce_small", in_specs=[vmem] + [pl.BlockSpec(memory_space=pl.ANY)] * len(after),
        out_specs=vmem, out_shape=jax.ShapeDtypeStruct((rows, cols), F32),
        scratch_shapes=[pltpu.VMEM((8, rows, cols), F32), pltpu.SemaphoreType.DMA((8,)), pltpu.SemaphoreType.DMA((8,))],
    )(vec, *after)


def _adamw(w, g, m, v, *, name, after=()):
    rows, cols = w.shape
    tr = _row_block(rows, cols) if rows % 8 == 0 else rows
    bc1 = 1.0 - ADAM_B1 ** ADAM_STEP
    bc2 = 1.0 - ADAM_B2 ** ADAM_STEP

    def body(w_ref, g_ref, m_ref, v_ref, *rest):
        go_ref, d_ref, mo_ref, vo_ref = rest[len(after):]
        gv = g_ref[...]
        go_ref[...] = gv
        mn = ADAM_B1 * m_ref[...] + (1.0 - ADAM_B1) * gv
        vn = ADAM_B2 * v_ref[...] + (1.0 - ADAM_B2) * (gv * gv)
        mo_ref[...] = mn
        vo_ref[...] = vn
        d_ref[...] = -ADAM_LR * ((mn / bc1) / (jnp.sqrt(vn / bc2) + ADAM_EPS) + ADAM_WD * w_ref[...])

    blk = pl.BlockSpec((tr, cols), lambda i: (i, 0))
    shape = jax.ShapeDtypeStruct((rows, cols), F32)
    return pl.pallas_call(
        body, name=name, grid=(rows // tr,), in_specs=[blk] * 4 + [pl.BlockSpec(memory_space=pl.ANY)] * len(after),
        out_specs=[blk] * 4, out_shape=[shape] * 4, compiler_params=_params(("parallel",)),
    )(w, g, m, v, *after)


def _to_bf16_in_place(w, axis, pos, *, name, after=None):
    rows, cols = w.shape
    tr = _row_block(rows, cols)
    nb = rows // tr

    def body(pos_ref, w_ref, *rest):
        rest[-1][...] = w_ref[...].astype(BF16)

    if axis == 1:
        o_spec = pl.BlockSpec((tr, cols), lambda i, pos: (i, pos[1]))
        full_shape = (rows, N_CHIPS * cols)
    else:
        o_spec = pl.BlockSpec((tr, cols), lambda i, pos: (pos[1] * nb + i, 0))
        full_shape = (N_CHIPS * rows, cols)
    in_specs = [pl.BlockSpec((tr, cols), lambda i, pos: (i, 0))]
    operands = [pos, w]
    if after is not None:
        in_specs.append(pl.BlockSpec(after.shape, lambda i, pos: (0, 0)))
        operands.append(after)
    return pl.pallas_call(
        body, name=name, out_shape=jax.ShapeDtypeStruct(full_shape, BF16),
        grid_spec=pltpu.PrefetchScalarGridSpec(num_scalar_prefetch=1, grid=(nb,), in_specs=in_specs, out_specs=o_spec),
        compiler_params=_params(("parallel",)),
    )(*operands)


def _split_gather_start(full, axis):
    rows_cols = full.shape

    def body(buf_ref, *rest):
        sems = rest[:6]
        token_ref = rest[7]
        x, y, c, chips = _mesh_position()
        place = _region(buf_ref, axis, 2 * x + y, c, rows_cols[axis] // N_CHIPS, rows_cols[1 - axis] // 2)
        for k in range(3):
            pltpu.make_async_remote_copy(
                src_ref=place, dst_ref=place, send_sem=sems[k], recv_sem=sems[3 + k],
                device_id=(chips[k][0], chips[k][1], c), device_id_type=MESH).start()
        token_ref[...] = jnp.zeros_like(token_ref)

    hbm = pl.BlockSpec(memory_space=pltpu.HBM)
    sem = pl.BlockSpec(memory_space=pltpu.SEMAPHORE)
    res = pl.pallas_call(
        body, name="all_gather_w_in_start",
        out_shape=(*[pltpu.SemaphoreType.DMA(())] * 6, pltpu.HBM(full.shape, full.dtype),
                   jax.ShapeDtypeStruct((8, HEAD_DIM), F32)),
        in_specs=(hbm,), out_specs=(*[sem] * 6, hbm, pl.BlockSpec(memory_space=pltpu.VMEM)),
        input_output_aliases={0: 6},
        compiler_params=pltpu.CompilerParams(has_side_effects=pltpu.SideEffectType.DATAFLOW_SIDE_EFFECTING),
    )(pltpu.with_memory_space_constraint(full, pltpu.HBM))
    return list(res[:6]), res[6], res[7]


def _split_gather_wait(sems, full, axis, peer, after):
    rows_cols = full.shape

    def body(buf_ref, send_sem, recv_sem, *rest):
        x, y, c, chips = _mesh_position()

        def copy(shard):
            place = _region(buf_ref, axis, shard, c, rows_cols[axis] // N_CHIPS, rows_cols[1 - axis] // 2)
            return pltpu.make_async_remote_copy(
                src_ref=place, dst_ref=place, send_sem=send_sem, recv_sem=recv_sem,
                device_id=(chips[peer][0], chips[peer][1], c), device_id_type=MESH)

        copy(2 * x + y).wait_send()
        copy(2 * chips[peer][0] + chips[peer][1]).wait_recv()

    hbm = pl.BlockSpec(memory_space=pltpu.HBM)
    sem = pl.BlockSpec(memory_space=pltpu.SEMAPHORE)
    return pl.pallas_call(
        body, name="all_gather_w_in_wait_%d" % peer, out_shape=pltpu.HBM(full.shape, full.dtype),
        in_specs=(hbm, sem, sem, *[pl.BlockSpec(memory_space=pl.ANY)] * len(after)), out_specs=hbm,
        input_output_aliases={0: 0},
        compiler_params=pltpu.CompilerParams(has_side_effects=pltpu.SideEffectType.DATAFLOW_SIDE_EFFECTING),
    )(full, sems[peer], sems[3 + peer], *after)


def _in_proj_part(n1, w_in, proj, shard_ids, which, *, out_cols):
    m, kdim = n1.shape
    cols = out_cols // N_CHIPS
    tm = _tile(m, 1024)

    def body(ids_ref, a_ref, b_ref, *rest):
        rest[-1][...] = _dot(a_ref[...], b_ref[...]).astype(BF16)

    in_specs = [pl.BlockSpec((tm, kdim), lambda i, ids: (i, 0)),
                pl.BlockSpec((kdim, cols), lambda i, ids: (0, ids[which]))]
    operands = [shard_ids, n1, w_in]
    if proj is not None:
        in_specs.append(pl.BlockSpec(memory_space=pl.ANY))
        operands.append(proj)
    return pl.pallas_call(
        body, name="in_proj_%d" % which, out_shape=jax.ShapeDtypeStruct((m, out_cols), BF16),
        grid_spec=pltpu.PrefetchScalarGridSpec(
            num_scalar_prefetch=1, grid=(m // tm,), in_specs=in_specs,
            out_specs=pl.BlockSpec((tm, cols), lambda i, ids: (i, ids[which]))),
        input_output_aliases={3: 0} if proj is not None else {},
        compiler_params=_params(("parallel",)),
    )(*operands)


BIG = ("w_in", "w_out", "w_gate", "w_up", "w_down")
BIG_AXIS = dict(w_in=1, w_out=0, w_gate=1, w_up=1, w_down=0)
SMALL = ("norm_mix_w", "ret_decay_fwd", "ret_decay_bwd", "ret_norm_w", "norm_ffn_w", "norm_final_w")
ALL_WEIGHTS = ("norm_mix_w", "w_in", "ret_decay_fwd", "ret_decay_bwd", "ret_norm_w", "w_out", "norm_ffn_w",
               "w_gate", "w_up", "w_down", "norm_final_w")
SMALL_ROW = 128 * 8


def _pack_small(small):
    pieces = [jnp.reshape(small["loss"], (1,))] + [jnp.reshape(small[k], (-1,)) for k in SMALL]
    rows = []
    for p in pieces:
        pad = -p.shape[0] % (8 * SMALL_ROW)
        rows.append(jnp.reshape(jnp.pad(p, (0, pad)), (-1, SMALL_ROW)))
    return jnp.concatenate(rows, axis=0)


def _unpack_small(block, like):
    out, row = {}, 0
    for k in ("loss",) + SMALL:
        size = 1 if k == "loss" else like[k].size
        nrows = -(-size // (8 * SMALL_ROW)) * 8
        out[k] = jnp.reshape(block[row:row + nrows], (-1,))[:size]
        row += nrows
    return out


def kernel(x, norm_mix_w, w_in, ret_decay_fwd, ret_decay_bwd, ret_norm_w, w_out, norm_ffn_w, w_gate, w_up, w_down, norm_final_w, loss_target, m_norm_mix_w, m_w_in, m_ret_decay_fwd, m_ret_decay_bwd, m_ret_norm_w, m_w_out, m_norm_ffn_w, m_w_gate, m_w_up, m_w_down, m_norm_final_w, v_norm_mix_w, v_w_in, v_ret_decay_fwd, v_ret_decay_bwd, v_ret_norm_w, v_w_out, v_norm_ffn_w, v_w_gate, v_w_up, v_w_down, v_norm_final_w):
    weights = dict(norm_mix_w=norm_mix_w, w_in=w_in, ret_decay_fwd=ret_decay_fwd, ret_decay_bwd=ret_decay_bwd,
                   ret_norm_w=ret_norm_w, w_out=w_out, norm_ffn_w=norm_ffn_w, w_gate=w_gate, w_up=w_up,
                   w_down=w_down, norm_final_w=norm_final_w)
    m_in = dict(norm_mix_w=m_norm_mix_w, w_in=m_w_in, ret_decay_fwd=m_ret_decay_fwd, ret_decay_bwd=m_ret_decay_bwd,
                ret_norm_w=m_ret_norm_w, w_out=m_w_out, norm_ffn_w=m_norm_ffn_w, w_gate=m_w_gate, w_up=m_w_up,
                w_down=m_w_down, norm_final_w=m_norm_final_w)
    v_in = dict(norm_mix_w=v_norm_mix_w, w_in=v_w_in, ret_decay_fwd=v_ret_decay_fwd, ret_decay_bwd=v_ret_decay_bwd,
                ret_norm_w=v_ret_norm_w, w_out=v_w_out, norm_ffn_w=v_norm_ffn_w, w_gate=v_w_gate, w_up=v_w_up,
                w_down=v_w_down, norm_final_w=v_norm_final_w)
    pos = jnp.stack([lax.axis_index("c"), 2 * lax.axis_index("x") + lax.axis_index("y")]).astype(jnp.int32)

    own = {"w_in": _to_bf16_in_place(weights["w_in"][0], BIG_AXIS["w_in"], pos, name="cast_w_in")}
    w_in_started = _split_gather_start(own["w_in"], BIG_AXIS["w_in"])
    queued, token = {}, w_in_started[2]
    for k in ("w_gate", "w_out", "w_up", "w_down"):
        own[k] = _to_bf16_in_place(weights[k][0], BIG_AXIS[k], pos, name="cast_" + k, after=token)
        queued[k] = _split_start(_gather_job([own[k]], [BIG_AXIS[k]], "ici"), name="all_gather_%s_start" % k)
        token = queued[k]["token"]
    cx, cy = lax.axis_index("x"), lax.axis_index("y")
    shard_ids = jnp.stack([2 * cx + cy, 2 * (1 - cx) + cy, 2 * cx + 1 - cy, 2 * (1 - cx) + 1 - cy]).astype(jnp.int32)

    dx, grad_w, small, w_in_pending = _step(
        x[0], loss_target[0], norm_mix_w, ret_decay_fwd[0], ret_decay_bwd[0], ret_norm_w, norm_ffn_w,
        norm_final_w[None, :], own, w_in_started, queued, shard_ids, pos)

    delta, new_m, new_v = {}, {}, {}

    def update(k, after):
        shape = weights[k].shape
        as2d = (lambda t: jnp.reshape(t, (-1, shape[-1])))
        grad_w[k], delta[k], new_m[k], new_v[k] = (jnp.reshape(t, shape) for t in _adamw(
            as2d(weights[k]), as2d(grad_w[k]), as2d(m_in[k]), as2d(v_in[k]), name="adamw_" + k, after=after))

    others = [k for k in BIG if k != "w_in"]
    for k in others:
        update(k, [w_in_pending["sending"]["token"]])
    _, parts = _split_wait(w_in_pending["sending"], [dx] + [delta[k] for k in others], name="grad_send_w_in_wait")

    half = _sum_chip_parts(w_in_pending["grad"], w_in_pending["received"], parts, BIG_AXIS["w_in"], pos,
                           name="grad_sum_parts_w_in")
    joining = _split_start(_join_job([half], [BIG_AXIS["w_in"]]), name="grad_join_w_in_start")

    like = {k: weights[k] for k in SMALL}
    reduced = _unpack_small(_all_reduce_small(_pack_small(small), after=[joining["token"]]), like)
    loss = reduced["loss"][0]
    for k in SMALL:
        grad_w[k] = jnp.reshape(reduced[k], (1, -1))
        update(k, [])
    (grad_w["w_in"],) = _split_wait(joining, [delta[k] for k in SMALL], name="grad_join_w_in_wait")
    update("w_in", [])

    return (loss, dx[None], *[grad_w[k] for k in ALL_WEIGHTS], *[delta[k] for k in ALL_WEIGHTS],
            *[new_m[k] for k in ALL_WEIGHTS], *[new_v[k] for k in ALL_WEIGHTS])
```

```python
import functools
import math

import numpy as np
import jax
import jax.numpy as jnp
from jax import lax
from jax.experimental import pallas as pl
from jax.experimental.pallas import tpu as pltpu

F32 = jnp.float32
BF16 = jnp.bfloat16
MESH = pl.DeviceIdType.MESH

HEAD_DIM = 128
RET_CHUNK = 128
RET_UNROLL = 8
EPS = 1e-6
DILATED_PATTERNS = ((128, 1), (512, 4), (2048, 16))
ATT_BLOCK = 256
ATT_REACH = max(w // 2 for w, _ in DILATED_PATTERNS)
ATT_NEAR = ATT_BLOCK
ATT_CLASSES = DILATED_PATTERNS[-1][1]
assert all(w // 2 <= ATT_NEAR for w, _ in DILATED_PATTERNS[:-1])
ATT_KB = -(-ATT_NEAR // ATT_BLOCK)
ATT_WINDOW = 2 * ATT_KB + 1
ATT_FAR_GROUP = 16
ATT_NEAR_GROUP = 8
MASKED = -1e30
ROW_MAX_INIT = -1e29
N_CHIPS = 4
VMEM_LIMIT_BYTES = 56 * 1024 * 1024
ELEM_BLOCK_BYTES = 2 * 1024 * 1024
WEIGHT_GRAD_GROUP = 4

ADAM_LR = 0.001
ADAM_B1 = 0.9
ADAM_B2 = 0.999
ADAM_EPS = 1e-08
ADAM_WD = 0.01
ADAM_STEP = 10


def _params(sem=None):
    return pltpu.CompilerParams(dimension_semantics=sem, vmem_limit_bytes=VMEM_LIMIT_BYTES)


def _sigmoid(x):
    return 0.5 * jnp.tanh(0.5 * x) + 0.5


class _Job:
    def __init__(self, *, ins=(), ios=(), outs=(), sems=(), start, finish):
        self.ins, self.ios, self.outs, self.sems = list(ins), list(ios), list(outs), list(sems)
        self.start, self.finish = start, finish

    def results(self):
        return [jax.ShapeDtypeStruct(a.shape, a.dtype) for a in self.ios] + self.outs


def _call(body, *, name, grid, in_specs, out_specs, out_shape, operands, scratch_shapes=(), semantics=None, jobs=(),
          after=(), updates=None):
    in_specs, out_specs, out_shape = list(in_specs), list(out_specs), list(out_shape)
    scratch_shapes = list(scratch_shapes)
    if not jobs:
        n_real = len(in_specs)

        def ordered(*refs):
            body(*refs[:n_real], *refs[n_real + len(after):])

        outs = pl.pallas_call(
            ordered if after else body, name=name, grid=grid,
            in_specs=in_specs + [pl.BlockSpec(memory_space=pl.ANY)] * len(after), out_specs=out_specs,
            out_shape=out_shape, scratch_shapes=scratch_shapes, input_output_aliases=dict(updates or {}),
            compiler_params=_params(semantics))(*operands, *after)
        return outs, []
    n_in, n_out, n_scratch = len(in_specs), len(out_specs), len(scratch_shapes)
    extra_in, extra_out, sems, aliases = [], [], [], dict(updates or {})
    for job in jobs:
        extra_in += job.ins
        for t in range(len(job.ios)):
            aliases[n_in + len(extra_in) + t] = n_out + len(extra_out) + t
        extra_in += job.ios
        extra_out += job.results()
        sems += job.sems

    def carried(*refs):
        x_in = refs[n_in:n_in + len(extra_in)]
        first_out = n_in + len(extra_in) + len(after)
        x_out = refs[first_out + n_out:first_out + n_out + len(extra_out)]
        x_sem = refs[len(refs) - len(sems):]
        views, i_in, i_out, i_sem = [], 0, 0, 0
        for job in jobs:
            data = list(x_in[i_in:i_in + len(job.ins)]) + list(x_out[i_out:i_out + len(job.results())])
            views.append((data, x_sem[i_sem:i_sem + len(job.sems)]))
            i_in += len(job.ins) + len(job.ios)
            i_out += len(job.results())
            i_sem += len(job.sems)
        steps = [pl.program_id(d) for d in range(len(grid))]

        @pl.when(functools.reduce(jnp.logical_and, [s == 0 for s in steps]))
        def _():
            for job, (data, sem) in zip(jobs, views):
                job.start(data, sem)

        body(*refs[:n_in], *refs[first_out:first_out + n_out],
             *refs[len(refs) - len(sems) - n_scratch:len(refs) - len(sems)])

        @pl.when(functools.reduce(jnp.logical_and, [s == g - 1 for s, g in zip(steps, grid)]))
        def _():
            for job, (data, sem) in zip(jobs, views):
                job.finish(data, sem)

    hbm = pl.BlockSpec(memory_space=pl.ANY)
    res = pl.pallas_call(
        carried, name=name, grid=grid, in_specs=in_specs + [hbm] * (len(extra_in) + len(after)),
        out_specs=out_specs + [hbm] * len(extra_out), out_shape=out_shape + extra_out,
        input_output_aliases=aliases, scratch_shapes=scratch_shapes + sems,
        compiler_params=_params(("arbitrary",) * len(grid)),
    )(*operands, *extra_in, *after)
    carried_results, at = [], n_out
    for job in jobs:
        carried_results.append(list(res[at:at + len(job.results())]))
        at += len(job.results())
    return list(res[:n_out]), carried_results


def _run_jobs(jobs, *, name):
    first = jobs[0]
    n_in, n_io = len(first.ins), len(first.ios)
    out_shape = first.results()
    n_sems = [len(job.sems) for job in jobs]

    def body(*refs):
        data = list(refs[:n_in]) + list(refs[n_in + n_io:n_in + n_io + len(out_shape)])
        at = n_in + n_io + len(out_shape)
        for job, ns in zip(jobs, n_sems):
            job.start(data, refs[at:at + ns])
            job.finish(data, refs[at:at + ns])
            at += ns

    hbm = pl.BlockSpec(memory_space=pl.ANY)
    return pl.pallas_call(
        body, name=name, in_specs=[hbm] * (n_in + n_io), out_specs=[hbm] * len(out_shape), out_shape=out_shape,
        input_output_aliases={n_in + t: t for t in range(n_io)},
        scratch_shapes=[s for job in jobs for s in job.sems],
    )(*first.ins, *first.ios)


class _SemaphoreGrid:
    def __init__(self, refs, shape):
        self.refs, self.shape = list(refs), tuple(shape)

    @property
    def at(self):
        return self

    def __getitem__(self, index):
        index = index if isinstance(index, tuple) else (index,)
        flat = 0
        for i, extent in zip(index, self.shape):
            flat = flat * extent + i
        return self.refs[flat]


def _semaphore_grids(job, refs):
    grids, at = [], 0
    for sem in job.sems:
        count = math.prod(sem.shape)
        grids.append(_SemaphoreGrid(refs[at:at + count], sem.shape))
        at += count
    return grids


def _split_start(job, *, name):
    arrays = job.ins + job.ios + [lax.empty(s.shape, s.dtype) for s in job.outs]
    n, ns = len(arrays), sum(math.prod(sem.shape) for sem in job.sems)

    def body(*refs):
        job.start(list(refs[:n]), _semaphore_grids(job, refs[n:n + ns]))
        refs[-1][...] = jnp.zeros_like(refs[-1])

    hbm = pl.BlockSpec(memory_space=pltpu.HBM)
    res = pl.pallas_call(
        body, name=name,
        out_shape=(*[pltpu.SemaphoreType.DMA(())] * ns, *[pltpu.HBM(a.shape, a.dtype) for a in arrays],
                   jax.ShapeDtypeStruct((8, HEAD_DIM), F32)),
        in_specs=[hbm] * n,
        out_specs=(*[pl.BlockSpec(memory_space=pltpu.SEMAPHORE)] * ns, *[hbm] * n,
                   pl.BlockSpec(memory_space=pltpu.VMEM)),
        input_output_aliases={t: ns + t for t in range(n)},
        compiler_params=pltpu.CompilerParams(has_side_effects=pltpu.SideEffectType.DATAFLOW_SIDE_EFFECTING),
    )(*[pltpu.with_memory_space_constraint(a, pltpu.HBM) for a in arrays])
    return dict(job=job, sems=list(res[:ns]), arrays=list(res[ns:ns + n]), token=res[-1])


def _split_wait(started, after, *, name):
    job, arrays, sems = started["job"], started["arrays"], started["sems"]
    n, ns = len(arrays), len(sems)

    def body(*refs):
        job.finish(list(refs[:n]), _semaphore_grids(job, refs[n:n + ns]))

    hbm = pl.BlockSpec(memory_space=pltpu.HBM)
    return pl.pallas_call(
        body, name=name, out_shape=[pltpu.HBM(a.shape, a.dtype) for a in arrays],
        in_specs=[hbm] * n + [pl.BlockSpec(memory_space=pltpu.SEMAPHORE)] * ns
        + [pl.BlockSpec(memory_space=pl.ANY)] * len(after),
        out_specs=[hbm] * n, input_output_aliases={t: t for t in range(n)},
        compiler_params=pltpu.CompilerParams(has_side_effects=pltpu.SideEffectType.DATAFLOW_SIDE_EFFECTING),
    )(*arrays, *sems, *after)


def _dot(a, b, ta=False, tb=False):
    return lax.dot_general(a, b, (((0 if ta else 1,), (1 if tb else 0,)), ((), ())),
                           preferred_element_type=F32)


def _tile(n, want):
    t = min(n, want) // 128 * 128
    while n % t:
        t -= 128
    return t


def _a_spec(ta, tm, tk):
    return pl.BlockSpec((tk, tm), lambda i, j, k: (k, i)) if ta else pl.BlockSpec((tm, tk), lambda i, j, k: (i, k))


def _b_spec(tb, tk, tn):
    return pl.BlockSpec((tn, tk), lambda i, j, k: (j, k)) if tb else pl.BlockSpec((tk, tn), lambda i, j, k: (k, j))


def _accumulate(accs, nk, products, finish):
    if nk == 1:
        finish(*products())
        return
    k = pl.program_id(2)

    @pl.when(k == 0)
    def _():
        for acc, p in zip(accs, products()):
            acc[...] = p

    if nk > 2:
        @pl.when(jnp.logical_and(k > 0, k < nk - 1))
        def _():
            for acc, p in zip(accs, products()):
                acc[...] += p

    @pl.when(k == nk - 1)
    def _():
        finish(*[acc[...] + p for acc, p in zip(accs, products())])


def _matmul(a, b, *, name, ta=False, tb=False, out_dtype=F32, residual=None, tm=1024, tn=1024, tk=2048, jobs=()):
    m, kdim = (a.shape[1], a.shape[0]) if ta else a.shape
    n = b.shape[0] if tb else b.shape[1]
    tm, tn, tk = _tile(m, tm), _tile(n, tn), _tile(kdim, tk)
    nk = kdim // tk

    def body(*refs):
        a_ref, b_ref = refs[:2]
        r_ref = refs[2] if residual is not None else None
        o_ref = refs[-1] if nk == 1 else refs[-2]

        def finish(total):
            if residual is not None:
                total = total + r_ref[...]
            o_ref[...] = total.astype(out_dtype)

        _accumulate(refs[-1:] if nk > 1 else (), nk, lambda: (_dot(a_ref[...], b_ref[...], ta, tb),), finish)

    o_spec = pl.BlockSpec((tm, tn), lambda i, j, k: (i, j))
    in_specs = [_a_spec(ta, tm, tk), _b_spec(tb, tk, tn)]
    operands = [a, b]
    if residual is not None:
        in_specs.append(o_spec)
        operands.append(residual)
    (out,), carried = _call(
        body, name=name, grid=(m // tm, n // tn, nk), in_specs=in_specs, out_specs=[o_spec],
        out_shape=[jax.ShapeDtypeStruct((m, n), out_dtype)], operands=operands,
        scratch_shapes=[pltpu.VMEM((tm, tn), F32)] * (nk > 1),
        semantics=("parallel", "parallel", "arbitrary"), jobs=jobs)
    return (out, carried) if jobs else out


def _matmul_pieces_nt(pieces, b, *, name, tm=512, tn=1024, jobs=(), after=()):
    m, kp = pieces[0].shape
    n = b.shape[0]
    tm, tn = _tile(m, tm), _tile(n, tn)
    count = len(pieces)

    def body(*refs):
        b_ref, o_ref = refs[count], refs[count + 1]
        total = _dot(refs[0][...], b_ref[:, pl.ds(0, kp)], tb=True)
        for p in range(1, count):
            total = total + _dot(refs[p][...], b_ref[:, pl.ds(p * kp, kp)], tb=True)
        o_ref[...] = total

    piece = pl.BlockSpec((tm, kp), lambda j, i: (i, 0))
    (out,), carried = _call(
        body, name=name, grid=(n // tn, m // tm),
        in_specs=[piece] * count + [pl.BlockSpec((tn, count * kp), lambda j, i: (j, 0))],
        out_specs=[pl.BlockSpec((tm, tn), lambda j, i: (i, j))],
        out_shape=[jax.ShapeDtypeStruct((m, n), F32)], operands=[*pieces, b],
        semantics=("parallel", "parallel"), jobs=jobs, after=after)
    return (out, carried) if jobs else out


def _weight_grad_pieces(a, pieces, *, name):
    tokens, m = a.shape
    np_ = pieces[0].shape[1]
    tm = 1024 if m % 1024 == 0 else _tile(m, 1408)
    tn = _tile(np_, 512)
    nb = np_ // tn
    out = None
    for first in range(0, len(pieces), WEIGHT_GRAD_GROUP):
        group = pieces[first:first + WEIGHT_GRAD_GROUP]

        def body(*refs, count=len(group)):
            t_now = pl.program_id(1) // nb
            for t in range(count):
                @pl.when(t_now == t)
                def _(t=t):
                    refs[-1][...] = _dot(refs[0][...], refs[1 + t][...], ta=True)

        def piece_spec(t):
            return pl.BlockSpec((tokens, tn), lambda i, j: (0, jnp.clip(j - t * nb, 0, nb - 1)))

        in_specs = [pl.BlockSpec((tokens, tm), lambda i, j: (0, i))] + [piece_spec(t) for t in range(len(group))]
        operands = [a, *group]
        if out is not None:
            in_specs.append(pl.BlockSpec(memory_space=pl.ANY))
            operands.append(out)
        out = pl.pallas_call(
            body, name="%s_%d" % (name, first), grid=(m // tm, nb * len(group)), in_specs=in_specs,
            out_specs=pl.BlockSpec((tm, tn), lambda i, j, first=first: (i, first * nb + j)),
            out_shape=jax.ShapeDtypeStruct((m, len(pieces) * np_), F32),
            input_output_aliases={len(operands) - 1: 0} if out is not None else {},
            compiler_params=_params(("parallel", "arbitrary")),
        )(*operands)
    return out


def _weight_grad(a, g, *, name, jobs=()):
    tokens, m = a.shape
    tm = 1024 if m % 1024 == 0 else _tile(m, 1408)
    return _matmul(a, g, name=name, ta=True, tm=tm, tn=512, tk=tokens, jobs=jobs)


def _swiglu_fwd(n2, w_gate, w_up, *, tm=1024, tn=512, tk=2048, jobs=()):
    m, kdim = n2.shape
    n = w_gate.shape[1]
    tm, tn, tk = _tile(m, tm), _tile(n, tn), _tile(kdim, tk)
    nk = kdim // tk

    def body(a_ref, g_ref, u_ref, gate_ref, up_ref, act_ref, *acc):
        def products():
            a = a_ref[...]
            return _dot(a, g_ref[...]), _dot(a, u_ref[...])

        def finish(g, u):
            gate_ref[...] = g.astype(BF16)
            up_ref[...] = u.astype(BF16)
            act_ref[...] = (g * _sigmoid(g) * u).astype(BF16)

        _accumulate(acc, nk, products, finish)

    o_spec = pl.BlockSpec((tm, tn), lambda i, j, k: (i, j))
    o_shape = jax.ShapeDtypeStruct((m, n), BF16)
    return _call(
        body, name="swiglu_fwd", grid=(m // tm, n // tn, nk),
        in_specs=[_a_spec(False, tm, tk), _b_spec(False, tk, tn), _b_spec(False, tk, tn)],
        out_specs=[o_spec] * 3, out_shape=[o_shape] * 3, operands=[n2, w_gate, w_up],
        scratch_shapes=[pltpu.VMEM((tm, tn), F32)] * (2 * (nk > 1)),
        semantics=("parallel", "parallel", "arbitrary"), jobs=jobs)


def _swiglu_bwd_act(dh2, w_down, gate, up, *, tm=1024, tn=512, tk=2048):
    m, kdim = dh2.shape
    n = w_down.shape[0]
    tm, tn, tk = _tile(m, tm), _tile(n, tn), _tile(kdim, tk)
    nk = kdim // tk

    sub = _tile(tn, 256)

    def body(a_ref, b_ref, gate_ref, up_ref, dgate_ref, dup_ref, *acc):
        def finish(dact, cols=slice(None)):
            g = gate_ref[:, cols].astype(F32)
            u = up_ref[:, cols].astype(F32)
            sg = _sigmoid(g)
            dup_ref[:, cols] = (dact * g * sg).astype(BF16)
            dgate_ref[:, cols] = (dact * u * sg * (1.0 + g * (1.0 - sg))).astype(BF16)

        if nk == 1:
            a = a_ref[...]
            for c in range(tn // sub):
                cols = pl.ds(c * sub, sub)
                finish(_dot(a, b_ref[cols, :], tb=True), cols)
        else:
            _accumulate(acc, nk, lambda: (_dot(a_ref[...], b_ref[...], tb=True),), finish)

    o_spec = pl.BlockSpec((tm, tn), lambda i, j, k: (i, j))
    o_shape = jax.ShapeDtypeStruct((m, n), BF16)
    return pl.pallas_call(
        body, name="swiglu_bwd_act", grid=(m // tm, n // tn, nk),
        in_specs=[_a_spec(False, tm, tk), _b_spec(True, tk, tn), o_spec, o_spec],
        out_specs=[o_spec] * 2, out_shape=[o_shape] * 2,
        scratch_shapes=[pltpu.VMEM((tm, tn), F32)] * (nk > 1),
        compiler_params=_params(("parallel", "parallel", "arbitrary")),
    )(dh2, w_down, gate, up)


def _swiglu_bwd_in(dgate, dup, w_gate, w_up, *, tm=1024, tn=1024, tk=1408, jobs=()):
    m, kdim = dgate.shape
    n = w_gate.shape[0]
    tm, tn, tk = _tile(m, tm), _tile(n, tn), _tile(kdim, tk)
    nk = kdim // tk

    def body(a1_ref, a2_ref, b1_ref, b2_ref, o_ref, *acc):
        def product():
            return (_dot(a1_ref[...], b1_ref[...], tb=True) + _dot(a2_ref[...], b2_ref[...], tb=True),)

        def finish(total):
            o_ref[...] = total

        _accumulate(acc, nk, product, finish)

    a_spec, b_spec = _a_spec(False, tm, tk), _b_spec(True, tk, tn)
    (out,), carried = _call(
        body, name="swiglu_bwd_in", grid=(m // tm, n // tn, nk),
        in_specs=[a_spec, a_spec, b_spec, b_spec],
        out_specs=[pl.BlockSpec((tm, tn), lambda i, j, k: (i, j))],
        out_shape=[jax.ShapeDtypeStruct((m, n), F32)], operands=[dgate, dup, w_gate, w_up],
        scratch_shapes=[pltpu.VMEM((tm, tn), F32)] * (nk > 1),
        semantics=("parallel", "parallel", "arbitrary"), jobs=jobs)
    return out, carried


def _row_block(rows, cols):
    tr = min(rows, max(16, ELEM_BLOCK_BYTES // (4 * cols) // 16 * 16))
    while rows % tr:
        tr -= 16
    return tr


def _rmsnorm_fwd(x, g, *, name, after=None):
    s, d = x.shape
    tr = _row_block(s, d)

    def body(x_ref, g_ref, *rest):
        xv = x_ref[...]
        r = lax.rsqrt(jnp.mean(xv * xv, axis=-1, keepdims=True) + EPS)
        rest[-1][...] = (xv * r * g_ref[...]).astype(BF16)

    row = pl.BlockSpec((tr, d), lambda i: (i, 0))
    in_specs = [row, pl.BlockSpec((1, d), lambda i: (0, 0))]
    operands = [x, g]
    if after is not None:
        in_specs.append(pl.BlockSpec(after.shape, lambda i: (0, 0)))
        operands.append(after)
    return pl.pallas_call(
        body, name=name, grid=(s // tr,), in_specs=in_specs,
        out_specs=row, out_shape=jax.ShapeDtypeStruct((s, d), BF16),
        compiler_params=_params(("parallel",)),
    )(*operands)


def _rmsnorm_bwd_rows(xv, gv, dy):
    r = lax.rsqrt(jnp.mean(xv * xv, axis=-1, keepdims=True) + EPS)
    xhat = xv * r
    dxh = dy * gv
    dx = r * (dxh - xhat * jnp.mean(dxh * xhat, axis=-1, keepdims=True))
    return dx, dy * xhat


def _rmsnorm_bwd(dn, x, g, skip, *, name, after=()):
    s, d = x.shape
    tr = _row_block(s, d)

    def body(dn_ref, x_ref, g_ref, skip_ref, *rest):
        dx_ref, dxb_ref, dg_ref = rest[len(after):]
        dx, dgr = _rmsnorm_bwd_rows(x_ref[...], g_ref[...], dn_ref[...])
        dx = dx + skip_ref[...]
        dx_ref[...] = dx
        dxb_ref[...] = dx.astype(BF16)

        @pl.when(pl.program_id(0) == 0)
        def _():
            dg_ref[...] = jnp.zeros_like(dg_ref)

        dg_ref[...] += jnp.sum(dgr, axis=0, keepdims=True)

    row = pl.BlockSpec((tr, d), lambda i: (i, 0))
    vec = pl.BlockSpec((1, d), lambda i: (0, 0))
    return pl.pallas_call(
        body, name=name, grid=(s // tr,),
        in_specs=[row, row, vec, row] + [pl.BlockSpec(memory_space=pl.ANY)] * len(after),
        out_specs=[row, row, vec],
        out_shape=[jax.ShapeDtypeStruct((s, d), F32), jax.ShapeDtypeStruct((s, d), BF16),
                   jax.ShapeDtypeStruct((1, d), F32)],
        compiler_params=_params(("arbitrary",)),
    )(dn, x, g, skip, *after)


def _loss_head(h2, g, target):
    s, d = h2.shape
    tr = _row_block(s, d)

    def body(h_ref, g_ref, t_ref, dh_ref, dhb_ref, dg_ref, loss_ref):
        hv = h_ref[...]
        gv = g_ref[...]
        r = lax.rsqrt(jnp.mean(hv * hv, axis=-1, keepdims=True) + EPS)
        err = hv * r * gv - t_ref[...]
        dx, dgr = _rmsnorm_bwd_rows(hv, gv, err * (1.0 / d))
        dh_ref[...] = dx
        dhb_ref[...] = dx.astype(BF16)

        @pl.when(pl.program_id(0) == 0)
        def _():
            dg_ref[...] = jnp.zeros_like(dg_ref)
            loss_ref[...] = jnp.zeros_like(loss_ref)

        dg_ref[...] += jnp.sum(dgr, axis=0, keepdims=True)
        row_loss = jnp.mean(err * err, axis=-1, keepdims=True)
        loss_ref[...] += 0.5 * jnp.sum(row_loss, axis=0, keepdims=True)

    row = pl.BlockSpec((tr, d), lambda i: (i, 0))
    vec = pl.BlockSpec((1, d), lambda i: (0, 0))
    one = pl.BlockSpec((1, 1), lambda i: (0, 0))
    return pl.pallas_call(
        body, name="loss_head", grid=(s // tr,), in_specs=[row, vec, row],
        out_specs=[row, row, vec, one],
        out_shape=[jax.ShapeDtypeStruct((s, d), F32), jax.ShapeDtypeStruct((s, d), BF16),
                   jax.ShapeDtypeStruct((1, d), F32), jax.ShapeDtypeStruct((1, 1), F32)],
        compiler_params=_params(("arbitrary",)),
    )(h2, g, target)


def _attention_bias_tables():
    k = np.arange(-ATT_KB, ATT_KB + 1)[:, None, None]
    delta = k * ATT_BLOCK + np.arange(ATT_BLOCK)[None, None, :] - np.arange(ATT_BLOCK)[None, :, None]
    dist = np.abs(delta)
    count = np.zeros(delta.shape, np.int32)
    for window, dilation in DILATED_PATTERNS:
        count += (delta % dilation == 0) & (dist <= min(window // 2, ATT_NEAR))
    logc = np.where(count > 0, np.log(np.maximum(count, 1)), MASKED)
    return dist.astype(np.float32), logc.astype(np.float32)


def _far_bias_tables(per_class):
    steps = np.abs(np.arange(per_class)[:, None] - np.arange(per_class)[None, :]) * ATT_CLASSES
    valid = (steps > ATT_NEAR) & (steps <= ATT_REACH)
    return steps.astype(np.float32), np.where(valid, 0.0, MASKED).astype(np.float32)


def _to_classes(x):
    s, cols = x.shape
    return jnp.reshape(jnp.transpose(jnp.reshape(x, (s // ATT_CLASSES, ATT_CLASSES, cols)), (1, 0, 2)), (s, cols))


def _from_classes(x):
    s, cols = x.shape
    return jnp.reshape(jnp.transpose(jnp.reshape(x, (ATT_CLASSES, s // ATT_CLASSES, cols)), (1, 0, 2)), (s, cols))


def _head_bias(bias_ref, slope, dist_ref, logc_ref):
    for kk in range(ATT_WINDOW):
        bias_ref[kk] = logc_ref[kk] - slope * dist_ref[kk]
    bias_ref[ATT_WINDOW] = jnp.full((ATT_BLOCK, ATT_BLOCK), MASKED, F32)


def _window_start(i, nq, nwin):
    return jnp.clip(i - ATT_KB, 0, nq - nwin)


def _window_block(j, i):
    rows = pl.ds(pl.multiple_of(j * ATT_BLOCK, ATT_BLOCK), ATT_BLOCK)
    kk = j - i + ATT_KB
    return rows, jnp.where(jnp.logical_and(kk >= 0, kk < ATT_WINDOW), kk, ATT_WINDOW)


def _attention_far_fwd(qkv, slopes, n_heads, jobs=(), after=()):
    s = qkv.shape[0]
    per_class = s // ATT_CLASSES
    scale = HEAD_DIM ** -0.5
    dist, logc = _far_bias_tables(per_class)

    def body(slope_ref, q_ref, k_ref, v_ref, dist_ref, logc_ref, o_ref, lse_ref):
        bias = logc_ref[...] - slope_ref[pl.program_id(0)] * dist_ref[...]
        for a in range(ATT_FAR_GROUP):
            rows = pl.ds(a * per_class, per_class)
            sc = _dot(q_ref[rows, :], k_ref[rows, :], tb=True) * scale + bias
            m = jnp.maximum(jnp.max(sc, axis=-1, keepdims=True), ROW_MAX_INIT)
            p = jnp.exp(sc - m)
            l = jnp.maximum(jnp.sum(p, axis=-1, keepdims=True), 1e-30)
            o_ref[rows, :] = (_dot(p.astype(BF16), v_ref[rows, :]) / l).astype(BF16)
            lse_ref[rows, :] = jnp.broadcast_to(m + jnp.log(l), (per_class, HEAD_DIM))

    hh = n_heads
    blk = pl.BlockSpec((ATT_FAR_GROUP * per_class, HEAD_DIM), lambda h, r: (r, h))
    table = pl.BlockSpec(dist.shape, lambda h, r: (0, 0))
    return _call(
        body, name="attention_far_fwd", grid=(hh, ATT_CLASSES // ATT_FAR_GROUP),
        in_specs=[pl.BlockSpec(memory_space=pltpu.SMEM), blk,
                  pl.BlockSpec((ATT_FAR_GROUP * per_class, HEAD_DIM), lambda h, r: (r, hh + h)),
                  pl.BlockSpec((ATT_FAR_GROUP * per_class, HEAD_DIM), lambda h, r: (r, 2 * hh + h)), table, table],
        out_specs=[blk, blk],
        out_shape=[jax.ShapeDtypeStruct((s, hh * HEAD_DIM), BF16), jax.ShapeDtypeStruct((s, hh * HEAD_DIM), F32)],
        operands=[slopes, qkv, qkv, qkv, jnp.asarray(dist), jnp.asarray(logc)],
        semantics=("parallel", "parallel"), jobs=jobs, after=after)


def _attention_fwd(proj, slopes, far_out, far_lse, n_heads, jobs=(), after=()):
    s = proj.shape[0]
    nq = s // ATT_BLOCK
    scale = HEAD_DIM ** -0.5
    dist, logc = _attention_bias_tables()

    nwin = min(ATT_WINDOW, nq)

    group = math.gcd(ATT_NEAR_GROUP, nq)

    def body(slope_ref, q_ref, k_ref, v_ref, fo_ref, fl_ref, dist_ref, logc_ref, o_ref, lse_ref, bias_ref, s_ref):
        h, step = pl.program_id(0), pl.program_id(1)

        @pl.when(step == 0)
        def _():
            _head_bias(bias_ref, slope_ref[h], dist_ref, logc_ref)

        for a in range(group):
            i = step * group + a
            mine = pl.ds(a * ATT_BLOCK, ATT_BLOCK)
            q = q_ref[mine, :]
            first = _window_start(i, nq, nwin)
            m = jnp.full((ATT_BLOCK, 1), ROW_MAX_INIT, F32)
            for b in range(nwin):
                rows, kk = _window_block(first + b, i)
                sc = _dot(q, k_ref[rows, :], tb=True) * scale + bias_ref[kk]
                s_ref[a * nwin + b] = sc
                m = jnp.maximum(m, jnp.max(sc, axis=-1, keepdims=True))
            l = jnp.zeros((ATT_BLOCK, 1), F32)
            acc = jnp.zeros((ATT_BLOCK, HEAD_DIM), F32)
            for b in range(nwin):
                rows, _ = _window_block(first + b, i)
                p = jnp.exp(s_ref[a * nwin + b] - m)
                l = l + jnp.sum(p, axis=-1, keepdims=True)
                acc = acc + _dot(p.astype(BF16), v_ref[rows, :])
            near_lse = m + jnp.log(l)
            far_lse_col = fl_ref[mine, :1]
            lse = jnp.maximum(near_lse, far_lse_col)
            lse = lse + jnp.log(jnp.exp(near_lse - lse) + jnp.exp(far_lse_col - lse))
            o_ref[mine, :] = (acc * (jnp.exp(near_lse - lse) / l)
                              + fo_ref[mine, :].astype(F32) * jnp.exp(far_lse_col - lse)).astype(BF16)
            lse_ref[mine, :] = jnp.broadcast_to(lse, (ATT_BLOCK, HEAD_DIM))

    hh = n_heads
    blk = pl.BlockSpec((group * ATT_BLOCK, HEAD_DIM), lambda h, i: (i, h))
    table = pl.BlockSpec(dist.shape, lambda h, i: (0, 0, 0))
    return _call(
        body, name="attention_fwd", grid=(hh, nq // group),
        in_specs=[pl.BlockSpec(memory_space=pltpu.SMEM), blk,
                  pl.BlockSpec((s, HEAD_DIM), lambda h, i: (0, hh + h)),
                  pl.BlockSpec((s, HEAD_DIM), lambda h, i: (0, 2 * hh + h)), blk, blk, table, table],
        out_specs=[blk, blk],
        out_shape=[jax.ShapeDtypeStruct((s, hh * HEAD_DIM), BF16), jax.ShapeDtypeStruct((s, hh * HEAD_DIM), F32)],
        operands=[slopes, proj, proj, proj, far_out, far_lse, jnp.asarray(dist), jnp.asarray(logc)],
        scratch_shapes=[pltpu.VMEM((ATT_WINDOW + 1, ATT_BLOCK, ATT_BLOCK), F32),
                        pltpu.VMEM((group * nwin, ATT_BLOCK, ATT_BLOCK), F32)],
        semantics=("parallel", "arbitrary"), jobs=jobs, after=after)


def _attention_far_bwd(qkv, slopes, out, dout, lse, n_heads):
    s = qkv.shape[0]
    per_class = s // ATT_CLASSES
    scale = HEAD_DIM ** -0.5
    dist, logc = _far_bias_tables(per_class)

    def body(slope_ref, q_ref, k_ref, v_ref, o_ref, do_ref, lse_ref, dist_ref, logc_ref, dq_ref, dk_ref, dv_ref):
        bias = logc_ref[...] - slope_ref[pl.program_id(0)] * dist_ref[...]
        for a in range(ATT_FAR_GROUP):
            rows = pl.ds(a * per_class, per_class)
            q, k, do = q_ref[rows, :], k_ref[rows, :], do_ref[rows, :]
            delta = jnp.sum(do.astype(F32) * o_ref[rows, :].astype(F32), axis=-1, keepdims=True)
            p = jnp.exp(_dot(q, k, tb=True) * scale + bias - lse_ref[rows, :1])
            dv_ref[rows, :] = _dot(p.astype(BF16), do, ta=True).astype(BF16)
            ds = (p * (_dot(do, v_ref[rows, :], tb=True) - delta) * scale).astype(BF16)
            dk_ref[rows, :] = _dot(ds, q, ta=True).astype(BF16)
            dq_ref[rows, :] = _dot(ds, k).astype(BF16)

    hh = n_heads
    blk = pl.BlockSpec((ATT_FAR_GROUP * per_class, HEAD_DIM), lambda h, r: (r, h))
    table = pl.BlockSpec(dist.shape, lambda h, r: (0, 0))
    o_shape = jax.ShapeDtypeStruct((s, hh * HEAD_DIM), BF16)
    return pl.pallas_call(
        body, name="attention_far_bwd", grid=(hh, ATT_CLASSES // ATT_FAR_GROUP),
        in_specs=[pl.BlockSpec(memory_space=pltpu.SMEM), blk,
                  pl.BlockSpec((ATT_FAR_GROUP * per_class, HEAD_DIM), lambda h, r: (r, hh + h)),
                  pl.BlockSpec((ATT_FAR_GROUP * per_class, HEAD_DIM), lambda h, r: (r, 2 * hh + h)),
                  blk, blk, blk, table, table],
        out_specs=[blk] * 3, out_shape=[o_shape] * 3,
        compiler_params=_params(("parallel", "parallel")),
    )(slopes, qkv, qkv, qkv, out, dout, lse, jnp.asarray(dist), jnp.asarray(logc))


def _attention_bwd(proj, slopes, out, lse, dmixed, far_grads, n_heads, jobs=()):
    s = proj.shape[0]
    nq = s // ATT_BLOCK
    scale = HEAD_DIM ** -0.5
    dist, logc = _attention_bias_tables()

    nwin = min(ATT_WINDOW, nq)
    group = math.gcd(ATT_NEAR_GROUP, nq)

    def body(slope_ref, q_ref, k_ref, v_ref, o_ref, do_ref, lse_ref, fdq_ref, fdk_ref, fdv_ref, dist_ref, logc_ref,
             dq_ref, dk_ref, dv_ref, dk_acc, dv_acc, bias_ref):
        h, step = pl.program_id(0), pl.program_id(1)

        @pl.when(step == 0)
        def _():
            dk_acc[...] = jnp.zeros_like(dk_acc)
            dv_acc[...] = jnp.zeros_like(dv_acc)
            _head_bias(bias_ref, slope_ref[h], dist_ref, logc_ref)

        for a in range(group):
            i = step * group + a
            mine = pl.ds(a * ATT_BLOCK, ATT_BLOCK)
            q = q_ref[mine, :]
            do = do_ref[mine, :]
            lse_col = lse_ref[mine, :1]
            delta = jnp.sum(do.astype(F32) * o_ref[mine, :].astype(F32), axis=-1, keepdims=True)
            first = _window_start(i, nq, nwin)
            dq = jnp.zeros((ATT_BLOCK, HEAD_DIM), F32)
            for b in range(nwin):
                rows, kk = _window_block(first + b, i)
                kj = k_ref[rows, :]
                vj = v_ref[rows, :]
                p = jnp.exp(_dot(q, kj, tb=True) * scale + bias_ref[kk] - lse_col)
                dv_acc[rows, :] += _dot(p.astype(BF16), do, ta=True)
                dp = _dot(do, vj, tb=True)
                ds = (p * (dp - delta) * scale).astype(BF16)
                dk_acc[rows, :] += _dot(ds, q, ta=True)
                dq = dq + _dot(ds, kj)
            dq_ref[mine, :] = (dq + fdq_ref[mine, :].astype(F32)).astype(BF16)

        @pl.when(step == nq // group - 1)
        def _():
            dk_ref[...] = (dk_acc[...] + fdk_ref[...].astype(F32)).astype(BF16)
            dv_ref[...] = (dv_acc[...] + fdv_ref[...].astype(F32)).astype(BF16)

    hh = n_heads
    blk = pl.BlockSpec((group * ATT_BLOCK, HEAD_DIM), lambda h, i: (i, h))
    col = pl.BlockSpec((s, HEAD_DIM), lambda h, i: (0, h))
    table = pl.BlockSpec(dist.shape, lambda h, i: (0, 0, 0))
    o_shape = jax.ShapeDtypeStruct((s, hh * HEAD_DIM), BF16)
    return _call(
        body, name="attention_bwd", grid=(hh, nq // group),
        in_specs=[pl.BlockSpec(memory_space=pltpu.SMEM), blk,
                  pl.BlockSpec((s, HEAD_DIM), lambda h, i: (0, hh + h)),
                  pl.BlockSpec((s, HEAD_DIM), lambda h, i: (0, 2 * hh + h)),
                  blk, blk, blk, blk, col, col, table, table],
        out_specs=[blk, col, col], out_shape=[o_shape] * 3,
        operands=[slopes, proj, proj, proj, out, dmixed, lse, *far_grads, jnp.asarray(dist), jnp.asarray(logc)],
        scratch_shapes=[pltpu.VMEM((s, HEAD_DIM), F32)] * 2
        + [pltpu.VMEM((ATT_WINDOW + 1, ATT_BLOCK, ATT_BLOCK), F32)],
        semantics=("parallel", "arbitrary"), jobs=jobs)


def _ret_decays(lgc, lga, strict_c, strict_a):
    c = RET_CHUNK
    rel = (lax.broadcasted_iota(jnp.int32, (c, c), 0) - lax.broadcasted_iota(jnp.int32, (c, c), 1)).astype(F32)
    in_c = (rel > 0) if strict_c else (rel >= 0)
    in_a = (rel < 0) if strict_a else (rel <= 0)
    mask = (jnp.where(in_c, jnp.exp(lgc * jnp.maximum(rel, 0.0)), 0.0)
            + jnp.where(in_a, jnp.exp(lga * jnp.maximum(-rel, 0.0)), 0.0))
    idx = lax.broadcasted_iota(jnp.int32, (c, 1), 0).astype(F32)
    ones = jnp.ones((1, HEAD_DIM), F32)
    dec = dict(
        rel=rel, mask=mask, idx=idx,
        a_c=jnp.exp(lgc * (idx + 1.0)), b_c=jnp.exp(lgc * (c - 1.0 - idx)), chunk_c=jnp.exp(ones * (lgc * c)),
        a_a=jnp.exp(lga * (c - idx)), b_a=jnp.exp(lga * idx), chunk_a=jnp.exp(ones * (lga * c)),
    )
    return dec


def _scaled(x, col):
    return (x.astype(F32) * col).astype(BF16)


def _chunk_rows(i):
    return pl.ds(pl.multiple_of(i * RET_CHUNK, RET_CHUNK), RET_CHUNK)


def _chunk_loop(nc, step, init, unroll=RET_UNROLL):
    group = math.gcd(nc, unroll)

    def trip(t, carry):
        for u in range(group):
            carry = step(t * group + u, carry)
        return carry

    return lax.fori_loop(0, nc // group, trip, init)


def _retention(a, b, c, lg_c, lg_a, *, strict_c, strict_a, scale, n_heads, name, gate=None, norm_w=None, jobs=(),
               heads=None, so_far=None, after=()):
    s = a[0].shape[0]
    nc = s // RET_CHUNK
    epilogue = gate is not None
    first_head, head_count = heads if heads is not None else (0, n_heads)

    def body(*refs):
        lgc_ref, lga_ref, a_ref, b_ref, c_ref = refs[:5]
        if epilogue:
            g_ref, w_ref = refs[5:7]
            o_ref, mix_ref, sa_ref = refs[-3:]
        else:
            o_ref, sa_ref = refs[-2:]
        h = first_head + pl.program_id(0)
        dec = _ret_decays(lgc_ref[h], lga_ref[h], strict_c, strict_a)

        def reverse(t, state):
            i = nc - 1 - t
            sa_ref[i] = state.astype(BF16)
            rows = _chunk_rows(i)
            return state * dec["chunk_a"] + _dot(_scaled(b_ref[rows, :], dec["b_a"]), c_ref[rows, :], ta=True)

        _chunk_loop(nc, reverse, jnp.zeros((HEAD_DIM, HEAD_DIM), F32))

        def forward(i, state):
            rows = _chunk_rows(i)
            ai, bi, ci = a_ref[rows, :], b_ref[rows, :], c_ref[rows, :]
            inner = (_dot(ai, bi, tb=True) * dec["mask"]).astype(BF16)
            out = (_dot(inner, ci) + _dot(_scaled(ai, dec["a_c"]), state.astype(BF16))
                   + _dot(_scaled(ai, dec["a_a"]), sa_ref[i])) * scale
            o_ref[rows, :] = out.astype(BF16)
            if epilogue:
                r = lax.rsqrt(jnp.mean(out * out, axis=-1, keepdims=True) + EPS)
                g = g_ref[rows, :].astype(F32)
                mix_ref[rows, :] = (out * r * w_ref[...] * (g * _sigmoid(g))).astype(BF16)
            return state * dec["chunk_c"] + _dot(_scaled(bi, dec["b_c"]), ci, ta=True)

        _chunk_loop(nc, forward, jnp.zeros((HEAD_DIM, HEAD_DIM), F32))

    def col(first):
        return pl.BlockSpec((s, HEAD_DIM), lambda h: (0, first + first_head + h))

    smem = pl.BlockSpec(memory_space=pltpu.SMEM)
    in_specs = [smem, smem, col(a[1]), col(b[1]), col(c[1])]
    operands = [lg_c, lg_a, a[0], b[0], c[0]]
    o_shape = jax.ShapeDtypeStruct((s, n_heads * HEAD_DIM), BF16)
    out_specs, out_shape = [col(0)], [o_shape]
    if epilogue:
        in_specs += [col(gate[1]), pl.BlockSpec((1, HEAD_DIM), lambda h: (0, first_head + h))]
        operands += [gate[0], norm_w]
        out_specs, out_shape = [col(0)] * 2, [o_shape] * 2
    updates = None
    if so_far is not None:
        updates = {len(operands) + t: t for t in range(len(so_far))}
        in_specs += [pl.BlockSpec(memory_space=pl.ANY)] * len(so_far)
        operands += list(so_far)
    res, carried = _call(
        body, name=name, grid=(head_count,), in_specs=in_specs, out_specs=out_specs, out_shape=out_shape,
        operands=operands, scratch_shapes=[pltpu.VMEM((nc, HEAD_DIM, HEAD_DIM), BF16)],
        semantics=("parallel",), jobs=jobs, updates=updates, after=after)
    res = res if epilogue else res[0]
    return (res, carried) if jobs else res


def _retention_decay_grads(a, b, c, e, lg_c, lg_a, *, scale, n_heads):
    s = a[0].shape[0]
    nc = s // RET_CHUNK
    cf = float(RET_CHUNK)

    def body(lgc_ref, lga_ref, a_ref, b_ref, c_ref, e_ref, gc_ref, ga_ref, sa_ref, ta_ref):
        h = pl.program_id(0)
        lgc, lga = lgc_ref[h], lga_ref[h]
        dec = _ret_decays(lgc, lga, True, True)
        rel, idx = dec["rel"], dec["idx"]
        w_c = jnp.where(rel > 0, rel * jnp.exp(lgc * jnp.maximum(rel, 0.0)), 0.0)
        w_a = jnp.where(rel < 0, -rel * jnp.exp(lga * jnp.maximum(-rel, 0.0)), 0.0)
        zero = jnp.zeros((HEAD_DIM, HEAD_DIM), F32)

        def reverse(t, carry):
            st, dst = carry
            i = nc - 1 - t
            sa_ref[i] = st.astype(BF16)
            ta_ref[i] = dst.astype(BF16)
            rows = _chunk_rows(i)
            bi, ci = b_ref[rows, :], c_ref[rows, :]
            st_new = st * dec["chunk_a"] + _dot(_scaled(bi, dec["b_a"]), ci, ta=True)
            dst_new = (cf * st + dst) * dec["chunk_a"] + _dot(_scaled(bi, idx * dec["b_a"]), ci, ta=True)
            return st_new, dst_new

        _chunk_loop(nc, reverse, (zero, zero))

        def forward(i, carry):
            st, dst, acc_c, acc_a = carry
            rows = _chunk_rows(i)
            ai, bi, ci = a_ref[rows, :], b_ref[rows, :], c_ref[rows, :]
            ev = e_ref[rows, :].astype(F32)
            pg = _dot(ai, bi, tb=True) * _dot(e_ref[rows, :], ci, tb=True)
            a_c, a_a = _scaled(ai, dec["a_c"]), _scaled(ai, dec["a_a"])
            inter_c = _dot(a_c, st.astype(BF16)) * (idx + 1.0) + _dot(a_c, dst.astype(BF16))
            inter_a = _dot(a_a, sa_ref[i]) * (cf - idx) + _dot(a_a, ta_ref[i])
            acc_c = acc_c + jnp.sum(pg * w_c, axis=0, keepdims=True) + jnp.sum(inter_c * ev, axis=0, keepdims=True)
            acc_a = acc_a + jnp.sum(pg * w_a, axis=0, keepdims=True) + jnp.sum(inter_a * ev, axis=0, keepdims=True)
            st_new = st * dec["chunk_c"] + _dot(_scaled(bi, dec["b_c"]), ci, ta=True)
            dst_new = ((cf * st + dst) * dec["chunk_c"]
                       + _dot(_scaled(bi, (cf - 1.0 - idx) * dec["b_c"]), ci, ta=True))
            return st_new, dst_new, acc_c, acc_a

        row = jnp.zeros((1, HEAD_DIM), F32)
        _, _, acc_c, acc_a = _chunk_loop(nc, forward, (zero, zero, row, row))
        gc_ref[...] = jnp.broadcast_to(jnp.sum(acc_c, axis=-1, keepdims=True) * scale, gc_ref.shape)
        ga_ref[...] = jnp.broadcast_to(jnp.sum(acc_a, axis=-1, keepdims=True) * scale, ga_ref.shape)

    def col(first):
        return pl.BlockSpec((s, HEAD_DIM), lambda h: (0, first + h))

    smem = pl.BlockSpec(memory_space=pltpu.SMEM)
    o_spec = pl.BlockSpec((1, 8, HEAD_DIM), lambda h: (h, 0, 0))
    o_shape = jax.ShapeDtypeStruct((n_heads, 8, HEAD_DIM), F32)
    gc, ga = pl.pallas_call(
        body, name="retention_decay_grads", grid=(n_heads,),
        in_specs=[smem, smem, col(a[1]), col(b[1]), col(c[1]), col(e[1])],
        out_specs=[o_spec] * 2, out_shape=[o_shape] * 2,
        scratch_shapes=[pltpu.VMEM((nc, HEAD_DIM, HEAD_DIM), BF16)] * 2,
        compiler_params=_params(("parallel",)),
    )(lg_c, lg_a, a[0], b[0], c[0], e[0])
    return gc[:, 0, 0], ga[:, 0, 0]


def _ret_gate_bwd(dmixed, first_col, out, proj, gate_col, norm_w, n_heads):
    s = out.shape[0]
    tr = _row_block(s, 8 * HEAD_DIM)

    def body(dm_ref, o_ref, g_ref, w_ref, do_ref, dg_ref, dw_ref):
        dm = dm_ref[...].astype(F32)
        ov = o_ref[...].astype(F32)
        g = g_ref[...].astype(F32)
        w = w_ref[...]
        r = lax.rsqrt(jnp.mean(ov * ov, axis=-1, keepdims=True) + EPS)
        ohat = ov * r
        sg = _sigmoid(g)
        silu = g * sg
        dg_ref[...] = (dm * ohat * w * sg * (1.0 + g * (1.0 - sg))).astype(BF16)
        dohat = dm * w * silu
        do_ref[...] = (r * (dohat - ohat * jnp.mean(dohat * ohat, axis=-1, keepdims=True))).astype(BF16)

        @pl.when(pl.program_id(1) == 0)
        def _():
            dw_ref[...] = jnp.zeros_like(dw_ref)

        dw_ref[...] += jnp.sum(dm * ohat * silu, axis=0, keepdims=True)

    def blk(first):
        return pl.BlockSpec((tr, HEAD_DIM), lambda h, i: (i, first + h))

    vec = pl.BlockSpec((1, HEAD_DIM), lambda h, i: (0, h))
    o_shape = jax.ShapeDtypeStruct((s, n_heads * HEAD_DIM), BF16)
    return pl.pallas_call(
        body, name="ret_gate_bwd", grid=(n_heads, s // tr),
        in_specs=[blk(first_col), blk(0), blk(gate_col), vec],
        out_specs=[blk(0), blk(0), vec],
        out_shape=[o_shape, o_shape, jax.ShapeDtypeStruct((1, n_heads * HEAD_DIM), F32)],
        compiler_params=_params(("parallel", "arbitrary")),
    )(dmixed, out, proj, norm_w)


def _step(x, target, norm_mix_w, ret_decay_fwd, ret_decay_bwd, ret_norm_w, norm_ffn_w, norm_final_w, own,
          w_in_started, queued, shard_ids, pos):
    d = x.shape[1]
    nh = d // (2 * HEAD_DIM)
    scale = HEAD_DIM ** -0.5
    slopes = jnp.exp2(-8.0 * jnp.arange(1, nh + 1, dtype=F32) / nh)
    lg_f = -jnp.exp(ret_decay_fwd)
    lg_b = -jnp.exp(ret_decay_bwd)
    q_r, k_r, v_r, g_r = 3 * nh, 4 * nh, 5 * nh, 6 * nh
    ax = BIG_AXIS

    def gather(names, arrays, stage, part=None, peers=(0, 1, 2)):
        return _gather_job(arrays, [ax[k] for k in names], stage, part, peers)

    def add_halves(k, g, received):
        return _add_halves(g, received, ax[k], pos, name="grad_add_halves_" + k)

    def sum_parts(k, g, received, parts):
        return _sum_chip_parts(g, received, parts, ax[k], pos, name="grad_sum_parts_" + k)

    sems, w_in, token = w_in_started
    n1 = _rmsnorm_fwd(x, norm_mix_w, name="norm_mix_fwd", after=token)
    proj = _in_proj_part(n1, w_in, None, shard_ids, 0, out_cols=w_in.shape[1])
    for peer in range(3):
        behind = [proj] + ([queued["w_down"]["token"]] if peer == 0 else [])
        w_in = _split_gather_wait(sems, w_in, ax["w_in"], peer, behind)
        (w_in,) = _run_jobs([gather(["w_in"], [w_in], "d2d", peers=(peer,))], name="all_gather_w_in_sibling_%d" % peer)
        proj = _in_proj_part(n1, w_in, proj, shard_ids, 1 + peer, out_cols=w_in.shape[1])
    (w_gate,) = _split_wait(queued["w_gate"], [proj], name="all_gather_w_gate_wait")
    qkv_classes = _to_classes(proj[:, :3 * nh * HEAD_DIM])
    (ret, ret_mixed), [[w_gate]] = _retention(
        (proj, q_r), (proj, k_r), (proj, v_r), lg_f, lg_b, strict_c=False, strict_a=True, scale=scale, n_heads=nh,
        name="retention_fwd_first", gate=(proj, g_r), norm_w=ret_norm_w, heads=(0, nh // 2),
        jobs=[gather(["w_gate"], [w_gate], "d2d")])
    (far_out, far_lse), _ = _attention_far_fwd(qkv_classes, slopes, nh, after=[ret_mixed])
    ret, ret_mixed = _retention(
        (proj, q_r), (proj, k_r), (proj, v_r), lg_f, lg_b, strict_c=False, strict_a=True, scale=scale, n_heads=nh,
        name="retention_fwd_second", gate=(proj, g_r), norm_w=ret_norm_w, heads=(nh // 2, nh - nh // 2),
        so_far=[ret, ret_mixed], after=[far_out])
    (w_out,) = _split_wait(queued["w_out"], [ret_mixed], name="all_gather_w_out_wait")
    (attn, lse), [[w_out]] = _attention_fwd(
        proj, slopes, _from_classes(far_out), _from_classes(far_lse), nh, after=[ret_mixed],
        jobs=[gather(["w_out"], [w_out], "d2d")])
    mixed = jnp.concatenate([attn, ret_mixed], axis=1)
    (w_up,) = _split_wait(queued["w_up"], [mixed], name="all_gather_w_up_wait")
    h1, [[w_up]] = _matmul(mixed, w_out, name="out_proj", residual=x, jobs=[gather(["w_up"], [w_up], "d2d")])
    n2 = _rmsnorm_fwd(h1, norm_ffn_w, name="norm_ffn_fwd")
    (w_down,) = _split_wait(queued["w_down"], [n2], name="all_gather_w_down_wait")
    (gate, up, act), [[w_down]] = _swiglu_fwd(n2, w_gate, w_up, jobs=[gather(["w_down"], [w_down], "d2d")])
    h2 = _matmul(act, w_down, name="down_proj", residual=h1, tk=2816)
    dh2, dh2_b, d_norm_final, loss = _loss_head(h2, norm_final_w, target)

    dgate, dup = _swiglu_bwd_act(dh2_b, w_down, gate, up)
    g_down = _weight_grad(act, dh2_b, name="grad_w_down")
    g_gate, [[r_down]] = _weight_grad(n2, dgate, name="grad_w_gate", jobs=[_exchange_job([g_down], [ax["w_down"]])])
    s_down = add_halves("w_down", g_down, r_down)
    g_up, [[r_gate], [p_down]] = _weight_grad(
        n2, dup, name="grad_w_up",
        jobs=[_exchange_job([g_gate], [ax["w_gate"]]), _send_sums_job([s_down], [ax["w_down"]], (0, 1, 2))])
    s_gate = add_halves("w_gate", g_gate, r_gate)
    dn2, [[r_up], [p_gate], [p_down]] = _swiglu_bwd_in(
        dgate, dup, w_gate, w_up,
        jobs=[_exchange_job([g_up], [ax["w_up"]]), _send_sums_job([s_gate], [ax["w_gate"]]),
              _send_sums_job([s_down], [ax["w_down"]], (1, 1, 2), landing=[p_down])])
    h_down = sum_parts("w_down", g_down, r_down, p_down)
    s_up = add_halves("w_up", g_up, r_up)
    h_gate = sum_parts("w_gate", g_gate, r_gate, p_gate)
    dh1, dh1_b, d_norm_ffn = _rmsnorm_bwd(dn2, h1, norm_ffn_w, dh2, name="norm_ffn_bwd")

    dmixed, [[gr_down], [p_up]] = _matmul(
        dh1_b, w_out, name="out_proj_bwd", tb=True, out_dtype=BF16,
        jobs=[_join_job([h_down], [ax["w_down"]]), _send_sums_job([s_up], [ax["w_up"]], (0, 1, 4))])
    far_in = [_to_classes(t) for t in (attn, dmixed[:, :nh * HEAD_DIM], lse)]
    g_out, [[p_up]] = _weight_grad(mixed, dh1_b, name="grad_w_out",
                                   jobs=[_send_sums_job([s_up], [ax["w_up"]], (1, 1, 4), landing=[p_up])])
    d_ret, dg_r, d_ret_norm = _ret_gate_bwd(dmixed, nh, ret, proj, g_r, ret_norm_w, nh)
    far_grads = _attention_far_bwd(qkv_classes, slopes, *far_in, nh)
    far_grads = [_from_classes(t) for t in far_grads]
    dq_r, [[gr_gate], [p_up]] = _retention(
        (d_ret, 0), (proj, v_r), (proj, k_r), lg_f, lg_b, strict_c=False, strict_a=True, scale=scale, n_heads=nh,
        name="retention_dq",
        jobs=[_join_job([h_gate], [ax["w_gate"]]), _send_sums_job([s_up], [ax["w_up"]], (2, 1, 4), landing=[p_up])])
    (dq_a, dk_a, dv_a), [[p_up], [r_out]] = _attention_bwd(
        proj, slopes, attn, lse, dmixed, far_grads, nh,
        jobs=[_send_sums_job([s_up], [ax["w_up"]], (3, 1, 4), landing=[p_up]),
              _exchange_job([g_out], [ax["w_out"]])])
    s_out = add_halves("w_out", g_out, r_out)
    h_up = sum_parts("w_up", g_up, r_up, p_up)
    dv_r, [[p_out], [gr_up]] = _retention(
        (proj, k_r), (proj, q_r), (d_ret, 0), lg_b, lg_f, strict_c=True, strict_a=False, scale=scale, n_heads=nh,
        name="retention_dv", jobs=[_send_sums_job([s_out], [ax["w_out"]]), _join_job([h_up], [ax["w_up"]])])
    h_out = sum_parts("w_out", g_out, r_out, p_out)
    dk_r, [[gr_out]] = _retention(
        (proj, v_r), (d_ret, 0), (proj, q_r), lg_b, lg_f, strict_c=True, strict_a=False, scale=scale, n_heads=nh,
        name="retention_dk", jobs=[_join_job([h_out], [ax["w_out"]])])
    dlg_f, dlg_b = _retention_decay_grads((proj, q_r), (proj, k_r), (proj, v_r), (d_ret, 0), lg_f, lg_b,
                                          scale=scale, n_heads=nh)
    dproj = [dq_a, dk_a, dv_a, dq_r, dk_r, dv_r, dg_r]
    g_in = _weight_grad_pieces(n1, dproj, name="grad_w_in")
    exchange = _split_start(_exchange_job([g_in], [ax["w_in"]]), name="grad_exchange_w_in_start")
    dn1 = _matmul_pieces_nt(dproj, w_in, name="in_proj_bwd", after=[exchange["token"]])
    g_in, r_in = _split_wait(exchange, [dn1], name="grad_exchange_w_in_wait")
    s_in = add_halves("w_in", g_in, r_in)
    sending = _split_start(_send_sums_job([s_in], [ax["w_in"]]), name="grad_send_w_in_start")
    dx, _, d_norm_mix = _rmsnorm_bwd(dn1, x, norm_mix_w, dh1, name="norm_mix_bwd", after=[sending["token"]])

    small = dict(loss=loss[0, 0], norm_mix_w=d_norm_mix, ret_decay_fwd=dlg_f * lg_f, ret_decay_bwd=dlg_b * lg_b,
                 ret_norm_w=d_ret_norm, norm_ffn_w=d_norm_ffn, norm_final_w=d_norm_final)
    return (dx, dict(w_out=gr_out, w_gate=gr_gate, w_up=gr_up, w_down=gr_down), small,
            dict(sending=sending, grad=g_in, received=r_in))


def _mesh_position():
    x, y, c = lax.axis_index("x"), lax.axis_index("y"), lax.axis_index("c")
    chips = [(1 - x, y), (x, 1 - y), (1 - x, 1 - y)]
    return x, y, c, chips


def _span(span):
    if span is None:
        return slice(None)
    start, size, step = span
    return pl.ds(start if isinstance(start, int) else pl.multiple_of(start, step), size)


def _part_rows(part, rows):
    first, count, of = part
    return first * (rows // of), count * (rows // of), rows // of


def _region(ref, axis, shard, half, shard_size, half_size, part=None, total_rows=None):
    along = None if shard is None else (shard * shard_size, shard_size, shard_size)
    other = None if half is None else (half * half_size, half_size, half_size)
    rows, cols = (other, along) if axis == 1 else (along, other)
    if part is not None:
        start, size, _ = rows if rows is not None else (0, total_rows, None)
        offset, size, step = _part_rows(part, size)
        rows = (start + offset, size, step)
    return ref.at[_span(rows), _span(cols)]


def _gather_job(full, axes, stage, part=None, peers=(0, 1, 2)):
    n = len(full)

    def copies(refs, sems):
        send_sem, recv_sem = sems
        x, y, c, chips = _mesh_position()
        me = 2 * x + y

        def copy(w, k, shard, half, target):
            rows_cols = full[w].shape
            place = _region(refs[w], axes[w], shard, half, rows_cols[axes[w]] // N_CHIPS, rows_cols[1 - axes[w]] // 2,
                            part)
            return pltpu.make_async_remote_copy(
                src_ref=place, dst_ref=place, send_sem=send_sem.at[w, k], recv_sem=recv_sem.at[w, k],
                device_id=target, device_id_type=MESH)

        def sent(w, k):
            if stage == "ici":
                return copy(w, k, me, c, (chips[k][0], chips[k][1], c))
            return copy(w, k, 2 * chips[k][0] + chips[k][1], c, (x, y, 1 - c))

        def landed(w, k):
            return copy(w, k, 2 * chips[k][0] + chips[k][1], c if stage == "ici" else 1 - c, (x, y, 1 - c))

        return sent, landed

    def start(refs, sems):
        sent, _ = copies(refs, sems)
        for w in range(n):
            for k in peers:
                sent(w, k).start()

    def finish(refs, sems):
        sent, landed = copies(refs, sems)
        for w in range(n):
            for k in peers:
                landed(w, k).wait_recv()
                sent(w, k).wait_send()

    return _Job(ios=full, sems=[pltpu.SemaphoreType.DMA((n, 3))] * 2, start=start, finish=finish)


def _exchange_job(grads, axes):
    n = len(grads)

    def half_shape(w):
        return tuple(d // 2 if a != axes[w] else d for a, d in enumerate(grads[w].shape))

    def copy(refs, sems, w):
        x, y, c, _ = _mesh_position()
        return pltpu.make_async_remote_copy(
            src_ref=_region(refs[w], axes[w], None, 1 - c, 0, half_shape(w)[1 - axes[w]]), dst_ref=refs[n + w],
            send_sem=sems[0].at[w], recv_sem=sems[1].at[w], device_id=(x, y, 1 - c), device_id_type=MESH)

    def start(refs, sems):
        for w in range(n):
            copy(refs, sems, w).start()

    def finish(refs, sems):
        for w in range(n):
            copy(refs, sems, w).wait()

    return _Job(ins=grads, outs=[jax.ShapeDtypeStruct(half_shape(w), F32) for w in range(n)],
                sems=[pltpu.SemaphoreType.DMA((n,))] * 2, start=start, finish=finish)


def _half_block_spec(axis, block, half_blocks, use_half):
    if axis == 1:
        if use_half:
            return pl.BlockSpec(block, lambda i, pos: (pos[0] * half_blocks + i, 0))
        return pl.BlockSpec(block, lambda i, pos: (i, 0))
    if use_half:
        return pl.BlockSpec(block, lambda i, pos: (i, pos[0]))
    return pl.BlockSpec(block, lambda i, pos: (i, 0))


def _add_halves(grad, received, axis, pos, *, name):
    rows, cols = received.shape
    tr = _row_block(rows, cols)
    nb = rows // tr

    def body(pos_ref, g_ref, r_ref, o_ref):
        o_ref[...] = (g_ref[...] + r_ref[...]).astype(BF16)

    blk = (tr, cols)
    return pl.pallas_call(
        body, name=name, out_shape=jax.ShapeDtypeStruct((rows, cols), BF16),
        grid_spec=pltpu.PrefetchScalarGridSpec(
            num_scalar_prefetch=1, grid=(nb,),
            in_specs=[_half_block_spec(axis, blk, nb, True), _half_block_spec(axis, blk, nb, False)],
            out_specs=_half_block_spec(axis, blk, nb, False)),
        compiler_params=_params(("parallel",)),
    )(pos, grad, received)


def _send_sums_job(sums, axes, part=None, landing=None):
    n = len(sums)

    def part_shape(w):
        return tuple(d // N_CHIPS if a == axes[w] else d for a, d in enumerate(sums[w].shape))

    def copy(refs, sems, w, k):
        x, y, c, chips = _mesh_position()
        shard = 2 * chips[k][0] + chips[k][1]
        rows = part_shape(w)[0]
        dst = refs[n + w].at[k]
        if part is not None:
            offset, size, _ = _part_rows(part, rows)
            dst = refs[n + w].at[k, pl.ds(offset, size), :]
        return pltpu.make_async_remote_copy(
            src_ref=_region(refs[w], axes[w], shard, None, part_shape(w)[axes[w]], 0, part, rows), dst_ref=dst,
            send_sem=sems[0].at[w, k], recv_sem=sems[1].at[w, k],
            device_id=(chips[k][0], chips[k][1], c), device_id_type=MESH)

    def start(refs, sems):
        for w in range(n):
            for k in range(3):
                copy(refs, sems, w, k).start()

    def finish(refs, sems):
        for w in range(n):
            for k in range(3):
                copy(refs, sems, w, k).wait()

    sems = [pltpu.SemaphoreType.DMA((n, 3))] * 2
    if landing is not None:
        return _Job(ins=sums, ios=landing, sems=sems, start=start, finish=finish)
    return _Job(ins=sums, outs=[jax.ShapeDtypeStruct((3,) + part_shape(w), BF16) for w in range(n)],
                sems=sems, start=start, finish=finish)


def _sum_chip_parts(grad, received, parts, axis, pos, *, name):
    _, rows, cols = parts.shape
    tr = _row_block(rows, cols)
    nb = rows // tr
    blk = (tr, cols)

    def body(pos_ref, g_ref, r_ref, p_ref, o_ref):
        total = g_ref[...] + r_ref[...]
        for k in range(3):
            total = total + p_ref[k].astype(F32)
        o_ref[...] = total

    if axis == 1:
        g_spec = pl.BlockSpec(blk, lambda i, pos: (pos[0] * nb + i, pos[1]))
        r_spec = pl.BlockSpec(blk, lambda i, pos: (i, pos[1]))
        o_spec = pl.BlockSpec(blk, lambda i, pos: (pos[0] * nb + i, 0))
        shard_shape = (2 * rows, cols)
    else:
        g_spec = pl.BlockSpec(blk, lambda i, pos: (pos[1] * nb + i, pos[0]))
        r_spec = pl.BlockSpec(blk, lambda i, pos: (pos[1] * nb + i, 0))
        o_spec = pl.BlockSpec(blk, lambda i, pos: (i, pos[0]))
        shard_shape = (rows, 2 * cols)
    return pl.pallas_call(
        body, name=name, out_shape=jax.ShapeDtypeStruct(shard_shape, F32),
        grid_spec=pltpu.PrefetchScalarGridSpec(
            num_scalar_prefetch=1, grid=(nb,),
            in_specs=[g_spec, r_spec, pl.BlockSpec((3,) + blk, lambda i, pos: (0, i, 0))],
            out_specs=o_spec),
        compiler_params=_params(("parallel",)),
    )(pos, grad, received, parts)


def _join_job(shards, axes):
    n = len(shards)

    def copy(refs, sems, w, other):
        x, y, c, _ = _mesh_position()
        place = _region(refs[w], axes[w], None, 1 - c if other else c, 0, shards[w].shape[1 - axes[w]] // 2)
        return pltpu.make_async_remote_copy(
            src_ref=place, dst_ref=place, send_sem=sems[0].at[w], recv_sem=sems[1].at[w],
            device_id=(x, y, 1 - c), device_id_type=MESH)

    def start(refs, sems):
        for w in range(n):
            copy(refs, sems, w, False).start()

    def finish(refs, sems):
        for w in range(n):
            copy(refs, sems, w, True).wait_recv()
            copy(refs, sems, w, False).wait_send()

    return _Job(ios=shards, sems=[pltpu.SemaphoreType.DMA((n,))] * 2, start=start, finish=finish)


def _all_reduce_small(vec, after=()):
    rows, cols = vec.shape

    def body(v_ref, *rest):
        o_ref, land_ref, send_sem, recv_sem = rest[len(after):]
        x, y, c, _ = _mesh_position()
        me = 4 * x + 2 * y + c
        land_ref[me] = v_ref[...]
        copies = []
        for k in range(1, 8):
            px, py, pc = x ^ (k >> 2), y ^ ((k >> 1) & 1), c ^ (k & 1)
            copies.append(pltpu.make_async_remote_copy(
                src_ref=v_ref, dst_ref=land_ref.at[me], send_sem=send_sem.at[k], recv_sem=recv_sem.at[k],
                device_id=(px, py, pc), device_id_type=MESH))
        for cp in copies:
            cp.start()
        for k in range(1, 8):
            peer = me ^ k
            pltpu.make_async_remote_copy(
                src_ref=v_ref, dst_ref=land_ref.at[peer], send_sem=send_sem.at[k], recv_sem=recv_sem.at[k],
                device_id=(x, y, c), device_id_type=MESH).wait_recv()
        for cp in copies:
            cp.wait_send()
        total = land_ref[0]
        for k in range(1, 8):
            total = total + land_ref[k]
        o_ref[...] = total

    vmem = pl.BlockSpec(memory_space=pltpu.VMEM)
    return pl.pallas_call(
        body, name="all_reduce_small", in_specs=[vmem] + [pl.BlockSpec(memory_space=pl.ANY)] * len(after),
        out_specs=vmem, out_shape=jax.ShapeDtypeStruct((rows, cols), F32),
        scratch_shapes=[pltpu.VMEM((8, rows, cols), F32), pltpu.SemaphoreType.DMA((8,)), pltpu.SemaphoreType.DMA((8,))],
    )(vec, *after)


def _adamw(w, g, m, v, *, name, after=()):
    rows, cols = w.shape
    tr = _row_block(rows, cols) if rows % 8 == 0 else rows
    bc1 = 1.0 - ADAM_B1 ** ADAM_STEP
    bc2 = 1.0 - ADAM_B2 ** ADAM_STEP

    def body(w_ref, g_ref, m_ref, v_ref, *rest):
        go_ref, d_ref, mo_ref, vo_ref = rest[len(after):]
        gv = g_ref[...]
        go_ref[...] = gv
        mn = ADAM_B1 * m_ref[...] + (1.0 - ADAM_B1) * gv
        vn = ADAM_B2 * v_ref[...] + (1.0 - ADAM_B2) * (gv * gv)
        mo_ref[...] = mn
        vo_ref[...] = vn
        d_ref[...] = -ADAM_LR * ((mn / bc1) / (jnp.sqrt(vn / bc2) + ADAM_EPS) + ADAM_WD * w_ref[...])

    blk = pl.BlockSpec((tr, cols), lambda i: (i, 0))
    shape = jax.ShapeDtypeStruct((rows, cols), F32)
    return pl.pallas_call(
        body, name=name, grid=(rows // tr,), in_specs=[blk] * 4 + [pl.BlockSpec(memory_space=pl.ANY)] * len(after),
        out_specs=[blk] * 4, out_shape=[shape] * 4, compiler_params=_params(("parallel",)),
    )(w, g, m, v, *after)


def _to_bf16_in_place(w, axis, pos, *, name, after=None):
    rows, cols = w.shape
    tr = _row_block(rows, cols)
    nb = rows // tr

    def body(pos_ref, w_ref, *rest):
        rest[-1][...] = w_ref[...].astype(BF16)

    if axis == 1:
        o_spec = pl.BlockSpec((tr, cols), lambda i, pos: (i, pos[1]))
        full_shape = (rows, N_CHIPS * cols)
    else:
        o_spec = pl.BlockSpec((tr, cols), lambda i, pos: (pos[1] * nb + i, 0))
        full_shape = (N_CHIPS * rows, cols)
    in_specs = [pl.BlockSpec((tr, cols), lambda i, pos: (i, 0))]
    operands = [pos, w]
    if after is not None:
        in_specs.append(pl.BlockSpec(after.shape, lambda i, pos: (0, 0)))
        operands.append(after)
    return pl.pallas_call(
        body, name=name, out_shape=jax.ShapeDtypeStruct(full_shape, BF16),
        grid_spec=pltpu.PrefetchScalarGridSpec(num_scalar_prefetch=1, grid=(nb,), in_specs=in_specs, out_specs=o_spec),
        compiler_params=_params(("parallel",)),
    )(*operands)


def _split_gather_start(full, axis):
    rows_cols = full.shape

    def body(buf_ref, *rest):
        sems = rest[:6]
        token_ref = rest[7]
        x, y, c, chips = _mesh_position()
        place = _region(buf_ref, axis, 2 * x + y, c, rows_cols[axis] // N_CHIPS, rows_cols[1 - axis] // 2)
        for k in range(3):
            pltpu.make_async_remote_copy(
                src_ref=place, dst_ref=place, send_sem=sems[k], recv_sem=sems[3 + k],
                device_id=(chips[k][0], chips[k][1], c), device_id_type=MESH).start()
        token_ref[...] = jnp.zeros_like(token_ref)

    hbm = pl.BlockSpec(memory_space=pltpu.HBM)
    sem = pl.BlockSpec(memory_space=pltpu.SEMAPHORE)
    res = pl.pallas_call(
        body, name="all_gather_w_in_start",
        out_shape=(*[pltpu.SemaphoreType.DMA(())] * 6, pltpu.HBM(full.shape, full.dtype),
                   jax.ShapeDtypeStruct((8, HEAD_DIM), F32)),
        in_specs=(hbm,), out_specs=(*[sem] * 6, hbm, pl.BlockSpec(memory_space=pltpu.VMEM)),
        input_output_aliases={0: 6},
        compiler_params=pltpu.CompilerParams(has_side_effects=pltpu.SideEffectType.DATAFLOW_SIDE_EFFECTING),
    )(pltpu.with_memory_space_constraint(full, pltpu.HBM))
    return list(res[:6]), res[6], res[7]


def _split_gather_wait(sems, full, axis, peer, after):
    rows_cols = full.shape

    def body(buf_ref, send_sem, recv_sem, *rest):
        x, y, c, chips = _mesh_position()

        def copy(shard):
            place = _region(buf_ref, axis, shard, c, rows_cols[axis] // N_CHIPS, rows_cols[1 - axis] // 2)
            return pltpu.make_async_remote_copy(
                src_ref=place, dst_ref=place, send_sem=send_sem, recv_sem=recv_sem,
                device_id=(chips[peer][0], chips[peer][1], c), device_id_type=MESH)

        copy(2 * x + y).wait_send()
        copy(2 * chips[peer][0] + chips[peer][1]).wait_recv()

    hbm = pl.BlockSpec(memory_space=pltpu.HBM)
    sem = pl.BlockSpec(memory_space=pltpu.SEMAPHORE)
    return pl.pallas_call(
        body, name="all_gather_w_in_wait_%d" % peer, out_shape=pltpu.HBM(full.shape, full.dtype),
        in_specs=(hbm, sem, sem, *[pl.BlockSpec(memory_space=pl.ANY)] * len(after)), out_specs=hbm,
        input_output_aliases={0: 0},
        compiler_params=pltpu.CompilerParams(has_side_effects=pltpu.SideEffectType.DATAFLOW_SIDE_EFFECTING),
    )(full, sems[peer], sems[3 + peer], *after)


def _in_proj_part(n1, w_in, proj, shard_ids, which, *, out_cols):
    m, kdim = n1.shape
    cols = out_cols // N_CHIPS
    tm = _tile(m, 1024)

    def body(ids_ref, a_ref, b_ref, *rest):
        rest[-1][...] = _dot(a_ref[...], b_ref[...]).astype(BF16)

    in_specs = [pl.BlockSpec((tm, kdim), lambda i, ids: (i, 0)),
                pl.BlockSpec((kdim, cols), lambda i, ids: (0, ids[which]))]
    operands = [shard_ids, n1, w_in]
    if proj is not None:
        in_specs.append(pl.BlockSpec(memory_space=pl.ANY))
        operands.append(proj)
    return pl.pallas_call(
        body, name="in_proj_%d" % which, out_shape=jax.ShapeDtypeStruct((m, out_cols), BF16),
        grid_spec=pltpu.PrefetchScalarGridSpec(
            num_scalar_prefetch=1, grid=(m // tm,), in_specs=in_specs,
            out_specs=pl.BlockSpec((tm, cols), lambda i, ids: (i, ids[which]))),
        input_output_aliases={3: 0} if proj is not None else {},
        compiler_params=_params(("parallel",)),
    )(*operands)


BIG = ("w_in", "w_out", "w_gate", "w_up", "w_down")
BIG_AXIS = dict(w_in=1, w_out=0, w_gate=1, w_up=1, w_down=0)
SMALL = ("norm_mix_w", "ret_decay_fwd", "ret_decay_bwd", "ret_norm_w", "norm_ffn_w", "norm_final_w")
ALL_WEIGHTS = ("norm_mix_w", "w_in", "ret_decay_fwd", "ret_decay_bwd", "ret_norm_w", "w_out", "norm_ffn_w",
               "w_gate", "w_up", "w_down", "norm_final_w")
SMALL_ROW = 128 * 8


def _pack_small(small):
    pieces = [jnp.reshape(small["loss"], (1,))] + [jnp.reshape(small[k], (-1,)) for k in SMALL]
    rows = []
    for p in pieces:
        pad = -p.shape[0] % (8 * SMALL_ROW)
        rows.append(jnp.reshape(jnp.pad(p, (0, pad)), (-1, SMALL_ROW)))
    return jnp.concatenate(rows, axis=0)


def _unpack_small(block, like):
    out, row = {}, 0
    for k in ("loss",) + SMALL:
        size = 1 if k == "loss" else like[k].size
        nrows = -(-size // (8 * SMALL_ROW)) * 8
        out[k] = jnp.reshape(block[row:row + nrows], (-1,))[:size]
        row += nrows
    return out


def kernel(x, norm_mix_w, w_in, ret_decay_fwd, ret_decay_bwd, ret_norm_w, w_out, norm_ffn_w, w_gate, w_up, w_down, norm_final_w, loss_target, m_norm_mix_w, m_w_in, m_ret_decay_fwd, m_ret_decay_bwd, m_ret_norm_w, m_w_out, m_norm_ffn_w, m_w_gate, m_w_up, m_w_down, m_norm_final_w, v_norm_mix_w, v_w_in, v_ret_decay_fwd, v_ret_decay_bwd, v_ret_norm_w, v_w_out, v_norm_ffn_w, v_w_gate, v_w_up, v_w_down, v_norm_final_w):
    weights = dict(norm_mix_w=norm_mix_w, w_in=w_in, ret_decay_fwd=ret_decay_fwd, ret_decay_bwd=ret_decay_bwd,
                   ret_norm_w=ret_norm_w, w_out=w_out, norm_ffn_w=norm_ffn_w, w_gate=w_gate, w_up=w_up,
                   w_down=w_down, norm_final_w=norm_final_w)
    m_in = dict(norm_mix_w=m_norm_mix_w, w_in=m_w_in, ret_decay_fwd=m_ret_decay_fwd, ret_decay_bwd=m_ret_decay_bwd,
                ret_norm_w=m_ret_norm_w, w_out=m_w_out, norm_ffn_w=m_norm_ffn_w, w_gate=m_w_gate, w_up=m_w_up,
                w_down=m_w_down, norm_final_w=m_norm_final_w)
    v_in = dict(norm_mix_w=v_norm_mix_w, w_in=v_w_in, ret_decay_fwd=v_ret_decay_fwd, ret_decay_bwd=v_ret_decay_bwd,
                ret_norm_w=v_ret_norm_w, w_out=v_w_out, norm_ffn_w=v_norm_ffn_w, w_gate=v_w_gate, w_up=v_w_up,
                w_down=v_w_down, norm_final_w=v_norm_final_w)
    pos = jnp.stack([lax.axis_index("c"), 2 * lax.axis_index("x") + lax.axis_index("y")]).astype(jnp.int32)

    own = {"w_in": _to_bf16_in_place(weights["w_in"][0], BIG_AXIS["w_in"], pos, name="cast_w_in")}
    w_in_started = _split_gather_start(own["w_in"], BIG_AXIS["w_in"])
    queued, token = {}, w_in_started[2]
    for k in ("w_gate", "w_out", "w_up", "w_down"):
        own[k] = _to_bf16_in_place(weights[k][0], BIG_AXIS[k], pos, name="cast_" + k, after=token)
        queued[k] = _split_start(_gather_job([own[k]], [BIG_AXIS[k]], "ici"), name="all_gather_%s_start" % k)
        token = queued[k]["token"]
    cx, cy = lax.axis_index("x"), lax.axis_index("y")
    shard_ids = jnp.stack([2 * cx + cy, 2 * (1 - cx) + cy, 2 * cx + 1 - cy, 2 * (1 - cx) + 1 - cy]).astype(jnp.int32)

    dx, grad_w, small, w_in_pending = _step(
        x[0], loss_target[0], norm_mix_w, ret_decay_fwd[0], ret_decay_bwd[0], ret_norm_w, norm_ffn_w,
        norm_final_w[None, :], own, w_in_started, queued, shard_ids, pos)

    delta, new_m, new_v = {}, {}, {}

    def update(k, after):
        shape = weights[k].shape
        as2d = (lambda t: jnp.reshape(t, (-1, shape[-1])))
        grad_w[k], delta[k], new_m[k], new_v[k] = (jnp.reshape(t, shape) for t in _adamw(
            as2d(weights[k]), as2d(grad_w[k]), as2d(m_in[k]), as2d(v_in[k]), name="adamw_" + k, after=after))

    others = [k for k in BIG if k != "w_in"]
    for k in others:
        update(k, [w_in_pending["sending"]["token"]])
    _, parts = _split_wait(w_in_pending["sending"], [dx] + [delta[k] for k in others], name="grad_send_w_in_wait")

    half = _sum_chip_parts(w_in_pending["grad"], w_in_pending["received"], parts, BIG_AXIS["w_in"], pos,
                           name="grad_sum_parts_w_in")
    joining = _split_start(_join_job([half], [BIG_AXIS["w_in"]]), name="grad_join_w_in_start")

    like = {k: weights[k] for k in SMALL}
    reduced = _unpack_small(_all_reduce_small(_pack_small(small), after=[joining["token"]]), like)
    loss = reduced["loss"][0]
    for k in SMALL:
        grad_w[k] = jnp.reshape(reduced[k], (1, -1))
        update(k, [])
    (grad_w["w_in"],) = _split_wait(joining, [delta[k] for k in SMALL], name="grad_join_w_in_wait")
    update("w_in", [])

    return (loss, dx[None], *[grad_w[k] for k in ALL_WEIGHTS], *[delta[k] for k in ALL_WEIGHTS],
            *[new_m[k] for k in ALL_WEIGHTS], *[new_v[k] for k in ALL_WEIGHTS])
```

```python
import functools
import math

import numpy as np
import jax
import jax.numpy as jnp
from jax import lax
from jax.experimental import pallas as pl
from jax.experimental.pallas import tpu as pltpu

F32 = jnp.float32
BF16 = jnp.bfloat16
MESH = pl.DeviceIdType.MESH

HEAD_DIM = 128
RET_CHUNK = 128
RET_UNROLL = 8
EPS = 1e-6
DILATED_PATTERNS = ((128, 1), (512, 4), (2048, 16))
ATT_BLOCK = 256
ATT_REACH = max(w // 2 for w, _ in DILATED_PATTERNS)
ATT_NEAR = ATT_BLOCK
ATT_CLASSES = DILATED_PATTERNS[-1][1]
assert all(w // 2 <= ATT_NEAR for w, _ in DILATED_PATTERNS[:-1])
ATT_KB = -(-ATT_NEAR // ATT_BLOCK)
ATT_WINDOW = 2 * ATT_KB + 1
ATT_FAR_GROUP = 8
ATT_NEAR_GROUP = 8
MASKED = -1e30
ROW_MAX_INIT = -1e29
N_CHIPS = 4
VMEM_LIMIT_BYTES = 56 * 1024 * 1024
ELEM_BLOCK_BYTES = 2 * 1024 * 1024
RING_SLOTS = 3
WEIGHT_GRAD_GROUP = 4

ADAM_LR = 0.001
ADAM_B1 = 0.9
ADAM_B2 = 0.999
ADAM_EPS = 1e-08
ADAM_WD = 0.01
ADAM_STEP = 10


def _params(sem=None):
    return pltpu.CompilerParams(dimension_semantics=sem, vmem_limit_bytes=VMEM_LIMIT_BYTES)


def _sigmoid(x):
    return 0.5 * jnp.tanh(0.5 * x) + 0.5


class _Job:
    def __init__(self, *, ins=(), ios=(), outs=(), sems=(), start, finish):
        self.ins, self.ios, self.outs, self.sems = list(ins), list(ios), list(outs), list(sems)
        self.start, self.finish = start, finish

    def results(self):
        return [jax.ShapeDtypeStruct(a.shape, a.dtype) for a in self.ios] + self.outs


def _call(body, *, name, grid, in_specs, out_specs, out_shape, operands, scratch_shapes=(), semantics=None, jobs=(),
          after=(), updates=None):
    in_specs, out_specs, out_shape = list(in_specs), list(out_specs), list(out_shape)
    scratch_shapes = list(scratch_shapes)
    if not jobs:
        n_real = len(in_specs)

        def ordered(*refs):
            body(*refs[:n_real], *refs[n_real + len(after):])

        outs = pl.pallas_call(
            ordered if after else body, name=name, grid=grid,
            in_specs=in_specs + [pl.BlockSpec(memory_space=pl.ANY)] * len(after), out_specs=out_specs,
            out_shape=out_shape, scratch_shapes=scratch_shapes, input_output_aliases=dict(updates or {}),
            compiler_params=_params(semantics))(*operands, *after)
        return outs, []
    n_in, n_out, n_scratch = len(in_specs), len(out_specs), len(scratch_shapes)
    extra_in, extra_out, sems, aliases = [], [], [], dict(updates or {})
    for job in jobs:
        extra_in += job.ins
        for t in range(len(job.ios)):
            aliases[n_in + len(extra_in) + t] = n_out + len(extra_out) + t
        extra_in += job.ios
        extra_out += job.results()
        sems += job.sems

    def carried(*refs):
        x_in = refs[n_in:n_in + len(extra_in)]
        first_out = n_in + len(extra_in) + len(after)
        x_out = refs[first_out + n_out:first_out + n_out + len(extra_out)]
        x_sem = refs[len(refs) - len(sems):]
        views, i_in, i_out, i_sem = [], 0, 0, 0
        for job in jobs:
            data = list(x_in[i_in:i_in + len(job.ins)]) + list(x_out[i_out:i_out + len(job.results())])
            views.append((data, x_sem[i_sem:i_sem + len(job.sems)]))
            i_in += len(job.ins) + len(job.ios)
            i_out += len(job.results())
            i_sem += len(job.sems)
        steps = [pl.program_id(d) for d in range(len(grid))]

        @pl.when(functools.reduce(jnp.logical_and, [s == 0 for s in steps]))
        def _():
            for job, (data, sem) in zip(jobs, views):
                job.start(data, sem)

        body(*refs[:n_in], *refs[first_out:first_out + n_out],
             *refs[len(refs) - len(sems) - n_scratch:len(refs) - len(sems)])

        @pl.when(functools.reduce(jnp.logical_and, [s == g - 1 for s, g in zip(steps, grid)]))
        def _():
            for job, (data, sem) in zip(jobs, views):
                job.finish(data, sem)

    hbm = pl.BlockSpec(memory_space=pl.ANY)
    res = pl.pallas_call(
        carried, name=name, grid=grid, in_specs=in_specs + [hbm] * (len(extra_in) + len(after)),
        out_specs=out_specs + [hbm] * len(extra_out), out_shape=out_shape + extra_out,
        input_output_aliases=aliases, scratch_shapes=scratch_shapes + sems,
        compiler_params=_params(("arbitrary",) * len(grid)),
    )(*operands, *extra_in, *after)
    carried_results, at = [], n_out
    for job in jobs:
        carried_results.append(list(res[at:at + len(job.results())]))
        at += len(job.results())
    return list(res[:n_out]), carried_results


def _run_jobs(jobs, *, name):
    first = jobs[0]
    n_in, n_io = len(first.ins), len(first.ios)
    out_shape = first.results()
    n_sems = [len(job.sems) for job in jobs]

    def body(*refs):
        data = list(refs[:n_in]) + list(refs[n_in + n_io:n_in + n_io + len(out_shape)])
        at = n_in + n_io + len(out_shape)
        for job, ns in zip(jobs, n_sems):
            job.start(data, refs[at:at + ns])
            job.finish(data, refs[at:at + ns])
            at += ns

    hbm = pl.BlockSpec(memory_space=pl.ANY)
    return pl.pallas_call(
        body, name=name, in_specs=[hbm] * (n_in + n_io), out_specs=[hbm] * len(out_shape), out_shape=out_shape,
        input_output_aliases={n_in + t: t for t in range(n_io)},
        scratch_shapes=[s for job in jobs for s in job.sems],
    )(*first.ins, *first.ios)


class _SemaphoreGrid:
    def __init__(self, refs, shape):
        self.refs, self.shape = list(refs), tuple(shape)

    @property
    def at(self):
        return self

    def __getitem__(self, index):
        index = index if isinstance(index, tuple) else (index,)
        flat = 0
        for i, extent in zip(index, self.shape):
            flat = flat * extent + i
        return self.refs[flat]


def _semaphore_grids(job, refs):
    grids, at = [], 0
    for sem in job.sems:
        count = math.prod(sem.shape)
        grids.append(_SemaphoreGrid(refs[at:at + count], sem.shape))
        at += count
    return grids


def _split_start(job, *, name):
    arrays = job.ins + job.ios + [lax.empty(s.shape, s.dtype) for s in job.outs]
    n, ns = len(arrays), sum(math.prod(sem.shape) for sem in job.sems)

    def body(*refs):
        job.start(list(refs[:n]), _semaphore_grids(job, refs[n:n + ns]))
        refs[-1][...] = jnp.zeros_like(refs[-1])

    hbm = pl.BlockSpec(memory_space=pltpu.HBM)
    res = pl.pallas_call(
        body, name=name,
        out_shape=(*[pltpu.SemaphoreType.DMA(())] * ns, *[pltpu.HBM(a.shape, a.dtype) for a in arrays],
                   jax.ShapeDtypeStruct((8, HEAD_DIM), F32)),
        in_specs=[hbm] * n,
        out_specs=(*[pl.BlockSpec(memory_space=pltpu.SEMAPHORE)] * ns, *[hbm] * n,
                   pl.BlockSpec(memory_space=pltpu.VMEM)),
        input_output_aliases={t: ns + t for t in range(n)},
        compiler_params=pltpu.CompilerParams(has_side_effects=pltpu.SideEffectType.DATAFLOW_SIDE_EFFECTING),
    )(*[pltpu.with_memory_space_constraint(a, pltpu.HBM) for a in arrays])
    return dict(job=job, sems=list(res[:ns]), arrays=list(res[ns:ns + n]), token=res[-1])


def _split_wait(started, after, *, name):
    job, arrays, sems = started["job"], started["arrays"], started["sems"]
    n, ns = len(arrays), len(sems)

    def body(*refs):
        job.finish(list(refs[:n]), _semaphore_grids(job, refs[n:n + ns]))

    hbm = pl.BlockSpec(memory_space=pltpu.HBM)
    return pl.pallas_call(
        body, name=name, out_shape=[pltpu.HBM(a.shape, a.dtype) for a in arrays],
        in_specs=[hbm] * n + [pl.BlockSpec(memory_space=pltpu.SEMAPHORE)] * ns
        + [pl.BlockSpec(memory_space=pl.ANY)] * len(after),
        out_specs=[hbm] * n, input_output_aliases={t: t for t in range(n)},
        compiler_params=pltpu.CompilerParams(has_side_effects=pltpu.SideEffectType.DATAFLOW_SIDE_EFFECTING),
    )(*arrays, *sems, *after)


def _dot(a, b, ta=False, tb=False):
    return lax.dot_general(a, b, (((0 if ta else 1,), (1 if tb else 0,)), ((), ())),
                           preferred_element_type=F32)


def _tile(n, want):
    t = min(n, want) // 128 * 128
    while n % t:
        t -= 128
    return t


def _a_spec(ta, tm, tk):
    return pl.BlockSpec((tk, tm), lambda i, j, k: (k, i)) if ta else pl.BlockSpec((tm, tk), lambda i, j, k: (i, k))


def _b_spec(tb, tk, tn):
    return pl.BlockSpec((tn, tk), lambda i, j, k: (j, k)) if tb else pl.BlockSpec((tk, tn), lambda i, j, k: (k, j))


def _accumulate(accs, nk, products, finish):
    if nk == 1:
        finish(*products())
        return
    k = pl.program_id(2)

    @pl.when(k == 0)
    def _():
        for acc, p in zip(accs, products()):
            acc[...] = p

    if nk > 2:
        @pl.when(jnp.logical_and(k > 0, k < nk - 1))
        def _():
            for acc, p in zip(accs, products()):
                acc[...] += p

    @pl.when(k == nk - 1)
    def _():
        finish(*[acc[...] + p for acc, p in zip(accs, products())])


def _matmul(a, b, *, name, ta=False, tb=False, out_dtype=F32, residual=None, tm=1024, tn=1024, tk=2048, jobs=()):
    m, kdim = (a.shape[1], a.shape[0]) if ta else a.shape
    n = b.shape[0] if tb else b.shape[1]
    tm, tn, tk = _tile(m, tm), _tile(n, tn), _tile(kdim, tk)
    nk = kdim // tk

    def body(*refs):
        a_ref, b_ref = refs[:2]
        r_ref = refs[2] if residual is not None else None
        o_ref = refs[-1] if nk == 1 else refs[-2]

        def finish(total):
            if residual is not None:
                total = total + r_ref[...]
            o_ref[...] = total.astype(out_dtype)

        _accumulate(refs[-1:] if nk > 1 else (), nk, lambda: (_dot(a_ref[...], b_ref[...], ta, tb),), finish)

    o_spec = pl.BlockSpec((tm, tn), lambda i, j, k: (i, j))
    in_specs = [_a_spec(ta, tm, tk), _b_spec(tb, tk, tn)]
    operands = [a, b]
    if residual is not None:
        in_specs.append(o_spec)
        operands.append(residual)
    (out,), carried = _call(
        body, name=name, grid=(m // tm, n // tn, nk), in_specs=in_specs, out_specs=[o_spec],
        out_shape=[jax.ShapeDtypeStruct((m, n), out_dtype)], operands=operands,
        scratch_shapes=[pltpu.VMEM((tm, tn), F32)] * (nk > 1),
        semantics=("parallel", "parallel", "arbitrary"), jobs=jobs)
    return (out, carried) if jobs else out


def _matmul_pieces_nt(pieces, b, *, name, tm=512, tn=1024, jobs=(), after=()):
    m, kp = pieces[0].shape
    n = b.shape[0]
    tm, tn = _tile(m, tm), _tile(n, tn)
    count = len(pieces)

    def body(*refs):
        b_ref, o_ref = refs[count], refs[count + 1]
        total = _dot(refs[0][...], b_ref[:, pl.ds(0, kp)], tb=True)
        for p in range(1, count):
            total = total + _dot(refs[p][...], b_ref[:, pl.ds(p * kp, kp)], tb=True)
        o_ref[...] = total

    piece = pl.BlockSpec((tm, kp), lambda j, i: (i, 0))
    (out,), carried = _call(
        body, name=name, grid=(n // tn, m // tm),
        in_specs=[piece] * count + [pl.BlockSpec((tn, count * kp), lambda j, i: (j, 0))],
        out_specs=[pl.BlockSpec((tm, tn), lambda j, i: (i, j))],
        out_shape=[jax.ShapeDtypeStruct((m, n), F32)], operands=[*pieces, b],
        semantics=("parallel", "parallel"), jobs=jobs, after=after)
    return (out, carried) if jobs else out


def _weight_grad_pieces(a, pieces, *, name):
    tokens, m = a.shape
    np_ = pieces[0].shape[1]
    tm = 1024 if m % 1024 == 0 else _tile(m, 1408)
    tn = _tile(np_, 512)
    nb = np_ // tn
    out = None
    for first in range(0, len(pieces), WEIGHT_GRAD_GROUP):
        group = pieces[first:first + WEIGHT_GRAD_GROUP]

        def body(*refs, count=len(group)):
            t_now = pl.program_id(1) // nb
            for t in range(count):
                @pl.when(t_now == t)
                def _(t=t):
                    refs[-1][...] = _dot(refs[0][...], refs[1 + t][...], ta=True)

        def piece_spec(t):
            return pl.BlockSpec((tokens, tn), lambda i, j: (0, jnp.clip(j - t * nb, 0, nb - 1)))

        in_specs = [pl.BlockSpec((tokens, tm), lambda i, j: (0, i))] + [piece_spec(t) for t in range(len(group))]
        operands = [a, *group]
        if out is not None:
            in_specs.append(pl.BlockSpec(memory_space=pl.ANY))
            operands.append(out)
        out = pl.pallas_call(
            body, name="%s_%d" % (name, first), grid=(m // tm, nb * len(group)), in_specs=in_specs,
            out_specs=pl.BlockSpec((tm, tn), lambda i, j, first=first: (i, first * nb + j)),
            out_shape=jax.ShapeDtypeStruct((m, len(pieces) * np_), F32),
            input_output_aliases={len(operands) - 1: 0} if out is not None else {},
            compiler_params=_params(("parallel", "arbitrary")),
        )(*operands)
    return out


def _weight_grad(a, g, *, name, jobs=()):
    tokens, m = a.shape
    tm = 1024 if m % 1024 == 0 else _tile(m, 1408)
    return _matmul(a, g, name=name, ta=True, tm=tm, tn=512, tk=tokens, jobs=jobs)


def _swiglu_fwd(n2, w_gate, w_up, *, tm=1024, tn=512, tk=2048, jobs=()):
    m, kdim = n2.shape
    n = w_gate.shape[1]
    tm, tn, tk = _tile(m, tm), _tile(n, tn), _tile(kdim, tk)
    nk = kdim // tk

    def body(a_ref, g_ref, u_ref, gate_ref, up_ref, act_ref, *acc):
        def products():
            a = a_ref[...]
            return _dot(a, g_ref[...]), _dot(a, u_ref[...])

        def finish(g, u):
            gate_ref[...] = g.astype(BF16)
            up_ref[...] = u.astype(BF16)
            act_ref[...] = (g * _sigmoid(g) * u).astype(BF16)

        _accumulate(acc, nk, products, finish)

    o_spec = pl.BlockSpec((tm, tn), lambda i, j, k: (i, j))
    o_shape = jax.ShapeDtypeStruct((m, n), BF16)
    return _call(
        body, name="swiglu_fwd", grid=(m // tm, n // tn, nk),
        in_specs=[_a_spec(False, tm, tk), _b_spec(False, tk, tn), _b_spec(False, tk, tn)],
        out_specs=[o_spec] * 3, out_shape=[o_shape] * 3, operands=[n2, w_gate, w_up],
        scratch_shapes=[pltpu.VMEM((tm, tn), F32)] * (2 * (nk > 1)),
        semantics=("parallel", "parallel", "arbitrary"), jobs=jobs)


def _swiglu_bwd_act(dh2, w_down, gate, up, *, tm=1024, tn=512, tk=2048):
    m, kdim = dh2.shape
    n = w_down.shape[0]
    tm, tn, tk = _tile(m, tm), _tile(n, tn), _tile(kdim, tk)
    assert tk == kdim
    nj = n // tn
    steps = (m // tm) * nj
    sub = _tile(tn, 256)

    def body(a_ref, b_hbm, gate_ref, up_ref, dgate_ref, dup_ref, ring, ring_sem):
        step = pl.program_id(0) * nj + pl.program_id(1)

        def fetch(t):
            rows = pl.ds(pl.multiple_of((t % nj) * tn, tn), tn)
            return pltpu.make_async_copy(b_hbm.at[rows, :], ring.at[t % RING_SLOTS], ring_sem.at[t % RING_SLOTS])

        @pl.when(step == 0)
        def _():
            for t in range(min(RING_SLOTS - 1, steps)):
                fetch(t).start()

        @pl.when(step + RING_SLOTS - 1 < steps)
        def _():
            fetch(step + RING_SLOTS - 1).start()

        fetch(step).wait()
        b_ref = ring.at[step % RING_SLOTS]
        a = a_ref[...]
        for c in range(tn // sub):
            cols = pl.ds(c * sub, sub)
            dact = _dot(a, b_ref[cols, :], tb=True)
            g = gate_ref[:, cols].astype(F32)
            u = up_ref[:, cols].astype(F32)
            sg = _sigmoid(g)
            dup_ref[:, cols] = (dact * g * sg).astype(BF16)
            dgate_ref[:, cols] = (dact * u * sg * (1.0 + g * (1.0 - sg))).astype(BF16)

    o_spec = pl.BlockSpec((tm, tn), lambda i, j: (i, j))
    o_shape = jax.ShapeDtypeStruct((m, n), BF16)
    return pl.pallas_call(
        body, name="swiglu_bwd_act", grid=(m // tm, nj),
        in_specs=[pl.BlockSpec((tm, tk), lambda i, j: (i, 0)), pl.BlockSpec(memory_space=pl.ANY), o_spec, o_spec],
        out_specs=[o_spec] * 2, out_shape=[o_shape] * 2,
        scratch_shapes=[pltpu.VMEM((RING_SLOTS, tn, tk), BF16), pltpu.SemaphoreType.DMA((RING_SLOTS,))],
        compiler_params=_params(("arbitrary", "arbitrary")),
    )(dh2, w_down, gate, up)


def _swiglu_bwd_in(dgate, dup, w_gate, w_up, *, tm=1024, tn=1024, tk=1408, jobs=()):
    m, kdim = dgate.shape
    n = w_gate.shape[0]
    tm, tn, tk = _tile(m, tm), _tile(n, tn), _tile(kdim, tk)
    nk = kdim // tk

    def body(a1_ref, a2_ref, b1_ref, b2_ref, o_ref, *acc):
        def product():
            return (_dot(a1_ref[...], b1_ref[...], tb=True) + _dot(a2_ref[...], b2_ref[...], tb=True),)

        def finish(total):
            o_ref[...] = total

        _accumulate(acc, nk, product, finish)

    a_spec, b_spec = _a_spec(False, tm, tk), _b_spec(True, tk, tn)
    (out,), carried = _call(
        body, name="swiglu_bwd_in", grid=(m // tm, n // tn, nk),
        in_specs=[a_spec, a_spec, b_spec, b_spec],
        out_specs=[pl.BlockSpec((tm, tn), lambda i, j, k: (i, j))],
        out_shape=[jax.ShapeDtypeStruct((m, n), F32)], operands=[dgate, dup, w_gate, w_up],
        scratch_shapes=[pltpu.VMEM((tm, tn), F32)] * (nk > 1),
        semantics=("parallel", "parallel", "arbitrary"), jobs=jobs)
    return out, carried


def _row_block(rows, cols):
    tr = min(rows, max(16, ELEM_BLOCK_BYTES // (4 * cols) // 16 * 16))
    while rows % tr:
        tr -= 16
    return tr


def _rmsnorm_fwd(x, g, *, name, after=None):
    s, d = x.shape
    tr = _row_block(s, d)

    def body(x_ref, g_ref, *rest):
        xv = x_ref[...]
        r = lax.rsqrt(jnp.mean(xv * xv, axis=-1, keepdims=True) + EPS)
        rest[-1][...] = (xv * r * g_ref[...]).astype(BF16)

    row = pl.BlockSpec((tr, d), lambda i: (i, 0))
    in_specs = [row, pl.BlockSpec((1, d), lambda i: (0, 0))]
    operands = [x, g]
    if after is not None:
        in_specs.append(pl.BlockSpec(after.shape, lambda i: (0, 0)))
        operands.append(after)
    return pl.pallas_call(
        body, name=name, grid=(s // tr,), in_specs=in_specs,
        out_specs=row, out_shape=jax.ShapeDtypeStruct((s, d), BF16),
        compiler_params=_params(("parallel",)),
    )(*operands)


def _rmsnorm_bwd_rows(xv, gv, dy):
    r = lax.rsqrt(jnp.mean(xv * xv, axis=-1, keepdims=True) + EPS)
    xhat = xv * r
    dxh = dy * gv
    dx = r * (dxh - xhat * jnp.mean(dxh * xhat, axis=-1, keepdims=True))
    return dx, dy * xhat


def _rmsnorm_bwd(dn, x, g, skip, *, name, after=()):
    s, d = x.shape
    tr = _row_block(s, d)

    def body(dn_ref, x_ref, g_ref, skip_ref, *rest):
        dx_ref, dxb_ref, dg_ref = rest[len(after):]
        dx, dgr = _rmsnorm_bwd_rows(x_ref[...], g_ref[...], dn_ref[...])
        dx = dx + skip_ref[...]
        dx_ref[...] = dx
        dxb_ref[...] = dx.astype(BF16)

        @pl.when(pl.program_id(0) == 0)
        def _():
            dg_ref[...] = jnp.zeros_like(dg_ref)

        dg_ref[...] += jnp.sum(dgr, axis=0, keepdims=True)

    row = pl.BlockSpec((tr, d), lambda i: (i, 0))
    vec = pl.BlockSpec((1, d), lambda i: (0, 0))
    return pl.pallas_call(
        body, name=name, grid=(s // tr,),
        in_specs=[row, row, vec, row] + [pl.BlockSpec(memory_space=pl.ANY)] * len(after),
        out_specs=[row, row, vec],
        out_shape=[jax.ShapeDtypeStruct((s, d), F32), jax.ShapeDtypeStruct((s, d), BF16),
                   jax.ShapeDtypeStruct((1, d), F32)],
        compiler_params=_params(("arbitrary",)),
    )(dn, x, g, skip, *after)


def _loss_head(h2, g, target):
    s, d = h2.shape
    tr = _row_block(s, d)

    def body(h_ref, g_ref, t_ref, dh_ref, dhb_ref, dg_ref, loss_ref):
        hv = h_ref[...]
        gv = g_ref[...]
        r = lax.rsqrt(jnp.mean(hv * hv, axis=-1, keepdims=True) + EPS)
        err = hv * r * gv - t_ref[...]
        dx, dgr = _rmsnorm_bwd_rows(hv, gv, err * (1.0 / d))
        dh_ref[...] = dx
        dhb_ref[...] = dx.astype(BF16)

        @pl.when(pl.program_id(0) == 0)
        def _():
            dg_ref[...] = jnp.zeros_like(dg_ref)
            loss_ref[...] = jnp.zeros_like(loss_ref)

        dg_ref[...] += jnp.sum(dgr, axis=0, keepdims=True)
        row_loss = jnp.mean(err * err, axis=-1, keepdims=True)
        loss_ref[...] += 0.5 * jnp.sum(row_loss, axis=0, keepdims=True)

    row = pl.BlockSpec((tr, d), lambda i: (i, 0))
    vec = pl.BlockSpec((1, d), lambda i: (0, 0))
    one = pl.BlockSpec((1, 1), lambda i: (0, 0))
    return pl.pallas_call(
        body, name="loss_head", grid=(s // tr,), in_specs=[row, vec, row],
        out_specs=[row, row, vec, one],
        out_shape=[jax.ShapeDtypeStruct((s, d), F32), jax.ShapeDtypeStruct((s, d), BF16),
                   jax.ShapeDtypeStruct((1, d), F32), jax.ShapeDtypeStruct((1, 1), F32)],
        compiler_params=_params(("arbitrary",)),
    )(h2, g, target)


def _attention_bias_tables():
    k = np.arange(-ATT_KB, ATT_KB + 1)[:, None, None]
    delta = k * ATT_BLOCK + np.arange(ATT_BLOCK)[None, None, :] - np.arange(ATT_BLOCK)[None, :, None]
    dist = np.abs(delta)
    count = np.zeros(delta.shape, np.int32)
    for window, dilation in DILATED_PATTERNS:
        count += (delta % dilation == 0) & (dist <= min(window // 2, ATT_NEAR))
    logc = np.where(count > 0, np.log(np.maximum(count, 1)), MASKED)
    return dist.astype(np.float32), logc.astype(np.float32)


def _far_bias_tables(per_class):
    steps = np.abs(np.arange(per_class)[:, None] - np.arange(per_class)[None, :]) * ATT_CLASSES
    valid = (steps > ATT_NEAR) & (steps <= ATT_REACH)
    return steps.astype(np.float32), np.where(valid, 0.0, MASKED).astype(np.float32)


def _to_classes(x):
    s, cols = x.shape
    return jnp.reshape(jnp.transpose(jnp.reshape(x, (s // ATT_CLASSES, ATT_CLASSES, cols)), (1, 0, 2)), (s, cols))


def _from_classes(x):
    s, cols = x.shape
    return jnp.reshape(jnp.transpose(jnp.reshape(x, (ATT_CLASSES, s // ATT_CLASSES, cols)), (1, 0, 2)), (s, cols))


def _head_bias(bias_ref, slope, dist_ref, logc_ref):
    for kk in range(ATT_WINDOW):
        bias_ref[kk] = logc_ref[kk] - slope * dist_ref[kk]
    bias_ref[ATT_WINDOW] = jnp.full((ATT_BLOCK, ATT_BLOCK), MASKED, F32)


def _window_start(i, nq, nwin):
    return jnp.clip(i - ATT_KB, 0, nq - nwin)


def _window_block(j, i):
    rows = pl.ds(pl.multiple_of(j * ATT_BLOCK, ATT_BLOCK), ATT_BLOCK)
    kk = j - i + ATT_KB
    return rows, jnp.where(jnp.logical_and(kk >= 0, kk < ATT_WINDOW), kk, ATT_WINDOW)


def _attention_far_fwd(qkv, slopes, n_heads, jobs=(), after=()):
    s = qkv.shape[0]
    per_class = s // ATT_CLASSES
    scale = HEAD_DIM ** -0.5
    dist, logc = _far_bias_tables(per_class)

    def body(slope_ref, q_ref, k_ref, v_ref, dist_ref, logc_ref, o_ref, lse_ref):
        bias = logc_ref[...] - slope_ref[pl.program_id(0)] * dist_ref[...]
        for a in range(ATT_FAR_GROUP):
            rows = pl.ds(a * per_class, per_class)
            sc = _dot(q_ref[rows, :], k_ref[rows, :], tb=True) * scale + bias
            m = jnp.maximum(jnp.max(sc, axis=-1, keepdims=True), ROW_MAX_INIT)
            p = jnp.exp(sc - m)
            l = jnp.maximum(jnp.sum(p, axis=-1, keepdims=True), 1e-30)
            o_ref[rows, :] = (_dot(p.astype(BF16), v_ref[rows, :]) / l).astype(BF16)
            lse_ref[rows, :] = jnp.broadcast_to(m + jnp.log(l), (per_class, HEAD_DIM))

    hh = n_heads
    blk = pl.BlockSpec((ATT_FAR_GROUP * per_class, HEAD_DIM), lambda h, r: (r, h))
    table = pl.BlockSpec(dist.shape, lambda h, r: (0, 0))
    return _call(
        body, name="attention_far_fwd", grid=(hh, ATT_CLASSES // ATT_FAR_GROUP),
        in_specs=[pl.BlockSpec(memory_space=pltpu.SMEM), blk,
                  pl.BlockSpec((ATT_FAR_GROUP * per_class, HEAD_DIM), lambda h, r: (r, hh + h)),
                  pl.BlockSpec((ATT_FAR_GROUP * per_class, HEAD_DIM), lambda h, r: (r, 2 * hh + h)), table, table],
        out_specs=[blk, blk],
        out_shape=[jax.ShapeDtypeStruct((s, hh * HEAD_DIM), BF16), jax.ShapeDtypeStruct((s, hh * HEAD_DIM), F32)],
        operands=[slopes, qkv, qkv, qkv, jnp.asarray(dist), jnp.asarray(logc)],
        semantics=("parallel", "parallel"), jobs=jobs, after=after)


def _attention_fwd(proj, slopes, far_out, far_lse, n_heads, jobs=(), after=()):
    s = proj.shape[0]
    nq = s // ATT_BLOCK
    scale = HEAD_DIM ** -0.5
    dist, logc = _attention_bias_tables()

    nwin = min(ATT_WINDOW, nq)

    group = math.gcd(ATT_NEAR_GROUP, nq)

    def body(slope_ref, q_ref, k_ref, v_ref, fo_ref, fl_ref, dist_ref, logc_ref, o_ref, lse_ref, bias_ref, s_ref):
        h, step = pl.program_id(0), pl.program_id(1)

        @pl.when(step == 0)
        def _():
            _head_bias(bias_ref, slope_ref[h], dist_ref, logc_ref)

        for a in range(group):
            i = step * group + a
            mine = pl.ds(a * ATT_BLOCK, ATT_BLOCK)
            q = q_ref[mine, :]
            first = _window_start(i, nq, nwin)
            m = jnp.full((ATT_BLOCK, 1), ROW_MAX_INIT, F32)
            for b in range(nwin):
                rows, kk = _window_block(first + b, i)
                sc = _dot(q, k_ref[rows, :], tb=True) * scale + bias_ref[kk]
                s_ref[a * nwin + b] = sc
                m = jnp.maximum(m, jnp.max(sc, axis=-1, keepdims=True))
            l = jnp.zeros((ATT_BLOCK, 1), F32)
            acc = jnp.zeros((ATT_BLOCK, HEAD_DIM), F32)
            for b in range(nwin):
                rows, _ = _window_block(first + b, i)
                p = jnp.exp(s_ref[a * nwin + b] - m)
                l = l + jnp.sum(p, axis=-1, keepdims=True)
                acc = acc + _dot(p.astype(BF16), v_ref[rows, :])
            near_lse = m + jnp.log(l)
            far_lse_col = fl_ref[mine, :1]
            lse = jnp.maximum(near_lse, far_lse_col)
            lse = lse + jnp.log(jnp.exp(near_lse - lse) + jnp.exp(far_lse_col - lse))
            o_ref[mine, :] = (acc * (jnp.exp(near_lse - lse) / l)
                              + fo_ref[mine, :].astype(F32) * jnp.exp(far_lse_col - lse)).astype(BF16)
            lse_ref[mine, :] = jnp.broadcast_to(lse, (ATT_BLOCK, HEAD_DIM))

    hh = n_heads
    blk = pl.BlockSpec((group * ATT_BLOCK, HEAD_DIM), lambda h, i: (i, h))
    table = pl.BlockSpec(dist.shape, lambda h, i: (0, 0, 0))
    return _call(
        body, name="attention_fwd", grid=(hh, nq // group),
        in_specs=[pl.BlockSpec(memory_space=pltpu.SMEM), blk,
                  pl.BlockSpec((s, HEAD_DIM), lambda h, i: (0, hh + h)),
                  pl.BlockSpec((s, HEAD_DIM), lambda h, i: (0, 2 * hh + h)), blk, blk, table, table],
        out_specs=[blk, blk],
        out_shape=[jax.ShapeDtypeStruct((s, hh * HEAD_DIM), BF16), jax.ShapeDtypeStruct((s, hh * HEAD_DIM), F32)],
        operands=[slopes, proj, proj, proj, far_out, far_lse, jnp.asarray(dist), jnp.asarray(logc)],
        scratch_shapes=[pltpu.VMEM((ATT_WINDOW + 1, ATT_BLOCK, ATT_BLOCK), F32),
                        pltpu.VMEM((group * nwin, ATT_BLOCK, ATT_BLOCK), F32)],
        semantics=("parallel", "arbitrary"), jobs=jobs, after=after)


def _attention_far_bwd(qkv, slopes, out, dout, lse, n_heads):
    s = qkv.shape[0]
    per_class = s // ATT_CLASSES
    scale = HEAD_DIM ** -0.5
    dist, logc = _far_bias_tables(per_class)

    def body(slope_ref, q_ref, k_ref, v_ref, o_ref, do_ref, lse_ref, dist_ref, logc_ref, dq_ref, dk_ref, dv_ref):
        bias = logc_ref[...] - slope_ref[pl.program_id(0)] * dist_ref[...]
        for a in range(ATT_FAR_GROUP):
            rows = pl.ds(a * per_class, per_class)
            q, k, do = q_ref[rows, :], k_ref[rows, :], do_ref[rows, :]
            delta = jnp.sum(do.astype(F32) * o_ref[rows, :].astype(F32), axis=-1, keepdims=True)
            p = jnp.exp(_dot(q, k, tb=True) * scale + bias - lse_ref[rows, :1])
            dv_ref[rows, :] = _dot(p.astype(BF16), do, ta=True).astype(BF16)
            ds = (p * (_dot(do, v_ref[rows, :], tb=True) - delta) * scale).astype(BF16)
            dk_ref[rows, :] = _dot(ds, q, ta=True).astype(BF16)
            dq_ref[rows, :] = _dot(ds, k).astype(BF16)

    hh = n_heads
    blk = pl.BlockSpec((ATT_FAR_GROUP * per_class, HEAD_DIM), lambda h, r: (r, h))
    table = pl.BlockSpec(dist.shape, lambda h, r: (0, 0))
    o_shape = jax.ShapeDtypeStruct((s, hh * HEAD_DIM), BF16)
    return pl.pallas_call(
        body, name="attention_far_bwd", grid=(hh, ATT_CLASSES // ATT_FAR_GROUP),
        in_specs=[pl.BlockSpec(memory_space=pltpu.SMEM), blk,
                  pl.BlockSpec((ATT_FAR_GROUP * per_class, HEAD_DIM), lambda h, r: (r, hh + h)),
                  pl.BlockSpec((ATT_FAR_GROUP * per_class, HEAD_DIM), lambda h, r: (r, 2 * hh + h)),
                  blk, blk, blk, table, table],
        out_specs=[blk] * 3, out_shape=[o_shape] * 3,
        compiler_params=_params(("parallel", "parallel")),
    )(slopes, qkv, qkv, qkv, out, dout, lse, jnp.asarray(dist), jnp.asarray(logc))


def _attention_bwd(proj, slopes, out, lse, dmixed, far_grads, n_heads, jobs=()):
    s = proj.shape[0]
    nq = s // ATT_BLOCK
    scale = HEAD_DIM ** -0.5
    dist, logc = _attention_bias_tables()

    nwin = min(ATT_WINDOW, nq)
    group = math.gcd(ATT_NEAR_GROUP, nq)

    def body(slope_ref, q_ref, k_ref, v_ref, o_ref, do_ref, lse_ref, fdq_ref, fdk_ref, fdv_ref, dist_ref, logc_ref,
             dq_ref, dk_ref, dv_ref, dk_acc, dv_acc, bias_ref):
        h, step = pl.program_id(0), pl.program_id(1)

        @pl.when(step == 0)
        def _():
            dk_acc[...] = jnp.zeros_like(dk_acc)
            dv_acc[...] = jnp.zeros_like(dv_acc)
            _head_bias(bias_ref, slope_ref[h], dist_ref, logc_ref)

        for a in range(group):
            i = step * group + a
            mine = pl.ds(a * ATT_BLOCK, ATT_BLOCK)
            q = q_ref[mine, :]
            do = do_ref[mine, :]
            lse_col = lse_ref[mine, :1]
            delta = jnp.sum(do.astype(F32) * o_ref[mine, :].astype(F32), axis=-1, keepdims=True)
            first = _window_start(i, nq, nwin)
            dq = jnp.zeros((ATT_BLOCK, HEAD_DIM), F32)
            for b in range(nwin):
                rows, kk = _window_block(first + b, i)
                kj = k_ref[rows, :]
                vj = v_ref[rows, :]
                p = jnp.exp(_dot(q, kj, tb=True) * scale + bias_ref[kk] - lse_col)
                dv_acc[rows, :] += _dot(p.astype(BF16), do, ta=True)
                dp = _dot(do, vj, tb=True)
                ds = (p * (dp - delta) * scale).astype(BF16)
                dk_acc[rows, :] += _dot(ds, q, ta=True)
                dq = dq + _dot(ds, kj)
            dq_ref[mine, :] = (dq + fdq_ref[mine, :].astype(F32)).astype(BF16)

        @pl.when(step == nq // group - 1)
        def _():
            dk_ref[...] = (dk_acc[...] + fdk_ref[...].astype(F32)).astype(BF16)
            dv_ref[...] = (dv_acc[...] + fdv_ref[...].astype(F32)).astype(BF16)

    hh = n_heads
    blk = pl.BlockSpec((group * ATT_BLOCK, HEAD_DIM), lambda h, i: (i, h))
    col = pl.BlockSpec((s, HEAD_DIM), lambda h, i: (0, h))
    table = pl.BlockSpec(dist.shape, lambda h, i: (0, 0, 0))
    o_shape = jax.ShapeDtypeStruct((s, hh * HEAD_DIM), BF16)
    return _call(
        body, name="attention_bwd", grid=(hh, nq // group),
        in_specs=[pl.BlockSpec(memory_space=pltpu.SMEM), blk,
                  pl.BlockSpec((s, HEAD_DIM), lambda h, i: (0, hh + h)),
                  pl.BlockSpec((s, HEAD_DIM), lambda h, i: (0, 2 * hh + h)),
                  blk, blk, blk, blk, col, col, table, table],
        out_specs=[blk, col, col], out_shape=[o_shape] * 3,
        operands=[slopes, proj, proj, proj, out, dmixed, lse, *far_grads, jnp.asarray(dist), jnp.asarray(logc)],
        scratch_shapes=[pltpu.VMEM((s, HEAD_DIM), F32)] * 2
        + [pltpu.VMEM((ATT_WINDOW + 1, ATT_BLOCK, ATT_BLOCK), F32)],
        semantics=("parallel", "arbitrary"), jobs=jobs)


def _ret_decays(lgc, lga, strict_c, strict_a):
    c = RET_CHUNK
    rel = (lax.broadcasted_iota(jnp.int32, (c, c), 0) - lax.broadcasted_iota(jnp.int32, (c, c), 1)).astype(F32)
    in_c = (rel > 0) if strict_c else (rel >= 0)
    in_a = (rel < 0) if strict_a else (rel <= 0)
    mask = (jnp.where(in_c, jnp.exp(lgc * jnp.maximum(rel, 0.0)), 0.0)
            + jnp.where(in_a, jnp.exp(lga * jnp.maximum(-rel, 0.0)), 0.0))
    idx = lax.broadcasted_iota(jnp.int32, (c, 1), 0).astype(F32)
    ones = jnp.ones((1, HEAD_DIM), F32)
    dec = dict(
        rel=rel, mask=mask, idx=idx,
        a_c=jnp.exp(lgc * (idx + 1.0)), b_c=jnp.exp(lgc * (c - 1.0 - idx)), chunk_c=jnp.exp(ones * (lgc * c)),
        a_a=jnp.exp(lga * (c - idx)), b_a=jnp.exp(lga * idx), chunk_a=jnp.exp(ones * (lga * c)),
    )
    return dec


def _scaled(x, col):
    return (x.astype(F32) * col).astype(BF16)


def _chunk_rows(i):
    return pl.ds(pl.multiple_of(i * RET_CHUNK, RET_CHUNK), RET_CHUNK)


def _chunk_loop(nc, step, init, unroll=RET_UNROLL):
    group = math.gcd(nc, unroll)

    def trip(t, carry):
        for u in range(group):
            carry = step(t * group + u, carry)
        return carry

    return lax.fori_loop(0, nc // group, trip, init)


def _retention(a, b, c, lg_c, lg_a, *, strict_c, strict_a, scale, n_heads, name, gate=None, norm_w=None, jobs=(),
               heads=None, so_far=None, after=()):
    s = a[0].shape[0]
    nc = s // RET_CHUNK
    epilogue = gate is not None
    first_head, head_count = heads if heads is not None else (0, n_heads)

    def body(*refs):
        lgc_ref, lga_ref, a_ref, b_ref, c_ref = refs[:5]
        if epilogue:
            g_ref, w_ref = refs[5:7]
            o_ref, mix_ref, sa_ref = refs[-3:]
        else:
            o_ref, sa_ref = refs[-2:]
        h = first_head + pl.program_id(0)
        dec = _ret_decays(lgc_ref[h], lga_ref[h], strict_c, strict_a)

        def reverse(t, state):
            i = nc - 1 - t
            sa_ref[i] = state.astype(BF16)
            rows = _chunk_rows(i)
            return state * dec["chunk_a"] + _dot(_scaled(b_ref[rows, :], dec["b_a"]), c_ref[rows, :], ta=True)

        _chunk_loop(nc, reverse, jnp.zeros((HEAD_DIM, HEAD_DIM), F32))

        def forward(i, state):
            rows = _chunk_rows(i)
            ai, bi, ci = a_ref[rows, :], b_ref[rows, :], c_ref[rows, :]
            inner = (_dot(ai, bi, tb=True) * dec["mask"]).astype(BF16)
            out = (_dot(inner, ci) + _dot(_scaled(ai, dec["a_c"]), state.astype(BF16))
                   + _dot(_scaled(ai, dec["a_a"]), sa_ref[i])) * scale
            o_ref[rows, :] = out.astype(BF16)
            if epilogue:
                r = lax.rsqrt(jnp.mean(out * out, axis=-1, keepdims=True) + EPS)
                g = g_ref[rows, :].astype(F32)
                mix_ref[rows, :] = (out * r * w_ref[...] * (g * _sigmoid(g))).astype(BF16)
            return state * dec["chunk_c"] + _dot(_scaled(bi, dec["b_c"]), ci, ta=True)

        _chunk_loop(nc, forward, jnp.zeros((HEAD_DIM, HEAD_DIM), F32))

    def col(first):
        return pl.BlockSpec((s, HEAD_DIM), lambda h: (0, first + first_head + h))

    smem = pl.BlockSpec(memory_space=pltpu.SMEM)
    in_specs = [smem, smem, col(a[1]), col(b[1]), col(c[1])]
    operands = [lg_c, lg_a, a[0], b[0], c[0]]
    o_shape = jax.ShapeDtypeStruct((s, n_heads * HEAD_DIM), BF16)
    out_specs, out_shape = [col(0)], [o_shape]
    if epilogue:
        in_specs += [col(gate[1]), pl.BlockSpec((1, HEAD_DIM), lambda h: (0, first_head + h))]
        operands += [gate[0], norm_w]
        out_specs, out_shape = [col(0)] * 2, [o_shape] * 2
    updates = None
    if so_far is not None:
        updates = {len(operands) + t: t for t in range(len(so_far))}
        in_specs += [pl.BlockSpec(memory_space=pl.ANY)] * len(so_far)
        operands += list(so_far)
    res, carried = _call(
        body, name=name, grid=(head_count,), in_specs=in_specs, out_specs=out_specs, out_shape=out_shape,
        operands=operands, scratch_shapes=[pltpu.VMEM((nc, HEAD_DIM, HEAD_DIM), BF16)],
        semantics=("parallel",), jobs=jobs, updates=updates, after=after)
    res = res if epilogue else res[0]
    return (res, carried) if jobs else res


def _retention_decay_grads(a, b, c, e, lg_c, lg_a, *, scale, n_heads):
    s = a[0].shape[0]
    nc = s // RET_CHUNK
    cf = float(RET_CHUNK)

    def body(lgc_ref, lga_ref, a_ref, b_ref, c_ref, e_ref, gc_ref, ga_ref, sa_ref, ta_ref):
        h = pl.program_id(0)
        lgc, lga = lgc_ref[h], lga_ref[h]
        dec = _ret_decays(lgc, lga, True, True)
        rel, idx = dec["rel"], dec["idx"]
        w_c = jnp.where(rel > 0, rel * jnp.exp(lgc * jnp.maximum(rel, 0.0)), 0.0)
        w_a = jnp.where(rel < 0, -rel * jnp.exp(lga * jnp.maximum(-rel, 0.0)), 0.0)
        zero = jnp.zeros((HEAD_DIM, HEAD_DIM), F32)

        def reverse(t, carry):
            st, dst = carry
            i = nc - 1 - t
            sa_ref[i] = st.astype(BF16)
            ta_ref[i] = dst.astype(BF16)
            rows = _chunk_rows(i)
            bi, ci = b_ref[rows, :], c_ref[rows, :]
            st_new = st * dec["chunk_a"] + _dot(_scaled(bi, dec["b_a"]), ci, ta=True)
            dst_new = (cf * st + dst) * dec["chunk_a"] + _dot(_scaled(bi, idx * dec["b_a"]), ci, ta=True)
            return st_new, dst_new

        _chunk_loop(nc, reverse, (zero, zero))

        def forward(i, carry):
            st, dst, acc_c, acc_a = carry
            rows = _chunk_rows(i)
            ai, bi, ci = a_ref[rows, :], b_ref[rows, :], c_ref[rows, :]
            ev = e_ref[rows, :].astype(F32)
            pg = _dot(ai, bi, tb=True) * _dot(e_ref[rows, :], ci, tb=True)
            a_c, a_a = _scaled(ai, dec["a_c"]), _scaled(ai, dec["a_a"])
            inter_c = _dot(a_c, st.astype(BF16)) * (idx + 1.0) + _dot(a_c, dst.astype(BF16))
            inter_a = _dot(a_a, sa_ref[i]) * (cf - idx) + _dot(a_a, ta_ref[i])
            acc_c = acc_c + jnp.sum(pg * w_c, axis=0, keepdims=True) + jnp.sum(inter_c * ev, axis=0, keepdims=True)
            acc_a = acc_a + jnp.sum(pg * w_a, axis=0, keepdims=True) + jnp.sum(inter_a * ev, axis=0, keepdims=True)
            st_new = st * dec["chunk_c"] + _dot(_scaled(bi, dec["b_c"]), ci, ta=True)
            dst_new = ((cf * st + dst) * dec["chunk_c"]
                       + _dot(_scaled(bi, (cf - 1.0 - idx) * dec["b_c"]), ci, ta=True))
            return st_new, dst_new, acc_c, acc_a

        row = jnp.zeros((1, HEAD_DIM), F32)
        _, _, acc_c, acc_a = _chunk_loop(nc, forward, (zero, zero, row, row))
        gc_ref[...] = jnp.broadcast_to(jnp.sum(acc_c, axis=-1, keepdims=True) * scale, gc_ref.shape)
        ga_ref[...] = jnp.broadcast_to(jnp.sum(acc_a, axis=-1, keepdims=True) * scale, ga_ref.shape)

    def col(first):
        return pl.BlockSpec((s, HEAD_DIM), lambda h: (0, first + h))

    smem = pl.BlockSpec(memory_space=pltpu.SMEM)
    o_spec = pl.BlockSpec((1, 8, HEAD_DIM), lambda h: (h, 0, 0))
    o_shape = jax.ShapeDtypeStruct((n_heads, 8, HEAD_DIM), F32)
    gc, ga = pl.pallas_call(
        body, name="retention_decay_grads", grid=(n_heads,),
        in_specs=[smem, smem, col(a[1]), col(b[1]), col(c[1]), col(e[1])],
        out_specs=[o_spec] * 2, out_shape=[o_shape] * 2,
        scratch_shapes=[pltpu.VMEM((nc, HEAD_DIM, HEAD_DIM), BF16)] * 2,
        compiler_params=_params(("parallel",)),
    )(lg_c, lg_a, a[0], b[0], c[0], e[0])
    return gc[:, 0, 0], ga[:, 0, 0]


def _ret_gate_bwd(dmixed, first_col, out, proj, gate_col, norm_w, n_heads):
    s = out.shape[0]
    tr = _row_block(s, 8 * HEAD_DIM)

    def body(dm_ref, o_ref, g_ref, w_ref, do_ref, dg_ref, dw_ref):
        dm = dm_ref[...].astype(F32)
        ov = o_ref[...].astype(F32)
        g = g_ref[...].astype(F32)
        w = w_ref[...]
        r = lax.rsqrt(jnp.mean(ov * ov, axis=-1, keepdims=True) + EPS)
        ohat = ov * r
        sg = _sigmoid(g)
        silu = g * sg
        dg_ref[...] = (dm * ohat * w * sg * (1.0 + g * (1.0 - sg))).astype(BF16)
        dohat = dm * w * silu
        do_ref[...] = (r * (dohat - ohat * jnp.mean(dohat * ohat, axis=-1, keepdims=True))).astype(BF16)

        @pl.when(pl.program_id(1) == 0)
        def _():
            dw_ref[...] = jnp.zeros_like(dw_ref)

        dw_ref[...] += jnp.sum(dm * ohat * silu, axis=0, keepdims=True)

    def blk(first):
        return pl.BlockSpec((tr, HEAD_DIM), lambda h, i: (i, first + h))

    vec = pl.BlockSpec((1, HEAD_DIM), lambda h, i: (0, h))
    o_shape = jax.ShapeDtypeStruct((s, n_heads * HEAD_DIM), BF16)
    return pl.pallas_call(
        body, name="ret_gate_bwd", grid=(n_heads, s // tr),
        in_specs=[blk(first_col), blk(0), blk(gate_col), vec],
        out_specs=[blk(0), blk(0), vec],
        out_shape=[o_shape, o_shape, jax.ShapeDtypeStruct((1, n_heads * HEAD_DIM), F32)],
        compiler_params=_params(("parallel", "arbitrary")),
    )(dmixed, out, proj, norm_w)


def _step(x, target, norm_mix_w, ret_decay_fwd, ret_decay_bwd, ret_norm_w, norm_ffn_w, norm_final_w, own,
          w_in_started, queued, shard_ids, pos):
    d = x.shape[1]
    nh = d // (2 * HEAD_DIM)
    scale = HEAD_DIM ** -0.5
    slopes = jnp.exp2(-8.0 * jnp.arange(1, nh + 1, dtype=F32) / nh)
    lg_f = -jnp.exp(ret_decay_fwd)
    lg_b = -jnp.exp(ret_decay_bwd)
    q_r, k_r, v_r, g_r = 3 * nh, 4 * nh, 5 * nh, 6 * nh
    ax = BIG_AXIS

    def gather(names, arrays, stage, part=None, peers=(0, 1, 2)):
        return _gather_job(arrays, [ax[k] for k in names], stage, part, peers)

    def add_halves(k, g, received):
        return _add_halves(g, received, ax[k], pos, name="grad_add_halves_" + k)

    def sum_parts(k, g, received, parts):
        return _sum_chip_parts(g, received, parts, ax[k], pos, name="grad_sum_parts_" + k)

    sems, w_in, token = w_in_started
    n1 = _rmsnorm_fwd(x, norm_mix_w, name="norm_mix_fwd", after=token)
    proj = _in_proj_part(n1, w_in, None, shard_ids, 0, out_cols=w_in.shape[1])
    for peer in range(3):
        behind = [proj] + ([queued["w_down"]["token"]] if peer == 0 else [])
        w_in = _split_gather_wait(sems, w_in, ax["w_in"], peer, behind)
        (w_in,) = _run_jobs([gather(["w_in"], [w_in], "d2d", peers=(peer,))], name="all_gather_w_in_sibling_%d" % peer)
        proj = _in_proj_part(n1, w_in, proj, shard_ids, 1 + peer, out_cols=w_in.shape[1])
    (w_gate,) = _split_wait(queued["w_gate"], [proj], name="all_gather_w_gate_wait")
    qkv_classes = _to_classes(proj[:, :3 * nh * HEAD_DIM])
    (ret, ret_mixed), [[w_gate]] = _retention(
        (proj, q_r), (proj, k_r), (proj, v_r), lg_f, lg_b, strict_c=False, strict_a=True, scale=scale, n_heads=nh,
        name="retention_fwd_first", gate=(proj, g_r), norm_w=ret_norm_w, heads=(0, nh // 2),
        jobs=[gather(["w_gate"], [w_gate], "d2d")])
    (far_out, far_lse), _ = _attention_far_fwd(qkv_classes, slopes, nh, after=[ret_mixed])
    ret, ret_mixed = _retention(
        (proj, q_r), (proj, k_r), (proj, v_r), lg_f, lg_b, strict_c=False, strict_a=True, scale=scale, n_heads=nh,
        name="retention_fwd_second", gate=(proj, g_r), norm_w=ret_norm_w, heads=(nh // 2, nh - nh // 2),
        so_far=[ret, ret_mixed], after=[far_out])
    (w_out,) = _split_wait(queued["w_out"], [ret_mixed], name="all_gather_w_out_wait")
    (attn, lse), [[w_out]] = _attention_fwd(
        proj, slopes, _from_classes(far_out), _from_classes(far_lse), nh, after=[ret_mixed],
        jobs=[gather(["w_out"], [w_out], "d2d")])
    mixed = jnp.concatenate([attn, ret_mixed], axis=1)
    (w_up,) = _split_wait(queued["w_up"], [mixed], name="all_gather_w_up_wait")
    h1, [[w_up]] = _matmul(mixed, w_out, name="out_proj", residual=x, jobs=[gather(["w_up"], [w_up], "d2d")])
    n2 = _rmsnorm_fwd(h1, norm_ffn_w, name="norm_ffn_fwd")
    (w_down,) = _split_wait(queued["w_down"], [n2], name="all_gather_w_down_wait")
    (gate, up, act), [[w_down]] = _swiglu_fwd(n2, w_gate, w_up, jobs=[gather(["w_down"], [w_down], "d2d")])
    h2 = _matmul(act, w_down, name="down_proj", residual=h1, tk=2816)
    dh2, dh2_b, d_norm_final, loss = _loss_head(h2, norm_final_w, target)

    dgate, dup = _swiglu_bwd_act(dh2_b, w_down, gate, up)
    g_down = _weight_grad(act, dh2_b, name="grad_w_down")
    g_gate, [[r_down]] = _weight_grad(n2, dgate, name="grad_w_gate", jobs=[_exchange_job([g_down], [ax["w_down"]])])
    s_down = add_halves("w_down", g_down, r_down)
    g_up, [[r_gate], [p_down]] = _weight_grad(
        n2, dup, name="grad_w_up",
        jobs=[_exchange_job([g_gate], [ax["w_gate"]]), _send_sums_job([s_down], [ax["w_down"]], (0, 1, 2))])
    s_gate = add_halves("w_gate", g_gate, r_gate)
    dn2, [[r_up], [p_gate], [p_down]] = _swiglu_bwd_in(
        dgate, dup, w_gate, w_up,
        jobs=[_exchange_job([g_up], [ax["w_up"]]), _send_sums_job([s_gate], [ax["w_gate"]]),
              _send_sums_job([s_down], [ax["w_down"]], (1, 1, 2), landing=[p_down])])
    h_down = sum_parts("w_down", g_down, r_down, p_down)
    s_up = add_halves("w_up", g_up, r_up)
    h_gate = sum_parts("w_gate", g_gate, r_gate, p_gate)
    dh1, dh1_b, d_norm_ffn = _rmsnorm_bwd(dn2, h1, norm_ffn_w, dh2, name="norm_ffn_bwd")

    dmixed, [[gr_down], [p_up]] = _matmul(
        dh1_b, w_out, name="out_proj_bwd", tb=True, out_dtype=BF16,
        jobs=[_join_job([h_down], [ax["w_down"]]), _send_sums_job([s_up], [ax["w_up"]], (0, 1, 4))])
    far_in = [_to_classes(t) for t in (attn, dmixed[:, :nh * HEAD_DIM], lse)]
    g_out, [[p_up]] = _weight_grad(mixed, dh1_b, name="grad_w_out",
                                   jobs=[_send_sums_job([s_up], [ax["w_up"]], (1, 1, 4), landing=[p_up])])
    d_ret, dg_r, d_ret_norm = _ret_gate_bwd(dmixed, nh, ret, proj, g_r, ret_norm_w, nh)
    far_grads = _attention_far_bwd(qkv_classes, slopes, *far_in, nh)
    far_grads = [_from_classes(t) for t in far_grads]
    dq_r, [[gr_gate], [p_up]] = _retention(
        (d_ret, 0), (proj, v_r), (proj, k_r), lg_f, lg_b, strict_c=False, strict_a=True, scale=scale, n_heads=nh,
        name="retention_dq",
        jobs=[_join_job([h_gate], [ax["w_gate"]]), _send_sums_job([s_up], [ax["w_up"]], (2, 1, 4), landing=[p_up])])
    (dq_a, dk_a, dv_a), [[p_up], [r_out]] = _attention_bwd(
        proj, slopes, attn, lse, dmixed, far_grads, nh,
        jobs=[_send_sums_job([s_up], [ax["w_up"]], (3, 1, 4), landing=[p_up]),
              _exchange_job([g_out], [ax["w_out"]])])
    s_out = add_halves("w_out", g_out, r_out)
    h_up = sum_parts("w_up", g_up, r_up, p_up)
    dv_r, [[p_out], [gr_up]] = _retention(
        (proj, k_r), (proj, q_r), (d_ret, 0), lg_b, lg_f, strict_c=True, strict_a=False, scale=scale, n_heads=nh,
        name="retention_dv", jobs=[_send_sums_job([s_out], [ax["w_out"]]), _join_job([h_up], [ax["w_up"]])])
    h_out = sum_parts("w_out", g_out, r_out, p_out)
    dk_r, [[gr_out]] = _retention(
        (proj, v_r), (d_ret, 0), (proj, q_r), lg_b, lg_f, strict_c=True, strict_a=False, scale=scale, n_heads=nh,
        name="retention_dk", jobs=[_join_job([h_out], [ax["w_out"]])])
    dlg_f, dlg_b = _retention_decay_grads((proj, q_r), (proj, k_r), (proj, v_r), (d_ret, 0), lg_f, lg_b,
                                          scale=scale, n_heads=nh)
    dproj = [dq_a, dk_a, dv_a, dq_r, dk_r, dv_r, dg_r]
    g_in = _weight_grad_pieces(n1, dproj, name="grad_w_in")
    exchange = _split_start(_exchange_job([g_in], [ax["w_in"]]), name="grad_exchange_w_in_start")
    dn1 = _matmul_pieces_nt(dproj, w_in, name="in_proj_bwd", after=[exchange["token"]])
    g_in, r_in = _split_wait(exchange, [dn1], name="grad_exchange_w_in_wait")
    s_in = add_halves("w_in", g_in, r_in)
    sending = _split_start(_send_sums_job([s_in], [ax["w_in"]]), name="grad_send_w_in_start")
    dx, _, d_norm_mix = _rmsnorm_bwd(dn1, x, norm_mix_w, dh1, name="norm_mix_bwd", after=[sending["token"]])

    small = dict(loss=loss[0, 0], norm_mix_w=d_norm_mix, ret_decay_fwd=dlg_f * lg_f, ret_decay_bwd=dlg_b * lg_b,
                 ret_norm_w=d_ret_norm, norm_ffn_w=d_norm_ffn, norm_final_w=d_norm_final)
    return (dx, dict(w_out=gr_out, w_gate=gr_gate, w_up=gr_up, w_down=gr_down), small,
            dict(sending=sending, grad=g_in, received=r_in))


def _mesh_position():
    x, y, c = lax.axis_index("x"), lax.axis_index("y"), lax.axis_index("c")
    chips = [(1 - x, y), (x, 1 - y), (1 - x, 1 - y)]
    return x, y, c, chips


def _span(span):
    if span is None:
        return slice(None)
    start, size, step = span
    return pl.ds(start if isinstance(start, int) else pl.multiple_of(start, step), size)


def _part_rows(part, rows):
    first, count, of = part
    return first * (rows // of), count * (rows // of), rows // of


def _region(ref, axis, shard, half, shard_size, half_size, part=None, total_rows=None):
    along = None if shard is None else (shard * shard_size, shard_size, shard_size)
    other = None if half is None else (half * half_size, half_size, half_size)
    rows, cols = (other, along) if axis == 1 else (along, other)
    if part is not None:
        start, size, _ = rows if rows is not None else (0, total_rows, None)
        offset, size, step = _part_rows(part, size)
        rows = (start + offset, size, step)
    return ref.at[_span(rows), _span(cols)]


def _gather_job(full, axes, stage, part=None, peers=(0, 1, 2)):
    n = len(full)

    def copies(refs, sems):
        send_sem, recv_sem = sems
        x, y, c, chips = _mesh_position()
        me = 2 * x + y

        def copy(w, k, shard, half, target):
            rows_cols = full[w].shape
            place = _region(refs[w], axes[w], shard, half, rows_cols[axes[w]] // N_CHIPS, rows_cols[1 - axes[w]] // 2,
                            part)
            return pltpu.make_async_remote_copy(
                src_ref=place, dst_ref=place, send_sem=send_sem.at[w, k], recv_sem=recv_sem.at[w, k],
                device_id=target, device_id_type=MESH)

        def sent(w, k):
            if stage == "ici":
                return copy(w, k, me, c, (chips[k][0], chips[k][1], c))
            return copy(w, k, 2 * chips[k][0] + chips[k][1], c, (x, y, 1 - c))

        def landed(w, k):
            return copy(w, k, 2 * chips[k][0] + chips[k][1], c if stage == "ici" else 1 - c, (x, y, 1 - c))

        return sent, landed

    def start(refs, sems):
        sent, _ = copies(refs, sems)
        for w in range(n):
            for k in peers:
                sent(w, k).start()

    def finish(refs, sems):
        sent, landed = copies(refs, sems)
        for w in range(n):
            for k in peers:
                landed(w, k).wait_recv()
                sent(w, k).wait_send()

    return _Job(ios=full, sems=[pltpu.SemaphoreType.DMA((n, 3))] * 2, start=start, finish=finish)


def _exchange_job(grads, axes):
    n = len(grads)

    def half_shape(w):
        return tuple(d // 2 if a != axes[w] else d for a, d in enumerate(grads[w].shape))

    def copy(refs, sems, w):
        x, y, c, _ = _mesh_position()
        return pltpu.make_async_remote_copy(
            src_ref=_region(refs[w], axes[w], None, 1 - c, 0, half_shape(w)[1 - axes[w]]), dst_ref=refs[n + w],
            send_sem=sems[0].at[w], recv_sem=sems[1].at[w], device_id=(x, y, 1 - c), device_id_type=MESH)

    def start(refs, sems):
        for w in range(n):
            copy(refs, sems, w).start()

    def finish(refs, sems):
        for w in range(n):
            copy(refs, sems, w).wait()

    return _Job(ins=grads, outs=[jax.ShapeDtypeStruct(half_shape(w), F32) for w in range(n)],
                sems=[pltpu.SemaphoreType.DMA((n,))] * 2, start=start, finish=finish)


def _half_block_spec(axis, block, half_blocks, use_half):
    if axis == 1:
        if use_half:
            return pl.BlockSpec(block, lambda i, pos: (pos[0] * half_blocks + i, 0))
        return pl.BlockSpec(block, lambda i, pos: (i, 0))
    if use_half:
        return pl.BlockSpec(block, lambda i, pos: (i, pos[0]))
    return pl.BlockSpec(block, lambda i, pos: (i, 0))


def _add_halves(grad, received, axis, pos, *, name):
    rows, cols = received.shape
    tr = _row_block(rows, cols)
    nb = rows // tr

    def body(pos_ref, g_ref, r_ref, o_ref):
        o_ref[...] = (g_ref[...] + r_ref[...]).astype(BF16)

    blk = (tr, cols)
    return pl.pallas_call(
        body, name=name, out_shape=jax.ShapeDtypeStruct((rows, cols), BF16),
        grid_spec=pltpu.PrefetchScalarGridSpec(
            num_scalar_prefetch=1, grid=(nb,),
            in_specs=[_half_block_spec(axis, blk, nb, True), _half_block_spec(axis, blk, nb, False)],
            out_specs=_half_block_spec(axis, blk, nb, False)),
        compiler_params=_params(("parallel",)),
    )(pos, grad, received)


def _send_sums_job(sums, axes, part=None, landing=None):
    n = len(sums)

    def part_shape(w):
        return tuple(d // N_CHIPS if a == axes[w] else d for a, d in enumerate(sums[w].shape))

    def copy(refs, sems, w, k):
        x, y, c, chips = _mesh_position()
        shard = 2 * chips[k][0] + chips[k][1]
        rows = part_shape(w)[0]
        dst = refs[n + w].at[k]
        if part is not None:
            offset, size, _ = _part_rows(part, rows)
            dst = refs[n + w].at[k, pl.ds(offset, size), :]
        return pltpu.make_async_remote_copy(
            src_ref=_region(refs[w], axes[w], shard, None, part_shape(w)[axes[w]], 0, part, rows), dst_ref=dst,
            send_sem=sems[0].at[w, k], recv_sem=sems[1].at[w, k],
            device_id=(chips[k][0], chips[k][1], c), device_id_type=MESH)

    def start(refs, sems):
        for w in range(n):
            for k in range(3):
                copy(refs, sems, w, k).start()

    def finish(refs, sems):
        for w in range(n):
            for k in range(3):
                copy(refs, sems, w, k).wait()

    sems = [pltpu.SemaphoreType.DMA((n, 3))] * 2
    if landing is not None:
        return _Job(ins=sums, ios=landing, sems=sems, start=start, finish=finish)
    return _Job(ins=sums, outs=[jax.ShapeDtypeStruct((3,) + part_shape(w), BF16) for w in range(n)],
                sems=sems, start=start, finish=finish)


def _sum_chip_parts(grad, received, parts, axis, pos, *, name):
    _, rows, cols = parts.shape
    tr = _row_block(rows, cols)
    nb = rows // tr
    blk = (tr, cols)

    def body(pos_ref, g_ref, r_ref, p_ref, o_ref):
        total = g_ref[...] + r_ref[...]
        for k in range(3):
            total = total + p_ref[k].astype(F32)
        o_ref[...] = total

    if axis == 1:
        g_spec = pl.BlockSpec(blk, lambda i, pos: (pos[0] * nb + i, pos[1]))
        r_spec = pl.BlockSpec(blk, lambda i, pos: (i, pos[1]))
        o_spec = pl.BlockSpec(blk, lambda i, pos: (pos[0] * nb + i, 0))
        shard_shape = (2 * rows, cols)
    else:
        g_spec = pl.BlockSpec(blk, lambda i, pos: (pos[1] * nb + i, pos[0]))
        r_spec = pl.BlockSpec(blk, lambda i, pos: (pos[1] * nb + i, 0))
        o_spec = pl.BlockSpec(blk, lambda i, pos: (i, pos[0]))
        shard_shape = (rows, 2 * cols)
    return pl.pallas_call(
        body, name=name, out_shape=jax.ShapeDtypeStruct(shard_shape, F32),
        grid_spec=pltpu.PrefetchScalarGridSpec(
            num_scalar_prefetch=1, grid=(nb,),
            in_specs=[g_spec, r_spec, pl.BlockSpec((3,) + blk, lambda i, pos: (0, i, 0))],
            out_specs=o_spec),
        compiler_params=_params(("parallel",)),
    )(pos, grad, received, parts)


def _join_job(shards, axes):
    n = len(shards)

    def copy(refs, sems, w, other):
        x, y, c, _ = _mesh_position()
        place = _region(refs[w], axes[w], None, 1 - c if other else c, 0, shards[w].shape[1 - axes[w]] // 2)
        return pltpu.make_async_remote_copy(
            src_ref=place, dst_ref=place, send_sem=sems[0].at[w], recv_sem=sems[1].at[w],
            device_id=(x, y, 1 - c), device_id_type=MESH)

    def start(refs, sems):
        for w in range(n):
            copy(refs, sems, w, False).start()

    def finish(refs, sems):
        for w in range(n):
            copy(refs, sems, w, True).wait_recv()
            copy(refs, sems, w, False).wait_send()

    return _Job(ios=shards, sems=[pltpu.SemaphoreType.DMA((n,))] * 2, start=start, finish=finish)


def _all_reduce_small(vec, after=()):
    rows, cols = vec.shape

    def body(v_ref, *rest):
        o_ref, land_ref, send_sem, recv_sem = rest[len(after):]
        x, y, c, _ = _mesh_position()
        me = 4 * x + 2 * y + c
        land_ref[me] = v_ref[...]
        copies = []
        for k in range(1, 8):
            px, py, pc = x ^ (k >> 2), y ^ ((k >> 1) & 1), c ^ (k & 1)
            copies.append(pltpu.make_async_remote_copy(
                src_ref=v_ref, dst_ref=land_ref.at[me], send_sem=send_sem.at[k], recv_sem=recv_sem.at[k],
                device_id=(px, py, pc), device_id_type=MESH))
        for cp in copies:
            cp.start()
        for k in range(1, 8):
            peer = me ^ k
            pltpu.make_async_remote_copy(
                src_ref=v_ref, dst_ref=land_ref.at[peer], send_sem=send_sem.at[k], recv_sem=recv_sem.at[k],
                device_id=(x, y, c), device_id_type=MESH).wait_recv()
        for cp in copies:
            cp.wait_send()
        total = land_ref[0]
        for k in range(1, 8):
            total = total + land_ref[k]
        o_ref[...] = total

    vmem = pl.BlockSpec(memory_space=pltpu.VMEM)
    return pl.pallas_call(
        body, name="all_reduce_small", in_specs=[vmem] + [pl.BlockSpec(memory_space=pl.ANY)] * len(after),
        out_specs=vmem, out_shape=jax.ShapeDtypeStruct((rows, cols), F32),
        scratch_shapes=[pltpu.VMEM((8, rows, cols), F32), pltpu.SemaphoreType.DMA((8,)), pltpu.SemaphoreType.DMA((8,))],
    )(vec, *after)


def _adamw(w, g, m, v, *, name, after=()):
    rows, cols = w.shape
    tr = _row_block(rows, cols) if rows % 8 == 0 else rows
    bc1 = 1.0 - ADAM_B1 ** ADAM_STEP
    bc2 = 1.0 - ADAM_B2 ** ADAM_STEP

    def body(w_ref, g_ref, m_ref, v_ref, *rest):
        go_ref, d_ref, mo_ref, vo_ref = rest[len(after):]
        gv = g_ref[...]
        go_ref[...] = gv
        mn = ADAM_B1 * m_ref[...] + (1.0 - ADAM_B1) * gv
        vn = ADAM_B2 * v_ref[...] + (1.0 - ADAM_B2) * (gv * gv)
        mo_ref[...] = mn
        vo_ref[...] = vn
        d_ref[...] = -ADAM_LR * ((mn / bc1) / (jnp.sqrt(vn / bc2) + ADAM_EPS) + ADAM_WD * w_ref[...])

    blk = pl.BlockSpec((tr, cols), lambda i: (i, 0))
    shape = jax.ShapeDtypeStruct((rows, cols), F32)
    return pl.pallas_call(
        body, name=name, grid=(rows // tr,), in_specs=[blk] * 4 + [pl.BlockSpec(memory_space=pl.ANY)] * len(after),
        out_specs=[blk] * 4, out_shape=[shape] * 4, compiler_params=_params(("parallel",)),
    )(w, g, m, v, *after)


def _to_bf16_in_place(w, axis, pos, *, name, after=None):
    rows, cols = w.shape
    tr = _row_block(rows, cols)
    nb = rows // tr

    def body(pos_ref, w_ref, *rest):
        rest[-1][...] = w_ref[...].astype(BF16)

    if axis == 1:
        o_spec = pl.BlockSpec((tr, cols), lambda i, pos: (i, pos[1]))
        full_shape = (rows, N_CHIPS * cols)
    else:
        o_spec = pl.BlockSpec((tr, cols), lambda i, pos: (pos[1] * nb + i, 0))
        full_shape = (N_CHIPS * rows, cols)
    in_specs = [pl.BlockSpec((tr, cols), lambda i, pos: (i, 0))]
    operands = [pos, w]
    if after is not None:
        in_specs.append(pl.BlockSpec(after.shape, lambda i, pos: (0, 0)))
        operands.append(after)
    return pl.pallas_call(
        body, name=name, out_shape=jax.ShapeDtypeStruct(full_shape, BF16),
        grid_spec=pltpu.PrefetchScalarGridSpec(num_scalar_prefetch=1, grid=(nb,), in_specs=in_specs, out_specs=o_spec),
        compiler_params=_params(("parallel",)),
    )(*operands)


def _split_gather_start(full, axis):
    rows_cols = full.shape

    def body(buf_ref, *rest):
        sems = rest[:6]
        token_ref = rest[7]
        x, y, c, chips = _mesh_position()
        place = _region(buf_ref, axis, 2 * x + y, c, rows_cols[axis] // N_CHIPS, rows_cols[1 - axis] // 2)
        for k in range(3):
            pltpu.make_async_remote_copy(
                src_ref=place, dst_ref=place, send_sem=sems[k], recv_sem=sems[3 + k],
                device_id=(chips[k][0], chips[k][1], c), device_id_type=MESH).start()
        token_ref[...] = jnp.zeros_like(token_ref)

    hbm = pl.BlockSpec(memory_space=pltpu.HBM)
    sem = pl.BlockSpec(memory_space=pltpu.SEMAPHORE)
    res = pl.pallas_call(
        body, name="all_gather_w_in_start",
        out_shape=(*[pltpu.SemaphoreType.DMA(())] * 6, pltpu.HBM(full.shape, full.dtype),
                   jax.ShapeDtypeStruct((8, HEAD_DIM), F32)),
        in_specs=(hbm,), out_specs=(*[sem] * 6, hbm, pl.BlockSpec(memory_space=pltpu.VMEM)),
        input_output_aliases={0: 6},
        compiler_params=pltpu.CompilerParams(has_side_effects=pltpu.SideEffectType.DATAFLOW_SIDE_EFFECTING),
    )(pltpu.with_memory_space_constraint(full, pltpu.HBM))
    return list(res[:6]), res[6], res[7]


def _split_gather_wait(sems, full, axis, peer, after):
    rows_cols = full.shape

    def body(buf_ref, send_sem, recv_sem, *rest):
        x, y, c, chips = _mesh_position()

        def copy(shard):
            place = _region(buf_ref, axis, shard, c, rows_cols[axis] // N_CHIPS, rows_cols[1 - axis] // 2)
            return pltpu.make_async_remote_copy(
                src_ref=place, dst_ref=place, send_sem=send_sem, recv_sem=recv_sem,
                device_id=(chips[peer][0], chips[peer][1], c), device_id_type=MESH)

        copy(2 * x + y).wait_send()
        copy(2 * chips[peer][0] + chips[peer][1]).wait_recv()

    hbm = pl.BlockSpec(memory_space=pltpu.HBM)
    sem = pl.BlockSpec(memory_space=pltpu.SEMAPHORE)
    return pl.pallas_call(
        body, name="all_gather_w_in_wait_%d" % peer, out_shape=pltpu.HBM(full.shape, full.dtype),
        in_specs=(hbm, sem, sem, *[pl.BlockSpec(memory_space=pl.ANY)] * len(after)), out_specs=hbm,
        input_output_aliases={0: 0},
        compiler_params=pltpu.CompilerParams(has_side_effects=pltpu.SideEffectType.DATAFLOW_SIDE_EFFECTING),
    )(full, sems[peer], sems[3 + peer], *after)


def _in_proj_part(n1, w_in, proj, shard_ids, which, *, out_cols):
    m, kdim = n1.shape
    cols = out_cols // N_CHIPS
    tm = _tile(m, 1024)

    def body(ids_ref, a_ref, b_ref, *rest):
        rest[-1][...] = _dot(a_ref[...], b_ref[...]).astype(BF16)

    in_specs = [pl.BlockSpec((tm, kdim), lambda i, ids: (i, 0)),
                pl.BlockSpec((kdim, cols), lambda i, ids: (0, ids[which]))]
    operands = [shard_ids, n1, w_in]
    if proj is not None:
        in_specs.append(pl.BlockSpec(memory_space=pl.ANY))
        operands.append(proj)
    return pl.pallas_call(
        body, name="in_proj_%d" % which, out_shape=jax.ShapeDtypeStruct((m, out_cols), BF16),
        grid_spec=pltpu.PrefetchScalarGridSpec(
            num_scalar_prefetch=1, grid=(m // tm,), in_specs=in_specs,
            out_specs=pl.BlockSpec((tm, cols), lambda i, ids: (i, ids[which]))),
        input_output_aliases={3: 0} if proj is not None else {},
        compiler_params=_params(("parallel",)),
    )(*operands)


BIG = ("w_in", "w_out", "w_gate", "w_up", "w_down")
BIG_AXIS = dict(w_in=1, w_out=0, w_gate=1, w_up=1, w_down=0)
SMALL = ("norm_mix_w", "ret_decay_fwd", "ret_decay_bwd", "ret_norm_w", "norm_ffn_w", "norm_final_w")
ALL_WEIGHTS = ("norm_mix_w", "w_in", "ret_decay_fwd", "ret_decay_bwd", "ret_norm_w", "w_out", "norm_ffn_w",
               "w_gate", "w_up", "w_down", "norm_final_w")
SMALL_ROW = 128 * 8


def _pack_small(small):
    pieces = [jnp.reshape(small["loss"], (1,))] + [jnp.reshape(small[k], (-1,)) for k in SMALL]
    rows = []
    for p in pieces:
        pad = -p.shape[0] % (8 * SMALL_ROW)
        rows.append(jnp.reshape(jnp.pad(p, (0, pad)), (-1, SMALL_ROW)))
    return jnp.concatenate(rows, axis=0)


def _unpack_small(block, like):
    out, row = {}, 0
    for k in ("loss",) + SMALL:
        size = 1 if k == "loss" else like[k].size
        nrows = -(-size // (8 * SMALL_ROW)) * 8
        out[k] = jnp.reshape(block[row:row + nrows], (-1,))[:size]
        row += nrows
    return out


def kernel(x, norm_mix_w, w_in, ret_decay_fwd, ret_decay_bwd, ret_norm_w, w_out, norm_ffn_w, w_gate, w_up, w_down, norm_final_w, loss_target, m_norm_mix_w, m_w_in, m_ret_decay_fwd, m_ret_decay_bwd, m_ret_norm_w, m_w_out, m_norm_ffn_w, m_w_gate, m_w_up, m_w_down, m_norm_final_w, v_norm_mix_w, v_w_in, v_ret_decay_fwd, v_ret_decay_bwd, v_ret_norm_w, v_w_out, v_norm_ffn_w, v_w_gate, v_w_up, v_w_down, v_norm_final_w):
    weights = dict(norm_mix_w=norm_mix_w, w_in=w_in, ret_decay_fwd=ret_decay_fwd, ret_decay_bwd=ret_decay_bwd,
                   ret_norm_w=ret_norm_w, w_out=w_out, norm_ffn_w=norm_ffn_w, w_gate=w_gate, w_up=w_up,
                   w_down=w_down, norm_final_w=norm_final_w)
    m_in = dict(norm_mix_w=m_norm_mix_w, w_in=m_w_in, ret_decay_fwd=m_ret_decay_fwd, ret_decay_bwd=m_ret_decay_bwd,
                ret_norm_w=m_ret_norm_w, w_out=m_w_out, norm_ffn_w=m_norm_ffn_w, w_gate=m_w_gate, w_up=m_w_up,
                w_down=m_w_down, norm_final_w=m_norm_final_w)
    v_in = dict(norm_mix_w=v_norm_mix_w, w_in=v_w_in, ret_decay_fwd=v_ret_decay_fwd, ret_decay_bwd=v_ret_decay_bwd,
                ret_norm_w=v_ret_norm_w, w_out=v_w_out, norm_ffn_w=v_norm_ffn_w, w_gate=v_w_gate, w_up=v_w_up,
                w_down=v_w_down, norm_final_w=v_norm_final_w)
    pos = jnp.stack([lax.axis_index("c"), 2 * lax.axis_index("x") + lax.axis_index("y")]).astype(jnp.int32)

    own = {"w_in": _to_bf16_in_place(weights["w_in"][0], BIG_AXIS["w_in"], pos, name="cast_w_in")}
    w_in_started = _split_gather_start(own["w_in"], BIG_AXIS["w_in"])
    queued, token = {}, w_in_started[2]
    for k in ("w_gate", "w_out", "w_up", "w_down"):
        own[k] = _to_bf16_in_place(weights[k][0], BIG_AXIS[k], pos, name="cast_" + k, after=token)
        queued[k] = _split_start(_gather_job([own[k]], [BIG_AXIS[k]], "ici"), name="all_gather_%s_start" % k)
        token = queued[k]["token"]
    cx, cy = lax.axis_index("x"), lax.axis_index("y")
    shard_ids = jnp.stack([2 * cx + cy, 2 * (1 - cx) + cy, 2 * cx + 1 - cy, 2 * (1 - cx) + 1 - cy]).astype(jnp.int32)

    dx, grad_w, small, w_in_pending = _step(
        x[0], loss_target[0], norm_mix_w, ret_decay_fwd[0], ret_decay_bwd[0], ret_norm_w, norm_ffn_w,
        norm_final_w[None, :], own, w_in_started, queued, shard_ids, pos)

    delta, new_m, new_v = {}, {}, {}

    def update(k, after):
        shape = weights[k].shape
        as2d = (lambda t: jnp.reshape(t, (-1, shape[-1])))
        grad_w[k], delta[k], new_m[k], new_v[k] = (jnp.reshape(t, shape) for t in _adamw(
            as2d(weights[k]), as2d(grad_w[k]), as2d(m_in[k]), as2d(v_in[k]), name="adamw_" + k, after=after))

    others = [k for k in BIG if k != "w_in"]
    for k in others:
        update(k, [w_in_pending["sending"]["token"]])
    _, parts = _split_wait(w_in_pending["sending"], [dx] + [delta[k] for k in others], name="grad_send_w_in_wait")

    half = _sum_chip_parts(w_in_pending["grad"], w_in_pending["received"], parts, BIG_AXIS["w_in"], pos,
                           name="grad_sum_parts_w_in")
    joining = _split_start(_join_job([half], [BIG_AXIS["w_in"]]), name="grad_join_w_in_start")

    like = {k: weights[k] for k in SMALL}
    reduced = _unpack_small(_all_reduce_small(_pack_small(small), after=[joining["token"]]), like)
    loss = reduced["loss"][0]
    for k in SMALL:
        grad_w[k] = jnp.reshape(reduced[k], (1, -1))
        update(k, [])
    (grad_w["w_in"],) = _split_wait(joining, [delta[k] for k in SMALL], name="grad_join_w_in_wait")
    update("w_in", [])

    return (loss, dx[None], *[grad_w[k] for k in ALL_WEIGHTS], *[delta[k] for k in ALL_WEIGHTS],
            *[new_m[k] for k in ALL_WEIGHTS], *[new_v[k] for k in ALL_WEIGHTS])
```

```python
import functools
import math

import numpy as np
import jax
import jax.numpy as jnp
from jax import lax
from jax.experimental import pallas as pl
from jax.experimental.pallas import tpu as pltpu

F32 = jnp.float32
BF16 = jnp.bfloat16
MESH = pl.DeviceIdType.MESH

HEAD_DIM = 128
RET_CHUNK = 128
RET_UNROLL = 8
EPS = 1e-6
DILATED_PATTERNS = ((128, 1), (512, 4), (2048, 16))
ATT_BLOCK = 256
ATT_REACH = max(w // 2 for w, _ in DILATED_PATTERNS)
ATT_NEAR = ATT_BLOCK
ATT_CLASSES = DILATED_PATTERNS[-1][1]
assert all(w // 2 <= ATT_NEAR for w, _ in DILATED_PATTERNS[:-1])
ATT_KB = -(-ATT_NEAR // ATT_BLOCK)
ATT_WINDOW = 2 * ATT_KB + 1
ATT_FAR_GROUP = 8
ATT_NEAR_GROUP = 8
MASKED = -1e30
ROW_MAX_INIT = -1e29
N_CHIPS = 4
VMEM_LIMIT_BYTES = 56 * 1024 * 1024
ELEM_BLOCK_BYTES = 2 * 1024 * 1024
WEIGHT_GRAD_GROUP = 4

ADAM_LR = 0.001
ADAM_B1 = 0.9
ADAM_B2 = 0.999
ADAM_EPS = 1e-08
ADAM_WD = 0.01
ADAM_STEP = 10


def _params(sem=None):
    return pltpu.CompilerParams(dimension_semantics=sem, vmem_limit_bytes=VMEM_LIMIT_BYTES)


def _sigmoid(x):
    return 0.5 * jnp.tanh(0.5 * x) + 0.5


class _Job:
    def __init__(self, *, ins=(), ios=(), outs=(), sems=(), start, finish):
        self.ins, self.ios, self.outs, self.sems = list(ins), list(ios), list(outs), list(sems)
        self.start, self.finish = start, finish

    def results(self):
        return [jax.ShapeDtypeStruct(a.shape, a.dtype) for a in self.ios] + self.outs


def _call(body, *, name, grid, in_specs, out_specs, out_shape, operands, scratch_shapes=(), semantics=None, jobs=(),
          after=(), updates=None):
    in_specs, out_specs, out_shape = list(in_specs), list(out_specs), list(out_shape)
    scratch_shapes = list(scratch_shapes)
    if not jobs:
        n_real = len(in_specs)

        def ordered(*refs):
            body(*refs[:n_real], *refs[n_real + len(after):])

        outs = pl.pallas_call(
            ordered if after else body, name=name, grid=grid,
            in_specs=in_specs + [pl.BlockSpec(memory_space=pl.ANY)] * len(after), out_specs=out_specs,
            out_shape=out_shape, scratch_shapes=scratch_shapes, input_output_aliases=dict(updates or {}),
            compiler_params=_params(semantics))(*operands, *after)
        return outs, []
    n_in, n_out, n_scratch = len(in_specs), len(out_specs), len(scratch_shapes)
    extra_in, extra_out, sems, aliases = [], [], [], dict(updates or {})
    for job in jobs:
        extra_in += job.ins
        for t in range(len(job.ios)):
            aliases[n_in + len(extra_in) + t] = n_out + len(extra_out) + t
        extra_in += job.ios
        extra_out += job.results()
        sems += job.sems

    def carried(*refs):
        x_in = refs[n_in:n_in + len(extra_in)]
        first_out = n_in + len(extra_in) + len(after)
        x_out = refs[first_out + n_out:first_out + n_out + len(extra_out)]
        x_sem = refs[len(refs) - len(sems):]
        views, i_in, i_out, i_sem = [], 0, 0, 0
        for job in jobs:
            data = list(x_in[i_in:i_in + len(job.ins)]) + list(x_out[i_out:i_out + len(job.results())])
            views.append((data, x_sem[i_sem:i_sem + len(job.sems)]))
            i_in += len(job.ins) + len(job.ios)
            i_out += len(job.results())
            i_sem += len(job.sems)
        steps = [pl.program_id(d) for d in range(len(grid))]

        @pl.when(functools.reduce(jnp.logical_and, [s == 0 for s in steps]))
        def _():
            for job, (data, sem) in zip(jobs, views):
                job.start(data, sem)

        body(*refs[:n_in], *refs[first_out:first_out + n_out],
             *refs[len(refs) - len(sems) - n_scratch:len(refs) - len(sems)])

        @pl.when(functools.reduce(jnp.logical_and, [s == g - 1 for s, g in zip(steps, grid)]))
        def _():
            for job, (data, sem) in zip(jobs, views):
                job.finish(data, sem)

    hbm = pl.BlockSpec(memory_space=pl.ANY)
    res = pl.pallas_call(
        carried, name=name, grid=grid, in_specs=in_specs + [hbm] * (len(extra_in) + len(after)),
        out_specs=out_specs + [hbm] * len(extra_out), out_shape=out_shape + extra_out,
        input_output_aliases=aliases, scratch_shapes=scratch_shapes + sems,
        compiler_params=_params(("arbitrary",) * len(grid)),
    )(*operands, *extra_in, *after)
    carried_results, at = [], n_out
    for job in jobs:
        carried_results.append(list(res[at:at + len(job.results())]))
        at += len(job.results())
    return list(res[:n_out]), carried_results


def _run_jobs(jobs, *, name):
    first = jobs[0]
    n_in, n_io = len(first.ins), len(first.ios)
    out_shape = first.results()
    n_sems = [len(job.sems) for job in jobs]

    def body(*refs):
        data = list(refs[:n_in]) + list(refs[n_in + n_io:n_in + n_io + len(out_shape)])
        at = n_in + n_io + len(out_shape)
        for job, ns in zip(jobs, n_sems):
            job.start(data, refs[at:at + ns])
            job.finish(data, refs[at:at + ns])
            at += ns

    hbm = pl.BlockSpec(memory_space=pl.ANY)
    return pl.pallas_call(
        body, name=name, in_specs=[hbm] * (n_in + n_io), out_specs=[hbm] * len(out_shape), out_shape=out_shape,
        input_output_aliases={n_in + t: t for t in range(n_io)},
        scratch_shapes=[s for job in jobs for s in job.sems],
    )(*first.ins, *first.ios)


class _SemaphoreGrid:
    def __init__(self, refs, shape):
        self.refs, self.shape = list(refs), tuple(shape)

    @property
    def at(self):
        return self

    def __getitem__(self, index):
        index = index if isinstance(index, tuple) else (index,)
        flat = 0
        for i, extent in zip(index, self.shape):
            flat = flat * extent + i
        return self.refs[flat]


def _semaphore_grids(job, refs):
    grids, at = [], 0
    for sem in job.sems:
        count = math.prod(sem.shape)
        grids.append(_SemaphoreGrid(refs[at:at + count], sem.shape))
        at += count
    return grids


def _split_start(job, *, name):
    arrays = job.ins + job.ios + [lax.empty(s.shape, s.dtype) for s in job.outs]
    n, ns = len(arrays), sum(math.prod(sem.shape) for sem in job.sems)

    def body(*refs):
        job.start(list(refs[:n]), _semaphore_grids(job, refs[n:n + ns]))
        refs[-1][...] = jnp.zeros_like(refs[-1])

    hbm = pl.BlockSpec(memory_space=pltpu.HBM)
    res = pl.pallas_call(
        body, name=name,
        out_shape=(*[pltpu.SemaphoreType.DMA(())] * ns, *[pltpu.HBM(a.shape, a.dtype) for a in arrays],
                   jax.ShapeDtypeStruct((8, HEAD_DIM), F32)),
        in_specs=[hbm] * n,
        out_specs=(*[pl.BlockSpec(memory_space=pltpu.SEMAPHORE)] * ns, *[hbm] * n,
                   pl.BlockSpec(memory_space=pltpu.VMEM)),
        input_output_aliases={t: ns + t for t in range(n)},
        compiler_params=pltpu.CompilerParams(has_side_effects=pltpu.SideEffectType.DATAFLOW_SIDE_EFFECTING),
    )(*[pltpu.with_memory_space_constraint(a, pltpu.HBM) for a in arrays])
    return dict(job=job, sems=list(res[:ns]), arrays=list(res[ns:ns + n]), token=res[-1])


def _split_wait(started, after, *, name):
    job, arrays, sems = started["job"], started["arrays"], started["sems"]
    n, ns = len(arrays), len(sems)

    def body(*refs):
        job.finish(list(refs[:n]), _semaphore_grids(job, refs[n:n + ns]))

    hbm = pl.BlockSpec(memory_space=pltpu.HBM)
    return pl.pallas_call(
        body, name=name, out_shape=[pltpu.HBM(a.shape, a.dtype) for a in arrays],
        in_specs=[hbm] * n + [pl.BlockSpec(memory_space=pltpu.SEMAPHORE)] * ns
        + [pl.BlockSpec(memory_space=pl.ANY)] * len(after),
        out_specs=[hbm] * n, input_output_aliases={t: t for t in range(n)},
        compiler_params=pltpu.CompilerParams(has_side_effects=pltpu.SideEffectType.DATAFLOW_SIDE_EFFECTING),
    )(*arrays, *sems, *after)


def _dot(a, b, ta=False, tb=False):
    return lax.dot_general(a, b, (((0 if ta else 1,), (1 if tb else 0,)), ((), ())),
                           preferred_element_type=F32)


def _tile(n, want):
    t = min(n, want) // 128 * 128
    while n % t:
        t -= 128
    return t


def _a_spec(ta, tm, tk):
    return pl.BlockSpec((tk, tm), lambda i, j, k: (k, i)) if ta else pl.BlockSpec((tm, tk), lambda i, j, k: (i, k))


def _b_spec(tb, tk, tn):
    return pl.BlockSpec((tn, tk), lambda i, j, k: (j, k)) if tb else pl.BlockSpec((tk, tn), lambda i, j, k: (k, j))


def _accumulate(accs, nk, products, finish):
    if nk == 1:
        finish(*products())
        return
    k = pl.program_id(2)

    @pl.when(k == 0)
    def _():
        for acc, p in zip(accs, products()):
            acc[...] = p

    if nk > 2:
        @pl.when(jnp.logical_and(k > 0, k < nk - 1))
        def _():
            for acc, p in zip(accs, products()):
                acc[...] += p

    @pl.when(k == nk - 1)
    def _():
        finish(*[acc[...] + p for acc, p in zip(accs, products())])


def _matmul(a, b, *, name, ta=False, tb=False, out_dtype=F32, residual=None, tm=1024, tn=1024, tk=2048, jobs=()):
    m, kdim = (a.shape[1], a.shape[0]) if ta else a.shape
    n = b.shape[0] if tb else b.shape[1]
    tm, tn, tk = _tile(m, tm), _tile(n, tn), _tile(kdim, tk)
    nk = kdim // tk

    def body(*refs):
        a_ref, b_ref = refs[:2]
        r_ref = refs[2] if residual is not None else None
        o_ref = refs[-1] if nk == 1 else refs[-2]

        def finish(total):
            if residual is not None:
                total = total + r_ref[...]
            o_ref[...] = total.astype(out_dtype)

        _accumulate(refs[-1:] if nk > 1 else (), nk, lambda: (_dot(a_ref[...], b_ref[...], ta, tb),), finish)

    o_spec = pl.BlockSpec((tm, tn), lambda i, j, k: (i, j))
    in_specs = [_a_spec(ta, tm, tk), _b_spec(tb, tk, tn)]
    operands = [a, b]
    if residual is not None:
        in_specs.append(o_spec)
        operands.append(residual)
    (out,), carried = _call(
        body, name=name, grid=(m // tm, n // tn, nk), in_specs=in_specs, out_specs=[o_spec],
        out_shape=[jax.ShapeDtypeStruct((m, n), out_dtype)], operands=operands,
        scratch_shapes=[pltpu.VMEM((tm, tn), F32)] * (nk > 1),
        semantics=("parallel", "parallel", "arbitrary"), jobs=jobs)
    return (out, carried) if jobs else out


def _matmul_pieces_nt(pieces, b, *, name, tm=512, tn=1024, jobs=(), after=()):
    m, kp = pieces[0].shape
    n = b.shape[0]
    tm, tn = _tile(m, tm), _tile(n, tn)
    count = len(pieces)

    def body(*refs):
        b_ref, o_ref = refs[count], refs[count + 1]
        total = _dot(refs[0][...], b_ref[:, pl.ds(0, kp)], tb=True)
        for p in range(1, count):
            total = total + _dot(refs[p][...], b_ref[:, pl.ds(p * kp, kp)], tb=True)
        o_ref[...] = total

    piece = pl.BlockSpec((tm, kp), lambda j, i: (i, 0))
    (out,), carried = _call(
        body, name=name, grid=(n // tn, m // tm),
        in_specs=[piece] * count + [pl.BlockSpec((tn, count * kp), lambda j, i: (j, 0))],
        out_specs=[pl.BlockSpec((tm, tn), lambda j, i: (i, j))],
        out_shape=[jax.ShapeDtypeStruct((m, n), F32)], operands=[*pieces, b],
        semantics=("parallel", "parallel"), jobs=jobs, after=after)
    return (out, carried) if jobs else out


def _weight_grad_pieces(a, pieces, *, name):
    tokens, m = a.shape
    np_ = pieces[0].shape[1]
    tm = 1024 if m % 1024 == 0 else _tile(m, 1408)
    tn = _tile(np_, 512)
    nb = np_ // tn
    out = None
    for first in range(0, len(pieces), WEIGHT_GRAD_GROUP):
        group = pieces[first:first + WEIGHT_GRAD_GROUP]

        def body(*refs, count=len(group)):
            t_now = pl.program_id(1) // nb
            for t in range(count):
                @pl.when(t_now == t)
                def _(t=t):
                    refs[-1][...] = _dot(refs[0][...], refs[1 + t][...], ta=True)

        def piece_spec(t):
            return pl.BlockSpec((tokens, tn), lambda i, j: (0, jnp.clip(j - t * nb, 0, nb - 1)))

        in_specs = [pl.BlockSpec((tokens, tm), lambda i, j: (0, i))] + [piece_spec(t) for t in range(len(group))]
        operands = [a, *group]
        if out is not None:
            in_specs.append(pl.BlockSpec(memory_space=pl.ANY))
            operands.append(out)
        out = pl.pallas_call(
            body, name="%s_%d" % (name, first), grid=(m // tm, nb * len(group)), in_specs=in_specs,
            out_specs=pl.BlockSpec((tm, tn), lambda i, j, first=first: (i, first * nb + j)),
            out_shape=jax.ShapeDtypeStruct((m, len(pieces) * np_), F32),
            input_output_aliases={len(operands) - 1: 0} if out is not None else {},
            compiler_params=_params(("parallel", "arbitrary")),
        )(*operands)
    return out


def _weight_grad(a, g, *, name, jobs=()):
    tokens, m = a.shape
    tm = 1024 if m % 1024 == 0 else _tile(m, 1408)
    return _matmul(a, g, name=name, ta=True, tm=tm, tn=512, tk=tokens, jobs=jobs)


def _swiglu_fwd(n2, w_gate, w_up, *, tm=1024, tn=512, tk=2048, jobs=()):
    m, kdim = n2.shape
    n = w_gate.shape[1]
    tm, tn, tk = _tile(m, tm), _tile(n, tn), _tile(kdim, tk)
    nk = kdim // tk

    def body(a_ref, g_ref, u_ref, gate_ref, up_ref, act_ref, *acc):
        def products():
            a = a_ref[...]
            return _dot(a, g_ref[...]), _dot(a, u_ref[...])

        def finish(g, u):
            sg = _sigmoid(g)
            silu = g * sg
            gate_ref[...] = (u * sg * (1.0 + g * (1.0 - sg))).astype(BF16)
            up_ref[...] = silu.astype(BF16)
            act_ref[...] = (silu * u).astype(BF16)

        _accumulate(acc, nk, products, finish)

    o_spec = pl.BlockSpec((tm, tn), lambda i, j, k: (i, j))
    o_shape = jax.ShapeDtypeStruct((m, n), BF16)
    return _call(
        body, name="swiglu_fwd", grid=(m // tm, n // tn, nk),
        in_specs=[_a_spec(False, tm, tk), _b_spec(False, tk, tn), _b_spec(False, tk, tn)],
        out_specs=[o_spec] * 3, out_shape=[o_shape] * 3, operands=[n2, w_gate, w_up],
        scratch_shapes=[pltpu.VMEM((tm, tn), F32)] * (2 * (nk > 1)),
        semantics=("parallel", "parallel", "arbitrary"), jobs=jobs)


def _swiglu_bwd_act(dh2, w_down, gate, up, *, tm=1024, tn=512, tk=2048):
    m, kdim = dh2.shape
    n = w_down.shape[0]
    tm, tn, tk = _tile(m, tm), _tile(n, tn), _tile(kdim, tk)
    nk = kdim // tk

    sub = _tile(tn, 256)

    def body(a_ref, b_ref, gate_ref, up_ref, dgate_ref, dup_ref, *acc):
        def finish(dact, cols=slice(None)):
            dup_ref[:, cols] = (dact * up_ref[:, cols].astype(F32)).astype(BF16)
            dgate_ref[:, cols] = (dact * gate_ref[:, cols].astype(F32)).astype(BF16)

        if nk == 1:
            a = a_ref[...]
            for c in range(tn // sub):
                cols = pl.ds(c * sub, sub)
                finish(_dot(a, b_ref[cols, :], tb=True), cols)
        else:
            _accumulate(acc, nk, lambda: (_dot(a_ref[...], b_ref[...], tb=True),), finish)

    o_spec = pl.BlockSpec((tm, tn), lambda i, j, k: (i, j))
    o_shape = jax.ShapeDtypeStruct((m, n), BF16)
    return pl.pallas_call(
        body, name="swiglu_bwd_act", grid=(m // tm, n // tn, nk),
        in_specs=[_a_spec(False, tm, tk), _b_spec(True, tk, tn), o_spec, o_spec],
        out_specs=[o_spec] * 2, out_shape=[o_shape] * 2,
        scratch_shapes=[pltpu.VMEM((tm, tn), F32)] * (nk > 1),
        compiler_params=_params(("parallel", "parallel", "arbitrary")),
    )(dh2, w_down, gate, up)


def _swiglu_bwd_in(dgate, dup, w_gate, w_up, *, tm=1024, tn=1024, tk=1408, jobs=()):
    m, kdim = dgate.shape
    n = w_gate.shape[0]
    tm, tn, tk = _tile(m, tm), _tile(n, tn), _tile(kdim, tk)
    nk = kdim // tk

    def body(a1_ref, a2_ref, b1_ref, b2_ref, o_ref, *acc):
        def product():
            return (_dot(a1_ref[...], b1_ref[...], tb=True) + _dot(a2_ref[...], b2_ref[...], tb=True),)

        def finish(total):
            o_ref[...] = total

        _accumulate(acc, nk, product, finish)

    a_spec, b_spec = _a_spec(False, tm, tk), _b_spec(True, tk, tn)
    (out,), carried = _call(
        body, name="swiglu_bwd_in", grid=(m // tm, n // tn, nk),
        in_specs=[a_spec, a_spec, b_spec, b_spec],
        out_specs=[pl.BlockSpec((tm, tn), lambda i, j, k: (i, j))],
        out_shape=[jax.ShapeDtypeStruct((m, n), F32)], operands=[dgate, dup, w_gate, w_up],
        scratch_shapes=[pltpu.VMEM((tm, tn), F32)] * (nk > 1),
        semantics=("parallel", "parallel", "arbitrary"), jobs=jobs)
    return out, carried


def _row_block(rows, cols):
    tr = min(rows, max(16, ELEM_BLOCK_BYTES // (4 * cols) // 16 * 16))
    while rows % tr:
        tr -= 16
    return tr


def _rmsnorm_fwd(x, g, *, name, after=None):
    s, d = x.shape
    tr = _row_block(s, d)

    def body(x_ref, g_ref, *rest):
        xv = x_ref[...]
        r = lax.rsqrt(jnp.mean(xv * xv, axis=-1, keepdims=True) + EPS)
        rest[-1][...] = (xv * r * g_ref[...]).astype(BF16)

    row = pl.BlockSpec((tr, d), lambda i: (i, 0))
    in_specs = [row, pl.BlockSpec((1, d), lambda i: (0, 0))]
    operands = [x, g]
    if after is not None:
        in_specs.append(pl.BlockSpec(after.shape, lambda i: (0, 0)))
        operands.append(after)
    return pl.pallas_call(
        body, name=name, grid=(s // tr,), in_specs=in_specs,
        out_specs=row, out_shape=jax.ShapeDtypeStruct((s, d), BF16),
        compiler_params=_params(("parallel",)),
    )(*operands)


def _rmsnorm_bwd_rows(xv, gv, dy):
    r = lax.rsqrt(jnp.mean(xv * xv, axis=-1, keepdims=True) + EPS)
    xhat = xv * r
    dxh = dy * gv
    dx = r * (dxh - xhat * jnp.mean(dxh * xhat, axis=-1, keepdims=True))
    return dx, dy * xhat


def _rmsnorm_bwd(dn, x, g, skip, *, name, after=()):
    s, d = x.shape
    tr = _row_block(s, d)

    def body(dn_ref, x_ref, g_ref, skip_ref, *rest):
        dx_ref, dxb_ref, dg_ref = rest[len(after):]
        dx, dgr = _rmsnorm_bwd_rows(x_ref[...], g_ref[...], dn_ref[...])
        dx = dx + skip_ref[...]
        dx_ref[...] = dx
        dxb_ref[...] = dx.astype(BF16)

        @pl.when(pl.program_id(0) == 0)
        def _():
            dg_ref[...] = jnp.zeros_like(dg_ref)

        dg_ref[...] += jnp.sum(dgr, axis=0, keepdims=True)

    row = pl.BlockSpec((tr, d), lambda i: (i, 0))
    vec = pl.BlockSpec((1, d), lambda i: (0, 0))
    return pl.pallas_call(
        body, name=name, grid=(s // tr,),
        in_specs=[row, row, vec, row] + [pl.BlockSpec(memory_space=pl.ANY)] * len(after),
        out_specs=[row, row, vec],
        out_shape=[jax.ShapeDtypeStruct((s, d), F32), jax.ShapeDtypeStruct((s, d), BF16),
                   jax.ShapeDtypeStruct((1, d), F32)],
        compiler_params=_params(("arbitrary",)),
    )(dn, x, g, skip, *after)


def _loss_head(h2, g, target):
    s, d = h2.shape
    tr = _row_block(s, d)

    def body(h_ref, g_ref, t_ref, dh_ref, dhb_ref, dg_ref, loss_ref):
        hv = h_ref[...]
        gv = g_ref[...]
        r = lax.rsqrt(jnp.mean(hv * hv, axis=-1, keepdims=True) + EPS)
        err = hv * r * gv - t_ref[...]
        dx, dgr = _rmsnorm_bwd_rows(hv, gv, err * (1.0 / d))
        dh_ref[...] = dx
        dhb_ref[...] = dx.astype(BF16)

        @pl.when(pl.program_id(0) == 0)
        def _():
            dg_ref[...] = jnp.zeros_like(dg_ref)
            loss_ref[...] = jnp.zeros_like(loss_ref)

        dg_ref[...] += jnp.sum(dgr, axis=0, keepdims=True)
        row_loss = jnp.mean(err * err, axis=-1, keepdims=True)
        loss_ref[...] += 0.5 * jnp.sum(row_loss, axis=0, keepdims=True)

    row = pl.BlockSpec((tr, d), lambda i: (i, 0))
    vec = pl.BlockSpec((1, d), lambda i: (0, 0))
    one = pl.BlockSpec((1, 1), lambda i: (0, 0))
    return pl.pallas_call(
        body, name="loss_head", grid=(s // tr,), in_specs=[row, vec, row],
        out_specs=[row, row, vec, one],
        out_shape=[jax.ShapeDtypeStruct((s, d), F32), jax.ShapeDtypeStruct((s, d), BF16),
                   jax.ShapeDtypeStruct((1, d), F32), jax.ShapeDtypeStruct((1, 1), F32)],
        compiler_params=_params(("arbitrary",)),
    )(h2, g, target)


def _attention_bias_tables():
    k = np.arange(-ATT_KB, ATT_KB + 1)[:, None, None]
    delta = k * ATT_BLOCK + np.arange(ATT_BLOCK)[None, None, :] - np.arange(ATT_BLOCK)[None, :, None]
    dist = np.abs(delta)
    count = np.zeros(delta.shape, np.int32)
    for window, dilation in DILATED_PATTERNS:
        count += (delta % dilation == 0) & (dist <= min(window // 2, ATT_NEAR))
    logc = np.where(count > 0, np.log(np.maximum(count, 1)), MASKED)
    return dist.astype(np.float32), logc.astype(np.float32)


def _far_bias_tables(per_class):
    steps = np.abs(np.arange(per_class)[:, None] - np.arange(per_class)[None, :]) * ATT_CLASSES
    valid = (steps > ATT_NEAR) & (steps <= ATT_REACH)
    return steps.astype(np.float32), np.where(valid, 0.0, MASKED).astype(np.float32)


def _to_classes(x):
    s, cols = x.shape
    return jnp.reshape(jnp.transpose(jnp.reshape(x, (s // ATT_CLASSES, ATT_CLASSES, cols)), (1, 0, 2)), (s, cols))


def _from_classes(x):
    s, cols = x.shape
    return jnp.reshape(jnp.transpose(jnp.reshape(x, (ATT_CLASSES, s // ATT_CLASSES, cols)), (1, 0, 2)), (s, cols))


def _head_bias(bias_ref, slope, dist_ref, logc_ref):
    for kk in range(ATT_WINDOW):
        bias_ref[kk] = logc_ref[kk] - slope * dist_ref[kk]
    bias_ref[ATT_WINDOW] = jnp.full((ATT_BLOCK, ATT_BLOCK), MASKED, F32)


def _window_start(i, nq, nwin):
    return jnp.clip(i - ATT_KB, 0, nq - nwin)


def _window_block(j, i):
    rows = pl.ds(pl.multiple_of(j * ATT_BLOCK, ATT_BLOCK), ATT_BLOCK)
    kk = j - i + ATT_KB
    return rows, jnp.where(jnp.logical_and(kk >= 0, kk < ATT_WINDOW), kk, ATT_WINDOW)


def _attention_far_fwd(qkv, slopes, n_heads, jobs=(), after=()):
    s = qkv.shape[0]
    per_class = s // ATT_CLASSES
    scale = HEAD_DIM ** -0.5
    dist, logc = _far_bias_tables(per_class)

    def body(slope_ref, q_ref, k_ref, v_ref, dist_ref, logc_ref, o_ref, lse_ref):
        bias = logc_ref[...] - slope_ref[pl.program_id(0)] * dist_ref[...]
        for a in range(ATT_FAR_GROUP):
            rows = pl.ds(a * per_class, per_class)
            sc = _dot(q_ref[rows, :], k_ref[rows, :], tb=True) * scale + bias
            m = jnp.maximum(jnp.max(sc, axis=-1, keepdims=True), ROW_MAX_INIT)
            p = jnp.exp(sc - m)
            l = jnp.maximum(jnp.sum(p, axis=-1, keepdims=True), 1e-30)
            o_ref[rows, :] = (_dot(p.astype(BF16), v_ref[rows, :]) / l).astype(BF16)
            lse_ref[rows, :] = jnp.broadcast_to(m + jnp.log(l), (per_class, HEAD_DIM))

    hh = n_heads
    blk = pl.BlockSpec((ATT_FAR_GROUP * per_class, HEAD_DIM), lambda h, r: (r, h))
    table = pl.BlockSpec(dist.shape, lambda h, r: (0, 0))
    return _call(
        body, name="attention_far_fwd", grid=(hh, ATT_CLASSES // ATT_FAR_GROUP),
        in_specs=[pl.BlockSpec(memory_space=pltpu.SMEM), blk,
                  pl.BlockSpec((ATT_FAR_GROUP * per_class, HEAD_DIM), lambda h, r: (r, hh + h)),
                  pl.BlockSpec((ATT_FAR_GROUP * per_class, HEAD_DIM), lambda h, r: (r, 2 * hh + h)), table, table],
        out_specs=[blk, blk],
        out_shape=[jax.ShapeDtypeStruct((s, hh * HEAD_DIM), BF16), jax.ShapeDtypeStruct((s, hh * HEAD_DIM), F32)],
        operands=[slopes, qkv, qkv, qkv, jnp.asarray(dist), jnp.asarray(logc)],
        semantics=("parallel", "parallel"), jobs=jobs, after=after)


def _attention_fwd(proj, slopes, far_out, far_lse, n_heads, jobs=(), after=()):
    s = proj.shape[0]
    nq = s // ATT_BLOCK
    scale = HEAD_DIM ** -0.5
    dist, logc = _attention_bias_tables()

    nwin = min(ATT_WINDOW, nq)

    group = math.gcd(ATT_NEAR_GROUP, nq)

    def body(slope_ref, q_ref, k_ref, v_ref, fo_ref, fl_ref, dist_ref, logc_ref, o_ref, lse_ref, bias_ref, s_ref):
        h, step = pl.program_id(0), pl.program_id(1)

        @pl.when(step == 0)
        def _():
            _head_bias(bias_ref, slope_ref[h], dist_ref, logc_ref)

        for a in range(group):
            i = step * group + a
            mine = pl.ds(a * ATT_BLOCK, ATT_BLOCK)
            q = q_ref[mine, :]
            first = _window_start(i, nq, nwin)
            m = jnp.full((ATT_BLOCK, 1), ROW_MAX_INIT, F32)
            for b in range(nwin):
                rows, kk = _window_block(first + b, i)
                sc = _dot(q, k_ref[rows, :], tb=True) * scale + bias_ref[kk]
                s_ref[a * nwin + b] = sc
                m = jnp.maximum(m, jnp.max(sc, axis=-1, keepdims=True))
            l = jnp.zeros((ATT_BLOCK, 1), F32)
            acc = jnp.zeros((ATT_BLOCK, HEAD_DIM), F32)
            for b in range(nwin):
                rows, _ = _window_block(first + b, i)
                p = jnp.exp(s_ref[a * nwin + b] - m)
                l = l + jnp.sum(p, axis=-1, keepdims=True)
                acc = acc + _dot(p.astype(BF16), v_ref[rows, :])
            near_lse = m + jnp.log(l)
            far_lse_col = fl_ref[mine, :1]
            lse = jnp.maximum(near_lse, far_lse_col)
            lse = lse + jnp.log(jnp.exp(near_lse - lse) + jnp.exp(far_lse_col - lse))
            o_ref[mine, :] = (acc * (jnp.exp(near_lse - lse) / l)
                              + fo_ref[mine, :].astype(F32) * jnp.exp(far_lse_col - lse)).astype(BF16)
            lse_ref[mine, :] = jnp.broadcast_to(lse, (ATT_BLOCK, HEAD_DIM))

    hh = n_heads
    blk = pl.BlockSpec((group * ATT_BLOCK, HEAD_DIM), lambda h, i: (i, h))
    table = pl.BlockSpec(dist.shape, lambda h, i: (0, 0, 0))
    return _call(
        body, name="attention_fwd", grid=(hh, nq // group),
        in_specs=[pl.BlockSpec(memory_space=pltpu.SMEM), blk,
                  pl.BlockSpec((s, HEAD_DIM), lambda h, i: (0, hh + h)),
                  pl.BlockSpec((s, HEAD_DIM), lambda h, i: (0, 2 * hh + h)), blk, blk, table, table],
        out_specs=[blk, blk],
        out_shape=[jax.ShapeDtypeStruct((s, hh * HEAD_DIM), BF16), jax.ShapeDtypeStruct((s, hh * HEAD_DIM), F32)],
        operands=[slopes, proj, proj, proj, far_out, far_lse, jnp.asarray(dist), jnp.asarray(logc)],
        scratch_shapes=[pltpu.VMEM((ATT_WINDOW + 1, ATT_BLOCK, ATT_BLOCK), F32),
                        pltpu.VMEM((group * nwin, ATT_BLOCK, ATT_BLOCK), F32)],
        semantics=("parallel", "arbitrary"), jobs=jobs, after=after)


def _attention_far_bwd(qkv, slopes, out, dout, lse, n_heads):
    s = qkv.shape[0]
    per_class = s // ATT_CLASSES
    scale = HEAD_DIM ** -0.5
    dist, logc = _far_bias_tables(per_class)

    def body(slope_ref, q_ref, k_ref, v_ref, o_ref, do_ref, lse_ref, dist_ref, logc_ref, dq_ref, dk_ref, dv_ref):
        bias = logc_ref[...] - slope_ref[pl.program_id(0)] * dist_ref[...]
        for a in range(ATT_FAR_GROUP):
            rows = pl.ds(a * per_class, per_class)
            q, k, do = q_ref[rows, :], k_ref[rows, :], do_ref[rows, :]
            delta = jnp.sum(do.astype(F32) * o_ref[rows, :].astype(F32), axis=-1, keepdims=True)
            p = jnp.exp(_dot(q, k, tb=True) * scale + bias - lse_ref[rows, :1])
            dv_ref[rows, :] = _dot(p.astype(BF16), do, ta=True).astype(BF16)
            ds = (p * (_dot(do, v_ref[rows, :], tb=True) - delta) * scale).astype(BF16)
            dk_ref[rows, :] = _dot(ds, q, ta=True).astype(BF16)
            dq_ref[rows, :] = _dot(ds, k).astype(BF16)

    hh = n_heads
    blk = pl.BlockSpec((ATT_FAR_GROUP * per_class, HEAD_DIM), lambda h, r: (r, h))
    table = pl.BlockSpec(dist.shape, lambda h, r: (0, 0))
    o_shape = jax.ShapeDtypeStruct((s, hh * HEAD_DIM), BF16)
    return pl.pallas_call(
        body, name="attention_far_bwd", grid=(hh, ATT_CLASSES // ATT_FAR_GROUP),
        in_specs=[pl.BlockSpec(memory_space=pltpu.SMEM), blk,
                  pl.BlockSpec((ATT_FAR_GROUP * per_class, HEAD_DIM), lambda h, r: (r, hh + h)),
                  pl.BlockSpec((ATT_FAR_GROUP * per_class, HEAD_DIM), lambda h, r: (r, 2 * hh + h)),
                  blk, blk, blk, table, table],
        out_specs=[blk] * 3, out_shape=[o_shape] * 3,
        compiler_params=_params(("parallel", "parallel")),
    )(slopes, qkv, qkv, qkv, out, dout, lse, jnp.asarray(dist), jnp.asarray(logc))


def _attention_bwd(proj, slopes, out, lse, dmixed, far_grads, n_heads, jobs=()):
    s = proj.shape[0]
    nq = s // ATT_BLOCK
    scale = HEAD_DIM ** -0.5
    dist, logc = _attention_bias_tables()

    nwin = min(ATT_WINDOW, nq)
    group = math.gcd(ATT_NEAR_GROUP, nq)

    def body(slope_ref, q_ref, k_ref, v_ref, o_ref, do_ref, lse_ref, fdq_ref, fdk_ref, fdv_ref, dist_ref, logc_ref,
             dq_ref, dk_ref, dv_ref, dk_acc, dv_acc, bias_ref):
        h, step = pl.program_id(0), pl.program_id(1)

        @pl.when(step == 0)
        def _():
            dk_acc[...] = jnp.zeros_like(dk_acc)
            dv_acc[...] = jnp.zeros_like(dv_acc)
            _head_bias(bias_ref, slope_ref[h], dist_ref, logc_ref)

        for a in range(group):
            i = step * group + a
            mine = pl.ds(a * ATT_BLOCK, ATT_BLOCK)
            q = q_ref[mine, :]
            do = do_ref[mine, :]
            lse_col = lse_ref[mine, :1]
            delta = jnp.sum(do.astype(F32) * o_ref[mine, :].astype(F32), axis=-1, keepdims=True)
            first = _window_start(i, nq, nwin)
            dq = jnp.zeros((ATT_BLOCK, HEAD_DIM), F32)
            for b in range(nwin):
                rows, kk = _window_block(first + b, i)
                kj = k_ref[rows, :]
                vj = v_ref[rows, :]
                p = jnp.exp(_dot(q, kj, tb=True) * scale + bias_ref[kk] - lse_col)
                dv_acc[rows, :] += _dot(p.astype(BF16), do, ta=True)
                dp = _dot(do, vj, tb=True)
                ds = (p * (dp - delta) * scale).astype(BF16)
                dk_acc[rows, :] += _dot(ds, q, ta=True)
                dq = dq + _dot(ds, kj)
            dq_ref[mine, :] = (dq + fdq_ref[mine, :].astype(F32)).astype(BF16)

        @pl.when(step == nq // group - 1)
        def _():
            dk_ref[...] = (dk_acc[...] + fdk_ref[...].astype(F32)).astype(BF16)
            dv_ref[...] = (dv_acc[...] + fdv_ref[...].astype(F32)).astype(BF16)

    hh = n_heads
    blk = pl.BlockSpec((group * ATT_BLOCK, HEAD_DIM), lambda h, i: (i, h))
    col = pl.BlockSpec((s, HEAD_DIM), lambda h, i: (0, h))
    table = pl.BlockSpec(dist.shape, lambda h, i: (0, 0, 0))
    o_shape = jax.ShapeDtypeStruct((s, hh * HEAD_DIM), BF16)
    return _call(
        body, name="attention_bwd", grid=(hh, nq // group),
        in_specs=[pl.BlockSpec(memory_space=pltpu.SMEM), blk,
                  pl.BlockSpec((s, HEAD_DIM), lambda h, i: (0, hh + h)),
                  pl.BlockSpec((s, HEAD_DIM), lambda h, i: (0, 2 * hh + h)),
                  blk, blk, blk, blk, col, col, table, table],
        out_specs=[blk, col, col], out_shape=[o_shape] * 3,
        operands=[slopes, proj, proj, proj, out, dmixed, lse, *far_grads, jnp.asarray(dist), jnp.asarray(logc)],
        scratch_shapes=[pltpu.VMEM((s, HEAD_DIM), F32)] * 2
        + [pltpu.VMEM((ATT_WINDOW + 1, ATT_BLOCK, ATT_BLOCK), F32)],
        semantics=("parallel", "arbitrary"), jobs=jobs)


def _ret_decays(lgc, lga, strict_c, strict_a):
    c = RET_CHUNK
    rel = (lax.broadcasted_iota(jnp.int32, (c, c), 0) - lax.broadcasted_iota(jnp.int32, (c, c), 1)).astype(F32)
    in_c = (rel > 0) if strict_c else (rel >= 0)
    in_a = (rel < 0) if strict_a else (rel <= 0)
    mask = (jnp.where(in_c, jnp.exp(lgc * jnp.maximum(rel, 0.0)), 0.0)
            + jnp.where(in_a, jnp.exp(lga * jnp.maximum(-rel, 0.0)), 0.0))
    idx = lax.broadcasted_iota(jnp.int32, (c, 1), 0).astype(F32)
    ones = jnp.ones((1, HEAD_DIM), F32)
    dec = dict(
        rel=rel, mask=mask, idx=idx,
        a_c=jnp.exp(lgc * (idx + 1.0)), b_c=jnp.exp(lgc * (c - 1.0 - idx)), chunk_c=jnp.exp(ones * (lgc * c)),
        a_a=jnp.exp(lga * (c - idx)), b_a=jnp.exp(lga * idx), chunk_a=jnp.exp(ones * (lga * c)),
    )
    return dec


def _scaled(x, col):
    return (x.astype(F32) * col).astype(BF16)


def _chunk_rows(i):
    return pl.ds(pl.multiple_of(i * RET_CHUNK, RET_CHUNK), RET_CHUNK)


def _chunk_loop(nc, step, init, unroll=RET_UNROLL):
    group = math.gcd(nc, unroll)

    def trip(t, carry):
        for u in range(group):
            carry = step(t * group + u, carry)
        return carry

    return lax.fori_loop(0, nc // group, trip, init)


def _retention(a, b, c, lg_c, lg_a, *, strict_c, strict_a, scale, n_heads, name, gate=None, norm_w=None, jobs=(),
               heads=None, so_far=None, after=()):
    s = a[0].shape[0]
    nc = s // RET_CHUNK
    epilogue = gate is not None
    first_head, head_count = heads if heads is not None else (0, n_heads)

    def body(*refs):
        lgc_ref, lga_ref, a_ref, b_ref, c_ref = refs[:5]
        if epilogue:
            g_ref, w_ref = refs[5:7]
            o_ref, mix_ref, sa_ref = refs[-3:]
        else:
            o_ref, sa_ref = refs[-2:]
        h = first_head + pl.program_id(0)
        dec = _ret_decays(lgc_ref[h], lga_ref[h], strict_c, strict_a)

        def reverse(t, state):
            i = nc - 1 - t
            sa_ref[i] = state.astype(BF16)
            rows = _chunk_rows(i)
            return state * dec["chunk_a"] + _dot(_scaled(b_ref[rows, :], dec["b_a"]), c_ref[rows, :], ta=True)

        _chunk_loop(nc, reverse, jnp.zeros((HEAD_DIM, HEAD_DIM), F32))

        def forward(i, state):
            rows = _chunk_rows(i)
            ai, bi, ci = a_ref[rows, :], b_ref[rows, :], c_ref[rows, :]
            inner = (_dot(ai, bi, tb=True) * dec["mask"]).astype(BF16)
            out = (_dot(inner, ci) + _dot(_scaled(ai, dec["a_c"]), state.astype(BF16))
                   + _dot(_scaled(ai, dec["a_a"]), sa_ref[i])) * scale
            o_ref[rows, :] = out.astype(BF16)
            if epilogue:
                r = lax.rsqrt(jnp.mean(out * out, axis=-1, keepdims=True) + EPS)
                g = g_ref[rows, :].astype(F32)
                mix_ref[rows, :] = (out * r * w_ref[...] * (g * _sigmoid(g))).astype(BF16)
            return state * dec["chunk_c"] + _dot(_scaled(bi, dec["b_c"]), ci, ta=True)

        _chunk_loop(nc, forward, jnp.zeros((HEAD_DIM, HEAD_DIM), F32))

    def col(first):
        return pl.BlockSpec((s, HEAD_DIM), lambda h: (0, first + first_head + h))

    smem = pl.BlockSpec(memory_space=pltpu.SMEM)
    in_specs = [smem, smem, col(a[1]), col(b[1]), col(c[1])]
    operands = [lg_c, lg_a, a[0], b[0], c[0]]
    o_shape = jax.ShapeDtypeStruct((s, n_heads * HEAD_DIM), BF16)
    out_specs, out_shape = [col(0)], [o_shape]
    if epilogue:
        in_specs += [col(gate[1]), pl.BlockSpec((1, HEAD_DIM), lambda h: (0, first_head + h))]
        operands += [gate[0], norm_w]
        out_specs, out_shape = [col(0)] * 2, [o_shape] * 2
    updates = None
    if so_far is not None:
        updates = {len(operands) + t: t for t in range(len(so_far))}
        in_specs += [pl.BlockSpec(memory_space=pl.ANY)] * len(so_far)
        operands += list(so_far)
    res, carried = _call(
        body, name=name, grid=(head_count,), in_specs=in_specs, out_specs=out_specs, out_shape=out_shape,
        operands=operands, scratch_shapes=[pltpu.VMEM((nc, HEAD_DIM, HEAD_DIM), BF16)],
        semantics=("parallel",), jobs=jobs, updates=updates, after=after)
    res = res if epilogue else res[0]
    return (res, carried) if jobs else res


def _retention_decay_grads(a, b, c, e, lg_c, lg_a, *, scale, n_heads):
    s = a[0].shape[0]
    nc = s // RET_CHUNK
    cf = float(RET_CHUNK)

    def body(lgc_ref, lga_ref, a_ref, b_ref, c_ref, e_ref, gc_ref, ga_ref, sa_ref, ta_ref):
        h = pl.program_id(0)
        lgc, lga = lgc_ref[h], lga_ref[h]
        dec = _ret_decays(lgc, lga, True, True)
        rel, idx = dec["rel"], dec["idx"]
        w_c = jnp.where(rel > 0, rel * jnp.exp(lgc * jnp.maximum(rel, 0.0)), 0.0)
        w_a = jnp.where(rel < 0, -rel * jnp.exp(lga * jnp.maximum(-rel, 0.0)), 0.0)
        zero = jnp.zeros((HEAD_DIM, HEAD_DIM), F32)

        def reverse(t, carry):
            st, dst = carry
            i = nc - 1 - t
            sa_ref[i] = st.astype(BF16)
            ta_ref[i] = dst.astype(BF16)
            rows = _chunk_rows(i)
            bi, ci = b_ref[rows, :], c_ref[rows, :]
            st_new = st * dec["chunk_a"] + _dot(_scaled(bi, dec["b_a"]), ci, ta=True)
            dst_new = (cf * st + dst) * dec["chunk_a"] + _dot(_scaled(bi, idx * dec["b_a"]), ci, ta=True)
            return st_new, dst_new

        _chunk_loop(nc, reverse, (zero, zero))

        def forward(i, carry):
            st, dst, acc_c, acc_a = carry
            rows = _chunk_rows(i)
            ai, bi, ci = a_ref[rows, :], b_ref[rows, :], c_ref[rows, :]
            ev = e_ref[rows, :].astype(F32)
            pg = _dot(ai, bi, tb=True) * _dot(e_ref[rows, :], ci, tb=True)
            a_c, a_a = _scaled(ai, dec["a_c"]), _scaled(ai, dec["a_a"])
            inter_c = _dot(a_c, st.astype(BF16)) * (idx + 1.0) + _dot(a_c, dst.astype(BF16))
            inter_a = _dot(a_a, sa_ref[i]) * (cf - idx) + _dot(a_a, ta_ref[i])
            acc_c = acc_c + jnp.sum(pg * w_c, axis=0, keepdims=True) + jnp.sum(inter_c * ev, axis=0, keepdims=True)
            acc_a = acc_a + jnp.sum(pg * w_a, axis=0, keepdims=True) + jnp.sum(inter_a * ev, axis=0, keepdims=True)
            st_new = st * dec["chunk_c"] + _dot(_scaled(bi, dec["b_c"]), ci, ta=True)
            dst_new = ((cf * st + dst) * dec["chunk_c"]
                       + _dot(_scaled(bi, (cf - 1.0 - idx) * dec["b_c"]), ci, ta=True))
            return st_new, dst_new, acc_c, acc_a

        row = jnp.zeros((1, HEAD_DIM), F32)
        _, _, acc_c, acc_a = _chunk_loop(nc, forward, (zero, zero, row, row))
        gc_ref[...] = jnp.broadcast_to(jnp.sum(acc_c, axis=-1, keepdims=True) * scale, gc_ref.shape)
        ga_ref[...] = jnp.broadcast_to(jnp.sum(acc_a, axis=-1, keepdims=True) * scale, ga_ref.shape)

    def col(first):
        return pl.BlockSpec((s, HEAD_DIM), lambda h: (0, first + h))

    smem = pl.BlockSpec(memory_space=pltpu.SMEM)
    o_spec = pl.BlockSpec((1, 8, HEAD_DIM), lambda h: (h, 0, 0))
    o_shape = jax.ShapeDtypeStruct((n_heads, 8, HEAD_DIM), F32)
    gc, ga = pl.pallas_call(
        body, name="retention_decay_grads", grid=(n_heads,),
        in_specs=[smem, smem, col(a[1]), col(b[1]), col(c[1]), col(e[1])],
        out_specs=[o_spec] * 2, out_shape=[o_shape] * 2,
        scratch_shapes=[pltpu.VMEM((nc, HEAD_DIM, HEAD_DIM), BF16)] * 2,
        compiler_params=_params(("parallel",)),
    )(lg_c, lg_a, a[0], b[0], c[0], e[0])
    return gc[:, 0, 0], ga[:, 0, 0]


def _ret_gate_bwd(dmixed, first_col, out, proj, gate_col, norm_w, n_heads):
    s = out.shape[0]
    tr = _row_block(s, 8 * HEAD_DIM)

    def body(dm_ref, o_ref, g_ref, w_ref, do_ref, dg_ref, dw_ref):
        dm = dm_ref[...].astype(F32)
        ov = o_ref[...].astype(F32)
        g = g_ref[...].astype(F32)
        w = w_ref[...]
        r = lax.rsqrt(jnp.mean(ov * ov, axis=-1, keepdims=True) + EPS)
        ohat = ov * r
        sg = _sigmoid(g)
        silu = g * sg
        dg_ref[...] = (dm * ohat * w * sg * (1.0 + g * (1.0 - sg))).astype(BF16)
        dohat = dm * w * silu
        do_ref[...] = (r * (dohat - ohat * jnp.mean(dohat * ohat, axis=-1, keepdims=True))).astype(BF16)

        @pl.when(pl.program_id(1) == 0)
        def _():
            dw_ref[...] = jnp.zeros_like(dw_ref)

        dw_ref[...] += jnp.sum(dm * ohat * silu, axis=0, keepdims=True)

    def blk(first):
        return pl.BlockSpec((tr, HEAD_DIM), lambda h, i: (i, first + h))

    vec = pl.BlockSpec((1, HEAD_DIM), lambda h, i: (0, h))
    o_shape = jax.ShapeDtypeStruct((s, n_heads * HEAD_DIM), BF16)
    return pl.pallas_call(
        body, name="ret_gate_bwd", grid=(n_heads, s // tr),
        in_specs=[blk(first_col), blk(0), blk(gate_col), vec],
        out_specs=[blk(0), blk(0), vec],
        out_shape=[o_shape, o_shape, jax.ShapeDtypeStruct((1, n_heads * HEAD_DIM), F32)],
        compiler_params=_params(("parallel", "arbitrary")),
    )(dmixed, out, proj, norm_w)


def _step(x, target, norm_mix_w, ret_decay_fwd, ret_decay_bwd, ret_norm_w, norm_ffn_w, norm_final_w, own,
          w_in_started, queued, shard_ids, pos):
    d = x.shape[1]
    nh = d // (2 * HEAD_DIM)
    scale = HEAD_DIM ** -0.5
    slopes = jnp.exp2(-8.0 * jnp.arange(1, nh + 1, dtype=F32) / nh)
    lg_f = -jnp.exp(ret_decay_fwd)
    lg_b = -jnp.exp(ret_decay_bwd)
    q_r, k_r, v_r, g_r = 3 * nh, 4 * nh, 5 * nh, 6 * nh
    ax = BIG_AXIS

    def gather(names, arrays, stage, part=None, peers=(0, 1, 2)):
        return _gather_job(arrays, [ax[k] for k in names], stage, part, peers)

    def add_halves(k, g, received):
        return _add_halves(g, received, ax[k], pos, name="grad_add_halves_" + k)

    def sum_parts(k, g, received, parts):
        return _sum_chip_parts(g, received, parts, ax[k], pos, name="grad_sum_parts_" + k)

    sems, w_in, token = w_in_started
    n1 = _rmsnorm_fwd(x, norm_mix_w, name="norm_mix_fwd", after=token)
    proj = _in_proj_part(n1, w_in, None, shard_ids, 0, out_cols=w_in.shape[1])
    for peer in range(3):
        behind = [proj] + ([queued["w_down"]["token"]] if peer == 0 else [])
        w_in = _split_gather_wait(sems, w_in, ax["w_in"], peer, behind)
        (w_in,) = _run_jobs([gather(["w_in"], [w_in], "d2d", peers=(peer,))], name="all_gather_w_in_sibling_%d" % peer)
        proj = _in_proj_part(n1, w_in, proj, shard_ids, 1 + peer, out_cols=w_in.shape[1])
    (w_gate,) = _split_wait(queued["w_gate"], [proj], name="all_gather_w_gate_wait")
    qkv_classes = _to_classes(proj[:, :3 * nh * HEAD_DIM])
    (ret, ret_mixed), [[w_gate]] = _retention(
        (proj, q_r), (proj, k_r), (proj, v_r), lg_f, lg_b, strict_c=False, strict_a=True, scale=scale, n_heads=nh,
        name="retention_fwd_first", gate=(proj, g_r), norm_w=ret_norm_w, heads=(0, nh // 2),
        jobs=[gather(["w_gate"], [w_gate], "d2d")])
    (far_out, far_lse), _ = _attention_far_fwd(qkv_classes, slopes, nh, after=[ret_mixed])
    ret, ret_mixed = _retention(
        (proj, q_r), (proj, k_r), (proj, v_r), lg_f, lg_b, strict_c=False, strict_a=True, scale=scale, n_heads=nh,
        name="retention_fwd_second", gate=(proj, g_r), norm_w=ret_norm_w, heads=(nh // 2, nh - nh // 2),
        so_far=[ret, ret_mixed], after=[far_out])
    (w_out,) = _split_wait(queued["w_out"], [ret_mixed], name="all_gather_w_out_wait")
    (attn, lse), [[w_out]] = _attention_fwd(
        proj, slopes, _from_classes(far_out), _from_classes(far_lse), nh, after=[ret_mixed],
        jobs=[gather(["w_out"], [w_out], "d2d")])
    mixed = jnp.concatenate([attn, ret_mixed], axis=1)
    (w_up,) = _split_wait(queued["w_up"], [mixed], name="all_gather_w_up_wait")
    h1, [[w_up]] = _matmul(mixed, w_out, name="out_proj", residual=x, jobs=[gather(["w_up"], [w_up], "d2d")])
    n2 = _rmsnorm_fwd(h1, norm_ffn_w, name="norm_ffn_fwd")
    (w_down,) = _split_wait(queued["w_down"], [n2], name="all_gather_w_down_wait")
    (gate, up, act), [[w_down]] = _swiglu_fwd(n2, w_gate, w_up, jobs=[gather(["w_down"], [w_down], "d2d")])
    h2 = _matmul(act, w_down, name="down_proj", residual=h1, tk=2816)
    dh2, dh2_b, d_norm_final, loss = _loss_head(h2, norm_final_w, target)

    dgate, dup = _swiglu_bwd_act(dh2_b, w_down, gate, up)
    g_down = _weight_grad(act, dh2_b, name="grad_w_down")
    g_gate, [[r_down]] = _weight_grad(n2, dgate, name="grad_w_gate", jobs=[_exchange_job([g_down], [ax["w_down"]])])
    s_down = add_halves("w_down", g_down, r_down)
    g_up, [[r_gate], [p_down]] = _weight_grad(
        n2, dup, name="grad_w_up",
        jobs=[_exchange_job([g_gate], [ax["w_gate"]]), _send_sums_job([s_down], [ax["w_down"]], (0, 1, 2))])
    s_gate = add_halves("w_gate", g_gate, r_gate)
    dn2, [[r_up], [p_gate], [p_down]] = _swiglu_bwd_in(
        dgate, dup, w_gate, w_up,
        jobs=[_exchange_job([g_up], [ax["w_up"]]), _send_sums_job([s_gate], [ax["w_gate"]]),
              _send_sums_job([s_down], [ax["w_down"]], (1, 1, 2), landing=[p_down])])
    h_down = sum_parts("w_down", g_down, r_down, p_down)
    s_up = add_halves("w_up", g_up, r_up)
    h_gate = sum_parts("w_gate", g_gate, r_gate, p_gate)
    dh1, dh1_b, d_norm_ffn = _rmsnorm_bwd(dn2, h1, norm_ffn_w, dh2, name="norm_ffn_bwd")

    dmixed, [[gr_down], [p_up]] = _matmul(
        dh1_b, w_out, name="out_proj_bwd", tb=True, out_dtype=BF16,
        jobs=[_join_job([h_down], [ax["w_down"]]), _send_sums_job([s_up], [ax["w_up"]], (0, 1, 4))])
    far_in = [_to_classes(t) for t in (attn, dmixed[:, :nh * HEAD_DIM], lse)]
    g_out, [[p_up]] = _weight_grad(mixed, dh1_b, name="grad_w_out",
                                   jobs=[_send_sums_job([s_up], [ax["w_up"]], (1, 1, 4), landing=[p_up])])
    d_ret, dg_r, d_ret_norm = _ret_gate_bwd(dmixed, nh, ret, proj, g_r, ret_norm_w, nh)
    far_grads = _attention_far_bwd(qkv_classes, slopes, *far_in, nh)
    far_grads = [_from_classes(t) for t in far_grads]
    dq_r, [[gr_gate], [p_up]] = _retention(
        (d_ret, 0), (proj, v_r), (proj, k_r), lg_f, lg_b, strict_c=False, strict_a=True, scale=scale, n_heads=nh,
        name="retention_dq",
        jobs=[_join_job([h_gate], [ax["w_gate"]]), _send_sums_job([s_up], [ax["w_up"]], (2, 1, 4), landing=[p_up])])
    (dq_a, dk_a, dv_a), [[p_up], [r_out]] = _attention_bwd(
        proj, slopes, attn, lse, dmixed, far_grads, nh,
        jobs=[_send_sums_job([s_up], [ax["w_up"]], (3, 1, 4), landing=[p_up]),
              _exchange_job([g_out], [ax["w_out"]])])
    s_out = add_halves("w_out", g_out, r_out)
    h_up = sum_parts("w_up", g_up, r_up, p_up)
    dv_r, [[p_out], [gr_up]] = _retention(
        (proj, k_r), (proj, q_r), (d_ret, 0), lg_b, lg_f, strict_c=True, strict_a=False, scale=scale, n_heads=nh,
        name="retention_dv", jobs=[_send_sums_job([s_out], [ax["w_out"]]), _join_job([h_up], [ax["w_up"]])])
    h_out = sum_parts("w_out", g_out, r_out, p_out)
    dk_r, [[gr_out]] = _retention(
        (proj, v_r), (d_ret, 0), (proj, q_r), lg_b, lg_f, strict_c=True, strict_a=False, scale=scale, n_heads=nh,
        name="retention_dk", jobs=[_join_job([h_out], [ax["w_out"]])])
    dlg_f, dlg_b = _retention_decay_grads((proj, q_r), (proj, k_r), (proj, v_r), (d_ret, 0), lg_f, lg_b,
                                          scale=scale, n_heads=nh)
    dproj = [dq_a, dk_a, dv_a, dq_r, dk_r, dv_r, dg_r]
    g_in = _weight_grad_pieces(n1, dproj, name="grad_w_in")
    exchange = _split_start(_exchange_job([g_in], [ax["w_in"]]), name="grad_exchange_w_in_start")
    dn1 = _matmul_pieces_nt(dproj, w_in, name="in_proj_bwd", after=[exchange["token"]])
    g_in, r_in = _split_wait(exchange, [dn1], name="grad_exchange_w_in_wait")
    s_in = add_halves("w_in", g_in, r_in)
    sending = _split_start(_send_sums_job([s_in], [ax["w_in"]]), name="grad_send_w_in_start")
    dx, _, d_norm_mix = _rmsnorm_bwd(dn1, x, norm_mix_w, dh1, name="norm_mix_bwd", after=[sending["token"]])

    small = dict(loss=loss[0, 0], norm_mix_w=d_norm_mix, ret_decay_fwd=dlg_f * lg_f, ret_decay_bwd=dlg_b * lg_b,
                 ret_norm_w=d_ret_norm, norm_ffn_w=d_norm_ffn, norm_final_w=d_norm_final)
    return (dx, dict(w_out=gr_out, w_gate=gr_gate, w_up=gr_up, w_down=gr_down), small,
            dict(sending=sending, grad=g_in, received=r_in))


def _mesh_position():
    x, y, c = lax.axis_index("x"), lax.axis_index("y"), lax.axis_index("c")
    chips = [(1 - x, y), (x, 1 - y), (1 - x, 1 - y)]
    return x, y, c, chips


def _span(span):
    if span is None:
        return slice(None)
    start, size, step = span
    return pl.ds(start if isinstance(start, int) else pl.multiple_of(start, step), size)


def _part_rows(part, rows):
    first, count, of = part
    return first * (rows // of), count * (rows // of), rows // of


def _region(ref, axis, shard, half, shard_size, half_size, part=None, total_rows=None):
    along = None if shard is None else (shard * shard_size, shard_size, shard_size)
    other = None if half is None else (half * half_size, half_size, half_size)
    rows, cols = (other, along) if axis == 1 else (along, other)
    if part is not None:
        start, size, _ = rows if rows is not None else (0, total_rows, None)
        offset, size, step = _part_rows(part, size)
        rows = (start + offset, size, step)
    return ref.at[_span(rows), _span(cols)]


def _gather_job(full, axes, stage, part=None, peers=(0, 1, 2)):
    n = len(full)

    def copies(refs, sems):
        send_sem, recv_sem = sems
        x, y, c, chips = _mesh_position()
        me = 2 * x + y

        def copy(w, k, shard, half, target):
            rows_cols = full[w].shape
            place = _region(refs[w], axes[w], shard, half, rows_cols[axes[w]] // N_CHIPS, rows_cols[1 - axes[w]] // 2,
                            part)
            return pltpu.make_async_remote_copy(
                src_ref=place, dst_ref=place, send_sem=send_sem.at[w, k], recv_sem=recv_sem.at[w, k],
                device_id=target, device_id_type=MESH)

        def sent(w, k):
            if stage == "ici":
                return copy(w, k, me, c, (chips[k][0], chips[k][1], c))
            return copy(w, k, 2 * chips[k][0] + chips[k][1], c, (x, y, 1 - c))

        def landed(w, k):
            return copy(w, k, 2 * chips[k][0] + chips[k][1], c if stage == "ici" else 1 - c, (x, y, 1 - c))

        return sent, landed

    def start(refs, sems):
        sent, _ = copies(refs, sems)
        for w in range(n):
            for k in peers:
                sent(w, k).start()

    def finish(refs, sems):
        sent, landed = copies(refs, sems)
        for w in range(n):
            for k in peers:
                landed(w, k).wait_recv()
                sent(w, k).wait_send()

    return _Job(ios=full, sems=[pltpu.SemaphoreType.DMA((n, 3))] * 2, start=start, finish=finish)


def _exchange_job(grads, axes):
    n = len(grads)

    def half_shape(w):
        return tuple(d // 2 if a != axes[w] else d for a, d in enumerate(grads[w].shape))

    def copy(refs, sems, w):
        x, y, c, _ = _mesh_position()
        return pltpu.make_async_remote_copy(
            src_ref=_region(refs[w], axes[w], None, 1 - c, 0, half_shape(w)[1 - axes[w]]), dst_ref=refs[n + w],
            send_sem=sems[0].at[w], recv_sem=sems[1].at[w], device_id=(x, y, 1 - c), device_id_type=MESH)

    def start(refs, sems):
        for w in range(n):
            copy(refs, sems, w).start()

    def finish(refs, sems):
        for w in range(n):
            copy(refs, sems, w).wait()

    return _Job(ins=grads, outs=[jax.ShapeDtypeStruct(half_shape(w), F32) for w in range(n)],
                sems=[pltpu.SemaphoreType.DMA((n,))] * 2, start=start, finish=finish)


def _half_block_spec(axis, block, half_blocks, use_half):
    if axis == 1:
        if use_half:
            return pl.BlockSpec(block, lambda i, pos: (pos[0] * half_blocks + i, 0))
        return pl.BlockSpec(block, lambda i, pos: (i, 0))
    if use_half:
        return pl.BlockSpec(block, lambda i, pos: (i, pos[0]))
    return pl.BlockSpec(block, lambda i, pos: (i, 0))


def _add_halves(grad, received, axis, pos, *, name):
    rows, cols = received.shape
    tr = _row_block(rows, cols)
    nb = rows // tr

    def body(pos_ref, g_ref, r_ref, o_ref):
        o_ref[...] = (g_ref[...] + r_ref[...]).astype(BF16)

    blk = (tr, cols)
    return pl.pallas_call(
        body, name=name, out_shape=jax.ShapeDtypeStruct((rows, cols), BF16),
        grid_spec=pltpu.PrefetchScalarGridSpec(
            num_scalar_prefetch=1, grid=(nb,),
            in_specs=[_half_block_spec(axis, blk, nb, True), _half_block_spec(axis, blk, nb, False)],
            out_specs=_half_block_spec(axis, blk, nb, False)),
        compiler_params=_params(("parallel",)),
    )(pos, grad, received)


def _send_sums_job(sums, axes, part=None, landing=None):
    n = len(sums)

    def part_shape(w):
        return tuple(d // N_CHIPS if a == axes[w] else d for a, d in enumerate(sums[w].shape))

    def copy(refs, sems, w, k):
        x, y, c, chips = _mesh_position()
        shard = 2 * chips[k][0] + chips[k][1]
        rows = part_shape(w)[0]
        dst = refs[n + w].at[k]
        if part is not None:
            offset, size, _ = _part_rows(part, rows)
            dst = refs[n + w].at[k, pl.ds(offset, size), :]
        return pltpu.make_async_remote_copy(
            src_ref=_region(refs[w], axes[w], shard, None, part_shape(w)[axes[w]], 0, part, rows), dst_ref=dst,
            send_sem=sems[0].at[w, k], recv_sem=sems[1].at[w, k],
            device_id=(chips[k][0], chips[k][1], c), device_id_type=MESH)

    def start(refs, sems):
        for w in range(n):
            for k in range(3):
                copy(refs, sems, w, k).start()

    def finish(refs, sems):
        for w in range(n):
            for k in range(3):
                copy(refs, sems, w, k).wait()

    sems = [pltpu.SemaphoreType.DMA((n, 3))] * 2
    if landing is not None:
        return _Job(ins=sums, ios=landing, sems=sems, start=start, finish=finish)
    return _Job(ins=sums, outs=[jax.ShapeDtypeStruct((3,) + part_shape(w), BF16) for w in range(n)],
                sems=sems, start=start, finish=finish)


def _sum_chip_parts(grad, received, parts, axis, pos, *, name):
    _, rows, cols = parts.shape
    tr = _row_block(rows, cols)
    nb = rows // tr
    blk = (tr, cols)

    def body(pos_ref, g_ref, r_ref, p_ref, o_ref):
        total = g_ref[...] + r_ref[...]
        for k in range(3):
            total = total + p_ref[k].astype(F32)
        o_ref[...] = total

    if axis == 1:
        g_spec = pl.BlockSpec(blk, lambda i, pos: (pos[0] * nb + i, pos[1]))
        r_spec = pl.BlockSpec(blk, lambda i, pos: (i, pos[1]))
        o_spec = pl.BlockSpec(blk, lambda i, pos: (pos[0] * nb + i, 0))
        shard_shape = (2 * rows, cols)
    else:
        g_spec = pl.BlockSpec(blk, lambda i, pos: (pos[1] * nb + i, pos[0]))
        r_spec = pl.BlockSpec(blk, lambda i, pos: (pos[1] * nb + i, 0))
        o_spec = pl.BlockSpec(blk, lambda i, pos: (i, pos[0]))
        shard_shape = (rows, 2 * cols)
    return pl.pallas_call(
        body, name=name, out_shape=jax.ShapeDtypeStruct(shard_shape, F32),
        grid_spec=pltpu.PrefetchScalarGridSpec(
            num_scalar_prefetch=1, grid=(nb,),
            in_specs=[g_spec, r_spec, pl.BlockSpec((3,) + blk, lambda i, pos: (0, i, 0))],
            out_specs=o_spec),
        compiler_params=_params(("parallel",)),
    )(pos, grad, received, parts)


def _join_job(shards, axes):
    n = len(shards)

    def copy(refs, sems, w, other):
        x, y, c, _ = _mesh_position()
        place = _region(refs[w], axes[w], None, 1 - c if other else c, 0, shards[w].shape[1 - axes[w]] // 2)
        return pltpu.make_async_remote_copy(
            src_ref=place, dst_ref=place, send_sem=sems[0].at[w], recv_sem=sems[1].at[w],
            device_id=(x, y, 1 - c), device_id_type=MESH)

    def start(refs, sems):
        for w in range(n):
            copy(refs, sems, w, False).start()

    def finish(refs, sems):
        for w in range(n):
            copy(refs, sems, w, True).wait_recv()
            copy(refs, sems, w, False).wait_send()

    return _Job(ios=shards, sems=[pltpu.SemaphoreType.DMA((n,))] * 2, start=start, finish=finish)


def _all_reduce_small(vec, after=()):
    rows, cols = vec.shape

    def body(v_ref, *rest):
        o_ref, land_ref, send_sem, recv_sem = rest[len(after):]
        x, y, c, _ = _mesh_position()
        me = 4 * x + 2 * y + c
        land_ref[me] = v_ref[...]
        copies = []
        for k in range(1, 8):
            px, py, pc = x ^ (k >> 2), y ^ ((k >> 1) & 1), c ^ (k & 1)
            copies.append(pltpu.make_async_remote_copy(
                src_ref=v_ref, dst_ref=land_ref.at[me], send_sem=send_sem.at[k], recv_sem=recv_sem.at[k],
                device_id=(px, py, pc), device_id_type=MESH))
        for cp in copies:
            cp.start()
        for k in range(1, 8):
            peer = me ^ k
            pltpu.make_async_remote_copy(
                src_ref=v_ref, dst_ref=land_ref.at[peer], send_sem=send_sem.at[k], recv_sem=recv_sem.at[k],
                device_id=(x, y, c), device_id_type=MESH).wait_recv()
        for cp in copies:
            cp.wait_send()
        total = land_ref[0]
        for k in range(1, 8):
            total = total + land_ref[k]
        o_ref[...] = total

    vmem = pl.BlockSpec(memory_space=pltpu.VMEM)
    return pl.pallas_call(
        body, name="all_reduce_small", in_specs=[vmem] + [pl.BlockSpec(memory_space=pl.ANY)] * len(after),
        out_specs=vmem, out_shape=jax.ShapeDtypeStruct((rows, cols), F32),
        scratch_shapes=[pltpu.VMEM((8, rows, cols), F32), pltpu.SemaphoreType.DMA((8,)), pltpu.SemaphoreType.DMA((8,))],
    )(vec, *after)


def _adamw(w, g, m, v, *, name, after=()):
    rows, cols = w.shape
    tr = _row_block(rows, cols) if rows % 8 == 0 else rows
    bc1 = 1.0 - ADAM_B1 ** ADAM_STEP
    bc2 = 1.0 - ADAM_B2 ** ADAM_STEP

    def body(w_ref, g_ref, m_ref, v_ref, *rest):
        go_ref, d_ref, mo_ref, vo_ref = rest[len(after):]
        gv = g_ref[...]
        go_ref[...] = gv
        mn = ADAM_B1 * m_ref[...] + (1.0 - ADAM_B1) * gv
        vn = ADAM_B2 * v_ref[...] + (1.0 - ADAM_B2) * (gv * gv)
        mo_ref[...] = mn
        vo_ref[...] = vn
        d_ref[...] = -ADAM_LR * ((mn / bc1) / (jnp.sqrt(vn / bc2) + ADAM_EPS) + ADAM_WD * w_ref[...])

    blk = pl.BlockSpec((tr, cols), lambda i: (i, 0))
    shape = jax.ShapeDtypeStruct((rows, cols), F32)
    return pl.pallas_call(
        body, name=name, grid=(rows // tr,), in_specs=[blk] * 4 + [pl.BlockSpec(memory_space=pl.ANY)] * len(after),
        out_specs=[blk] * 4, out_shape=[shape] * 4, compiler_params=_params(("parallel",)),
    )(w, g, m, v, *after)


def _to_bf16_in_place(w, axis, pos, *, name, after=None):
    rows, cols = w.shape
    tr = _row_block(rows, cols)
    nb = rows // tr

    def body(pos_ref, w_ref, *rest):
        rest[-1][...] = w_ref[...].astype(BF16)

    if axis == 1:
        o_spec = pl.BlockSpec((tr, cols), lambda i, pos: (i, pos[1]))
        full_shape = (rows, N_CHIPS * cols)
    else:
        o_spec = pl.BlockSpec((tr, cols), lambda i, pos: (pos[1] * nb + i, 0))
        full_shape = (N_CHIPS * rows, cols)
    in_specs = [pl.BlockSpec((tr, cols), lambda i, pos: (i, 0))]
    operands = [pos, w]
    if after is not None:
        in_specs.append(pl.BlockSpec(after.shape, lambda i, pos: (0, 0)))
        operands.append(after)
    return pl.pallas_call(
        body, name=name, out_shape=jax.ShapeDtypeStruct(full_shape, BF16),
        grid_spec=pltpu.PrefetchScalarGridSpec(num_scalar_prefetch=1, grid=(nb,), in_specs=in_specs, out_specs=o_spec),
        compiler_params=_params(("parallel",)),
    )(*operands)


def _split_gather_start(full, axis):
    rows_cols = full.shape

    def body(buf_ref, *rest):
        sems = rest[:6]
        token_ref = rest[7]
        x, y, c, chips = _mesh_position()
        place = _region(buf_ref, axis, 2 * x + y, c, rows_cols[axis] // N_CHIPS, rows_cols[1 - axis] // 2)
        for k in range(3):
            pltpu.make_async_remote_copy(
                src_ref=place, dst_ref=place, send_sem=sems[k], recv_sem=sems[3 + k],
                device_id=(chips[k][0], chips[k][1], c), device_id_type=MESH).start()
        token_ref[...] = jnp.zeros_like(token_ref)

    hbm = pl.BlockSpec(memory_space=pltpu.HBM)
    sem = pl.BlockSpec(memory_space=pltpu.SEMAPHORE)
    res = pl.pallas_call(
        body, name="all_gather_w_in_start",
        out_shape=(*[pltpu.SemaphoreType.DMA(())] * 6, pltpu.HBM(full.shape, full.dtype),
                   jax.ShapeDtypeStruct((8, HEAD_DIM), F32)),
        in_specs=(hbm,), out_specs=(*[sem] * 6, hbm, pl.BlockSpec(memory_space=pltpu.VMEM)),
        input_output_aliases={0: 6},
        compiler_params=pltpu.CompilerParams(has_side_effects=pltpu.SideEffectType.DATAFLOW_SIDE_EFFECTING),
    )(pltpu.with_memory_space_constraint(full, pltpu.HBM))
    return list(res[:6]), res[6], res[7]


def _split_gather_wait(sems, full, axis, peer, after):
    rows_cols = full.shape

    def body(buf_ref, send_sem, recv_sem, *rest):
        x, y, c, chips = _mesh_position()

        def copy(shard):
            place = _region(buf_ref, axis, shard, c, rows_cols[axis] // N_CHIPS, rows_cols[1 - axis] // 2)
            return pltpu.make_async_remote_copy(
                src_ref=place, dst_ref=place, send_sem=send_sem, recv_sem=recv_sem,
                device_id=(chips[peer][0], chips[peer][1], c), device_id_type=MESH)

        copy(2 * x + y).wait_send()
        copy(2 * chips[peer][0] + chips[peer][1]).wait_recv()

    hbm = pl.BlockSpec(memory_space=pltpu.HBM)
    sem = pl.BlockSpec(memory_space=pltpu.SEMAPHORE)
    return pl.pallas_call(
        body, name="all_gather_w_in_wait_%d" % peer, out_shape=pltpu.HBM(full.shape, full.dtype),
        in_specs=(hbm, sem, sem, *[pl.BlockSpec(memory_space=pl.ANY)] * len(after)), out_specs=hbm,
        input_output_aliases={0: 0},
        compiler_params=pltpu.CompilerParams(has_side_effects=pltpu.SideEffectType.DATAFLOW_SIDE_EFFECTING),
    )(full, sems[peer], sems[3 + peer], *after)


def _in_proj_part(n1, w_in, proj, shard_ids, which, *, out_cols):
    m, kdim = n1.shape
    cols = out_cols // N_CHIPS
    tm = _tile(m, 1024)

    def body(ids_ref, a_ref, b_ref, *rest):
        rest[-1][...] = _dot(a_ref[...], b_ref[...]).astype(BF16)

    in_specs = [pl.BlockSpec((tm, kdim), lambda i, ids: (i, 0)),
                pl.BlockSpec((kdim, cols), lambda i, ids: (0, ids[which]))]
    operands = [shard_ids, n1, w_in]
    if proj is not None:
        in_specs.append(pl.BlockSpec(memory_space=pl.ANY))
        operands.append(proj)
    return pl.pallas_call(
        body, name="in_proj_%d" % which, out_shape=jax.ShapeDtypeStruct((m, out_cols), BF16),
        grid_spec=pltpu.PrefetchScalarGridSpec(
            num_scalar_prefetch=1, grid=(m // tm,), in_specs=in_specs,
            out_specs=pl.BlockSpec((tm, cols), lambda i, ids: (i, ids[which]))),
        input_output_aliases={3: 0} if proj is not None else {},
        compiler_params=_params(("parallel",)),
    )(*operands)


BIG = ("w_in", "w_out", "w_gate", "w_up", "w_down")
BIG_AXIS = dict(w_in=1, w_out=0, w_gate=1, w_up=1, w_down=0)
SMALL = ("norm_mix_w", "ret_decay_fwd", "ret_decay_bwd", "ret_norm_w", "norm_ffn_w", "norm_final_w")
ALL_WEIGHTS = ("norm_mix_w", "w_in", "ret_decay_fwd", "ret_decay_bwd", "ret_norm_w", "w_out", "norm_ffn_w",
               "w_gate", "w_up", "w_down", "norm_final_w")
SMALL_ROW = 128 * 8


def _pack_small(small):
    pieces = [jnp.reshape(small["loss"], (1,))] + [jnp.reshape(small[k], (-1,)) for k in SMALL]
    rows = []
    for p in pieces:
        pad = -p.shape[0] % (8 * SMALL_ROW)
        rows.append(jnp.reshape(jnp.pad(p, (0, pad)), (-1, SMALL_ROW)))
    return jnp.concatenate(rows, axis=0)


def _unpack_small(block, like):
    out, row = {}, 0
    for k in ("loss",) + SMALL:
        size = 1 if k == "loss" else like[k].size
        nrows = -(-size // (8 * SMALL_ROW)) * 8
        out[k] = jnp.reshape(block[row:row + nrows], (-1,))[:size]
        row += nrows
    return out


def kernel(x, norm_mix_w, w_in, ret_decay_fwd, ret_decay_bwd, ret_norm_w, w_out, norm_ffn_w, w_gate, w_up, w_down, norm_final_w, loss_target, m_norm_mix_w, m_w_in, m_ret_decay_fwd, m_ret_decay_bwd, m_ret_norm_w, m_w_out, m_norm_ffn_w, m_w_gate, m_w_up, m_w_down, m_norm_final_w, v_norm_mix_w, v_w_in, v_ret_decay_fwd, v_ret_decay_bwd, v_ret_norm_w, v_w_out, v_norm_ffn_w, v_w_gate, v_w_up, v_w_down, v_norm_final_w):
    weights = dict(norm_mix_w=norm_mix_w, w_in=w_in, ret_decay_fwd=ret_decay_fwd, ret_decay_bwd=ret_decay_bwd,
                   ret_norm_w=ret_norm_w, w_out=w_out, norm_ffn_w=norm_ffn_w, w_gate=w_gate, w_up=w_up,
                   w_down=w_down, norm_final_w=norm_final_w)
    m_in = dict(norm_mix_w=m_norm_mix_w, w_in=m_w_in, ret_decay_fwd=m_ret_decay_fwd, ret_decay_bwd=m_ret_decay_bwd,
                ret_norm_w=m_ret_norm_w, w_out=m_w_out, norm_ffn_w=m_norm_ffn_w, w_gate=m_w_gate, w_up=m_w_up,
                w_down=m_w_down, norm_final_w=m_norm_final_w)
    v_in = dict(norm_mix_w=v_norm_mix_w, w_in=v_w_in, ret_decay_fwd=v_ret_decay_fwd, ret_decay_bwd=v_ret_decay_bwd,
                ret_norm_w=v_ret_norm_w, w_out=v_w_out, norm_ffn_w=v_norm_ffn_w, w_gate=v_w_gate, w_up=v_w_up,
                w_down=v_w_down, norm_final_w=v_norm_final_w)
    pos = jnp.stack([lax.axis_index("c"), 2 * lax.axis_index("x") + lax.axis_index("y")]).astype(jnp.int32)

    own = {"w_in": _to_bf16_in_place(weights["w_in"][0], BIG_AXIS["w_in"], pos, name="cast_w_in")}
    w_in_started = _split_gather_start(own["w_in"], BIG_AXIS["w_in"])
    queued, token = {}, w_in_started[2]
    for k in ("w_gate", "w_out", "w_up", "w_down"):
        own[k] = _to_bf16_in_place(weights[k][0], BIG_AXIS[k], pos, name="cast_" + k, after=token)
        queued[k] = _split_start(_gather_job([own[k]], [BIG_AXIS[k]], "ici"), name="all_gather_%s_start" % k)
        token = queued[k]["token"]
    cx, cy = lax.axis_index("x"), lax.axis_index("y")
    shard_ids = jnp.stack([2 * cx + cy, 2 * (1 - cx) + cy, 2 * cx + 1 - cy, 2 * (1 - cx) + 1 - cy]).astype(jnp.int32)

    dx, grad_w, small, w_in_pending = _step(
        x[0], loss_target[0], norm_mix_w, ret_decay_fwd[0], ret_decay_bwd[0], ret_norm_w, norm_ffn_w,
        norm_final_w[None, :], own, w_in_started, queued, shard_ids, pos)

    delta, new_m, new_v = {}, {}, {}

    def update(k, after):
        shape = weights[k].shape
        as2d = (lambda t: jnp.reshape(t, (-1, shape[-1])))
        grad_w[k], delta[k], new_m[k], new_v[k] = (jnp.reshape(t, shape) for t in _adamw(
            as2d(weights[k]), as2d(grad_w[k]), as2d(m_in[k]), as2d(v_in[k]), name="adamw_" + k, after=after))

    others = [k for k in BIG if k != "w_in"]
    for k in others:
        update(k, [w_in_pending["sending"]["token"]])
    _, parts = _split_wait(w_in_pending["sending"], [dx] + [delta[k] for k in others], name="grad_send_w_in_wait")

    half = _sum_chip_parts(w_in_pending["grad"], w_in_pending["received"], parts, BIG_AXIS["w_in"], pos,
                           name="grad_sum_parts_w_in")
    joining = _split_start(_join_job([half], [BIG_AXIS["w_in"]]), name="grad_join_w_in_start")

    like = {k: weights[k] for k in SMALL}
    reduced = _unpack_small(_all_reduce_small(_pack_small(small), after=[joining["token"]]), like)
    loss = reduced["loss"][0]
    for k in SMALL:
        grad_w[k] = jnp.reshape(reduced[k], (1, -1))
        update(k, [])
    (grad_w["w_in"],) = _split_wait(joining, [delta[k] for k in SMALL], name="grad_join_w_in_wait")
    update("w_in", [])

    return (loss, dx[None], *[grad_w[k] for k in ALL_WEIGHTS], *[delta[k] for k in ALL_WEIGHTS],
            *[new_m[k] for k in ALL_WEIGHTS], *[new_v[k] for k in ALL_WEIGHTS])
```
